```python
import jax, jax.numpy as jnp
from jax import lax
import numpy as np

D_MODEL = 1024
BATCH = 8
SEQ = 2048
DEPTH = 2

D_FF = 2816
CHUNK = 128
A_HEADS = 4
A_HEAD_DIM = 128
D_A = A_HEADS * A_HEAD_DIM
B_GROUPS = 8
B_GROUP_DIM = 64
D_B = B_GROUPS * B_GROUP_DIM
D_MIX = D_A + D_B
D_IN_AB = 2 * D_A + 3 * D_B
CONV_W = 3
POOL_WINDOWS = (2, 4, 8, 16)
POOL_GROUPS = len(POOL_WINDOWS)
POOL_GROUP_DIM = D_MODEL // POOL_GROUPS
N_SUB = 3
N_EVEN = (DEPTH + 1) // 2
N_ODD = DEPTH // 2
EPS = 1e-6

kernel_name = "hybrid_gmlp_shortconv_pool_macaron_adaln"


def rmsnorm(x, g):
    xf = x.astype(jnp.float32)
    y = xf * lax.rsqrt(jnp.mean(xf * xf, axis=-1, keepdims=True) + EPS)
    return (y * g.astype(jnp.float32)).astype(x.dtype)


def layernorm(x, g):
    xf = x.astype(jnp.float32)
    mu = jnp.mean(xf, axis=-1, keepdims=True)
    var = jnp.mean(jnp.square(xf - mu), axis=-1, keepdims=True)
    y = (xf - mu) * lax.rsqrt(var + EPS)
    return (y * g.astype(jnp.float32)).astype(x.dtype)


def modulate(x, g, mod):
    shift, scale, gate = jnp.split(mod, 3, axis=-1)
    h = rmsnorm(x, g) * (1.0 + scale[:, None, :]) + shift[:, None, :]
    return h, gate[:, None, :]


def swiglu(h, w_in, w_out):
    gu = h @ w_in
    g, u = jnp.split(gu, 2, axis=-1)
    return (jax.nn.silu(g) * u) @ w_out


def spatial_gating(u, v, norm_v, w_s, b_s):
    bsz, s, _ = v.shape
    n_chunks = s // CHUNK
    v = layernorm(v, norm_v)
    vc = v.reshape(bsz, n_chunks, CHUNK, A_HEADS, A_HEAD_DIM)
    mask = jnp.tril(jnp.ones((CHUNK, CHUNK), dtype=w_s.dtype))
    z = jnp.einsum('hts,bnshd->bnthd', w_s * mask[None], vc)
    z = z + jnp.transpose(b_s)[None, None, :, :, None]
    return u * z.reshape(bsz, s, D_A)


def causal_short_conv(x, w):
    s = x.shape[1]
    xp = jnp.pad(x, ((0, 0), (CONV_W - 1, 0), (0, 0)))
    y = w[0] * xp[:, 0:s]
    for k in range(1, CONV_W):
        y = y + w[k] * xp[:, k:k + s]
    return y


def mixer_ab(h, w_in, norm_v, w_s, b_s, conv_w, w_out):
    proj = h @ w_in
    u, v, bg, cg, xb = jnp.split(
        proj, [D_A, 2 * D_A, 2 * D_A + D_B, 2 * D_A + 2 * D_B], axis=-1)
    y_a = spatial_gating(jax.nn.gelu(u), jax.nn.gelu(v), norm_v, w_s, b_s)
    y_b = bg * causal_short_conv(cg * xb, conv_w)
    return jnp.concatenate([y_a, y_b], axis=-1) @ w_out


def mixer_pool(h, w_grp, scale):
    s = h.shape[1]
    cum = jnp.cumsum(h.astype(jnp.float32), axis=1)
    t = jnp.arange(s)
    outs = []
    for i, w in enumerate(POOL_WINDOWS):
        sl = slice(i * POOL_GROUP_DIM, (i + 1) * POOL_GROUP_DIM)
        cg = cum[..., sl]
        prev = jnp.pad(cg, ((0, 0), (w, 0), (0, 0)))[:, :s]
        cnt = jnp.minimum(t + 1, w).astype(jnp.float32)[None, :, None]
        p = ((cg - prev) / cnt).astype(h.dtype) - h[..., sl]
        outs.append(p @ w_grp[i])
    return jnp.concatenate(outs, axis=-1) * scale


def _fwd_setup_inputs(seed: int = 0) -> dict:
    key = jax.random.key(seed)
    ks = jax.random.split(key, 20)
    f32 = jnp.float32
    nrm = lambda k, shape, s: (jax.random.normal(k, shape, f32) * s)
    x = jax.random.normal(ks[0], (BATCH, SEQ, D_MODEL), f32)
    c = jax.random.normal(ks[1], (BATCH, D_MODEL), f32)
    norm_g = 1.0 + nrm(ks[2], (DEPTH, N_SUB, D_MODEL), 0.02)
    w_mod = nrm(ks[3], (DEPTH, D_MODEL, N_SUB * 3 * D_MODEL), 0.5 * D_MODEL ** -0.5)
    b_mod = nrm(ks[4], (DEPTH, N_SUB * 3 * D_MODEL), 0.01)
    w_ffn_in = nrm(ks[5], (DEPTH, 2, D_MODEL, 2 * D_FF), D_MODEL ** -0.5)
    w_ffn_out = nrm(ks[6], (DEPTH, 2, D_FF, D_MODEL), D_FF ** -0.5)
    ab_w_in = nrm(ks[7], (N_EVEN, D_MODEL, D_IN_AB), D_MODEL ** -0.5)
    ab_norm_v = 1.0 + nrm(ks[8], (N_EVEN, D_A), 0.02)
    ab_w_s = nrm(ks[9], (N_EVEN, A_HEADS, CHUNK, CHUNK), CHUNK ** -0.5)
    ab_b_s = 1.0 + nrm(ks[10], (N_EVEN, A_HEADS, CHUNK), 0.02)
    ab_conv_w = nrm(ks[11], (N_EVEN, CONV_W, D_B), CONV_W ** -0.5)
    ab_w_out = nrm(ks[12], (N_EVEN, D_MIX, D_MODEL), D_MIX ** -0.5)
    pool_w_grp = nrm(ks[13], (N_ODD, POOL_GROUPS, POOL_GROUP_DIM, POOL_GROUP_DIM), POOL_GROUP_DIM ** -0.5)
    pool_scale = 1.0 + nrm(ks[14], (N_ODD, D_MODEL), 0.1)
    final_g = 1.0 + nrm(ks[15], (D_MODEL,), 0.02)
    return {"x": x, "c": c, "norm_g": norm_g, "w_mod": w_mod, "b_mod": b_mod,
            "w_ffn_in": w_ffn_in, "w_ffn_out": w_ffn_out,
            "ab_w_in": ab_w_in, "ab_norm_v": ab_norm_v, "ab_w_s": ab_w_s, "ab_b_s": ab_b_s,
            "ab_conv_w": ab_conv_w, "ab_w_out": ab_w_out,
            "pool_w_grp": pool_w_grp, "pool_scale": pool_scale, "final_g": final_g}


def _fwd_reference(x, c, norm_g, w_mod, b_mod, w_ffn_in, w_ffn_out,
              ab_w_in, ab_norm_v, ab_w_s, ab_b_s, ab_conv_w, ab_w_out,
              pool_w_grp, pool_scale, final_g):
    c_act = jax.nn.silu(c)
    for l in range(DEPTH):
        mod = c_act @ w_mod[l] + b_mod[l]
        mod_f1, mod_mx, mod_f2 = jnp.split(mod, N_SUB, axis=-1)
        h, gate = modulate(x, norm_g[l, 0], mod_f1)
        x = x + 0.5 * gate * swiglu(h, w_ffn_in[l, 0], w_ffn_out[l, 0])
        h, gate = modulate(x, norm_g[l, 1], mod_mx)
        if l % 2 == 0:
            j = l // 2
            y = mixer_ab(h, ab_w_in[j], ab_norm_v[j], ab_w_s[j], ab_b_s[j],
                         ab_conv_w[j], ab_w_out[j])
        else:
            j = l // 2
            y = mixer_pool(h, pool_w_grp[j], pool_scale[j])
        x = x + gate * y
        h, gate = modulate(x, norm_g[l, 2], mod_f2)
        x = x + 0.5 * gate * swiglu(h, w_ffn_in[l, 1], w_ffn_out[l, 1])
    return rmsnorm(x, final_g)


import jax as _jax
import jax.numpy as _jnp

TWIN_FORMAT = 'train_step'
FWD_PARAMS = ['x', 'c', 'norm_g', 'w_mod', 'b_mod', 'w_ffn_in', 'w_ffn_out', 'ab_w_in', 'ab_norm_v', 'ab_w_s', 'ab_b_s', 'ab_conv_w', 'ab_w_out', 'pool_w_grp', 'pool_scale', 'final_g']
TWIN_WEIGHTS = ['norm_g', 'w_mod', 'b_mod', 'w_ffn_in', 'w_ffn_out', 'ab_w_in', 'ab_norm_v', 'ab_w_s', 'ab_b_s', 'ab_conv_w', 'ab_w_out', 'pool_w_grp', 'pool_scale', 'final_g']
TWIN_DIFF_INPUT = 'x'
TWIN_INPUTS = ['x', 'c', 'norm_g', 'w_mod', 'b_mod', 'w_ffn_in', 'w_ffn_out', 'ab_w_in', 'ab_norm_v', 'ab_w_s', 'ab_b_s', 'ab_conv_w', 'ab_w_out', 'pool_w_grp', 'pool_scale', 'final_g', 'loss_target', 'm_norm_g', 'm_w_mod', 'm_b_mod', 'm_w_ffn_in', 'm_w_ffn_out', 'm_ab_w_in', 'm_ab_norm_v', 'm_ab_w_s', 'm_ab_b_s', 'm_ab_conv_w', 'm_ab_w_out', 'm_pool_w_grp', 'm_pool_scale', 'm_final_g', 'v_norm_g', 'v_w_mod', 'v_b_mod', 'v_w_ffn_in', 'v_w_ffn_out', 'v_ab_w_in', 'v_ab_norm_v', 'v_ab_w_s', 'v_ab_b_s', 'v_ab_conv_w', 'v_ab_w_out', 'v_pool_w_grp', 'v_pool_scale', 'v_final_g']
TWIN_OUTPUTS = ['loss', 'grad_x', 'grad_norm_g', 'grad_w_mod', 'grad_b_mod', 'grad_w_ffn_in', 'grad_w_ffn_out', 'grad_ab_w_in', 'grad_ab_norm_v', 'grad_ab_w_s', 'grad_ab_b_s', 'grad_ab_conv_w', 'grad_ab_w_out', 'grad_pool_w_grp', 'grad_pool_scale', 'grad_final_g', 'delta_norm_g', 'delta_w_mod', 'delta_b_mod', 'delta_w_ffn_in', 'delta_w_ffn_out', 'delta_ab_w_in', 'delta_ab_norm_v', 'delta_ab_w_s', 'delta_ab_b_s', 'delta_ab_conv_w', 'delta_ab_w_out', 'delta_pool_w_grp', 'delta_pool_scale', 'delta_final_g', 'new_m_norm_g', 'new_m_w_mod', 'new_m_b_mod', 'new_m_w_ffn_in', 'new_m_w_ffn_out', 'new_m_ab_w_in', 'new_m_ab_norm_v', 'new_m_ab_w_s', 'new_m_ab_b_s', 'new_m_ab_conv_w', 'new_m_ab_w_out', 'new_m_pool_w_grp', 'new_m_pool_scale', 'new_m_final_g', 'new_v_norm_g', 'new_v_w_mod', 'new_v_b_mod', 'new_v_w_ffn_in', 'new_v_w_ffn_out', 'new_v_ab_w_in', 'new_v_ab_norm_v', 'new_v_ab_w_s', 'new_v_ab_b_s', 'new_v_ab_conv_w', 'new_v_ab_w_out', 'new_v_pool_w_grp', 'new_v_pool_scale', 'new_v_final_g']
TWIN_LEAF_KINDS = {'loss': 'loss', 'grad_x': 'grad_x', 'grad_norm_g': 'grad_w', 'grad_w_mod': 'grad_w', 'grad_b_mod': 'grad_w', 'grad_w_ffn_in': 'grad_w', 'grad_w_ffn_out': 'grad_w', 'grad_ab_w_in': 'grad_w', 'grad_ab_norm_v': 'grad_w', 'grad_ab_w_s': 'grad_w', 'grad_ab_b_s': 'grad_w', 'grad_ab_conv_w': 'grad_w', 'grad_ab_w_out': 'grad_w', 'grad_pool_w_grp': 'grad_w', 'grad_pool_scale': 'grad_w', 'grad_final_g': 'grad_w', 'delta_norm_g': 'delta_w', 'delta_w_mod': 'delta_w', 'delta_b_mod': 'delta_w', 'delta_w_ffn_in': 'delta_w', 'delta_w_ffn_out': 'delta_w', 'delta_ab_w_in': 'delta_w', 'delta_ab_norm_v': 'delta_w', 'delta_ab_w_s': 'delta_w', 'delta_ab_b_s': 'delta_w', 'delta_ab_conv_w': 'delta_w', 'delta_ab_w_out': 'delta_w', 'delta_pool_w_grp': 'delta_w', 'delta_pool_scale': 'delta_w', 'delta_final_g': 'delta_w', 'new_m_norm_g': 'new_m', 'new_m_w_mod': 'new_m', 'new_m_b_mod': 'new_m', 'new_m_w_ffn_in': 'new_m', 'new_m_w_ffn_out': 'new_m', 'new_m_ab_w_in': 'new_m', 'new_m_ab_norm_v': 'new_m', 'new_m_ab_w_s': 'new_m', 'new_m_ab_b_s': 'new_m', 'new_m_ab_conv_w': 'new_m', 'new_m_ab_w_out': 'new_m', 'new_m_pool_w_grp': 'new_m', 'new_m_pool_scale': 'new_m', 'new_m_final_g': 'new_m', 'new_v_norm_g': 'new_v', 'new_v_w_mod': 'new_v', 'new_v_b_mod': 'new_v', 'new_v_w_ffn_in': 'new_v', 'new_v_w_ffn_out': 'new_v', 'new_v_ab_w_in': 'new_v', 'new_v_ab_norm_v': 'new_v', 'new_v_ab_w_s': 'new_v', 'new_v_ab_b_s': 'new_v', 'new_v_ab_conv_w': 'new_v', 'new_v_ab_w_out': 'new_v', 'new_v_pool_w_grp': 'new_v', 'new_v_pool_scale': 'new_v', 'new_v_final_g': 'new_v'}


def _forward(args):
    return _fwd_reference(*[args[k] for k in FWD_PARAMS])


def _output_shape():
    out = _jax.eval_shape(lambda: _forward(_fwd_setup_inputs(0)))
    return out.shape, out.dtype

N_MICROBATCH = 1
ADAM_LR = 0.001
ADAM_B1 = 0.9
ADAM_B2 = 0.999
ADAM_EPS = 1e-08
ADAM_WD = 0.01
ADAM_STEP = 10
PER_EXAMPLE_BATCH_AXIS = {'x': 0, 'c': 0, 'loss_target': 0}
SHARED_INPUTS = []
_WEIGHT_DTYPES = {'norm_g': _jnp.float32, 'w_mod': _jnp.float32, 'b_mod': _jnp.float32, 'w_ffn_in': _jnp.float32, 'w_ffn_out': _jnp.float32, 'ab_w_in': _jnp.float32, 'ab_norm_v': _jnp.float32, 'ab_w_s': _jnp.float32, 'ab_b_s': _jnp.float32, 'ab_conv_w': _jnp.float32, 'ab_w_out': _jnp.float32, 'pool_w_grp': _jnp.float32, 'pool_scale': _jnp.float32, 'final_g': _jnp.float32}
MOMENT_SCALE = {'norm_g': 3.220391e-02, 'w_mod': 3.419234e-02, 'b_mod': 5.711524e-02, 'w_ffn_in': 8.191901e-03, 'w_ffn_out': 1.335633e-02, 'ab_w_in': 4.085740e-02, 'ab_norm_v': 1.825645e-02, 'ab_w_s': 1.870293e-02, 'ab_b_s': 2.661307e-02, 'ab_conv_w': 4.827121e-02, 'ab_w_out': 4.043877e-02, 'pool_w_grp': 3.245126e-02, 'pool_scale': 5.083530e-02, 'final_g': 1.607915e+01}


def _to_microbatches(a, axis):
    t = _jnp.moveaxis(a, axis, 0)
    t = t.reshape((N_MICROBATCH, t.shape[0] // N_MICROBATCH) + t.shape[1:])
    return _jnp.moveaxis(t, 1, axis + 1)


def setup_inputs(seed: int = 0) -> dict:
    inp = _fwd_setup_inputs(seed)
    key = _jax.random.fold_in(_jax.random.key(seed), 7919)
    shape, _ = _output_shape()
    out = dict(inp)
    out["loss_target"] = _jax.random.normal(_jax.random.fold_in(key, 0), shape, _jnp.float32)
    for i, name in enumerate(TWIN_WEIGHTS):
        w = inp[name].astype(_jnp.float32)
        if MOMENT_SCALE is None:
            s = _jnp.sqrt(_jnp.mean(_jnp.square(w)) + 1e-30)
        else:
            s = MOMENT_SCALE[name]
        km, kv = _jax.random.split(_jax.random.fold_in(key, i + 1))
        out[name] = w
        out["m_" + name] = s * _jax.random.normal(km, w.shape, _jnp.float32)
        out["v_" + name] = (s * s) * _jax.random.uniform(kv, w.shape, _jnp.float32, 0.5, 1.5)
    if N_MICROBATCH > 1:
        for name, axis in PER_EXAMPLE_BATCH_AXIS.items():
            out[name] = _to_microbatches(out[name], axis)
    return {'x': out['x'], 'c': out['c'], 'norm_g': out['norm_g'], 'w_mod': out['w_mod'], 'b_mod': out['b_mod'], 'w_ffn_in': out['w_ffn_in'], 'w_ffn_out': out['w_ffn_out'], 'ab_w_in': out['ab_w_in'], 'ab_norm_v': out['ab_norm_v'], 'ab_w_s': out['ab_w_s'], 'ab_b_s': out['ab_b_s'], 'ab_conv_w': out['ab_conv_w'], 'ab_w_out': out['ab_w_out'], 'pool_w_grp': out['pool_w_grp'], 'pool_scale': out['pool_scale'], 'final_g': out['final_g'], 'loss_target': out['loss_target'], 'm_norm_g': out['m_norm_g'], 'm_w_mod': out['m_w_mod'], 'm_b_mod': out['m_b_mod'], 'm_w_ffn_in': out['m_w_ffn_in'], 'm_w_ffn_out': out['m_w_ffn_out'], 'm_ab_w_in': out['m_ab_w_in'], 'm_ab_norm_v': out['m_ab_norm_v'], 'm_ab_w_s': out['m_ab_w_s'], 'm_ab_b_s': out['m_ab_b_s'], 'm_ab_conv_w': out['m_ab_conv_w'], 'm_ab_w_out': out['m_ab_w_out'], 'm_pool_w_grp': out['m_pool_w_grp'], 'm_pool_scale': out['m_pool_scale'], 'm_final_g': out['m_final_g'], 'v_norm_g': out['v_norm_g'], 'v_w_mod': out['v_w_mod'], 'v_b_mod': out['v_b_mod'], 'v_w_ffn_in': out['v_w_ffn_in'], 'v_w_ffn_out': out['v_w_ffn_out'], 'v_ab_w_in': out['v_ab_w_in'], 'v_ab_norm_v': out['v_ab_norm_v'], 'v_ab_w_s': out['v_ab_w_s'], 'v_ab_b_s': out['v_ab_b_s'], 'v_ab_conv_w': out['v_ab_conv_w'], 'v_ab_w_out': out['v_ab_w_out'], 'v_pool_w_grp': out['v_pool_w_grp'], 'v_pool_scale': out['v_pool_scale'], 'v_final_g': out['v_final_g']}


def _loss(weights, diff, rest, loss_target):
    with _jax.named_scope("forward"):
        args = {**rest, TWIN_DIFF_INPUT: diff, **{k: w.astype(_WEIGHT_DTYPES[k]) for k, w in weights.items()}}
        y = _forward(args)
    with _jax.named_scope("loss_head"):
        err = _jnp.square(y.astype(_jnp.float32) - loss_target)
        return 0.5 * _jnp.sum(_jnp.mean(err, axis=-1)) if err.ndim else 0.5 * err


def _adamw(w, g, m, v):
    m = ADAM_B1 * m + (1.0 - ADAM_B1) * g
    v = ADAM_B2 * v + (1.0 - ADAM_B2) * _jnp.square(g)
    m_hat = m / (1.0 - ADAM_B1 ** ADAM_STEP)
    v_hat = v / (1.0 - ADAM_B2 ** ADAM_STEP)
    delta = -ADAM_LR * (m_hat / (_jnp.sqrt(v_hat) + ADAM_EPS) + ADAM_WD * w)
    return delta, m, v


def reference(x, c, norm_g, w_mod, b_mod, w_ffn_in, w_ffn_out, ab_w_in, ab_norm_v, ab_w_s, ab_b_s, ab_conv_w, ab_w_out, pool_w_grp, pool_scale, final_g, loss_target, m_norm_g, m_w_mod, m_b_mod, m_w_ffn_in, m_w_ffn_out, m_ab_w_in, m_ab_norm_v, m_ab_w_s, m_ab_b_s, m_ab_conv_w, m_ab_w_out, m_pool_w_grp, m_pool_scale, m_final_g, v_norm_g, v_w_mod, v_b_mod, v_w_ffn_in, v_w_ffn_out, v_ab_w_in, v_ab_norm_v, v_ab_w_s, v_ab_b_s, v_ab_conv_w, v_ab_w_out, v_pool_w_grp, v_pool_scale, v_final_g):
    given = dict(x=x, c=c, norm_g=norm_g, w_mod=w_mod, b_mod=b_mod, w_ffn_in=w_ffn_in, w_ffn_out=w_ffn_out, ab_w_in=ab_w_in, ab_norm_v=ab_norm_v, ab_w_s=ab_w_s, ab_b_s=ab_b_s, ab_conv_w=ab_conv_w, ab_w_out=ab_w_out, pool_w_grp=pool_w_grp, pool_scale=pool_scale, final_g=final_g, loss_target=loss_target, m_norm_g=m_norm_g, m_w_mod=m_w_mod, m_b_mod=m_b_mod, m_w_ffn_in=m_w_ffn_in, m_w_ffn_out=m_w_ffn_out, m_ab_w_in=m_ab_w_in, m_ab_norm_v=m_ab_norm_v, m_ab_w_s=m_ab_w_s, m_ab_b_s=m_ab_b_s, m_ab_conv_w=m_ab_conv_w, m_ab_w_out=m_ab_w_out, m_pool_w_grp=m_pool_w_grp, m_pool_scale=m_pool_scale, m_final_g=m_final_g, v_norm_g=v_norm_g, v_w_mod=v_w_mod, v_b_mod=v_b_mod, v_w_ffn_in=v_w_ffn_in, v_w_ffn_out=v_w_ffn_out, v_ab_w_in=v_ab_w_in, v_ab_norm_v=v_ab_norm_v, v_ab_w_s=v_ab_w_s, v_ab_b_s=v_ab_b_s, v_ab_conv_w=v_ab_conv_w, v_ab_w_out=v_ab_w_out, v_pool_w_grp=v_pool_w_grp, v_pool_scale=v_pool_scale, v_final_g=v_final_g)
    weights = {n: given[n] for n in TWIN_WEIGHTS}
    shared = {n: given[n] for n in SHARED_INPUTS}
    per_example = {n: given[n] for n in ['x', 'c']}
    grad_fn = _jax.value_and_grad(_loss, argnums=(0, 1))

    def one_microbatch(ex, loss_target):
        ex = dict(ex)
        diff = ex.pop(TWIN_DIFF_INPUT)
        return grad_fn(weights, diff, {**shared, **ex}, loss_target)

    if N_MICROBATCH == 1:
        loss, (grad_w, grad_x) = one_microbatch(per_example, given["loss_target"])
    else:
        def body(carry, xs):
            loss_sum, grad_sum = carry
            l_k, (gw_k, gx_k) = one_microbatch(xs[0], xs[1])
            with _jax.named_scope("update"):
                return (loss_sum + l_k, _jax.tree.map(_jnp.add, grad_sum, gw_k)), gx_k

        init = (_jnp.zeros((), _jnp.float32), _jax.tree.map(_jnp.zeros_like, weights))
        (loss, grad_w), grad_x = _jax.lax.scan(body, init, (per_example, given["loss_target"]))
    with _jax.named_scope("update"):
        delta_w, new_m, new_v = {}, {}, {}
        for n in TWIN_WEIGHTS:
            delta_w[n], new_m[n], new_v[n] = _adamw(weights[n], grad_w[n], given["m_" + n], given["v_" + n])
    return (loss, grad_x, *[grad_w[n] for n in TWIN_WEIGHTS], *[delta_w[n] for n in TWIN_WEIGHTS],
            *[new_m[n] for n in TWIN_WEIGHTS], *[new_v[n] for n in TWIN_WEIGHTS])
```

```python
import functools
import math

import jax
import jax.numpy as jnp
from jax import lax
from jax.experimental import pallas as pl
from jax.experimental.pallas import tpu as pltpu

F32 = jnp.float32
BF16 = jnp.bfloat16
MESH = pl.DeviceIdType.MESH

EPS = 1e-6
ADAM_LR = 0.001
ADAM_B1 = 0.9
ADAM_B2 = 0.999
ADAM_EPS = 1e-08
ADAM_WD = 0.01
ADAM_STEP = 10
POOL_WINDOWS = (2, 4, 8, 16)
POOL_HALO = 16
CONV_HALO = 8
N_CHIPS = 4
N_DEV = 8
VMEM_LIMIT_BYTES = 48 * 1024 * 1024
EW_BLOCK_ELEMS = 256 * 1024


def _params(*sem):
    return pltpu.CompilerParams(dimension_semantics=sem or None, vmem_limit_bytes=VMEM_LIMIT_BYTES)


def _pick(n, prefs):
    for p in prefs:
        if p <= n and n % p == 0:
            return p
    return n


def _row_tile(rows, cols):
    best = None
    for d in range(16, rows + 1, 16):
        if rows % d == 0 and d * cols <= EW_BLOCK_ELEMS:
            best = d
    return best or rows


def _dot(a, b):
    return jnp.dot(a, b, preferred_element_type=F32)


def _dot_nt(a, b):
    return lax.dot_general(a, b, (((1,), (1,)), ((), ())), preferred_element_type=F32)


def _dot_tn(a, b):
    return lax.dot_general(a, b, (((0,), (0,)), ((), ())), preferred_element_type=F32)


def _sigmoid(x):
    return 1.0 / (1.0 + jnp.exp(-x))


_GELU_C = math.sqrt(2.0 / math.pi)


def _gelu(x):
    x2 = x * x
    t = jnp.tanh(_GELU_C * (x + 0.044715 * x2 * x))
    val = 0.5 * x * (1.0 + t)
    grad = 0.5 * (1.0 + t) + 0.5 * x * (1.0 - t * t) * (_GELU_C * (1.0 + 3.0 * 0.044715 * x2))
    return val, grad


def _rstd(x):
    return lax.rsqrt(jnp.mean(x * x, axis=-1, keepdims=True) + EPS)


def _modulate(x, vec_ref):
    return (x * _rstd(x)) * vec_ref[0:1, :] * (1.0 + vec_ref[2:3, :]) + vec_ref[1:2, :]


def _modulate_bwd(x, dh, vec_ref, dvec_ref):
    gn, sh, sc = vec_ref[0:1, :], vec_ref[1:2, :], vec_ref[2:3, :]
    rstd = _rstd(x)
    r = x * rstd
    dvec_ref[0:1, :] += jnp.sum(dh * r * (1.0 + sc), axis=0, keepdims=True)
    dvec_ref[1:2, :] += jnp.sum(dh, axis=0, keepdims=True)
    dvec_ref[2:3, :] += jnp.sum(dh * r * gn, axis=0, keepdims=True)
    gm = gn * (1.0 + sc)
    dr = dh * gm
    dx = rstd * (dr - r * jnp.mean(dr * r, axis=-1, keepdims=True))
    return dx, r * gm + sh


def _adam(w, g, m, v):
    m = ADAM_B1 * m + (1.0 - ADAM_B1) * g
    v = ADAM_B2 * v + (1.0 - ADAM_B2) * (g * g)
    m_hat = m / (1.0 - ADAM_B1**ADAM_STEP)
    v_hat = v / (1.0 - ADAM_B2**ADAM_STEP)
    delta = -ADAM_LR * (m_hat / (jnp.sqrt(v_hat) + ADAM_EPS) + ADAM_WD * w)
    return delta, m, v


def _place():
    return lax.axis_index("x"), lax.axis_index("y"), lax.axis_index("c")


def _flip(k):
    x, y, c = _place()
    return (1 - x if k & 4 else x, 1 - y if k & 2 else y, 1 - c if k & 1 else c)


def _all_gather_small(arrs, name):
    n = len(arrs)

    def body(*refs):
        ins, outs = refs[:n], refs[n : 2 * n]
        send, recv, loc = refs[2 * n :]
        x, y, c = _place()
        me = 4 * x + 2 * y + c
        local = [pltpu.make_async_copy(ins[a], outs[a].at[me], loc.at[a]) for a in range(n)]
        for cp in local:
            cp.start()
        remote = []
        for a in range(n):
            for k in range(1, N_DEV):
                s = a * (N_DEV - 1) + k - 1
                remote.append(
                    pltpu.make_async_remote_copy(
                        src_ref=ins[a], dst_ref=outs[a].at[me], send_sem=send.at[s], recv_sem=recv.at[s],
                        device_id=_flip(k), device_id_type=MESH,
                    )
                )
        for cp in remote:
            cp.start()
        for cp in remote:
            cp.wait()
        for cp in local:
            cp.wait()

    vm = pl.BlockSpec(memory_space=pltpu.VMEM)
    return pl.pallas_call(
        body,
        name=name,
        out_shape=[jax.ShapeDtypeStruct((N_DEV,) + a.shape, a.dtype) for a in arrs],
        in_specs=[vm] * n,
        out_specs=[vm] * n,
        scratch_shapes=[
            pltpu.SemaphoreType.DMA((n * (N_DEV - 1),)),
            pltpu.SemaphoreType.DMA((n * (N_DEV - 1),)),
            pltpu.SemaphoreType.DMA((n,)),
        ],
        compiler_params=pltpu.CompilerParams(vmem_limit_bytes=VMEM_LIMIT_BYTES),
    )(*arrs)


class _Big:
    def __init__(self, full, sax, hax):
        self.full, self.sax, self.hax = tuple(full), sax, hax
        self.ns = full[sax] // N_CHIPS
        self.nh = full[hax] // 2
        self.shard = tuple(self.ns if i == sax else d for i, d in enumerate(full))
        self.half = tuple(self.nh if i == hax else d for i, d in enumerate(full))
        self.block = tuple(self.ns if i == sax else (self.nh if i == hax else d) for i, d in enumerate(full))

    def view(self, ref, chip=None, half=None):
        idx = [slice(None)] * len(self.full)
        if chip is not None:
            idx[self.sax] = pl.ds(chip * self.ns, self.ns)
        if half is not None:
            idx[self.hax] = pl.ds(half * self.nh, self.nh)
        return ref.at[tuple(idx)]


_ANY = pl.BlockSpec(memory_space=pl.ANY)


def _comm_call(body, name, ins, out_shapes, n_remote, n_local):
    return pl.pallas_call(
        body,
        name=name,
        out_shape=out_shapes,
        in_specs=[_ANY] * len(ins),
        out_specs=[_ANY] * len(out_shapes),
        scratch_shapes=[
            pltpu.SemaphoreType.DMA((n_remote,)),
            pltpu.SemaphoreType.DMA((n_remote,)),
            pltpu.SemaphoreType.DMA((max(n_local, 1),)),
        ],
    )(*ins)


def _gather_weights(shards, bigs):
    n = len(shards)

    def body(*refs):
        ins, outs = refs[:n], refs[n : 2 * n]
        send, recv, loc = refs[2 * n :]
        x, y, c = _place()
        chips = [(1 - x, y), (x, 1 - y), (1 - x, 1 - y)]

        def copy(a, k, chip, half, to, src=None):
            dst = bigs[a].view(outs[a], 2 * chip[0] + chip[1], half)
            return pltpu.make_async_remote_copy(
                src_ref=dst if src is None else src, dst_ref=dst, send_sem=send.at[6 * a + k], recv_sem=recv.at[6 * a + k],
                device_id=to, device_id_type=MESH,
            )

        local = [pltpu.make_async_copy(ins[a], bigs[a].view(outs[a], 2 * x + y), loc.at[a]) for a in range(n)]
        for cp in local:
            cp.start()
        first = [
            copy(a, j, (x, y), c, (*chip, c), src=bigs[a].view(ins[a], None, c))
            for j, chip in enumerate(chips)
            for a in range(n)
        ]
        for cp in first:
            cp.start()
        passed = []
        for j, chip in enumerate(chips):
            for a in range(n):
                copy(a, j, chip, c, (x, y, c)).wait_recv()
                fwd = copy(a, 3 + j, chip, c, (x, y, 1 - c))
                fwd.start()
                passed.append(fwd)
        for j, chip in enumerate(chips):
            for a in range(n):
                copy(a, 3 + j, chip, 1 - c, (x, y, c)).wait_recv()
        for cp in first + passed:
            cp.wait_send()
        for cp in local:
            cp.wait()

    outs = [jax.ShapeDtypeStruct(b.full, BF16) for b in bigs]
    return _comm_call(body, "gather_weights", shards, outs, 6 * n, n)


def _pair_exchange(grads, bigs):
    n = len(grads)

    def body(*refs):
        ins, outs = refs[:n], refs[n : 2 * n]
        send, recv, _ = refs[2 * n :]
        x, y, c = _place()
        cps = [
            pltpu.make_async_remote_copy(
                src_ref=bigs[a].view(ins[a], None, 1 - c), dst_ref=outs[a], send_sem=send.at[a], recv_sem=recv.at[a],
                device_id=(x, y, 1 - c), device_id_type=MESH,
            )
            for a in range(n)
        ]
        for cp in cps:
            cp.start()
        for cp in cps:
            cp.wait()

    outs = [jax.ShapeDtypeStruct(b.half, BF16) for b in bigs]
    return _comm_call(body, "grad_pair_exchange", grads, outs, n, 0)


def _chip_exchange(sums, bigs):
    n = len(sums)

    def body(*refs):
        ins, outs = refs[:n], refs[n : 2 * n]
        send, recv, loc = refs[2 * n :]
        x, y, c = _place()
        chips = [(1 - x, y), (x, 1 - y), (1 - x, 1 - y)]
        local = [pltpu.make_async_copy(bigs[a].view(ins[a], 2 * x + y), outs[a].at[0], loc.at[a]) for a in range(n)]
        for cp in local:
            cp.start()
        cps = [
            pltpu.make_async_remote_copy(
                src_ref=bigs[a].view(ins[a], 2 * chip[0] + chip[1]), dst_ref=outs[a].at[1 + j],
                send_sem=send.at[3 * a + j], recv_sem=recv.at[3 * a + j], device_id=(*chip, c), device_id_type=MESH,
            )
            for j, chip in enumerate(chips)
            for a in range(n)
        ]
        for cp in cps:
            cp.start()
        for cp in cps:
            cp.wait()
        for cp in local:
            cp.wait()

    outs = [jax.ShapeDtypeStruct((N_CHIPS,) + b.block, BF16) for b in bigs]
    return _comm_call(body, "grad_chip_exchange", sums, outs, 3 * n, n)


def _pair_broadcast(totals, bigs):
    n = len(totals)

    def body(*refs):
        ins, outs = refs[:n], refs[n : 2 * n]
        send, recv, loc = refs[2 * n :]
        x, y, c = _place()
        local = [pltpu.make_async_copy(ins[a], bigs[a].view(outs[a], None, c), loc.at[a]) for a in range(n)]
        for cp in local:
            cp.start()
        cps = []
        for a in range(n):
            mine = bigs[a].view(outs[a], None, c)
            theirs = bigs[a].view(outs[a], None, 1 - c)
            out = pltpu.make_async_remote_copy(
                src_ref=ins[a], dst_ref=mine, send_sem=send.at[a], recv_sem=recv.at[a], device_id=(x, y, 1 - c), device_id_type=MESH
            )
            out.start()
            cps.append((out, pltpu.make_async_remote_copy(
                src_ref=ins[a], dst_ref=theirs, send_sem=send.at[a], recv_sem=recv.at[a], device_id=(x, y, 1 - c), device_id_type=MESH
            )))
        for out, arrival in cps:
            out.wait_send()
            arrival.wait_recv()
        for cp in local:
            cp.wait()

    outs = [jax.ShapeDtypeStruct(b.shard, F32) for b in bigs]
    return _comm_call(body, "grad_pair_broadcast", totals, outs, n, n)


def _flat2d(a):
    return a.reshape(-1, a.shape[-1])


def _cast_bf16(w, name):
    w2 = _flat2d(w)
    rows, cols = w2.shape
    tr = _row_tile(rows, cols)

    def body(w_ref, o_ref):
        o_ref[...] = w_ref[...].astype(BF16)

    spec = pl.BlockSpec((tr, cols), lambda i: (i, 0))
    out = pl.pallas_call(
        body, name=name, grid=(rows // tr,), in_specs=[spec], out_specs=spec,
        out_shape=jax.ShapeDtypeStruct(w2.shape, BF16), compiler_params=_params("parallel"),
    )(w2)
    return out.reshape(w.shape)


def _pair_sum(g_full, recv_half, big, my_c, name):
    g2 = _flat2d(g_full)
    r2 = _flat2d(recv_half)
    rows, cols = r2.shape
    tr = _row_tile(rows, cols)
    nb = rows // tr
    by_rows = g2.shape[1] == cols

    def body(c_ref, g_ref, r_ref, o_ref):
        o_ref[...] = (g_ref[...].astype(F32) + r_ref[...].astype(F32)).astype(BF16)

    if by_rows:
        g_spec = pl.BlockSpec((tr, cols), lambda i, c_ref: (c_ref[0] * nb + i, 0))
    else:
        g_spec = pl.BlockSpec((tr, cols), lambda i, c_ref: (i, c_ref[0]))
    spec = pl.BlockSpec((tr, cols), lambda i, c_ref: (i, 0))
    out = pl.pallas_call(
        body, name=name,
        grid_spec=pltpu.PrefetchScalarGridSpec(num_scalar_prefetch=1, grid=(nb,), in_specs=[g_spec, spec], out_specs=spec),
        out_shape=jax.ShapeDtypeStruct(r2.shape, BF16), compiler_params=_params("parallel"),
    )(my_c.reshape(1).astype(jnp.int32), g2, r2)
    return out.reshape(recv_half.shape)


def _chip_sum(parts, name):
    p3 = parts.reshape(N_CHIPS, -1, parts.shape[-1])
    _, rows, cols = p3.shape
    tr = _row_tile(rows, cols)

    def body(p_ref, o_ref):
        acc = p_ref[0].astype(F32)
        for k in range(1, N_CHIPS):
            acc = acc + p_ref[k].astype(F32)
        o_ref[...] = acc

    out = pl.pallas_call(
        body, name=name, grid=(rows // tr,),
        in_specs=[pl.BlockSpec((N_CHIPS, tr, cols), lambda i: (0, i, 0))],
        out_specs=pl.BlockSpec((tr, cols), lambda i: (i, 0)),
        out_shape=jax.ShapeDtypeStruct((rows, cols), F32), compiler_params=_params("parallel"),
    )(p3)
    return out.reshape(parts.shape[1:])


def _adam_big(w, g, m, v, name):
    shape = w.shape
    w2, g2, m2, v2 = (_flat2d(a) for a in (w, g, m, v))
    rows, cols = w2.shape
    tr = _row_tile(rows, cols)

    def body(w_ref, g_ref, m_ref, v_ref, d_ref, mo_ref, vo_ref):
        d, mo, vo = _adam(w_ref[...], g_ref[...], m_ref[...], v_ref[...])
        d_ref[...] = d
        mo_ref[...] = mo
        vo_ref[...] = vo

    spec = pl.BlockSpec((tr, cols), lambda i: (i, 0))
    outs = pl.pallas_call(
        body, name=name, grid=(rows // tr,), in_specs=[spec] * 4, out_specs=[spec] * 3,
        out_shape=[jax.ShapeDtypeStruct(w2.shape, F32)] * 3, compiler_params=_params("parallel"),
    )(w2, g2, m2, v2)
    return tuple(o.reshape(shape) for o in outs)


def _mod_fwd(c_all, w_mod, b_cols):
    n_layers, d, n = w_mod.shape
    tn = _pick(n, (768, 512, 384, 256, 128))

    def body(c_ref, w_ref, b_ref, o_ref):
        cv = c_ref[...]
        ca = (cv * _sigmoid(cv)).astype(BF16)
        o_ref[0] = _dot(ca, w_ref[0].astype(BF16)) + b_ref[0]

    return pl.pallas_call(
        body, name="mod_fwd", grid=(n_layers, n // tn),
        in_specs=[
            pl.BlockSpec((N_DEV, d), lambda l, j: (0, 0)),
            pl.BlockSpec((1, d, tn), lambda l, j: (l, 0, j)),
            pl.BlockSpec((1, 1, tn), lambda l, j: (l, 0, j)),
        ],
        out_specs=pl.BlockSpec((1, N_DEV, tn), lambda l, j: (l, 0, j)),
        out_shape=jax.ShapeDtypeStruct((n_layers, N_DEV, n), F32), compiler_params=_params("parallel", "parallel"),
    )(c_all, w_mod, b_cols)


def _mod_bwd_adam(c_all_t, dmod_cols, w, m, v):
    n_layers, d, n = w.shape
    tn = _pick(n, (384, 256, 128))

    def body(c_ref, dm_ref, w_ref, m_ref, v_ref, g_ref, d_ref, mo_ref, vo_ref):
        cv = c_ref[...]
        ca = (cv * _sigmoid(cv)).astype(BF16)
        g = _dot(ca, dm_ref[0].astype(BF16))
        g_ref[0] = g
        dl, mo, vo = _adam(w_ref[0], g, m_ref[0], v_ref[0])
        d_ref[0] = dl
        mo_ref[0] = mo
        vo_ref[0] = vo

    wspec = pl.BlockSpec((1, d, tn), lambda l, j: (l, 0, j))
    return pl.pallas_call(
        body, name="mod_bwd_adam", grid=(n_layers, n // tn),
        in_specs=[
            pl.BlockSpec((d, N_DEV), lambda l, j: (0, 0)),
            pl.BlockSpec((1, N_DEV, tn), lambda l, j: (l, 0, j)),
            wspec, wspec, wspec,
        ],
        out_specs=[wspec] * 4,
        out_shape=[jax.ShapeDtypeStruct(w.shape, F32)] * 4, compiler_params=_params("parallel", "parallel"),
    )(c_all_t, dmod_cols, w, m, v)


def _ffn_fwd(x, vec, w_in, w_out, l, k):
    s, d = x.shape
    f = w_out.shape[2]
    tm = _pick(s, (1024, 512, 256, 128))
    tf = _pick(f, (256, 128))
    nf = f // tf

    def body(x_ref, vec_ref, wg_ref, wu_ref, wo_ref, xo_ref, g_ref, u_ref, y_ref, h_sc, acc_sc):
        j = pl.program_id(1)

        @pl.when(j == 0)
        def _():
            h_sc[...] = _modulate(x_ref[...], vec_ref).astype(BF16)
            acc_sc[...] = jnp.zeros_like(acc_sc)

        h = h_sc[...]
        g = _dot(h, wg_ref[0, 0])
        u = _dot(h, wu_ref[0, 0])
        g_ref[...] = g.astype(BF16)
        u_ref[...] = u.astype(BF16)
        a = (g * _sigmoid(g) * u).astype(BF16)
        acc_sc[...] += _dot(a, wo_ref[0, 0])

        @pl.when(j == nf - 1)
        def _():
            yv = acc_sc[...]
            xo_ref[...] = x_ref[...] + 0.5 * vec_ref[3:4, :] * yv
            y_ref[...] = yv.astype(BF16)

    row = pl.BlockSpec((tm, d), lambda i, j: (i, 0))
    hid = pl.BlockSpec((tm, tf), lambda i, j: (i, j))
    return pl.pallas_call(
        body, name=f"ffn_fwd_{l}{k}", grid=(s // tm, nf),
        in_specs=[
            row,
            pl.BlockSpec((8, d), lambda i, j: (0, 0)),
            pl.BlockSpec((1, 1, d, tf), lambda i, j: (l, k, 0, j)),
            pl.BlockSpec((1, 1, d, tf), lambda i, j: (l, k, 0, nf + j)),
            pl.BlockSpec((1, 1, tf, d), lambda i, j: (l, k, j, 0)),
        ],
        out_specs=[row, hid, hid, row],
        out_shape=[
            jax.ShapeDtypeStruct((s, d), F32),
            jax.ShapeDtypeStruct((s, f), BF16),
            jax.ShapeDtypeStruct((s, f), BF16),
            jax.ShapeDtypeStruct((s, d), BF16),
        ],
        scratch_shapes=[pltpu.VMEM((tm, d), BF16), pltpu.VMEM((tm, d), F32)],
        compiler_params=_params("parallel", "arbitrary"),
    )(x, vec, w_in, w_in, w_out)


def _ffn_bwd(dxo, x, vec, gg, uu, y, w_in, w_out, l, k):
    s, d = x.shape
    f = w_out.shape[2]
    tm = _pick(s, (512, 256, 128))
    tf = _pick(f, (256, 128))
    nf = f // tf

    def body(dxo_ref, x_ref, vec_ref, g_ref, u_ref, y_ref, wg_ref, wu_ref, wo_ref,
             dx_ref, dg_ref, du_ref, a_ref, h_ref, dy_ref, dvec_ref, acc_sc):
        i, j = pl.program_id(0), pl.program_id(1)

        @pl.when((i == 0) & (j == 0))
        def _():
            dvec_ref[...] = jnp.zeros_like(dvec_ref)

        @pl.when(j == 0)
        def _():
            dxo_v = dxo_ref[...]
            dy_ref[...] = (0.5 * vec_ref[3:4, :] * dxo_v).astype(BF16)
            dvec_ref[3:4, :] += 0.5 * jnp.sum(dxo_v * y_ref[...].astype(F32), axis=0, keepdims=True)
            acc_sc[...] = jnp.zeros_like(acc_sc)

        da = _dot_nt(dy_ref[...], wo_ref[0, 0])
        g = g_ref[...].astype(F32)
        u = u_ref[...].astype(F32)
        sig = _sigmoid(g)
        sl = g * sig
        a_ref[...] = (sl * u).astype(BF16)
        dg = (da * u * (sig * (1.0 + g * (1.0 - sig)))).astype(BF16)
        du = (da * sl).astype(BF16)
        dg_ref[...] = dg
        du_ref[...] = du
        acc_sc[...] += _dot_nt(dg, wg_ref[0, 0]) + _dot_nt(du, wu_ref[0, 0])

        @pl.when(j == nf - 1)
        def _():
            dx, h = _modulate_bwd(x_ref[...], acc_sc[...], vec_ref, dvec_ref)
            dx_ref[...] = dxo_ref[...] + dx
            h_ref[...] = h.astype(BF16)

    row = pl.BlockSpec((tm, d), lambda i, j: (i, 0))
    hid = pl.BlockSpec((tm, tf), lambda i, j: (i, j))
    vecs = pl.BlockSpec((8, d), lambda i, j: (0, 0))
    return pl.pallas_call(
        body, name=f"ffn_bwd_{l}{k}", grid=(s // tm, nf),
        in_specs=[
            row, row, vecs, hid, hid, row,
            pl.BlockSpec((1, 1, d, tf), lambda i, j: (l, k, 0, j)),
            pl.BlockSpec((1, 1, d, tf), lambda i, j: (l, k, 0, nf + j)),
            pl.BlockSpec((1, 1, tf, d), lambda i, j: (l, k, j, 0)),
        ],
        out_specs=[row, hid, hid, hid, row, row, vecs],
        out_shape=[
            jax.ShapeDtypeStruct((s, d), F32),
            jax.ShapeDtypeStruct((s, f), BF16),
            jax.ShapeDtypeStruct((s, f), BF16),
            jax.ShapeDtypeStruct((s, f), BF16),
            jax.ShapeDtypeStruct((s, d), BF16),
            jax.ShapeDtypeStruct((s, d), BF16),
            jax.ShapeDtypeStruct((8, d), F32),
        ],
        scratch_shapes=[pltpu.VMEM((tm, d), F32)],
        compiler_params=_params("arbitrary", "arbitrary"),
    )(dxo, x, vec, gg, uu, y, w_in, w_in, w_out)


def _grad_matmul(a, b, full_shape, lead, col0, prev, name):
    s, k1 = a.shape
    n = b.shape[1]
    tk = _pick(k1, (512, 256, 128))
    tn = _pick(n, (1408, 1024, 640, 512, 256, 128))
    nl = len(lead)
    assert col0 % tn == 0

    def body(*refs):
        a_ref, b_ref, o_ref = refs[0], refs[1], refs[-1]
        o_ref[(0,) * nl] = _dot_tn(a_ref[...], b_ref[...]).astype(BF16)

    in_specs = [pl.BlockSpec((s, tk), lambda i, j: (0, i)), pl.BlockSpec((s, tn), lambda i, j: (0, j))]
    ins = [a, b]
    aliases = {}
    if prev is not None:
        in_specs.append(_ANY)
        ins.append(prev)
        aliases = {2: 0}
    return pl.pallas_call(
        body, name=name, grid=(k1 // tk, n // tn), in_specs=in_specs,
        out_specs=pl.BlockSpec((1,) * nl + (tk, tn), lambda i, j: tuple(lead) + (i, col0 // tn + j)),
        out_shape=jax.ShapeDtypeStruct(full_shape, BF16), input_output_aliases=aliases,
        compiler_params=_params("parallel", "parallel"),
    )(*ins)


def _proj_mod_fwd(x, vec, w):
    s, d = x.shape
    n = w.shape[2]
    tm = _pick(s, (512, 256, 128))
    tn = _pick(n, (640, 512, 256, 128))

    def body(x_ref, vec_ref, w_ref, o_ref, h_sc):
        @pl.when(pl.program_id(1) == 0)
        def _():
            h_sc[...] = _modulate(x_ref[...], vec_ref).astype(BF16)

        o_ref[...] = _dot(h_sc[...], w_ref[0])

    return pl.pallas_call(
        body, name="ab_in_fwd", grid=(s // tm, n // tn),
        in_specs=[
            pl.BlockSpec((tm, d), lambda i, j: (i, 0)),
            pl.BlockSpec((8, d), lambda i, j: (0, 0)),
            pl.BlockSpec((1, d, tn), lambda i, j: (0, 0, j)),
        ],
        out_specs=pl.BlockSpec((tm, tn), lambda i, j: (i, j)),
        out_shape=jax.ShapeDtypeStruct((s, n), F32),
        scratch_shapes=[pltpu.VMEM((tm, d), BF16)],
        compiler_params=_params("parallel", "arbitrary"),
    )(x, vec, w)


def _proj_res_fwd(a, w, x, vec):
    s, kd = a.shape
    d = x.shape[1]
    tm = _pick(s, (512, 256, 128))

    def body(a_ref, w_ref, x_ref, vec_ref, xo_ref, y_ref):
        yv = _dot(a_ref[...], w_ref[0])
        xo_ref[...] = x_ref[...] + vec_ref[3:4, :] * yv
        y_ref[...] = yv.astype(BF16)

    row = pl.BlockSpec((tm, d), lambda i: (i, 0))
    return pl.pallas_call(
        body, name="ab_out_fwd", grid=(s // tm,),
        in_specs=[
            pl.BlockSpec((tm, kd), lambda i: (i, 0)),
            pl.BlockSpec((1, kd, d), lambda i: (0, 0, 0)),
            row,
            pl.BlockSpec((8, d), lambda i: (0, 0)),
        ],
        out_specs=[row, row],
        out_shape=[jax.ShapeDtypeStruct((s, d), F32), jax.ShapeDtypeStruct((s, d), BF16)],
        compiler_params=_params("parallel"),
    )(a, w, x, vec)


def _proj_res_bwd(dxo, y, vec, w):
    s, d = dxo.shape
    kd = w.shape[1]
    tm = _pick(s, (512, 256, 128))

    def body(dxo_ref, y_ref, vec_ref, w_ref, dy_ref, da_ref, dgate_ref):
        @pl.when(pl.program_id(0) == 0)
        def _():
            dgate_ref[...] = jnp.zeros_like(dgate_ref)

        dxo_v = dxo_ref[...]
        dy = (vec_ref[3:4, :] * dxo_v).astype(BF16)
        dy_ref[...] = dy
        dgate_ref[3:4, :] += jnp.sum(dxo_v * y_ref[...].astype(F32), axis=0, keepdims=True)
        da_ref[...] = _dot_nt(dy, w_ref[0]).astype(BF16)

    row = pl.BlockSpec((tm, d), lambda i: (i, 0))
    vecs = pl.BlockSpec((8, d), lambda i: (0, 0))
    return pl.pallas_call(
        body, name="ab_out_bwd", grid=(s // tm,),
        in_specs=[row, row, vecs, pl.BlockSpec((1, kd, d), lambda i: (0, 0, 0))],
        out_specs=[row, pl.BlockSpec((tm, kd), lambda i: (i, 0)), vecs],
        out_shape=[
            jax.ShapeDtypeStruct((s, d), BF16),
            jax.ShapeDtypeStruct((s, kd), BF16),
            jax.ShapeDtypeStruct((8, d), F32),
        ],
        compiler_params=_params("arbitrary"),
    )(dxo, y, vec, w)


def _proj_mod_bwd(dproj, w, x, vec, dxo, dvec_in):
    s, n = dproj.shape
    d = x.shape[1]
    tm = _pick(s, (512, 256, 128))

    def body(dp_ref, w_ref, x_ref, vec_ref, dxo_ref, dvi_ref, dx_ref, h_ref, dvec_ref):
        @pl.when(pl.program_id(0) == 0)
        def _():
            dvec_ref[...] = dvi_ref[...]

        dh = _dot_nt(dp_ref[...], w_ref[0])
        dx, h = _modulate_bwd(x_ref[...], dh, vec_ref, dvec_ref)
        dx_ref[...] = dxo_ref[...] + dx
        h_ref[...] = h.astype(BF16)

    row = pl.BlockSpec((tm, d), lambda i: (i, 0))
    vecs = pl.BlockSpec((8, d), lambda i: (0, 0))
    return pl.pallas_call(
        body, name="ab_in_bwd", grid=(s // tm,),
        in_specs=[pl.BlockSpec((tm, n), lambda i: (i, 0)), pl.BlockSpec((1, d, n), lambda i: (0, 0, 0)), row, vecs, row, vecs],
        out_specs=[row, row, vecs],
        out_shape=[jax.ShapeDtypeStruct((s, d), F32), jax.ShapeDtypeStruct((s, d), BF16), jax.ShapeDtypeStruct((8, d), F32)],
        compiler_params=_params("arbitrary"),
    )(dproj, w, x, vec, dxo, dvec_in)


def _tril(n):
    return lax.broadcasted_iota(jnp.int32, (n, n), 0) >= lax.broadcasted_iota(jnp.int32, (n, n), 1)


def _layernorm_stats(gv):
    mu = jnp.mean(gv, axis=-1, keepdims=True)
    cen = gv - mu
    rstd = lax.rsqrt(jnp.mean(cen * cen, axis=-1, keepdims=True) + EPS)
    return cen * rstd, rstd


def _shift_down(q, k, above_ref, c_cg, c_xb, first):
    width = q.shape[1]
    rows = lax.broadcasted_iota(jnp.int32, q.shape, 0)
    out = pltpu.roll(q, k, 0)
    for r in range(k):
        src = CONV_HALO - k + r
        above = above_ref[src : src + 1, c_cg : c_cg + width] * above_ref[src : src + 1, c_xb : c_xb + width]
        above = jnp.where(first, 0.0, above)
        out = jnp.where(rows == r, above, out)
    return out


def _ab_mix_fwd(proj, norm_v, w_s, b_rows, conv_w):
    s, n = proj.shape
    heads, chunk, _ = w_s.shape
    da = norm_v.shape[1]
    hd = da // heads
    db = conv_w.shape[1]
    tm = _pick(s, (512, 256, 128))

    def body(p_ref, ph_ref, nv_ref, ws_ref, b_ref, cw_ref, o_ref):
        first = pl.program_id(0) == 0
        gu, _ = _gelu(p_ref[:, 0:da])
        gv, _ = _gelu(p_ref[:, da : 2 * da])
        xhat, _ = _layernorm_stats(gv)
        vn = (xhat * nv_ref[...]).astype(BF16)
        mask = _tril(chunk)
        for hh in range(heads):
            wm = jnp.where(mask, ws_ref[hh], 0.0).astype(BF16)
            cols = slice(hh * hd, (hh + 1) * hd)
            for nn in range(tm // chunk):
                rows = slice(nn * chunk, (nn + 1) * chunk)
                z = _dot(wm, vn[rows, cols]) + b_ref[:, cols]
                o_ref[rows, cols] = (gu[rows, cols] * z).astype(BF16)
        c_cg, c_xb = 2 * da + db, 2 * da + 2 * db
        bg = p_ref[:, 2 * da : 2 * da + db]
        q = p_ref[:, c_cg : c_cg + db] * p_ref[:, c_xb : c_xb + db]
        q1 = _shift_down(q, 1, ph_ref, c_cg, c_xb, first)
        q2 = _shift_down(q, 2, ph_ref, c_cg, c_xb, first)
        conv = cw_ref[0:1, :] * q2 + cw_ref[1:2, :] * q1 + cw_ref[2:3, :] * q
        o_ref[:, da : da + db] = (bg * conv).astype(BF16)

    nh = tm // CONV_HALO
    return pl.pallas_call(
        body, name="ab_mix_fwd", grid=(s // tm,),
        in_specs=[
            pl.BlockSpec((tm, n), lambda i: (i, 0)),
            pl.BlockSpec((CONV_HALO, n), lambda i: (jnp.maximum(i * nh - 1, 0), 0)),
            pl.BlockSpec((1, da), lambda i: (0, 0)),
            pl.BlockSpec((heads, chunk, chunk), lambda i: (0, 0, 0)),
            pl.BlockSpec((chunk, da), lambda i: (0, 0)),
            pl.BlockSpec((3, db), lambda i: (0, 0)),
        ],
        out_specs=pl.BlockSpec((tm, da + db), lambda i: (i, 0)),
        out_shape=jax.ShapeDtypeStruct((s, da + db), BF16),
        compiler_params=_params("parallel"),
    )(proj, proj, norm_v, w_s, b_rows, conv_w)


def _ab_mix_bwd(proj, dcat, norm_v, w_s, b_rows, conv_w):
    s, n = proj.shape
    heads, chunk, _ = w_s.shape
    da = norm_v.shape[1]
    hd = da // heads
    db = conv_w.shape[1]
    tm = _pick(s, (512, 256, 128))
    nblk = s // tm
    dhalo = 2 * CONV_HALO

    def body(p_ref, pa_ref, pb_ref, dc_ref, dcb_ref, nv_ref, ws_ref, b_ref, cw_ref,
             dp_ref, dnv_ref, dws_ref, dzs_ref, dcw_ref, dvn_sc):
        i = pl.program_id(0)
        first, last = i == 0, i == nblk - 1

        @pl.when(first)
        def _():
            dnv_ref[...] = jnp.zeros_like(dnv_ref)
            dws_ref[...] = jnp.zeros_like(dws_ref)
            dzs_ref[...] = jnp.zeros_like(dzs_ref)
            dcw_ref[...] = jnp.zeros_like(dcw_ref)

        uu = p_ref[:, 0:da]
        gu, gu_grad = _gelu(uu)
        gv, gv_grad = _gelu(p_ref[:, da : 2 * da])
        xhat, rstd = _layernorm_stats(gv)
        nv = nv_ref[...]
        vn = (xhat * nv).astype(BF16)
        dya = dc_ref[:, 0:da].astype(F32)
        dz = (dya * gu).astype(BF16)
        mask = _tril(chunk)
        for hh in range(heads):
            wm = jnp.where(mask, ws_ref[hh], 0.0).astype(BF16)
            cols = slice(hh * hd, (hh + 1) * hd)
            dws = jnp.zeros((chunk, chunk), F32)
            for nn in range(tm // chunk):
                rows = slice(nn * chunk, (nn + 1) * chunk)
                z = _dot(wm, vn[rows, cols]) + b_ref[:, cols]
                dp_ref[rows, cols] = (dya[rows, cols] * z * gu_grad[rows, cols]).astype(BF16)
                dz_blk = dz[rows, cols]
                dws = dws + _dot_nt(dz_blk, vn[rows, cols])
                dzs_ref[:, cols] += dz_blk.astype(F32)
                dvn = _dot_tn(wm, dz_blk)
                dnv_ref[:, cols] += jnp.sum(dvn * xhat[rows, cols], axis=0, keepdims=True)
                dvn_sc[rows, cols] = dvn
            dws_ref[hh] += jnp.where(mask, dws, 0.0)
        dxhat = dvn_sc[...] * nv
        dgv = rstd * (dxhat - jnp.mean(dxhat, axis=-1, keepdims=True) - xhat * jnp.mean(dxhat * xhat, axis=-1, keepdims=True))
        dp_ref[:, da : 2 * da] = (dgv * gv_grad).astype(BF16)

        c_bg, c_cg, c_xb = 2 * da, 2 * da + db, 2 * da + 2 * db
        bg = p_ref[:, c_bg : c_bg + db]
        cg = p_ref[:, c_cg : c_cg + db]
        xb = p_ref[:, c_xb : c_xb + db]
        q = cg * xb
        q1 = _shift_down(q, 1, pa_ref, c_cg, c_xb, first)
        q2 = _shift_down(q, 2, pa_ref, c_cg, c_xb, first)
        dyb = dc_ref[:, da : da + db].astype(F32)
        conv = cw_ref[0:1, :] * q2 + cw_ref[1:2, :] * q1 + cw_ref[2:3, :] * q
        dp_ref[:, c_bg : c_bg + db] = (dyb * conv).astype(BF16)
        e = dyb * bg
        dcw_ref[0:1, :] += jnp.sum(e * q2, axis=0, keepdims=True)
        dcw_ref[1:2, :] += jnp.sum(e * q1, axis=0, keepdims=True)
        dcw_ref[2:3, :] += jnp.sum(e * q, axis=0, keepdims=True)
        rows = lax.broadcasted_iota(jnp.int32, e.shape, 0)
        dq = cw_ref[2:3, :] * e
        for kk in (1, 2):
            ek = pltpu.roll(e, tm - kk, 0)
            for r in range(kk):
                below = dcb_ref[r : r + 1, da : da + db].astype(F32) * pb_ref[r : r + 1, c_bg : c_bg + db]
                below = jnp.where(last, 0.0, below)
                ek = jnp.where(rows == tm - kk + r, below, ek)
            dq = dq + cw_ref[2 - kk : 3 - kk, :] * ek
        dp_ref[:, c_cg : c_cg + db] = (dq * xb).astype(BF16)
        dp_ref[:, c_xb : c_xb + db] = (dq * cg).astype(BF16)

    nh = tm // CONV_HALO
    nhb = tm // dhalo
    const2 = lambda i: (0, 0)
    return pl.pallas_call(
        body, name="ab_mix_bwd", grid=(nblk,),
        in_specs=[
            pl.BlockSpec((tm, n), lambda i: (i, 0)),
            pl.BlockSpec((CONV_HALO, n), lambda i: (jnp.maximum(i * nh - 1, 0), 0)),
            pl.BlockSpec((CONV_HALO, n), lambda i: (jnp.minimum((i + 1) * nh, s // CONV_HALO - 1), 0)),
            pl.BlockSpec((tm, da + db), lambda i: (i, 0)),
            pl.BlockSpec((dhalo, da + db), lambda i: (jnp.minimum((i + 1) * nhb, s // dhalo - 1), 0)),
            pl.BlockSpec((1, da), const2),
            pl.BlockSpec((heads, chunk, chunk), lambda i: (0, 0, 0)),
            pl.BlockSpec((chunk, da), const2),
            pl.BlockSpec((3, db), const2),
        ],
        out_specs=[
            pl.BlockSpec((tm, n), lambda i: (i, 0)),
            pl.BlockSpec((1, da), const2),
            pl.BlockSpec((heads, chunk, chunk), lambda i: (0, 0, 0)),
            pl.BlockSpec((chunk, da), const2),
            pl.BlockSpec((3, db), const2),
        ],
        out_shape=[
            jax.ShapeDtypeStruct((s, n), BF16),
            jax.ShapeDtypeStruct((1, da), F32),
            jax.ShapeDtypeStruct((heads, chunk, chunk), F32),
            jax.ShapeDtypeStruct((chunk, da), F32),
            jax.ShapeDtypeStruct((3, db), F32),
        ],
        scratch_shapes=[pltpu.VMEM((tm, da), F32)],
        compiler_params=_params("arbitrary"),
    )(proj, proj, proj, dcat, dcat, norm_v, w_s, b_rows, conv_w)


def _pool_counts(tm, i, w):
    t = i * tm + lax.broadcasted_iota(jnp.int32, (tm, 1), 0)
    return jnp.minimum(t + 1, w).astype(F32)


def _pool_fwd(x, vec, w_grp, scale):
    s, d = x.shape
    groups, gd, _ = w_grp.shape
    tm = _pick(s, (512, 256, 128))

    def body(x_ref, xa_ref, vec_ref, w_ref, sc_ref, xo_ref, p_ref, o_ref):
        i = pl.program_id(0)
        h = _modulate(x_ref[...], vec_ref)
        ha = jnp.where(i == 0, 0.0, _modulate(xa_ref[...], vec_ref))
        ext = jnp.concatenate([ha, h], axis=0)
        for gi, w in enumerate(POOL_WINDOWS):
            cols = slice(gi * gd, (gi + 1) * gd)
            acc = ext[:, cols]
            step = 1
            while step < w:
                acc = acc + pltpu.roll(acc, step, 0)
                step *= 2
            p = (acc[POOL_HALO:, :] / _pool_counts(tm, i, w) - h[:, cols]).astype(BF16)
            p_ref[:, cols] = p
            o_ref[:, cols] = _dot(p, w_ref[gi]).astype(BF16)
        xo_ref[...] = x_ref[...] + vec_ref[3:4, :] * (o_ref[...].astype(F32) * sc_ref[...])

    nh = tm // POOL_HALO
    row = pl.BlockSpec((tm, d), lambda i: (i, 0))
    return pl.pallas_call(
        body, name="pool_fwd", grid=(s // tm,),
        in_specs=[
            row,
            pl.BlockSpec((POOL_HALO, d), lambda i: (jnp.maximum(i * nh - 1, 0), 0)),
            pl.BlockSpec((8, d), lambda i: (0, 0)),
            pl.BlockSpec((groups, gd, gd), lambda i: (0, 0, 0)),
            pl.BlockSpec((1, d), lambda i: (0, 0)),
        ],
        out_specs=[row, row, row],
        out_shape=[jax.ShapeDtypeStruct((s, d), F32), jax.ShapeDtypeStruct((s, d), BF16), jax.ShapeDtypeStruct((s, d), BF16)],
        compiler_params=_params("parallel"),
    )(x, x, vec, w_grp, scale)


def _pool_bwd(dxo, x, vec, p, o, w_grp, scale):
    s, d = x.shape
    groups, gd, _ = w_grp.shape
    tm = _pick(s, (512, 256, 128))
    nblk = s // tm

    def body(dxo_ref, dxb_ref, x_ref, vec_ref, p_ref, o_ref, w_ref, sc_ref, dx_ref, dw_ref, dsc_ref, dvec_ref, dw_sc):
        i = pl.program_id(0)

        @pl.when(i == 0)
        def _():
            dw_sc[...] = jnp.zeros_like(dw_sc)
            dsc_ref[...] = jnp.zeros_like(dsc_ref)
            dvec_ref[...] = jnp.zeros_like(dvec_ref)

        gate, sc = vec_ref[3:4, :], sc_ref[...]
        dxo_v = dxo_ref[...]
        ov = o_ref[...].astype(F32)
        dvec_ref[3:4, :] += jnp.sum(dxo_v * (ov * sc), axis=0, keepdims=True)
        dy = gate * dxo_v
        dsc_ref[...] += jnp.sum(dy * ov, axis=0, keepdims=True)
        dout = (dy * sc).astype(BF16)
        dout_b = jnp.where(i == nblk - 1, 0.0, gate * dxb_ref[...] * sc).astype(BF16)
        for gi, w in enumerate(POOL_WINDOWS):
            cols = slice(gi * gd, (gi + 1) * gd)
            dw_sc[gi] += _dot_tn(p_ref[:, cols], dout[:, cols])
            wb = w_ref[gi]
            dp = _dot_nt(dout[:, cols], wb)
            dp_b = _dot_nt(dout_b[:, cols], wb)
            e = dp / _pool_counts(tm, i, w)
            t_below = (i + 1) * tm + lax.broadcasted_iota(jnp.int32, (POOL_HALO, 1), 0)
            e_b = dp_b / jnp.minimum(t_below + 1, w).astype(F32)
            acc = jnp.concatenate([e, e_b], axis=0)
            step = 1
            while step < w:
                acc = acc + pltpu.roll(acc, tm + POOL_HALO - step, 0)
                step *= 2
            dx_ref[:, cols] = acc[:tm, :] - dp
        dx, _ = _modulate_bwd(x_ref[...], dx_ref[...], vec_ref, dvec_ref)
        dx_ref[...] = dxo_v + dx

        @pl.when(i == nblk - 1)
        def _():
            dw_ref[...] = dw_sc[...].astype(BF16)

    nh = tm // POOL_HALO
    row = pl.BlockSpec((tm, d), lambda i: (i, 0))
    vecs = pl.BlockSpec((8, d), lambda i: (0, 0))
    wspec = pl.BlockSpec((groups, gd, gd), lambda i: (0, 0, 0))
    return pl.pallas_call(
        body, name="pool_bwd", grid=(nblk,),
        in_specs=[
            row,
            pl.BlockSpec((POOL_HALO, d), lambda i: (jnp.minimum((i + 1) * nh, s // POOL_HALO - 1), 0)),
            row, vecs, row, row, wspec,
            pl.BlockSpec((1, d), lambda i: (0, 0)),
        ],
        out_specs=[row, wspec, pl.BlockSpec((1, d), lambda i: (0, 0)), vecs],
        out_shape=[
            jax.ShapeDtypeStruct((s, d), F32),
            jax.ShapeDtypeStruct((groups, gd, gd), BF16),
            jax.ShapeDtypeStruct((1, d), F32),
            jax.ShapeDtypeStruct((8, d), F32),
        ],
        scratch_shapes=[pltpu.VMEM((groups, gd, gd), F32)],
        compiler_params=_params("arbitrary"),
    )(dxo, dxo, x, vec, p, o, w_grp, scale)


def _loss_head(x, gain, target):
    s, d = x.shape
    tm = _pick(s, (512, 256, 128))

    def body(x_ref, g_ref, t_ref, dx_ref, aux_ref):
        @pl.when(pl.program_id(0) == 0)
        def _():
            aux_ref[...] = jnp.zeros_like(aux_ref)

        xv = x_ref[...]
        rstd = _rstd(xv)
        r = xv * rstd
        gain_v = g_ref[...]
        err = r * gain_v - t_ref[...]
        aux_ref[1:2, :] += jnp.sum(err * err, axis=0, keepdims=True)
        dout = err * (1.0 / d)
        aux_ref[0:1, :] += jnp.sum(dout * r, axis=0, keepdims=True)
        dr = dout * gain_v
        dx_ref[...] = rstd * (dr - r * jnp.mean(dr * r, axis=-1, keepdims=True))

    row = pl.BlockSpec((tm, d), lambda i: (i, 0))
    return pl.pallas_call(
        body, name="loss_head", grid=(s // tm,),
        in_specs=[row, pl.BlockSpec((1, d), lambda i: (0, 0)), row],
        out_specs=[row, pl.BlockSpec((8, d), lambda i: (0, 0))],
        out_shape=[jax.ShapeDtypeStruct((s, d), F32), jax.ShapeDtypeStruct((8, d), F32)],
        compiler_params=_params("arbitrary"),
    )(x, gain, target)


def _small_adam(gathered, gathered_ws, layout, smalls, chip):
    names = list(smalls)
    n = len(names)

    def body(*refs):
        chip_ref, g_ref, gws_ref = refs[0], refs[1], refs[2]
        wmv = refs[3 : 3 + 3 * n]
        outs = refs[3 + 3 * n : 3 + 7 * n]
        total = refs[-1]
        total[...] = g_ref[0]
        for kdev in range(1, N_DEV):
            total[...] += g_ref[kdev]
        total_ws = gws_ref[0]
        for kdev in range(1, N_DEV):
            total_ws = total_ws + gws_ref[kdev]
        my_chip = chip_ref[0]
        for a, name in enumerate(names):
            w_ref, m_ref, v_ref = wmv[3 * a : 3 * a + 3]
            if name == "ab_w_s":
                g = total_ws
            else:
                row0, rows, col0, cols = layout[name]
                if col0 is None:
                    g = jnp.zeros((rows, cols), F32)
                    for j in range(N_CHIPS):
                        g = g + jnp.where(my_chip == j, total[row0 : row0 + rows, j * cols : (j + 1) * cols], 0.0)
                else:
                    g = total[row0 : row0 + rows, col0 : col0 + cols]
            dl, mo, vo = _adam(w_ref[...], g, m_ref[...], v_ref[...])
            outs[4 * a][...] = g
            outs[4 * a + 1][...] = dl
            outs[4 * a + 2][...] = mo
            outs[4 * a + 3][...] = vo

    vm = pl.BlockSpec(memory_space=pltpu.VMEM)
    ins = [gathered, gathered_ws]
    out_shapes = []
    for name in names:
        ins.extend(smalls[name])
        out_shapes.extend([jax.ShapeDtypeStruct(smalls[name][0].shape, F32)] * 4)
    res = pl.pallas_call(
        body, name="small_adam",
        grid_spec=pltpu.PrefetchScalarGridSpec(
            num_scalar_prefetch=1, grid=(1,),
            in_specs=[pl.BlockSpec(a.shape, functools.partial(lambda nd, i, c: (0,) * nd, a.ndim)) for a in ins],
            out_specs=[pl.BlockSpec(o.shape, functools.partial(lambda nd, i, c: (0,) * nd, len(o.shape))) for o in out_shapes],
            scratch_shapes=[pltpu.VMEM(gathered.shape[1:], F32)],
        ),
        out_shape=out_shapes, compiler_params=_params("arbitrary"),
    )(chip.reshape(1).astype(jnp.int32), *ins)
    return {name: res[4 * a : 4 * a + 4] for a, name in enumerate(names)}


def _pad_rows(a, rows=8):
    extra = (-a.shape[0]) % rows
    return jnp.pad(a, ((0, extra), (0, 0))) if extra else a


def _pad_cols(a, cols):
    return jnp.pad(a, ((0, 0), (0, cols - a.shape[1]))) if a.shape[1] < cols else a


def kernel(x, c, norm_g, w_mod, b_mod, w_ffn_in, w_ffn_out, ab_w_in, ab_norm_v, ab_w_s, ab_b_s, ab_conv_w, ab_w_out, pool_w_grp, pool_scale, final_g, loss_target, m_norm_g, m_w_mod, m_b_mod, m_w_ffn_in, m_w_ffn_out, m_ab_w_in, m_ab_norm_v, m_ab_w_s, m_ab_b_s, m_ab_conv_w, m_ab_w_out, m_pool_w_grp, m_pool_scale, m_final_g, v_norm_g, v_w_mod, v_b_mod, v_w_ffn_in, v_w_ffn_out, v_ab_w_in, v_ab_norm_v, v_ab_w_s, v_ab_b_s, v_ab_conv_w, v_ab_w_out, v_pool_w_grp, v_pool_scale, v_final_g):
    ix, iy, ic = _place()
    chip = 2 * ix + iy
    me = 4 * ix + 2 * iy + ic
    s, d = x.shape[1], x.shape[2]
    x0 = x.reshape(s, d)
    target = loss_target.reshape(s, d)
    n_layers = norm_g.shape[0]
    dq = d // N_CHIPS
    heads, chunk = ab_w_s.shape[1], ab_w_s.shape[2]
    da = ab_norm_v.shape[1]
    db = ab_conv_w.shape[2] * N_CHIPS
    assert n_layers == 2 and da % heads == 0

    cw_pad = _pad_cols(ab_conv_w.reshape(3, db // N_CHIPS), dq)
    packed = jnp.concatenate(
        [_pad_rows(c.reshape(N_CHIPS, dq)), _pad_rows(norm_g.reshape(-1, dq)), _pad_rows(pool_scale.reshape(1, dq)), _pad_rows(cw_pad)],
        axis=0,
    )
    (small_all,) = _all_gather_small([packed], "gather_small_inputs")
    by_chip = small_all[0::2]
    c_all = small_all[:, 0:N_CHIPS, :].reshape(N_DEV, d)
    norm_full = by_chip[:, 8 : 8 + 3 * n_layers, :].transpose(1, 0, 2).reshape(3 * n_layers, d)
    pool_scale_full = by_chip[:, 16:17, :].transpose(1, 0, 2).reshape(1, d)
    conv_full = by_chip[:, 24:27, : db // N_CHIPS].transpose(1, 0, 2).reshape(3, db)

    ncol = w_mod.shape[2]
    b_cols = lax.dynamic_slice(b_mod, (0, chip * ncol), (n_layers, ncol)).reshape(n_layers, 1, ncol)
    mod_cols = _mod_fwd(c_all, w_mod, b_cols)
    (mod_all,) = _all_gather_small([mod_cols.reshape(n_layers * N_DEV, ncol)], "gather_mod")
    mod_mine = lax.dynamic_index_in_dim(mod_all[0::2].reshape(N_CHIPS, n_layers, N_DEV, ncol), me, axis=2, keepdims=False)
    mod = mod_mine.transpose(1, 0, 2).reshape(n_layers, 3, 3, d)

    vecs = {
        (l, sub): _pad_rows(jnp.concatenate([norm_full[3 * l + sub][None], mod[l, sub]], axis=0))
        for l in range(n_layers)
        for sub in range(3)
    }

    def vec_of(l, sub):
        return vecs[l, sub]

    bigs = {
        "w_ffn_in": _Big((n_layers, 2, d, w_ffn_in.shape[3] * N_CHIPS), 3, 0),
        "w_ffn_out": _Big((n_layers, 2, w_ffn_out.shape[2] * N_CHIPS, d), 2, 0),
        "ab_w_in": _Big((1, d, ab_w_in.shape[2] * N_CHIPS), 2, 1),
        "ab_w_out": _Big((1, ab_w_out.shape[1] * N_CHIPS, d), 1, 2),
        "pool_w_grp": _Big((pool_w_grp.shape[1], pool_w_grp.shape[2] * N_CHIPS, pool_w_grp.shape[3]), 1, 0),
    }
    big_names = list(bigs)
    big_w = {"w_ffn_in": w_ffn_in, "w_ffn_out": w_ffn_out, "ab_w_in": ab_w_in, "ab_w_out": ab_w_out, "pool_w_grp": pool_w_grp[0]}
    big_m = {"w_ffn_in": m_w_ffn_in, "w_ffn_out": m_w_ffn_out, "ab_w_in": m_ab_w_in, "ab_w_out": m_ab_w_out, "pool_w_grp": m_pool_w_grp[0]}
    big_v = {"w_ffn_in": v_w_ffn_in, "w_ffn_out": v_w_ffn_out, "ab_w_in": v_ab_w_in, "ab_w_out": v_ab_w_out, "pool_w_grp": v_pool_w_grp[0]}
    shards16 = [_cast_bf16(big_w[nm], "cast_" + nm) for nm in big_names]
    full16 = dict(zip(big_names, _gather_weights(shards16, [bigs[nm] for nm in big_names])))
    wf_in, wf_out = full16["w_ffn_in"], full16["w_ffn_out"]
    f_hidden = wf_out.shape[2]

    b_rows = jnp.broadcast_to(ab_b_s[0].T[:, :, None], (chunk, heads, da // heads)).reshape(chunk, da)
    saved = {}
    xs = x0
    for l in range(n_layers):
        saved[l, 0, "x"] = xs
        xs, gg, uu, yb = _ffn_fwd(xs, vec_of(l, 0), wf_in, wf_out, l, 0)
        saved[l, 0, "act"] = (gg, uu, yb)
        saved[l, 1, "x"] = xs
        if l % 2 == 0:
            proj = _proj_mod_fwd(xs, vec_of(l, 1), full16["ab_w_in"])
            cat = _ab_mix_fwd(proj, ab_norm_v, ab_w_s[0], b_rows, conv_full)
            xs, yb = _proj_res_fwd(cat, full16["ab_w_out"], xs, vec_of(l, 1))
            saved[l, 1, "act"] = (proj, cat, yb)
        else:
            xs, pp, oo = _pool_fwd(xs, vec_of(l, 1), full16["pool_w_grp"], pool_scale_full)
            saved[l, 1, "act"] = (pp, oo)
        saved[l, 2, "x"] = xs
        xs, gg, uu, yb = _ffn_fwd(xs, vec_of(l, 2), wf_in, wf_out, l, 1)
        saved[l, 2, "act"] = (gg, uu, yb)
    dxs, aux = _loss_head(xs, final_g.reshape(1, d), target)
    loss = lax.psum(0.5 * jnp.sum(aux[1]) / d, ("x", "y", "c"))

    grads = {nm: None for nm in big_names}
    dvecs = {}

    def ffn_back(dxs, l, sub, k):
        gg, uu, yb = saved[l, sub, "act"]
        dxs, dg, du, a, h, dy, dvec = _ffn_bwd(dxs, saved[l, sub, "x"], vec_of(l, sub), gg, uu, yb, wf_in, wf_out, l, k)
        gin = bigs["w_ffn_in"].full
        grads["w_ffn_in"] = _grad_matmul(h, dg, gin, (l, k), 0, grads["w_ffn_in"], f"dw_in_g_{l}{k}")
        grads["w_ffn_in"] = _grad_matmul(h, du, gin, (l, k), f_hidden, grads["w_ffn_in"], f"dw_in_u_{l}{k}")
        grads["w_ffn_out"] = _grad_matmul(a, dy, bigs["w_ffn_out"].full, (l, k), 0, grads["w_ffn_out"], f"dw_out_{l}{k}")
        dvecs[l, sub] = dvec
        return dxs

    small_g = {}
    for l in reversed(range(n_layers)):
        dxs = ffn_back(dxs, l, 2, 1)
        if l % 2 == 0:
            proj, cat, yb = saved[l, 1, "act"]
            dy, dcat, dgate = _proj_res_bwd(dxs, yb, vec_of(l, 1), full16["ab_w_out"])
            grads["ab_w_out"] = _grad_matmul(cat, dy, bigs["ab_w_out"].full, (0,), 0, None, "dw_ab_out")
            dproj, small_g["ab_norm_v"], small_g["ab_w_s"], dzs, small_g["ab_conv_w"] = _ab_mix_bwd(
                proj, dcat, ab_norm_v, ab_w_s[0], b_rows, conv_full
            )
            small_g["ab_b_s"] = dzs.reshape(chunk, heads, da // heads).sum(axis=2).T
            dxs, h, dvecs[l, 1] = _proj_mod_bwd(dproj, full16["ab_w_in"], saved[l, 1, "x"], vec_of(l, 1), dxs, dgate)
            grads["ab_w_in"] = _grad_matmul(h, dproj, bigs["ab_w_in"].full, (0,), 0, None, "dw_ab_in")
        else:
            pp, oo = saved[l, 1, "act"]
            dxs, grads["pool_w_grp"], small_g["pool_scale"], dvecs[l, 1] = _pool_bwd(
                dxs, saved[l, 1, "x"], vec_of(l, 1), pp, oo, full16["pool_w_grp"], pool_scale_full
            )
        dxs = ffn_back(dxs, l, 0, 0)
    grad_x = dxs.reshape(x.shape)

    blist = [bigs[nm] for nm in big_names]
    recv_half = _pair_exchange([grads[nm] for nm in big_names], blist)
    chip_sums = [_pair_sum(grads[nm], r, bigs[nm], ic, "pair_sum_" + nm) for nm, r in zip(big_names, recv_half)]
    parts = _chip_exchange(chip_sums, blist)
    totals = [_chip_sum(p, "chip_sum_" + nm) for nm, p in zip(big_names, parts)]
    g_shards = dict(zip(big_names, _pair_broadcast(totals, blist)))

    out = {}
    for nm in big_names:
        g = g_shards[nm].reshape(big_w[nm].shape)
        out[nm] = (g,) + _adam_big(big_w[nm], g, big_m[nm], big_v[nm], "adam_" + nm)
    out["pool_w_grp"] = tuple(a[None] for a in out["pool_w_grp"])

    dgain = jnp.stack([dvecs[l, sub][0] for l in range(n_layers) for sub in range(3)])
    dmod = jnp.concatenate([dvecs[l, sub][1:4] for l in range(n_layers) for sub in range(3)], axis=0)
    pieces = [
        dgain, dmod, aux[0:1], _pad_cols(small_g["ab_norm_v"], d), small_g["pool_scale"],
        _pad_cols(small_g["ab_conv_w"], d), _pad_cols(small_g["ab_b_s"], d),
    ]
    row0, layout_rows = 0, []
    for pc in pieces:
        layout_rows.append(row0)
        row0 += -(-pc.shape[0] // 8) * 8
    packed_g = jnp.concatenate([_pad_rows(pc) for pc in pieces], axis=0)
    g_all, gws_all = _all_gather_small([packed_g, small_g["ab_w_s"].reshape(heads * chunk, chunk)], "gather_small_grads")
    layout = {
        "norm_g": (layout_rows[0], 3 * n_layers, None, dq),
        "b_mod": (layout_rows[1], 9 * n_layers, 0, d),
        "final_g": (layout_rows[2], 1, 0, d),
        "ab_norm_v": (layout_rows[3], 1, 0, da),
        "pool_scale": (layout_rows[4], 1, None, dq),
        "ab_conv_w": (layout_rows[5], 3, None, db // N_CHIPS),
        "ab_b_s": (layout_rows[6], heads, 0, chunk),
    }
    shapes2d = {
        "norm_g": (3 * n_layers, dq), "b_mod": (9 * n_layers, d), "final_g": (1, d), "ab_norm_v": (1, da),
        "pool_scale": (1, dq), "ab_conv_w": (3, db // N_CHIPS), "ab_b_s": (heads, chunk), "ab_w_s": (heads * chunk, chunk),
    }
    small_w = {"norm_g": (norm_g, m_norm_g, v_norm_g), "b_mod": (b_mod, m_b_mod, v_b_mod), "final_g": (final_g, m_final_g, v_final_g),
               "ab_norm_v": (ab_norm_v, m_ab_norm_v, v_ab_norm_v), "pool_scale": (pool_scale, m_pool_scale, v_pool_scale),
               "ab_conv_w": (ab_conv_w, m_ab_conv_w, v_ab_conv_w), "ab_b_s": (ab_b_s, m_ab_b_s, v_ab_b_s), "ab_w_s": (ab_w_s, m_ab_w_s, v_ab_w_s)}
    smalls = {nm: tuple(a.reshape(shapes2d[nm]) for a in wmv) for nm, wmv in small_w.items()}
    small_out = _small_adam(g_all, gws_all, layout, smalls, chip)
    for nm, res in small_out.items():
        out[nm] = tuple(a.reshape(small_w[nm][0].shape) for a in res)

    dmod_all = g_all[:, layout_rows[1] : layout_rows[1] + 9 * n_layers, :].reshape(N_DEV, n_layers, 9 * d)
    dmod_cols = lax.dynamic_slice(dmod_all, (0, 0, chip * ncol), (N_DEV, n_layers, ncol)).transpose(1, 0, 2)
    out["w_mod"] = tuple(_mod_bwd_adam(c_all.T, dmod_cols, w_mod, m_w_mod, v_w_mod))

    order = ["norm_g", "w_mod", "b_mod", "w_ffn_in", "w_ffn_out", "ab_w_in", "ab_norm_v", "ab_w_s", "ab_b_s", "ab_conv_w", "ab_w_out", "pool_w_grp", "pool_scale", "final_g"]
    return (loss, grad_x, *[out[nm][0] for nm in order], *[out[nm][1] for nm in order], *[out[nm][2] for nm in order], *[out[nm][3] for nm in order])
```

```python
import functools
import math

import jax
import jax.numpy as jnp
from jax import lax
from jax.experimental import pallas as pl
from jax.experimental.pallas import tpu as pltpu

F32 = jnp.float32
BF16 = jnp.bfloat16
MESH = pl.DeviceIdType.MESH

EPS = 1e-6
ADAM_LR = 0.001
ADAM_B1 = 0.9
ADAM_B2 = 0.999
ADAM_EPS = 1e-08
ADAM_WD = 0.01
ADAM_STEP = 10
POOL_WINDOWS = (2, 4, 8, 16)
POOL_HALO = 16
CONV_HALO = 8
N_CHIPS = 4
N_DEV = 8
VMEM_LIMIT_BYTES = 48 * 1024 * 1024
EW_BLOCK_ELEMS = 256 * 1024


def _params(*sem):
    return pltpu.CompilerParams(dimension_semantics=sem or None, vmem_limit_bytes=VMEM_LIMIT_BYTES)


def _pick(n, prefs):
    for p in prefs:
        if p <= n and n % p == 0:
            return p
    return n


def _row_tile(rows, cols):
    best = None
    for d in range(16, rows + 1, 16):
        if rows % d == 0 and d * cols <= EW_BLOCK_ELEMS:
            best = d
    return best or rows


def _dot(a, b):
    return jnp.dot(a, b, preferred_element_type=F32)


def _dot_nt(a, b):
    return lax.dot_general(a, b, (((1,), (1,)), ((), ())), preferred_element_type=F32)


def _dot_tn(a, b):
    return lax.dot_general(a, b, (((0,), (0,)), ((), ())), preferred_element_type=F32)


def _sigmoid(x):
    return 1.0 / (1.0 + jnp.exp(-x))


_GELU_C = math.sqrt(2.0 / math.pi)


def _gelu(x):
    x2 = x * x
    t = jnp.tanh(_GELU_C * (x + 0.044715 * x2 * x))
    val = 0.5 * x * (1.0 + t)
    grad = 0.5 * (1.0 + t) + 0.5 * x * (1.0 - t * t) * (_GELU_C * (1.0 + 3.0 * 0.044715 * x2))
    return val, grad


def _rstd(x):
    return lax.rsqrt(jnp.mean(x * x, axis=-1, keepdims=True) + EPS)


def _modulate(x, vec_ref):
    return (x * _rstd(x)) * vec_ref[0:1, :] * (1.0 + vec_ref[2:3, :]) + vec_ref[1:2, :]


def _modulate_bwd(x, dh, vec_ref, dvec_ref):
    gn, sh, sc = vec_ref[0:1, :], vec_ref[1:2, :], vec_ref[2:3, :]
    rstd = _rstd(x)
    r = x * rstd
    dvec_ref[0:1, :] += jnp.sum(dh * r * (1.0 + sc), axis=0, keepdims=True)
    dvec_ref[1:2, :] += jnp.sum(dh, axis=0, keepdims=True)
    dvec_ref[2:3, :] += jnp.sum(dh * r * gn, axis=0, keepdims=True)
    gm = gn * (1.0 + sc)
    dr = dh * gm
    dx = rstd * (dr - r * jnp.mean(dr * r, axis=-1, keepdims=True))
    return dx, r * gm + sh


def _adam(w, g, m, v):
    m = ADAM_B1 * m + (1.0 - ADAM_B1) * g
    v = ADAM_B2 * v + (1.0 - ADAM_B2) * (g * g)
    m_hat = m / (1.0 - ADAM_B1**ADAM_STEP)
    v_hat = v / (1.0 - ADAM_B2**ADAM_STEP)
    delta = -ADAM_LR * (m_hat / (jnp.sqrt(v_hat) + ADAM_EPS) + ADAM_WD * w)
    return delta, m, v


def _place():
    return lax.axis_index("x"), lax.axis_index("y"), lax.axis_index("c")


def _flip(k):
    x, y, c = _place()
    return (1 - x if k & 4 else x, 1 - y if k & 2 else y, 1 - c if k & 1 else c)


def _all_gather_small(arrs, name):
    n = len(arrs)

    def body(*refs):
        ins, outs = refs[:n], refs[n : 2 * n]
        send, recv, loc = refs[2 * n :]
        x, y, c = _place()
        me = 4 * x + 2 * y + c
        local = [pltpu.make_async_copy(ins[a], outs[a].at[me], loc.at[a]) for a in range(n)]
        for cp in local:
            cp.start()
        remote = []
        for a in range(n):
            for k in range(1, N_DEV):
                s = a * (N_DEV - 1) + k - 1
                remote.append(
                    pltpu.make_async_remote_copy(
                        src_ref=ins[a], dst_ref=outs[a].at[me], send_sem=send.at[s], recv_sem=recv.at[s],
                        device_id=_flip(k), device_id_type=MESH,
                    )
                )
        for cp in remote:
            cp.start()
        for cp in remote:
            cp.wait()
        for cp in local:
            cp.wait()

    vm = pl.BlockSpec(memory_space=pltpu.VMEM)
    return pl.pallas_call(
        body,
        name=name,
        out_shape=[jax.ShapeDtypeStruct((N_DEV,) + a.shape, a.dtype) for a in arrs],
        in_specs=[vm] * n,
        out_specs=[vm] * n,
        scratch_shapes=[
            pltpu.SemaphoreType.DMA((n * (N_DEV - 1),)),
            pltpu.SemaphoreType.DMA((n * (N_DEV - 1),)),
            pltpu.SemaphoreType.DMA((n,)),
        ],
        compiler_params=pltpu.CompilerParams(vmem_limit_bytes=VMEM_LIMIT_BYTES),
    )(*arrs)


class _Big:
    KINDS = {"full": (True, True), "half": (True, False), "shard": (False, True), "block": (False, False)}

    def __init__(self, full, sax, hax):
        nd = len(full)
        to3 = lambda ax: 0 if ax < nd - 2 else ax - (nd - 2) + 1
        self.f3 = (math.prod(full[:-2]), full[-2], full[-1])
        self.s3, self.h3 = to3(sax), to3(hax)
        assert self.s3 != self.h3
        self.bd = tuple(self.f3[a] // (N_CHIPS if a == self.s3 else 1) // (2 if a == self.h3 else 1) for a in range(3))
        self.tile = (1, _row_tile(self.bd[1], self.bd[2]), self.bd[2])
        self.grid = tuple(self.bd[a] // self.tile[a] for a in range(3))

    def dims(self, kind):
        chips, halves = self.KINDS[kind]
        return tuple(
            self.bd[a] * (N_CHIPS if chips and a == self.s3 else 1) * (2 if halves and a == self.h3 else 1) for a in range(3)
        )

    def view(self, ref, chip=None, half=None):
        idx = [slice(None)] * 3
        if chip is not None:
            idx[self.s3] = pl.ds(chip * self.bd[self.s3], self.bd[self.s3])
        if half is not None:
            idx[self.h3] = pl.ds(half * self.bd[self.h3], self.bd[self.h3])
        return ref.at[tuple(idx)]

    def spec(self, chip_from=None, half_from=None, lead=()):
        extra = "grid" in (chip_from, half_from)

        def index(*args):
            pref, idx = args[-1], list(args[int(extra) : -1])
            if chip_from:
                idx[self.s3] += (pref[0] if chip_from == "pref" else args[0]) * self.grid[self.s3]
            if half_from:
                idx[self.h3] += (pref[1] if half_from == "pref" else args[0]) * self.grid[self.h3]
            return (0,) * len(lead) + tuple(idx)

        return pl.BlockSpec(tuple(lead) + self.tile, index)


_ANY = pl.BlockSpec(memory_space=pl.ANY)


def _comm_call(body, name, ins, out_shapes, n_remote, aliases=None):
    return pl.pallas_call(
        body,
        name=name,
        out_shape=out_shapes,
        in_specs=[_ANY] * len(ins),
        out_specs=[_ANY] * len(out_shapes),
        input_output_aliases=aliases or {},
        scratch_shapes=[pltpu.SemaphoreType.DMA((n_remote,)), pltpu.SemaphoreType.DMA((n_remote,))],
    )(*ins)


def _gather_weights(fulls, bigs):
    n = len(fulls)

    def body(*refs):
        outs = refs[n : 2 * n]
        send, recv = refs[2 * n :]
        x, y, c = _place()
        chips = [(1 - x, y), (x, 1 - y), (1 - x, 1 - y)]

        def copy(a, k, chip, half, to):
            blk = bigs[a].view(outs[a], 2 * chip[0] + chip[1], half)
            return pltpu.make_async_remote_copy(
                src_ref=blk, dst_ref=blk, send_sem=send.at[6 * a + k], recv_sem=recv.at[6 * a + k], device_id=to, device_id_type=MESH
            )

        first = [copy(a, j, (x, y), c, (*chip, c)) for j, chip in enumerate(chips) for a in range(n)]
        for cp in first:
            cp.start()
        passed = []
        for j, chip in enumerate(chips):
            for a in range(n):
                copy(a, j, chip, c, (x, y, c)).wait_recv()
                fwd = copy(a, 3 + j, chip, c, (x, y, 1 - c))
                fwd.start()
                passed.append(fwd)
        for j, chip in enumerate(chips):
            for a in range(n):
                copy(a, 3 + j, chip, 1 - c, (x, y, c)).wait_recv()
        for cp in first + passed:
            cp.wait_send()

    outs = [jax.ShapeDtypeStruct(b.dims("full"), BF16) for b in bigs]
    return _comm_call(body, "gather_weights", fulls, outs, 6 * n, {a: a for a in range(n)})


def _pair_exchange(grads, bigs):
    n = len(grads)

    def body(*refs):
        ins, outs = refs[:n], refs[n : 2 * n]
        send, recv = refs[2 * n :]
        x, y, c = _place()
        cps = [
            pltpu.make_async_remote_copy(
                src_ref=bigs[a].view(ins[a], None, 1 - c), dst_ref=outs[a], send_sem=send.at[a], recv_sem=recv.at[a],
                device_id=(x, y, 1 - c), device_id_type=MESH,
            )
            for a in range(n)
        ]
        for cp in cps:
            cp.start()
        for cp in cps:
            cp.wait()

    outs = [jax.ShapeDtypeStruct(b.dims("half"), BF16) for b in bigs]
    return _comm_call(body, "grad_pair_exchange", grads, outs, n)


def _chip_exchange(sums, bigs):
    n = len(sums)

    def body(*refs):
        ins, outs = refs[:n], refs[n : 2 * n]
        send, recv = refs[2 * n :]
        x, y, c = _place()
        chips = [(1 - x, y), (x, 1 - y), (1 - x, 1 - y)]
        cps = [
            pltpu.make_async_remote_copy(
                src_ref=bigs[a].view(ins[a], 2 * chip[0] + chip[1]), dst_ref=outs[a].at[j],
                send_sem=send.at[3 * a + j], recv_sem=recv.at[3 * a + j], device_id=(*chip, c), device_id_type=MESH,
            )
            for j, chip in enumerate(chips)
            for a in range(n)
        ]
        for cp in cps:
            cp.start()
        for cp in cps:
            cp.wait()

    outs = [jax.ShapeDtypeStruct((N_CHIPS - 1,) + b.dims("block"), BF16) for b in bigs]
    return _comm_call(body, "grad_chip_exchange", sums, outs, 3 * n)


def _pair_broadcast(shards, bigs):
    n = len(shards)

    def body(*refs):
        outs = refs[n : 2 * n]
        send, recv = refs[2 * n :]
        x, y, c = _place()
        cps = []
        for a in range(n):
            out = pltpu.make_async_remote_copy(
                src_ref=bigs[a].view(outs[a], None, c), dst_ref=bigs[a].view(outs[a], None, c),
                send_sem=send.at[a], recv_sem=recv.at[a], device_id=(x, y, 1 - c), device_id_type=MESH,
            )
            out.start()
            arrival = pltpu.make_async_remote_copy(
                src_ref=bigs[a].view(outs[a], None, 1 - c), dst_ref=bigs[a].view(outs[a], None, 1 - c),
                send_sem=send.at[a], recv_sem=recv.at[a], device_id=(x, y, 1 - c), device_id_type=MESH,
            )
            cps.append((out, arrival))
        for out, arrival in cps:
            out.wait_send()
            arrival.wait_recv()

    outs = [jax.ShapeDtypeStruct(b.dims("shard"), F32) for b in bigs]
    return _comm_call(body, "grad_pair_broadcast", shards, outs, n, {a: a for a in range(n)})


def _flat2d(a):
    return a.reshape(-1, a.shape[-1])


def _tile_call(body, name, big, where, extra, ins, in_specs, out_spec, out_shape):
    grid = ((extra,) if extra else ()) + big.grid
    return pl.pallas_call(
        body, name=name,
        grid_spec=pltpu.PrefetchScalarGridSpec(num_scalar_prefetch=1, grid=grid, in_specs=in_specs, out_specs=out_spec),
        out_shape=out_shape, compiler_params=_params(*["parallel"] * len(grid)),
    )(where, *ins)


def _cast_into_full(w, big, where, name):
    def body(_, w_ref, o_ref):
        o_ref[...] = w_ref[...].astype(BF16)

    return _tile_call(
        body, name, big, where, 2, [w.reshape(big.dims("shard"))], [big.spec(None, "grid")], big.spec("pref", "grid"),
        jax.ShapeDtypeStruct(big.dims("full"), BF16),
    )


def _pair_sum(g_full, recv_half, big, where, name):
    def body(_, g_ref, r_ref, o_ref):
        o_ref[...] = (g_ref[...].astype(F32) + r_ref[...].astype(F32)).astype(BF16)

    half = big.spec("grid", None)
    return _tile_call(
        body, name, big, where, N_CHIPS, [g_full, recv_half], [big.spec("grid", "pref"), half], half,
        jax.ShapeDtypeStruct(big.dims("half"), BF16),
    )


def _chip_sum(chip_sum, parts, big, where, name):
    def body(_, own_ref, p_ref, o_ref):
        acc = own_ref[...].astype(F32)
        for k in range(N_CHIPS - 1):
            acc = acc + p_ref[k].astype(F32)
        o_ref[...] = acc

    return _tile_call(
        body, name, big, where, 0, [chip_sum, parts], [big.spec("pref", None), big.spec(None, None, lead=(N_CHIPS - 1,))],
        big.spec(None, "pref"), jax.ShapeDtypeStruct(big.dims("shard"), F32),
    )


def _adam_big(w, g, m, v, name):
    shape = w.shape
    w2, g2, m2, v2 = (_flat2d(a) for a in (w, g, m, v))
    rows, cols = w2.shape
    tr = _row_tile(rows, cols)

    def body(w_ref, g_ref, m_ref, v_ref, d_ref, mo_ref, vo_ref):
        d, mo, vo = _adam(w_ref[...], g_ref[...], m_ref[...], v_ref[...])
        d_ref[...] = d
        mo_ref[...] = mo
        vo_ref[...] = vo

    spec = pl.BlockSpec((tr, cols), lambda i: (i, 0))
    outs = pl.pallas_call(
        body, name=name, grid=(rows // tr,), in_specs=[spec] * 4, out_specs=[spec] * 3,
        out_shape=[jax.ShapeDtypeStruct(w2.shape, F32)] * 3, compiler_params=_params("parallel"),
    )(w2, g2, m2, v2)
    return tuple(o.reshape(shape) for o in outs)


def _mod_fwd(c_all, w_mod, b_cols):
    n_layers, d, n = w_mod.shape
    tn = _pick(n, (768, 512, 384, 256, 128))

    def body(c_ref, w_ref, b_ref, o_ref):
        cv = c_ref[...]
        ca = (cv * _sigmoid(cv)).astype(BF16)
        o_ref[0] = _dot(ca, w_ref[0].astype(BF16)) + b_ref[0]

    return pl.pallas_call(
        body, name="mod_fwd", grid=(n_layers, n // tn),
        in_specs=[
            pl.BlockSpec((N_DEV, d), lambda l, j: (0, 0)),
            pl.BlockSpec((1, d, tn), lambda l, j: (l, 0, j)),
            pl.BlockSpec((1, 1, tn), lambda l, j: (l, 0, j)),
        ],
        out_specs=pl.BlockSpec((1, N_DEV, tn), lambda l, j: (l, 0, j)),
        out_shape=jax.ShapeDtypeStruct((n_layers, N_DEV, n), F32), compiler_params=_params("parallel", "parallel"),
    )(c_all, w_mod, b_cols)


def _mod_bwd_adam(c_all_t, dmod_cols, w, m, v):
    n_layers, d, n = w.shape
    tn = _pick(n, (384, 256, 128))

    def body(c_ref, dm_ref, w_ref, m_ref, v_ref, g_ref, d_ref, mo_ref, vo_ref):
        cv = c_ref[...]
        ca = (cv * _sigmoid(cv)).astype(BF16)
        g = _dot(ca, dm_ref[0].astype(BF16))
        g_ref[0] = g
        dl, mo, vo = _adam(w_ref[0], g, m_ref[0], v_ref[0])
        d_ref[0] = dl
        mo_ref[0] = mo
        vo_ref[0] = vo

    wspec = pl.BlockSpec((1, d, tn), lambda l, j: (l, 0, j))
    return pl.pallas_call(
        body, name="mod_bwd_adam", grid=(n_layers, n // tn),
        in_specs=[
            pl.BlockSpec((d, N_DEV), lambda l, j: (0, 0)),
            pl.BlockSpec((1, N_DEV, tn), lambda l, j: (l, 0, j)),
            wspec, wspec, wspec,
        ],
        out_specs=[wspec] * 4,
        out_shape=[jax.ShapeDtypeStruct(w.shape, F32)] * 4, compiler_params=_params("parallel", "parallel"),
    )(c_all_t, dmod_cols, w, m, v)


def _ffn_fwd(x, vec, w_in, w_out, l, k):
    s, d = x.shape
    f = w_out.shape[2]
    tm = _pick(s, (1024, 512, 256, 128))
    tf = _pick(f, (256, 128))
    nf = f // tf

    def body(x_ref, vec_ref, wg_ref, wu_ref, wo_ref, xo_ref, g_ref, u_ref, y_ref, h_sc, acc_sc):
        j = pl.program_id(1)

        @pl.when(j == 0)
        def _():
            h_sc[...] = _modulate(x_ref[...], vec_ref).astype(BF16)
            acc_sc[...] = jnp.zeros_like(acc_sc)

        h = h_sc[...]
        g = _dot(h, wg_ref[0, 0])
        u = _dot(h, wu_ref[0, 0])
        g_ref[...] = g.astype(BF16)
        u_ref[...] = u.astype(BF16)
        a = (g * _sigmoid(g) * u).astype(BF16)
        acc_sc[...] += _dot(a, wo_ref[0, 0])

        @pl.when(j == nf - 1)
        def _():
            yv = acc_sc[...]
            xo_ref[...] = x_ref[...] + 0.5 * vec_ref[3:4, :] * yv
            y_ref[...] = yv.astype(BF16)

    row = pl.BlockSpec((tm, d), lambda i, j: (i, 0))
    hid = pl.BlockSpec((tm, tf), lambda i, j: (i, j))
    return pl.pallas_call(
        body, name=f"ffn_fwd_{l}{k}", grid=(s // tm, nf),
        in_specs=[
            row,
            pl.BlockSpec((8, d), lambda i, j: (0, 0)),
            pl.BlockSpec((1, 1, d, tf), lambda i, j: (l, k, 0, j)),
            pl.BlockSpec((1, 1, d, tf), lambda i, j: (l, k, 0, nf + j)),
            pl.BlockSpec((1, 1, tf, d), lambda i, j: (l, k, j, 0)),
        ],
        out_specs=[row, hid, hid, row],
        out_shape=[
            jax.ShapeDtypeStruct((s, d), F32),
            jax.ShapeDtypeStruct((s, f), BF16),
            jax.ShapeDtypeStruct((s, f), BF16),
            jax.ShapeDtypeStruct((s, d), BF16),
        ],
        scratch_shapes=[pltpu.VMEM((tm, d), BF16), pltpu.VMEM((tm, d), F32)],
        compiler_params=_params("parallel", "arbitrary"),
    )(x, vec, w_in, w_in, w_out)


def _ffn_bwd(dxo, x, vec, gg, uu, y, w_in, w_out, l, k):
    s, d = x.shape
    f = w_out.shape[2]
    tm = _pick(s, (512, 256, 128))
    tf = _pick(f, (256, 128))
    nf = f // tf

    def body(dxo_ref, x_ref, vec_ref, g_ref, u_ref, y_ref, wg_ref, wu_ref, wo_ref,
             dx_ref, dg_ref, du_ref, a_ref, h_ref, dy_ref, dvec_ref, acc_sc):
        i, j = pl.program_id(0), pl.program_id(1)

        @pl.when((i == 0) & (j == 0))
        def _():
            dvec_ref[...] = jnp.zeros_like(dvec_ref)

        @pl.when(j == 0)
        def _():
            dxo_v = dxo_ref[...]
            dy_ref[...] = (0.5 * vec_ref[3:4, :] * dxo_v).astype(BF16)
            dvec_ref[3:4, :] += 0.5 * jnp.sum(dxo_v * y_ref[...].astype(F32), axis=0, keepdims=True)
            acc_sc[...] = jnp.zeros_like(acc_sc)

        da = _dot_nt(dy_ref[...], wo_ref[0, 0])
        g = g_ref[...].astype(F32)
        u = u_ref[...].astype(F32)
        sig = _sigmoid(g)
        sl = g * sig
        a_ref[...] = (sl * u).astype(BF16)
        dg = (da * u * (sig * (1.0 + g * (1.0 - sig)))).astype(BF16)
        du = (da * sl).astype(BF16)
        dg_ref[...] = dg
        du_ref[...] = du
        acc_sc[...] += _dot_nt(dg, wg_ref[0, 0]) + _dot_nt(du, wu_ref[0, 0])

        @pl.when(j == nf - 1)
        def _():
            dx, h = _modulate_bwd(x_ref[...], acc_sc[...], vec_ref, dvec_ref)
            dx_ref[...] = dxo_ref[...] + dx
            h_ref[...] = h.astype(BF16)

    row = pl.BlockSpec((tm, d), lambda i, j: (i, 0))
    hid = pl.BlockSpec((tm, tf), lambda i, j: (i, j))
    vecs = pl.BlockSpec((8, d), lambda i, j: (0, 0))
    return pl.pallas_call(
        body, name=f"ffn_bwd_{l}{k}", grid=(s // tm, nf),
        in_specs=[
            row, row, vecs, hid, hid, row,
            pl.BlockSpec((1, 1, d, tf), lambda i, j: (l, k, 0, j)),
            pl.BlockSpec((1, 1, d, tf), lambda i, j: (l, k, 0, nf + j)),
            pl.BlockSpec((1, 1, tf, d), lambda i, j: (l, k, j, 0)),
        ],
        out_specs=[row, hid, hid, hid, row, row, vecs],
        out_shape=[
            jax.ShapeDtypeStruct((s, d), F32),
            jax.ShapeDtypeStruct((s, f), BF16),
            jax.ShapeDtypeStruct((s, f), BF16),
            jax.ShapeDtypeStruct((s, f), BF16),
            jax.ShapeDtypeStruct((s, d), BF16),
            jax.ShapeDtypeStruct((s, d), BF16),
            jax.ShapeDtypeStruct((8, d), F32),
        ],
        scratch_shapes=[pltpu.VMEM((tm, d), F32)],
        compiler_params=_params("arbitrary", "arbitrary"),
    )(dxo, x, vec, gg, uu, y, w_in, w_in, w_out)


def _grad_matmul(a, b, full_shape, lead, col0, prev, name):
    s, k1 = a.shape
    n = b.shape[1]
    tk = _pick(k1, (512, 256, 128))
    tn = _pick(n, (1408, 1024, 640, 512, 256, 128))
    nl = len(lead)
    assert col0 % tn == 0

    def body(*refs):
        a_ref, b_ref, o_ref = refs[0], refs[1], refs[-1]
        o_ref[(0,) * nl] = _dot_tn(a_ref[...], b_ref[...]).astype(BF16)

    in_specs = [pl.BlockSpec((s, tk), lambda i, j: (0, i)), pl.BlockSpec((s, tn), lambda i, j: (0, j))]
    ins = [a, b]
    aliases = {}
    if prev is not None:
        in_specs.append(_ANY)
        ins.append(prev)
        aliases = {2: 0}
    return pl.pallas_call(
        body, name=name, grid=(k1 // tk, n // tn), in_specs=in_specs,
        out_specs=pl.BlockSpec((1,) * nl + (tk, tn), lambda i, j: tuple(lead) + (i, col0 // tn + j)),
        out_shape=jax.ShapeDtypeStruct(full_shape, BF16), input_output_aliases=aliases,
        compiler_params=_params("parallel", "parallel"),
    )(*ins)


def _proj_mod_fwd(x, vec, w):
    s, d = x.shape
    n = w.shape[2]
    tm = _pick(s, (512, 256, 128))
    tn = _pick(n, (640, 512, 256, 128))

    def body(x_ref, vec_ref, w_ref, o_ref, h_sc):
        @pl.when(pl.program_id(1) == 0)
        def _():
            h_sc[...] = _modulate(x_ref[...], vec_ref).astype(BF16)

        o_ref[...] = _dot(h_sc[...], w_ref[0])

    return pl.pallas_call(
        body, name="ab_in_fwd", grid=(s // tm, n // tn),
        in_specs=[
            pl.BlockSpec((tm, d), lambda i, j: (i, 0)),
            pl.BlockSpec((8, d), lambda i, j: (0, 0)),
            pl.BlockSpec((1, d, tn), lambda i, j: (0, 0, j)),
        ],
        out_specs=pl.BlockSpec((tm, tn), lambda i, j: (i, j)),
        out_shape=jax.ShapeDtypeStruct((s, n), F32),
        scratch_shapes=[pltpu.VMEM((tm, d), BF16)],
        compiler_params=_params("parallel", "arbitrary"),
    )(x, vec, w)


def _proj_res_fwd(a, w, x, vec):
    s, kd = a.shape
    d = x.shape[1]
    tm = _pick(s, (512, 256, 128))

    def body(a_ref, w_ref, x_ref, vec_ref, xo_ref, y_ref):
        yv = _dot(a_ref[...], w_ref[0])
        xo_ref[...] = x_ref[...] + vec_ref[3:4, :] * yv
        y_ref[...] = yv.astype(BF16)

    row = pl.BlockSpec((tm, d), lambda i: (i, 0))
    return pl.pallas_call(
        body, name="ab_out_fwd", grid=(s // tm,),
        in_specs=[
            pl.BlockSpec((tm, kd), lambda i: (i, 0)),
            pl.BlockSpec((1, kd, d), lambda i: (0, 0, 0)),
            row,
            pl.BlockSpec((8, d), lambda i: (0, 0)),
        ],
        out_specs=[row, row],
        out_shape=[jax.ShapeDtypeStruct((s, d), F32), jax.ShapeDtypeStruct((s, d), BF16)],
        compiler_params=_params("parallel"),
    )(a, w, x, vec)


def _proj_res_bwd(dxo, y, vec, w):
    s, d = dxo.shape
    kd = w.shape[1]
    tm = _pick(s, (512, 256, 128))

    def body(dxo_ref, y_ref, vec_ref, w_ref, dy_ref, da_ref, dgate_ref):
        @pl.when(pl.program_id(0) == 0)
        def _():
            dgate_ref[...] = jnp.zeros_like(dgate_ref)

        dxo_v = dxo_ref[...]
        dy = (vec_ref[3:4, :] * dxo_v).astype(BF16)
        dy_ref[...] = dy
        dgate_ref[3:4, :] += jnp.sum(dxo_v * y_ref[...].astype(F32), axis=0, keepdims=True)
        da_ref[...] = _dot_nt(dy, w_ref[0]).astype(BF16)

    row = pl.BlockSpec((tm, d), lambda i: (i, 0))
    vecs = pl.BlockSpec((8, d), lambda i: (0, 0))
    return pl.pallas_call(
        body, name="ab_out_bwd", grid=(s // tm,),
        in_specs=[row, row, vecs, pl.BlockSpec((1, kd, d), lambda i: (0, 0, 0))],
        out_specs=[row, pl.BlockSpec((tm, kd), lambda i: (i, 0)), vecs],
        out_shape=[
            jax.ShapeDtypeStruct((s, d), BF16),
            jax.ShapeDtypeStruct((s, kd), BF16),
            jax.ShapeDtypeStruct((8, d), F32),
        ],
        compiler_params=_params("arbitrary"),
    )(dxo, y, vec, w)


def _proj_mod_bwd(dproj, w, x, vec, dxo, dvec_in):
    s, n = dproj.shape
    d = x.shape[1]
    tm = _pick(s, (512, 256, 128))

    def body(dp_ref, w_ref, x_ref, vec_ref, dxo_ref, dvi_ref, dx_ref, h_ref, dvec_ref):
        @pl.when(pl.program_id(0) == 0)
        def _():
            dvec_ref[...] = dvi_ref[...]

        dh = _dot_nt(dp_ref[...], w_ref[0])
        dx, h = _modulate_bwd(x_ref[...], dh, vec_ref, dvec_ref)
        dx_ref[...] = dxo_ref[...] + dx
        h_ref[...] = h.astype(BF16)

    row = pl.BlockSpec((tm, d), lambda i: (i, 0))
    vecs = pl.BlockSpec((8, d), lambda i: (0, 0))
    return pl.pallas_call(
        body, name="ab_in_bwd", grid=(s // tm,),
        in_specs=[pl.BlockSpec((tm, n), lambda i: (i, 0)), pl.BlockSpec((1, d, n), lambda i: (0, 0, 0)), row, vecs, row, vecs],
        out_specs=[row, row, vecs],
        out_shape=[jax.ShapeDtypeStruct((s, d), F32), jax.ShapeDtypeStruct((s, d), BF16), jax.ShapeDtypeStruct((8, d), F32)],
        compiler_params=_params("arbitrary"),
    )(dproj, w, x, vec, dxo, dvec_in)


def _tril(n):
    return lax.broadcasted_iota(jnp.int32, (n, n), 0) >= lax.broadcasted_iota(jnp.int32, (n, n), 1)


def _layernorm_stats(gv):
    mu = jnp.mean(gv, axis=-1, keepdims=True)
    cen = gv - mu
    rstd = lax.rsqrt(jnp.mean(cen * cen, axis=-1, keepdims=True) + EPS)
    return cen * rstd, rstd


def _shift_down(q, k, above_ref, c_cg, c_xb, first):
    width = q.shape[1]
    rows = lax.broadcasted_iota(jnp.int32, q.shape, 0)
    out = pltpu.roll(q, k, 0)
    for r in range(k):
        src = CONV_HALO - k + r
        above = above_ref[src : src + 1, c_cg : c_cg + width] * above_ref[src : src + 1, c_xb : c_xb + width]
        above = jnp.where(first, 0.0, above)
        out = jnp.where(rows == r, above, out)
    return out


def _ab_mix_fwd(proj, norm_v, w_s, b_rows, conv_w):
    s, n = proj.shape
    heads, chunk, _ = w_s.shape
    da = norm_v.shape[1]
    hd = da // heads
    db = conv_w.shape[1]
    tm = _pick(s, (512, 256, 128))

    def body(p_ref, ph_ref, nv_ref, ws_ref, b_ref, cw_ref, o_ref):
        first = pl.program_id(0) == 0
        gu, _ = _gelu(p_ref[:, 0:da])
        gv, _ = _gelu(p_ref[:, da : 2 * da])
        xhat, _ = _layernorm_stats(gv)
        vn = (xhat * nv_ref[...]).astype(BF16)
        mask = _tril(chunk)
        for hh in range(heads):
            wm = jnp.where(mask, ws_ref[hh], 0.0).astype(BF16)
            cols = slice(hh * hd, (hh + 1) * hd)
            for nn in range(tm // chunk):
                rows = slice(nn * chunk, (nn + 1) * chunk)
                z = _dot(wm, vn[rows, cols]) + b_ref[:, cols]
                o_ref[rows, cols] = (gu[rows, cols] * z).astype(BF16)
        c_cg, c_xb = 2 * da + db, 2 * da + 2 * db
        bg = p_ref[:, 2 * da : 2 * da + db]
        q = p_ref[:, c_cg : c_cg + db] * p_ref[:, c_xb : c_xb + db]
        q1 = _shift_down(q, 1, ph_ref, c_cg, c_xb, first)
        q2 = _shift_down(q, 2, ph_ref, c_cg, c_xb, first)
        conv = cw_ref[0:1, :] * q2 + cw_ref[1:2, :] * q1 + cw_ref[2:3, :] * q
        o_ref[:, da : da + db] = (bg * conv).astype(BF16)

    nh = tm // CONV_HALO
    return pl.pallas_call(
        body, name="ab_mix_fwd", grid=(s // tm,),
        in_specs=[
            pl.BlockSpec((tm, n), lambda i: (i, 0)),
            pl.BlockSpec((CONV_HALO, n), lambda i: (jnp.maximum(i * nh - 1, 0), 0)),
            pl.BlockSpec((1, da), lambda i: (0, 0)),
            pl.BlockSpec((heads, chunk, chunk), lambda i: (0, 0, 0)),
            pl.BlockSpec((chunk, da), lambda i: (0, 0)),
            pl.BlockSpec((3, db), lambda i: (0, 0)),
        ],
        out_specs=pl.BlockSpec((tm, da + db), lambda i: (i, 0)),
        out_shape=jax.ShapeDtypeStruct((s, da + db), BF16),
        compiler_params=_params("parallel"),
    )(proj, proj, norm_v, w_s, b_rows, conv_w)


def _ab_mix_bwd(proj, dcat, norm_v, w_s, b_rows, conv_w):
    s, n = proj.shape
    heads, chunk, _ = w_s.shape
    da = norm_v.shape[1]
    hd = da // heads
    db = conv_w.shape[1]
    tm = _pick(s, (512, 256, 128))
    nblk = s // tm
    dhalo = 2 * CONV_HALO

    def body(p_ref, pa_ref, pb_ref, dc_ref, dcb_ref, nv_ref, ws_ref, b_ref, cw_ref,
             dp_ref, dnv_ref, dws_ref, dzs_ref, dcw_ref, dvn_sc):
        i = pl.program_id(0)
        first, last = i == 0, i == nblk - 1

        @pl.when(first)
        def _():
            dnv_ref[...] = jnp.zeros_like(dnv_ref)
            dws_ref[...] = jnp.zeros_like(dws_ref)
            dzs_ref[...] = jnp.zeros_like(dzs_ref)
            dcw_ref[...] = jnp.zeros_like(dcw_ref)

        uu = p_ref[:, 0:da]
        gu, gu_grad = _gelu(uu)
        gv, gv_grad = _gelu(p_ref[:, da : 2 * da])
        xhat, rstd = _layernorm_stats(gv)
        nv = nv_ref[...]
        vn = (xhat * nv).astype(BF16)
        dya = dc_ref[:, 0:da].astype(F32)
        dz = (dya * gu).astype(BF16)
        mask = _tril(chunk)
        for hh in range(heads):
            wm = jnp.where(mask, ws_ref[hh], 0.0).astype(BF16)
            cols = slice(hh * hd, (hh + 1) * hd)
            dws = jnp.zeros((chunk, chunk), F32)
            for nn in range(tm // chunk):
                rows = slice(nn * chunk, (nn + 1) * chunk)
                z = _dot(wm, vn[rows, cols]) + b_ref[:, cols]
                dp_ref[rows, cols] = (dya[rows, cols] * z * gu_grad[rows, cols]).astype(BF16)
                dz_blk = dz[rows, cols]
                dws = dws + _dot_nt(dz_blk, vn[rows, cols])
                dzs_ref[:, cols] += dz_blk.astype(F32)
                dvn = _dot_tn(wm, dz_blk)
                dnv_ref[:, cols] += jnp.sum(dvn * xhat[rows, cols], axis=0, keepdims=True)
                dvn_sc[rows, cols] = dvn
            dws_ref[hh] += jnp.where(mask, dws, 0.0)
        dxhat = dvn_sc[...] * nv
        dgv = rstd * (dxhat - jnp.mean(dxhat, axis=-1, keepdims=True) - xhat * jnp.mean(dxhat * xhat, axis=-1, keepdims=True))
        dp_ref[:, da : 2 * da] = (dgv * gv_grad).astype(BF16)

        c_bg, c_cg, c_xb = 2 * da, 2 * da + db, 2 * da + 2 * db
        bg = p_ref[:, c_bg : c_bg + db]
        cg = p_ref[:, c_cg : c_cg + db]
        xb = p_ref[:, c_xb : c_xb + db]
        q = cg * xb
        q1 = _shift_down(q, 1, pa_ref, c_cg, c_xb, first)
        q2 = _shift_down(q, 2, pa_ref, c_cg, c_xb, first)
        dyb = dc_ref[:, da : da + db].astype(F32)
        conv = cw_ref[0:1, :] * q2 + cw_ref[1:2, :] * q1 + cw_ref[2:3, :] * q
        dp_ref[:, c_bg : c_bg + db] = (dyb * conv).astype(BF16)
        e = dyb * bg
        dcw_ref[0:1, :] += jnp.sum(e * q2, axis=0, keepdims=True)
        dcw_ref[1:2, :] += jnp.sum(e * q1, axis=0, keepdims=True)
        dcw_ref[2:3, :] += jnp.sum(e * q, axis=0, keepdims=True)
        rows = lax.broadcasted_iota(jnp.int32, e.shape, 0)
        dq = cw_ref[2:3, :] * e
        for kk in (1, 2):
            ek = pltpu.roll(e, tm - kk, 0)
            for r in range(kk):
                below = dcb_ref[r : r + 1, da : da + db].astype(F32) * pb_ref[r : r + 1, c_bg : c_bg + db]
                below = jnp.where(last, 0.0, below)
                ek = jnp.where(rows == tm - kk + r, below, ek)
            dq = dq + cw_ref[2 - kk : 3 - kk, :] * ek
        dp_ref[:, c_cg : c_cg + db] = (dq * xb).astype(BF16)
        dp_ref[:, c_xb : c_xb + db] = (dq * cg).astype(BF16)

    nh = tm // CONV_HALO
    nhb = tm // dhalo
    const2 = lambda i: (0, 0)
    return pl.pallas_call(
        body, name="ab_mix_bwd", grid=(nblk,),
        in_specs=[
            pl.BlockSpec((tm, n), lambda i: (i, 0)),
            pl.BlockSpec((CONV_HALO, n), lambda i: (jnp.maximum(i * nh - 1, 0), 0)),
            pl.BlockSpec((CONV_HALO, n), lambda i: (jnp.minimum((i + 1) * nh, s // CONV_HALO - 1), 0)),
            pl.BlockSpec((tm, da + db), lambda i: (i, 0)),
            pl.BlockSpec((dhalo, da + db), lambda i: (jnp.minimum((i + 1) * nhb, s // dhalo - 1), 0)),
            pl.BlockSpec((1, da), const2),
            pl.BlockSpec((heads, chunk, chunk), lambda i: (0, 0, 0)),
            pl.BlockSpec((chunk, da), const2),
            pl.BlockSpec((3, db), const2),
        ],
        out_specs=[
            pl.BlockSpec((tm, n), lambda i: (i, 0)),
            pl.BlockSpec((1, da), const2),
            pl.BlockSpec((heads, chunk, chunk), lambda i: (0, 0, 0)),
            pl.BlockSpec((chunk, da), const2),
            pl.BlockSpec((3, db), const2),
        ],
        out_shape=[
            jax.ShapeDtypeStruct((s, n), BF16),
            jax.ShapeDtypeStruct((1, da), F32),
            jax.ShapeDtypeStruct((heads, chunk, chunk), F32),
            jax.ShapeDtypeStruct((chunk, da), F32),
            jax.ShapeDtypeStruct((3, db), F32),
        ],
        scratch_shapes=[pltpu.VMEM((tm, da), F32)],
        compiler_params=_params("arbitrary"),
    )(proj, proj, proj, dcat, dcat, norm_v, w_s, b_rows, conv_w)


def _pool_counts(tm, i, w):
    t = i * tm + lax.broadcasted_iota(jnp.int32, (tm, 1), 0)
    return jnp.minimum(t + 1, w).astype(F32)


def _pool_fwd(x, vec, w_grp, scale):
    s, d = x.shape
    groups, gd, _ = w_grp.shape
    tm = _pick(s, (512, 256, 128))

    def body(x_ref, xa_ref, vec_ref, w_ref, sc_ref, xo_ref, p_ref, o_ref):
        i = pl.program_id(0)
        h = _modulate(x_ref[...], vec_ref)
        ha = jnp.where(i == 0, 0.0, _modulate(xa_ref[...], vec_ref))
        ext = jnp.concatenate([ha, h], axis=0)
        for gi, w in enumerate(POOL_WINDOWS):
            cols = slice(gi * gd, (gi + 1) * gd)
            acc = ext[:, cols]
            step = 1
            while step < w:
                acc = acc + pltpu.roll(acc, step, 0)
                step *= 2
            p = (acc[POOL_HALO:, :] / _pool_counts(tm, i, w) - h[:, cols]).astype(BF16)
            p_ref[:, cols] = p
            o_ref[:, cols] = _dot(p, w_ref[gi]).astype(BF16)
        xo_ref[...] = x_ref[...] + vec_ref[3:4, :] * (o_ref[...].astype(F32) * sc_ref[...])

    nh = tm // POOL_HALO
    row = pl.BlockSpec((tm, d), lambda i: (i, 0))
    return pl.pallas_call(
        body, name="pool_fwd", grid=(s // tm,),
        in_specs=[
            row,
            pl.BlockSpec((POOL_HALO, d), lambda i: (jnp.maximum(i * nh - 1, 0), 0)),
            pl.BlockSpec((8, d), lambda i: (0, 0)),
            pl.BlockSpec((groups, gd, gd), lambda i: (0, 0, 0)),
            pl.BlockSpec((1, d), lambda i: (0, 0)),
        ],
        out_specs=[row, row, row],
        out_shape=[jax.ShapeDtypeStruct((s, d), F32), jax.ShapeDtypeStruct((s, d), BF16), jax.ShapeDtypeStruct((s, d), BF16)],
        compiler_params=_params("parallel"),
    )(x, x, vec, w_grp, scale)


def _pool_bwd(dxo, x, vec, p, o, w_grp, scale):
    s, d = x.shape
    groups, gd, _ = w_grp.shape
    tm = _pick(s, (512, 256, 128))
    nblk = s // tm

    def body(dxo_ref, dxb_ref, x_ref, vec_ref, p_ref, o_ref, w_ref, sc_ref, dx_ref, dw_ref, dsc_ref, dvec_ref, dw_sc):
        i = pl.program_id(0)

        @pl.when(i == 0)
        def _():
            dw_sc[...] = jnp.zeros_like(dw_sc)
            dsc_ref[...] = jnp.zeros_like(dsc_ref)
            dvec_ref[...] = jnp.zeros_like(dvec_ref)

        gate, sc = vec_ref[3:4, :], sc_ref[...]
        dxo_v = dxo_ref[...]
        ov = o_ref[...].astype(F32)
        dvec_ref[3:4, :] += jnp.sum(dxo_v * (ov * sc), axis=0, keepdims=True)
        dy = gate * dxo_v
        dsc_ref[...] += jnp.sum(dy * ov, axis=0, keepdims=True)
        dout = (dy * sc).astype(BF16)
        dout_b = jnp.where(i == nblk - 1, 0.0, gate * dxb_ref[...] * sc).astype(BF16)
        for gi, w in enumerate(POOL_WINDOWS):
            cols = slice(gi * gd, (gi + 1) * gd)
            dw_sc[gi] += _dot_tn(p_ref[:, cols], dout[:, cols])
            wb = w_ref[gi]
            dp = _dot_nt(dout[:, cols], wb)
            dp_b = _dot_nt(dout_b[:, cols], wb)
            e = dp / _pool_counts(tm, i, w)
            t_below = (i + 1) * tm + lax.broadcasted_iota(jnp.int32, (POOL_HALO, 1), 0)
            e_b = dp_b / jnp.minimum(t_below + 1, w).astype(F32)
            acc = jnp.concatenate([e, e_b], axis=0)
            step = 1
            while step < w:
                acc = acc + pltpu.roll(acc, tm + POOL_HALO - step, 0)
                step *= 2
            dx_ref[:, cols] = acc[:tm, :] - dp
        dx, _ = _modulate_bwd(x_ref[...], dx_ref[...], vec_ref, dvec_ref)
        dx_ref[...] = dxo_v + dx

        @pl.when(i == nblk - 1)
        def _():
            dw_ref[...] = dw_sc[...].astype(BF16)

    nh = tm // POOL_HALO
    row = pl.BlockSpec((tm, d), lambda i: (i, 0))
    vecs = pl.BlockSpec((8, d), lambda i: (0, 0))
    wspec = pl.BlockSpec((groups, gd, gd), lambda i: (0, 0, 0))
    return pl.pallas_call(
        body, name="pool_bwd", grid=(nblk,),
        in_specs=[
            row,
            pl.BlockSpec((POOL_HALO, d), lambda i: (jnp.minimum((i + 1) * nh, s // POOL_HALO - 1), 0)),
            row, vecs, row, row, wspec,
            pl.BlockSpec((1, d), lambda i: (0, 0)),
        ],
        out_specs=[row, wspec, pl.BlockSpec((1, d), lambda i: (0, 0)), vecs],
        out_shape=[
            jax.ShapeDtypeStruct((s, d), F32),
            jax.ShapeDtypeStruct((groups, gd, gd), BF16),
            jax.ShapeDtypeStruct((1, d), F32),
            jax.ShapeDtypeStruct((8, d), F32),
        ],
        scratch_shapes=[pltpu.VMEM((groups, gd, gd), F32)],
        compiler_params=_params("arbitrary"),
    )(dxo, dxo, x, vec, p, o, w_grp, scale)


def _loss_head(x, gain, target):
    s, d = x.shape
    tm = _pick(s, (512, 256, 128))

    def body(x_ref, g_ref, t_ref, dx_ref, aux_ref):
        @pl.when(pl.program_id(0) == 0)
        def _():
            aux_ref[...] = jnp.zeros_like(aux_ref)

        xv = x_ref[...]
        rstd = _rstd(xv)
        r = xv * rstd
        gain_v = g_ref[...]
        err = r * gain_v - t_ref[...]
        aux_ref[1:2, :] += jnp.sum(err * err, axis=0, keepdims=True)
        dout = err * (1.0 / d)
        aux_ref[0:1, :] += jnp.sum(dout * r, axis=0, keepdims=True)
        dr = dout * gain_v
        dx_ref[...] = rstd * (dr - r * jnp.mean(dr * r, axis=-1, keepdims=True))

    row = pl.BlockSpec((tm, d), lambda i: (i, 0))
    return pl.pallas_call(
        body, name="loss_head", grid=(s // tm,),
        in_specs=[row, pl.BlockSpec((1, d), lambda i: (0, 0)), row],
        out_specs=[row, pl.BlockSpec((8, d), lambda i: (0, 0))],
        out_shape=[jax.ShapeDtypeStruct((s, d), F32), jax.ShapeDtypeStruct((8, d), F32)],
        compiler_params=_params("arbitrary"),
    )(x, gain, target)


def _small_adam(gathered, gathered_ws, layout, smalls, chip):
    names = list(smalls)
    n = len(names)

    def body(*refs):
        chip_ref, g_ref, gws_ref = refs[0], refs[1], refs[2]
        wmv = refs[3 : 3 + 3 * n]
        outs = refs[3 + 3 * n : 3 + 7 * n]
        total = refs[-1]
        total[...] = g_ref[0]
        for kdev in range(1, N_DEV):
            total[...] += g_ref[kdev]
        total_ws = gws_ref[0]
        for kdev in range(1, N_DEV):
            total_ws = total_ws + gws_ref[kdev]
        my_chip = chip_ref[0]
        for a, name in enumerate(names):
            w_ref, m_ref, v_ref = wmv[3 * a : 3 * a + 3]
            if name == "ab_w_s":
                g = total_ws
            else:
                row0, rows, col0, cols = layout[name]
                if col0 is None:
                    g = jnp.zeros((rows, cols), F32)
                    for j in range(N_CHIPS):
                        g = g + jnp.where(my_chip == j, total[row0 : row0 + rows, j * cols : (j + 1) * cols], 0.0)
                else:
                    g = total[row0 : row0 + rows, col0 : col0 + cols]
            dl, mo, vo = _adam(w_ref[...], g, m_ref[...], v_ref[...])
            outs[4 * a][...] = g
            outs[4 * a + 1][...] = dl
            outs[4 * a + 2][...] = mo
            outs[4 * a + 3][...] = vo

    vm = pl.BlockSpec(memory_space=pltpu.VMEM)
    ins = [gathered, gathered_ws]
    out_shapes = []
    for name in names:
        ins.extend(smalls[name])
        out_shapes.extend([jax.ShapeDtypeStruct(smalls[name][0].shape, F32)] * 4)
    res = pl.pallas_call(
        body, name="small_adam",
        grid_spec=pltpu.PrefetchScalarGridSpec(
            num_scalar_prefetch=1, grid=(1,),
            in_specs=[pl.BlockSpec(a.shape, functools.partial(lambda nd, i, c: (0,) * nd, a.ndim)) for a in ins],
            out_specs=[pl.BlockSpec(o.shape, functools.partial(lambda nd, i, c: (0,) * nd, len(o.shape))) for o in out_shapes],
            scratch_shapes=[pltpu.VMEM(gathered.shape[1:], F32)],
        ),
        out_shape=out_shapes, compiler_params=_params("arbitrary"),
    )(chip.reshape(1).astype(jnp.int32), *ins)
    return {name: res[4 * a : 4 * a + 4] for a, name in enumerate(names)}


def _pad_rows(a, rows=8):
    extra = (-a.shape[0]) % rows
    return jnp.pad(a, ((0, extra), (0, 0))) if extra else a


def _pad_cols(a, cols):
    return jnp.pad(a, ((0, 0), (0, cols - a.shape[1]))) if a.shape[1] < cols else a


def kernel(x, c, norm_g, w_mod, b_mod, w_ffn_in, w_ffn_out, ab_w_in, ab_norm_v, ab_w_s, ab_b_s, ab_conv_w, ab_w_out, pool_w_grp, pool_scale, final_g, loss_target, m_norm_g, m_w_mod, m_b_mod, m_w_ffn_in, m_w_ffn_out, m_ab_w_in, m_ab_norm_v, m_ab_w_s, m_ab_b_s, m_ab_conv_w, m_ab_w_out, m_pool_w_grp, m_pool_scale, m_final_g, v_norm_g, v_w_mod, v_b_mod, v_w_ffn_in, v_w_ffn_out, v_ab_w_in, v_ab_norm_v, v_ab_w_s, v_ab_b_s, v_ab_conv_w, v_ab_w_out, v_pool_w_grp, v_pool_scale, v_final_g):
    ix, iy, ic = _place()
    chip = 2 * ix + iy
    me = 4 * ix + 2 * iy + ic
    s, d = x.shape[1], x.shape[2]
    x0 = x.reshape(s, d)
    target = loss_target.reshape(s, d)
    n_layers = norm_g.shape[0]
    dq = d // N_CHIPS
    heads, chunk = ab_w_s.shape[1], ab_w_s.shape[2]
    da = ab_norm_v.shape[1]
    db = ab_conv_w.shape[2] * N_CHIPS
    assert n_layers == 2 and da % heads == 0

    cw_pad = _pad_cols(ab_conv_w.reshape(3, db // N_CHIPS), dq)
    packed = jnp.concatenate(
        [_pad_rows(c.reshape(N_CHIPS, dq)), _pad_rows(norm_g.reshape(-1, dq)), _pad_rows(pool_scale.reshape(1, dq)), _pad_rows(cw_pad)],
        axis=0,
    )
    (small_all,) = _all_gather_small([packed], "gather_small_inputs")
    by_chip = small_all[0::2]
    c_all = small_all[:, 0:N_CHIPS, :].reshape(N_DEV, d)
    norm_full = by_chip[:, 8 : 8 + 3 * n_layers, :].transpose(1, 0, 2).reshape(3 * n_layers, d)
    pool_scale_full = by_chip[:, 16:17, :].transpose(1, 0, 2).reshape(1, d)
    conv_full = by_chip[:, 24:27, : db // N_CHIPS].transpose(1, 0, 2).reshape(3, db)

    ncol = w_mod.shape[2]
    b_cols = lax.dynamic_slice(b_mod, (0, chip * ncol), (n_layers, ncol)).reshape(n_layers, 1, ncol)
    mod_cols = _mod_fwd(c_all, w_mod, b_cols)
    (mod_all,) = _all_gather_small([mod_cols.reshape(n_layers * N_DEV, ncol)], "gather_mod")
    mod_mine = lax.dynamic_index_in_dim(mod_all[0::2].reshape(N_CHIPS, n_layers, N_DEV, ncol), me, axis=2, keepdims=False)
    mod = mod_mine.transpose(1, 0, 2).reshape(n_layers, 3, 3, d)

    vecs = {
        (l, sub): _pad_rows(jnp.concatenate([norm_full[3 * l + sub][None], mod[l, sub]], axis=0))
        for l in range(n_layers)
        for sub in range(3)
    }

    def vec_of(l, sub):
        return vecs[l, sub]

    full_shape = {
        "w_ffn_in": (n_layers, 2, d, w_ffn_in.shape[3] * N_CHIPS),
        "w_ffn_out": (n_layers, 2, w_ffn_out.shape[2] * N_CHIPS, d),
        "ab_w_in": (1, d, ab_w_in.shape[2] * N_CHIPS),
        "ab_w_out": (1, ab_w_out.shape[1] * N_CHIPS, d),
        "pool_w_grp": (pool_w_grp.shape[1], pool_w_grp.shape[2] * N_CHIPS, pool_w_grp.shape[3]),
    }
    split_axes = {"w_ffn_in": (3, 0), "w_ffn_out": (2, 0), "ab_w_in": (2, 1), "ab_w_out": (1, 2), "pool_w_grp": (1, 0)}
    bigs = {nm: _Big(full_shape[nm], *split_axes[nm]) for nm in full_shape}
    big_names = list(bigs)
    blist = [bigs[nm] for nm in big_names]
    where = jnp.stack([chip, ic]).astype(jnp.int32)
    big_w = {"w_ffn_in": w_ffn_in, "w_ffn_out": w_ffn_out, "ab_w_in": ab_w_in, "ab_w_out": ab_w_out, "pool_w_grp": pool_w_grp[0]}
    big_m = {"w_ffn_in": m_w_ffn_in, "w_ffn_out": m_w_ffn_out, "ab_w_in": m_ab_w_in, "ab_w_out": m_ab_w_out, "pool_w_grp": m_pool_w_grp[0]}
    big_v = {"w_ffn_in": v_w_ffn_in, "w_ffn_out": v_w_ffn_out, "ab_w_in": v_ab_w_in, "ab_w_out": v_ab_w_out, "pool_w_grp": v_pool_w_grp[0]}
    placed = [_cast_into_full(big_w[nm], bigs[nm], where, "cast_" + nm) for nm in big_names]
    full16 = {nm: w.reshape(full_shape[nm]) for nm, w in zip(big_names, _gather_weights(placed, blist))}
    wf_in, wf_out = full16["w_ffn_in"], full16["w_ffn_out"]
    f_hidden = wf_out.shape[2]

    b_rows = jnp.broadcast_to(ab_b_s[0].T[:, :, None], (chunk, heads, da // heads)).reshape(chunk, da)
    saved = {}
    xs = x0
    for l in range(n_layers):
        saved[l, 0, "x"] = xs
        xs, gg, uu, yb = _ffn_fwd(xs, vec_of(l, 0), wf_in, wf_out, l, 0)
        saved[l, 0, "act"] = (gg, uu, yb)
        saved[l, 1, "x"] = xs
        if l % 2 == 0:
            proj = _proj_mod_fwd(xs, vec_of(l, 1), full16["ab_w_in"])
            cat = _ab_mix_fwd(proj, ab_norm_v, ab_w_s[0], b_rows, conv_full)
            xs, yb = _proj_res_fwd(cat, full16["ab_w_out"], xs, vec_of(l, 1))
            saved[l, 1, "act"] = (proj, cat, yb)
        else:
            xs, pp, oo = _pool_fwd(xs, vec_of(l, 1), full16["pool_w_grp"], pool_scale_full)
            saved[l, 1, "act"] = (pp, oo)
        saved[l, 2, "x"] = xs
        xs, gg, uu, yb = _ffn_fwd(xs, vec_of(l, 2), wf_in, wf_out, l, 1)
        saved[l, 2, "act"] = (gg, uu, yb)
    dxs, aux = _loss_head(xs, final_g.reshape(1, d), target)
    loss = lax.psum(0.5 * jnp.sum(aux[1]) / d, ("x", "y", "c"))

    grads = {nm: None for nm in big_names}
    dvecs = {}

    def ffn_back(dxs, l, sub, k):
        gg, uu, yb = saved[l, sub, "act"]
        dxs, dg, du, a, h, dy, dvec = _ffn_bwd(dxs, saved[l, sub, "x"], vec_of(l, sub), gg, uu, yb, wf_in, wf_out, l, k)
        gin = full_shape["w_ffn_in"]
        grads["w_ffn_in"] = _grad_matmul(h, dg, gin, (l, k), 0, grads["w_ffn_in"], f"dw_in_g_{l}{k}")
        grads["w_ffn_in"] = _grad_matmul(h, du, gin, (l, k), f_hidden, grads["w_ffn_in"], f"dw_in_u_{l}{k}")
        grads["w_ffn_out"] = _grad_matmul(a, dy, full_shape["w_ffn_out"], (l, k), 0, grads["w_ffn_out"], f"dw_out_{l}{k}")
        dvecs[l, sub] = dvec
        return dxs

    small_g = {}
    for l in reversed(range(n_layers)):
        dxs = ffn_back(dxs, l, 2, 1)
        if l % 2 == 0:
            proj, cat, yb = saved[l, 1, "act"]
            dy, dcat, dgate = _proj_res_bwd(dxs, yb, vec_of(l, 1), full16["ab_w_out"])
            grads["ab_w_out"] = _grad_matmul(cat, dy, full_shape["ab_w_out"], (0,), 0, None, "dw_ab_out")
            dproj, small_g["ab_norm_v"], small_g["ab_w_s"], dzs, small_g["ab_conv_w"] = _ab_mix_bwd(
                proj, dcat, ab_norm_v, ab_w_s[0], b_rows, conv_full
            )
            small_g["ab_b_s"] = dzs.reshape(chunk, heads, da // heads).sum(axis=2).T
            dxs, h, dvecs[l, 1] = _proj_mod_bwd(dproj, full16["ab_w_in"], saved[l, 1, "x"], vec_of(l, 1), dxs, dgate)
            grads["ab_w_in"] = _grad_matmul(h, dproj, full_shape["ab_w_in"], (0,), 0, None, "dw_ab_in")
        else:
            pp, oo = saved[l, 1, "act"]
            dxs, grads["pool_w_grp"], small_g["pool_scale"], dvecs[l, 1] = _pool_bwd(
                dxs, saved[l, 1, "x"], vec_of(l, 1), pp, oo, full16["pool_w_grp"], pool_scale_full
            )
        dxs = ffn_back(dxs, l, 0, 0)
    grad_x = dxs.reshape(x.shape)

    g_full = [grads[nm].reshape(bigs[nm].dims("full")) for nm in big_names]
    recv_half = _pair_exchange(g_full, blist)
    chip_sums = [_pair_sum(g, r, b, where, "pair_sum_" + nm) for nm, g, r, b in zip(big_names, g_full, recv_half, blist)]
    parts = _chip_exchange(chip_sums, blist)
    totals = [_chip_sum(s, p, b, where, "chip_sum_" + nm) for nm, s, p, b in zip(big_names, chip_sums, parts, blist)]
    g_shards = dict(zip(big_names, _pair_broadcast(totals, blist)))

    out = {}
    for nm in big_names:
        g = g_shards[nm].reshape(big_w[nm].shape)
        out[nm] = (g,) + _adam_big(big_w[nm], g, big_m[nm], big_v[nm], "adam_" + nm)
    out["pool_w_grp"] = tuple(a[None] for a in out["pool_w_grp"])

    dgain = jnp.stack([dvecs[l, sub][0] for l in range(n_layers) for sub in range(3)])
    dmod = jnp.concatenate([dvecs[l, sub][1:4] for l in range(n_layers) for sub in range(3)], axis=0)
    pieces = [
        dgain, dmod, aux[0:1], _pad_cols(small_g["ab_norm_v"], d), small_g["pool_scale"],
        _pad_cols(small_g["ab_conv_w"], d), _pad_cols(small_g["ab_b_s"], d),
    ]
    row0, layout_rows = 0, []
    for pc in pieces:
        layout_rows.append(row0)
        row0 += -(-pc.shape[0] // 8) * 8
    packed_g = jnp.concatenate([_pad_rows(pc) for pc in pieces], axis=0)
    g_all, gws_all = _all_gather_small([packed_g, small_g["ab_w_s"].reshape(heads * chunk, chunk)], "gather_small_grads")
    layout = {
        "norm_g": (layout_rows[0], 3 * n_layers, None, dq),
        "b_mod": (layout_rows[1], 9 * n_layers, 0, d),
        "final_g": (layout_rows[2], 1, 0, d),
        "ab_norm_v": (layout_rows[3], 1, 0, da),
        "pool_scale": (layout_rows[4], 1, None, dq),
        "ab_conv_w": (layout_rows[5], 3, None, db // N_CHIPS),
        "ab_b_s": (layout_rows[6], heads, 0, chunk),
    }
    shapes2d = {
        "norm_g": (3 * n_layers, dq), "b_mod": (9 * n_layers, d), "final_g": (1, d), "ab_norm_v": (1, da),
        "pool_scale": (1, dq), "ab_conv_w": (3, db // N_CHIPS), "ab_b_s": (heads, chunk), "ab_w_s": (heads * chunk, chunk),
    }
    small_w = {"norm_g": (norm_g, m_norm_g, v_norm_g), "b_mod": (b_mod, m_b_mod, v_b_mod), "final_g": (final_g, m_final_g, v_final_g),
               "ab_norm_v": (ab_norm_v, m_ab_norm_v, v_ab_norm_v), "pool_scale": (pool_scale, m_pool_scale, v_pool_scale),
               "ab_conv_w": (ab_conv_w, m_ab_conv_w, v_ab_conv_w), "ab_b_s": (ab_b_s, m_ab_b_s, v_ab_b_s), "ab_w_s": (ab_w_s, m_ab_w_s, v_ab_w_s)}
    smalls = {nm: tuple(a.reshape(shapes2d[nm]) for a in wmv) for nm, wmv in small_w.items()}
    small_out = _small_adam(g_all, gws_all, layout, smalls, chip)
    for nm, res in small_out.items():
        out[nm] = tuple(a.reshape(small_w[nm][0].shape) for a in res)

    dmod_all = g_all[:, layout_rows[1] : layout_rows[1] + 9 * n_layers, :].reshape(N_DEV, n_layers, 9 * d)
    dmod_cols = lax.dynamic_slice(dmod_all, (0, 0, chip * ncol), (N_DEV, n_layers, ncol)).transpose(1, 0, 2)
    out["w_mod"] = tuple(_mod_bwd_adam(c_all.T, dmod_cols, w_mod, m_w_mod, v_w_mod))

    order = ["norm_g", "w_mod", "b_mod", "w_ffn_in", "w_ffn_out", "ab_w_in", "ab_norm_v", "ab_w_s", "ab_b_s", "ab_conv_w", "ab_w_out", "pool_w_grp", "pool_scale", "final_g"]
    return (loss, grad_x, *[out[nm][0] for nm in order], *[out[nm][1] for nm in order], *[out[nm][2] for nm in order], *[out[nm][3] for nm in order])
```

```python
import functools
import math

import jax
import jax.numpy as jnp
from jax import lax
from jax.experimental import pallas as pl
from jax.experimental.pallas import tpu as pltpu

F32 = jnp.float32
BF16 = jnp.bfloat16
MESH = pl.DeviceIdType.MESH

EPS = 1e-6
ADAM_LR = 0.001
ADAM_B1 = 0.9
ADAM_B2 = 0.999
ADAM_EPS = 1e-08
ADAM_WD = 0.01
ADAM_STEP = 10
POOL_WINDOWS = (2, 4, 8, 16)
POOL_HALO = 16
CONV_HALO = 8
N_CHIPS = 4
N_DEV = 8
VMEM_LIMIT_BYTES = 48 * 1024 * 1024
EW_BLOCK_ELEMS = 256 * 1024


def _pick(n, prefs):
    for p in prefs:
        if p <= n and n % p == 0:
            return p
    return n


def _row_tile(rows, cols):
    best = None
    for d in range(16, rows + 1, 16):
        if rows % d == 0 and d * cols <= EW_BLOCK_ELEMS:
            best = d
    return best or rows


def _dot(a, b):
    return jnp.dot(a, b, preferred_element_type=F32)


def _dot_nt(a, b):
    return lax.dot_general(a, b, (((1,), (1,)), ((), ())), preferred_element_type=F32)


def _dot_tn(a, b):
    return lax.dot_general(a, b, (((0,), (0,)), ((), ())), preferred_element_type=F32)


def _sigmoid(x):
    return 1.0 / (1.0 + jnp.exp(-x))


_GELU_C = math.sqrt(2.0 / math.pi)


def _gelu(x):
    x2 = x * x
    t = jnp.tanh(_GELU_C * (x + 0.044715 * x2 * x))
    val = 0.5 * x * (1.0 + t)
    grad = 0.5 * (1.0 + t) + 0.5 * x * (1.0 - t * t) * (_GELU_C * (1.0 + 3.0 * 0.044715 * x2))
    return val, grad


def _rstd(x):
    return lax.rsqrt(jnp.mean(x * x, axis=-1, keepdims=True) + EPS)


def _modulate(x, vec_ref):
    return (x * _rstd(x)) * vec_ref[0:1, :] * (1.0 + vec_ref[2:3, :]) + vec_ref[1:2, :]


def _modulate_bwd(x, dh, vec_ref, dvec_ref):
    gn, sh, sc = vec_ref[0:1, :], vec_ref[1:2, :], vec_ref[2:3, :]
    rstd = _rstd(x)
    r = x * rstd
    dvec_ref[0:1, :] += jnp.sum(dh * r * (1.0 + sc), axis=0, keepdims=True)
    dvec_ref[1:2, :] += jnp.sum(dh, axis=0, keepdims=True)
    dvec_ref[2:3, :] += jnp.sum(dh * r * gn, axis=0, keepdims=True)
    gm = gn * (1.0 + sc)
    dr = dh * gm
    dx = rstd * (dr - r * jnp.mean(dr * r, axis=-1, keepdims=True))
    return dx, r * gm + sh


def _adam(w, g, m, v):
    m = ADAM_B1 * m + (1.0 - ADAM_B1) * g
    v = ADAM_B2 * v + (1.0 - ADAM_B2) * (g * g)
    m_hat = m / (1.0 - ADAM_B1**ADAM_STEP)
    v_hat = v / (1.0 - ADAM_B2**ADAM_STEP)
    delta = -ADAM_LR * (m_hat / (jnp.sqrt(v_hat) + ADAM_EPS) + ADAM_WD * w)
    return delta, m, v


_ANY = pl.BlockSpec(memory_space=pl.ANY)


class _Phase:
    def __init__(self, ins, out_shapes, aliases, n_sems, start, finish, then):
        self.ins, self.out_shapes, self.aliases, self.n_sems = list(ins), list(out_shapes), dict(aliases), n_sems
        self.start, self.finish, self.then = start, finish, then


def _call(body, name, grid, in_specs, out_specs, out_shape, ins, scratch=(), prefetch=(), phases=()):
    n_pre, n_in, n_out, n_sc = len(prefetch), len(in_specs), len(out_specs), len(scratch)
    ph_in = [len(p.ins) for p in phases]
    ph_out = [len(p.out_shapes) for p in phases]

    def kernel_body(*refs):
        pos = [0]

        def take(k):
            pos[0] += k
            return refs[pos[0] - k : pos[0]]

        pre, ins_ = take(n_pre), take(n_in)
        p_ins = [take(k) for k in ph_in]
        outs_ = take(n_out)
        p_outs = [take(k) for k in ph_out]
        sc = take(n_sc)
        sems = [take(2) for _ in phases]
        if phases:
            ids = [pl.program_id(a) for a in range(len(grid))]
            first = functools.reduce(jnp.logical_and, [i == 0 for i in ids])
            last = functools.reduce(jnp.logical_and, [i == g - 1 for i, g in zip(ids, grid)])

            @pl.when(first)
            def _():
                for p, pi, po, (send, recv) in zip(phases, p_ins, p_outs, sems):
                    p.start(pi, po, send, recv)

        if body is not None:
            body(*pre, *ins_, *outs_, *sc)
        if phases:

            @pl.when(last)
            def _():
                for p, pi, po, (send, recv) in zip(phases, p_ins, p_outs, sems):
                    p.finish(pi, po, send, recv)

    aliases = {}
    i0, o0 = n_pre + n_in, n_out
    for p in phases:
        for i, o in p.aliases.items():
            aliases[i0 + i] = o0 + o
        i0 += len(p.ins)
        o0 += len(p.out_shapes)
    all_in = list(in_specs) + [_ANY] * sum(ph_in)
    all_out = list(out_specs) + [_ANY] * sum(ph_out)
    all_scratch = list(scratch)
    for p in phases:
        all_scratch += [pltpu.SemaphoreType.DMA((p.n_sems,)), pltpu.SemaphoreType.DMA((p.n_sems,))]
    shapes = list(out_shape) + [s for p in phases for s in p.out_shapes]
    operands = list(prefetch) + list(ins) + [a for p in phases for a in p.ins]
    sem = ("arbitrary",) * len(grid)
    params = pltpu.CompilerParams(dimension_semantics=sem, vmem_limit_bytes=VMEM_LIMIT_BYTES)
    if n_pre:
        res = pl.pallas_call(
            kernel_body, name=name, out_shape=shapes, input_output_aliases=aliases, compiler_params=params,
            grid_spec=pltpu.PrefetchScalarGridSpec(
                num_scalar_prefetch=n_pre, grid=grid, in_specs=all_in, out_specs=all_out, scratch_shapes=all_scratch
            ),
        )(*operands)
    else:
        res = pl.pallas_call(
            kernel_body, name=name, grid=grid, in_specs=all_in, out_specs=all_out, out_shape=shapes,
            scratch_shapes=all_scratch, input_output_aliases=aliases, compiler_params=params,
        )(*operands)
    res = list(res)
    outs, rest = res[:n_out], res[n_out:]
    p_res = []
    for k in ph_out:
        p_res.append(rest[:k])
        rest = rest[k:]
    return outs, p_res


def _place():
    return lax.axis_index("x"), lax.axis_index("y"), lax.axis_index("c")


def _other_chips():
    x, y, _ = _place()
    return [(1 - x, y), (x, 1 - y), (1 - x, 1 - y)]


def _flip(k):
    x, y, c = _place()
    return (1 - x if k & 4 else x, 1 - y if k & 2 else y, 1 - c if k & 1 else c)


def _remote(src, dst, send, recv, k, to):
    return pltpu.make_async_remote_copy(
        src_ref=src, dst_ref=dst, send_sem=send.at[k], recv_sem=recv.at[k], device_id=to, device_id_type=MESH
    )


def _all_gather_small(arrs, name):
    n = len(arrs)

    def body(*refs):
        ins, outs = refs[:n], refs[n : 2 * n]
        send, recv, loc = refs[2 * n :]
        x, y, c = _place()
        me = 4 * x + 2 * y + c
        local = [pltpu.make_async_copy(ins[a], outs[a].at[me], loc.at[a]) for a in range(n)]
        for cp in local:
            cp.start()
        remote = [
            _remote(ins[a], outs[a].at[me], send, recv, a * (N_DEV - 1) + k - 1, _flip(k))
            for a in range(n)
            for k in range(1, N_DEV)
        ]
        for cp in remote:
            cp.start()
        for cp in remote:
            cp.wait()
        for cp in local:
            cp.wait()

    vm = pl.BlockSpec(memory_space=pltpu.VMEM)
    return pl.pallas_call(
        body,
        name=name,
        out_shape=[jax.ShapeDtypeStruct((N_DEV,) + a.shape, a.dtype) for a in arrs],
        in_specs=[vm] * n,
        out_specs=[vm] * n,
        scratch_shapes=[
            pltpu.SemaphoreType.DMA((n * (N_DEV - 1),)),
            pltpu.SemaphoreType.DMA((n * (N_DEV - 1),)),
            pltpu.SemaphoreType.DMA((n,)),
        ],
        compiler_params=pltpu.CompilerParams(vmem_limit_bytes=VMEM_LIMIT_BYTES),
    )(*arrs)


class _Big:
    KINDS = {"full": (True, True), "half": (True, False), "shard": (False, True), "block": (False, False)}

    def __init__(self, f3, s3, h3):
        assert s3 != h3
        self.f3, self.s3, self.h3 = tuple(f3), s3, h3
        self.bd = tuple(f3[a] // (N_CHIPS if a == s3 else 1) // (2 if a == h3 else 1) for a in range(3))
        self.tile = (1, _row_tile(self.bd[1], self.bd[2]), self.bd[2])
        self.grid = tuple(self.bd[a] // self.tile[a] for a in range(3))

    def dims(self, kind):
        chips, halves = self.KINDS[kind]
        return tuple(
            self.bd[a] * (N_CHIPS if chips and a == self.s3 else 1) * (2 if halves and a == self.h3 else 1) for a in range(3)
        )

    def view(self, ref, chip=None, half=None, batch0=0, both_halves=True):
        start = [batch0, 0, 0]
        size = list(ref.shape)
        size[0] = self.bd[0] * (2 if self.h3 == 0 and both_halves else 1)
        if chip is not None:
            start[self.s3] += chip * self.bd[self.s3]
            size[self.s3] = self.bd[self.s3]
        if half is not None:
            start[self.h3] += half * self.bd[self.h3]
            size[self.h3] = self.bd[self.h3]
        return ref.at[tuple(pl.ds(st, sz) for st, sz in zip(start, size))]

    def spec(self, chip_from=None, half_from=None, lead=(), batch0=0):
        extra = "grid" in (chip_from, half_from)

        def index(*args):
            pref, idx = args[-1], list(args[int(extra) : -1])
            idx[0] += batch0
            if chip_from:
                idx[self.s3] += (pref[0] if chip_from == "pref" else args[0]) * self.grid[self.s3]
            if half_from:
                idx[self.h3] += (pref[1] if half_from == "pref" else args[0]) * self.grid[self.h3]
            return (0,) * len(lead) + tuple(idx)

        return pl.BlockSpec(tuple(lead) + self.tile, index)


def _same(arrs):
    return [jax.ShapeDtypeStruct(a.shape, a.dtype) for a in arrs]


def _phase_gather_ici(arrs, bigs, then):
    n = len(arrs)

    def copies(outs, send, recv, arriving):
        x, y, c = _place()
        return [
            _remote(blk, blk, send, recv, 3 * a + j, (*chip, c))
            for j, chip in enumerate(_other_chips())
            for a in range(n)
            for blk in [bigs[a].view(outs[a], 2 * chip[0] + chip[1] if arriving else 2 * x + y, c)]
        ]

    def start(ins, outs, send, recv):
        for cp in copies(outs, send, recv, False):
            cp.start()

    def finish(ins, outs, send, recv):
        for cp in copies(outs, send, recv, True):
            cp.wait_recv()
        for cp in copies(outs, send, recv, False):
            cp.wait_send()

    return _Phase(arrs, _same(arrs), {a: a for a in range(n)}, 3 * n, start, finish, then)


def _phase_gather_sibling(arrs, bigs, then):
    n = len(arrs)

    def copies(outs, send, recv, arriving):
        x, y, c = _place()
        return [
            _remote(blk, blk, send, recv, 3 * a + j, (x, y, 1 - c))
            for j, chip in enumerate(_other_chips())
            for a in range(n)
            for blk in [bigs[a].view(outs[a], 2 * chip[0] + chip[1], 1 - c if arriving else c)]
        ]

    def start(ins, outs, send, recv):
        for cp in copies(outs, send, recv, False):
            cp.start()

    def finish(ins, outs, send, recv):
        for cp in copies(outs, send, recv, True):
            cp.wait_recv()
        for cp in copies(outs, send, recv, False):
            cp.wait_send()

    return _Phase(arrs, _same(arrs), {a: a for a in range(n)}, 3 * n, start, finish, then)


def _phase_pair_exchange(grads, bigs, then):
    n = len(grads)

    def copies(ins, outs, send, recv):
        x, y, c = _place()
        return [_remote(bigs[a].view(ins[a], None, 1 - c), outs[a], send, recv, a, (x, y, 1 - c)) for a in range(n)]

    def start(ins, outs, send, recv):
        for cp in copies(ins, outs, send, recv):
            cp.start()

    def finish(ins, outs, send, recv):
        for cp in copies(ins, outs, send, recv):
            cp.wait()

    shapes = [jax.ShapeDtypeStruct(b.dims("half"), BF16) for b in bigs]
    return _Phase(grads, shapes, {}, n, start, finish, then)


def _phase_chip_exchange(sums, bigs, then):
    n = len(sums)

    def copies(ins, outs, send, recv):
        _, _, c = _place()
        return [
            _remote(bigs[a].view(ins[a], 2 * chip[0] + chip[1], both_halves=False), outs[a].at[j], send, recv, 3 * a + j, (*chip, c))
            for j, chip in enumerate(_other_chips())
            for a in range(n)
        ]

    def start(ins, outs, send, recv):
        for cp in copies(ins, outs, send, recv):
            cp.start()

    def finish(ins, outs, send, recv):
        for cp in copies(ins, outs, send, recv):
            cp.wait()

    shapes = [jax.ShapeDtypeStruct((N_CHIPS - 1,) + b.dims("block"), BF16) for b in bigs]
    return _Phase(sums, shapes, {}, 3 * n, start, finish, then)


def _phase_pair_broadcast(stacks, bigs, batch0s, then):
    n = len(stacks)

    def start(ins, outs, send, recv):
        x, y, c = _place()
        for a in range(n):
            blk = bigs[a].view(outs[a], None, c, batch0s[a])
            _remote(blk, blk, send, recv, a, (x, y, 1 - c)).start()

    def finish(ins, outs, send, recv):
        x, y, c = _place()
        for a in range(n):
            mine = bigs[a].view(outs[a], None, c, batch0s[a])
            theirs = bigs[a].view(outs[a], None, 1 - c, batch0s[a])
            _remote(mine, mine, send, recv, a, (x, y, 1 - c)).wait_send()
            _remote(theirs, theirs, send, recv, a, (x, y, 1 - c)).wait_recv()

    return _Phase(stacks, _same(stacks), {a: a for a in range(n)}, n, start, finish, then)


def _tile_call(body, name, big, where, extra, ins, in_specs, out_specs, out_shape, phases=()):
    grid = ((extra,) if extra else ()) + big.grid
    return _call(body, name, grid, in_specs, out_specs, out_shape, ins, prefetch=(where,), phases=phases)


def _cast_into_full(w_stack, batch0, big, where, name, phases=()):
    def body(_, w_ref, o_ref):
        o_ref[...] = w_ref[...].astype(BF16)

    return _tile_call(
        body, name, big, where, 2, [w_stack], [big.spec(None, "grid", batch0=batch0)], [big.spec("pref", "grid")],
        [jax.ShapeDtypeStruct(big.dims("full"), BF16)], phases,
    )


def _pair_sum(g_full, recv_half, big, where, name, phases=()):
    def body(_, g_ref, r_ref, o_ref):
        o_ref[...] = (g_ref[...].astype(F32) + r_ref[...].astype(F32)).astype(BF16)

    half = big.spec("grid", None)
    return _tile_call(
        body, name, big, where, N_CHIPS, [g_full, recv_half], [big.spec("grid", "pref"), half], [half],
        [jax.ShapeDtypeStruct(big.dims("half"), BF16)], phases,
    )


def _chip_sum(chip_sum, parts, big, where, stack, stack_shape, batch0, name, phases=()):
    def body(_, own_ref, p_ref, *rest):
        acc = own_ref[...].astype(F32)
        for k in range(N_CHIPS - 1):
            acc = acc + p_ref[k].astype(F32)
        rest[-1][...] = acc

    ins = [chip_sum, parts] + ([stack] if stack is not None else [])
    in_specs = [big.spec("pref", None), big.spec(None, None, lead=(N_CHIPS - 1,))] + ([_ANY] if stack is not None else [])
    grid = big.grid
    n_pre = 1
    aliases = {n_pre + 2: 0} if stack is not None else {}
    return pl.pallas_call(
        body, name=name, out_shape=jax.ShapeDtypeStruct(stack_shape, F32), input_output_aliases=aliases,
        grid_spec=pltpu.PrefetchScalarGridSpec(
            num_scalar_prefetch=n_pre, grid=grid, in_specs=in_specs, out_specs=big.spec(None, "pref", batch0=batch0)
        ),
        compiler_params=pltpu.CompilerParams(dimension_semantics=("arbitrary",) * len(grid), vmem_limit_bytes=VMEM_LIMIT_BYTES),
    )(where, *ins)


def _adam_rows(w, g, m, v, batch0, nb, prev, name, phases=()):
    b, r, c = w.shape
    tr = _row_tile(r, c)

    def body(w_ref, g_ref, m_ref, v_ref, *rest):
        d_ref, mo_ref, vo_ref = rest[-3:]
        d, mo, vo = _adam(w_ref[...], g_ref[...], m_ref[...], v_ref[...])
        d_ref[...] = d
        mo_ref[...] = mo
        vo_ref[...] = vo

    spec = pl.BlockSpec((1, tr, c), lambda bb, i: (batch0 + bb, i, 0))
    ins = [w, g, m, v] + (list(prev) if prev is not None else [])
    in_specs = [spec] * 4 + ([_ANY] * 3 if prev is not None else [])
    if prev is not None:
        assert not phases
        res = pl.pallas_call(
            body, name=name, grid=(nb, r // tr), in_specs=in_specs, out_specs=[spec] * 3,
            out_shape=[jax.ShapeDtypeStruct(w.shape, F32)] * 3, input_output_aliases={4: 0, 5: 1, 6: 2},
            compiler_params=pltpu.CompilerParams(dimension_semantics=("arbitrary",) * 2, vmem_limit_bytes=VMEM_LIMIT_BYTES),
        )(*ins)
        return list(res), []
    return _call(body, name, (nb, r // tr), in_specs, [spec] * 3, [jax.ShapeDtypeStruct(w.shape, F32)] * 3, ins, phases=phases)


def _mod_fwd(c_all, w_mod, b_cols, phases=()):
    n_layers, d, n = w_mod.shape
    tn = _pick(n, (768, 512, 384, 256, 128))

    def body(c_ref, w_ref, b_ref, o_ref):
        cv = c_ref[...]
        ca = (cv * _sigmoid(cv)).astype(BF16)
        o_ref[0] = _dot(ca, w_ref[0].astype(BF16)) + b_ref[0]

    return _call(
        body, "mod_fwd", (n_layers, n // tn),
        [
            pl.BlockSpec((N_DEV, d), lambda l, j: (0, 0)),
            pl.BlockSpec((1, d, tn), lambda l, j: (l, 0, j)),
            pl.BlockSpec((1, 1, tn), lambda l, j: (l, 0, j)),
        ],
        [pl.BlockSpec((1, N_DEV, tn), lambda l, j: (l, 0, j))],
        [jax.ShapeDtypeStruct((n_layers, N_DEV, n), F32)], [c_all, w_mod, b_cols], phases=phases,
    )


def _mod_bwd_adam(c_all_t, dmod_cols, w, m, v, phases=()):
    n_layers, d, n = w.shape
    tn = _pick(n, (384, 256, 128))

    def body(c_ref, dm_ref, w_ref, m_ref, v_ref, g_ref, d_ref, mo_ref, vo_ref):
        cv = c_ref[...]
        ca = (cv * _sigmoid(cv)).astype(BF16)
        g = _dot(ca, dm_ref[0].astype(BF16))
        g_ref[0] = g
        dl, mo, vo = _adam(w_ref[0], g, m_ref[0], v_ref[0])
        d_ref[0] = dl
        mo_ref[0] = mo
        vo_ref[0] = vo

    wspec = pl.BlockSpec((1, d, tn), lambda l, j: (l, 0, j))
    return _call(
        body, "mod_bwd_adam", (n_layers, n // tn),
        [pl.BlockSpec((d, N_DEV), lambda l, j: (0, 0)), pl.BlockSpec((1, N_DEV, tn), lambda l, j: (l, 0, j)), wspec, wspec, wspec],
        [wspec] * 4, [jax.ShapeDtypeStruct(w.shape, F32)] * 4, [c_all_t, dmod_cols, w, m, v], phases=phases,
    )


def _ffn_fwd(x, vec, w_in, w_out, name, phases=()):
    s, d = x.shape
    f = w_out.shape[1]
    tm = _pick(s, (1024, 512, 256, 128))
    tf = _pick(f, (256, 128))
    nf = f // tf

    def body(x_ref, vec_ref, wg_ref, wu_ref, wo_ref, xo_ref, g_ref, u_ref, y_ref, h_sc, acc_sc):
        j = pl.program_id(1)

        @pl.when(j == 0)
        def _():
            h_sc[...] = _modulate(x_ref[...], vec_ref).astype(BF16)
            acc_sc[...] = jnp.zeros_like(acc_sc)

        h = h_sc[...]
        g = _dot(h, wg_ref[0])
        u = _dot(h, wu_ref[0])
        g_ref[...] = g.astype(BF16)
        u_ref[...] = u.astype(BF16)
        a = (g * _sigmoid(g) * u).astype(BF16)
        acc_sc[...] += _dot(a, wo_ref[0])

        @pl.when(j == nf - 1)
        def _():
            yv = acc_sc[...]
            xo_ref[...] = x_ref[...] + 0.5 * vec_ref[3:4, :] * yv
            y_ref[...] = yv.astype(BF16)

    row = pl.BlockSpec((tm, d), lambda i, j: (i, 0))
    hid = pl.BlockSpec((tm, tf), lambda i, j: (i, j))
    return _call(
        body, name, (s // tm, nf),
        [
            row,
            pl.BlockSpec((8, d), lambda i, j: (0, 0)),
            pl.BlockSpec((1, d, tf), lambda i, j: (0, 0, j)),
            pl.BlockSpec((1, d, tf), lambda i, j: (0, 0, nf + j)),
            pl.BlockSpec((1, tf, d), lambda i, j: (0, j, 0)),
        ],
        [row, hid, hid, row],
        [
            jax.ShapeDtypeStruct((s, d), F32),
            jax.ShapeDtypeStruct((s, f), BF16),
            jax.ShapeDtypeStruct((s, f), BF16),
            jax.ShapeDtypeStruct((s, d), BF16),
        ],
        [x, vec, w_in, w_in, w_out],
        scratch=[pltpu.VMEM((tm, d), BF16), pltpu.VMEM((tm, d), F32)], phases=phases,
    )


def _ffn_bwd(dxo, x, vec, gg, uu, y, w_in, w_out, name, phases=()):
    s, d = x.shape
    f = w_out.shape[1]
    tm = _pick(s, (512, 256, 128))
    tf = _pick(f, (256, 128))
    nf = f // tf

    def body(dxo_ref, x_ref, vec_ref, g_ref, u_ref, y_ref, wg_ref, wu_ref, wo_ref,
             dx_ref, dg_ref, du_ref, a_ref, h_ref, dy_ref, dvec_ref, acc_sc):
        i, j = pl.program_id(0), pl.program_id(1)

        @pl.when((i == 0) & (j == 0))
        def _():
            dvec_ref[...] = jnp.zeros_like(dvec_ref)

        @pl.when(j == 0)
        def _():
            dxo_v = dxo_ref[...]
            dy_ref[...] = (0.5 * vec_ref[3:4, :] * dxo_v).astype(BF16)
            dvec_ref[3:4, :] += 0.5 * jnp.sum(dxo_v * y_ref[...].astype(F32), axis=0, keepdims=True)
            acc_sc[...] = jnp.zeros_like(acc_sc)

        da = _dot_nt(dy_ref[...], wo_ref[0])
        g = g_ref[...].astype(F32)
        u = u_ref[...].astype(F32)
        sig = _sigmoid(g)
        sl = g * sig
        a_ref[...] = (sl * u).astype(BF16)
        dg = (da * u * (sig * (1.0 + g * (1.0 - sig)))).astype(BF16)
        du = (da * sl).astype(BF16)
        dg_ref[...] = dg
        du_ref[...] = du
        acc_sc[...] += _dot_nt(dg, wg_ref[0]) + _dot_nt(du, wu_ref[0])

        @pl.when(j == nf - 1)
        def _():
            dx, h = _modulate_bwd(x_ref[...], acc_sc[...], vec_ref, dvec_ref)
            dx_ref[...] = dxo_ref[...] + dx
            h_ref[...] = h.astype(BF16)

    row = pl.BlockSpec((tm, d), lambda i, j: (i, 0))
    hid = pl.BlockSpec((tm, tf), lambda i, j: (i, j))
    vecs = pl.BlockSpec((8, d), lambda i, j: (0, 0))
    return _call(
        body, name, (s // tm, nf),
        [
            row, row, vecs, hid, hid, row,
            pl.BlockSpec((1, d, tf), lambda i, j: (0, 0, j)),
            pl.BlockSpec((1, d, tf), lambda i, j: (0, 0, nf + j)),
            pl.BlockSpec((1, tf, d), lambda i, j: (0, j, 0)),
        ],
        [row, hid, hid, hid, row, row, vecs],
        [
            jax.ShapeDtypeStruct((s, d), F32),
            jax.ShapeDtypeStruct((s, f), BF16),
            jax.ShapeDtypeStruct((s, f), BF16),
            jax.ShapeDtypeStruct((s, f), BF16),
            jax.ShapeDtypeStruct((s, d), BF16),
            jax.ShapeDtypeStruct((s, d), BF16),
            jax.ShapeDtypeStruct((8, d), F32),
        ],
        [dxo, x, vec, gg, uu, y, w_in, w_in, w_out],
        scratch=[pltpu.VMEM((tm, d), F32)], phases=phases,
    )


def _grad_matmul(a, b, full_shape, col0, prev, name, phases=()):
    s, k1 = a.shape
    n = b.shape[1]
    tk = _pick(k1, (512, 256, 128))
    tn = _pick(n, (1408, 1024, 640, 512, 256, 128))
    assert col0 % tn == 0

    def body(a_ref, b_ref, *rest):
        rest[-1][0] = _dot_tn(a_ref[...], b_ref[...]).astype(BF16)

    in_specs = [pl.BlockSpec((s, tk), lambda i, j: (0, i)), pl.BlockSpec((s, tn), lambda i, j: (0, j))]
    out_spec = pl.BlockSpec((1, tk, tn), lambda i, j: (0, i, col0 // tn + j))
    shape = jax.ShapeDtypeStruct(full_shape, BF16)
    if prev is not None:
        assert not phases
        res = pl.pallas_call(
            body, name=name, grid=(k1 // tk, n // tn), in_specs=in_specs + [_ANY], out_specs=out_spec, out_shape=shape,
            input_output_aliases={2: 0},
            compiler_params=pltpu.CompilerParams(dimension_semantics=("arbitrary",) * 2, vmem_limit_bytes=VMEM_LIMIT_BYTES),
        )(a, b, prev)
        return [res], []
    return _call(body, name, (k1 // tk, n // tn), in_specs, [out_spec], [shape], [a, b], phases=phases)


def _proj_mod_fwd(x, vec, w, phases=()):
    s, d = x.shape
    n = w.shape[2]
    tm = _pick(s, (512, 256, 128))
    tn = _pick(n, (640, 512, 256, 128))

    def body(x_ref, vec_ref, w_ref, o_ref, h_sc):
        @pl.when(pl.program_id(1) == 0)
        def _():
            h_sc[...] = _modulate(x_ref[...], vec_ref).astype(BF16)

        o_ref[...] = _dot(h_sc[...], w_ref[0])

    return _call(
        body, "ab_in_fwd", (s // tm, n // tn),
        [
            pl.BlockSpec((tm, d), lambda i, j: (i, 0)),
            pl.BlockSpec((8, d), lambda i, j: (0, 0)),
            pl.BlockSpec((1, d, tn), lambda i, j: (0, 0, j)),
        ],
        [pl.BlockSpec((tm, tn), lambda i, j: (i, j))],
        [jax.ShapeDtypeStruct((s, n), F32)], [x, vec, w],
        scratch=[pltpu.VMEM((tm, d), BF16)], phases=phases,
    )


def _proj_res_fwd(a, w, x, vec, phases=()):
    s, kd = a.shape
    d = x.shape[1]
    tm = _pick(s, (512, 256, 128))

    def body(a_ref, w_ref, x_ref, vec_ref, xo_ref, y_ref):
        yv = _dot(a_ref[...], w_ref[0])
        xo_ref[...] = x_ref[...] + vec_ref[3:4, :] * yv
        y_ref[...] = yv.astype(BF16)

    row = pl.BlockSpec((tm, d), lambda i: (i, 0))
    return _call(
        body, "ab_out_fwd", (s // tm,),
        [pl.BlockSpec((tm, kd), lambda i: (i, 0)), pl.BlockSpec((1, kd, d), lambda i: (0, 0, 0)), row, pl.BlockSpec((8, d), lambda i: (0, 0))],
        [row, row],
        [jax.ShapeDtypeStruct((s, d), F32), jax.ShapeDtypeStruct((s, d), BF16)], [a, w, x, vec], phases=phases,
    )


def _proj_res_bwd(dxo, y, vec, w, phases=()):
    s, d = dxo.shape
    kd = w.shape[1]
    tm = _pick(s, (512, 256, 128))

    def body(dxo_ref, y_ref, vec_ref, w_ref, dy_ref, da_ref, dgate_ref):
        @pl.when(pl.program_id(0) == 0)
        def _():
            dgate_ref[...] = jnp.zeros_like(dgate_ref)

        dxo_v = dxo_ref[...]
        dy = (vec_ref[3:4, :] * dxo_v).astype(BF16)
        dy_ref[...] = dy
        dgate_ref[3:4, :] += jnp.sum(dxo_v * y_ref[...].astype(F32), axis=0, keepdims=True)
        da_ref[...] = _dot_nt(dy, w_ref[0]).astype(BF16)

    row = pl.BlockSpec((tm, d), lambda i: (i, 0))
    vecs = pl.BlockSpec((8, d), lambda i: (0, 0))
    return _call(
        body, "ab_out_bwd", (s // tm,),
        [row, row, vecs, pl.BlockSpec((1, kd, d), lambda i: (0, 0, 0))],
        [row, pl.BlockSpec((tm, kd), lambda i: (i, 0)), vecs],
        [jax.ShapeDtypeStruct((s, d), BF16), jax.ShapeDtypeStruct((s, kd), BF16), jax.ShapeDtypeStruct((8, d), F32)],
        [dxo, y, vec, w], phases=phases,
    )


def _proj_mod_bwd(dproj, w, x, vec, dxo, dvec_in, phases=()):
    s, n = dproj.shape
    d = x.shape[1]
    tm = _pick(s, (512, 256, 128))

    def body(dp_ref, w_ref, x_ref, vec_ref, dxo_ref, dvi_ref, dx_ref, h_ref, dvec_ref):
        @pl.when(pl.program_id(0) == 0)
        def _():
            dvec_ref[...] = dvi_ref[...]

        dh = _dot_nt(dp_ref[...], w_ref[0])
        dx, h = _modulate_bwd(x_ref[...], dh, vec_ref, dvec_ref)
        dx_ref[...] = dxo_ref[...] + dx
        h_ref[...] = h.astype(BF16)

    row = pl.BlockSpec((tm, d), lambda i: (i, 0))
    vecs = pl.BlockSpec((8, d), lambda i: (0, 0))
    return _call(
        body, "ab_in_bwd", (s // tm,),
        [pl.BlockSpec((tm, n), lambda i: (i, 0)), pl.BlockSpec((1, d, n), lambda i: (0, 0, 0)), row, vecs, row, vecs],
        [row, row, vecs],
        [jax.ShapeDtypeStruct((s, d), F32), jax.ShapeDtypeStruct((s, d), BF16), jax.ShapeDtypeStruct((8, d), F32)],
        [dproj, w, x, vec, dxo, dvec_in], phases=phases,
    )


def _tril(n):
    return lax.broadcasted_iota(jnp.int32, (n, n), 0) >= lax.broadcasted_iota(jnp.int32, (n, n), 1)


def _layernorm_stats(gv):
    mu = jnp.mean(gv, axis=-1, keepdims=True)
    cen = gv - mu
    rstd = lax.rsqrt(jnp.mean(cen * cen, axis=-1, keepdims=True) + EPS)
    return cen * rstd, rstd


def _shift_down(q, k, above_ref, c_cg, c_xb, first):
    width = q.shape[1]
    rows = lax.broadcasted_iota(jnp.int32, q.shape, 0)
    out = pltpu.roll(q, k, 0)
    for r in range(k):
        src = CONV_HALO - k + r
        above = above_ref[src : src + 1, c_cg : c_cg + width] * above_ref[src : src + 1, c_xb : c_xb + width]
        above = jnp.where(first, 0.0, above)
        out = jnp.where(rows == r, above, out)
    return out


def _ab_mix_fwd(proj, norm_v, w_s, b_rows, conv_w, phases=()):
    s, n = proj.shape
    heads, chunk, _ = w_s.shape
    da = norm_v.shape[1]
    hd = da // heads
    db = conv_w.shape[1]
    tm = _pick(s, (512, 256, 128))

    def body(p_ref, ph_ref, nv_ref, ws_ref, b_ref, cw_ref, o_ref):
        first = pl.program_id(0) == 0
        gu, _ = _gelu(p_ref[:, 0:da])
        gv, _ = _gelu(p_ref[:, da : 2 * da])
        xhat, _ = _layernorm_stats(gv)
        vn = (xhat * nv_ref[...]).astype(BF16)
        mask = _tril(chunk)
        for hh in range(heads):
            wm = jnp.where(mask, ws_ref[hh], 0.0).astype(BF16)
            cols = slice(hh * hd, (hh + 1) * hd)
            for nn in range(tm // chunk):
                rows = slice(nn * chunk, (nn + 1) * chunk)
                z = _dot(wm, vn[rows, cols]) + b_ref[:, cols]
                o_ref[rows, cols] = (gu[rows, cols] * z).astype(BF16)
        c_cg, c_xb = 2 * da + db, 2 * da + 2 * db
        bg = p_ref[:, 2 * da : 2 * da + db]
        q = p_ref[:, c_cg : c_cg + db] * p_ref[:, c_xb : c_xb + db]
        q1 = _shift_down(q, 1, ph_ref, c_cg, c_xb, first)
        q2 = _shift_down(q, 2, ph_ref, c_cg, c_xb, first)
        conv = cw_ref[0:1, :] * q2 + cw_ref[1:2, :] * q1 + cw_ref[2:3, :] * q
        o_ref[:, da : da + db] = (bg * conv).astype(BF16)

    nh = tm // CONV_HALO
    return _call(
        body, "ab_mix_fwd", (s // tm,),
        [
            pl.BlockSpec((tm, n), lambda i: (i, 0)),
            pl.BlockSpec((CONV_HALO, n), lambda i: (jnp.maximum(i * nh - 1, 0), 0)),
            pl.BlockSpec((1, da), lambda i: (0, 0)),
            pl.BlockSpec((heads, chunk, chunk), lambda i: (0, 0, 0)),
            pl.BlockSpec((chunk, da), lambda i: (0, 0)),
            pl.BlockSpec((3, db), lambda i: (0, 0)),
        ],
        [pl.BlockSpec((tm, da + db), lambda i: (i, 0))],
        [jax.ShapeDtypeStruct((s, da + db), BF16)], [proj, proj, norm_v, w_s, b_rows, conv_w], phases=phases,
    )


def _ab_mix_bwd(proj, dcat, norm_v, w_s, b_rows, conv_w, phases=()):
    s, n = proj.shape
    heads, chunk, _ = w_s.shape
    da = norm_v.shape[1]
    hd = da // heads
    db = conv_w.shape[1]
    tm = _pick(s, (512, 256, 128))
    nblk = s // tm
    dhalo = 2 * CONV_HALO

    def body(p_ref, pa_ref, pb_ref, dc_ref, dcb_ref, nv_ref, ws_ref, b_ref, cw_ref,
             dp_ref, dnv_ref, dws_ref, dzs_ref, dcw_ref, dvn_sc):
        i = pl.program_id(0)
        first, last = i == 0, i == nblk - 1

        @pl.when(first)
        def _():
            dnv_ref[...] = jnp.zeros_like(dnv_ref)
            dws_ref[...] = jnp.zeros_like(dws_ref)
            dzs_ref[...] = jnp.zeros_like(dzs_ref)
            dcw_ref[...] = jnp.zeros_like(dcw_ref)

        uu = p_ref[:, 0:da]
        gu, gu_grad = _gelu(uu)
        gv, gv_grad = _gelu(p_ref[:, da : 2 * da])
        xhat, rstd = _layernorm_stats(gv)
        nv = nv_ref[...]
        vn = (xhat * nv).astype(BF16)
        dya = dc_ref[:, 0:da].astype(F32)
        dz = (dya * gu).astype(BF16)
        mask = _tril(chunk)
        for hh in range(heads):
            wm = jnp.where(mask, ws_ref[hh], 0.0).astype(BF16)
            cols = slice(hh * hd, (hh + 1) * hd)
            dws = jnp.zeros((chunk, chunk), F32)
            for nn in range(tm // chunk):
                rows = slice(nn * chunk, (nn + 1) * chunk)
                z = _dot(wm, vn[rows, cols]) + b_ref[:, cols]
                dp_ref[rows, cols] = (dya[rows, cols] * z * gu_grad[rows, cols]).astype(BF16)
                dz_blk = dz[rows, cols]
                dws = dws + _dot_nt(dz_blk, vn[rows, cols])
                dzs_ref[:, cols] += dz_blk.astype(F32)
                dvn = _dot_tn(wm, dz_blk)
                dnv_ref[:, cols] += jnp.sum(dvn * xhat[rows, cols], axis=0, keepdims=True)
                dvn_sc[rows, cols] = dvn
            dws_ref[hh] += jnp.where(mask, dws, 0.0)
        dxhat = dvn_sc[...] * nv
        dgv = rstd * (dxhat - jnp.mean(dxhat, axis=-1, keepdims=True) - xhat * jnp.mean(dxhat * xhat, axis=-1, keepdims=True))
        dp_ref[:, da : 2 * da] = (dgv * gv_grad).astype(BF16)

        c_bg, c_cg, c_xb = 2 * da, 2 * da + db, 2 * da + 2 * db
        bg = p_ref[:, c_bg : c_bg + db]
        cg = p_ref[:, c_cg : c_cg + db]
        xb = p_ref[:, c_xb : c_xb + db]
        q = cg * xb
        q1 = _shift_down(q, 1, pa_ref, c_cg, c_xb, first)
        q2 = _shift_down(q, 2, pa_ref, c_cg, c_xb, first)
        dyb = dc_ref[:, da : da + db].astype(F32)
        conv = cw_ref[0:1, :] * q2 + cw_ref[1:2, :] * q1 + cw_ref[2:3, :] * q
        dp_ref[:, c_bg : c_bg + db] = (dyb * conv).astype(BF16)
        e = dyb * bg
        dcw_ref[0:1, :] += jnp.sum(e * q2, axis=0, keepdims=True)
        dcw_ref[1:2, :] += jnp.sum(e * q1, axis=0, keepdims=True)
        dcw_ref[2:3, :] += jnp.sum(e * q, axis=0, keepdims=True)
        rows = lax.broadcasted_iota(jnp.int32, e.shape, 0)
        dq = cw_ref[2:3, :] * e
        for kk in (1, 2):
            ek = pltpu.roll(e, tm - kk, 0)
            for r in range(kk):
                below = dcb_ref[r : r + 1, da : da + db].astype(F32) * pb_ref[r : r + 1, c_bg : c_bg + db]
                below = jnp.where(last, 0.0, below)
                ek = jnp.where(rows == tm - kk + r, below, ek)
            dq = dq + cw_ref[2 - kk : 3 - kk, :] * ek
        dp_ref[:, c_cg : c_cg + db] = (dq * xb).astype(BF16)
        dp_ref[:, c_xb : c_xb + db] = (dq * cg).astype(BF16)

    nh = tm // CONV_HALO
    nhb = tm // dhalo
    const2 = lambda i: (0, 0)
    return _call(
        body, "ab_mix_bwd", (nblk,),
        [
            pl.BlockSpec((tm, n), lambda i: (i, 0)),
            pl.BlockSpec((CONV_HALO, n), lambda i: (jnp.maximum(i * nh - 1, 0), 0)),
            pl.BlockSpec((CONV_HALO, n), lambda i: (jnp.minimum((i + 1) * nh, s // CONV_HALO - 1), 0)),
            pl.BlockSpec((tm, da + db), lambda i: (i, 0)),
            pl.BlockSpec((dhalo, da + db), lambda i: (jnp.minimum((i + 1) * nhb, s // dhalo - 1), 0)),
            pl.BlockSpec((1, da), const2),
            pl.BlockSpec((heads, chunk, chunk), lambda i: (0, 0, 0)),
            pl.BlockSpec((chunk, da), const2),
            pl.BlockSpec((3, db), const2),
        ],
        [
            pl.BlockSpec((tm, n), lambda i: (i, 0)),
            pl.BlockSpec((1, da), const2),
            pl.BlockSpec((heads, chunk, chunk), lambda i: (0, 0, 0)),
            pl.BlockSpec((chunk, da), const2),
            pl.BlockSpec((3, db), const2),
        ],
        [
            jax.ShapeDtypeStruct((s, n), BF16),
            jax.ShapeDtypeStruct((1, da), F32),
            jax.ShapeDtypeStruct((heads, chunk, chunk), F32),
            jax.ShapeDtypeStruct((chunk, da), F32),
            jax.ShapeDtypeStruct((3, db), F32),
        ],
        [proj, proj, proj, dcat, dcat, norm_v, w_s, b_rows, conv_w],
        scratch=[pltpu.VMEM((tm, da), F32)], phases=phases,
    )


def _pool_counts(tm, i, w):
    t = i * tm + lax.broadcasted_iota(jnp.int32, (tm, 1), 0)
    return jnp.minimum(t + 1, w).astype(F32)


def _pool_fwd(x, vec, w_grp, scale, phases=()):
    s, d = x.shape
    groups, gd, _ = w_grp.shape
    tm = _pick(s, (512, 256, 128))

    def body(x_ref, xa_ref, vec_ref, w_ref, sc_ref, xo_ref, p_ref, o_ref):
        i = pl.program_id(0)
        h = _modulate(x_ref[...], vec_ref)
        ha = jnp.where(i == 0, 0.0, _modulate(xa_ref[...], vec_ref))
        ext = jnp.concatenate([ha, h], axis=0)
        for gi, w in enumerate(POOL_WINDOWS):
            cols = slice(gi * gd, (gi + 1) * gd)
            acc = ext[:, cols]
            step = 1
            while step < w:
                acc = acc + pltpu.roll(acc, step, 0)
                step *= 2
            p = (acc[POOL_HALO:, :] / _pool_counts(tm, i, w) - h[:, cols]).astype(BF16)
            p_ref[:, cols] = p
            o_ref[:, cols] = _dot(p, w_ref[gi]).astype(BF16)
        xo_ref[...] = x_ref[...] + vec_ref[3:4, :] * (o_ref[...].astype(F32) * sc_ref[...])

    nh = tm // POOL_HALO
    row = pl.BlockSpec((tm, d), lambda i: (i, 0))
    return _call(
        body, "pool_fwd", (s // tm,),
        [
            row,
            pl.BlockSpec((POOL_HALO, d), lambda i: (jnp.maximum(i * nh - 1, 0), 0)),
            pl.BlockSpec((8, d), lambda i: (0, 0)),
            pl.BlockSpec((groups, gd, gd), lambda i: (0, 0, 0)),
            pl.BlockSpec((1, d), lambda i: (0, 0)),
        ],
        [row, row, row],
        [jax.ShapeDtypeStruct((s, d), F32), jax.ShapeDtypeStruct((s, d), BF16), jax.ShapeDtypeStruct((s, d), BF16)],
        [x, x, vec, w_grp, scale], phases=phases,
    )


def _pool_bwd(dxo, x, vec, p, o, w_grp, scale, phases=()):
    s, d = x.shape
    groups, gd, _ = w_grp.shape
    tm = _pick(s, (512, 256, 128))
    nblk = s // tm

    def body(dxo_ref, dxb_ref, x_ref, vec_ref, p_ref, o_ref, w_ref, sc_ref, dx_ref, dw_ref, dsc_ref, dvec_ref, dw_sc):
        i = pl.program_id(0)

        @pl.when(i == 0)
        def _():
            dw_sc[...] = jnp.zeros_like(dw_sc)
            dsc_ref[...] = jnp.zeros_like(dsc_ref)
            dvec_ref[...] = jnp.zeros_like(dvec_ref)

        gate, sc = vec_ref[3:4, :], sc_ref[...]
        dxo_v = dxo_ref[...]
        ov = o_ref[...].astype(F32)
        dvec_ref[3:4, :] += jnp.sum(dxo_v * (ov * sc), axis=0, keepdims=True)
        dy = gate * dxo_v
        dsc_ref[...] += jnp.sum(dy * ov, axis=0, keepdims=True)
        dout = (dy * sc).astype(BF16)
        dout_b = jnp.where(i == nblk - 1, 0.0, gate * dxb_ref[...] * sc).astype(BF16)
        for gi, w in enumerate(POOL_WINDOWS):
            cols = slice(gi * gd, (gi + 1) * gd)
            dw_sc[gi] += _dot_tn(p_ref[:, cols], dout[:, cols])
            wb = w_ref[gi]
            dp = _dot_nt(dout[:, cols], wb)
            dp_b = _dot_nt(dout_b[:, cols], wb)
            e = dp / _pool_counts(tm, i, w)
            t_below = (i + 1) * tm + lax.broadcasted_iota(jnp.int32, (POOL_HALO, 1), 0)
            e_b = dp_b / jnp.minimum(t_below + 1, w).astype(F32)
            acc = jnp.concatenate([e, e_b], axis=0)
            step = 1
            while step < w:
                acc = acc + pltpu.roll(acc, tm + POOL_HALO - step, 0)
                step *= 2
            dx_ref[:, cols] = acc[:tm, :] - dp
        dx, _ = _modulate_bwd(x_ref[...], dx_ref[...], vec_ref, dvec_ref)
        dx_ref[...] = dxo_v + dx

        @pl.when(i == nblk - 1)
        def _():
            dw_ref[...] = dw_sc[...].astype(BF16)

    nh = tm // POOL_HALO
    row = pl.BlockSpec((tm, d), lambda i: (i, 0))
    vecs = pl.BlockSpec((8, d), lambda i: (0, 0))
    wspec = pl.BlockSpec((groups, gd, gd), lambda i: (0, 0, 0))
    return _call(
        body, "pool_bwd", (nblk,),
        [
            row,
            pl.BlockSpec((POOL_HALO, d), lambda i: (jnp.minimum((i + 1) * nh, s // POOL_HALO - 1), 0)),
            row, vecs, row, row, wspec,
            pl.BlockSpec((1, d), lambda i: (0, 0)),
        ],
        [row, wspec, pl.BlockSpec((1, d), lambda i: (0, 0)), vecs],
        [
            jax.ShapeDtypeStruct((s, d), F32),
            jax.ShapeDtypeStruct((groups, gd, gd), BF16),
            jax.ShapeDtypeStruct((1, d), F32),
            jax.ShapeDtypeStruct((8, d), F32),
        ],
        [dxo, dxo, x, vec, p, o, w_grp, scale],
        scratch=[pltpu.VMEM((groups, gd, gd), F32)], phases=phases,
    )


def _loss_head(x, gain, target, phases=()):
    s, d = x.shape
    tm = _pick(s, (512, 256, 128))

    def body(x_ref, g_ref, t_ref, dx_ref, aux_ref):
        @pl.when(pl.program_id(0) == 0)
        def _():
            aux_ref[...] = jnp.zeros_like(aux_ref)

        xv = x_ref[...]
        rstd = _rstd(xv)
        r = xv * rstd
        gain_v = g_ref[...]
        err = r * gain_v - t_ref[...]
        aux_ref[1:2, :] += jnp.sum(err * err, axis=0, keepdims=True)
        dout = err * (1.0 / d)
        aux_ref[0:1, :] += jnp.sum(dout * r, axis=0, keepdims=True)
        dr = dout * gain_v
        dx_ref[...] = rstd * (dr - r * jnp.mean(dr * r, axis=-1, keepdims=True))

    row = pl.BlockSpec((tm, d), lambda i: (i, 0))
    return _call(
        body, "loss_head", (s // tm,),
        [row, pl.BlockSpec((1, d), lambda i: (0, 0)), row],
        [row, pl.BlockSpec((8, d), lambda i: (0, 0))],
        [jax.ShapeDtypeStruct((s, d), F32), jax.ShapeDtypeStruct((8, d), F32)], [x, gain, target], phases=phases,
    )


def _small_adam(gathered, gathered_ws, layout, smalls, chip):
    names = list(smalls)
    n = len(names)

    def body(*refs):
        chip_ref, g_ref, gws_ref = refs[0], refs[1], refs[2]
        wmv = refs[3 : 3 + 3 * n]
        outs = refs[3 + 3 * n : 3 + 7 * n]
        total = refs[-1]
        total[...] = g_ref[0]
        for kdev in range(1, N_DEV):
            total[...] += g_ref[kdev]
        total_ws = gws_ref[0]
        for kdev in range(1, N_DEV):
            total_ws = total_ws + gws_ref[kdev]
        my_chip = chip_ref[0]
        for a, name in enumerate(names):
            w_ref, m_ref, v_ref = wmv[3 * a : 3 * a + 3]
            if name == "ab_w_s":
                g = total_ws
            else:
                row0, rows, col0, cols = layout[name]
                if col0 is None:
                    g = jnp.zeros((rows, cols), F32)
                    for j in range(N_CHIPS):
                        g = g + jnp.where(my_chip == j, total[row0 : row0 + rows, j * cols : (j + 1) * cols], 0.0)
                else:
                    g = total[row0 : row0 + rows, col0 : col0 + cols]
            dl, mo, vo = _adam(w_ref[...], g, m_ref[...], v_ref[...])
            outs[4 * a][...] = g
            outs[4 * a + 1][...] = dl
            outs[4 * a + 2][...] = mo
            outs[4 * a + 3][...] = vo

    ins = [gathered, gathered_ws]
    out_shapes = []
    for name in names:
        ins.extend(smalls[name])
        out_shapes.extend([jax.ShapeDtypeStruct(smalls[name][0].shape, F32)] * 4)
    whole = lambda shape: pl.BlockSpec(shape, functools.partial(lambda nd, i, c: (0,) * nd, len(shape)))
    res = pl.pallas_call(
        body, name="small_adam",
        grid_spec=pltpu.PrefetchScalarGridSpec(
            num_scalar_prefetch=1, grid=(1,),
            in_specs=[whole(a.shape) for a in ins], out_specs=[whole(o.shape) for o in out_shapes],
            scratch_shapes=[pltpu.VMEM(gathered.shape[1:], F32)],
        ),
        out_shape=out_shapes,
        compiler_params=pltpu.CompilerParams(dimension_semantics=("arbitrary",), vmem_limit_bytes=VMEM_LIMIT_BYTES),
    )(chip.reshape(1).astype(jnp.int32), *ins)
    return {name: res[4 * a : 4 * a + 4] for a, name in enumerate(names)}


def _pad_rows(a, rows=8):
    extra = (-a.shape[0]) % rows
    return jnp.pad(a, ((0, extra), (0, 0))) if extra else a


def _pad_cols(a, cols):
    return jnp.pad(a, ((0, 0), (0, cols - a.shape[1]))) if a.shape[1] < cols else a


def _run(fn, *phases):
    outs, p_outs = fn(list(phases))
    for p, po in zip(phases, p_outs):
        p.then(po)
    return outs


def kernel(x, c, norm_g, w_mod, b_mod, w_ffn_in, w_ffn_out, ab_w_in, ab_norm_v, ab_w_s, ab_b_s, ab_conv_w, ab_w_out, pool_w_grp, pool_scale, final_g, loss_target, m_norm_g, m_w_mod, m_b_mod, m_w_ffn_in, m_w_ffn_out, m_ab_w_in, m_ab_norm_v, m_ab_w_s, m_ab_b_s, m_ab_conv_w, m_ab_w_out, m_pool_w_grp, m_pool_scale, m_final_g, v_norm_g, v_w_mod, v_b_mod, v_w_ffn_in, v_w_ffn_out, v_ab_w_in, v_ab_norm_v, v_ab_w_s, v_ab_b_s, v_ab_conv_w, v_ab_w_out, v_pool_w_grp, v_pool_scale, v_final_g):
    ix, iy, ic = _place()
    chip = 2 * ix + iy
    me = 4 * ix + 2 * iy + ic
    where = jnp.stack([chip, ic]).astype(jnp.int32)
    s, d = x.shape[1], x.shape[2]
    x0 = x.reshape(s, d)
    target = loss_target.reshape(s, d)
    n_layers = norm_g.shape[0]
    dq = d // N_CHIPS
    heads, chunk = ab_w_s.shape[1], ab_w_s.shape[2]
    da = ab_norm_v.shape[1]
    db = ab_conv_w.shape[2] * N_CHIPS
    f_hidden = w_ffn_out.shape[2] * N_CHIPS
    assert n_layers == 2 and da % heads == 0

    cw_pad = _pad_cols(ab_conv_w.reshape(3, db // N_CHIPS), dq)
    packed = jnp.concatenate(
        [_pad_rows(c.reshape(N_CHIPS, dq)), _pad_rows(norm_g.reshape(-1, dq)), _pad_rows(pool_scale.reshape(1, dq)), _pad_rows(cw_pad)],
        axis=0,
    )
    (small_all,) = _all_gather_small([packed], "gather_small_inputs")
    by_chip = small_all[0::2]
    c_all = small_all[:, 0:N_CHIPS, :].reshape(N_DEV, d)
    norm_full = by_chip[:, 8 : 8 + 3 * n_layers, :].transpose(1, 0, 2).reshape(3 * n_layers, d)
    pool_scale_full = by_chip[:, 16:17, :].transpose(1, 0, 2).reshape(1, d)
    conv_full = by_chip[:, 24:27, : db // N_CHIPS].transpose(1, 0, 2).reshape(3, db)
    ncol = w_mod.shape[2]
    b_cols = lax.dynamic_slice(b_mod, (0, chip * ncol), (n_layers, ncol)).reshape(n_layers, 1, ncol)

    stacks = {
        "w_ffn_in": tuple(a.reshape((-1,) + a.shape[2:]) for a in (w_ffn_in, m_w_ffn_in, v_w_ffn_in)),
        "w_ffn_out": tuple(a.reshape((-1,) + a.shape[2:]) for a in (w_ffn_out, m_w_ffn_out, v_w_ffn_out)),
        "ab_w_in": (ab_w_in, m_ab_w_in, v_ab_w_in),
        "ab_w_out": (ab_w_out, m_ab_w_out, v_ab_w_out),
        "pool_w_grp": (pool_w_grp[0], m_pool_w_grp[0], v_pool_w_grp[0]),
    }
    big_in = _Big((1, d, 2 * f_hidden), 2, 1)
    big_out = _Big((1, f_hidden, d), 1, 2)
    units = {}
    for l in range(n_layers):
        for k in range(2):
            units[f"in{l}{k}"] = (big_in, "w_ffn_in", 2 * l + k)
            units[f"out{l}{k}"] = (big_out, "w_ffn_out", 2 * l + k)
    units["abin"] = (_Big((1, d, ab_w_in.shape[2] * N_CHIPS), 2, 1), "ab_w_in", 0)
    units["about"] = (_Big((1, ab_w_out.shape[1] * N_CHIPS, d), 1, 2), "ab_w_out", 0)
    units["pool"] = (_Big((pool_w_grp.shape[1], pool_w_grp.shape[2] * N_CHIPS, pool_w_grp.shape[3]), 1, 0), "pool_w_grp", 0)
    big = {u: g for u, (g, _, _) in units.items()}

    weight = {}
    complete = set()

    def cast(u):
        g, st, b0 = units[u]

        def launch(phases):
            (weight[u],), p_outs = _cast_into_full(stacks[st][0], b0, g, where, "cast_" + u, phases)
            return None, p_outs

        return launch

    def gather_ici(*us):
        def then(outs):
            for u, o in zip(us, outs):
                weight[u] = o

        return _phase_gather_ici([weight[u] for u in us], [big[u] for u in us], then)

    def gather_sibling(*us):
        def then(outs):
            for u, o in zip(us, outs):
                weight[u] = o
                complete.add(u)

        return _phase_gather_sibling([weight[u] for u in us], [big[u] for u in us], then)

    def w_of(u):
        assert u in complete, u
        return weight[u]

    _run(cast("in00"))
    _run(cast("out00"))
    mod_cols = _run(lambda phases: _mod_fwd(c_all, w_mod, b_cols, phases), gather_ici("in00"))[0]
    _run(cast("abin"), gather_sibling("in00"), gather_ici("out00"))
    _run(cast("about"), gather_sibling("out00"), gather_ici("abin"))
    _run(cast("in01"), gather_sibling("abin"), gather_ici("about"))
    _run(cast("out01"), gather_sibling("about"))
    for u in ("in10", "out10", "pool", "in11", "out11"):
        _run(cast(u))
    (mod_all,) = _all_gather_small([mod_cols.reshape(n_layers * N_DEV, ncol)], "gather_mod")
    mod_mine = lax.dynamic_index_in_dim(mod_all[0::2].reshape(N_CHIPS, n_layers, N_DEV, ncol), me, axis=2, keepdims=False)
    mod = mod_mine.transpose(1, 0, 2).reshape(n_layers, 3, 3, d)
    vecs = {
        (l, sub): _pad_rows(jnp.concatenate([norm_full[3 * l + sub][None], mod[l, sub]], axis=0))
        for l in range(n_layers)
        for sub in range(3)
    }
    b_rows = jnp.broadcast_to(ab_b_s[0].T[:, :, None], (chunk, heads, da // heads)).reshape(chunk, da)

    saved = {}

    def ffn_forward(xs, l, sub, k, *phases):
        saved[l, sub, "x"] = xs
        xs, gg, uu, yb = _run(
            lambda ph: _ffn_fwd(xs, vecs[l, sub], w_of(f"in{l}{k}"), w_of(f"out{l}{k}"), f"ffn_fwd_{l}{k}", ph), *phases
        )
        saved[l, sub, "act"] = (gg, uu, yb)
        return xs

    xs = ffn_forward(x0, 0, 0, 0, gather_ici("in01"))
    saved[0, 1, "x"] = xs
    (proj,) = _run(lambda ph: _proj_mod_fwd(xs, vecs[0, 1], w_of("abin"), ph), gather_sibling("in01"), gather_ici("out01"))
    (cat,) = _run(lambda ph: _ab_mix_fwd(proj, ab_norm_v, ab_w_s[0], b_rows, conv_full, ph), gather_sibling("out01"), gather_ici("in10"))
    xs, yb = _run(lambda ph: _proj_res_fwd(cat, w_of("about"), xs, vecs[0, 1], ph), gather_sibling("in10"), gather_ici("out10", "pool"))
    saved[0, 1, "act"] = (proj, cat, yb)
    xs = ffn_forward(xs, 0, 2, 1, gather_sibling("out10", "pool"), gather_ici("in11"))
    xs = ffn_forward(xs, 1, 0, 0, gather_sibling("in11"), gather_ici("out11"))
    saved[1, 1, "x"] = xs
    xs, pp, oo = _run(lambda ph: _pool_fwd(xs, vecs[1, 1], w_of("pool"), pool_scale_full, ph), gather_sibling("out11"))
    saved[1, 1, "act"] = (pp, oo)
    xs = ffn_forward(xs, 1, 2, 1)
    dxs, aux = _run(lambda ph: _loss_head(xs, final_g.reshape(1, d), target, ph))
    loss = lax.psum(0.5 * jnp.sum(aux[1]) / d, ("x", "y", "c"))

    grad = {}
    recv = {}
    csum = {}
    parts = {}
    reduced = {}
    done = set()
    dvecs, small_g = {}, {}

    def pair_exchange(*us):
        def then(outs):
            for u, o in zip(us, outs):
                recv[u] = o

        return _phase_pair_exchange([grad[u] for u in us], [big[u] for u in us], then)

    def pair_sum(u, *phases):
        def launch(ph):
            (csum[u],), p_outs = _pair_sum(grad[u], recv[u], big[u], where, "pair_sum_" + u, ph)
            return None, p_outs

        _run(launch, *phases)

    def chip_exchange(*us):
        def then(outs):
            for u, o in zip(us, outs):
                parts[u] = o

        return _phase_chip_exchange([csum[u] for u in us], [big[u] for u in us], then)

    def chip_sum(*us):
        for u in us:
            g, st, b0 = units[u]
            reduced[st] = _chip_sum(csum[u], parts[u], g, where, reduced.get(st), stacks[st][0].shape, b0, "chip_sum_" + u)

    def pair_broadcast(*us):
        sts = [units[u][1] for u in us]
        assert len(set(sts)) == len(sts)

        def then(outs):
            for u, st, o in zip(us, sts, outs):
                reduced[st] = o
                done.add(u)

        return _phase_pair_broadcast([reduced[st] for st in sts], [big[u] for u in us], [units[u][2] for u in us], then)

    def ffn_backward(dxs, l, sub, k, carried_bwd, carried_out, carried_g):
        gg, uu, yb = saved[l, sub, "act"]
        w_in, w_out = w_of(f"in{l}{k}"), w_of(f"out{l}{k}")
        dxs, dg, du, a, h, dy, dvecs[l, sub] = _run(
            lambda ph: _ffn_bwd(dxs, saved[l, sub, "x"], vecs[l, sub], gg, uu, yb, w_in, w_out, f"ffn_bwd_{l}{k}", ph), *carried_bwd()
        )
        uo, ui = f"out{l}{k}", f"in{l}{k}"
        (grad[uo],) = _run(lambda ph: _grad_matmul(a, dy, big[uo].f3, 0, None, f"dw_out_{l}{k}", ph), *carried_out())
        (gi,) = _run(lambda ph: _grad_matmul(h, dg, big[ui].f3, 0, None, f"dw_in_g_{l}{k}", ph), pair_exchange(uo), *carried_g())
        (grad[ui],) = _run(lambda ph: _grad_matmul(h, du, big[ui].f3, f_hidden, gi, f"dw_in_u_{l}{k}", ph))
        return dxs

    none = lambda: ()
    dxs = ffn_backward(dxs, 1, 2, 1, none, none, none)
    pp, oo = saved[1, 1, "act"]
    dxs, grad["pool"], small_g["pool_scale"], dvecs[1, 1] = _run(
        lambda ph: _pool_bwd(dxs, saved[1, 1, "x"], vecs[1, 1], pp, oo, w_of("pool"), pool_scale_full, ph)
    )
    pair_sum("out11", pair_exchange("in11", "pool"))
    pair_sum("in11")
    pair_sum("pool")

    def after_11():
        return (chip_exchange("in11", "out11", "pool"),)

    def bcast_11():
        chip_sum("in11", "out11", "pool")
        return (pair_broadcast("in11", "out11", "pool"),)

    dxs = ffn_backward(dxs, 1, 0, 0, after_11, bcast_11, none)
    pair_sum("out10", pair_exchange("in10"))
    pair_sum("in10")

    def after_10():
        return (chip_exchange("in10", "out10"),)

    def bcast_10():
        chip_sum("in10", "out10")
        return (pair_broadcast("in10", "out10"),)

    dxs = ffn_backward(dxs, 0, 2, 1, after_10, bcast_10, none)
    pair_sum("out01", pair_exchange("in01"))
    pair_sum("in01")

    proj, cat, yb = saved[0, 1, "act"]
    dy, dcat, dgate = _run(lambda ph: _proj_res_bwd(dxs, yb, vecs[0, 1], w_of("about"), ph))
    (grad["about"],) = _run(lambda ph: _grad_matmul(cat, dy, big["about"].f3, 0, None, "dw_ab_out", ph))
    dproj, small_g["ab_norm_v"], small_g["ab_w_s"], dzs, small_g["ab_conv_w"] = _run(
        lambda ph: _ab_mix_bwd(proj, dcat, ab_norm_v, ab_w_s[0], b_rows, conv_full, ph), chip_exchange("out01")
    )
    small_g["ab_b_s"] = dzs.reshape(chunk, heads, da // heads).sum(axis=2).T
    chip_sum("out01")
    dxs, h, dvecs[0, 1] = _run(
        lambda ph: _proj_mod_bwd(dproj, w_of("abin"), saved[0, 1, "x"], vecs[0, 1], dxs, dgate, ph),
        pair_exchange("about"), pair_broadcast("out01"),
    )
    (grad["abin"],) = _run(lambda ph: _grad_matmul(h, dproj, big["abin"].f3, 0, None, "dw_ab_in", ph))
    pair_sum("about", pair_exchange("abin"))
    pair_sum("abin")

    def after_01():
        return (chip_exchange("in01", "abin", "about"),)

    def bcast_01():
        chip_sum("in01", "abin", "about")
        return (pair_broadcast("in01", "abin", "about"),)

    dxs = ffn_backward(dxs, 0, 0, 0, after_01, bcast_01, none)
    grad_x = dxs.reshape(x.shape)
    pair_sum("out00", pair_exchange("in00"))
    pair_sum("in00")

    def adam_rows(st, b0, nb, prev, *phases):
        w3, m3, v3 = stacks[st]
        for u, (_, ust, ub0) in units.items():
            assert ust != st or not (b0 <= ub0 < b0 + nb) or u in done, u
        outs = _run(lambda ph: _adam_rows(w3, reduced[st], m3, v3, b0, nb, prev, f"adam_{st}_{b0}", ph), *phases)
        return tuple(outs)

    nb_late = 2 * n_layers - 1
    upd_in = adam_rows("w_ffn_in", 1, nb_late, None, chip_exchange("in00"))
    upd_out = adam_rows("w_ffn_out", 1, nb_late, None, chip_exchange("out00"))
    chip_sum("in00", "out00")
    upd = {
        "ab_w_in": adam_rows("ab_w_in", 0, 1, None, pair_broadcast("in00", "out00")),
        "ab_w_out": adam_rows("ab_w_out", 0, 1, None),
        "pool_w_grp": adam_rows("pool_w_grp", 0, pool_w_grp.shape[1], None),
        "w_ffn_in": adam_rows("w_ffn_in", 0, 1, upd_in),
        "w_ffn_out": adam_rows("w_ffn_out", 0, 1, upd_out),
    }
    out = {}
    for st, (dl, mo, vo) in upd.items():
        shape = {"w_ffn_in": w_ffn_in.shape, "w_ffn_out": w_ffn_out.shape, "pool_w_grp": pool_w_grp.shape}.get(st, stacks[st][0].shape)
        out[st] = tuple(a.reshape(shape) for a in (reduced[st], dl, mo, vo))

    dgain = jnp.stack([dvecs[l, sub][0] for l in range(n_layers) for sub in range(3)])
    dmod = jnp.concatenate([dvecs[l, sub][1:4] for l in range(n_layers) for sub in range(3)], axis=0)
    pieces = [
        dgain, dmod, aux[0:1], _pad_cols(small_g["ab_norm_v"], d), small_g["pool_scale"],
        _pad_cols(small_g["ab_conv_w"], d), _pad_cols(small_g["ab_b_s"], d),
    ]
    row0, layout_rows = 0, []
    for pc in pieces:
        layout_rows.append(row0)
        row0 += -(-pc.shape[0] // 8) * 8
    packed_g = jnp.concatenate([_pad_rows(pc) for pc in pieces], axis=0)
    g_all, gws_all = _all_gather_small([packed_g, small_g["ab_w_s"].reshape(heads * chunk, chunk)], "gather_small_grads")
    layout = {
        "norm_g": (layout_rows[0], 3 * n_layers, None, dq),
        "b_mod": (layout_rows[1], 9 * n_layers, 0, d),
        "final_g": (layout_rows[2], 1, 0, d),
        "ab_norm_v": (layout_rows[3], 1, 0, da),
        "pool_scale": (layout_rows[4], 1, None, dq),
        "ab_conv_w": (layout_rows[5], 3, None, db // N_CHIPS),
        "ab_b_s": (layout_rows[6], heads, 0, chunk),
    }
    shapes2d = {
        "norm_g": (3 * n_layers, dq), "b_mod": (9 * n_layers, d), "final_g": (1, d), "ab_norm_v": (1, da),
        "pool_scale": (1, dq), "ab_conv_w": (3, db // N_CHIPS), "ab_b_s": (heads, chunk), "ab_w_s": (heads * chunk, chunk),
    }
    small_w = {"norm_g": (norm_g, m_norm_g, v_norm_g), "b_mod": (b_mod, m_b_mod, v_b_mod), "final_g": (final_g, m_final_g, v_final_g),
               "ab_norm_v": (ab_norm_v, m_ab_norm_v, v_ab_norm_v), "pool_scale": (pool_scale, m_pool_scale, v_pool_scale),
               "ab_conv_w": (ab_conv_w, m_ab_conv_w, v_ab_conv_w), "ab_b_s": (ab_b_s, m_ab_b_s, v_ab_b_s), "ab_w_s": (ab_w_s, m_ab_w_s, v_ab_w_s)}
    smalls = {nm: tuple(a.reshape(shapes2d[nm]) for a in wmv) for nm, wmv in small_w.items()}
    small_out = _small_adam(g_all, gws_all, layout, smalls, chip)
    for nm, res in small_out.items():
        out[nm] = tuple(a.reshape(small_w[nm][0].shape) for a in res)

    dmod_all = g_all[:, layout_rows[1] : layout_rows[1] + 9 * n_layers, :].reshape(N_DEV, n_layers, 9 * d)
    dmod_cols = lax.dynamic_slice(dmod_all, (0, 0, chip * ncol), (N_DEV, n_layers, ncol)).transpose(1, 0, 2)
    out["w_mod"] = tuple(_run(lambda ph: _mod_bwd_adam(c_all.T, dmod_cols, w_mod, m_w_mod, v_w_mod, ph)))

    order = ["norm_g", "w_mod", "b_mod", "w_ffn_in", "w_ffn_out", "ab_w_in", "ab_norm_v", "ab_w_s", "ab_b_s", "ab_conv_w", "ab_w_out", "pool_w_grp", "pool_scale", "final_g"]
    return (loss, grad_x, *[out[nm][0] for nm in order], *[out[nm][1] for nm in order], *[out[nm][2] for nm in order], *[out[nm][3] for nm in order])
```

```python
import functools
import math

import jax
import jax.numpy as jnp
from jax import lax
from jax.experimental import pallas as pl
from jax.experimental.pallas import tpu as pltpu

F32 = jnp.float32
BF16 = jnp.bfloat16
MESH = pl.DeviceIdType.MESH

EPS = 1e-6
ADAM_LR = 0.001
ADAM_B1 = 0.9
ADAM_B2 = 0.999
ADAM_EPS = 1e-08
ADAM_WD = 0.01
ADAM_STEP = 10
POOL_WINDOWS = (2, 4, 8, 16)
POOL_HALO = 16
CONV_HALO = 8
N_CHIPS = 4
N_DEV = 8
VMEM_LIMIT_BYTES = 48 * 1024 * 1024
EW_BLOCK_ELEMS = 256 * 1024


def _pick(n, prefs):
    for p in prefs:
        if p <= n and n % p == 0:
            return p
    return n


def _row_tile(rows, cols):
    best = None
    for d in range(16, rows + 1, 16):
        if rows % d == 0 and d * cols <= EW_BLOCK_ELEMS:
            best = d
    return best or rows


def _dot(a, b):
    return jnp.dot(a, b, preferred_element_type=F32)


def _dot_nt(a, b):
    return lax.dot_general(a, b, (((1,), (1,)), ((), ())), preferred_element_type=F32)


def _dot_tn(a, b):
    return lax.dot_general(a, b, (((0,), (0,)), ((), ())), preferred_element_type=F32)


def _sigmoid(x):
    return 1.0 / (1.0 + jnp.exp(-x))


_GELU_C = math.sqrt(2.0 / math.pi)


def _gelu(x):
    x2 = x * x
    t = jnp.tanh(_GELU_C * (x + 0.044715 * x2 * x))
    val = 0.5 * x * (1.0 + t)
    grad = 0.5 * (1.0 + t) + 0.5 * x * (1.0 - t * t) * (_GELU_C * (1.0 + 3.0 * 0.044715 * x2))
    return val, grad


def _rstd(x):
    return lax.rsqrt(jnp.mean(x * x, axis=-1, keepdims=True) + EPS)


def _modulate(x, vec_ref):
    return (x * _rstd(x)) * vec_ref[0:1, :] * (1.0 + vec_ref[2:3, :]) + vec_ref[1:2, :]


def _modulate_bwd(x, dh, vec_ref, dvec_ref):
    gn, sh, sc = vec_ref[0:1, :], vec_ref[1:2, :], vec_ref[2:3, :]
    rstd = _rstd(x)
    r = x * rstd
    dvec_ref[0:1, :] += jnp.sum(dh * r * (1.0 + sc), axis=0, keepdims=True)
    dvec_ref[1:2, :] += jnp.sum(dh, axis=0, keepdims=True)
    dvec_ref[2:3, :] += jnp.sum(dh * r * gn, axis=0, keepdims=True)
    gm = gn * (1.0 + sc)
    dr = dh * gm
    dx = rstd * (dr - r * jnp.mean(dr * r, axis=-1, keepdims=True))
    return dx, r * gm + sh


def _adam(w, g, m, v):
    m = ADAM_B1 * m + (1.0 - ADAM_B1) * g
    v = ADAM_B2 * v + (1.0 - ADAM_B2) * (g * g)
    m_hat = m / (1.0 - ADAM_B1**ADAM_STEP)
    v_hat = v / (1.0 - ADAM_B2**ADAM_STEP)
    delta = -ADAM_LR * (m_hat / (jnp.sqrt(v_hat) + ADAM_EPS) + ADAM_WD * w)
    return delta, m, v


_ANY = pl.BlockSpec(memory_space=pl.ANY)


class _Phase:
    def __init__(self, ins, out_shapes, aliases, n_sems, start, finish, then):
        self.ins, self.out_shapes, self.aliases, self.n_sems = list(ins), list(out_shapes), dict(aliases), n_sems
        self.start, self.finish, self.then = start, finish, then


def _call(body, name, grid, in_specs, out_specs, out_shape, ins, scratch=(), prefetch=(), phases=(), in_place=None):
    n_pre, n_in, n_out, n_sc = len(prefetch), len(in_specs), len(out_specs), len(scratch)
    ph_in = [len(p.ins) for p in phases]
    ph_out = [len(p.out_shapes) for p in phases]

    def kernel_body(*refs):
        pos = [0]

        def take(k):
            pos[0] += k
            return refs[pos[0] - k : pos[0]]

        pre, ins_ = take(n_pre), take(n_in)
        p_ins = [take(k) for k in ph_in]
        outs_ = take(n_out)
        p_outs = [take(k) for k in ph_out]
        sc = take(n_sc)
        sems = [take(2) for _ in phases]
        if phases:
            ids = [pl.program_id(a) for a in range(len(grid))]
            first = functools.reduce(jnp.logical_and, [i == 0 for i in ids])
            last = functools.reduce(jnp.logical_and, [i == g - 1 for i, g in zip(ids, grid)])

            @pl.when(first)
            def _():
                for p, pi, po, (send, recv) in zip(phases, p_ins, p_outs, sems):
                    p.start(pi, po, send, recv)

        if body is not None:
            body(*pre, *ins_, *outs_, *sc)
        if phases:

            @pl.when(last)
            def _():
                for p, pi, po, (send, recv) in zip(phases, p_ins, p_outs, sems):
                    p.finish(pi, po, send, recv)

    aliases = {n_pre + i: o for i, o in (in_place or {}).items()}
    i0, o0 = n_pre + n_in, n_out
    for p in phases:
        for i, o in p.aliases.items():
            aliases[i0 + i] = o0 + o
        i0 += len(p.ins)
        o0 += len(p.out_shapes)
    all_in = list(in_specs) + [_ANY] * sum(ph_in)
    all_out = list(out_specs) + [_ANY] * sum(ph_out)
    all_scratch = list(scratch)
    for p in phases:
        all_scratch += [pltpu.SemaphoreType.DMA((p.n_sems,)), pltpu.SemaphoreType.DMA((p.n_sems,))]
    shapes = list(out_shape) + [s for p in phases for s in p.out_shapes]
    operands = list(prefetch) + list(ins) + [a for p in phases for a in p.ins]
    sem = ("arbitrary",) * len(grid)
    params = pltpu.CompilerParams(dimension_semantics=sem, vmem_limit_bytes=VMEM_LIMIT_BYTES)
    if n_pre:
        res = pl.pallas_call(
            kernel_body, name=name, out_shape=shapes, input_output_aliases=aliases, compiler_params=params,
            grid_spec=pltpu.PrefetchScalarGridSpec(
                num_scalar_prefetch=n_pre, grid=grid, in_specs=all_in, out_specs=all_out, scratch_shapes=all_scratch
            ),
        )(*operands)
    else:
        res = pl.pallas_call(
            kernel_body, name=name, grid=grid, in_specs=all_in, out_specs=all_out, out_shape=shapes,
            scratch_shapes=all_scratch, input_output_aliases=aliases, compiler_params=params,
        )(*operands)
    res = list(res)
    outs, rest = res[:n_out], res[n_out:]
    p_res = []
    for k in ph_out:
        p_res.append(rest[:k])
        rest = rest[k:]
    return outs, p_res


def _place():
    return lax.axis_index("x"), lax.axis_index("y"), lax.axis_index("c")


def _other_chips():
    x, y, _ = _place()
    return [(1 - x, y), (x, 1 - y), (1 - x, 1 - y)]


def _flip(k):
    x, y, c = _place()
    return (1 - x if k & 4 else x, 1 - y if k & 2 else y, 1 - c if k & 1 else c)


def _remote(src, dst, send, recv, k, to):
    return pltpu.make_async_remote_copy(
        src_ref=src, dst_ref=dst, send_sem=send.at[k], recv_sem=recv.at[k], device_id=to, device_id_type=MESH
    )


def _phase_small_gather(arrs, then):
    n = len(arrs)

    def copies(ins, outs, send, recv):
        x, y, c = _place()
        me = 4 * x + 2 * y + c
        local = [pltpu.make_async_copy(ins[a], outs[a].at[me], send.at[a * N_DEV]) for a in range(n)]
        remote = [_remote(ins[a], outs[a].at[me], send, recv, a * N_DEV + k, _flip(k)) for a in range(n) for k in range(1, N_DEV)]
        return local, remote

    def start(ins, outs, send, recv):
        local, remote = copies(ins, outs, send, recv)
        for cp in local + remote:
            cp.start()

    def finish(ins, outs, send, recv):
        local, remote = copies(ins, outs, send, recv)
        for cp in remote + local:
            cp.wait()

    shapes = [jax.ShapeDtypeStruct((N_DEV,) + a.shape, a.dtype) for a in arrs]
    return _Phase(arrs, shapes, {}, n * N_DEV, start, finish, then)


def _flush(name, *phases):
    _, p_outs = _call(None, name, (1,), [], [], [], [], phases=list(phases))
    for p, po in zip(phases, p_outs):
        p.then(po)


class _Big:
    KINDS = {"full": (True, True), "half": (True, False), "shard": (False, True), "block": (False, False)}

    def __init__(self, f3, s3, h3):
        assert s3 != h3
        self.f3, self.s3, self.h3 = tuple(f3), s3, h3
        self.bd = tuple(f3[a] // (N_CHIPS if a == s3 else 1) // (2 if a == h3 else 1) for a in range(3))
        self.tile = (1, _row_tile(self.bd[1], self.bd[2]), self.bd[2])
        self.grid = tuple(self.bd[a] // self.tile[a] for a in range(3))

    def dims(self, kind):
        chips, halves = self.KINDS[kind]
        return tuple(
            self.bd[a] * (N_CHIPS if chips and a == self.s3 else 1) * (2 if halves and a == self.h3 else 1) for a in range(3)
        )

    def view(self, ref, chip=None, half=None, batch0=0, both_halves=True):
        start = [batch0, 0, 0]
        size = list(ref.shape)
        size[0] = self.bd[0] * (2 if self.h3 == 0 and both_halves else 1)
        if chip is not None:
            start[self.s3] += chip * self.bd[self.s3]
            size[self.s3] = self.bd[self.s3]
        if half is not None:
            start[self.h3] += half * self.bd[self.h3]
            size[self.h3] = self.bd[self.h3]
        return ref.at[tuple(pl.ds(st, sz) for st, sz in zip(start, size))]

    def spec(self, chip_from=None, half_from=None, lead=(), batch0=0):
        extra = "grid" in (chip_from, half_from)

        def index(*args):
            pref, idx = args[-1], list(args[int(extra) : -1])
            idx[0] += batch0
            if chip_from:
                idx[self.s3] += (pref[0] if chip_from == "pref" else args[0]) * self.grid[self.s3]
            if half_from:
                idx[self.h3] += (pref[1] if half_from == "pref" else args[0]) * self.grid[self.h3]
            return (0,) * len(lead) + tuple(idx)

        return pl.BlockSpec(tuple(lead) + self.tile, index)


def _same(arrs):
    return [jax.ShapeDtypeStruct(a.shape, a.dtype) for a in arrs]


def _phase_gather_ici(arrs, bigs, then):
    n = len(arrs)

    def copies(outs, send, recv, arriving):
        x, y, c = _place()
        return [
            _remote(blk, blk, send, recv, 3 * a + j, (*chip, c))
            for j, chip in enumerate(_other_chips())
            for a in range(n)
            for blk in [bigs[a].view(outs[a], 2 * chip[0] + chip[1] if arriving else 2 * x + y, c)]
        ]

    def start(ins, outs, send, recv):
        for cp in copies(outs, send, recv, False):
            cp.start()

    def finish(ins, outs, send, recv):
        for cp in copies(outs, send, recv, True):
            cp.wait_recv()
        for cp in copies(outs, send, recv, False):
            cp.wait_send()

    return _Phase(arrs, _same(arrs), {a: a for a in range(n)}, 3 * n, start, finish, then)


def _phase_gather_sibling(arrs, bigs, then):
    n = len(arrs)

    def copies(outs, send, recv, arriving):
        x, y, c = _place()
        return [
            _remote(blk, blk, send, recv, 3 * a + j, (x, y, 1 - c))
            for j, chip in enumerate(_other_chips())
            for a in range(n)
            for blk in [bigs[a].view(outs[a], 2 * chip[0] + chip[1], 1 - c if arriving else c)]
        ]

    def start(ins, outs, send, recv):
        for cp in copies(outs, send, recv, False):
            cp.start()

    def finish(ins, outs, send, recv):
        for cp in copies(outs, send, recv, True):
            cp.wait_recv()
        for cp in copies(outs, send, recv, False):
            cp.wait_send()

    return _Phase(arrs, _same(arrs), {a: a for a in range(n)}, 3 * n, start, finish, then)


def _phase_pair_exchange(grads, bigs, then):
    n = len(grads)

    def copies(ins, outs, send, recv):
        x, y, c = _place()
        return [_remote(bigs[a].view(ins[a], None, 1 - c), outs[a], send, recv, a, (x, y, 1 - c)) for a in range(n)]

    def start(ins, outs, send, recv):
        for cp in copies(ins, outs, send, recv):
            cp.start()

    def finish(ins, outs, send, recv):
        for cp in copies(ins, outs, send, recv):
            cp.wait()

    shapes = [jax.ShapeDtypeStruct(b.dims("half"), BF16) for b in bigs]
    return _Phase(grads, shapes, {}, n, start, finish, then)


def _phase_chip_exchange(sums, bigs, then):
    n = len(sums)

    def copies(ins, outs, send, recv):
        _, _, c = _place()
        return [
            _remote(bigs[a].view(ins[a], 2 * chip[0] + chip[1], both_halves=False), outs[a].at[j], send, recv, 3 * a + j, (*chip, c))
            for j, chip in enumerate(_other_chips())
            for a in range(n)
        ]

    def start(ins, outs, send, recv):
        for cp in copies(ins, outs, send, recv):
            cp.start()

    def finish(ins, outs, send, recv):
        for cp in copies(ins, outs, send, recv):
            cp.wait()

    shapes = [jax.ShapeDtypeStruct((N_CHIPS - 1,) + b.dims("block"), BF16) for b in bigs]
    return _Phase(sums, shapes, {}, 3 * n, start, finish, then)


def _phase_pair_broadcast(stacks, bigs, batch0s, then):
    n = len(stacks)

    def start(ins, outs, send, recv):
        x, y, c = _place()
        for a in range(n):
            blk = bigs[a].view(outs[a], None, c, batch0s[a])
            _remote(blk, blk, send, recv, a, (x, y, 1 - c)).start()

    def finish(ins, outs, send, recv):
        x, y, c = _place()
        for a in range(n):
            mine = bigs[a].view(outs[a], None, c, batch0s[a])
            theirs = bigs[a].view(outs[a], None, 1 - c, batch0s[a])
            _remote(mine, mine, send, recv, a, (x, y, 1 - c)).wait_send()
            _remote(theirs, theirs, send, recv, a, (x, y, 1 - c)).wait_recv()

    return _Phase(stacks, _same(stacks), {a: a for a in range(n)}, n, start, finish, then)


def _tile_call(body, name, big, where, extra, ins, in_specs, out_specs, out_shape, phases=()):
    grid = ((extra,) if extra else ()) + big.grid
    return _call(body, name, grid, in_specs, out_specs, out_shape, ins, prefetch=(where,), phases=phases)


def _cast_into_full(w_stack, batch0, big, where, name, phases=()):
    def body(_, w_ref, o_ref):
        o_ref[...] = w_ref[...].astype(BF16)

    return _tile_call(
        body, name, big, where, 2, [w_stack], [big.spec(None, "grid", batch0=batch0)], [big.spec("pref", "grid")],
        [jax.ShapeDtypeStruct(big.dims("full"), BF16)], phases,
    )


def _pair_sum(g_full, recv_half, big, where, name, phases=()):
    def body(_, g_ref, r_ref, o_ref):
        o_ref[...] = (g_ref[...].astype(F32) + r_ref[...].astype(F32)).astype(BF16)

    half = big.spec("grid", None)
    return _tile_call(
        body, name, big, where, N_CHIPS, [g_full, recv_half], [big.spec("grid", "pref"), half], [half],
        [jax.ShapeDtypeStruct(big.dims("half"), BF16)], phases,
    )


def _chip_sum(chip_sum, parts, big, where, stack, stack_shape, batch0, name, phases=()):
    def body(_, own_ref, p_ref, *rest):
        acc = own_ref[...].astype(F32)
        for k in range(N_CHIPS - 1):
            acc = acc + p_ref[k].astype(F32)
        rest[-1][...] = acc

    ins = [chip_sum, parts] + ([stack] if stack is not None else [])
    in_specs = [big.spec("pref", None), big.spec(None, None, lead=(N_CHIPS - 1,))] + ([_ANY] if stack is not None else [])
    return _call(
        body, name, big.grid, in_specs, [big.spec(None, "pref", batch0=batch0)], [jax.ShapeDtypeStruct(stack_shape, F32)], ins,
        prefetch=(where,), phases=phases, in_place={2: 0} if stack is not None else None,
    )


def _adam_rows(w, g, m, v, batch0, nb, prev, name):
    b, r, c = w.shape
    tr = _row_tile(r, c)

    def body(w_ref, g_ref, m_ref, v_ref, *rest):
        go_ref, d_ref, mo_ref, vo_ref = rest[-4:]
        gv = g_ref[...]
        d, mo, vo = _adam(w_ref[...], gv, m_ref[...], v_ref[...])
        go_ref[...] = gv
        d_ref[...] = d
        mo_ref[...] = mo
        vo_ref[...] = vo

    spec = pl.BlockSpec((1, tr, c), lambda bb, i: (batch0 + bb, i, 0))
    ins = [w, g, m, v] + (list(prev) if prev is not None else [])
    return pl.pallas_call(
        body, name=name, grid=(nb, r // tr), in_specs=[spec] * 4 + ([_ANY] * 4 if prev is not None else []), out_specs=[spec] * 4,
        out_shape=[jax.ShapeDtypeStruct(w.shape, F32)] * 4,
        input_output_aliases={4: 0, 5: 1, 6: 2, 7: 3} if prev is not None else {},
        compiler_params=pltpu.CompilerParams(dimension_semantics=("arbitrary",) * 2, vmem_limit_bytes=VMEM_LIMIT_BYTES),
    )(*ins)


def _mod_fwd(c_all, w_mod, b_cols, phases=()):
    n_layers, d, n = w_mod.shape
    tn = _pick(n, (768, 512, 384, 256, 128))

    def body(c_ref, w_ref, b_ref, o_ref):
        cv = c_ref[...]
        ca = (cv * _sigmoid(cv)).astype(BF16)
        o_ref[0] = _dot(ca, w_ref[0].astype(BF16)) + b_ref[0]

    return _call(
        body, "mod_fwd", (n_layers, n // tn),
        [
            pl.BlockSpec((N_DEV, d), lambda l, j: (0, 0)),
            pl.BlockSpec((1, d, tn), lambda l, j: (l, 0, j)),
            pl.BlockSpec((1, 1, tn), lambda l, j: (l, 0, j)),
        ],
        [pl.BlockSpec((1, N_DEV, tn), lambda l, j: (l, 0, j))],
        [jax.ShapeDtypeStruct((n_layers, N_DEV, n), F32)], [c_all, w_mod, b_cols], phases=phases,
    )


def _mod_bwd_adam(c_all_t, dmod_cols, w, m, v, phases=()):
    n_layers, d, n = w.shape
    tn = _pick(n, (384, 256, 128))

    def body(c_ref, dm_ref, w_ref, m_ref, v_ref, g_ref, d_ref, mo_ref, vo_ref):
        cv = c_ref[...]
        ca = (cv * _sigmoid(cv)).astype(BF16)
        g = _dot(ca, dm_ref[0].astype(BF16))
        g_ref[0] = g
        dl, mo, vo = _adam(w_ref[0], g, m_ref[0], v_ref[0])
        d_ref[0] = dl
        mo_ref[0] = mo
        vo_ref[0] = vo

    wspec = pl.BlockSpec((1, d, tn), lambda l, j: (l, 0, j))
    return _call(
        body, "mod_bwd_adam", (n_layers, n // tn),
        [pl.BlockSpec((d, N_DEV), lambda l, j: (0, 0)), pl.BlockSpec((1, N_DEV, tn), lambda l, j: (l, 0, j)), wspec, wspec, wspec],
        [wspec] * 4, [jax.ShapeDtypeStruct(w.shape, F32)] * 4, [c_all_t, dmod_cols, w, m, v], phases=phases,
    )


def _ffn_fwd(x, vec, w_in, w_out, name, phases=()):
    s, d = x.shape
    f = w_out.shape[1]
    tm = _pick(s, (1024, 512, 256, 128))
    tf = _pick(f, (256, 128))
    nf = f // tf

    def body(x_ref, vec_ref, wg_ref, wu_ref, wo_ref, xo_ref, g_ref, u_ref, y_ref, h_sc, acc_sc):
        j = pl.program_id(1)

        @pl.when(j == 0)
        def _():
            h_sc[...] = _modulate(x_ref[...], vec_ref).astype(BF16)
            acc_sc[...] = jnp.zeros_like(acc_sc)

        h = h_sc[...]
        g = _dot(h, wg_ref[0])
        u = _dot(h, wu_ref[0])
        g_ref[...] = g.astype(BF16)
        u_ref[...] = u.astype(BF16)
        a = (g * _sigmoid(g) * u).astype(BF16)
        acc_sc[...] += _dot(a, wo_ref[0])

        @pl.when(j == nf - 1)
        def _():
            yv = acc_sc[...]
            xo_ref[...] = x_ref[...] + 0.5 * vec_ref[3:4, :] * yv
            y_ref[...] = yv.astype(BF16)

    row = pl.BlockSpec((tm, d), lambda i, j: (i, 0))
    hid = pl.BlockSpec((tm, tf), lambda i, j: (i, j))
    return _call(
        body, name, (s // tm, nf),
        [
            row,
            pl.BlockSpec((8, d), lambda i, j: (0, 0)),
            pl.BlockSpec((1, d, tf), lambda i, j: (0, 0, j)),
            pl.BlockSpec((1, d, tf), lambda i, j: (0, 0, nf + j)),
            pl.BlockSpec((1, tf, d), lambda i, j: (0, j, 0)),
        ],
        [row, hid, hid, row],
        [
            jax.ShapeDtypeStruct((s, d), F32),
            jax.ShapeDtypeStruct((s, f), BF16),
            jax.ShapeDtypeStruct((s, f), BF16),
            jax.ShapeDtypeStruct((s, d), BF16),
        ],
        [x, vec, w_in, w_in, w_out],
        scratch=[pltpu.VMEM((tm, d), BF16), pltpu.VMEM((tm, d), F32)], phases=phases,
    )


def _ffn_bwd(dxo, x, vec, gg, uu, y, w_in, w_out, name, phases=()):
    s, d = x.shape
    f = w_out.shape[1]
    tm = _pick(s, (512, 256, 128))
    tf = _pick(f, (256, 128))
    nf = f // tf

    def body(dxo_ref, x_ref, vec_ref, g_ref, u_ref, y_ref, wg_ref, wu_ref, wo_ref,
             dx_ref, dg_ref, du_ref, a_ref, h_ref, dy_ref, dvec_ref, acc_sc):
        i, j = pl.program_id(0), pl.program_id(1)

        @pl.when((i == 0) & (j == 0))
        def _():
            dvec_ref[...] = jnp.zeros_like(dvec_ref)

        @pl.when(j == 0)
        def _():
            dxo_v = dxo_ref[...]
            dy_ref[...] = (0.5 * vec_ref[3:4, :] * dxo_v).astype(BF16)
            dvec_ref[3:4, :] += 0.5 * jnp.sum(dxo_v * y_ref[...].astype(F32), axis=0, keepdims=True)
            acc_sc[...] = jnp.zeros_like(acc_sc)

        da = _dot_nt(dy_ref[...], wo_ref[0])
        g = g_ref[...].astype(F32)
        u = u_ref[...].astype(F32)
        sig = _sigmoid(g)
        sl = g * sig
        a_ref[...] = (sl * u).astype(BF16)
        dg = (da * u * (sig * (1.0 + g * (1.0 - sig)))).astype(BF16)
        du = (da * sl).astype(BF16)
        dg_ref[...] = dg
        du_ref[...] = du
        acc_sc[...] += _dot_nt(dg, wg_ref[0]) + _dot_nt(du, wu_ref[0])

        @pl.when(j == nf - 1)
        def _():
            dx, h = _modulate_bwd(x_ref[...], acc_sc[...], vec_ref, dvec_ref)
            dx_ref[...] = dxo_ref[...] + dx
            h_ref[...] = h.astype(BF16)

    row = pl.BlockSpec((tm, d), lambda i, j: (i, 0))
    hid = pl.BlockSpec((tm, tf), lambda i, j: (i, j))
    vecs = pl.BlockSpec((8, d), lambda i, j: (0, 0))
    return _call(
        body, name, (s // tm, nf),
        [
            row, row, vecs, hid, hid, row,
            pl.BlockSpec((1, d, tf), lambda i, j: (0, 0, j)),
            pl.BlockSpec((1, d, tf), lambda i, j: (0, 0, nf + j)),
            pl.BlockSpec((1, tf, d), lambda i, j: (0, j, 0)),
        ],
        [row, hid, hid, hid, row, row, vecs],
        [
            jax.ShapeDtypeStruct((s, d), F32),
            jax.ShapeDtypeStruct((s, f), BF16),
            jax.ShapeDtypeStruct((s, f), BF16),
            jax.ShapeDtypeStruct((s, f), BF16),
            jax.ShapeDtypeStruct((s, d), BF16),
            jax.ShapeDtypeStruct((s, d), BF16),
            jax.ShapeDtypeStruct((8, d), F32),
        ],
        [dxo, x, vec, gg, uu, y, w_in, w_in, w_out],
        scratch=[pltpu.VMEM((tm, d), F32)], phases=phases,
    )


def _grad_matmul(a, b, full_shape, col0, prev, name, phases=()):
    s, k1 = a.shape
    n = b.shape[1]
    tk = _pick(k1, (512, 256, 128))
    tn = _pick(n, (1408, 1024, 640, 512, 256, 128))
    assert col0 % tn == 0

    def body(a_ref, b_ref, *rest):
        rest[-1][0] = _dot_tn(a_ref[...], b_ref[...]).astype(BF16)

    in_specs = [pl.BlockSpec((s, tk), lambda i, j: (0, i)), pl.BlockSpec((s, tn), lambda i, j: (0, j))]
    out_spec = pl.BlockSpec((1, tk, tn), lambda i, j: (0, i, col0 // tn + j))
    shape = jax.ShapeDtypeStruct(full_shape, BF16)
    if prev is not None:
        return _call(body, name, (k1 // tk, n // tn), in_specs + [_ANY], [out_spec], [shape], [a, b, prev], phases=phases, in_place={2: 0})
    return _call(body, name, (k1 // tk, n // tn), in_specs, [out_spec], [shape], [a, b], phases=phases)


def _proj_mod_fwd(x, vec, w, phases=()):
    s, d = x.shape
    n = w.shape[2]
    tm = _pick(s, (512, 256, 128))
    tn = _pick(n, (640, 512, 256, 128))

    def body(x_ref, vec_ref, w_ref, o_ref, h_sc):
        @pl.when(pl.program_id(1) == 0)
        def _():
            h_sc[...] = _modulate(x_ref[...], vec_ref).astype(BF16)

        o_ref[...] = _dot(h_sc[...], w_ref[0])

    return _call(
        body, "ab_in_fwd", (s // tm, n // tn),
        [
            pl.BlockSpec((tm, d), lambda i, j: (i, 0)),
            pl.BlockSpec((8, d), lambda i, j: (0, 0)),
            pl.BlockSpec((1, d, tn), lambda i, j: (0, 0, j)),
        ],
        [pl.BlockSpec((tm, tn), lambda i, j: (i, j))],
        [jax.ShapeDtypeStruct((s, n), F32)], [x, vec, w],
        scratch=[pltpu.VMEM((tm, d), BF16)], phases=phases,
    )


def _proj_res_fwd(a, w, x, vec, phases=()):
    s, kd = a.shape
    d = x.shape[1]
    tm = _pick(s, (512, 256, 128))

    def body(a_ref, w_ref, x_ref, vec_ref, xo_ref, y_ref):
        yv = _dot(a_ref[...], w_ref[0])
        xo_ref[...] = x_ref[...] + vec_ref[3:4, :] * yv
        y_ref[...] = yv.astype(BF16)

    row = pl.BlockSpec((tm, d), lambda i: (i, 0))
    return _call(
        body, "ab_out_fwd", (s // tm,),
        [pl.BlockSpec((tm, kd), lambda i: (i, 0)), pl.BlockSpec((1, kd, d), lambda i: (0, 0, 0)), row, pl.BlockSpec((8, d), lambda i: (0, 0))],
        [row, row],
        [jax.ShapeDtypeStruct((s, d), F32), jax.ShapeDtypeStruct((s, d), BF16)], [a, w, x, vec], phases=phases,
    )


def _proj_res_bwd(dxo, y, vec, w, phases=()):
    s, d = dxo.shape
    kd = w.shape[1]
    tm = _pick(s, (512, 256, 128))

    def body(dxo_ref, y_ref, vec_ref, w_ref, dy_ref, da_ref, dgate_ref):
        @pl.when(pl.program_id(0) == 0)
        def _():
            dgate_ref[...] = jnp.zeros_like(dgate_ref)

        dxo_v = dxo_ref[...]
        dy = (vec_ref[3:4, :] * dxo_v).astype(BF16)
        dy_ref[...] = dy
        dgate_ref[3:4, :] += jnp.sum(dxo_v * y_ref[...].astype(F32), axis=0, keepdims=True)
        da_ref[...] = _dot_nt(dy, w_ref[0]).astype(BF16)

    row = pl.BlockSpec((tm, d), lambda i: (i, 0))
    vecs = pl.BlockSpec((8, d), lambda i: (0, 0))
    return _call(
        body, "ab_out_bwd", (s // tm,),
        [row, row, vecs, pl.BlockSpec((1, kd, d), lambda i: (0, 0, 0))],
        [row, pl.BlockSpec((tm, kd), lambda i: (i, 0)), vecs],
        [jax.ShapeDtypeStruct((s, d), BF16), jax.ShapeDtypeStruct((s, kd), BF16), jax.ShapeDtypeStruct((8, d), F32)],
        [dxo, y, vec, w], phases=phases,
    )


def _proj_mod_bwd(dproj, w, x, vec, dxo, dvec_in, phases=()):
    s, n = dproj.shape
    d = x.shape[1]
    tm = _pick(s, (512, 256, 128))

    def body(dp_ref, w_ref, x_ref, vec_ref, dxo_ref, dvi_ref, dx_ref, h_ref, dvec_ref):
        @pl.when(pl.program_id(0) == 0)
        def _():
            dvec_ref[...] = dvi_ref[...]

        dh = _dot_nt(dp_ref[...], w_ref[0])
        dx, h = _modulate_bwd(x_ref[...], dh, vec_ref, dvec_ref)
        dx_ref[...] = dxo_ref[...] + dx
        h_ref[...] = h.astype(BF16)

    row = pl.BlockSpec((tm, d), lambda i: (i, 0))
    vecs = pl.BlockSpec((8, d), lambda i: (0, 0))
    return _call(
        body, "ab_in_bwd", (s // tm,),
        [pl.BlockSpec((tm, n), lambda i: (i, 0)), pl.BlockSpec((1, d, n), lambda i: (0, 0, 0)), row, vecs, row, vecs],
        [row, row, vecs],
        [jax.ShapeDtypeStruct((s, d), F32), jax.ShapeDtypeStruct((s, d), BF16), jax.ShapeDtypeStruct((8, d), F32)],
        [dproj, w, x, vec, dxo, dvec_in], phases=phases,
    )


def _tril(n):
    return lax.broadcasted_iota(jnp.int32, (n, n), 0) >= lax.broadcasted_iota(jnp.int32, (n, n), 1)


def _layernorm_stats(gv):
    mu = jnp.mean(gv, axis=-1, keepdims=True)
    cen = gv - mu
    rstd = lax.rsqrt(jnp.mean(cen * cen, axis=-1, keepdims=True) + EPS)
    return cen * rstd, rstd


def _shift_down(q, k, above_ref, c_cg, c_xb, first):
    width = q.shape[1]
    rows = lax.broadcasted_iota(jnp.int32, q.shape, 0)
    out = pltpu.roll(q, k, 0)
    for r in range(k):
        src = CONV_HALO - k + r
        above = above_ref[src : src + 1, c_cg : c_cg + width] * above_ref[src : src + 1, c_xb : c_xb + width]
        above = jnp.where(first, 0.0, above)
        out = jnp.where(rows == r, above, out)
    return out


def _ab_mix_fwd(proj, norm_v, w_s, b_rows, conv_w, phases=()):
    s, n = proj.shape
    heads, chunk, _ = w_s.shape
    da = norm_v.shape[1]
    hd = da // heads
    db = conv_w.shape[1]
    tm = _pick(s, (512, 256, 128))

    def body(p_ref, ph_ref, nv_ref, ws_ref, b_ref, cw_ref, o_ref):
        first = pl.program_id(0) == 0
        gu, _ = _gelu(p_ref[:, 0:da])
        gv, _ = _gelu(p_ref[:, da : 2 * da])
        xhat, _ = _layernorm_stats(gv)
        vn = (xhat * nv_ref[...]).astype(BF16)
        mask = _tril(chunk)
        for hh in range(heads):
            wm = jnp.where(mask, ws_ref[hh], 0.0).astype(BF16)
            cols = slice(hh * hd, (hh + 1) * hd)
            for nn in range(tm // chunk):
                rows = slice(nn * chunk, (nn + 1) * chunk)
                z = _dot(wm, vn[rows, cols]) + b_ref[:, cols]
                o_ref[rows, cols] = (gu[rows, cols] * z).astype(BF16)
        c_cg, c_xb = 2 * da + db, 2 * da + 2 * db
        bg = p_ref[:, 2 * da : 2 * da + db]
        q = p_ref[:, c_cg : c_cg + db] * p_ref[:, c_xb : c_xb + db]
        q1 = _shift_down(q, 1, ph_ref, c_cg, c_xb, first)
        q2 = _shift_down(q, 2, ph_ref, c_cg, c_xb, first)
        conv = cw_ref[0:1, :] * q2 + cw_ref[1:2, :] * q1 + cw_ref[2:3, :] * q
        o_ref[:, da : da + db] = (bg * conv).astype(BF16)

    nh = tm // CONV_HALO
    return _call(
        body, "ab_mix_fwd", (s // tm,),
        [
            pl.BlockSpec((tm, n), lambda i: (i, 0)),
            pl.BlockSpec((CONV_HALO, n), lambda i: (jnp.maximum(i * nh - 1, 0), 0)),
            pl.BlockSpec((1, da), lambda i: (0, 0)),
            pl.BlockSpec((heads, chunk, chunk), lambda i: (0, 0, 0)),
            pl.BlockSpec((chunk, da), lambda i: (0, 0)),
            pl.BlockSpec((3, db), lambda i: (0, 0)),
        ],
        [pl.BlockSpec((tm, da + db), lambda i: (i, 0))],
        [jax.ShapeDtypeStruct((s, da + db), BF16)], [proj, proj, norm_v, w_s, b_rows, conv_w], phases=phases,
    )


def _ab_mix_bwd(proj, dcat, norm_v, w_s, b_rows, conv_w, phases=()):
    s, n = proj.shape
    heads, chunk, _ = w_s.shape
    da = norm_v.shape[1]
    hd = da // heads
    db = conv_w.shape[1]
    tm = _pick(s, (512, 256, 128))
    nblk = s // tm
    dhalo = 2 * CONV_HALO

    def body(p_ref, pa_ref, pb_ref, dc_ref, dcb_ref, nv_ref, ws_ref, b_ref, cw_ref,
             dp_ref, dnv_ref, dws_ref, dzs_ref, dcw_ref, dvn_sc):
        i = pl.program_id(0)
        first, last = i == 0, i == nblk - 1

        @pl.when(first)
        def _():
            dnv_ref[...] = jnp.zeros_like(dnv_ref)
            dws_ref[...] = jnp.zeros_like(dws_ref)
            dzs_ref[...] = jnp.zeros_like(dzs_ref)
            dcw_ref[...] = jnp.zeros_like(dcw_ref)

        uu = p_ref[:, 0:da]
        gu, gu_grad = _gelu(uu)
        gv, gv_grad = _gelu(p_ref[:, da : 2 * da])
        xhat, rstd = _layernorm_stats(gv)
        nv = nv_ref[...]
        vn = (xhat * nv).astype(BF16)
        dya = dc_ref[:, 0:da].astype(F32)
        dz = (dya * gu).astype(BF16)
        mask = _tril(chunk)
        for hh in range(heads):
            wm = jnp.where(mask, ws_ref[hh], 0.0).astype(BF16)
            cols = slice(hh * hd, (hh + 1) * hd)
            dws = jnp.zeros((chunk, chunk), F32)
            for nn in range(tm // chunk):
                rows = slice(nn * chunk, (nn + 1) * chunk)
                z = _dot(wm, vn[rows, cols]) + b_ref[:, cols]
                dp_ref[rows, cols] = (dya[rows, cols] * z * gu_grad[rows, cols]).astype(BF16)
                dz_blk = dz[rows, cols]
                dws = dws + _dot_nt(dz_blk, vn[rows, cols])
                dzs_ref[:, cols] += dz_blk.astype(F32)
                dvn = _dot_tn(wm, dz_blk)
                dnv_ref[:, cols] += jnp.sum(dvn * xhat[rows, cols], axis=0, keepdims=True)
                dvn_sc[rows, cols] = dvn
            dws_ref[hh] += jnp.where(mask, dws, 0.0)
        dxhat = dvn_sc[...] * nv
        dgv = rstd * (dxhat - jnp.mean(dxhat, axis=-1, keepdims=True) - xhat * jnp.mean(dxhat * xhat, axis=-1, keepdims=True))
        dp_ref[:, da : 2 * da] = (dgv * gv_grad).astype(BF16)

        c_bg, c_cg, c_xb = 2 * da, 2 * da + db, 2 * da + 2 * db
        bg = p_ref[:, c_bg : c_bg + db]
        cg = p_ref[:, c_cg : c_cg + db]
        xb = p_ref[:, c_xb : c_xb + db]
        q = cg * xb
        q1 = _shift_down(q, 1, pa_ref, c_cg, c_xb, first)
        q2 = _shift_down(q, 2, pa_ref, c_cg, c_xb, first)
        dyb = dc_ref[:, da : da + db].astype(F32)
        conv = cw_ref[0:1, :] * q2 + cw_ref[1:2, :] * q1 + cw_ref[2:3, :] * q
        dp_ref[:, c_bg : c_bg + db] = (dyb * conv).astype(BF16)
        e = dyb * bg
        dcw_ref[0:1, :] += jnp.sum(e * q2, axis=0, keepdims=True)
        dcw_ref[1:2, :] += jnp.sum(e * q1, axis=0, keepdims=True)
        dcw_ref[2:3, :] += jnp.sum(e * q, axis=0, keepdims=True)
        rows = lax.broadcasted_iota(jnp.int32, e.shape, 0)
        dq = cw_ref[2:3, :] * e
        for kk in (1, 2):
            ek = pltpu.roll(e, tm - kk, 0)
            for r in range(kk):
                below = dcb_ref[r : r + 1, da : da + db].astype(F32) * pb_ref[r : r + 1, c_bg : c_bg + db]
                below = jnp.where(last, 0.0, below)
                ek = jnp.where(rows == tm - kk + r, below, ek)
            dq = dq + cw_ref[2 - kk : 3 - kk, :] * ek
        dp_ref[:, c_cg : c_cg + db] = (dq * xb).astype(BF16)
        dp_ref[:, c_xb : c_xb + db] = (dq * cg).astype(BF16)

    nh = tm // CONV_HALO
    nhb = tm // dhalo
    const2 = lambda i: (0, 0)
    return _call(
        body, "ab_mix_bwd", (nblk,),
        [
            pl.BlockSpec((tm, n), lambda i: (i, 0)),
            pl.BlockSpec((CONV_HALO, n), lambda i: (jnp.maximum(i * nh - 1, 0), 0)),
            pl.BlockSpec((CONV_HALO, n), lambda i: (jnp.minimum((i + 1) * nh, s // CONV_HALO - 1), 0)),
            pl.BlockSpec((tm, da + db), lambda i: (i, 0)),
            pl.BlockSpec((dhalo, da + db), lambda i: (jnp.minimum((i + 1) * nhb, s // dhalo - 1), 0)),
            pl.BlockSpec((1, da), const2),
            pl.BlockSpec((heads, chunk, chunk), lambda i: (0, 0, 0)),
            pl.BlockSpec((chunk, da), const2),
            pl.BlockSpec((3, db), const2),
        ],
        [
            pl.BlockSpec((tm, n), lambda i: (i, 0)),
            pl.BlockSpec((1, da), const2),
            pl.BlockSpec((heads, chunk, chunk), lambda i: (0, 0, 0)),
            pl.BlockSpec((chunk, da), const2),
            pl.BlockSpec((3, db), const2),
        ],
        [
            jax.ShapeDtypeStruct((s, n), BF16),
            jax.ShapeDtypeStruct((1, da), F32),
            jax.ShapeDtypeStruct((heads, chunk, chunk), F32),
            jax.ShapeDtypeStruct((chunk, da), F32),
            jax.ShapeDtypeStruct((3, db), F32),
        ],
        [proj, proj, proj, dcat, dcat, norm_v, w_s, b_rows, conv_w],
        scratch=[pltpu.VMEM((tm, da), F32)], phases=phases,
    )


def _pool_counts(tm, i, w):
    t = i * tm + lax.broadcasted_iota(jnp.int32, (tm, 1), 0)
    return jnp.minimum(t + 1, w).astype(F32)


def _pool_fwd(x, vec, w_grp, scale, phases=()):
    s, d = x.shape
    groups, gd, _ = w_grp.shape
    tm = _pick(s, (512, 256, 128))

    def body(x_ref, xa_ref, vec_ref, w_ref, sc_ref, xo_ref, p_ref, o_ref):
        i = pl.program_id(0)
        h = _modulate(x_ref[...], vec_ref)
        ha = jnp.where(i == 0, 0.0, _modulate(xa_ref[...], vec_ref))
        ext = jnp.concatenate([ha, h], axis=0)
        for gi, w in enumerate(POOL_WINDOWS):
            cols = slice(gi * gd, (gi + 1) * gd)
            acc = ext[:, cols]
            step = 1
            while step < w:
                acc = acc + pltpu.roll(acc, step, 0)
                step *= 2
            p = (acc[POOL_HALO:, :] / _pool_counts(tm, i, w) - h[:, cols]).astype(BF16)
            p_ref[:, cols] = p
            o_ref[:, cols] = _dot(p, w_ref[gi]).astype(BF16)
        xo_ref[...] = x_ref[...] + vec_ref[3:4, :] * (o_ref[...].astype(F32) * sc_ref[...])

    nh = tm // POOL_HALO
    row = pl.BlockSpec((tm, d), lambda i: (i, 0))
    return _call(
        body, "pool_fwd", (s // tm,),
        [
            row,
            pl.BlockSpec((POOL_HALO, d), lambda i: (jnp.maximum(i * nh - 1, 0), 0)),
            pl.BlockSpec((8, d), lambda i: (0, 0)),
            pl.BlockSpec((groups, gd, gd), lambda i: (0, 0, 0)),
            pl.BlockSpec((1, d), lambda i: (0, 0)),
        ],
        [row, row, row],
        [jax.ShapeDtypeStruct((s, d), F32), jax.ShapeDtypeStruct((s, d), BF16), jax.ShapeDtypeStruct((s, d), BF16)],
        [x, x, vec, w_grp, scale], phases=phases,
    )


def _pool_bwd(dxo, x, vec, p, o, w_grp, scale, phases=()):
    s, d = x.shape
    groups, gd, _ = w_grp.shape
    tm = _pick(s, (512, 256, 128))
    nblk = s // tm

    def body(dxo_ref, dxb_ref, x_ref, vec_ref, p_ref, o_ref, w_ref, sc_ref, dx_ref, dw_ref, dsc_ref, dvec_ref, dw_sc):
        i = pl.program_id(0)

        @pl.when(i == 0)
        def _():
            dw_sc[...] = jnp.zeros_like(dw_sc)
            dsc_ref[...] = jnp.zeros_like(dsc_ref)
            dvec_ref[...] = jnp.zeros_like(dvec_ref)

        gate, sc = vec_ref[3:4, :], sc_ref[...]
        dxo_v = dxo_ref[...]
        ov = o_ref[...].astype(F32)
        dvec_ref[3:4, :] += jnp.sum(dxo_v * (ov * sc), axis=0, keepdims=True)
        dy = gate * dxo_v
        dsc_ref[...] += jnp.sum(dy * ov, axis=0, keepdims=True)
        dout = (dy * sc).astype(BF16)
        dout_b = jnp.where(i == nblk - 1, 0.0, gate * dxb_ref[...] * sc).astype(BF16)
        for gi, w in enumerate(POOL_WINDOWS):
            cols = slice(gi * gd, (gi + 1) * gd)
            dw_sc[gi] += _dot_tn(p_ref[:, cols], dout[:, cols])
            wb = w_ref[gi]
            dp = _dot_nt(dout[:, cols], wb)
            dp_b = _dot_nt(dout_b[:, cols], wb)
            e = dp / _pool_counts(tm, i, w)
            t_below = (i + 1) * tm + lax.broadcasted_iota(jnp.int32, (POOL_HALO, 1), 0)
            e_b = dp_b / jnp.minimum(t_below + 1, w).astype(F32)
            acc = jnp.concatenate([e, e_b], axis=0)
            step = 1
            while step < w:
                acc = acc + pltpu.roll(acc, tm + POOL_HALO - step, 0)
                step *= 2
            dx_ref[:, cols] = acc[:tm, :] - dp
        dx, _ = _modulate_bwd(x_ref[...], dx_ref[...], vec_ref, dvec_ref)
        dx_ref[...] = dxo_v + dx

        @pl.when(i == nblk - 1)
        def _():
            dw_ref[...] = dw_sc[...].astype(BF16)

    nh = tm // POOL_HALO
    row = pl.BlockSpec((tm, d), lambda i: (i, 0))
    vecs = pl.BlockSpec((8, d), lambda i: (0, 0))
    wspec = pl.BlockSpec((groups, gd, gd), lambda i: (0, 0, 0))
    return _call(
        body, "pool_bwd", (nblk,),
        [
            row,
            pl.BlockSpec((POOL_HALO, d), lambda i: (jnp.minimum((i + 1) * nh, s // POOL_HALO - 1), 0)),
            row, vecs, row, row, wspec,
            pl.BlockSpec((1, d), lambda i: (0, 0)),
        ],
        [row, wspec, pl.BlockSpec((1, d), lambda i: (0, 0)), vecs],
        [
            jax.ShapeDtypeStruct((s, d), F32),
            jax.ShapeDtypeStruct((groups, gd, gd), BF16),
            jax.ShapeDtypeStruct((1, d), F32),
            jax.ShapeDtypeStruct((8, d), F32),
        ],
        [dxo, dxo, x, vec, p, o, w_grp, scale],
        scratch=[pltpu.VMEM((groups, gd, gd), F32)], phases=phases,
    )


def _loss_head(x, gain, target, phases=()):
    s, d = x.shape
    tm = _pick(s, (512, 256, 128))

    def body(x_ref, g_ref, t_ref, dx_ref, aux_ref):
        @pl.when(pl.program_id(0) == 0)
        def _():
            aux_ref[...] = jnp.zeros_like(aux_ref)

        xv = x_ref[...]
        rstd = _rstd(xv)
        r = xv * rstd
        gain_v = g_ref[...]
        err = r * gain_v - t_ref[...]
        aux_ref[1:2, :] += jnp.sum(err * err, axis=0, keepdims=True)
        dout = err * (1.0 / d)
        aux_ref[0:1, :] += jnp.sum(dout * r, axis=0, keepdims=True)
        dr = dout * gain_v
        dx_ref[...] = rstd * (dr - r * jnp.mean(dr * r, axis=-1, keepdims=True))

    row = pl.BlockSpec((tm, d), lambda i: (i, 0))
    return _call(
        body, "loss_head", (s // tm,),
        [row, pl.BlockSpec((1, d), lambda i: (0, 0)), row],
        [row, pl.BlockSpec((8, d), lambda i: (0, 0))],
        [jax.ShapeDtypeStruct((s, d), F32), jax.ShapeDtypeStruct((8, d), F32)], [x, gain, target], phases=phases,
    )


def _small_adam(gathered, gathered_ws, layout, smalls, chip):
    names = list(smalls)
    n = len(names)

    def body(*refs):
        chip_ref, g_ref, gws_ref = refs[0], refs[1], refs[2]
        wmv = refs[3 : 3 + 3 * n]
        outs = refs[3 + 3 * n : 3 + 7 * n]
        total = refs[-1]
        total[...] = g_ref[0]
        for kdev in range(1, N_DEV):
            total[...] += g_ref[kdev]
        total_ws = gws_ref[0]
        for kdev in range(1, N_DEV):
            total_ws = total_ws + gws_ref[kdev]
        my_chip = chip_ref[0]
        for a, name in enumerate(names):
            w_ref, m_ref, v_ref = wmv[3 * a : 3 * a + 3]
            if name == "ab_w_s":
                g = total_ws
            else:
                row0, rows, col0, cols = layout[name]
                if col0 is None:
                    g = jnp.zeros((rows, cols), F32)
                    for j in range(N_CHIPS):
                        g = g + jnp.where(my_chip == j, total[row0 : row0 + rows, j * cols : (j + 1) * cols], 0.0)
                else:
                    g = total[row0 : row0 + rows, col0 : col0 + cols]
            dl, mo, vo = _adam(w_ref[...], g, m_ref[...], v_ref[...])
            outs[4 * a][...] = g
            outs[4 * a + 1][...] = dl
            outs[4 * a + 2][...] = mo
            outs[4 * a + 3][...] = vo

    ins = [gathered, gathered_ws]
    out_shapes = []
    for name in names:
        ins.extend(smalls[name])
        out_shapes.extend([jax.ShapeDtypeStruct(smalls[name][0].shape, F32)] * 4)
    whole = lambda shape: pl.BlockSpec(shape, functools.partial(lambda nd, i, c: (0,) * nd, len(shape)))
    res = pl.pallas_call(
        body, name="small_adam",
        grid_spec=pltpu.PrefetchScalarGridSpec(
            num_scalar_prefetch=1, grid=(1,),
            in_specs=[whole(a.shape) for a in ins], out_specs=[whole(o.shape) for o in out_shapes],
            scratch_shapes=[pltpu.VMEM(gathered.shape[1:], F32)],
        ),
        out_shape=out_shapes,
        compiler_params=pltpu.CompilerParams(dimension_semantics=("arbitrary",), vmem_limit_bytes=VMEM_LIMIT_BYTES),
    )(chip.reshape(1).astype(jnp.int32), *ins)
    return {name: res[4 * a : 4 * a + 4] for a, name in enumerate(names)}


def _pad_rows(a, rows=8):
    extra = (-a.shape[0]) % rows
    return jnp.pad(a, ((0, extra), (0, 0))) if extra else a


def _pad_cols(a, cols):
    return jnp.pad(a, ((0, 0), (0, cols - a.shape[1]))) if a.shape[1] < cols else a


def _run(fn, *phases):
    outs, p_outs = fn(list(phases))
    for p, po in zip(phases, p_outs):
        p.then(po)
    return outs


def kernel(x, c, norm_g, w_mod, b_mod, w_ffn_in, w_ffn_out, ab_w_in, ab_norm_v, ab_w_s, ab_b_s, ab_conv_w, ab_w_out, pool_w_grp, pool_scale, final_g, loss_target, m_norm_g, m_w_mod, m_b_mod, m_w_ffn_in, m_w_ffn_out, m_ab_w_in, m_ab_norm_v, m_ab_w_s, m_ab_b_s, m_ab_conv_w, m_ab_w_out, m_pool_w_grp, m_pool_scale, m_final_g, v_norm_g, v_w_mod, v_b_mod, v_w_ffn_in, v_w_ffn_out, v_ab_w_in, v_ab_norm_v, v_ab_w_s, v_ab_b_s, v_ab_conv_w, v_ab_w_out, v_pool_w_grp, v_pool_scale, v_final_g):
    ix, iy, ic = _place()
    chip = 2 * ix + iy
    me = 4 * ix + 2 * iy + ic
    where = jnp.stack([chip, ic]).astype(jnp.int32)
    s, d = x.shape[1], x.shape[2]
    x0 = x.reshape(s, d)
    target = loss_target.reshape(s, d)
    n_layers = norm_g.shape[0]
    dq = d // N_CHIPS
    heads, chunk = ab_w_s.shape[1], ab_w_s.shape[2]
    da = ab_norm_v.shape[1]
    db = ab_conv_w.shape[2] * N_CHIPS
    f_hidden = w_ffn_out.shape[2] * N_CHIPS
    assert n_layers == 2 and da % heads == 0

    cw_pad = _pad_cols(ab_conv_w.reshape(3, db // N_CHIPS), dq)
    packed = jnp.concatenate(
        [_pad_rows(c.reshape(N_CHIPS, dq)), _pad_rows(norm_g.reshape(-1, dq)), _pad_rows(pool_scale.reshape(1, dq)), _pad_rows(cw_pad)],
        axis=0,
    )
    ncol = w_mod.shape[2]
    b_cols = lax.dynamic_slice(b_mod, (0, chip * ncol), (n_layers, ncol)).reshape(n_layers, 1, ncol)
    small = {}

    def small_gather(key, arrs):
        def then(outs):
            small[key] = outs

        return _phase_small_gather(arrs, then)

    stacks = {
        "w_ffn_in": tuple(a.reshape((-1,) + a.shape[2:]) for a in (w_ffn_in, m_w_ffn_in, v_w_ffn_in)),
        "w_ffn_out": tuple(a.reshape((-1,) + a.shape[2:]) for a in (w_ffn_out, m_w_ffn_out, v_w_ffn_out)),
        "ab_w_in": (ab_w_in, m_ab_w_in, v_ab_w_in),
        "ab_w_out": (ab_w_out, m_ab_w_out, v_ab_w_out),
        "pool_w_grp": (pool_w_grp[0], m_pool_w_grp[0], v_pool_w_grp[0]),
    }
    big_in = _Big((1, d, 2 * f_hidden), 2, 1)
    big_out = _Big((1, f_hidden, d), 1, 2)
    units = {}
    for l in range(n_layers):
        for k in range(2):
            units[f"in{l}{k}"] = (big_in, "w_ffn_in", 2 * l + k)
            units[f"out{l}{k}"] = (big_out, "w_ffn_out", 2 * l + k)
    units["abin"] = (_Big((1, d, ab_w_in.shape[2] * N_CHIPS), 2, 1), "ab_w_in", 0)
    units["about"] = (_Big((1, ab_w_out.shape[1] * N_CHIPS, d), 1, 2), "ab_w_out", 0)
    units["pool"] = (_Big((pool_w_grp.shape[1], pool_w_grp.shape[2] * N_CHIPS, pool_w_grp.shape[3]), 1, 0), "pool_w_grp", 0)
    big = {u: g for u, (g, _, _) in units.items()}

    weight = {}
    complete = set()

    def cast(u):
        g, st, b0 = units[u]

        def launch(phases):
            (weight[u],), p_outs = _cast_into_full(stacks[st][0], b0, g, where, "cast_" + u, phases)
            return None, p_outs

        return launch

    def gather_ici(*us):
        def then(outs):
            for u, o in zip(us, outs):
                weight[u] = o

        return _phase_gather_ici([weight[u] for u in us], [big[u] for u in us], then)

    def gather_sibling(*us):
        def then(outs):
            for u, o in zip(us, outs):
                weight[u] = o
                complete.add(u)

        return _phase_gather_sibling([weight[u] for u in us], [big[u] for u in us], then)

    def w_of(u):
        assert u in complete, u
        return weight[u]

    _run(cast("in00"), small_gather("inputs", [packed]))
    _run(cast("out00"))
    small_all = small["inputs"][0]
    by_chip = small_all[0::2]
    c_all = small_all[:, 0:N_CHIPS, :].reshape(N_DEV, d)
    norm_full = by_chip[:, 8 : 8 + 3 * n_layers, :].transpose(1, 0, 2).reshape(3 * n_layers, d)
    pool_scale_full = by_chip[:, 16:17, :].transpose(1, 0, 2).reshape(1, d)
    conv_full = by_chip[:, 24:27, : db // N_CHIPS].transpose(1, 0, 2).reshape(3, db)
    mod_cols = _run(lambda phases: _mod_fwd(c_all, w_mod, b_cols, phases), gather_ici("in00"))[0]
    _run(cast("abin"), gather_sibling("in00"), gather_ici("out00"), small_gather("mod", [mod_cols.reshape(n_layers * N_DEV, ncol)]))
    _run(cast("about"), gather_sibling("out00"), gather_ici("abin"))
    _run(cast("in01"), gather_sibling("abin"), gather_ici("about"))
    _run(cast("out01"), gather_sibling("about"))
    for u in ("in10", "out10", "pool", "in11", "out11"):
        _run(cast(u))
    mod_all = small["mod"][0]
    mod_mine = lax.dynamic_index_in_dim(mod_all[0::2].reshape(N_CHIPS, n_layers, N_DEV, ncol), me, axis=2, keepdims=False)
    mod = mod_mine.transpose(1, 0, 2).reshape(n_layers, 3, 3, d)
    vecs = {
        (l, sub): _pad_rows(jnp.concatenate([norm_full[3 * l + sub][None], mod[l, sub]], axis=0))
        for l in range(n_layers)
        for sub in range(3)
    }
    b_rows = jnp.broadcast_to(ab_b_s[0].T[:, :, None], (chunk, heads, da // heads)).reshape(chunk, da)

    saved = {}

    def ffn_forward(xs, l, sub, k, *phases):
        saved[l, sub, "x"] = xs
        xs, gg, uu, yb = _run(
            lambda ph: _ffn_fwd(xs, vecs[l, sub], w_of(f"in{l}{k}"), w_of(f"out{l}{k}"), f"ffn_fwd_{l}{k}", ph), *phases
        )
        saved[l, sub, "act"] = (gg, uu, yb)
        return xs

    xs = ffn_forward(x0, 0, 0, 0, gather_ici("in01"))
    saved[0, 1, "x"] = xs
    (proj,) = _run(lambda ph: _proj_mod_fwd(xs, vecs[0, 1], w_of("abin"), ph), gather_sibling("in01"), gather_ici("out01"))
    (cat,) = _run(lambda ph: _ab_mix_fwd(proj, ab_norm_v, ab_w_s[0], b_rows, conv_full, ph), gather_sibling("out01"), gather_ici("in10"))
    xs, yb = _run(lambda ph: _proj_res_fwd(cat, w_of("about"), xs, vecs[0, 1], ph), gather_sibling("in10"), gather_ici("out10", "pool"))
    saved[0, 1, "act"] = (proj, cat, yb)
    xs = ffn_forward(xs, 0, 2, 1, gather_sibling("out10", "pool"), gather_ici("in11"))
    xs = ffn_forward(xs, 1, 0, 0, gather_sibling("in11"), gather_ici("out11"))
    saved[1, 1, "x"] = xs
    xs, pp, oo = _run(lambda ph: _pool_fwd(xs, vecs[1, 1], w_of("pool"), pool_scale_full, ph), gather_sibling("out11"))
    saved[1, 1, "act"] = (pp, oo)
    xs = ffn_forward(xs, 1, 2, 1)
    dxs, aux = _run(lambda ph: _loss_head(xs, final_g.reshape(1, d), target, ph))
    loss = lax.psum(0.5 * jnp.sum(aux[1]) / d, ("x", "y", "c"))

    grad = {}
    recv = {}
    csum = {}
    parts = {}
    reduced = {}
    done = set()
    dvecs, small_g = {}, {}

    def pair_exchange(*us):
        def then(outs):
            for u, o in zip(us, outs):
                recv[u] = o

        return _phase_pair_exchange([grad[u] for u in us], [big[u] for u in us], then)

    def pair_sum(u, *phases):
        def launch(ph):
            (csum[u],), p_outs = _pair_sum(grad[u], recv[u], big[u], where, "pair_sum_" + u, ph)
            return None, p_outs

        _run(launch, *phases)

    def chip_exchange(*us):
        def then(outs):
            for u, o in zip(us, outs):
                parts[u] = o

        return _phase_chip_exchange([csum[u] for u in us], [big[u] for u in us], then)

    def chip_sum(*us, carried=()):
        for n_u, u in enumerate(us):
            g, st, b0 = units[u]

            def launch(ph):
                (reduced[st],), p_outs = _chip_sum(
                    csum[u], parts[u], g, where, reduced.get(st), stacks[st][0].shape, b0, "chip_sum_" + u, ph
                )
                return None, p_outs

            _run(launch, *(carried if n_u == 0 else ()))

    def pair_broadcast(*us):
        sts = [units[u][1] for u in us]
        assert len(set(sts)) == len(sts)

        def then(outs):
            for u, st, o in zip(us, sts, outs):
                reduced[st] = o
                done.add(u)

        return _phase_pair_broadcast([reduced[st] for st in sts], [big[u] for u in us], [units[u][2] for u in us], then)

    def ffn_backward(dxs, l, sub, k, carried_bwd, carried_out, carried_u):
        gg, uu, yb = saved[l, sub, "act"]
        w_in, w_out = w_of(f"in{l}{k}"), w_of(f"out{l}{k}")
        dxs, dg, du, a, h, dy, dvecs[l, sub] = _run(
            lambda ph: _ffn_bwd(dxs, saved[l, sub, "x"], vecs[l, sub], gg, uu, yb, w_in, w_out, f"ffn_bwd_{l}{k}", ph), *carried_bwd()
        )
        uo, ui = f"out{l}{k}", f"in{l}{k}"
        (grad[uo],) = _run(lambda ph: _grad_matmul(a, dy, big[uo].f3, 0, None, f"dw_out_{l}{k}", ph), *carried_out())
        (gu,) = _run(lambda ph: _grad_matmul(h, du, big[ui].f3, f_hidden, None, f"dw_in_u_{l}{k}", ph), pair_exchange(uo))
        (grad[ui],) = _run(lambda ph: _grad_matmul(h, dg, big[ui].f3, 0, gu, f"dw_in_g_{l}{k}", ph), *carried_u())
        return dxs

    none = lambda: ()
    dxs = ffn_backward(dxs, 1, 2, 1, none, none, none)
    pp, oo = saved[1, 1, "act"]
    dxs, grad["pool"], small_g["pool_scale"], dvecs[1, 1] = _run(
        lambda ph: _pool_bwd(dxs, saved[1, 1, "x"], vecs[1, 1], pp, oo, w_of("pool"), pool_scale_full, ph)
    )
    pair_sum("out11", pair_exchange("in11", "pool"))
    pair_sum("in11")
    pair_sum("pool")

    def after_11():
        return (chip_exchange("in11", "out11", "pool"),)

    def bcast_11():
        chip_sum("in11", "out11", "pool")
        return (pair_broadcast("in11", "out11", "pool"),)

    dxs = ffn_backward(dxs, 1, 0, 0, after_11, bcast_11, none)
    pair_sum("out10", pair_exchange("in10"))
    pair_sum("in10")

    def after_10():
        return (chip_exchange("in10", "out10"),)

    def bcast_10():
        chip_sum("in10", "out10")
        return (pair_broadcast("in10", "out10"),)

    dxs = ffn_backward(dxs, 0, 2, 1, after_10, bcast_10, none)
    pair_sum("out01", pair_exchange("in01"))
    pair_sum("in01")

    proj, cat, yb = saved[0, 1, "act"]
    dy, dcat, dgate = _run(lambda ph: _proj_res_bwd(dxs, yb, vecs[0, 1], w_of("about"), ph))
    (grad["about"],) = _run(lambda ph: _grad_matmul(cat, dy, big["about"].f3, 0, None, "dw_ab_out", ph))
    dproj, small_g["ab_norm_v"], small_g["ab_w_s"], dzs, small_g["ab_conv_w"] = _run(
        lambda ph: _ab_mix_bwd(proj, dcat, ab_norm_v, ab_w_s[0], b_rows, conv_full, ph), chip_exchange("out01")
    )
    small_g["ab_b_s"] = dzs.reshape(chunk, heads, da // heads).sum(axis=2).T
    chip_sum("out01")
    dxs, h, dvecs[0, 1] = _run(
        lambda ph: _proj_mod_bwd(dproj, w_of("abin"), saved[0, 1, "x"], vecs[0, 1], dxs, dgate, ph),
        pair_exchange("about"), pair_broadcast("out01"),
    )
    (grad["abin"],) = _run(lambda ph: _grad_matmul(h, dproj, big["abin"].f3, 0, None, "dw_ab_in", ph))
    pair_sum("about", pair_exchange("abin"))
    pair_sum("abin")

    def after_01():
        return (chip_exchange("in01", "abin", "about"),)

    def bcast_01():
        chip_sum("in01", "abin", "about")
        return (pair_broadcast("in01", "abin", "about"),)

    def reduce_out00():
        pair_sum("out00")
        return (chip_exchange("out00"),)

    dxs = ffn_backward(dxs, 0, 0, 0, after_01, bcast_01, reduce_out00)
    grad_x = dxs.reshape(x.shape)

    dgain = jnp.stack([dvecs[l, sub][0] for l in range(n_layers) for sub in range(3)])
    dmod = jnp.concatenate([dvecs[l, sub][1:4] for l in range(n_layers) for sub in range(3)], axis=0)
    pieces = {
        "norm_g": (dgain, None, dq), "final_g": (aux[0:1], 0, d), "pool_scale": (small_g["pool_scale"], None, dq),
        "b_mod": (dmod, 0, d), "ab_norm_v": (small_g["ab_norm_v"], 0, da), "ab_conv_w": (small_g["ab_conv_w"], None, db // N_CHIPS),
        "ab_b_s": (small_g["ab_b_s"], 0, chunk),
    }
    layout, row0 = {}, 0
    for nm, (pc, col0, cols) in pieces.items():
        layout[nm] = (row0, pc.shape[0], col0, cols)
        row0 += pc.shape[0]
    packed_rows = -(-row0 // 8) * 8
    packed_g = sum(
        jnp.pad(pc, ((layout[nm][0], packed_rows - layout[nm][0] - pc.shape[0]), (0, d - pc.shape[1])))
        for nm, (pc, _, _) in pieces.items()
    )

    chip_sum("out00", carried=(pair_exchange("in00"),))
    pair_sum("in00", pair_broadcast("out00"))
    _flush("reduce_last", chip_exchange("in00"), small_gather("grads", [packed_g, small_g["ab_w_s"].reshape(heads * chunk, chunk)]))
    chip_sum("in00")
    _flush("broadcast_last", pair_broadcast("in00"))
    g_all, gws_all = small["grads"]

    assert done == set(units)
    out = {}
    for st, (w3, m3, v3) in stacks.items():
        shape = {"w_ffn_in": w_ffn_in.shape, "w_ffn_out": w_ffn_out.shape, "pool_w_grp": pool_w_grp.shape}.get(st, w3.shape)
        out[st] = tuple(a.reshape(shape) for a in _adam_rows(w3, reduced[st], m3, v3, 0, w3.shape[0], None, "adam_" + st))

    shapes2d = {
        "norm_g": (3 * n_layers, dq), "b_mod": (9 * n_layers, d), "final_g": (1, d), "ab_norm_v": (1, da),
        "pool_scale": (1, dq), "ab_conv_w": (3, db // N_CHIPS), "ab_b_s": (heads, chunk), "ab_w_s": (heads * chunk, chunk),
    }
    small_w = {"norm_g": (norm_g, m_norm_g, v_norm_g), "b_mod": (b_mod, m_b_mod, v_b_mod), "final_g": (final_g, m_final_g, v_final_g),
               "ab_norm_v": (ab_norm_v, m_ab_norm_v, v_ab_norm_v), "pool_scale": (pool_scale, m_pool_scale, v_pool_scale),
               "ab_conv_w": (ab_conv_w, m_ab_conv_w, v_ab_conv_w), "ab_b_s": (ab_b_s, m_ab_b_s, v_ab_b_s), "ab_w_s": (ab_w_s, m_ab_w_s, v_ab_w_s)}
    smalls = {nm: tuple(a.reshape(shapes2d[nm]) for a in wmv) for nm, wmv in small_w.items()}
    small_out = _small_adam(g_all, gws_all, layout, smalls, chip)
    for nm, res in small_out.items():
        out[nm] = tuple(a.reshape(small_w[nm][0].shape) for a in res)

    mod_row0 = layout["b_mod"][0]
    dmod_all = g_all[:, mod_row0 : mod_row0 + 9 * n_layers, :].reshape(N_DEV, n_layers, 9 * d)
    dmod_cols = lax.dynamic_slice(dmod_all, (0, 0, chip * ncol), (N_DEV, n_layers, ncol)).transpose(1, 0, 2)
    out["w_mod"] = tuple(_run(lambda ph: _mod_bwd_adam(c_all.T, dmod_cols, w_mod, m_w_mod, v_w_mod, ph)))

    order = ["norm_g", "w_mod", "b_mod", "w_ffn_in", "w_ffn_out", "ab_w_in", "ab_norm_v", "ab_w_s", "ab_b_s", "ab_conv_w", "ab_w_out", "pool_w_grp", "pool_scale", "final_g"]
    return (loss, grad_x, *[out[nm][0] for nm in order], *[out[nm][1] for nm in order], *[out[nm][2] for nm in order], *[out[nm][3] for nm in order])
```

```python
import functools
import math

import jax
import jax.numpy as jnp
from jax import lax
from jax.experimental import pallas as pl
from jax.experimental.pallas import tpu as pltpu

F32 = jnp.float32
BF16 = jnp.bfloat16
MESH = pl.DeviceIdType.MESH

EPS = 1e-6
ADAM_LR = 0.001
ADAM_B1 = 0.9
ADAM_B2 = 0.999
ADAM_EPS = 1e-08
ADAM_WD = 0.01
ADAM_STEP = 10
POOL_WINDOWS = (2, 4, 8, 16)
POOL_HALO = 16
CONV_HALO = 8
N_CHIPS = 4
N_DEV = 8
VMEM_LIMIT_BYTES = 48 * 1024 * 1024
EW_BLOCK_ELEMS = 256 * 1024


def _pick(n, prefs):
    for p in prefs:
        if p <= n and n % p == 0:
            return p
    return n


def _row_tile(rows, cols):
    best = None
    for d in range(16, rows + 1, 16):
        if rows % d == 0 and d * cols <= EW_BLOCK_ELEMS:
            best = d
    return best or rows


def _dot(a, b):
    return jnp.dot(a, b, preferred_element_type=F32)


def _dot_nt(a, b):
    return lax.dot_general(a, b, (((1,), (1,)), ((), ())), preferred_element_type=F32)


def _dot_tn(a, b):
    return lax.dot_general(a, b, (((0,), (0,)), ((), ())), preferred_element_type=F32)


def _sigmoid(x):
    return 1.0 / (1.0 + jnp.exp(-x))


_GELU_C = math.sqrt(2.0 / math.pi)


def _gelu(x):
    x2 = x * x
    t = jnp.tanh(_GELU_C * (x + 0.044715 * x2 * x))
    val = 0.5 * x * (1.0 + t)
    grad = 0.5 * (1.0 + t) + 0.5 * x * (1.0 - t * t) * (_GELU_C * (1.0 + 3.0 * 0.044715 * x2))
    return val, grad


def _rstd(x):
    return lax.rsqrt(jnp.mean(x * x, axis=-1, keepdims=True) + EPS)


def _modulate(x, vec_ref):
    return (x * _rstd(x)) * vec_ref[0:1, :] * (1.0 + vec_ref[2:3, :]) + vec_ref[1:2, :]


def _modulate_bwd(x, dh, vec_ref, dvec_ref):
    gn, sh, sc = vec_ref[0:1, :], vec_ref[1:2, :], vec_ref[2:3, :]
    rstd = _rstd(x)
    r = x * rstd
    dvec_ref[0:1, :] += jnp.sum(dh * r * (1.0 + sc), axis=0, keepdims=True)
    dvec_ref[1:2, :] += jnp.sum(dh, axis=0, keepdims=True)
    dvec_ref[2:3, :] += jnp.sum(dh * r * gn, axis=0, keepdims=True)
    gm = gn * (1.0 + sc)
    dr = dh * gm
    dx = rstd * (dr - r * jnp.mean(dr * r, axis=-1, keepdims=True))
    return dx, r * gm + sh


def _adam(w, g, m, v):
    m = ADAM_B1 * m + (1.0 - ADAM_B1) * g
    v = ADAM_B2 * v + (1.0 - ADAM_B2) * (g * g)
    m_hat = m / (1.0 - ADAM_B1**ADAM_STEP)
    v_hat = v / (1.0 - ADAM_B2**ADAM_STEP)
    delta = -ADAM_LR * (m_hat / (jnp.sqrt(v_hat) + ADAM_EPS) + ADAM_WD * w)
    return delta, m, v


_ANY = pl.BlockSpec(memory_space=pl.ANY)


class _Phase:
    def __init__(self, ins, out_shapes, aliases, n_sems, start, finish, then):
        self.ins, self.out_shapes, self.aliases, self.n_sems = list(ins), list(out_shapes), dict(aliases), n_sems
        self.start, self.finish, self.then = start, finish, then


def _call(body, name, grid, in_specs, out_specs, out_shape, ins, scratch=(), prefetch=(), phases=(), in_place=None):
    n_pre, n_in, n_out, n_sc = len(prefetch), len(in_specs), len(out_specs), len(scratch)
    ph_in = [len(p.ins) for p in phases]
    ph_out = [len(p.out_shapes) for p in phases]

    def kernel_body(*refs):
        pos = [0]

        def take(k):
            pos[0] += k
            return refs[pos[0] - k : pos[0]]

        pre, ins_ = take(n_pre), take(n_in)
        p_ins = [take(k) for k in ph_in]
        outs_ = take(n_out)
        p_outs = [take(k) for k in ph_out]
        sc = take(n_sc)
        sems = [take(2) for _ in phases]
        if phases:
            ids = [pl.program_id(a) for a in range(len(grid))]
            first = functools.reduce(jnp.logical_and, [i == 0 for i in ids])
            last = functools.reduce(jnp.logical_and, [i == g - 1 for i, g in zip(ids, grid)])

            @pl.when(first)
            def _():
                for p, pi, po, (send, recv) in zip(phases, p_ins, p_outs, sems):
                    p.start(pi, po, send, recv)

        if body is not None:
            body(*pre, *ins_, *outs_, *sc)
        if phases:

            @pl.when(last)
            def _():
                for p, pi, po, (send, recv) in zip(phases, p_ins, p_outs, sems):
                    p.finish(pi, po, send, recv)

    aliases = {n_pre + i: o for i, o in (in_place or {}).items()}
    i0, o0 = n_pre + n_in, n_out
    for p in phases:
        for i, o in p.aliases.items():
            aliases[i0 + i] = o0 + o
        i0 += len(p.ins)
        o0 += len(p.out_shapes)
    all_in = list(in_specs) + [_ANY] * sum(ph_in)
    all_out = list(out_specs) + [_ANY] * sum(ph_out)
    all_scratch = list(scratch)
    for p in phases:
        all_scratch += [pltpu.SemaphoreType.DMA((p.n_sems,)), pltpu.SemaphoreType.DMA((p.n_sems,))]
    shapes = list(out_shape) + [s for p in phases for s in p.out_shapes]
    operands = list(prefetch) + list(ins) + [a for p in phases for a in p.ins]
    sem = ("arbitrary",) * len(grid)
    params = pltpu.CompilerParams(dimension_semantics=sem, vmem_limit_bytes=VMEM_LIMIT_BYTES)
    if n_pre:
        res = pl.pallas_call(
            kernel_body, name=name, out_shape=shapes, input_output_aliases=aliases, compiler_params=params,
            grid_spec=pltpu.PrefetchScalarGridSpec(
                num_scalar_prefetch=n_pre, grid=grid, in_specs=all_in, out_specs=all_out, scratch_shapes=all_scratch
            ),
        )(*operands)
    else:
        res = pl.pallas_call(
            kernel_body, name=name, grid=grid, in_specs=all_in, out_specs=all_out, out_shape=shapes,
            scratch_shapes=all_scratch, input_output_aliases=aliases, compiler_params=params,
        )(*operands)
    res = list(res)
    outs, rest = res[:n_out], res[n_out:]
    p_res = []
    for k in ph_out:
        p_res.append(rest[:k])
        rest = rest[k:]
    return outs, p_res


def _place():
    return lax.axis_index("x"), lax.axis_index("y"), lax.axis_index("c")


def _other_chips():
    x, y, _ = _place()
    return [(1 - x, y), (x, 1 - y), (1 - x, 1 - y)]


def _flip(k):
    x, y, c = _place()
    return (1 - x if k & 4 else x, 1 - y if k & 2 else y, 1 - c if k & 1 else c)


def _remote(src, dst, send, recv, k, to):
    return pltpu.make_async_remote_copy(
        src_ref=src, dst_ref=dst, send_sem=send.at[k], recv_sem=recv.at[k], device_id=to, device_id_type=MESH
    )


def _phase_small_gather(arrs, then):
    n = len(arrs)

    def copies(ins, outs, send, recv):
        x, y, c = _place()
        me = 4 * x + 2 * y + c
        local = [pltpu.make_async_copy(ins[a], outs[a].at[me], send.at[a * N_DEV]) for a in range(n)]
        remote = [_remote(ins[a], outs[a].at[me], send, recv, a * N_DEV + k, _flip(k)) for a in range(n) for k in range(1, N_DEV)]
        return local, remote

    def start(ins, outs, send, recv):
        local, remote = copies(ins, outs, send, recv)
        for cp in local + remote:
            cp.start()

    def finish(ins, outs, send, recv):
        local, remote = copies(ins, outs, send, recv)
        for cp in remote + local:
            cp.wait()

    shapes = [jax.ShapeDtypeStruct((N_DEV,) + a.shape, a.dtype) for a in arrs]
    return _Phase(arrs, shapes, {}, n * N_DEV, start, finish, then)


def _flush(name, *phases):
    _, p_outs = _call(None, name, (1,), [], [], [], [], phases=list(phases))
    for p, po in zip(phases, p_outs):
        p.then(po)


class _Big:
    KINDS = {"full": (True, True), "half": (True, False), "shard": (False, True), "block": (False, False)}

    def __init__(self, f3, s3, h3):
        assert s3 != h3
        self.f3, self.s3, self.h3 = tuple(f3), s3, h3
        self.bd = tuple(f3[a] // (N_CHIPS if a == s3 else 1) // (2 if a == h3 else 1) for a in range(3))
        self.tile = (1, _row_tile(self.bd[1], self.bd[2]), self.bd[2])
        self.grid = tuple(self.bd[a] // self.tile[a] for a in range(3))

    def dims(self, kind):
        chips, halves = self.KINDS[kind]
        return tuple(
            self.bd[a] * (N_CHIPS if chips and a == self.s3 else 1) * (2 if halves and a == self.h3 else 1) for a in range(3)
        )

    def view(self, ref, chip=None, half=None, batch0=0, both_halves=True):
        start = [batch0, 0, 0]
        size = list(ref.shape)
        size[0] = self.bd[0] * (2 if self.h3 == 0 and both_halves else 1)
        if chip is not None:
            start[self.s3] += chip * self.bd[self.s3]
            size[self.s3] = self.bd[self.s3]
        if half is not None:
            start[self.h3] += half * self.bd[self.h3]
            size[self.h3] = self.bd[self.h3]
        return ref.at[tuple(pl.ds(st, sz) for st, sz in zip(start, size))]

    def spec(self, chip_from=None, half_from=None, lead=(), batch0=0):
        extra = "grid" in (chip_from, half_from)

        def index(*args):
            pref, idx = args[-1], list(args[int(extra) : -1])
            idx[0] += batch0
            if chip_from:
                idx[self.s3] += (pref[0] if chip_from == "pref" else args[0]) * self.grid[self.s3]
            if half_from:
                idx[self.h3] += (pref[1] if half_from == "pref" else args[0]) * self.grid[self.h3]
            return (0,) * len(lead) + tuple(idx)

        return pl.BlockSpec(tuple(lead) + self.tile, index)


def _same(arrs):
    return [jax.ShapeDtypeStruct(a.shape, a.dtype) for a in arrs]


def _phase_gather_ici(arrs, bigs, then):
    n = len(arrs)

    def copies(outs, send, recv, arriving):
        x, y, c = _place()
        return [
            _remote(blk, blk, send, recv, 3 * a + j, (*chip, c))
            for j, chip in enumerate(_other_chips())
            for a in range(n)
            for blk in [bigs[a].view(outs[a], 2 * chip[0] + chip[1] if arriving else 2 * x + y, c)]
        ]

    def start(ins, outs, send, recv):
        for cp in copies(outs, send, recv, False):
            cp.start()

    def finish(ins, outs, send, recv):
        for cp in copies(outs, send, recv, True):
            cp.wait_recv()
        for cp in copies(outs, send, recv, False):
            cp.wait_send()

    return _Phase(arrs, _same(arrs), {a: a for a in range(n)}, 3 * n, start, finish, then)


def _phase_gather_sibling(arrs, bigs, then):
    n = len(arrs)

    def copies(outs, send, recv, arriving):
        x, y, c = _place()
        return [
            _remote(blk, blk, send, recv, 3 * a + j, (x, y, 1 - c))
            for j, chip in enumerate(_other_chips())
            for a in range(n)
            for blk in [bigs[a].view(outs[a], 2 * chip[0] + chip[1], 1 - c if arriving else c)]
        ]

    def start(ins, outs, send, recv):
        for cp in copies(outs, send, recv, False):
            cp.start()

    def finish(ins, outs, send, recv):
        for cp in copies(outs, send, recv, True):
            cp.wait_recv()
        for cp in copies(outs, send, recv, False):
            cp.wait_send()

    return _Phase(arrs, _same(arrs), {a: a for a in range(n)}, 3 * n, start, finish, then)


def _phase_pair_exchange(grads, bigs, then):
    n = len(grads)

    def copies(ins, outs, send, recv):
        x, y, c = _place()
        srcs = [ins[a] if ins[a].shape == outs[a].shape else bigs[a].view(ins[a], None, 1 - c) for a in range(n)]
        return [_remote(srcs[a], outs[a], send, recv, a, (x, y, 1 - c)) for a in range(n)]

    def start(ins, outs, send, recv):
        for cp in copies(ins, outs, send, recv):
            cp.start()

    def finish(ins, outs, send, recv):
        for cp in copies(ins, outs, send, recv):
            cp.wait()

    shapes = [jax.ShapeDtypeStruct(b.dims("half"), BF16) for b in bigs]
    return _Phase(grads, shapes, {}, n, start, finish, then)


def _phase_chip_exchange(sums, bigs, then):
    n = len(sums)

    def copies(ins, outs, send, recv):
        _, _, c = _place()
        return [
            _remote(bigs[a].view(ins[a], 2 * chip[0] + chip[1], both_halves=False), outs[a].at[j], send, recv, 3 * a + j, (*chip, c))
            for j, chip in enumerate(_other_chips())
            for a in range(n)
        ]

    def start(ins, outs, send, recv):
        for cp in copies(ins, outs, send, recv):
            cp.start()

    def finish(ins, outs, send, recv):
        for cp in copies(ins, outs, send, recv):
            cp.wait()

    shapes = [jax.ShapeDtypeStruct((N_CHIPS - 1,) + b.dims("block"), BF16) for b in bigs]
    return _Phase(sums, shapes, {}, 3 * n, start, finish, then)


def _phase_pair_broadcast(stacks, bigs, batch0s, then):
    n = len(stacks)

    def start(ins, outs, send, recv):
        x, y, c = _place()
        for a in range(n):
            blk = bigs[a].view(outs[a], None, c, batch0s[a])
            _remote(blk, blk, send, recv, a, (x, y, 1 - c)).start()

    def finish(ins, outs, send, recv):
        x, y, c = _place()
        for a in range(n):
            mine = bigs[a].view(outs[a], None, c, batch0s[a])
            theirs = bigs[a].view(outs[a], None, 1 - c, batch0s[a])
            _remote(mine, mine, send, recv, a, (x, y, 1 - c)).wait_send()
            _remote(theirs, theirs, send, recv, a, (x, y, 1 - c)).wait_recv()

    return _Phase(stacks, _same(stacks), {a: a for a in range(n)}, n, start, finish, then)


def _tile_call(body, name, big, where, extra, ins, in_specs, out_specs, out_shape, phases=()):
    grid = ((extra,) if extra else ()) + big.grid
    return _call(body, name, grid, in_specs, out_specs, out_shape, ins, prefetch=(where,), phases=phases)


def _cast_into_full(w_stack, batch0, big, where, name, phases=()):
    def body(_, w_ref, o_ref):
        o_ref[...] = w_ref[...].astype(BF16)

    return _tile_call(
        body, name, big, where, 2, [w_stack], [big.spec(None, "grid", batch0=batch0)], [big.spec("pref", "grid")],
        [jax.ShapeDtypeStruct(big.dims("full"), BF16)], phases,
    )


def _pair_sum(g_full, recv_half, big, where, name, phases=()):
    def body(_, g_ref, r_ref, o_ref):
        o_ref[...] = (g_ref[...].astype(F32) + r_ref[...].astype(F32)).astype(BF16)

    half = big.spec("grid", None)
    return _tile_call(
        body, name, big, where, N_CHIPS, [g_full, recv_half], [big.spec("grid", "pref"), half], [half],
        [jax.ShapeDtypeStruct(big.dims("half"), BF16)], phases,
    )


def _chip_sum(chip_sum, parts, big, where, stack, stack_shape, batch0, name, phases=()):
    def body(_, own_ref, p_ref, *rest):
        acc = own_ref[...].astype(F32)
        for k in range(N_CHIPS - 1):
            acc = acc + p_ref[k].astype(F32)
        rest[-1][...] = acc

    ins = [chip_sum, parts] + ([stack] if stack is not None else [])
    in_specs = [big.spec("pref", None), big.spec(None, None, lead=(N_CHIPS - 1,))] + ([_ANY] if stack is not None else [])
    return _call(
        body, name, big.grid, in_specs, [big.spec(None, "pref", batch0=batch0)], [jax.ShapeDtypeStruct(stack_shape, F32)], ins,
        prefetch=(where,), phases=phases, in_place={2: 0} if stack is not None else None,
    )


def _adam_rows(w, g, m, v, batch0, nb, prev, name):
    b, r, c = w.shape
    tr = _row_tile(r, c)

    def body(w_ref, g_ref, m_ref, v_ref, *rest):
        go_ref, d_ref, mo_ref, vo_ref = rest[-4:]
        gv = g_ref[...]
        d, mo, vo = _adam(w_ref[...], gv, m_ref[...], v_ref[...])
        go_ref[...] = gv
        d_ref[...] = d
        mo_ref[...] = mo
        vo_ref[...] = vo

    spec = pl.BlockSpec((1, tr, c), lambda bb, i: (batch0 + bb, i, 0))
    ins = [w, g, m, v] + (list(prev) if prev is not None else [])
    return pl.pallas_call(
        body, name=name, grid=(nb, r // tr), in_specs=[spec] * 4 + ([_ANY] * 4 if prev is not None else []), out_specs=[spec] * 4,
        out_shape=[jax.ShapeDtypeStruct(w.shape, F32)] * 4,
        input_output_aliases={4: 0, 5: 1, 6: 2, 7: 3} if prev is not None else {},
        compiler_params=pltpu.CompilerParams(dimension_semantics=("arbitrary",) * 2, vmem_limit_bytes=VMEM_LIMIT_BYTES),
    )(*ins)


def _mod_fwd(c_all, w_mod, b_cols, phases=()):
    n_layers, d, n = w_mod.shape
    tn = _pick(n, (768, 512, 384, 256, 128))

    def body(c_ref, w_ref, b_ref, o_ref):
        cv = c_ref[...]
        ca = (cv * _sigmoid(cv)).astype(BF16)
        o_ref[0] = _dot(ca, w_ref[0].astype(BF16)) + b_ref[0]

    return _call(
        body, "mod_fwd", (n_layers, n // tn),
        [
            pl.BlockSpec((N_DEV, d), lambda l, j: (0, 0)),
            pl.BlockSpec((1, d, tn), lambda l, j: (l, 0, j)),
            pl.BlockSpec((1, 1, tn), lambda l, j: (l, 0, j)),
        ],
        [pl.BlockSpec((1, N_DEV, tn), lambda l, j: (l, 0, j))],
        [jax.ShapeDtypeStruct((n_layers, N_DEV, n), F32)], [c_all, w_mod, b_cols], phases=phases,
    )


def _mod_bwd_adam(c_all_t, dmod_cols, w, m, v, phases=()):
    n_layers, d, n = w.shape
    tn = _pick(n, (384, 256, 128))

    def body(c_ref, dm_ref, w_ref, m_ref, v_ref, g_ref, d_ref, mo_ref, vo_ref):
        cv = c_ref[...]
        ca = (cv * _sigmoid(cv)).astype(BF16)
        g = _dot(ca, dm_ref[0].astype(BF16))
        g_ref[0] = g
        dl, mo, vo = _adam(w_ref[0], g, m_ref[0], v_ref[0])
        d_ref[0] = dl
        mo_ref[0] = mo
        vo_ref[0] = vo

    wspec = pl.BlockSpec((1, d, tn), lambda l, j: (l, 0, j))
    return _call(
        body, "mod_bwd_adam", (n_layers, n // tn),
        [pl.BlockSpec((d, N_DEV), lambda l, j: (0, 0)), pl.BlockSpec((1, N_DEV, tn), lambda l, j: (l, 0, j)), wspec, wspec, wspec],
        [wspec] * 4, [jax.ShapeDtypeStruct(w.shape, F32)] * 4, [c_all_t, dmod_cols, w, m, v], phases=phases,
    )


def _ffn_fwd(x, vec, w_in, w_out, name, phases=()):
    s, d = x.shape
    f = w_out.shape[1]
    tm = _pick(s, (1024, 512, 256, 128))
    tf = _pick(f, (256, 128))
    nf = f // tf

    def body(x_ref, vec_ref, wg_ref, wu_ref, wo_ref, xo_ref, g_ref, u_ref, y_ref, h_sc, acc_sc):
        j = pl.program_id(1)

        @pl.when(j == 0)
        def _():
            h_sc[...] = _modulate(x_ref[...], vec_ref).astype(BF16)
            acc_sc[...] = jnp.zeros_like(acc_sc)

        h = h_sc[...]
        g = _dot(h, wg_ref[0])
        u = _dot(h, wu_ref[0])
        g_ref[...] = g.astype(BF16)
        u_ref[...] = u.astype(BF16)
        a = (g * _sigmoid(g) * u).astype(BF16)
        acc_sc[...] += _dot(a, wo_ref[0])

        @pl.when(j == nf - 1)
        def _():
            yv = acc_sc[...]
            xo_ref[...] = x_ref[...] + 0.5 * vec_ref[3:4, :] * yv
            y_ref[...] = yv.astype(BF16)

    row = pl.BlockSpec((tm, d), lambda i, j: (i, 0))
    hid = pl.BlockSpec((tm, tf), lambda i, j: (i, j))
    return _call(
        body, name, (s // tm, nf),
        [
            row,
            pl.BlockSpec((8, d), lambda i, j: (0, 0)),
            pl.BlockSpec((1, d, tf), lambda i, j: (0, 0, j)),
            pl.BlockSpec((1, d, tf), lambda i, j: (0, 0, nf + j)),
            pl.BlockSpec((1, tf, d), lambda i, j: (0, j, 0)),
        ],
        [row, hid, hid, row],
        [
            jax.ShapeDtypeStruct((s, d), F32),
            jax.ShapeDtypeStruct((s, f), BF16),
            jax.ShapeDtypeStruct((s, f), BF16),
            jax.ShapeDtypeStruct((s, d), BF16),
        ],
        [x, vec, w_in, w_in, w_out],
        scratch=[pltpu.VMEM((tm, d), BF16), pltpu.VMEM((tm, d), F32)], phases=phases,
    )


def _ffn_bwd(dxo, x, vec, gg, uu, y, w_in, w_out, name, phases=()):
    s, d = x.shape
    f = w_out.shape[1]
    tm = _pick(s, (512, 256, 128))
    tf = _pick(f, (256, 128))
    nf = f // tf

    def body(dxo_ref, x_ref, vec_ref, g_ref, u_ref, y_ref, wg_ref, wu_ref, wo_ref,
             dx_ref, dg_ref, du_ref, a_ref, h_ref, dy_ref, dvec_ref, acc_sc):
        i, j = pl.program_id(0), pl.program_id(1)

        @pl.when((i == 0) & (j == 0))
        def _():
            dvec_ref[...] = jnp.zeros_like(dvec_ref)

        @pl.when(j == 0)
        def _():
            dxo_v = dxo_ref[...]
            dy_ref[...] = (0.5 * vec_ref[3:4, :] * dxo_v).astype(BF16)
            dvec_ref[3:4, :] += 0.5 * jnp.sum(dxo_v * y_ref[...].astype(F32), axis=0, keepdims=True)
            acc_sc[...] = jnp.zeros_like(acc_sc)

        da = _dot_nt(dy_ref[...], wo_ref[0])
        g = g_ref[...].astype(F32)
        u = u_ref[...].astype(F32)
        sig = _sigmoid(g)
        sl = g * sig
        a_ref[...] = (sl * u).astype(BF16)
        dg = (da * u * (sig * (1.0 + g * (1.0 - sig)))).astype(BF16)
        du = (da * sl).astype(BF16)
        dg_ref[...] = dg
        du_ref[...] = du
        acc_sc[...] += _dot_nt(dg, wg_ref[0]) + _dot_nt(du, wu_ref[0])

        @pl.when(j == nf - 1)
        def _():
            dx, h = _modulate_bwd(x_ref[...], acc_sc[...], vec_ref, dvec_ref)
            dx_ref[...] = dxo_ref[...] + dx
            h_ref[...] = h.astype(BF16)

    row = pl.BlockSpec((tm, d), lambda i, j: (i, 0))
    hid = pl.BlockSpec((tm, tf), lambda i, j: (i, j))
    vecs = pl.BlockSpec((8, d), lambda i, j: (0, 0))
    return _call(
        body, name, (s // tm, nf),
        [
            row, row, vecs, hid, hid, row,
            pl.BlockSpec((1, d, tf), lambda i, j: (0, 0, j)),
            pl.BlockSpec((1, d, tf), lambda i, j: (0, 0, nf + j)),
            pl.BlockSpec((1, tf, d), lambda i, j: (0, j, 0)),
        ],
        [row, hid, hid, hid, row, row, vecs],
        [
            jax.ShapeDtypeStruct((s, d), F32),
            jax.ShapeDtypeStruct((s, f), BF16),
            jax.ShapeDtypeStruct((s, f), BF16),
            jax.ShapeDtypeStruct((s, f), BF16),
            jax.ShapeDtypeStruct((s, d), BF16),
            jax.ShapeDtypeStruct((s, d), BF16),
            jax.ShapeDtypeStruct((8, d), F32),
        ],
        [dxo, x, vec, gg, uu, y, w_in, w_in, w_out],
        scratch=[pltpu.VMEM((tm, d), F32)], phases=phases,
    )


def _grad_half(a, b, big, where, mine, col0, prev, recv, name, phases=()):
    s, k1 = a.shape
    n = b.shape[1]
    rows_halved = big.h3 == 1
    kk, nn = (k1 // 2, n) if rows_halved else (k1, n // 2)
    tk = _pick(kk, (1408, 1024, 512, 256, 128))
    tn = _pick(nn, (1408, 1024, 640, 512, 256, 128))
    nkb, nnb = kk // tk, nn // tn
    assert col0 % tn == 0 and (recv is None) == (not mine)

    def half(pref):
        return pref[1] if mine else 1 - pref[1]

    def body(_, a_ref, b_ref, *rest):
        acc = _dot_tn(a_ref[...], b_ref[...])
        if recv is not None:
            acc = acc + rest[0][0].astype(F32)
        rest[-1][0] = acc.astype(BF16)

    out_spec = pl.BlockSpec((1, tk, tn), lambda i, j, pref: (0, i, col0 // tn + j))
    in_specs = [
        pl.BlockSpec((s, tk), lambda i, j, pref: (0, i + (half(pref) * nkb if rows_halved else 0))),
        pl.BlockSpec((s, tn), lambda i, j, pref: (0, j + (0 if rows_halved else half(pref) * nnb))),
    ]
    ins = [a, b]
    if recv is not None:
        in_specs.append(out_spec)
        ins.append(recv)
    in_place = None
    if prev is not None:
        in_place = {len(ins): 0}
        in_specs.append(_ANY)
        ins.append(prev)
    return _call(
        body, name, (nkb, nnb), in_specs, [out_spec], [jax.ShapeDtypeStruct(big.dims("half"), BF16)], ins,
        prefetch=(where,), phases=phases, in_place=in_place,
    )


def _proj_mod_fwd(x, vec, w, phases=()):
    s, d = x.shape
    n = w.shape[2]
    tm = _pick(s, (512, 256, 128))
    tn = _pick(n, (640, 512, 256, 128))

    def body(x_ref, vec_ref, w_ref, o_ref, h_sc):
        @pl.when(pl.program_id(1) == 0)
        def _():
            h_sc[...] = _modulate(x_ref[...], vec_ref).astype(BF16)

        o_ref[...] = _dot(h_sc[...], w_ref[0])

    return _call(
        body, "ab_in_fwd", (s // tm, n // tn),
        [
            pl.BlockSpec((tm, d), lambda i, j: (i, 0)),
            pl.BlockSpec((8, d), lambda i, j: (0, 0)),
            pl.BlockSpec((1, d, tn), lambda i, j: (0, 0, j)),
        ],
        [pl.BlockSpec((tm, tn), lambda i, j: (i, j))],
        [jax.ShapeDtypeStruct((s, n), F32)], [x, vec, w],
        scratch=[pltpu.VMEM((tm, d), BF16)], phases=phases,
    )


def _proj_res_fwd(a, w, x, vec, phases=()):
    s, kd = a.shape
    d = x.shape[1]
    tm = _pick(s, (512, 256, 128))

    def body(a_ref, w_ref, x_ref, vec_ref, xo_ref, y_ref):
        yv = _dot(a_ref[...], w_ref[0])
        xo_ref[...] = x_ref[...] + vec_ref[3:4, :] * yv
        y_ref[...] = yv.astype(BF16)

    row = pl.BlockSpec((tm, d), lambda i: (i, 0))
    return _call(
        body, "ab_out_fwd", (s // tm,),
        [pl.BlockSpec((tm, kd), lambda i: (i, 0)), pl.BlockSpec((1, kd, d), lambda i: (0, 0, 0)), row, pl.BlockSpec((8, d), lambda i: (0, 0))],
        [row, row],
        [jax.ShapeDtypeStruct((s, d), F32), jax.ShapeDtypeStruct((s, d), BF16)], [a, w, x, vec], phases=phases,
    )


def _proj_res_bwd(dxo, y, vec, w, phases=()):
    s, d = dxo.shape
    kd = w.shape[1]
    tm = _pick(s, (512, 256, 128))

    def body(dxo_ref, y_ref, vec_ref, w_ref, dy_ref, da_ref, dgate_ref):
        @pl.when(pl.program_id(0) == 0)
        def _():
            dgate_ref[...] = jnp.zeros_like(dgate_ref)

        dxo_v = dxo_ref[...]
        dy = (vec_ref[3:4, :] * dxo_v).astype(BF16)
        dy_ref[...] = dy
        dgate_ref[3:4, :] += jnp.sum(dxo_v * y_ref[...].astype(F32), axis=0, keepdims=True)
        da_ref[...] = _dot_nt(dy, w_ref[0]).astype(BF16)

    row = pl.BlockSpec((tm, d), lambda i: (i, 0))
    vecs = pl.BlockSpec((8, d), lambda i: (0, 0))
    return _call(
        body, "ab_out_bwd", (s // tm,),
        [row, row, vecs, pl.BlockSpec((1, kd, d), lambda i: (0, 0, 0))],
        [row, pl.BlockSpec((tm, kd), lambda i: (i, 0)), vecs],
        [jax.ShapeDtypeStruct((s, d), BF16), jax.ShapeDtypeStruct((s, kd), BF16), jax.ShapeDtypeStruct((8, d), F32)],
        [dxo, y, vec, w], phases=phases,
    )


def _proj_mod_bwd(dproj, w, x, vec, dxo, dvec_in, phases=()):
    s, n = dproj.shape
    d = x.shape[1]
    tm = _pick(s, (512, 256, 128))

    def body(dp_ref, w_ref, x_ref, vec_ref, dxo_ref, dvi_ref, dx_ref, h_ref, dvec_ref):
        @pl.when(pl.program_id(0) == 0)
        def _():
            dvec_ref[...] = dvi_ref[...]

        dh = _dot_nt(dp_ref[...], w_ref[0])
        dx, h = _modulate_bwd(x_ref[...], dh, vec_ref, dvec_ref)
        dx_ref[...] = dxo_ref[...] + dx
        h_ref[...] = h.astype(BF16)

    row = pl.BlockSpec((tm, d), lambda i: (i, 0))
    vecs = pl.BlockSpec((8, d), lambda i: (0, 0))
    return _call(
        body, "ab_in_bwd", (s // tm,),
        [pl.BlockSpec((tm, n), lambda i: (i, 0)), pl.BlockSpec((1, d, n), lambda i: (0, 0, 0)), row, vecs, row, vecs],
        [row, row, vecs],
        [jax.ShapeDtypeStruct((s, d), F32), jax.ShapeDtypeStruct((s, d), BF16), jax.ShapeDtypeStruct((8, d), F32)],
        [dproj, w, x, vec, dxo, dvec_in], phases=phases,
    )


def _tril(n):
    return lax.broadcasted_iota(jnp.int32, (n, n), 0) >= lax.broadcasted_iota(jnp.int32, (n, n), 1)


def _layernorm_stats(gv):
    mu = jnp.mean(gv, axis=-1, keepdims=True)
    cen = gv - mu
    rstd = lax.rsqrt(jnp.mean(cen * cen, axis=-1, keepdims=True) + EPS)
    return cen * rstd, rstd


def _shift_down(q, k, above_ref, c_cg, c_xb, first):
    width = q.shape[1]
    rows = lax.broadcasted_iota(jnp.int32, q.shape, 0)
    out = pltpu.roll(q, k, 0)
    for r in range(k):
        src = CONV_HALO - k + r
        above = above_ref[src : src + 1, c_cg : c_cg + width] * above_ref[src : src + 1, c_xb : c_xb + width]
        above = jnp.where(first, 0.0, above)
        out = jnp.where(rows == r, above, out)
    return out


def _ab_mix_fwd(proj, norm_v, w_s, b_rows, conv_w, phases=()):
    s, n = proj.shape
    heads, chunk, _ = w_s.shape
    da = norm_v.shape[1]
    hd = da // heads
    db = conv_w.shape[1]
    tm = _pick(s, (512, 256, 128))

    def body(p_ref, ph_ref, nv_ref, ws_ref, b_ref, cw_ref, o_ref):
        first = pl.program_id(0) == 0
        gu, _ = _gelu(p_ref[:, 0:da])
        gv, _ = _gelu(p_ref[:, da : 2 * da])
        xhat, _ = _layernorm_stats(gv)
        vn = (xhat * nv_ref[...]).astype(BF16)
        mask = _tril(chunk)
        for hh in range(heads):
            wm = jnp.where(mask, ws_ref[hh], 0.0).astype(BF16)
            cols = slice(hh * hd, (hh + 1) * hd)
            for nn in range(tm // chunk):
                rows = slice(nn * chunk, (nn + 1) * chunk)
                z = _dot(wm, vn[rows, cols]) + b_ref[:, cols]
                o_ref[rows, cols] = (gu[rows, cols] * z).astype(BF16)
        c_cg, c_xb = 2 * da + db, 2 * da + 2 * db
        bg = p_ref[:, 2 * da : 2 * da + db]
        q = p_ref[:, c_cg : c_cg + db] * p_ref[:, c_xb : c_xb + db]
        q1 = _shift_down(q, 1, ph_ref, c_cg, c_xb, first)
        q2 = _shift_down(q, 2, ph_ref, c_cg, c_xb, first)
        conv = cw_ref[0:1, :] * q2 + cw_ref[1:2, :] * q1 + cw_ref[2:3, :] * q
        o_ref[:, da : da + db] = (bg * conv).astype(BF16)

    nh = tm // CONV_HALO
    return _call(
        body, "ab_mix_fwd", (s // tm,),
        [
            pl.BlockSpec((tm, n), lambda i: (i, 0)),
            pl.BlockSpec((CONV_HALO, n), lambda i: (jnp.maximum(i * nh - 1, 0), 0)),
            pl.BlockSpec((1, da), lambda i: (0, 0)),
            pl.BlockSpec((heads, chunk, chunk), lambda i: (0, 0, 0)),
            pl.BlockSpec((chunk, da), lambda i: (0, 0)),
            pl.BlockSpec((3, db), lambda i: (0, 0)),
        ],
        [pl.BlockSpec((tm, da + db), lambda i: (i, 0))],
        [jax.ShapeDtypeStruct((s, da + db), BF16)], [proj, proj, norm_v, w_s, b_rows, conv_w], phases=phases,
    )


def _ab_mix_bwd(proj, dcat, norm_v, w_s, b_rows, conv_w, phases=()):
    s, n = proj.shape
    heads, chunk, _ = w_s.shape
    da = norm_v.shape[1]
    hd = da // heads
    db = conv_w.shape[1]
    tm = _pick(s, (512, 256, 128))
    nblk = s // tm
    dhalo = 2 * CONV_HALO

    def body(p_ref, pa_ref, pb_ref, dc_ref, dcb_ref, nv_ref, ws_ref, b_ref, cw_ref,
             dp_ref, dnv_ref, dws_ref, dzs_ref, dcw_ref, dvn_sc):
        i = pl.program_id(0)
        first, last = i == 0, i == nblk - 1

        @pl.when(first)
        def _():
            dnv_ref[...] = jnp.zeros_like(dnv_ref)
            dws_ref[...] = jnp.zeros_like(dws_ref)
            dzs_ref[...] = jnp.zeros_like(dzs_ref)
            dcw_ref[...] = jnp.zeros_like(dcw_ref)

        uu = p_ref[:, 0:da]
        gu, gu_grad = _gelu(uu)
        gv, gv_grad = _gelu(p_ref[:, da : 2 * da])
        xhat, rstd = _layernorm_stats(gv)
        nv = nv_ref[...]
        vn = (xhat * nv).astype(BF16)
        dya = dc_ref[:, 0:da].astype(F32)
        dz = (dya * gu).astype(BF16)
        mask = _tril(chunk)
        for hh in range(heads):
            wm = jnp.where(mask, ws_ref[hh], 0.0).astype(BF16)
            cols = slice(hh * hd, (hh + 1) * hd)
            dws = jnp.zeros((chunk, chunk), F32)
            for nn in range(tm // chunk):
                rows = slice(nn * chunk, (nn + 1) * chunk)
                z = _dot(wm, vn[rows, cols]) + b_ref[:, cols]
                dp_ref[rows, cols] = (dya[rows, cols] * z * gu_grad[rows, cols]).astype(BF16)
                dz_blk = dz[rows, cols]
                dws = dws + _dot_nt(dz_blk, vn[rows, cols])
                dzs_ref[:, cols] += dz_blk.astype(F32)
                dvn = _dot_tn(wm, dz_blk)
                dnv_ref[:, cols] += jnp.sum(dvn * xhat[rows, cols], axis=0, keepdims=True)
                dvn_sc[rows, cols] = dvn
            dws_ref[hh] += jnp.where(mask, dws, 0.0)
        dxhat = dvn_sc[...] * nv
        dgv = rstd * (dxhat - jnp.mean(dxhat, axis=-1, keepdims=True) - xhat * jnp.mean(dxhat * xhat, axis=-1, keepdims=True))
        dp_ref[:, da : 2 * da] = (dgv * gv_grad).astype(BF16)

        c_bg, c_cg, c_xb = 2 * da, 2 * da + db, 2 * da + 2 * db
        bg = p_ref[:, c_bg : c_bg + db]
        cg = p_ref[:, c_cg : c_cg + db]
        xb = p_ref[:, c_xb : c_xb + db]
        q = cg * xb
        q1 = _shift_down(q, 1, pa_ref, c_cg, c_xb, first)
        q2 = _shift_down(q, 2, pa_ref, c_cg, c_xb, first)
        dyb = dc_ref[:, da : da + db].astype(F32)
        conv = cw_ref[0:1, :] * q2 + cw_ref[1:2, :] * q1 + cw_ref[2:3, :] * q
        dp_ref[:, c_bg : c_bg + db] = (dyb * conv).astype(BF16)
        e = dyb * bg
        dcw_ref[0:1, :] += jnp.sum(e * q2, axis=0, keepdims=True)
        dcw_ref[1:2, :] += jnp.sum(e * q1, axis=0, keepdims=True)
        dcw_ref[2:3, :] += jnp.sum(e * q, axis=0, keepdims=True)
        rows = lax.broadcasted_iota(jnp.int32, e.shape, 0)
        dq = cw_ref[2:3, :] * e
        for kk in (1, 2):
            ek = pltpu.roll(e, tm - kk, 0)
            for r in range(kk):
                below = dcb_ref[r : r + 1, da : da + db].astype(F32) * pb_ref[r : r + 1, c_bg : c_bg + db]
                below = jnp.where(last, 0.0, below)
                ek = jnp.where(rows == tm - kk + r, below, ek)
            dq = dq + cw_ref[2 - kk : 3 - kk, :] * ek
        dp_ref[:, c_cg : c_cg + db] = (dq * xb).astype(BF16)
        dp_ref[:, c_xb : c_xb + db] = (dq * cg).astype(BF16)

    nh = tm // CONV_HALO
    nhb = tm // dhalo
    const2 = lambda i: (0, 0)
    return _call(
        body, "ab_mix_bwd", (nblk,),
        [
            pl.BlockSpec((tm, n), lambda i: (i, 0)),
            pl.BlockSpec((CONV_HALO, n), lambda i: (jnp.maximum(i * nh - 1, 0), 0)),
            pl.BlockSpec((CONV_HALO, n), lambda i: (jnp.minimum((i + 1) * nh, s // CONV_HALO - 1), 0)),
            pl.BlockSpec((tm, da + db), lambda i: (i, 0)),
            pl.BlockSpec((dhalo, da + db), lambda i: (jnp.minimum((i + 1) * nhb, s // dhalo - 1), 0)),
            pl.BlockSpec((1, da), const2),
            pl.BlockSpec((heads, chunk, chunk), lambda i: (0, 0, 0)),
            pl.BlockSpec((chunk, da), const2),
            pl.BlockSpec((3, db), const2),
        ],
        [
            pl.BlockSpec((tm, n), lambda i: (i, 0)),
            pl.BlockSpec((1, da), const2),
            pl.BlockSpec((heads, chunk, chunk), lambda i: (0, 0, 0)),
            pl.BlockSpec((chunk, da), const2),
            pl.BlockSpec((3, db), const2),
        ],
        [
            jax.ShapeDtypeStruct((s, n), BF16),
            jax.ShapeDtypeStruct((1, da), F32),
            jax.ShapeDtypeStruct((heads, chunk, chunk), F32),
            jax.ShapeDtypeStruct((chunk, da), F32),
            jax.ShapeDtypeStruct((3, db), F32),
        ],
        [proj, proj, proj, dcat, dcat, norm_v, w_s, b_rows, conv_w],
        scratch=[pltpu.VMEM((tm, da), F32)], phases=phases,
    )


def _pool_counts(tm, i, w):
    t = i * tm + lax.broadcasted_iota(jnp.int32, (tm, 1), 0)
    return jnp.minimum(t + 1, w).astype(F32)


def _pool_fwd(x, vec, w_grp, scale, phases=()):
    s, d = x.shape
    groups, gd, _ = w_grp.shape
    tm = _pick(s, (512, 256, 128))

    def body(x_ref, xa_ref, vec_ref, w_ref, sc_ref, xo_ref, p_ref, o_ref):
        i = pl.program_id(0)
        h = _modulate(x_ref[...], vec_ref)
        ha = jnp.where(i == 0, 0.0, _modulate(xa_ref[...], vec_ref))
        ext = jnp.concatenate([ha, h], axis=0)
        for gi, w in enumerate(POOL_WINDOWS):
            cols = slice(gi * gd, (gi + 1) * gd)
            acc = ext[:, cols]
            step = 1
            while step < w:
                acc = acc + pltpu.roll(acc, step, 0)
                step *= 2
            p = (acc[POOL_HALO:, :] / _pool_counts(tm, i, w) - h[:, cols]).astype(BF16)
            p_ref[:, cols] = p
            o_ref[:, cols] = _dot(p, w_ref[gi]).astype(BF16)
        xo_ref[...] = x_ref[...] + vec_ref[3:4, :] * (o_ref[...].astype(F32) * sc_ref[...])

    nh = tm // POOL_HALO
    row = pl.BlockSpec((tm, d), lambda i: (i, 0))
    return _call(
        body, "pool_fwd", (s // tm,),
        [
            row,
            pl.BlockSpec((POOL_HALO, d), lambda i: (jnp.maximum(i * nh - 1, 0), 0)),
            pl.BlockSpec((8, d), lambda i: (0, 0)),
            pl.BlockSpec((groups, gd, gd), lambda i: (0, 0, 0)),
            pl.BlockSpec((1, d), lambda i: (0, 0)),
        ],
        [row, row, row],
        [jax.ShapeDtypeStruct((s, d), F32), jax.ShapeDtypeStruct((s, d), BF16), jax.ShapeDtypeStruct((s, d), BF16)],
        [x, x, vec, w_grp, scale], phases=phases,
    )


def _pool_bwd(dxo, x, vec, p, o, w_grp, scale, phases=()):
    s, d = x.shape
    groups, gd, _ = w_grp.shape
    tm = _pick(s, (512, 256, 128))
    nblk = s // tm

    def body(dxo_ref, dxb_ref, x_ref, vec_ref, p_ref, o_ref, w_ref, sc_ref, dx_ref, dw_ref, dsc_ref, dvec_ref, dw_sc):
        i = pl.program_id(0)

        @pl.when(i == 0)
        def _():
            dw_sc[...] = jnp.zeros_like(dw_sc)
            dsc_ref[...] = jnp.zeros_like(dsc_ref)
            dvec_ref[...] = jnp.zeros_like(dvec_ref)

        gate, sc = vec_ref[3:4, :], sc_ref[...]
        dxo_v = dxo_ref[...]
        ov = o_ref[...].astype(F32)
        dvec_ref[3:4, :] += jnp.sum(dxo_v * (ov * sc), axis=0, keepdims=True)
        dy = gate * dxo_v
        dsc_ref[...] += jnp.sum(dy * ov, axis=0, keepdims=True)
        dout = (dy * sc).astype(BF16)
        dout_b = jnp.where(i == nblk - 1, 0.0, gate * dxb_ref[...] * sc).astype(BF16)
        for gi, w in enumerate(POOL_WINDOWS):
            cols = slice(gi * gd, (gi + 1) * gd)
            dw_sc[gi] += _dot_tn(p_ref[:, cols], dout[:, cols])
            wb = w_ref[gi]
            dp = _dot_nt(dout[:, cols], wb)
            dp_b = _dot_nt(dout_b[:, cols], wb)
            e = dp / _pool_counts(tm, i, w)
            t_below = (i + 1) * tm + lax.broadcasted_iota(jnp.int32, (POOL_HALO, 1), 0)
            e_b = dp_b / jnp.minimum(t_below + 1, w).astype(F32)
            acc = jnp.concatenate([e, e_b], axis=0)
            step = 1
            while step < w:
                acc = acc + pltpu.roll(acc, tm + POOL_HALO - step, 0)
                step *= 2
            dx_ref[:, cols] = acc[:tm, :] - dp
        dx, _ = _modulate_bwd(x_ref[...], dx_ref[...], vec_ref, dvec_ref)
        dx_ref[...] = dxo_v + dx

        @pl.when(i == nblk - 1)
        def _():
            dw_ref[...] = dw_sc[...].astype(BF16)

    nh = tm // POOL_HALO
    row = pl.BlockSpec((tm, d), lambda i: (i, 0))
    vecs = pl.BlockSpec((8, d), lambda i: (0, 0))
    wspec = pl.BlockSpec((groups, gd, gd), lambda i: (0, 0, 0))
    return _call(
        body, "pool_bwd", (nblk,),
        [
            row,
            pl.BlockSpec((POOL_HALO, d), lambda i: (jnp.minimum((i + 1) * nh, s // POOL_HALO - 1), 0)),
            row, vecs, row, row, wspec,
            pl.BlockSpec((1, d), lambda i: (0, 0)),
        ],
        [row, wspec, pl.BlockSpec((1, d), lambda i: (0, 0)), vecs],
        [
            jax.ShapeDtypeStruct((s, d), F32),
            jax.ShapeDtypeStruct((groups, gd, gd), BF16),
            jax.ShapeDtypeStruct((1, d), F32),
            jax.ShapeDtypeStruct((8, d), F32),
        ],
        [dxo, dxo, x, vec, p, o, w_grp, scale],
        scratch=[pltpu.VMEM((groups, gd, gd), F32)], phases=phases,
    )


def _loss_head(x, gain, target, phases=()):
    s, d = x.shape
    tm = _pick(s, (512, 256, 128))

    def body(x_ref, g_ref, t_ref, dx_ref, aux_ref):
        @pl.when(pl.program_id(0) == 0)
        def _():
            aux_ref[...] = jnp.zeros_like(aux_ref)

        xv = x_ref[...]
        rstd = _rstd(xv)
        r = xv * rstd
        gain_v = g_ref[...]
        err = r * gain_v - t_ref[...]
        aux_ref[1:2, :] += jnp.sum(err * err, axis=0, keepdims=True)
        dout = err * (1.0 / d)
        aux_ref[0:1, :] += jnp.sum(dout * r, axis=0, keepdims=True)
        dr = dout * gain_v
        dx_ref[...] = rstd * (dr - r * jnp.mean(dr * r, axis=-1, keepdims=True))

    row = pl.BlockSpec((tm, d), lambda i: (i, 0))
    return _call(
        body, "loss_head", (s // tm,),
        [row, pl.BlockSpec((1, d), lambda i: (0, 0)), row],
        [row, pl.BlockSpec((8, d), lambda i: (0, 0))],
        [jax.ShapeDtypeStruct((s, d), F32), jax.ShapeDtypeStruct((8, d), F32)], [x, gain, target], phases=phases,
    )


def _small_adam(gathered, gathered_ws, layout, smalls, chip):
    names = list(smalls)
    n = len(names)

    def body(*refs):
        chip_ref, g_ref, gws_ref = refs[0], refs[1], refs[2]
        wmv = refs[3 : 3 + 3 * n]
        outs = refs[3 + 3 * n : 3 + 7 * n]
        total = refs[-1]
        total[...] = g_ref[0]
        for kdev in range(1, N_DEV):
            total[...] += g_ref[kdev]
        total_ws = gws_ref[0]
        for kdev in range(1, N_DEV):
            total_ws = total_ws + gws_ref[kdev]
        my_chip = chip_ref[0]
        for a, name in enumerate(names):
            w_ref, m_ref, v_ref = wmv[3 * a : 3 * a + 3]
            if name == "ab_w_s":
                g = total_ws
            else:
                row0, rows, col0, cols = layout[name]
                if col0 is None:
                    g = jnp.zeros((rows, cols), F32)
                    for j in range(N_CHIPS):
                        g = g + jnp.where(my_chip == j, total[row0 : row0 + rows, j * cols : (j + 1) * cols], 0.0)
                else:
                    g = total[row0 : row0 + rows, col0 : col0 + cols]
            dl, mo, vo = _adam(w_ref[...], g, m_ref[...], v_ref[...])
            outs[4 * a][...] = g
            outs[4 * a + 1][...] = dl
            outs[4 * a + 2][...] = mo
            outs[4 * a + 3][...] = vo

    ins = [gathered, gathered_ws]
    out_shapes = []
    for name in names:
        ins.extend(smalls[name])
        out_shapes.extend([jax.ShapeDtypeStruct(smalls[name][0].shape, F32)] * 4)
    whole = lambda shape: pl.BlockSpec(shape, functools.partial(lambda nd, i, c: (0,) * nd, len(shape)))
    res = pl.pallas_call(
        body, name="small_adam",
        grid_spec=pltpu.PrefetchScalarGridSpec(
            num_scalar_prefetch=1, grid=(1,),
            in_specs=[whole(a.shape) for a in ins], out_specs=[whole(o.shape) for o in out_shapes],
            scratch_shapes=[pltpu.VMEM(gathered.shape[1:], F32)],
        ),
        out_shape=out_shapes,
        compiler_params=pltpu.CompilerParams(dimension_semantics=("arbitrary",), vmem_limit_bytes=VMEM_LIMIT_BYTES),
    )(chip.reshape(1).astype(jnp.int32), *ins)
    return {name: res[4 * a : 4 * a + 4] for a, name in enumerate(names)}


def _pad_rows(a, rows=8):
    extra = (-a.shape[0]) % rows
    return jnp.pad(a, ((0, extra), (0, 0))) if extra else a


def _pad_cols(a, cols):
    return jnp.pad(a, ((0, 0), (0, cols - a.shape[1]))) if a.shape[1] < cols else a


def _run(fn, *phases):
    outs, p_outs = fn(list(phases))
    for p, po in zip(phases, p_outs):
        p.then(po)
    return outs


def kernel(x, c, norm_g, w_mod, b_mod, w_ffn_in, w_ffn_out, ab_w_in, ab_norm_v, ab_w_s, ab_b_s, ab_conv_w, ab_w_out, pool_w_grp, pool_scale, final_g, loss_target, m_norm_g, m_w_mod, m_b_mod, m_w_ffn_in, m_w_ffn_out, m_ab_w_in, m_ab_norm_v, m_ab_w_s, m_ab_b_s, m_ab_conv_w, m_ab_w_out, m_pool_w_grp, m_pool_scale, m_final_g, v_norm_g, v_w_mod, v_b_mod, v_w_ffn_in, v_w_ffn_out, v_ab_w_in, v_ab_norm_v, v_ab_w_s, v_ab_b_s, v_ab_conv_w, v_ab_w_out, v_pool_w_grp, v_pool_scale, v_final_g):
    ix, iy, ic = _place()
    chip = 2 * ix + iy
    me = 4 * ix + 2 * iy + ic
    where = jnp.stack([chip, ic]).astype(jnp.int32)
    s, d = x.shape[1], x.shape[2]
    x0 = x.reshape(s, d)
    target = loss_target.reshape(s, d)
    n_layers = norm_g.shape[0]
    dq = d // N_CHIPS
    heads, chunk = ab_w_s.shape[1], ab_w_s.shape[2]
    da = ab_norm_v.shape[1]
    db = ab_conv_w.shape[2] * N_CHIPS
    f_hidden = w_ffn_out.shape[2] * N_CHIPS
    assert n_layers == 2 and da % heads == 0

    cw_pad = _pad_cols(ab_conv_w.reshape(3, db // N_CHIPS), dq)
    packed = jnp.concatenate(
        [_pad_rows(c.reshape(N_CHIPS, dq)), _pad_rows(norm_g.reshape(-1, dq)), _pad_rows(pool_scale.reshape(1, dq)), _pad_rows(cw_pad)],
        axis=0,
    )
    ncol = w_mod.shape[2]
    b_cols = lax.dynamic_slice(b_mod, (0, chip * ncol), (n_layers, ncol)).reshape(n_layers, 1, ncol)
    small = {}

    def small_gather(key, arrs):
        def then(outs):
            small[key] = outs

        return _phase_small_gather(arrs, then)

    stacks = {
        "w_ffn_in": tuple(a.reshape((-1,) + a.shape[2:]) for a in (w_ffn_in, m_w_ffn_in, v_w_ffn_in)),
        "w_ffn_out": tuple(a.reshape((-1,) + a.shape[2:]) for a in (w_ffn_out, m_w_ffn_out, v_w_ffn_out)),
        "ab_w_in": (ab_w_in, m_ab_w_in, v_ab_w_in),
        "ab_w_out": (ab_w_out, m_ab_w_out, v_ab_w_out),
        "pool_w_grp": (pool_w_grp[0], m_pool_w_grp[0], v_pool_w_grp[0]),
    }
    big_in = _Big((1, d, 2 * f_hidden), 2, 1)
    big_out = _Big((1, f_hidden, d), 1, 2)
    units = {}
    for l in range(n_layers):
        for k in range(2):
            units[f"in{l}{k}"] = (big_in, "w_ffn_in", 2 * l + k)
            units[f"out{l}{k}"] = (big_out, "w_ffn_out", 2 * l + k)
    units["abin"] = (_Big((1, d, ab_w_in.shape[2] * N_CHIPS), 2, 1), "ab_w_in", 0)
    units["about"] = (_Big((1, ab_w_out.shape[1] * N_CHIPS, d), 1, 2), "ab_w_out", 0)
    units["pool"] = (_Big((pool_w_grp.shape[1], pool_w_grp.shape[2] * N_CHIPS, pool_w_grp.shape[3]), 1, 0), "pool_w_grp", 0)
    big = {u: g for u, (g, _, _) in units.items()}

    weight = {}
    complete = set()

    def cast(u):
        g, st, b0 = units[u]

        def launch(phases):
            (weight[u],), p_outs = _cast_into_full(stacks[st][0], b0, g, where, "cast_" + u, phases)
            return None, p_outs

        return launch

    def gather_ici(*us):
        def then(outs):
            for u, o in zip(us, outs):
                weight[u] = o

        return _phase_gather_ici([weight[u] for u in us], [big[u] for u in us], then)

    def gather_sibling(*us):
        def then(outs):
            for u, o in zip(us, outs):
                weight[u] = o
                complete.add(u)

        return _phase_gather_sibling([weight[u] for u in us], [big[u] for u in us], then)

    def w_of(u):
        assert u in complete, u
        return weight[u]

    _run(cast("in00"), small_gather("inputs", [packed]))
    _run(cast("out00"))
    small_all = small["inputs"][0]
    by_chip = small_all[0::2]
    c_all = small_all[:, 0:N_CHIPS, :].reshape(N_DEV, d)
    norm_full = by_chip[:, 8 : 8 + 3 * n_layers, :].transpose(1, 0, 2).reshape(3 * n_layers, d)
    pool_scale_full = by_chip[:, 16:17, :].transpose(1, 0, 2).reshape(1, d)
    conv_full = by_chip[:, 24:27, : db // N_CHIPS].transpose(1, 0, 2).reshape(3, db)
    mod_cols = _run(lambda phases: _mod_fwd(c_all, w_mod, b_cols, phases), gather_ici("in00"))[0]
    _run(cast("abin"), gather_sibling("in00"), gather_ici("out00"), small_gather("mod", [mod_cols.reshape(n_layers * N_DEV, ncol)]))
    _run(cast("about"), gather_sibling("out00"), gather_ici("abin"))
    _run(cast("in01"), gather_sibling("abin"), gather_ici("about"))
    _run(cast("out01"), gather_sibling("about"))
    for u in ("in10", "out10", "pool", "in11", "out11"):
        _run(cast(u))
    mod_all = small["mod"][0]
    mod_mine = lax.dynamic_index_in_dim(mod_all[0::2].reshape(N_CHIPS, n_layers, N_DEV, ncol), me, axis=2, keepdims=False)
    mod = mod_mine.transpose(1, 0, 2).reshape(n_layers, 3, 3, d)
    vecs = {
        (l, sub): _pad_rows(jnp.concatenate([norm_full[3 * l + sub][None], mod[l, sub]], axis=0))
        for l in range(n_layers)
        for sub in range(3)
    }
    b_rows = jnp.broadcast_to(ab_b_s[0].T[:, :, None], (chunk, heads, da // heads)).reshape(chunk, da)

    saved = {}

    def ffn_forward(xs, l, sub, k, *phases):
        saved[l, sub, "x"] = xs
        xs, gg, uu, yb = _run(
            lambda ph: _ffn_fwd(xs, vecs[l, sub], w_of(f"in{l}{k}"), w_of(f"out{l}{k}"), f"ffn_fwd_{l}{k}", ph), *phases
        )
        saved[l, sub, "act"] = (gg, uu, yb)
        return xs

    xs = ffn_forward(x0, 0, 0, 0, gather_ici("in01"))
    saved[0, 1, "x"] = xs
    (proj,) = _run(lambda ph: _proj_mod_fwd(xs, vecs[0, 1], w_of("abin"), ph), gather_sibling("in01"), gather_ici("out01"))
    (cat,) = _run(lambda ph: _ab_mix_fwd(proj, ab_norm_v, ab_w_s[0], b_rows, conv_full, ph), gather_sibling("out01"), gather_ici("in10"))
    xs, yb = _run(lambda ph: _proj_res_fwd(cat, w_of("about"), xs, vecs[0, 1], ph), gather_sibling("in10"), gather_ici("out10", "pool"))
    saved[0, 1, "act"] = (proj, cat, yb)
    xs = ffn_forward(xs, 0, 2, 1, gather_sibling("out10", "pool"), gather_ici("in11"))
    xs = ffn_forward(xs, 1, 0, 0, gather_sibling("in11"), gather_ici("out11"))
    saved[1, 1, "x"] = xs
    xs, pp, oo = _run(lambda ph: _pool_fwd(xs, vecs[1, 1], w_of("pool"), pool_scale_full, ph), gather_sibling("out11"))
    saved[1, 1, "act"] = (pp, oo)
    xs = ffn_forward(xs, 1, 2, 1)
    dxs, aux = _run(lambda ph: _loss_head(xs, final_g.reshape(1, d), target, ph))
    loss = lax.psum(0.5 * jnp.sum(aux[1]) / d, ("x", "y", "c"))

    grad = {}
    recv = {}
    csum = {}
    parts = {}
    reduced = {}
    done = set()
    dvecs, small_g = {}, {}

    def pair_exchange(*us):
        def then(outs):
            for u, o in zip(us, outs):
                recv[u] = o

        return _phase_pair_exchange([grad[u] for u in us], [big[u] for u in us], then)

    def grad_half(u, a, b, mine, name, *phases, col0=0, prev=None):
        (res,) = _run(lambda ph: _grad_half(a, b, big[u], where, mine, col0, prev, recv[u] if mine else None, name, ph), *phases)
        return res

    def pair_sum(u, *phases):
        def launch(ph):
            (csum[u],), p_outs = _pair_sum(grad[u], recv[u], big[u], where, "pair_sum_" + u, ph)
            return None, p_outs

        _run(launch, *phases)

    def chip_exchange(*us):
        def then(outs):
            for u, o in zip(us, outs):
                parts[u] = o

        return _phase_chip_exchange([csum[u] for u in us], [big[u] for u in us], then)

    def chip_sum(*us, carried=()):
        for n_u, u in enumerate(us):
            g, st, b0 = units[u]

            def launch(ph):
                (reduced[st],), p_outs = _chip_sum(
                    csum[u], parts[u], g, where, reduced.get(st), stacks[st][0].shape, b0, "chip_sum_" + u, ph
                )
                return None, p_outs

            _run(launch, *(carried if n_u == 0 else ()))

    def pair_broadcast(*us):
        sts = [units[u][1] for u in us]
        assert len(set(sts)) == len(sts)

        def then(outs):
            for u, st, o in zip(us, sts, outs):
                reduced[st] = o
                done.add(u)

        return _phase_pair_broadcast([reduced[st] for st in sts], [big[u] for u in us], [units[u][2] for u in us], then)

    def ffn_backward(dxs, l, sub, k, carried_bwd, carried_send, carried_mine):
        gg, uu, yb = saved[l, sub, "act"]
        w_in, w_out = w_of(f"in{l}{k}"), w_of(f"out{l}{k}")
        dxs, dg, du, a, h, dy, dvecs[l, sub] = _run(
            lambda ph: _ffn_bwd(dxs, saved[l, sub, "x"], vecs[l, sub], gg, uu, yb, w_in, w_out, f"ffn_bwd_{l}{k}", ph), *carried_bwd()
        )
        uo, ui, tag = f"out{l}{k}", f"in{l}{k}", f"{l}{k}"
        grad[uo] = grad_half(uo, a, dy, False, "dw_out_send_" + tag, *carried_send())
        part = grad_half(ui, h, du, False, "dw_in_u_send_" + tag, pair_exchange(uo), col0=f_hidden)
        grad[ui] = grad_half(ui, h, dg, False, "dw_in_g_send_" + tag, prev=part)
        csum[uo] = grad_half(uo, a, dy, True, "dw_out_" + tag, pair_exchange(ui))
        part = grad_half(ui, h, du, True, "dw_in_u_" + tag, *carried_mine(), col0=f_hidden)
        csum[ui] = grad_half(ui, h, dg, True, "dw_in_g_" + tag, prev=part)
        return dxs

    none = lambda: ()
    dxs = ffn_backward(dxs, 1, 2, 1, none, none, none)
    pp, oo = saved[1, 1, "act"]
    dxs, grad["pool"], small_g["pool_scale"], dvecs[1, 1] = _run(
        lambda ph: _pool_bwd(dxs, saved[1, 1, "x"], vecs[1, 1], pp, oo, w_of("pool"), pool_scale_full, ph)
    )

    def after_11():
        return (chip_exchange("in11", "out11"), pair_exchange("pool"))

    def bcast_11():
        chip_sum("in11", "out11")
        pair_sum("pool")
        return (pair_broadcast("in11", "out11"), chip_exchange("pool"))

    dxs = ffn_backward(dxs, 1, 0, 0, after_11, bcast_11, none)

    def after_10():
        return (chip_exchange("in10", "out10"),)

    def bcast_10():
        chip_sum("in10", "out10", "pool")
        return (pair_broadcast("in10", "out10", "pool"),)

    dxs = ffn_backward(dxs, 0, 2, 1, after_10, bcast_10, none)

    proj, cat, yb = saved[0, 1, "act"]
    dy, dcat, dgate = _run(lambda ph: _proj_res_bwd(dxs, yb, vecs[0, 1], w_of("about"), ph))
    grad["about"] = grad_half("about", cat, dy, False, "dw_ab_out_send")
    dproj, small_g["ab_norm_v"], small_g["ab_w_s"], dzs, small_g["ab_conv_w"] = _run(
        lambda ph: _ab_mix_bwd(proj, dcat, ab_norm_v, ab_w_s[0], b_rows, conv_full, ph), chip_exchange("out01"), pair_exchange("about")
    )
    small_g["ab_b_s"] = dzs.reshape(chunk, heads, da // heads).sum(axis=2).T
    dxs, h, dvecs[0, 1] = _run(lambda ph: _proj_mod_bwd(dproj, w_of("abin"), saved[0, 1, "x"], vecs[0, 1], dxs, dgate, ph))
    grad["abin"] = grad_half("abin", h, dproj, False, "dw_ab_in_send")
    chip_sum("out01", carried=(pair_exchange("abin"),))
    csum["about"] = grad_half("about", cat, dy, True, "dw_ab_out", pair_broadcast("out01"))
    csum["abin"] = grad_half("abin", h, dproj, True, "dw_ab_in")

    def after_01():
        return (chip_exchange("in01", "abin", "about"),)

    def bcast_01():
        chip_sum("in01", "abin", "about")
        return (pair_broadcast("in01", "abin", "about"),)

    def reduce_out00():
        return (chip_exchange("out00"),)

    dxs = ffn_backward(dxs, 0, 0, 0, after_01, bcast_01, reduce_out00)
    grad_x = dxs.reshape(x.shape)

    dgain = jnp.stack([dvecs[l, sub][0] for l in range(n_layers) for sub in range(3)])
    dmod = jnp.concatenate([dvecs[l, sub][1:4] for l in range(n_layers) for sub in range(3)], axis=0)
    pieces = {
        "norm_g": (dgain, None, dq), "final_g": (aux[0:1], 0, d), "pool_scale": (small_g["pool_scale"], None, dq),
        "b_mod": (dmod, 0, d), "ab_norm_v": (small_g["ab_norm_v"], 0, da), "ab_conv_w": (small_g["ab_conv_w"], None, db // N_CHIPS),
        "ab_b_s": (small_g["ab_b_s"], 0, chunk),
    }
    layout, row0 = {}, 0
    for nm, (pc, col0, cols) in pieces.items():
        layout[nm] = (row0, pc.shape[0], col0, cols)
        row0 += pc.shape[0]
    packed_rows = -(-row0 // 8) * 8
    packed_g = sum(
        jnp.pad(pc, ((layout[nm][0], packed_rows - layout[nm][0] - pc.shape[0]), (0, d - pc.shape[1])))
        for nm, (pc, _, _) in pieces.items()
    )

    chip_sum("out00")
    _flush(
        "reduce_last", chip_exchange("in00"), pair_broadcast("out00"),
        small_gather("grads", [packed_g, small_g["ab_w_s"].reshape(heads * chunk, chunk)]),
    )
    chip_sum("in00")
    _flush("broadcast_last", pair_broadcast("in00"))
    g_all, gws_all = small["grads"]

    assert done == set(units)
    out = {}
    for st, (w3, m3, v3) in stacks.items():
        shape = {"w_ffn_in": w_ffn_in.shape, "w_ffn_out": w_ffn_out.shape, "pool_w_grp": pool_w_grp.shape}.get(st, w3.shape)
        out[st] = tuple(a.reshape(shape) for a in _adam_rows(w3, reduced[st], m3, v3, 0, w3.shape[0], None, "adam_" + st))

    shapes2d = {
        "norm_g": (3 * n_layers, dq), "b_mod": (9 * n_layers, d), "final_g": (1, d), "ab_norm_v": (1, da),
        "pool_scale": (1, dq), "ab_conv_w": (3, db // N_CHIPS), "ab_b_s": (heads, chunk), "ab_w_s": (heads * chunk, chunk),
    }
    small_w = {"norm_g": (norm_g, m_norm_g, v_norm_g), "b_mod": (b_mod, m_b_mod, v_b_mod), "final_g": (final_g, m_final_g, v_final_g),
               "ab_norm_v": (ab_norm_v, m_ab_norm_v, v_ab_norm_v), "pool_scale": (pool_scale, m_pool_scale, v_pool_scale),
               "ab_conv_w": (ab_conv_w, m_ab_conv_w, v_ab_conv_w), "ab_b_s": (ab_b_s, m_ab_b_s, v_ab_b_s), "ab_w_s": (ab_w_s, m_ab_w_s, v_ab_w_s)}
    smalls = {nm: tuple(a.reshape(shapes2d[nm]) for a in wmv) for nm, wmv in small_w.items()}
    small_out = _small_adam(g_all, gws_all, layout, smalls, chip)
    for nm, res in small_out.items():
        out[nm] = tuple(a.reshape(small_w[nm][0].shape) for a in res)

    mod_row0 = layout["b_mod"][0]
    dmod_all = g_all[:, mod_row0 : mod_row0 + 9 * n_layers, :].reshape(N_DEV, n_layers, 9 * d)
    dmod_cols = lax.dynamic_slice(dmod_all, (0, 0, chip * ncol), (N_DEV, n_layers, ncol)).transpose(1, 0, 2)
    out["w_mod"] = tuple(_run(lambda ph: _mod_bwd_adam(c_all.T, dmod_cols, w_mod, m_w_mod, v_w_mod, ph)))

    order = ["norm_g", "w_mod", "b_mod", "w_ffn_in", "w_ffn_out", "ab_w_in", "ab_norm_v", "ab_w_s", "ab_b_s", "ab_conv_w", "ab_w_out", "pool_w_grp", "pool_scale", "final_g"]
    return (loss, grad_x, *[out[nm][0] for nm in order], *[out[nm][1] for nm in order], *[out[nm][2] for nm in order], *[out[nm][3] for nm in order])
```

```python
import functools
import math

import jax
import jax.numpy as jnp
from jax import lax
from jax.experimental import pallas as pl
from jax.experimental.pallas import tpu as pltpu

F32 = jnp.float32
BF16 = jnp.bfloat16
MESH = pl.DeviceIdType.MESH

EPS = 1e-6
ADAM_LR = 0.001
ADAM_B1 = 0.9
ADAM_B2 = 0.999
ADAM_EPS = 1e-08
ADAM_WD = 0.01
ADAM_STEP = 10
POOL_WINDOWS = (2, 4, 8, 16)
POOL_HALO = 16
CONV_HALO = 8
N_CHIPS = 4
N_DEV = 8
VMEM_LIMIT_BYTES = 48 * 1024 * 1024
EW_BLOCK_ELEMS = 256 * 1024


def _pick(n, prefs):
    for p in prefs:
        if p <= n and n % p == 0:
            return p
    return n


def _row_tile(rows, cols):
    best = None
    for d in range(16, rows + 1, 16):
        if rows % d == 0 and d * cols <= EW_BLOCK_ELEMS:
            best = d
    return best or rows


def _dot(a, b):
    return jnp.dot(a, b, preferred_element_type=F32)


def _dot_nt(a, b):
    return lax.dot_general(a, b, (((1,), (1,)), ((), ())), preferred_element_type=F32)


def _dot_tn(a, b):
    return lax.dot_general(a, b, (((0,), (0,)), ((), ())), preferred_element_type=F32)


def _sigmoid(x):
    return 1.0 / (1.0 + jnp.exp(-x))


_GELU_C = math.sqrt(2.0 / math.pi)


def _gelu(x):
    x2 = x * x
    t = jnp.tanh(_GELU_C * (x + 0.044715 * x2 * x))
    val = 0.5 * x * (1.0 + t)
    grad = 0.5 * (1.0 + t) + 0.5 * x * (1.0 - t * t) * (_GELU_C * (1.0 + 3.0 * 0.044715 * x2))
    return val, grad


def _rstd(x):
    return lax.rsqrt(jnp.mean(x * x, axis=-1, keepdims=True) + EPS)


def _modulate(x, vec_ref):
    return (x * _rstd(x)) * vec_ref[0:1, :] * (1.0 + vec_ref[2:3, :]) + vec_ref[1:2, :]


def _modulate_bwd(x, dh, vec_ref, dvec_ref):
    gn, sh, sc = vec_ref[0:1, :], vec_ref[1:2, :], vec_ref[2:3, :]
    rstd = _rstd(x)
    r = x * rstd
    dvec_ref[0:1, :] += jnp.sum(dh * r * (1.0 + sc), axis=0, keepdims=True)
    dvec_ref[1:2, :] += jnp.sum(dh, axis=0, keepdims=True)
    dvec_ref[2:3, :] += jnp.sum(dh * r * gn, axis=0, keepdims=True)
    gm = gn * (1.0 + sc)
    dr = dh * gm
    dx = rstd * (dr - r * jnp.mean(dr * r, axis=-1, keepdims=True))
    return dx, r * gm + sh


def _adam(w, g, m, v):
    m = ADAM_B1 * m + (1.0 - ADAM_B1) * g
    v = ADAM_B2 * v + (1.0 - ADAM_B2) * (g * g)
    m_hat = m / (1.0 - ADAM_B1**ADAM_STEP)
    v_hat = v / (1.0 - ADAM_B2**ADAM_STEP)
    delta = -ADAM_LR * (m_hat / (jnp.sqrt(v_hat) + ADAM_EPS) + ADAM_WD * w)
    return delta, m, v


_ANY = pl.BlockSpec(memory_space=pl.ANY)


class _Phase:
    def __init__(self, ins, out_shapes, aliases, n_sems, start, finish, then):
        self.ins, self.out_shapes, self.aliases, self.n_sems = list(ins), list(out_shapes), dict(aliases), n_sems
        self.start, self.finish, self.then = start, finish, then


def _call(body, name, grid, in_specs, out_specs, out_shape, ins, scratch=(), prefetch=(), phases=(), in_place=None):
    n_pre, n_in, n_out, n_sc = len(prefetch), len(in_specs), len(out_specs), len(scratch)
    ph_in = [len(p.ins) for p in phases]
    ph_out = [len(p.out_shapes) for p in phases]

    def kernel_body(*refs):
        pos = [0]

        def take(k):
            pos[0] += k
            return refs[pos[0] - k : pos[0]]

        pre, ins_ = take(n_pre), take(n_in)
        p_ins = [take(k) for k in ph_in]
        outs_ = take(n_out)
        p_outs = [take(k) for k in ph_out]
        sc = take(n_sc)
        sems = [take(2) for _ in phases]
        if phases:
            ids = [pl.program_id(a) for a in range(len(grid))]
            first = functools.reduce(jnp.logical_and, [i == 0 for i in ids])
            last = functools.reduce(jnp.logical_and, [i == g - 1 for i, g in zip(ids, grid)])

            @pl.when(first)
            def _():
                for p, pi, po, (send, recv) in zip(phases, p_ins, p_outs, sems):
                    p.start(pi, po, send, recv)

        if body is not None:
            body(*pre, *ins_, *outs_, *sc)
        if phases:

            @pl.when(last)
            def _():
                for p, pi, po, (send, recv) in zip(phases, p_ins, p_outs, sems):
                    p.finish(pi, po, send, recv)

    aliases = {n_pre + i: o for i, o in (in_place or {}).items()}
    i0, o0 = n_pre + n_in, n_out
    for p in phases:
        for i, o in p.aliases.items():
            aliases[i0 + i] = o0 + o
        i0 += len(p.ins)
        o0 += len(p.out_shapes)
    all_in = list(in_specs) + [_ANY] * sum(ph_in)
    all_out = list(out_specs) + [_ANY] * sum(ph_out)
    all_scratch = list(scratch)
    for p in phases:
        all_scratch += [pltpu.SemaphoreType.DMA((p.n_sems,)), pltpu.SemaphoreType.DMA((p.n_sems,))]
    shapes = list(out_shape) + [s for p in phases for s in p.out_shapes]
    operands = list(prefetch) + list(ins) + [a for p in phases for a in p.ins]
    sem = ("arbitrary",) * len(grid)
    params = pltpu.CompilerParams(dimension_semantics=sem, vmem_limit_bytes=VMEM_LIMIT_BYTES)
    if n_pre:
        res = pl.pallas_call(
            kernel_body, name=name, out_shape=shapes, input_output_aliases=aliases, compiler_params=params,
            grid_spec=pltpu.PrefetchScalarGridSpec(
                num_scalar_prefetch=n_pre, grid=grid, in_specs=all_in, out_specs=all_out, scratch_shapes=all_scratch
            ),
        )(*operands)
    else:
        res = pl.pallas_call(
            kernel_body, name=name, grid=grid, in_specs=all_in, out_specs=all_out, out_shape=shapes,
            scratch_shapes=all_scratch, input_output_aliases=aliases, compiler_params=params,
        )(*operands)
    res = list(res)
    outs, rest = res[:n_out], res[n_out:]
    p_res = []
    for k in ph_out:
        p_res.append(rest[:k])
        rest = rest[k:]
    return outs, p_res


def _place():
    return lax.axis_index("x"), lax.axis_index("y"), lax.axis_index("c")


def _other_chips():
    x, y, _ = _place()
    return [(1 - x, y), (x, 1 - y), (1 - x, 1 - y)]


def _flip(k):
    x, y, c = _place()
    return (1 - x if k & 4 else x, 1 - y if k & 2 else y, 1 - c if k & 1 else c)


def _remote(src, dst, send, recv, k, to):
    return pltpu.make_async_remote_copy(
        src_ref=src, dst_ref=dst, send_sem=send.at[k], recv_sem=recv.at[k], device_id=to, device_id_type=MESH
    )


def _phase_small_gather(arrs, then):
    n = len(arrs)

    def copies(ins, outs, send, recv):
        x, y, c = _place()
        me = 4 * x + 2 * y + c
        local = [pltpu.make_async_copy(ins[a], outs[a].at[me], send.at[a * N_DEV]) for a in range(n)]
        remote = [_remote(ins[a], outs[a].at[me], send, recv, a * N_DEV + k, _flip(k)) for a in range(n) for k in range(1, N_DEV)]
        return local, remote

    def start(ins, outs, send, recv):
        local, remote = copies(ins, outs, send, recv)
        for cp in local + remote:
            cp.start()

    def finish(ins, outs, send, recv):
        local, remote = copies(ins, outs, send, recv)
        for cp in remote + local:
            cp.wait()

    shapes = [jax.ShapeDtypeStruct((N_DEV,) + a.shape, a.dtype) for a in arrs]
    return _Phase(arrs, shapes, {}, n * N_DEV, start, finish, then)


def _flush(name, *phases):
    _, p_outs = _call(None, name, (1,), [], [], [], [], phases=list(phases))
    for p, po in zip(phases, p_outs):
        p.then(po)


class _Big:
    KINDS = {"full": (True, True), "half": (True, False), "shard": (False, True), "block": (False, False)}

    def __init__(self, f3, s3, h3):
        assert s3 != h3
        self.f3, self.s3, self.h3 = tuple(f3), s3, h3
        self.bd = tuple(f3[a] // (N_CHIPS if a == s3 else 1) // (2 if a == h3 else 1) for a in range(3))
        self.tile = (1, _row_tile(self.bd[1], self.bd[2]), self.bd[2])
        self.grid = tuple(self.bd[a] // self.tile[a] for a in range(3))

    def dims(self, kind):
        chips, halves = self.KINDS[kind]
        return tuple(
            self.bd[a] * (N_CHIPS if chips and a == self.s3 else 1) * (2 if halves and a == self.h3 else 1) for a in range(3)
        )

    def view(self, ref, chip=None, half=None, batch0=0, both_halves=True):
        start = [batch0, 0, 0]
        size = list(ref.shape)
        size[0] = self.bd[0] * (2 if self.h3 == 0 and both_halves else 1)
        if chip is not None:
            start[self.s3] += chip * self.bd[self.s3]
            size[self.s3] = self.bd[self.s3]
        if half is not None:
            start[self.h3] += half * self.bd[self.h3]
            size[self.h3] = self.bd[self.h3]
        return ref.at[tuple(pl.ds(st, sz) for st, sz in zip(start, size))]

    def spec(self, chip_from=None, half_from=None, lead=(), batch0=0):
        extra = "grid" in (chip_from, half_from)

        def index(*args):
            pref, idx = args[-1], list(args[int(extra) : -1])
            idx[0] += batch0
            if chip_from:
                idx[self.s3] += (pref[0] if chip_from == "pref" else args[0]) * self.grid[self.s3]
            if half_from:
                idx[self.h3] += (pref[1] if half_from == "pref" else args[0]) * self.grid[self.h3]
            return (0,) * len(lead) + tuple(idx)

        return pl.BlockSpec(tuple(lead) + self.tile, index)


def _same(arrs):
    return [jax.ShapeDtypeStruct(a.shape, a.dtype) for a in arrs]


def _phase_gather_ici(arrs, bigs, then):
    n = len(arrs)

    def copies(outs, send, recv, arriving):
        x, y, c = _place()
        return [
            _remote(blk, blk, send, recv, 3 * a + j, (*chip, c))
            for j, chip in enumerate(_other_chips())
            for a in range(n)
            for blk in [bigs[a].view(outs[a], 2 * chip[0] + chip[1] if arriving else 2 * x + y, c)]
        ]

    def start(ins, outs, send, recv):
        for cp in copies(outs, send, recv, False):
            cp.start()

    def finish(ins, outs, send, recv):
        for cp in copies(outs, send, recv, True):
            cp.wait_recv()
        for cp in copies(outs, send, recv, False):
            cp.wait_send()

    return _Phase(arrs, _same(arrs), {a: a for a in range(n)}, 3 * n, start, finish, then)


def _phase_gather_sibling(arrs, bigs, then):
    n = len(arrs)

    def copies(outs, send, recv, arriving):
        x, y, c = _place()
        return [
            _remote(blk, blk, send, recv, 3 * a + j, (x, y, 1 - c))
            for j, chip in enumerate(_other_chips())
            for a in range(n)
            for blk in [bigs[a].view(outs[a], 2 * chip[0] + chip[1], 1 - c if arriving else c)]
        ]

    def start(ins, outs, send, recv):
        for cp in copies(outs, send, recv, False):
            cp.start()

    def finish(ins, outs, send, recv):
        for cp in copies(outs, send, recv, True):
            cp.wait_recv()
        for cp in copies(outs, send, recv, False):
            cp.wait_send()

    return _Phase(arrs, _same(arrs), {a: a for a in range(n)}, 3 * n, start, finish, then)


def _phase_pair_exchange(grads, bigs, then):
    n = len(grads)

    def copies(ins, outs, send, recv):
        x, y, c = _place()
        srcs = [ins[a] if ins[a].shape == outs[a].shape else bigs[a].view(ins[a], None, 1 - c) for a in range(n)]
        return [_remote(srcs[a], outs[a], send, recv, a, (x, y, 1 - c)) for a in range(n)]

    def start(ins, outs, send, recv):
        for cp in copies(ins, outs, send, recv):
            cp.start()

    def finish(ins, outs, send, recv):
        for cp in copies(ins, outs, send, recv):
            cp.wait()

    shapes = [jax.ShapeDtypeStruct(b.dims("half"), BF16) for b in bigs]
    return _Phase(grads, shapes, {}, n, start, finish, then)


def _phase_chip_exchange(sums, bigs, then):
    n = len(sums)

    def copies(ins, outs, send, recv):
        _, _, c = _place()
        return [
            _remote(bigs[a].view(ins[a], 2 * chip[0] + chip[1], both_halves=False), outs[a].at[j], send, recv, 3 * a + j, (*chip, c))
            for j, chip in enumerate(_other_chips())
            for a in range(n)
        ]

    def start(ins, outs, send, recv):
        for cp in copies(ins, outs, send, recv):
            cp.start()

    def finish(ins, outs, send, recv):
        for cp in copies(ins, outs, send, recv):
            cp.wait()

    shapes = [jax.ShapeDtypeStruct((N_CHIPS - 1,) + b.dims("block"), BF16) for b in bigs]
    return _Phase(sums, shapes, {}, 3 * n, start, finish, then)


def _phase_pair_broadcast(stacks, bigs, batch0s, then):
    n = len(stacks)

    def start(ins, outs, send, recv):
        x, y, c = _place()
        for a in range(n):
            blk = bigs[a].view(outs[a], None, c, batch0s[a])
            _remote(blk, blk, send, recv, a, (x, y, 1 - c)).start()

    def finish(ins, outs, send, recv):
        x, y, c = _place()
        for a in range(n):
            mine = bigs[a].view(outs[a], None, c, batch0s[a])
            theirs = bigs[a].view(outs[a], None, 1 - c, batch0s[a])
            _remote(mine, mine, send, recv, a, (x, y, 1 - c)).wait_send()
            _remote(theirs, theirs, send, recv, a, (x, y, 1 - c)).wait_recv()

    return _Phase(stacks, _same(stacks), {a: a for a in range(n)}, n, start, finish, then)


def _tile_call(body, name, big, where, extra, ins, in_specs, out_specs, out_shape, phases=()):
    grid = ((extra,) if extra else ()) + big.grid
    return _call(body, name, grid, in_specs, out_specs, out_shape, ins, prefetch=(where,), phases=phases)


def _cast_into_full(w_stack, batch0, big, where, name, phases=()):
    def body(_, w_ref, o_ref):
        o_ref[...] = w_ref[...].astype(BF16)

    return _tile_call(
        body, name, big, where, 2, [w_stack], [big.spec(None, "grid", batch0=batch0)], [big.spec("pref", "grid")],
        [jax.ShapeDtypeStruct(big.dims("full"), BF16)], phases,
    )


def _pair_sum(g_full, recv_half, big, where, name, phases=()):
    def body(_, g_ref, r_ref, o_ref):
        o_ref[...] = (g_ref[...].astype(F32) + r_ref[...].astype(F32)).astype(BF16)

    half = big.spec("grid", None)
    return _tile_call(
        body, name, big, where, N_CHIPS, [g_full, recv_half], [big.spec("grid", "pref"), half], [half],
        [jax.ShapeDtypeStruct(big.dims("half"), BF16)], phases,
    )


def _chip_sum(chip_sum, parts, big, where, stack, stack_shape, batch0, name, phases=()):
    def body(_, own_ref, p_ref, *rest):
        acc = own_ref[...].astype(F32)
        for k in range(N_CHIPS - 1):
            acc = acc + p_ref[k].astype(F32)
        rest[-1][...] = acc

    ins = [chip_sum, parts] + ([stack] if stack is not None else [])
    in_specs = [big.spec("pref", None), big.spec(None, None, lead=(N_CHIPS - 1,))] + ([_ANY] if stack is not None else [])
    return _call(
        body, name, big.grid, in_specs, [big.spec(None, "pref", batch0=batch0)], [jax.ShapeDtypeStruct(stack_shape, F32)], ins,
        prefetch=(where,), phases=phases, in_place={2: 0} if stack is not None else None,
    )


def _adam_rows(w, g, m, v, batch0, nb, prev, name):
    b, r, c = w.shape
    tr = _row_tile(r, c)

    def body(w_ref, g_ref, m_ref, v_ref, *rest):
        go_ref, d_ref, mo_ref, vo_ref = rest[-4:]
        gv = g_ref[...]
        d, mo, vo = _adam(w_ref[...], gv, m_ref[...], v_ref[...])
        go_ref[...] = gv
        d_ref[...] = d
        mo_ref[...] = mo
        vo_ref[...] = vo

    spec = pl.BlockSpec((1, tr, c), lambda bb, i: (batch0 + bb, i, 0))
    ins = [w, g, m, v] + (list(prev) if prev is not None else [])
    return pl.pallas_call(
        body, name=name, grid=(nb, r // tr), in_specs=[spec] * 4 + ([_ANY] * 4 if prev is not None else []), out_specs=[spec] * 4,
        out_shape=[jax.ShapeDtypeStruct(w.shape, F32)] * 4,
        input_output_aliases={4: 0, 5: 1, 6: 2, 7: 3} if prev is not None else {},
        compiler_params=pltpu.CompilerParams(dimension_semantics=("arbitrary",) * 2, vmem_limit_bytes=VMEM_LIMIT_BYTES),
    )(*ins)


def _mod_fwd(c_all, w_mod, b_cols, phases=()):
    n_layers, d, n = w_mod.shape
    tn = _pick(n, (768, 512, 384, 256, 128))

    def body(c_ref, w_ref, b_ref, o_ref):
        cv = c_ref[...]
        ca = (cv * _sigmoid(cv)).astype(BF16)
        o_ref[0] = _dot(ca, w_ref[0].astype(BF16)) + b_ref[0]

    return _call(
        body, "mod_fwd", (n_layers, n // tn),
        [
            pl.BlockSpec((N_DEV, d), lambda l, j: (0, 0)),
            pl.BlockSpec((1, d, tn), lambda l, j: (l, 0, j)),
            pl.BlockSpec((1, 1, tn), lambda l, j: (l, 0, j)),
        ],
        [pl.BlockSpec((1, N_DEV, tn), lambda l, j: (l, 0, j))],
        [jax.ShapeDtypeStruct((n_layers, N_DEV, n), F32)], [c_all, w_mod, b_cols], phases=phases,
    )


def _mod_bwd_adam(c_all_t, dmod_cols, w, m, v, phases=()):
    n_layers, d, n = w.shape
    tn = _pick(n, (384, 256, 128))

    def body(c_ref, dm_ref, w_ref, m_ref, v_ref, g_ref, d_ref, mo_ref, vo_ref):
        cv = c_ref[...]
        ca = (cv * _sigmoid(cv)).astype(BF16)
        g = _dot(ca, dm_ref[0].astype(BF16))
        g_ref[0] = g
        dl, mo, vo = _adam(w_ref[0], g, m_ref[0], v_ref[0])
        d_ref[0] = dl
        mo_ref[0] = mo
        vo_ref[0] = vo

    wspec = pl.BlockSpec((1, d, tn), lambda l, j: (l, 0, j))
    return _call(
        body, "mod_bwd_adam", (n_layers, n // tn),
        [pl.BlockSpec((d, N_DEV), lambda l, j: (0, 0)), pl.BlockSpec((1, N_DEV, tn), lambda l, j: (l, 0, j)), wspec, wspec, wspec],
        [wspec] * 4, [jax.ShapeDtypeStruct(w.shape, F32)] * 4, [c_all_t, dmod_cols, w, m, v], phases=phases,
    )


def _ffn_fwd(x, vec, w_in, w_out, name, phases=()):
    s, d = x.shape
    f = w_out.shape[1]
    tm = _pick(s, (1024, 512, 256, 128))
    tf = _pick(f, (256, 128))
    nf = f // tf

    def body(x_ref, vec_ref, wg_ref, wu_ref, wo_ref, xo_ref, g_ref, u_ref, y_ref, h_sc, acc_sc):
        j = pl.program_id(1)

        @pl.when(j == 0)
        def _():
            h_sc[...] = _modulate(x_ref[...], vec_ref).astype(BF16)
            acc_sc[...] = jnp.zeros_like(acc_sc)

        h = h_sc[...]
        g = _dot(h, wg_ref[0])
        u = _dot(h, wu_ref[0])
        g_ref[...] = g.astype(BF16)
        u_ref[...] = u.astype(BF16)
        a = (g * _sigmoid(g) * u).astype(BF16)
        acc_sc[...] += _dot(a, wo_ref[0])

        @pl.when(j == nf - 1)
        def _():
            yv = acc_sc[...]
            xo_ref[...] = x_ref[...] + 0.5 * vec_ref[3:4, :] * yv
            y_ref[...] = yv.astype(BF16)

    row = pl.BlockSpec((tm, d), lambda i, j: (i, 0))
    hid = pl.BlockSpec((tm, tf), lambda i, j: (i, j))
    return _call(
        body, name, (s // tm, nf),
        [
            row,
            pl.BlockSpec((8, d), lambda i, j: (0, 0)),
            pl.BlockSpec((1, d, tf), lambda i, j: (0, 0, j)),
            pl.BlockSpec((1, d, tf), lambda i, j: (0, 0, nf + j)),
            pl.BlockSpec((1, tf, d), lambda i, j: (0, j, 0)),
        ],
        [row, hid, hid, row],
        [
            jax.ShapeDtypeStruct((s, d), F32),
            jax.ShapeDtypeStruct((s, f), BF16),
            jax.ShapeDtypeStruct((s, f), BF16),
            jax.ShapeDtypeStruct((s, d), BF16),
        ],
        [x, vec, w_in, w_in, w_out],
        scratch=[pltpu.VMEM((tm, d), BF16), pltpu.VMEM((tm, d), F32)], phases=phases,
    )


def _ffn_bwd(dxo, x, vec, gg, uu, y, w_in, w_out, name, phases=()):
    s, d = x.shape
    f = w_out.shape[1]
    tm = _pick(s, (512, 256, 128))
    tf = _pick(f, (256, 128))
    nf = f // tf

    def body(dxo_ref, x_ref, vec_ref, g_ref, u_ref, y_ref, wg_ref, wu_ref, wo_ref,
             dx_ref, dg_ref, du_ref, a_ref, h_ref, dy_ref, dvec_ref, acc_sc):
        i, j = pl.program_id(0), pl.program_id(1)

        @pl.when((i == 0) & (j == 0))
        def _():
            dvec_ref[...] = jnp.zeros_like(dvec_ref)

        @pl.when(j == 0)
        def _():
            dxo_v = dxo_ref[...]
            dy_ref[...] = (0.5 * vec_ref[3:4, :] * dxo_v).astype(BF16)
            dvec_ref[3:4, :] += 0.5 * jnp.sum(dxo_v * y_ref[...].astype(F32), axis=0, keepdims=True)
            acc_sc[...] = jnp.zeros_like(acc_sc)

        da = _dot_nt(dy_ref[...], wo_ref[0])
        g = g_ref[...].astype(F32)
        u = u_ref[...].astype(F32)
        sig = _sigmoid(g)
        sl = g * sig
        a_ref[...] = (sl * u).T.astype(BF16)
        dg = (da * u * (sig * (1.0 + g * (1.0 - sig)))).astype(BF16)
        du = (da * sl).astype(BF16)
        dg_ref[...] = dg
        du_ref[...] = du
        acc_sc[...] += _dot_nt(dg, wg_ref[0]) + _dot_nt(du, wu_ref[0])

        @pl.when(j == nf - 1)
        def _():
            dx, h = _modulate_bwd(x_ref[...], acc_sc[...], vec_ref, dvec_ref)
            dx_ref[...] = dxo_ref[...] + dx
            h_ref[...] = h.T.astype(BF16)

    row = pl.BlockSpec((tm, d), lambda i, j: (i, 0))
    hid = pl.BlockSpec((tm, tf), lambda i, j: (i, j))
    vecs = pl.BlockSpec((8, d), lambda i, j: (0, 0))
    return _call(
        body, name, (s // tm, nf),
        [
            row, row, vecs, hid, hid, row,
            pl.BlockSpec((1, d, tf), lambda i, j: (0, 0, j)),
            pl.BlockSpec((1, d, tf), lambda i, j: (0, 0, nf + j)),
            pl.BlockSpec((1, tf, d), lambda i, j: (0, j, 0)),
        ],
        [row, hid, hid, pl.BlockSpec((tf, tm), lambda i, j: (j, i)), pl.BlockSpec((d, tm), lambda i, j: (0, i)), row, vecs],
        [
            jax.ShapeDtypeStruct((s, d), F32),
            jax.ShapeDtypeStruct((s, f), BF16),
            jax.ShapeDtypeStruct((s, f), BF16),
            jax.ShapeDtypeStruct((f, s), BF16),
            jax.ShapeDtypeStruct((d, s), BF16),
            jax.ShapeDtypeStruct((s, d), BF16),
            jax.ShapeDtypeStruct((8, d), F32),
        ],
        [dxo, x, vec, gg, uu, y, w_in, w_in, w_out],
        scratch=[pltpu.VMEM((tm, d), F32)], phases=phases,
    )


def _grad_half(a, b, big, where, mine, col0, prev, recv, name, phases=(), a_transposed=False):
    k1, s = a.shape if a_transposed else a.shape[::-1]
    n = b.shape[1]
    rows_halved = big.h3 == 1
    kk, nn = (k1 // 2, n) if rows_halved else (k1, n // 2)
    tk = _pick(kk, (1408, 1024, 512, 256, 128))
    tn = _pick(nn, (1408, 1024, 640, 512, 256, 128))
    nkb, nnb = kk // tk, nn // tn
    assert col0 % tn == 0 and (recv is None) == (not mine)

    def half(pref):
        return pref[1] if mine else 1 - pref[1]

    def body(_, a_ref, b_ref, *rest):
        acc = _dot(a_ref[...], b_ref[...]) if a_transposed else _dot_tn(a_ref[...], b_ref[...])
        if recv is not None:
            acc = acc + rest[0][0].astype(F32)
        rest[-1][0] = acc.astype(BF16)

    def a_block(i, j, pref):
        return i + (half(pref) * nkb if rows_halved else 0)

    out_spec = pl.BlockSpec((1, tk, tn), lambda i, j, pref: (0, i, col0 // tn + j))
    in_specs = [
        pl.BlockSpec((tk, s), lambda i, j, pref: (a_block(i, j, pref), 0)) if a_transposed
        else pl.BlockSpec((s, tk), lambda i, j, pref: (0, a_block(i, j, pref))),
        pl.BlockSpec((s, tn), lambda i, j, pref: (0, j + (0 if rows_halved else half(pref) * nnb))),
    ]
    ins = [a, b]
    if recv is not None:
        in_specs.append(out_spec)
        ins.append(recv)
    in_place = None
    if prev is not None:
        in_place = {len(ins): 0}
        in_specs.append(_ANY)
        ins.append(prev)
    return _call(
        body, name, (nkb, nnb), in_specs, [out_spec], [jax.ShapeDtypeStruct(big.dims("half"), BF16)], ins,
        prefetch=(where,), phases=phases, in_place=in_place,
    )


def _proj_mod_fwd(x, vec, w, phases=()):
    s, d = x.shape
    n = w.shape[2]
    tm = _pick(s, (512, 256, 128))
    tn = _pick(n, (640, 512, 256, 128))

    def body(x_ref, vec_ref, w_ref, o_ref, h_sc):
        @pl.when(pl.program_id(1) == 0)
        def _():
            h_sc[...] = _modulate(x_ref[...], vec_ref).astype(BF16)

        o_ref[...] = _dot(h_sc[...], w_ref[0])

    return _call(
        body, "ab_in_fwd", (s // tm, n // tn),
        [
            pl.BlockSpec((tm, d), lambda i, j: (i, 0)),
            pl.BlockSpec((8, d), lambda i, j: (0, 0)),
            pl.BlockSpec((1, d, tn), lambda i, j: (0, 0, j)),
        ],
        [pl.BlockSpec((tm, tn), lambda i, j: (i, j))],
        [jax.ShapeDtypeStruct((s, n), F32)], [x, vec, w],
        scratch=[pltpu.VMEM((tm, d), BF16)], phases=phases,
    )


def _proj_res_fwd(a, w, x, vec, phases=()):
    s, kd = a.shape
    d = x.shape[1]
    tm = _pick(s, (512, 256, 128))

    def body(a_ref, w_ref, x_ref, vec_ref, xo_ref, y_ref):
        yv = _dot(a_ref[...], w_ref[0])
        xo_ref[...] = x_ref[...] + vec_ref[3:4, :] * yv
        y_ref[...] = yv.astype(BF16)

    row = pl.BlockSpec((tm, d), lambda i: (i, 0))
    return _call(
        body, "ab_out_fwd", (s // tm,),
        [pl.BlockSpec((tm, kd), lambda i: (i, 0)), pl.BlockSpec((1, kd, d), lambda i: (0, 0, 0)), row, pl.BlockSpec((8, d), lambda i: (0, 0))],
        [row, row],
        [jax.ShapeDtypeStruct((s, d), F32), jax.ShapeDtypeStruct((s, d), BF16)], [a, w, x, vec], phases=phases,
    )


def _proj_res_bwd(dxo, y, vec, w, phases=()):
    s, d = dxo.shape
    kd = w.shape[1]
    tm = _pick(s, (512, 256, 128))

    def body(dxo_ref, y_ref, vec_ref, w_ref, dy_ref, da_ref, dgate_ref):
        @pl.when(pl.program_id(0) == 0)
        def _():
            dgate_ref[...] = jnp.zeros_like(dgate_ref)

        dxo_v = dxo_ref[...]
        dy = (vec_ref[3:4, :] * dxo_v).astype(BF16)
        dy_ref[...] = dy
        dgate_ref[3:4, :] += jnp.sum(dxo_v * y_ref[...].astype(F32), axis=0, keepdims=True)
        da_ref[...] = _dot_nt(dy, w_ref[0]).astype(BF16)

    row = pl.BlockSpec((tm, d), lambda i: (i, 0))
    vecs = pl.BlockSpec((8, d), lambda i: (0, 0))
    return _call(
        body, "ab_out_bwd", (s // tm,),
        [row, row, vecs, pl.BlockSpec((1, kd, d), lambda i: (0, 0, 0))],
        [row, pl.BlockSpec((tm, kd), lambda i: (i, 0)), vecs],
        [jax.ShapeDtypeStruct((s, d), BF16), jax.ShapeDtypeStruct((s, kd), BF16), jax.ShapeDtypeStruct((8, d), F32)],
        [dxo, y, vec, w], phases=phases,
    )


def _proj_mod_bwd(dproj, w, x, vec, dxo, dvec_in, phases=()):
    s, n = dproj.shape
    d = x.shape[1]
    tm = _pick(s, (512, 256, 128))

    def body(dp_ref, w_ref, x_ref, vec_ref, dxo_ref, dvi_ref, dx_ref, h_ref, dvec_ref):
        @pl.when(pl.program_id(0) == 0)
        def _():
            dvec_ref[...] = dvi_ref[...]

        dh = _dot_nt(dp_ref[...], w_ref[0])
        dx, h = _modulate_bwd(x_ref[...], dh, vec_ref, dvec_ref)
        dx_ref[...] = dxo_ref[...] + dx
        h_ref[...] = h.astype(BF16)

    row = pl.BlockSpec((tm, d), lambda i: (i, 0))
    vecs = pl.BlockSpec((8, d), lambda i: (0, 0))
    return _call(
        body, "ab_in_bwd", (s // tm,),
        [pl.BlockSpec((tm, n), lambda i: (i, 0)), pl.BlockSpec((1, d, n), lambda i: (0, 0, 0)), row, vecs, row, vecs],
        [row, row, vecs],
        [jax.ShapeDtypeStruct((s, d), F32), jax.ShapeDtypeStruct((s, d), BF16), jax.ShapeDtypeStruct((8, d), F32)],
        [dproj, w, x, vec, dxo, dvec_in], phases=phases,
    )


def _tril(n):
    return lax.broadcasted_iota(jnp.int32, (n, n), 0) >= lax.broadcasted_iota(jnp.int32, (n, n), 1)


def _layernorm_stats(gv):
    mu = jnp.mean(gv, axis=-1, keepdims=True)
    cen = gv - mu
    rstd = lax.rsqrt(jnp.mean(cen * cen, axis=-1, keepdims=True) + EPS)
    return cen * rstd, rstd


def _shift_down(q, k, above_ref, c_cg, c_xb, first):
    width = q.shape[1]
    rows = lax.broadcasted_iota(jnp.int32, q.shape, 0)
    out = pltpu.roll(q, k, 0)
    for r in range(k):
        src = CONV_HALO - k + r
        above = above_ref[src : src + 1, c_cg : c_cg + width] * above_ref[src : src + 1, c_xb : c_xb + width]
        above = jnp.where(first, 0.0, above)
        out = jnp.where(rows == r, above, out)
    return out


def _ab_mix_fwd(proj, norm_v, w_s, b_rows, conv_w, phases=()):
    s, n = proj.shape
    heads, chunk, _ = w_s.shape
    da = norm_v.shape[1]
    hd = da // heads
    db = conv_w.shape[1]
    tm = _pick(s, (512, 256, 128))

    def body(p_ref, ph_ref, nv_ref, ws_ref, b_ref, cw_ref, o_ref):
        first = pl.program_id(0) == 0
        gu, _ = _gelu(p_ref[:, 0:da])
        gv, _ = _gelu(p_ref[:, da : 2 * da])
        xhat, _ = _layernorm_stats(gv)
        vn = (xhat * nv_ref[...]).astype(BF16)
        mask = _tril(chunk)
        for hh in range(heads):
            wm = jnp.where(mask, ws_ref[hh], 0.0).astype(BF16)
            cols = slice(hh * hd, (hh + 1) * hd)
            for nn in range(tm // chunk):
                rows = slice(nn * chunk, (nn + 1) * chunk)
                z = _dot(wm, vn[rows, cols]) + b_ref[:, cols]
                o_ref[rows, cols] = (gu[rows, cols] * z).astype(BF16)
        c_cg, c_xb = 2 * da + db, 2 * da + 2 * db
        bg = p_ref[:, 2 * da : 2 * da + db]
        q = p_ref[:, c_cg : c_cg + db] * p_ref[:, c_xb : c_xb + db]
        q1 = _shift_down(q, 1, ph_ref, c_cg, c_xb, first)
        q2 = _shift_down(q, 2, ph_ref, c_cg, c_xb, first)
        conv = cw_ref[0:1, :] * q2 + cw_ref[1:2, :] * q1 + cw_ref[2:3, :] * q
        o_ref[:, da : da + db] = (bg * conv).astype(BF16)

    nh = tm // CONV_HALO
    return _call(
        body, "ab_mix_fwd", (s // tm,),
        [
            pl.BlockSpec((tm, n), lambda i: (i, 0)),
            pl.BlockSpec((CONV_HALO, n), lambda i: (jnp.maximum(i * nh - 1, 0), 0)),
            pl.BlockSpec((1, da), lambda i: (0, 0)),
            pl.BlockSpec((heads, chunk, chunk), lambda i: (0, 0, 0)),
            pl.BlockSpec((chunk, da), lambda i: (0, 0)),
            pl.BlockSpec((3, db), lambda i: (0, 0)),
        ],
        [pl.BlockSpec((tm, da + db), lambda i: (i, 0))],
        [jax.ShapeDtypeStruct((s, da + db), BF16)], [proj, proj, norm_v, w_s, b_rows, conv_w], phases=phases,
    )


def _ab_mix_bwd(proj, dcat, norm_v, w_s, b_rows, conv_w, phases=()):
    s, n = proj.shape
    heads, chunk, _ = w_s.shape
    da = norm_v.shape[1]
    hd = da // heads
    db = conv_w.shape[1]
    tm = _pick(s, (512, 256, 128))
    nblk = s // tm
    dhalo = 2 * CONV_HALO

    def body(p_ref, pa_ref, pb_ref, dc_ref, dcb_ref, nv_ref, ws_ref, b_ref, cw_ref,
             dp_ref, dnv_ref, dws_ref, dzs_ref, dcw_ref, dvn_sc):
        i = pl.program_id(0)
        first, last = i == 0, i == nblk - 1

        @pl.when(first)
        def _():
            dnv_ref[...] = jnp.zeros_like(dnv_ref)
            dws_ref[...] = jnp.zeros_like(dws_ref)
            dzs_ref[...] = jnp.zeros_like(dzs_ref)
            dcw_ref[...] = jnp.zeros_like(dcw_ref)

        uu = p_ref[:, 0:da]
        gu, gu_grad = _gelu(uu)
        gv, gv_grad = _gelu(p_ref[:, da : 2 * da])
        xhat, rstd = _layernorm_stats(gv)
        nv = nv_ref[...]
        vn = (xhat * nv).astype(BF16)
        dya = dc_ref[:, 0:da].astype(F32)
        dz = (dya * gu).astype(BF16)
        mask = _tril(chunk)
        for hh in range(heads):
            wm = jnp.where(mask, ws_ref[hh], 0.0).astype(BF16)
            cols = slice(hh * hd, (hh + 1) * hd)
            dws = jnp.zeros((chunk, chunk), F32)
            for nn in range(tm // chunk):
                rows = slice(nn * chunk, (nn + 1) * chunk)
                z = _dot(wm, vn[rows, cols]) + b_ref[:, cols]
                dp_ref[rows, cols] = (dya[rows, cols] * z * gu_grad[rows, cols]).astype(BF16)
                dz_blk = dz[rows, cols]
                dws = dws + _dot_nt(dz_blk, vn[rows, cols])
                dzs_ref[:, cols] += dz_blk.astype(F32)
                dvn = _dot_tn(wm, dz_blk)
                dnv_ref[:, cols] += jnp.sum(dvn * xhat[rows, cols], axis=0, keepdims=True)
                dvn_sc[rows, cols] = dvn
            dws_ref[hh] += jnp.where(mask, dws, 0.0)
        dxhat = dvn_sc[...] * nv
        dgv = rstd * (dxhat - jnp.mean(dxhat, axis=-1, keepdims=True) - xhat * jnp.mean(dxhat * xhat, axis=-1, keepdims=True))
        dp_ref[:, da : 2 * da] = (dgv * gv_grad).astype(BF16)

        c_bg, c_cg, c_xb = 2 * da, 2 * da + db, 2 * da + 2 * db
        bg = p_ref[:, c_bg : c_bg + db]
        cg = p_ref[:, c_cg : c_cg + db]
        xb = p_ref[:, c_xb : c_xb + db]
        q = cg * xb
        q1 = _shift_down(q, 1, pa_ref, c_cg, c_xb, first)
        q2 = _shift_down(q, 2, pa_ref, c_cg, c_xb, first)
        dyb = dc_ref[:, da : da + db].astype(F32)
        conv = cw_ref[0:1, :] * q2 + cw_ref[1:2, :] * q1 + cw_ref[2:3, :] * q
        dp_ref[:, c_bg : c_bg + db] = (dyb * conv).astype(BF16)
        e = dyb * bg
        dcw_ref[0:1, :] += jnp.sum(e * q2, axis=0, keepdims=True)
        dcw_ref[1:2, :] += jnp.sum(e * q1, axis=0, keepdims=True)
        dcw_ref[2:3, :] += jnp.sum(e * q, axis=0, keepdims=True)
        rows = lax.broadcasted_iota(jnp.int32, e.shape, 0)
        dq = cw_ref[2:3, :] * e
        for kk in (1, 2):
            ek = pltpu.roll(e, tm - kk, 0)
            for r in range(kk):
                below = dcb_ref[r : r + 1, da : da + db].astype(F32) * pb_ref[r : r + 1, c_bg : c_bg + db]
                below = jnp.where(last, 0.0, below)
                ek = jnp.where(rows == tm - kk + r, below, ek)
            dq = dq + cw_ref[2 - kk : 3 - kk, :] * ek
        dp_ref[:, c_cg : c_cg + db] = (dq * xb).astype(BF16)
        dp_ref[:, c_xb : c_xb + db] = (dq * cg).astype(BF16)

    nh = tm // CONV_HALO
    nhb = tm // dhalo
    const2 = lambda i: (0, 0)
    return _call(
        body, "ab_mix_bwd", (nblk,),
        [
            pl.BlockSpec((tm, n), lambda i: (i, 0)),
            pl.BlockSpec((CONV_HALO, n), lambda i: (jnp.maximum(i * nh - 1, 0), 0)),
            pl.BlockSpec((CONV_HALO, n), lambda i: (jnp.minimum((i + 1) * nh, s // CONV_HALO - 1), 0)),
            pl.BlockSpec((tm, da + db), lambda i: (i, 0)),
            pl.BlockSpec((dhalo, da + db), lambda i: (jnp.minimum((i + 1) * nhb, s // dhalo - 1), 0)),
            pl.BlockSpec((1, da), const2),
            pl.BlockSpec((heads, chunk, chunk), lambda i: (0, 0, 0)),
            pl.BlockSpec((chunk, da), const2),
            pl.BlockSpec((3, db), const2),
        ],
        [
            pl.BlockSpec((tm, n), lambda i: (i, 0)),
            pl.BlockSpec((1, da), const2),
            pl.BlockSpec((heads, chunk, chunk), lambda i: (0, 0, 0)),
            pl.BlockSpec((chunk, da), const2),
            pl.BlockSpec((3, db), const2),
        ],
        [
            jax.ShapeDtypeStruct((s, n), BF16),
            jax.ShapeDtypeStruct((1, da), F32),
            jax.ShapeDtypeStruct((heads, chunk, chunk), F32),
            jax.ShapeDtypeStruct((chunk, da), F32),
            jax.ShapeDtypeStruct((3, db), F32),
        ],
        [proj, proj, proj, dcat, dcat, norm_v, w_s, b_rows, conv_w],
        scratch=[pltpu.VMEM((tm, da), F32)], phases=phases,
    )


def _pool_counts(tm, i, w):
    t = i * tm + lax.broadcasted_iota(jnp.int32, (tm, 1), 0)
    return jnp.minimum(t + 1, w).astype(F32)


def _pool_fwd(x, vec, w_grp, scale, phases=()):
    s, d = x.shape
    groups, gd, _ = w_grp.shape
    tm = _pick(s, (512, 256, 128))

    def body(x_ref, xa_ref, vec_ref, w_ref, sc_ref, xo_ref, p_ref, o_ref):
        i = pl.program_id(0)
        h = _modulate(x_ref[...], vec_ref)
        ha = jnp.where(i == 0, 0.0, _modulate(xa_ref[...], vec_ref))
        ext = jnp.concatenate([ha, h], axis=0)
        for gi, w in enumerate(POOL_WINDOWS):
            cols = slice(gi * gd, (gi + 1) * gd)
            acc = ext[:, cols]
            step = 1
            while step < w:
                acc = acc + pltpu.roll(acc, step, 0)
                step *= 2
            p = (acc[POOL_HALO:, :] / _pool_counts(tm, i, w) - h[:, cols]).astype(BF16)
            p_ref[:, cols] = p
            o_ref[:, cols] = _dot(p, w_ref[gi]).astype(BF16)
        xo_ref[...] = x_ref[...] + vec_ref[3:4, :] * (o_ref[...].astype(F32) * sc_ref[...])

    nh = tm // POOL_HALO
    row = pl.BlockSpec((tm, d), lambda i: (i, 0))
    return _call(
        body, "pool_fwd", (s // tm,),
        [
            row,
            pl.BlockSpec((POOL_HALO, d), lambda i: (jnp.maximum(i * nh - 1, 0), 0)),
            pl.BlockSpec((8, d), lambda i: (0, 0)),
            pl.BlockSpec((groups, gd, gd), lambda i: (0, 0, 0)),
            pl.BlockSpec((1, d), lambda i: (0, 0)),
        ],
        [row, row, row],
        [jax.ShapeDtypeStruct((s, d), F32), jax.ShapeDtypeStruct((s, d), BF16), jax.ShapeDtypeStruct((s, d), BF16)],
        [x, x, vec, w_grp, scale], phases=phases,
    )


def _pool_bwd(dxo, x, vec, p, o, w_grp, scale, phases=()):
    s, d = x.shape
    groups, gd, _ = w_grp.shape
    tm = _pick(s, (512, 256, 128))
    nblk = s // tm

    def body(dxo_ref, dxb_ref, x_ref, vec_ref, p_ref, o_ref, w_ref, sc_ref, dx_ref, dw_ref, dsc_ref, dvec_ref, dw_sc):
        i = pl.program_id(0)

        @pl.when(i == 0)
        def _():
            dw_sc[...] = jnp.zeros_like(dw_sc)
            dsc_ref[...] = jnp.zeros_like(dsc_ref)
            dvec_ref[...] = jnp.zeros_like(dvec_ref)

        gate, sc = vec_ref[3:4, :], sc_ref[...]
        dxo_v = dxo_ref[...]
        ov = o_ref[...].astype(F32)
        dvec_ref[3:4, :] += jnp.sum(dxo_v * (ov * sc), axis=0, keepdims=True)
        dy = gate * dxo_v
        dsc_ref[...] += jnp.sum(dy * ov, axis=0, keepdims=True)
        dout = (dy * sc).astype(BF16)
        dout_b = jnp.where(i == nblk - 1, 0.0, gate * dxb_ref[...] * sc).astype(BF16)
        for gi, w in enumerate(POOL_WINDOWS):
            cols = slice(gi * gd, (gi + 1) * gd)
            dw_sc[gi] += _dot_tn(p_ref[:, cols], dout[:, cols])
            wb = w_ref[gi]
            dp = _dot_nt(dout[:, cols], wb)
            dp_b = _dot_nt(dout_b[:, cols], wb)
            e = dp / _pool_counts(tm, i, w)
            t_below = (i + 1) * tm + lax.broadcasted_iota(jnp.int32, (POOL_HALO, 1), 0)
            e_b = dp_b / jnp.minimum(t_below + 1, w).astype(F32)
            acc = jnp.concatenate([e, e_b], axis=0)
            step = 1
            while step < w:
                acc = acc + pltpu.roll(acc, tm + POOL_HALO - step, 0)
                step *= 2
            dx_ref[:, cols] = acc[:tm, :] - dp
        dx, _ = _modulate_bwd(x_ref[...], dx_ref[...], vec_ref, dvec_ref)
        dx_ref[...] = dxo_v + dx

        @pl.when(i == nblk - 1)
        def _():
            dw_ref[...] = dw_sc[...].astype(BF16)

    nh = tm // POOL_HALO
    row = pl.BlockSpec((tm, d), lambda i: (i, 0))
    vecs = pl.BlockSpec((8, d), lambda i: (0, 0))
    wspec = pl.BlockSpec((groups, gd, gd), lambda i: (0, 0, 0))
    return _call(
        body, "pool_bwd", (nblk,),
        [
            row,
            pl.BlockSpec((POOL_HALO, d), lambda i: (jnp.minimum((i + 1) * nh, s // POOL_HALO - 1), 0)),
            row, vecs, row, row, wspec,
            pl.BlockSpec((1, d), lambda i: (0, 0)),
        ],
        [row, wspec, pl.BlockSpec((1, d), lambda i: (0, 0)), vecs],
        [
            jax.ShapeDtypeStruct((s, d), F32),
            jax.ShapeDtypeStruct((groups, gd, gd), BF16),
            jax.ShapeDtypeStruct((1, d), F32),
            jax.ShapeDtypeStruct((8, d), F32),
        ],
        [dxo, dxo, x, vec, p, o, w_grp, scale],
        scratch=[pltpu.VMEM((groups, gd, gd), F32)], phases=phases,
    )


def _loss_head(x, gain, target, phases=()):
    s, d = x.shape
    tm = _pick(s, (512, 256, 128))

    def body(x_ref, g_ref, t_ref, dx_ref, aux_ref):
        @pl.when(pl.program_id(0) == 0)
        def _():
            aux_ref[...] = jnp.zeros_like(aux_ref)

        xv = x_ref[...]
        rstd = _rstd(xv)
        r = xv * rstd
        gain_v = g_ref[...]
        err = r * gain_v - t_ref[...]
        aux_ref[1:2, :] += jnp.sum(err * err, axis=0, keepdims=True)
        dout = err * (1.0 / d)
        aux_ref[0:1, :] += jnp.sum(dout * r, axis=0, keepdims=True)
        dr = dout * gain_v
        dx_ref[...] = rstd * (dr - r * jnp.mean(dr * r, axis=-1, keepdims=True))

    row = pl.BlockSpec((tm, d), lambda i: (i, 0))
    return _call(
        body, "loss_head", (s // tm,),
        [row, pl.BlockSpec((1, d), lambda i: (0, 0)), row],
        [row, pl.BlockSpec((8, d), lambda i: (0, 0))],
        [jax.ShapeDtypeStruct((s, d), F32), jax.ShapeDtypeStruct((8, d), F32)], [x, gain, target], phases=phases,
    )


def _small_adam(gathered, gathered_ws, layout, smalls, chip):
    names = list(smalls)
    n = len(names)

    def body(*refs):
        chip_ref, g_ref, gws_ref = refs[0], refs[1], refs[2]
        wmv = refs[3 : 3 + 3 * n]
        outs = refs[3 + 3 * n : 3 + 7 * n]
        total = refs[-1]
        total[...] = g_ref[0]
        for kdev in range(1, N_DEV):
            total[...] += g_ref[kdev]
        total_ws = gws_ref[0]
        for kdev in range(1, N_DEV):
            total_ws = total_ws + gws_ref[kdev]
        my_chip = chip_ref[0]
        for a, name in enumerate(names):
            w_ref, m_ref, v_ref = wmv[3 * a : 3 * a + 3]
            if name == "ab_w_s":
                g = total_ws
            else:
                row0, rows, col0, cols = layout[name]
                if col0 is None:
                    g = jnp.zeros((rows, cols), F32)
                    for j in range(N_CHIPS):
                        g = g + jnp.where(my_chip == j, total[row0 : row0 + rows, j * cols : (j + 1) * cols], 0.0)
                else:
                    g = total[row0 : row0 + rows, col0 : col0 + cols]
            dl, mo, vo = _adam(w_ref[...], g, m_ref[...], v_ref[...])
            outs[4 * a][...] = g
            outs[4 * a + 1][...] = dl
            outs[4 * a + 2][...] = mo
            outs[4 * a + 3][...] = vo

    ins = [gathered, gathered_ws]
    out_shapes = []
    for name in names:
        ins.extend(smalls[name])
        out_shapes.extend([jax.ShapeDtypeStruct(smalls[name][0].shape, F32)] * 4)
    whole = lambda shape: pl.BlockSpec(shape, functools.partial(lambda nd, i, c: (0,) * nd, len(shape)))
    res = pl.pallas_call(
        body, name="small_adam",
        grid_spec=pltpu.PrefetchScalarGridSpec(
            num_scalar_prefetch=1, grid=(1,),
            in_specs=[whole(a.shape) for a in ins], out_specs=[whole(o.shape) for o in out_shapes],
            scratch_shapes=[pltpu.VMEM(gathered.shape[1:], F32)],
        ),
        out_shape=out_shapes,
        compiler_params=pltpu.CompilerParams(dimension_semantics=("arbitrary",), vmem_limit_bytes=VMEM_LIMIT_BYTES),
    )(chip.reshape(1).astype(jnp.int32), *ins)
    return {name: res[4 * a : 4 * a + 4] for a, name in enumerate(names)}


def _pad_rows(a, rows=8):
    extra = (-a.shape[0]) % rows
    return jnp.pad(a, ((0, extra), (0, 0))) if extra else a


def _pad_cols(a, cols):
    return jnp.pad(a, ((0, 0), (0, cols - a.shape[1]))) if a.shape[1] < cols else a


def _run(fn, *phases):
    outs, p_outs = fn(list(phases))
    for p, po in zip(phases, p_outs):
        p.then(po)
    return outs


def kernel(x, c, norm_g, w_mod, b_mod, w_ffn_in, w_ffn_out, ab_w_in, ab_norm_v, ab_w_s, ab_b_s, ab_conv_w, ab_w_out, pool_w_grp, pool_scale, final_g, loss_target, m_norm_g, m_w_mod, m_b_mod, m_w_ffn_in, m_w_ffn_out, m_ab_w_in, m_ab_norm_v, m_ab_w_s, m_ab_b_s, m_ab_conv_w, m_ab_w_out, m_pool_w_grp, m_pool_scale, m_final_g, v_norm_g, v_w_mod, v_b_mod, v_w_ffn_in, v_w_ffn_out, v_ab_w_in, v_ab_norm_v, v_ab_w_s, v_ab_b_s, v_ab_conv_w, v_ab_w_out, v_pool_w_grp, v_pool_scale, v_final_g):
    ix, iy, ic = _place()
    chip = 2 * ix + iy
    me = 4 * ix + 2 * iy + ic
    where = jnp.stack([chip, ic]).astype(jnp.int32)
    s, d = x.shape[1], x.shape[2]
    x0 = x.reshape(s, d)
    target = loss_target.reshape(s, d)
    n_layers = norm_g.shape[0]
    dq = d // N_CHIPS
    heads, chunk = ab_w_s.shape[1], ab_w_s.shape[2]
    da = ab_norm_v.shape[1]
    db = ab_conv_w.shape[2] * N_CHIPS
    f_hidden = w_ffn_out.shape[2] * N_CHIPS
    assert n_layers == 2 and da % heads == 0

    cw_pad = _pad_cols(ab_conv_w.reshape(3, db // N_CHIPS), dq)
    packed = jnp.concatenate(
        [_pad_rows(c.reshape(N_CHIPS, dq)), _pad_rows(norm_g.reshape(-1, dq)), _pad_rows(pool_scale.reshape(1, dq)), _pad_rows(cw_pad)],
        axis=0,
    )
    ncol = w_mod.shape[2]
    b_cols = lax.dynamic_slice(b_mod, (0, chip * ncol), (n_layers, ncol)).reshape(n_layers, 1, ncol)
    small = {}

    def small_gather(key, arrs):
        def then(outs):
            small[key] = outs

        return _phase_small_gather(arrs, then)

    stacks = {
        "w_ffn_in": tuple(a.reshape((-1,) + a.shape[2:]) for a in (w_ffn_in, m_w_ffn_in, v_w_ffn_in)),
        "w_ffn_out": tuple(a.reshape((-1,) + a.shape[2:]) for a in (w_ffn_out, m_w_ffn_out, v_w_ffn_out)),
        "ab_w_in": (ab_w_in, m_ab_w_in, v_ab_w_in),
        "ab_w_out": (ab_w_out, m_ab_w_out, v_ab_w_out),
        "pool_w_grp": (pool_w_grp[0], m_pool_w_grp[0], v_pool_w_grp[0]),
    }
    big_in = _Big((1, d, 2 * f_hidden), 2, 1)
    big_out = _Big((1, f_hidden, d), 1, 2)
    units = {}
    for l in range(n_layers):
        for k in range(2):
            units[f"in{l}{k}"] = (big_in, "w_ffn_in", 2 * l + k)
            units[f"out{l}{k}"] = (big_out, "w_ffn_out", 2 * l + k)
    units["abin"] = (_Big((1, d, ab_w_in.shape[2] * N_CHIPS), 2, 1), "ab_w_in", 0)
    units["about"] = (_Big((1, ab_w_out.shape[1] * N_CHIPS, d), 1, 2), "ab_w_out", 0)
    units["pool"] = (_Big((pool_w_grp.shape[1], pool_w_grp.shape[2] * N_CHIPS, pool_w_grp.shape[3]), 1, 0), "pool_w_grp", 0)
    big = {u: g for u, (g, _, _) in units.items()}

    weight = {}
    complete = set()

    def cast(u):
        g, st, b0 = units[u]

        def launch(phases):
            (weight[u],), p_outs = _cast_into_full(stacks[st][0], b0, g, where, "cast_" + u, phases)
            return None, p_outs

        return launch

    def gather_ici(*us):
        def then(outs):
            for u, o in zip(us, outs):
                weight[u] = o

        return _phase_gather_ici([weight[u] for u in us], [big[u] for u in us], then)

    def gather_sibling(*us):
        def then(outs):
            for u, o in zip(us, outs):
                weight[u] = o
                complete.add(u)

        return _phase_gather_sibling([weight[u] for u in us], [big[u] for u in us], then)

    def w_of(u):
        assert u in complete, u
        return weight[u]

    _run(cast("in00"), small_gather("inputs", [packed]))
    _run(cast("out00"))
    small_all = small["inputs"][0]
    by_chip = small_all[0::2]
    c_all = small_all[:, 0:N_CHIPS, :].reshape(N_DEV, d)
    norm_full = by_chip[:, 8 : 8 + 3 * n_layers, :].transpose(1, 0, 2).reshape(3 * n_layers, d)
    pool_scale_full = by_chip[:, 16:17, :].transpose(1, 0, 2).reshape(1, d)
    conv_full = by_chip[:, 24:27, : db // N_CHIPS].transpose(1, 0, 2).reshape(3, db)
    mod_cols = _run(lambda phases: _mod_fwd(c_all, w_mod, b_cols, phases), gather_ici("in00"))[0]
    _run(cast("abin"), gather_sibling("in00"), gather_ici("out00"), small_gather("mod", [mod_cols.reshape(n_layers * N_DEV, ncol)]))
    _run(cast("about"), gather_sibling("out00"), gather_ici("abin"))
    _run(cast("in01"), gather_sibling("abin"), gather_ici("about"))
    _run(cast("out01"), gather_sibling("about"))
    for u in ("in10", "out10", "pool", "in11", "out11"):
        _run(cast(u))
    mod_all = small["mod"][0]
    mod_mine = lax.dynamic_index_in_dim(mod_all[0::2].reshape(N_CHIPS, n_layers, N_DEV, ncol), me, axis=2, keepdims=False)
    mod = mod_mine.transpose(1, 0, 2).reshape(n_layers, 3, 3, d)
    vecs = {
        (l, sub): _pad_rows(jnp.concatenate([norm_full[3 * l + sub][None], mod[l, sub]], axis=0))
        for l in range(n_layers)
        for sub in range(3)
    }
    b_rows = jnp.broadcast_to(ab_b_s[0].T[:, :, None], (chunk, heads, da // heads)).reshape(chunk, da)

    saved = {}

    def ffn_forward(xs, l, sub, k, *phases):
        saved[l, sub, "x"] = xs
        xs, gg, uu, yb = _run(
            lambda ph: _ffn_fwd(xs, vecs[l, sub], w_of(f"in{l}{k}"), w_of(f"out{l}{k}"), f"ffn_fwd_{l}{k}", ph), *phases
        )
        saved[l, sub, "act"] = (gg, uu, yb)
        return xs

    xs = ffn_forward(x0, 0, 0, 0, gather_ici("in01"))
    saved[0, 1, "x"] = xs
    (proj,) = _run(lambda ph: _proj_mod_fwd(xs, vecs[0, 1], w_of("abin"), ph), gather_sibling("in01"), gather_ici("out01"))
    (cat,) = _run(lambda ph: _ab_mix_fwd(proj, ab_norm_v, ab_w_s[0], b_rows, conv_full, ph), gather_sibling("out01"), gather_ici("in10"))
    xs, yb = _run(lambda ph: _proj_res_fwd(cat, w_of("about"), xs, vecs[0, 1], ph), gather_sibling("in10"), gather_ici("out10", "pool"))
    saved[0, 1, "act"] = (proj, cat, yb)
    xs = ffn_forward(xs, 0, 2, 1, gather_sibling("out10", "pool"), gather_ici("in11"))
    xs = ffn_forward(xs, 1, 0, 0, gather_sibling("in11"), gather_ici("out11"))
    saved[1, 1, "x"] = xs
    xs, pp, oo = _run(lambda ph: _pool_fwd(xs, vecs[1, 1], w_of("pool"), pool_scale_full, ph), gather_sibling("out11"))
    saved[1, 1, "act"] = (pp, oo)
    xs = ffn_forward(xs, 1, 2, 1)
    dxs, aux = _run(lambda ph: _loss_head(xs, final_g.reshape(1, d), target, ph))
    loss = lax.psum(0.5 * jnp.sum(aux[1]) / d, ("x", "y", "c"))

    grad = {}
    recv = {}
    csum = {}
    parts = {}
    reduced = {}
    done = set()
    dvecs, small_g = {}, {}

    def pair_exchange(*us):
        def then(outs):
            for u, o in zip(us, outs):
                recv[u] = o

        return _phase_pair_exchange([grad[u] for u in us], [big[u] for u in us], then)

    def grad_half(u, a, b, mine, name, *phases, col0=0, prev=None, a_t=False):
        (res,) = _run(
            lambda ph: _grad_half(a, b, big[u], where, mine, col0, prev, recv[u] if mine else None, name, ph, a_t), *phases
        )
        return res

    def pair_sum(u, *phases):
        def launch(ph):
            (csum[u],), p_outs = _pair_sum(grad[u], recv[u], big[u], where, "pair_sum_" + u, ph)
            return None, p_outs

        _run(launch, *phases)

    def chip_exchange(*us):
        def then(outs):
            for u, o in zip(us, outs):
                parts[u] = o

        return _phase_chip_exchange([csum[u] for u in us], [big[u] for u in us], then)

    def chip_sum(*us, carried=()):
        for n_u, u in enumerate(us):
            g, st, b0 = units[u]

            def launch(ph):
                (reduced[st],), p_outs = _chip_sum(
                    csum[u], parts[u], g, where, reduced.get(st), stacks[st][0].shape, b0, "chip_sum_" + u, ph
                )
                return None, p_outs

            _run(launch, *(carried if n_u == 0 else ()))

    def pair_broadcast(*us):
        sts = [units[u][1] for u in us]
        assert len(set(sts)) == len(sts)

        def then(outs):
            for u, st, o in zip(us, sts, outs):
                reduced[st] = o
                done.add(u)

        return _phase_pair_broadcast([reduced[st] for st in sts], [big[u] for u in us], [units[u][2] for u in us], then)

    def ffn_backward(dxs, l, sub, k, carried_bwd, carried_send, carried_mine):
        gg, uu, yb = saved[l, sub, "act"]
        w_in, w_out = w_of(f"in{l}{k}"), w_of(f"out{l}{k}")
        dxs, dg, du, a, h, dy, dvecs[l, sub] = _run(
            lambda ph: _ffn_bwd(dxs, saved[l, sub, "x"], vecs[l, sub], gg, uu, yb, w_in, w_out, f"ffn_bwd_{l}{k}", ph), *carried_bwd()
        )
        uo, ui, tag = f"out{l}{k}", f"in{l}{k}", f"{l}{k}"
        grad[uo] = grad_half(uo, a, dy, False, "dw_out_send_" + tag, *carried_send(), a_t=True)
        part = grad_half(ui, h, du, False, "dw_in_u_send_" + tag, pair_exchange(uo), col0=f_hidden, a_t=True)
        grad[ui] = grad_half(ui, h, dg, False, "dw_in_g_send_" + tag, prev=part, a_t=True)
        csum[uo] = grad_half(uo, a, dy, True, "dw_out_" + tag, pair_exchange(ui), a_t=True)
        part = grad_half(ui, h, du, True, "dw_in_u_" + tag, *carried_mine(), col0=f_hidden, a_t=True)
        csum[ui] = grad_half(ui, h, dg, True, "dw_in_g_" + tag, prev=part, a_t=True)
        return dxs

    none = lambda: ()
    dxs = ffn_backward(dxs, 1, 2, 1, none, none, none)
    pp, oo = saved[1, 1, "act"]
    dxs, grad["pool"], small_g["pool_scale"], dvecs[1, 1] = _run(
        lambda ph: _pool_bwd(dxs, saved[1, 1, "x"], vecs[1, 1], pp, oo, w_of("pool"), pool_scale_full, ph)
    )

    def after_11():
        return (chip_exchange("in11", "out11"), pair_exchange("pool"))

    def bcast_11():
        chip_sum("in11", "out11")
        pair_sum("pool")
        return (pair_broadcast("in11", "out11"), chip_exchange("pool"))

    dxs = ffn_backward(dxs, 1, 0, 0, after_11, bcast_11, none)

    def after_10():
        return (chip_exchange("in10", "out10"),)

    def bcast_10():
        chip_sum("in10", "out10", "pool")
        return (pair_broadcast("in10", "out10", "pool"),)

    dxs = ffn_backward(dxs, 0, 2, 1, after_10, bcast_10, none)

    proj, cat, yb = saved[0, 1, "act"]
    dy, dcat, dgate = _run(lambda ph: _proj_res_bwd(dxs, yb, vecs[0, 1], w_of("about"), ph))
    grad["about"] = grad_half("about", cat, dy, False, "dw_ab_out_send")
    dproj, small_g["ab_norm_v"], small_g["ab_w_s"], dzs, small_g["ab_conv_w"] = _run(
        lambda ph: _ab_mix_bwd(proj, dcat, ab_norm_v, ab_w_s[0], b_rows, conv_full, ph), chip_exchange("out01"), pair_exchange("about")
    )
    small_g["ab_b_s"] = dzs.reshape(chunk, heads, da // heads).sum(axis=2).T
    dxs, h, dvecs[0, 1] = _run(lambda ph: _proj_mod_bwd(dproj, w_of("abin"), saved[0, 1, "x"], vecs[0, 1], dxs, dgate, ph))
    grad["abin"] = grad_half("abin", h, dproj, False, "dw_ab_in_send")
    chip_sum("out01", carried=(pair_exchange("abin"),))
    csum["about"] = grad_half("about", cat, dy, True, "dw_ab_out", pair_broadcast("out01"))
    csum["abin"] = grad_half("abin", h, dproj, True, "dw_ab_in")

    def after_01():
        return (chip_exchange("in01", "abin", "about"),)

    def bcast_01():
        chip_sum("in01", "abin", "about")
        return (pair_broadcast("in01", "abin", "about"),)

    def reduce_out00():
        return (chip_exchange("out00"),)

    dxs = ffn_backward(dxs, 0, 0, 0, after_01, bcast_01, reduce_out00)
    grad_x = dxs.reshape(x.shape)

    dgain = jnp.stack([dvecs[l, sub][0] for l in range(n_layers) for sub in range(3)])
    dmod = jnp.concatenate([dvecs[l, sub][1:4] for l in range(n_layers) for sub in range(3)], axis=0)
    pieces = {
        "norm_g": (dgain, None, dq), "final_g": (aux[0:1], 0, d), "pool_scale": (small_g["pool_scale"], None, dq),
        "b_mod": (dmod, 0, d), "ab_norm_v": (small_g["ab_norm_v"], 0, da), "ab_conv_w": (small_g["ab_conv_w"], None, db // N_CHIPS),
        "ab_b_s": (small_g["ab_b_s"], 0, chunk),
    }
    layout, row0 = {}, 0
    for nm, (pc, col0, cols) in pieces.items():
        layout[nm] = (row0, pc.shape[0], col0, cols)
        row0 += pc.shape[0]
    packed_rows = -(-row0 // 8) * 8
    packed_g = sum(
        jnp.pad(pc, ((layout[nm][0], packed_rows - layout[nm][0] - pc.shape[0]), (0, d - pc.shape[1])))
        for nm, (pc, _, _) in pieces.items()
    )

    chip_sum("out00")
    _flush(
        "reduce_last", chip_exchange("in00"), pair_broadcast("out00"),
        small_gather("grads", [packed_g, small_g["ab_w_s"].reshape(heads * chunk, chunk)]),
    )
    chip_sum("in00")
    _flush("broadcast_last", pair_broadcast("in00"))
    g_all, gws_all = small["grads"]

    assert done == set(units)
    out = {}
    for st, (w3, m3, v3) in stacks.items():
        shape = {"w_ffn_in": w_ffn_in.shape, "w_ffn_out": w_ffn_out.shape, "pool_w_grp": pool_w_grp.shape}.get(st, w3.shape)
        out[st] = tuple(a.reshape(shape) for a in _adam_rows(w3, reduced[st], m3, v3, 0, w3.shape[0], None, "adam_" + st))

    shapes2d = {
        "norm_g": (3 * n_layers, dq), "b_mod": (9 * n_layers, d), "final_g": (1, d), "ab_norm_v": (1, da),
        "pool_scale": (1, dq), "ab_conv_w": (3, db // N_CHIPS), "ab_b_s": (heads, chunk), "ab_w_s": (heads * chunk, chunk),
    }
    small_w = {"norm_g": (norm_g, m_norm_g, v_norm_g), "b_mod": (b_mod, m_b_mod, v_b_mod), "final_g": (final_g, m_final_g, v_final_g),
               "ab_norm_v": (ab_norm_v, m_ab_norm_v, v_ab_norm_v), "pool_scale": (pool_scale, m_pool_scale, v_pool_scale),
               "ab_conv_w": (ab_conv_w, m_ab_conv_w, v_ab_conv_w), "ab_b_s": (ab_b_s, m_ab_b_s, v_ab_b_s), "ab_w_s": (ab_w_s, m_ab_w_s, v_ab_w_s)}
    smalls = {nm: tuple(a.reshape(shapes2d[nm]) for a in wmv) for nm, wmv in small_w.items()}
    small_out = _small_adam(g_all, gws_all, layout, smalls, chip)
    for nm, res in small_out.items():
        out[nm] = tuple(a.reshape(small_w[nm][0].shape) for a in res)

    mod_row0 = layout["b_mod"][0]
    dmod_all = g_all[:, mod_row0 : mod_row0 + 9 * n_layers, :].reshape(N_DEV, n_layers, 9 * d)
    dmod_cols = lax.dynamic_slice(dmod_all, (0, 0, chip * ncol), (N_DEV, n_layers, ncol)).transpose(1, 0, 2)
    out["w_mod"] = tuple(_run(lambda ph: _mod_bwd_adam(c_all.T, dmod_cols, w_mod, m_w_mod, v_w_mod, ph)))

    order = ["norm_g", "w_mod", "b_mod", "w_ffn_in", "w_ffn_out", "ab_w_in", "ab_norm_v", "ab_w_s", "ab_b_s", "ab_conv_w", "ab_w_out", "pool_w_grp", "pool_scale", "final_g"]
    return (loss, grad_x, *[out[nm][0] for nm in order], *[out[nm][1] for nm in order], *[out[nm][2] for nm in order], *[out[nm][3] for nm in order])
```

```python
import functools
import math

import jax
import jax.numpy as jnp
from jax import lax
from jax.experimental import pallas as pl
from jax.experimental.pallas import tpu as pltpu

F32 = jnp.float32
BF16 = jnp.bfloat16
MESH = pl.DeviceIdType.MESH

EPS = 1e-6
ADAM_LR = 0.001
ADAM_B1 = 0.9
ADAM_B2 = 0.999
ADAM_EPS = 1e-08
ADAM_WD = 0.01
ADAM_STEP = 10
POOL_WINDOWS = (2, 4, 8, 16)
POOL_HALO = 16
CONV_HALO = 8
N_CHIPS = 4
N_DEV = 8
VMEM_LIMIT_BYTES = 48 * 1024 * 1024
EW_BLOCK_ELEMS = 256 * 1024


def _pick(n, prefs):
    for p in prefs:
        if p <= n and n % p == 0:
            return p
    return n


def _row_tile(rows, cols):
    best = None
    for d in range(16, rows + 1, 16):
        if rows % d == 0 and d * cols <= EW_BLOCK_ELEMS:
            best = d
    return best or rows


def _dot(a, b):
    return jnp.dot(a, b, preferred_element_type=F32)


def _dot_nt(a, b):
    return lax.dot_general(a, b, (((1,), (1,)), ((), ())), preferred_element_type=F32)


def _dot_tn(a, b):
    return lax.dot_general(a, b, (((0,), (0,)), ((), ())), preferred_element_type=F32)


def _sigmoid(x):
    return 1.0 / (1.0 + jnp.exp(-x))


_GELU_C = math.sqrt(2.0 / math.pi)


def _gelu(x):
    x2 = x * x
    t = jnp.tanh(_GELU_C * (x + 0.044715 * x2 * x))
    val = 0.5 * x * (1.0 + t)
    grad = 0.5 * (1.0 + t) + 0.5 * x * (1.0 - t * t) * (_GELU_C * (1.0 + 3.0 * 0.044715 * x2))
    return val, grad


def _rstd(x):
    return lax.rsqrt(jnp.mean(x * x, axis=-1, keepdims=True) + EPS)


def _modulate(x, vec_ref):
    return (x * _rstd(x)) * vec_ref[0:1, :] * (1.0 + vec_ref[2:3, :]) + vec_ref[1:2, :]


def _modulate_bwd(x, dh, vec_ref, dvec_ref):
    gn, sh, sc = vec_ref[0:1, :], vec_ref[1:2, :], vec_ref[2:3, :]
    rstd = _rstd(x)
    r = x * rstd
    dvec_ref[0:1, :] += jnp.sum(dh * r * (1.0 + sc), axis=0, keepdims=True)
    dvec_ref[1:2, :] += jnp.sum(dh, axis=0, keepdims=True)
    dvec_ref[2:3, :] += jnp.sum(dh * r * gn, axis=0, keepdims=True)
    gm = gn * (1.0 + sc)
    dr = dh * gm
    dx = rstd * (dr - r * jnp.mean(dr * r, axis=-1, keepdims=True))
    return dx, r * gm + sh


def _adam(w, g, m, v):
    m = ADAM_B1 * m + (1.0 - ADAM_B1) * g
    v = ADAM_B2 * v + (1.0 - ADAM_B2) * (g * g)
    m_hat = m / (1.0 - ADAM_B1**ADAM_STEP)
    v_hat = v / (1.0 - ADAM_B2**ADAM_STEP)
    delta = -ADAM_LR * (m_hat / (jnp.sqrt(v_hat) + ADAM_EPS) + ADAM_WD * w)
    return delta, m, v


_ANY = pl.BlockSpec(memory_space=pl.ANY)


class _Phase:
    def __init__(self, ins, out_shapes, aliases, n_sems, start, finish, then):
        self.ins, self.out_shapes, self.aliases, self.n_sems = list(ins), list(out_shapes), dict(aliases), n_sems
        self.start, self.finish, self.then = start, finish, then


def _call(body, name, grid, in_specs, out_specs, out_shape, ins, scratch=(), prefetch=(), phases=(), in_place=None):
    n_pre, n_in, n_out, n_sc = len(prefetch), len(in_specs), len(out_specs), len(scratch)
    ph_in = [len(p.ins) for p in phases]
    ph_out = [len(p.out_shapes) for p in phases]

    def kernel_body(*refs):
        pos = [0]

        def take(k):
            pos[0] += k
            return refs[pos[0] - k : pos[0]]

        pre, ins_ = take(n_pre), take(n_in)
        p_ins = [take(k) for k in ph_in]
        outs_ = take(n_out)
        p_outs = [take(k) for k in ph_out]
        sc = take(n_sc)
        sems = [take(2) for _ in phases]
        if phases:
            ids = [pl.program_id(a) for a in range(len(grid))]
            first = functools.reduce(jnp.logical_and, [i == 0 for i in ids])
            last = functools.reduce(jnp.logical_and, [i == g - 1 for i, g in zip(ids, grid)])

            @pl.when(first)
            def _():
                for p, pi, po, (send, recv) in zip(phases, p_ins, p_outs, sems):
                    p.start(pi, po, send, recv)

        if body is not None:
            body(*pre, *ins_, *outs_, *sc)
        if phases:

            @pl.when(last)
            def _():
                for p, pi, po, (send, recv) in zip(phases, p_ins, p_outs, sems):
                    p.finish(pi, po, send, recv)

    aliases = {n_pre + i: o for i, o in (in_place or {}).items()}
    i0, o0 = n_pre + n_in, n_out
    for p in phases:
        for i, o in p.aliases.items():
            aliases[i0 + i] = o0 + o
        i0 += len(p.ins)
        o0 += len(p.out_shapes)
    all_in = list(in_specs) + [_ANY] * sum(ph_in)
    all_out = list(out_specs) + [_ANY] * sum(ph_out)
    all_scratch = list(scratch)
    for p in phases:
        all_scratch += [pltpu.SemaphoreType.DMA((p.n_sems,)), pltpu.SemaphoreType.DMA((p.n_sems,))]
    shapes = list(out_shape) + [s for p in phases for s in p.out_shapes]
    operands = list(prefetch) + list(ins) + [a for p in phases for a in p.ins]
    sem = ("arbitrary",) * len(grid)
    params = pltpu.CompilerParams(dimension_semantics=sem, vmem_limit_bytes=VMEM_LIMIT_BYTES)
    if n_pre:
        res = pl.pallas_call(
            kernel_body, name=name, out_shape=shapes, input_output_aliases=aliases, compiler_params=params,
            grid_spec=pltpu.PrefetchScalarGridSpec(
                num_scalar_prefetch=n_pre, grid=grid, in_specs=all_in, out_specs=all_out, scratch_shapes=all_scratch
            ),
        )(*operands)
    else:
        res = pl.pallas_call(
            kernel_body, name=name, grid=grid, in_specs=all_in, out_specs=all_out, out_shape=shapes,
            scratch_shapes=all_scratch, input_output_aliases=aliases, compiler_params=params,
        )(*operands)
    res = list(res)
    outs, rest = res[:n_out], res[n_out:]
    p_res = []
    for k in ph_out:
        p_res.append(rest[:k])
        rest = rest[k:]
    return outs, p_res


def _place():
    return lax.axis_index("x"), lax.axis_index("y"), lax.axis_index("c")


def _other_chips():
    x, y, _ = _place()
    return [(1 - x, y), (x, 1 - y), (1 - x, 1 - y)]


def _flip(k):
    x, y, c = _place()
    return (1 - x if k & 4 else x, 1 - y if k & 2 else y, 1 - c if k & 1 else c)


def _remote(src, dst, send, recv, k, to):
    return pltpu.make_async_remote_copy(
        src_ref=src, dst_ref=dst, send_sem=send.at[k], recv_sem=recv.at[k], device_id=to, device_id_type=MESH
    )


def _phase_small_gather(arrs, then):
    n = len(arrs)

    def copies(ins, outs, send, recv):
        x, y, c = _place()
        me = 4 * x + 2 * y + c
        local = [pltpu.make_async_copy(ins[a], outs[a].at[me], send.at[a * N_DEV]) for a in range(n)]
        remote = [_remote(ins[a], outs[a].at[me], send, recv, a * N_DEV + k, _flip(k)) for a in range(n) for k in range(1, N_DEV)]
        return local, remote

    def start(ins, outs, send, recv):
        local, remote = copies(ins, outs, send, recv)
        for cp in local + remote:
            cp.start()

    def finish(ins, outs, send, recv):
        local, remote = copies(ins, outs, send, recv)
        for cp in remote + local:
            cp.wait()

    shapes = [jax.ShapeDtypeStruct((N_DEV,) + a.shape, a.dtype) for a in arrs]
    return _Phase(arrs, shapes, {}, n * N_DEV, start, finish, then)


def _flush(name, *phases):
    _, p_outs = _call(None, name, (1,), [], [], [], [], phases=list(phases))
    for p, po in zip(phases, p_outs):
        p.then(po)


class _Big:
    KINDS = {"full": (True, True), "half": (True, False), "shard": (False, True), "block": (False, False)}

    def __init__(self, f3, s3, h3):
        assert s3 != h3
        self.f3, self.s3, self.h3 = tuple(f3), s3, h3
        self.bd = tuple(f3[a] // (N_CHIPS if a == s3 else 1) // (2 if a == h3 else 1) for a in range(3))
        self.tile = (1, _row_tile(self.bd[1], self.bd[2]), self.bd[2])
        self.grid = tuple(self.bd[a] // self.tile[a] for a in range(3))

    def dims(self, kind):
        chips, halves = self.KINDS[kind]
        return tuple(
            self.bd[a] * (N_CHIPS if chips and a == self.s3 else 1) * (2 if halves and a == self.h3 else 1) for a in range(3)
        )

    def view(self, ref, chip=None, half=None, batch0=0, both_halves=True):
        start = [batch0, 0, 0]
        size = list(ref.shape)
        size[0] = self.bd[0] * (2 if self.h3 == 0 and both_halves else 1)
        if chip is not None:
            start[self.s3] += chip * self.bd[self.s3]
            size[self.s3] = self.bd[self.s3]
        if half is not None:
            start[self.h3] += half * self.bd[self.h3]
            size[self.h3] = self.bd[self.h3]
        return ref.at[tuple(pl.ds(st, sz) for st, sz in zip(start, size))]

    def spec(self, chip_from=None, half_from=None, lead=(), batch0=0):
        extra = "grid" in (chip_from, half_from)

        def index(*args):
            pref, idx = args[-1], list(args[int(extra) : -1])
            idx[0] += batch0
            if chip_from:
                idx[self.s3] += (pref[0] if chip_from == "pref" else args[0]) * self.grid[self.s3]
            if half_from:
                idx[self.h3] += (pref[1] if half_from == "pref" else args[0]) * self.grid[self.h3]
            return (0,) * len(lead) + tuple(idx)

        return pl.BlockSpec(tuple(lead) + self.tile, index)


def _same(arrs):
    return [jax.ShapeDtypeStruct(a.shape, a.dtype) for a in arrs]


def _phase_gather_ici(arrs, bigs, then):
    n = len(arrs)

    def copies(outs, send, recv, arriving):
        x, y, c = _place()
        return [
            _remote(blk, blk, send, recv, 3 * a + j, (*chip, c))
            for j, chip in enumerate(_other_chips())
            for a in range(n)
            for blk in [bigs[a].view(outs[a], 2 * chip[0] + chip[1] if arriving else 2 * x + y, c)]
        ]

    def start(ins, outs, send, recv):
        for cp in copies(outs, send, recv, False):
            cp.start()

    def finish(ins, outs, send, recv):
        for cp in copies(outs, send, recv, True):
            cp.wait_recv()
        for cp in copies(outs, send, recv, False):
            cp.wait_send()

    return _Phase(arrs, _same(arrs), {a: a for a in range(n)}, 3 * n, start, finish, then)


def _phase_gather_sibling(arrs, bigs, then):
    n = len(arrs)

    def copies(outs, send, recv, arriving):
        x, y, c = _place()
        return [
            _remote(blk, blk, send, recv, 3 * a + j, (x, y, 1 - c))
            for j, chip in enumerate(_other_chips())
            for a in range(n)
            for blk in [bigs[a].view(outs[a], 2 * chip[0] + chip[1], 1 - c if arriving else c)]
        ]

    def start(ins, outs, send, recv):
        for cp in copies(outs, send, recv, False):
            cp.start()

    def finish(ins, outs, send, recv):
        for cp in copies(outs, send, recv, True):
            cp.wait_recv()
        for cp in copies(outs, send, recv, False):
            cp.wait_send()

    return _Phase(arrs, _same(arrs), {a: a for a in range(n)}, 3 * n, start, finish, then)


def _phase_pair_exchange(grads, bigs, then):
    n = len(grads)

    def copies(ins, outs, send, recv):
        x, y, c = _place()
        srcs = [ins[a] if ins[a].shape == outs[a].shape else bigs[a].view(ins[a], None, 1 - c) for a in range(n)]
        return [_remote(srcs[a], outs[a], send, recv, a, (x, y, 1 - c)) for a in range(n)]

    def start(ins, outs, send, recv):
        for cp in copies(ins, outs, send, recv):
            cp.start()

    def finish(ins, outs, send, recv):
        for cp in copies(ins, outs, send, recv):
            cp.wait()

    shapes = [jax.ShapeDtypeStruct(b.dims("half"), BF16) for b in bigs]
    return _Phase(grads, shapes, {}, n, start, finish, then)


def _phase_chip_exchange(sums, bigs, then):
    n = len(sums)

    def copies(ins, outs, send, recv):
        _, _, c = _place()
        return [
            _remote(bigs[a].view(ins[a], 2 * chip[0] + chip[1], both_halves=False), outs[a].at[j], send, recv, 3 * a + j, (*chip, c))
            for j, chip in enumerate(_other_chips())
            for a in range(n)
        ]

    def start(ins, outs, send, recv):
        for cp in copies(ins, outs, send, recv):
            cp.start()

    def finish(ins, outs, send, recv):
        for cp in copies(ins, outs, send, recv):
            cp.wait()

    shapes = [jax.ShapeDtypeStruct((N_CHIPS - 1,) + b.dims("block"), BF16) for b in bigs]
    return _Phase(sums, shapes, {}, 3 * n, start, finish, then)


def _phase_pair_broadcast(stacks, bigs, batch0s, then):
    n = len(stacks)

    def start(ins, outs, send, recv):
        x, y, c = _place()
        for a in range(n):
            blk = bigs[a].view(outs[a], None, c, batch0s[a])
            _remote(blk, blk, send, recv, a, (x, y, 1 - c)).start()

    def finish(ins, outs, send, recv):
        x, y, c = _place()
        for a in range(n):
            mine = bigs[a].view(outs[a], None, c, batch0s[a])
            theirs = bigs[a].view(outs[a], None, 1 - c, batch0s[a])
            _remote(mine, mine, send, recv, a, (x, y, 1 - c)).wait_send()
            _remote(theirs, theirs, send, recv, a, (x, y, 1 - c)).wait_recv()

    return _Phase(stacks, _same(stacks), {a: a for a in range(n)}, n, start, finish, then)


def _tile_call(body, name, big, where, extra, ins, in_specs, out_specs, out_shape, phases=()):
    grid = ((extra,) if extra else ()) + big.grid
    return _call(body, name, grid, in_specs, out_specs, out_shape, ins, prefetch=(where,), phases=phases)


def _cast_into_full(w_stack, batch0, big, where, name, phases=()):
    def body(_, w_ref, o_ref):
        o_ref[...] = w_ref[...].astype(BF16)

    return _tile_call(
        body, name, big, where, 2, [w_stack], [big.spec(None, "grid", batch0=batch0)], [big.spec("pref", "grid")],
        [jax.ShapeDtypeStruct(big.dims("full"), BF16)], phases,
    )


def _pair_sum(g_full, recv_half, big, where, name, phases=()):
    def body(_, g_ref, r_ref, o_ref):
        o_ref[...] = (g_ref[...].astype(F32) + r_ref[...].astype(F32)).astype(BF16)

    half = big.spec("grid", None)
    return _tile_call(
        body, name, big, where, N_CHIPS, [g_full, recv_half], [big.spec("grid", "pref"), half], [half],
        [jax.ShapeDtypeStruct(big.dims("half"), BF16)], phases,
    )


def _chip_sum(chip_sum, parts, big, where, stack, stack_shape, batch0, name, phases=()):
    def body(_, own_ref, p_ref, *rest):
        acc = own_ref[...].astype(F32)
        for k in range(N_CHIPS - 1):
            acc = acc + p_ref[k].astype(F32)
        rest[-1][...] = acc

    ins = [chip_sum, parts] + ([stack] if stack is not None else [])
    in_specs = [big.spec("pref", None), big.spec(None, None, lead=(N_CHIPS - 1,))] + ([_ANY] if stack is not None else [])
    return _call(
        body, name, big.grid, in_specs, [big.spec(None, "pref", batch0=batch0)], [jax.ShapeDtypeStruct(stack_shape, F32)], ins,
        prefetch=(where,), phases=phases, in_place={2: 0} if stack is not None else None,
    )


def _adam_rows(w, g, m, v, batch0, nb, prev, name):
    b, r, c = w.shape
    tr = _row_tile(r, c)

    def body(w_ref, g_ref, m_ref, v_ref, *rest):
        go_ref, d_ref, mo_ref, vo_ref = rest[-4:]
        gv = g_ref[...]
        d, mo, vo = _adam(w_ref[...], gv, m_ref[...], v_ref[...])
        go_ref[...] = gv
        d_ref[...] = d
        mo_ref[...] = mo
        vo_ref[...] = vo

    spec = pl.BlockSpec((1, tr, c), lambda bb, i: (batch0 + bb, i, 0))
    ins = [w, g, m, v] + (list(prev) if prev is not None else [])
    return pl.pallas_call(
        body, name=name, grid=(nb, r // tr), in_specs=[spec] * 4 + ([_ANY] * 4 if prev is not None else []), out_specs=[spec] * 4,
        out_shape=[jax.ShapeDtypeStruct(w.shape, F32)] * 4,
        input_output_aliases={4: 0, 5: 1, 6: 2, 7: 3} if prev is not None else {},
        compiler_params=pltpu.CompilerParams(dimension_semantics=("arbitrary",) * 2, vmem_limit_bytes=VMEM_LIMIT_BYTES),
    )(*ins)


def _mod_fwd(c_all, w_mod, b_cols, phases=()):
    n_layers, d, n = w_mod.shape
    tn = _pick(n, (768, 512, 384, 256, 128))

    def body(c_ref, w_ref, b_ref, o_ref):
        cv = c_ref[...]
        ca = (cv * _sigmoid(cv)).astype(BF16)
        o_ref[0] = _dot(ca, w_ref[0].astype(BF16)) + b_ref[0]

    return _call(
        body, "mod_fwd", (n_layers, n // tn),
        [
            pl.BlockSpec((N_DEV, d), lambda l, j: (0, 0)),
            pl.BlockSpec((1, d, tn), lambda l, j: (l, 0, j)),
            pl.BlockSpec((1, 1, tn), lambda l, j: (l, 0, j)),
        ],
        [pl.BlockSpec((1, N_DEV, tn), lambda l, j: (l, 0, j))],
        [jax.ShapeDtypeStruct((n_layers, N_DEV, n), F32)], [c_all, w_mod, b_cols], phases=phases,
    )


def _mod_bwd_adam(c_all_t, dmod_cols, w, m, v, phases=()):
    n_layers, d, n = w.shape
    tn = _pick(n, (384, 256, 128))

    def body(c_ref, dm_ref, w_ref, m_ref, v_ref, g_ref, d_ref, mo_ref, vo_ref):
        cv = c_ref[...]
        ca = (cv * _sigmoid(cv)).astype(BF16)
        g = _dot(ca, dm_ref[0].astype(BF16))
        g_ref[0] = g
        dl, mo, vo = _adam(w_ref[0], g, m_ref[0], v_ref[0])
        d_ref[0] = dl
        mo_ref[0] = mo
        vo_ref[0] = vo

    wspec = pl.BlockSpec((1, d, tn), lambda l, j: (l, 0, j))
    return _call(
        body, "mod_bwd_adam", (n_layers, n // tn),
        [pl.BlockSpec((d, N_DEV), lambda l, j: (0, 0)), pl.BlockSpec((1, N_DEV, tn), lambda l, j: (l, 0, j)), wspec, wspec, wspec],
        [wspec] * 4, [jax.ShapeDtypeStruct(w.shape, F32)] * 4, [c_all_t, dmod_cols, w, m, v], phases=phases,
    )


def _ffn_fwd(x, vec, w_in, w_out, name, phases=()):
    s, d = x.shape
    f = w_out.shape[1]
    tm = _pick(s, (1024, 512, 256, 128))
    tf = _pick(f, (256, 128))
    nf = f // tf

    def body(x_ref, vec_ref, wg_ref, wu_ref, wo_ref, xo_ref, g_ref, u_ref, y_ref, h_sc, acc_sc):
        j = pl.program_id(1)

        @pl.when(j == 0)
        def _():
            h_sc[...] = _modulate(x_ref[...], vec_ref).astype(BF16)
            acc_sc[...] = jnp.zeros_like(acc_sc)

        h = h_sc[...]
        g = _dot(h, wg_ref[0])
        u = _dot(h, wu_ref[0])
        g_ref[...] = g.astype(BF16)
        u_ref[...] = u.astype(BF16)
        a = (g * _sigmoid(g) * u).astype(BF16)
        acc_sc[...] += _dot(a, wo_ref[0])

        @pl.when(j == nf - 1)
        def _():
            yv = acc_sc[...]
            xo_ref[...] = x_ref[...] + 0.5 * vec_ref[3:4, :] * yv
            y_ref[...] = yv.astype(BF16)

    row = pl.BlockSpec((tm, d), lambda i, j: (i, 0))
    hid = pl.BlockSpec((tm, tf), lambda i, j: (i, j))
    return _call(
        body, name, (s // tm, nf),
        [
            row,
            pl.BlockSpec((8, d), lambda i, j: (0, 0)),
            pl.BlockSpec((1, d, tf), lambda i, j: (0, 0, j)),
            pl.BlockSpec((1, d, tf), lambda i, j: (0, 0, nf + j)),
            pl.BlockSpec((1, tf, d), lambda i, j: (0, j, 0)),
        ],
        [row, hid, hid, row],
        [
            jax.ShapeDtypeStruct((s, d), F32),
            jax.ShapeDtypeStruct((s, f), BF16),
            jax.ShapeDtypeStruct((s, f), BF16),
            jax.ShapeDtypeStruct((s, d), BF16),
        ],
        [x, vec, w_in, w_in, w_out],
        scratch=[pltpu.VMEM((tm, d), BF16), pltpu.VMEM((tm, d), F32)], phases=phases,
    )


def _ffn_bwd_act(dxo, vec, gg, uu, y, w_out, name, phases=()):
    s, d = dxo.shape
    f = w_out.shape[1]
    tm = _pick(s, (256, 128))
    tf = _pick(f, (1408, 1024, 512, 256, 128))
    nf = f // tf

    def body(dxo_ref, vec_ref, g_ref, u_ref, y_ref, wo_ref, dy_ref, a_ref, dgu_ref, dvec_ref):
        i, j = pl.program_id(0), pl.program_id(1)

        @pl.when((i == 0) & (j == 0))
        def _():
            dvec_ref[...] = jnp.zeros_like(dvec_ref)

        @pl.when(j == 0)
        def _():
            dxo_v = dxo_ref[...]
            dy_ref[...] = (0.5 * vec_ref[3:4, :] * dxo_v).astype(BF16)
            dvec_ref[3:4, :] += 0.5 * jnp.sum(dxo_v * y_ref[...].astype(F32), axis=0, keepdims=True)

        da = _dot_nt(dy_ref[...], wo_ref[0])
        g = g_ref[...].astype(F32)
        u = u_ref[...].astype(F32)
        sig = _sigmoid(g)
        sl = g * sig
        a_ref[...] = (sl * u).astype(BF16)
        dgu_ref[0] = (da * u * (sig * (1.0 + g * (1.0 - sig)))).astype(BF16)
        dgu_ref[1] = (da * sl).astype(BF16)

    row = pl.BlockSpec((tm, d), lambda i, j: (i, 0))
    hid = pl.BlockSpec((tm, tf), lambda i, j: (i, j))
    vecs = pl.BlockSpec((8, d), lambda i, j: (0, 0))
    return _call(
        body, name, (s // tm, nf),
        [row, vecs, hid, hid, row, pl.BlockSpec((1, tf, d), lambda i, j: (0, j, 0))],
        [row, hid, pl.BlockSpec((2, tm, tf), lambda i, j: (0, i, j)), vecs],
        [
            jax.ShapeDtypeStruct((s, d), BF16),
            jax.ShapeDtypeStruct((s, f), BF16),
            jax.ShapeDtypeStruct((2, s, f), BF16),
            jax.ShapeDtypeStruct((8, d), F32),
        ],
        [dxo, vec, gg, uu, y, w_out], phases=phases,
    )


def _grad_half(a, b, big, where, mine, col0, prev, recv, name, phases=()):
    s, k1 = a.shape
    b, b_part = b if isinstance(b, tuple) else (b[None], 0)
    n = b.shape[2]
    rows_halved = big.h3 == 1
    kk, nn = (k1 // 2, n) if rows_halved else (k1, n // 2)
    tk = _pick(kk, (1408, 1024, 512, 256, 128))
    tn = _pick(nn, (1408, 1024, 640, 512, 256, 128))
    nkb, nnb = kk // tk, nn // tn
    assert col0 % tn == 0 and (recv is None) == (not mine)

    def half(pref):
        return pref[1] if mine else 1 - pref[1]

    def body(_, a_ref, b_ref, *rest):
        acc = _dot_tn(a_ref[...], b_ref[0])
        if recv is not None:
            acc = acc + rest[0][0].astype(F32)
        rest[-1][0] = acc.astype(BF16)

    out_spec = pl.BlockSpec((1, tk, tn), lambda i, j, pref: (0, i, col0 // tn + j))
    in_specs = [
        pl.BlockSpec((s, tk), lambda i, j, pref: (0, i + (half(pref) * nkb if rows_halved else 0))),
        pl.BlockSpec((1, s, tn), lambda i, j, pref: (b_part, 0, j + (0 if rows_halved else half(pref) * nnb))),
    ]
    ins = [a, b]
    if recv is not None:
        in_specs.append(out_spec)
        ins.append(recv)
    in_place = None
    if prev is not None:
        in_place = {len(ins): 0}
        in_specs.append(_ANY)
        ins.append(prev)
    return _call(
        body, name, (nkb, nnb), in_specs, [out_spec], [jax.ShapeDtypeStruct(big.dims("half"), BF16)], ins,
        prefetch=(where,), phases=phases, in_place=in_place,
    )


def _proj_mod_fwd(x, vec, w, phases=()):
    s, d = x.shape
    n = w.shape[2]
    tm = _pick(s, (512, 256, 128))
    tn = _pick(n, (640, 512, 256, 128))

    def body(x_ref, vec_ref, w_ref, o_ref, h_sc):
        @pl.when(pl.program_id(1) == 0)
        def _():
            h_sc[...] = _modulate(x_ref[...], vec_ref).astype(BF16)

        o_ref[...] = _dot(h_sc[...], w_ref[0])

    return _call(
        body, "ab_in_fwd", (s // tm, n // tn),
        [
            pl.BlockSpec((tm, d), lambda i, j: (i, 0)),
            pl.BlockSpec((8, d), lambda i, j: (0, 0)),
            pl.BlockSpec((1, d, tn), lambda i, j: (0, 0, j)),
        ],
        [pl.BlockSpec((tm, tn), lambda i, j: (i, j))],
        [jax.ShapeDtypeStruct((s, n), F32)], [x, vec, w],
        scratch=[pltpu.VMEM((tm, d), BF16)], phases=phases,
    )


def _proj_res_fwd(a, w, x, vec, phases=()):
    s, kd = a.shape
    d = x.shape[1]
    tm = _pick(s, (512, 256, 128))

    def body(a_ref, w_ref, x_ref, vec_ref, xo_ref, y_ref):
        yv = _dot(a_ref[...], w_ref[0])
        xo_ref[...] = x_ref[...] + vec_ref[3:4, :] * yv
        y_ref[...] = yv.astype(BF16)

    row = pl.BlockSpec((tm, d), lambda i: (i, 0))
    return _call(
        body, "ab_out_fwd", (s // tm,),
        [pl.BlockSpec((tm, kd), lambda i: (i, 0)), pl.BlockSpec((1, kd, d), lambda i: (0, 0, 0)), row, pl.BlockSpec((8, d), lambda i: (0, 0))],
        [row, row],
        [jax.ShapeDtypeStruct((s, d), F32), jax.ShapeDtypeStruct((s, d), BF16)], [a, w, x, vec], phases=phases,
    )


def _proj_res_bwd(dxo, y, vec, w, phases=()):
    s, d = dxo.shape
    kd = w.shape[1]
    tm = _pick(s, (512, 256, 128))

    def body(dxo_ref, y_ref, vec_ref, w_ref, dy_ref, da_ref, dgate_ref):
        @pl.when(pl.program_id(0) == 0)
        def _():
            dgate_ref[...] = jnp.zeros_like(dgate_ref)

        dxo_v = dxo_ref[...]
        dy = (vec_ref[3:4, :] * dxo_v).astype(BF16)
        dy_ref[...] = dy
        dgate_ref[3:4, :] += jnp.sum(dxo_v * y_ref[...].astype(F32), axis=0, keepdims=True)
        da_ref[...] = _dot_nt(dy, w_ref[0]).astype(BF16)

    row = pl.BlockSpec((tm, d), lambda i: (i, 0))
    vecs = pl.BlockSpec((8, d), lambda i: (0, 0))
    return _call(
        body, "ab_out_bwd", (s // tm,),
        [row, row, vecs, pl.BlockSpec((1, kd, d), lambda i: (0, 0, 0))],
        [row, pl.BlockSpec((tm, kd), lambda i: (i, 0)), vecs],
        [jax.ShapeDtypeStruct((s, d), BF16), jax.ShapeDtypeStruct((s, kd), BF16), jax.ShapeDtypeStruct((8, d), F32)],
        [dxo, y, vec, w], phases=phases,
    )


def _proj_mod_bwd(dproj, w, x, vec, dxo, dvec_in, name, phases=()):
    parts, s, n_part = dproj.shape
    d = x.shape[1]
    tm = _pick(s, (512, 256, 128))
    tk = _pick(n_part, (1408, 1280, 1024, 512, 256, 128))
    per_part = n_part // tk
    nk = parts * per_part

    def body(dp_ref, w_ref, x_ref, vec_ref, dxo_ref, dvi_ref, dx_ref, h_ref, dvec_ref, acc_sc):
        i, k = pl.program_id(0), pl.program_id(1)

        @pl.when((i == 0) & (k == 0))
        def _():
            dvec_ref[...] = dvi_ref[...]

        @pl.when(k == 0)
        def _():
            acc_sc[...] = jnp.zeros_like(acc_sc)

        acc_sc[...] += _dot_nt(dp_ref[0], w_ref[0])

        @pl.when(k == nk - 1)
        def _():
            dx, h = _modulate_bwd(x_ref[...], acc_sc[...], vec_ref, dvec_ref)
            dx_ref[...] = dxo_ref[...] + dx
            h_ref[...] = h.astype(BF16)

    row = pl.BlockSpec((tm, d), lambda i, k: (i, 0))
    vecs = pl.BlockSpec((8, d), lambda i, k: (0, 0))
    return _call(
        body, name, (s // tm, nk),
        [
            pl.BlockSpec((1, tm, tk), lambda i, k: (k // per_part, i, k % per_part)),
            pl.BlockSpec((1, d, tk), lambda i, k: (0, 0, k)),
            row, vecs, row, vecs,
        ],
        [row, row, vecs],
        [jax.ShapeDtypeStruct((s, d), F32), jax.ShapeDtypeStruct((s, d), BF16), jax.ShapeDtypeStruct((8, d), F32)],
        [dproj, w, x, vec, dxo, dvec_in], scratch=[pltpu.VMEM((tm, d), F32)], phases=phases,
    )


def _tril(n):
    return lax.broadcasted_iota(jnp.int32, (n, n), 0) >= lax.broadcasted_iota(jnp.int32, (n, n), 1)


def _layernorm_stats(gv):
    mu = jnp.mean(gv, axis=-1, keepdims=True)
    cen = gv - mu
    rstd = lax.rsqrt(jnp.mean(cen * cen, axis=-1, keepdims=True) + EPS)
    return cen * rstd, rstd


def _shift_down(q, k, above_ref, c_cg, c_xb, first):
    width = q.shape[1]
    rows = lax.broadcasted_iota(jnp.int32, q.shape, 0)
    out = pltpu.roll(q, k, 0)
    for r in range(k):
        src = CONV_HALO - k + r
        above = above_ref[src : src + 1, c_cg : c_cg + width] * above_ref[src : src + 1, c_xb : c_xb + width]
        above = jnp.where(first, 0.0, above)
        out = jnp.where(rows == r, above, out)
    return out


def _ab_mix_fwd(proj, norm_v, w_s, b_rows, conv_w, phases=()):
    s, n = proj.shape
    heads, chunk, _ = w_s.shape
    da = norm_v.shape[1]
    hd = da // heads
    db = conv_w.shape[1]
    tm = _pick(s, (512, 256, 128))

    def body(p_ref, ph_ref, nv_ref, ws_ref, b_ref, cw_ref, o_ref):
        first = pl.program_id(0) == 0
        gu, _ = _gelu(p_ref[:, 0:da])
        gv, _ = _gelu(p_ref[:, da : 2 * da])
        xhat, _ = _layernorm_stats(gv)
        vn = (xhat * nv_ref[...]).astype(BF16)
        mask = _tril(chunk)
        for hh in range(heads):
            wm = jnp.where(mask, ws_ref[hh], 0.0).astype(BF16)
            cols = slice(hh * hd, (hh + 1) * hd)
            for nn in range(tm // chunk):
                rows = slice(nn * chunk, (nn + 1) * chunk)
                z = _dot(wm, vn[rows, cols]) + b_ref[:, cols]
                o_ref[rows, cols] = (gu[rows, cols] * z).astype(BF16)
        c_cg, c_xb = 2 * da + db, 2 * da + 2 * db
        bg = p_ref[:, 2 * da : 2 * da + db]
        q = p_ref[:, c_cg : c_cg + db] * p_ref[:, c_xb : c_xb + db]
        q1 = _shift_down(q, 1, ph_ref, c_cg, c_xb, first)
        q2 = _shift_down(q, 2, ph_ref, c_cg, c_xb, first)
        conv = cw_ref[0:1, :] * q2 + cw_ref[1:2, :] * q1 + cw_ref[2:3, :] * q
        o_ref[:, da : da + db] = (bg * conv).astype(BF16)

    nh = tm // CONV_HALO
    return _call(
        body, "ab_mix_fwd", (s // tm,),
        [
            pl.BlockSpec((tm, n), lambda i: (i, 0)),
            pl.BlockSpec((CONV_HALO, n), lambda i: (jnp.maximum(i * nh - 1, 0), 0)),
            pl.BlockSpec((1, da), lambda i: (0, 0)),
            pl.BlockSpec((heads, chunk, chunk), lambda i: (0, 0, 0)),
            pl.BlockSpec((chunk, da), lambda i: (0, 0)),
            pl.BlockSpec((3, db), lambda i: (0, 0)),
        ],
        [pl.BlockSpec((tm, da + db), lambda i: (i, 0))],
        [jax.ShapeDtypeStruct((s, da + db), BF16)], [proj, proj, norm_v, w_s, b_rows, conv_w], phases=phases,
    )


def _ab_mix_bwd(proj, dcat, norm_v, w_s, b_rows, conv_w, phases=()):
    s, n = proj.shape
    heads, chunk, _ = w_s.shape
    da = norm_v.shape[1]
    hd = da // heads
    db = conv_w.shape[1]
    tm = _pick(s, (512, 256, 128))
    nblk = s // tm
    dhalo = 2 * CONV_HALO

    def body(p_ref, pa_ref, pb_ref, dc_ref, dcb_ref, nv_ref, ws_ref, b_ref, cw_ref,
             dp_ref, dnv_ref, dws_ref, dzs_ref, dcw_ref, dvn_sc):
        i = pl.program_id(0)
        first, last = i == 0, i == nblk - 1

        @pl.when(first)
        def _():
            dnv_ref[...] = jnp.zeros_like(dnv_ref)
            dws_ref[...] = jnp.zeros_like(dws_ref)
            dzs_ref[...] = jnp.zeros_like(dzs_ref)
            dcw_ref[...] = jnp.zeros_like(dcw_ref)

        uu = p_ref[:, 0:da]
        gu, gu_grad = _gelu(uu)
        gv, gv_grad = _gelu(p_ref[:, da : 2 * da])
        xhat, rstd = _layernorm_stats(gv)
        nv = nv_ref[...]
        vn = (xhat * nv).astype(BF16)
        dya = dc_ref[:, 0:da].astype(F32)
        dz = (dya * gu).astype(BF16)
        mask = _tril(chunk)
        for hh in range(heads):
            wm = jnp.where(mask, ws_ref[hh], 0.0).astype(BF16)
            cols = slice(hh * hd, (hh + 1) * hd)
            dws = jnp.zeros((chunk, chunk), F32)
            for nn in range(tm // chunk):
                rows = slice(nn * chunk, (nn + 1) * chunk)
                z = _dot(wm, vn[rows, cols]) + b_ref[:, cols]
                dp_ref[rows, cols] = (dya[rows, cols] * z * gu_grad[rows, cols]).astype(BF16)
                dz_blk = dz[rows, cols]
                dws = dws + _dot_nt(dz_blk, vn[rows, cols])
                dzs_ref[:, cols] += dz_blk.astype(F32)
                dvn = _dot_tn(wm, dz_blk)
                dnv_ref[:, cols] += jnp.sum(dvn * xhat[rows, cols], axis=0, keepdims=True)
                dvn_sc[rows, cols] = dvn
            dws_ref[hh] += jnp.where(mask, dws, 0.0)
        dxhat = dvn_sc[...] * nv
        dgv = rstd * (dxhat - jnp.mean(dxhat, axis=-1, keepdims=True) - xhat * jnp.mean(dxhat * xhat, axis=-1, keepdims=True))
        dp_ref[:, da : 2 * da] = (dgv * gv_grad).astype(BF16)

        c_bg, c_cg, c_xb = 2 * da, 2 * da + db, 2 * da + 2 * db
        bg = p_ref[:, c_bg : c_bg + db]
        cg = p_ref[:, c_cg : c_cg + db]
        xb = p_ref[:, c_xb : c_xb + db]
        q = cg * xb
        q1 = _shift_down(q, 1, pa_ref, c_cg, c_xb, first)
        q2 = _shift_down(q, 2, pa_ref, c_cg, c_xb, first)
        dyb = dc_ref[:, da : da + db].astype(F32)
        conv = cw_ref[0:1, :] * q2 + cw_ref[1:2, :] * q1 + cw_ref[2:3, :] * q
        dp_ref[:, c_bg : c_bg + db] = (dyb * conv).astype(BF16)
        e = dyb * bg
        dcw_ref[0:1, :] += jnp.sum(e * q2, axis=0, keepdims=True)
        dcw_ref[1:2, :] += jnp.sum(e * q1, axis=0, keepdims=True)
        dcw_ref[2:3, :] += jnp.sum(e * q, axis=0, keepdims=True)
        rows = lax.broadcasted_iota(jnp.int32, e.shape, 0)
        dq = cw_ref[2:3, :] * e
        for kk in (1, 2):
            ek = pltpu.roll(e, tm - kk, 0)
            for r in range(kk):
                below = dcb_ref[r : r + 1, da : da + db].astype(F32) * pb_ref[r : r + 1, c_bg : c_bg + db]
                below = jnp.where(last, 0.0, below)
                ek = jnp.where(rows == tm - kk + r, below, ek)
            dq = dq + cw_ref[2 - kk : 3 - kk, :] * ek
        dp_ref[:, c_cg : c_cg + db] = (dq * xb).astype(BF16)
        dp_ref[:, c_xb : c_xb + db] = (dq * cg).astype(BF16)

    nh = tm // CONV_HALO
    nhb = tm // dhalo
    const2 = lambda i: (0, 0)
    return _call(
        body, "ab_mix_bwd", (nblk,),
        [
            pl.BlockSpec((tm, n), lambda i: (i, 0)),
            pl.BlockSpec((CONV_HALO, n), lambda i: (jnp.maximum(i * nh - 1, 0), 0)),
            pl.BlockSpec((CONV_HALO, n), lambda i: (jnp.minimum((i + 1) * nh, s // CONV_HALO - 1), 0)),
            pl.BlockSpec((tm, da + db), lambda i: (i, 0)),
            pl.BlockSpec((dhalo, da + db), lambda i: (jnp.minimum((i + 1) * nhb, s // dhalo - 1), 0)),
            pl.BlockSpec((1, da), const2),
            pl.BlockSpec((heads, chunk, chunk), lambda i: (0, 0, 0)),
            pl.BlockSpec((chunk, da), const2),
            pl.BlockSpec((3, db), const2),
        ],
        [
            pl.BlockSpec((tm, n), lambda i: (i, 0)),
            pl.BlockSpec((1, da), const2),
            pl.BlockSpec((heads, chunk, chunk), lambda i: (0, 0, 0)),
            pl.BlockSpec((chunk, da), const2),
            pl.BlockSpec((3, db), const2),
        ],
        [
            jax.ShapeDtypeStruct((s, n), BF16),
            jax.ShapeDtypeStruct((1, da), F32),
            jax.ShapeDtypeStruct((heads, chunk, chunk), F32),
            jax.ShapeDtypeStruct((chunk, da), F32),
            jax.ShapeDtypeStruct((3, db), F32),
        ],
        [proj, proj, proj, dcat, dcat, norm_v, w_s, b_rows, conv_w],
        scratch=[pltpu.VMEM((tm, da), F32)], phases=phases,
    )


def _pool_counts(tm, i, w):
    t = i * tm + lax.broadcasted_iota(jnp.int32, (tm, 1), 0)
    return jnp.minimum(t + 1, w).astype(F32)


def _pool_fwd(x, vec, w_grp, scale, phases=()):
    s, d = x.shape
    groups, gd, _ = w_grp.shape
    tm = _pick(s, (512, 256, 128))

    def body(x_ref, xa_ref, vec_ref, w_ref, sc_ref, xo_ref, p_ref, o_ref):
        i = pl.program_id(0)
        h = _modulate(x_ref[...], vec_ref)
        ha = jnp.where(i == 0, 0.0, _modulate(xa_ref[...], vec_ref))
        ext = jnp.concatenate([ha, h], axis=0)
        for gi, w in enumerate(POOL_WINDOWS):
            cols = slice(gi * gd, (gi + 1) * gd)
            acc = ext[:, cols]
            step = 1
            while step < w:
                acc = acc + pltpu.roll(acc, step, 0)
                step *= 2
            p = (acc[POOL_HALO:, :] / _pool_counts(tm, i, w) - h[:, cols]).astype(BF16)
            p_ref[:, cols] = p
            o_ref[:, cols] = _dot(p, w_ref[gi]).astype(BF16)
        xo_ref[...] = x_ref[...] + vec_ref[3:4, :] * (o_ref[...].astype(F32) * sc_ref[...])

    nh = tm // POOL_HALO
    row = pl.BlockSpec((tm, d), lambda i: (i, 0))
    return _call(
        body, "pool_fwd", (s // tm,),
        [
            row,
            pl.BlockSpec((POOL_HALO, d), lambda i: (jnp.maximum(i * nh - 1, 0), 0)),
            pl.BlockSpec((8, d), lambda i: (0, 0)),
            pl.BlockSpec((groups, gd, gd), lambda i: (0, 0, 0)),
            pl.BlockSpec((1, d), lambda i: (0, 0)),
        ],
        [row, row, row],
        [jax.ShapeDtypeStruct((s, d), F32), jax.ShapeDtypeStruct((s, d), BF16), jax.ShapeDtypeStruct((s, d), BF16)],
        [x, x, vec, w_grp, scale], phases=phases,
    )


def _pool_bwd(dxo, x, vec, p, o, w_grp, scale, phases=()):
    s, d = x.shape
    groups, gd, _ = w_grp.shape
    tm = _pick(s, (512, 256, 128))
    nblk = s // tm

    def body(dxo_ref, dxb_ref, x_ref, vec_ref, p_ref, o_ref, w_ref, sc_ref, dx_ref, dw_ref, dsc_ref, dvec_ref, dw_sc):
        i = pl.program_id(0)

        @pl.when(i == 0)
        def _():
            dw_sc[...] = jnp.zeros_like(dw_sc)
            dsc_ref[...] = jnp.zeros_like(dsc_ref)
            dvec_ref[...] = jnp.zeros_like(dvec_ref)

        gate, sc = vec_ref[3:4, :], sc_ref[...]
        dxo_v = dxo_ref[...]
        ov = o_ref[...].astype(F32)
        dvec_ref[3:4, :] += jnp.sum(dxo_v * (ov * sc), axis=0, keepdims=True)
        dy = gate * dxo_v
        dsc_ref[...] += jnp.sum(dy * ov, axis=0, keepdims=True)
        dout = (dy * sc).astype(BF16)
        dout_b = jnp.where(i == nblk - 1, 0.0, gate * dxb_ref[...] * sc).astype(BF16)
        for gi, w in enumerate(POOL_WINDOWS):
            cols = slice(gi * gd, (gi + 1) * gd)
            dw_sc[gi] += _dot_tn(p_ref[:, cols], dout[:, cols])
            wb = w_ref[gi]
            dp = _dot_nt(dout[:, cols], wb)
            dp_b = _dot_nt(dout_b[:, cols], wb)
            e = dp / _pool_counts(tm, i, w)
            t_below = (i + 1) * tm + lax.broadcasted_iota(jnp.int32, (POOL_HALO, 1), 0)
            e_b = dp_b / jnp.minimum(t_below + 1, w).astype(F32)
            acc = jnp.concatenate([e, e_b], axis=0)
            step = 1
            while step < w:
                acc = acc + pltpu.roll(acc, tm + POOL_HALO - step, 0)
                step *= 2
            dx_ref[:, cols] = acc[:tm, :] - dp
        dx, _ = _modulate_bwd(x_ref[...], dx_ref[...], vec_ref, dvec_ref)
        dx_ref[...] = dxo_v + dx

        @pl.when(i == nblk - 1)
        def _():
            dw_ref[...] = dw_sc[...].astype(BF16)

    nh = tm // POOL_HALO
    row = pl.BlockSpec((tm, d), lambda i: (i, 0))
    vecs = pl.BlockSpec((8, d), lambda i: (0, 0))
    wspec = pl.BlockSpec((groups, gd, gd), lambda i: (0, 0, 0))
    return _call(
        body, "pool_bwd", (nblk,),
        [
            row,
            pl.BlockSpec((POOL_HALO, d), lambda i: (jnp.minimum((i + 1) * nh, s // POOL_HALO - 1), 0)),
            row, vecs, row, row, wspec,
            pl.BlockSpec((1, d), lambda i: (0, 0)),
        ],
        [row, wspec, pl.BlockSpec((1, d), lambda i: (0, 0)), vecs],
        [
            jax.ShapeDtypeStruct((s, d), F32),
            jax.ShapeDtypeStruct((groups, gd, gd), BF16),
            jax.ShapeDtypeStruct((1, d), F32),
            jax.ShapeDtypeStruct((8, d), F32),
        ],
        [dxo, dxo, x, vec, p, o, w_grp, scale],
        scratch=[pltpu.VMEM((groups, gd, gd), F32)], phases=phases,
    )


def _loss_head(x, gain, target, phases=()):
    s, d = x.shape
    tm = _pick(s, (512, 256, 128))

    def body(x_ref, g_ref, t_ref, dx_ref, aux_ref):
        @pl.when(pl.program_id(0) == 0)
        def _():
            aux_ref[...] = jnp.zeros_like(aux_ref)

        xv = x_ref[...]
        rstd = _rstd(xv)
        r = xv * rstd
        gain_v = g_ref[...]
        err = r * gain_v - t_ref[...]
        aux_ref[1:2, :] += jnp.sum(err * err, axis=0, keepdims=True)
        dout = err * (1.0 / d)
        aux_ref[0:1, :] += jnp.sum(dout * r, axis=0, keepdims=True)
        dr = dout * gain_v
        dx_ref[...] = rstd * (dr - r * jnp.mean(dr * r, axis=-1, keepdims=True))

    row = pl.BlockSpec((tm, d), lambda i: (i, 0))
    return _call(
        body, "loss_head", (s // tm,),
        [row, pl.BlockSpec((1, d), lambda i: (0, 0)), row],
        [row, pl.BlockSpec((8, d), lambda i: (0, 0))],
        [jax.ShapeDtypeStruct((s, d), F32), jax.ShapeDtypeStruct((8, d), F32)], [x, gain, target], phases=phases,
    )


def _small_adam(gathered, gathered_ws, layout, smalls, chip):
    names = list(smalls)
    n = len(names)

    def body(*refs):
        chip_ref, g_ref, gws_ref = refs[0], refs[1], refs[2]
        wmv = refs[3 : 3 + 3 * n]
        outs = refs[3 + 3 * n : 3 + 7 * n]
        total = refs[-1]
        total[...] = g_ref[0]
        for kdev in range(1, N_DEV):
            total[...] += g_ref[kdev]
        total_ws = gws_ref[0]
        for kdev in range(1, N_DEV):
            total_ws = total_ws + gws_ref[kdev]
        my_chip = chip_ref[0]
        for a, name in enumerate(names):
            w_ref, m_ref, v_ref = wmv[3 * a : 3 * a + 3]
            if name == "ab_w_s":
                g = total_ws
            else:
                row0, rows, col0, cols = layout[name]
                if col0 is None:
                    g = jnp.zeros((rows, cols), F32)
                    for j in range(N_CHIPS):
                        g = g + jnp.where(my_chip == j, total[row0 : row0 + rows, j * cols : (j + 1) * cols], 0.0)
                else:
                    g = total[row0 : row0 + rows, col0 : col0 + cols]
            dl, mo, vo = _adam(w_ref[...], g, m_ref[...], v_ref[...])
            outs[4 * a][...] = g
            outs[4 * a + 1][...] = dl
            outs[4 * a + 2][...] = mo
            outs[4 * a + 3][...] = vo

    ins = [gathered, gathered_ws]
    out_shapes = []
    for name in names:
        ins.extend(smalls[name])
        out_shapes.extend([jax.ShapeDtypeStruct(smalls[name][0].shape, F32)] * 4)
    whole = lambda shape: pl.BlockSpec(shape, functools.partial(lambda nd, i, c: (0,) * nd, len(shape)))
    res = pl.pallas_call(
        body, name="small_adam",
        grid_spec=pltpu.PrefetchScalarGridSpec(
            num_scalar_prefetch=1, grid=(1,),
            in_specs=[whole(a.shape) for a in ins], out_specs=[whole(o.shape) for o in out_shapes],
            scratch_shapes=[pltpu.VMEM(gathered.shape[1:], F32)],
        ),
        out_shape=out_shapes,
        compiler_params=pltpu.CompilerParams(dimension_semantics=("arbitrary",), vmem_limit_bytes=VMEM_LIMIT_BYTES),
    )(chip.reshape(1).astype(jnp.int32), *ins)
    return {name: res[4 * a : 4 * a + 4] for a, name in enumerate(names)}


def _pad_rows(a, rows=8):
    extra = (-a.shape[0]) % rows
    return jnp.pad(a, ((0, extra), (0, 0))) if extra else a


def _pad_cols(a, cols):
    return jnp.pad(a, ((0, 0), (0, cols - a.shape[1]))) if a.shape[1] < cols else a


def _run(fn, *phases):
    outs, p_outs = fn(list(phases))
    for p, po in zip(phases, p_outs):
        p.then(po)
    return outs


def kernel(x, c, norm_g, w_mod, b_mod, w_ffn_in, w_ffn_out, ab_w_in, ab_norm_v, ab_w_s, ab_b_s, ab_conv_w, ab_w_out, pool_w_grp, pool_scale, final_g, loss_target, m_norm_g, m_w_mod, m_b_mod, m_w_ffn_in, m_w_ffn_out, m_ab_w_in, m_ab_norm_v, m_ab_w_s, m_ab_b_s, m_ab_conv_w, m_ab_w_out, m_pool_w_grp, m_pool_scale, m_final_g, v_norm_g, v_w_mod, v_b_mod, v_w_ffn_in, v_w_ffn_out, v_ab_w_in, v_ab_norm_v, v_ab_w_s, v_ab_b_s, v_ab_conv_w, v_ab_w_out, v_pool_w_grp, v_pool_scale, v_final_g):
    ix, iy, ic = _place()
    chip = 2 * ix + iy
    me = 4 * ix + 2 * iy + ic
    where = jnp.stack([chip, ic]).astype(jnp.int32)
    s, d = x.shape[1], x.shape[2]
    x0 = x.reshape(s, d)
    target = loss_target.reshape(s, d)
    n_layers = norm_g.shape[0]
    dq = d // N_CHIPS
    heads, chunk = ab_w_s.shape[1], ab_w_s.shape[2]
    da = ab_norm_v.shape[1]
    db = ab_conv_w.shape[2] * N_CHIPS
    f_hidden = w_ffn_out.shape[2] * N_CHIPS
    assert n_layers == 2 and da % heads == 0

    cw_pad = _pad_cols(ab_conv_w.reshape(3, db // N_CHIPS), dq)
    packed = jnp.concatenate(
        [_pad_rows(c.reshape(N_CHIPS, dq)), _pad_rows(norm_g.reshape(-1, dq)), _pad_rows(pool_scale.reshape(1, dq)), _pad_rows(cw_pad)],
        axis=0,
    )
    ncol = w_mod.shape[2]
    b_cols = lax.dynamic_slice(b_mod, (0, chip * ncol), (n_layers, ncol)).reshape(n_layers, 1, ncol)
    small = {}

    def small_gather(key, arrs):
        def then(outs):
            small[key] = outs

        return _phase_small_gather(arrs, then)

    stacks = {
        "w_ffn_in": tuple(a.reshape((-1,) + a.shape[2:]) for a in (w_ffn_in, m_w_ffn_in, v_w_ffn_in)),
        "w_ffn_out": tuple(a.reshape((-1,) + a.shape[2:]) for a in (w_ffn_out, m_w_ffn_out, v_w_ffn_out)),
        "ab_w_in": (ab_w_in, m_ab_w_in, v_ab_w_in),
        "ab_w_out": (ab_w_out, m_ab_w_out, v_ab_w_out),
        "pool_w_grp": (pool_w_grp[0], m_pool_w_grp[0], v_pool_w_grp[0]),
    }
    big_in = _Big((1, d, 2 * f_hidden), 2, 1)
    big_out = _Big((1, f_hidden, d), 1, 2)
    units = {}
    for l in range(n_layers):
        for k in range(2):
            units[f"in{l}{k}"] = (big_in, "w_ffn_in", 2 * l + k)
            units[f"out{l}{k}"] = (big_out, "w_ffn_out", 2 * l + k)
    units["abin"] = (_Big((1, d, ab_w_in.shape[2] * N_CHIPS), 2, 1), "ab_w_in", 0)
    units["about"] = (_Big((1, ab_w_out.shape[1] * N_CHIPS, d), 1, 2), "ab_w_out", 0)
    units["pool"] = (_Big((pool_w_grp.shape[1], pool_w_grp.shape[2] * N_CHIPS, pool_w_grp.shape[3]), 1, 0), "pool_w_grp", 0)
    big = {u: g for u, (g, _, _) in units.items()}

    weight = {}
    complete = set()

    def cast(u):
        g, st, b0 = units[u]

        def launch(phases):
            (weight[u],), p_outs = _cast_into_full(stacks[st][0], b0, g, where, "cast_" + u, phases)
            return None, p_outs

        return launch

    def gather_ici(*us):
        def then(outs):
            for u, o in zip(us, outs):
                weight[u] = o

        return _phase_gather_ici([weight[u] for u in us], [big[u] for u in us], then)

    def gather_sibling(*us):
        def then(outs):
            for u, o in zip(us, outs):
                weight[u] = o
                complete.add(u)

        return _phase_gather_sibling([weight[u] for u in us], [big[u] for u in us], then)

    def w_of(u):
        assert u in complete, u
        return weight[u]

    _run(cast("in00"), small_gather("inputs", [packed]))
    _run(cast("out00"))
    small_all = small["inputs"][0]
    by_chip = small_all[0::2]
    c_all = small_all[:, 0:N_CHIPS, :].reshape(N_DEV, d)
    norm_full = by_chip[:, 8 : 8 + 3 * n_layers, :].transpose(1, 0, 2).reshape(3 * n_layers, d)
    pool_scale_full = by_chip[:, 16:17, :].transpose(1, 0, 2).reshape(1, d)
    conv_full = by_chip[:, 24:27, : db // N_CHIPS].transpose(1, 0, 2).reshape(3, db)
    mod_cols = _run(lambda phases: _mod_fwd(c_all, w_mod, b_cols, phases), gather_ici("in00"))[0]
    _run(cast("abin"), gather_sibling("in00"), gather_ici("out00"), small_gather("mod", [mod_cols.reshape(n_layers * N_DEV, ncol)]))
    _run(cast("about"), gather_sibling("out00"), gather_ici("abin"))
    _run(cast("in01"), gather_sibling("abin"), gather_ici("about"))
    _run(cast("out01"), gather_sibling("about"))
    for u in ("in10", "out10", "pool", "in11", "out11"):
        _run(cast(u))
    mod_all = small["mod"][0]
    mod_mine = lax.dynamic_index_in_dim(mod_all[0::2].reshape(N_CHIPS, n_layers, N_DEV, ncol), me, axis=2, keepdims=False)
    mod = mod_mine.transpose(1, 0, 2).reshape(n_layers, 3, 3, d)
    vecs = {
        (l, sub): _pad_rows(jnp.concatenate([norm_full[3 * l + sub][None], mod[l, sub]], axis=0))
        for l in range(n_layers)
        for sub in range(3)
    }
    b_rows = jnp.broadcast_to(ab_b_s[0].T[:, :, None], (chunk, heads, da // heads)).reshape(chunk, da)

    saved = {}

    def ffn_forward(xs, l, sub, k, *phases):
        saved[l, sub, "x"] = xs
        xs, gg, uu, yb = _run(
            lambda ph: _ffn_fwd(xs, vecs[l, sub], w_of(f"in{l}{k}"), w_of(f"out{l}{k}"), f"ffn_fwd_{l}{k}", ph), *phases
        )
        saved[l, sub, "act"] = (gg, uu, yb)
        return xs

    xs = ffn_forward(x0, 0, 0, 0, gather_ici("in01"))
    saved[0, 1, "x"] = xs
    (proj,) = _run(lambda ph: _proj_mod_fwd(xs, vecs[0, 1], w_of("abin"), ph), gather_sibling("in01"), gather_ici("out01"))
    (cat,) = _run(lambda ph: _ab_mix_fwd(proj, ab_norm_v, ab_w_s[0], b_rows, conv_full, ph), gather_sibling("out01"), gather_ici("in10"))
    xs, yb = _run(lambda ph: _proj_res_fwd(cat, w_of("about"), xs, vecs[0, 1], ph), gather_sibling("in10"), gather_ici("out10", "pool"))
    saved[0, 1, "act"] = (proj, cat, yb)
    xs = ffn_forward(xs, 0, 2, 1, gather_sibling("out10", "pool"), gather_ici("in11"))
    xs = ffn_forward(xs, 1, 0, 0, gather_sibling("in11"), gather_ici("out11"))
    saved[1, 1, "x"] = xs
    xs, pp, oo = _run(lambda ph: _pool_fwd(xs, vecs[1, 1], w_of("pool"), pool_scale_full, ph), gather_sibling("out11"))
    saved[1, 1, "act"] = (pp, oo)
    xs = ffn_forward(xs, 1, 2, 1)
    dxs, aux = _run(lambda ph: _loss_head(xs, final_g.reshape(1, d), target, ph))
    loss = lax.psum(0.5 * jnp.sum(aux[1]) / d, ("x", "y", "c"))

    grad = {}
    recv = {}
    csum = {}
    parts = {}
    reduced = {}
    done = set()
    dvecs, small_g = {}, {}

    def pair_exchange(*us):
        def then(outs):
            for u, o in zip(us, outs):
                recv[u] = o

        return _phase_pair_exchange([grad[u] for u in us], [big[u] for u in us], then)

    def grad_half(u, a, b, mine, name, *phases, col0=0, prev=None):
        (res,) = _run(lambda ph: _grad_half(a, b, big[u], where, mine, col0, prev, recv[u] if mine else None, name, ph), *phases)
        return res

    def pair_sum(u, *phases):
        def launch(ph):
            (csum[u],), p_outs = _pair_sum(grad[u], recv[u], big[u], where, "pair_sum_" + u, ph)
            return None, p_outs

        _run(launch, *phases)

    def chip_exchange(*us):
        def then(outs):
            for u, o in zip(us, outs):
                parts[u] = o

        return _phase_chip_exchange([csum[u] for u in us], [big[u] for u in us], then)

    def chip_sum(*us, carried=()):
        for n_u, u in enumerate(us):
            g, st, b0 = units[u]

            def launch(ph):
                (reduced[st],), p_outs = _chip_sum(
                    csum[u], parts[u], g, where, reduced.get(st), stacks[st][0].shape, b0, "chip_sum_" + u, ph
                )
                return None, p_outs

            _run(launch, *(carried if n_u == 0 else ()))

    def pair_broadcast(*us):
        sts = [units[u][1] for u in us]
        assert len(set(sts)) == len(sts)

        def then(outs):
            for u, st, o in zip(us, sts, outs):
                reduced[st] = o
                done.add(u)

        return _phase_pair_broadcast([reduced[st] for st in sts], [big[u] for u in us], [units[u][2] for u in us], then)

    def ffn_backward(dxs, l, sub, k, carried_bwd, carried_send, carried_mine):
        gg, uu, yb = saved[l, sub, "act"]
        w_in, w_out = w_of(f"in{l}{k}"), w_of(f"out{l}{k}")
        uo, ui, tag = f"out{l}{k}", f"in{l}{k}", f"{l}{k}"
        with_act, with_in = carried_bwd()
        dy, a, dgu, dgate = _run(lambda ph: _ffn_bwd_act(dxs, vecs[l, sub], gg, uu, yb, w_out, "ffn_bwd_act_" + tag, ph), *with_act)
        dxs, h, dvecs[l, sub] = _run(
            lambda ph: _proj_mod_bwd(dgu, w_in, saved[l, sub, "x"], vecs[l, sub], dxs, dgate, "ffn_bwd_in_" + tag, ph), *with_in
        )
        grad[uo] = grad_half(uo, a, dy, False, "dw_out_send_" + tag, *carried_send())
        part = grad_half(ui, h, (dgu, 1), False, "dw_in_u_send_" + tag, pair_exchange(uo), col0=f_hidden)
        grad[ui] = grad_half(ui, h, (dgu, 0), False, "dw_in_g_send_" + tag, prev=part)
        csum[uo] = grad_half(uo, a, dy, True, "dw_out_" + tag, pair_exchange(ui))
        part = grad_half(ui, h, (dgu, 1), True, "dw_in_u_" + tag, *carried_mine(), col0=f_hidden)
        csum[ui] = grad_half(ui, h, (dgu, 0), True, "dw_in_g_" + tag, prev=part)
        return dxs

    none = lambda: ()
    dxs = ffn_backward(dxs, 1, 2, 1, lambda: ((), ()), none, none)
    pp, oo = saved[1, 1, "act"]
    dxs, grad["pool"], small_g["pool_scale"], dvecs[1, 1] = _run(
        lambda ph: _pool_bwd(dxs, saved[1, 1, "x"], vecs[1, 1], pp, oo, w_of("pool"), pool_scale_full, ph)
    )

    def after_11():
        return (chip_exchange("out11"), pair_exchange("pool")), (chip_exchange("in11"),)

    def bcast_11():
        chip_sum("in11", "out11")
        pair_sum("pool")
        return (pair_broadcast("in11", "out11"), chip_exchange("pool"))

    dxs = ffn_backward(dxs, 1, 0, 0, after_11, bcast_11, none)

    def after_10():
        return (chip_exchange("out10"),), (chip_exchange("in10"),)

    def bcast_10():
        chip_sum("in10", "out10", "pool")
        return (pair_broadcast("in10", "out10", "pool"),)

    dxs = ffn_backward(dxs, 0, 2, 1, after_10, bcast_10, none)

    proj, cat, yb = saved[0, 1, "act"]
    dy, dcat, dgate = _run(lambda ph: _proj_res_bwd(dxs, yb, vecs[0, 1], w_of("about"), ph))
    grad["about"] = grad_half("about", cat, dy, False, "dw_ab_out_send")
    dproj, small_g["ab_norm_v"], small_g["ab_w_s"], dzs, small_g["ab_conv_w"] = _run(
        lambda ph: _ab_mix_bwd(proj, dcat, ab_norm_v, ab_w_s[0], b_rows, conv_full, ph), chip_exchange("out01"), pair_exchange("about")
    )
    small_g["ab_b_s"] = dzs.reshape(chunk, heads, da // heads).sum(axis=2).T
    dxs, h, dvecs[0, 1] = _run(
        lambda ph: _proj_mod_bwd(dproj[None], w_of("abin"), saved[0, 1, "x"], vecs[0, 1], dxs, dgate, "ab_in_bwd", ph)
    )
    grad["abin"] = grad_half("abin", h, dproj, False, "dw_ab_in_send")
    chip_sum("out01", carried=(pair_exchange("abin"),))
    csum["about"] = grad_half("about", cat, dy, True, "dw_ab_out", pair_broadcast("out01"))
    csum["abin"] = grad_half("abin", h, dproj, True, "dw_ab_in")

    def after_01():
        return (chip_exchange("abin", "about"),), (chip_exchange("in01"),)

    def bcast_01():
        chip_sum("in01", "abin", "about")
        return (pair_broadcast("in01", "abin", "about"),)

    def reduce_out00():
        return (chip_exchange("out00"),)

    dxs = ffn_backward(dxs, 0, 0, 0, after_01, bcast_01, reduce_out00)
    grad_x = dxs.reshape(x.shape)

    dgain = jnp.stack([dvecs[l, sub][0] for l in range(n_layers) for sub in range(3)])
    dmod = jnp.concatenate([dvecs[l, sub][1:4] for l in range(n_layers) for sub in range(3)], axis=0)
    pieces = {
        "norm_g": (dgain, None, dq), "final_g": (aux[0:1], 0, d), "pool_scale": (small_g["pool_scale"], None, dq),
        "b_mod": (dmod, 0, d), "ab_norm_v": (small_g["ab_norm_v"], 0, da), "ab_conv_w": (small_g["ab_conv_w"], None, db // N_CHIPS),
        "ab_b_s": (small_g["ab_b_s"], 0, chunk),
    }
    layout, row0 = {}, 0
    for nm, (pc, col0, cols) in pieces.items():
        layout[nm] = (row0, pc.shape[0], col0, cols)
        row0 += pc.shape[0]
    packed_rows = -(-row0 // 8) * 8
    packed_g = sum(
        jnp.pad(pc, ((layout[nm][0], packed_rows - layout[nm][0] - pc.shape[0]), (0, d - pc.shape[1])))
        for nm, (pc, _, _) in pieces.items()
    )

    chip_sum("out00")
    _flush(
        "reduce_last", chip_exchange("in00"), pair_broadcast("out00"),
        small_gather("grads", [packed_g, small_g["ab_w_s"].reshape(heads * chunk, chunk)]),
    )
    chip_sum("in00")
    _flush("broadcast_last", pair_broadcast("in00"))
    g_all, gws_all = small["grads"]

    assert done == set(units)
    out = {}
    for st, (w3, m3, v3) in stacks.items():
        shape = {"w_ffn_in": w_ffn_in.shape, "w_ffn_out": w_ffn_out.shape, "pool_w_grp": pool_w_grp.shape}.get(st, w3.shape)
        out[st] = tuple(a.reshape(shape) for a in _adam_rows(w3, reduced[st], m3, v3, 0, w3.shape[0], None, "adam_" + st))

    shapes2d = {
        "norm_g": (3 * n_layers, dq), "b_mod": (9 * n_layers, d), "final_g": (1, d), "ab_norm_v": (1, da),
        "pool_scale": (1, dq), "ab_conv_w": (3, db // N_CHIPS), "ab_b_s": (heads, chunk), "ab_w_s": (heads * chunk, chunk),
    }
    small_w = {"norm_g": (norm_g, m_norm_g, v_norm_g), "b_mod": (b_mod, m_b_mod, v_b_mod), "final_g": (final_g, m_final_g, v_final_g),
               "ab_norm_v": (ab_norm_v, m_ab_norm_v, v_ab_norm_v), "pool_scale": (pool_scale, m_pool_scale, v_pool_scale),
               "ab_conv_w": (ab_conv_w, m_ab_conv_w, v_ab_conv_w), "ab_b_s": (ab_b_s, m_ab_b_s, v_ab_b_s), "ab_w_s": (ab_w_s, m_ab_w_s, v_ab_w_s)}
    smalls = {nm: tuple(a.reshape(shapes2d[nm]) for a in wmv) for nm, wmv in small_w.items()}
    small_out = _small_adam(g_all, gws_all, layout, smalls, chip)
    for nm, res in small_out.items():
        out[nm] = tuple(a.reshape(small_w[nm][0].shape) for a in res)

    mod_row0 = layout["b_mod"][0]
    dmod_all = g_all[:, mod_row0 : mod_row0 + 9 * n_layers, :].reshape(N_DEV, n_layers, 9 * d)
    dmod_cols = lax.dynamic_slice(dmod_all, (0, 0, chip * ncol), (N_DEV, n_layers, ncol)).transpose(1, 0, 2)
    out["w_mod"] = tuple(_run(lambda ph: _mod_bwd_adam(c_all.T, dmod_cols, w_mod, m_w_mod, v_w_mod, ph)))

    order = ["norm_g", "w_mod", "b_mod", "w_ffn_in", "w_ffn_out", "ab_w_in", "ab_norm_v", "ab_w_s", "ab_b_s", "ab_conv_w", "ab_w_out", "pool_w_grp", "pool_scale", "final_g"]
    return (loss, grad_x, *[out[nm][0] for nm in order], *[out[nm][1] for nm in order], *[out[nm][2] for nm in order], *[out[nm][3] for nm in order])
```

```python
import functools
import math

import jax
import jax.numpy as jnp
from jax import lax
from jax.experimental import pallas as pl
from jax.experimental.pallas import tpu as pltpu

F32 = jnp.float32
BF16 = jnp.bfloat16
MESH = pl.DeviceIdType.MESH

EPS = 1e-6
ADAM_LR = 0.001
ADAM_B1 = 0.9
ADAM_B2 = 0.999
ADAM_EPS = 1e-08
ADAM_WD = 0.01
ADAM_STEP = 10
POOL_WINDOWS = (2, 4, 8, 16)
POOL_HALO = 16
CONV_HALO = 8
N_CHIPS = 4
N_DEV = 8
VMEM_LIMIT_BYTES = 48 * 1024 * 1024
EW_BLOCK_ELEMS = 256 * 1024


def _pick(n, prefs):
    for p in prefs:
        if p <= n and n % p == 0:
            return p
    return n


def _row_tile(rows, cols):
    best = None
    for d in range(16, rows + 1, 16):
        if rows % d == 0 and d * cols <= EW_BLOCK_ELEMS:
            best = d
    return best or rows


def _dot(a, b):
    return jnp.dot(a, b, preferred_element_type=F32)


def _dot_nt(a, b):
    return lax.dot_general(a, b, (((1,), (1,)), ((), ())), preferred_element_type=F32)


def _dot_tn(a, b):
    return lax.dot_general(a, b, (((0,), (0,)), ((), ())), preferred_element_type=F32)


def _sigmoid(x):
    return 0.5 * jnp.tanh(0.5 * x) + 0.5


_GELU_C = math.sqrt(2.0 / math.pi)


def _gelu(x):
    x2 = x * x
    t = jnp.tanh(_GELU_C * (x + 0.044715 * x2 * x))
    val = 0.5 * x * (1.0 + t)
    grad = 0.5 * (1.0 + t) + 0.5 * x * (1.0 - t * t) * (_GELU_C * (1.0 + 3.0 * 0.044715 * x2))
    return val, grad


def _rstd(x):
    return lax.rsqrt(jnp.mean(x * x, axis=-1, keepdims=True) + EPS)


def _modulate(x, vec_ref):
    return (x * _rstd(x)) * vec_ref[0:1, :] * (1.0 + vec_ref[2:3, :]) + vec_ref[1:2, :]


def _modulate_bwd(x, dh, vec_ref, dvec_ref):
    gn, sh, sc = vec_ref[0:1, :], vec_ref[1:2, :], vec_ref[2:3, :]
    rstd = _rstd(x)
    r = x * rstd
    dvec_ref[0:1, :] += jnp.sum(dh * r * (1.0 + sc), axis=0, keepdims=True)
    dvec_ref[1:2, :] += jnp.sum(dh, axis=0, keepdims=True)
    dvec_ref[2:3, :] += jnp.sum(dh * r * gn, axis=0, keepdims=True)
    gm = gn * (1.0 + sc)
    dr = dh * gm
    dx = rstd * (dr - r * jnp.mean(dr * r, axis=-1, keepdims=True))
    return dx, r * gm + sh


def _adam(w, g, m, v):
    m = ADAM_B1 * m + (1.0 - ADAM_B1) * g
    v = ADAM_B2 * v + (1.0 - ADAM_B2) * (g * g)
    m_hat = m / (1.0 - ADAM_B1**ADAM_STEP)
    v_hat = v / (1.0 - ADAM_B2**ADAM_STEP)
    delta = -ADAM_LR * (m_hat / (jnp.sqrt(v_hat) + ADAM_EPS) + ADAM_WD * w)
    return delta, m, v


_ANY = pl.BlockSpec(memory_space=pl.ANY)


class _Phase:
    def __init__(self, ins, out_shapes, aliases, n_sems, start, finish, then):
        self.ins, self.out_shapes, self.aliases, self.n_sems = list(ins), list(out_shapes), dict(aliases), n_sems
        self.start, self.finish, self.then = start, finish, then


def _call(body, name, grid, in_specs, out_specs, out_shape, ins, scratch=(), prefetch=(), phases=(), in_place=None):
    n_pre, n_in, n_out, n_sc = len(prefetch), len(in_specs), len(out_specs), len(scratch)
    ph_in = [len(p.ins) for p in phases]
    ph_out = [len(p.out_shapes) for p in phases]

    def kernel_body(*refs):
        pos = [0]

        def take(k):
            pos[0] += k
            return refs[pos[0] - k : pos[0]]

        pre, ins_ = take(n_pre), take(n_in)
        p_ins = [take(k) for k in ph_in]
        outs_ = take(n_out)
        p_outs = [take(k) for k in ph_out]
        sc = take(n_sc)
        sems = [take(2) for _ in phases]
        if phases:
            ids = [pl.program_id(a) for a in range(len(grid))]
            first = functools.reduce(jnp.logical_and, [i == 0 for i in ids])
            last = functools.reduce(jnp.logical_and, [i == g - 1 for i, g in zip(ids, grid)])

            @pl.when(first)
            def _():
                for p, pi, po, (send, recv) in zip(phases, p_ins, p_outs, sems):
                    p.start(pi, po, send, recv)

        if body is not None:
            body(*pre, *ins_, *outs_, *sc)
        if phases:

            @pl.when(last)
            def _():
                for p, pi, po, (send, recv) in zip(phases, p_ins, p_outs, sems):
                    p.finish(pi, po, send, recv)

    aliases = {n_pre + i: o for i, o in (in_place or {}).items()}
    i0, o0 = n_pre + n_in, n_out
    for p in phases:
        for i, o in p.aliases.items():
            aliases[i0 + i] = o0 + o
        i0 += len(p.ins)
        o0 += len(p.out_shapes)
    all_in = list(in_specs) + [_ANY] * sum(ph_in)
    all_out = list(out_specs) + [_ANY] * sum(ph_out)
    all_scratch = list(scratch)
    for p in phases:
        all_scratch += [pltpu.SemaphoreType.DMA((p.n_sems,)), pltpu.SemaphoreType.DMA((p.n_sems,))]
    shapes = list(out_shape) + [s for p in phases for s in p.out_shapes]
    operands = list(prefetch) + list(ins) + [a for p in phases for a in p.ins]
    sem = ("arbitrary",) * len(grid)
    params = pltpu.CompilerParams(dimension_semantics=sem, vmem_limit_bytes=VMEM_LIMIT_BYTES)
    if n_pre:
        res = pl.pallas_call(
            kernel_body, name=name, out_shape=shapes, input_output_aliases=aliases, compiler_params=params,
            grid_spec=pltpu.PrefetchScalarGridSpec(
                num_scalar_prefetch=n_pre, grid=grid, in_specs=all_in, out_specs=all_out, scratch_shapes=all_scratch
            ),
        )(*operands)
    else:
        res = pl.pallas_call(
            kernel_body, name=name, grid=grid, in_specs=all_in, out_specs=all_out, out_shape=shapes,
            scratch_shapes=all_scratch, input_output_aliases=aliases, compiler_params=params,
        )(*operands)
    res = list(res)
    outs, rest = res[:n_out], res[n_out:]
    p_res = []
    for k in ph_out:
        p_res.append(rest[:k])
        rest = rest[k:]
    return outs, p_res


def _place():
    return lax.axis_index("x"), lax.axis_index("y"), lax.axis_index("c")


def _other_chips():
    x, y, _ = _place()
    return [(1 - x, y), (x, 1 - y), (1 - x, 1 - y)]


def _flip(k):
    x, y, c = _place()
    return (1 - x if k & 4 else x, 1 - y if k & 2 else y, 1 - c if k & 1 else c)


def _remote(src, dst, send, recv, k, to):
    return pltpu.make_async_remote_copy(
        src_ref=src, dst_ref=dst, send_sem=send.at[k], recv_sem=recv.at[k], device_id=to, device_id_type=MESH
    )


def _phase_small_gather(arrs, then):
    n = len(arrs)

    def copies(ins, outs, send, recv):
        x, y, c = _place()
        me = 4 * x + 2 * y + c
        local = [pltpu.make_async_copy(ins[a], outs[a].at[me], send.at[a * N_DEV]) for a in range(n)]
        remote = [_remote(ins[a], outs[a].at[me], send, recv, a * N_DEV + k, _flip(k)) for a in range(n) for k in range(1, N_DEV)]
        return local, remote

    def start(ins, outs, send, recv):
        local, remote = copies(ins, outs, send, recv)
        for cp in local + remote:
            cp.start()

    def finish(ins, outs, send, recv):
        local, remote = copies(ins, outs, send, recv)
        for cp in remote + local:
            cp.wait()

    shapes = [jax.ShapeDtypeStruct((N_DEV,) + a.shape, a.dtype) for a in arrs]
    return _Phase(arrs, shapes, {}, n * N_DEV, start, finish, then)


def _flush(name, *phases):
    _, p_outs = _call(None, name, (1,), [], [], [], [], phases=list(phases))
    for p, po in zip(phases, p_outs):
        p.then(po)


class _Big:
    KINDS = {"full": (True, True), "half": (True, False), "shard": (False, True), "block": (False, False)}

    def __init__(self, f3, s3, h3):
        assert s3 != h3
        self.f3, self.s3, self.h3 = tuple(f3), s3, h3
        self.bd = tuple(f3[a] // (N_CHIPS if a == s3 else 1) // (2 if a == h3 else 1) for a in range(3))
        self.tile = (1, _row_tile(self.bd[1], self.bd[2]), self.bd[2])
        self.grid = tuple(self.bd[a] // self.tile[a] for a in range(3))

    def dims(self, kind):
        chips, halves = self.KINDS[kind]
        return tuple(
            self.bd[a] * (N_CHIPS if chips and a == self.s3 else 1) * (2 if halves and a == self.h3 else 1) for a in range(3)
        )

    def view(self, ref, chip=None, half=None, batch0=0, both_halves=True):
        start = [batch0, 0, 0]
        size = list(ref.shape)
        size[0] = self.bd[0] * (2 if self.h3 == 0 and both_halves else 1)
        if chip is not None:
            start[self.s3] += chip * self.bd[self.s3]
            size[self.s3] = self.bd[self.s3]
        if half is not None:
            start[self.h3] += half * self.bd[self.h3]
            size[self.h3] = self.bd[self.h3]
        return ref.at[tuple(pl.ds(st, sz) for st, sz in zip(start, size))]

    def spec(self, chip_from=None, half_from=None, lead=(), batch0=0):
        extra = "grid" in (chip_from, half_from)

        def index(*args):
            pref, idx = args[-1], list(args[int(extra) : -1])
            idx[0] += batch0
            if chip_from:
                idx[self.s3] += (pref[0] if chip_from == "pref" else args[0]) * self.grid[self.s3]
            if half_from:
                idx[self.h3] += (pref[1] if half_from == "pref" else args[0]) * self.grid[self.h3]
            return (0,) * len(lead) + tuple(idx)

        return pl.BlockSpec(tuple(lead) + self.tile, index)


def _same(arrs):
    return [jax.ShapeDtypeStruct(a.shape, a.dtype) for a in arrs]


def _phase_gather_ici(arrs, bigs, then):
    n = len(arrs)

    def copies(outs, send, recv, arriving):
        x, y, c = _place()
        return [
            _remote(blk, blk, send, recv, 3 * a + j, (*chip, c))
            for j, chip in enumerate(_other_chips())
            for a in range(n)
            for blk in [bigs[a].view(outs[a], 2 * chip[0] + chip[1] if arriving else 2 * x + y, c)]
        ]

    def start(ins, outs, send, recv):
        for cp in copies(outs, send, recv, False):
            cp.start()

    def finish(ins, outs, send, recv):
        for cp in copies(outs, send, recv, True):
            cp.wait_recv()
        for cp in copies(outs, send, recv, False):
            cp.wait_send()

    return _Phase(arrs, _same(arrs), {a: a for a in range(n)}, 3 * n, start, finish, then)


def _phase_gather_sibling(arrs, bigs, then):
    n = len(arrs)

    def copies(outs, send, recv, arriving):
        x, y, c = _place()
        return [
            _remote(blk, blk, send, recv, 3 * a + j, (x, y, 1 - c))
            for j, chip in enumerate(_other_chips())
            for a in range(n)
            for blk in [bigs[a].view(outs[a], 2 * chip[0] + chip[1], 1 - c if arriving else c)]
        ]

    def start(ins, outs, send, recv):
        for cp in copies(outs, send, recv, False):
            cp.start()

    def finish(ins, outs, send, recv):
        for cp in copies(outs, send, recv, True):
            cp.wait_recv()
        for cp in copies(outs, send, recv, False):
            cp.wait_send()

    return _Phase(arrs, _same(arrs), {a: a for a in range(n)}, 3 * n, start, finish, then)


def _phase_pair_exchange(grads, bigs, then):
    n = len(grads)

    def copies(ins, outs, send, recv):
        x, y, c = _place()
        srcs = [ins[a] if ins[a].shape == outs[a].shape else bigs[a].view(ins[a], None, 1 - c) for a in range(n)]
        return [_remote(srcs[a], outs[a], send, recv, a, (x, y, 1 - c)) for a in range(n)]

    def start(ins, outs, send, recv):
        for cp in copies(ins, outs, send, recv):
            cp.start()

    def finish(ins, outs, send, recv):
        for cp in copies(ins, outs, send, recv):
            cp.wait()

    shapes = [jax.ShapeDtypeStruct(b.dims("half"), BF16) for b in bigs]
    return _Phase(grads, shapes, {}, n, start, finish, then)


def _phase_chip_exchange(sums, bigs, then):
    n = len(sums)

    def copies(ins, outs, send, recv):
        _, _, c = _place()
        return [
            _remote(bigs[a].view(ins[a], 2 * chip[0] + chip[1], both_halves=False), outs[a].at[j], send, recv, 3 * a + j, (*chip, c))
            for j, chip in enumerate(_other_chips())
            for a in range(n)
        ]

    def start(ins, outs, send, recv):
        for cp in copies(ins, outs, send, recv):
            cp.start()

    def finish(ins, outs, send, recv):
        for cp in copies(ins, outs, send, recv):
            cp.wait()

    shapes = [jax.ShapeDtypeStruct((N_CHIPS - 1,) + b.dims("block"), BF16) for b in bigs]
    return _Phase(sums, shapes, {}, 3 * n, start, finish, then)


_HBM = pl.BlockSpec(memory_space=pltpu.HBM)
_SEM = pl.BlockSpec(memory_space=pltpu.SEMAPHORE)
_DATAFLOW = pltpu.SideEffectType.DATAFLOW_SIDE_EFFECTING


def _chip_exchange_copies(srcs, zones, bigs, send, recv):
    _, _, c = _place()
    return [
        _remote(bigs[a].view(srcs[a], 2 * chip[0] + chip[1], both_halves=False), zones[a].at[j], send, recv, 3 * a + j, (*chip, c))
        for j, chip in enumerate(_other_chips())
        for a in range(len(srcs))
    ]


def _chip_exchange_start(sums, bigs, name):
    n = len(sums)
    zones = [lax.empty((N_CHIPS - 1,) + b.dims("block"), BF16) for b in bigs]

    def body(*refs):
        send, recv = refs[2 * n], refs[2 * n + 1]
        for cp in _chip_exchange_copies(refs[:n], refs[n : 2 * n], bigs, send, recv):
            cp.start()
        refs[-1][...] = jnp.zeros_like(refs[-1])

    operands = [pltpu.with_memory_space_constraint(a, pltpu.HBM) for a in list(sums) + zones]
    res = pl.pallas_call(
        body, name=name,
        out_shape=[pltpu.SemaphoreType.DMA((3 * n,)), pltpu.SemaphoreType.DMA((3 * n,))]
        + [pltpu.HBM(a.shape, a.dtype) for a in operands] + [jax.ShapeDtypeStruct((8, 128), F32)],
        in_specs=[_HBM] * (2 * n), out_specs=[_SEM, _SEM] + [_HBM] * (2 * n) + [pl.BlockSpec(memory_space=pltpu.VMEM)],
        input_output_aliases={i: 2 + i for i in range(2 * n)},
        compiler_params=pltpu.CompilerParams(has_side_effects=_DATAFLOW),
    )(*operands)
    return res[0], res[1], list(res[2 : 2 + n]), list(res[2 + n : 2 + 2 * n])


def _chip_exchange_wait(send, recv, sums, zones, bigs, after, name):
    n = len(sums)

    def body(*refs):
        for cp in _chip_exchange_copies(refs[:n], refs[n : 2 * n], bigs, refs[2 * n], refs[2 * n + 1]):
            cp.wait_send()
            cp.wait_recv()

    res = pl.pallas_call(
        body, name=name, out_shape=[pltpu.HBM(a.shape, a.dtype) for a in list(sums) + list(zones)],
        in_specs=[_HBM] * (2 * n) + [_SEM, _SEM, _ANY], out_specs=[_HBM] * (2 * n),
        input_output_aliases={i: i for i in range(2 * n)},
        compiler_params=pltpu.CompilerParams(has_side_effects=_DATAFLOW),
    )(*sums, *zones, send, recv, after)
    return list(res[:n]), list(res[n:])


def _phase_pair_broadcast(stacks, bigs, batch0s, then):
    n = len(stacks)

    def start(ins, outs, send, recv):
        x, y, c = _place()
        for a in range(n):
            blk = bigs[a].view(outs[a], None, c, batch0s[a])
            _remote(blk, blk, send, recv, a, (x, y, 1 - c)).start()

    def finish(ins, outs, send, recv):
        x, y, c = _place()
        for a in range(n):
            mine = bigs[a].view(outs[a], None, c, batch0s[a])
            theirs = bigs[a].view(outs[a], None, 1 - c, batch0s[a])
            _remote(mine, mine, send, recv, a, (x, y, 1 - c)).wait_send()
            _remote(theirs, theirs, send, recv, a, (x, y, 1 - c)).wait_recv()

    return _Phase(stacks, _same(stacks), {a: a for a in range(n)}, n, start, finish, then)


def _tile_call(body, name, big, where, extra, ins, in_specs, out_specs, out_shape, phases=()):
    grid = ((extra,) if extra else ()) + big.grid
    return _call(body, name, grid, in_specs, out_specs, out_shape, ins, prefetch=(where,), phases=phases)


def _cast_into_full(w_stack, batch0, big, where, name, phases=()):
    def body(_, w_ref, o_ref):
        o_ref[...] = w_ref[...].astype(BF16)

    return _tile_call(
        body, name, big, where, 2, [w_stack], [big.spec(None, "grid", batch0=batch0)], [big.spec("pref", "grid")],
        [jax.ShapeDtypeStruct(big.dims("full"), BF16)], phases,
    )


def _pair_sum(g_full, recv_half, big, where, name, phases=()):
    def body(_, g_ref, r_ref, o_ref):
        o_ref[...] = (g_ref[...].astype(F32) + r_ref[...].astype(F32)).astype(BF16)

    half = big.spec("grid", None)
    return _tile_call(
        body, name, big, where, N_CHIPS, [g_full, recv_half], [big.spec("grid", "pref"), half], [half],
        [jax.ShapeDtypeStruct(big.dims("half"), BF16)], phases,
    )


def _chip_sum(chip_sum, parts, big, where, stack, stack_shape, batch0, name, phases=()):
    def body(_, own_ref, p_ref, *rest):
        acc = own_ref[...].astype(F32)
        for k in range(N_CHIPS - 1):
            acc = acc + p_ref[k].astype(F32)
        rest[-1][...] = acc

    ins = [chip_sum, parts] + ([stack] if stack is not None else [])
    in_specs = [big.spec("pref", None), big.spec(None, None, lead=(N_CHIPS - 1,))] + ([_ANY] if stack is not None else [])
    return _call(
        body, name, big.grid, in_specs, [big.spec(None, "pref", batch0=batch0)], [jax.ShapeDtypeStruct(stack_shape, F32)], ins,
        prefetch=(where,), phases=phases, in_place={2: 0} if stack is not None else None,
    )


def _adam_rows(w, g, m, v, batch0, nb, prev, name):
    b, r, c = w.shape
    tr = _row_tile(r, c)

    def body(w_ref, g_ref, m_ref, v_ref, *rest):
        go_ref, d_ref, mo_ref, vo_ref = rest[-4:]
        gv = g_ref[...]
        d, mo, vo = _adam(w_ref[...], gv, m_ref[...], v_ref[...])
        go_ref[...] = gv
        d_ref[...] = d
        mo_ref[...] = mo
        vo_ref[...] = vo

    spec = pl.BlockSpec((1, tr, c), lambda bb, i: (batch0 + bb, i, 0))
    ins = [w, g, m, v] + (list(prev) if prev is not None else [])
    return pl.pallas_call(
        body, name=name, grid=(nb, r // tr), in_specs=[spec] * 4 + ([_ANY] * 4 if prev is not None else []), out_specs=[spec] * 4,
        out_shape=[jax.ShapeDtypeStruct(w.shape, F32)] * 4,
        input_output_aliases={4: 0, 5: 1, 6: 2, 7: 3} if prev is not None else {},
        compiler_params=pltpu.CompilerParams(dimension_semantics=("arbitrary",) * 2, vmem_limit_bytes=VMEM_LIMIT_BYTES),
    )(*ins)


def _mod_fwd(c_all, w_mod, b_cols, phases=()):
    n_layers, d, n = w_mod.shape
    tn = _pick(n, (768, 512, 384, 256, 128))

    def body(c_ref, w_ref, b_ref, o_ref):
        cv = c_ref[...]
        ca = (cv * _sigmoid(cv)).astype(BF16)
        o_ref[0] = _dot(ca, w_ref[0].astype(BF16)) + b_ref[0]

    return _call(
        body, "mod_fwd", (n_layers, n // tn),
        [
            pl.BlockSpec((N_DEV, d), lambda l, j: (0, 0)),
            pl.BlockSpec((1, d, tn), lambda l, j: (l, 0, j)),
            pl.BlockSpec((1, 1, tn), lambda l, j: (l, 0, j)),
        ],
        [pl.BlockSpec((1, N_DEV, tn), lambda l, j: (l, 0, j))],
        [jax.ShapeDtypeStruct((n_layers, N_DEV, n), F32)], [c_all, w_mod, b_cols], phases=phases,
    )


def _mod_bwd_adam(c_all_t, dmod_cols, w, m, v, phases=()):
    n_layers, d, n = w.shape
    tn = _pick(n, (384, 256, 128))

    def body(c_ref, dm_ref, w_ref, m_ref, v_ref, g_ref, d_ref, mo_ref, vo_ref):
        cv = c_ref[...]
        ca = (cv * _sigmoid(cv)).astype(BF16)
        g = _dot(ca, dm_ref[0].astype(BF16))
        g_ref[0] = g
        dl, mo, vo = _adam(w_ref[0], g, m_ref[0], v_ref[0])
        d_ref[0] = dl
        mo_ref[0] = mo
        vo_ref[0] = vo

    wspec = pl.BlockSpec((1, d, tn), lambda l, j: (l, 0, j))
    return _call(
        body, "mod_bwd_adam", (n_layers, n // tn),
        [pl.BlockSpec((d, N_DEV), lambda l, j: (0, 0)), pl.BlockSpec((1, N_DEV, tn), lambda l, j: (l, 0, j)), wspec, wspec, wspec],
        [wspec] * 4, [jax.ShapeDtypeStruct(w.shape, F32)] * 4, [c_all_t, dmod_cols, w, m, v], phases=phases,
    )


def _ffn_fwd(x, vec, w_in, w_out, name, phases=()):
    s, d = x.shape
    f = w_out.shape[1]
    tm = _pick(s, (1024, 512, 256, 128))
    tf = _pick(f, (256, 128))
    nf = f // tf

    def body(x_ref, vec_ref, wg_ref, wu_ref, wo_ref, xo_ref, g_ref, u_ref, y_ref, h_sc, acc_sc):
        j = pl.program_id(1)

        @pl.when(j == 0)
        def _():
            h_sc[...] = _modulate(x_ref[...], vec_ref).astype(BF16)
            acc_sc[...] = jnp.zeros_like(acc_sc)

        h = h_sc[...]
        g = _dot(h, wg_ref[0])
        u = _dot(h, wu_ref[0])
        g_ref[...] = g.astype(BF16)
        u_ref[...] = u.astype(BF16)
        a = (g * _sigmoid(g) * u).astype(BF16)
        acc_sc[...] += _dot(a, wo_ref[0])

        @pl.when(j == nf - 1)
        def _():
            yv = acc_sc[...]
            xo_ref[...] = x_ref[...] + 0.5 * vec_ref[3:4, :] * yv
            y_ref[...] = yv.astype(BF16)

    row = pl.BlockSpec((tm, d), lambda i, j: (i, 0))
    hid = pl.BlockSpec((tm, tf), lambda i, j: (i, j))
    return _call(
        body, name, (s // tm, nf),
        [
            row,
            pl.BlockSpec((8, d), lambda i, j: (0, 0)),
            pl.BlockSpec((1, d, tf), lambda i, j: (0, 0, j)),
            pl.BlockSpec((1, d, tf), lambda i, j: (0, 0, nf + j)),
            pl.BlockSpec((1, tf, d), lambda i, j: (0, j, 0)),
        ],
        [row, hid, hid, row],
        [
            jax.ShapeDtypeStruct((s, d), F32),
            jax.ShapeDtypeStruct((s, f), BF16),
            jax.ShapeDtypeStruct((s, f), BF16),
            jax.ShapeDtypeStruct((s, d), BF16),
        ],
        [x, vec, w_in, w_in, w_out],
        scratch=[pltpu.VMEM((tm, d), BF16), pltpu.VMEM((tm, d), F32)], phases=phases,
    )


def _ffn_bwd(dxo, x, vec, gg, uu, y, w_in, w_out, name, phases=()):
    s, d = x.shape
    f = w_out.shape[1]
    tm = _pick(s, (512, 256, 128))
    tf = _pick(f, (256, 128))
    nf = f // tf

    def body(dxo_ref, x_ref, vec_ref, g_ref, u_ref, y_ref, wg_ref, wu_ref, wo_ref,
             dx_ref, dg_ref, du_ref, a_ref, h_ref, dy_ref, dvec_ref, acc_sc):
        i, j = pl.program_id(0), pl.program_id(1)

        @pl.when((i == 0) & (j == 0))
        def _():
            dvec_ref[...] = jnp.zeros_like(dvec_ref)

        @pl.when(j == 0)
        def _():
            dxo_v = dxo_ref[...]
            dy_ref[...] = (0.5 * vec_ref[3:4, :] * dxo_v).astype(BF16)
            dvec_ref[3:4, :] += 0.5 * jnp.sum(dxo_v * y_ref[...].astype(F32), axis=0, keepdims=True)
            acc_sc[...] = jnp.zeros_like(acc_sc)

        da = _dot_nt(dy_ref[...], wo_ref[0])
        g = g_ref[...].astype(F32)
        u = u_ref[...].astype(F32)
        sig = _sigmoid(g)
        sl = g * sig
        a_ref[...] = (sl * u).astype(BF16)
        dg = (da * u * (sig * (1.0 + g * (1.0 - sig)))).astype(BF16)
        du = (da * sl).astype(BF16)
        dg_ref[...] = dg
        du_ref[...] = du
        acc_sc[...] += _dot_nt(dg, wg_ref[0]) + _dot_nt(du, wu_ref[0])

        @pl.when(j == nf - 1)
        def _():
            dx, h = _modulate_bwd(x_ref[...], acc_sc[...], vec_ref, dvec_ref)
            dx_ref[...] = dxo_ref[...] + dx
            h_ref[...] = h.astype(BF16)

    row = pl.BlockSpec((tm, d), lambda i, j: (i, 0))
    hid = pl.BlockSpec((tm, tf), lambda i, j: (i, j))
    vecs = pl.BlockSpec((8, d), lambda i, j: (0, 0))
    return _call(
        body, name, (s // tm, nf),
        [
            row, row, vecs, hid, hid, row,
            pl.BlockSpec((1, d, tf), lambda i, j: (0, 0, j)),
            pl.BlockSpec((1, d, tf), lambda i, j: (0, 0, nf + j)),
            pl.BlockSpec((1, tf, d), lambda i, j: (0, j, 0)),
        ],
        [row, hid, hid, hid, row, row, vecs],
        [
            jax.ShapeDtypeStruct((s, d), F32),
            jax.ShapeDtypeStruct((s, f), BF16),
            jax.ShapeDtypeStruct((s, f), BF16),
            jax.ShapeDtypeStruct((s, f), BF16),
            jax.ShapeDtypeStruct((s, d), BF16),
            jax.ShapeDtypeStruct((s, d), BF16),
            jax.ShapeDtypeStruct((8, d), F32),
        ],
        [dxo, x, vec, gg, uu, y, w_in, w_in, w_out],
        scratch=[pltpu.VMEM((tm, d), F32)], phases=phases,
    )


def _grad_half(a, b, big, where, mine, col0, prev, recv, name, phases=()):
    s, k1 = a.shape
    b, b_part = b if isinstance(b, tuple) else (b[None], 0)
    n = b.shape[2]
    rows_halved = big.h3 == 1
    kk, nn = (k1 // 2, n) if rows_halved else (k1, n // 2)
    tk = _pick(kk, (1408, 1024, 512, 256, 128))
    tn = _pick(nn, (1408, 1024, 640, 512, 256, 128))
    nkb, nnb = kk // tk, nn // tn
    assert col0 % tn == 0 and (recv is None) == (not mine)

    def half(pref):
        return pref[1] if mine else 1 - pref[1]

    def body(_, a_ref, b_ref, *rest):
        acc = _dot_tn(a_ref[...], b_ref[0])
        if recv is not None:
            acc = acc + rest[0][0].astype(F32)
        rest[-1][0] = acc.astype(BF16)

    out_spec = pl.BlockSpec((1, tk, tn), lambda i, j, pref: (0, i, col0 // tn + j))
    in_specs = [
        pl.BlockSpec((s, tk), lambda i, j, pref: (0, i + (half(pref) * nkb if rows_halved else 0))),
        pl.BlockSpec((1, s, tn), lambda i, j, pref: (b_part, 0, j + (0 if rows_halved else half(pref) * nnb))),
    ]
    ins = [a, b]
    if recv is not None:
        in_specs.append(out_spec)
        ins.append(recv)
    in_place = None
    if prev is not None:
        in_place = {len(ins): 0}
        in_specs.append(_ANY)
        ins.append(prev)
    return _call(
        body, name, (nkb, nnb), in_specs, [out_spec], [jax.ShapeDtypeStruct(big.dims("half"), BF16)], ins,
        prefetch=(where,), phases=phases, in_place=in_place,
    )


def _proj_mod_fwd(x, vec, w, phases=()):
    s, d = x.shape
    n = w.shape[2]
    tm = _pick(s, (512, 256, 128))
    tn = _pick(n, (640, 512, 256, 128))

    def body(x_ref, vec_ref, w_ref, o_ref, h_sc):
        @pl.when(pl.program_id(1) == 0)
        def _():
            h_sc[...] = _modulate(x_ref[...], vec_ref).astype(BF16)

        o_ref[...] = _dot(h_sc[...], w_ref[0])

    return _call(
        body, "ab_in_fwd", (s // tm, n // tn),
        [
            pl.BlockSpec((tm, d), lambda i, j: (i, 0)),
            pl.BlockSpec((8, d), lambda i, j: (0, 0)),
            pl.BlockSpec((1, d, tn), lambda i, j: (0, 0, j)),
        ],
        [pl.BlockSpec((tm, tn), lambda i, j: (i, j))],
        [jax.ShapeDtypeStruct((s, n), F32)], [x, vec, w],
        scratch=[pltpu.VMEM((tm, d), BF16)], phases=phases,
    )


def _proj_res_fwd(a, w, x, vec, phases=()):
    s, kd = a.shape
    d = x.shape[1]
    tm = _pick(s, (512, 256, 128))

    def body(a_ref, w_ref, x_ref, vec_ref, xo_ref, y_ref):
        yv = _dot(a_ref[...], w_ref[0])
        xo_ref[...] = x_ref[...] + vec_ref[3:4, :] * yv
        y_ref[...] = yv.astype(BF16)

    row = pl.BlockSpec((tm, d), lambda i: (i, 0))
    return _call(
        body, "ab_out_fwd", (s // tm,),
        [pl.BlockSpec((tm, kd), lambda i: (i, 0)), pl.BlockSpec((1, kd, d), lambda i: (0, 0, 0)), row, pl.BlockSpec((8, d), lambda i: (0, 0))],
        [row, row],
        [jax.ShapeDtypeStruct((s, d), F32), jax.ShapeDtypeStruct((s, d), BF16)], [a, w, x, vec], phases=phases,
    )


def _proj_res_bwd(dxo, y, vec, w, phases=()):
    s, d = dxo.shape
    kd = w.shape[1]
    tm = _pick(s, (512, 256, 128))

    def body(dxo_ref, y_ref, vec_ref, w_ref, dy_ref, da_ref, dgate_ref):
        @pl.when(pl.program_id(0) == 0)
        def _():
            dgate_ref[...] = jnp.zeros_like(dgate_ref)

        dxo_v = dxo_ref[...]
        dy = (vec_ref[3:4, :] * dxo_v).astype(BF16)
        dy_ref[...] = dy
        dgate_ref[3:4, :] += jnp.sum(dxo_v * y_ref[...].astype(F32), axis=0, keepdims=True)
        da_ref[...] = _dot_nt(dy, w_ref[0]).astype(BF16)

    row = pl.BlockSpec((tm, d), lambda i: (i, 0))
    vecs = pl.BlockSpec((8, d), lambda i: (0, 0))
    return _call(
        body, "ab_out_bwd", (s // tm,),
        [row, row, vecs, pl.BlockSpec((1, kd, d), lambda i: (0, 0, 0))],
        [row, pl.BlockSpec((tm, kd), lambda i: (i, 0)), vecs],
        [jax.ShapeDtypeStruct((s, d), BF16), jax.ShapeDtypeStruct((s, kd), BF16), jax.ShapeDtypeStruct((8, d), F32)],
        [dxo, y, vec, w], phases=phases,
    )


def _proj_mod_bwd(dproj, w, x, vec, dxo, dvec_in, name, phases=()):
    parts, s, n_part = dproj.shape
    d = x.shape[1]
    tm = _pick(s, (512, 256, 128))
    tk = _pick(n_part, (1408, 1280, 1024, 512, 256, 128))
    per_part = n_part // tk
    nk = parts * per_part

    def body(dp_ref, w_ref, x_ref, vec_ref, dxo_ref, dvi_ref, dx_ref, h_ref, dvec_ref, acc_sc):
        i, k = pl.program_id(0), pl.program_id(1)

        @pl.when((i == 0) & (k == 0))
        def _():
            dvec_ref[...] = dvi_ref[...]

        @pl.when(k == 0)
        def _():
            acc_sc[...] = jnp.zeros_like(acc_sc)

        acc_sc[...] += _dot_nt(dp_ref[0], w_ref[0])

        @pl.when(k == nk - 1)
        def _():
            dx, h = _modulate_bwd(x_ref[...], acc_sc[...], vec_ref, dvec_ref)
            dx_ref[...] = dxo_ref[...] + dx
            h_ref[...] = h.astype(BF16)

    row = pl.BlockSpec((tm, d), lambda i, k: (i, 0))
    vecs = pl.BlockSpec((8, d), lambda i, k: (0, 0))
    return _call(
        body, name, (s // tm, nk),
        [
            pl.BlockSpec((1, tm, tk), lambda i, k: (k // per_part, i, k % per_part)),
            pl.BlockSpec((1, d, tk), lambda i, k: (0, 0, k)),
            row, vecs, row, vecs,
        ],
        [row, row, vecs],
        [jax.ShapeDtypeStruct((s, d), F32), jax.ShapeDtypeStruct((s, d), BF16), jax.ShapeDtypeStruct((8, d), F32)],
        [dproj, w, x, vec, dxo, dvec_in], scratch=[pltpu.VMEM((tm, d), F32)], phases=phases,
    )


def _tril(n):
    return lax.broadcasted_iota(jnp.int32, (n, n), 0) >= lax.broadcasted_iota(jnp.int32, (n, n), 1)


def _layernorm_stats(gv):
    mu = jnp.mean(gv, axis=-1, keepdims=True)
    cen = gv - mu
    rstd = lax.rsqrt(jnp.mean(cen * cen, axis=-1, keepdims=True) + EPS)
    return cen * rstd, rstd


def _shift_down(q, k, above_ref, c_cg, c_xb, first):
    width = q.shape[1]
    rows = lax.broadcasted_iota(jnp.int32, q.shape, 0)
    out = pltpu.roll(q, k, 0)
    for r in range(k):
        src = CONV_HALO - k + r
        above = above_ref[src : src + 1, c_cg : c_cg + width] * above_ref[src : src + 1, c_xb : c_xb + width]
        above = jnp.where(first, 0.0, above)
        out = jnp.where(rows == r, above, out)
    return out


def _ab_mix_fwd(proj, norm_v, w_s, b_rows, conv_w, phases=()):
    s, n = proj.shape
    heads, chunk, _ = w_s.shape
    da = norm_v.shape[1]
    hd = da // heads
    db = conv_w.shape[1]
    tm = _pick(s, (512, 256, 128))

    def body(p_ref, ph_ref, nv_ref, ws_ref, b_ref, cw_ref, o_ref):
        first = pl.program_id(0) == 0
        gu, _ = _gelu(p_ref[:, 0:da])
        gv, _ = _gelu(p_ref[:, da : 2 * da])
        xhat, _ = _layernorm_stats(gv)
        vn = (xhat * nv_ref[...]).astype(BF16)
        mask = _tril(chunk)
        for hh in range(heads):
            wm = jnp.where(mask, ws_ref[hh], 0.0).astype(BF16)
            cols = slice(hh * hd, (hh + 1) * hd)
            for nn in range(tm // chunk):
                rows = slice(nn * chunk, (nn + 1) * chunk)
                z = _dot(wm, vn[rows, cols]) + b_ref[:, cols]
                o_ref[rows, cols] = (gu[rows, cols] * z).astype(BF16)
        c_cg, c_xb = 2 * da + db, 2 * da + 2 * db
        bg = p_ref[:, 2 * da : 2 * da + db]
        q = p_ref[:, c_cg : c_cg + db] * p_ref[:, c_xb : c_xb + db]
        q1 = _shift_down(q, 1, ph_ref, c_cg, c_xb, first)
        q2 = _shift_down(q, 2, ph_ref, c_cg, c_xb, first)
        conv = cw_ref[0:1, :] * q2 + cw_ref[1:2, :] * q1 + cw_ref[2:3, :] * q
        o_ref[:, da : da + db] = (bg * conv).astype(BF16)

    nh = tm // CONV_HALO
    return _call(
        body, "ab_mix_fwd", (s // tm,),
        [
            pl.BlockSpec((tm, n), lambda i: (i, 0)),
            pl.BlockSpec((CONV_HALO, n), lambda i: (jnp.maximum(i * nh - 1, 0), 0)),
            pl.BlockSpec((1, da), lambda i: (0, 0)),
            pl.BlockSpec((heads, chunk, chunk), lambda i: (0, 0, 0)),
            pl.BlockSpec((chunk, da), lambda i: (0, 0)),
            pl.BlockSpec((3, db), lambda i: (0, 0)),
        ],
        [pl.BlockSpec((tm, da + db), lambda i: (i, 0))],
        [jax.ShapeDtypeStruct((s, da + db), BF16)], [proj, proj, norm_v, w_s, b_rows, conv_w], phases=phases,
    )


def _ab_mix_bwd(proj, dcat, norm_v, w_s, b_rows, conv_w, phases=()):
    s, n = proj.shape
    heads, chunk, _ = w_s.shape
    da = norm_v.shape[1]
    hd = da // heads
    db = conv_w.shape[1]
    tm = _pick(s, (512, 256, 128))
    nblk = s // tm
    dhalo = 2 * CONV_HALO

    def body(p_ref, pa_ref, pb_ref, dc_ref, dcb_ref, nv_ref, ws_ref, b_ref, cw_ref,
             dp_ref, dnv_ref, dws_ref, dzs_ref, dcw_ref, dvn_sc):
        i = pl.program_id(0)
        first, last = i == 0, i == nblk - 1

        @pl.when(first)
        def _():
            dnv_ref[...] = jnp.zeros_like(dnv_ref)
            dws_ref[...] = jnp.zeros_like(dws_ref)
            dzs_ref[...] = jnp.zeros_like(dzs_ref)
            dcw_ref[...] = jnp.zeros_like(dcw_ref)

        uu = p_ref[:, 0:da]
        gu, gu_grad = _gelu(uu)
        gv, gv_grad = _gelu(p_ref[:, da : 2 * da])
        xhat, rstd = _layernorm_stats(gv)
        nv = nv_ref[...]
        vn = (xhat * nv).astype(BF16)
        dya = dc_ref[:, 0:da].astype(F32)
        dz = (dya * gu).astype(BF16)
        mask = _tril(chunk)
        for hh in range(heads):
            wm = jnp.where(mask, ws_ref[hh], 0.0).astype(BF16)
            cols = slice(hh * hd, (hh + 1) * hd)
            dws = jnp.zeros((chunk, chunk), F32)
            for nn in range(tm // chunk):
                rows = slice(nn * chunk, (nn + 1) * chunk)
                z = _dot(wm, vn[rows, cols]) + b_ref[:, cols]
                dp_ref[rows, cols] = (dya[rows, cols] * z * gu_grad[rows, cols]).astype(BF16)
                dz_blk = dz[rows, cols]
                dws = dws + _dot_nt(dz_blk, vn[rows, cols])
                dzs_ref[:, cols] += dz_blk.astype(F32)
                dvn = _dot_tn(wm, dz_blk)
                dnv_ref[:, cols] += jnp.sum(dvn * xhat[rows, cols], axis=0, keepdims=True)
                dvn_sc[rows, cols] = dvn
            dws_ref[hh] += jnp.where(mask, dws, 0.0)
        dxhat = dvn_sc[...] * nv
        dgv = rstd * (dxhat - jnp.mean(dxhat, axis=-1, keepdims=True) - xhat * jnp.mean(dxhat * xhat, axis=-1, keepdims=True))
        dp_ref[:, da : 2 * da] = (dgv * gv_grad).astype(BF16)

        c_bg, c_cg, c_xb = 2 * da, 2 * da + db, 2 * da + 2 * db
        bg = p_ref[:, c_bg : c_bg + db]
        cg = p_ref[:, c_cg : c_cg + db]
        xb = p_ref[:, c_xb : c_xb + db]
        q = cg * xb
        q1 = _shift_down(q, 1, pa_ref, c_cg, c_xb, first)
        q2 = _shift_down(q, 2, pa_ref, c_cg, c_xb, first)
        dyb = dc_ref[:, da : da + db].astype(F32)
        conv = cw_ref[0:1, :] * q2 + cw_ref[1:2, :] * q1 + cw_ref[2:3, :] * q
        dp_ref[:, c_bg : c_bg + db] = (dyb * conv).astype(BF16)
        e = dyb * bg
        dcw_ref[0:1, :] += jnp.sum(e * q2, axis=0, keepdims=True)
        dcw_ref[1:2, :] += jnp.sum(e * q1, axis=0, keepdims=True)
        dcw_ref[2:3, :] += jnp.sum(e * q, axis=0, keepdims=True)
        rows = lax.broadcasted_iota(jnp.int32, e.shape, 0)
        dq = cw_ref[2:3, :] * e
        for kk in (1, 2):
            ek = pltpu.roll(e, tm - kk, 0)
            for r in range(kk):
                below = dcb_ref[r : r + 1, da : da + db].astype(F32) * pb_ref[r : r + 1, c_bg : c_bg + db]
                below = jnp.where(last, 0.0, below)
                ek = jnp.where(rows == tm - kk + r, below, ek)
            dq = dq + cw_ref[2 - kk : 3 - kk, :] * ek
        dp_ref[:, c_cg : c_cg + db] = (dq * xb).astype(BF16)
        dp_ref[:, c_xb : c_xb + db] = (dq * cg).astype(BF16)

    nh = tm // CONV_HALO
    nhb = tm // dhalo
    const2 = lambda i: (0, 0)
    return _call(
        body, "ab_mix_bwd", (nblk,),
        [
            pl.BlockSpec((tm, n), lambda i: (i, 0)),
            pl.BlockSpec((CONV_HALO, n), lambda i: (jnp.maximum(i * nh - 1, 0), 0)),
            pl.BlockSpec((CONV_HALO, n), lambda i: (jnp.minimum((i + 1) * nh, s // CONV_HALO - 1), 0)),
            pl.BlockSpec((tm, da + db), lambda i: (i, 0)),
            pl.BlockSpec((dhalo, da + db), lambda i: (jnp.minimum((i + 1) * nhb, s // dhalo - 1), 0)),
            pl.BlockSpec((1, da), const2),
            pl.BlockSpec((heads, chunk, chunk), lambda i: (0, 0, 0)),
            pl.BlockSpec((chunk, da), const2),
            pl.BlockSpec((3, db), const2),
        ],
        [
            pl.BlockSpec((tm, n), lambda i: (i, 0)),
            pl.BlockSpec((1, da), const2),
            pl.BlockSpec((heads, chunk, chunk), lambda i: (0, 0, 0)),
            pl.BlockSpec((chunk, da), const2),
            pl.BlockSpec((3, db), const2),
        ],
        [
            jax.ShapeDtypeStruct((s, n), BF16),
            jax.ShapeDtypeStruct((1, da), F32),
            jax.ShapeDtypeStruct((heads, chunk, chunk), F32),
            jax.ShapeDtypeStruct((chunk, da), F32),
            jax.ShapeDtypeStruct((3, db), F32),
        ],
        [proj, proj, proj, dcat, dcat, norm_v, w_s, b_rows, conv_w],
        scratch=[pltpu.VMEM((tm, da), F32)], phases=phases,
    )


def _pool_counts(tm, i, w):
    t = i * tm + lax.broadcasted_iota(jnp.int32, (tm, 1), 0)
    return jnp.minimum(t + 1, w).astype(F32)


def _pool_fwd(x, vec, w_grp, scale, phases=()):
    s, d = x.shape
    groups, gd, _ = w_grp.shape
    tm = _pick(s, (512, 256, 128))

    def body(x_ref, xa_ref, vec_ref, w_ref, sc_ref, xo_ref, p_ref, o_ref):
        i = pl.program_id(0)
        h = _modulate(x_ref[...], vec_ref)
        ha = jnp.where(i == 0, 0.0, _modulate(xa_ref[...], vec_ref))
        ext = jnp.concatenate([ha, h], axis=0)
        for gi, w in enumerate(POOL_WINDOWS):
            cols = slice(gi * gd, (gi + 1) * gd)
            acc = ext[:, cols]
            step = 1
            while step < w:
                acc = acc + pltpu.roll(acc, step, 0)
                step *= 2
            p = (acc[POOL_HALO:, :] / _pool_counts(tm, i, w) - h[:, cols]).astype(BF16)
            p_ref[:, cols] = p
            o_ref[:, cols] = _dot(p, w_ref[gi]).astype(BF16)
        xo_ref[...] = x_ref[...] + vec_ref[3:4, :] * (o_ref[...].astype(F32) * sc_ref[...])

    nh = tm // POOL_HALO
    row = pl.BlockSpec((tm, d), lambda i: (i, 0))
    return _call(
        body, "pool_fwd", (s // tm,),
        [
            row,
            pl.BlockSpec((POOL_HALO, d), lambda i: (jnp.maximum(i * nh - 1, 0), 0)),
            pl.BlockSpec((8, d), lambda i: (0, 0)),
            pl.BlockSpec((groups, gd, gd), lambda i: (0, 0, 0)),
            pl.BlockSpec((1, d), lambda i: (0, 0)),
        ],
        [row, row, row],
        [jax.ShapeDtypeStruct((s, d), F32), jax.ShapeDtypeStruct((s, d), BF16), jax.ShapeDtypeStruct((s, d), BF16)],
        [x, x, vec, w_grp, scale], phases=phases,
    )


def _pool_bwd(dxo, x, vec, p, o, w_grp, scale, phases=()):
    s, d = x.shape
    groups, gd, _ = w_grp.shape
    tm = _pick(s, (512, 256, 128))
    nblk = s // tm

    def body(dxo_ref, dxb_ref, x_ref, vec_ref, p_ref, o_ref, w_ref, sc_ref, dx_ref, dw_ref, dsc_ref, dvec_ref, dw_sc):
        i = pl.program_id(0)

        @pl.when(i == 0)
        def _():
            dw_sc[...] = jnp.zeros_like(dw_sc)
            dsc_ref[...] = jnp.zeros_like(dsc_ref)
            dvec_ref[...] = jnp.zeros_like(dvec_ref)

        gate, sc = vec_ref[3:4, :], sc_ref[...]
        dxo_v = dxo_ref[...]
        ov = o_ref[...].astype(F32)
        dvec_ref[3:4, :] += jnp.sum(dxo_v * (ov * sc), axis=0, keepdims=True)
        dy = gate * dxo_v
        dsc_ref[...] += jnp.sum(dy * ov, axis=0, keepdims=True)
        dout = (dy * sc).astype(BF16)
        dout_b = jnp.where(i == nblk - 1, 0.0, gate * dxb_ref[...] * sc).astype(BF16)
        for gi, w in enumerate(POOL_WINDOWS):
            cols = slice(gi * gd, (gi + 1) * gd)
            dw_sc[gi] += _dot_tn(p_ref[:, cols], dout[:, cols])
            wb = w_ref[gi]
            dp = _dot_nt(dout[:, cols], wb)
            dp_b = _dot_nt(dout_b[:, cols], wb)
            e = dp / _pool_counts(tm, i, w)
            t_below = (i + 1) * tm + lax.broadcasted_iota(jnp.int32, (POOL_HALO, 1), 0)
            e_b = dp_b / jnp.minimum(t_below + 1, w).astype(F32)
            acc = jnp.concatenate([e, e_b], axis=0)
            step = 1
            while step < w:
                acc = acc + pltpu.roll(acc, tm + POOL_HALO - step, 0)
                step *= 2
            dx_ref[:, cols] = acc[:tm, :] - dp
        dx, _ = _modulate_bwd(x_ref[...], dx_ref[...], vec_ref, dvec_ref)
        dx_ref[...] = dxo_v + dx

        @pl.when(i == nblk - 1)
        def _():
            dw_ref[...] = dw_sc[...].astype(BF16)

    nh = tm // POOL_HALO
    row = pl.BlockSpec((tm, d), lambda i: (i, 0))
    vecs = pl.BlockSpec((8, d), lambda i: (0, 0))
    wspec = pl.BlockSpec((groups, gd, gd), lambda i: (0, 0, 0))
    return _call(
        body, "pool_bwd", (nblk,),
        [
            row,
            pl.BlockSpec((POOL_HALO, d), lambda i: (jnp.minimum((i + 1) * nh, s // POOL_HALO - 1), 0)),
            row, vecs, row, row, wspec,
            pl.BlockSpec((1, d), lambda i: (0, 0)),
        ],
        [row, wspec, pl.BlockSpec((1, d), lambda i: (0, 0)), vecs],
        [
            jax.ShapeDtypeStruct((s, d), F32),
            jax.ShapeDtypeStruct((groups, gd, gd), BF16),
            jax.ShapeDtypeStruct((1, d), F32),
            jax.ShapeDtypeStruct((8, d), F32),
        ],
        [dxo, dxo, x, vec, p, o, w_grp, scale],
        scratch=[pltpu.VMEM((groups, gd, gd), F32)], phases=phases,
    )


def _loss_head(x, gain, target, phases=()):
    s, d = x.shape
    tm = _pick(s, (512, 256, 128))

    def body(x_ref, g_ref, t_ref, dx_ref, aux_ref):
        @pl.when(pl.program_id(0) == 0)
        def _():
            aux_ref[...] = jnp.zeros_like(aux_ref)

        xv = x_ref[...]
        rstd = _rstd(xv)
        r = xv * rstd
        gain_v = g_ref[...]
        err = r * gain_v - t_ref[...]
        aux_ref[1:2, :] += jnp.sum(err * err, axis=0, keepdims=True)
        dout = err * (1.0 / d)
        aux_ref[0:1, :] += jnp.sum(dout * r, axis=0, keepdims=True)
        dr = dout * gain_v
        dx_ref[...] = rstd * (dr - r * jnp.mean(dr * r, axis=-1, keepdims=True))

    row = pl.BlockSpec((tm, d), lambda i: (i, 0))
    return _call(
        body, "loss_head", (s // tm,),
        [row, pl.BlockSpec((1, d), lambda i: (0, 0)), row],
        [row, pl.BlockSpec((8, d), lambda i: (0, 0))],
        [jax.ShapeDtypeStruct((s, d), F32), jax.ShapeDtypeStruct((8, d), F32)], [x, gain, target], phases=phases,
    )


def _small_adam(gathered, gathered_ws, layout, smalls, chip):
    names = list(smalls)
    n = len(names)

    def body(*refs):
        chip_ref, g_ref, gws_ref = refs[0], refs[1], refs[2]
        wmv = refs[3 : 3 + 3 * n]
        outs = refs[3 + 3 * n : 3 + 7 * n]
        total = refs[-1]
        total[...] = g_ref[0]
        for kdev in range(1, N_DEV):
            total[...] += g_ref[kdev]
        total_ws = gws_ref[0]
        for kdev in range(1, N_DEV):
            total_ws = total_ws + gws_ref[kdev]
        my_chip = chip_ref[0]
        for a, name in enumerate(names):
            w_ref, m_ref, v_ref = wmv[3 * a : 3 * a + 3]
            if name == "ab_w_s":
                g = total_ws
            else:
                row0, rows, col0, cols = layout[name]
                if col0 is None:
                    g = jnp.zeros((rows, cols), F32)
                    for j in range(N_CHIPS):
                        g = g + jnp.where(my_chip == j, total[row0 : row0 + rows, j * cols : (j + 1) * cols], 0.0)
                else:
                    g = total[row0 : row0 + rows, col0 : col0 + cols]
            dl, mo, vo = _adam(w_ref[...], g, m_ref[...], v_ref[...])
            outs[4 * a][...] = g
            outs[4 * a + 1][...] = dl
            outs[4 * a + 2][...] = mo
            outs[4 * a + 3][...] = vo

    ins = [gathered, gathered_ws]
    out_shapes = []
    for name in names:
        ins.extend(smalls[name])
        out_shapes.extend([jax.ShapeDtypeStruct(smalls[name][0].shape, F32)] * 4)
    whole = lambda shape: pl.BlockSpec(shape, functools.partial(lambda nd, i, c: (0,) * nd, len(shape)))
    res = pl.pallas_call(
        body, name="small_adam",
        grid_spec=pltpu.PrefetchScalarGridSpec(
            num_scalar_prefetch=1, grid=(1,),
            in_specs=[whole(a.shape) for a in ins], out_specs=[whole(o.shape) for o in out_shapes],
            scratch_shapes=[pltpu.VMEM(gathered.shape[1:], F32)],
        ),
        out_shape=out_shapes,
        compiler_params=pltpu.CompilerParams(dimension_semantics=("arbitrary",), vmem_limit_bytes=VMEM_LIMIT_BYTES),
    )(chip.reshape(1).astype(jnp.int32), *ins)
    return {name: res[4 * a : 4 * a + 4] for a, name in enumerate(names)}


def _pad_rows(a, rows=8):
    extra = (-a.shape[0]) % rows
    return jnp.pad(a, ((0, extra), (0, 0))) if extra else a


def _pad_cols(a, cols):
    return jnp.pad(a, ((0, 0), (0, cols - a.shape[1]))) if a.shape[1] < cols else a


def _run(fn, *phases):
    outs, p_outs = fn(list(phases))
    for p, po in zip(phases, p_outs):
        p.then(po)
    return outs


def kernel(x, c, norm_g, w_mod, b_mod, w_ffn_in, w_ffn_out, ab_w_in, ab_norm_v, ab_w_s, ab_b_s, ab_conv_w, ab_w_out, pool_w_grp, pool_scale, final_g, loss_target, m_norm_g, m_w_mod, m_b_mod, m_w_ffn_in, m_w_ffn_out, m_ab_w_in, m_ab_norm_v, m_ab_w_s, m_ab_b_s, m_ab_conv_w, m_ab_w_out, m_pool_w_grp, m_pool_scale, m_final_g, v_norm_g, v_w_mod, v_b_mod, v_w_ffn_in, v_w_ffn_out, v_ab_w_in, v_ab_norm_v, v_ab_w_s, v_ab_b_s, v_ab_conv_w, v_ab_w_out, v_pool_w_grp, v_pool_scale, v_final_g):
    ix, iy, ic = _place()
    chip = 2 * ix + iy
    me = 4 * ix + 2 * iy + ic
    where = jnp.stack([chip, ic]).astype(jnp.int32)
    s, d = x.shape[1], x.shape[2]
    x0 = x.reshape(s, d)
    target = loss_target.reshape(s, d)
    n_layers = norm_g.shape[0]
    dq = d // N_CHIPS
    heads, chunk = ab_w_s.shape[1], ab_w_s.shape[2]
    da = ab_norm_v.shape[1]
    db = ab_conv_w.shape[2] * N_CHIPS
    f_hidden = w_ffn_out.shape[2] * N_CHIPS
    assert n_layers == 2 and da % heads == 0

    cw_pad = _pad_cols(ab_conv_w.reshape(3, db // N_CHIPS), dq)
    packed = jnp.concatenate(
        [_pad_rows(c.reshape(N_CHIPS, dq)), _pad_rows(norm_g.reshape(-1, dq)), _pad_rows(pool_scale.reshape(1, dq)), _pad_rows(cw_pad)],
        axis=0,
    )
    ncol = w_mod.shape[2]
    b_cols = lax.dynamic_slice(b_mod, (0, chip * ncol), (n_layers, ncol)).reshape(n_layers, 1, ncol)
    small = {}

    def small_gather(key, arrs):
        def then(outs):
            small[key] = outs

        return _phase_small_gather(arrs, then)

    stacks = {
        "w_ffn_in": tuple(a.reshape((-1,) + a.shape[2:]) for a in (w_ffn_in, m_w_ffn_in, v_w_ffn_in)),
        "w_ffn_out": tuple(a.reshape((-1,) + a.shape[2:]) for a in (w_ffn_out, m_w_ffn_out, v_w_ffn_out)),
        "ab_w_in": (ab_w_in, m_ab_w_in, v_ab_w_in),
        "ab_w_out": (ab_w_out, m_ab_w_out, v_ab_w_out),
        "pool_w_grp": (pool_w_grp[0], m_pool_w_grp[0], v_pool_w_grp[0]),
    }
    big_in = _Big((1, d, 2 * f_hidden), 2, 1)
    big_out = _Big((1, f_hidden, d), 1, 2)
    units = {}
    for l in range(n_layers):
        for k in range(2):
            units[f"in{l}{k}"] = (big_in, "w_ffn_in", 2 * l + k)
            units[f"out{l}{k}"] = (big_out, "w_ffn_out", 2 * l + k)
    units["abin"] = (_Big((1, d, ab_w_in.shape[2] * N_CHIPS), 2, 1), "ab_w_in", 0)
    units["about"] = (_Big((1, ab_w_out.shape[1] * N_CHIPS, d), 1, 2), "ab_w_out", 0)
    units["pool"] = (_Big((pool_w_grp.shape[1], pool_w_grp.shape[2] * N_CHIPS, pool_w_grp.shape[3]), 1, 0), "pool_w_grp", 0)
    big = {u: g for u, (g, _, _) in units.items()}

    weight = {}
    complete = set()

    def cast(u):
        g, st, b0 = units[u]

        def launch(phases):
            (weight[u],), p_outs = _cast_into_full(stacks[st][0], b0, g, where, "cast_" + u, phases)
            return None, p_outs

        return launch

    def gather_ici(*us):
        def then(outs):
            for u, o in zip(us, outs):
                weight[u] = o

        return _phase_gather_ici([weight[u] for u in us], [big[u] for u in us], then)

    def gather_sibling(*us):
        def then(outs):
            for u, o in zip(us, outs):
                weight[u] = o
                complete.add(u)

        return _phase_gather_sibling([weight[u] for u in us], [big[u] for u in us], then)

    def w_of(u):
        assert u in complete, u
        return weight[u]

    _run(cast("in00"), small_gather("inputs", [packed]))
    _run(cast("out00"))
    small_all = small["inputs"][0]
    by_chip = small_all[0::2]
    c_all = small_all[:, 0:N_CHIPS, :].reshape(N_DEV, d)
    norm_full = by_chip[:, 8 : 8 + 3 * n_layers, :].transpose(1, 0, 2).reshape(3 * n_layers, d)
    pool_scale_full = by_chip[:, 16:17, :].transpose(1, 0, 2).reshape(1, d)
    conv_full = by_chip[:, 24:27, : db // N_CHIPS].transpose(1, 0, 2).reshape(3, db)
    mod_cols = _run(lambda phases: _mod_fwd(c_all, w_mod, b_cols, phases), gather_ici("in00"))[0]
    _run(cast("abin"), gather_sibling("in00"), gather_ici("out00"), small_gather("mod", [mod_cols.reshape(n_layers * N_DEV, ncol)]))
    _run(cast("about"), gather_sibling("out00"), gather_ici("abin"))
    _run(cast("in01"), gather_sibling("abin"), gather_ici("about"))
    _run(cast("out01"), gather_sibling("about"))
    for u in ("in10", "out10", "pool", "in11", "out11"):
        _run(cast(u))
    mod_all = small["mod"][0]
    mod_mine = lax.dynamic_index_in_dim(mod_all[0::2].reshape(N_CHIPS, n_layers, N_DEV, ncol), me, axis=2, keepdims=False)
    mod = mod_mine.transpose(1, 0, 2).reshape(n_layers, 3, 3, d)
    vecs = {
        (l, sub): _pad_rows(jnp.concatenate([norm_full[3 * l + sub][None], mod[l, sub]], axis=0))
        for l in range(n_layers)
        for sub in range(3)
    }
    b_rows = jnp.broadcast_to(ab_b_s[0].T[:, :, None], (chunk, heads, da // heads)).reshape(chunk, da)

    saved = {}

    def ffn_forward(xs, l, sub, k, *phases):
        saved[l, sub, "x"] = xs
        xs, gg, uu, yb = _run(
            lambda ph: _ffn_fwd(xs, vecs[l, sub], w_of(f"in{l}{k}"), w_of(f"out{l}{k}"), f"ffn_fwd_{l}{k}", ph), *phases
        )
        saved[l, sub, "act"] = (gg, uu, yb)
        return xs

    xs = ffn_forward(x0, 0, 0, 0, gather_ici("in01"))
    saved[0, 1, "x"] = xs
    (proj,) = _run(lambda ph: _proj_mod_fwd(xs, vecs[0, 1], w_of("abin"), ph), gather_sibling("in01"), gather_ici("out01"))
    (cat,) = _run(lambda ph: _ab_mix_fwd(proj, ab_norm_v, ab_w_s[0], b_rows, conv_full, ph), gather_sibling("out01"), gather_ici("in10"))
    xs, yb = _run(lambda ph: _proj_res_fwd(cat, w_of("about"), xs, vecs[0, 1], ph), gather_sibling("in10"), gather_ici("out10", "pool"))
    saved[0, 1, "act"] = (proj, cat, yb)
    xs = ffn_forward(xs, 0, 2, 1, gather_sibling("out10", "pool"), gather_ici("in11"))
    xs = ffn_forward(xs, 1, 0, 0, gather_sibling("in11"), gather_ici("out11"))
    saved[1, 1, "x"] = xs
    xs, pp, oo = _run(lambda ph: _pool_fwd(xs, vecs[1, 1], w_of("pool"), pool_scale_full, ph), gather_sibling("out11"))
    saved[1, 1, "act"] = (pp, oo)
    xs = ffn_forward(xs, 1, 2, 1)
    dxs, aux = _run(lambda ph: _loss_head(xs, final_g.reshape(1, d), target, ph))
    loss = lax.psum(0.5 * jnp.sum(aux[1]) / d, ("x", "y", "c"))

    grad = {}
    recv = {}
    csum = {}
    parts = {}
    reduced = {}
    done = set()
    dvecs, small_g = {}, {}

    def pair_exchange(*us):
        def then(outs):
            for u, o in zip(us, outs):
                recv[u] = o

        return _phase_pair_exchange([grad[u] for u in us], [big[u] for u in us], then)

    def grad_half(u, a, b, mine, name, *phases, col0=0, prev=None):
        (res,) = _run(lambda ph: _grad_half(a, b, big[u], where, mine, col0, prev, recv[u] if mine else None, name, ph), *phases)
        return res

    def pair_sum(u, *phases):
        def launch(ph):
            (csum[u],), p_outs = _pair_sum(grad[u], recv[u], big[u], where, "pair_sum_" + u, ph)
            return None, p_outs

        _run(launch, *phases)

    def chip_exchange(*us):
        def then(outs):
            for u, o in zip(us, outs):
                parts[u] = o

        return _phase_chip_exchange([csum[u] for u in us], [big[u] for u in us], then)

    def chip_sum(*us, carried=()):
        for n_u, u in enumerate(us):
            g, st, b0 = units[u]

            def launch(ph):
                (reduced[st],), p_outs = _chip_sum(
                    csum[u], parts[u], g, where, reduced.get(st), stacks[st][0].shape, b0, "chip_sum_" + u, ph
                )
                return None, p_outs

            _run(launch, *(carried if n_u == 0 else ()))

    def pair_broadcast(*us):
        sts = [units[u][1] for u in us]
        assert len(set(sts)) == len(sts)

        def then(outs):
            for u, st, o in zip(us, sts, outs):
                reduced[st] = o
                done.add(u)

        return _phase_pair_broadcast([reduced[st] for st in sts], [big[u] for u in us], [units[u][2] for u in us], then)

    def ffn_backward(dxs, l, sub, k, carried_bwd, carried_send, carried_mine):
        gg, uu, yb = saved[l, sub, "act"]
        w_in, w_out = w_of(f"in{l}{k}"), w_of(f"out{l}{k}")
        uo, ui, tag = f"out{l}{k}", f"in{l}{k}", f"{l}{k}"
        dxs, dg, du, a, h, dy, dvecs[l, sub] = _run(
            lambda ph: _ffn_bwd(dxs, saved[l, sub, "x"], vecs[l, sub], gg, uu, yb, w_in, w_out, "ffn_bwd_" + tag, ph), *carried_bwd()
        )
        grad[uo] = grad_half(uo, a, dy, False, "dw_out_send_" + tag, *carried_send())
        part = grad_half(ui, h, du, False, "dw_in_u_send_" + tag, pair_exchange(uo), col0=f_hidden)
        grad[ui] = grad_half(ui, h, dg, False, "dw_in_g_send_" + tag, prev=part)
        csum[uo] = grad_half(uo, a, dy, True, "dw_out_" + tag, pair_exchange(ui))
        part = grad_half(ui, h, du, True, "dw_in_u_" + tag, *carried_mine(), col0=f_hidden)
        csum[ui] = grad_half(ui, h, dg, True, "dw_in_g_" + tag, prev=part)
        return dxs

    none = lambda: ()
    dxs = ffn_backward(dxs, 1, 2, 1, none, none, none)
    pp, oo = saved[1, 1, "act"]
    dxs, grad["pool"], small_g["pool_scale"], dvecs[1, 1] = _run(
        lambda ph: _pool_bwd(dxs, saved[1, 1, "x"], vecs[1, 1], pp, oo, w_of("pool"), pool_scale_full, ph)
    )

    def after_11():
        return (chip_exchange("in11", "out11"), pair_exchange("pool"))

    def bcast_11():
        chip_sum("in11", "out11")
        pair_sum("pool")
        return (pair_broadcast("in11", "out11"), chip_exchange("pool"))

    dxs = ffn_backward(dxs, 1, 0, 0, after_11, bcast_11, none)

    def after_10():
        return (chip_exchange("in10", "out10"),)

    def bcast_10():
        chip_sum("in10", "out10", "pool")
        return (pair_broadcast("in10", "out10", "pool"),)

    dxs = ffn_backward(dxs, 0, 2, 1, after_10, bcast_10, none)

    proj, cat, yb = saved[0, 1, "act"]
    dy, dcat, dgate = _run(lambda ph: _proj_res_bwd(dxs, yb, vecs[0, 1], w_of("about"), ph))
    grad["about"] = grad_half("about", cat, dy, False, "dw_ab_out_send")
    dproj, small_g["ab_norm_v"], small_g["ab_w_s"], dzs, small_g["ab_conv_w"] = _run(
        lambda ph: _ab_mix_bwd(proj, dcat, ab_norm_v, ab_w_s[0], b_rows, conv_full, ph), chip_exchange("out01"), pair_exchange("about")
    )
    small_g["ab_b_s"] = dzs.reshape(chunk, heads, da // heads).sum(axis=2).T
    dxs, h, dvecs[0, 1] = _run(
        lambda ph: _proj_mod_bwd(dproj[None], w_of("abin"), saved[0, 1, "x"], vecs[0, 1], dxs, dgate, "ab_in_bwd", ph)
    )
    grad["abin"] = grad_half("abin", h, dproj, False, "dw_ab_in_send")
    chip_sum("out01", carried=(pair_exchange("abin"),))
    csum["about"] = grad_half("about", cat, dy, True, "dw_ab_out", pair_broadcast("out01"))
    csum["abin"] = grad_half("abin", h, dproj, True, "dw_ab_in")

    def after_01():
        return (chip_exchange("in01", "abin", "about"),)

    def bcast_01():
        chip_sum("in01", "abin", "about")
        return (pair_broadcast("in01", "abin", "about"),)

    def reduce_out00():
        return (chip_exchange("out00"),)

    dxs = ffn_backward(dxs, 0, 0, 0, after_01, bcast_01, reduce_out00)
    grad_x = dxs.reshape(x.shape)

    dgain = jnp.stack([dvecs[l, sub][0] for l in range(n_layers) for sub in range(3)])
    dmod = jnp.concatenate([dvecs[l, sub][1:4] for l in range(n_layers) for sub in range(3)], axis=0)
    pieces = {
        "norm_g": (dgain, None, dq), "final_g": (aux[0:1], 0, d), "pool_scale": (small_g["pool_scale"], None, dq),
        "b_mod": (dmod, 0, d), "ab_norm_v": (small_g["ab_norm_v"], 0, da), "ab_conv_w": (small_g["ab_conv_w"], None, db // N_CHIPS),
        "ab_b_s": (small_g["ab_b_s"], 0, chunk),
    }
    layout, row0 = {}, 0
    for nm, (pc, col0, cols) in pieces.items():
        layout[nm] = (row0, pc.shape[0], col0, cols)
        row0 += pc.shape[0]
    packed_rows = -(-row0 // 8) * 8
    packed_g = sum(
        jnp.pad(pc, ((layout[nm][0], packed_rows - layout[nm][0] - pc.shape[0]), (0, d - pc.shape[1])))
        for nm, (pc, _, _) in pieces.items()
    )

    last_send, last_recv, last_sums, last_zones = _chip_exchange_start([csum["in00"]], [big["in00"]], "reduce_last_start")
    chip_sum("out00")
    _flush(
        "broadcast_out00", pair_broadcast("out00"),
        small_gather("grads", [packed_g, small_g["ab_w_s"].reshape(heads * chunk, chunk)]),
    )
    g_all, gws_all = small["grads"]

    out = {}

    def adam_stack(st):
        w3, m3, v3 = stacks[st]
        assert all(u in done for u, (_, ust, _) in units.items() if ust == st), st
        shape = {"w_ffn_in": w_ffn_in.shape, "w_ffn_out": w_ffn_out.shape, "pool_w_grp": pool_w_grp.shape}.get(st, w3.shape)
        out[st] = tuple(a.reshape(shape) for a in _adam_rows(w3, reduced[st], m3, v3, 0, w3.shape[0], None, "adam_" + st))

    for st in ("w_ffn_out", "ab_w_in", "ab_w_out", "pool_w_grp"):
        adam_stack(st)

    shapes2d = {
        "norm_g": (3 * n_layers, dq), "b_mod": (9 * n_layers, d), "final_g": (1, d), "ab_norm_v": (1, da),
        "pool_scale": (1, dq), "ab_conv_w": (3, db // N_CHIPS), "ab_b_s": (heads, chunk), "ab_w_s": (heads * chunk, chunk),
    }
    small_w = {"norm_g": (norm_g, m_norm_g, v_norm_g), "b_mod": (b_mod, m_b_mod, v_b_mod), "final_g": (final_g, m_final_g, v_final_g),
               "ab_norm_v": (ab_norm_v, m_ab_norm_v, v_ab_norm_v), "pool_scale": (pool_scale, m_pool_scale, v_pool_scale),
               "ab_conv_w": (ab_conv_w, m_ab_conv_w, v_ab_conv_w), "ab_b_s": (ab_b_s, m_ab_b_s, v_ab_b_s), "ab_w_s": (ab_w_s, m_ab_w_s, v_ab_w_s)}
    smalls = {nm: tuple(a.reshape(shapes2d[nm]) for a in wmv) for nm, wmv in small_w.items()}
    small_out = _small_adam(g_all, gws_all, layout, smalls, chip)
    for nm, res in small_out.items():
        out[nm] = tuple(a.reshape(small_w[nm][0].shape) for a in res)

    mod_row0 = layout["b_mod"][0]
    dmod_all = g_all[:, mod_row0 : mod_row0 + 9 * n_layers, :].reshape(N_DEV, n_layers, 9 * d)
    dmod_cols = lax.dynamic_slice(dmod_all, (0, 0, chip * ncol), (N_DEV, n_layers, ncol)).transpose(1, 0, 2)
    out["w_mod"] = tuple(_run(lambda ph: _mod_bwd_adam(c_all.T, dmod_cols, w_mod, m_w_mod, v_w_mod, ph)))

    (csum["in00"],), (parts["in00"],) = _chip_exchange_wait(
        last_send, last_recv, last_sums, last_zones, [big["in00"]], out["w_mod"][1], "reduce_last_wait"
    )
    chip_sum("in00")
    _flush("broadcast_last", pair_broadcast("in00"))
    adam_stack("w_ffn_in")

    order = ["norm_g", "w_mod", "b_mod", "w_ffn_in", "w_ffn_out", "ab_w_in", "ab_norm_v", "ab_w_s", "ab_b_s", "ab_conv_w", "ab_w_out", "pool_w_grp", "pool_scale", "final_g"]
    return (loss, grad_x, *[out[nm][0] for nm in order], *[out[nm][1] for nm in order], *[out[nm][2] for nm in order], *[out[nm][3] for nm in order])
```

```python
import functools
import math

import jax
import jax.numpy as jnp
from jax import lax
from jax.experimental import pallas as pl
from jax.experimental.pallas import tpu as pltpu

F32 = jnp.float32
BF16 = jnp.bfloat16
MESH = pl.DeviceIdType.MESH

EPS = 1e-6
ADAM_LR = 0.001
ADAM_B1 = 0.9
ADAM_B2 = 0.999
ADAM_EPS = 1e-08
ADAM_WD = 0.01
ADAM_STEP = 10
POOL_WINDOWS = (2, 4, 8, 16)
POOL_HALO = 16
CONV_HALO = 8
N_CHIPS = 4
N_DEV = 8
VMEM_LIMIT_BYTES = 48 * 1024 * 1024
EW_BLOCK_ELEMS = 256 * 1024


def _pick(n, prefs):
    for p in prefs:
        if p <= n and n % p == 0:
            return p
    return n


def _row_tile(rows, cols):
    best = None
    for d in range(16, rows + 1, 16):
        if rows % d == 0 and d * cols <= EW_BLOCK_ELEMS:
            best = d
    return best or rows


def _dot(a, b):
    return jnp.dot(a, b, preferred_element_type=F32)


def _dot_nt(a, b):
    return lax.dot_general(a, b, (((1,), (1,)), ((), ())), preferred_element_type=F32)


def _dot_tn(a, b):
    return lax.dot_general(a, b, (((0,), (0,)), ((), ())), preferred_element_type=F32)


def _sigmoid(x):
    return 0.5 * jnp.tanh(0.5 * x) + 0.5


_GELU_C = math.sqrt(2.0 / math.pi)


def _gelu(x):
    x2 = x * x
    t = jnp.tanh(_GELU_C * (x + 0.044715 * x2 * x))
    val = 0.5 * x * (1.0 + t)
    grad = 0.5 * (1.0 + t) + 0.5 * x * (1.0 - t * t) * (_GELU_C * (1.0 + 3.0 * 0.044715 * x2))
    return val, grad


def _rstd(x):
    return lax.rsqrt(jnp.mean(x * x, axis=-1, keepdims=True) + EPS)


def _modulate(x, vec_ref):
    return (x * _rstd(x)) * vec_ref[0:1, :] * (1.0 + vec_ref[2:3, :]) + vec_ref[1:2, :]


def _modulate_bwd(x, dh, vec_ref, dvec_ref):
    gn, sh, sc = vec_ref[0:1, :], vec_ref[1:2, :], vec_ref[2:3, :]
    rstd = _rstd(x)
    r = x * rstd
    dvec_ref[0:1, :] += jnp.sum(dh * r * (1.0 + sc), axis=0, keepdims=True)
    dvec_ref[1:2, :] += jnp.sum(dh, axis=0, keepdims=True)
    dvec_ref[2:3, :] += jnp.sum(dh * r * gn, axis=0, keepdims=True)
    gm = gn * (1.0 + sc)
    dr = dh * gm
    dx = rstd * (dr - r * jnp.mean(dr * r, axis=-1, keepdims=True))
    return dx, r * gm + sh


def _adam(w, g, m, v):
    m = ADAM_B1 * m + (1.0 - ADAM_B1) * g
    v = ADAM_B2 * v + (1.0 - ADAM_B2) * (g * g)
    m_hat = m / (1.0 - ADAM_B1**ADAM_STEP)
    v_hat = v / (1.0 - ADAM_B2**ADAM_STEP)
    delta = -ADAM_LR * (m_hat / (jnp.sqrt(v_hat) + ADAM_EPS) + ADAM_WD * w)
    return delta, m, v


_ANY = pl.BlockSpec(memory_space=pl.ANY)


class _Phase:
    def __init__(self, ins, out_shapes, aliases, n_sems, start, finish, then):
        self.ins, self.out_shapes, self.aliases, self.n_sems = list(ins), list(out_shapes), dict(aliases), n_sems
        self.start, self.finish, self.then = start, finish, then


def _call(body, name, grid, in_specs, out_specs, out_shape, ins, scratch=(), prefetch=(), phases=(), in_place=None):
    n_pre, n_in, n_out, n_sc = len(prefetch), len(in_specs), len(out_specs), len(scratch)
    ph_in = [len(p.ins) for p in phases]
    ph_out = [len(p.out_shapes) for p in phases]

    def kernel_body(*refs):
        pos = [0]

        def take(k):
            pos[0] += k
            return refs[pos[0] - k : pos[0]]

        pre, ins_ = take(n_pre), take(n_in)
        p_ins = [take(k) for k in ph_in]
        outs_ = take(n_out)
        p_outs = [take(k) for k in ph_out]
        sc = take(n_sc)
        sems = [take(2) for _ in phases]
        if phases:
            ids = [pl.program_id(a) for a in range(len(grid))]
            first = functools.reduce(jnp.logical_and, [i == 0 for i in ids])
            last = functools.reduce(jnp.logical_and, [i == g - 1 for i, g in zip(ids, grid)])

            @pl.when(first)
            def _():
                for p, pi, po, (send, recv) in zip(phases, p_ins, p_outs, sems):
                    p.start(pi, po, send, recv)

        if body is not None:
            body(*pre, *ins_, *outs_, *sc)
        if phases:

            @pl.when(last)
            def _():
                for p, pi, po, (send, recv) in zip(phases, p_ins, p_outs, sems):
                    p.finish(pi, po, send, recv)

    aliases = {n_pre + i: o for i, o in (in_place or {}).items()}
    i0, o0 = n_pre + n_in, n_out
    for p in phases:
        for i, o in p.aliases.items():
            aliases[i0 + i] = o0 + o
        i0 += len(p.ins)
        o0 += len(p.out_shapes)
    all_in = list(in_specs) + [_ANY] * sum(ph_in)
    all_out = list(out_specs) + [_ANY] * sum(ph_out)
    all_scratch = list(scratch)
    for p in phases:
        all_scratch += [pltpu.SemaphoreType.DMA((p.n_sems,)), pltpu.SemaphoreType.DMA((p.n_sems,))]
    shapes = list(out_shape) + [s for p in phases for s in p.out_shapes]
    operands = list(prefetch) + list(ins) + [a for p in phases for a in p.ins]
    sem = ("arbitrary",) * len(grid)
    params = pltpu.CompilerParams(dimension_semantics=sem, vmem_limit_bytes=VMEM_LIMIT_BYTES)
    if n_pre:
        res = pl.pallas_call(
            kernel_body, name=name, out_shape=shapes, input_output_aliases=aliases, compiler_params=params,
            grid_spec=pltpu.PrefetchScalarGridSpec(
                num_scalar_prefetch=n_pre, grid=grid, in_specs=all_in, out_specs=all_out, scratch_shapes=all_scratch
            ),
        )(*operands)
    else:
        res = pl.pallas_call(
            kernel_body, name=name, grid=grid, in_specs=all_in, out_specs=all_out, out_shape=shapes,
            scratch_shapes=all_scratch, input_output_aliases=aliases, compiler_params=params,
        )(*operands)
    res = list(res)
    outs, rest = res[:n_out], res[n_out:]
    p_res = []
    for k in ph_out:
        p_res.append(rest[:k])
        rest = rest[k:]
    return outs, p_res


def _place():
    return lax.axis_index("x"), lax.axis_index("y"), lax.axis_index("c")


def _other_chips():
    x, y, _ = _place()
    return [(1 - x, y), (x, 1 - y), (1 - x, 1 - y)]


def _flip(k):
    x, y, c = _place()
    return (1 - x if k & 4 else x, 1 - y if k & 2 else y, 1 - c if k & 1 else c)


def _remote(src, dst, send, recv, k, to):
    return pltpu.make_async_remote_copy(
        src_ref=src, dst_ref=dst, send_sem=send.at[k], recv_sem=recv.at[k], device_id=to, device_id_type=MESH
    )


def _phase_small_gather(arrs, then):
    n = len(arrs)

    def copies(ins, outs, send, recv):
        x, y, c = _place()
        me = 4 * x + 2 * y + c
        local = [pltpu.make_async_copy(ins[a], outs[a].at[me], send.at[a * N_DEV]) for a in range(n)]
        remote = [_remote(ins[a], outs[a].at[me], send, recv, a * N_DEV + k, _flip(k)) for a in range(n) for k in range(1, N_DEV)]
        return local, remote

    def start(ins, outs, send, recv):
        local, remote = copies(ins, outs, send, recv)
        for cp in local + remote:
            cp.start()

    def finish(ins, outs, send, recv):
        local, remote = copies(ins, outs, send, recv)
        for cp in remote + local:
            cp.wait()

    shapes = [jax.ShapeDtypeStruct((N_DEV,) + a.shape, a.dtype) for a in arrs]
    return _Phase(arrs, shapes, {}, n * N_DEV, start, finish, then)


def _flush(name, *phases):
    _, p_outs = _call(None, name, (1,), [], [], [], [], phases=list(phases))
    for p, po in zip(phases, p_outs):
        p.then(po)


class _Big:
    KINDS = {"full": (True, True), "half": (True, False), "shard": (False, True), "block": (False, False)}

    def __init__(self, f3, s3, h3):
        assert s3 != h3
        self.f3, self.s3, self.h3 = tuple(f3), s3, h3
        self.bd = tuple(f3[a] // (N_CHIPS if a == s3 else 1) // (2 if a == h3 else 1) for a in range(3))
        self.tile = (1, _row_tile(self.bd[1], self.bd[2]), self.bd[2])
        self.grid = tuple(self.bd[a] // self.tile[a] for a in range(3))

    def dims(self, kind):
        chips, halves = self.KINDS[kind]
        return tuple(
            self.bd[a] * (N_CHIPS if chips and a == self.s3 else 1) * (2 if halves and a == self.h3 else 1) for a in range(3)
        )

    def view(self, ref, chip=None, half=None, batch0=0, both_halves=True):
        start = [batch0, 0, 0]
        size = list(ref.shape)
        size[0] = self.bd[0] * (2 if self.h3 == 0 and both_halves else 1)
        if chip is not None:
            start[self.s3] += chip * self.bd[self.s3]
            size[self.s3] = self.bd[self.s3]
        if half is not None:
            start[self.h3] += half * self.bd[self.h3]
            size[self.h3] = self.bd[self.h3]
        return ref.at[tuple(pl.ds(st, sz) for st, sz in zip(start, size))]

    def spec(self, chip_from=None, half_from=None, lead=(), batch0=0):
        extra = "grid" in (chip_from, half_from)

        def index(*args):
            pref, idx = args[-1], list(args[int(extra) : -1])
            idx[0] += batch0
            if chip_from:
                idx[self.s3] += (pref[0] if chip_from == "pref" else args[0]) * self.grid[self.s3]
            if half_from:
                idx[self.h3] += (pref[1] if half_from == "pref" else args[0]) * self.grid[self.h3]
            return (0,) * len(lead) + tuple(idx)

        return pl.BlockSpec(tuple(lead) + self.tile, index)


def _same(arrs):
    return [jax.ShapeDtypeStruct(a.shape, a.dtype) for a in arrs]


def _phase_gather_ici(arrs, bigs, then):
    n = len(arrs)

    def copies(outs, send, recv, arriving):
        x, y, c = _place()
        return [
            _remote(blk, blk, send, recv, 3 * a + j, (*chip, c))
            for j, chip in enumerate(_other_chips())
            for a in range(n)
            for blk in [bigs[a].view(outs[a], 2 * chip[0] + chip[1] if arriving else 2 * x + y, c)]
        ]

    def start(ins, outs, send, recv):
        for cp in copies(outs, send, recv, False):
            cp.start()

    def finish(ins, outs, send, recv):
        for cp in copies(outs, send, recv, True):
            cp.wait_recv()
        for cp in copies(outs, send, recv, False):
            cp.wait_send()

    return _Phase(arrs, _same(arrs), {a: a for a in range(n)}, 3 * n, start, finish, then)


def _phase_gather_sibling(arrs, bigs, then):
    n = len(arrs)

    def copies(outs, send, recv, arriving):
        x, y, c = _place()
        return [
            _remote(blk, blk, send, recv, 3 * a + j, (x, y, 1 - c))
            for j, chip in enumerate(_other_chips())
            for a in range(n)
            for blk in [bigs[a].view(outs[a], 2 * chip[0] + chip[1], 1 - c if arriving else c)]
        ]

    def start(ins, outs, send, recv):
        for cp in copies(outs, send, recv, False):
            cp.start()

    def finish(ins, outs, send, recv):
        for cp in copies(outs, send, recv, True):
            cp.wait_recv()
        for cp in copies(outs, send, recv, False):
            cp.wait_send()

    return _Phase(arrs, _same(arrs), {a: a for a in range(n)}, 3 * n, start, finish, then)


def _phase_pair_exchange(grads, bigs, then):
    n = len(grads)

    def copies(ins, outs, send, recv):
        x, y, c = _place()
        srcs = [ins[a] if ins[a].shape == outs[a].shape else bigs[a].view(ins[a], None, 1 - c) for a in range(n)]
        return [_remote(srcs[a], outs[a], send, recv, a, (x, y, 1 - c)) for a in range(n)]

    def start(ins, outs, send, recv):
        for cp in copies(ins, outs, send, recv):
            cp.start()

    def finish(ins, outs, send, recv):
        for cp in copies(ins, outs, send, recv):
            cp.wait()

    shapes = [jax.ShapeDtypeStruct(b.dims("half"), BF16) for b in bigs]
    return _Phase(grads, shapes, {}, n, start, finish, then)


def _phase_chip_exchange(sums, bigs, then):
    n = len(sums)

    def copies(ins, outs, send, recv):
        _, _, c = _place()
        return [
            _remote(bigs[a].view(ins[a], 2 * chip[0] + chip[1], both_halves=False), outs[a].at[j], send, recv, 3 * a + j, (*chip, c))
            for j, chip in enumerate(_other_chips())
            for a in range(n)
        ]

    def start(ins, outs, send, recv):
        for cp in copies(ins, outs, send, recv):
            cp.start()

    def finish(ins, outs, send, recv):
        for cp in copies(ins, outs, send, recv):
            cp.wait()

    shapes = [jax.ShapeDtypeStruct((N_CHIPS - 1,) + b.dims("block"), BF16) for b in bigs]
    return _Phase(sums, shapes, {}, 3 * n, start, finish, then)


_HBM = pl.BlockSpec(memory_space=pltpu.HBM)
_SEM = pl.BlockSpec(memory_space=pltpu.SEMAPHORE)
_DATAFLOW = pltpu.SideEffectType.DATAFLOW_SIDE_EFFECTING


def _chip_exchange_copies(srcs, zones, bigs, send, recv):
    _, _, c = _place()
    return [
        _remote(bigs[a].view(srcs[a], 2 * chip[0] + chip[1], both_halves=False), zones[a].at[j], send, recv, 3 * a + j, (*chip, c))
        for j, chip in enumerate(_other_chips())
        for a in range(len(srcs))
    ]


def _chip_exchange_start(sums, bigs, name):
    n = len(sums)
    zones = [lax.empty((N_CHIPS - 1,) + b.dims("block"), BF16) for b in bigs]

    def body(*refs):
        send, recv = refs[2 * n], refs[2 * n + 1]
        for cp in _chip_exchange_copies(refs[:n], refs[n : 2 * n], bigs, send, recv):
            cp.start()
        refs[-1][...] = jnp.zeros_like(refs[-1])

    operands = [pltpu.with_memory_space_constraint(a, pltpu.HBM) for a in list(sums) + zones]
    res = pl.pallas_call(
        body, name=name,
        out_shape=[pltpu.SemaphoreType.DMA((3 * n,)), pltpu.SemaphoreType.DMA((3 * n,))]
        + [pltpu.HBM(a.shape, a.dtype) for a in operands] + [jax.ShapeDtypeStruct((8, 128), F32)],
        in_specs=[_HBM] * (2 * n), out_specs=[_SEM, _SEM] + [_HBM] * (2 * n) + [pl.BlockSpec(memory_space=pltpu.VMEM)],
        input_output_aliases={i: 2 + i for i in range(2 * n)},
        compiler_params=pltpu.CompilerParams(has_side_effects=_DATAFLOW),
    )(*operands)
    return res[0], res[1], list(res[2 : 2 + n]), list(res[2 + n : 2 + 2 * n]), res[-1]


def _chip_exchange_wait(send, recv, sums, zones, bigs, after, name):
    n = len(sums)

    def body(*refs):
        for cp in _chip_exchange_copies(refs[:n], refs[n : 2 * n], bigs, refs[2 * n], refs[2 * n + 1]):
            cp.wait_send()
            cp.wait_recv()

    res = pl.pallas_call(
        body, name=name, out_shape=[pltpu.HBM(a.shape, a.dtype) for a in list(sums) + list(zones)],
        in_specs=[_HBM] * (2 * n) + [_SEM, _SEM] + [_ANY] * len(after), out_specs=[_HBM] * (2 * n),
        input_output_aliases={i: i for i in range(2 * n)},
        compiler_params=pltpu.CompilerParams(has_side_effects=_DATAFLOW),
    )(*sums, *zones, send, recv, *after)
    return list(res[:n]), list(res[n:])


def _phase_pair_broadcast(stacks, bigs, batch0s, then):
    n = len(stacks)

    def start(ins, outs, send, recv):
        x, y, c = _place()
        for a in range(n):
            blk = bigs[a].view(outs[a], None, c, batch0s[a])
            _remote(blk, blk, send, recv, a, (x, y, 1 - c)).start()

    def finish(ins, outs, send, recv):
        x, y, c = _place()
        for a in range(n):
            mine = bigs[a].view(outs[a], None, c, batch0s[a])
            theirs = bigs[a].view(outs[a], None, 1 - c, batch0s[a])
            _remote(mine, mine, send, recv, a, (x, y, 1 - c)).wait_send()
            _remote(theirs, theirs, send, recv, a, (x, y, 1 - c)).wait_recv()

    return _Phase(stacks, _same(stacks), {a: a for a in range(n)}, n, start, finish, then)


def _tile_call(body, name, big, where, extra, ins, in_specs, out_specs, out_shape, phases=()):
    grid = ((extra,) if extra else ()) + big.grid
    return _call(body, name, grid, in_specs, out_specs, out_shape, ins, prefetch=(where,), phases=phases)


def _cast_into_full(w_stack, batch0, big, where, name, phases=()):
    def body(_, w_ref, o_ref):
        o_ref[...] = w_ref[...].astype(BF16)

    return _tile_call(
        body, name, big, where, 2, [w_stack], [big.spec(None, "grid", batch0=batch0)], [big.spec("pref", "grid")],
        [jax.ShapeDtypeStruct(big.dims("full"), BF16)], phases,
    )


def _pair_sum(g_full, recv_half, big, where, name, phases=()):
    def body(_, g_ref, r_ref, o_ref):
        o_ref[...] = (g_ref[...].astype(F32) + r_ref[...].astype(F32)).astype(BF16)

    half = big.spec("grid", None)
    return _tile_call(
        body, name, big, where, N_CHIPS, [g_full, recv_half], [big.spec("grid", "pref"), half], [half],
        [jax.ShapeDtypeStruct(big.dims("half"), BF16)], phases,
    )


def _chip_sum(chip_sum, parts, big, where, stack, stack_shape, batch0, name, phases=()):
    def body(_, own_ref, p_ref, *rest):
        acc = own_ref[...].astype(F32)
        for k in range(N_CHIPS - 1):
            acc = acc + p_ref[k].astype(F32)
        rest[-1][...] = acc

    ins = [chip_sum, parts] + ([stack] if stack is not None else [])
    in_specs = [big.spec("pref", None), big.spec(None, None, lead=(N_CHIPS - 1,))] + ([_ANY] if stack is not None else [])
    return _call(
        body, name, big.grid, in_specs, [big.spec(None, "pref", batch0=batch0)], [jax.ShapeDtypeStruct(stack_shape, F32)], ins,
        prefetch=(where,), phases=phases, in_place={2: 0} if stack is not None else None,
    )


def _adam_stack(w, g, m, v, name, after=()):
    b, r, c = w.shape
    tr = _row_tile(r, c)

    def body(w_ref, g_ref, m_ref, v_ref, *rest):
        go_ref, d_ref, mo_ref, vo_ref = rest[-4:]
        gv = g_ref[...]
        d, mo, vo = _adam(w_ref[...], gv, m_ref[...], v_ref[...])
        go_ref[...] = gv
        d_ref[...] = d
        mo_ref[...] = mo
        vo_ref[...] = vo

    spec = pl.BlockSpec((1, tr, c), lambda bb, i: (bb, i, 0))
    outs, _ = _call(
        body, name, (b, r // tr), [spec] * 4 + [_ANY] * len(after), [spec] * 4, [jax.ShapeDtypeStruct(w.shape, F32)] * 4,
        [w, g, m, v, *after],
    )
    return outs


def _mod_fwd(c_all, w_mod, b_cols, phases=()):
    n_layers, d, n = w_mod.shape
    tn = _pick(n, (768, 512, 384, 256, 128))

    def body(c_ref, w_ref, b_ref, o_ref):
        cv = c_ref[...]
        ca = (cv * _sigmoid(cv)).astype(BF16)
        o_ref[0] = _dot(ca, w_ref[0].astype(BF16)) + b_ref[0]

    return _call(
        body, "mod_fwd", (n_layers, n // tn),
        [
            pl.BlockSpec((N_DEV, d), lambda l, j: (0, 0)),
            pl.BlockSpec((1, d, tn), lambda l, j: (l, 0, j)),
            pl.BlockSpec((1, 1, tn), lambda l, j: (l, 0, j)),
        ],
        [pl.BlockSpec((1, N_DEV, tn), lambda l, j: (l, 0, j))],
        [jax.ShapeDtypeStruct((n_layers, N_DEV, n), F32)], [c_all, w_mod, b_cols], phases=phases,
    )


def _mod_bwd_adam(c_all_t, dmod_cols, w, m, v, after=()):
    n_layers, d, n = w.shape
    tn = _pick(n, (384, 256, 128))

    def body(c_ref, dm_ref, w_ref, m_ref, v_ref, *rest):
        g_ref, d_ref, mo_ref, vo_ref = rest[-4:]
        cv = c_ref[...]
        ca = (cv * _sigmoid(cv)).astype(BF16)
        g = _dot(ca, dm_ref[0].astype(BF16))
        g_ref[0] = g
        dl, mo, vo = _adam(w_ref[0], g, m_ref[0], v_ref[0])
        d_ref[0] = dl
        mo_ref[0] = mo
        vo_ref[0] = vo

    wspec = pl.BlockSpec((1, d, tn), lambda l, j: (l, 0, j))
    outs, _ = _call(
        body, "mod_bwd_adam", (n_layers, n // tn),
        [pl.BlockSpec((d, N_DEV), lambda l, j: (0, 0)), pl.BlockSpec((1, N_DEV, tn), lambda l, j: (l, 0, j)), wspec, wspec, wspec]
        + [_ANY] * len(after),
        [wspec] * 4, [jax.ShapeDtypeStruct(w.shape, F32)] * 4, [c_all_t, dmod_cols, w, m, v, *after],
    )
    return outs


def _ffn_fwd(x, vec, w_in, w_out, name, phases=()):
    s, d = x.shape
    f = w_out.shape[1]
    tm = _pick(s, (1024, 512, 256, 128))
    tf = _pick(f, (256, 128))
    nf = f // tf

    def body(x_ref, vec_ref, wg_ref, wu_ref, wo_ref, xo_ref, g_ref, u_ref, y_ref, h_sc, acc_sc):
        j = pl.program_id(1)

        @pl.when(j == 0)
        def _():
            h_sc[...] = _modulate(x_ref[...], vec_ref).astype(BF16)
            acc_sc[...] = jnp.zeros_like(acc_sc)

        h = h_sc[...]
        g = _dot(h, wg_ref[0])
        u = _dot(h, wu_ref[0])
        g_ref[...] = g.astype(BF16)
        u_ref[...] = u.astype(BF16)
        a = (g * _sigmoid(g) * u).astype(BF16)
        acc_sc[...] += _dot(a, wo_ref[0])

        @pl.when(j == nf - 1)
        def _():
            yv = acc_sc[...]
            xo_ref[...] = x_ref[...] + 0.5 * vec_ref[3:4, :] * yv
            y_ref[...] = yv.astype(BF16)

    row = pl.BlockSpec((tm, d), lambda i, j: (i, 0))
    hid = pl.BlockSpec((tm, tf), lambda i, j: (i, j))
    return _call(
        body, name, (s // tm, nf),
        [
            row,
            pl.BlockSpec((8, d), lambda i, j: (0, 0)),
            pl.BlockSpec((1, d, tf), lambda i, j: (0, 0, j)),
            pl.BlockSpec((1, d, tf), lambda i, j: (0, 0, nf + j)),
            pl.BlockSpec((1, tf, d), lambda i, j: (0, j, 0)),
        ],
        [row, hid, hid, row],
        [
            jax.ShapeDtypeStruct((s, d), F32),
            jax.ShapeDtypeStruct((s, f), BF16),
            jax.ShapeDtypeStruct((s, f), BF16),
            jax.ShapeDtypeStruct((s, d), BF16),
        ],
        [x, vec, w_in, w_in, w_out],
        scratch=[pltpu.VMEM((tm, d), BF16), pltpu.VMEM((tm, d), F32)], phases=phases,
    )


def _ffn_bwd(dxo, x, vec, gg, uu, y, w_in, w_out, name, phases=()):
    s, d = x.shape
    f = w_out.shape[1]
    tm = _pick(s, (512, 256, 128))
    tf = _pick(f, (256, 128))
    nf = f // tf

    def body(dxo_ref, x_ref, vec_ref, g_ref, u_ref, y_ref, wg_ref, wu_ref, wo_ref,
             dx_ref, dg_ref, du_ref, a_ref, h_ref, dy_ref, dvec_ref, acc_sc):
        i, j = pl.program_id(0), pl.program_id(1)

        @pl.when((i == 0) & (j == 0))
        def _():
            dvec_ref[...] = jnp.zeros_like(dvec_ref)

        @pl.when(j == 0)
        def _():
            dxo_v = dxo_ref[...]
            dy_ref[...] = (0.5 * vec_ref[3:4, :] * dxo_v).astype(BF16)
            dvec_ref[3:4, :] += 0.5 * jnp.sum(dxo_v * y_ref[...].astype(F32), axis=0, keepdims=True)
            acc_sc[...] = jnp.zeros_like(acc_sc)

        da = _dot_nt(dy_ref[...], wo_ref[0])
        g = g_ref[...].astype(F32)
        u = u_ref[...].astype(F32)
        sig = _sigmoid(g)
        sl = g * sig
        a_ref[...] = (sl * u).astype(BF16)
        dg = (da * u * (sig * (1.0 + g * (1.0 - sig)))).astype(BF16)
        du = (da * sl).astype(BF16)
        dg_ref[...] = dg
        du_ref[...] = du
        acc_sc[...] += _dot_nt(dg, wg_ref[0]) + _dot_nt(du, wu_ref[0])

        @pl.when(j == nf - 1)
        def _():
            dx, h = _modulate_bwd(x_ref[...], acc_sc[...], vec_ref, dvec_ref)
            dx_ref[...] = dxo_ref[...] + dx
            h_ref[...] = h.astype(BF16)

    row = pl.BlockSpec((tm, d), lambda i, j: (i, 0))
    hid = pl.BlockSpec((tm, tf), lambda i, j: (i, j))
    vecs = pl.BlockSpec((8, d), lambda i, j: (0, 0))
    return _call(
        body, name, (s // tm, nf),
        [
            row, row, vecs, hid, hid, row,
            pl.BlockSpec((1, d, tf), lambda i, j: (0, 0, j)),
            pl.BlockSpec((1, d, tf), lambda i, j: (0, 0, nf + j)),
            pl.BlockSpec((1, tf, d), lambda i, j: (0, j, 0)),
        ],
        [row, hid, hid, hid, row, row, vecs],
        [
            jax.ShapeDtypeStruct((s, d), F32),
            jax.ShapeDtypeStruct((s, f), BF16),
            jax.ShapeDtypeStruct((s, f), BF16),
            jax.ShapeDtypeStruct((s, f), BF16),
            jax.ShapeDtypeStruct((s, d), BF16),
            jax.ShapeDtypeStruct((s, d), BF16),
            jax.ShapeDtypeStruct((8, d), F32),
        ],
        [dxo, x, vec, gg, uu, y, w_in, w_in, w_out],
        scratch=[pltpu.VMEM((tm, d), F32)], phases=phases,
    )


def _grad_half(a, b, big, where, mine, col0, prev, recv, name, phases=()):
    s, k1 = a.shape
    b, b_part = b if isinstance(b, tuple) else (b[None], 0)
    n = b.shape[2]
    rows_halved = big.h3 == 1
    kk, nn = (k1 // 2, n) if rows_halved else (k1, n // 2)
    tk = _pick(kk, (1408, 1024, 512, 256, 128))
    tn = _pick(nn, (1408, 1024, 640, 512, 256, 128))
    nkb, nnb = kk // tk, nn // tn
    assert col0 % tn == 0 and (recv is None) == (not mine)

    def half(pref):
        return pref[1] if mine else 1 - pref[1]

    def body(_, a_ref, b_ref, *rest):
        acc = _dot_tn(a_ref[...], b_ref[0])
        if recv is not None:
            acc = acc + rest[0][0].astype(F32)
        rest[-1][0] = acc.astype(BF16)

    out_spec = pl.BlockSpec((1, tk, tn), lambda i, j, pref: (0, i, col0 // tn + j))
    in_specs = [
        pl.BlockSpec((s, tk), lambda i, j, pref: (0, i + (half(pref) * nkb if rows_halved else 0))),
        pl.BlockSpec((1, s, tn), lambda i, j, pref: (b_part, 0, j + (0 if rows_halved else half(pref) * nnb))),
    ]
    ins = [a, b]
    if recv is not None:
        in_specs.append(out_spec)
        ins.append(recv)
    in_place = None
    if prev is not None:
        in_place = {len(ins): 0}
        in_specs.append(_ANY)
        ins.append(prev)
    return _call(
        body, name, (nkb, nnb), in_specs, [out_spec], [jax.ShapeDtypeStruct(big.dims("half"), BF16)], ins,
        prefetch=(where,), phases=phases, in_place=in_place,
    )


def _proj_mod_fwd(x, vec, w, phases=()):
    s, d = x.shape
    n = w.shape[2]
    tm = _pick(s, (512, 256, 128))
    tn = _pick(n, (640, 512, 256, 128))

    def body(x_ref, vec_ref, w_ref, o_ref, h_sc):
        @pl.when(pl.program_id(1) == 0)
        def _():
            h_sc[...] = _modulate(x_ref[...], vec_ref).astype(BF16)

        o_ref[...] = _dot(h_sc[...], w_ref[0])

    return _call(
        body, "ab_in_fwd", (s // tm, n // tn),
        [
            pl.BlockSpec((tm, d), lambda i, j: (i, 0)),
            pl.BlockSpec((8, d), lambda i, j: (0, 0)),
            pl.BlockSpec((1, d, tn), lambda i, j: (0, 0, j)),
        ],
        [pl.BlockSpec((tm, tn), lambda i, j: (i, j))],
        [jax.ShapeDtypeStruct((s, n), F32)], [x, vec, w],
        scratch=[pltpu.VMEM((tm, d), BF16)], phases=phases,
    )


def _proj_res_fwd(a, w, x, vec, phases=()):
    s, kd = a.shape
    d = x.shape[1]
    tm = _pick(s, (512, 256, 128))

    def body(a_ref, w_ref, x_ref, vec_ref, xo_ref, y_ref):
        yv = _dot(a_ref[...], w_ref[0])
        xo_ref[...] = x_ref[...] + vec_ref[3:4, :] * yv
        y_ref[...] = yv.astype(BF16)

    row = pl.BlockSpec((tm, d), lambda i: (i, 0))
    return _call(
        body, "ab_out_fwd", (s // tm,),
        [pl.BlockSpec((tm, kd), lambda i: (i, 0)), pl.BlockSpec((1, kd, d), lambda i: (0, 0, 0)), row, pl.BlockSpec((8, d), lambda i: (0, 0))],
        [row, row],
        [jax.ShapeDtypeStruct((s, d), F32), jax.ShapeDtypeStruct((s, d), BF16)], [a, w, x, vec], phases=phases,
    )


def _proj_res_bwd(dxo, y, vec, w, phases=()):
    s, d = dxo.shape
    kd = w.shape[1]
    tm = _pick(s, (512, 256, 128))

    def body(dxo_ref, y_ref, vec_ref, w_ref, dy_ref, da_ref, dgate_ref):
        @pl.when(pl.program_id(0) == 0)
        def _():
            dgate_ref[...] = jnp.zeros_like(dgate_ref)

        dxo_v = dxo_ref[...]
        dy = (vec_ref[3:4, :] * dxo_v).astype(BF16)
        dy_ref[...] = dy
        dgate_ref[3:4, :] += jnp.sum(dxo_v * y_ref[...].astype(F32), axis=0, keepdims=True)
        da_ref[...] = _dot_nt(dy, w_ref[0]).astype(BF16)

    row = pl.BlockSpec((tm, d), lambda i: (i, 0))
    vecs = pl.BlockSpec((8, d), lambda i: (0, 0))
    return _call(
        body, "ab_out_bwd", (s // tm,),
        [row, row, vecs, pl.BlockSpec((1, kd, d), lambda i: (0, 0, 0))],
        [row, pl.BlockSpec((tm, kd), lambda i: (i, 0)), vecs],
        [jax.ShapeDtypeStruct((s, d), BF16), jax.ShapeDtypeStruct((s, kd), BF16), jax.ShapeDtypeStruct((8, d), F32)],
        [dxo, y, vec, w], phases=phases,
    )


def _proj_mod_bwd(dproj, w, x, vec, dxo, dvec_in, name, phases=()):
    parts, s, n_part = dproj.shape
    d = x.shape[1]
    tm = _pick(s, (512, 256, 128))
    tk = _pick(n_part, (1408, 1280, 1024, 512, 256, 128))
    per_part = n_part // tk
    nk = parts * per_part

    def body(dp_ref, w_ref, x_ref, vec_ref, dxo_ref, dvi_ref, dx_ref, h_ref, dvec_ref, acc_sc):
        i, k = pl.program_id(0), pl.program_id(1)

        @pl.when((i == 0) & (k == 0))
        def _():
            dvec_ref[...] = dvi_ref[...]

        @pl.when(k == 0)
        def _():
            acc_sc[...] = jnp.zeros_like(acc_sc)

        acc_sc[...] += _dot_nt(dp_ref[0], w_ref[0])

        @pl.when(k == nk - 1)
        def _():
            dx, h = _modulate_bwd(x_ref[...], acc_sc[...], vec_ref, dvec_ref)
            dx_ref[...] = dxo_ref[...] + dx
            h_ref[...] = h.astype(BF16)

    row = pl.BlockSpec((tm, d), lambda i, k: (i, 0))
    vecs = pl.BlockSpec((8, d), lambda i, k: (0, 0))
    return _call(
        body, name, (s // tm, nk),
        [
            pl.BlockSpec((1, tm, tk), lambda i, k: (k // per_part, i, k % per_part)),
            pl.BlockSpec((1, d, tk), lambda i, k: (0, 0, k)),
            row, vecs, row, vecs,
        ],
        [row, row, vecs],
        [jax.ShapeDtypeStruct((s, d), F32), jax.ShapeDtypeStruct((s, d), BF16), jax.ShapeDtypeStruct((8, d), F32)],
        [dproj, w, x, vec, dxo, dvec_in], scratch=[pltpu.VMEM((tm, d), F32)], phases=phases,
    )


def _tril(n):
    return lax.broadcasted_iota(jnp.int32, (n, n), 0) >= lax.broadcasted_iota(jnp.int32, (n, n), 1)


def _layernorm_stats(gv):
    mu = jnp.mean(gv, axis=-1, keepdims=True)
    cen = gv - mu
    rstd = lax.rsqrt(jnp.mean(cen * cen, axis=-1, keepdims=True) + EPS)
    return cen * rstd, rstd


def _shift_down(q, k, above_ref, c_cg, c_xb, first):
    width = q.shape[1]
    rows = lax.broadcasted_iota(jnp.int32, q.shape, 0)
    out = pltpu.roll(q, k, 0)
    for r in range(k):
        src = CONV_HALO - k + r
        above = above_ref[src : src + 1, c_cg : c_cg + width] * above_ref[src : src + 1, c_xb : c_xb + width]
        above = jnp.where(first, 0.0, above)
        out = jnp.where(rows == r, above, out)
    return out


def _ab_mix_fwd(proj, norm_v, w_s, b_rows, conv_w, phases=()):
    s, n = proj.shape
    heads, chunk, _ = w_s.shape
    da = norm_v.shape[1]
    hd = da // heads
    db = conv_w.shape[1]
    tm = _pick(s, (512, 256, 128))

    def body(p_ref, ph_ref, nv_ref, ws_ref, b_ref, cw_ref, o_ref):
        first = pl.program_id(0) == 0
        gu, _ = _gelu(p_ref[:, 0:da])
        gv, _ = _gelu(p_ref[:, da : 2 * da])
        xhat, _ = _layernorm_stats(gv)
        vn = (xhat * nv_ref[...]).astype(BF16)
        mask = _tril(chunk)
        for hh in range(heads):
            wm = jnp.where(mask, ws_ref[hh], 0.0).astype(BF16)
            cols = slice(hh * hd, (hh + 1) * hd)
            for nn in range(tm // chunk):
                rows = slice(nn * chunk, (nn + 1) * chunk)
                z = _dot(wm, vn[rows, cols]) + b_ref[:, cols]
                o_ref[rows, cols] = (gu[rows, cols] * z).astype(BF16)
        c_cg, c_xb = 2 * da + db, 2 * da + 2 * db
        bg = p_ref[:, 2 * da : 2 * da + db]
        q = p_ref[:, c_cg : c_cg + db] * p_ref[:, c_xb : c_xb + db]
        q1 = _shift_down(q, 1, ph_ref, c_cg, c_xb, first)
        q2 = _shift_down(q, 2, ph_ref, c_cg, c_xb, first)
        conv = cw_ref[0:1, :] * q2 + cw_ref[1:2, :] * q1 + cw_ref[2:3, :] * q
        o_ref[:, da : da + db] = (bg * conv).astype(BF16)

    nh = tm // CONV_HALO
    return _call(
        body, "ab_mix_fwd", (s // tm,),
        [
            pl.BlockSpec((tm, n), lambda i: (i, 0)),
            pl.BlockSpec((CONV_HALO, n), lambda i: (jnp.maximum(i * nh - 1, 0), 0)),
            pl.BlockSpec((1, da), lambda i: (0, 0)),
            pl.BlockSpec((heads, chunk, chunk), lambda i: (0, 0, 0)),
            pl.BlockSpec((chunk, da), lambda i: (0, 0)),
            pl.BlockSpec((3, db), lambda i: (0, 0)),
        ],
        [pl.BlockSpec((tm, da + db), lambda i: (i, 0))],
        [jax.ShapeDtypeStruct((s, da + db), BF16)], [proj, proj, norm_v, w_s, b_rows, conv_w], phases=phases,
    )


def _ab_mix_bwd(proj, dcat, norm_v, w_s, b_rows, conv_w, phases=()):
    s, n = proj.shape
    heads, chunk, _ = w_s.shape
    da = norm_v.shape[1]
    hd = da // heads
    db = conv_w.shape[1]
    tm = _pick(s, (512, 256, 128))
    nblk = s // tm
    dhalo = 2 * CONV_HALO

    def body(p_ref, pa_ref, pb_ref, dc_ref, dcb_ref, nv_ref, ws_ref, b_ref, cw_ref,
             dp_ref, dnv_ref, dws_ref, dzs_ref, dcw_ref, dvn_sc):
        i = pl.program_id(0)
        first, last = i == 0, i == nblk - 1

        @pl.when(first)
        def _():
            dnv_ref[...] = jnp.zeros_like(dnv_ref)
            dws_ref[...] = jnp.zeros_like(dws_ref)
            dzs_ref[...] = jnp.zeros_like(dzs_ref)
            dcw_ref[...] = jnp.zeros_like(dcw_ref)

        uu = p_ref[:, 0:da]
        gu, gu_grad = _gelu(uu)
        gv, gv_grad = _gelu(p_ref[:, da : 2 * da])
        xhat, rstd = _layernorm_stats(gv)
        nv = nv_ref[...]
        vn = (xhat * nv).astype(BF16)
        dya = dc_ref[:, 0:da].astype(F32)
        dz = (dya * gu).astype(BF16)
        mask = _tril(chunk)
        for hh in range(heads):
            wm = jnp.where(mask, ws_ref[hh], 0.0).astype(BF16)
            cols = slice(hh * hd, (hh + 1) * hd)
            dws = jnp.zeros((chunk, chunk), F32)
            for nn in range(tm // chunk):
                rows = slice(nn * chunk, (nn + 1) * chunk)
                z = _dot(wm, vn[rows, cols]) + b_ref[:, cols]
                dp_ref[rows, cols] = (dya[rows, cols] * z * gu_grad[rows, cols]).astype(BF16)
                dz_blk = dz[rows, cols]
                dws = dws + _dot_nt(dz_blk, vn[rows, cols])
                dzs_ref[:, cols] += dz_blk.astype(F32)
                dvn = _dot_tn(wm, dz_blk)
                dnv_ref[:, cols] += jnp.sum(dvn * xhat[rows, cols], axis=0, keepdims=True)
                dvn_sc[rows, cols] = dvn
            dws_ref[hh] += jnp.where(mask, dws, 0.0)
        dxhat = dvn_sc[...] * nv
        dgv = rstd * (dxhat - jnp.mean(dxhat, axis=-1, keepdims=True) - xhat * jnp.mean(dxhat * xhat, axis=-1, keepdims=True))
        dp_ref[:, da : 2 * da] = (dgv * gv_grad).astype(BF16)

        c_bg, c_cg, c_xb = 2 * da, 2 * da + db, 2 * da + 2 * db
        bg = p_ref[:, c_bg : c_bg + db]
        cg = p_ref[:, c_cg : c_cg + db]
        xb = p_ref[:, c_xb : c_xb + db]
        q = cg * xb
        q1 = _shift_down(q, 1, pa_ref, c_cg, c_xb, first)
        q2 = _shift_down(q, 2, pa_ref, c_cg, c_xb, first)
        dyb = dc_ref[:, da : da + db].astype(F32)
        conv = cw_ref[0:1, :] * q2 + cw_ref[1:2, :] * q1 + cw_ref[2:3, :] * q
        dp_ref[:, c_bg : c_bg + db] = (dyb * conv).astype(BF16)
        e = dyb * bg
        dcw_ref[0:1, :] += jnp.sum(e * q2, axis=0, keepdims=True)
        dcw_ref[1:2, :] += jnp.sum(e * q1, axis=0, keepdims=True)
        dcw_ref[2:3, :] += jnp.sum(e * q, axis=0, keepdims=True)
        rows = lax.broadcasted_iota(jnp.int32, e.shape, 0)
        dq = cw_ref[2:3, :] * e
        for kk in (1, 2):
            ek = pltpu.roll(e, tm - kk, 0)
            for r in range(kk):
                below = dcb_ref[r : r + 1, da : da + db].astype(F32) * pb_ref[r : r + 1, c_bg : c_bg + db]
                below = jnp.where(last, 0.0, below)
                ek = jnp.where(rows == tm - kk + r, below, ek)
            dq = dq + cw_ref[2 - kk : 3 - kk, :] * ek
        dp_ref[:, c_cg : c_cg + db] = (dq * xb).astype(BF16)
        dp_ref[:, c_xb : c_xb + db] = (dq * cg).astype(BF16)

    nh = tm // CONV_HALO
    nhb = tm // dhalo
    const2 = lambda i: (0, 0)
    return _call(
        body, "ab_mix_bwd", (nblk,),
        [
            pl.BlockSpec((tm, n), lambda i: (i, 0)),
            pl.BlockSpec((CONV_HALO, n), lambda i: (jnp.maximum(i * nh - 1, 0), 0)),
            pl.BlockSpec((CONV_HALO, n), lambda i: (jnp.minimum((i + 1) * nh, s // CONV_HALO - 1), 0)),
            pl.BlockSpec((tm, da + db), lambda i: (i, 0)),
            pl.BlockSpec((dhalo, da + db), lambda i: (jnp.minimum((i + 1) * nhb, s // dhalo - 1), 0)),
            pl.BlockSpec((1, da), const2),
            pl.BlockSpec((heads, chunk, chunk), lambda i: (0, 0, 0)),
            pl.BlockSpec((chunk, da), const2),
            pl.BlockSpec((3, db), const2),
        ],
        [
            pl.BlockSpec((tm, n), lambda i: (i, 0)),
            pl.BlockSpec((1, da), const2),
            pl.BlockSpec((heads, chunk, chunk), lambda i: (0, 0, 0)),
            pl.BlockSpec((chunk, da), const2),
            pl.BlockSpec((3, db), const2),
        ],
        [
            jax.ShapeDtypeStruct((s, n), BF16),
            jax.ShapeDtypeStruct((1, da), F32),
            jax.ShapeDtypeStruct((heads, chunk, chunk), F32),
            jax.ShapeDtypeStruct((chunk, da), F32),
            jax.ShapeDtypeStruct((3, db), F32),
        ],
        [proj, proj, proj, dcat, dcat, norm_v, w_s, b_rows, conv_w],
        scratch=[pltpu.VMEM((tm, da), F32)], phases=phases,
    )


def _pool_counts(tm, i, w):
    t = i * tm + lax.broadcasted_iota(jnp.int32, (tm, 1), 0)
    return jnp.minimum(t + 1, w).astype(F32)


def _pool_fwd(x, vec, w_grp, scale, phases=()):
    s, d = x.shape
    groups, gd, _ = w_grp.shape
    tm = _pick(s, (512, 256, 128))

    def body(x_ref, xa_ref, vec_ref, w_ref, sc_ref, xo_ref, p_ref, o_ref):
        i = pl.program_id(0)
        h = _modulate(x_ref[...], vec_ref)
        ha = jnp.where(i == 0, 0.0, _modulate(xa_ref[...], vec_ref))
        ext = jnp.concatenate([ha, h], axis=0)
        for gi, w in enumerate(POOL_WINDOWS):
            cols = slice(gi * gd, (gi + 1) * gd)
            acc = ext[:, cols]
            step = 1
            while step < w:
                acc = acc + pltpu.roll(acc, step, 0)
                step *= 2
            p = (acc[POOL_HALO:, :] / _pool_counts(tm, i, w) - h[:, cols]).astype(BF16)
            p_ref[:, cols] = p
            o_ref[:, cols] = _dot(p, w_ref[gi]).astype(BF16)
        xo_ref[...] = x_ref[...] + vec_ref[3:4, :] * (o_ref[...].astype(F32) * sc_ref[...])

    nh = tm // POOL_HALO
    row = pl.BlockSpec((tm, d), lambda i: (i, 0))
    return _call(
        body, "pool_fwd", (s // tm,),
        [
            row,
            pl.BlockSpec((POOL_HALO, d), lambda i: (jnp.maximum(i * nh - 1, 0), 0)),
            pl.BlockSpec((8, d), lambda i: (0, 0)),
            pl.BlockSpec((groups, gd, gd), lambda i: (0, 0, 0)),
            pl.BlockSpec((1, d), lambda i: (0, 0)),
        ],
        [row, row, row],
        [jax.ShapeDtypeStruct((s, d), F32), jax.ShapeDtypeStruct((s, d), BF16), jax.ShapeDtypeStruct((s, d), BF16)],
        [x, x, vec, w_grp, scale], phases=phases,
    )


def _pool_bwd(dxo, x, vec, p, o, w_grp, scale, phases=()):
    s, d = x.shape
    groups, gd, _ = w_grp.shape
    tm = _pick(s, (512, 256, 128))
    nblk = s // tm

    def body(dxo_ref, dxb_ref, x_ref, vec_ref, p_ref, o_ref, w_ref, sc_ref, dx_ref, dw_ref, dsc_ref, dvec_ref, dw_sc):
        i = pl.program_id(0)

        @pl.when(i == 0)
        def _():
            dw_sc[...] = jnp.zeros_like(dw_sc)
            dsc_ref[...] = jnp.zeros_like(dsc_ref)
            dvec_ref[...] = jnp.zeros_like(dvec_ref)

        gate, sc = vec_ref[3:4, :], sc_ref[...]
        dxo_v = dxo_ref[...]
        ov = o_ref[...].astype(F32)
        dvec_ref[3:4, :] += jnp.sum(dxo_v * (ov * sc), axis=0, keepdims=True)
        dy = gate * dxo_v
        dsc_ref[...] += jnp.sum(dy * ov, axis=0, keepdims=True)
        dout = (dy * sc).astype(BF16)
        dout_b = jnp.where(i == nblk - 1, 0.0, gate * dxb_ref[...] * sc).astype(BF16)
        for gi, w in enumerate(POOL_WINDOWS):
            cols = slice(gi * gd, (gi + 1) * gd)
            dw_sc[gi] += _dot_tn(p_ref[:, cols], dout[:, cols])
            wb = w_ref[gi]
            dp = _dot_nt(dout[:, cols], wb)
            dp_b = _dot_nt(dout_b[:, cols], wb)
            e = dp / _pool_counts(tm, i, w)
            t_below = (i + 1) * tm + lax.broadcasted_iota(jnp.int32, (POOL_HALO, 1), 0)
            e_b = dp_b / jnp.minimum(t_below + 1, w).astype(F32)
            acc = jnp.concatenate([e, e_b], axis=0)
            step = 1
            while step < w:
                acc = acc + pltpu.roll(acc, tm + POOL_HALO - step, 0)
                step *= 2
            dx_ref[:, cols] = acc[:tm, :] - dp
        dx, _ = _modulate_bwd(x_ref[...], dx_ref[...], vec_ref, dvec_ref)
        dx_ref[...] = dxo_v + dx

        @pl.when(i == nblk - 1)
        def _():
            dw_ref[...] = dw_sc[...].astype(BF16)

    nh = tm // POOL_HALO
    row = pl.BlockSpec((tm, d), lambda i: (i, 0))
    vecs = pl.BlockSpec((8, d), lambda i: (0, 0))
    wspec = pl.BlockSpec((groups, gd, gd), lambda i: (0, 0, 0))
    return _call(
        body, "pool_bwd", (nblk,),
        [
            row,
            pl.BlockSpec((POOL_HALO, d), lambda i: (jnp.minimum((i + 1) * nh, s // POOL_HALO - 1), 0)),
            row, vecs, row, row, wspec,
            pl.BlockSpec((1, d), lambda i: (0, 0)),
        ],
        [row, wspec, pl.BlockSpec((1, d), lambda i: (0, 0)), vecs],
        [
            jax.ShapeDtypeStruct((s, d), F32),
            jax.ShapeDtypeStruct((groups, gd, gd), BF16),
            jax.ShapeDtypeStruct((1, d), F32),
            jax.ShapeDtypeStruct((8, d), F32),
        ],
        [dxo, dxo, x, vec, p, o, w_grp, scale],
        scratch=[pltpu.VMEM((groups, gd, gd), F32)], phases=phases,
    )


def _loss_head(x, gain, target, phases=()):
    s, d = x.shape
    tm = _pick(s, (512, 256, 128))

    def body(x_ref, g_ref, t_ref, dx_ref, aux_ref):
        @pl.when(pl.program_id(0) == 0)
        def _():
            aux_ref[...] = jnp.zeros_like(aux_ref)

        xv = x_ref[...]
        rstd = _rstd(xv)
        r = xv * rstd
        gain_v = g_ref[...]
        err = r * gain_v - t_ref[...]
        aux_ref[1:2, :] += jnp.sum(err * err, axis=0, keepdims=True)
        dout = err * (1.0 / d)
        aux_ref[0:1, :] += jnp.sum(dout * r, axis=0, keepdims=True)
        dr = dout * gain_v
        dx_ref[...] = rstd * (dr - r * jnp.mean(dr * r, axis=-1, keepdims=True))

    row = pl.BlockSpec((tm, d), lambda i: (i, 0))
    return _call(
        body, "loss_head", (s // tm,),
        [row, pl.BlockSpec((1, d), lambda i: (0, 0)), row],
        [row, pl.BlockSpec((8, d), lambda i: (0, 0))],
        [jax.ShapeDtypeStruct((s, d), F32), jax.ShapeDtypeStruct((8, d), F32)], [x, gain, target], phases=phases,
    )


def _small_adam(gathered, gathered_ws, layout, smalls, chip):
    names = list(smalls)
    n = len(names)

    def body(*refs):
        chip_ref, g_ref, gws_ref = refs[0], refs[1], refs[2]
        wmv = refs[3 : 3 + 3 * n]
        outs = refs[3 + 3 * n : 3 + 7 * n]
        total = refs[-1]
        total[...] = g_ref[0]
        for kdev in range(1, N_DEV):
            total[...] += g_ref[kdev]
        total_ws = gws_ref[0]
        for kdev in range(1, N_DEV):
            total_ws = total_ws + gws_ref[kdev]
        my_chip = chip_ref[0]
        for a, name in enumerate(names):
            w_ref, m_ref, v_ref = wmv[3 * a : 3 * a + 3]
            if name == "ab_w_s":
                g = total_ws
            else:
                row0, rows, col0, cols = layout[name]
                if col0 is None:
                    g = jnp.zeros((rows, cols), F32)
                    for j in range(N_CHIPS):
                        g = g + jnp.where(my_chip == j, total[row0 : row0 + rows, j * cols : (j + 1) * cols], 0.0)
                else:
                    g = total[row0 : row0 + rows, col0 : col0 + cols]
            dl, mo, vo = _adam(w_ref[...], g, m_ref[...], v_ref[...])
            outs[4 * a][...] = g
            outs[4 * a + 1][...] = dl
            outs[4 * a + 2][...] = mo
            outs[4 * a + 3][...] = vo

    ins = [gathered, gathered_ws]
    out_shapes = []
    for name in names:
        ins.extend(smalls[name])
        out_shapes.extend([jax.ShapeDtypeStruct(smalls[name][0].shape, F32)] * 4)
    whole = lambda shape: pl.BlockSpec(shape, functools.partial(lambda nd, i, c: (0,) * nd, len(shape)))
    res = pl.pallas_call(
        body, name="small_adam",
        grid_spec=pltpu.PrefetchScalarGridSpec(
            num_scalar_prefetch=1, grid=(1,),
            in_specs=[whole(a.shape) for a in ins], out_specs=[whole(o.shape) for o in out_shapes],
            scratch_shapes=[pltpu.VMEM(gathered.shape[1:], F32)],
        ),
        out_shape=out_shapes,
        compiler_params=pltpu.CompilerParams(dimension_semantics=("arbitrary",), vmem_limit_bytes=VMEM_LIMIT_BYTES),
    )(chip.reshape(1).astype(jnp.int32), *ins)
    return {name: res[4 * a : 4 * a + 4] for a, name in enumerate(names)}


def _pad_rows(a, rows=8):
    extra = (-a.shape[0]) % rows
    return jnp.pad(a, ((0, extra), (0, 0))) if extra else a


def _pad_cols(a, cols):
    return jnp.pad(a, ((0, 0), (0, cols - a.shape[1]))) if a.shape[1] < cols else a


def _run(fn, *phases):
    outs, p_outs = fn(list(phases))
    for p, po in zip(phases, p_outs):
        p.then(po)
    return outs


def kernel(x, c, norm_g, w_mod, b_mod, w_ffn_in, w_ffn_out, ab_w_in, ab_norm_v, ab_w_s, ab_b_s, ab_conv_w, ab_w_out, pool_w_grp, pool_scale, final_g, loss_target, m_norm_g, m_w_mod, m_b_mod, m_w_ffn_in, m_w_ffn_out, m_ab_w_in, m_ab_norm_v, m_ab_w_s, m_ab_b_s, m_ab_conv_w, m_ab_w_out, m_pool_w_grp, m_pool_scale, m_final_g, v_norm_g, v_w_mod, v_b_mod, v_w_ffn_in, v_w_ffn_out, v_ab_w_in, v_ab_norm_v, v_ab_w_s, v_ab_b_s, v_ab_conv_w, v_ab_w_out, v_pool_w_grp, v_pool_scale, v_final_g):
    ix, iy, ic = _place()
    chip = 2 * ix + iy
    me = 4 * ix + 2 * iy + ic
    where = jnp.stack([chip, ic]).astype(jnp.int32)
    s, d = x.shape[1], x.shape[2]
    x0 = x.reshape(s, d)
    target = loss_target.reshape(s, d)
    n_layers = norm_g.shape[0]
    dq = d // N_CHIPS
    heads, chunk = ab_w_s.shape[1], ab_w_s.shape[2]
    da = ab_norm_v.shape[1]
    db = ab_conv_w.shape[2] * N_CHIPS
    f_hidden = w_ffn_out.shape[2] * N_CHIPS
    assert n_layers == 2 and da % heads == 0

    cw_pad = _pad_cols(ab_conv_w.reshape(3, db // N_CHIPS), dq)
    packed = jnp.concatenate(
        [_pad_rows(c.reshape(N_CHIPS, dq)), _pad_rows(norm_g.reshape(-1, dq)), _pad_rows(pool_scale.reshape(1, dq)), _pad_rows(cw_pad)],
        axis=0,
    )
    ncol = w_mod.shape[2]
    b_cols = lax.dynamic_slice(b_mod, (0, chip * ncol), (n_layers, ncol)).reshape(n_layers, 1, ncol)
    small = {}

    def small_gather(key, arrs):
        def then(outs):
            small[key] = outs

        return _phase_small_gather(arrs, then)

    stacks = {
        "w_ffn_in": tuple(a.reshape((-1,) + a.shape[2:]) for a in (w_ffn_in, m_w_ffn_in, v_w_ffn_in)),
        "w_ffn_out": tuple(a.reshape((-1,) + a.shape[2:]) for a in (w_ffn_out, m_w_ffn_out, v_w_ffn_out)),
        "ab_w_in": (ab_w_in, m_ab_w_in, v_ab_w_in),
        "ab_w_out": (ab_w_out, m_ab_w_out, v_ab_w_out),
        "pool_w_grp": (pool_w_grp[0], m_pool_w_grp[0], v_pool_w_grp[0]),
    }
    big_in = _Big((1, d, 2 * f_hidden), 2, 1)
    big_out = _Big((1, f_hidden, d), 1, 2)
    units = {}
    for l in range(n_layers):
        for k in range(2):
            units[f"in{l}{k}"] = (big_in, "w_ffn_in", 2 * l + k)
            units[f"out{l}{k}"] = (big_out, "w_ffn_out", 2 * l + k)
    units["abin"] = (_Big((1, d, ab_w_in.shape[2] * N_CHIPS), 2, 1), "ab_w_in", 0)
    units["about"] = (_Big((1, ab_w_out.shape[1] * N_CHIPS, d), 1, 2), "ab_w_out", 0)
    units["pool"] = (_Big((pool_w_grp.shape[1], pool_w_grp.shape[2] * N_CHIPS, pool_w_grp.shape[3]), 1, 0), "pool_w_grp", 0)
    big = {u: g for u, (g, _, _) in units.items()}

    weight = {}
    complete = set()

    def cast(u):
        g, st, b0 = units[u]

        def launch(phases):
            (weight[u],), p_outs = _cast_into_full(stacks[st][0], b0, g, where, "cast_" + u, phases)
            return None, p_outs

        return launch

    def gather_ici(*us):
        def then(outs):
            for u, o in zip(us, outs):
                weight[u] = o

        return _phase_gather_ici([weight[u] for u in us], [big[u] for u in us], then)

    def gather_sibling(*us):
        def then(outs):
            for u, o in zip(us, outs):
                weight[u] = o
                complete.add(u)

        return _phase_gather_sibling([weight[u] for u in us], [big[u] for u in us], then)

    def w_of(u):
        assert u in complete, u
        return weight[u]

    _run(cast("in00"), small_gather("inputs", [packed]))
    _run(cast("out00"))
    small_all = small["inputs"][0]
    by_chip = small_all[0::2]
    c_all = small_all[:, 0:N_CHIPS, :].reshape(N_DEV, d)
    norm_full = by_chip[:, 8 : 8 + 3 * n_layers, :].transpose(1, 0, 2).reshape(3 * n_layers, d)
    pool_scale_full = by_chip[:, 16:17, :].transpose(1, 0, 2).reshape(1, d)
    conv_full = by_chip[:, 24:27, : db // N_CHIPS].transpose(1, 0, 2).reshape(3, db)
    mod_cols = _run(lambda phases: _mod_fwd(c_all, w_mod, b_cols, phases), gather_ici("in00"))[0]
    _run(cast("abin"), gather_sibling("in00"), gather_ici("out00"), small_gather("mod", [mod_cols.reshape(n_layers * N_DEV, ncol)]))
    _run(cast("about"), gather_sibling("out00"), gather_ici("abin"))
    _run(cast("in01"), gather_sibling("abin"), gather_ici("about"))
    _run(cast("out01"), gather_sibling("about"))
    for u in ("in10", "out10", "pool", "in11", "out11"):
        _run(cast(u))
    mod_all = small["mod"][0]
    mod_mine = lax.dynamic_index_in_dim(mod_all[0::2].reshape(N_CHIPS, n_layers, N_DEV, ncol), me, axis=2, keepdims=False)
    mod = mod_mine.transpose(1, 0, 2).reshape(n_layers, 3, 3, d)
    vecs = {
        (l, sub): _pad_rows(jnp.concatenate([norm_full[3 * l + sub][None], mod[l, sub]], axis=0))
        for l in range(n_layers)
        for sub in range(3)
    }
    b_rows = jnp.broadcast_to(ab_b_s[0].T[:, :, None], (chunk, heads, da // heads)).reshape(chunk, da)

    saved = {}

    def ffn_forward(xs, l, sub, k, *phases):
        saved[l, sub, "x"] = xs
        xs, gg, uu, yb = _run(
            lambda ph: _ffn_fwd(xs, vecs[l, sub], w_of(f"in{l}{k}"), w_of(f"out{l}{k}"), f"ffn_fwd_{l}{k}", ph), *phases
        )
        saved[l, sub, "act"] = (gg, uu, yb)
        return xs

    xs = ffn_forward(x0, 0, 0, 0, gather_ici("in01"))
    saved[0, 1, "x"] = xs
    (proj,) = _run(lambda ph: _proj_mod_fwd(xs, vecs[0, 1], w_of("abin"), ph), gather_sibling("in01"), gather_ici("out01"))
    (cat,) = _run(lambda ph: _ab_mix_fwd(proj, ab_norm_v, ab_w_s[0], b_rows, conv_full, ph), gather_sibling("out01"), gather_ici("in10"))
    xs, yb = _run(lambda ph: _proj_res_fwd(cat, w_of("about"), xs, vecs[0, 1], ph), gather_sibling("in10"), gather_ici("out10", "pool"))
    saved[0, 1, "act"] = (proj, cat, yb)
    xs = ffn_forward(xs, 0, 2, 1, gather_sibling("out10", "pool"), gather_ici("in11"))
    xs = ffn_forward(xs, 1, 0, 0, gather_sibling("in11"), gather_ici("out11"))
    saved[1, 1, "x"] = xs
    xs, pp, oo = _run(lambda ph: _pool_fwd(xs, vecs[1, 1], w_of("pool"), pool_scale_full, ph), gather_sibling("out11"))
    saved[1, 1, "act"] = (pp, oo)
    xs = ffn_forward(xs, 1, 2, 1)
    dxs, aux = _run(lambda ph: _loss_head(xs, final_g.reshape(1, d), target, ph))
    loss = lax.psum(0.5 * jnp.sum(aux[1]) / d, ("x", "y", "c"))

    grad = {}
    recv = {}
    csum = {}
    parts = {}
    reduced = {}
    done = set()
    dvecs, small_g = {}, {}

    def pair_exchange(*us):
        def then(outs):
            for u, o in zip(us, outs):
                recv[u] = o

        return _phase_pair_exchange([grad[u] for u in us], [big[u] for u in us], then)

    def grad_half(u, a, b, mine, name, *phases, col0=0, prev=None):
        (res,) = _run(lambda ph: _grad_half(a, b, big[u], where, mine, col0, prev, recv[u] if mine else None, name, ph), *phases)
        return res

    def pair_sum(u, *phases):
        def launch(ph):
            (csum[u],), p_outs = _pair_sum(grad[u], recv[u], big[u], where, "pair_sum_" + u, ph)
            return None, p_outs

        _run(launch, *phases)

    def chip_exchange(*us):
        def then(outs):
            for u, o in zip(us, outs):
                parts[u] = o

        return _phase_chip_exchange([csum[u] for u in us], [big[u] for u in us], then)

    def chip_sum(*us, carried=()):
        for n_u, u in enumerate(us):
            g, st, b0 = units[u]

            def launch(ph):
                (reduced[st],), p_outs = _chip_sum(
                    csum[u], parts[u], g, where, reduced.get(st), stacks[st][0].shape, b0, "chip_sum_" + u, ph
                )
                return None, p_outs

            _run(launch, *(carried if n_u == 0 else ()))

    def pair_broadcast(*us):
        sts = [units[u][1] for u in us]
        assert len(set(sts)) == len(sts)

        def then(outs):
            for u, st, o in zip(us, sts, outs):
                reduced[st] = o
                done.add(u)

        return _phase_pair_broadcast([reduced[st] for st in sts], [big[u] for u in us], [units[u][2] for u in us], then)

    def ffn_backward(dxs, l, sub, k, carried_bwd, carried_send, carried_mine):
        gg, uu, yb = saved[l, sub, "act"]
        w_in, w_out = w_of(f"in{l}{k}"), w_of(f"out{l}{k}")
        uo, ui, tag = f"out{l}{k}", f"in{l}{k}", f"{l}{k}"
        dxs, dg, du, a, h, dy, dvecs[l, sub] = _run(
            lambda ph: _ffn_bwd(dxs, saved[l, sub, "x"], vecs[l, sub], gg, uu, yb, w_in, w_out, "ffn_bwd_" + tag, ph), *carried_bwd()
        )
        grad[uo] = grad_half(uo, a, dy, False, "dw_out_send_" + tag, *carried_send())
        part = grad_half(ui, h, du, False, "dw_in_u_send_" + tag, pair_exchange(uo), col0=f_hidden)
        grad[ui] = grad_half(ui, h, dg, False, "dw_in_g_send_" + tag, prev=part)
        csum[uo] = grad_half(uo, a, dy, True, "dw_out_" + tag, pair_exchange(ui))
        part = grad_half(ui, h, du, True, "dw_in_u_" + tag, *carried_mine(), col0=f_hidden)
        csum[ui] = grad_half(ui, h, dg, True, "dw_in_g_" + tag, prev=part)
        return dxs

    none = lambda: ()
    dxs = ffn_backward(dxs, 1, 2, 1, none, none, none)
    pp, oo = saved[1, 1, "act"]
    dxs, grad["pool"], small_g["pool_scale"], dvecs[1, 1] = _run(
        lambda ph: _pool_bwd(dxs, saved[1, 1, "x"], vecs[1, 1], pp, oo, w_of("pool"), pool_scale_full, ph)
    )

    def after_11():
        return (chip_exchange("in11", "out11"), pair_exchange("pool"))

    def bcast_11():
        chip_sum("in11", "out11")
        pair_sum("pool")
        return (pair_broadcast("in11", "out11"), chip_exchange("pool"))

    dxs = ffn_backward(dxs, 1, 0, 0, after_11, bcast_11, none)

    def after_10():
        return (chip_exchange("in10", "out10"),)

    def bcast_10():
        chip_sum("in10", "out10", "pool")
        return (pair_broadcast("in10", "out10", "pool"),)

    dxs = ffn_backward(dxs, 0, 2, 1, after_10, bcast_10, none)

    proj, cat, yb = saved[0, 1, "act"]
    dy, dcat, dgate = _run(lambda ph: _proj_res_bwd(dxs, yb, vecs[0, 1], w_of("about"), ph))
    grad["about"] = grad_half("about", cat, dy, False, "dw_ab_out_send")
    dproj, small_g["ab_norm_v"], small_g["ab_w_s"], dzs, small_g["ab_conv_w"] = _run(
        lambda ph: _ab_mix_bwd(proj, dcat, ab_norm_v, ab_w_s[0], b_rows, conv_full, ph), chip_exchange("out01"), pair_exchange("about")
    )
    small_g["ab_b_s"] = dzs.reshape(chunk, heads, da // heads).sum(axis=2).T
    dxs, h, dvecs[0, 1] = _run(
        lambda ph: _proj_mod_bwd(dproj[None], w_of("abin"), saved[0, 1, "x"], vecs[0, 1], dxs, dgate, "ab_in_bwd", ph)
    )
    grad["abin"] = grad_half("abin", h, dproj, False, "dw_ab_in_send")
    chip_sum("out01", carried=(pair_exchange("abin"),))
    csum["about"] = grad_half("about", cat, dy, True, "dw_ab_out", pair_broadcast("out01"))
    csum["abin"] = grad_half("abin", h, dproj, True, "dw_ab_in")

    def after_01():
        return (chip_exchange("in01", "abin", "about"),)

    def bcast_01():
        chip_sum("in01", "abin", "about")
        return (pair_broadcast("in01", "abin", "about"),)

    def reduce_out00():
        return (chip_exchange("out00"),)

    dxs = ffn_backward(dxs, 0, 0, 0, after_01, bcast_01, reduce_out00)
    grad_x = dxs.reshape(x.shape)

    dgain = jnp.stack([dvecs[l, sub][0] for l in range(n_layers) for sub in range(3)])
    dmod = jnp.concatenate([dvecs[l, sub][1:4] for l in range(n_layers) for sub in range(3)], axis=0)
    pieces = {
        "norm_g": (dgain, None, dq), "final_g": (aux[0:1], 0, d), "pool_scale": (small_g["pool_scale"], None, dq),
        "b_mod": (dmod, 0, d), "ab_norm_v": (small_g["ab_norm_v"], 0, da), "ab_conv_w": (small_g["ab_conv_w"], None, db // N_CHIPS),
        "ab_b_s": (small_g["ab_b_s"], 0, chunk),
    }
    layout, row0 = {}, 0
    for nm, (pc, col0, cols) in pieces.items():
        layout[nm] = (row0, pc.shape[0], col0, cols)
        row0 += pc.shape[0]
    packed_rows = -(-row0 // 8) * 8
    packed_g = sum(
        jnp.pad(pc, ((layout[nm][0], packed_rows - layout[nm][0] - pc.shape[0]), (0, d - pc.shape[1])))
        for nm, (pc, _, _) in pieces.items()
    )

    last_send, last_recv, last_sums, last_zones, started = _chip_exchange_start([csum["in00"]], [big["in00"]], "reduce_last_start")
    chip_sum("out00")
    _flush(
        "broadcast_out00", pair_broadcast("out00"),
        small_gather("grads", [packed_g, small_g["ab_w_s"].reshape(heads * chunk, chunk)]),
    )
    g_all, gws_all = small["grads"]

    out = {}

    def adam_stack(st, after=()):
        w3, m3, v3 = stacks[st]
        assert all(u in done for u, (_, ust, _) in units.items() if ust == st), st
        shape = {"w_ffn_in": w_ffn_in.shape, "w_ffn_out": w_ffn_out.shape, "pool_w_grp": pool_w_grp.shape}.get(st, w3.shape)
        out[st] = tuple(a.reshape(shape) for a in _adam_stack(w3, reduced[st], m3, v3, "adam_" + st, after))

    for st in ("w_ffn_out", "ab_w_in", "ab_w_out", "pool_w_grp"):
        adam_stack(st, (started,))

    shapes2d = {
        "norm_g": (3 * n_layers, dq), "b_mod": (9 * n_layers, d), "final_g": (1, d), "ab_norm_v": (1, da),
        "pool_scale": (1, dq), "ab_conv_w": (3, db // N_CHIPS), "ab_b_s": (heads, chunk), "ab_w_s": (heads * chunk, chunk),
    }
    small_w = {"norm_g": (norm_g, m_norm_g, v_norm_g), "b_mod": (b_mod, m_b_mod, v_b_mod), "final_g": (final_g, m_final_g, v_final_g),
               "ab_norm_v": (ab_norm_v, m_ab_norm_v, v_ab_norm_v), "pool_scale": (pool_scale, m_pool_scale, v_pool_scale),
               "ab_conv_w": (ab_conv_w, m_ab_conv_w, v_ab_conv_w), "ab_b_s": (ab_b_s, m_ab_b_s, v_ab_b_s), "ab_w_s": (ab_w_s, m_ab_w_s, v_ab_w_s)}
    smalls = {nm: tuple(a.reshape(shapes2d[nm]) for a in wmv) for nm, wmv in small_w.items()}
    small_out = _small_adam(g_all, gws_all, layout, smalls, chip)
    for nm, res in small_out.items():
        out[nm] = tuple(a.reshape(small_w[nm][0].shape) for a in res)

    mod_row0 = layout["b_mod"][0]
    dmod_all = g_all[:, mod_row0 : mod_row0 + 9 * n_layers, :].reshape(N_DEV, n_layers, 9 * d)
    dmod_cols = lax.dynamic_slice(dmod_all, (0, 0, chip * ncol), (N_DEV, n_layers, ncol)).transpose(1, 0, 2)
    out["w_mod"] = tuple(_mod_bwd_adam(c_all.T, dmod_cols, w_mod, m_w_mod, v_w_mod, (started,)))

    (csum["in00"],), (parts["in00"],) = _chip_exchange_wait(
        last_send, last_recv, last_sums, last_zones, [big["in00"]],
        [out[st][1] for st in ("w_mod", "w_ffn_out", "ab_w_in", "ab_w_out", "pool_w_grp")], "reduce_last_wait",
    )
    chip_sum("in00")
    _flush("broadcast_last", pair_broadcast("in00"))
    adam_stack("w_ffn_in")

    order = ["norm_g", "w_mod", "b_mod", "w_ffn_in", "w_ffn_out", "ab_w_in", "ab_norm_v", "ab_w_s", "ab_b_s", "ab_conv_w", "ab_w_out", "pool_w_grp", "pool_scale", "final_g"]
    return (loss, grad_x, *[out[nm][0] for nm in order], *[out[nm][1] for nm in order], *[out[nm][2] for nm in order], *[out[nm][3] for nm in order])
```

```python
import functools
import math

import jax
import jax.numpy as jnp
from jax import lax
from jax.experimental import pallas as pl
from jax.experimental.pallas import tpu as pltpu

F32 = jnp.float32
BF16 = jnp.bfloat16
MESH = pl.DeviceIdType.MESH

EPS = 1e-6
ADAM_LR = 0.001
ADAM_B1 = 0.9
ADAM_B2 = 0.999
ADAM_EPS = 1e-08
ADAM_WD = 0.01
ADAM_STEP = 10
POOL_WINDOWS = (2, 4, 8, 16)
POOL_HALO = 16
CONV_HALO = 8
N_CHIPS = 4
N_DEV = 8
VMEM_LIMIT_BYTES = 48 * 1024 * 1024
EW_BLOCK_ELEMS = 256 * 1024


def _pick(n, prefs):
    for p in prefs:
        if p <= n and n % p == 0:
            return p
    return n


def _row_tile(rows, cols):
    best = None
    for d in range(16, rows + 1, 16):
        if rows % d == 0 and d * cols <= EW_BLOCK_ELEMS:
            best = d
    return best or rows


def _dot(a, b):
    return jnp.dot(a, b, preferred_element_type=F32)


def _dot_nt(a, b):
    return lax.dot_general(a, b, (((1,), (1,)), ((), ())), preferred_element_type=F32)


def _dot_tn(a, b):
    return lax.dot_general(a, b, (((0,), (0,)), ((), ())), preferred_element_type=F32)


def _sigmoid(x):
    return 0.5 * jnp.tanh(0.5 * x) + 0.5


_GELU_C = math.sqrt(2.0 / math.pi)


def _gelu(x):
    x2 = x * x
    t = jnp.tanh(_GELU_C * (x + 0.044715 * x2 * x))
    val = 0.5 * x * (1.0 + t)
    grad = 0.5 * (1.0 + t) + 0.5 * x * (1.0 - t * t) * (_GELU_C * (1.0 + 3.0 * 0.044715 * x2))
    return val, grad


def _rstd(x):
    return lax.rsqrt(jnp.mean(x * x, axis=-1, keepdims=True) + EPS)


def _modulate(x, vec_ref):
    return (x * _rstd(x)) * vec_ref[0:1, :] * (1.0 + vec_ref[2:3, :]) + vec_ref[1:2, :]


def _modulate_bwd(x, dh, vec_ref, dvec_ref):
    gn, sh, sc = vec_ref[0:1, :], vec_ref[1:2, :], vec_ref[2:3, :]
    rstd = _rstd(x)
    r = x * rstd
    dvec_ref[0:1, :] += jnp.sum(dh * r * (1.0 + sc), axis=0, keepdims=True)
    dvec_ref[1:2, :] += jnp.sum(dh, axis=0, keepdims=True)
    dvec_ref[2:3, :] += jnp.sum(dh * r * gn, axis=0, keepdims=True)
    gm = gn * (1.0 + sc)
    dr = dh * gm
    dx = rstd * (dr - r * jnp.mean(dr * r, axis=-1, keepdims=True))
    return dx, r * gm + sh


def _adam(w, g, m, v):
    m = ADAM_B1 * m + (1.0 - ADAM_B1) * g
    v = ADAM_B2 * v + (1.0 - ADAM_B2) * (g * g)
    m_hat = m / (1.0 - ADAM_B1**ADAM_STEP)
    v_hat = v / (1.0 - ADAM_B2**ADAM_STEP)
    delta = -ADAM_LR * (m_hat / (jnp.sqrt(v_hat) + ADAM_EPS) + ADAM_WD * w)
    return delta, m, v


_ANY = pl.BlockSpec(memory_space=pl.ANY)


class _Phase:
    def __init__(self, ins, out_shapes, aliases, n_sems, start, finish, then):
        self.ins, self.out_shapes, self.aliases, self.n_sems = list(ins), list(out_shapes), dict(aliases), n_sems
        self.start, self.finish, self.then = start, finish, then


def _call(body, name, grid, in_specs, out_specs, out_shape, ins, scratch=(), prefetch=(), phases=(), in_place=None):
    n_pre, n_in, n_out, n_sc = len(prefetch), len(in_specs), len(out_specs), len(scratch)
    ph_in = [len(p.ins) for p in phases]
    ph_out = [len(p.out_shapes) for p in phases]

    def kernel_body(*refs):
        pos = [0]

        def take(k):
            pos[0] += k
            return refs[pos[0] - k : pos[0]]

        pre, ins_ = take(n_pre), take(n_in)
        p_ins = [take(k) for k in ph_in]
        outs_ = take(n_out)
        p_outs = [take(k) for k in ph_out]
        sc = take(n_sc)
        sems = [take(2) for _ in phases]
        if phases:
            ids = [pl.program_id(a) for a in range(len(grid))]
            first = functools.reduce(jnp.logical_and, [i == 0 for i in ids])
            last = functools.reduce(jnp.logical_and, [i == g - 1 for i, g in zip(ids, grid)])

            @pl.when(first)
            def _():
                for p, pi, po, (send, recv) in zip(phases, p_ins, p_outs, sems):
                    p.start(pi, po, send, recv)

        if body is not None:
            body(*pre, *ins_, *outs_, *sc)
        if phases:

            @pl.when(last)
            def _():
                for p, pi, po, (send, recv) in zip(phases, p_ins, p_outs, sems):
                    p.finish(pi, po, send, recv)

    aliases = {n_pre + i: o for i, o in (in_place or {}).items()}
    i0, o0 = n_pre + n_in, n_out
    for p in phases:
        for i, o in p.aliases.items():
            aliases[i0 + i] = o0 + o
        i0 += len(p.ins)
        o0 += len(p.out_shapes)
    all_in = list(in_specs) + [_ANY] * sum(ph_in)
    all_out = list(out_specs) + [_ANY] * sum(ph_out)
    all_scratch = list(scratch)
    for p in phases:
        all_scratch += [pltpu.SemaphoreType.DMA((p.n_sems,)), pltpu.SemaphoreType.DMA((p.n_sems,))]
    shapes = list(out_shape) + [s for p in phases for s in p.out_shapes]
    operands = list(prefetch) + list(ins) + [a for p in phases for a in p.ins]
    sem = ("arbitrary",) * len(grid)
    params = pltpu.CompilerParams(dimension_semantics=sem, vmem_limit_bytes=VMEM_LIMIT_BYTES)
    if n_pre:
        res = pl.pallas_call(
            kernel_body, name=name, out_shape=shapes, input_output_aliases=aliases, compiler_params=params,
            grid_spec=pltpu.PrefetchScalarGridSpec(
                num_scalar_prefetch=n_pre, grid=grid, in_specs=all_in, out_specs=all_out, scratch_shapes=all_scratch
            ),
        )(*operands)
    else:
        res = pl.pallas_call(
            kernel_body, name=name, grid=grid, in_specs=all_in, out_specs=all_out, out_shape=shapes,
            scratch_shapes=all_scratch, input_output_aliases=aliases, compiler_params=params,
        )(*operands)
    res = list(res)
    outs, rest = res[:n_out], res[n_out:]
    p_res = []
    for k in ph_out:
        p_res.append(rest[:k])
        rest = rest[k:]
    return outs, p_res


def _place():
    return lax.axis_index("x"), lax.axis_index("y"), lax.axis_index("c")


def _other_chips():
    x, y, _ = _place()
    return [(1 - x, y), (x, 1 - y), (1 - x, 1 - y)]


def _flip(k):
    x, y, c = _place()
    return (1 - x if k & 4 else x, 1 - y if k & 2 else y, 1 - c if k & 1 else c)


def _remote(src, dst, send, recv, k, to):
    return pltpu.make_async_remote_copy(
        src_ref=src, dst_ref=dst, send_sem=send.at[k], recv_sem=recv.at[k], device_id=to, device_id_type=MESH
    )


def _phase_small_gather(arrs, then):
    n = len(arrs)

    def copies(ins, outs, send, recv):
        x, y, c = _place()
        me = 4 * x + 2 * y + c
        local = [pltpu.make_async_copy(ins[a], outs[a].at[me], send.at[a * N_DEV]) for a in range(n)]
        remote = [_remote(ins[a], outs[a].at[me], send, recv, a * N_DEV + k, _flip(k)) for a in range(n) for k in range(1, N_DEV)]
        return local, remote

    def start(ins, outs, send, recv):
        local, remote = copies(ins, outs, send, recv)
        for cp in local + remote:
            cp.start()

    def finish(ins, outs, send, recv):
        local, remote = copies(ins, outs, send, recv)
        for cp in remote + local:
            cp.wait()

    shapes = [jax.ShapeDtypeStruct((N_DEV,) + a.shape, a.dtype) for a in arrs]
    return _Phase(arrs, shapes, {}, n * N_DEV, start, finish, then)


def _flush(name, *phases):
    _, p_outs = _call(None, name, (1,), [], [], [], [], phases=list(phases))
    for p, po in zip(phases, p_outs):
        p.then(po)


class _Big:
    KINDS = {"full": (True, True), "half": (True, False), "shard": (False, True), "block": (False, False)}

    def __init__(self, f3, s3, h3):
        assert s3 != h3
        self.f3, self.s3, self.h3 = tuple(f3), s3, h3
        self.bd = tuple(f3[a] // (N_CHIPS if a == s3 else 1) // (2 if a == h3 else 1) for a in range(3))
        self.tile = (1, _row_tile(self.bd[1], self.bd[2]), self.bd[2])
        self.grid = tuple(self.bd[a] // self.tile[a] for a in range(3))

    def dims(self, kind):
        chips, halves = self.KINDS[kind]
        return tuple(
            self.bd[a] * (N_CHIPS if chips and a == self.s3 else 1) * (2 if halves and a == self.h3 else 1) for a in range(3)
        )

    def view(self, ref, chip=None, half=None, batch0=0, both_halves=True):
        start = [batch0, 0, 0]
        size = list(ref.shape)
        size[0] = self.bd[0] * (2 if self.h3 == 0 and both_halves else 1)
        if chip is not None:
            start[self.s3] += chip * self.bd[self.s3]
            size[self.s3] = self.bd[self.s3]
        if half is not None:
            start[self.h3] += half * self.bd[self.h3]
            size[self.h3] = self.bd[self.h3]
        return ref.at[tuple(pl.ds(st, sz) for st, sz in zip(start, size))]

    def spec(self, chip_from=None, half_from=None, lead=(), batch0=0):
        extra = "grid" in (chip_from, half_from)

        def index(*args):
            pref, idx = args[-1], list(args[int(extra) : -1])
            idx[0] += batch0
            if chip_from:
                idx[self.s3] += (pref[0] if chip_from == "pref" else args[0]) * self.grid[self.s3]
            if half_from:
                idx[self.h3] += (pref[1] if half_from == "pref" else args[0]) * self.grid[self.h3]
            return (0,) * len(lead) + tuple(idx)

        return pl.BlockSpec(tuple(lead) + self.tile, index)


def _same(arrs):
    return [jax.ShapeDtypeStruct(a.shape, a.dtype) for a in arrs]


def _phase_gather_ici(arrs, bigs, then):
    n = len(arrs)

    def copies(outs, send, recv, arriving):
        x, y, c = _place()
        return [
            _remote(blk, blk, send, recv, 3 * a + j, (*chip, c))
            for j, chip in enumerate(_other_chips())
            for a in range(n)
            for blk in [bigs[a].view(outs[a], 2 * chip[0] + chip[1] if arriving else 2 * x + y, c)]
        ]

    def start(ins, outs, send, recv):
        for cp in copies(outs, send, recv, False):
            cp.start()

    def finish(ins, outs, send, recv):
        for cp in copies(outs, send, recv, True):
            cp.wait_recv()
        for cp in copies(outs, send, recv, False):
            cp.wait_send()

    return _Phase(arrs, _same(arrs), {a: a for a in range(n)}, 3 * n, start, finish, then)


def _phase_gather_sibling(arrs, bigs, then):
    n = len(arrs)

    def copies(outs, send, recv, arriving):
        x, y, c = _place()
        return [
            _remote(blk, blk, send, recv, 3 * a + j, (x, y, 1 - c))
            for j, chip in enumerate(_other_chips())
            for a in range(n)
            for blk in [bigs[a].view(outs[a], 2 * chip[0] + chip[1], 1 - c if arriving else c)]
        ]

    def start(ins, outs, send, recv):
        for cp in copies(outs, send, recv, False):
            cp.start()

    def finish(ins, outs, send, recv):
        for cp in copies(outs, send, recv, True):
            cp.wait_recv()
        for cp in copies(outs, send, recv, False):
            cp.wait_send()

    return _Phase(arrs, _same(arrs), {a: a for a in range(n)}, 3 * n, start, finish, then)


def _phase_pair_exchange(grads, bigs, then):
    n = len(grads)

    def copies(ins, outs, send, recv):
        x, y, c = _place()
        srcs = [ins[a] if ins[a].shape == outs[a].shape else bigs[a].view(ins[a], None, 1 - c) for a in range(n)]
        return [_remote(srcs[a], outs[a], send, recv, a, (x, y, 1 - c)) for a in range(n)]

    def start(ins, outs, send, recv):
        for cp in copies(ins, outs, send, recv):
            cp.start()

    def finish(ins, outs, send, recv):
        for cp in copies(ins, outs, send, recv):
            cp.wait()

    shapes = [jax.ShapeDtypeStruct(b.dims("half"), BF16) for b in bigs]
    return _Phase(grads, shapes, {}, n, start, finish, then)


def _phase_chip_exchange(sums, bigs, then):
    n = len(sums)

    def copies(ins, outs, send, recv):
        _, _, c = _place()
        return [
            _remote(bigs[a].view(ins[a], 2 * chip[0] + chip[1], both_halves=False), outs[a].at[j], send, recv, 3 * a + j, (*chip, c))
            for j, chip in enumerate(_other_chips())
            for a in range(n)
        ]

    def start(ins, outs, send, recv):
        for cp in copies(ins, outs, send, recv):
            cp.start()

    def finish(ins, outs, send, recv):
        for cp in copies(ins, outs, send, recv):
            cp.wait()

    shapes = [jax.ShapeDtypeStruct((N_CHIPS - 1,) + b.dims("block"), BF16) for b in bigs]
    return _Phase(sums, shapes, {}, 3 * n, start, finish, then)


_HBM = pl.BlockSpec(memory_space=pltpu.HBM)
_SEM = pl.BlockSpec(memory_space=pltpu.SEMAPHORE)
_DATAFLOW = pltpu.SideEffectType.DATAFLOW_SIDE_EFFECTING


class _InFlight:
    def __init__(self, phase, send, recv, arrays, token):
        self.phase, self.send, self.recv, self.arrays, self.token = phase, send, recv, arrays, token


def _phase_results(phase, refs):
    n_in = len(phase.ins)
    updated = {o: i for i, o in phase.aliases.items()}
    fresh = [o for o in range(len(phase.out_shapes)) if o not in updated]
    return [refs[updated[o]] if o in updated else refs[n_in + fresh.index(o)] for o in range(len(phase.out_shapes))]


def _split_start(phase, name):
    n_in = len(phase.ins)
    fresh = [s for o, s in enumerate(phase.out_shapes) if o not in phase.aliases.values()]
    arrays = list(phase.ins) + [lax.empty(s.shape, s.dtype) for s in fresh]
    n = len(arrays)

    def body(*refs):
        phase.start(refs[:n_in], _phase_results(phase, refs[:n]), refs[n], refs[n + 1])
        refs[-1][...] = jnp.zeros_like(refs[-1])

    operands = [pltpu.with_memory_space_constraint(a, pltpu.HBM) for a in arrays]
    res = pl.pallas_call(
        body, name=name,
        out_shape=[pltpu.SemaphoreType.DMA((phase.n_sems,)), pltpu.SemaphoreType.DMA((phase.n_sems,))]
        + [pltpu.HBM(a.shape, a.dtype) for a in arrays] + [jax.ShapeDtypeStruct((8, 128), F32)],
        in_specs=[_HBM] * n, out_specs=[_SEM, _SEM] + [_HBM] * n + [pl.BlockSpec(memory_space=pltpu.VMEM)],
        input_output_aliases={i: 2 + i for i in range(n)},
        compiler_params=pltpu.CompilerParams(has_side_effects=_DATAFLOW),
    )(*operands)
    return _InFlight(phase, res[0], res[1], list(res[2 : 2 + n]), res[-1])


def _split_wait(flight, after, name):
    phase, n = flight.phase, len(flight.arrays)
    n_in = len(phase.ins)

    def body(*refs):
        phase.finish(refs[:n_in], _phase_results(phase, refs[:n]), refs[n], refs[n + 1])

    res = pl.pallas_call(
        body, name=name, out_shape=[pltpu.HBM(a.shape, a.dtype) for a in flight.arrays],
        in_specs=[_HBM] * n + [_SEM, _SEM] + [_ANY] * len(after), out_specs=[_HBM] * n,
        input_output_aliases={i: i for i in range(n)},
        compiler_params=pltpu.CompilerParams(has_side_effects=_DATAFLOW),
    )(*flight.arrays, flight.send, flight.recv, *after)
    res = list(res)
    phase.then(_phase_results(phase, res))
    return res[:n_in]


def _phase_pair_broadcast(stacks, bigs, batch0s, then):
    n = len(stacks)

    def start(ins, outs, send, recv):
        x, y, c = _place()
        for a in range(n):
            blk = bigs[a].view(outs[a], None, c, batch0s[a])
            _remote(blk, blk, send, recv, a, (x, y, 1 - c)).start()

    def finish(ins, outs, send, recv):
        x, y, c = _place()
        for a in range(n):
            mine = bigs[a].view(outs[a], None, c, batch0s[a])
            theirs = bigs[a].view(outs[a], None, 1 - c, batch0s[a])
            _remote(mine, mine, send, recv, a, (x, y, 1 - c)).wait_send()
            _remote(theirs, theirs, send, recv, a, (x, y, 1 - c)).wait_recv()

    return _Phase(stacks, _same(stacks), {a: a for a in range(n)}, n, start, finish, then)


def _tile_call(body, name, big, where, extra, ins, in_specs, out_specs, out_shape, phases=()):
    grid = ((extra,) if extra else ()) + big.grid
    return _call(body, name, grid, in_specs, out_specs, out_shape, ins, prefetch=(where,), phases=phases)


def _cast_into_full(w_stack, batch0, big, where, name, phases=()):
    def body(_, w_ref, o_ref):
        o_ref[...] = w_ref[...].astype(BF16)

    return _tile_call(
        body, name, big, where, 2, [w_stack], [big.spec(None, "grid", batch0=batch0)], [big.spec("pref", "grid")],
        [jax.ShapeDtypeStruct(big.dims("full"), BF16)], phases,
    )


def _pair_sum(g_full, recv_half, big, where, name, phases=()):
    def body(_, g_ref, r_ref, o_ref):
        o_ref[...] = (g_ref[...].astype(F32) + r_ref[...].astype(F32)).astype(BF16)

    half = big.spec("grid", None)
    return _tile_call(
        body, name, big, where, N_CHIPS, [g_full, recv_half], [big.spec("grid", "pref"), half], [half],
        [jax.ShapeDtypeStruct(big.dims("half"), BF16)], phases,
    )


def _chip_sum(chip_sum, parts, big, where, stack, stack_shape, batch0, name, phases=()):
    def body(_, own_ref, p_ref, *rest):
        acc = own_ref[...].astype(F32)
        for k in range(N_CHIPS - 1):
            acc = acc + p_ref[k].astype(F32)
        rest[-1][...] = acc

    ins = [chip_sum, parts] + ([stack] if stack is not None else [])
    in_specs = [big.spec("pref", None), big.spec(None, None, lead=(N_CHIPS - 1,))] + ([_ANY] if stack is not None else [])
    return _call(
        body, name, big.grid, in_specs, [big.spec(None, "pref", batch0=batch0)], [jax.ShapeDtypeStruct(stack_shape, F32)], ins,
        prefetch=(where,), phases=phases, in_place={2: 0} if stack is not None else None,
    )


def _adam_stack(w, g, m, v, name, after=()):
    b, r, c = w.shape
    tr = _row_tile(r, c)

    def body(w_ref, g_ref, m_ref, v_ref, *rest):
        go_ref, d_ref, mo_ref, vo_ref = rest[-4:]
        gv = g_ref[...]
        d, mo, vo = _adam(w_ref[...], gv, m_ref[...], v_ref[...])
        go_ref[...] = gv
        d_ref[...] = d
        mo_ref[...] = mo
        vo_ref[...] = vo

    spec = pl.BlockSpec((1, tr, c), lambda bb, i: (bb, i, 0))
    outs, _ = _call(
        body, name, (b, r // tr), [spec] * 4 + [_ANY] * len(after), [spec] * 4, [jax.ShapeDtypeStruct(w.shape, F32)] * 4,
        [w, g, m, v, *after],
    )
    return outs


def _mod_fwd(c_all, w_mod, b_cols, phases=()):
    n_layers, d, n = w_mod.shape
    tn = _pick(n, (768, 512, 384, 256, 128))

    def body(c_ref, w_ref, b_ref, o_ref):
        cv = c_ref[...]
        ca = (cv * _sigmoid(cv)).astype(BF16)
        o_ref[0] = _dot(ca, w_ref[0].astype(BF16)) + b_ref[0]

    return _call(
        body, "mod_fwd", (n_layers, n // tn),
        [
            pl.BlockSpec((N_DEV, d), lambda l, j: (0, 0)),
            pl.BlockSpec((1, d, tn), lambda l, j: (l, 0, j)),
            pl.BlockSpec((1, 1, tn), lambda l, j: (l, 0, j)),
        ],
        [pl.BlockSpec((1, N_DEV, tn), lambda l, j: (l, 0, j))],
        [jax.ShapeDtypeStruct((n_layers, N_DEV, n), F32)], [c_all, w_mod, b_cols], phases=phases,
    )


def _mod_bwd_adam(c_all_t, dmod_cols, w, m, v, after=()):
    n_layers, d, n = w.shape
    tn = _pick(n, (384, 256, 128))

    def body(c_ref, dm_ref, w_ref, m_ref, v_ref, *rest):
        g_ref, d_ref, mo_ref, vo_ref = rest[-4:]
        cv = c_ref[...]
        ca = (cv * _sigmoid(cv)).astype(BF16)
        g = _dot(ca, dm_ref[0].astype(BF16))
        g_ref[0] = g
        dl, mo, vo = _adam(w_ref[0], g, m_ref[0], v_ref[0])
        d_ref[0] = dl
        mo_ref[0] = mo
        vo_ref[0] = vo

    wspec = pl.BlockSpec((1, d, tn), lambda l, j: (l, 0, j))
    outs, _ = _call(
        body, "mod_bwd_adam", (n_layers, n // tn),
        [pl.BlockSpec((d, N_DEV), lambda l, j: (0, 0)), pl.BlockSpec((1, N_DEV, tn), lambda l, j: (l, 0, j)), wspec, wspec, wspec]
        + [_ANY] * len(after),
        [wspec] * 4, [jax.ShapeDtypeStruct(w.shape, F32)] * 4, [c_all_t, dmod_cols, w, m, v, *after],
    )
    return outs


def _ffn_fwd(x, vec, w_in, w_out, name, phases=()):
    s, d = x.shape
    f = w_out.shape[1]
    tm = _pick(s, (1024, 512, 256, 128))
    tf = _pick(f, (256, 128))
    nf = f // tf

    def body(x_ref, vec_ref, wg_ref, wu_ref, wo_ref, xo_ref, g_ref, u_ref, y_ref, h_sc, acc_sc):
        j = pl.program_id(1)

        @pl.when(j == 0)
        def _():
            h_sc[...] = _modulate(x_ref[...], vec_ref).astype(BF16)
            acc_sc[...] = jnp.zeros_like(acc_sc)

        h = h_sc[...]
        g = _dot(h, wg_ref[0])
        u = _dot(h, wu_ref[0])
        g_ref[...] = g.astype(BF16)
        u_ref[...] = u.astype(BF16)
        a = (g * _sigmoid(g) * u).astype(BF16)
        acc_sc[...] += _dot(a, wo_ref[0])

        @pl.when(j == nf - 1)
        def _():
            yv = acc_sc[...]
            xo_ref[...] = x_ref[...] + 0.5 * vec_ref[3:4, :] * yv
            y_ref[...] = yv.astype(BF16)

    row = pl.BlockSpec((tm, d), lambda i, j: (i, 0))
    hid = pl.BlockSpec((tm, tf), lambda i, j: (i, j))
    return _call(
        body, name, (s // tm, nf),
        [
            row,
            pl.BlockSpec((8, d), lambda i, j: (0, 0)),
            pl.BlockSpec((1, d, tf), lambda i, j: (0, 0, j)),
            pl.BlockSpec((1, d, tf), lambda i, j: (0, 0, nf + j)),
            pl.BlockSpec((1, tf, d), lambda i, j: (0, j, 0)),
        ],
        [row, hid, hid, row],
        [
            jax.ShapeDtypeStruct((s, d), F32),
            jax.ShapeDtypeStruct((s, f), BF16),
            jax.ShapeDtypeStruct((s, f), BF16),
            jax.ShapeDtypeStruct((s, d), BF16),
        ],
        [x, vec, w_in, w_in, w_out],
        scratch=[pltpu.VMEM((tm, d), BF16), pltpu.VMEM((tm, d), F32)], phases=phases,
    )


def _ffn_bwd(dxo, x, vec, gg, uu, y, w_in, w_out, name, phases=()):
    s, d = x.shape
    f = w_out.shape[1]
    tm = _pick(s, (512, 256, 128))
    tf = _pick(f, (256, 128))
    nf = f // tf

    def body(dxo_ref, x_ref, vec_ref, g_ref, u_ref, y_ref, wg_ref, wu_ref, wo_ref,
             dx_ref, dg_ref, du_ref, a_ref, h_ref, dy_ref, dvec_ref, acc_sc):
        i, j = pl.program_id(0), pl.program_id(1)

        @pl.when((i == 0) & (j == 0))
        def _():
            dvec_ref[...] = jnp.zeros_like(dvec_ref)

        @pl.when(j == 0)
        def _():
            dxo_v = dxo_ref[...]
            dy_ref[...] = (0.5 * vec_ref[3:4, :] * dxo_v).astype(BF16)
            dvec_ref[3:4, :] += 0.5 * jnp.sum(dxo_v * y_ref[...].astype(F32), axis=0, keepdims=True)
            acc_sc[...] = jnp.zeros_like(acc_sc)

        da = _dot_nt(dy_ref[...], wo_ref[0])
        g = g_ref[...].astype(F32)
        u = u_ref[...].astype(F32)
        sig = _sigmoid(g)
        sl = g * sig
        a_ref[...] = (sl * u).astype(BF16)
        dg = (da * u * (sig * (1.0 + g * (1.0 - sig)))).astype(BF16)
        du = (da * sl).astype(BF16)
        dg_ref[...] = dg
        du_ref[...] = du
        acc_sc[...] += _dot_nt(dg, wg_ref[0]) + _dot_nt(du, wu_ref[0])

        @pl.when(j == nf - 1)
        def _():
            dx, h = _modulate_bwd(x_ref[...], acc_sc[...], vec_ref, dvec_ref)
            dx_ref[...] = dxo_ref[...] + dx
            h_ref[...] = h.astype(BF16)

    row = pl.BlockSpec((tm, d), lambda i, j: (i, 0))
    hid = pl.BlockSpec((tm, tf), lambda i, j: (i, j))
    vecs = pl.BlockSpec((8, d), lambda i, j: (0, 0))
    return _call(
        body, name, (s // tm, nf),
        [
            row, row, vecs, hid, hid, row,
            pl.BlockSpec((1, d, tf), lambda i, j: (0, 0, j)),
            pl.BlockSpec((1, d, tf), lambda i, j: (0, 0, nf + j)),
            pl.BlockSpec((1, tf, d), lambda i, j: (0, j, 0)),
        ],
        [row, hid, hid, hid, row, row, vecs],
        [
            jax.ShapeDtypeStruct((s, d), F32),
            jax.ShapeDtypeStruct((s, f), BF16),
            jax.ShapeDtypeStruct((s, f), BF16),
            jax.ShapeDtypeStruct((s, f), BF16),
            jax.ShapeDtypeStruct((s, d), BF16),
            jax.ShapeDtypeStruct((s, d), BF16),
            jax.ShapeDtypeStruct((8, d), F32),
        ],
        [dxo, x, vec, gg, uu, y, w_in, w_in, w_out],
        scratch=[pltpu.VMEM((tm, d), F32)], phases=phases,
    )


def _grad_half(a, b, big, where, mine, col0, prev, recv, name, phases=()):
    s, k1 = a.shape
    b, b_part = b if isinstance(b, tuple) else (b[None], 0)
    n = b.shape[2]
    rows_halved = big.h3 == 1
    kk, nn = (k1 // 2, n) if rows_halved else (k1, n // 2)
    tk = _pick(kk, (1408, 1024, 512, 256, 128))
    tn = _pick(nn, (1408, 1024, 640, 512, 256, 128))
    nkb, nnb = kk // tk, nn // tn
    assert col0 % tn == 0 and (recv is None) == (not mine)

    def half(pref):
        return pref[1] if mine else 1 - pref[1]

    def body(_, a_ref, b_ref, *rest):
        acc = _dot_tn(a_ref[...], b_ref[0])
        if recv is not None:
            acc = acc + rest[0][0].astype(F32)
        rest[-1][0] = acc.astype(BF16)

    out_spec = pl.BlockSpec((1, tk, tn), lambda i, j, pref: (0, i, col0 // tn + j))
    in_specs = [
        pl.BlockSpec((s, tk), lambda i, j, pref: (0, i + (half(pref) * nkb if rows_halved else 0))),
        pl.BlockSpec((1, s, tn), lambda i, j, pref: (b_part, 0, j + (0 if rows_halved else half(pref) * nnb))),
    ]
    ins = [a, b]
    if recv is not None:
        in_specs.append(out_spec)
        ins.append(recv)
    in_place = None
    if prev is not None:
        in_place = {len(ins): 0}
        in_specs.append(_ANY)
        ins.append(prev)
    return _call(
        body, name, (nkb, nnb), in_specs, [out_spec], [jax.ShapeDtypeStruct(big.dims("half"), BF16)], ins,
        prefetch=(where,), phases=phases, in_place=in_place,
    )


def _proj_mod_fwd(x, vec, w, phases=()):
    s, d = x.shape
    n = w.shape[2]
    tm = _pick(s, (512, 256, 128))
    tn = _pick(n, (640, 512, 256, 128))

    def body(x_ref, vec_ref, w_ref, o_ref, h_sc):
        @pl.when(pl.program_id(1) == 0)
        def _():
            h_sc[...] = _modulate(x_ref[...], vec_ref).astype(BF16)

        o_ref[...] = _dot(h_sc[...], w_ref[0])

    return _call(
        body, "ab_in_fwd", (s // tm, n // tn),
        [
            pl.BlockSpec((tm, d), lambda i, j: (i, 0)),
            pl.BlockSpec((8, d), lambda i, j: (0, 0)),
            pl.BlockSpec((1, d, tn), lambda i, j: (0, 0, j)),
        ],
        [pl.BlockSpec((tm, tn), lambda i, j: (i, j))],
        [jax.ShapeDtypeStruct((s, n), F32)], [x, vec, w],
        scratch=[pltpu.VMEM((tm, d), BF16)], phases=phases,
    )


def _proj_res_fwd(a, w, x, vec, phases=()):
    s, kd = a.shape
    d = x.shape[1]
    tm = _pick(s, (512, 256, 128))

    def body(a_ref, w_ref, x_ref, vec_ref, xo_ref, y_ref):
        yv = _dot(a_ref[...], w_ref[0])
        xo_ref[...] = x_ref[...] + vec_ref[3:4, :] * yv
        y_ref[...] = yv.astype(BF16)

    row = pl.BlockSpec((tm, d), lambda i: (i, 0))
    return _call(
        body, "ab_out_fwd", (s // tm,),
        [pl.BlockSpec((tm, kd), lambda i: (i, 0)), pl.BlockSpec((1, kd, d), lambda i: (0, 0, 0)), row, pl.BlockSpec((8, d), lambda i: (0, 0))],
        [row, row],
        [jax.ShapeDtypeStruct((s, d), F32), jax.ShapeDtypeStruct((s, d), BF16)], [a, w, x, vec], phases=phases,
    )


def _proj_res_bwd(dxo, y, vec, w, phases=()):
    s, d = dxo.shape
    kd = w.shape[1]
    tm = _pick(s, (512, 256, 128))

    def body(dxo_ref, y_ref, vec_ref, w_ref, dy_ref, da_ref, dgate_ref):
        @pl.when(pl.program_id(0) == 0)
        def _():
            dgate_ref[...] = jnp.zeros_like(dgate_ref)

        dxo_v = dxo_ref[...]
        dy = (vec_ref[3:4, :] * dxo_v).astype(BF16)
        dy_ref[...] = dy
        dgate_ref[3:4, :] += jnp.sum(dxo_v * y_ref[...].astype(F32), axis=0, keepdims=True)
        da_ref[...] = _dot_nt(dy, w_ref[0]).astype(BF16)

    row = pl.BlockSpec((tm, d), lambda i: (i, 0))
    vecs = pl.BlockSpec((8, d), lambda i: (0, 0))
    return _call(
        body, "ab_out_bwd", (s // tm,),
        [row, row, vecs, pl.BlockSpec((1, kd, d), lambda i: (0, 0, 0))],
        [row, pl.BlockSpec((tm, kd), lambda i: (i, 0)), vecs],
        [jax.ShapeDtypeStruct((s, d), BF16), jax.ShapeDtypeStruct((s, kd), BF16), jax.ShapeDtypeStruct((8, d), F32)],
        [dxo, y, vec, w], phases=phases,
    )


def _proj_mod_bwd(dproj, w, x, vec, dxo, dvec_in, name, phases=()):
    parts, s, n_part = dproj.shape
    d = x.shape[1]
    tm = _pick(s, (512, 256, 128))
    tk = _pick(n_part, (1408, 1280, 1024, 512, 256, 128))
    per_part = n_part // tk
    nk = parts * per_part

    def body(dp_ref, w_ref, x_ref, vec_ref, dxo_ref, dvi_ref, dx_ref, h_ref, dvec_ref, acc_sc):
        i, k = pl.program_id(0), pl.program_id(1)

        @pl.when((i == 0) & (k == 0))
        def _():
            dvec_ref[...] = dvi_ref[...]

        @pl.when(k == 0)
        def _():
            acc_sc[...] = jnp.zeros_like(acc_sc)

        acc_sc[...] += _dot_nt(dp_ref[0], w_ref[0])

        @pl.when(k == nk - 1)
        def _():
            dx, h = _modulate_bwd(x_ref[...], acc_sc[...], vec_ref, dvec_ref)
            dx_ref[...] = dxo_ref[...] + dx
            h_ref[...] = h.astype(BF16)

    row = pl.BlockSpec((tm, d), lambda i, k: (i, 0))
    vecs = pl.BlockSpec((8, d), lambda i, k: (0, 0))
    return _call(
        body, name, (s // tm, nk),
        [
            pl.BlockSpec((1, tm, tk), lambda i, k: (k // per_part, i, k % per_part)),
            pl.BlockSpec((1, d, tk), lambda i, k: (0, 0, k)),
            row, vecs, row, vecs,
        ],
        [row, row, vecs],
        [jax.ShapeDtypeStruct((s, d), F32), jax.ShapeDtypeStruct((s, d), BF16), jax.ShapeDtypeStruct((8, d), F32)],
        [dproj, w, x, vec, dxo, dvec_in], scratch=[pltpu.VMEM((tm, d), F32)], phases=phases,
    )


def _tril(n):
    return lax.broadcasted_iota(jnp.int32, (n, n), 0) >= lax.broadcasted_iota(jnp.int32, (n, n), 1)


def _layernorm_stats(gv):
    mu = jnp.mean(gv, axis=-1, keepdims=True)
    cen = gv - mu
    rstd = lax.rsqrt(jnp.mean(cen * cen, axis=-1, keepdims=True) + EPS)
    return cen * rstd, rstd


def _shift_down(q, k, above_ref, c_cg, c_xb, first):
    width = q.shape[1]
    rows = lax.broadcasted_iota(jnp.int32, q.shape, 0)
    out = pltpu.roll(q, k, 0)
    for r in range(k):
        src = CONV_HALO - k + r
        above = above_ref[src : src + 1, c_cg : c_cg + width] * above_ref[src : src + 1, c_xb : c_xb + width]
        above = jnp.where(first, 0.0, above)
        out = jnp.where(rows == r, above, out)
    return out


def _ab_mix_fwd(proj, norm_v, w_s, b_rows, conv_w, phases=()):
    s, n = proj.shape
    heads, chunk, _ = w_s.shape
    da = norm_v.shape[1]
    hd = da // heads
    db = conv_w.shape[1]
    tm = _pick(s, (512, 256, 128))

    def body(p_ref, ph_ref, nv_ref, ws_ref, b_ref, cw_ref, o_ref):
        first = pl.program_id(0) == 0
        gu, _ = _gelu(p_ref[:, 0:da])
        gv, _ = _gelu(p_ref[:, da : 2 * da])
        xhat, _ = _layernorm_stats(gv)
        vn = (xhat * nv_ref[...]).astype(BF16)
        mask = _tril(chunk)
        for hh in range(heads):
            wm = jnp.where(mask, ws_ref[hh], 0.0).astype(BF16)
            cols = slice(hh * hd, (hh + 1) * hd)
            for nn in range(tm // chunk):
                rows = slice(nn * chunk, (nn + 1) * chunk)
                z = _dot(wm, vn[rows, cols]) + b_ref[:, cols]
                o_ref[rows, cols] = (gu[rows, cols] * z).astype(BF16)
        c_cg, c_xb = 2 * da + db, 2 * da + 2 * db
        bg = p_ref[:, 2 * da : 2 * da + db]
        q = p_ref[:, c_cg : c_cg + db] * p_ref[:, c_xb : c_xb + db]
        q1 = _shift_down(q, 1, ph_ref, c_cg, c_xb, first)
        q2 = _shift_down(q, 2, ph_ref, c_cg, c_xb, first)
        conv = cw_ref[0:1, :] * q2 + cw_ref[1:2, :] * q1 + cw_ref[2:3, :] * q
        o_ref[:, da : da + db] = (bg * conv).astype(BF16)

    nh = tm // CONV_HALO
    return _call(
        body, "ab_mix_fwd", (s // tm,),
        [
            pl.BlockSpec((tm, n), lambda i: (i, 0)),
            pl.BlockSpec((CONV_HALO, n), lambda i: (jnp.maximum(i * nh - 1, 0), 0)),
            pl.BlockSpec((1, da), lambda i: (0, 0)),
            pl.BlockSpec((heads, chunk, chunk), lambda i: (0, 0, 0)),
            pl.BlockSpec((chunk, da), lambda i: (0, 0)),
            pl.BlockSpec((3, db), lambda i: (0, 0)),
        ],
        [pl.BlockSpec((tm, da + db), lambda i: (i, 0))],
        [jax.ShapeDtypeStruct((s, da + db), BF16)], [proj, proj, norm_v, w_s, b_rows, conv_w], phases=phases,
    )


def _ab_mix_bwd(proj, dcat, norm_v, w_s, b_rows, conv_w, phases=()):
    s, n = proj.shape
    heads, chunk, _ = w_s.shape
    da = norm_v.shape[1]
    hd = da // heads
    db = conv_w.shape[1]
    tm = _pick(s, (512, 256, 128))
    nblk = s // tm
    dhalo = 2 * CONV_HALO

    def body(p_ref, pa_ref, pb_ref, dc_ref, dcb_ref, nv_ref, ws_ref, b_ref, cw_ref,
             dp_ref, dnv_ref, dws_ref, dzs_ref, dcw_ref, dvn_sc):
        i = pl.program_id(0)
        first, last = i == 0, i == nblk - 1

        @pl.when(first)
        def _():
            dnv_ref[...] = jnp.zeros_like(dnv_ref)
            dws_ref[...] = jnp.zeros_like(dws_ref)
            dzs_ref[...] = jnp.zeros_like(dzs_ref)
            dcw_ref[...] = jnp.zeros_like(dcw_ref)

        uu = p_ref[:, 0:da]
        gu, gu_grad = _gelu(uu)
        gv, gv_grad = _gelu(p_ref[:, da : 2 * da])
        xhat, rstd = _layernorm_stats(gv)
        nv = nv_ref[...]
        vn = (xhat * nv).astype(BF16)
        dya = dc_ref[:, 0:da].astype(F32)
        dz = (dya * gu).astype(BF16)
        mask = _tril(chunk)
        for hh in range(heads):
            wm = jnp.where(mask, ws_ref[hh], 0.0).astype(BF16)
            cols = slice(hh * hd, (hh + 1) * hd)
            dws = jnp.zeros((chunk, chunk), F32)
            for nn in range(tm // chunk):
                rows = slice(nn * chunk, (nn + 1) * chunk)
                z = _dot(wm, vn[rows, cols]) + b_ref[:, cols]
                dp_ref[rows, cols] = (dya[rows, cols] * z * gu_grad[rows, cols]).astype(BF16)
                dz_blk = dz[rows, cols]
                dws = dws + _dot_nt(dz_blk, vn[rows, cols])
                dzs_ref[:, cols] += dz_blk.astype(F32)
                dvn = _dot_tn(wm, dz_blk)
                dnv_ref[:, cols] += jnp.sum(dvn * xhat[rows, cols], axis=0, keepdims=True)
                dvn_sc[rows, cols] = dvn
            dws_ref[hh] += jnp.where(mask, dws, 0.0)
        dxhat = dvn_sc[...] * nv
        dgv = rstd * (dxhat - jnp.mean(dxhat, axis=-1, keepdims=True) - xhat * jnp.mean(dxhat * xhat, axis=-1, keepdims=True))
        dp_ref[:, da : 2 * da] = (dgv * gv_grad).astype(BF16)

        c_bg, c_cg, c_xb = 2 * da, 2 * da + db, 2 * da + 2 * db
        bg = p_ref[:, c_bg : c_bg + db]
        cg = p_ref[:, c_cg : c_cg + db]
        xb = p_ref[:, c_xb : c_xb + db]
        q = cg * xb
        q1 = _shift_down(q, 1, pa_ref, c_cg, c_xb, first)
        q2 = _shift_down(q, 2, pa_ref, c_cg, c_xb, first)
        dyb = dc_ref[:, da : da + db].astype(F32)
        conv = cw_ref[0:1, :] * q2 + cw_ref[1:2, :] * q1 + cw_ref[2:3, :] * q
        dp_ref[:, c_bg : c_bg + db] = (dyb * conv).astype(BF16)
        e = dyb * bg
        dcw_ref[0:1, :] += jnp.sum(e * q2, axis=0, keepdims=True)
        dcw_ref[1:2, :] += jnp.sum(e * q1, axis=0, keepdims=True)
        dcw_ref[2:3, :] += jnp.sum(e * q, axis=0, keepdims=True)
        rows = lax.broadcasted_iota(jnp.int32, e.shape, 0)
        dq = cw_ref[2:3, :] * e
        for kk in (1, 2):
            ek = pltpu.roll(e, tm - kk, 0)
            for r in range(kk):
                below = dcb_ref[r : r + 1, da : da + db].astype(F32) * pb_ref[r : r + 1, c_bg : c_bg + db]
                below = jnp.where(last, 0.0, below)
                ek = jnp.where(rows == tm - kk + r, below, ek)
            dq = dq + cw_ref[2 - kk : 3 - kk, :] * ek
        dp_ref[:, c_cg : c_cg + db] = (dq * xb).astype(BF16)
        dp_ref[:, c_xb : c_xb + db] = (dq * cg).astype(BF16)

    nh = tm // CONV_HALO
    nhb = tm // dhalo
    const2 = lambda i: (0, 0)
    return _call(
        body, "ab_mix_bwd", (nblk,),
        [
            pl.BlockSpec((tm, n), lambda i: (i, 0)),
            pl.BlockSpec((CONV_HALO, n), lambda i: (jnp.maximum(i * nh - 1, 0), 0)),
            pl.BlockSpec((CONV_HALO, n), lambda i: (jnp.minimum((i + 1) * nh, s // CONV_HALO - 1), 0)),
            pl.BlockSpec((tm, da + db), lambda i: (i, 0)),
            pl.BlockSpec((dhalo, da + db), lambda i: (jnp.minimum((i + 1) * nhb, s // dhalo - 1), 0)),
            pl.BlockSpec((1, da), const2),
            pl.BlockSpec((heads, chunk, chunk), lambda i: (0, 0, 0)),
            pl.BlockSpec((chunk, da), const2),
            pl.BlockSpec((3, db), const2),
        ],
        [
            pl.BlockSpec((tm, n), lambda i: (i, 0)),
            pl.BlockSpec((1, da), const2),
            pl.BlockSpec((heads, chunk, chunk), lambda i: (0, 0, 0)),
            pl.BlockSpec((chunk, da), const2),
            pl.BlockSpec((3, db), const2),
        ],
        [
            jax.ShapeDtypeStruct((s, n), BF16),
            jax.ShapeDtypeStruct((1, da), F32),
            jax.ShapeDtypeStruct((heads, chunk, chunk), F32),
            jax.ShapeDtypeStruct((chunk, da), F32),
            jax.ShapeDtypeStruct((3, db), F32),
        ],
        [proj, proj, proj, dcat, dcat, norm_v, w_s, b_rows, conv_w],
        scratch=[pltpu.VMEM((tm, da), F32)], phases=phases,
    )


def _pool_counts(tm, i, w):
    t = i * tm + lax.broadcasted_iota(jnp.int32, (tm, 1), 0)
    return jnp.minimum(t + 1, w).astype(F32)


def _pool_fwd(x, vec, w_grp, scale, phases=()):
    s, d = x.shape
    groups, gd, _ = w_grp.shape
    tm = _pick(s, (512, 256, 128))

    def body(x_ref, xa_ref, vec_ref, w_ref, sc_ref, xo_ref, p_ref, o_ref):
        i = pl.program_id(0)
        h = _modulate(x_ref[...], vec_ref)
        ha = jnp.where(i == 0, 0.0, _modulate(xa_ref[...], vec_ref))
        ext = jnp.concatenate([ha, h], axis=0)
        for gi, w in enumerate(POOL_WINDOWS):
            cols = slice(gi * gd, (gi + 1) * gd)
            acc = ext[:, cols]
            step = 1
            while step < w:
                acc = acc + pltpu.roll(acc, step, 0)
                step *= 2
            p = (acc[POOL_HALO:, :] / _pool_counts(tm, i, w) - h[:, cols]).astype(BF16)
            p_ref[:, cols] = p
            o_ref[:, cols] = _dot(p, w_ref[gi]).astype(BF16)
        xo_ref[...] = x_ref[...] + vec_ref[3:4, :] * (o_ref[...].astype(F32) * sc_ref[...])

    nh = tm // POOL_HALO
    row = pl.BlockSpec((tm, d), lambda i: (i, 0))
    return _call(
        body, "pool_fwd", (s // tm,),
        [
            row,
            pl.BlockSpec((POOL_HALO, d), lambda i: (jnp.maximum(i * nh - 1, 0), 0)),
            pl.BlockSpec((8, d), lambda i: (0, 0)),
            pl.BlockSpec((groups, gd, gd), lambda i: (0, 0, 0)),
            pl.BlockSpec((1, d), lambda i: (0, 0)),
        ],
        [row, row, row],
        [jax.ShapeDtypeStruct((s, d), F32), jax.ShapeDtypeStruct((s, d), BF16), jax.ShapeDtypeStruct((s, d), BF16)],
        [x, x, vec, w_grp, scale], phases=phases,
    )


def _pool_bwd(dxo, x, vec, p, o, w_grp, scale, phases=()):
    s, d = x.shape
    groups, gd, _ = w_grp.shape
    tm = _pick(s, (512, 256, 128))
    nblk = s // tm

    def body(dxo_ref, dxb_ref, x_ref, vec_ref, p_ref, o_ref, w_ref, sc_ref, dx_ref, dw_ref, dsc_ref, dvec_ref, dw_sc):
        i = pl.program_id(0)

        @pl.when(i == 0)
        def _():
            dw_sc[...] = jnp.zeros_like(dw_sc)
            dsc_ref[...] = jnp.zeros_like(dsc_ref)
            dvec_ref[...] = jnp.zeros_like(dvec_ref)

        gate, sc = vec_ref[3:4, :], sc_ref[...]
        dxo_v = dxo_ref[...]
        ov = o_ref[...].astype(F32)
        dvec_ref[3:4, :] += jnp.sum(dxo_v * (ov * sc), axis=0, keepdims=True)
        dy = gate * dxo_v
        dsc_ref[...] += jnp.sum(dy * ov, axis=0, keepdims=True)
        dout = (dy * sc).astype(BF16)
        dout_b = jnp.where(i == nblk - 1, 0.0, gate * dxb_ref[...] * sc).astype(BF16)
        for gi, w in enumerate(POOL_WINDOWS):
            cols = slice(gi * gd, (gi + 1) * gd)
            dw_sc[gi] += _dot_tn(p_ref[:, cols], dout[:, cols])
            wb = w_ref[gi]
            dp = _dot_nt(dout[:, cols], wb)
            dp_b = _dot_nt(dout_b[:, cols], wb)
            e = dp / _pool_counts(tm, i, w)
            t_below = (i + 1) * tm + lax.broadcasted_iota(jnp.int32, (POOL_HALO, 1), 0)
            e_b = dp_b / jnp.minimum(t_below + 1, w).astype(F32)
            acc = jnp.concatenate([e, e_b], axis=0)
            step = 1
            while step < w:
                acc = acc + pltpu.roll(acc, tm + POOL_HALO - step, 0)
                step *= 2
            dx_ref[:, cols] = acc[:tm, :] - dp
        dx, _ = _modulate_bwd(x_ref[...], dx_ref[...], vec_ref, dvec_ref)
        dx_ref[...] = dxo_v + dx

        @pl.when(i == nblk - 1)
        def _():
            dw_ref[...] = dw_sc[...].astype(BF16)

    nh = tm // POOL_HALO
    row = pl.BlockSpec((tm, d), lambda i: (i, 0))
    vecs = pl.BlockSpec((8, d), lambda i: (0, 0))
    wspec = pl.BlockSpec((groups, gd, gd), lambda i: (0, 0, 0))
    return _call(
        body, "pool_bwd", (nblk,),
        [
            row,
            pl.BlockSpec((POOL_HALO, d), lambda i: (jnp.minimum((i + 1) * nh, s // POOL_HALO - 1), 0)),
            row, vecs, row, row, wspec,
            pl.BlockSpec((1, d), lambda i: (0, 0)),
        ],
        [row, wspec, pl.BlockSpec((1, d), lambda i: (0, 0)), vecs],
        [
            jax.ShapeDtypeStruct((s, d), F32),
            jax.ShapeDtypeStruct((groups, gd, gd), BF16),
            jax.ShapeDtypeStruct((1, d), F32),
            jax.ShapeDtypeStruct((8, d), F32),
        ],
        [dxo, dxo, x, vec, p, o, w_grp, scale],
        scratch=[pltpu.VMEM((groups, gd, gd), F32)], phases=phases,
    )


def _loss_head(x, gain, target, phases=()):
    s, d = x.shape
    tm = _pick(s, (512, 256, 128))

    def body(x_ref, g_ref, t_ref, dx_ref, aux_ref):
        @pl.when(pl.program_id(0) == 0)
        def _():
            aux_ref[...] = jnp.zeros_like(aux_ref)

        xv = x_ref[...]
        rstd = _rstd(xv)
        r = xv * rstd
        gain_v = g_ref[...]
        err = r * gain_v - t_ref[...]
        aux_ref[1:2, :] += jnp.sum(err * err, axis=0, keepdims=True)
        dout = err * (1.0 / d)
        aux_ref[0:1, :] += jnp.sum(dout * r, axis=0, keepdims=True)
        dr = dout * gain_v
        dx_ref[...] = rstd * (dr - r * jnp.mean(dr * r, axis=-1, keepdims=True))

    row = pl.BlockSpec((tm, d), lambda i: (i, 0))
    return _call(
        body, "loss_head", (s // tm,),
        [row, pl.BlockSpec((1, d), lambda i: (0, 0)), row],
        [row, pl.BlockSpec((8, d), lambda i: (0, 0))],
        [jax.ShapeDtypeStruct((s, d), F32), jax.ShapeDtypeStruct((8, d), F32)], [x, gain, target], phases=phases,
    )


def _small_adam(gathered, gathered_ws, layout, smalls, chip):
    names = list(smalls)
    n = len(names)

    def body(*refs):
        chip_ref, g_ref, gws_ref = refs[0], refs[1], refs[2]
        wmv = refs[3 : 3 + 3 * n]
        outs = refs[3 + 3 * n : 3 + 7 * n]
        total = refs[-1]
        total[...] = g_ref[0]
        for kdev in range(1, N_DEV):
            total[...] += g_ref[kdev]
        total_ws = gws_ref[0]
        for kdev in range(1, N_DEV):
            total_ws = total_ws + gws_ref[kdev]
        my_chip = chip_ref[0]
        for a, name in enumerate(names):
            w_ref, m_ref, v_ref = wmv[3 * a : 3 * a + 3]
            if name == "ab_w_s":
                g = total_ws
            else:
                row0, rows, col0, cols = layout[name]
                if col0 is None:
                    g = jnp.zeros((rows, cols), F32)
                    for j in range(N_CHIPS):
                        g = g + jnp.where(my_chip == j, total[row0 : row0 + rows, j * cols : (j + 1) * cols], 0.0)
                else:
                    g = total[row0 : row0 + rows, col0 : col0 + cols]
            dl, mo, vo = _adam(w_ref[...], g, m_ref[...], v_ref[...])
            outs[4 * a][...] = g
            outs[4 * a + 1][...] = dl
            outs[4 * a + 2][...] = mo
            outs[4 * a + 3][...] = vo

    ins = [gathered, gathered_ws]
    out_shapes = []
    for name in names:
        ins.extend(smalls[name])
        out_shapes.extend([jax.ShapeDtypeStruct(smalls[name][0].shape, F32)] * 4)
    whole = lambda shape: pl.BlockSpec(shape, functools.partial(lambda nd, i, c: (0,) * nd, len(shape)))
    res = pl.pallas_call(
        body, name="small_adam",
        grid_spec=pltpu.PrefetchScalarGridSpec(
            num_scalar_prefetch=1, grid=(1,),
            in_specs=[whole(a.shape) for a in ins], out_specs=[whole(o.shape) for o in out_shapes],
            scratch_shapes=[pltpu.VMEM(gathered.shape[1:], F32)],
        ),
        out_shape=out_shapes,
        compiler_params=pltpu.CompilerParams(dimension_semantics=("arbitrary",), vmem_limit_bytes=VMEM_LIMIT_BYTES),
    )(chip.reshape(1).astype(jnp.int32), *ins)
    return {name: res[4 * a : 4 * a + 4] for a, name in enumerate(names)}


def _pad_rows(a, rows=8):
    extra = (-a.shape[0]) % rows
    return jnp.pad(a, ((0, extra), (0, 0))) if extra else a


def _pad_cols(a, cols):
    return jnp.pad(a, ((0, 0), (0, cols - a.shape[1]))) if a.shape[1] < cols else a


def _run(fn, *phases):
    outs, p_outs = fn(list(phases))
    for p, po in zip(phases, p_outs):
        p.then(po)
    return outs


def kernel(x, c, norm_g, w_mod, b_mod, w_ffn_in, w_ffn_out, ab_w_in, ab_norm_v, ab_w_s, ab_b_s, ab_conv_w, ab_w_out, pool_w_grp, pool_scale, final_g, loss_target, m_norm_g, m_w_mod, m_b_mod, m_w_ffn_in, m_w_ffn_out, m_ab_w_in, m_ab_norm_v, m_ab_w_s, m_ab_b_s, m_ab_conv_w, m_ab_w_out, m_pool_w_grp, m_pool_scale, m_final_g, v_norm_g, v_w_mod, v_b_mod, v_w_ffn_in, v_w_ffn_out, v_ab_w_in, v_ab_norm_v, v_ab_w_s, v_ab_b_s, v_ab_conv_w, v_ab_w_out, v_pool_w_grp, v_pool_scale, v_final_g):
    ix, iy, ic = _place()
    chip = 2 * ix + iy
    me = 4 * ix + 2 * iy + ic
    where = jnp.stack([chip, ic]).astype(jnp.int32)
    s, d = x.shape[1], x.shape[2]
    x0 = x.reshape(s, d)
    target = loss_target.reshape(s, d)
    n_layers = norm_g.shape[0]
    dq = d // N_CHIPS
    heads, chunk = ab_w_s.shape[1], ab_w_s.shape[2]
    da = ab_norm_v.shape[1]
    db = ab_conv_w.shape[2] * N_CHIPS
    f_hidden = w_ffn_out.shape[2] * N_CHIPS
    assert n_layers == 2 and da % heads == 0

    cw_pad = _pad_cols(ab_conv_w.reshape(3, db // N_CHIPS), dq)
    packed = jnp.concatenate(
        [_pad_rows(c.reshape(N_CHIPS, dq)), _pad_rows(norm_g.reshape(-1, dq)), _pad_rows(pool_scale.reshape(1, dq)), _pad_rows(cw_pad)],
        axis=0,
    )
    ncol = w_mod.shape[2]
    b_cols = lax.dynamic_slice(b_mod, (0, chip * ncol), (n_layers, ncol)).reshape(n_layers, 1, ncol)
    small = {}

    def small_gather(key, arrs):
        def then(outs):
            small[key] = outs

        return _phase_small_gather(arrs, then)

    stacks = {
        "w_ffn_in": tuple(a.reshape((-1,) + a.shape[2:]) for a in (w_ffn_in, m_w_ffn_in, v_w_ffn_in)),
        "w_ffn_out": tuple(a.reshape((-1,) + a.shape[2:]) for a in (w_ffn_out, m_w_ffn_out, v_w_ffn_out)),
        "ab_w_in": (ab_w_in, m_ab_w_in, v_ab_w_in),
        "ab_w_out": (ab_w_out, m_ab_w_out, v_ab_w_out),
        "pool_w_grp": (pool_w_grp[0], m_pool_w_grp[0], v_pool_w_grp[0]),
    }
    big_in = _Big((1, d, 2 * f_hidden), 2, 1)
    big_out = _Big((1, f_hidden, d), 1, 2)
    units = {}
    for l in range(n_layers):
        for k in range(2):
            units[f"in{l}{k}"] = (big_in, "w_ffn_in", 2 * l + k)
            units[f"out{l}{k}"] = (big_out, "w_ffn_out", 2 * l + k)
    units["abin"] = (_Big((1, d, ab_w_in.shape[2] * N_CHIPS), 2, 1), "ab_w_in", 0)
    units["about"] = (_Big((1, ab_w_out.shape[1] * N_CHIPS, d), 1, 2), "ab_w_out", 0)
    units["pool"] = (_Big((pool_w_grp.shape[1], pool_w_grp.shape[2] * N_CHIPS, pool_w_grp.shape[3]), 1, 0), "pool_w_grp", 0)
    big = {u: g for u, (g, _, _) in units.items()}

    weight = {}
    complete = set()

    def cast(u):
        g, st, b0 = units[u]

        def launch(phases):
            (weight[u],), p_outs = _cast_into_full(stacks[st][0], b0, g, where, "cast_" + u, phases)
            return None, p_outs

        return launch

    def gather_ici(*us):
        def then(outs):
            for u, o in zip(us, outs):
                weight[u] = o

        return _phase_gather_ici([weight[u] for u in us], [big[u] for u in us], then)

    def gather_sibling(*us):
        def then(outs):
            for u, o in zip(us, outs):
                weight[u] = o
                complete.add(u)

        return _phase_gather_sibling([weight[u] for u in us], [big[u] for u in us], then)

    def w_of(u):
        assert u in complete, u
        return weight[u]

    _run(cast("in00"), small_gather("inputs", [packed]))
    _run(cast("out00"))
    small_all = small["inputs"][0]
    by_chip = small_all[0::2]
    c_all = small_all[:, 0:N_CHIPS, :].reshape(N_DEV, d)
    norm_full = by_chip[:, 8 : 8 + 3 * n_layers, :].transpose(1, 0, 2).reshape(3 * n_layers, d)
    pool_scale_full = by_chip[:, 16:17, :].transpose(1, 0, 2).reshape(1, d)
    conv_full = by_chip[:, 24:27, : db // N_CHIPS].transpose(1, 0, 2).reshape(3, db)
    pieces = [("in00", "out00"), ("abin", "about"), ("in01", "out01"), ("in10", "out10", "pool"), ("in11", "out11")]
    in_flight = {}

    def start_gather(p):
        in_flight[p] = _split_start(gather_ici(*pieces[p]), f"gather_{p}_start")

    def finish_gather(p, after):
        _split_wait(in_flight.pop(p), after, f"gather_{p}_wait")
        if p + 2 < len(pieces):
            start_gather(p + 2)
        _split_wait(_split_start(gather_sibling(*pieces[p]), f"gather_{p}_forward"), [], f"gather_{p}_forwarded")

    start_gather(0)
    mod_cols = _run(lambda phases: _mod_fwd(c_all, w_mod, b_cols, phases))[0]
    _run(cast("abin"), small_gather("mod", [mod_cols.reshape(n_layers * N_DEV, ncol)]))
    _run(cast("about"))
    start_gather(1)
    for u in ("in01", "out01", "in10", "out10", "pool", "in11", "out11"):
        _run(cast(u))
    mod_all = small["mod"][0]
    mod_mine = lax.dynamic_index_in_dim(mod_all[0::2].reshape(N_CHIPS, n_layers, N_DEV, ncol), me, axis=2, keepdims=False)
    mod = mod_mine.transpose(1, 0, 2).reshape(n_layers, 3, 3, d)
    vecs = {
        (l, sub): _pad_rows(jnp.concatenate([norm_full[3 * l + sub][None], mod[l, sub]], axis=0))
        for l in range(n_layers)
        for sub in range(3)
    }
    b_rows = jnp.broadcast_to(ab_b_s[0].T[:, :, None], (chunk, heads, da // heads)).reshape(chunk, da)

    saved = {}

    def ffn_forward(xs, l, sub, k, *phases):
        saved[l, sub, "x"] = xs
        xs, gg, uu, yb = _run(
            lambda ph: _ffn_fwd(xs, vecs[l, sub], w_of(f"in{l}{k}"), w_of(f"out{l}{k}"), f"ffn_fwd_{l}{k}", ph), *phases
        )
        saved[l, sub, "act"] = (gg, uu, yb)
        return xs

    finish_gather(0, [vecs[0, 0]])
    xs = ffn_forward(x0, 0, 0, 0)
    saved[0, 1, "x"] = xs
    finish_gather(1, [xs])
    (proj,) = _run(lambda ph: _proj_mod_fwd(xs, vecs[0, 1], w_of("abin"), ph))
    (cat,) = _run(lambda ph: _ab_mix_fwd(proj, ab_norm_v, ab_w_s[0], b_rows, conv_full, ph))
    xs, yb = _run(lambda ph: _proj_res_fwd(cat, w_of("about"), xs, vecs[0, 1], ph))
    saved[0, 1, "act"] = (proj, cat, yb)
    finish_gather(2, [xs])
    xs = ffn_forward(xs, 0, 2, 1)
    finish_gather(3, [xs])
    xs = ffn_forward(xs, 1, 0, 0)
    saved[1, 1, "x"] = xs
    xs, pp, oo = _run(lambda ph: _pool_fwd(xs, vecs[1, 1], w_of("pool"), pool_scale_full, ph))
    saved[1, 1, "act"] = (pp, oo)
    finish_gather(4, [xs])
    xs = ffn_forward(xs, 1, 2, 1)
    dxs, aux = _run(lambda ph: _loss_head(xs, final_g.reshape(1, d), target, ph))
    loss = lax.psum(0.5 * jnp.sum(aux[1]) / d, ("x", "y", "c"))

    grad = {}
    recv = {}
    csum = {}
    parts = {}
    reduced = {}
    done = set()
    dvecs, small_g = {}, {}

    def pair_exchange(*us):
        def then(outs):
            for u, o in zip(us, outs):
                recv[u] = o

        return _phase_pair_exchange([grad[u] for u in us], [big[u] for u in us], then)

    def grad_half(u, a, b, mine, name, *phases, col0=0, prev=None):
        (res,) = _run(lambda ph: _grad_half(a, b, big[u], where, mine, col0, prev, recv[u] if mine else None, name, ph), *phases)
        return res

    def pair_sum(u, *phases):
        def launch(ph):
            (csum[u],), p_outs = _pair_sum(grad[u], recv[u], big[u], where, "pair_sum_" + u, ph)
            return None, p_outs

        _run(launch, *phases)

    def chip_exchange(*us):
        def then(outs):
            for u, o in zip(us, outs):
                parts[u] = o

        return _phase_chip_exchange([csum[u] for u in us], [big[u] for u in us], then)

    def chip_sum(*us, carried=()):
        for n_u, u in enumerate(us):
            g, st, b0 = units[u]

            def launch(ph):
                (reduced[st],), p_outs = _chip_sum(
                    csum[u], parts[u], g, where, reduced.get(st), stacks[st][0].shape, b0, "chip_sum_" + u, ph
                )
                return None, p_outs

            _run(launch, *(carried if n_u == 0 else ()))

    def pair_broadcast(*us):
        sts = [units[u][1] for u in us]
        assert len(set(sts)) == len(sts)

        def then(outs):
            for u, st, o in zip(us, sts, outs):
                reduced[st] = o
                done.add(u)

        return _phase_pair_broadcast([reduced[st] for st in sts], [big[u] for u in us], [units[u][2] for u in us], then)

    def ffn_backward(dxs, l, sub, k, carried_bwd, carried_send, carried_mine):
        gg, uu, yb = saved[l, sub, "act"]
        w_in, w_out = w_of(f"in{l}{k}"), w_of(f"out{l}{k}")
        uo, ui, tag = f"out{l}{k}", f"in{l}{k}", f"{l}{k}"
        dxs, dg, du, a, h, dy, dvecs[l, sub] = _run(
            lambda ph: _ffn_bwd(dxs, saved[l, sub, "x"], vecs[l, sub], gg, uu, yb, w_in, w_out, "ffn_bwd_" + tag, ph), *carried_bwd()
        )
        grad[uo] = grad_half(uo, a, dy, False, "dw_out_send_" + tag, *carried_send())
        part = grad_half(ui, h, du, False, "dw_in_u_send_" + tag, pair_exchange(uo), col0=f_hidden)
        grad[ui] = grad_half(ui, h, dg, False, "dw_in_g_send_" + tag, prev=part)
        csum[uo] = grad_half(uo, a, dy, True, "dw_out_" + tag, pair_exchange(ui))
        part = grad_half(ui, h, du, True, "dw_in_u_" + tag, *carried_mine(), col0=f_hidden)
        csum[ui] = grad_half(ui, h, dg, True, "dw_in_g_" + tag, prev=part)
        return dxs

    none = lambda: ()
    dxs = ffn_backward(dxs, 1, 2, 1, none, none, none)
    pp, oo = saved[1, 1, "act"]
    dxs, grad["pool"], small_g["pool_scale"], dvecs[1, 1] = _run(
        lambda ph: _pool_bwd(dxs, saved[1, 1, "x"], vecs[1, 1], pp, oo, w_of("pool"), pool_scale_full, ph)
    )

    def after_11():
        return (chip_exchange("in11", "out11"), pair_exchange("pool"))

    def bcast_11():
        chip_sum("in11", "out11")
        pair_sum("pool")
        return (pair_broadcast("in11", "out11"), chip_exchange("pool"))

    dxs = ffn_backward(dxs, 1, 0, 0, after_11, bcast_11, none)

    def after_10():
        return (chip_exchange("in10", "out10"),)

    def bcast_10():
        chip_sum("in10", "out10", "pool")
        return (pair_broadcast("in10", "out10", "pool"),)

    dxs = ffn_backward(dxs, 0, 2, 1, after_10, bcast_10, none)

    proj, cat, yb = saved[0, 1, "act"]
    dy, dcat, dgate = _run(lambda ph: _proj_res_bwd(dxs, yb, vecs[0, 1], w_of("about"), ph))
    grad["about"] = grad_half("about", cat, dy, False, "dw_ab_out_send")
    dproj, small_g["ab_norm_v"], small_g["ab_w_s"], dzs, small_g["ab_conv_w"] = _run(
        lambda ph: _ab_mix_bwd(proj, dcat, ab_norm_v, ab_w_s[0], b_rows, conv_full, ph), chip_exchange("out01"), pair_exchange("about")
    )
    small_g["ab_b_s"] = dzs.reshape(chunk, heads, da // heads).sum(axis=2).T
    dxs, h, dvecs[0, 1] = _run(
        lambda ph: _proj_mod_bwd(dproj[None], w_of("abin"), saved[0, 1, "x"], vecs[0, 1], dxs, dgate, "ab_in_bwd", ph)
    )
    grad["abin"] = grad_half("abin", h, dproj, False, "dw_ab_in_send")
    chip_sum("out01", carried=(pair_exchange("abin"),))
    csum["about"] = grad_half("about", cat, dy, True, "dw_ab_out", pair_broadcast("out01"))
    csum["abin"] = grad_half("abin", h, dproj, True, "dw_ab_in")

    def after_01():
        return (chip_exchange("in01", "abin", "about"),)

    def bcast_01():
        chip_sum("in01", "abin", "about")
        return (pair_broadcast("in01", "abin", "about"),)

    def reduce_out00():
        return (chip_exchange("out00"),)

    dxs = ffn_backward(dxs, 0, 0, 0, after_01, bcast_01, reduce_out00)
    grad_x = dxs.reshape(x.shape)

    dgain = jnp.stack([dvecs[l, sub][0] for l in range(n_layers) for sub in range(3)])
    dmod = jnp.concatenate([dvecs[l, sub][1:4] for l in range(n_layers) for sub in range(3)], axis=0)
    pieces = {
        "norm_g": (dgain, None, dq), "final_g": (aux[0:1], 0, d), "pool_scale": (small_g["pool_scale"], None, dq),
        "b_mod": (dmod, 0, d), "ab_norm_v": (small_g["ab_norm_v"], 0, da), "ab_conv_w": (small_g["ab_conv_w"], None, db // N_CHIPS),
        "ab_b_s": (small_g["ab_b_s"], 0, chunk),
    }
    layout, row0 = {}, 0
    for nm, (pc, col0, cols) in pieces.items():
        layout[nm] = (row0, pc.shape[0], col0, cols)
        row0 += pc.shape[0]
    packed_rows = -(-row0 // 8) * 8
    packed_g = sum(
        jnp.pad(pc, ((layout[nm][0], packed_rows - layout[nm][0] - pc.shape[0]), (0, d - pc.shape[1])))
        for nm, (pc, _, _) in pieces.items()
    )

    last = _split_start(chip_exchange("in00"), "reduce_last_start")
    started = last.token
    chip_sum("out00")
    _flush(
        "broadcast_out00", pair_broadcast("out00"),
        small_gather("grads", [packed_g, small_g["ab_w_s"].reshape(heads * chunk, chunk)]),
    )
    g_all, gws_all = small["grads"]

    out = {}

    def adam_stack(st, after=()):
        w3, m3, v3 = stacks[st]
        assert all(u in done for u, (_, ust, _) in units.items() if ust == st), st
        shape = {"w_ffn_in": w_ffn_in.shape, "w_ffn_out": w_ffn_out.shape, "pool_w_grp": pool_w_grp.shape}.get(st, w3.shape)
        out[st] = tuple(a.reshape(shape) for a in _adam_stack(w3, reduced[st], m3, v3, "adam_" + st, after))

    for st in ("w_ffn_out", "ab_w_in", "ab_w_out", "pool_w_grp"):
        adam_stack(st, (started,))

    shapes2d = {
        "norm_g": (3 * n_layers, dq), "b_mod": (9 * n_layers, d), "final_g": (1, d), "ab_norm_v": (1, da),
        "pool_scale": (1, dq), "ab_conv_w": (3, db // N_CHIPS), "ab_b_s": (heads, chunk), "ab_w_s": (heads * chunk, chunk),
    }
    small_w = {"norm_g": (norm_g, m_norm_g, v_norm_g), "b_mod": (b_mod, m_b_mod, v_b_mod), "final_g": (final_g, m_final_g, v_final_g),
               "ab_norm_v": (ab_norm_v, m_ab_norm_v, v_ab_norm_v), "pool_scale": (pool_scale, m_pool_scale, v_pool_scale),
               "ab_conv_w": (ab_conv_w, m_ab_conv_w, v_ab_conv_w), "ab_b_s": (ab_b_s, m_ab_b_s, v_ab_b_s), "ab_w_s": (ab_w_s, m_ab_w_s, v_ab_w_s)}
    smalls = {nm: tuple(a.reshape(shapes2d[nm]) for a in wmv) for nm, wmv in small_w.items()}
    small_out = _small_adam(g_all, gws_all, layout, smalls, chip)
    for nm, res in small_out.items():
        out[nm] = tuple(a.reshape(small_w[nm][0].shape) for a in res)

    mod_row0 = layout["b_mod"][0]
    dmod_all = g_all[:, mod_row0 : mod_row0 + 9 * n_layers, :].reshape(N_DEV, n_layers, 9 * d)
    dmod_cols = lax.dynamic_slice(dmod_all, (0, 0, chip * ncol), (N_DEV, n_layers, ncol)).transpose(1, 0, 2)
    out["w_mod"] = tuple(_mod_bwd_adam(c_all.T, dmod_cols, w_mod, m_w_mod, v_w_mod, (started,)))

    (csum["in00"],) = _split_wait(
        last, [out[st][1] for st in ("w_mod", "w_ffn_out", "ab_w_in", "ab_w_out", "pool_w_grp")], "reduce_last_wait"
    )
    chip_sum("in00")
    _flush("broadcast_last", pair_broadcast("in00"))
    adam_stack("w_ffn_in")

    order = ["norm_g", "w_mod", "b_mod", "w_ffn_in", "w_ffn_out", "ab_w_in", "ab_norm_v", "ab_w_s", "ab_b_s", "ab_conv_w", "ab_w_out", "pool_w_grp", "pool_scale", "final_g"]
    return (loss, grad_x, *[out[nm][0] for nm in order], *[out[nm][1] for nm in order], *[out[nm][2] for nm in order], *[out[nm][3] for nm in order])
```

```python
import functools
import math

import jax
import jax.numpy as jnp
from jax import lax
from jax.experimental import pallas as pl
from jax.experimental.pallas import tpu as pltpu

F32 = jnp.float32
BF16 = jnp.bfloat16
MESH = pl.DeviceIdType.MESH

EPS = 1e-6
ADAM_LR = 0.001
ADAM_B1 = 0.9
ADAM_B2 = 0.999
ADAM_EPS = 1e-08
ADAM_WD = 0.01
ADAM_STEP = 10
POOL_WINDOWS = (2, 4, 8, 16)
POOL_HALO = 16
CONV_HALO = 8
N_CHIPS = 4
N_DEV = 8
VMEM_LIMIT_BYTES = 48 * 1024 * 1024
EW_BLOCK_ELEMS = 256 * 1024


def _pick(n, prefs):
    for p in prefs:
        if p <= n and n % p == 0:
            return p
    return n


def _row_tile(rows, cols):
    best = None
    for d in range(16, rows + 1, 16):
        if rows % d == 0 and d * cols <= EW_BLOCK_ELEMS:
            best = d
    return best or rows


def _dot(a, b):
    return jnp.dot(a, b, preferred_element_type=F32)


def _dot_nt(a, b):
    return lax.dot_general(a, b, (((1,), (1,)), ((), ())), preferred_element_type=F32)


def _dot_tn(a, b):
    return lax.dot_general(a, b, (((0,), (0,)), ((), ())), preferred_element_type=F32)


def _sigmoid(x):
    return 0.5 * jnp.tanh(0.5 * x) + 0.5


_GELU_C = math.sqrt(2.0 / math.pi)


def _gelu(x):
    x2 = x * x
    t = jnp.tanh(_GELU_C * (x + 0.044715 * x2 * x))
    val = 0.5 * x * (1.0 + t)
    grad = 0.5 * (1.0 + t) + 0.5 * x * (1.0 - t * t) * (_GELU_C * (1.0 + 3.0 * 0.044715 * x2))
    return val, grad


def _rstd(x):
    return lax.rsqrt(jnp.mean(x * x, axis=-1, keepdims=True) + EPS)


def _modulate(x, vec_ref):
    return (x * _rstd(x)) * vec_ref[0:1, :] * (1.0 + vec_ref[2:3, :]) + vec_ref[1:2, :]


def _modulate_bwd(x, dh, vec_ref, dvec_ref):
    gn, sh, sc = vec_ref[0:1, :], vec_ref[1:2, :], vec_ref[2:3, :]
    rstd = _rstd(x)
    r = x * rstd
    dvec_ref[0:1, :] += jnp.sum(dh * r * (1.0 + sc), axis=0, keepdims=True)
    dvec_ref[1:2, :] += jnp.sum(dh, axis=0, keepdims=True)
    dvec_ref[2:3, :] += jnp.sum(dh * r * gn, axis=0, keepdims=True)
    gm = gn * (1.0 + sc)
    dr = dh * gm
    dx = rstd * (dr - r * jnp.mean(dr * r, axis=-1, keepdims=True))
    return dx, r * gm + sh


def _adam(w, g, m, v):
    m = ADAM_B1 * m + (1.0 - ADAM_B1) * g
    v = ADAM_B2 * v + (1.0 - ADAM_B2) * (g * g)
    m_hat = m / (1.0 - ADAM_B1**ADAM_STEP)
    v_hat = v / (1.0 - ADAM_B2**ADAM_STEP)
    delta = -ADAM_LR * (m_hat / (jnp.sqrt(v_hat) + ADAM_EPS) + ADAM_WD * w)
    return delta, m, v


_ANY = pl.BlockSpec(memory_space=pl.ANY)


class _Phase:
    def __init__(self, ins, out_shapes, aliases, n_sems, start, finish, then):
        self.ins, self.out_shapes, self.aliases, self.n_sems = list(ins), list(out_shapes), dict(aliases), n_sems
        self.start, self.finish, self.then = start, finish, then


def _call(body, name, grid, in_specs, out_specs, out_shape, ins, scratch=(), prefetch=(), phases=(), in_place=None):
    n_pre, n_in, n_out, n_sc = len(prefetch), len(in_specs), len(out_specs), len(scratch)
    ph_in = [len(p.ins) for p in phases]
    ph_out = [len(p.out_shapes) for p in phases]

    def kernel_body(*refs):
        pos = [0]

        def take(k):
            pos[0] += k
            return refs[pos[0] - k : pos[0]]

        pre, ins_ = take(n_pre), take(n_in)
        p_ins = [take(k) for k in ph_in]
        outs_ = take(n_out)
        p_outs = [take(k) for k in ph_out]
        sc = take(n_sc)
        sems = [take(2) for _ in phases]
        if phases:
            ids = [pl.program_id(a) for a in range(len(grid))]
            first = functools.reduce(jnp.logical_and, [i == 0 for i in ids])
            last = functools.reduce(jnp.logical_and, [i == g - 1 for i, g in zip(ids, grid)])

            @pl.when(first)
            def _():
                for p, pi, po, (send, recv) in zip(phases, p_ins, p_outs, sems):
                    p.start(pi, po, send, recv)

        if body is not None:
            body(*pre, *ins_, *outs_, *sc)
        if phases:

            @pl.when(last)
            def _():
                for p, pi, po, (send, recv) in zip(phases, p_ins, p_outs, sems):
                    p.finish(pi, po, send, recv)

    aliases = {n_pre + i: o for i, o in (in_place or {}).items()}
    i0, o0 = n_pre + n_in, n_out
    for p in phases:
        for i, o in p.aliases.items():
            aliases[i0 + i] = o0 + o
        i0 += len(p.ins)
        o0 += len(p.out_shapes)
    all_in = list(in_specs) + [_ANY] * sum(ph_in)
    all_out = list(out_specs) + [_ANY] * sum(ph_out)
    all_scratch = list(scratch)
    for p in phases:
        all_scratch += [pltpu.SemaphoreType.DMA((p.n_sems,)), pltpu.SemaphoreType.DMA((p.n_sems,))]
    shapes = list(out_shape) + [s for p in phases for s in p.out_shapes]
    operands = list(prefetch) + list(ins) + [a for p in phases for a in p.ins]
    sem = ("arbitrary",) * len(grid)
    params = pltpu.CompilerParams(dimension_semantics=sem, vmem_limit_bytes=VMEM_LIMIT_BYTES)
    if n_pre:
        res = pl.pallas_call(
            kernel_body, name=name, out_shape=shapes, input_output_aliases=aliases, compiler_params=params,
            grid_spec=pltpu.PrefetchScalarGridSpec(
                num_scalar_prefetch=n_pre, grid=grid, in_specs=all_in, out_specs=all_out, scratch_shapes=all_scratch
            ),
        )(*operands)
    else:
        res = pl.pallas_call(
            kernel_body, name=name, grid=grid, in_specs=all_in, out_specs=all_out, out_shape=shapes,
            scratch_shapes=all_scratch, input_output_aliases=aliases, compiler_params=params,
        )(*operands)
    res = list(res)
    outs, rest = res[:n_out], res[n_out:]
    p_res = []
    for k in ph_out:
        p_res.append(rest[:k])
        rest = rest[k:]
    return outs, p_res


def _place():
    return lax.axis_index("x"), lax.axis_index("y"), lax.axis_index("c")


def _other_chips():
    x, y, _ = _place()
    return [(1 - x, y), (x, 1 - y), (1 - x, 1 - y)]


def _flip(k):
    x, y, c = _place()
    return (1 - x if k & 4 else x, 1 - y if k & 2 else y, 1 - c if k & 1 else c)


def _remote(src, dst, send, recv, k, to):
    return pltpu.make_async_remote_copy(
        src_ref=src, dst_ref=dst, send_sem=send.at[k], recv_sem=recv.at[k], device_id=to, device_id_type=MESH
    )


def _phase_small_gather(arrs, then):
    n = len(arrs)

    def copies(ins, outs, send, recv):
        x, y, c = _place()
        me = 4 * x + 2 * y + c
        local = [pltpu.make_async_copy(ins[a], outs[a].at[me], send.at[a * N_DEV]) for a in range(n)]
        remote = [_remote(ins[a], outs[a].at[me], send, recv, a * N_DEV + k, _flip(k)) for a in range(n) for k in range(1, N_DEV)]
        return local, remote

    def start(ins, outs, send, recv):
        local, remote = copies(ins, outs, send, recv)
        for cp in local + remote:
            cp.start()

    def finish(ins, outs, send, recv):
        local, remote = copies(ins, outs, send, recv)
        for cp in remote + local:
            cp.wait()

    shapes = [jax.ShapeDtypeStruct((N_DEV,) + a.shape, a.dtype) for a in arrs]
    return _Phase(arrs, shapes, {}, n * N_DEV, start, finish, then)


def _after(*arrs):
    nothing = lambda *args: None
    return _Phase(arrs, [], {}, 1, nothing, nothing, nothing)


def _flush(name, *phases):
    _, p_outs = _call(None, name, (1,), [], [], [], [], phases=list(phases))
    for p, po in zip(phases, p_outs):
        p.then(po)


class _Big:
    KINDS = {"full": (True, True), "half": (True, False), "shard": (False, True), "block": (False, False)}

    def __init__(self, f3, s3, h3):
        assert s3 != h3
        self.f3, self.s3, self.h3 = tuple(f3), s3, h3
        self.bd = tuple(f3[a] // (N_CHIPS if a == s3 else 1) // (2 if a == h3 else 1) for a in range(3))
        self.tile = (1, _row_tile(self.bd[1], self.bd[2]), self.bd[2])
        self.grid = tuple(self.bd[a] // self.tile[a] for a in range(3))

    def dims(self, kind):
        chips, halves = self.KINDS[kind]
        return tuple(
            self.bd[a] * (N_CHIPS if chips and a == self.s3 else 1) * (2 if halves and a == self.h3 else 1) for a in range(3)
        )

    def view(self, ref, chip=None, half=None, batch0=0, both_halves=True):
        start = [batch0, 0, 0]
        size = list(ref.shape)
        size[0] = self.bd[0] * (2 if self.h3 == 0 and both_halves else 1)
        if chip is not None:
            start[self.s3] += chip * self.bd[self.s3]
            size[self.s3] = self.bd[self.s3]
        if half is not None:
            start[self.h3] += half * self.bd[self.h3]
            size[self.h3] = self.bd[self.h3]
        return ref.at[tuple(pl.ds(st, sz) for st, sz in zip(start, size))]

    def spec(self, chip_from=None, half_from=None, lead=(), batch0=0):
        extra = "grid" in (chip_from, half_from)

        def index(*args):
            pref, idx = args[-1], list(args[int(extra) : -1])
            idx[0] += batch0
            if chip_from:
                idx[self.s3] += (pref[0] if chip_from == "pref" else args[0]) * self.grid[self.s3]
            if half_from:
                idx[self.h3] += (pref[1] if half_from == "pref" else args[0]) * self.grid[self.h3]
            return (0,) * len(lead) + tuple(idx)

        return pl.BlockSpec(tuple(lead) + self.tile, index)


def _same(arrs):
    return [jax.ShapeDtypeStruct(a.shape, a.dtype) for a in arrs]


def _phase_gather_ici(arrs, bigs, then):
    n = len(arrs)

    def copies(outs, send, recv, arriving):
        x, y, c = _place()
        return [
            _remote(blk, blk, send, recv, 3 * a + j, (*chip, c))
            for j, chip in enumerate(_other_chips())
            for a in range(n)
            for blk in [bigs[a].view(outs[a], 2 * chip[0] + chip[1] if arriving else 2 * x + y, c)]
        ]

    def start(ins, outs, send, recv):
        for cp in copies(outs, send, recv, False):
            cp.start()

    def finish(ins, outs, send, recv):
        for cp in copies(outs, send, recv, True):
            cp.wait_recv()
        for cp in copies(outs, send, recv, False):
            cp.wait_send()

    return _Phase(arrs, _same(arrs), {a: a for a in range(n)}, 3 * n, start, finish, then)


def _phase_gather_sibling(arrs, bigs, then):
    n = len(arrs)

    def copies(outs, send, recv, arriving):
        x, y, c = _place()
        return [
            _remote(blk, blk, send, recv, 3 * a + j, (x, y, 1 - c))
            for j, chip in enumerate(_other_chips())
            for a in range(n)
            for blk in [bigs[a].view(outs[a], 2 * chip[0] + chip[1], 1 - c if arriving else c)]
        ]

    def start(ins, outs, send, recv):
        for cp in copies(outs, send, recv, False):
            cp.start()

    def finish(ins, outs, send, recv):
        for cp in copies(outs, send, recv, True):
            cp.wait_recv()
        for cp in copies(outs, send, recv, False):
            cp.wait_send()

    return _Phase(arrs, _same(arrs), {a: a for a in range(n)}, 3 * n, start, finish, then)


def _phase_pair_exchange(grads, bigs, then):
    n = len(grads)

    def copies(ins, outs, send, recv):
        x, y, c = _place()
        srcs = [ins[a] if ins[a].shape == outs[a].shape else bigs[a].view(ins[a], None, 1 - c) for a in range(n)]
        return [_remote(srcs[a], outs[a], send, recv, a, (x, y, 1 - c)) for a in range(n)]

    def start(ins, outs, send, recv):
        for cp in copies(ins, outs, send, recv):
            cp.start()

    def finish(ins, outs, send, recv):
        for cp in copies(ins, outs, send, recv):
            cp.wait()

    shapes = [jax.ShapeDtypeStruct(b.dims("half"), BF16) for b in bigs]
    return _Phase(grads, shapes, {}, n, start, finish, then)


def _phase_chip_exchange(sums, bigs, then):
    n = len(sums)

    def copies(ins, outs, send, recv):
        _, _, c = _place()
        return [
            _remote(bigs[a].view(ins[a], 2 * chip[0] + chip[1], both_halves=False), outs[a].at[j], send, recv, 3 * a + j, (*chip, c))
            for j, chip in enumerate(_other_chips())
            for a in range(n)
        ]

    def start(ins, outs, send, recv):
        for cp in copies(ins, outs, send, recv):
            cp.start()

    def finish(ins, outs, send, recv):
        for cp in copies(ins, outs, send, recv):
            cp.wait()

    shapes = [jax.ShapeDtypeStruct((N_CHIPS - 1,) + b.dims("block"), BF16) for b in bigs]
    return _Phase(sums, shapes, {}, 3 * n, start, finish, then)


_HBM = pl.BlockSpec(memory_space=pltpu.HBM)
_SEM = pl.BlockSpec(memory_space=pltpu.SEMAPHORE)
_DATAFLOW = pltpu.SideEffectType.DATAFLOW_SIDE_EFFECTING


class _InFlight:
    def __init__(self, phase, send, recv, arrays, token):
        self.phase, self.send, self.recv, self.arrays, self.token = phase, send, recv, arrays, token


def _phase_results(phase, refs):
    n_in = len(phase.ins)
    updated = {o: i for i, o in phase.aliases.items()}
    fresh = [o for o in range(len(phase.out_shapes)) if o not in updated]
    return [refs[updated[o]] if o in updated else refs[n_in + fresh.index(o)] for o in range(len(phase.out_shapes))]


def _split_start(phase, name):
    n_in = len(phase.ins)
    fresh = [s for o, s in enumerate(phase.out_shapes) if o not in phase.aliases.values()]
    arrays = list(phase.ins) + [lax.empty(s.shape, s.dtype) for s in fresh]
    n = len(arrays)

    def body(*refs):
        phase.start(refs[:n_in], _phase_results(phase, refs[:n]), refs[n], refs[n + 1])
        refs[-1][...] = jnp.zeros_like(refs[-1])

    operands = [pltpu.with_memory_space_constraint(a, pltpu.HBM) for a in arrays]
    res = pl.pallas_call(
        body, name=name,
        out_shape=[pltpu.SemaphoreType.DMA((phase.n_sems,)), pltpu.SemaphoreType.DMA((phase.n_sems,))]
        + [pltpu.HBM(a.shape, a.dtype) for a in arrays] + [jax.ShapeDtypeStruct((8, 128), F32)],
        in_specs=[_HBM] * n, out_specs=[_SEM, _SEM] + [_HBM] * n + [pl.BlockSpec(memory_space=pltpu.VMEM)],
        input_output_aliases={i: 2 + i for i in range(n)},
        compiler_params=pltpu.CompilerParams(has_side_effects=_DATAFLOW),
    )(*operands)
    return _InFlight(phase, res[0], res[1], list(res[2 : 2 + n]), res[-1])


def _split_wait(flight, after, name):
    phase, n = flight.phase, len(flight.arrays)
    n_in = len(phase.ins)

    def body(*refs):
        phase.finish(refs[:n_in], _phase_results(phase, refs[:n]), refs[n], refs[n + 1])

    res = pl.pallas_call(
        body, name=name, out_shape=[pltpu.HBM(a.shape, a.dtype) for a in flight.arrays],
        in_specs=[_HBM] * n + [_SEM, _SEM] + [_ANY] * len(after), out_specs=[_HBM] * n,
        input_output_aliases={i: i for i in range(n)},
        compiler_params=pltpu.CompilerParams(has_side_effects=_DATAFLOW),
    )(*flight.arrays, flight.send, flight.recv, *after)
    res = list(res)
    phase.then(_phase_results(phase, res))
    return res[:n_in]


def _phase_pair_broadcast(stacks, bigs, batch0s, then):
    n = len(stacks)

    def start(ins, outs, send, recv):
        x, y, c = _place()
        for a in range(n):
            blk = bigs[a].view(outs[a], None, c, batch0s[a])
            _remote(blk, blk, send, recv, a, (x, y, 1 - c)).start()

    def finish(ins, outs, send, recv):
        x, y, c = _place()
        for a in range(n):
            mine = bigs[a].view(outs[a], None, c, batch0s[a])
            theirs = bigs[a].view(outs[a], None, 1 - c, batch0s[a])
            _remote(mine, mine, send, recv, a, (x, y, 1 - c)).wait_send()
            _remote(theirs, theirs, send, recv, a, (x, y, 1 - c)).wait_recv()

    return _Phase(stacks, _same(stacks), {a: a for a in range(n)}, n, start, finish, then)


def _tile_call(body, name, big, where, extra, ins, in_specs, out_specs, out_shape, phases=()):
    grid = ((extra,) if extra else ()) + big.grid
    return _call(body, name, grid, in_specs, out_specs, out_shape, ins, prefetch=(where,), phases=phases)


def _cast_into_full(w_stack, batch0, big, where, name, phases=()):
    def body(_, w_ref, o_ref):
        o_ref[...] = w_ref[...].astype(BF16)

    return _tile_call(
        body, name, big, where, 2, [w_stack], [big.spec(None, "grid", batch0=batch0)], [big.spec("pref", "grid")],
        [jax.ShapeDtypeStruct(big.dims("full"), BF16)], phases,
    )


def _pair_sum(g_full, recv_half, big, where, name, phases=()):
    def body(_, g_ref, r_ref, o_ref):
        o_ref[...] = (g_ref[...].astype(F32) + r_ref[...].astype(F32)).astype(BF16)

    half = big.spec("grid", None)
    return _tile_call(
        body, name, big, where, N_CHIPS, [g_full, recv_half], [big.spec("grid", "pref"), half], [half],
        [jax.ShapeDtypeStruct(big.dims("half"), BF16)], phases,
    )


def _chip_sum(chip_sum, parts, big, where, stack, stack_shape, batch0, name, phases=()):
    def body(_, own_ref, p_ref, *rest):
        acc = own_ref[...].astype(F32)
        for k in range(N_CHIPS - 1):
            acc = acc + p_ref[k].astype(F32)
        rest[-1][...] = acc

    ins = [chip_sum, parts] + ([stack] if stack is not None else [])
    in_specs = [big.spec("pref", None), big.spec(None, None, lead=(N_CHIPS - 1,))] + ([_ANY] if stack is not None else [])
    return _call(
        body, name, big.grid, in_specs, [big.spec(None, "pref", batch0=batch0)], [jax.ShapeDtypeStruct(stack_shape, F32)], ins,
        prefetch=(where,), phases=phases, in_place={2: 0} if stack is not None else None,
    )


def _adam_stack(w, g, m, v, name, after=()):
    b, r, c = w.shape
    tr = _row_tile(r, c)

    def body(w_ref, g_ref, m_ref, v_ref, *rest):
        go_ref, d_ref, mo_ref, vo_ref = rest[-4:]
        gv = g_ref[...]
        d, mo, vo = _adam(w_ref[...], gv, m_ref[...], v_ref[...])
        go_ref[...] = gv
        d_ref[...] = d
        mo_ref[...] = mo
        vo_ref[...] = vo

    spec = pl.BlockSpec((1, tr, c), lambda bb, i: (bb, i, 0))
    outs, _ = _call(
        body, name, (b, r // tr), [spec] * 4 + [_ANY] * len(after), [spec] * 4, [jax.ShapeDtypeStruct(w.shape, F32)] * 4,
        [w, g, m, v, *after],
    )
    return outs


def _mod_fwd(c_all, w_mod, b_cols, phases=()):
    n_layers, d, n = w_mod.shape
    tn = _pick(n, (768, 512, 384, 256, 128))

    def body(c_ref, w_ref, b_ref, o_ref):
        cv = c_ref[...]
        ca = (cv * _sigmoid(cv)).astype(BF16)
        o_ref[0] = _dot(ca, w_ref[0].astype(BF16)) + b_ref[0]

    return _call(
        body, "mod_fwd", (n_layers, n // tn),
        [
            pl.BlockSpec((N_DEV, d), lambda l, j: (0, 0)),
            pl.BlockSpec((1, d, tn), lambda l, j: (l, 0, j)),
            pl.BlockSpec((1, 1, tn), lambda l, j: (l, 0, j)),
        ],
        [pl.BlockSpec((1, N_DEV, tn), lambda l, j: (l, 0, j))],
        [jax.ShapeDtypeStruct((n_layers, N_DEV, n), F32)], [c_all, w_mod, b_cols], phases=phases,
    )


def _mod_bwd_adam(c_all_t, dmod_cols, w, m, v, after=()):
    n_layers, d, n = w.shape
    tn = _pick(n, (384, 256, 128))

    def body(c_ref, dm_ref, w_ref, m_ref, v_ref, *rest):
        g_ref, d_ref, mo_ref, vo_ref = rest[-4:]
        cv = c_ref[...]
        ca = (cv * _sigmoid(cv)).astype(BF16)
        g = _dot(ca, dm_ref[0].astype(BF16))
        g_ref[0] = g
        dl, mo, vo = _adam(w_ref[0], g, m_ref[0], v_ref[0])
        d_ref[0] = dl
        mo_ref[0] = mo
        vo_ref[0] = vo

    wspec = pl.BlockSpec((1, d, tn), lambda l, j: (l, 0, j))
    outs, _ = _call(
        body, "mod_bwd_adam", (n_layers, n // tn),
        [pl.BlockSpec((d, N_DEV), lambda l, j: (0, 0)), pl.BlockSpec((1, N_DEV, tn), lambda l, j: (l, 0, j)), wspec, wspec, wspec]
        + [_ANY] * len(after),
        [wspec] * 4, [jax.ShapeDtypeStruct(w.shape, F32)] * 4, [c_all_t, dmod_cols, w, m, v, *after],
    )
    return outs


def _ffn_fwd(x, vec, w_in, w_out, name, phases=()):
    s, d = x.shape
    f = w_out.shape[1]
    tm = _pick(s, (1024, 512, 256, 128))
    tf = _pick(f, (256, 128))
    nf = f // tf

    def body(x_ref, vec_ref, wg_ref, wu_ref, wo_ref, xo_ref, g_ref, u_ref, y_ref, h_sc, acc_sc):
        j = pl.program_id(1)

        @pl.when(j == 0)
        def _():
            h_sc[...] = _modulate(x_ref[...], vec_ref).astype(BF16)
            acc_sc[...] = jnp.zeros_like(acc_sc)

        h = h_sc[...]
        g = _dot(h, wg_ref[0])
        u = _dot(h, wu_ref[0])
        g_ref[...] = g.astype(BF16)
        u_ref[...] = u.astype(BF16)
        a = (g * _sigmoid(g) * u).astype(BF16)
        acc_sc[...] += _dot(a, wo_ref[0])

        @pl.when(j == nf - 1)
        def _():
            yv = acc_sc[...]
            xo_ref[...] = x_ref[...] + 0.5 * vec_ref[3:4, :] * yv
            y_ref[...] = yv.astype(BF16)

    row = pl.BlockSpec((tm, d), lambda i, j: (i, 0))
    hid = pl.BlockSpec((tm, tf), lambda i, j: (i, j))
    return _call(
        body, name, (s // tm, nf),
        [
            row,
            pl.BlockSpec((8, d), lambda i, j: (0, 0)),
            pl.BlockSpec((1, d, tf), lambda i, j: (0, 0, j)),
            pl.BlockSpec((1, d, tf), lambda i, j: (0, 0, nf + j)),
            pl.BlockSpec((1, tf, d), lambda i, j: (0, j, 0)),
        ],
        [row, hid, hid, row],
        [
            jax.ShapeDtypeStruct((s, d), F32),
            jax.ShapeDtypeStruct((s, f), BF16),
            jax.ShapeDtypeStruct((s, f), BF16),
            jax.ShapeDtypeStruct((s, d), BF16),
        ],
        [x, vec, w_in, w_in, w_out],
        scratch=[pltpu.VMEM((tm, d), BF16), pltpu.VMEM((tm, d), F32)], phases=phases,
    )


def _ffn_bwd(dxo, x, vec, gg, uu, y, w_in, w_out, name, phases=()):
    s, d = x.shape
    f = w_out.shape[1]
    tm = _pick(s, (512, 256, 128))
    tf = _pick(f, (256, 128))
    nf = f // tf

    def body(dxo_ref, x_ref, vec_ref, g_ref, u_ref, y_ref, wg_ref, wu_ref, wo_ref,
             dx_ref, dg_ref, du_ref, a_ref, h_ref, dy_ref, dvec_ref, acc_sc):
        i, j = pl.program_id(0), pl.program_id(1)

        @pl.when((i == 0) & (j == 0))
        def _():
            dvec_ref[...] = jnp.zeros_like(dvec_ref)

        @pl.when(j == 0)
        def _():
            dxo_v = dxo_ref[...]
            dy_ref[...] = (0.5 * vec_ref[3:4, :] * dxo_v).astype(BF16)
            dvec_ref[3:4, :] += 0.5 * jnp.sum(dxo_v * y_ref[...].astype(F32), axis=0, keepdims=True)
            acc_sc[...] = jnp.zeros_like(acc_sc)

        da = _dot_nt(dy_ref[...], wo_ref[0])
        g = g_ref[...].astype(F32)
        u = u_ref[...].astype(F32)
        sig = _sigmoid(g)
        sl = g * sig
        a_ref[...] = (sl * u).astype(BF16)
        dg = (da * u * (sig * (1.0 + g * (1.0 - sig)))).astype(BF16)
        du = (da * sl).astype(BF16)
        dg_ref[...] = dg
        du_ref[...] = du
        acc_sc[...] += _dot_nt(dg, wg_ref[0]) + _dot_nt(du, wu_ref[0])

        @pl.when(j == nf - 1)
        def _():
            dx, h = _modulate_bwd(x_ref[...], acc_sc[...], vec_ref, dvec_ref)
            dx_ref[...] = dxo_ref[...] + dx
            h_ref[...] = h.astype(BF16)

    row = pl.BlockSpec((tm, d), lambda i, j: (i, 0))
    hid = pl.BlockSpec((tm, tf), lambda i, j: (i, j))
    vecs = pl.BlockSpec((8, d), lambda i, j: (0, 0))
    return _call(
        body, name, (s // tm, nf),
        [
            row, row, vecs, hid, hid, row,
            pl.BlockSpec((1, d, tf), lambda i, j: (0, 0, j)),
            pl.BlockSpec((1, d, tf), lambda i, j: (0, 0, nf + j)),
            pl.BlockSpec((1, tf, d), lambda i, j: (0, j, 0)),
        ],
        [row, hid, hid, hid, row, row, vecs],
        [
            jax.ShapeDtypeStruct((s, d), F32),
            jax.ShapeDtypeStruct((s, f), BF16),
            jax.ShapeDtypeStruct((s, f), BF16),
            jax.ShapeDtypeStruct((s, f), BF16),
            jax.ShapeDtypeStruct((s, d), BF16),
            jax.ShapeDtypeStruct((s, d), BF16),
            jax.ShapeDtypeStruct((8, d), F32),
        ],
        [dxo, x, vec, gg, uu, y, w_in, w_in, w_out],
        scratch=[pltpu.VMEM((tm, d), F32)], phases=phases,
    )


def _grad_half(a, b, big, where, mine, col0, prev, recv, name, phases=()):
    s, k1 = a.shape
    b, b_part = b if isinstance(b, tuple) else (b[None], 0)
    n = b.shape[2]
    rows_halved = big.h3 == 1
    kk, nn = (k1 // 2, n) if rows_halved else (k1, n // 2)
    tk = _pick(kk, (1408, 1024, 512, 256, 128))
    tn = _pick(nn, (1408, 1024, 640, 512, 256, 128))
    nkb, nnb = kk // tk, nn // tn
    assert col0 % tn == 0 and (recv is None) == (not mine)

    def half(pref):
        return pref[1] if mine else 1 - pref[1]

    def body(_, a_ref, b_ref, *rest):
        acc = _dot_tn(a_ref[...], b_ref[0])
        if recv is not None:
            acc = acc + rest[0][0].astype(F32)
        rest[-1][0] = acc.astype(BF16)

    out_spec = pl.BlockSpec((1, tk, tn), lambda i, j, pref: (0, i, col0 // tn + j))
    in_specs = [
        pl.BlockSpec((s, tk), lambda i, j, pref: (0, i + (half(pref) * nkb if rows_halved else 0))),
        pl.BlockSpec((1, s, tn), lambda i, j, pref: (b_part, 0, j + (0 if rows_halved else half(pref) * nnb))),
    ]
    ins = [a, b]
    if recv is not None:
        in_specs.append(out_spec)
        ins.append(recv)
    in_place = None
    if prev is not None:
        in_place = {len(ins): 0}
        in_specs.append(_ANY)
        ins.append(prev)
    return _call(
        body, name, (nkb, nnb), in_specs, [out_spec], [jax.ShapeDtypeStruct(big.dims("half"), BF16)], ins,
        prefetch=(where,), phases=phases, in_place=in_place,
    )


def _proj_mod_fwd(x, vec, w, phases=()):
    s, d = x.shape
    n = w.shape[2]
    tm = _pick(s, (512, 256, 128))
    tn = _pick(n, (640, 512, 256, 128))

    def body(x_ref, vec_ref, w_ref, o_ref, h_sc):
        @pl.when(pl.program_id(1) == 0)
        def _():
            h_sc[...] = _modulate(x_ref[...], vec_ref).astype(BF16)

        o_ref[...] = _dot(h_sc[...], w_ref[0])

    return _call(
        body, "ab_in_fwd", (s // tm, n // tn),
        [
            pl.BlockSpec((tm, d), lambda i, j: (i, 0)),
            pl.BlockSpec((8, d), lambda i, j: (0, 0)),
            pl.BlockSpec((1, d, tn), lambda i, j: (0, 0, j)),
        ],
        [pl.BlockSpec((tm, tn), lambda i, j: (i, j))],
        [jax.ShapeDtypeStruct((s, n), F32)], [x, vec, w],
        scratch=[pltpu.VMEM((tm, d), BF16)], phases=phases,
    )


def _proj_res_fwd(a, w, x, vec, phases=()):
    s, kd = a.shape
    d = x.shape[1]
    tm = _pick(s, (512, 256, 128))

    def body(a_ref, w_ref, x_ref, vec_ref, xo_ref, y_ref):
        yv = _dot(a_ref[...], w_ref[0])
        xo_ref[...] = x_ref[...] + vec_ref[3:4, :] * yv
        y_ref[...] = yv.astype(BF16)

    row = pl.BlockSpec((tm, d), lambda i: (i, 0))
    return _call(
        body, "ab_out_fwd", (s // tm,),
        [pl.BlockSpec((tm, kd), lambda i: (i, 0)), pl.BlockSpec((1, kd, d), lambda i: (0, 0, 0)), row, pl.BlockSpec((8, d), lambda i: (0, 0))],
        [row, row],
        [jax.ShapeDtypeStruct((s, d), F32), jax.ShapeDtypeStruct((s, d), BF16)], [a, w, x, vec], phases=phases,
    )


def _proj_res_bwd(dxo, y, vec, w, phases=()):
    s, d = dxo.shape
    kd = w.shape[1]
    tm = _pick(s, (512, 256, 128))

    def body(dxo_ref, y_ref, vec_ref, w_ref, dy_ref, da_ref, dgate_ref):
        @pl.when(pl.program_id(0) == 0)
        def _():
            dgate_ref[...] = jnp.zeros_like(dgate_ref)

        dxo_v = dxo_ref[...]
        dy = (vec_ref[3:4, :] * dxo_v).astype(BF16)
        dy_ref[...] = dy
        dgate_ref[3:4, :] += jnp.sum(dxo_v * y_ref[...].astype(F32), axis=0, keepdims=True)
        da_ref[...] = _dot_nt(dy, w_ref[0]).astype(BF16)

    row = pl.BlockSpec((tm, d), lambda i: (i, 0))
    vecs = pl.BlockSpec((8, d), lambda i: (0, 0))
    return _call(
        body, "ab_out_bwd", (s // tm,),
        [row, row, vecs, pl.BlockSpec((1, kd, d), lambda i: (0, 0, 0))],
        [row, pl.BlockSpec((tm, kd), lambda i: (i, 0)), vecs],
        [jax.ShapeDtypeStruct((s, d), BF16), jax.ShapeDtypeStruct((s, kd), BF16), jax.ShapeDtypeStruct((8, d), F32)],
        [dxo, y, vec, w], phases=phases,
    )


def _proj_mod_bwd(dproj, w, x, vec, dxo, dvec_in, name, phases=()):
    parts, s, n_part = dproj.shape
    d = x.shape[1]
    tm = _pick(s, (512, 256, 128))
    tk = _pick(n_part, (1408, 1280, 1024, 512, 256, 128))
    per_part = n_part // tk
    nk = parts * per_part

    def body(dp_ref, w_ref, x_ref, vec_ref, dxo_ref, dvi_ref, dx_ref, h_ref, dvec_ref, acc_sc):
        i, k = pl.program_id(0), pl.program_id(1)

        @pl.when((i == 0) & (k == 0))
        def _():
            dvec_ref[...] = dvi_ref[...]

        @pl.when(k == 0)
        def _():
            acc_sc[...] = jnp.zeros_like(acc_sc)

        acc_sc[...] += _dot_nt(dp_ref[0], w_ref[0])

        @pl.when(k == nk - 1)
        def _():
            dx, h = _modulate_bwd(x_ref[...], acc_sc[...], vec_ref, dvec_ref)
            dx_ref[...] = dxo_ref[...] + dx
            h_ref[...] = h.astype(BF16)

    row = pl.BlockSpec((tm, d), lambda i, k: (i, 0))
    vecs = pl.BlockSpec((8, d), lambda i, k: (0, 0))
    return _call(
        body, name, (s // tm, nk),
        [
            pl.BlockSpec((1, tm, tk), lambda i, k: (k // per_part, i, k % per_part)),
            pl.BlockSpec((1, d, tk), lambda i, k: (0, 0, k)),
            row, vecs, row, vecs,
        ],
        [row, row, vecs],
        [jax.ShapeDtypeStruct((s, d), F32), jax.ShapeDtypeStruct((s, d), BF16), jax.ShapeDtypeStruct((8, d), F32)],
        [dproj, w, x, vec, dxo, dvec_in], scratch=[pltpu.VMEM((tm, d), F32)], phases=phases,
    )


def _tril(n):
    return lax.broadcasted_iota(jnp.int32, (n, n), 0) >= lax.broadcasted_iota(jnp.int32, (n, n), 1)


def _layernorm_stats(gv):
    mu = jnp.mean(gv, axis=-1, keepdims=True)
    cen = gv - mu
    rstd = lax.rsqrt(jnp.mean(cen * cen, axis=-1, keepdims=True) + EPS)
    return cen * rstd, rstd


def _shift_down(q, k, above_ref, c_cg, c_xb, first):
    width = q.shape[1]
    rows = lax.broadcasted_iota(jnp.int32, q.shape, 0)
    out = pltpu.roll(q, k, 0)
    for r in range(k):
        src = CONV_HALO - k + r
        above = above_ref[src : src + 1, c_cg : c_cg + width] * above_ref[src : src + 1, c_xb : c_xb + width]
        above = jnp.where(first, 0.0, above)
        out = jnp.where(rows == r, above, out)
    return out


def _ab_mix_fwd(proj, norm_v, w_s, b_rows, conv_w, phases=()):
    s, n = proj.shape
    heads, chunk, _ = w_s.shape
    da = norm_v.shape[1]
    hd = da // heads
    db = conv_w.shape[1]
    tm = _pick(s, (512, 256, 128))

    def body(p_ref, ph_ref, nv_ref, ws_ref, b_ref, cw_ref, o_ref):
        first = pl.program_id(0) == 0
        gu, _ = _gelu(p_ref[:, 0:da])
        gv, _ = _gelu(p_ref[:, da : 2 * da])
        xhat, _ = _layernorm_stats(gv)
        vn = (xhat * nv_ref[...]).astype(BF16)
        mask = _tril(chunk)
        for hh in range(heads):
            wm = jnp.where(mask, ws_ref[hh], 0.0).astype(BF16)
            cols = slice(hh * hd, (hh + 1) * hd)
            for nn in range(tm // chunk):
                rows = slice(nn * chunk, (nn + 1) * chunk)
                z = _dot(wm, vn[rows, cols]) + b_ref[:, cols]
                o_ref[rows, cols] = (gu[rows, cols] * z).astype(BF16)
        c_cg, c_xb = 2 * da + db, 2 * da + 2 * db
        bg = p_ref[:, 2 * da : 2 * da + db]
        q = p_ref[:, c_cg : c_cg + db] * p_ref[:, c_xb : c_xb + db]
        q1 = _shift_down(q, 1, ph_ref, c_cg, c_xb, first)
        q2 = _shift_down(q, 2, ph_ref, c_cg, c_xb, first)
        conv = cw_ref[0:1, :] * q2 + cw_ref[1:2, :] * q1 + cw_ref[2:3, :] * q
        o_ref[:, da : da + db] = (bg * conv).astype(BF16)

    nh = tm // CONV_HALO
    return _call(
        body, "ab_mix_fwd", (s // tm,),
        [
            pl.BlockSpec((tm, n), lambda i: (i, 0)),
            pl.BlockSpec((CONV_HALO, n), lambda i: (jnp.maximum(i * nh - 1, 0), 0)),
            pl.BlockSpec((1, da), lambda i: (0, 0)),
            pl.BlockSpec((heads, chunk, chunk), lambda i: (0, 0, 0)),
            pl.BlockSpec((chunk, da), lambda i: (0, 0)),
            pl.BlockSpec((3, db), lambda i: (0, 0)),
        ],
        [pl.BlockSpec((tm, da + db), lambda i: (i, 0))],
        [jax.ShapeDtypeStruct((s, da + db), BF16)], [proj, proj, norm_v, w_s, b_rows, conv_w], phases=phases,
    )


def _ab_mix_bwd(proj, dcat, norm_v, w_s, b_rows, conv_w, phases=()):
    s, n = proj.shape
    heads, chunk, _ = w_s.shape
    da = norm_v.shape[1]
    hd = da // heads
    db = conv_w.shape[1]
    tm = _pick(s, (512, 256, 128))
    nblk = s // tm
    dhalo = 2 * CONV_HALO

    def body(p_ref, pa_ref, pb_ref, dc_ref, dcb_ref, nv_ref, ws_ref, b_ref, cw_ref,
             dp_ref, dnv_ref, dws_ref, dzs_ref, dcw_ref, dvn_sc):
        i = pl.program_id(0)
        first, last = i == 0, i == nblk - 1

        @pl.when(first)
        def _():
            dnv_ref[...] = jnp.zeros_like(dnv_ref)
            dws_ref[...] = jnp.zeros_like(dws_ref)
            dzs_ref[...] = jnp.zeros_like(dzs_ref)
            dcw_ref[...] = jnp.zeros_like(dcw_ref)

        uu = p_ref[:, 0:da]
        gu, gu_grad = _gelu(uu)
        gv, gv_grad = _gelu(p_ref[:, da : 2 * da])
        xhat, rstd = _layernorm_stats(gv)
        nv = nv_ref[...]
        vn = (xhat * nv).astype(BF16)
        dya = dc_ref[:, 0:da].astype(F32)
        dz = (dya * gu).astype(BF16)
        mask = _tril(chunk)
        for hh in range(heads):
            wm = jnp.where(mask, ws_ref[hh], 0.0).astype(BF16)
            cols = slice(hh * hd, (hh + 1) * hd)
            dws = jnp.zeros((chunk, chunk), F32)
            for nn in range(tm // chunk):
                rows = slice(nn * chunk, (nn + 1) * chunk)
                z = _dot(wm, vn[rows, cols]) + b_ref[:, cols]
                dp_ref[rows, cols] = (dya[rows, cols] * z * gu_grad[rows, cols]).astype(BF16)
                dz_blk = dz[rows, cols]
                dws = dws + _dot_nt(dz_blk, vn[rows, cols])
                dzs_ref[:, cols] += dz_blk.astype(F32)
                dvn = _dot_tn(wm, dz_blk)
                dnv_ref[:, cols] += jnp.sum(dvn * xhat[rows, cols], axis=0, keepdims=True)
                dvn_sc[rows, cols] = dvn
            dws_ref[hh] += jnp.where(mask, dws, 0.0)
        dxhat = dvn_sc[...] * nv
        dgv = rstd * (dxhat - jnp.mean(dxhat, axis=-1, keepdims=True) - xhat * jnp.mean(dxhat * xhat, axis=-1, keepdims=True))
        dp_ref[:, da : 2 * da] = (dgv * gv_grad).astype(BF16)

        c_bg, c_cg, c_xb = 2 * da, 2 * da + db, 2 * da + 2 * db
        bg = p_ref[:, c_bg : c_bg + db]
        cg = p_ref[:, c_cg : c_cg + db]
        xb = p_ref[:, c_xb : c_xb + db]
        q = cg * xb
        q1 = _shift_down(q, 1, pa_ref, c_cg, c_xb, first)
        q2 = _shift_down(q, 2, pa_ref, c_cg, c_xb, first)
        dyb = dc_ref[:, da : da + db].astype(F32)
        conv = cw_ref[0:1, :] * q2 + cw_ref[1:2, :] * q1 + cw_ref[2:3, :] * q
        dp_ref[:, c_bg : c_bg + db] = (dyb * conv).astype(BF16)
        e = dyb * bg
        dcw_ref[0:1, :] += jnp.sum(e * q2, axis=0, keepdims=True)
        dcw_ref[1:2, :] += jnp.sum(e * q1, axis=0, keepdims=True)
        dcw_ref[2:3, :] += jnp.sum(e * q, axis=0, keepdims=True)
        rows = lax.broadcasted_iota(jnp.int32, e.shape, 0)
        dq = cw_ref[2:3, :] * e
        for kk in (1, 2):
            ek = pltpu.roll(e, tm - kk, 0)
            for r in range(kk):
                below = dcb_ref[r : r + 1, da : da + db].astype(F32) * pb_ref[r : r + 1, c_bg : c_bg + db]
                below = jnp.where(last, 0.0, below)
                ek = jnp.where(rows == tm - kk + r, below, ek)
            dq = dq + cw_ref[2 - kk : 3 - kk, :] * ek
        dp_ref[:, c_cg : c_cg + db] = (dq * xb).astype(BF16)
        dp_ref[:, c_xb : c_xb + db] = (dq * cg).astype(BF16)

    nh = tm // CONV_HALO
    nhb = tm // dhalo
    const2 = lambda i: (0, 0)
    return _call(
        body, "ab_mix_bwd", (nblk,),
        [
            pl.BlockSpec((tm, n), lambda i: (i, 0)),
            pl.BlockSpec((CONV_HALO, n), lambda i: (jnp.maximum(i * nh - 1, 0), 0)),
            pl.BlockSpec((CONV_HALO, n), lambda i: (jnp.minimum((i + 1) * nh, s // CONV_HALO - 1), 0)),
            pl.BlockSpec((tm, da + db), lambda i: (i, 0)),
            pl.BlockSpec((dhalo, da + db), lambda i: (jnp.minimum((i + 1) * nhb, s // dhalo - 1), 0)),
            pl.BlockSpec((1, da), const2),
            pl.BlockSpec((heads, chunk, chunk), lambda i: (0, 0, 0)),
            pl.BlockSpec((chunk, da), const2),
            pl.BlockSpec((3, db), const2),
        ],
        [
            pl.BlockSpec((tm, n), lambda i: (i, 0)),
            pl.BlockSpec((1, da), const2),
            pl.BlockSpec((heads, chunk, chunk), lambda i: (0, 0, 0)),
            pl.BlockSpec((chunk, da), const2),
            pl.BlockSpec((3, db), const2),
        ],
        [
            jax.ShapeDtypeStruct((s, n), BF16),
            jax.ShapeDtypeStruct((1, da), F32),
            jax.ShapeDtypeStruct((heads, chunk, chunk), F32),
            jax.ShapeDtypeStruct((chunk, da), F32),
            jax.ShapeDtypeStruct((3, db), F32),
        ],
        [proj, proj, proj, dcat, dcat, norm_v, w_s, b_rows, conv_w],
        scratch=[pltpu.VMEM((tm, da), F32)], phases=phases,
    )


def _pool_counts(tm, i, w):
    t = i * tm + lax.broadcasted_iota(jnp.int32, (tm, 1), 0)
    return jnp.minimum(t + 1, w).astype(F32)


def _pool_fwd(x, vec, w_grp, scale, phases=()):
    s, d = x.shape
    groups, gd, _ = w_grp.shape
    tm = _pick(s, (512, 256, 128))

    def body(x_ref, xa_ref, vec_ref, w_ref, sc_ref, xo_ref, p_ref, o_ref):
        i = pl.program_id(0)
        h = _modulate(x_ref[...], vec_ref)
        ha = jnp.where(i == 0, 0.0, _modulate(xa_ref[...], vec_ref))
        ext = jnp.concatenate([ha, h], axis=0)
        for gi, w in enumerate(POOL_WINDOWS):
            cols = slice(gi * gd, (gi + 1) * gd)
            acc = ext[:, cols]
            step = 1
            while step < w:
                acc = acc + pltpu.roll(acc, step, 0)
                step *= 2
            p = (acc[POOL_HALO:, :] / _pool_counts(tm, i, w) - h[:, cols]).astype(BF16)
            p_ref[:, cols] = p
            o_ref[:, cols] = _dot(p, w_ref[gi]).astype(BF16)
        xo_ref[...] = x_ref[...] + vec_ref[3:4, :] * (o_ref[...].astype(F32) * sc_ref[...])

    nh = tm // POOL_HALO
    row = pl.BlockSpec((tm, d), lambda i: (i, 0))
    return _call(
        body, "pool_fwd", (s // tm,),
        [
            row,
            pl.BlockSpec((POOL_HALO, d), lambda i: (jnp.maximum(i * nh - 1, 0), 0)),
            pl.BlockSpec((8, d), lambda i: (0, 0)),
            pl.BlockSpec((groups, gd, gd), lambda i: (0, 0, 0)),
            pl.BlockSpec((1, d), lambda i: (0, 0)),
        ],
        [row, row, row],
        [jax.ShapeDtypeStruct((s, d), F32), jax.ShapeDtypeStruct((s, d), BF16), jax.ShapeDtypeStruct((s, d), BF16)],
        [x, x, vec, w_grp, scale], phases=phases,
    )


def _pool_bwd(dxo, x, vec, p, o, w_grp, scale, phases=()):
    s, d = x.shape
    groups, gd, _ = w_grp.shape
    tm = _pick(s, (512, 256, 128))
    nblk = s // tm

    def body(dxo_ref, dxb_ref, x_ref, vec_ref, p_ref, o_ref, w_ref, sc_ref, dx_ref, dw_ref, dsc_ref, dvec_ref, dw_sc):
        i = pl.program_id(0)

        @pl.when(i == 0)
        def _():
            dw_sc[...] = jnp.zeros_like(dw_sc)
            dsc_ref[...] = jnp.zeros_like(dsc_ref)
            dvec_ref[...] = jnp.zeros_like(dvec_ref)

        gate, sc = vec_ref[3:4, :], sc_ref[...]
        dxo_v = dxo_ref[...]
        ov = o_ref[...].astype(F32)
        dvec_ref[3:4, :] += jnp.sum(dxo_v * (ov * sc), axis=0, keepdims=True)
        dy = gate * dxo_v
        dsc_ref[...] += jnp.sum(dy * ov, axis=0, keepdims=True)
        dout = (dy * sc).astype(BF16)
        dout_b = jnp.where(i == nblk - 1, 0.0, gate * dxb_ref[...] * sc).astype(BF16)
        for gi, w in enumerate(POOL_WINDOWS):
            cols = slice(gi * gd, (gi + 1) * gd)
            dw_sc[gi] += _dot_tn(p_ref[:, cols], dout[:, cols])
            wb = w_ref[gi]
            dp = _dot_nt(dout[:, cols], wb)
            dp_b = _dot_nt(dout_b[:, cols], wb)
            e = dp / _pool_counts(tm, i, w)
            t_below = (i + 1) * tm + lax.broadcasted_iota(jnp.int32, (POOL_HALO, 1), 0)
            e_b = dp_b / jnp.minimum(t_below + 1, w).astype(F32)
            acc = jnp.concatenate([e, e_b], axis=0)
            step = 1
            while step < w:
                acc = acc + pltpu.roll(acc, tm + POOL_HALO - step, 0)
                step *= 2
            dx_ref[:, cols] = acc[:tm, :] - dp
        dx, _ = _modulate_bwd(x_ref[...], dx_ref[...], vec_ref, dvec_ref)
        dx_ref[...] = dxo_v + dx

        @pl.when(i == nblk - 1)
        def _():
            dw_ref[...] = dw_sc[...].astype(BF16)

    nh = tm // POOL_HALO
    row = pl.BlockSpec((tm, d), lambda i: (i, 0))
    vecs = pl.BlockSpec((8, d), lambda i: (0, 0))
    wspec = pl.BlockSpec((groups, gd, gd), lambda i: (0, 0, 0))
    return _call(
        body, "pool_bwd", (nblk,),
        [
            row,
            pl.BlockSpec((POOL_HALO, d), lambda i: (jnp.minimum((i + 1) * nh, s // POOL_HALO - 1), 0)),
            row, vecs, row, row, wspec,
            pl.BlockSpec((1, d), lambda i: (0, 0)),
        ],
        [row, wspec, pl.BlockSpec((1, d), lambda i: (0, 0)), vecs],
        [
            jax.ShapeDtypeStruct((s, d), F32),
            jax.ShapeDtypeStruct((groups, gd, gd), BF16),
            jax.ShapeDtypeStruct((1, d), F32),
            jax.ShapeDtypeStruct((8, d), F32),
        ],
        [dxo, dxo, x, vec, p, o, w_grp, scale],
        scratch=[pltpu.VMEM((groups, gd, gd), F32)], phases=phases,
    )


def _loss_head(x, gain, target, phases=()):
    s, d = x.shape
    tm = _pick(s, (512, 256, 128))

    def body(x_ref, g_ref, t_ref, dx_ref, aux_ref):
        @pl.when(pl.program_id(0) == 0)
        def _():
            aux_ref[...] = jnp.zeros_like(aux_ref)

        xv = x_ref[...]
        rstd = _rstd(xv)
        r = xv * rstd
        gain_v = g_ref[...]
        err = r * gain_v - t_ref[...]
        aux_ref[1:2, :] += jnp.sum(err * err, axis=0, keepdims=True)
        dout = err * (1.0 / d)
        aux_ref[0:1, :] += jnp.sum(dout * r, axis=0, keepdims=True)
        dr = dout * gain_v
        dx_ref[...] = rstd * (dr - r * jnp.mean(dr * r, axis=-1, keepdims=True))

    row = pl.BlockSpec((tm, d), lambda i: (i, 0))
    return _call(
        body, "loss_head", (s // tm,),
        [row, pl.BlockSpec((1, d), lambda i: (0, 0)), row],
        [row, pl.BlockSpec((8, d), lambda i: (0, 0))],
        [jax.ShapeDtypeStruct((s, d), F32), jax.ShapeDtypeStruct((8, d), F32)], [x, gain, target], phases=phases,
    )


def _small_adam(gathered, gathered_ws, layout, smalls, chip):
    names = list(smalls)
    n = len(names)

    def body(*refs):
        chip_ref, g_ref, gws_ref = refs[0], refs[1], refs[2]
        wmv = refs[3 : 3 + 3 * n]
        outs = refs[3 + 3 * n : 3 + 7 * n]
        total = refs[-1]
        total[...] = g_ref[0]
        for kdev in range(1, N_DEV):
            total[...] += g_ref[kdev]
        total_ws = gws_ref[0]
        for kdev in range(1, N_DEV):
            total_ws = total_ws + gws_ref[kdev]
        my_chip = chip_ref[0]
        for a, name in enumerate(names):
            w_ref, m_ref, v_ref = wmv[3 * a : 3 * a + 3]
            if name == "ab_w_s":
                g = total_ws
            else:
                row0, rows, col0, cols = layout[name]
                if col0 is None:
                    g = jnp.zeros((rows, cols), F32)
                    for j in range(N_CHIPS):
                        g = g + jnp.where(my_chip == j, total[row0 : row0 + rows, j * cols : (j + 1) * cols], 0.0)
                else:
                    g = total[row0 : row0 + rows, col0 : col0 + cols]
            dl, mo, vo = _adam(w_ref[...], g, m_ref[...], v_ref[...])
            outs[4 * a][...] = g
            outs[4 * a + 1][...] = dl
            outs[4 * a + 2][...] = mo
            outs[4 * a + 3][...] = vo

    ins = [gathered, gathered_ws]
    out_shapes = []
    for name in names:
        ins.extend(smalls[name])
        out_shapes.extend([jax.ShapeDtypeStruct(smalls[name][0].shape, F32)] * 4)
    whole = lambda shape: pl.BlockSpec(shape, functools.partial(lambda nd, i, c: (0,) * nd, len(shape)))
    res = pl.pallas_call(
        body, name="small_adam",
        grid_spec=pltpu.PrefetchScalarGridSpec(
            num_scalar_prefetch=1, grid=(1,),
            in_specs=[whole(a.shape) for a in ins], out_specs=[whole(o.shape) for o in out_shapes],
            scratch_shapes=[pltpu.VMEM(gathered.shape[1:], F32)],
        ),
        out_shape=out_shapes,
        compiler_params=pltpu.CompilerParams(dimension_semantics=("arbitrary",), vmem_limit_bytes=VMEM_LIMIT_BYTES),
    )(chip.reshape(1).astype(jnp.int32), *ins)
    return {name: res[4 * a : 4 * a + 4] for a, name in enumerate(names)}


def _pad_rows(a, rows=8):
    extra = (-a.shape[0]) % rows
    return jnp.pad(a, ((0, extra), (0, 0))) if extra else a


def _pad_cols(a, cols):
    return jnp.pad(a, ((0, 0), (0, cols - a.shape[1]))) if a.shape[1] < cols else a


def _run(fn, *phases):
    outs, p_outs = fn(list(phases))
    for p, po in zip(phases, p_outs):
        p.then(po)
    return outs


def kernel(x, c, norm_g, w_mod, b_mod, w_ffn_in, w_ffn_out, ab_w_in, ab_norm_v, ab_w_s, ab_b_s, ab_conv_w, ab_w_out, pool_w_grp, pool_scale, final_g, loss_target, m_norm_g, m_w_mod, m_b_mod, m_w_ffn_in, m_w_ffn_out, m_ab_w_in, m_ab_norm_v, m_ab_w_s, m_ab_b_s, m_ab_conv_w, m_ab_w_out, m_pool_w_grp, m_pool_scale, m_final_g, v_norm_g, v_w_mod, v_b_mod, v_w_ffn_in, v_w_ffn_out, v_ab_w_in, v_ab_norm_v, v_ab_w_s, v_ab_b_s, v_ab_conv_w, v_ab_w_out, v_pool_w_grp, v_pool_scale, v_final_g):
    ix, iy, ic = _place()
    chip = 2 * ix + iy
    me = 4 * ix + 2 * iy + ic
    where = jnp.stack([chip, ic]).astype(jnp.int32)
    s, d = x.shape[1], x.shape[2]
    x0 = x.reshape(s, d)
    target = loss_target.reshape(s, d)
    n_layers = norm_g.shape[0]
    dq = d // N_CHIPS
    heads, chunk = ab_w_s.shape[1], ab_w_s.shape[2]
    da = ab_norm_v.shape[1]
    db = ab_conv_w.shape[2] * N_CHIPS
    f_hidden = w_ffn_out.shape[2] * N_CHIPS
    assert n_layers == 2 and da % heads == 0

    cw_pad = _pad_cols(ab_conv_w.reshape(3, db // N_CHIPS), dq)
    packed = jnp.concatenate(
        [_pad_rows(c.reshape(N_CHIPS, dq)), _pad_rows(norm_g.reshape(-1, dq)), _pad_rows(pool_scale.reshape(1, dq)), _pad_rows(cw_pad)],
        axis=0,
    )
    ncol = w_mod.shape[2]
    b_cols = lax.dynamic_slice(b_mod, (0, chip * ncol), (n_layers, ncol)).reshape(n_layers, 1, ncol)
    small = {}

    def small_gather(key, arrs):
        def then(outs):
            small[key] = outs

        return _phase_small_gather(arrs, then)

    stacks = {
        "w_ffn_in": tuple(a.reshape((-1,) + a.shape[2:]) for a in (w_ffn_in, m_w_ffn_in, v_w_ffn_in)),
        "w_ffn_out": tuple(a.reshape((-1,) + a.shape[2:]) for a in (w_ffn_out, m_w_ffn_out, v_w_ffn_out)),
        "ab_w_in": (ab_w_in, m_ab_w_in, v_ab_w_in),
        "ab_w_out": (ab_w_out, m_ab_w_out, v_ab_w_out),
        "pool_w_grp": (pool_w_grp[0], m_pool_w_grp[0], v_pool_w_grp[0]),
    }
    big_in = _Big((1, d, 2 * f_hidden), 2, 1)
    big_out = _Big((1, f_hidden, d), 1, 2)
    units = {}
    for l in range(n_layers):
        for k in range(2):
            units[f"in{l}{k}"] = (big_in, "w_ffn_in", 2 * l + k)
            units[f"out{l}{k}"] = (big_out, "w_ffn_out", 2 * l + k)
    units["abin"] = (_Big((1, d, ab_w_in.shape[2] * N_CHIPS), 2, 1), "ab_w_in", 0)
    units["about"] = (_Big((1, ab_w_out.shape[1] * N_CHIPS, d), 1, 2), "ab_w_out", 0)
    units["pool"] = (_Big((pool_w_grp.shape[1], pool_w_grp.shape[2] * N_CHIPS, pool_w_grp.shape[3]), 1, 0), "pool_w_grp", 0)
    big = {u: g for u, (g, _, _) in units.items()}

    weight = {}
    complete = set()

    def cast(u):
        g, st, b0 = units[u]

        def launch(phases):
            (weight[u],), p_outs = _cast_into_full(stacks[st][0], b0, g, where, "cast_" + u, phases)
            return None, p_outs

        return launch

    def gather_ici(*us):
        def then(outs):
            for u, o in zip(us, outs):
                weight[u] = o

        return _phase_gather_ici([weight[u] for u in us], [big[u] for u in us], then)

    def gather_sibling(*us):
        def then(outs):
            for u, o in zip(us, outs):
                weight[u] = o
                complete.add(u)

        return _phase_gather_sibling([weight[u] for u in us], [big[u] for u in us], then)

    def w_of(u):
        assert u in complete, u
        return weight[u]

    _run(cast("in00"), small_gather("inputs", [packed]))
    _run(cast("out00"))
    small_all = small["inputs"][0]
    by_chip = small_all[0::2]
    c_all = small_all[:, 0:N_CHIPS, :].reshape(N_DEV, d)
    norm_full = by_chip[:, 8 : 8 + 3 * n_layers, :].transpose(1, 0, 2).reshape(3 * n_layers, d)
    pool_scale_full = by_chip[:, 16:17, :].transpose(1, 0, 2).reshape(1, d)
    conv_full = by_chip[:, 24:27, : db // N_CHIPS].transpose(1, 0, 2).reshape(3, db)
    pieces = [("in00", "out00"), ("abin", "about"), ("in01", "out01"), ("in10", "out10", "pool"), ("in11", "out11")]
    in_flight = {}

    def start_gather(p):
        in_flight[p] = _split_start(gather_ici(*pieces[p]), f"gather_{p}_start")

    def started():
        return _after(*[flight.token for flight in in_flight.values()])

    def finish_gather(p, after):
        flight = in_flight.pop(p)
        _split_wait(flight, list(after) + list(started().ins), f"gather_{p}_wait")
        if p + 2 < len(pieces):
            start_gather(p + 2)
        _split_wait(_split_start(gather_sibling(*pieces[p]), f"gather_{p}_forward"), [], f"gather_{p}_forwarded")

    start_gather(0)
    mod_cols = _run(lambda phases: _mod_fwd(c_all, w_mod, b_cols, phases))[0]
    _run(cast("abin"), small_gather("mod", [mod_cols.reshape(n_layers * N_DEV, ncol)]))
    _run(cast("about"))
    start_gather(1)
    for u in ("in01", "out01", "in10", "out10", "pool", "in11", "out11"):
        _run(cast(u))
    mod_all = small["mod"][0]
    mod_mine = lax.dynamic_index_in_dim(mod_all[0::2].reshape(N_CHIPS, n_layers, N_DEV, ncol), me, axis=2, keepdims=False)
    mod = mod_mine.transpose(1, 0, 2).reshape(n_layers, 3, 3, d)
    vecs = {
        (l, sub): _pad_rows(jnp.concatenate([norm_full[3 * l + sub][None], mod[l, sub]], axis=0))
        for l in range(n_layers)
        for sub in range(3)
    }
    b_rows = jnp.broadcast_to(ab_b_s[0].T[:, :, None], (chunk, heads, da // heads)).reshape(chunk, da)

    saved = {}

    def ffn_forward(xs, l, sub, k, *phases):
        saved[l, sub, "x"] = xs
        xs, gg, uu, yb = _run(
            lambda ph: _ffn_fwd(xs, vecs[l, sub], w_of(f"in{l}{k}"), w_of(f"out{l}{k}"), f"ffn_fwd_{l}{k}", ph), *phases
        )
        saved[l, sub, "act"] = (gg, uu, yb)
        return xs

    finish_gather(0, [vecs[0, 0]])
    xs = ffn_forward(x0, 0, 0, 0, started())
    saved[0, 1, "x"] = xs
    finish_gather(1, [xs])
    (proj,) = _run(lambda ph: _proj_mod_fwd(xs, vecs[0, 1], w_of("abin"), ph), started())
    (cat,) = _run(lambda ph: _ab_mix_fwd(proj, ab_norm_v, ab_w_s[0], b_rows, conv_full, ph))
    xs, yb = _run(lambda ph: _proj_res_fwd(cat, w_of("about"), xs, vecs[0, 1], ph))
    saved[0, 1, "act"] = (proj, cat, yb)
    finish_gather(2, [xs])
    xs = ffn_forward(xs, 0, 2, 1, started())
    finish_gather(3, [xs])
    xs = ffn_forward(xs, 1, 0, 0, started())
    saved[1, 1, "x"] = xs
    xs, pp, oo = _run(lambda ph: _pool_fwd(xs, vecs[1, 1], w_of("pool"), pool_scale_full, ph))
    saved[1, 1, "act"] = (pp, oo)
    finish_gather(4, [xs])
    xs = ffn_forward(xs, 1, 2, 1)
    dxs, aux = _run(lambda ph: _loss_head(xs, final_g.reshape(1, d), target, ph))
    loss = lax.psum(0.5 * jnp.sum(aux[1]) / d, ("x", "y", "c"))

    grad = {}
    recv = {}
    csum = {}
    parts = {}
    reduced = {}
    done = set()
    dvecs, small_g = {}, {}

    def pair_exchange(*us):
        def then(outs):
            for u, o in zip(us, outs):
                recv[u] = o

        return _phase_pair_exchange([grad[u] for u in us], [big[u] for u in us], then)

    def grad_half(u, a, b, mine, name, *phases, col0=0, prev=None):
        (res,) = _run(lambda ph: _grad_half(a, b, big[u], where, mine, col0, prev, recv[u] if mine else None, name, ph), *phases)
        return res

    def pair_sum(u, *phases):
        def launch(ph):
            (csum[u],), p_outs = _pair_sum(grad[u], recv[u], big[u], where, "pair_sum_" + u, ph)
            return None, p_outs

        _run(launch, *phases)

    def chip_exchange(*us):
        def then(outs):
            for u, o in zip(us, outs):
                parts[u] = o

        return _phase_chip_exchange([csum[u] for u in us], [big[u] for u in us], then)

    def chip_sum(*us, carried=()):
        for n_u, u in enumerate(us):
            g, st, b0 = units[u]

            def launch(ph):
                (reduced[st],), p_outs = _chip_sum(
                    csum[u], parts[u], g, where, reduced.get(st), stacks[st][0].shape, b0, "chip_sum_" + u, ph
                )
                return None, p_outs

            _run(launch, *(carried if n_u == 0 else ()))

    def pair_broadcast(*us):
        sts = [units[u][1] for u in us]
        assert len(set(sts)) == len(sts)

        def then(outs):
            for u, st, o in zip(us, sts, outs):
                reduced[st] = o
                done.add(u)

        return _phase_pair_broadcast([reduced[st] for st in sts], [big[u] for u in us], [units[u][2] for u in us], then)

    def ffn_backward(dxs, l, sub, k, carried_bwd, carried_send, carried_mine):
        gg, uu, yb = saved[l, sub, "act"]
        w_in, w_out = w_of(f"in{l}{k}"), w_of(f"out{l}{k}")
        uo, ui, tag = f"out{l}{k}", f"in{l}{k}", f"{l}{k}"
        dxs, dg, du, a, h, dy, dvecs[l, sub] = _run(
            lambda ph: _ffn_bwd(dxs, saved[l, sub, "x"], vecs[l, sub], gg, uu, yb, w_in, w_out, "ffn_bwd_" + tag, ph), *carried_bwd()
        )
        grad[uo] = grad_half(uo, a, dy, False, "dw_out_send_" + tag, *carried_send())
        part = grad_half(ui, h, du, False, "dw_in_u_send_" + tag, pair_exchange(uo), col0=f_hidden)
        grad[ui] = grad_half(ui, h, dg, False, "dw_in_g_send_" + tag, prev=part)
        csum[uo] = grad_half(uo, a, dy, True, "dw_out_" + tag, pair_exchange(ui))
        part = grad_half(ui, h, du, True, "dw_in_u_" + tag, *carried_mine(), col0=f_hidden)
        csum[ui] = grad_half(ui, h, dg, True, "dw_in_g_" + tag, prev=part)
        return dxs

    none = lambda: ()
    dxs = ffn_backward(dxs, 1, 2, 1, none, none, none)
    pp, oo = saved[1, 1, "act"]
    dxs, grad["pool"], small_g["pool_scale"], dvecs[1, 1] = _run(
        lambda ph: _pool_bwd(dxs, saved[1, 1, "x"], vecs[1, 1], pp, oo, w_of("pool"), pool_scale_full, ph)
    )

    def after_11():
        return (chip_exchange("in11", "out11"), pair_exchange("pool"))

    def bcast_11():
        chip_sum("in11", "out11")
        pair_sum("pool")
        return (pair_broadcast("in11", "out11"), chip_exchange("pool"))

    dxs = ffn_backward(dxs, 1, 0, 0, after_11, bcast_11, none)

    def after_10():
        return (chip_exchange("in10", "out10"),)

    def bcast_10():
        chip_sum("in10", "out10", "pool")
        return (pair_broadcast("in10", "out10", "pool"),)

    dxs = ffn_backward(dxs, 0, 2, 1, after_10, bcast_10, none)

    proj, cat, yb = saved[0, 1, "act"]
    dy, dcat, dgate = _run(lambda ph: _proj_res_bwd(dxs, yb, vecs[0, 1], w_of("about"), ph))
    grad["about"] = grad_half("about", cat, dy, False, "dw_ab_out_send")
    dproj, small_g["ab_norm_v"], small_g["ab_w_s"], dzs, small_g["ab_conv_w"] = _run(
        lambda ph: _ab_mix_bwd(proj, dcat, ab_norm_v, ab_w_s[0], b_rows, conv_full, ph), chip_exchange("out01"), pair_exchange("about")
    )
    small_g["ab_b_s"] = dzs.reshape(chunk, heads, da // heads).sum(axis=2).T
    dxs, h, dvecs[0, 1] = _run(
        lambda ph: _proj_mod_bwd(dproj[None], w_of("abin"), saved[0, 1, "x"], vecs[0, 1], dxs, dgate, "ab_in_bwd", ph)
    )
    grad["abin"] = grad_half("abin", h, dproj, False, "dw_ab_in_send")
    chip_sum("out01", carried=(pair_exchange("abin"),))
    csum["about"] = grad_half("about", cat, dy, True, "dw_ab_out", pair_broadcast("out01"))
    csum["abin"] = grad_half("abin", h, dproj, True, "dw_ab_in")

    def after_01():
        return (chip_exchange("in01", "abin", "about"),)

    def bcast_01():
        chip_sum("in01", "abin", "about")
        return (pair_broadcast("in01", "abin", "about"),)

    def reduce_out00():
        return (chip_exchange("out00"),)

    dxs = ffn_backward(dxs, 0, 0, 0, after_01, bcast_01, reduce_out00)
    grad_x = dxs.reshape(x.shape)

    dgain = jnp.stack([dvecs[l, sub][0] for l in range(n_layers) for sub in range(3)])
    dmod = jnp.concatenate([dvecs[l, sub][1:4] for l in range(n_layers) for sub in range(3)], axis=0)
    pieces = {
        "norm_g": (dgain, None, dq), "final_g": (aux[0:1], 0, d), "pool_scale": (small_g["pool_scale"], None, dq),
        "b_mod": (dmod, 0, d), "ab_norm_v": (small_g["ab_norm_v"], 0, da), "ab_conv_w": (small_g["ab_conv_w"], None, db // N_CHIPS),
        "ab_b_s": (small_g["ab_b_s"], 0, chunk),
    }
    layout, row0 = {}, 0
    for nm, (pc, col0, cols) in pieces.items():
        layout[nm] = (row0, pc.shape[0], col0, cols)
        row0 += pc.shape[0]
    packed_rows = -(-row0 // 8) * 8
    packed_g = sum(
        jnp.pad(pc, ((layout[nm][0], packed_rows - layout[nm][0] - pc.shape[0]), (0, d - pc.shape[1])))
        for nm, (pc, _, _) in pieces.items()
    )

    last = _split_start(chip_exchange("in00"), "reduce_last_start")
    started = last.token
    chip_sum("out00")
    _flush(
        "broadcast_out00", pair_broadcast("out00"),
        small_gather("grads", [packed_g, small_g["ab_w_s"].reshape(heads * chunk, chunk)]),
    )
    g_all, gws_all = small["grads"]

    out = {}

    def adam_stack(st, after=()):
        w3, m3, v3 = stacks[st]
        assert all(u in done for u, (_, ust, _) in units.items() if ust == st), st
        shape = {"w_ffn_in": w_ffn_in.shape, "w_ffn_out": w_ffn_out.shape, "pool_w_grp": pool_w_grp.shape}.get(st, w3.shape)
        out[st] = tuple(a.reshape(shape) for a in _adam_stack(w3, reduced[st], m3, v3, "adam_" + st, after))

    for st in ("w_ffn_out", "ab_w_in", "ab_w_out", "pool_w_grp"):
        adam_stack(st, (started,))

    shapes2d = {
        "norm_g": (3 * n_layers, dq), "b_mod": (9 * n_layers, d), "final_g": (1, d), "ab_norm_v": (1, da),
        "pool_scale": (1, dq), "ab_conv_w": (3, db // N_CHIPS), "ab_b_s": (heads, chunk), "ab_w_s": (heads * chunk, chunk),
    }
    small_w = {"norm_g": (norm_g, m_norm_g, v_norm_g), "b_mod": (b_mod, m_b_mod, v_b_mod), "final_g": (final_g, m_final_g, v_final_g),
               "ab_norm_v": (ab_norm_v, m_ab_norm_v, v_ab_norm_v), "pool_scale": (pool_scale, m_pool_scale, v_pool_scale),
               "ab_conv_w": (ab_conv_w, m_ab_conv_w, v_ab_conv_w), "ab_b_s": (ab_b_s, m_ab_b_s, v_ab_b_s), "ab_w_s": (ab_w_s, m_ab_w_s, v_ab_w_s)}
    smalls = {nm: tuple(a.reshape(shapes2d[nm]) for a in wmv) for nm, wmv in small_w.items()}
    small_out = _small_adam(g_all, gws_all, layout, smalls, chip)
    for nm, res in small_out.items():
        out[nm] = tuple(a.reshape(small_w[nm][0].shape) for a in res)

    mod_row0 = layout["b_mod"][0]
    dmod_all = g_all[:, mod_row0 : mod_row0 + 9 * n_layers, :].reshape(N_DEV, n_layers, 9 * d)
    dmod_cols = lax.dynamic_slice(dmod_all, (0, 0, chip * ncol), (N_DEV, n_layers, ncol)).transpose(1, 0, 2)
    out["w_mod"] = tuple(_mod_bwd_adam(c_all.T, dmod_cols, w_mod, m_w_mod, v_w_mod, (started,)))

    (csum["in00"],) = _split_wait(
        last, [out[st][1] for st in ("w_mod", "w_ffn_out", "ab_w_in", "ab_w_out", "pool_w_grp")], "reduce_last_wait"
    )
    chip_sum("in00")
    _flush("broadcast_last", pair_broadcast("in00"))
    adam_stack("w_ffn_in")

    order = ["norm_g", "w_mod", "b_mod", "w_ffn_in", "w_ffn_out", "ab_w_in", "ab_norm_v", "ab_w_s", "ab_b_s", "ab_conv_w", "ab_w_out", "pool_w_grp", "pool_scale", "final_g"]
    return (loss, grad_x, *[out[nm][0] for nm in order], *[out[nm][1] for nm in order], *[out[nm][2] for nm in order], *[out[nm][3] for nm in order])
```

```python
import functools
import math

import jax
import jax.numpy as jnp
from jax import lax
from jax.experimental import pallas as pl
from jax.experimental.pallas import tpu as pltpu

F32 = jnp.float32
BF16 = jnp.bfloat16
MESH = pl.DeviceIdType.MESH

EPS = 1e-6
ADAM_LR = 0.001
ADAM_B1 = 0.9
ADAM_B2 = 0.999
ADAM_EPS = 1e-08
ADAM_WD = 0.01
ADAM_STEP = 10
POOL_WINDOWS = (2, 4, 8, 16)
POOL_HALO = 16
CONV_HALO = 8
N_CHIPS = 4
N_DEV = 8
VMEM_LIMIT_BYTES = 48 * 1024 * 1024
EW_BLOCK_ELEMS = 256 * 1024


def _pick(n, prefs):
    for p in prefs:
        if p <= n and n % p == 0:
            return p
    return n


def _row_tile(rows, cols):
    best = None
    for d in range(16, rows + 1, 16):
        if rows % d == 0 and d * cols <= EW_BLOCK_ELEMS:
            best = d
    return best or rows


def _dot(a, b):
    return jnp.dot(a, b, preferred_element_type=F32)


def _dot_nt(a, b):
    return lax.dot_general(a, b, (((1,), (1,)), ((), ())), preferred_element_type=F32)


def _dot_tn(a, b):
    return lax.dot_general(a, b, (((0,), (0,)), ((), ())), preferred_element_type=F32)


def _sigmoid(x):
    return 0.5 * jnp.tanh(0.5 * x) + 0.5


_GELU_C = math.sqrt(2.0 / math.pi)


def _gelu(x):
    x2 = x * x
    t = jnp.tanh(_GELU_C * (x + 0.044715 * x2 * x))
    val = 0.5 * x * (1.0 + t)
    grad = 0.5 * (1.0 + t) + 0.5 * x * (1.0 - t * t) * (_GELU_C * (1.0 + 3.0 * 0.044715 * x2))
    return val, grad


def _rstd(x):
    return lax.rsqrt(jnp.mean(x * x, axis=-1, keepdims=True) + EPS)


def _modulate(x, vec_ref):
    return (x * _rstd(x)) * vec_ref[0:1, :] * (1.0 + vec_ref[2:3, :]) + vec_ref[1:2, :]


def _modulate_bwd(x, dh, vec_ref, dvec_ref):
    gn, sh, sc = vec_ref[0:1, :], vec_ref[1:2, :], vec_ref[2:3, :]
    rstd = _rstd(x)
    r = x * rstd
    dvec_ref[0:1, :] += jnp.sum(dh * r * (1.0 + sc), axis=0, keepdims=True)
    dvec_ref[1:2, :] += jnp.sum(dh, axis=0, keepdims=True)
    dvec_ref[2:3, :] += jnp.sum(dh * r * gn, axis=0, keepdims=True)
    gm = gn * (1.0 + sc)
    dr = dh * gm
    dx = rstd * (dr - r * jnp.mean(dr * r, axis=-1, keepdims=True))
    return dx, r * gm + sh


def _adam(w, g, m, v):
    m = ADAM_B1 * m + (1.0 - ADAM_B1) * g
    v = ADAM_B2 * v + (1.0 - ADAM_B2) * (g * g)
    m_hat = m / (1.0 - ADAM_B1**ADAM_STEP)
    v_hat = v / (1.0 - ADAM_B2**ADAM_STEP)
    delta = -ADAM_LR * (m_hat / (jnp.sqrt(v_hat) + ADAM_EPS) + ADAM_WD * w)
    return delta, m, v


_ANY = pl.BlockSpec(memory_space=pl.ANY)


class _Phase:
    def __init__(self, ins, out_shapes, aliases, n_sems, start, finish, then):
        self.ins, self.out_shapes, self.aliases, self.n_sems = list(ins), list(out_shapes), dict(aliases), n_sems
        self.start, self.finish, self.then = start, finish, then


def _call(body, name, grid, in_specs, out_specs, out_shape, ins, scratch=(), prefetch=(), phases=(), in_place=None):
    n_pre, n_in, n_out, n_sc = len(prefetch), len(in_specs), len(out_specs), len(scratch)
    ph_in = [len(p.ins) for p in phases]
    ph_out = [len(p.out_shapes) for p in phases]

    def kernel_body(*refs):
        pos = [0]

        def take(k):
            pos[0] += k
            return refs[pos[0] - k : pos[0]]

        pre, ins_ = take(n_pre), take(n_in)
        p_ins = [take(k) for k in ph_in]
        outs_ = take(n_out)
        p_outs = [take(k) for k in ph_out]
        sc = take(n_sc)
        sems = [take(2) for _ in phases]
        if phases:
            ids = [pl.program_id(a) for a in range(len(grid))]
            first = functools.reduce(jnp.logical_and, [i == 0 for i in ids])
            last = functools.reduce(jnp.logical_and, [i == g - 1 for i, g in zip(ids, grid)])

            @pl.when(first)
            def _():
                for p, pi, po, (send, recv) in zip(phases, p_ins, p_outs, sems):
                    p.start(pi, po, send, recv)

        if body is not None:
            body(*pre, *ins_, *outs_, *sc)
        if phases:

            @pl.when(last)
            def _():
                for p, pi, po, (send, recv) in zip(phases, p_ins, p_outs, sems):
                    p.finish(pi, po, send, recv)

    aliases = {n_pre + i: o for i, o in (in_place or {}).items()}
    i0, o0 = n_pre + n_in, n_out
    for p in phases:
        for i, o in p.aliases.items():
            aliases[i0 + i] = o0 + o
        i0 += len(p.ins)
        o0 += len(p.out_shapes)
    all_in = list(in_specs) + [_ANY] * sum(ph_in)
    all_out = list(out_specs) + [_ANY] * sum(ph_out)
    all_scratch = list(scratch)
    for p in phases:
        all_scratch += [pltpu.SemaphoreType.DMA((p.n_sems,)), pltpu.SemaphoreType.DMA((p.n_sems,))]
    shapes = list(out_shape) + [s for p in phases for s in p.out_shapes]
    operands = list(prefetch) + list(ins) + [a for p in phases for a in p.ins]
    sem = ("arbitrary",) * len(grid)
    params = pltpu.CompilerParams(dimension_semantics=sem, vmem_limit_bytes=VMEM_LIMIT_BYTES)
    if n_pre:
        res = pl.pallas_call(
            kernel_body, name=name, out_shape=shapes, input_output_aliases=aliases, compiler_params=params,
            grid_spec=pltpu.PrefetchScalarGridSpec(
                num_scalar_prefetch=n_pre, grid=grid, in_specs=all_in, out_specs=all_out, scratch_shapes=all_scratch
            ),
        )(*operands)
    else:
        res = pl.pallas_call(
            kernel_body, name=name, grid=grid, in_specs=all_in, out_specs=all_out, out_shape=shapes,
            scratch_shapes=all_scratch, input_output_aliases=aliases, compiler_params=params,
        )(*operands)
    res = list(res)
    outs, rest = res[:n_out], res[n_out:]
    p_res = []
    for k in ph_out:
        p_res.append(rest[:k])
        rest = rest[k:]
    return outs, p_res


def _place():
    return lax.axis_index("x"), lax.axis_index("y"), lax.axis_index("c")


def _other_chips():
    x, y, _ = _place()
    return [(1 - x, y), (x, 1 - y), (1 - x, 1 - y)]


def _flip(k):
    x, y, c = _place()
    return (1 - x if k & 4 else x, 1 - y if k & 2 else y, 1 - c if k & 1 else c)


def _remote(src, dst, send, recv, k, to):
    return pltpu.make_async_remote_copy(
        src_ref=src, dst_ref=dst, send_sem=send.at[k], recv_sem=recv.at[k], device_id=to, device_id_type=MESH
    )


def _phase_small_gather(arrs, then):
    n = len(arrs)

    def copies(ins, outs, send, recv):
        x, y, c = _place()
        me = 4 * x + 2 * y + c
        local = [pltpu.make_async_copy(ins[a], outs[a].at[me], send.at[a * N_DEV]) for a in range(n)]
        remote = [_remote(ins[a], outs[a].at[me], send, recv, a * N_DEV + k, _flip(k)) for a in range(n) for k in range(1, N_DEV)]
        return local, remote

    def start(ins, outs, send, recv):
        local, remote = copies(ins, outs, send, recv)
        for cp in local + remote:
            cp.start()

    def finish(ins, outs, send, recv):
        local, remote = copies(ins, outs, send, recv)
        for cp in remote + local:
            cp.wait()

    shapes = [jax.ShapeDtypeStruct((N_DEV,) + a.shape, a.dtype) for a in arrs]
    return _Phase(arrs, shapes, {}, n * N_DEV, start, finish, then)


def _after(*arrs):
    nothing = lambda *args: None
    return _Phase(arrs, [], {}, 1, nothing, nothing, nothing)


def _flush(name, *phases):
    _, p_outs = _call(None, name, (1,), [], [], [], [], phases=list(phases))
    for p, po in zip(phases, p_outs):
        p.then(po)


class _Big:
    KINDS = {"full": (True, True), "half": (True, False), "shard": (False, True), "block": (False, False)}

    def __init__(self, f3, s3, h3):
        assert s3 != h3
        self.f3, self.s3, self.h3 = tuple(f3), s3, h3
        self.bd = tuple(f3[a] // (N_CHIPS if a == s3 else 1) // (2 if a == h3 else 1) for a in range(3))
        self.tile = (1, _row_tile(self.bd[1], self.bd[2]), self.bd[2])
        self.grid = tuple(self.bd[a] // self.tile[a] for a in range(3))

    def dims(self, kind):
        chips, halves = self.KINDS[kind]
        return tuple(
            self.bd[a] * (N_CHIPS if chips and a == self.s3 else 1) * (2 if halves and a == self.h3 else 1) for a in range(3)
        )

    def view(self, ref, chip=None, half=None, batch0=0, both_halves=True):
        start = [batch0, 0, 0]
        size = list(ref.shape)
        size[0] = self.bd[0] * (2 if self.h3 == 0 and both_halves else 1)
        if chip is not None:
            start[self.s3] += chip * self.bd[self.s3]
            size[self.s3] = self.bd[self.s3]
        if half is not None:
            start[self.h3] += half * self.bd[self.h3]
            size[self.h3] = self.bd[self.h3]
        return ref.at[tuple(pl.ds(st, sz) for st, sz in zip(start, size))]

    def spec(self, chip_from=None, half_from=None, lead=(), batch0=0):
        extra = "grid" in (chip_from, half_from)

        def index(*args):
            pref, idx = args[-1], list(args[int(extra) : -1])
            idx[0] += batch0
            if chip_from:
                idx[self.s3] += (pref[0] if chip_from == "pref" else args[0]) * self.grid[self.s3]
            if half_from:
                idx[self.h3] += (pref[1] if half_from == "pref" else args[0]) * self.grid[self.h3]
            return (0,) * len(lead) + tuple(idx)

        return pl.BlockSpec(tuple(lead) + self.tile, index)


def _same(arrs):
    return [jax.ShapeDtypeStruct(a.shape, a.dtype) for a in arrs]


def _phase_gather_ici(arrs, bigs, then):
    n = len(arrs)

    def copies(outs, send, recv, arriving):
        x, y, c = _place()
        return [
            _remote(blk, blk, send, recv, 3 * a + j, (*chip, c))
            for j, chip in enumerate(_other_chips())
            for a in range(n)
            for blk in [bigs[a].view(outs[a], 2 * chip[0] + chip[1] if arriving else 2 * x + y, c)]
        ]

    def start(ins, outs, send, recv):
        for cp in copies(outs, send, recv, False):
            cp.start()

    def finish(ins, outs, send, recv):
        for cp in copies(outs, send, recv, True):
            cp.wait_recv()
        for cp in copies(outs, send, recv, False):
            cp.wait_send()

    return _Phase(arrs, _same(arrs), {a: a for a in range(n)}, 3 * n, start, finish, then)


def _phase_gather_sibling(arrs, bigs, then):
    n = len(arrs)

    def copies(outs, send, recv, arriving):
        x, y, c = _place()
        return [
            _remote(blk, blk, send, recv, 3 * a + j, (x, y, 1 - c))
            for j, chip in enumerate(_other_chips())
            for a in range(n)
            for blk in [bigs[a].view(outs[a], 2 * chip[0] + chip[1], 1 - c if arriving else c)]
        ]

    def start(ins, outs, send, recv):
        for cp in copies(outs, send, recv, False):
            cp.start()

    def finish(ins, outs, send, recv):
        for cp in copies(outs, send, recv, True):
            cp.wait_recv()
        for cp in copies(outs, send, recv, False):
            cp.wait_send()

    return _Phase(arrs, _same(arrs), {a: a for a in range(n)}, 3 * n, start, finish, then)


def _phase_pair_exchange(grads, bigs, then):
    n = len(grads)

    def copies(ins, outs, send, recv):
        x, y, c = _place()
        srcs = [ins[a] if ins[a].shape == outs[a].shape else bigs[a].view(ins[a], None, 1 - c) for a in range(n)]
        return [_remote(srcs[a], outs[a], send, recv, a, (x, y, 1 - c)) for a in range(n)]

    def start(ins, outs, send, recv):
        for cp in copies(ins, outs, send, recv):
            cp.start()

    def finish(ins, outs, send, recv):
        for cp in copies(ins, outs, send, recv):
            cp.wait()

    shapes = [jax.ShapeDtypeStruct(b.dims("half"), BF16) for b in bigs]
    return _Phase(grads, shapes, {}, n, start, finish, then)


def _phase_chip_exchange(sums, bigs, then):
    n = len(sums)

    def copies(ins, outs, send, recv):
        _, _, c = _place()
        return [
            _remote(bigs[a].view(ins[a], 2 * chip[0] + chip[1], both_halves=False), outs[a].at[j], send, recv, 3 * a + j, (*chip, c))
            for j, chip in enumerate(_other_chips())
            for a in range(n)
        ]

    def start(ins, outs, send, recv):
        for cp in copies(ins, outs, send, recv):
            cp.start()

    def finish(ins, outs, send, recv):
        for cp in copies(ins, outs, send, recv):
            cp.wait()

    shapes = [jax.ShapeDtypeStruct((N_CHIPS - 1,) + b.dims("block"), BF16) for b in bigs]
    return _Phase(sums, shapes, {}, 3 * n, start, finish, then)


_HBM = pl.BlockSpec(memory_space=pltpu.HBM)
_SEM = pl.BlockSpec(memory_space=pltpu.SEMAPHORE)
_DATAFLOW = pltpu.SideEffectType.DATAFLOW_SIDE_EFFECTING


class _InFlight:
    def __init__(self, phase, send, recv, arrays, token):
        self.phase, self.send, self.recv, self.arrays, self.token = phase, send, recv, arrays, token


def _phase_results(phase, refs):
    n_in = len(phase.ins)
    updated = {o: i for i, o in phase.aliases.items()}
    fresh = [o for o in range(len(phase.out_shapes)) if o not in updated]
    return [refs[updated[o]] if o in updated else refs[n_in + fresh.index(o)] for o in range(len(phase.out_shapes))]


def _split_start(phase, name):
    n_in = len(phase.ins)
    fresh = [s for o, s in enumerate(phase.out_shapes) if o not in phase.aliases.values()]
    arrays = list(phase.ins) + [lax.empty(s.shape, s.dtype) for s in fresh]
    n = len(arrays)

    def body(*refs):
        phase.start(refs[:n_in], _phase_results(phase, refs[:n]), refs[n], refs[n + 1])
        refs[-1][...] = jnp.zeros_like(refs[-1])

    operands = [pltpu.with_memory_space_constraint(a, pltpu.HBM) for a in arrays]
    res = pl.pallas_call(
        body, name=name,
        out_shape=[pltpu.SemaphoreType.DMA((phase.n_sems,)), pltpu.SemaphoreType.DMA((phase.n_sems,))]
        + [pltpu.HBM(a.shape, a.dtype) for a in arrays] + [jax.ShapeDtypeStruct((8, 128), F32)],
        in_specs=[_HBM] * n, out_specs=[_SEM, _SEM] + [_HBM] * n + [pl.BlockSpec(memory_space=pltpu.VMEM)],
        input_output_aliases={i: 2 + i for i in range(n)},
        compiler_params=pltpu.CompilerParams(has_side_effects=_DATAFLOW),
    )(*operands)
    return _InFlight(phase, res[0], res[1], list(res[2 : 2 + n]), res[-1])


def _split_wait(flight, after, name):
    phase, n = flight.phase, len(flight.arrays)
    n_in = len(phase.ins)

    def body(*refs):
        phase.finish(refs[:n_in], _phase_results(phase, refs[:n]), refs[n], refs[n + 1])

    res = pl.pallas_call(
        body, name=name, out_shape=[pltpu.HBM(a.shape, a.dtype) for a in flight.arrays],
        in_specs=[_HBM] * n + [_SEM, _SEM] + [_ANY] * len(after), out_specs=[_HBM] * n,
        input_output_aliases={i: i for i in range(n)},
        compiler_params=pltpu.CompilerParams(has_side_effects=_DATAFLOW),
    )(*flight.arrays, flight.send, flight.recv, *after)
    res = list(res)
    phase.then(_phase_results(phase, res))
    return res[:n_in]


def _phase_pair_broadcast(stacks, bigs, batch0s, then):
    n = len(stacks)

    def start(ins, outs, send, recv):
        x, y, c = _place()
        for a in range(n):
            blk = bigs[a].view(outs[a], None, c, batch0s[a])
            _remote(blk, blk, send, recv, a, (x, y, 1 - c)).start()

    def finish(ins, outs, send, recv):
        x, y, c = _place()
        for a in range(n):
            mine = bigs[a].view(outs[a], None, c, batch0s[a])
            theirs = bigs[a].view(outs[a], None, 1 - c, batch0s[a])
            _remote(mine, mine, send, recv, a, (x, y, 1 - c)).wait_send()
            _remote(theirs, theirs, send, recv, a, (x, y, 1 - c)).wait_recv()

    return _Phase(stacks, _same(stacks), {a: a for a in range(n)}, n, start, finish, then)


def _tile_call(body, name, big, where, extra, ins, in_specs, out_specs, out_shape, phases=()):
    grid = ((extra,) if extra else ()) + big.grid
    return _call(body, name, grid, in_specs, out_specs, out_shape, ins, prefetch=(where,), phases=phases)


def _cast_into_full(w_stack, batch0, big, where, name, phases=()):
    def body(_, w_ref, o_ref):
        o_ref[...] = w_ref[...].astype(BF16)

    return _tile_call(
        body, name, big, where, 2, [w_stack], [big.spec(None, "grid", batch0=batch0)], [big.spec("pref", "grid")],
        [jax.ShapeDtypeStruct(big.dims("full"), BF16)], phases,
    )


def _pair_sum(g_full, recv_half, big, where, name, phases=()):
    def body(_, g_ref, r_ref, o_ref):
        o_ref[...] = (g_ref[...].astype(F32) + r_ref[...].astype(F32)).astype(BF16)

    half = big.spec("grid", None)
    return _tile_call(
        body, name, big, where, N_CHIPS, [g_full, recv_half], [big.spec("grid", "pref"), half], [half],
        [jax.ShapeDtypeStruct(big.dims("half"), BF16)], phases,
    )


def _chip_sum(chip_sum, parts, big, where, stack, stack_shape, batch0, name, phases=()):
    def body(_, own_ref, p_ref, *rest):
        acc = own_ref[...].astype(F32)
        for k in range(N_CHIPS - 1):
            acc = acc + p_ref[k].astype(F32)
        rest[-1][...] = acc

    ins = [chip_sum, parts] + ([stack] if stack is not None else [])
    in_specs = [big.spec("pref", None), big.spec(None, None, lead=(N_CHIPS - 1,))] + ([_ANY] if stack is not None else [])
    return _call(
        body, name, big.grid, in_specs, [big.spec(None, "pref", batch0=batch0)], [jax.ShapeDtypeStruct(stack_shape, F32)], ins,
        prefetch=(where,), phases=phases, in_place={2: 0} if stack is not None else None,
    )


def _adam_stack(w, g, m, v, name, after=()):
    b, r, c = w.shape
    tr = _row_tile(r, c)

    def body(w_ref, g_ref, m_ref, v_ref, *rest):
        go_ref, d_ref, mo_ref, vo_ref = rest[-4:]
        gv = g_ref[...]
        d, mo, vo = _adam(w_ref[...], gv, m_ref[...], v_ref[...])
        go_ref[...] = gv
        d_ref[...] = d
        mo_ref[...] = mo
        vo_ref[...] = vo

    spec = pl.BlockSpec((1, tr, c), lambda bb, i: (bb, i, 0))
    outs, _ = _call(
        body, name, (b, r // tr), [spec] * 4 + [_ANY] * len(after), [spec] * 4, [jax.ShapeDtypeStruct(w.shape, F32)] * 4,
        [w, g, m, v, *after],
    )
    return outs


def _mod_fwd(c_all, w_mod, b_cols, phases=()):
    n_layers, d, n = w_mod.shape
    tn = _pick(n, (768, 512, 384, 256, 128))

    def body(c_ref, w_ref, b_ref, o_ref):
        cv = c_ref[...]
        ca = (cv * _sigmoid(cv)).astype(BF16)
        o_ref[0] = _dot(ca, w_ref[0].astype(BF16)) + b_ref[0]

    return _call(
        body, "mod_fwd", (n_layers, n // tn),
        [
            pl.BlockSpec((N_DEV, d), lambda l, j: (0, 0)),
            pl.BlockSpec((1, d, tn), lambda l, j: (l, 0, j)),
            pl.BlockSpec((1, 1, tn), lambda l, j: (l, 0, j)),
        ],
        [pl.BlockSpec((1, N_DEV, tn), lambda l, j: (l, 0, j))],
        [jax.ShapeDtypeStruct((n_layers, N_DEV, n), F32)], [c_all, w_mod, b_cols], phases=phases,
    )


def _mod_bwd_adam(c_all_t, dmod_cols, w, m, v, after=()):
    n_layers, d, n = w.shape
    tn = _pick(n, (384, 256, 128))

    def body(c_ref, dm_ref, w_ref, m_ref, v_ref, *rest):
        g_ref, d_ref, mo_ref, vo_ref = rest[-4:]
        cv = c_ref[...]
        ca = (cv * _sigmoid(cv)).astype(BF16)
        g = _dot(ca, dm_ref[0].astype(BF16))
        g_ref[0] = g
        dl, mo, vo = _adam(w_ref[0], g, m_ref[0], v_ref[0])
        d_ref[0] = dl
        mo_ref[0] = mo
        vo_ref[0] = vo

    wspec = pl.BlockSpec((1, d, tn), lambda l, j: (l, 0, j))
    outs, _ = _call(
        body, "mod_bwd_adam", (n_layers, n // tn),
        [pl.BlockSpec((d, N_DEV), lambda l, j: (0, 0)), pl.BlockSpec((1, N_DEV, tn), lambda l, j: (l, 0, j)), wspec, wspec, wspec]
        + [_ANY] * len(after),
        [wspec] * 4, [jax.ShapeDtypeStruct(w.shape, F32)] * 4, [c_all_t, dmod_cols, w, m, v, *after],
    )
    return outs


def _ffn_fwd(x, vec, w_in, w_out, name, phases=()):
    s, d = x.shape
    f = w_out.shape[1]
    tm = _pick(s, (1024, 512, 256, 128))
    tf = _pick(f, (256, 128))
    nf = f // tf

    def body(x_ref, vec_ref, wg_ref, wu_ref, wo_ref, xo_ref, g_ref, u_ref, y_ref, h_sc, acc_sc):
        j = pl.program_id(1)

        @pl.when(j == 0)
        def _():
            h_sc[...] = _modulate(x_ref[...], vec_ref).astype(BF16)
            acc_sc[...] = jnp.zeros_like(acc_sc)

        h = h_sc[...]
        g = _dot(h, wg_ref[0])
        u = _dot(h, wu_ref[0])
        g_ref[...] = g.astype(BF16)
        u_ref[...] = u.astype(BF16)
        a = (g * _sigmoid(g) * u).astype(BF16)
        acc_sc[...] += _dot(a, wo_ref[0])

        @pl.when(j == nf - 1)
        def _():
            yv = acc_sc[...]
            xo_ref[...] = x_ref[...] + 0.5 * vec_ref[3:4, :] * yv
            y_ref[...] = yv.astype(BF16)

    row = pl.BlockSpec((tm, d), lambda i, j: (i, 0))
    hid = pl.BlockSpec((tm, tf), lambda i, j: (i, j))
    return _call(
        body, name, (s // tm, nf),
        [
            row,
            pl.BlockSpec((8, d), lambda i, j: (0, 0)),
            pl.BlockSpec((1, d, tf), lambda i, j: (0, 0, j)),
            pl.BlockSpec((1, d, tf), lambda i, j: (0, 0, nf + j)),
            pl.BlockSpec((1, tf, d), lambda i, j: (0, j, 0)),
        ],
        [row, hid, hid, row],
        [
            jax.ShapeDtypeStruct((s, d), F32),
            jax.ShapeDtypeStruct((s, f), BF16),
            jax.ShapeDtypeStruct((s, f), BF16),
            jax.ShapeDtypeStruct((s, d), BF16),
        ],
        [x, vec, w_in, w_in, w_out],
        scratch=[pltpu.VMEM((tm, d), BF16), pltpu.VMEM((tm, d), F32)], phases=phases,
    )


def _ffn_bwd(dxo, x, vec, gg, uu, y, w_in, w_out, name, phases=()):
    s, d = x.shape
    f = w_out.shape[1]
    tm = _pick(s, (512, 256, 128))
    tf = _pick(f, (256, 128))
    nf = f // tf

    def body(dxo_ref, x_ref, vec_ref, g_ref, u_ref, y_ref, wg_ref, wu_ref, wo_ref,
             dx_ref, dg_ref, du_ref, a_ref, h_ref, dy_ref, dvec_ref, acc_sc):
        i, j = pl.program_id(0), pl.program_id(1)

        @pl.when((i == 0) & (j == 0))
        def _():
            dvec_ref[...] = jnp.zeros_like(dvec_ref)

        @pl.when(j == 0)
        def _():
            dxo_v = dxo_ref[...]
            dy_ref[...] = (0.5 * vec_ref[3:4, :] * dxo_v).astype(BF16)
            dvec_ref[3:4, :] += 0.5 * jnp.sum(dxo_v * y_ref[...].astype(F32), axis=0, keepdims=True)
            acc_sc[...] = jnp.zeros_like(acc_sc)

        da = _dot_nt(dy_ref[...], wo_ref[0])
        g = g_ref[...].astype(F32)
        u = u_ref[...].astype(F32)
        sig = _sigmoid(g)
        sl = g * sig
        a_ref[...] = (sl * u).astype(BF16)
        dg = (da * u * (sig * (1.0 + g * (1.0 - sig)))).astype(BF16)
        du = (da * sl).astype(BF16)
        dg_ref[...] = dg
        du_ref[...] = du
        acc_sc[...] += _dot_nt(dg, wg_ref[0]) + _dot_nt(du, wu_ref[0])

        @pl.when(j == nf - 1)
        def _():
            dx, h = _modulate_bwd(x_ref[...], acc_sc[...], vec_ref, dvec_ref)
            dx_ref[...] = dxo_ref[...] + dx
            h_ref[...] = h.astype(BF16)

    row = pl.BlockSpec((tm, d), lambda i, j: (i, 0))
    hid = pl.BlockSpec((tm, tf), lambda i, j: (i, j))
    vecs = pl.BlockSpec((8, d), lambda i, j: (0, 0))
    return _call(
        body, name, (s // tm, nf),
        [
            row, row, vecs, hid, hid, row,
            pl.BlockSpec((1, d, tf), lambda i, j: (0, 0, j)),
            pl.BlockSpec((1, d, tf), lambda i, j: (0, 0, nf + j)),
            pl.BlockSpec((1, tf, d), lambda i, j: (0, j, 0)),
        ],
        [row, hid, hid, hid, row, row, vecs],
        [
            jax.ShapeDtypeStruct((s, d), F32),
            jax.ShapeDtypeStruct((s, f), BF16),
            jax.ShapeDtypeStruct((s, f), BF16),
            jax.ShapeDtypeStruct((s, f), BF16),
            jax.ShapeDtypeStruct((s, d), BF16),
            jax.ShapeDtypeStruct((s, d), BF16),
            jax.ShapeDtypeStruct((8, d), F32),
        ],
        [dxo, x, vec, gg, uu, y, w_in, w_in, w_out],
        scratch=[pltpu.VMEM((tm, d), F32)], phases=phases,
    )


def _grad_half(a, b, big, where, mine, col0, prev, recv, name, phases=()):
    s, k1 = a.shape
    b, b_part = b if isinstance(b, tuple) else (b[None], 0)
    n = b.shape[2]
    rows_halved = big.h3 == 1
    kk, nn = (k1 // 2, n) if rows_halved else (k1, n // 2)
    tk = _pick(kk, (1408, 1024, 512, 256, 128))
    tn = _pick(nn, (1408, 1024, 640, 512, 256, 128))
    nkb, nnb = kk // tk, nn // tn
    assert col0 % tn == 0 and (recv is None) == (not mine)

    def half(pref):
        return pref[1] if mine else 1 - pref[1]

    def body(_, a_ref, b_ref, *rest):
        acc = _dot_tn(a_ref[...], b_ref[0])
        if recv is not None:
            acc = acc + rest[0][0].astype(F32)
        rest[-1][0] = acc.astype(BF16)

    out_spec = pl.BlockSpec((1, tk, tn), lambda i, j, pref: (0, i, col0 // tn + j))
    in_specs = [
        pl.BlockSpec((s, tk), lambda i, j, pref: (0, i + (half(pref) * nkb if rows_halved else 0))),
        pl.BlockSpec((1, s, tn), lambda i, j, pref: (b_part, 0, j + (0 if rows_halved else half(pref) * nnb))),
    ]
    ins = [a, b]
    if recv is not None:
        in_specs.append(out_spec)
        ins.append(recv)
    in_place = None
    if prev is not None:
        in_place = {len(ins): 0}
        in_specs.append(_ANY)
        ins.append(prev)
    return _call(
        body, name, (nkb, nnb), in_specs, [out_spec], [jax.ShapeDtypeStruct(big.dims("half"), BF16)], ins,
        prefetch=(where,), phases=phases, in_place=in_place,
    )


def _proj_mod_fwd(x, vec, w, phases=()):
    s, d = x.shape
    n = w.shape[2]
    tm = _pick(s, (512, 256, 128))
    tn = _pick(n, (640, 512, 256, 128))

    def body(x_ref, vec_ref, w_ref, o_ref, h_sc):
        @pl.when(pl.program_id(1) == 0)
        def _():
            h_sc[...] = _modulate(x_ref[...], vec_ref).astype(BF16)

        o_ref[...] = _dot(h_sc[...], w_ref[0])

    return _call(
        body, "ab_in_fwd", (s // tm, n // tn),
        [
            pl.BlockSpec((tm, d), lambda i, j: (i, 0)),
            pl.BlockSpec((8, d), lambda i, j: (0, 0)),
            pl.BlockSpec((1, d, tn), lambda i, j: (0, 0, j)),
        ],
        [pl.BlockSpec((tm, tn), lambda i, j: (i, j))],
        [jax.ShapeDtypeStruct((s, n), F32)], [x, vec, w],
        scratch=[pltpu.VMEM((tm, d), BF16)], phases=phases,
    )


def _proj_res_fwd(a, w, x, vec, phases=()):
    s, kd = a.shape
    d = x.shape[1]
    tm = _pick(s, (512, 256, 128))

    def body(a_ref, w_ref, x_ref, vec_ref, xo_ref, y_ref):
        yv = _dot(a_ref[...], w_ref[0])
        xo_ref[...] = x_ref[...] + vec_ref[3:4, :] * yv
        y_ref[...] = yv.astype(BF16)

    row = pl.BlockSpec((tm, d), lambda i: (i, 0))
    return _call(
        body, "ab_out_fwd", (s // tm,),
        [pl.BlockSpec((tm, kd), lambda i: (i, 0)), pl.BlockSpec((1, kd, d), lambda i: (0, 0, 0)), row, pl.BlockSpec((8, d), lambda i: (0, 0))],
        [row, row],
        [jax.ShapeDtypeStruct((s, d), F32), jax.ShapeDtypeStruct((s, d), BF16)], [a, w, x, vec], phases=phases,
    )


def _proj_res_bwd(dxo, y, vec, w, phases=()):
    s, d = dxo.shape
    kd = w.shape[1]
    tm = _pick(s, (512, 256, 128))

    def body(dxo_ref, y_ref, vec_ref, w_ref, dy_ref, da_ref, dgate_ref):
        @pl.when(pl.program_id(0) == 0)
        def _():
            dgate_ref[...] = jnp.zeros_like(dgate_ref)

        dxo_v = dxo_ref[...]
        dy = (vec_ref[3:4, :] * dxo_v).astype(BF16)
        dy_ref[...] = dy
        dgate_ref[3:4, :] += jnp.sum(dxo_v * y_ref[...].astype(F32), axis=0, keepdims=True)
        da_ref[...] = _dot_nt(dy, w_ref[0]).astype(BF16)

    row = pl.BlockSpec((tm, d), lambda i: (i, 0))
    vecs = pl.BlockSpec((8, d), lambda i: (0, 0))
    return _call(
        body, "ab_out_bwd", (s // tm,),
        [row, row, vecs, pl.BlockSpec((1, kd, d), lambda i: (0, 0, 0))],
        [row, pl.BlockSpec((tm, kd), lambda i: (i, 0)), vecs],
        [jax.ShapeDtypeStruct((s, d), BF16), jax.ShapeDtypeStruct((s, kd), BF16), jax.ShapeDtypeStruct((8, d), F32)],
        [dxo, y, vec, w], phases=phases,
    )


def _proj_mod_bwd(dproj, w, x, vec, dxo, dvec_in, name, phases=()):
    parts, s, n_part = dproj.shape
    d = x.shape[1]
    tm = _pick(s, (512, 256, 128))
    tk = _pick(n_part, (1408, 1280, 1024, 512, 256, 128))
    per_part = n_part // tk
    nk = parts * per_part

    def body(dp_ref, w_ref, x_ref, vec_ref, dxo_ref, dvi_ref, dx_ref, h_ref, dvec_ref, acc_sc):
        i, k = pl.program_id(0), pl.program_id(1)

        @pl.when((i == 0) & (k == 0))
        def _():
            dvec_ref[...] = dvi_ref[...]

        @pl.when(k == 0)
        def _():
            acc_sc[...] = jnp.zeros_like(acc_sc)

        acc_sc[...] += _dot_nt(dp_ref[0], w_ref[0])

        @pl.when(k == nk - 1)
        def _():
            dx, h = _modulate_bwd(x_ref[...], acc_sc[...], vec_ref, dvec_ref)
            dx_ref[...] = dxo_ref[...] + dx
            h_ref[...] = h.astype(BF16)

    row = pl.BlockSpec((tm, d), lambda i, k: (i, 0))
    vecs = pl.BlockSpec((8, d), lambda i, k: (0, 0))
    return _call(
        body, name, (s // tm, nk),
        [
            pl.BlockSpec((1, tm, tk), lambda i, k: (k // per_part, i, k % per_part)),
            pl.BlockSpec((1, d, tk), lambda i, k: (0, 0, k)),
            row, vecs, row, vecs,
        ],
        [row, row, vecs],
        [jax.ShapeDtypeStruct((s, d), F32), jax.ShapeDtypeStruct((s, d), BF16), jax.ShapeDtypeStruct((8, d), F32)],
        [dproj, w, x, vec, dxo, dvec_in], scratch=[pltpu.VMEM((tm, d), F32)], phases=phases,
    )


def _tril(n):
    return lax.broadcasted_iota(jnp.int32, (n, n), 0) >= lax.broadcasted_iota(jnp.int32, (n, n), 1)


def _layernorm_stats(gv):
    mu = jnp.mean(gv, axis=-1, keepdims=True)
    cen = gv - mu
    rstd = lax.rsqrt(jnp.mean(cen * cen, axis=-1, keepdims=True) + EPS)
    return cen * rstd, rstd


def _shift_down(q, k, above_ref, c_cg, c_xb, first):
    width = q.shape[1]
    rows = lax.broadcasted_iota(jnp.int32, q.shape, 0)
    out = pltpu.roll(q, k, 0)
    for r in range(k):
        src = CONV_HALO - k + r
        above = above_ref[src : src + 1, c_cg : c_cg + width] * above_ref[src : src + 1, c_xb : c_xb + width]
        above = jnp.where(first, 0.0, above)
        out = jnp.where(rows == r, above, out)
    return out


def _ab_mix_fwd(proj, norm_v, w_s, b_rows, conv_w, phases=()):
    s, n = proj.shape
    heads, chunk, _ = w_s.shape
    da = norm_v.shape[1]
    hd = da // heads
    db = conv_w.shape[1]
    tm = _pick(s, (512, 256, 128))

    def body(p_ref, ph_ref, nv_ref, ws_ref, b_ref, cw_ref, o_ref):
        first = pl.program_id(0) == 0
        gu, _ = _gelu(p_ref[:, 0:da])
        gv, _ = _gelu(p_ref[:, da : 2 * da])
        xhat, _ = _layernorm_stats(gv)
        vn = (xhat * nv_ref[...]).astype(BF16)
        mask = _tril(chunk)
        for hh in range(heads):
            wm = jnp.where(mask, ws_ref[hh], 0.0).astype(BF16)
            cols = slice(hh * hd, (hh + 1) * hd)
            for nn in range(tm // chunk):
                rows = slice(nn * chunk, (nn + 1) * chunk)
                z = _dot(wm, vn[rows, cols]) + b_ref[:, cols]
                o_ref[rows, cols] = (gu[rows, cols] * z).astype(BF16)
        c_cg, c_xb = 2 * da + db, 2 * da + 2 * db
        bg = p_ref[:, 2 * da : 2 * da + db]
        q = p_ref[:, c_cg : c_cg + db] * p_ref[:, c_xb : c_xb + db]
        q1 = _shift_down(q, 1, ph_ref, c_cg, c_xb, first)
        q2 = _shift_down(q, 2, ph_ref, c_cg, c_xb, first)
        conv = cw_ref[0:1, :] * q2 + cw_ref[1:2, :] * q1 + cw_ref[2:3, :] * q
        o_ref[:, da : da + db] = (bg * conv).astype(BF16)

    nh = tm // CONV_HALO
    return _call(
        body, "ab_mix_fwd", (s // tm,),
        [
            pl.BlockSpec((tm, n), lambda i: (i, 0)),
            pl.BlockSpec((CONV_HALO, n), lambda i: (jnp.maximum(i * nh - 1, 0), 0)),
            pl.BlockSpec((1, da), lambda i: (0, 0)),
            pl.BlockSpec((heads, chunk, chunk), lambda i: (0, 0, 0)),
            pl.BlockSpec((chunk, da), lambda i: (0, 0)),
            pl.BlockSpec((3, db), lambda i: (0, 0)),
        ],
        [pl.BlockSpec((tm, da + db), lambda i: (i, 0))],
        [jax.ShapeDtypeStruct((s, da + db), BF16)], [proj, proj, norm_v, w_s, b_rows, conv_w], phases=phases,
    )


def _ab_mix_bwd(proj, dcat, norm_v, w_s, b_rows, conv_w, phases=()):
    s, n = proj.shape
    heads, chunk, _ = w_s.shape
    da = norm_v.shape[1]
    hd = da // heads
    db = conv_w.shape[1]
    tm = _pick(s, (512, 256, 128))
    nblk = s // tm
    dhalo = 2 * CONV_HALO

    def body(p_ref, pa_ref, pb_ref, dc_ref, dcb_ref, nv_ref, ws_ref, b_ref, cw_ref,
             dp_ref, dnv_ref, dws_ref, dzs_ref, dcw_ref, dvn_sc):
        i = pl.program_id(0)
        first, last = i == 0, i == nblk - 1

        @pl.when(first)
        def _():
            dnv_ref[...] = jnp.zeros_like(dnv_ref)
            dws_ref[...] = jnp.zeros_like(dws_ref)
            dzs_ref[...] = jnp.zeros_like(dzs_ref)
            dcw_ref[...] = jnp.zeros_like(dcw_ref)

        uu = p_ref[:, 0:da]
        gu, gu_grad = _gelu(uu)
        gv, gv_grad = _gelu(p_ref[:, da : 2 * da])
        xhat, rstd = _layernorm_stats(gv)
        nv = nv_ref[...]
        vn = (xhat * nv).astype(BF16)
        dya = dc_ref[:, 0:da].astype(F32)
        dz = (dya * gu).astype(BF16)
        mask = _tril(chunk)
        for hh in range(heads):
            wm = jnp.where(mask, ws_ref[hh], 0.0).astype(BF16)
            cols = slice(hh * hd, (hh + 1) * hd)
            dws = jnp.zeros((chunk, chunk), F32)
            for nn in range(tm // chunk):
                rows = slice(nn * chunk, (nn + 1) * chunk)
                z = _dot(wm, vn[rows, cols]) + b_ref[:, cols]
                dp_ref[rows, cols] = (dya[rows, cols] * z * gu_grad[rows, cols]).astype(BF16)
                dz_blk = dz[rows, cols]
                dws = dws + _dot_nt(dz_blk, vn[rows, cols])
                dzs_ref[:, cols] += dz_blk.astype(F32)
                dvn = _dot_tn(wm, dz_blk)
                dnv_ref[:, cols] += jnp.sum(dvn * xhat[rows, cols], axis=0, keepdims=True)
                dvn_sc[rows, cols] = dvn
            dws_ref[hh] += jnp.where(mask, dws, 0.0)
        dxhat = dvn_sc[...] * nv
        dgv = rstd * (dxhat - jnp.mean(dxhat, axis=-1, keepdims=True) - xhat * jnp.mean(dxhat * xhat, axis=-1, keepdims=True))
        dp_ref[:, da : 2 * da] = (dgv * gv_grad).astype(BF16)

        c_bg, c_cg, c_xb = 2 * da, 2 * da + db, 2 * da + 2 * db
        bg = p_ref[:, c_bg : c_bg + db]
        cg = p_ref[:, c_cg : c_cg + db]
        xb = p_ref[:, c_xb : c_xb + db]
        q = cg * xb
        q1 = _shift_down(q, 1, pa_ref, c_cg, c_xb, first)
        q2 = _shift_down(q, 2, pa_ref, c_cg, c_xb, first)
        dyb = dc_ref[:, da : da + db].astype(F32)
        conv = cw_ref[0:1, :] * q2 + cw_ref[1:2, :] * q1 + cw_ref[2:3, :] * q
        dp_ref[:, c_bg : c_bg + db] = (dyb * conv).astype(BF16)
        e = dyb * bg
        dcw_ref[0:1, :] += jnp.sum(e * q2, axis=0, keepdims=True)
        dcw_ref[1:2, :] += jnp.sum(e * q1, axis=0, keepdims=True)
        dcw_ref[2:3, :] += jnp.sum(e * q, axis=0, keepdims=True)
        rows = lax.broadcasted_iota(jnp.int32, e.shape, 0)
        dq = cw_ref[2:3, :] * e
        for kk in (1, 2):
            ek = pltpu.roll(e, tm - kk, 0)
            for r in range(kk):
                below = dcb_ref[r : r + 1, da : da + db].astype(F32) * pb_ref[r : r + 1, c_bg : c_bg + db]
                below = jnp.where(last, 0.0, below)
                ek = jnp.where(rows == tm - kk + r, below, ek)
            dq = dq + cw_ref[2 - kk : 3 - kk, :] * ek
        dp_ref[:, c_cg : c_cg + db] = (dq * xb).astype(BF16)
        dp_ref[:, c_xb : c_xb + db] = (dq * cg).astype(BF16)

    nh = tm // CONV_HALO
    nhb = tm // dhalo
    const2 = lambda i: (0, 0)
    return _call(
        body, "ab_mix_bwd", (nblk,),
        [
            pl.BlockSpec((tm, n), lambda i: (i, 0)),
            pl.BlockSpec((CONV_HALO, n), lambda i: (jnp.maximum(i * nh - 1, 0), 0)),
            pl.BlockSpec((CONV_HALO, n), lambda i: (jnp.minimum((i + 1) * nh, s // CONV_HALO - 1), 0)),
            pl.BlockSpec((tm, da + db), lambda i: (i, 0)),
            pl.BlockSpec((dhalo, da + db), lambda i: (jnp.minimum((i + 1) * nhb, s // dhalo - 1), 0)),
            pl.BlockSpec((1, da), const2),
            pl.BlockSpec((heads, chunk, chunk), lambda i: (0, 0, 0)),
            pl.BlockSpec((chunk, da), const2),
            pl.BlockSpec((3, db), const2),
        ],
        [
            pl.BlockSpec((tm, n), lambda i: (i, 0)),
            pl.BlockSpec((1, da), const2),
            pl.BlockSpec((heads, chunk, chunk), lambda i: (0, 0, 0)),
            pl.BlockSpec((chunk, da), const2),
            pl.BlockSpec((3, db), const2),
        ],
        [
            jax.ShapeDtypeStruct((s, n), BF16),
            jax.ShapeDtypeStruct((1, da), F32),
            jax.ShapeDtypeStruct((heads, chunk, chunk), F32),
            jax.ShapeDtypeStruct((chunk, da), F32),
            jax.ShapeDtypeStruct((3, db), F32),
        ],
        [proj, proj, proj, dcat, dcat, norm_v, w_s, b_rows, conv_w],
        scratch=[pltpu.VMEM((tm, da), F32)], phases=phases,
    )


def _pool_counts(tm, i, w):
    t = i * tm + lax.broadcasted_iota(jnp.int32, (tm, 1), 0)
    return jnp.minimum(t + 1, w).astype(F32)


def _pool_fwd(x, vec, w_grp, scale, phases=()):
    s, d = x.shape
    groups, gd, _ = w_grp.shape
    tm = _pick(s, (512, 256, 128))

    def body(x_ref, xa_ref, vec_ref, w_ref, sc_ref, xo_ref, p_ref, o_ref):
        i = pl.program_id(0)
        h = _modulate(x_ref[...], vec_ref)
        ha = jnp.where(i == 0, 0.0, _modulate(xa_ref[...], vec_ref))
        ext = jnp.concatenate([ha, h], axis=0)
        for gi, w in enumerate(POOL_WINDOWS):
            cols = slice(gi * gd, (gi + 1) * gd)
            acc = ext[:, cols]
            step = 1
            while step < w:
                acc = acc + pltpu.roll(acc, step, 0)
                step *= 2
            p = (acc[POOL_HALO:, :] / _pool_counts(tm, i, w) - h[:, cols]).astype(BF16)
            p_ref[:, cols] = p
            o_ref[:, cols] = _dot(p, w_ref[gi]).astype(BF16)
        xo_ref[...] = x_ref[...] + vec_ref[3:4, :] * (o_ref[...].astype(F32) * sc_ref[...])

    nh = tm // POOL_HALO
    row = pl.BlockSpec((tm, d), lambda i: (i, 0))
    return _call(
        body, "pool_fwd", (s // tm,),
        [
            row,
            pl.BlockSpec((POOL_HALO, d), lambda i: (jnp.maximum(i * nh - 1, 0), 0)),
            pl.BlockSpec((8, d), lambda i: (0, 0)),
            pl.BlockSpec((groups, gd, gd), lambda i: (0, 0, 0)),
            pl.BlockSpec((1, d), lambda i: (0, 0)),
        ],
        [row, row, row],
        [jax.ShapeDtypeStruct((s, d), F32), jax.ShapeDtypeStruct((s, d), BF16), jax.ShapeDtypeStruct((s, d), BF16)],
        [x, x, vec, w_grp, scale], phases=phases,
    )


def _pool_bwd(dxo, x, vec, p, o, w_grp, scale, phases=()):
    s, d = x.shape
    groups, gd, _ = w_grp.shape
    tm = _pick(s, (512, 256, 128))
    nblk = s // tm

    def body(dxo_ref, dxb_ref, x_ref, vec_ref, p_ref, o_ref, w_ref, sc_ref, dx_ref, dw_ref, dsc_ref, dvec_ref, dw_sc):
        i = pl.program_id(0)

        @pl.when(i == 0)
        def _():
            dw_sc[...] = jnp.zeros_like(dw_sc)
            dsc_ref[...] = jnp.zeros_like(dsc_ref)
            dvec_ref[...] = jnp.zeros_like(dvec_ref)

        gate, sc = vec_ref[3:4, :], sc_ref[...]
        dxo_v = dxo_ref[...]
        ov = o_ref[...].astype(F32)
        dvec_ref[3:4, :] += jnp.sum(dxo_v * (ov * sc), axis=0, keepdims=True)
        dy = gate * dxo_v
        dsc_ref[...] += jnp.sum(dy * ov, axis=0, keepdims=True)
        dout = (dy * sc).astype(BF16)
        dout_b = jnp.where(i == nblk - 1, 0.0, gate * dxb_ref[...] * sc).astype(BF16)
        for gi, w in enumerate(POOL_WINDOWS):
            cols = slice(gi * gd, (gi + 1) * gd)
            dw_sc[gi] += _dot_tn(p_ref[:, cols], dout[:, cols])
            wb = w_ref[gi]
            dp = _dot_nt(dout[:, cols], wb)
            dp_b = _dot_nt(dout_b[:, cols], wb)
            e = dp / _pool_counts(tm, i, w)
            t_below = (i + 1) * tm + lax.broadcasted_iota(jnp.int32, (POOL_HALO, 1), 0)
            e_b = dp_b / jnp.minimum(t_below + 1, w).astype(F32)
            acc = jnp.concatenate([e, e_b], axis=0)
            step = 1
            while step < w:
                acc = acc + pltpu.roll(acc, tm + POOL_HALO - step, 0)
                step *= 2
            dx_ref[:, cols] = acc[:tm, :] - dp
        dx, _ = _modulate_bwd(x_ref[...], dx_ref[...], vec_ref, dvec_ref)
        dx_ref[...] = dxo_v + dx

        @pl.when(i == nblk - 1)
        def _():
            dw_ref[...] = dw_sc[...].astype(BF16)

    nh = tm // POOL_HALO
    row = pl.BlockSpec((tm, d), lambda i: (i, 0))
    vecs = pl.BlockSpec((8, d), lambda i: (0, 0))
    wspec = pl.BlockSpec((groups, gd, gd), lambda i: (0, 0, 0))
    return _call(
        body, "pool_bwd", (nblk,),
        [
            row,
            pl.BlockSpec((POOL_HALO, d), lambda i: (jnp.minimum((i + 1) * nh, s // POOL_HALO - 1), 0)),
            row, vecs, row, row, wspec,
            pl.BlockSpec((1, d), lambda i: (0, 0)),
        ],
        [row, wspec, pl.BlockSpec((1, d), lambda i: (0, 0)), vecs],
        [
            jax.ShapeDtypeStruct((s, d), F32),
            jax.ShapeDtypeStruct((groups, gd, gd), BF16),
            jax.ShapeDtypeStruct((1, d), F32),
            jax.ShapeDtypeStruct((8, d), F32),
        ],
        [dxo, dxo, x, vec, p, o, w_grp, scale],
        scratch=[pltpu.VMEM((groups, gd, gd), F32)], phases=phases,
    )


def _loss_head(x, gain, target, phases=()):
    s, d = x.shape
    tm = _pick(s, (512, 256, 128))

    def body(x_ref, g_ref, t_ref, dx_ref, aux_ref):
        @pl.when(pl.program_id(0) == 0)
        def _():
            aux_ref[...] = jnp.zeros_like(aux_ref)

        xv = x_ref[...]
        rstd = _rstd(xv)
        r = xv * rstd
        gain_v = g_ref[...]
        err = r * gain_v - t_ref[...]
        aux_ref[1:2, :] += jnp.sum(err * err, axis=0, keepdims=True)
        dout = err * (1.0 / d)
        aux_ref[0:1, :] += jnp.sum(dout * r, axis=0, keepdims=True)
        dr = dout * gain_v
        dx_ref[...] = rstd * (dr - r * jnp.mean(dr * r, axis=-1, keepdims=True))

    row = pl.BlockSpec((tm, d), lambda i: (i, 0))
    return _call(
        body, "loss_head", (s // tm,),
        [row, pl.BlockSpec((1, d), lambda i: (0, 0)), row],
        [row, pl.BlockSpec((8, d), lambda i: (0, 0))],
        [jax.ShapeDtypeStruct((s, d), F32), jax.ShapeDtypeStruct((8, d), F32)], [x, gain, target], phases=phases,
    )


def _small_adam(gathered, gathered_ws, layout, smalls, chip):
    names = list(smalls)
    n = len(names)

    def body(*refs):
        chip_ref, g_ref, gws_ref = refs[0], refs[1], refs[2]
        wmv = refs[3 : 3 + 3 * n]
        outs = refs[3 + 3 * n : 3 + 7 * n]
        total = refs[-1]
        total[...] = g_ref[0]
        for kdev in range(1, N_DEV):
            total[...] += g_ref[kdev]
        total_ws = gws_ref[0]
        for kdev in range(1, N_DEV):
            total_ws = total_ws + gws_ref[kdev]
        my_chip = chip_ref[0]
        for a, name in enumerate(names):
            w_ref, m_ref, v_ref = wmv[3 * a : 3 * a + 3]
            if name == "ab_w_s":
                g = total_ws
            else:
                row0, rows, col0, cols = layout[name]
                if col0 is None:
                    g = jnp.zeros((rows, cols), F32)
                    for j in range(N_CHIPS):
                        g = g + jnp.where(my_chip == j, total[row0 : row0 + rows, j * cols : (j + 1) * cols], 0.0)
                else:
                    g = total[row0 : row0 + rows, col0 : col0 + cols]
            dl, mo, vo = _adam(w_ref[...], g, m_ref[...], v_ref[...])
            outs[4 * a][...] = g
            outs[4 * a + 1][...] = dl
            outs[4 * a + 2][...] = mo
            outs[4 * a + 3][...] = vo

    ins = [gathered, gathered_ws]
    out_shapes = []
    for name in names:
        ins.extend(smalls[name])
        out_shapes.extend([jax.ShapeDtypeStruct(smalls[name][0].shape, F32)] * 4)
    whole = lambda shape: pl.BlockSpec(shape, functools.partial(lambda nd, i, c: (0,) * nd, len(shape)))
    res = pl.pallas_call(
        body, name="small_adam",
        grid_spec=pltpu.PrefetchScalarGridSpec(
            num_scalar_prefetch=1, grid=(1,),
            in_specs=[whole(a.shape) for a in ins], out_specs=[whole(o.shape) for o in out_shapes],
            scratch_shapes=[pltpu.VMEM(gathered.shape[1:], F32)],
        ),
        out_shape=out_shapes,
        compiler_params=pltpu.CompilerParams(dimension_semantics=("arbitrary",), vmem_limit_bytes=VMEM_LIMIT_BYTES),
    )(chip.reshape(1).astype(jnp.int32), *ins)
    return {name: res[4 * a : 4 * a + 4] for a, name in enumerate(names)}


def _pad_rows(a, rows=8):
    extra = (-a.shape[0]) % rows
    return jnp.pad(a, ((0, extra), (0, 0))) if extra else a


def _pad_cols(a, cols):
    return jnp.pad(a, ((0, 0), (0, cols - a.shape[1]))) if a.shape[1] < cols else a


def _run(fn, *phases):
    outs, p_outs = fn(list(phases))
    for p, po in zip(phases, p_outs):
        p.then(po)
    return outs


def kernel(x, c, norm_g, w_mod, b_mod, w_ffn_in, w_ffn_out, ab_w_in, ab_norm_v, ab_w_s, ab_b_s, ab_conv_w, ab_w_out, pool_w_grp, pool_scale, final_g, loss_target, m_norm_g, m_w_mod, m_b_mod, m_w_ffn_in, m_w_ffn_out, m_ab_w_in, m_ab_norm_v, m_ab_w_s, m_ab_b_s, m_ab_conv_w, m_ab_w_out, m_pool_w_grp, m_pool_scale, m_final_g, v_norm_g, v_w_mod, v_b_mod, v_w_ffn_in, v_w_ffn_out, v_ab_w_in, v_ab_norm_v, v_ab_w_s, v_ab_b_s, v_ab_conv_w, v_ab_w_out, v_pool_w_grp, v_pool_scale, v_final_g):
    ix, iy, ic = _place()
    chip = 2 * ix + iy
    me = 4 * ix + 2 * iy + ic
    where = jnp.stack([chip, ic]).astype(jnp.int32)
    s, d = x.shape[1], x.shape[2]
    x0 = x.reshape(s, d)
    target = loss_target.reshape(s, d)
    n_layers = norm_g.shape[0]
    dq = d // N_CHIPS
    heads, chunk = ab_w_s.shape[1], ab_w_s.shape[2]
    da = ab_norm_v.shape[1]
    db = ab_conv_w.shape[2] * N_CHIPS
    f_hidden = w_ffn_out.shape[2] * N_CHIPS
    assert n_layers == 2 and da % heads == 0

    cw_pad = _pad_cols(ab_conv_w.reshape(3, db // N_CHIPS), dq)
    packed = jnp.concatenate(
        [_pad_rows(c.reshape(N_CHIPS, dq)), _pad_rows(norm_g.reshape(-1, dq)), _pad_rows(pool_scale.reshape(1, dq)), _pad_rows(cw_pad)],
        axis=0,
    )
    ncol = w_mod.shape[2]
    b_cols = lax.dynamic_slice(b_mod, (0, chip * ncol), (n_layers, ncol)).reshape(n_layers, 1, ncol)
    small = {}

    def small_gather(key, arrs):
        def then(outs):
            small[key] = outs

        return _phase_small_gather(arrs, then)

    stacks = {
        "w_ffn_in": tuple(a.reshape((-1,) + a.shape[2:]) for a in (w_ffn_in, m_w_ffn_in, v_w_ffn_in)),
        "w_ffn_out": tuple(a.reshape((-1,) + a.shape[2:]) for a in (w_ffn_out, m_w_ffn_out, v_w_ffn_out)),
        "ab_w_in": (ab_w_in, m_ab_w_in, v_ab_w_in),
        "ab_w_out": (ab_w_out, m_ab_w_out, v_ab_w_out),
        "pool_w_grp": (pool_w_grp[0], m_pool_w_grp[0], v_pool_w_grp[0]),
    }
    big_in = _Big((1, d, 2 * f_hidden), 2, 1)
    big_out = _Big((1, f_hidden, d), 1, 2)
    units = {}
    for l in range(n_layers):
        for k in range(2):
            units[f"in{l}{k}"] = (big_in, "w_ffn_in", 2 * l + k)
            units[f"out{l}{k}"] = (big_out, "w_ffn_out", 2 * l + k)
    units["abin"] = (_Big((1, d, ab_w_in.shape[2] * N_CHIPS), 2, 1), "ab_w_in", 0)
    units["about"] = (_Big((1, ab_w_out.shape[1] * N_CHIPS, d), 1, 2), "ab_w_out", 0)
    units["pool"] = (_Big((pool_w_grp.shape[1], pool_w_grp.shape[2] * N_CHIPS, pool_w_grp.shape[3]), 1, 0), "pool_w_grp", 0)
    big = {u: g for u, (g, _, _) in units.items()}

    weight = {}
    complete = set()

    def cast(u):
        g, st, b0 = units[u]

        def launch(phases):
            (weight[u],), p_outs = _cast_into_full(stacks[st][0], b0, g, where, "cast_" + u, phases)
            return None, p_outs

        return launch

    def gather_ici(*us):
        def then(outs):
            for u, o in zip(us, outs):
                weight[u] = o

        return _phase_gather_ici([weight[u] for u in us], [big[u] for u in us], then)

    def gather_sibling(*us):
        def then(outs):
            for u, o in zip(us, outs):
                weight[u] = o
                complete.add(u)

        return _phase_gather_sibling([weight[u] for u in us], [big[u] for u in us], then)

    def w_of(u):
        assert u in complete, u
        return weight[u]

    _run(cast("in00"), small_gather("inputs", [packed]))
    _run(cast("out00"))
    small_all = small["inputs"][0]
    by_chip = small_all[0::2]
    c_all = small_all[:, 0:N_CHIPS, :].reshape(N_DEV, d)
    norm_full = by_chip[:, 8 : 8 + 3 * n_layers, :].transpose(1, 0, 2).reshape(3 * n_layers, d)
    pool_scale_full = by_chip[:, 16:17, :].transpose(1, 0, 2).reshape(1, d)
    conv_full = by_chip[:, 24:27, : db // N_CHIPS].transpose(1, 0, 2).reshape(3, db)
    pieces = [("in00", "out00"), ("abin", "about"), ("in01", "out01"), ("in10", "out10", "pool"), ("in11", "out11")]
    in_flight = {}

    def start_gather(p):
        in_flight[p] = _split_start(gather_ici(*pieces[p]), f"gather_{p}_start")

    def started():
        return _after(*[flight.token for flight in in_flight.values()])

    def finish_gather(p, after, meanwhile=None):
        flight = in_flight.pop(p)
        _split_wait(flight, list(after) + list(started().ins), f"gather_{p}_wait")
        crossing = _split_start(gather_sibling(*pieces[p]), f"gather_{p}_forward")
        behind = [crossing.token]
        if p + 2 < len(pieces):
            for u in pieces[p + 2]:
                _run(cast(u), _after(crossing.token))
            start_gather(p + 2)
            behind = list(started().ins)
        if meanwhile is not None:
            behind = behind + meanwhile(_after(crossing.token))
        _split_wait(crossing, behind, f"gather_{p}_forwarded")

    start_gather(0)
    mod_cols = _run(lambda phases: _mod_fwd(c_all, w_mod, b_cols, phases))[0]
    _run(cast("abin"), small_gather("mod", [mod_cols.reshape(n_layers * N_DEV, ncol)]))
    _run(cast("about"))
    start_gather(1)
    mod_all = small["mod"][0]
    mod_mine = lax.dynamic_index_in_dim(mod_all[0::2].reshape(N_CHIPS, n_layers, N_DEV, ncol), me, axis=2, keepdims=False)
    mod = mod_mine.transpose(1, 0, 2).reshape(n_layers, 3, 3, d)
    vecs = {
        (l, sub): _pad_rows(jnp.concatenate([norm_full[3 * l + sub][None], mod[l, sub]], axis=0))
        for l in range(n_layers)
        for sub in range(3)
    }
    b_rows = jnp.broadcast_to(ab_b_s[0].T[:, :, None], (chunk, heads, da // heads)).reshape(chunk, da)

    saved = {}

    def ffn_forward(xs, l, sub, k, *phases):
        saved[l, sub, "x"] = xs
        xs, gg, uu, yb = _run(
            lambda ph: _ffn_fwd(xs, vecs[l, sub], w_of(f"in{l}{k}"), w_of(f"out{l}{k}"), f"ffn_fwd_{l}{k}", ph), *phases
        )
        saved[l, sub, "act"] = (gg, uu, yb)
        return xs

    finish_gather(0, [vecs[0, 0]])
    xs = ffn_forward(x0, 0, 0, 0, started())
    saved[0, 1, "x"] = xs
    finish_gather(1, [xs])
    (proj,) = _run(lambda ph: _proj_mod_fwd(xs, vecs[0, 1], w_of("abin"), ph), started())
    (cat,) = _run(lambda ph: _ab_mix_fwd(proj, ab_norm_v, ab_w_s[0], b_rows, conv_full, ph))
    xs, yb = _run(lambda ph: _proj_res_fwd(cat, w_of("about"), xs, vecs[0, 1], ph))
    saved[0, 1, "act"] = (proj, cat, yb)
    finish_gather(2, [xs])
    xs = ffn_forward(xs, 0, 2, 1, started())
    finish_gather(3, [xs])
    xs = ffn_forward(xs, 1, 0, 0, started())
    saved[1, 1, "x"] = xs
    pooled = []

    def pool_forward(behind):
        pooled.extend(_run(lambda ph: _pool_fwd(xs, vecs[1, 1], w_of("pool"), pool_scale_full, ph), behind))
        return [pooled[0]]

    finish_gather(4, [xs], pool_forward)
    xs, pp, oo = pooled
    saved[1, 1, "act"] = (pp, oo)
    xs = ffn_forward(xs, 1, 2, 1)
    dxs, aux = _run(lambda ph: _loss_head(xs, final_g.reshape(1, d), target, ph))
    loss = lax.psum(0.5 * jnp.sum(aux[1]) / d, ("x", "y", "c"))

    grad = {}
    recv = {}
    csum = {}
    parts = {}
    reduced = {}
    done = set()
    dvecs, small_g = {}, {}

    def pair_exchange(*us):
        def then(outs):
            for u, o in zip(us, outs):
                recv[u] = o

        return _phase_pair_exchange([grad[u] for u in us], [big[u] for u in us], then)

    def grad_half(u, a, b, mine, name, *phases, col0=0, prev=None):
        (res,) = _run(lambda ph: _grad_half(a, b, big[u], where, mine, col0, prev, recv[u] if mine else None, name, ph), *phases)
        return res

    def pair_sum(u, *phases):
        def launch(ph):
            (csum[u],), p_outs = _pair_sum(grad[u], recv[u], big[u], where, "pair_sum_" + u, ph)
            return None, p_outs

        _run(launch, *phases)

    def chip_exchange(*us):
        def then(outs):
            for u, o in zip(us, outs):
                parts[u] = o

        return _phase_chip_exchange([csum[u] for u in us], [big[u] for u in us], then)

    def chip_sum(*us, carried=()):
        for n_u, u in enumerate(us):
            g, st, b0 = units[u]

            def launch(ph):
                (reduced[st],), p_outs = _chip_sum(
                    csum[u], parts[u], g, where, reduced.get(st), stacks[st][0].shape, b0, "chip_sum_" + u, ph
                )
                return None, p_outs

            _run(launch, *(carried if n_u == 0 else ()))

    def pair_broadcast(*us):
        sts = [units[u][1] for u in us]
        assert len(set(sts)) == len(sts)

        def then(outs):
            for u, st, o in zip(us, sts, outs):
                reduced[st] = o
                done.add(u)

        return _phase_pair_broadcast([reduced[st] for st in sts], [big[u] for u in us], [units[u][2] for u in us], then)

    def ffn_backward(dxs, l, sub, k, carried_bwd, carried_send, carried_mine):
        gg, uu, yb = saved[l, sub, "act"]
        w_in, w_out = w_of(f"in{l}{k}"), w_of(f"out{l}{k}")
        uo, ui, tag = f"out{l}{k}", f"in{l}{k}", f"{l}{k}"
        dxs, dg, du, a, h, dy, dvecs[l, sub] = _run(
            lambda ph: _ffn_bwd(dxs, saved[l, sub, "x"], vecs[l, sub], gg, uu, yb, w_in, w_out, "ffn_bwd_" + tag, ph), *carried_bwd()
        )
        grad[uo] = grad_half(uo, a, dy, False, "dw_out_send_" + tag, *carried_send())
        part = grad_half(ui, h, du, False, "dw_in_u_send_" + tag, pair_exchange(uo), col0=f_hidden)
        grad[ui] = grad_half(ui, h, dg, False, "dw_in_g_send_" + tag, prev=part)
        csum[uo] = grad_half(uo, a, dy, True, "dw_out_" + tag, pair_exchange(ui))
        part = grad_half(ui, h, du, True, "dw_in_u_" + tag, *carried_mine(), col0=f_hidden)
        csum[ui] = grad_half(ui, h, dg, True, "dw_in_g_" + tag, prev=part)
        return dxs

    none = lambda: ()
    dxs = ffn_backward(dxs, 1, 2, 1, none, none, none)
    pp, oo = saved[1, 1, "act"]
    dxs, grad["pool"], small_g["pool_scale"], dvecs[1, 1] = _run(
        lambda ph: _pool_bwd(dxs, saved[1, 1, "x"], vecs[1, 1], pp, oo, w_of("pool"), pool_scale_full, ph)
    )

    def after_11():
        return (chip_exchange("in11", "out11"), pair_exchange("pool"))

    def bcast_11():
        chip_sum("in11", "out11")
        pair_sum("pool")
        return (pair_broadcast("in11", "out11"), chip_exchange("pool"))

    dxs = ffn_backward(dxs, 1, 0, 0, after_11, bcast_11, none)

    def after_10():
        return (chip_exchange("in10", "out10"),)

    def bcast_10():
        chip_sum("in10", "out10", "pool")
        return (pair_broadcast("in10", "out10", "pool"),)

    dxs = ffn_backward(dxs, 0, 2, 1, after_10, bcast_10, none)

    proj, cat, yb = saved[0, 1, "act"]
    dy, dcat, dgate = _run(lambda ph: _proj_res_bwd(dxs, yb, vecs[0, 1], w_of("about"), ph))
    grad["about"] = grad_half("about", cat, dy, False, "dw_ab_out_send")
    dproj, small_g["ab_norm_v"], small_g["ab_w_s"], dzs, small_g["ab_conv_w"] = _run(
        lambda ph: _ab_mix_bwd(proj, dcat, ab_norm_v, ab_w_s[0], b_rows, conv_full, ph), chip_exchange("out01"), pair_exchange("about")
    )
    small_g["ab_b_s"] = dzs.reshape(chunk, heads, da // heads).sum(axis=2).T
    dxs, h, dvecs[0, 1] = _run(
        lambda ph: _proj_mod_bwd(dproj[None], w_of("abin"), saved[0, 1, "x"], vecs[0, 1], dxs, dgate, "ab_in_bwd", ph)
    )
    grad["abin"] = grad_half("abin", h, dproj, False, "dw_ab_in_send")
    chip_sum("out01", carried=(pair_exchange("abin"),))
    csum["about"] = grad_half("about", cat, dy, True, "dw_ab_out", pair_broadcast("out01"))
    csum["abin"] = grad_half("abin", h, dproj, True, "dw_ab_in")

    def after_01():
        return (chip_exchange("in01", "abin", "about"),)

    layout = {}
    tail = {}

    def pack_small_grads():
        dgain = jnp.stack([dvecs[l, sub][0] for l in range(n_layers) for sub in range(3)])
        dmod = jnp.concatenate([dvecs[l, sub][1:4] for l in range(n_layers) for sub in range(3)], axis=0)
        rows = {
            "norm_g": (dgain, None, dq), "final_g": (aux[0:1], 0, d), "pool_scale": (small_g["pool_scale"], None, dq),
            "b_mod": (dmod, 0, d), "ab_norm_v": (small_g["ab_norm_v"], 0, da),
            "ab_conv_w": (small_g["ab_conv_w"], None, db // N_CHIPS), "ab_b_s": (small_g["ab_b_s"], 0, chunk),
        }
        row0 = 0
        for nm, (pc, col0, cols) in rows.items():
            layout[nm] = (row0, pc.shape[0], col0, cols)
            row0 += pc.shape[0]
        packed_rows = -(-row0 // 8) * 8
        return sum(
            jnp.pad(pc, ((layout[nm][0], packed_rows - layout[nm][0] - pc.shape[0]), (0, d - pc.shape[1])))
            for nm, (pc, _, _) in rows.items()
        )

    def bcast_01():
        chip_sum("in01", "abin", "about")
        grads_small = [pack_small_grads(), small_g["ab_w_s"].reshape(heads * chunk, chunk)]
        tail["small"] = _split_start(small_gather("grads", grads_small), "gather_small_grads_start")
        return (pair_broadcast("in01", "abin", "about"), _after(tail["small"].token))

    def reduce_out00():
        tail["out00"] = _split_start(chip_exchange("out00"), "reduce_out00_start")
        return (_after(tail["out00"].token),)

    dxs = ffn_backward(dxs, 0, 0, 0, after_01, bcast_01, reduce_out00)
    grad_x = dxs.reshape(x.shape)

    last = _split_start(chip_exchange("in00"), "reduce_last_start")
    (csum["out00"],) = _split_wait(tail["out00"], [last.token], "reduce_out00_wait")
    chip_sum("out00")
    _flush("broadcast_out00", pair_broadcast("out00"))
    _split_wait(tail["small"], [reduced["w_ffn_out"]], "gather_small_grads_wait")
    g_all, gws_all = small["grads"]

    out = {}

    def adam_stack(st, after=()):
        w3, m3, v3 = stacks[st]
        assert all(u in done for u, (_, ust, _) in units.items() if ust == st), st
        shape = {"w_ffn_in": w_ffn_in.shape, "w_ffn_out": w_ffn_out.shape, "pool_w_grp": pool_w_grp.shape}.get(st, w3.shape)
        out[st] = tuple(a.reshape(shape) for a in _adam_stack(w3, reduced[st], m3, v3, "adam_" + st, after))

    for st in ("w_ffn_out", "ab_w_in", "ab_w_out", "pool_w_grp"):
        adam_stack(st, (last.token,))

    shapes2d = {
        "norm_g": (3 * n_layers, dq), "b_mod": (9 * n_layers, d), "final_g": (1, d), "ab_norm_v": (1, da),
        "pool_scale": (1, dq), "ab_conv_w": (3, db // N_CHIPS), "ab_b_s": (heads, chunk), "ab_w_s": (heads * chunk, chunk),
    }
    small_w = {"norm_g": (norm_g, m_norm_g, v_norm_g), "b_mod": (b_mod, m_b_mod, v_b_mod), "final_g": (final_g, m_final_g, v_final_g),
               "ab_norm_v": (ab_norm_v, m_ab_norm_v, v_ab_norm_v), "pool_scale": (pool_scale, m_pool_scale, v_pool_scale),
               "ab_conv_w": (ab_conv_w, m_ab_conv_w, v_ab_conv_w), "ab_b_s": (ab_b_s, m_ab_b_s, v_ab_b_s), "ab_w_s": (ab_w_s, m_ab_w_s, v_ab_w_s)}
    smalls = {nm: tuple(a.reshape(shapes2d[nm]) for a in wmv) for nm, wmv in small_w.items()}
    small_out = _small_adam(g_all, gws_all, layout, smalls, chip)
    for nm, res in small_out.items():
        out[nm] = tuple(a.reshape(small_w[nm][0].shape) for a in res)

    mod_row0 = layout["b_mod"][0]
    dmod_all = g_all[:, mod_row0 : mod_row0 + 9 * n_layers, :].reshape(N_DEV, n_layers, 9 * d)
    dmod_cols = lax.dynamic_slice(dmod_all, (0, 0, chip * ncol), (N_DEV, n_layers, ncol)).transpose(1, 0, 2)
    out["w_mod"] = tuple(_mod_bwd_adam(c_all.T, dmod_cols, w_mod, m_w_mod, v_w_mod, (last.token,)))

    (csum["in00"],) = _split_wait(
        last, [out[st][1] for st in ("w_mod", "w_ffn_out", "ab_w_in", "ab_w_out", "pool_w_grp")], "reduce_last_wait"
    )
    chip_sum("in00")
    _flush("broadcast_last", pair_broadcast("in00"))
    adam_stack("w_ffn_in")

    order = ["norm_g", "w_mod", "b_mod", "w_ffn_in", "w_ffn_out", "ab_w_in", "ab_norm_v", "ab_w_s", "ab_b_s", "ab_conv_w", "ab_w_out", "pool_w_grp", "pool_scale", "final_g"]
    return (loss, grad_x, *[out[nm][0] for nm in order], *[out[nm][1] for nm in order], *[out[nm][2] for nm in order], *[out[nm][3] for nm in order])
```

```python
import functools
import math

import jax
import jax.numpy as jnp
from jax import lax
from jax.experimental import pallas as pl
from jax.experimental.pallas import tpu as pltpu

F32 = jnp.float32
BF16 = jnp.bfloat16
MESH = pl.DeviceIdType.MESH

EPS = 1e-6
ADAM_LR = 0.001
ADAM_B1 = 0.9
ADAM_B2 = 0.999
ADAM_EPS = 1e-08
ADAM_WD = 0.01
ADAM_STEP = 10
POOL_WINDOWS = (2, 4, 8, 16)
POOL_HALO = 16
CONV_HALO = 8
N_CHIPS = 4
N_DEV = 8
VMEM_LIMIT_BYTES = 48 * 1024 * 1024
EW_BLOCK_ELEMS = 256 * 1024


def _pick(n, prefs):
    for p in prefs:
        if p <= n and n % p == 0:
            return p
    return n


def _row_tile(rows, cols):
    best = None
    for d in range(16, rows + 1, 16):
        if rows % d == 0 and d * cols <= EW_BLOCK_ELEMS:
            best = d
    return best or rows


def _dot(a, b):
    return jnp.dot(a, b, preferred_element_type=F32)


def _dot_nt(a, b):
    return lax.dot_general(a, b, (((1,), (1,)), ((), ())), preferred_element_type=F32)


def _dot_tn(a, b):
    return lax.dot_general(a, b, (((0,), (0,)), ((), ())), preferred_element_type=F32)


def _sigmoid(x):
    return 0.5 * jnp.tanh(0.5 * x) + 0.5


_GELU_C = math.sqrt(2.0 / math.pi)


def _gelu(x):
    x2 = x * x
    t = jnp.tanh(_GELU_C * (x + 0.044715 * x2 * x))
    val = 0.5 * x * (1.0 + t)
    grad = 0.5 * (1.0 + t) + 0.5 * x * (1.0 - t * t) * (_GELU_C * (1.0 + 3.0 * 0.044715 * x2))
    return val, grad


def _rstd(x):
    return lax.rsqrt(jnp.mean(x * x, axis=-1, keepdims=True) + EPS)


def _modulate(x, vec_ref):
    return (x * _rstd(x)) * vec_ref[0:1, :] * (1.0 + vec_ref[2:3, :]) + vec_ref[1:2, :]


def _modulate_bwd(x, dh, vec_ref, dvec_ref):
    gn, sh, sc = vec_ref[0:1, :], vec_ref[1:2, :], vec_ref[2:3, :]
    rstd = _rstd(x)
    r = x * rstd
    dvec_ref[0:1, :] += jnp.sum(dh * r * (1.0 + sc), axis=0, keepdims=True)
    dvec_ref[1:2, :] += jnp.sum(dh, axis=0, keepdims=True)
    dvec_ref[2:3, :] += jnp.sum(dh * r * gn, axis=0, keepdims=True)
    gm = gn * (1.0 + sc)
    dr = dh * gm
    dx = rstd * (dr - r * jnp.mean(dr * r, axis=-1, keepdims=True))
    return dx, r * gm + sh


def _adam(w, g, m, v):
    m = ADAM_B1 * m + (1.0 - ADAM_B1) * g
    v = ADAM_B2 * v + (1.0 - ADAM_B2) * (g * g)
    m_hat = m / (1.0 - ADAM_B1**ADAM_STEP)
    v_hat = v / (1.0 - ADAM_B2**ADAM_STEP)
    delta = -ADAM_LR * (m_hat / (jnp.sqrt(v_hat) + ADAM_EPS) + ADAM_WD * w)
    return delta, m, v


_ANY = pl.BlockSpec(memory_space=pl.ANY)


class _Phase:
    def __init__(self, ins, out_shapes, aliases, n_sems, start, finish, then):
        self.ins, self.out_shapes, self.aliases, self.n_sems = list(ins), list(out_shapes), dict(aliases), n_sems
        self.start, self.finish, self.then = start, finish, then


def _call(body, name, grid, in_specs, out_specs, out_shape, ins, scratch=(), prefetch=(), phases=(), in_place=None):
    n_pre, n_in, n_out, n_sc = len(prefetch), len(in_specs), len(out_specs), len(scratch)
    ph_in = [len(p.ins) for p in phases]
    ph_out = [len(p.out_shapes) for p in phases]

    def kernel_body(*refs):
        pos = [0]

        def take(k):
            pos[0] += k
            return refs[pos[0] - k : pos[0]]

        pre, ins_ = take(n_pre), take(n_in)
        p_ins = [take(k) for k in ph_in]
        outs_ = take(n_out)
        p_outs = [take(k) for k in ph_out]
        sc = take(n_sc)
        sems = [take(2) for _ in phases]
        if phases:
            ids = [pl.program_id(a) for a in range(len(grid))]
            first = functools.reduce(jnp.logical_and, [i == 0 for i in ids])
            last = functools.reduce(jnp.logical_and, [i == g - 1 for i, g in zip(ids, grid)])

            @pl.when(first)
            def _():
                for p, pi, po, (send, recv) in zip(phases, p_ins, p_outs, sems):
                    p.start(pi, po, send, recv)

        if body is not None:
            body(*pre, *ins_, *outs_, *sc)
        if phases:

            @pl.when(last)
            def _():
                for p, pi, po, (send, recv) in zip(phases, p_ins, p_outs, sems):
                    p.finish(pi, po, send, recv)

    aliases = {n_pre + i: o for i, o in (in_place or {}).items()}
    i0, o0 = n_pre + n_in, n_out
    for p in phases:
        for i, o in p.aliases.items():
            aliases[i0 + i] = o0 + o
        i0 += len(p.ins)
        o0 += len(p.out_shapes)
    all_in = list(in_specs) + [_ANY] * sum(ph_in)
    all_out = list(out_specs) + [_ANY] * sum(ph_out)
    all_scratch = list(scratch)
    for p in phases:
        all_scratch += [pltpu.SemaphoreType.DMA((p.n_sems,)), pltpu.SemaphoreType.DMA((p.n_sems,))]
    shapes = list(out_shape) + [s for p in phases for s in p.out_shapes]
    operands = list(prefetch) + list(ins) + [a for p in phases for a in p.ins]
    sem = ("arbitrary",) * len(grid)
    params = pltpu.CompilerParams(dimension_semantics=sem, vmem_limit_bytes=VMEM_LIMIT_BYTES)
    if n_pre:
        res = pl.pallas_call(
            kernel_body, name=name, out_shape=shapes, input_output_aliases=aliases, compiler_params=params,
            grid_spec=pltpu.PrefetchScalarGridSpec(
                num_scalar_prefetch=n_pre, grid=grid, in_specs=all_in, out_specs=all_out, scratch_shapes=all_scratch
            ),
        )(*operands)
    else:
        res = pl.pallas_call(
            kernel_body, name=name, grid=grid, in_specs=all_in, out_specs=all_out, out_shape=shapes,
            scratch_shapes=all_scratch, input_output_aliases=aliases, compiler_params=params,
        )(*operands)
    res = list(res)
    outs, rest = res[:n_out], res[n_out:]
    p_res = []
    for k in ph_out:
        p_res.append(rest[:k])
        rest = rest[k:]
    return outs, p_res


def _place():
    return lax.axis_index("x"), lax.axis_index("y"), lax.axis_index("c")


def _other_chips():
    x, y, _ = _place()
    return [(1 - x, y), (x, 1 - y), (1 - x, 1 - y)]


def _flip(k):
    x, y, c = _place()
    return (1 - x if k & 4 else x, 1 - y if k & 2 else y, 1 - c if k & 1 else c)


def _remote(src, dst, send, recv, k, to):
    return pltpu.make_async_remote_copy(
        src_ref=src, dst_ref=dst, send_sem=send.at[k], recv_sem=recv.at[k], device_id=to, device_id_type=MESH
    )


def _phase_small_gather(arrs, then):
    n = len(arrs)

    def copies(ins, outs, send, recv):
        x, y, c = _place()
        me = 4 * x + 2 * y + c
        local = [pltpu.make_async_copy(ins[a], outs[a].at[me], send.at[a * N_DEV]) for a in range(n)]
        remote = [_remote(ins[a], outs[a].at[me], send, recv, a * N_DEV + k, _flip(k)) for a in range(n) for k in range(1, N_DEV)]
        return local, remote

    def start(ins, outs, send, recv):
        local, remote = copies(ins, outs, send, recv)
        for cp in local + remote:
            cp.start()

    def finish(ins, outs, send, recv):
        local, remote = copies(ins, outs, send, recv)
        for cp in remote + local:
            cp.wait()

    shapes = [jax.ShapeDtypeStruct((N_DEV,) + a.shape, a.dtype) for a in arrs]
    return _Phase(arrs, shapes, {}, n * N_DEV, start, finish, then)


def _after(*arrs):
    nothing = lambda *args: None
    return _Phase(arrs, [], {}, 1, nothing, nothing, nothing)


def _flush(name, *phases):
    _, p_outs = _call(None, name, (1,), [], [], [], [], phases=list(phases))
    for p, po in zip(phases, p_outs):
        p.then(po)


class _Big:
    KINDS = {"full": (True, True), "half": (True, False), "shard": (False, True), "block": (False, False)}

    def __init__(self, f3, s3, h3):
        assert s3 != h3
        self.f3, self.s3, self.h3 = tuple(f3), s3, h3
        self.bd = tuple(f3[a] // (N_CHIPS if a == s3 else 1) // (2 if a == h3 else 1) for a in range(3))
        self.tile = (1, _row_tile(self.bd[1], self.bd[2]), self.bd[2])
        self.grid = tuple(self.bd[a] // self.tile[a] for a in range(3))

    def dims(self, kind):
        chips, halves = self.KINDS[kind]
        return tuple(
            self.bd[a] * (N_CHIPS if chips and a == self.s3 else 1) * (2 if halves and a == self.h3 else 1) for a in range(3)
        )

    def view(self, ref, chip=None, half=None, batch0=0, both_halves=True):
        start = [batch0, 0, 0]
        size = list(ref.shape)
        size[0] = self.bd[0] * (2 if self.h3 == 0 and both_halves else 1)
        if chip is not None:
            start[self.s3] += chip * self.bd[self.s3]
            size[self.s3] = self.bd[self.s3]
        if half is not None:
            start[self.h3] += half * self.bd[self.h3]
            size[self.h3] = self.bd[self.h3]
        return ref.at[tuple(pl.ds(st, sz) for st, sz in zip(start, size))]

    def spec(self, chip_from=None, half_from=None, lead=(), batch0=0):
        extra = "grid" in (chip_from, half_from)

        def index(*args):
            pref, idx = args[-1], list(args[int(extra) : -1])
            idx[0] += batch0
            if chip_from:
                idx[self.s3] += (pref[0] if chip_from == "pref" else args[0]) * self.grid[self.s3]
            if half_from:
                idx[self.h3] += (pref[1] if half_from == "pref" else args[0]) * self.grid[self.h3]
            return (0,) * len(lead) + tuple(idx)

        return pl.BlockSpec(tuple(lead) + self.tile, index)


def _same(arrs):
    return [jax.ShapeDtypeStruct(a.shape, a.dtype) for a in arrs]


def _phase_gather_ici(arrs, bigs, then):
    n = len(arrs)

    def copies(outs, send, recv, arriving):
        x, y, c = _place()
        return [
            _remote(blk, blk, send, recv, 3 * a + j, (*chip, c))
            for j, chip in enumerate(_other_chips())
            for a in range(n)
            for blk in [bigs[a].view(outs[a], 2 * chip[0] + chip[1] if arriving else 2 * x + y, c)]
        ]

    def start(ins, outs, send, recv):
        for cp in copies(outs, send, recv, False):
            cp.start()

    def finish(ins, outs, send, recv):
        for cp in copies(outs, send, recv, True):
            cp.wait_recv()
        for cp in copies(outs, send, recv, False):
            cp.wait_send()

    return _Phase(arrs, _same(arrs), {a: a for a in range(n)}, 3 * n, start, finish, then)


def _phase_gather_sibling(arrs, bigs, then):
    n = len(arrs)

    def copies(outs, send, recv, arriving):
        x, y, c = _place()
        return [
            _remote(blk, blk, send, recv, 3 * a + j, (x, y, 1 - c))
            for j, chip in enumerate(_other_chips())
            for a in range(n)
            for blk in [bigs[a].view(outs[a], 2 * chip[0] + chip[1], 1 - c if arriving else c)]
        ]

    def start(ins, outs, send, recv):
        for cp in copies(outs, send, recv, False):
            cp.start()

    def finish(ins, outs, send, recv):
        for cp in copies(outs, send, recv, True):
            cp.wait_recv()
        for cp in copies(outs, send, recv, False):
            cp.wait_send()

    return _Phase(arrs, _same(arrs), {a: a for a in range(n)}, 3 * n, start, finish, then)


def _phase_pair_exchange(grads, bigs, then):
    n = len(grads)

    def copies(ins, outs, send, recv):
        x, y, c = _place()
        srcs = [ins[a] if ins[a].shape == outs[a].shape else bigs[a].view(ins[a], None, 1 - c) for a in range(n)]
        return [_remote(srcs[a], outs[a], send, recv, a, (x, y, 1 - c)) for a in range(n)]

    def start(ins, outs, send, recv):
        for cp in copies(ins, outs, send, recv):
            cp.start()

    def finish(ins, outs, send, recv):
        for cp in copies(ins, outs, send, recv):
            cp.wait()

    shapes = [jax.ShapeDtypeStruct(b.dims("half"), BF16) for b in bigs]
    return _Phase(grads, shapes, {}, n, start, finish, then)


def _phase_chip_exchange(sums, bigs, then):
    n = len(sums)

    def copies(ins, outs, send, recv):
        _, _, c = _place()
        return [
            _remote(bigs[a].view(ins[a], 2 * chip[0] + chip[1], both_halves=False), outs[a].at[j], send, recv, 3 * a + j, (*chip, c))
            for j, chip in enumerate(_other_chips())
            for a in range(n)
        ]

    def start(ins, outs, send, recv):
        for cp in copies(ins, outs, send, recv):
            cp.start()

    def finish(ins, outs, send, recv):
        for cp in copies(ins, outs, send, recv):
            cp.wait()

    shapes = [jax.ShapeDtypeStruct((N_CHIPS - 1,) + b.dims("block"), BF16) for b in bigs]
    return _Phase(sums, shapes, {}, 3 * n, start, finish, then)


_HBM = pl.BlockSpec(memory_space=pltpu.HBM)
_SEM = pl.BlockSpec(memory_space=pltpu.SEMAPHORE)
_DATAFLOW = pltpu.SideEffectType.DATAFLOW_SIDE_EFFECTING


class _InFlight:
    def __init__(self, phase, send, recv, arrays, token):
        self.phase, self.send, self.recv, self.arrays, self.token = phase, send, recv, arrays, token


def _phase_results(phase, refs):
    n_in = len(phase.ins)
    updated = {o: i for i, o in phase.aliases.items()}
    fresh = [o for o in range(len(phase.out_shapes)) if o not in updated]
    return [refs[updated[o]] if o in updated else refs[n_in + fresh.index(o)] for o in range(len(phase.out_shapes))]


def _split_start(phase, name):
    n_in = len(phase.ins)
    fresh = [s for o, s in enumerate(phase.out_shapes) if o not in phase.aliases.values()]
    arrays = list(phase.ins) + [lax.empty(s.shape, s.dtype) for s in fresh]
    n = len(arrays)

    def body(*refs):
        phase.start(refs[:n_in], _phase_results(phase, refs[:n]), refs[n], refs[n + 1])
        refs[-1][...] = jnp.zeros_like(refs[-1])

    operands = [pltpu.with_memory_space_constraint(a, pltpu.HBM) for a in arrays]
    res = pl.pallas_call(
        body, name=name,
        out_shape=[pltpu.SemaphoreType.DMA((phase.n_sems,)), pltpu.SemaphoreType.DMA((phase.n_sems,))]
        + [pltpu.HBM(a.shape, a.dtype) for a in arrays] + [jax.ShapeDtypeStruct((8, 128), F32)],
        in_specs=[_HBM] * n, out_specs=[_SEM, _SEM] + [_HBM] * n + [pl.BlockSpec(memory_space=pltpu.VMEM)],
        input_output_aliases={i: 2 + i for i in range(n)},
        compiler_params=pltpu.CompilerParams(has_side_effects=_DATAFLOW),
    )(*operands)
    return _InFlight(phase, res[0], res[1], list(res[2 : 2 + n]), res[-1])


def _split_wait(flight, after, name):
    phase, n = flight.phase, len(flight.arrays)
    n_in = len(phase.ins)

    def body(*refs):
        phase.finish(refs[:n_in], _phase_results(phase, refs[:n]), refs[n], refs[n + 1])

    res = pl.pallas_call(
        body, name=name, out_shape=[pltpu.HBM(a.shape, a.dtype) for a in flight.arrays],
        in_specs=[_HBM] * n + [_SEM, _SEM] + [_ANY] * len(after), out_specs=[_HBM] * n,
        input_output_aliases={i: i for i in range(n)},
        compiler_params=pltpu.CompilerParams(has_side_effects=_DATAFLOW),
    )(*flight.arrays, flight.send, flight.recv, *after)
    res = list(res)
    phase.then(_phase_results(phase, res))
    return res[:n_in]


def _phase_pair_broadcast(stacks, bigs, batch0s, then):
    n = len(stacks)

    def start(ins, outs, send, recv):
        x, y, c = _place()
        for a in range(n):
            blk = bigs[a].view(outs[a], None, c, batch0s[a])
            _remote(blk, blk, send, recv, a, (x, y, 1 - c)).start()

    def finish(ins, outs, send, recv):
        x, y, c = _place()
        for a in range(n):
            mine = bigs[a].view(outs[a], None, c, batch0s[a])
            theirs = bigs[a].view(outs[a], None, 1 - c, batch0s[a])
            _remote(mine, mine, send, recv, a, (x, y, 1 - c)).wait_send()
            _remote(theirs, theirs, send, recv, a, (x, y, 1 - c)).wait_recv()

    return _Phase(stacks, _same(stacks), {a: a for a in range(n)}, n, start, finish, then)


def _tile_call(body, name, big, where, extra, ins, in_specs, out_specs, out_shape, phases=()):
    grid = ((extra,) if extra else ()) + big.grid
    return _call(body, name, grid, in_specs, out_specs, out_shape, ins, prefetch=(where,), phases=phases)


def _cast_into_full(w_stack, batch0, big, where, name, phases=()):
    def body(_, w_ref, o_ref):
        o_ref[...] = w_ref[...].astype(BF16)

    return _tile_call(
        body, name, big, where, 2, [w_stack], [big.spec(None, "grid", batch0=batch0)], [big.spec("pref", "grid")],
        [jax.ShapeDtypeStruct(big.dims("full"), BF16)], phases,
    )


def _pair_sum(g_full, recv_half, big, where, name, phases=()):
    def body(_, g_ref, r_ref, o_ref):
        o_ref[...] = (g_ref[...].astype(F32) + r_ref[...].astype(F32)).astype(BF16)

    half = big.spec("grid", None)
    return _tile_call(
        body, name, big, where, N_CHIPS, [g_full, recv_half], [big.spec("grid", "pref"), half], [half],
        [jax.ShapeDtypeStruct(big.dims("half"), BF16)], phases,
    )


def _chip_sum(chip_sum, parts, big, where, stack, stack_shape, batch0, name, phases=()):
    def body(_, own_ref, p_ref, *rest):
        acc = own_ref[...].astype(F32)
        for k in range(N_CHIPS - 1):
            acc = acc + p_ref[k].astype(F32)
        rest[-1][...] = acc

    ins = [chip_sum, parts] + ([stack] if stack is not None else [])
    in_specs = [big.spec("pref", None), big.spec(None, None, lead=(N_CHIPS - 1,))] + ([_ANY] if stack is not None else [])
    return _call(
        body, name, big.grid, in_specs, [big.spec(None, "pref", batch0=batch0)], [jax.ShapeDtypeStruct(stack_shape, F32)], ins,
        prefetch=(where,), phases=phases, in_place={2: 0} if stack is not None else None,
    )


def _adam_stack(w, g, m, v, name, after=()):
    b, r, c = w.shape
    tr = _row_tile(r, c)

    def body(w_ref, g_ref, m_ref, v_ref, *rest):
        go_ref, d_ref, mo_ref, vo_ref = rest[-4:]
        gv = g_ref[...]
        d, mo, vo = _adam(w_ref[...], gv, m_ref[...], v_ref[...])
        go_ref[...] = gv
        d_ref[...] = d
        mo_ref[...] = mo
        vo_ref[...] = vo

    spec = pl.BlockSpec((1, tr, c), lambda bb, i: (bb, i, 0))
    outs, _ = _call(
        body, name, (b, r // tr), [spec] * 4 + [_ANY] * len(after), [spec] * 4, [jax.ShapeDtypeStruct(w.shape, F32)] * 4,
        [w, g, m, v, *after],
    )
    return outs


def _mod_fwd(c_all, w_mod, b_cols, phases=()):
    n_layers, d, n = w_mod.shape
    tn = _pick(n, (768, 512, 384, 256, 128))

    def body(c_ref, w_ref, b_ref, o_ref):
        cv = c_ref[...]
        ca = (cv * _sigmoid(cv)).astype(BF16)
        o_ref[0] = _dot(ca, w_ref[0].astype(BF16)) + b_ref[0]

    return _call(
        body, "mod_fwd", (n_layers, n // tn),
        [
            pl.BlockSpec((N_DEV, d), lambda l, j: (0, 0)),
            pl.BlockSpec((1, d, tn), lambda l, j: (l, 0, j)),
            pl.BlockSpec((1, 1, tn), lambda l, j: (l, 0, j)),
        ],
        [pl.BlockSpec((1, N_DEV, tn), lambda l, j: (l, 0, j))],
        [jax.ShapeDtypeStruct((n_layers, N_DEV, n), F32)], [c_all, w_mod, b_cols], phases=phases,
    )


def _mod_bwd_adam(c_all_t, dmod_cols, w, m, v, after=()):
    n_layers, d, n = w.shape
    tn = _pick(n, (384, 256, 128))

    def body(c_ref, dm_ref, w_ref, m_ref, v_ref, *rest):
        g_ref, d_ref, mo_ref, vo_ref = rest[-4:]
        cv = c_ref[...]
        ca = (cv * _sigmoid(cv)).astype(BF16)
        g = _dot(ca, dm_ref[0].astype(BF16))
        g_ref[0] = g
        dl, mo, vo = _adam(w_ref[0], g, m_ref[0], v_ref[0])
        d_ref[0] = dl
        mo_ref[0] = mo
        vo_ref[0] = vo

    wspec = pl.BlockSpec((1, d, tn), lambda l, j: (l, 0, j))
    outs, _ = _call(
        body, "mod_bwd_adam", (n_layers, n // tn),
        [pl.BlockSpec((d, N_DEV), lambda l, j: (0, 0)), pl.BlockSpec((1, N_DEV, tn), lambda l, j: (l, 0, j)), wspec, wspec, wspec]
        + [_ANY] * len(after),
        [wspec] * 4, [jax.ShapeDtypeStruct(w.shape, F32)] * 4, [c_all_t, dmod_cols, w, m, v, *after],
    )
    return outs


def _ffn_fwd(x, vec, w_in, w_out, name, phases=()):
    s, d = x.shape
    f = w_out.shape[1]
    tm = _pick(s, (1024, 512, 256, 128))
    tf = _pick(f, (256, 128))
    nf = f // tf

    def body(x_ref, vec_ref, wg_ref, wu_ref, wo_ref, xo_ref, g_ref, u_ref, y_ref, h_sc, acc_sc):
        j = pl.program_id(1)

        @pl.when(j == 0)
        def _():
            h_sc[...] = _modulate(x_ref[...], vec_ref).astype(BF16)
            acc_sc[...] = jnp.zeros_like(acc_sc)

        h = h_sc[...]
        g = _dot(h, wg_ref[0])
        u = _dot(h, wu_ref[0])
        g_ref[...] = g.astype(BF16)
        u_ref[...] = u.astype(BF16)
        a = (g * _sigmoid(g) * u).astype(BF16)
        acc_sc[...] += _dot(a, wo_ref[0])

        @pl.when(j == nf - 1)
        def _():
            yv = acc_sc[...]
            xo_ref[...] = x_ref[...] + 0.5 * vec_ref[3:4, :] * yv
            y_ref[...] = yv.astype(BF16)

    row = pl.BlockSpec((tm, d), lambda i, j: (i, 0))
    hid = pl.BlockSpec((tm, tf), lambda i, j: (i, j))
    return _call(
        body, name, (s // tm, nf),
        [
            row,
            pl.BlockSpec((8, d), lambda i, j: (0, 0)),
            pl.BlockSpec((1, d, tf), lambda i, j: (0, 0, j)),
            pl.BlockSpec((1, d, tf), lambda i, j: (0, 0, nf + j)),
            pl.BlockSpec((1, tf, d), lambda i, j: (0, j, 0)),
        ],
        [row, hid, hid, row],
        [
            jax.ShapeDtypeStruct((s, d), F32),
            jax.ShapeDtypeStruct((s, f), BF16),
            jax.ShapeDtypeStruct((s, f), BF16),
            jax.ShapeDtypeStruct((s, d), BF16),
        ],
        [x, vec, w_in, w_in, w_out],
        scratch=[pltpu.VMEM((tm, d), BF16), pltpu.VMEM((tm, d), F32)], phases=phases,
    )


def _ffn_bwd(dxo, x, vec, gg, uu, y, w_in, w_out, name, phases=()):
    s, d = x.shape
    f = w_out.shape[1]
    tm = _pick(s, (512, 256, 128))
    tf = _pick(f, (256, 128))
    nf = f // tf

    def body(dxo_ref, x_ref, vec_ref, g_ref, u_ref, y_ref, wg_ref, wu_ref, wo_ref,
             dx_ref, dg_ref, du_ref, a_ref, h_ref, dy_ref, dvec_ref, acc_sc):
        i, j = pl.program_id(0), pl.program_id(1)

        @pl.when((i == 0) & (j == 0))
        def _():
            dvec_ref[...] = jnp.zeros_like(dvec_ref)

        @pl.when(j == 0)
        def _():
            dxo_v = dxo_ref[...]
            dy_ref[...] = (0.5 * vec_ref[3:4, :] * dxo_v).astype(BF16)
            dvec_ref[3:4, :] += 0.5 * jnp.sum(dxo_v * y_ref[...].astype(F32), axis=0, keepdims=True)
            acc_sc[...] = jnp.zeros_like(acc_sc)

        da = _dot_nt(dy_ref[...], wo_ref[0])
        g = g_ref[...].astype(F32)
        u = u_ref[...].astype(F32)
        sig = _sigmoid(g)
        sl = g * sig
        a_ref[...] = (sl * u).astype(BF16)
        dg = (da * u * (sig * (1.0 + g * (1.0 - sig)))).astype(BF16)
        du = (da * sl).astype(BF16)
        dg_ref[...] = dg
        du_ref[...] = du
        acc_sc[...] += _dot_nt(dg, wg_ref[0]) + _dot_nt(du, wu_ref[0])

        @pl.when(j == nf - 1)
        def _():
            dx, h = _modulate_bwd(x_ref[...], acc_sc[...], vec_ref, dvec_ref)
            dx_ref[...] = dxo_ref[...] + dx
            h_ref[...] = h.astype(BF16)

    row = pl.BlockSpec((tm, d), lambda i, j: (i, 0))
    hid = pl.BlockSpec((tm, tf), lambda i, j: (i, j))
    vecs = pl.BlockSpec((8, d), lambda i, j: (0, 0))
    return _call(
        body, name, (s // tm, nf),
        [
            row, row, vecs, hid, hid, row,
            pl.BlockSpec((1, d, tf), lambda i, j: (0, 0, j)),
            pl.BlockSpec((1, d, tf), lambda i, j: (0, 0, nf + j)),
            pl.BlockSpec((1, tf, d), lambda i, j: (0, j, 0)),
        ],
        [row, hid, hid, hid, row, row, vecs],
        [
            jax.ShapeDtypeStruct((s, d), F32),
            jax.ShapeDtypeStruct((s, f), BF16),
            jax.ShapeDtypeStruct((s, f), BF16),
            jax.ShapeDtypeStruct((s, f), BF16),
            jax.ShapeDtypeStruct((s, d), BF16),
            jax.ShapeDtypeStruct((s, d), BF16),
            jax.ShapeDtypeStruct((8, d), F32),
        ],
        [dxo, x, vec, gg, uu, y, w_in, w_in, w_out],
        scratch=[pltpu.VMEM((tm, d), F32)], phases=phases,
    )


def _grad_half(a, bs, big, where, mine, recv, name, phases=()):
    s, k1 = a.shape
    n = bs[0].shape[1]
    groups = len(bs)
    rows_halved = big.h3 == 1
    assert rows_halved or groups == 1
    kk, nn = (k1 // 2, n) if rows_halved else (k1, n // 2)
    tk = _pick(kk, (1408, 1024, 512, 256, 128))
    tn = _pick(nn, (1408, 1024, 640, 512, 256, 128))
    nkb, nnb = kk // tk, nn // tn
    assert (recv is None) == (not mine)

    def half(pref):
        return pref[1] if mine else 1 - pref[1]

    def body(_, a_ref, *rest):
        q = pl.program_id(1)
        for p in range(groups):

            @pl.when(q == p)
            def _(p=p):
                acc = _dot_tn(a_ref[...], rest[p][...])
                if recv is not None:
                    acc = acc + rest[groups][0].astype(F32)
                rest[-1][0] = acc.astype(BF16)

    def b_block(p):
        def index(i, q, j, pref):
            jj = jnp.where(q == p, j, jnp.where(q < p, 0, nnb - 1))
            return (0, jj + (0 if rows_halved else half(pref) * nnb))

        return pl.BlockSpec((s, tn), index)

    out_spec = pl.BlockSpec((1, tk, tn), lambda i, q, j, pref: (0, i, q * nnb + j))
    in_specs = [pl.BlockSpec((s, tk), lambda i, q, j, pref: (0, i + (half(pref) * nkb if rows_halved else 0)))]
    in_specs += [b_block(p) for p in range(groups)]
    ins = [a, *bs]
    if recv is not None:
        in_specs.append(out_spec)
        ins.append(recv)
    return _call(
        body, name, (nkb, groups, nnb), in_specs, [out_spec], [jax.ShapeDtypeStruct(big.dims("half"), BF16)], ins,
        prefetch=(where,), phases=phases,
    )


def _proj_mod_fwd(x, vec, w, phases=()):
    s, d = x.shape
    n = w.shape[2]
    tm = _pick(s, (512, 256, 128))
    tn = _pick(n, (640, 512, 256, 128))

    def body(x_ref, vec_ref, w_ref, o_ref, h_sc):
        @pl.when(pl.program_id(1) == 0)
        def _():
            h_sc[...] = _modulate(x_ref[...], vec_ref).astype(BF16)

        o_ref[...] = _dot(h_sc[...], w_ref[0])

    return _call(
        body, "ab_in_fwd", (s // tm, n // tn),
        [
            pl.BlockSpec((tm, d), lambda i, j: (i, 0)),
            pl.BlockSpec((8, d), lambda i, j: (0, 0)),
            pl.BlockSpec((1, d, tn), lambda i, j: (0, 0, j)),
        ],
        [pl.BlockSpec((tm, tn), lambda i, j: (i, j))],
        [jax.ShapeDtypeStruct((s, n), F32)], [x, vec, w],
        scratch=[pltpu.VMEM((tm, d), BF16)], phases=phases,
    )


def _proj_res_fwd(a, w, x, vec, phases=()):
    s, kd = a.shape
    d = x.shape[1]
    tm = _pick(s, (512, 256, 128))

    def body(a_ref, w_ref, x_ref, vec_ref, xo_ref, y_ref):
        yv = _dot(a_ref[...], w_ref[0])
        xo_ref[...] = x_ref[...] + vec_ref[3:4, :] * yv
        y_ref[...] = yv.astype(BF16)

    row = pl.BlockSpec((tm, d), lambda i: (i, 0))
    return _call(
        body, "ab_out_fwd", (s // tm,),
        [pl.BlockSpec((tm, kd), lambda i: (i, 0)), pl.BlockSpec((1, kd, d), lambda i: (0, 0, 0)), row, pl.BlockSpec((8, d), lambda i: (0, 0))],
        [row, row],
        [jax.ShapeDtypeStruct((s, d), F32), jax.ShapeDtypeStruct((s, d), BF16)], [a, w, x, vec], phases=phases,
    )


def _proj_res_bwd(dxo, y, vec, w, phases=()):
    s, d = dxo.shape
    kd = w.shape[1]
    tm = _pick(s, (512, 256, 128))

    def body(dxo_ref, y_ref, vec_ref, w_ref, dy_ref, da_ref, dgate_ref):
        @pl.when(pl.program_id(0) == 0)
        def _():
            dgate_ref[...] = jnp.zeros_like(dgate_ref)

        dxo_v = dxo_ref[...]
        dy = (vec_ref[3:4, :] * dxo_v).astype(BF16)
        dy_ref[...] = dy
        dgate_ref[3:4, :] += jnp.sum(dxo_v * y_ref[...].astype(F32), axis=0, keepdims=True)
        da_ref[...] = _dot_nt(dy, w_ref[0]).astype(BF16)

    row = pl.BlockSpec((tm, d), lambda i: (i, 0))
    vecs = pl.BlockSpec((8, d), lambda i: (0, 0))
    return _call(
        body, "ab_out_bwd", (s // tm,),
        [row, row, vecs, pl.BlockSpec((1, kd, d), lambda i: (0, 0, 0))],
        [row, pl.BlockSpec((tm, kd), lambda i: (i, 0)), vecs],
        [jax.ShapeDtypeStruct((s, d), BF16), jax.ShapeDtypeStruct((s, kd), BF16), jax.ShapeDtypeStruct((8, d), F32)],
        [dxo, y, vec, w], phases=phases,
    )


def _proj_mod_bwd(dproj, w, x, vec, dxo, dvec_in, name, phases=()):
    parts, s, n_part = dproj.shape
    d = x.shape[1]
    tm = _pick(s, (512, 256, 128))
    tk = _pick(n_part, (1408, 1280, 1024, 512, 256, 128))
    per_part = n_part // tk
    nk = parts * per_part

    def body(dp_ref, w_ref, x_ref, vec_ref, dxo_ref, dvi_ref, dx_ref, h_ref, dvec_ref, acc_sc):
        i, k = pl.program_id(0), pl.program_id(1)

        @pl.when((i == 0) & (k == 0))
        def _():
            dvec_ref[...] = dvi_ref[...]

        @pl.when(k == 0)
        def _():
            acc_sc[...] = jnp.zeros_like(acc_sc)

        acc_sc[...] += _dot_nt(dp_ref[0], w_ref[0])

        @pl.when(k == nk - 1)
        def _():
            dx, h = _modulate_bwd(x_ref[...], acc_sc[...], vec_ref, dvec_ref)
            dx_ref[...] = dxo_ref[...] + dx
            h_ref[...] = h.astype(BF16)

    row = pl.BlockSpec((tm, d), lambda i, k: (i, 0))
    vecs = pl.BlockSpec((8, d), lambda i, k: (0, 0))
    return _call(
        body, name, (s // tm, nk),
        [
            pl.BlockSpec((1, tm, tk), lambda i, k: (k // per_part, i, k % per_part)),
            pl.BlockSpec((1, d, tk), lambda i, k: (0, 0, k)),
            row, vecs, row, vecs,
        ],
        [row, row, vecs],
        [jax.ShapeDtypeStruct((s, d), F32), jax.ShapeDtypeStruct((s, d), BF16), jax.ShapeDtypeStruct((8, d), F32)],
        [dproj, w, x, vec, dxo, dvec_in], scratch=[pltpu.VMEM((tm, d), F32)], phases=phases,
    )


def _tril(n):
    return lax.broadcasted_iota(jnp.int32, (n, n), 0) >= lax.broadcasted_iota(jnp.int32, (n, n), 1)


def _layernorm_stats(gv):
    mu = jnp.mean(gv, axis=-1, keepdims=True)
    cen = gv - mu
    rstd = lax.rsqrt(jnp.mean(cen * cen, axis=-1, keepdims=True) + EPS)
    return cen * rstd, rstd


def _shift_down(q, k, above_ref, c_cg, c_xb, first):
    width = q.shape[1]
    rows = lax.broadcasted_iota(jnp.int32, q.shape, 0)
    out = pltpu.roll(q, k, 0)
    for r in range(k):
        src = CONV_HALO - k + r
        above = above_ref[src : src + 1, c_cg : c_cg + width] * above_ref[src : src + 1, c_xb : c_xb + width]
        above = jnp.where(first, 0.0, above)
        out = jnp.where(rows == r, above, out)
    return out


def _ab_mix_fwd(proj, norm_v, w_s, b_rows, conv_w, phases=()):
    s, n = proj.shape
    heads, chunk, _ = w_s.shape
    da = norm_v.shape[1]
    hd = da // heads
    db = conv_w.shape[1]
    tm = _pick(s, (512, 256, 128))

    def body(p_ref, ph_ref, nv_ref, ws_ref, b_ref, cw_ref, o_ref):
        first = pl.program_id(0) == 0
        gu, _ = _gelu(p_ref[:, 0:da])
        gv, _ = _gelu(p_ref[:, da : 2 * da])
        xhat, _ = _layernorm_stats(gv)
        vn = (xhat * nv_ref[...]).astype(BF16)
        mask = _tril(chunk)
        for hh in range(heads):
            wm = jnp.where(mask, ws_ref[hh], 0.0).astype(BF16)
            cols = slice(hh * hd, (hh + 1) * hd)
            for nn in range(tm // chunk):
                rows = slice(nn * chunk, (nn + 1) * chunk)
                z = _dot(wm, vn[rows, cols]) + b_ref[:, cols]
                o_ref[rows, cols] = (gu[rows, cols] * z).astype(BF16)
        c_cg, c_xb = 2 * da + db, 2 * da + 2 * db
        bg = p_ref[:, 2 * da : 2 * da + db]
        q = p_ref[:, c_cg : c_cg + db] * p_ref[:, c_xb : c_xb + db]
        q1 = _shift_down(q, 1, ph_ref, c_cg, c_xb, first)
        q2 = _shift_down(q, 2, ph_ref, c_cg, c_xb, first)
        conv = cw_ref[0:1, :] * q2 + cw_ref[1:2, :] * q1 + cw_ref[2:3, :] * q
        o_ref[:, da : da + db] = (bg * conv).astype(BF16)

    nh = tm // CONV_HALO
    return _call(
        body, "ab_mix_fwd", (s // tm,),
        [
            pl.BlockSpec((tm, n), lambda i: (i, 0)),
            pl.BlockSpec((CONV_HALO, n), lambda i: (jnp.maximum(i * nh - 1, 0), 0)),
            pl.BlockSpec((1, da), lambda i: (0, 0)),
            pl.BlockSpec((heads, chunk, chunk), lambda i: (0, 0, 0)),
            pl.BlockSpec((chunk, da), lambda i: (0, 0)),
            pl.BlockSpec((3, db), lambda i: (0, 0)),
        ],
        [pl.BlockSpec((tm, da + db), lambda i: (i, 0))],
        [jax.ShapeDtypeStruct((s, da + db), BF16)], [proj, proj, norm_v, w_s, b_rows, conv_w], phases=phases,
    )


def _ab_mix_bwd(proj, dcat, norm_v, w_s, b_rows, conv_w, phases=()):
    s, n = proj.shape
    heads, chunk, _ = w_s.shape
    da = norm_v.shape[1]
    hd = da // heads
    db = conv_w.shape[1]
    tm = _pick(s, (512, 256, 128))
    nblk = s // tm
    dhalo = 2 * CONV_HALO

    def body(p_ref, pa_ref, pb_ref, dc_ref, dcb_ref, nv_ref, ws_ref, b_ref, cw_ref,
             dp_ref, dnv_ref, dws_ref, dzs_ref, dcw_ref, dvn_sc):
        i = pl.program_id(0)
        first, last = i == 0, i == nblk - 1

        @pl.when(first)
        def _():
            dnv_ref[...] = jnp.zeros_like(dnv_ref)
            dws_ref[...] = jnp.zeros_like(dws_ref)
            dzs_ref[...] = jnp.zeros_like(dzs_ref)
            dcw_ref[...] = jnp.zeros_like(dcw_ref)

        uu = p_ref[:, 0:da]
        gu, gu_grad = _gelu(uu)
        gv, gv_grad = _gelu(p_ref[:, da : 2 * da])
        xhat, rstd = _layernorm_stats(gv)
        nv = nv_ref[...]
        vn = (xhat * nv).astype(BF16)
        dya = dc_ref[:, 0:da].astype(F32)
        dz = (dya * gu).astype(BF16)
        mask = _tril(chunk)
        for hh in range(heads):
            wm = jnp.where(mask, ws_ref[hh], 0.0).astype(BF16)
            cols = slice(hh * hd, (hh + 1) * hd)
            dws = jnp.zeros((chunk, chunk), F32)
            for nn in range(tm // chunk):
                rows = slice(nn * chunk, (nn + 1) * chunk)
                z = _dot(wm, vn[rows, cols]) + b_ref[:, cols]
                dp_ref[rows, cols] = (dya[rows, cols] * z * gu_grad[rows, cols]).astype(BF16)
                dz_blk = dz[rows, cols]
                dws = dws + _dot_nt(dz_blk, vn[rows, cols])
                dzs_ref[:, cols] += dz_blk.astype(F32)
                dvn = _dot_tn(wm, dz_blk)
                dnv_ref[:, cols] += jnp.sum(dvn * xhat[rows, cols], axis=0, keepdims=True)
                dvn_sc[rows, cols] = dvn
            dws_ref[hh] += jnp.where(mask, dws, 0.0)
        dxhat = dvn_sc[...] * nv
        dgv = rstd * (dxhat - jnp.mean(dxhat, axis=-1, keepdims=True) - xhat * jnp.mean(dxhat * xhat, axis=-1, keepdims=True))
        dp_ref[:, da : 2 * da] = (dgv * gv_grad).astype(BF16)

        c_bg, c_cg, c_xb = 2 * da, 2 * da + db, 2 * da + 2 * db
        bg = p_ref[:, c_bg : c_bg + db]
        cg = p_ref[:, c_cg : c_cg + db]
        xb = p_ref[:, c_xb : c_xb + db]
        q = cg * xb
        q1 = _shift_down(q, 1, pa_ref, c_cg, c_xb, first)
        q2 = _shift_down(q, 2, pa_ref, c_cg, c_xb, first)
        dyb = dc_ref[:, da : da + db].astype(F32)
        conv = cw_ref[0:1, :] * q2 + cw_ref[1:2, :] * q1 + cw_ref[2:3, :] * q
        dp_ref[:, c_bg : c_bg + db] = (dyb * conv).astype(BF16)
        e = dyb * bg
        dcw_ref[0:1, :] += jnp.sum(e * q2, axis=0, keepdims=True)
        dcw_ref[1:2, :] += jnp.sum(e * q1, axis=0, keepdims=True)
        dcw_ref[2:3, :] += jnp.sum(e * q, axis=0, keepdims=True)
        rows = lax.broadcasted_iota(jnp.int32, e.shape, 0)
        dq = cw_ref[2:3, :] * e
        for kk in (1, 2):
            ek = pltpu.roll(e, tm - kk, 0)
            for r in range(kk):
                below = dcb_ref[r : r + 1, da : da + db].astype(F32) * pb_ref[r : r + 1, c_bg : c_bg + db]
                below = jnp.where(last, 0.0, below)
                ek = jnp.where(rows == tm - kk + r, below, ek)
            dq = dq + cw_ref[2 - kk : 3 - kk, :] * ek
        dp_ref[:, c_cg : c_cg + db] = (dq * xb).astype(BF16)
        dp_ref[:, c_xb : c_xb + db] = (dq * cg).astype(BF16)

    nh = tm // CONV_HALO
    nhb = tm // dhalo
    const2 = lambda i: (0, 0)
    return _call(
        body, "ab_mix_bwd", (nblk,),
        [
            pl.BlockSpec((tm, n), lambda i: (i, 0)),
            pl.BlockSpec((CONV_HALO, n), lambda i: (jnp.maximum(i * nh - 1, 0), 0)),
            pl.BlockSpec((CONV_HALO, n), lambda i: (jnp.minimum((i + 1) * nh, s // CONV_HALO - 1), 0)),
            pl.BlockSpec((tm, da + db), lambda i: (i, 0)),
            pl.BlockSpec((dhalo, da + db), lambda i: (jnp.minimum((i + 1) * nhb, s // dhalo - 1), 0)),
            pl.BlockSpec((1, da), const2),
            pl.BlockSpec((heads, chunk, chunk), lambda i: (0, 0, 0)),
            pl.BlockSpec((chunk, da), const2),
            pl.BlockSpec((3, db), const2),
        ],
        [
            pl.BlockSpec((tm, n), lambda i: (i, 0)),
            pl.BlockSpec((1, da), const2),
            pl.BlockSpec((heads, chunk, chunk), lambda i: (0, 0, 0)),
            pl.BlockSpec((chunk, da), const2),
            pl.BlockSpec((3, db), const2),
        ],
        [
            jax.ShapeDtypeStruct((s, n), BF16),
            jax.ShapeDtypeStruct((1, da), F32),
            jax.ShapeDtypeStruct((heads, chunk, chunk), F32),
            jax.ShapeDtypeStruct((chunk, da), F32),
            jax.ShapeDtypeStruct((3, db), F32),
        ],
        [proj, proj, proj, dcat, dcat, norm_v, w_s, b_rows, conv_w],
        scratch=[pltpu.VMEM((tm, da), F32)], phases=phases,
    )


def _pool_counts(tm, i, w):
    t = i * tm + lax.broadcasted_iota(jnp.int32, (tm, 1), 0)
    return jnp.minimum(t + 1, w).astype(F32)


def _pool_fwd(x, vec, w_grp, scale, phases=()):
    s, d = x.shape
    groups, gd, _ = w_grp.shape
    tm = _pick(s, (512, 256, 128))

    def body(x_ref, xa_ref, vec_ref, w_ref, sc_ref, xo_ref, p_ref, o_ref):
        i = pl.program_id(0)
        h = _modulate(x_ref[...], vec_ref)
        ha = jnp.where(i == 0, 0.0, _modulate(xa_ref[...], vec_ref))
        ext = jnp.concatenate([ha, h], axis=0)
        for gi, w in enumerate(POOL_WINDOWS):
            cols = slice(gi * gd, (gi + 1) * gd)
            acc = ext[:, cols]
            step = 1
            while step < w:
                acc = acc + pltpu.roll(acc, step, 0)
                step *= 2
            p = (acc[POOL_HALO:, :] / _pool_counts(tm, i, w) - h[:, cols]).astype(BF16)
            p_ref[:, cols] = p
            o_ref[:, cols] = _dot(p, w_ref[gi]).astype(BF16)
        xo_ref[...] = x_ref[...] + vec_ref[3:4, :] * (o_ref[...].astype(F32) * sc_ref[...])

    nh = tm // POOL_HALO
    row = pl.BlockSpec((tm, d), lambda i: (i, 0))
    return _call(
        body, "pool_fwd", (s // tm,),
        [
            row,
            pl.BlockSpec((POOL_HALO, d), lambda i: (jnp.maximum(i * nh - 1, 0), 0)),
            pl.BlockSpec((8, d), lambda i: (0, 0)),
            pl.BlockSpec((groups, gd, gd), lambda i: (0, 0, 0)),
            pl.BlockSpec((1, d), lambda i: (0, 0)),
        ],
        [row, row, row],
        [jax.ShapeDtypeStruct((s, d), F32), jax.ShapeDtypeStruct((s, d), BF16), jax.ShapeDtypeStruct((s, d), BF16)],
        [x, x, vec, w_grp, scale], phases=phases,
    )


def _pool_bwd(dxo, x, vec, p, o, w_grp, scale, phases=()):
    s, d = x.shape
    groups, gd, _ = w_grp.shape
    tm = _pick(s, (512, 256, 128))
    nblk = s // tm

    def body(dxo_ref, dxb_ref, x_ref, vec_ref, p_ref, o_ref, w_ref, sc_ref, dx_ref, dw_ref, dsc_ref, dvec_ref, dw_sc):
        i = pl.program_id(0)

        @pl.when(i == 0)
        def _():
            dw_sc[...] = jnp.zeros_like(dw_sc)
            dsc_ref[...] = jnp.zeros_like(dsc_ref)
            dvec_ref[...] = jnp.zeros_like(dvec_ref)

        gate, sc = vec_ref[3:4, :], sc_ref[...]
        dxo_v = dxo_ref[...]
        ov = o_ref[...].astype(F32)
        dvec_ref[3:4, :] += jnp.sum(dxo_v * (ov * sc), axis=0, keepdims=True)
        dy = gate * dxo_v
        dsc_ref[...] += jnp.sum(dy * ov, axis=0, keepdims=True)
        dout = (dy * sc).astype(BF16)
        dout_b = jnp.where(i == nblk - 1, 0.0, gate * dxb_ref[...] * sc).astype(BF16)
        for gi, w in enumerate(POOL_WINDOWS):
            cols = slice(gi * gd, (gi + 1) * gd)
            dw_sc[gi] += _dot_tn(p_ref[:, cols], dout[:, cols])
            wb = w_ref[gi]
            dp = _dot_nt(dout[:, cols], wb)
            dp_b = _dot_nt(dout_b[:, cols], wb)
            e = dp / _pool_counts(tm, i, w)
            t_below = (i + 1) * tm + lax.broadcasted_iota(jnp.int32, (POOL_HALO, 1), 0)
            e_b = dp_b / jnp.minimum(t_below + 1, w).astype(F32)
            acc = jnp.concatenate([e, e_b], axis=0)
            step = 1
            while step < w:
                acc = acc + pltpu.roll(acc, tm + POOL_HALO - step, 0)
                step *= 2
            dx_ref[:, cols] = acc[:tm, :] - dp
        dx, _ = _modulate_bwd(x_ref[...], dx_ref[...], vec_ref, dvec_ref)
        dx_ref[...] = dxo_v + dx

        @pl.when(i == nblk - 1)
        def _():
            dw_ref[...] = dw_sc[...].astype(BF16)

    nh = tm // POOL_HALO
    row = pl.BlockSpec((tm, d), lambda i: (i, 0))
    vecs = pl.BlockSpec((8, d), lambda i: (0, 0))
    wspec = pl.BlockSpec((groups, gd, gd), lambda i: (0, 0, 0))
    return _call(
        body, "pool_bwd", (nblk,),
        [
            row,
            pl.BlockSpec((POOL_HALO, d), lambda i: (jnp.minimum((i + 1) * nh, s // POOL_HALO - 1), 0)),
            row, vecs, row, row, wspec,
            pl.BlockSpec((1, d), lambda i: (0, 0)),
        ],
        [row, wspec, pl.BlockSpec((1, d), lambda i: (0, 0)), vecs],
        [
            jax.ShapeDtypeStruct((s, d), F32),
            jax.ShapeDtypeStruct((groups, gd, gd), BF16),
            jax.ShapeDtypeStruct((1, d), F32),
            jax.ShapeDtypeStruct((8, d), F32),
        ],
        [dxo, dxo, x, vec, p, o, w_grp, scale],
        scratch=[pltpu.VMEM((groups, gd, gd), F32)], phases=phases,
    )


def _loss_head(x, gain, target, phases=()):
    s, d = x.shape
    tm = _pick(s, (512, 256, 128))

    def body(x_ref, g_ref, t_ref, dx_ref, aux_ref):
        @pl.when(pl.program_id(0) == 0)
        def _():
            aux_ref[...] = jnp.zeros_like(aux_ref)

        xv = x_ref[...]
        rstd = _rstd(xv)
        r = xv * rstd
        gain_v = g_ref[...]
        err = r * gain_v - t_ref[...]
        aux_ref[1:2, :] += jnp.sum(err * err, axis=0, keepdims=True)
        dout = err * (1.0 / d)
        aux_ref[0:1, :] += jnp.sum(dout * r, axis=0, keepdims=True)
        dr = dout * gain_v
        dx_ref[...] = rstd * (dr - r * jnp.mean(dr * r, axis=-1, keepdims=True))

    row = pl.BlockSpec((tm, d), lambda i: (i, 0))
    return _call(
        body, "loss_head", (s // tm,),
        [row, pl.BlockSpec((1, d), lambda i: (0, 0)), row],
        [row, pl.BlockSpec((8, d), lambda i: (0, 0))],
        [jax.ShapeDtypeStruct((s, d), F32), jax.ShapeDtypeStruct((8, d), F32)], [x, gain, target], phases=phases,
    )


def _small_adam(gathered, gathered_ws, layout, smalls, chip):
    names = list(smalls)
    n = len(names)
    loss_row, _, _, n_feat = layout["loss"]

    def body(*refs):
        chip_ref, g_ref, gws_ref = refs[0], refs[1], refs[2]
        wmv = refs[3 : 3 + 3 * n]
        outs = refs[3 + 3 * n : 3 + 7 * n]
        total = refs[-1]
        total[...] = g_ref[0]
        for kdev in range(1, N_DEV):
            total[...] += g_ref[kdev]
        total_ws = gws_ref[0]
        for kdev in range(1, N_DEV):
            total_ws = total_ws + gws_ref[kdev]
        my_chip = chip_ref[0]
        for a, name in enumerate(names):
            w_ref, m_ref, v_ref = wmv[3 * a : 3 * a + 3]
            if name == "ab_w_s":
                g = total_ws
            else:
                row0, rows, col0, cols = layout[name]
                if col0 is None:
                    g = jnp.zeros((rows, cols), F32)
                    for j in range(N_CHIPS):
                        g = g + jnp.where(my_chip == j, total[row0 : row0 + rows, j * cols : (j + 1) * cols], 0.0)
                else:
                    g = total[row0 : row0 + rows, col0 : col0 + cols]
            dl, mo, vo = _adam(w_ref[...], g, m_ref[...], v_ref[...])
            outs[4 * a][...] = g
            outs[4 * a + 1][...] = dl
            outs[4 * a + 2][...] = mo
            outs[4 * a + 3][...] = vo
        refs[3 + 7 * n][...] = 0.5 * jnp.sum(total[loss_row : loss_row + 1, 0:n_feat], axis=1, keepdims=True) / n_feat

    ins = [gathered, gathered_ws]
    out_shapes = []
    for name in names:
        ins.extend(smalls[name])
        out_shapes.extend([jax.ShapeDtypeStruct(smalls[name][0].shape, F32)] * 4)
    out_shapes.append(jax.ShapeDtypeStruct((1, 1), F32))
    whole = lambda shape: pl.BlockSpec(shape, functools.partial(lambda nd, i, c: (0,) * nd, len(shape)))
    res = pl.pallas_call(
        body, name="small_adam",
        grid_spec=pltpu.PrefetchScalarGridSpec(
            num_scalar_prefetch=1, grid=(1,),
            in_specs=[whole(a.shape) for a in ins], out_specs=[whole(o.shape) for o in out_shapes],
            scratch_shapes=[pltpu.VMEM(gathered.shape[1:], F32)],
        ),
        out_shape=out_shapes,
        compiler_params=pltpu.CompilerParams(dimension_semantics=("arbitrary",), vmem_limit_bytes=VMEM_LIMIT_BYTES),
    )(chip.reshape(1).astype(jnp.int32), *ins)
    return {name: res[4 * a : 4 * a + 4] for a, name in enumerate(names)}, res[4 * n]


def _pad_rows(a, rows=8):
    extra = (-a.shape[0]) % rows
    return jnp.pad(a, ((0, extra), (0, 0))) if extra else a


def _pad_cols(a, cols):
    return jnp.pad(a, ((0, 0), (0, cols - a.shape[1]))) if a.shape[1] < cols else a


def _run(fn, *phases):
    outs, p_outs = fn(list(phases))
    for p, po in zip(phases, p_outs):
        p.then(po)
    return outs


def kernel(x, c, norm_g, w_mod, b_mod, w_ffn_in, w_ffn_out, ab_w_in, ab_norm_v, ab_w_s, ab_b_s, ab_conv_w, ab_w_out, pool_w_grp, pool_scale, final_g, loss_target, m_norm_g, m_w_mod, m_b_mod, m_w_ffn_in, m_w_ffn_out, m_ab_w_in, m_ab_norm_v, m_ab_w_s, m_ab_b_s, m_ab_conv_w, m_ab_w_out, m_pool_w_grp, m_pool_scale, m_final_g, v_norm_g, v_w_mod, v_b_mod, v_w_ffn_in, v_w_ffn_out, v_ab_w_in, v_ab_norm_v, v_ab_w_s, v_ab_b_s, v_ab_conv_w, v_ab_w_out, v_pool_w_grp, v_pool_scale, v_final_g):
    ix, iy, ic = _place()
    chip = 2 * ix + iy
    me = 4 * ix + 2 * iy + ic
    where = jnp.stack([chip, ic]).astype(jnp.int32)
    s, d = x.shape[1], x.shape[2]
    x0 = x.reshape(s, d)
    target = loss_target.reshape(s, d)
    n_layers = norm_g.shape[0]
    dq = d // N_CHIPS
    heads, chunk = ab_w_s.shape[1], ab_w_s.shape[2]
    da = ab_norm_v.shape[1]
    db = ab_conv_w.shape[2] * N_CHIPS
    f_hidden = w_ffn_out.shape[2] * N_CHIPS
    assert n_layers == 2 and da % heads == 0

    cw_pad = _pad_cols(ab_conv_w.reshape(3, db // N_CHIPS), dq)
    packed = jnp.concatenate(
        [_pad_rows(c.reshape(N_CHIPS, dq)), _pad_rows(norm_g.reshape(-1, dq)), _pad_rows(pool_scale.reshape(1, dq)), _pad_rows(cw_pad)],
        axis=0,
    )
    ncol = w_mod.shape[2]
    b_cols = lax.dynamic_slice(b_mod, (0, chip * ncol), (n_layers, ncol)).reshape(n_layers, 1, ncol)
    small = {}

    def small_gather(key, arrs):
        def then(outs):
            small[key] = outs

        return _phase_small_gather(arrs, then)

    stacks = {
        "w_ffn_in": tuple(a.reshape((-1,) + a.shape[2:]) for a in (w_ffn_in, m_w_ffn_in, v_w_ffn_in)),
        "w_ffn_out": tuple(a.reshape((-1,) + a.shape[2:]) for a in (w_ffn_out, m_w_ffn_out, v_w_ffn_out)),
        "ab_w_in": (ab_w_in, m_ab_w_in, v_ab_w_in),
        "ab_w_out": (ab_w_out, m_ab_w_out, v_ab_w_out),
        "pool_w_grp": (pool_w_grp[0], m_pool_w_grp[0], v_pool_w_grp[0]),
    }
    big_in = _Big((1, d, 2 * f_hidden), 2, 1)
    big_out = _Big((1, f_hidden, d), 1, 2)
    units = {}
    for l in range(n_layers):
        for k in range(2):
            units[f"in{l}{k}"] = (big_in, "w_ffn_in", 2 * l + k)
            units[f"out{l}{k}"] = (big_out, "w_ffn_out", 2 * l + k)
    units["abin"] = (_Big((1, d, ab_w_in.shape[2] * N_CHIPS), 2, 1), "ab_w_in", 0)
    units["about"] = (_Big((1, ab_w_out.shape[1] * N_CHIPS, d), 1, 2), "ab_w_out", 0)
    units["pool"] = (_Big((pool_w_grp.shape[1], pool_w_grp.shape[2] * N_CHIPS, pool_w_grp.shape[3]), 1, 0), "pool_w_grp", 0)
    big = {u: g for u, (g, _, _) in units.items()}

    weight = {}
    complete = set()

    def cast(u):
        g, st, b0 = units[u]

        def launch(phases):
            (weight[u],), p_outs = _cast_into_full(stacks[st][0], b0, g, where, "cast_" + u, phases)
            return None, p_outs

        return launch

    def gather_ici(*us):
        def then(outs):
            for u, o in zip(us, outs):
                weight[u] = o

        return _phase_gather_ici([weight[u] for u in us], [big[u] for u in us], then)

    def gather_sibling(*us):
        def then(outs):
            for u, o in zip(us, outs):
                weight[u] = o
                complete.add(u)

        return _phase_gather_sibling([weight[u] for u in us], [big[u] for u in us], then)

    def w_of(u):
        assert u in complete, u
        return weight[u]

    _run(cast("in00"), small_gather("inputs", [packed]))
    _run(cast("out00"))
    small_all = small["inputs"][0]
    by_chip = small_all[0::2]
    c_all = small_all[:, 0:N_CHIPS, :].reshape(N_DEV, d)
    norm_full = by_chip[:, 8 : 8 + 3 * n_layers, :].transpose(1, 0, 2).reshape(3 * n_layers, d)
    pool_scale_full = by_chip[:, 16:17, :].transpose(1, 0, 2).reshape(1, d)
    conv_full = by_chip[:, 24:27, : db // N_CHIPS].transpose(1, 0, 2).reshape(3, db)
    pieces = [("in00", "out00"), ("abin", "about"), ("in01", "out01"), ("in10", "out10", "pool"), ("in11", "out11")]
    in_flight = {}

    def start_gather(p):
        in_flight[p] = _split_start(gather_ici(*pieces[p]), f"gather_{p}_start")

    def started():
        return _after(*[flight.token for flight in in_flight.values()])

    def finish_gather(p, after, meanwhile=None):
        flight = in_flight.pop(p)
        _split_wait(flight, list(after) + list(started().ins), f"gather_{p}_wait")
        crossing = _split_start(gather_sibling(*pieces[p]), f"gather_{p}_forward")
        behind = [crossing.token]
        if p + 2 < len(pieces):
            for u in pieces[p + 2]:
                _run(cast(u), _after(crossing.token))
            start_gather(p + 2)
            behind = list(started().ins)
        if meanwhile is not None:
            behind = behind + meanwhile(_after(crossing.token))
        _split_wait(crossing, behind, f"gather_{p}_forwarded")

    start_gather(0)
    mod_cols = _run(lambda phases: _mod_fwd(c_all, w_mod, b_cols, phases), started())[0]
    _run(cast("abin"), small_gather("mod", [mod_cols.reshape(n_layers * N_DEV, ncol)]), started())
    _run(cast("about"), started())
    start_gather(1)
    mod_all = small["mod"][0]
    mod_mine = lax.dynamic_index_in_dim(mod_all[0::2].reshape(N_CHIPS, n_layers, N_DEV, ncol), me, axis=2, keepdims=False)
    mod = mod_mine.transpose(1, 0, 2).reshape(n_layers, 3, 3, d)
    vecs = {
        (l, sub): jnp.pad(norm_full[3 * l + sub][None], ((0, 7), (0, 0))) + jnp.pad(mod[l, sub], ((1, 4), (0, 0)))
        for l in range(n_layers)
        for sub in range(3)
    }
    b_rows = jnp.broadcast_to(ab_b_s[0].T[:, :, None], (chunk, heads, da // heads)).reshape(chunk, da)

    saved = {}

    def ffn_forward(xs, l, sub, k, *phases):
        saved[l, sub, "x"] = xs
        xs, gg, uu, yb = _run(
            lambda ph: _ffn_fwd(xs, vecs[l, sub], w_of(f"in{l}{k}"), w_of(f"out{l}{k}"), f"ffn_fwd_{l}{k}", ph), *phases
        )
        saved[l, sub, "act"] = (gg, uu, yb)
        return xs

    finish_gather(0, [vecs[0, 0]])
    xs = ffn_forward(x0, 0, 0, 0, started())
    saved[0, 1, "x"] = xs
    finish_gather(1, [xs])
    (proj,) = _run(lambda ph: _proj_mod_fwd(xs, vecs[0, 1], w_of("abin"), ph), started())
    (cat,) = _run(lambda ph: _ab_mix_fwd(proj, ab_norm_v, ab_w_s[0], b_rows, conv_full, ph))
    xs, yb = _run(lambda ph: _proj_res_fwd(cat, w_of("about"), xs, vecs[0, 1], ph))
    saved[0, 1, "act"] = (proj, cat, yb)
    finish_gather(2, [xs])
    xs = ffn_forward(xs, 0, 2, 1, started())
    finish_gather(3, [xs])
    xs = ffn_forward(xs, 1, 0, 0, started())
    saved[1, 1, "x"] = xs
    pooled = []

    def pool_forward(behind):
        pooled.extend(_run(lambda ph: _pool_fwd(xs, vecs[1, 1], w_of("pool"), pool_scale_full, ph), behind))
        return [pooled[0]]

    finish_gather(4, [xs], pool_forward)
    xs, pp, oo = pooled
    saved[1, 1, "act"] = (pp, oo)
    xs = ffn_forward(xs, 1, 2, 1)
    dxs, aux = _run(lambda ph: _loss_head(xs, final_g.reshape(1, d), target, ph))

    grad = {}
    recv = {}
    csum = {}
    parts = {}
    reduced = {}
    done = set()
    dvecs, small_g = {}, {}

    def pair_exchange(*us):
        def then(outs):
            for u, o in zip(us, outs):
                recv[u] = o

        return _phase_pair_exchange([grad[u] for u in us], [big[u] for u in us], then)

    def grad_half(u, a, bs, mine, name, *phases):
        (res,) = _run(lambda ph: _grad_half(a, bs, big[u], where, mine, recv[u] if mine else None, name, ph), *phases)
        return res

    def pair_sum(u, *phases):
        def launch(ph):
            (csum[u],), p_outs = _pair_sum(grad[u], recv[u], big[u], where, "pair_sum_" + u, ph)
            return None, p_outs

        _run(launch, *phases)

    def chip_exchange(*us):
        def then(outs):
            for u, o in zip(us, outs):
                parts[u] = o

        return _phase_chip_exchange([csum[u] for u in us], [big[u] for u in us], then)

    def chip_sum(*us, carried=()):
        for n_u, u in enumerate(us):
            g, st, b0 = units[u]

            def launch(ph):
                (reduced[st],), p_outs = _chip_sum(
                    csum[u], parts[u], g, where, reduced.get(st), stacks[st][0].shape, b0, "chip_sum_" + u, ph
                )
                return None, p_outs

            _run(launch, *(carried if n_u == 0 else ()))

    def pair_broadcast(*us):
        sts = [units[u][1] for u in us]
        assert len(set(sts)) == len(sts)

        def then(outs):
            for u, st, o in zip(us, sts, outs):
                reduced[st] = o
                done.add(u)

        return _phase_pair_broadcast([reduced[st] for st in sts], [big[u] for u in us], [units[u][2] for u in us], then)

    def ffn_backward(dxs, l, sub, k, carried_bwd, carried_send, carried_mine):
        gg, uu, yb = saved[l, sub, "act"]
        w_in, w_out = w_of(f"in{l}{k}"), w_of(f"out{l}{k}")
        uo, ui, tag = f"out{l}{k}", f"in{l}{k}", f"{l}{k}"
        dxs, dg, du, a, h, dy, dvecs[l, sub] = _run(
            lambda ph: _ffn_bwd(dxs, saved[l, sub, "x"], vecs[l, sub], gg, uu, yb, w_in, w_out, "ffn_bwd_" + tag, ph), *carried_bwd()
        )
        grad[uo] = grad_half(uo, a, [dy], False, "dw_out_send_" + tag, *carried_send())
        grad[ui] = grad_half(ui, h, [dg, du], False, "dw_in_send_" + tag, pair_exchange(uo))
        csum[uo] = grad_half(uo, a, [dy], True, "dw_out_" + tag, pair_exchange(ui))
        csum[ui] = grad_half(ui, h, [dg, du], True, "dw_in_" + tag, *carried_mine())
        return dxs

    none = lambda: ()
    dxs = ffn_backward(dxs, 1, 2, 1, none, none, none)
    pp, oo = saved[1, 1, "act"]
    dxs, grad["pool"], small_g["pool_scale"], dvecs[1, 1] = _run(
        lambda ph: _pool_bwd(dxs, saved[1, 1, "x"], vecs[1, 1], pp, oo, w_of("pool"), pool_scale_full, ph)
    )

    def after_11():
        return (chip_exchange("in11", "out11"), pair_exchange("pool"))

    def bcast_11():
        chip_sum("in11", "out11")
        pair_sum("pool")
        return (pair_broadcast("in11", "out11"), chip_exchange("pool"))

    dxs = ffn_backward(dxs, 1, 0, 0, after_11, bcast_11, none)

    def after_10():
        return (chip_exchange("in10", "out10"),)

    def bcast_10():
        chip_sum("in10", "out10", "pool")
        return (pair_broadcast("in10", "out10", "pool"),)

    dxs = ffn_backward(dxs, 0, 2, 1, after_10, bcast_10, none)

    proj, cat, yb = saved[0, 1, "act"]
    dy, dcat, dgate = _run(lambda ph: _proj_res_bwd(dxs, yb, vecs[0, 1], w_of("about"), ph))
    grad["about"] = grad_half("about", cat, [dy], False, "dw_ab_out_send")
    dproj, small_g["ab_norm_v"], small_g["ab_w_s"], dzs, small_g["ab_conv_w"] = _run(
        lambda ph: _ab_mix_bwd(proj, dcat, ab_norm_v, ab_w_s[0], b_rows, conv_full, ph), chip_exchange("out01"), pair_exchange("about")
    )
    small_g["ab_b_s"] = dzs.reshape(chunk, heads, da // heads).sum(axis=2).T
    dxs, h, dvecs[0, 1] = _run(
        lambda ph: _proj_mod_bwd(dproj[None], w_of("abin"), saved[0, 1, "x"], vecs[0, 1], dxs, dgate, "ab_in_bwd", ph)
    )
    grad["abin"] = grad_half("abin", h, [dproj], False, "dw_ab_in_send")
    chip_sum("out01", carried=(pair_exchange("abin"),))
    csum["about"] = grad_half("about", cat, [dy], True, "dw_ab_out", pair_broadcast("out01"))
    csum["abin"] = grad_half("abin", h, [dproj], True, "dw_ab_in")

    def after_01():
        return (chip_exchange("in01", "abin", "about"),)

    layout = {}
    tail = {}

    def pack_small_grads():
        dvec_all = jnp.stack([dvecs[l, sub] for l in range(n_layers) for sub in range(3)])
        dgain = dvec_all[:, 0, :]
        dmod = dvec_all[:, 1:4, :].reshape(3 * 3 * n_layers, d)
        rows = {
            "norm_g": (dgain, None, dq), "final_g": (aux[0:1], 0, d), "pool_scale": (small_g["pool_scale"], None, dq),
            "b_mod": (dmod, 0, d), "ab_norm_v": (small_g["ab_norm_v"], 0, da),
            "ab_conv_w": (small_g["ab_conv_w"], None, db // N_CHIPS), "ab_b_s": (small_g["ab_b_s"], 0, chunk),
            "loss": (aux[1:2], 0, d),
        }
        row0 = 0
        for nm, (pc, col0, cols) in rows.items():
            layout[nm] = (row0, pc.shape[0], col0, cols)
            row0 += pc.shape[0]
        packed_rows = -(-row0 // 8) * 8
        return sum(
            jnp.pad(pc, ((layout[nm][0], packed_rows - layout[nm][0] - pc.shape[0]), (0, d - pc.shape[1])))
            for nm, (pc, _, _) in rows.items()
        )

    def bcast_01():
        chip_sum("in01", "abin", "about")
        grads_small = [pack_small_grads(), small_g["ab_w_s"].reshape(heads * chunk, chunk)]
        tail["small"] = _split_start(small_gather("grads", grads_small), "gather_small_grads_start")
        return (pair_broadcast("in01", "abin", "about"), _after(tail["small"].token))

    def reduce_out00():
        tail["out00"] = _split_start(chip_exchange("out00"), "reduce_out00_start")
        return (_after(tail["out00"].token),)

    dxs = ffn_backward(dxs, 0, 0, 0, after_01, bcast_01, reduce_out00)
    grad_x = dxs.reshape(x.shape)

    last = _split_start(chip_exchange("in00"), "reduce_last_start")
    (csum["out00"],) = _split_wait(tail["out00"], [last.token], "reduce_out00_wait")
    chip_sum("out00")
    _flush("broadcast_out00", pair_broadcast("out00"))
    _split_wait(tail["small"], [reduced["w_ffn_out"]], "gather_small_grads_wait")
    g_all, gws_all = small["grads"]

    out = {}

    def adam_stack(st, after=()):
        w3, m3, v3 = stacks[st]
        assert all(u in done for u, (_, ust, _) in units.items() if ust == st), st
        shape = {"w_ffn_in": w_ffn_in.shape, "w_ffn_out": w_ffn_out.shape, "pool_w_grp": pool_w_grp.shape}.get(st, w3.shape)
        out[st] = tuple(a.reshape(shape) for a in _adam_stack(w3, reduced[st], m3, v3, "adam_" + st, after))

    for st in ("w_ffn_out", "ab_w_in", "ab_w_out", "pool_w_grp"):
        adam_stack(st, (last.token,))

    shapes2d = {
        "norm_g": (3 * n_layers, dq), "b_mod": (9 * n_layers, d), "final_g": (1, d), "ab_norm_v": (1, da),
        "pool_scale": (1, dq), "ab_conv_w": (3, db // N_CHIPS), "ab_b_s": (heads, chunk), "ab_w_s": (heads * chunk, chunk),
    }
    small_w = {"norm_g": (norm_g, m_norm_g, v_norm_g), "b_mod": (b_mod, m_b_mod, v_b_mod), "final_g": (final_g, m_final_g, v_final_g),
               "ab_norm_v": (ab_norm_v, m_ab_norm_v, v_ab_norm_v), "pool_scale": (pool_scale, m_pool_scale, v_pool_scale),
               "ab_conv_w": (ab_conv_w, m_ab_conv_w, v_ab_conv_w), "ab_b_s": (ab_b_s, m_ab_b_s, v_ab_b_s), "ab_w_s": (ab_w_s, m_ab_w_s, v_ab_w_s)}
    smalls = {nm: tuple(a.reshape(shapes2d[nm]) for a in wmv) for nm, wmv in small_w.items()}
    small_out, loss = _small_adam(g_all, gws_all, layout, smalls, chip)
    loss = loss.reshape(())
    for nm, res in small_out.items():
        out[nm] = tuple(a.reshape(small_w[nm][0].shape) for a in res)

    mod_row0 = layout["b_mod"][0]
    dmod_all = g_all[:, mod_row0 : mod_row0 + 9 * n_layers, :].reshape(N_DEV, n_layers, 9 * d)
    dmod_cols = lax.dynamic_slice(dmod_all, (0, 0, chip * ncol), (N_DEV, n_layers, ncol)).transpose(1, 0, 2)
    out["w_mod"] = tuple(_mod_bwd_adam(c_all.T, dmod_cols, w_mod, m_w_mod, v_w_mod, (last.token,)))

    (csum["in00"],) = _split_wait(
        last, [out[st][1] for st in ("w_mod", "w_ffn_out", "ab_w_in", "ab_w_out", "pool_w_grp")], "reduce_last_wait"
    )
    chip_sum("in00")
    _flush("broadcast_last", pair_broadcast("in00"))
    adam_stack("w_ffn_in")

    order = ["norm_g", "w_mod", "b_mod", "w_ffn_in", "w_ffn_out", "ab_w_in", "ab_norm_v", "ab_w_s", "ab_b_s", "ab_conv_w", "ab_w_out", "pool_w_grp", "pool_scale", "final_g"]
    return (loss, grad_x, *[out[nm][0] for nm in order], *[out[nm][1] for nm in order], *[out[nm][2] for nm in order], *[out[nm][3] for nm in order])
```

```python
import functools
import math

import jax
import jax.numpy as jnp
from jax import lax
from jax.experimental import pallas as pl
from jax.experimental.pallas import tpu as pltpu

F32 = jnp.float32
BF16 = jnp.bfloat16
MESH = pl.DeviceIdType.MESH

EPS = 1e-6
ADAM_LR = 0.001
ADAM_B1 = 0.9
ADAM_B2 = 0.999
ADAM_EPS = 1e-08
ADAM_WD = 0.01
ADAM_STEP = 10
POOL_WINDOWS = (2, 4, 8, 16)
POOL_HALO = 16
CONV_HALO = 8
N_CHIPS = 4
N_DEV = 8
VMEM_LIMIT_BYTES = 48 * 1024 * 1024
EW_BLOCK_ELEMS = 256 * 1024


def _pick(n, prefs):
    for p in prefs:
        if p <= n and n % p == 0:
            return p
    return n


def _row_tile(rows, cols):
    best = None
    for d in range(16, rows + 1, 16):
        if rows % d == 0 and d * cols <= EW_BLOCK_ELEMS:
            best = d
    return best or rows


def _dot(a, b):
    return jnp.dot(a, b, preferred_element_type=F32)


def _dot_nt(a, b):
    return lax.dot_general(a, b, (((1,), (1,)), ((), ())), preferred_element_type=F32)


def _dot_tn(a, b):
    return lax.dot_general(a, b, (((0,), (0,)), ((), ())), preferred_element_type=F32)


def _sigmoid(x):
    return 0.5 * jnp.tanh(0.5 * x) + 0.5


_GELU_C = math.sqrt(2.0 / math.pi)


def _gelu(x):
    x2 = x * x
    t = jnp.tanh(_GELU_C * (x + 0.044715 * x2 * x))
    val = 0.5 * x * (1.0 + t)
    grad = 0.5 * (1.0 + t) + 0.5 * x * (1.0 - t * t) * (_GELU_C * (1.0 + 3.0 * 0.044715 * x2))
    return val, grad


def _rstd(x):
    return lax.rsqrt(jnp.mean(x * x, axis=-1, keepdims=True) + EPS)


def _modulate(x, vec_ref):
    return (x * _rstd(x)) * vec_ref[0:1, :] * (1.0 + vec_ref[2:3, :]) + vec_ref[1:2, :]


def _modulate_bwd(x, dh, vec_ref, dvec_ref):
    gn, sh, sc = vec_ref[0:1, :], vec_ref[1:2, :], vec_ref[2:3, :]
    rstd = _rstd(x)
    r = x * rstd
    dvec_ref[0:1, :] += jnp.sum(dh * r * (1.0 + sc), axis=0, keepdims=True)
    dvec_ref[1:2, :] += jnp.sum(dh, axis=0, keepdims=True)
    dvec_ref[2:3, :] += jnp.sum(dh * r * gn, axis=0, keepdims=True)
    gm = gn * (1.0 + sc)
    dr = dh * gm
    dx = rstd * (dr - r * jnp.mean(dr * r, axis=-1, keepdims=True))
    return dx, r * gm + sh


def _adam(w, g, m, v):
    m = ADAM_B1 * m + (1.0 - ADAM_B1) * g
    v = ADAM_B2 * v + (1.0 - ADAM_B2) * (g * g)
    m_hat = m / (1.0 - ADAM_B1**ADAM_STEP)
    v_hat = v / (1.0 - ADAM_B2**ADAM_STEP)
    delta = -ADAM_LR * (m_hat / (jnp.sqrt(v_hat) + ADAM_EPS) + ADAM_WD * w)
    return delta, m, v


_ANY = pl.BlockSpec(memory_space=pl.ANY)


class _Phase:
    def __init__(self, ins, out_shapes, aliases, n_sems, start, finish, then):
        self.ins, self.out_shapes, self.aliases, self.n_sems = list(ins), list(out_shapes), dict(aliases), n_sems
        self.start, self.finish, self.then = start, finish, then


def _call(body, name, grid, in_specs, out_specs, out_shape, ins, scratch=(), prefetch=(), phases=(), in_place=None):
    n_pre, n_in, n_out, n_sc = len(prefetch), len(in_specs), len(out_specs), len(scratch)
    ph_in = [len(p.ins) for p in phases]
    ph_out = [len(p.out_shapes) for p in phases]

    def kernel_body(*refs):
        pos = [0]

        def take(k):
            pos[0] += k
            return refs[pos[0] - k : pos[0]]

        pre, ins_ = take(n_pre), take(n_in)
        p_ins = [take(k) for k in ph_in]
        outs_ = take(n_out)
        p_outs = [take(k) for k in ph_out]
        sc = take(n_sc)
        sems = [take(2) for _ in phases]
        if phases:
            ids = [pl.program_id(a) for a in range(len(grid))]
            first = functools.reduce(jnp.logical_and, [i == 0 for i in ids])
            last = functools.reduce(jnp.logical_and, [i == g - 1 for i, g in zip(ids, grid)])

            @pl.when(first)
            def _():
                for p, pi, po, (send, recv) in zip(phases, p_ins, p_outs, sems):
                    p.start(pi, po, send, recv)

        if body is not None:
            body(*pre, *ins_, *outs_, *sc)
        if phases:

            @pl.when(last)
            def _():
                for p, pi, po, (send, recv) in zip(phases, p_ins, p_outs, sems):
                    p.finish(pi, po, send, recv)

    aliases = {n_pre + i: o for i, o in (in_place or {}).items()}
    i0, o0 = n_pre + n_in, n_out
    for p in phases:
        for i, o in p.aliases.items():
            aliases[i0 + i] = o0 + o
        i0 += len(p.ins)
        o0 += len(p.out_shapes)
    all_in = list(in_specs) + [_ANY] * sum(ph_in)
    all_out = list(out_specs) + [_ANY] * sum(ph_out)
    all_scratch = list(scratch)
    for p in phases:
        all_scratch += [pltpu.SemaphoreType.DMA((p.n_sems,)), pltpu.SemaphoreType.DMA((p.n_sems,))]
    shapes = list(out_shape) + [s for p in phases for s in p.out_shapes]
    operands = list(prefetch) + list(ins) + [a for p in phases for a in p.ins]
    sem = ("arbitrary",) * len(grid)
    params = pltpu.CompilerParams(dimension_semantics=sem, vmem_limit_bytes=VMEM_LIMIT_BYTES)
    if n_pre:
        res = pl.pallas_call(
            kernel_body, name=name, out_shape=shapes, input_output_aliases=aliases, compiler_params=params,
            grid_spec=pltpu.PrefetchScalarGridSpec(
                num_scalar_prefetch=n_pre, grid=grid, in_specs=all_in, out_specs=all_out, scratch_shapes=all_scratch
            ),
        )(*operands)
    else:
        res = pl.pallas_call(
            kernel_body, name=name, grid=grid, in_specs=all_in, out_specs=all_out, out_shape=shapes,
            scratch_shapes=all_scratch, input_output_aliases=aliases, compiler_params=params,
        )(*operands)
    res = list(res)
    outs, rest = res[:n_out], res[n_out:]
    p_res = []
    for k in ph_out:
        p_res.append(rest[:k])
        rest = rest[k:]
    return outs, p_res


def _place():
    return lax.axis_index("x"), lax.axis_index("y"), lax.axis_index("c")


def _other_chips():
    x, y, _ = _place()
    return [(1 - x, y), (x, 1 - y), (1 - x, 1 - y)]


def _flip(k):
    x, y, c = _place()
    return (1 - x if k & 4 else x, 1 - y if k & 2 else y, 1 - c if k & 1 else c)


def _remote(src, dst, send, recv, k, to):
    return pltpu.make_async_remote_copy(
        src_ref=src, dst_ref=dst, send_sem=send.at[k], recv_sem=recv.at[k], device_id=to, device_id_type=MESH
    )


def _phase_small_gather(arrs, then):
    n = len(arrs)

    def copies(ins, outs, send, recv):
        x, y, c = _place()
        me = 4 * x + 2 * y + c
        local = [pltpu.make_async_copy(ins[a], outs[a].at[me], send.at[a * N_DEV]) for a in range(n)]
        remote = [_remote(ins[a], outs[a].at[me], send, recv, a * N_DEV + k, _flip(k)) for a in range(n) for k in range(1, N_DEV)]
        return local, remote

    def start(ins, outs, send, recv):
        local, remote = copies(ins, outs, send, recv)
        for cp in local + remote:
            cp.start()

    def finish(ins, outs, send, recv):
        local, remote = copies(ins, outs, send, recv)
        for cp in remote + local:
            cp.wait()

    shapes = [jax.ShapeDtypeStruct((N_DEV,) + a.shape, a.dtype) for a in arrs]
    return _Phase(arrs, shapes, {}, n * N_DEV, start, finish, then)


def _after(*arrs):
    nothing = lambda *args: None
    return _Phase(arrs, [], {}, 1, nothing, nothing, nothing)


def _flush(name, *phases):
    _, p_outs = _call(None, name, (1,), [], [], [], [], phases=list(phases))
    for p, po in zip(phases, p_outs):
        p.then(po)


class _Big:
    KINDS = {"full": (True, True), "half": (True, False), "shard": (False, True), "block": (False, False)}

    def __init__(self, f3, s3, h3):
        assert s3 != h3
        self.f3, self.s3, self.h3 = tuple(f3), s3, h3
        self.bd = tuple(f3[a] // (N_CHIPS if a == s3 else 1) // (2 if a == h3 else 1) for a in range(3))
        self.tile = (1, _row_tile(self.bd[1], self.bd[2]), self.bd[2])
        self.grid = tuple(self.bd[a] // self.tile[a] for a in range(3))

    def dims(self, kind):
        chips, halves = self.KINDS[kind]
        return tuple(
            self.bd[a] * (N_CHIPS if chips and a == self.s3 else 1) * (2 if halves and a == self.h3 else 1) for a in range(3)
        )

    def view(self, ref, chip=None, half=None, batch0=0, both_halves=True):
        start = [batch0, 0, 0]
        size = list(ref.shape)
        size[0] = self.bd[0] * (2 if self.h3 == 0 and both_halves else 1)
        if chip is not None:
            start[self.s3] += chip * self.bd[self.s3]
            size[self.s3] = self.bd[self.s3]
        if half is not None:
            start[self.h3] += half * self.bd[self.h3]
            size[self.h3] = self.bd[self.h3]
        return ref.at[tuple(pl.ds(st, sz) for st, sz in zip(start, size))]

    def spec(self, chip_from=None, half_from=None, lead=(), batch0=0):
        extra = "grid" in (chip_from, half_from)

        def index(*args):
            pref, idx = args[-1], list(args[int(extra) : -1])
            idx[0] += batch0
            if chip_from:
                idx[self.s3] += (pref[0] if chip_from == "pref" else args[0]) * self.grid[self.s3]
            if half_from:
                idx[self.h3] += (pref[1] if half_from == "pref" else args[0]) * self.grid[self.h3]
            return (0,) * len(lead) + tuple(idx)

        return pl.BlockSpec(tuple(lead) + self.tile, index)


def _same(arrs):
    return [jax.ShapeDtypeStruct(a.shape, a.dtype) for a in arrs]


def _phase_gather_ici(arrs, bigs, then):
    n = len(arrs)

    def copies(outs, send, recv, arriving):
        x, y, c = _place()
        return [
            _remote(blk, blk, send, recv, 3 * a + j, (*chip, c))
            for j, chip in enumerate(_other_chips())
            for a in range(n)
            for blk in [bigs[a].view(outs[a], 2 * chip[0] + chip[1] if arriving else 2 * x + y, c)]
        ]

    def start(ins, outs, send, recv):
        for cp in copies(outs, send, recv, False):
            cp.start()

    def finish(ins, outs, send, recv):
        for cp in copies(outs, send, recv, True):
            cp.wait_recv()
        for cp in copies(outs, send, recv, False):
            cp.wait_send()

    return _Phase(arrs, _same(arrs), {a: a for a in range(n)}, 3 * n, start, finish, then)


def _phase_gather_sibling(arrs, bigs, then):
    n = len(arrs)

    def copies(outs, send, recv, arriving):
        x, y, c = _place()
        return [
            _remote(blk, blk, send, recv, 3 * a + j, (x, y, 1 - c))
            for j, chip in enumerate(_other_chips())
            for a in range(n)
            for blk in [bigs[a].view(outs[a], 2 * chip[0] + chip[1], 1 - c if arriving else c)]
        ]

    def start(ins, outs, send, recv):
        for cp in copies(outs, send, recv, False):
            cp.start()

    def finish(ins, outs, send, recv):
        for cp in copies(outs, send, recv, True):
            cp.wait_recv()
        for cp in copies(outs, send, recv, False):
            cp.wait_send()

    return _Phase(arrs, _same(arrs), {a: a for a in range(n)}, 3 * n, start, finish, then)


def _phase_pair_exchange(grads, bigs, then):
    n = len(grads)

    def copies(ins, outs, send, recv):
        x, y, c = _place()
        srcs = [ins[a] if ins[a].shape == outs[a].shape else bigs[a].view(ins[a], None, 1 - c) for a in range(n)]
        return [_remote(srcs[a], outs[a], send, recv, a, (x, y, 1 - c)) for a in range(n)]

    def start(ins, outs, send, recv):
        for cp in copies(ins, outs, send, recv):
            cp.start()

    def finish(ins, outs, send, recv):
        for cp in copies(ins, outs, send, recv):
            cp.wait()

    shapes = [jax.ShapeDtypeStruct(b.dims("half"), BF16) for b in bigs]
    return _Phase(grads, shapes, {}, n, start, finish, then)


def _phase_chip_exchange(sums, bigs, then):
    n = len(sums)

    def copies(ins, outs, send, recv):
        _, _, c = _place()
        return [
            _remote(bigs[a].view(ins[a], 2 * chip[0] + chip[1], both_halves=False), outs[a].at[j], send, recv, 3 * a + j, (*chip, c))
            for j, chip in enumerate(_other_chips())
            for a in range(n)
        ]

    def start(ins, outs, send, recv):
        for cp in copies(ins, outs, send, recv):
            cp.start()

    def finish(ins, outs, send, recv):
        for cp in copies(ins, outs, send, recv):
            cp.wait()

    shapes = [jax.ShapeDtypeStruct((N_CHIPS - 1,) + b.dims("block"), BF16) for b in bigs]
    return _Phase(sums, shapes, {}, 3 * n, start, finish, then)


_HBM = pl.BlockSpec(memory_space=pltpu.HBM)
_SEM = pl.BlockSpec(memory_space=pltpu.SEMAPHORE)
_DATAFLOW = pltpu.SideEffectType.DATAFLOW_SIDE_EFFECTING


class _InFlight:
    def __init__(self, phase, send, recv, arrays, token):
        self.phase, self.send, self.recv, self.arrays, self.token = phase, send, recv, arrays, token


def _phase_results(phase, refs):
    n_in = len(phase.ins)
    updated = {o: i for i, o in phase.aliases.items()}
    fresh = [o for o in range(len(phase.out_shapes)) if o not in updated]
    return [refs[updated[o]] if o in updated else refs[n_in + fresh.index(o)] for o in range(len(phase.out_shapes))]


def _split_start(phase, name):
    n_in = len(phase.ins)
    fresh = [s for o, s in enumerate(phase.out_shapes) if o not in phase.aliases.values()]
    arrays = list(phase.ins) + [lax.empty(s.shape, s.dtype) for s in fresh]
    n = len(arrays)

    def body(*refs):
        phase.start(refs[:n_in], _phase_results(phase, refs[:n]), refs[n], refs[n + 1])
        refs[-1][...] = jnp.zeros_like(refs[-1])

    operands = [pltpu.with_memory_space_constraint(a, pltpu.HBM) for a in arrays]
    res = pl.pallas_call(
        body, name=name,
        out_shape=[pltpu.SemaphoreType.DMA((phase.n_sems,)), pltpu.SemaphoreType.DMA((phase.n_sems,))]
        + [pltpu.HBM(a.shape, a.dtype) for a in arrays] + [jax.ShapeDtypeStruct((8, 128), F32)],
        in_specs=[_HBM] * n, out_specs=[_SEM, _SEM] + [_HBM] * n + [pl.BlockSpec(memory_space=pltpu.VMEM)],
        input_output_aliases={i: 2 + i for i in range(n)},
        compiler_params=pltpu.CompilerParams(has_side_effects=_DATAFLOW),
    )(*operands)
    return _InFlight(phase, res[0], res[1], list(res[2 : 2 + n]), res[-1])


def _split_wait(flight, after, name):
    phase, n = flight.phase, len(flight.arrays)
    n_in = len(phase.ins)

    def body(*refs):
        phase.finish(refs[:n_in], _phase_results(phase, refs[:n]), refs[n], refs[n + 1])

    res = pl.pallas_call(
        body, name=name, out_shape=[pltpu.HBM(a.shape, a.dtype) for a in flight.arrays],
        in_specs=[_HBM] * n + [_SEM, _SEM] + [_ANY] * len(after), out_specs=[_HBM] * n,
        input_output_aliases={i: i for i in range(n)},
        compiler_params=pltpu.CompilerParams(has_side_effects=_DATAFLOW),
    )(*flight.arrays, flight.send, flight.recv, *after)
    res = list(res)
    phase.then(_phase_results(phase, res))
    return res[:n_in]


def _phase_pair_broadcast(stacks, bigs, batch0s, then):
    n = len(stacks)

    def start(ins, outs, send, recv):
        x, y, c = _place()
        for a in range(n):
            blk = bigs[a].view(outs[a], None, c, batch0s[a])
            _remote(blk, blk, send, recv, a, (x, y, 1 - c)).start()

    def finish(ins, outs, send, recv):
        x, y, c = _place()
        for a in range(n):
            mine = bigs[a].view(outs[a], None, c, batch0s[a])
            theirs = bigs[a].view(outs[a], None, 1 - c, batch0s[a])
            _remote(mine, mine, send, recv, a, (x, y, 1 - c)).wait_send()
            _remote(theirs, theirs, send, recv, a, (x, y, 1 - c)).wait_recv()

    return _Phase(stacks, _same(stacks), {a: a for a in range(n)}, n, start, finish, then)


def _tile_call(body, name, big, where, extra, ins, in_specs, out_specs, out_shape, phases=()):
    grid = ((extra,) if extra else ()) + big.grid
    return _call(body, name, grid, in_specs, out_specs, out_shape, ins, prefetch=(where,), phases=phases)


def _cast_into_full(w_stack, batch0, big, where, name, phases=()):
    def body(_, w_ref, o_ref):
        o_ref[...] = w_ref[...].astype(BF16)

    return _tile_call(
        body, name, big, where, 2, [w_stack], [big.spec(None, "grid", batch0=batch0)], [big.spec("pref", "grid")],
        [jax.ShapeDtypeStruct(big.dims("full"), BF16)], phases,
    )


def _pair_sum(g_full, recv_half, big, where, name, phases=()):
    def body(_, g_ref, r_ref, o_ref):
        o_ref[...] = (g_ref[...].astype(F32) + r_ref[...].astype(F32)).astype(BF16)

    half = big.spec("grid", None)
    return _tile_call(
        body, name, big, where, N_CHIPS, [g_full, recv_half], [big.spec("grid", "pref"), half], [half],
        [jax.ShapeDtypeStruct(big.dims("half"), BF16)], phases,
    )


def _chip_sum(chip_sum, parts, big, where, stack, stack_shape, batch0, name, phases=()):
    def body(_, own_ref, p_ref, *rest):
        acc = own_ref[...].astype(F32)
        for k in range(N_CHIPS - 1):
            acc = acc + p_ref[k].astype(F32)
        rest[-1][...] = acc

    ins = [chip_sum, parts] + ([stack] if stack is not None else [])
    in_specs = [big.spec("pref", None), big.spec(None, None, lead=(N_CHIPS - 1,))] + ([_ANY] if stack is not None else [])
    return _call(
        body, name, big.grid, in_specs, [big.spec(None, "pref", batch0=batch0)], [jax.ShapeDtypeStruct(stack_shape, F32)], ins,
        prefetch=(where,), phases=phases, in_place={2: 0} if stack is not None else None,
    )


def _adam_stack(w, g, m, v, name, after=()):
    b, r, c = w.shape
    tr = _row_tile(r, c)

    def body(w_ref, g_ref, m_ref, v_ref, *rest):
        go_ref, d_ref, mo_ref, vo_ref = rest[-4:]
        gv = g_ref[...]
        d, mo, vo = _adam(w_ref[...], gv, m_ref[...], v_ref[...])
        go_ref[...] = gv
        d_ref[...] = d
        mo_ref[...] = mo
        vo_ref[...] = vo

    spec = pl.BlockSpec((1, tr, c), lambda bb, i: (bb, i, 0))
    outs, _ = _call(
        body, name, (b, r // tr), [spec] * 4 + [_ANY] * len(after), [spec] * 4, [jax.ShapeDtypeStruct(w.shape, F32)] * 4,
        [w, g, m, v, *after],
    )
    return outs


def _mod_fwd(c_all, w_mod, b_cols, phases=()):
    n_layers, d, n = w_mod.shape
    tn = _pick(n, (768, 512, 384, 256, 128))

    def body(c_ref, w_ref, b_ref, o_ref):
        cv = c_ref[...]
        ca = (cv * _sigmoid(cv)).astype(BF16)
        o_ref[0] = _dot(ca, w_ref[0].astype(BF16)) + b_ref[0]

    return _call(
        body, "mod_fwd", (n_layers, n // tn),
        [
            pl.BlockSpec((N_DEV, d), lambda l, j: (0, 0)),
            pl.BlockSpec((1, d, tn), lambda l, j: (l, 0, j)),
            pl.BlockSpec((1, 1, tn), lambda l, j: (l, 0, j)),
        ],
        [pl.BlockSpec((1, N_DEV, tn), lambda l, j: (l, 0, j))],
        [jax.ShapeDtypeStruct((n_layers, N_DEV, n), F32)], [c_all, w_mod, b_cols], phases=phases,
    )


def _mod_bwd_adam(c_all_t, dmod_cols, w, m, v, after=()):
    n_layers, d, n = w.shape
    tn = _pick(n, (384, 256, 128))

    def body(c_ref, dm_ref, w_ref, m_ref, v_ref, *rest):
        g_ref, d_ref, mo_ref, vo_ref = rest[-4:]
        cv = c_ref[...]
        ca = (cv * _sigmoid(cv)).astype(BF16)
        g = _dot(ca, dm_ref[0].astype(BF16))
        g_ref[0] = g
        dl, mo, vo = _adam(w_ref[0], g, m_ref[0], v_ref[0])
        d_ref[0] = dl
        mo_ref[0] = mo
        vo_ref[0] = vo

    wspec = pl.BlockSpec((1, d, tn), lambda l, j: (l, 0, j))
    outs, _ = _call(
        body, "mod_bwd_adam", (n_layers, n // tn),
        [pl.BlockSpec((d, N_DEV), lambda l, j: (0, 0)), pl.BlockSpec((1, N_DEV, tn), lambda l, j: (l, 0, j)), wspec, wspec, wspec]
        + [_ANY] * len(after),
        [wspec] * 4, [jax.ShapeDtypeStruct(w.shape, F32)] * 4, [c_all_t, dmod_cols, w, m, v, *after],
    )
    return outs


def _ffn_fwd(x, vec, w_in, w_out, name, phases=()):
    s, d = x.shape
    f = w_out.shape[1]
    tm = _pick(s, (1024, 512, 256, 128))
    tf = _pick(f, (256, 128))
    nf = f // tf

    def body(x_ref, vec_ref, wg_ref, wu_ref, wo_ref, xo_ref, g_ref, u_ref, y_ref, h_sc, acc_sc):
        j = pl.program_id(1)

        @pl.when(j == 0)
        def _():
            h_sc[...] = _modulate(x_ref[...], vec_ref).astype(BF16)
            acc_sc[...] = jnp.zeros_like(acc_sc)

        h = h_sc[...]
        g = _dot(h, wg_ref[0])
        u = _dot(h, wu_ref[0])
        g_ref[...] = g.astype(BF16)
        u_ref[...] = u.astype(BF16)
        a = (g * _sigmoid(g) * u).astype(BF16)
        acc_sc[...] += _dot(a, wo_ref[0])

        @pl.when(j == nf - 1)
        def _():
            yv = acc_sc[...]
            xo_ref[...] = x_ref[...] + 0.5 * vec_ref[3:4, :] * yv
            y_ref[...] = yv.astype(BF16)

    row = pl.BlockSpec((tm, d), lambda i, j: (i, 0))
    hid = pl.BlockSpec((tm, tf), lambda i, j: (i, j))
    return _call(
        body, name, (s // tm, nf),
        [
            row,
            pl.BlockSpec((8, d), lambda i, j: (0, 0)),
            pl.BlockSpec((1, d, tf), lambda i, j: (0, 0, j)),
            pl.BlockSpec((1, d, tf), lambda i, j: (0, 0, nf + j)),
            pl.BlockSpec((1, tf, d), lambda i, j: (0, j, 0)),
        ],
        [row, hid, hid, row],
        [
            jax.ShapeDtypeStruct((s, d), F32),
            jax.ShapeDtypeStruct((s, f), BF16),
            jax.ShapeDtypeStruct((s, f), BF16),
            jax.ShapeDtypeStruct((s, d), BF16),
        ],
        [x, vec, w_in, w_in, w_out],
        scratch=[pltpu.VMEM((tm, d), BF16), pltpu.VMEM((tm, d), F32)], phases=phases,
    )


def _ffn_bwd(dxo, x, vec, gg, uu, y, w_in, w_out, name, phases=()):
    s, d = x.shape
    f = w_out.shape[1]
    tm = _pick(s, (512, 256, 128))
    tf = _pick(f, (256, 128))
    nf = f // tf

    def body(dxo_ref, x_ref, vec_ref, g_ref, u_ref, y_ref, wg_ref, wu_ref, wo_ref,
             dx_ref, dg_ref, du_ref, a_ref, h_ref, dy_ref, dvec_ref, acc_sc):
        i, j = pl.program_id(0), pl.program_id(1)

        @pl.when((i == 0) & (j == 0))
        def _():
            dvec_ref[...] = jnp.zeros_like(dvec_ref)

        @pl.when(j == 0)
        def _():
            dxo_v = dxo_ref[...]
            dy_ref[...] = (0.5 * vec_ref[3:4, :] * dxo_v).astype(BF16)
            dvec_ref[3:4, :] += 0.5 * jnp.sum(dxo_v * y_ref[...].astype(F32), axis=0, keepdims=True)
            acc_sc[...] = jnp.zeros_like(acc_sc)

        da = _dot_nt(dy_ref[...], wo_ref[0])
        g = g_ref[...].astype(F32)
        u = u_ref[...].astype(F32)
        sig = _sigmoid(g)
        sl = g * sig
        a_ref[...] = (sl * u).astype(BF16)
        dg = (da * u * (sig * (1.0 + g * (1.0 - sig)))).astype(BF16)
        du = (da * sl).astype(BF16)
        dg_ref[...] = dg
        du_ref[...] = du
        acc_sc[...] += _dot_nt(dg, wg_ref[0]) + _dot_nt(du, wu_ref[0])

        @pl.when(j == nf - 1)
        def _():
            dx, h = _modulate_bwd(x_ref[...], acc_sc[...], vec_ref, dvec_ref)
            dx_ref[...] = dxo_ref[...] + dx
            h_ref[...] = h.astype(BF16)

    row = pl.BlockSpec((tm, d), lambda i, j: (i, 0))
    hid = pl.BlockSpec((tm, tf), lambda i, j: (i, j))
    vecs = pl.BlockSpec((8, d), lambda i, j: (0, 0))
    return _call(
        body, name, (s // tm, nf),
        [
            row, row, vecs, hid, hid, row,
            pl.BlockSpec((1, d, tf), lambda i, j: (0, 0, j)),
            pl.BlockSpec((1, d, tf), lambda i, j: (0, 0, nf + j)),
            pl.BlockSpec((1, tf, d), lambda i, j: (0, j, 0)),
        ],
        [row, hid, hid, hid, row, row, vecs],
        [
            jax.ShapeDtypeStruct((s, d), F32),
            jax.ShapeDtypeStruct((s, f), BF16),
            jax.ShapeDtypeStruct((s, f), BF16),
            jax.ShapeDtypeStruct((s, f), BF16),
            jax.ShapeDtypeStruct((s, d), BF16),
            jax.ShapeDtypeStruct((s, d), BF16),
            jax.ShapeDtypeStruct((8, d), F32),
        ],
        [dxo, x, vec, gg, uu, y, w_in, w_in, w_out],
        scratch=[pltpu.VMEM((tm, d), F32)], phases=phases,
    )


def _grad_half(a, bs, big, where, mine, recv, name, phases=()):
    s, k1 = a.shape
    n = bs[0].shape[1]
    groups = len(bs)
    rows_halved = big.h3 == 1
    assert rows_halved or groups == 1
    kk, nn = (k1 // 2, n) if rows_halved else (k1, n // 2)
    tk = _pick(kk, (1408, 1024, 512, 256, 128))
    tn = _pick(nn, (1408, 1024, 640, 512, 256, 128))
    nkb, nnb = kk // tk, nn // tn
    assert (recv is None) == (not mine)

    def half(pref):
        return pref[1] if mine else 1 - pref[1]

    def body(_, a_ref, *rest):
        q = pl.program_id(1)
        for p in range(groups):

            @pl.when(q == p)
            def _(p=p):
                acc = _dot_tn(a_ref[...], rest[p][...])
                if recv is not None:
                    acc = acc + rest[groups][0].astype(F32)
                rest[-1][0] = acc.astype(BF16)

    def b_block(p):
        def index(i, q, j, pref):
            jj = jnp.where(q == p, j, jnp.where(q < p, 0, nnb - 1))
            return (0, jj + (0 if rows_halved else half(pref) * nnb))

        return pl.BlockSpec((s, tn), index)

    out_spec = pl.BlockSpec((1, tk, tn), lambda i, q, j, pref: (0, i, q * nnb + j))
    in_specs = [pl.BlockSpec((s, tk), lambda i, q, j, pref: (0, i + (half(pref) * nkb if rows_halved else 0)))]
    in_specs += [b_block(p) for p in range(groups)]
    ins = [a, *bs]
    if recv is not None:
        in_specs.append(out_spec)
        ins.append(recv)
    return _call(
        body, name, (nkb, groups, nnb), in_specs, [out_spec], [jax.ShapeDtypeStruct(big.dims("half"), BF16)], ins,
        prefetch=(where,), phases=phases,
    )


def _proj_mod_fwd(x, vec, w, phases=()):
    s, d = x.shape
    n = w.shape[2]
    tm = _pick(s, (512, 256, 128))
    tn = _pick(n, (640, 512, 256, 128))

    def body(x_ref, vec_ref, w_ref, o_ref, h_sc):
        @pl.when(pl.program_id(1) == 0)
        def _():
            h_sc[...] = _modulate(x_ref[...], vec_ref).astype(BF16)

        o_ref[...] = _dot(h_sc[...], w_ref[0])

    return _call(
        body, "ab_in_fwd", (s // tm, n // tn),
        [
            pl.BlockSpec((tm, d), lambda i, j: (i, 0)),
            pl.BlockSpec((8, d), lambda i, j: (0, 0)),
            pl.BlockSpec((1, d, tn), lambda i, j: (0, 0, j)),
        ],
        [pl.BlockSpec((tm, tn), lambda i, j: (i, j))],
        [jax.ShapeDtypeStruct((s, n), F32)], [x, vec, w],
        scratch=[pltpu.VMEM((tm, d), BF16)], phases=phases,
    )


def _proj_res_fwd(a, w, x, vec, phases=()):
    s, kd = a.shape
    d = x.shape[1]
    tm = _pick(s, (512, 256, 128))

    def body(a_ref, w_ref, x_ref, vec_ref, xo_ref, y_ref):
        yv = _dot(a_ref[...], w_ref[0])
        xo_ref[...] = x_ref[...] + vec_ref[3:4, :] * yv
        y_ref[...] = yv.astype(BF16)

    row = pl.BlockSpec((tm, d), lambda i: (i, 0))
    return _call(
        body, "ab_out_fwd", (s // tm,),
        [pl.BlockSpec((tm, kd), lambda i: (i, 0)), pl.BlockSpec((1, kd, d), lambda i: (0, 0, 0)), row, pl.BlockSpec((8, d), lambda i: (0, 0))],
        [row, row],
        [jax.ShapeDtypeStruct((s, d), F32), jax.ShapeDtypeStruct((s, d), BF16)], [a, w, x, vec], phases=phases,
    )


def _proj_res_bwd(dxo, y, vec, w, phases=()):
    s, d = dxo.shape
    kd = w.shape[1]
    tm = _pick(s, (512, 256, 128))

    def body(dxo_ref, y_ref, vec_ref, w_ref, dy_ref, da_ref, dgate_ref):
        @pl.when(pl.program_id(0) == 0)
        def _():
            dgate_ref[...] = jnp.zeros_like(dgate_ref)

        dxo_v = dxo_ref[...]
        dy = (vec_ref[3:4, :] * dxo_v).astype(BF16)
        dy_ref[...] = dy
        dgate_ref[3:4, :] += jnp.sum(dxo_v * y_ref[...].astype(F32), axis=0, keepdims=True)
        da_ref[...] = _dot_nt(dy, w_ref[0]).astype(BF16)

    row = pl.BlockSpec((tm, d), lambda i: (i, 0))
    vecs = pl.BlockSpec((8, d), lambda i: (0, 0))
    return _call(
        body, "ab_out_bwd", (s // tm,),
        [row, row, vecs, pl.BlockSpec((1, kd, d), lambda i: (0, 0, 0))],
        [row, pl.BlockSpec((tm, kd), lambda i: (i, 0)), vecs],
        [jax.ShapeDtypeStruct((s, d), BF16), jax.ShapeDtypeStruct((s, kd), BF16), jax.ShapeDtypeStruct((8, d), F32)],
        [dxo, y, vec, w], phases=phases,
    )


def _proj_mod_bwd(dproj, w, x, vec, dxo, dvec_in, name, phases=()):
    parts, s, n_part = dproj.shape
    d = x.shape[1]
    tm = _pick(s, (512, 256, 128))
    tk = _pick(n_part, (1408, 1280, 1024, 512, 256, 128))
    per_part = n_part // tk
    nk = parts * per_part

    def body(dp_ref, w_ref, x_ref, vec_ref, dxo_ref, dvi_ref, dx_ref, h_ref, dvec_ref, acc_sc):
        i, k = pl.program_id(0), pl.program_id(1)

        @pl.when((i == 0) & (k == 0))
        def _():
            dvec_ref[...] = dvi_ref[...]

        @pl.when(k == 0)
        def _():
            acc_sc[...] = jnp.zeros_like(acc_sc)

        acc_sc[...] += _dot_nt(dp_ref[0], w_ref[0])

        @pl.when(k == nk - 1)
        def _():
            dx, h = _modulate_bwd(x_ref[...], acc_sc[...], vec_ref, dvec_ref)
            dx_ref[...] = dxo_ref[...] + dx
            h_ref[...] = h.astype(BF16)

    row = pl.BlockSpec((tm, d), lambda i, k: (i, 0))
    vecs = pl.BlockSpec((8, d), lambda i, k: (0, 0))
    return _call(
        body, name, (s // tm, nk),
        [
            pl.BlockSpec((1, tm, tk), lambda i, k: (k // per_part, i, k % per_part)),
            pl.BlockSpec((1, d, tk), lambda i, k: (0, 0, k)),
            row, vecs, row, vecs,
        ],
        [row, row, vecs],
        [jax.ShapeDtypeStruct((s, d), F32), jax.ShapeDtypeStruct((s, d), BF16), jax.ShapeDtypeStruct((8, d), F32)],
        [dproj, w, x, vec, dxo, dvec_in], scratch=[pltpu.VMEM((tm, d), F32)], phases=phases,
    )


def _tril(n):
    return lax.broadcasted_iota(jnp.int32, (n, n), 0) >= lax.broadcasted_iota(jnp.int32, (n, n), 1)


def _layernorm_stats(gv):
    mu = jnp.mean(gv, axis=-1, keepdims=True)
    cen = gv - mu
    rstd = lax.rsqrt(jnp.mean(cen * cen, axis=-1, keepdims=True) + EPS)
    return cen * rstd, rstd


def _shift_down(q, k, above_ref, c_cg, c_xb, first):
    width = q.shape[1]
    rows = lax.broadcasted_iota(jnp.int32, q.shape, 0)
    out = pltpu.roll(q, k, 0)
    for r in range(k):
        src = CONV_HALO - k + r
        above = above_ref[src : src + 1, c_cg : c_cg + width] * above_ref[src : src + 1, c_xb : c_xb + width]
        above = jnp.where(first, 0.0, above)
        out = jnp.where(rows == r, above, out)
    return out


def _ab_mix_fwd(proj, norm_v, w_s, b_rows, conv_w, phases=()):
    s, n = proj.shape
    heads, chunk, _ = w_s.shape
    da = norm_v.shape[1]
    hd = da // heads
    db = conv_w.shape[1]
    tm = _pick(s, (512, 256, 128))

    def body(p_ref, ph_ref, nv_ref, ws_ref, b_ref, cw_ref, o_ref):
        first = pl.program_id(0) == 0
        gu, _ = _gelu(p_ref[:, 0:da])
        gv, _ = _gelu(p_ref[:, da : 2 * da])
        xhat, _ = _layernorm_stats(gv)
        vn = (xhat * nv_ref[...]).astype(BF16)
        mask = _tril(chunk)
        for hh in range(heads):
            wm = jnp.where(mask, ws_ref[hh], 0.0).astype(BF16)
            cols = slice(hh * hd, (hh + 1) * hd)
            for nn in range(tm // chunk):
                rows = slice(nn * chunk, (nn + 1) * chunk)
                z = _dot(wm, vn[rows, cols]) + b_ref[:, cols]
                o_ref[rows, cols] = (gu[rows, cols] * z).astype(BF16)
        c_cg, c_xb = 2 * da + db, 2 * da + 2 * db
        bg = p_ref[:, 2 * da : 2 * da + db]
        q = p_ref[:, c_cg : c_cg + db] * p_ref[:, c_xb : c_xb + db]
        q1 = _shift_down(q, 1, ph_ref, c_cg, c_xb, first)
        q2 = _shift_down(q, 2, ph_ref, c_cg, c_xb, first)
        conv = cw_ref[0:1, :] * q2 + cw_ref[1:2, :] * q1 + cw_ref[2:3, :] * q
        o_ref[:, da : da + db] = (bg * conv).astype(BF16)

    nh = tm // CONV_HALO
    return _call(
        body, "ab_mix_fwd", (s // tm,),
        [
            pl.BlockSpec((tm, n), lambda i: (i, 0)),
            pl.BlockSpec((CONV_HALO, n), lambda i: (jnp.maximum(i * nh - 1, 0), 0)),
            pl.BlockSpec((1, da), lambda i: (0, 0)),
            pl.BlockSpec((heads, chunk, chunk), lambda i: (0, 0, 0)),
            pl.BlockSpec((chunk, da), lambda i: (0, 0)),
            pl.BlockSpec((3, db), lambda i: (0, 0)),
        ],
        [pl.BlockSpec((tm, da + db), lambda i: (i, 0))],
        [jax.ShapeDtypeStruct((s, da + db), BF16)], [proj, proj, norm_v, w_s, b_rows, conv_w], phases=phases,
    )


def _ab_mix_bwd(proj, dcat, norm_v, w_s, b_rows, conv_w, phases=()):
    s, n = proj.shape
    heads, chunk, _ = w_s.shape
    da = norm_v.shape[1]
    hd = da // heads
    db = conv_w.shape[1]
    tm = _pick(s, (512, 256, 128))
    nblk = s // tm
    dhalo = 2 * CONV_HALO

    def body(p_ref, pa_ref, pb_ref, dc_ref, dcb_ref, nv_ref, ws_ref, b_ref, cw_ref,
             dp_ref, dnv_ref, dws_ref, dzs_ref, dcw_ref, dvn_sc):
        i = pl.program_id(0)
        first, last = i == 0, i == nblk - 1

        @pl.when(first)
        def _():
            dnv_ref[...] = jnp.zeros_like(dnv_ref)
            dws_ref[...] = jnp.zeros_like(dws_ref)
            dzs_ref[...] = jnp.zeros_like(dzs_ref)
            dcw_ref[...] = jnp.zeros_like(dcw_ref)

        uu = p_ref[:, 0:da]
        gu, gu_grad = _gelu(uu)
        gv, gv_grad = _gelu(p_ref[:, da : 2 * da])
        xhat, rstd = _layernorm_stats(gv)
        nv = nv_ref[...]
        vn = (xhat * nv).astype(BF16)
        dya = dc_ref[:, 0:da].astype(F32)
        dz = (dya * gu).astype(BF16)
        mask = _tril(chunk)
        for hh in range(heads):
            wm = jnp.where(mask, ws_ref[hh], 0.0).astype(BF16)
            cols = slice(hh * hd, (hh + 1) * hd)
            dws = jnp.zeros((chunk, chunk), F32)
            for nn in range(tm // chunk):
                rows = slice(nn * chunk, (nn + 1) * chunk)
                z = _dot(wm, vn[rows, cols]) + b_ref[:, cols]
                dp_ref[rows, cols] = (dya[rows, cols] * z * gu_grad[rows, cols]).astype(BF16)
                dz_blk = dz[rows, cols]
                dws = dws + _dot_nt(dz_blk, vn[rows, cols])
                dzs_ref[:, cols] += dz_blk.astype(F32)
                dvn = _dot_tn(wm, dz_blk)
                dnv_ref[:, cols] += jnp.sum(dvn * xhat[rows, cols], axis=0, keepdims=True)
                dvn_sc[rows, cols] = dvn
            dws_ref[hh] += jnp.where(mask, dws, 0.0)
        dxhat = dvn_sc[...] * nv
        dgv = rstd * (dxhat - jnp.mean(dxhat, axis=-1, keepdims=True) - xhat * jnp.mean(dxhat * xhat, axis=-1, keepdims=True))
        dp_ref[:, da : 2 * da] = (dgv * gv_grad).astype(BF16)

        c_bg, c_cg, c_xb = 2 * da, 2 * da + db, 2 * da + 2 * db
        bg = p_ref[:, c_bg : c_bg + db]
        cg = p_ref[:, c_cg : c_cg + db]
        xb = p_ref[:, c_xb : c_xb + db]
        q = cg * xb
        q1 = _shift_down(q, 1, pa_ref, c_cg, c_xb, first)
        q2 = _shift_down(q, 2, pa_ref, c_cg, c_xb, first)
        dyb = dc_ref[:, da : da + db].astype(F32)
        conv = cw_ref[0:1, :] * q2 + cw_ref[1:2, :] * q1 + cw_ref[2:3, :] * q
        dp_ref[:, c_bg : c_bg + db] = (dyb * conv).astype(BF16)
        e = dyb * bg
        dcw_ref[0:1, :] += jnp.sum(e * q2, axis=0, keepdims=True)
        dcw_ref[1:2, :] += jnp.sum(e * q1, axis=0, keepdims=True)
        dcw_ref[2:3, :] += jnp.sum(e * q, axis=0, keepdims=True)
        rows = lax.broadcasted_iota(jnp.int32, e.shape, 0)
        dq = cw_ref[2:3, :] * e
        for kk in (1, 2):
            ek = pltpu.roll(e, tm - kk, 0)
            for r in range(kk):
                below = dcb_ref[r : r + 1, da : da + db].astype(F32) * pb_ref[r : r + 1, c_bg : c_bg + db]
                below = jnp.where(last, 0.0, below)
                ek = jnp.where(rows == tm - kk + r, below, ek)
            dq = dq + cw_ref[2 - kk : 3 - kk, :] * ek
        dp_ref[:, c_cg : c_cg + db] = (dq * xb).astype(BF16)
        dp_ref[:, c_xb : c_xb + db] = (dq * cg).astype(BF16)

    nh = tm // CONV_HALO
    nhb = tm // dhalo
    const2 = lambda i: (0, 0)
    return _call(
        body, "ab_mix_bwd", (nblk,),
        [
            pl.BlockSpec((tm, n), lambda i: (i, 0)),
            pl.BlockSpec((CONV_HALO, n), lambda i: (jnp.maximum(i * nh - 1, 0), 0)),
            pl.BlockSpec((CONV_HALO, n), lambda i: (jnp.minimum((i + 1) * nh, s // CONV_HALO - 1), 0)),
            pl.BlockSpec((tm, da + db), lambda i: (i, 0)),
            pl.BlockSpec((dhalo, da + db), lambda i: (jnp.minimum((i + 1) * nhb, s // dhalo - 1), 0)),
            pl.BlockSpec((1, da), const2),
            pl.BlockSpec((heads, chunk, chunk), lambda i: (0, 0, 0)),
            pl.BlockSpec((chunk, da), const2),
            pl.BlockSpec((3, db), const2),
        ],
        [
            pl.BlockSpec((tm, n), lambda i: (i, 0)),
            pl.BlockSpec((1, da), const2),
            pl.BlockSpec((heads, chunk, chunk), lambda i: (0, 0, 0)),
            pl.BlockSpec((chunk, da), const2),
            pl.BlockSpec((3, db), const2),
        ],
        [
            jax.ShapeDtypeStruct((s, n), BF16),
            jax.ShapeDtypeStruct((1, da), F32),
            jax.ShapeDtypeStruct((heads, chunk, chunk), F32),
            jax.ShapeDtypeStruct((chunk, da), F32),
            jax.ShapeDtypeStruct((3, db), F32),
        ],
        [proj, proj, proj, dcat, dcat, norm_v, w_s, b_rows, conv_w],
        scratch=[pltpu.VMEM((tm, da), F32)], phases=phases,
    )


def _pool_counts(tm, i, w):
    t = i * tm + lax.broadcasted_iota(jnp.int32, (tm, 1), 0)
    return jnp.minimum(t + 1, w).astype(F32)


def _pool_fwd(x, vec, w_grp, scale, phases=()):
    s, d = x.shape
    groups, gd, _ = w_grp.shape
    tm = _pick(s, (512, 256, 128))

    def body(x_ref, xa_ref, vec_ref, w_ref, sc_ref, xo_ref, p_ref, o_ref):
        i = pl.program_id(0)
        h = _modulate(x_ref[...], vec_ref)
        ha = jnp.where(i == 0, 0.0, _modulate(xa_ref[...], vec_ref))
        ext = jnp.concatenate([ha, h], axis=0)
        for gi, w in enumerate(POOL_WINDOWS):
            cols = slice(gi * gd, (gi + 1) * gd)
            acc = ext[:, cols]
            step = 1
            while step < w:
                acc = acc + pltpu.roll(acc, step, 0)
                step *= 2
            p = (acc[POOL_HALO:, :] / _pool_counts(tm, i, w) - h[:, cols]).astype(BF16)
            p_ref[:, cols] = p
            o_ref[:, cols] = _dot(p, w_ref[gi]).astype(BF16)
        xo_ref[...] = x_ref[...] + vec_ref[3:4, :] * (o_ref[...].astype(F32) * sc_ref[...])

    nh = tm // POOL_HALO
    row = pl.BlockSpec((tm, d), lambda i: (i, 0))
    return _call(
        body, "pool_fwd", (s // tm,),
        [
            row,
            pl.BlockSpec((POOL_HALO, d), lambda i: (jnp.maximum(i * nh - 1, 0), 0)),
            pl.BlockSpec((8, d), lambda i: (0, 0)),
            pl.BlockSpec((groups, gd, gd), lambda i: (0, 0, 0)),
            pl.BlockSpec((1, d), lambda i: (0, 0)),
        ],
        [row, row, row],
        [jax.ShapeDtypeStruct((s, d), F32), jax.ShapeDtypeStruct((s, d), BF16), jax.ShapeDtypeStruct((s, d), BF16)],
        [x, x, vec, w_grp, scale], phases=phases,
    )


def _pool_bwd(dxo, x, vec, p, o, w_grp, scale, phases=()):
    s, d = x.shape
    groups, gd, _ = w_grp.shape
    tm = _pick(s, (512, 256, 128))
    nblk = s // tm

    def body(dxo_ref, dxb_ref, x_ref, vec_ref, p_ref, o_ref, w_ref, sc_ref, dx_ref, dw_ref, dsc_ref, dvec_ref, dw_sc):
        i = pl.program_id(0)

        @pl.when(i == 0)
        def _():
            dw_sc[...] = jnp.zeros_like(dw_sc)
            dsc_ref[...] = jnp.zeros_like(dsc_ref)
            dvec_ref[...] = jnp.zeros_like(dvec_ref)

        gate, sc = vec_ref[3:4, :], sc_ref[...]
        dxo_v = dxo_ref[...]
        ov = o_ref[...].astype(F32)
        dvec_ref[3:4, :] += jnp.sum(dxo_v * (ov * sc), axis=0, keepdims=True)
        dy = gate * dxo_v
        dsc_ref[...] += jnp.sum(dy * ov, axis=0, keepdims=True)
        dout = (dy * sc).astype(BF16)
        dout_b = jnp.where(i == nblk - 1, 0.0, gate * dxb_ref[...] * sc).astype(BF16)
        for gi, w in enumerate(POOL_WINDOWS):
            cols = slice(gi * gd, (gi + 1) * gd)
            dw_sc[gi] += _dot_tn(p_ref[:, cols], dout[:, cols])
            wb = w_ref[gi]
            dp = _dot_nt(dout[:, cols], wb)
            dp_b = _dot_nt(dout_b[:, cols], wb)
            e = dp / _pool_counts(tm, i, w)
            t_below = (i + 1) * tm + lax.broadcasted_iota(jnp.int32, (POOL_HALO, 1), 0)
            e_b = dp_b / jnp.minimum(t_below + 1, w).astype(F32)
            acc = jnp.concatenate([e, e_b], axis=0)
            step = 1
            while step < w:
                acc = acc + pltpu.roll(acc, tm + POOL_HALO - step, 0)
                step *= 2
            dx_ref[:, cols] = acc[:tm, :] - dp
        dx, _ = _modulate_bwd(x_ref[...], dx_ref[...], vec_ref, dvec_ref)
        dx_ref[...] = dxo_v + dx

        @pl.when(i == nblk - 1)
        def _():
            dw_ref[...] = dw_sc[...].astype(BF16)

    nh = tm // POOL_HALO
    row = pl.BlockSpec((tm, d), lambda i: (i, 0))
    vecs = pl.BlockSpec((8, d), lambda i: (0, 0))
    wspec = pl.BlockSpec((groups, gd, gd), lambda i: (0, 0, 0))
    return _call(
        body, "pool_bwd", (nblk,),
        [
            row,
            pl.BlockSpec((POOL_HALO, d), lambda i: (jnp.minimum((i + 1) * nh, s // POOL_HALO - 1), 0)),
            row, vecs, row, row, wspec,
            pl.BlockSpec((1, d), lambda i: (0, 0)),
        ],
        [row, wspec, pl.BlockSpec((1, d), lambda i: (0, 0)), vecs],
        [
            jax.ShapeDtypeStruct((s, d), F32),
            jax.ShapeDtypeStruct((groups, gd, gd), BF16),
            jax.ShapeDtypeStruct((1, d), F32),
            jax.ShapeDtypeStruct((8, d), F32),
        ],
        [dxo, dxo, x, vec, p, o, w_grp, scale],
        scratch=[pltpu.VMEM((groups, gd, gd), F32)], phases=phases,
    )


def _loss_head(x, gain, target, phases=()):
    s, d = x.shape
    tm = _pick(s, (512, 256, 128))

    def body(x_ref, g_ref, t_ref, dx_ref, aux_ref):
        @pl.when(pl.program_id(0) == 0)
        def _():
            aux_ref[...] = jnp.zeros_like(aux_ref)

        xv = x_ref[...]
        rstd = _rstd(xv)
        r = xv * rstd
        gain_v = g_ref[...]
        err = r * gain_v - t_ref[...]
        aux_ref[1:2, :] += jnp.sum(err * err, axis=0, keepdims=True)
        dout = err * (1.0 / d)
        aux_ref[0:1, :] += jnp.sum(dout * r, axis=0, keepdims=True)
        dr = dout * gain_v
        dx_ref[...] = rstd * (dr - r * jnp.mean(dr * r, axis=-1, keepdims=True))

    row = pl.BlockSpec((tm, d), lambda i: (i, 0))
    return _call(
        body, "loss_head", (s // tm,),
        [row, pl.BlockSpec((1, d), lambda i: (0, 0)), row],
        [row, pl.BlockSpec((8, d), lambda i: (0, 0))],
        [jax.ShapeDtypeStruct((s, d), F32), jax.ShapeDtypeStruct((8, d), F32)], [x, gain, target], phases=phases,
    )


def _small_adam(gathered, gathered_ws, layout, smalls, chip):
    names = list(smalls)
    n = len(names)
    loss_row, _, _, n_feat = layout["loss"]

    def body(*refs):
        chip_ref, g_ref, gws_ref = refs[0], refs[1], refs[2]
        wmv = refs[3 : 3 + 3 * n]
        outs = refs[3 + 3 * n : 3 + 7 * n]
        total = refs[-1]
        total[...] = g_ref[0]
        for kdev in range(1, N_DEV):
            total[...] += g_ref[kdev]
        total_ws = gws_ref[0]
        for kdev in range(1, N_DEV):
            total_ws = total_ws + gws_ref[kdev]
        my_chip = chip_ref[0]
        for a, name in enumerate(names):
            w_ref, m_ref, v_ref = wmv[3 * a : 3 * a + 3]
            if name == "ab_w_s":
                g = total_ws
            else:
                row0, rows, col0, cols = layout[name]
                if col0 is None:
                    g = jnp.zeros((rows, cols), F32)
                    for j in range(N_CHIPS):
                        g = g + jnp.where(my_chip == j, total[row0 : row0 + rows, j * cols : (j + 1) * cols], 0.0)
                else:
                    g = total[row0 : row0 + rows, col0 : col0 + cols]
            dl, mo, vo = _adam(w_ref[...], g, m_ref[...], v_ref[...])
            outs[4 * a][...] = g
            outs[4 * a + 1][...] = dl
            outs[4 * a + 2][...] = mo
            outs[4 * a + 3][...] = vo
        refs[3 + 7 * n][...] = 0.5 * jnp.sum(total[loss_row : loss_row + 1, 0:n_feat], axis=1, keepdims=True) / n_feat

    ins = [gathered, gathered_ws]
    out_shapes = []
    for name in names:
        ins.extend(smalls[name])
        out_shapes.extend([jax.ShapeDtypeStruct(smalls[name][0].shape, F32)] * 4)
    out_shapes.append(jax.ShapeDtypeStruct((1, 1), F32))
    whole = lambda shape: pl.BlockSpec(shape, functools.partial(lambda nd, i, c: (0,) * nd, len(shape)))
    res = pl.pallas_call(
        body, name="small_adam",
        grid_spec=pltpu.PrefetchScalarGridSpec(
            num_scalar_prefetch=1, grid=(1,),
            in_specs=[whole(a.shape) for a in ins], out_specs=[whole(o.shape) for o in out_shapes],
            scratch_shapes=[pltpu.VMEM(gathered.shape[1:], F32)],
        ),
        out_shape=out_shapes,
        compiler_params=pltpu.CompilerParams(dimension_semantics=("arbitrary",), vmem_limit_bytes=VMEM_LIMIT_BYTES),
    )(chip.reshape(1).astype(jnp.int32), *ins)
    return {name: res[4 * a : 4 * a + 4] for a, name in enumerate(names)}, res[4 * n]


def _pad_rows(a, rows=8):
    extra = (-a.shape[0]) % rows
    return jnp.pad(a, ((0, extra), (0, 0))) if extra else a


def _pad_cols(a, cols):
    return jnp.pad(a, ((0, 0), (0, cols - a.shape[1]))) if a.shape[1] < cols else a


def _run(fn, *phases):
    outs, p_outs = fn(list(phases))
    for p, po in zip(phases, p_outs):
        p.then(po)
    return outs


def kernel(x, c, norm_g, w_mod, b_mod, w_ffn_in, w_ffn_out, ab_w_in, ab_norm_v, ab_w_s, ab_b_s, ab_conv_w, ab_w_out, pool_w_grp, pool_scale, final_g, loss_target, m_norm_g, m_w_mod, m_b_mod, m_w_ffn_in, m_w_ffn_out, m_ab_w_in, m_ab_norm_v, m_ab_w_s, m_ab_b_s, m_ab_conv_w, m_ab_w_out, m_pool_w_grp, m_pool_scale, m_final_g, v_norm_g, v_w_mod, v_b_mod, v_w_ffn_in, v_w_ffn_out, v_ab_w_in, v_ab_norm_v, v_ab_w_s, v_ab_b_s, v_ab_conv_w, v_ab_w_out, v_pool_w_grp, v_pool_scale, v_final_g):
    ix, iy, ic = _place()
    chip = 2 * ix + iy
    me = 4 * ix + 2 * iy + ic
    where = jnp.stack([chip, ic]).astype(jnp.int32)
    s, d = x.shape[1], x.shape[2]
    x0 = x.reshape(s, d)
    target = loss_target.reshape(s, d)
    n_layers = norm_g.shape[0]
    dq = d // N_CHIPS
    heads, chunk = ab_w_s.shape[1], ab_w_s.shape[2]
    da = ab_norm_v.shape[1]
    db = ab_conv_w.shape[2] * N_CHIPS
    f_hidden = w_ffn_out.shape[2] * N_CHIPS
    assert n_layers == 2 and da % heads == 0

    cw_pad = _pad_cols(ab_conv_w.reshape(3, db // N_CHIPS), dq)
    packed = jnp.concatenate(
        [_pad_rows(c.reshape(N_CHIPS, dq)), _pad_rows(norm_g.reshape(-1, dq)), _pad_rows(pool_scale.reshape(1, dq)), _pad_rows(cw_pad)],
        axis=0,
    )
    ncol = w_mod.shape[2]
    b_cols = lax.dynamic_slice(b_mod, (0, chip * ncol), (n_layers, ncol)).reshape(n_layers, 1, ncol)
    small = {}

    def small_gather(key, arrs):
        def then(outs):
            small[key] = outs

        return _phase_small_gather(arrs, then)

    stacks = {
        "w_ffn_in": tuple(a.reshape((-1,) + a.shape[2:]) for a in (w_ffn_in, m_w_ffn_in, v_w_ffn_in)),
        "w_ffn_out": tuple(a.reshape((-1,) + a.shape[2:]) for a in (w_ffn_out, m_w_ffn_out, v_w_ffn_out)),
        "ab_w_in": (ab_w_in, m_ab_w_in, v_ab_w_in),
        "ab_w_out": (ab_w_out, m_ab_w_out, v_ab_w_out),
        "pool_w_grp": (pool_w_grp[0], m_pool_w_grp[0], v_pool_w_grp[0]),
    }
    big_in = _Big((1, d, 2 * f_hidden), 2, 1)
    big_out = _Big((1, f_hidden, d), 1, 2)
    units = {}
    for l in range(n_layers):
        for k in range(2):
            units[f"in{l}{k}"] = (big_in, "w_ffn_in", 2 * l + k)
            units[f"out{l}{k}"] = (big_out, "w_ffn_out", 2 * l + k)
    units["abin"] = (_Big((1, d, ab_w_in.shape[2] * N_CHIPS), 2, 1), "ab_w_in", 0)
    units["about"] = (_Big((1, ab_w_out.shape[1] * N_CHIPS, d), 1, 2), "ab_w_out", 0)
    units["pool"] = (_Big((pool_w_grp.shape[1], pool_w_grp.shape[2] * N_CHIPS, pool_w_grp.shape[3]), 1, 0), "pool_w_grp", 0)
    big = {u: g for u, (g, _, _) in units.items()}

    weight = {}
    complete = set()

    def cast(u):
        g, st, b0 = units[u]

        def launch(phases):
            (weight[u],), p_outs = _cast_into_full(stacks[st][0], b0, g, where, "cast_" + u, phases)
            return None, p_outs

        return launch

    def gather_ici(*us):
        def then(outs):
            for u, o in zip(us, outs):
                weight[u] = o

        return _phase_gather_ici([weight[u] for u in us], [big[u] for u in us], then)

    def gather_sibling(*us):
        def then(outs):
            for u, o in zip(us, outs):
                weight[u] = o
                complete.add(u)

        return _phase_gather_sibling([weight[u] for u in us], [big[u] for u in us], then)

    def w_of(u):
        assert u in complete, u
        return weight[u]

    _run(cast("in00"), small_gather("inputs", [packed]))
    small_all = small["inputs"][0]
    by_chip = small_all[0::2]
    c_all = small_all[:, 0:N_CHIPS, :].reshape(N_DEV, d)
    norm_full = by_chip[:, 8 : 8 + 3 * n_layers, :].transpose(1, 0, 2).reshape(3 * n_layers, d)
    pool_scale_full = by_chip[:, 16:17, :].transpose(1, 0, 2).reshape(1, d)
    conv_full = by_chip[:, 24:27, : db // N_CHIPS].transpose(1, 0, 2).reshape(3, db)
    pieces = [("in00", "out00"), ("abin", "about"), ("in01", "out01"), ("in10", "out10", "pool"), ("in11", "out11")]
    in_flight = {}

    def start_gather(p):
        in_flight[p] = _split_start(gather_ici(*pieces[p]), f"gather_{p}_start")

    def started():
        return _after(*[flight.token for flight in in_flight.values()])

    def finish_gather(p, after, meanwhile=None):
        flight = in_flight.pop(p)
        _split_wait(flight, list(after) + list(started().ins), f"gather_{p}_wait")
        crossing = _split_start(gather_sibling(*pieces[p]), f"gather_{p}_forward")
        behind = [crossing.token]
        if p + 2 < len(pieces):
            for u in pieces[p + 2]:
                _run(cast(u), _after(crossing.token))
            start_gather(p + 2)
            behind = list(started().ins)
        if meanwhile is not None:
            behind = behind + meanwhile(_after(crossing.token))
        _split_wait(crossing, behind, f"gather_{p}_forwarded")

    mod_cols = _run(lambda phases: _mod_fwd(c_all, w_mod, b_cols, phases))[0]
    _run(cast("out00"), small_gather("mod", [mod_cols.reshape(n_layers * N_DEV, ncol)]))
    start_gather(0)
    _run(cast("abin"), started())
    _run(cast("about"), started())
    start_gather(1)
    mod_all = small["mod"][0]
    mod_mine = lax.dynamic_index_in_dim(mod_all[0::2].reshape(N_CHIPS, n_layers, N_DEV, ncol), me, axis=2, keepdims=False)
    mod = mod_mine.transpose(1, 0, 2).reshape(n_layers, 3, 3, d)
    vecs = {
        (l, sub): jnp.pad(norm_full[3 * l + sub][None], ((0, 7), (0, 0))) + jnp.pad(mod[l, sub], ((1, 4), (0, 0)))
        for l in range(n_layers)
        for sub in range(3)
    }
    b_rows = jnp.broadcast_to(ab_b_s[0].T[:, :, None], (chunk, heads, da // heads)).reshape(chunk, da)

    saved = {}

    def ffn_forward(xs, l, sub, k, *phases):
        saved[l, sub, "x"] = xs
        xs, gg, uu, yb = _run(
            lambda ph: _ffn_fwd(xs, vecs[l, sub], w_of(f"in{l}{k}"), w_of(f"out{l}{k}"), f"ffn_fwd_{l}{k}", ph), *phases
        )
        saved[l, sub, "act"] = (gg, uu, yb)
        return xs

    finish_gather(0, [vecs[0, 0]])
    xs = ffn_forward(x0, 0, 0, 0, started())
    saved[0, 1, "x"] = xs
    finish_gather(1, [xs])
    (proj,) = _run(lambda ph: _proj_mod_fwd(xs, vecs[0, 1], w_of("abin"), ph), started())
    (cat,) = _run(lambda ph: _ab_mix_fwd(proj, ab_norm_v, ab_w_s[0], b_rows, conv_full, ph))
    xs, yb = _run(lambda ph: _proj_res_fwd(cat, w_of("about"), xs, vecs[0, 1], ph))
    saved[0, 1, "act"] = (proj, cat, yb)
    finish_gather(2, [xs])
    xs = ffn_forward(xs, 0, 2, 1, started())
    finish_gather(3, [xs])
    xs = ffn_forward(xs, 1, 0, 0, started())
    saved[1, 1, "x"] = xs
    pooled = []

    def pool_forward(behind):
        pooled.extend(_run(lambda ph: _pool_fwd(xs, vecs[1, 1], w_of("pool"), pool_scale_full, ph), behind))
        return [pooled[0]]

    finish_gather(4, [xs], pool_forward)
    xs, pp, oo = pooled
    saved[1, 1, "act"] = (pp, oo)
    xs = ffn_forward(xs, 1, 2, 1)
    dxs, aux = _run(lambda ph: _loss_head(xs, final_g.reshape(1, d), target, ph))

    grad = {}
    recv = {}
    csum = {}
    parts = {}
    reduced = {}
    done = set()
    dvecs, small_g = {}, {}

    def pair_exchange(*us):
        def then(outs):
            for u, o in zip(us, outs):
                recv[u] = o

        return _phase_pair_exchange([grad[u] for u in us], [big[u] for u in us], then)

    def grad_half(u, a, bs, mine, name, *phases):
        (res,) = _run(lambda ph: _grad_half(a, bs, big[u], where, mine, recv[u] if mine else None, name, ph), *phases)
        return res

    def pair_sum(u, *phases):
        def launch(ph):
            (csum[u],), p_outs = _pair_sum(grad[u], recv[u], big[u], where, "pair_sum_" + u, ph)
            return None, p_outs

        _run(launch, *phases)

    def chip_exchange(*us):
        def then(outs):
            for u, o in zip(us, outs):
                parts[u] = o

        return _phase_chip_exchange([csum[u] for u in us], [big[u] for u in us], then)

    def chip_sum(*us, carried=()):
        for n_u, u in enumerate(us):
            g, st, b0 = units[u]

            def launch(ph):
                (reduced[st],), p_outs = _chip_sum(
                    csum[u], parts[u], g, where, reduced.get(st), stacks[st][0].shape, b0, "chip_sum_" + u, ph
                )
                return None, p_outs

            _run(launch, *(carried if n_u == 0 else ()))

    def pair_broadcast(*us):
        sts = [units[u][1] for u in us]
        assert len(set(sts)) == len(sts)

        def then(outs):
            for u, st, o in zip(us, sts, outs):
                reduced[st] = o
                done.add(u)

        return _phase_pair_broadcast([reduced[st] for st in sts], [big[u] for u in us], [units[u][2] for u in us], then)

    def ffn_backward(dxs, l, sub, k, carried_bwd, carried_send, carried_mine):
        gg, uu, yb = saved[l, sub, "act"]
        w_in, w_out = w_of(f"in{l}{k}"), w_of(f"out{l}{k}")
        uo, ui, tag = f"out{l}{k}", f"in{l}{k}", f"{l}{k}"
        dxs, dg, du, a, h, dy, dvecs[l, sub] = _run(
            lambda ph: _ffn_bwd(dxs, saved[l, sub, "x"], vecs[l, sub], gg, uu, yb, w_in, w_out, "ffn_bwd_" + tag, ph), *carried_bwd()
        )
        grad[uo] = grad_half(uo, a, [dy], False, "dw_out_send_" + tag, *carried_send())
        grad[ui] = grad_half(ui, h, [dg, du], False, "dw_in_send_" + tag, pair_exchange(uo))
        csum[uo] = grad_half(uo, a, [dy], True, "dw_out_" + tag, pair_exchange(ui))
        csum[ui] = grad_half(ui, h, [dg, du], True, "dw_in_" + tag, *carried_mine())
        return dxs

    none = lambda: ()
    dxs = ffn_backward(dxs, 1, 2, 1, none, none, none)
    pp, oo = saved[1, 1, "act"]
    dxs, grad["pool"], small_g["pool_scale"], dvecs[1, 1] = _run(
        lambda ph: _pool_bwd(dxs, saved[1, 1, "x"], vecs[1, 1], pp, oo, w_of("pool"), pool_scale_full, ph)
    )

    def after_11():
        return (chip_exchange("in11", "out11"), pair_exchange("pool"))

    def bcast_11():
        chip_sum("in11", "out11")
        pair_sum("pool")
        return (pair_broadcast("in11", "out11"), chip_exchange("pool"))

    dxs = ffn_backward(dxs, 1, 0, 0, after_11, bcast_11, none)

    def after_10():
        return (chip_exchange("in10", "out10"),)

    def bcast_10():
        chip_sum("in10", "out10", "pool")
        return (pair_broadcast("in10", "out10", "pool"),)

    dxs = ffn_backward(dxs, 0, 2, 1, after_10, bcast_10, none)

    proj, cat, yb = saved[0, 1, "act"]
    dy, dcat, dgate = _run(lambda ph: _proj_res_bwd(dxs, yb, vecs[0, 1], w_of("about"), ph))
    grad["about"] = grad_half("about", cat, [dy], False, "dw_ab_out_send")
    dproj, small_g["ab_norm_v"], small_g["ab_w_s"], dzs, small_g["ab_conv_w"] = _run(
        lambda ph: _ab_mix_bwd(proj, dcat, ab_norm_v, ab_w_s[0], b_rows, conv_full, ph), chip_exchange("out01"), pair_exchange("about")
    )
    small_g["ab_b_s"] = dzs.reshape(chunk, heads, da // heads).sum(axis=2).T
    dxs, h, dvecs[0, 1] = _run(
        lambda ph: _proj_mod_bwd(dproj[None], w_of("abin"), saved[0, 1, "x"], vecs[0, 1], dxs, dgate, "ab_in_bwd", ph)
    )
    grad["abin"] = grad_half("abin", h, [dproj], False, "dw_ab_in_send")
    chip_sum("out01", carried=(pair_exchange("abin"),))
    csum["about"] = grad_half("about", cat, [dy], True, "dw_ab_out", pair_broadcast("out01"))
    csum["abin"] = grad_half("abin", h, [dproj], True, "dw_ab_in")

    layout = {}
    tail = {}

    def after_01():
        tail["01"] = _split_start(chip_exchange("in01", "abin", "about"), "reduce_01_start")
        return (_after(tail["01"].token),)

    def pack_small_grads():
        dvec_all = jnp.stack([dvecs[l, sub] for l in range(n_layers) for sub in range(3)])
        dgain = dvec_all[:, 0, :]
        dmod = dvec_all[:, 1:4, :].reshape(3 * 3 * n_layers, d)
        rows = {
            "norm_g": (dgain, None, dq), "final_g": (aux[0:1], 0, d), "pool_scale": (small_g["pool_scale"], None, dq),
            "b_mod": (dmod, 0, d), "ab_norm_v": (small_g["ab_norm_v"], 0, da),
            "ab_conv_w": (small_g["ab_conv_w"], None, db // N_CHIPS), "ab_b_s": (small_g["ab_b_s"], 0, chunk),
            "loss": (aux[1:2], 0, d),
        }
        row0 = 0
        for nm, (pc, col0, cols) in rows.items():
            layout[nm] = (row0, pc.shape[0], col0, cols)
            row0 += pc.shape[0]
        packed_rows = -(-row0 // 8) * 8
        return sum(
            jnp.pad(pc, ((layout[nm][0], packed_rows - layout[nm][0] - pc.shape[0]), (0, d - pc.shape[1])))
            for nm, (pc, _, _) in rows.items()
        )

    def bcast_01():
        csum["in01"], csum["abin"], csum["about"] = _split_wait(tail["01"], [dvecs[0, 0]], "reduce_01_wait")
        chip_sum("in01", "abin", "about")
        grads_small = [pack_small_grads(), small_g["ab_w_s"].reshape(heads * chunk, chunk)]
        tail["small"] = _split_start(small_gather("grads", grads_small), "gather_small_grads_start")
        return (pair_broadcast("in01", "abin", "about"), _after(tail["small"].token))

    def reduce_out00():
        tail["out00"] = _split_start(chip_exchange("out00"), "reduce_out00_start")
        return (_after(tail["out00"].token),)

    dxs = ffn_backward(dxs, 0, 0, 0, after_01, bcast_01, reduce_out00)
    grad_x = dxs.reshape(x.shape)

    last = _split_start(chip_exchange("in00"), "reduce_last_start")
    (csum["out00"],) = _split_wait(tail["out00"], [last.token], "reduce_out00_wait")
    chip_sum("out00")
    _flush("broadcast_out00", pair_broadcast("out00"))
    _split_wait(tail["small"], [reduced["w_ffn_out"]], "gather_small_grads_wait")
    g_all, gws_all = small["grads"]

    out = {}

    def adam_stack(st, after=()):
        w3, m3, v3 = stacks[st]
        assert all(u in done for u, (_, ust, _) in units.items() if ust == st), st
        shape = {"w_ffn_in": w_ffn_in.shape, "w_ffn_out": w_ffn_out.shape, "pool_w_grp": pool_w_grp.shape}.get(st, w3.shape)
        out[st] = tuple(a.reshape(shape) for a in _adam_stack(w3, reduced[st], m3, v3, "adam_" + st, after))

    for st in ("w_ffn_out", "ab_w_in", "ab_w_out", "pool_w_grp"):
        adam_stack(st, (last.token,))

    shapes2d = {
        "norm_g": (3 * n_layers, dq), "b_mod": (9 * n_layers, d), "final_g": (1, d), "ab_norm_v": (1, da),
        "pool_scale": (1, dq), "ab_conv_w": (3, db // N_CHIPS), "ab_b_s": (heads, chunk), "ab_w_s": (heads * chunk, chunk),
    }
    small_w = {"norm_g": (norm_g, m_norm_g, v_norm_g), "b_mod": (b_mod, m_b_mod, v_b_mod), "final_g": (final_g, m_final_g, v_final_g),
               "ab_norm_v": (ab_norm_v, m_ab_norm_v, v_ab_norm_v), "pool_scale": (pool_scale, m_pool_scale, v_pool_scale),
               "ab_conv_w": (ab_conv_w, m_ab_conv_w, v_ab_conv_w), "ab_b_s": (ab_b_s, m_ab_b_s, v_ab_b_s), "ab_w_s": (ab_w_s, m_ab_w_s, v_ab_w_s)}
    smalls = {nm: tuple(a.reshape(shapes2d[nm]) for a in wmv) for nm, wmv in small_w.items()}
    small_out, loss = _small_adam(g_all, gws_all, layout, smalls, chip)
    loss = loss.reshape(())
    for nm, res in small_out.items():
        out[nm] = tuple(a.reshape(small_w[nm][0].shape) for a in res)

    mod_row0 = layout["b_mod"][0]
    dmod_all = g_all[:, mod_row0 : mod_row0 + 9 * n_layers, :].reshape(N_DEV, n_layers, 9 * d)
    dmod_cols = lax.dynamic_slice(dmod_all, (0, 0, chip * ncol), (N_DEV, n_layers, ncol)).transpose(1, 0, 2)
    out["w_mod"] = tuple(_mod_bwd_adam(c_all.T, dmod_cols, w_mod, m_w_mod, v_w_mod, (last.token,)))

    (csum["in00"],) = _split_wait(
        last, [out[st][1] for st in ("w_mod", "w_ffn_out", "ab_w_in", "ab_w_out", "pool_w_grp")], "reduce_last_wait"
    )
    chip_sum("in00")
    _flush("broadcast_last", pair_broadcast("in00"))
    adam_stack("w_ffn_in")

    order = ["norm_g", "w_mod", "b_mod", "w_ffn_in", "w_ffn_out", "ab_w_in", "ab_norm_v", "ab_w_s", "ab_b_s", "ab_conv_w", "ab_w_out", "pool_w_grp", "pool_scale", "final_g"]
    return (loss, grad_x, *[out[nm][0] for nm in order], *[out[nm][1] for nm in order], *[out[nm][2] for nm in order], *[out[nm][3] for nm in order])
```

```python
import functools
import math

import jax
import jax.numpy as jnp
from jax import lax
from jax.experimental import pallas as pl
from jax.experimental.pallas import tpu as pltpu

F32 = jnp.float32
BF16 = jnp.bfloat16
MESH = pl.DeviceIdType.MESH

EPS = 1e-6
ADAM_LR = 0.001
ADAM_B1 = 0.9
ADAM_B2 = 0.999
ADAM_EPS = 1e-08
ADAM_WD = 0.01
ADAM_STEP = 10
POOL_WINDOWS = (2, 4, 8, 16)
POOL_HALO = 16
CONV_HALO = 8
N_CHIPS = 4
N_DEV = 8
VMEM_LIMIT_BYTES = 48 * 1024 * 1024
EW_BLOCK_ELEMS = 256 * 1024


def _pick(n, prefs):
    for p in prefs:
        if p <= n and n % p == 0:
            return p
    return n


def _row_tile(rows, cols):
    best = None
    for d in range(16, rows + 1, 16):
        if rows % d == 0 and d * cols <= EW_BLOCK_ELEMS:
            best = d
    return best or rows


def _dot(a, b):
    return jnp.dot(a, b, preferred_element_type=F32)


def _dot_nt(a, b):
    return lax.dot_general(a, b, (((1,), (1,)), ((), ())), preferred_element_type=F32)


def _dot_tn(a, b):
    return lax.dot_general(a, b, (((0,), (0,)), ((), ())), preferred_element_type=F32)


def _sigmoid(x):
    return 0.5 * jnp.tanh(0.5 * x) + 0.5


_GELU_C = math.sqrt(2.0 / math.pi)


def _gelu(x):
    x2 = x * x
    t = jnp.tanh(_GELU_C * (x + 0.044715 * x2 * x))
    val = 0.5 * x * (1.0 + t)
    grad = 0.5 * (1.0 + t) + 0.5 * x * (1.0 - t * t) * (_GELU_C * (1.0 + 3.0 * 0.044715 * x2))
    return val, grad


def _rstd(x):
    return lax.rsqrt(jnp.mean(x * x, axis=-1, keepdims=True) + EPS)


def _modulate(x, vec_ref):
    return (x * _rstd(x)) * vec_ref[0:1, :] * (1.0 + vec_ref[2:3, :]) + vec_ref[1:2, :]


def _modulate_bwd(x, dh, vec_ref, dvec_ref):
    gn, sh, sc = vec_ref[0:1, :], vec_ref[1:2, :], vec_ref[2:3, :]
    rstd = _rstd(x)
    r = x * rstd
    dvec_ref[0:1, :] += jnp.sum(dh * r * (1.0 + sc), axis=0, keepdims=True)
    dvec_ref[1:2, :] += jnp.sum(dh, axis=0, keepdims=True)
    dvec_ref[2:3, :] += jnp.sum(dh * r * gn, axis=0, keepdims=True)
    gm = gn * (1.0 + sc)
    dr = dh * gm
    dx = rstd * (dr - r * jnp.mean(dr * r, axis=-1, keepdims=True))
    return dx, r * gm + sh


def _adam(w, g, m, v):
    m = ADAM_B1 * m + (1.0 - ADAM_B1) * g
    v = ADAM_B2 * v + (1.0 - ADAM_B2) * (g * g)
    m_hat = m / (1.0 - ADAM_B1**ADAM_STEP)
    v_hat = v / (1.0 - ADAM_B2**ADAM_STEP)
    delta = -ADAM_LR * (m_hat / (jnp.sqrt(v_hat) + ADAM_EPS) + ADAM_WD * w)
    return delta, m, v


_ANY = pl.BlockSpec(memory_space=pl.ANY)


class _Phase:
    def __init__(self, ins, out_shapes, aliases, n_sems, start, finish, then):
        self.ins, self.out_shapes, self.aliases, self.n_sems = list(ins), list(out_shapes), dict(aliases), n_sems
        self.start, self.finish, self.then = start, finish, then


def _call(body, name, grid, in_specs, out_specs, out_shape, ins, scratch=(), prefetch=(), phases=(), in_place=None):
    n_pre, n_in, n_out, n_sc = len(prefetch), len(in_specs), len(out_specs), len(scratch)
    ph_in = [len(p.ins) for p in phases]
    ph_out = [len(p.out_shapes) for p in phases]

    def kernel_body(*refs):
        pos = [0]

        def take(k):
            pos[0] += k
            return refs[pos[0] - k : pos[0]]

        pre, ins_ = take(n_pre), take(n_in)
        p_ins = [take(k) for k in ph_in]
        outs_ = take(n_out)
        p_outs = [take(k) for k in ph_out]
        sc = take(n_sc)
        sems = [take(2) for _ in phases]
        if phases:
            ids = [pl.program_id(a) for a in range(len(grid))]
            first = functools.reduce(jnp.logical_and, [i == 0 for i in ids])
            last = functools.reduce(jnp.logical_and, [i == g - 1 for i, g in zip(ids, grid)])

            @pl.when(first)
            def _():
                for p, pi, po, (send, recv) in zip(phases, p_ins, p_outs, sems):
                    p.start(pi, po, send, recv)

        if body is not None:
            body(*pre, *ins_, *outs_, *sc)
        if phases:

            @pl.when(last)
            def _():
                for p, pi, po, (send, recv) in zip(phases, p_ins, p_outs, sems):
                    p.finish(pi, po, send, recv)

    aliases = {n_pre + i: o for i, o in (in_place or {}).items()}
    i0, o0 = n_pre + n_in, n_out
    for p in phases:
        for i, o in p.aliases.items():
            aliases[i0 + i] = o0 + o
        i0 += len(p.ins)
        o0 += len(p.out_shapes)
    all_in = list(in_specs) + [_ANY] * sum(ph_in)
    all_out = list(out_specs) + [_ANY] * sum(ph_out)
    all_scratch = list(scratch)
    for p in phases:
        all_scratch += [pltpu.SemaphoreType.DMA((p.n_sems,)), pltpu.SemaphoreType.DMA((p.n_sems,))]
    shapes = list(out_shape) + [s for p in phases for s in p.out_shapes]
    operands = list(prefetch) + list(ins) + [a for p in phases for a in p.ins]
    sem = ("arbitrary",) * len(grid)
    params = pltpu.CompilerParams(dimension_semantics=sem, vmem_limit_bytes=VMEM_LIMIT_BYTES)
    if n_pre:
        res = pl.pallas_call(
            kernel_body, name=name, out_shape=shapes, input_output_aliases=aliases, compiler_params=params,
            grid_spec=pltpu.PrefetchScalarGridSpec(
                num_scalar_prefetch=n_pre, grid=grid, in_specs=all_in, out_specs=all_out, scratch_shapes=all_scratch
            ),
        )(*operands)
    else:
        res = pl.pallas_call(
            kernel_body, name=name, grid=grid, in_specs=all_in, out_specs=all_out, out_shape=shapes,
            scratch_shapes=all_scratch, input_output_aliases=aliases, compiler_params=params,
        )(*operands)
    res = list(res)
    outs, rest = res[:n_out], res[n_out:]
    p_res = []
    for k in ph_out:
        p_res.append(rest[:k])
        rest = rest[k:]
    return outs, p_res


def _place():
    return lax.axis_index("x"), lax.axis_index("y"), lax.axis_index("c")


def _other_chips():
    x, y, _ = _place()
    return [(1 - x, y), (x, 1 - y), (1 - x, 1 - y)]


def _flip(k):
    x, y, c = _place()
    return (1 - x if k & 4 else x, 1 - y if k & 2 else y, 1 - c if k & 1 else c)


def _remote(src, dst, send, recv, k, to):
    return pltpu.make_async_remote_copy(
        src_ref=src, dst_ref=dst, send_sem=send.at[k], recv_sem=recv.at[k], device_id=to, device_id_type=MESH
    )


def _phase_small_gather(arrs, then):
    n = len(arrs)

    def copies(ins, outs, send, recv):
        x, y, c = _place()
        me = 4 * x + 2 * y + c
        local = [pltpu.make_async_copy(ins[a], outs[a].at[me], send.at[a * N_DEV]) for a in range(n)]
        remote = [_remote(ins[a], outs[a].at[me], send, recv, a * N_DEV + k, _flip(k)) for a in range(n) for k in range(1, N_DEV)]
        return local, remote

    def start(ins, outs, send, recv):
        local, remote = copies(ins, outs, send, recv)
        for cp in local + remote:
            cp.start()

    def finish(ins, outs, send, recv):
        local, remote = copies(ins, outs, send, recv)
        for cp in remote + local:
            cp.wait()

    shapes = [jax.ShapeDtypeStruct((N_DEV,) + a.shape, a.dtype) for a in arrs]
    return _Phase(arrs, shapes, {}, n * N_DEV, start, finish, then)


def _phase_small_exchange(arr, then):
    def copies(ins, outs, send, recv):
        x, y, c = _place()
        me = 4 * x + 2 * y + c
        local = pltpu.make_async_copy(ins[0].at[me], outs[0].at[me], send.at[0])
        remote = []
        for k in range(1, N_DEV):
            px, py, pc = _flip(k)
            remote.append(_remote(ins[0].at[4 * px + 2 * py + pc], outs[0].at[me], send, recv, k, (px, py, pc)))
        return [local] + remote

    def start(ins, outs, send, recv):
        for cp in copies(ins, outs, send, recv):
            cp.start()

    def finish(ins, outs, send, recv):
        for cp in copies(ins, outs, send, recv):
            cp.wait()

    return _Phase([arr], [jax.ShapeDtypeStruct(arr.shape, arr.dtype)], {}, N_DEV, start, finish, then)


def _after(*arrs):
    nothing = lambda *args: None
    return _Phase(arrs, [], {}, 1, nothing, nothing, nothing)


def _flush(name, *phases):
    _, p_outs = _call(None, name, (1,), [], [], [], [], phases=list(phases))
    for p, po in zip(phases, p_outs):
        p.then(po)


class _Big:
    KINDS = {"full": (True, True), "half": (True, False), "shard": (False, True), "block": (False, False)}

    def __init__(self, f3, s3, h3):
        assert s3 != h3
        self.f3, self.s3, self.h3 = tuple(f3), s3, h3
        self.bd = tuple(f3[a] // (N_CHIPS if a == s3 else 1) // (2 if a == h3 else 1) for a in range(3))
        self.tile = (1, _row_tile(self.bd[1], self.bd[2]), self.bd[2])
        self.grid = tuple(self.bd[a] // self.tile[a] for a in range(3))

    def dims(self, kind):
        chips, halves = self.KINDS[kind]
        return tuple(
            self.bd[a] * (N_CHIPS if chips and a == self.s3 else 1) * (2 if halves and a == self.h3 else 1) for a in range(3)
        )

    def view(self, ref, chip=None, half=None, batch0=0, both_halves=True):
        start = [batch0, 0, 0]
        size = list(ref.shape)
        size[0] = self.bd[0] * (2 if self.h3 == 0 and both_halves else 1)
        if chip is not None:
            start[self.s3] += chip * self.bd[self.s3]
            size[self.s3] = self.bd[self.s3]
        if half is not None:
            start[self.h3] += half * self.bd[self.h3]
            size[self.h3] = self.bd[self.h3]
        return ref.at[tuple(pl.ds(st, sz) for st, sz in zip(start, size))]

    def spec(self, chip_from=None, half_from=None, lead=(), batch0=0):
        extra = "grid" in (chip_from, half_from)

        def index(*args):
            pref, idx = args[-1], list(args[int(extra) : -1])
            idx[0] += batch0
            if chip_from:
                idx[self.s3] += (pref[0] if chip_from == "pref" else args[0]) * self.grid[self.s3]
            if half_from:
                idx[self.h3] += (pref[1] if half_from == "pref" else args[0]) * self.grid[self.h3]
            return (0,) * len(lead) + tuple(idx)

        return pl.BlockSpec(tuple(lead) + self.tile, index)


def _same(arrs):
    return [jax.ShapeDtypeStruct(a.shape, a.dtype) for a in arrs]


def _phase_gather_ici(arrs, bigs, then):
    n = len(arrs)

    def copies(outs, send, recv, arriving):
        x, y, c = _place()
        return [
            _remote(blk, blk, send, recv, 3 * a + j, (*chip, c))
            for j, chip in enumerate(_other_chips())
            for a in range(n)
            for blk in [bigs[a].view(outs[a], 2 * chip[0] + chip[1] if arriving else 2 * x + y, c)]
        ]

    def start(ins, outs, send, recv):
        for cp in copies(outs, send, recv, False):
            cp.start()

    def finish(ins, outs, send, recv):
        for cp in copies(outs, send, recv, True):
            cp.wait_recv()
        for cp in copies(outs, send, recv, False):
            cp.wait_send()

    return _Phase(arrs, _same(arrs), {a: a for a in range(n)}, 3 * n, start, finish, then)


def _phase_gather_sibling(arrs, bigs, then):
    n = len(arrs)

    def copies(outs, send, recv, arriving):
        x, y, c = _place()
        return [
            _remote(blk, blk, send, recv, 3 * a + j, (x, y, 1 - c))
            for j, chip in enumerate(_other_chips())
            for a in range(n)
            for blk in [bigs[a].view(outs[a], 2 * chip[0] + chip[1], 1 - c if arriving else c)]
        ]

    def start(ins, outs, send, recv):
        for cp in copies(outs, send, recv, False):
            cp.start()

    def finish(ins, outs, send, recv):
        for cp in copies(outs, send, recv, True):
            cp.wait_recv()
        for cp in copies(outs, send, recv, False):
            cp.wait_send()

    return _Phase(arrs, _same(arrs), {a: a for a in range(n)}, 3 * n, start, finish, then)


def _phase_pair_exchange(grads, bigs, then):
    n = len(grads)

    def copies(ins, outs, send, recv):
        x, y, c = _place()
        srcs = [ins[a] if ins[a].shape == outs[a].shape else bigs[a].view(ins[a], None, 1 - c) for a in range(n)]
        return [_remote(srcs[a], outs[a], send, recv, a, (x, y, 1 - c)) for a in range(n)]

    def start(ins, outs, send, recv):
        for cp in copies(ins, outs, send, recv):
            cp.start()

    def finish(ins, outs, send, recv):
        for cp in copies(ins, outs, send, recv):
            cp.wait()

    shapes = [jax.ShapeDtypeStruct(b.dims("half"), BF16) for b in bigs]
    return _Phase(grads, shapes, {}, n, start, finish, then)


def _phase_chip_exchange(sums, bigs, then):
    n = len(sums)

    def copies(ins, outs, send, recv):
        _, _, c = _place()
        return [
            _remote(bigs[a].view(ins[a], 2 * chip[0] + chip[1], both_halves=False), outs[a].at[j], send, recv, 3 * a + j, (*chip, c))
            for j, chip in enumerate(_other_chips())
            for a in range(n)
        ]

    def start(ins, outs, send, recv):
        for cp in copies(ins, outs, send, recv):
            cp.start()

    def finish(ins, outs, send, recv):
        for cp in copies(ins, outs, send, recv):
            cp.wait()

    shapes = [jax.ShapeDtypeStruct((N_CHIPS - 1,) + b.dims("block"), BF16) for b in bigs]
    return _Phase(sums, shapes, {}, 3 * n, start, finish, then)


_HBM = pl.BlockSpec(memory_space=pltpu.HBM)
_SEM = pl.BlockSpec(memory_space=pltpu.SEMAPHORE)
_DATAFLOW = pltpu.SideEffectType.DATAFLOW_SIDE_EFFECTING


class _InFlight:
    def __init__(self, phase, send, recv, arrays, token):
        self.phase, self.send, self.recv, self.arrays, self.token = phase, send, recv, arrays, token


def _phase_results(phase, refs):
    n_in = len(phase.ins)
    updated = {o: i for i, o in phase.aliases.items()}
    fresh = [o for o in range(len(phase.out_shapes)) if o not in updated]
    return [refs[updated[o]] if o in updated else refs[n_in + fresh.index(o)] for o in range(len(phase.out_shapes))]


def _split_start(phase, name):
    n_in = len(phase.ins)
    fresh = [s for o, s in enumerate(phase.out_shapes) if o not in phase.aliases.values()]
    arrays = list(phase.ins) + [lax.empty(s.shape, s.dtype) for s in fresh]
    n = len(arrays)

    def body(*refs):
        phase.start(refs[:n_in], _phase_results(phase, refs[:n]), refs[n], refs[n + 1])
        refs[-1][...] = jnp.zeros_like(refs[-1])

    operands = [pltpu.with_memory_space_constraint(a, pltpu.HBM) for a in arrays]
    res = pl.pallas_call(
        body, name=name,
        out_shape=[pltpu.SemaphoreType.DMA((phase.n_sems,)), pltpu.SemaphoreType.DMA((phase.n_sems,))]
        + [pltpu.HBM(a.shape, a.dtype) for a in arrays] + [jax.ShapeDtypeStruct((8, 128), F32)],
        in_specs=[_HBM] * n, out_specs=[_SEM, _SEM] + [_HBM] * n + [pl.BlockSpec(memory_space=pltpu.VMEM)],
        input_output_aliases={i: 2 + i for i in range(n)},
        compiler_params=pltpu.CompilerParams(has_side_effects=_DATAFLOW),
    )(*operands)
    return _InFlight(phase, res[0], res[1], list(res[2 : 2 + n]), res[-1])


def _split_wait(flight, after, name):
    phase, n = flight.phase, len(flight.arrays)
    n_in = len(phase.ins)

    def body(*refs):
        phase.finish(refs[:n_in], _phase_results(phase, refs[:n]), refs[n], refs[n + 1])

    res = pl.pallas_call(
        body, name=name, out_shape=[pltpu.HBM(a.shape, a.dtype) for a in flight.arrays],
        in_specs=[_HBM] * n + [_SEM, _SEM] + [_ANY] * len(after), out_specs=[_HBM] * n,
        input_output_aliases={i: i for i in range(n)},
        compiler_params=pltpu.CompilerParams(has_side_effects=_DATAFLOW),
    )(*flight.arrays, flight.send, flight.recv, *after)
    res = list(res)
    phase.then(_phase_results(phase, res))
    return res[:n_in]


def _phase_pair_broadcast(stacks, bigs, batch0s, then):
    n = len(stacks)

    def start(ins, outs, send, recv):
        x, y, c = _place()
        for a in range(n):
            blk = bigs[a].view(outs[a], None, c, batch0s[a])
            _remote(blk, blk, send, recv, a, (x, y, 1 - c)).start()

    def finish(ins, outs, send, recv):
        x, y, c = _place()
        for a in range(n):
            mine = bigs[a].view(outs[a], None, c, batch0s[a])
            theirs = bigs[a].view(outs[a], None, 1 - c, batch0s[a])
            _remote(mine, mine, send, recv, a, (x, y, 1 - c)).wait_send()
            _remote(theirs, theirs, send, recv, a, (x, y, 1 - c)).wait_recv()

    return _Phase(stacks, _same(stacks), {a: a for a in range(n)}, n, start, finish, then)


def _tile_call(body, name, big, where, extra, ins, in_specs, out_specs, out_shape, phases=()):
    grid = ((extra,) if extra else ()) + big.grid
    return _call(body, name, grid, in_specs, out_specs, out_shape, ins, prefetch=(where,), phases=phases)


def _cast_into_full(w_stack, batch0, big, where, name, phases=()):
    def body(_, w_ref, o_ref):
        o_ref[...] = w_ref[...].astype(BF16)

    return _tile_call(
        body, name, big, where, 2, [w_stack], [big.spec(None, "grid", batch0=batch0)], [big.spec("pref", "grid")],
        [jax.ShapeDtypeStruct(big.dims("full"), BF16)], phases,
    )


def _pair_sum(g_full, recv_half, big, where, name, phases=()):
    def body(_, g_ref, r_ref, o_ref):
        o_ref[...] = (g_ref[...].astype(F32) + r_ref[...].astype(F32)).astype(BF16)

    half = big.spec("grid", None)
    return _tile_call(
        body, name, big, where, N_CHIPS, [g_full, recv_half], [big.spec("grid", "pref"), half], [half],
        [jax.ShapeDtypeStruct(big.dims("half"), BF16)], phases,
    )


def _chip_sum(chip_sum, parts, big, where, stack, stack_shape, batch0, name, phases=()):
    def body(_, own_ref, p_ref, *rest):
        acc = own_ref[...].astype(F32)
        for k in range(N_CHIPS - 1):
            acc = acc + p_ref[k].astype(F32)
        rest[-1][...] = acc

    ins = [chip_sum, parts] + ([stack] if stack is not None else [])
    in_specs = [big.spec("pref", None), big.spec(None, None, lead=(N_CHIPS - 1,))] + ([_ANY] if stack is not None else [])
    return _call(
        body, name, big.grid, in_specs, [big.spec(None, "pref", batch0=batch0)], [jax.ShapeDtypeStruct(stack_shape, F32)], ins,
        prefetch=(where,), phases=phases, in_place={2: 0} if stack is not None else None,
    )


def _adam_stack(w, g, m, v, name, after=()):
    b, r, c = w.shape
    tr = _row_tile(r, c)

    def body(w_ref, g_ref, m_ref, v_ref, *rest):
        go_ref, d_ref, mo_ref, vo_ref = rest[-4:]
        gv = g_ref[...]
        d, mo, vo = _adam(w_ref[...], gv, m_ref[...], v_ref[...])
        go_ref[...] = gv
        d_ref[...] = d
        mo_ref[...] = mo
        vo_ref[...] = vo

    spec = pl.BlockSpec((1, tr, c), lambda bb, i: (bb, i, 0))
    outs, _ = _call(
        body, name, (b, r // tr), [spec] * 4 + [_ANY] * len(after), [spec] * 4, [jax.ShapeDtypeStruct(w.shape, F32)] * 4,
        [w, g, m, v, *after],
    )
    return outs


def _mod_fwd(c_all, w_mod, b_cols, phases=()):
    n_layers, d, n = w_mod.shape
    tn = _pick(n, (768, 512, 384, 256, 128))

    def body(c_ref, w_ref, b_ref, o_ref):
        cv = c_ref[...]
        ca = (cv * _sigmoid(cv)).astype(BF16)
        o_ref[0] = _dot(ca, w_ref[0].astype(BF16)) + b_ref[0]

    return _call(
        body, "mod_fwd", (n_layers, n // tn),
        [
            pl.BlockSpec((N_DEV, d), lambda l, j: (0, 0)),
            pl.BlockSpec((1, d, tn), lambda l, j: (l, 0, j)),
            pl.BlockSpec((1, 1, tn), lambda l, j: (l, 0, j)),
        ],
        [pl.BlockSpec((1, N_DEV, tn), lambda l, j: (l, 0, j))],
        [jax.ShapeDtypeStruct((n_layers, N_DEV, n), F32)], [c_all, w_mod, b_cols], phases=phases,
    )


def _mod_bwd_adam(c_all_t, dmod_cols, w, m, v, after=()):
    n_layers, d, n = w.shape
    tn = _pick(n, (384, 256, 128))

    def body(c_ref, dm_ref, w_ref, m_ref, v_ref, *rest):
        g_ref, d_ref, mo_ref, vo_ref = rest[-4:]
        cv = c_ref[...]
        ca = (cv * _sigmoid(cv)).astype(BF16)
        g = _dot(ca, dm_ref[0].astype(BF16))
        g_ref[0] = g
        dl, mo, vo = _adam(w_ref[0], g, m_ref[0], v_ref[0])
        d_ref[0] = dl
        mo_ref[0] = mo
        vo_ref[0] = vo

    wspec = pl.BlockSpec((1, d, tn), lambda l, j: (l, 0, j))
    outs, _ = _call(
        body, "mod_bwd_adam", (n_layers, n // tn),
        [pl.BlockSpec((d, N_DEV), lambda l, j: (0, 0)), pl.BlockSpec((1, N_DEV, tn), lambda l, j: (l, 0, j)), wspec, wspec, wspec]
        + [_ANY] * len(after),
        [wspec] * 4, [jax.ShapeDtypeStruct(w.shape, F32)] * 4, [c_all_t, dmod_cols, w, m, v, *after],
    )
    return outs


def _ffn_fwd(x, vec, w_in, w_out, name, phases=()):
    s, d = x.shape
    f = w_out.shape[1]
    tm = _pick(s, (1024, 512, 256, 128))
    tf = _pick(f, (256, 128))
    nf = f // tf

    def body(x_ref, vec_ref, wg_ref, wu_ref, wo_ref, xo_ref, g_ref, u_ref, y_ref, h_sc, acc_sc):
        j = pl.program_id(1)

        @pl.when(j == 0)
        def _():
            h_sc[...] = _modulate(x_ref[...], vec_ref).astype(BF16)
            acc_sc[...] = jnp.zeros_like(acc_sc)

        h = h_sc[...]
        g = _dot(h, wg_ref[0])
        u = _dot(h, wu_ref[0])
        g_ref[...] = g.astype(BF16)
        u_ref[...] = u.astype(BF16)
        a = (g * _sigmoid(g) * u).astype(BF16)
        acc_sc[...] += _dot(a, wo_ref[0])

        @pl.when(j == nf - 1)
        def _():
            yv = acc_sc[...]
            xo_ref[...] = x_ref[...] + 0.5 * vec_ref[3:4, :] * yv
            y_ref[...] = yv.astype(BF16)

    row = pl.BlockSpec((tm, d), lambda i, j: (i, 0))
    hid = pl.BlockSpec((tm, tf), lambda i, j: (i, j))
    return _call(
        body, name, (s // tm, nf),
        [
            row,
            pl.BlockSpec((8, d), lambda i, j: (0, 0)),
            pl.BlockSpec((1, d, tf), lambda i, j: (0, 0, j)),
            pl.BlockSpec((1, d, tf), lambda i, j: (0, 0, nf + j)),
            pl.BlockSpec((1, tf, d), lambda i, j: (0, j, 0)),
        ],
        [row, hid, hid, row],
        [
            jax.ShapeDtypeStruct((s, d), F32),
            jax.ShapeDtypeStruct((s, f), BF16),
            jax.ShapeDtypeStruct((s, f), BF16),
            jax.ShapeDtypeStruct((s, d), BF16),
        ],
        [x, vec, w_in, w_in, w_out],
        scratch=[pltpu.VMEM((tm, d), BF16), pltpu.VMEM((tm, d), F32)], phases=phases,
    )


def _ffn_bwd(dxo, x, vec, gg, uu, y, w_in, w_out, name, phases=()):
    s, d = x.shape
    f = w_out.shape[1]
    tm = _pick(s, (512, 256, 128))
    tf = _pick(f, (256, 128))
    nf = f // tf

    def body(dxo_ref, x_ref, vec_ref, g_ref, u_ref, y_ref, wg_ref, wu_ref, wo_ref,
             dx_ref, dg_ref, du_ref, a_ref, h_ref, dy_ref, dvec_ref, acc_sc):
        i, j = pl.program_id(0), pl.program_id(1)

        @pl.when((i == 0) & (j == 0))
        def _():
            dvec_ref[...] = jnp.zeros_like(dvec_ref)

        @pl.when(j == 0)
        def _():
            dxo_v = dxo_ref[...]
            dy_ref[...] = (0.5 * vec_ref[3:4, :] * dxo_v).astype(BF16)
            dvec_ref[3:4, :] += 0.5 * jnp.sum(dxo_v * y_ref[...].astype(F32), axis=0, keepdims=True)
            acc_sc[...] = jnp.zeros_like(acc_sc)

        da = _dot_nt(dy_ref[...], wo_ref[0])
        g = g_ref[...].astype(F32)
        u = u_ref[...].astype(F32)
        sig = _sigmoid(g)
        sl = g * sig
        a_ref[...] = (sl * u).astype(BF16)
        dg = (da * u * (sig * (1.0 + g * (1.0 - sig)))).astype(BF16)
        du = (da * sl).astype(BF16)
        dg_ref[...] = dg
        du_ref[...] = du
        acc_sc[...] += _dot_nt(dg, wg_ref[0]) + _dot_nt(du, wu_ref[0])

        @pl.when(j == nf - 1)
        def _():
            dx, h = _modulate_bwd(x_ref[...], acc_sc[...], vec_ref, dvec_ref)
            dx_ref[...] = dxo_ref[...] + dx
            h_ref[...] = h.astype(BF16)

    row = pl.BlockSpec((tm, d), lambda i, j: (i, 0))
    hid = pl.BlockSpec((tm, tf), lambda i, j: (i, j))
    vecs = pl.BlockSpec((8, d), lambda i, j: (0, 0))
    return _call(
        body, name, (s // tm, nf),
        [
            row, row, vecs, hid, hid, row,
            pl.BlockSpec((1, d, tf), lambda i, j: (0, 0, j)),
            pl.BlockSpec((1, d, tf), lambda i, j: (0, 0, nf + j)),
            pl.BlockSpec((1, tf, d), lambda i, j: (0, j, 0)),
        ],
        [row, hid, hid, hid, row, row, vecs],
        [
            jax.ShapeDtypeStruct((s, d), F32),
            jax.ShapeDtypeStruct((s, f), BF16),
            jax.ShapeDtypeStruct((s, f), BF16),
            jax.ShapeDtypeStruct((s, f), BF16),
            jax.ShapeDtypeStruct((s, d), BF16),
            jax.ShapeDtypeStruct((s, d), BF16),
            jax.ShapeDtypeStruct((8, d), F32),
        ],
        [dxo, x, vec, gg, uu, y, w_in, w_in, w_out],
        scratch=[pltpu.VMEM((tm, d), F32)], phases=phases,
    )


def _grad_half(a, bs, big, where, mine, recv, name, phases=()):
    s, k1 = a.shape
    n = bs[0].shape[1]
    groups = len(bs)
    rows_halved = big.h3 == 1
    assert rows_halved or groups == 1
    kk, nn = (k1 // 2, n) if rows_halved else (k1, n // 2)
    tk = _pick(kk, (1408, 1024, 512, 256, 128))
    tn = _pick(nn, (1408, 1024, 640, 512, 256, 128))
    nkb, nnb = kk // tk, nn // tn
    assert (recv is None) == (not mine)

    def half(pref):
        return pref[1] if mine else 1 - pref[1]

    def body(_, a_ref, *rest):
        q = pl.program_id(1)
        for p in range(groups):

            @pl.when(q == p)
            def _(p=p):
                acc = _dot_tn(a_ref[...], rest[p][...])
                if recv is not None:
                    acc = acc + rest[groups][0].astype(F32)
                rest[-1][0] = acc.astype(BF16)

    def b_block(p):
        def index(i, q, j, pref):
            jj = jnp.where(q == p, j, jnp.where(q < p, 0, nnb - 1))
            return (0, jj + (0 if rows_halved else half(pref) * nnb))

        return pl.BlockSpec((s, tn), index)

    out_spec = pl.BlockSpec((1, tk, tn), lambda i, q, j, pref: (0, i, q * nnb + j))
    in_specs = [pl.BlockSpec((s, tk), lambda i, q, j, pref: (0, i + (half(pref) * nkb if rows_halved else 0)))]
    in_specs += [b_block(p) for p in range(groups)]
    ins = [a, *bs]
    if recv is not None:
        in_specs.append(out_spec)
        ins.append(recv)
    return _call(
        body, name, (nkb, groups, nnb), in_specs, [out_spec], [jax.ShapeDtypeStruct(big.dims("half"), BF16)], ins,
        prefetch=(where,), phases=phases,
    )


def _proj_mod_fwd(x, vec, w, phases=()):
    s, d = x.shape
    n = w.shape[2]
    tm = _pick(s, (512, 256, 128))
    tn = _pick(n, (640, 512, 256, 128))

    def body(x_ref, vec_ref, w_ref, o_ref, h_sc):
        @pl.when(pl.program_id(1) == 0)
        def _():
            h_sc[...] = _modulate(x_ref[...], vec_ref).astype(BF16)

        o_ref[...] = _dot(h_sc[...], w_ref[0])

    return _call(
        body, "ab_in_fwd", (s // tm, n // tn),
        [
            pl.BlockSpec((tm, d), lambda i, j: (i, 0)),
            pl.BlockSpec((8, d), lambda i, j: (0, 0)),
            pl.BlockSpec((1, d, tn), lambda i, j: (0, 0, j)),
        ],
        [pl.BlockSpec((tm, tn), lambda i, j: (i, j))],
        [jax.ShapeDtypeStruct((s, n), F32)], [x, vec, w],
        scratch=[pltpu.VMEM((tm, d), BF16)], phases=phases,
    )


def _proj_res_fwd(a, w, x, vec, phases=()):
    s, kd = a.shape
    d = x.shape[1]
    tm = _pick(s, (512, 256, 128))

    def body(a_ref, w_ref, x_ref, vec_ref, xo_ref, y_ref):
        yv = _dot(a_ref[...], w_ref[0])
        xo_ref[...] = x_ref[...] + vec_ref[3:4, :] * yv
        y_ref[...] = yv.astype(BF16)

    row = pl.BlockSpec((tm, d), lambda i: (i, 0))
    return _call(
        body, "ab_out_fwd", (s // tm,),
        [pl.BlockSpec((tm, kd), lambda i: (i, 0)), pl.BlockSpec((1, kd, d), lambda i: (0, 0, 0)), row, pl.BlockSpec((8, d), lambda i: (0, 0))],
        [row, row],
        [jax.ShapeDtypeStruct((s, d), F32), jax.ShapeDtypeStruct((s, d), BF16)], [a, w, x, vec], phases=phases,
    )


def _proj_res_bwd(dxo, y, vec, w, phases=()):
    s, d = dxo.shape
    kd = w.shape[1]
    tm = _pick(s, (512, 256, 128))

    def body(dxo_ref, y_ref, vec_ref, w_ref, dy_ref, da_ref, dgate_ref):
        @pl.when(pl.program_id(0) == 0)
        def _():
            dgate_ref[...] = jnp.zeros_like(dgate_ref)

        dxo_v = dxo_ref[...]
        dy = (vec_ref[3:4, :] * dxo_v).astype(BF16)
        dy_ref[...] = dy
        dgate_ref[3:4, :] += jnp.sum(dxo_v * y_ref[...].astype(F32), axis=0, keepdims=True)
        da_ref[...] = _dot_nt(dy, w_ref[0]).astype(BF16)

    row = pl.BlockSpec((tm, d), lambda i: (i, 0))
    vecs = pl.BlockSpec((8, d), lambda i: (0, 0))
    return _call(
        body, "ab_out_bwd", (s // tm,),
        [row, row, vecs, pl.BlockSpec((1, kd, d), lambda i: (0, 0, 0))],
        [row, pl.BlockSpec((tm, kd), lambda i: (i, 0)), vecs],
        [jax.ShapeDtypeStruct((s, d), BF16), jax.ShapeDtypeStruct((s, kd), BF16), jax.ShapeDtypeStruct((8, d), F32)],
        [dxo, y, vec, w], phases=phases,
    )


def _proj_mod_bwd(dproj, w, x, vec, dxo, dvec_in, name, phases=()):
    parts, s, n_part = dproj.shape
    d = x.shape[1]
    tm = _pick(s, (512, 256, 128))
    tk = _pick(n_part, (1408, 1280, 1024, 512, 256, 128))
    per_part = n_part // tk
    nk = parts * per_part

    def body(dp_ref, w_ref, x_ref, vec_ref, dxo_ref, dvi_ref, dx_ref, h_ref, dvec_ref, acc_sc):
        i, k = pl.program_id(0), pl.program_id(1)

        @pl.when((i == 0) & (k == 0))
        def _():
            dvec_ref[...] = dvi_ref[...]

        @pl.when(k == 0)
        def _():
            acc_sc[...] = jnp.zeros_like(acc_sc)

        acc_sc[...] += _dot_nt(dp_ref[0], w_ref[0])

        @pl.when(k == nk - 1)
        def _():
            dx, h = _modulate_bwd(x_ref[...], acc_sc[...], vec_ref, dvec_ref)
            dx_ref[...] = dxo_ref[...] + dx
            h_ref[...] = h.astype(BF16)

    row = pl.BlockSpec((tm, d), lambda i, k: (i, 0))
    vecs = pl.BlockSpec((8, d), lambda i, k: (0, 0))
    return _call(
        body, name, (s // tm, nk),
        [
            pl.BlockSpec((1, tm, tk), lambda i, k: (k // per_part, i, k % per_part)),
            pl.BlockSpec((1, d, tk), lambda i, k: (0, 0, k)),
            row, vecs, row, vecs,
        ],
        [row, row, vecs],
        [jax.ShapeDtypeStruct((s, d), F32), jax.ShapeDtypeStruct((s, d), BF16), jax.ShapeDtypeStruct((8, d), F32)],
        [dproj, w, x, vec, dxo, dvec_in], scratch=[pltpu.VMEM((tm, d), F32)], phases=phases,
    )


def _tril(n):
    return lax.broadcasted_iota(jnp.int32, (n, n), 0) >= lax.broadcasted_iota(jnp.int32, (n, n), 1)


def _layernorm_stats(gv):
    mu = jnp.mean(gv, axis=-1, keepdims=True)
    cen = gv - mu
    rstd = lax.rsqrt(jnp.mean(cen * cen, axis=-1, keepdims=True) + EPS)
    return cen * rstd, rstd


def _shift_down(q, k, above_ref, c_cg, c_xb, first):
    width = q.shape[1]
    rows = lax.broadcasted_iota(jnp.int32, q.shape, 0)
    out = pltpu.roll(q, k, 0)
    for r in range(k):
        src = CONV_HALO - k + r
        above = above_ref[src : src + 1, c_cg : c_cg + width] * above_ref[src : src + 1, c_xb : c_xb + width]
        above = jnp.where(first, 0.0, above)
        out = jnp.where(rows == r, above, out)
    return out


def _ab_mix_fwd(proj, norm_v, w_s, b_rows, conv_w, phases=()):
    s, n = proj.shape
    heads, chunk, _ = w_s.shape
    da = norm_v.shape[1]
    hd = da // heads
    db = conv_w.shape[1]
    tm = _pick(s, (512, 256, 128))

    def body(p_ref, ph_ref, nv_ref, ws_ref, b_ref, cw_ref, o_ref):
        first = pl.program_id(0) == 0
        gu, _ = _gelu(p_ref[:, 0:da])
        gv, _ = _gelu(p_ref[:, da : 2 * da])
        xhat, _ = _layernorm_stats(gv)
        vn = (xhat * nv_ref[...]).astype(BF16)
        mask = _tril(chunk)
        for hh in range(heads):
            wm = jnp.where(mask, ws_ref[hh], 0.0).astype(BF16)
            cols = slice(hh * hd, (hh + 1) * hd)
            for nn in range(tm // chunk):
                rows = slice(nn * chunk, (nn + 1) * chunk)
                z = _dot(wm, vn[rows, cols]) + b_ref[:, cols]
                o_ref[rows, cols] = (gu[rows, cols] * z).astype(BF16)
        c_cg, c_xb = 2 * da + db, 2 * da + 2 * db
        bg = p_ref[:, 2 * da : 2 * da + db]
        q = p_ref[:, c_cg : c_cg + db] * p_ref[:, c_xb : c_xb + db]
        q1 = _shift_down(q, 1, ph_ref, c_cg, c_xb, first)
        q2 = _shift_down(q, 2, ph_ref, c_cg, c_xb, first)
        conv = cw_ref[0:1, :] * q2 + cw_ref[1:2, :] * q1 + cw_ref[2:3, :] * q
        o_ref[:, da : da + db] = (bg * conv).astype(BF16)

    nh = tm // CONV_HALO
    return _call(
        body, "ab_mix_fwd", (s // tm,),
        [
            pl.BlockSpec((tm, n), lambda i: (i, 0)),
            pl.BlockSpec((CONV_HALO, n), lambda i: (jnp.maximum(i * nh - 1, 0), 0)),
            pl.BlockSpec((1, da), lambda i: (0, 0)),
            pl.BlockSpec((heads, chunk, chunk), lambda i: (0, 0, 0)),
            pl.BlockSpec((chunk, da), lambda i: (0, 0)),
            pl.BlockSpec((3, db), lambda i: (0, 0)),
        ],
        [pl.BlockSpec((tm, da + db), lambda i: (i, 0))],
        [jax.ShapeDtypeStruct((s, da + db), BF16)], [proj, proj, norm_v, w_s, b_rows, conv_w], phases=phases,
    )


def _ab_mix_bwd(proj, dcat, norm_v, w_s, b_rows, conv_w, phases=()):
    s, n = proj.shape
    heads, chunk, _ = w_s.shape
    da = norm_v.shape[1]
    hd = da // heads
    db = conv_w.shape[1]
    tm = _pick(s, (512, 256, 128))
    nblk = s // tm
    dhalo = 2 * CONV_HALO

    def body(p_ref, pa_ref, pb_ref, dc_ref, dcb_ref, nv_ref, ws_ref, b_ref, cw_ref,
             dp_ref, dnv_ref, dws_ref, dzs_ref, dcw_ref, dvn_sc):
        i = pl.program_id(0)
        first, last = i == 0, i == nblk - 1

        @pl.when(first)
        def _():
            dnv_ref[...] = jnp.zeros_like(dnv_ref)
            dws_ref[...] = jnp.zeros_like(dws_ref)
            dzs_ref[...] = jnp.zeros_like(dzs_ref)
            dcw_ref[...] = jnp.zeros_like(dcw_ref)

        uu = p_ref[:, 0:da]
        gu, gu_grad = _gelu(uu)
        gv, gv_grad = _gelu(p_ref[:, da : 2 * da])
        xhat, rstd = _layernorm_stats(gv)
        nv = nv_ref[...]
        vn = (xhat * nv).astype(BF16)
        dya = dc_ref[:, 0:da].astype(F32)
        dz = (dya * gu).astype(BF16)
        mask = _tril(chunk)
        for hh in range(heads):
            wm = jnp.where(mask, ws_ref[hh], 0.0).astype(BF16)
            cols = slice(hh * hd, (hh + 1) * hd)
            dws = jnp.zeros((chunk, chunk), F32)
            for nn in range(tm // chunk):
                rows = slice(nn * chunk, (nn + 1) * chunk)
                z = _dot(wm, vn[rows, cols]) + b_ref[:, cols]
                dp_ref[rows, cols] = (dya[rows, cols] * z * gu_grad[rows, cols]).astype(BF16)
                dz_blk = dz[rows, cols]
                dws = dws + _dot_nt(dz_blk, vn[rows, cols])
                dzs_ref[:, cols] += dz_blk.astype(F32)
                dvn = _dot_tn(wm, dz_blk)
                dnv_ref[:, cols] += jnp.sum(dvn * xhat[rows, cols], axis=0, keepdims=True)
                dvn_sc[rows, cols] = dvn
            dws_ref[hh] += jnp.where(mask, dws, 0.0)
        dxhat = dvn_sc[...] * nv
        dgv = rstd * (dxhat - jnp.mean(dxhat, axis=-1, keepdims=True) - xhat * jnp.mean(dxhat * xhat, axis=-1, keepdims=True))
        dp_ref[:, da : 2 * da] = (dgv * gv_grad).astype(BF16)

        c_bg, c_cg, c_xb = 2 * da, 2 * da + db, 2 * da + 2 * db
        bg = p_ref[:, c_bg : c_bg + db]
        cg = p_ref[:, c_cg : c_cg + db]
        xb = p_ref[:, c_xb : c_xb + db]
        q = cg * xb
        q1 = _shift_down(q, 1, pa_ref, c_cg, c_xb, first)
        q2 = _shift_down(q, 2, pa_ref, c_cg, c_xb, first)
        dyb = dc_ref[:, da : da + db].astype(F32)
        conv = cw_ref[0:1, :] * q2 + cw_ref[1:2, :] * q1 + cw_ref[2:3, :] * q
        dp_ref[:, c_bg : c_bg + db] = (dyb * conv).astype(BF16)
        e = dyb * bg
        dcw_ref[0:1, :] += jnp.sum(e * q2, axis=0, keepdims=True)
        dcw_ref[1:2, :] += jnp.sum(e * q1, axis=0, keepdims=True)
        dcw_ref[2:3, :] += jnp.sum(e * q, axis=0, keepdims=True)
        rows = lax.broadcasted_iota(jnp.int32, e.shape, 0)
        dq = cw_ref[2:3, :] * e
        for kk in (1, 2):
            ek = pltpu.roll(e, tm - kk, 0)
            for r in range(kk):
                below = dcb_ref[r : r + 1, da : da + db].astype(F32) * pb_ref[r : r + 1, c_bg : c_bg + db]
                below = jnp.where(last, 0.0, below)
                ek = jnp.where(rows == tm - kk + r, below, ek)
            dq = dq + cw_ref[2 - kk : 3 - kk, :] * ek
        dp_ref[:, c_cg : c_cg + db] = (dq * xb).astype(BF16)
        dp_ref[:, c_xb : c_xb + db] = (dq * cg).astype(BF16)

    nh = tm // CONV_HALO
    nhb = tm // dhalo
    const2 = lambda i: (0, 0)
    return _call(
        body, "ab_mix_bwd", (nblk,),
        [
            pl.BlockSpec((tm, n), lambda i: (i, 0)),
            pl.BlockSpec((CONV_HALO, n), lambda i: (jnp.maximum(i * nh - 1, 0), 0)),
            pl.BlockSpec((CONV_HALO, n), lambda i: (jnp.minimum((i + 1) * nh, s // CONV_HALO - 1), 0)),
            pl.BlockSpec((tm, da + db), lambda i: (i, 0)),
            pl.BlockSpec((dhalo, da + db), lambda i: (jnp.minimum((i + 1) * nhb, s // dhalo - 1), 0)),
            pl.BlockSpec((1, da), const2),
            pl.BlockSpec((heads, chunk, chunk), lambda i: (0, 0, 0)),
            pl.BlockSpec((chunk, da), const2),
            pl.BlockSpec((3, db), const2),
        ],
        [
            pl.BlockSpec((tm, n), lambda i: (i, 0)),
            pl.BlockSpec((1, da), const2),
            pl.BlockSpec((heads, chunk, chunk), lambda i: (0, 0, 0)),
            pl.BlockSpec((chunk, da), const2),
            pl.BlockSpec((3, db), const2),
        ],
        [
            jax.ShapeDtypeStruct((s, n), BF16),
            jax.ShapeDtypeStruct((1, da), F32),
            jax.ShapeDtypeStruct((heads, chunk, chunk), F32),
            jax.ShapeDtypeStruct((chunk, da), F32),
            jax.ShapeDtypeStruct((3, db), F32),
        ],
        [proj, proj, proj, dcat, dcat, norm_v, w_s, b_rows, conv_w],
        scratch=[pltpu.VMEM((tm, da), F32)], phases=phases,
    )


def _pool_counts(tm, i, w):
    t = i * tm + lax.broadcasted_iota(jnp.int32, (tm, 1), 0)
    return jnp.minimum(t + 1, w).astype(F32)


def _pool_fwd(x, vec, w_grp, scale, phases=()):
    s, d = x.shape
    groups, gd, _ = w_grp.shape
    tm = _pick(s, (512, 256, 128))

    def body(x_ref, xa_ref, vec_ref, w_ref, sc_ref, xo_ref, p_ref, o_ref):
        i = pl.program_id(0)
        h = _modulate(x_ref[...], vec_ref)
        ha = jnp.where(i == 0, 0.0, _modulate(xa_ref[...], vec_ref))
        ext = jnp.concatenate([ha, h], axis=0)
        for gi, w in enumerate(POOL_WINDOWS):
            cols = slice(gi * gd, (gi + 1) * gd)
            acc = ext[:, cols]
            step = 1
            while step < w:
                acc = acc + pltpu.roll(acc, step, 0)
                step *= 2
            p = (acc[POOL_HALO:, :] / _pool_counts(tm, i, w) - h[:, cols]).astype(BF16)
            p_ref[:, cols] = p
            o_ref[:, cols] = _dot(p, w_ref[gi]).astype(BF16)
        xo_ref[...] = x_ref[...] + vec_ref[3:4, :] * (o_ref[...].astype(F32) * sc_ref[...])

    nh = tm // POOL_HALO
    row = pl.BlockSpec((tm, d), lambda i: (i, 0))
    return _call(
        body, "pool_fwd", (s // tm,),
        [
            row,
            pl.BlockSpec((POOL_HALO, d), lambda i: (jnp.maximum(i * nh - 1, 0), 0)),
            pl.BlockSpec((8, d), lambda i: (0, 0)),
            pl.BlockSpec((groups, gd, gd), lambda i: (0, 0, 0)),
            pl.BlockSpec((1, d), lambda i: (0, 0)),
        ],
        [row, row, row],
        [jax.ShapeDtypeStruct((s, d), F32), jax.ShapeDtypeStruct((s, d), BF16), jax.ShapeDtypeStruct((s, d), BF16)],
        [x, x, vec, w_grp, scale], phases=phases,
    )


def _pool_bwd(dxo, x, vec, p, o, w_grp, scale, phases=()):
    s, d = x.shape
    groups, gd, _ = w_grp.shape
    tm = _pick(s, (512, 256, 128))
    nblk = s // tm

    def body(dxo_ref, dxb_ref, x_ref, vec_ref, p_ref, o_ref, w_ref, sc_ref, dx_ref, dw_ref, dsc_ref, dvec_ref, dw_sc):
        i = pl.program_id(0)

        @pl.when(i == 0)
        def _():
            dw_sc[...] = jnp.zeros_like(dw_sc)
            dsc_ref[...] = jnp.zeros_like(dsc_ref)
            dvec_ref[...] = jnp.zeros_like(dvec_ref)

        gate, sc = vec_ref[3:4, :], sc_ref[...]
        dxo_v = dxo_ref[...]
        ov = o_ref[...].astype(F32)
        dvec_ref[3:4, :] += jnp.sum(dxo_v * (ov * sc), axis=0, keepdims=True)
        dy = gate * dxo_v
        dsc_ref[...] += jnp.sum(dy * ov, axis=0, keepdims=True)
        dout = (dy * sc).astype(BF16)
        dout_b = jnp.where(i == nblk - 1, 0.0, gate * dxb_ref[...] * sc).astype(BF16)
        for gi, w in enumerate(POOL_WINDOWS):
            cols = slice(gi * gd, (gi + 1) * gd)
            dw_sc[gi] += _dot_tn(p_ref[:, cols], dout[:, cols])
            wb = w_ref[gi]
            dp = _dot_nt(dout[:, cols], wb)
            dp_b = _dot_nt(dout_b[:, cols], wb)
            e = dp / _pool_counts(tm, i, w)
            t_below = (i + 1) * tm + lax.broadcasted_iota(jnp.int32, (POOL_HALO, 1), 0)
            e_b = dp_b / jnp.minimum(t_below + 1, w).astype(F32)
            acc = jnp.concatenate([e, e_b], axis=0)
            step = 1
            while step < w:
                acc = acc + pltpu.roll(acc, tm + POOL_HALO - step, 0)
                step *= 2
            dx_ref[:, cols] = acc[:tm, :] - dp
        dx, _ = _modulate_bwd(x_ref[...], dx_ref[...], vec_ref, dvec_ref)
        dx_ref[...] = dxo_v + dx

        @pl.when(i == nblk - 1)
        def _():
            dw_ref[...] = dw_sc[...].astype(BF16)

    nh = tm // POOL_HALO
    row = pl.BlockSpec((tm, d), lambda i: (i, 0))
    vecs = pl.BlockSpec((8, d), lambda i: (0, 0))
    wspec = pl.BlockSpec((groups, gd, gd), lambda i: (0, 0, 0))
    return _call(
        body, "pool_bwd", (nblk,),
        [
            row,
            pl.BlockSpec((POOL_HALO, d), lambda i: (jnp.minimum((i + 1) * nh, s // POOL_HALO - 1), 0)),
            row, vecs, row, row, wspec,
            pl.BlockSpec((1, d), lambda i: (0, 0)),
        ],
        [row, wspec, pl.BlockSpec((1, d), lambda i: (0, 0)), vecs],
        [
            jax.ShapeDtypeStruct((s, d), F32),
            jax.ShapeDtypeStruct((groups, gd, gd), BF16),
            jax.ShapeDtypeStruct((1, d), F32),
            jax.ShapeDtypeStruct((8, d), F32),
        ],
        [dxo, dxo, x, vec, p, o, w_grp, scale],
        scratch=[pltpu.VMEM((groups, gd, gd), F32)], phases=phases,
    )


def _loss_head(x, gain, target, phases=()):
    s, d = x.shape
    tm = _pick(s, (512, 256, 128))

    def body(x_ref, g_ref, t_ref, dx_ref, aux_ref):
        @pl.when(pl.program_id(0) == 0)
        def _():
            aux_ref[...] = jnp.zeros_like(aux_ref)

        xv = x_ref[...]
        rstd = _rstd(xv)
        r = xv * rstd
        gain_v = g_ref[...]
        err = r * gain_v - t_ref[...]
        aux_ref[1:2, :] += jnp.sum(err * err, axis=0, keepdims=True)
        dout = err * (1.0 / d)
        aux_ref[0:1, :] += jnp.sum(dout * r, axis=0, keepdims=True)
        dr = dout * gain_v
        dx_ref[...] = rstd * (dr - r * jnp.mean(dr * r, axis=-1, keepdims=True))

    row = pl.BlockSpec((tm, d), lambda i: (i, 0))
    return _call(
        body, "loss_head", (s // tm,),
        [row, pl.BlockSpec((1, d), lambda i: (0, 0)), row],
        [row, pl.BlockSpec((8, d), lambda i: (0, 0))],
        [jax.ShapeDtypeStruct((s, d), F32), jax.ShapeDtypeStruct((8, d), F32)], [x, gain, target], phases=phases,
    )


def _small_adam(gathered, gathered_ws, layout, smalls, chip):
    names = list(smalls)
    n = len(names)
    loss_row, _, _, n_feat = layout["loss"]

    def body(*refs):
        chip_ref, g_ref, gws_ref = refs[0], refs[1], refs[2]
        wmv = refs[3 : 3 + 3 * n]
        outs = refs[3 + 3 * n : 3 + 7 * n]
        total = refs[-1]
        total[...] = g_ref[0]
        for kdev in range(1, N_DEV):
            total[...] += g_ref[kdev]
        total_ws = gws_ref[0]
        for kdev in range(1, N_DEV):
            total_ws = total_ws + gws_ref[kdev]
        my_chip = chip_ref[0]
        for a, name in enumerate(names):
            w_ref, m_ref, v_ref = wmv[3 * a : 3 * a + 3]
            if name == "ab_w_s":
                g = total_ws
            else:
                row0, rows, col0, cols = layout[name]
                if col0 is None:
                    g = jnp.zeros((rows, cols), F32)
                    for j in range(N_CHIPS):
                        g = g + jnp.where(my_chip == j, total[row0 : row0 + rows, j * cols : (j + 1) * cols], 0.0)
                else:
                    g = total[row0 : row0 + rows, col0 : col0 + cols]
            dl, mo, vo = _adam(w_ref[...], g, m_ref[...], v_ref[...])
            outs[4 * a][...] = g
            outs[4 * a + 1][...] = dl
            outs[4 * a + 2][...] = mo
            outs[4 * a + 3][...] = vo
        refs[3 + 7 * n][...] = 0.5 * jnp.sum(total[loss_row : loss_row + 1, 0:n_feat], axis=1, keepdims=True) / n_feat

    ins = [gathered, gathered_ws]
    out_shapes = []
    for name in names:
        ins.extend(smalls[name])
        out_shapes.extend([jax.ShapeDtypeStruct(smalls[name][0].shape, F32)] * 4)
    out_shapes.append(jax.ShapeDtypeStruct((1, 1), F32))
    whole = lambda shape: pl.BlockSpec(shape, functools.partial(lambda nd, i, c: (0,) * nd, len(shape)))
    res = pl.pallas_call(
        body, name="small_adam",
        grid_spec=pltpu.PrefetchScalarGridSpec(
            num_scalar_prefetch=1, grid=(1,),
            in_specs=[whole(a.shape) for a in ins], out_specs=[whole(o.shape) for o in out_shapes],
            scratch_shapes=[pltpu.VMEM(gathered.shape[1:], F32)],
        ),
        out_shape=out_shapes,
        compiler_params=pltpu.CompilerParams(dimension_semantics=("arbitrary",), vmem_limit_bytes=VMEM_LIMIT_BYTES),
    )(chip.reshape(1).astype(jnp.int32), *ins)
    return {name: res[4 * a : 4 * a + 4] for a, name in enumerate(names)}, res[4 * n]


def _pad_rows(a, rows=8):
    extra = (-a.shape[0]) % rows
    return jnp.pad(a, ((0, extra), (0, 0))) if extra else a


def _pad_cols(a, cols):
    return jnp.pad(a, ((0, 0), (0, cols - a.shape[1]))) if a.shape[1] < cols else a


def _run(fn, *phases):
    outs, p_outs = fn(list(phases))
    for p, po in zip(phases, p_outs):
        p.then(po)
    return outs


def kernel(x, c, norm_g, w_mod, b_mod, w_ffn_in, w_ffn_out, ab_w_in, ab_norm_v, ab_w_s, ab_b_s, ab_conv_w, ab_w_out, pool_w_grp, pool_scale, final_g, loss_target, m_norm_g, m_w_mod, m_b_mod, m_w_ffn_in, m_w_ffn_out, m_ab_w_in, m_ab_norm_v, m_ab_w_s, m_ab_b_s, m_ab_conv_w, m_ab_w_out, m_pool_w_grp, m_pool_scale, m_final_g, v_norm_g, v_w_mod, v_b_mod, v_w_ffn_in, v_w_ffn_out, v_ab_w_in, v_ab_norm_v, v_ab_w_s, v_ab_b_s, v_ab_conv_w, v_ab_w_out, v_pool_w_grp, v_pool_scale, v_final_g):
    ix, iy, ic = _place()
    chip = 2 * ix + iy
    me = 4 * ix + 2 * iy + ic
    where = jnp.stack([chip, ic]).astype(jnp.int32)
    s, d = x.shape[1], x.shape[2]
    x0 = x.reshape(s, d)
    target = loss_target.reshape(s, d)
    n_layers = norm_g.shape[0]
    dq = d // N_CHIPS
    heads, chunk = ab_w_s.shape[1], ab_w_s.shape[2]
    da = ab_norm_v.shape[1]
    db = ab_conv_w.shape[2] * N_CHIPS
    f_hidden = w_ffn_out.shape[2] * N_CHIPS
    assert n_layers == 2 and da % heads == 0

    cw_pad = _pad_cols(ab_conv_w.reshape(3, db // N_CHIPS), dq)
    packed = jnp.concatenate(
        [_pad_rows(c.reshape(N_CHIPS, dq)), _pad_rows(norm_g.reshape(-1, dq)), _pad_rows(pool_scale.reshape(1, dq)), _pad_rows(cw_pad)],
        axis=0,
    )
    ncol = w_mod.shape[2]
    b_cols = lax.dynamic_slice(b_mod, (0, chip * ncol), (n_layers, ncol)).reshape(n_layers, 1, ncol)
    small = {}

    def small_gather(key, arrs):
        def then(outs):
            small[key] = outs

        return _phase_small_gather(arrs, then)

    stacks = {
        "w_ffn_in": tuple(a.reshape((-1,) + a.shape[2:]) for a in (w_ffn_in, m_w_ffn_in, v_w_ffn_in)),
        "w_ffn_out": tuple(a.reshape((-1,) + a.shape[2:]) for a in (w_ffn_out, m_w_ffn_out, v_w_ffn_out)),
        "ab_w_in": (ab_w_in, m_ab_w_in, v_ab_w_in),
        "ab_w_out": (ab_w_out, m_ab_w_out, v_ab_w_out),
        "pool_w_grp": (pool_w_grp[0], m_pool_w_grp[0], v_pool_w_grp[0]),
    }
    big_in = _Big((1, d, 2 * f_hidden), 2, 1)
    big_out = _Big((1, f_hidden, d), 1, 2)
    units = {}
    for l in range(n_layers):
        for k in range(2):
            units[f"in{l}{k}"] = (big_in, "w_ffn_in", 2 * l + k)
            units[f"out{l}{k}"] = (big_out, "w_ffn_out", 2 * l + k)
    units["abin"] = (_Big((1, d, ab_w_in.shape[2] * N_CHIPS), 2, 1), "ab_w_in", 0)
    units["about"] = (_Big((1, ab_w_out.shape[1] * N_CHIPS, d), 1, 2), "ab_w_out", 0)
    units["pool"] = (_Big((pool_w_grp.shape[1], pool_w_grp.shape[2] * N_CHIPS, pool_w_grp.shape[3]), 1, 0), "pool_w_grp", 0)
    big = {u: g for u, (g, _, _) in units.items()}

    weight = {}
    complete = set()

    def cast(u):
        g, st, b0 = units[u]

        def launch(phases):
            (weight[u],), p_outs = _cast_into_full(stacks[st][0], b0, g, where, "cast_" + u, phases)
            return None, p_outs

        return launch

    def gather_ici(*us):
        def then(outs):
            for u, o in zip(us, outs):
                weight[u] = o

        return _phase_gather_ici([weight[u] for u in us], [big[u] for u in us], then)

    def gather_sibling(*us):
        def then(outs):
            for u, o in zip(us, outs):
                weight[u] = o
                complete.add(u)

        return _phase_gather_sibling([weight[u] for u in us], [big[u] for u in us], then)

    def w_of(u):
        assert u in complete, u
        return weight[u]

    _run(cast("in00"), small_gather("inputs", [packed]))
    small_all = small["inputs"][0]
    by_chip = small_all[0::2]
    c_all = small_all[:, 0:N_CHIPS, :].reshape(N_DEV, d)
    norm_full = by_chip[:, 8 : 8 + 3 * n_layers, :].transpose(1, 0, 2).reshape(3 * n_layers, d)
    pool_scale_full = by_chip[:, 16:17, :].transpose(1, 0, 2).reshape(1, d)
    conv_full = by_chip[:, 24:27, : db // N_CHIPS].transpose(1, 0, 2).reshape(3, db)
    pieces = [("in00", "out00"), ("abin", "about"), ("in01", "out01"), ("in10", "out10", "pool"), ("in11", "out11")]
    in_flight = {}

    def start_gather(p):
        in_flight[p] = _split_start(gather_ici(*pieces[p]), f"gather_{p}_start")

    def started():
        return _after(*[flight.token for flight in in_flight.values()])

    def finish_gather(p, after, meanwhile=None):
        flight = in_flight.pop(p)
        _split_wait(flight, list(after) + list(started().ins), f"gather_{p}_wait")
        crossing = _split_start(gather_sibling(*pieces[p]), f"gather_{p}_forward")
        behind = [crossing.token]
        if p + 2 < len(pieces):
            for u in pieces[p + 2]:
                _run(cast(u), _after(crossing.token))
            start_gather(p + 2)
            behind = list(started().ins)
        if meanwhile is not None:
            behind = behind + meanwhile(_after(crossing.token))
        _split_wait(crossing, behind, f"gather_{p}_forwarded")

    mod_cols = _run(lambda phases: _mod_fwd(c_all, w_mod, b_cols, phases))[0]
    def mod_rows(outs):
        small["mod"] = outs

    _run(cast("out00"), _phase_small_exchange(mod_cols.transpose(1, 0, 2), mod_rows))
    start_gather(0)
    _run(cast("abin"), started())
    _run(cast("about"), started())
    start_gather(1)
    mod_mine = small["mod"][0][0::2]
    mod = mod_mine.transpose(1, 0, 2).reshape(n_layers, 3, 3, d)
    vecs = {
        (l, sub): jnp.pad(norm_full[3 * l + sub][None], ((0, 7), (0, 0))) + jnp.pad(mod[l, sub], ((1, 4), (0, 0)))
        for l in range(n_layers)
        for sub in range(3)
    }
    b_rows = jnp.broadcast_to(ab_b_s[0].T[:, :, None], (chunk, heads, da // heads)).reshape(chunk, da)

    saved = {}

    def ffn_forward(xs, l, sub, k, *phases):
        saved[l, sub, "x"] = xs
        xs, gg, uu, yb = _run(
            lambda ph: _ffn_fwd(xs, vecs[l, sub], w_of(f"in{l}{k}"), w_of(f"out{l}{k}"), f"ffn_fwd_{l}{k}", ph), *phases
        )
        saved[l, sub, "act"] = (gg, uu, yb)
        return xs

    finish_gather(0, [vecs[0, 0]])
    xs = ffn_forward(x0, 0, 0, 0, started())
    saved[0, 1, "x"] = xs
    finish_gather(1, [xs])
    (proj,) = _run(lambda ph: _proj_mod_fwd(xs, vecs[0, 1], w_of("abin"), ph), started())
    (cat,) = _run(lambda ph: _ab_mix_fwd(proj, ab_norm_v, ab_w_s[0], b_rows, conv_full, ph))
    xs, yb = _run(lambda ph: _proj_res_fwd(cat, w_of("about"), xs, vecs[0, 1], ph))
    saved[0, 1, "act"] = (proj, cat, yb)
    finish_gather(2, [xs])
    xs = ffn_forward(xs, 0, 2, 1, started())
    finish_gather(3, [xs])
    xs = ffn_forward(xs, 1, 0, 0, started())
    saved[1, 1, "x"] = xs
    pooled = []

    def pool_forward(behind):
        pooled.extend(_run(lambda ph: _pool_fwd(xs, vecs[1, 1], w_of("pool"), pool_scale_full, ph), behind))
        return [pooled[0]]

    finish_gather(4, [xs], pool_forward)
    xs, pp, oo = pooled
    saved[1, 1, "act"] = (pp, oo)
    xs = ffn_forward(xs, 1, 2, 1)
    dxs, aux = _run(lambda ph: _loss_head(xs, final_g.reshape(1, d), target, ph))

    grad = {}
    recv = {}
    csum = {}
    parts = {}
    reduced = {}
    done = set()
    dvecs, small_g = {}, {}

    def pair_exchange(*us):
        def then(outs):
            for u, o in zip(us, outs):
                recv[u] = o

        return _phase_pair_exchange([grad[u] for u in us], [big[u] for u in us], then)

    def grad_half(u, a, bs, mine, name, *phases):
        (res,) = _run(lambda ph: _grad_half(a, bs, big[u], where, mine, recv[u] if mine else None, name, ph), *phases)
        return res

    def pair_sum(u, *phases):
        def launch(ph):
            (csum[u],), p_outs = _pair_sum(grad[u], recv[u], big[u], where, "pair_sum_" + u, ph)
            return None, p_outs

        _run(launch, *phases)

    def chip_exchange(*us):
        def then(outs):
            for u, o in zip(us, outs):
                parts[u] = o

        return _phase_chip_exchange([csum[u] for u in us], [big[u] for u in us], then)

    def chip_sum(*us, carried=()):
        for n_u, u in enumerate(us):
            g, st, b0 = units[u]

            def launch(ph):
                (reduced[st],), p_outs = _chip_sum(
                    csum[u], parts[u], g, where, reduced.get(st), stacks[st][0].shape, b0, "chip_sum_" + u, ph
                )
                return None, p_outs

            _run(launch, *(carried if n_u == 0 else ()))

    def pair_broadcast(*us):
        sts = [units[u][1] for u in us]
        assert len(set(sts)) == len(sts)

        def then(outs):
            for u, st, o in zip(us, sts, outs):
                reduced[st] = o
                done.add(u)

        return _phase_pair_broadcast([reduced[st] for st in sts], [big[u] for u in us], [units[u][2] for u in us], then)

    def ffn_backward(dxs, l, sub, k, carried_bwd, carried_send, carried_mine):
        gg, uu, yb = saved[l, sub, "act"]
        w_in, w_out = w_of(f"in{l}{k}"), w_of(f"out{l}{k}")
        uo, ui, tag = f"out{l}{k}", f"in{l}{k}", f"{l}{k}"
        dxs, dg, du, a, h, dy, dvecs[l, sub] = _run(
            lambda ph: _ffn_bwd(dxs, saved[l, sub, "x"], vecs[l, sub], gg, uu, yb, w_in, w_out, "ffn_bwd_" + tag, ph), *carried_bwd()
        )
        grad[uo] = grad_half(uo, a, [dy], False, "dw_out_send_" + tag, *carried_send())
        grad[ui] = grad_half(ui, h, [dg, du], False, "dw_in_send_" + tag, pair_exchange(uo))
        csum[uo] = grad_half(uo, a, [dy], True, "dw_out_" + tag, pair_exchange(ui))
        csum[ui] = grad_half(ui, h, [dg, du], True, "dw_in_" + tag, *carried_mine())
        return dxs

    none = lambda: ()
    dxs = ffn_backward(dxs, 1, 2, 1, none, none, none)
    pp, oo = saved[1, 1, "act"]
    dxs, grad["pool"], small_g["pool_scale"], dvecs[1, 1] = _run(
        lambda ph: _pool_bwd(dxs, saved[1, 1, "x"], vecs[1, 1], pp, oo, w_of("pool"), pool_scale_full, ph)
    )

    def after_11():
        return (chip_exchange("in11", "out11"), pair_exchange("pool"))

    def bcast_11():
        chip_sum("in11", "out11")
        pair_sum("pool")
        return (pair_broadcast("in11", "out11"), chip_exchange("pool"))

    dxs = ffn_backward(dxs, 1, 0, 0, after_11, bcast_11, none)

    def after_10():
        return (chip_exchange("in10", "out10"),)

    def bcast_10():
        chip_sum("in10", "out10", "pool")
        return (pair_broadcast("in10", "out10", "pool"),)

    dxs = ffn_backward(dxs, 0, 2, 1, after_10, bcast_10, none)

    proj, cat, yb = saved[0, 1, "act"]
    out01 = _split_start(chip_exchange("out01"), "reduce_out01_start")
    dy, dcat, dgate = _run(lambda ph: _proj_res_bwd(dxs, yb, vecs[0, 1], w_of("about"), ph), _after(out01.token))
    grad["about"] = grad_half("about", cat, [dy], False, "dw_ab_out_send")
    dproj, small_g["ab_norm_v"], small_g["ab_w_s"], dzs, small_g["ab_conv_w"] = _run(
        lambda ph: _ab_mix_bwd(proj, dcat, ab_norm_v, ab_w_s[0], b_rows, conv_full, ph), pair_exchange("about")
    )
    small_g["ab_b_s"] = dzs.reshape(chunk, heads, da // heads).sum(axis=2).T
    dxs, h, dvecs[0, 1] = _run(
        lambda ph: _proj_mod_bwd(dproj[None], w_of("abin"), saved[0, 1, "x"], vecs[0, 1], dxs, dgate, "ab_in_bwd", ph)
    )
    grad["abin"] = grad_half("abin", h, [dproj], False, "dw_ab_in_send")
    (csum["out01"],) = _split_wait(out01, [grad["abin"]], "reduce_out01_wait")
    chip_sum("out01", carried=(pair_exchange("abin"),))
    csum["about"] = grad_half("about", cat, [dy], True, "dw_ab_out", pair_broadcast("out01"))
    csum["abin"] = grad_half("abin", h, [dproj], True, "dw_ab_in")

    layout = {}
    tail = {}

    def after_01():
        tail["01"] = _split_start(chip_exchange("in01", "abin", "about"), "reduce_01_start")
        return (_after(tail["01"].token),)

    def pack_small_grads():
        dvec_all = jnp.stack([dvecs[l, sub] for l in range(n_layers) for sub in range(3)])
        dgain = dvec_all[:, 0, :]
        dmod = dvec_all[:, 1:4, :].reshape(3 * 3 * n_layers, d)
        rows = {
            "norm_g": (dgain, None, dq), "final_g": (aux[0:1], 0, d), "pool_scale": (small_g["pool_scale"], None, dq),
            "b_mod": (dmod, 0, d), "ab_norm_v": (small_g["ab_norm_v"], 0, da),
            "ab_conv_w": (small_g["ab_conv_w"], None, db // N_CHIPS), "ab_b_s": (small_g["ab_b_s"], 0, chunk),
            "loss": (aux[1:2], 0, d),
        }
        row0 = 0
        for nm, (pc, col0, cols) in rows.items():
            layout[nm] = (row0, pc.shape[0], col0, cols)
            row0 += pc.shape[0]
        packed_rows = -(-row0 // 8) * 8
        return sum(
            jnp.pad(pc, ((layout[nm][0], packed_rows - layout[nm][0] - pc.shape[0]), (0, d - pc.shape[1])))
            for nm, (pc, _, _) in rows.items()
        )

    def bcast_01():
        csum["in01"], csum["abin"], csum["about"] = _split_wait(tail["01"], [dvecs[0, 0]], "reduce_01_wait")
        chip_sum("in01", "abin", "about")
        grads_small = [pack_small_grads(), small_g["ab_w_s"].reshape(heads * chunk, chunk)]
        tail["small"] = _split_start(small_gather("grads", grads_small), "gather_small_grads_start")
        return (pair_broadcast("in01", "abin", "about"), _after(tail["small"].token))

    def reduce_out00():
        tail["out00"] = _split_start(chip_exchange("out00"), "reduce_out00_start")
        return (_after(tail["out00"].token),)

    dxs = ffn_backward(dxs, 0, 0, 0, after_01, bcast_01, reduce_out00)
    grad_x = dxs.reshape(x.shape)

    last = _split_start(chip_exchange("in00"), "reduce_last_start")
    (csum["out00"],) = _split_wait(tail["out00"], [last.token], "reduce_out00_wait")
    chip_sum("out00")
    _flush("broadcast_out00", pair_broadcast("out00"))
    _split_wait(tail["small"], [reduced["w_ffn_out"]], "gather_small_grads_wait")
    g_all, gws_all = small["grads"]

    out = {}

    def adam_stack(st, after=()):
        w3, m3, v3 = stacks[st]
        assert all(u in done for u, (_, ust, _) in units.items() if ust == st), st
        shape = {"w_ffn_in": w_ffn_in.shape, "w_ffn_out": w_ffn_out.shape, "pool_w_grp": pool_w_grp.shape}.get(st, w3.shape)
        out[st] = tuple(a.reshape(shape) for a in _adam_stack(w3, reduced[st], m3, v3, "adam_" + st, after))

    for st in ("w_ffn_out", "ab_w_in", "ab_w_out", "pool_w_grp"):
        adam_stack(st, (last.token,))

    shapes2d = {
        "norm_g": (3 * n_layers, dq), "b_mod": (9 * n_layers, d), "final_g": (1, d), "ab_norm_v": (1, da),
        "pool_scale": (1, dq), "ab_conv_w": (3, db // N_CHIPS), "ab_b_s": (heads, chunk), "ab_w_s": (heads * chunk, chunk),
    }
    small_w = {"norm_g": (norm_g, m_norm_g, v_norm_g), "b_mod": (b_mod, m_b_mod, v_b_mod), "final_g": (final_g, m_final_g, v_final_g),
               "ab_norm_v": (ab_norm_v, m_ab_norm_v, v_ab_norm_v), "pool_scale": (pool_scale, m_pool_scale, v_pool_scale),
               "ab_conv_w": (ab_conv_w, m_ab_conv_w, v_ab_conv_w), "ab_b_s": (ab_b_s, m_ab_b_s, v_ab_b_s), "ab_w_s": (ab_w_s, m_ab_w_s, v_ab_w_s)}
    smalls = {nm: tuple(a.reshape(shapes2d[nm]) for a in wmv) for nm, wmv in small_w.items()}
    small_out, loss = _small_adam(g_all, gws_all, layout, smalls, chip)
    loss = loss.reshape(())
    for nm, res in small_out.items():
        out[nm] = tuple(a.reshape(small_w[nm][0].shape) for a in res)

    mod_row0 = layout["b_mod"][0]
    dmod_all = g_all[:, mod_row0 : mod_row0 + 9 * n_layers, :].reshape(N_DEV, n_layers, 9 * d)
    dmod_cols = lax.dynamic_slice(dmod_all, (0, 0, chip * ncol), (N_DEV, n_layers, ncol)).transpose(1, 0, 2)
    out["w_mod"] = tuple(_mod_bwd_adam(c_all.T, dmod_cols, w_mod, m_w_mod, v_w_mod, (last.token,)))

    (csum["in00"],) = _split_wait(
        last, [out[st][1] for st in ("w_mod", "w_ffn_out", "ab_w_in", "ab_w_out", "pool_w_grp")], "reduce_last_wait"
    )
    chip_sum("in00")
    _flush("broadcast_last", pair_broadcast("in00"))
    adam_stack("w_ffn_in")

    order = ["norm_g", "w_mod", "b_mod", "w_ffn_in", "w_ffn_out", "ab_w_in", "ab_norm_v", "ab_w_s", "ab_b_s", "ab_conv_w", "ab_w_out", "pool_w_grp", "pool_scale", "final_g"]
    return (loss, grad_x, *[out[nm][0] for nm in order], *[out[nm][1] for nm in order], *[out[nm][2] for nm in order], *[out[nm][3] for nm in order])
```

```python
import functools
import math

import jax
import jax.numpy as jnp
from jax import lax
from jax.experimental import pallas as pl
from jax.experimental.pallas import tpu as pltpu

F32 = jnp.float32
BF16 = jnp.bfloat16
MESH = pl.DeviceIdType.MESH

EPS = 1e-6
ADAM_LR = 0.001
ADAM_B1 = 0.9
ADAM_B2 = 0.999
ADAM_EPS = 1e-08
ADAM_WD = 0.01
ADAM_STEP = 10
POOL_WINDOWS = (2, 4, 8, 16)
POOL_HALO = 16
CONV_HALO = 8
N_CHIPS = 4
N_DEV = 8
VMEM_LIMIT_BYTES = 48 * 1024 * 1024
EW_BLOCK_ELEMS = 256 * 1024


def _pick(n, prefs):
    for p in prefs:
        if p <= n and n % p == 0:
            return p
    return n


def _row_tile(rows, cols):
    best = None
    for d in range(16, rows + 1, 16):
        if rows % d == 0 and d * cols <= EW_BLOCK_ELEMS:
            best = d
    return best or rows


def _dot(a, b):
    return jnp.dot(a, b, preferred_element_type=F32)


def _dot_nt(a, b):
    return lax.dot_general(a, b, (((1,), (1,)), ((), ())), preferred_element_type=F32)


def _dot_tn(a, b):
    return lax.dot_general(a, b, (((0,), (0,)), ((), ())), preferred_element_type=F32)


def _sigmoid(x):
    return 0.5 * jnp.tanh(0.5 * x) + 0.5


_GELU_C = math.sqrt(2.0 / math.pi)


def _gelu(x):
    x2 = x * x
    t = jnp.tanh(_GELU_C * (x + 0.044715 * x2 * x))
    val = 0.5 * x * (1.0 + t)
    grad = 0.5 * (1.0 + t) + 0.5 * x * (1.0 - t * t) * (_GELU_C * (1.0 + 3.0 * 0.044715 * x2))
    return val, grad


def _rstd(x):
    return lax.rsqrt(jnp.mean(x * x, axis=-1, keepdims=True) + EPS)


def _modulate(x, vec_ref):
    return (x * _rstd(x)) * vec_ref[0:1, :] * (1.0 + vec_ref[2:3, :]) + vec_ref[1:2, :]


def _modulate_bwd(x, dh, vec_ref, dvec_ref):
    gn, sh, sc = vec_ref[0:1, :], vec_ref[1:2, :], vec_ref[2:3, :]
    rstd = _rstd(x)
    r = x * rstd
    dvec_ref[0:1, :] += jnp.sum(dh * r * (1.0 + sc), axis=0, keepdims=True)
    dvec_ref[1:2, :] += jnp.sum(dh, axis=0, keepdims=True)
    dvec_ref[2:3, :] += jnp.sum(dh * r * gn, axis=0, keepdims=True)
    gm = gn * (1.0 + sc)
    dr = dh * gm
    dx = rstd * (dr - r * jnp.mean(dr * r, axis=-1, keepdims=True))
    return dx, r * gm + sh


def _adam(w, g, m, v):
    m = ADAM_B1 * m + (1.0 - ADAM_B1) * g
    v = ADAM_B2 * v + (1.0 - ADAM_B2) * (g * g)
    m_hat = m / (1.0 - ADAM_B1**ADAM_STEP)
    v_hat = v / (1.0 - ADAM_B2**ADAM_STEP)
    delta = -ADAM_LR * (m_hat / (jnp.sqrt(v_hat) + ADAM_EPS) + ADAM_WD * w)
    return delta, m, v


_ANY = pl.BlockSpec(memory_space=pl.ANY)


class _Phase:
    def __init__(self, ins, out_shapes, aliases, n_sems, start, finish, then):
        self.ins, self.out_shapes, self.aliases, self.n_sems = list(ins), list(out_shapes), dict(aliases), n_sems
        self.start, self.finish, self.then = start, finish, then


def _call(body, name, grid, in_specs, out_specs, out_shape, ins, scratch=(), prefetch=(), phases=(), in_place=None):
    n_pre, n_in, n_out, n_sc = len(prefetch), len(in_specs), len(out_specs), len(scratch)
    ph_in = [len(p.ins) for p in phases]
    ph_out = [len(p.out_shapes) for p in phases]

    def kernel_body(*refs):
        pos = [0]

        def take(k):
            pos[0] += k
            return refs[pos[0] - k : pos[0]]

        pre, ins_ = take(n_pre), take(n_in)
        p_ins = [take(k) for k in ph_in]
        outs_ = take(n_out)
        p_outs = [take(k) for k in ph_out]
        sc = take(n_sc)
        sems = [take(2) for _ in phases]
        if phases:
            ids = [pl.program_id(a) for a in range(len(grid))]
            first = functools.reduce(jnp.logical_and, [i == 0 for i in ids])
            last = functools.reduce(jnp.logical_and, [i == g - 1 for i, g in zip(ids, grid)])

            @pl.when(first)
            def _():
                for p, pi, po, (send, recv) in zip(phases, p_ins, p_outs, sems):
                    p.start(pi, po, send, recv)

        if body is not None:
            body(*pre, *ins_, *outs_, *sc)
        if phases:

            @pl.when(last)
            def _():
                for p, pi, po, (send, recv) in zip(phases, p_ins, p_outs, sems):
                    p.finish(pi, po, send, recv)

    aliases = {n_pre + i: o for i, o in (in_place or {}).items()}
    i0, o0 = n_pre + n_in, n_out
    for p in phases:
        for i, o in p.aliases.items():
            aliases[i0 + i] = o0 + o
        i0 += len(p.ins)
        o0 += len(p.out_shapes)
    all_in = list(in_specs) + [_ANY] * sum(ph_in)
    all_out = list(out_specs) + [_ANY] * sum(ph_out)
    all_scratch = list(scratch)
    for p in phases:
        all_scratch += [pltpu.SemaphoreType.DMA((p.n_sems,)), pltpu.SemaphoreType.DMA((p.n_sems,))]
    shapes = list(out_shape) + [s for p in phases for s in p.out_shapes]
    in_hbm = [pltpu.with_memory_space_constraint(a, pltpu.HBM) for a in ins]
    operands = list(prefetch) + in_hbm + [a for p in phases for a in p.ins]
    sem = ("arbitrary",) * len(grid)
    params = pltpu.CompilerParams(dimension_semantics=sem, vmem_limit_bytes=VMEM_LIMIT_BYTES)
    if n_pre:
        res = pl.pallas_call(
            kernel_body, name=name, out_shape=shapes, input_output_aliases=aliases, compiler_params=params,
            grid_spec=pltpu.PrefetchScalarGridSpec(
                num_scalar_prefetch=n_pre, grid=grid, in_specs=all_in, out_specs=all_out, scratch_shapes=all_scratch
            ),
        )(*operands)
    else:
        res = pl.pallas_call(
            kernel_body, name=name, grid=grid, in_specs=all_in, out_specs=all_out, out_shape=shapes,
            scratch_shapes=all_scratch, input_output_aliases=aliases, compiler_params=params,
        )(*operands)
    res = list(res)
    outs, rest = res[:n_out], res[n_out:]
    p_res = []
    for k in ph_out:
        p_res.append(rest[:k])
        rest = rest[k:]
    return outs, p_res


def _place():
    return lax.axis_index("x"), lax.axis_index("y"), lax.axis_index("c")


def _other_chips():
    x, y, _ = _place()
    return [(1 - x, y), (x, 1 - y), (1 - x, 1 - y)]


def _flip(k):
    x, y, c = _place()
    return (1 - x if k & 4 else x, 1 - y if k & 2 else y, 1 - c if k & 1 else c)


def _remote(src, dst, send, recv, k, to):
    return pltpu.make_async_remote_copy(
        src_ref=src, dst_ref=dst, send_sem=send.at[k], recv_sem=recv.at[k], device_id=to, device_id_type=MESH
    )


def _phase_small_gather(arrs, then):
    n = len(arrs)

    def copies(ins, outs, send, recv):
        x, y, c = _place()
        me = 4 * x + 2 * y + c
        local = [pltpu.make_async_copy(ins[a], outs[a].at[me], send.at[a * N_DEV]) for a in range(n)]
        remote = [_remote(ins[a], outs[a].at[me], send, recv, a * N_DEV + k, _flip(k)) for a in range(n) for k in range(1, N_DEV)]
        return local, remote

    def start(ins, outs, send, recv):
        local, remote = copies(ins, outs, send, recv)
        for cp in local + remote:
            cp.start()

    def finish(ins, outs, send, recv):
        local, remote = copies(ins, outs, send, recv)
        for cp in remote + local:
            cp.wait()

    shapes = [jax.ShapeDtypeStruct((N_DEV,) + a.shape, a.dtype) for a in arrs]
    return _Phase(arrs, shapes, {}, n * N_DEV, start, finish, then)


def _phase_small_exchange(arr, then):
    def copies(ins, outs, send, recv):
        x, y, c = _place()
        me = 4 * x + 2 * y + c
        local = pltpu.make_async_copy(ins[0].at[me], outs[0].at[me], send.at[0])
        remote = []
        for k in range(1, N_DEV):
            px, py, pc = _flip(k)
            remote.append(_remote(ins[0].at[4 * px + 2 * py + pc], outs[0].at[me], send, recv, k, (px, py, pc)))
        return [local] + remote

    def start(ins, outs, send, recv):
        for cp in copies(ins, outs, send, recv):
            cp.start()

    def finish(ins, outs, send, recv):
        for cp in copies(ins, outs, send, recv):
            cp.wait()

    return _Phase([arr], [jax.ShapeDtypeStruct(arr.shape, arr.dtype)], {}, N_DEV, start, finish, then)


def _after(*arrs):
    nothing = lambda *args: None
    return _Phase(arrs, [], {}, 1, nothing, nothing, nothing)


def _flush(name, *phases):
    _, p_outs = _call(None, name, (1,), [], [], [], [], phases=list(phases))
    for p, po in zip(phases, p_outs):
        p.then(po)


class _Big:
    KINDS = {"full": (True, True), "half": (True, False), "shard": (False, True), "block": (False, False)}

    def __init__(self, f3, s3, h3):
        assert s3 != h3
        self.f3, self.s3, self.h3 = tuple(f3), s3, h3
        self.bd = tuple(f3[a] // (N_CHIPS if a == s3 else 1) // (2 if a == h3 else 1) for a in range(3))
        self.tile = (1, _row_tile(self.bd[1], self.bd[2]), self.bd[2])
        self.grid = tuple(self.bd[a] // self.tile[a] for a in range(3))

    def dims(self, kind):
        chips, halves = self.KINDS[kind]
        return tuple(
            self.bd[a] * (N_CHIPS if chips and a == self.s3 else 1) * (2 if halves and a == self.h3 else 1) for a in range(3)
        )

    def view(self, ref, chip=None, half=None, batch0=0, both_halves=True):
        start = [batch0, 0, 0]
        size = list(ref.shape)
        size[0] = self.bd[0] * (2 if self.h3 == 0 and both_halves else 1)
        if chip is not None:
            start[self.s3] += chip * self.bd[self.s3]
            size[self.s3] = self.bd[self.s3]
        if half is not None:
            start[self.h3] += half * self.bd[self.h3]
            size[self.h3] = self.bd[self.h3]
        return ref.at[tuple(pl.ds(st, sz) for st, sz in zip(start, size))]

    def spec(self, chip_from=None, half_from=None, lead=(), batch0=0):
        extra = "grid" in (chip_from, half_from)

        def index(*args):
            pref, idx = args[-1], list(args[int(extra) : -1])
            idx[0] += batch0
            if chip_from:
                idx[self.s3] += (pref[0] if chip_from == "pref" else args[0]) * self.grid[self.s3]
            if half_from:
                idx[self.h3] += (pref[1] if half_from == "pref" else args[0]) * self.grid[self.h3]
            return (0,) * len(lead) + tuple(idx)

        return pl.BlockSpec(tuple(lead) + self.tile, index)


def _same(arrs):
    return [jax.ShapeDtypeStruct(a.shape, a.dtype) for a in arrs]


def _phase_gather_ici(arrs, bigs, then):
    n = len(arrs)

    def copies(outs, send, recv, arriving):
        x, y, c = _place()
        return [
            _remote(blk, blk, send, recv, 3 * a + j, (*chip, c))
            for j, chip in enumerate(_other_chips())
            for a in range(n)
            for blk in [bigs[a].view(outs[a], 2 * chip[0] + chip[1] if arriving else 2 * x + y, c)]
        ]

    def start(ins, outs, send, recv):
        for cp in copies(outs, send, recv, False):
            cp.start()

    def finish(ins, outs, send, recv):
        for cp in copies(outs, send, recv, True):
            cp.wait_recv()
        for cp in copies(outs, send, recv, False):
            cp.wait_send()

    return _Phase(arrs, _same(arrs), {a: a for a in range(n)}, 3 * n, start, finish, then)


def _phase_gather_sibling(arrs, bigs, then):
    n = len(arrs)

    def copies(outs, send, recv, arriving):
        x, y, c = _place()
        return [
            _remote(blk, blk, send, recv, 3 * a + j, (x, y, 1 - c))
            for j, chip in enumerate(_other_chips())
            for a in range(n)
            for blk in [bigs[a].view(outs[a], 2 * chip[0] + chip[1], 1 - c if arriving else c)]
        ]

    def start(ins, outs, send, recv):
        for cp in copies(outs, send, recv, False):
            cp.start()

    def finish(ins, outs, send, recv):
        for cp in copies(outs, send, recv, True):
            cp.wait_recv()
        for cp in copies(outs, send, recv, False):
            cp.wait_send()

    return _Phase(arrs, _same(arrs), {a: a for a in range(n)}, 3 * n, start, finish, then)


def _phase_pair_exchange(grads, bigs, then):
    n = len(grads)

    def copies(ins, outs, send, recv):
        x, y, c = _place()
        srcs = [ins[a] if ins[a].shape == outs[a].shape else bigs[a].view(ins[a], None, 1 - c) for a in range(n)]
        return [_remote(srcs[a], outs[a], send, recv, a, (x, y, 1 - c)) for a in range(n)]

    def start(ins, outs, send, recv):
        for cp in copies(ins, outs, send, recv):
            cp.start()

    def finish(ins, outs, send, recv):
        for cp in copies(ins, outs, send, recv):
            cp.wait()

    shapes = [jax.ShapeDtypeStruct(b.dims("half"), BF16) for b in bigs]
    return _Phase(grads, shapes, {}, n, start, finish, then)


def _phase_chip_exchange(sums, bigs, then):
    n = len(sums)

    def copies(ins, outs, send, recv):
        _, _, c = _place()
        return [
            _remote(bigs[a].view(ins[a], 2 * chip[0] + chip[1], both_halves=False), outs[a].at[j], send, recv, 3 * a + j, (*chip, c))
            for j, chip in enumerate(_other_chips())
            for a in range(n)
        ]

    def start(ins, outs, send, recv):
        for cp in copies(ins, outs, send, recv):
            cp.start()

    def finish(ins, outs, send, recv):
        for cp in copies(ins, outs, send, recv):
            cp.wait()

    shapes = [jax.ShapeDtypeStruct((N_CHIPS - 1,) + b.dims("block"), BF16) for b in bigs]
    return _Phase(sums, shapes, {}, 3 * n, start, finish, then)


_HBM = pl.BlockSpec(memory_space=pltpu.HBM)
_SEM = pl.BlockSpec(memory_space=pltpu.SEMAPHORE)
_DATAFLOW = pltpu.SideEffectType.DATAFLOW_SIDE_EFFECTING


class _InFlight:
    def __init__(self, phase, send, recv, arrays, token):
        self.phase, self.send, self.recv, self.arrays, self.token = phase, send, recv, arrays, token


def _phase_results(phase, refs):
    n_in = len(phase.ins)
    updated = {o: i for i, o in phase.aliases.items()}
    fresh = [o for o in range(len(phase.out_shapes)) if o not in updated]
    return [refs[updated[o]] if o in updated else refs[n_in + fresh.index(o)] for o in range(len(phase.out_shapes))]


def _split_start(phase, name):
    n_in = len(phase.ins)
    fresh = [s for o, s in enumerate(phase.out_shapes) if o not in phase.aliases.values()]
    arrays = list(phase.ins) + [lax.empty(s.shape, s.dtype) for s in fresh]
    n = len(arrays)

    def body(*refs):
        phase.start(refs[:n_in], _phase_results(phase, refs[:n]), refs[n], refs[n + 1])
        refs[-1][...] = jnp.zeros_like(refs[-1])

    operands = [pltpu.with_memory_space_constraint(a, pltpu.HBM) for a in arrays]
    res = pl.pallas_call(
        body, name=name,
        out_shape=[pltpu.SemaphoreType.DMA((phase.n_sems,)), pltpu.SemaphoreType.DMA((phase.n_sems,))]
        + [pltpu.HBM(a.shape, a.dtype) for a in arrays] + [jax.ShapeDtypeStruct((8, 128), F32)],
        in_specs=[_HBM] * n, out_specs=[_SEM, _SEM] + [_HBM] * n + [pl.BlockSpec(memory_space=pltpu.VMEM)],
        input_output_aliases={i: 2 + i for i in range(n)},
        compiler_params=pltpu.CompilerParams(has_side_effects=_DATAFLOW),
    )(*operands)
    return _InFlight(phase, res[0], res[1], list(res[2 : 2 + n]), res[-1])


def _split_wait(flight, after, name):
    phase, n = flight.phase, len(flight.arrays)
    n_in = len(phase.ins)

    def body(*refs):
        phase.finish(refs[:n_in], _phase_results(phase, refs[:n]), refs[n], refs[n + 1])

    res = pl.pallas_call(
        body, name=name, out_shape=[pltpu.HBM(a.shape, a.dtype) for a in flight.arrays],
        in_specs=[_HBM] * n + [_SEM, _SEM] + [_ANY] * len(after), out_specs=[_HBM] * n,
        input_output_aliases={i: i for i in range(n)},
        compiler_params=pltpu.CompilerParams(has_side_effects=_DATAFLOW),
    )(*flight.arrays, flight.send, flight.recv, *after)
    res = list(res)
    phase.then(_phase_results(phase, res))
    return res[:n_in]


def _phase_pair_broadcast(stacks, bigs, batch0s, then):
    n = len(stacks)

    def start(ins, outs, send, recv):
        x, y, c = _place()
        for a in range(n):
            blk = bigs[a].view(outs[a], None, c, batch0s[a])
            _remote(blk, blk, send, recv, a, (x, y, 1 - c)).start()

    def finish(ins, outs, send, recv):
        x, y, c = _place()
        for a in range(n):
            mine = bigs[a].view(outs[a], None, c, batch0s[a])
            theirs = bigs[a].view(outs[a], None, 1 - c, batch0s[a])
            _remote(mine, mine, send, recv, a, (x, y, 1 - c)).wait_send()
            _remote(theirs, theirs, send, recv, a, (x, y, 1 - c)).wait_recv()

    return _Phase(stacks, _same(stacks), {a: a for a in range(n)}, n, start, finish, then)


def _tile_call(body, name, big, where, extra, ins, in_specs, out_specs, out_shape, phases=()):
    grid = ((extra,) if extra else ()) + big.grid
    return _call(body, name, grid, in_specs, out_specs, out_shape, ins, prefetch=(where,), phases=phases)


def _cast_into_full(w_stack, batch0, big, where, name, phases=()):
    def body(_, w_ref, o_ref):
        o_ref[...] = w_ref[...].astype(BF16)

    return _tile_call(
        body, name, big, where, 2, [w_stack], [big.spec(None, "grid", batch0=batch0)], [big.spec("pref", "grid")],
        [jax.ShapeDtypeStruct(big.dims("full"), BF16)], phases,
    )


def _pair_sum(g_full, recv_half, big, where, name, phases=()):
    def body(_, g_ref, r_ref, o_ref):
        o_ref[...] = (g_ref[...].astype(F32) + r_ref[...].astype(F32)).astype(BF16)

    half = big.spec("grid", None)
    return _tile_call(
        body, name, big, where, N_CHIPS, [g_full, recv_half], [big.spec("grid", "pref"), half], [half],
        [jax.ShapeDtypeStruct(big.dims("half"), BF16)], phases,
    )


def _chip_sum(chip_sum, parts, big, where, stack, stack_shape, batch0, name, phases=()):
    def body(_, own_ref, p_ref, *rest):
        acc = own_ref[...].astype(F32)
        for k in range(N_CHIPS - 1):
            acc = acc + p_ref[k].astype(F32)
        rest[-1][...] = acc

    ins = [chip_sum, parts] + ([stack] if stack is not None else [])
    in_specs = [big.spec("pref", None), big.spec(None, None, lead=(N_CHIPS - 1,))] + ([_ANY] if stack is not None else [])
    return _call(
        body, name, big.grid, in_specs, [big.spec(None, "pref", batch0=batch0)], [jax.ShapeDtypeStruct(stack_shape, F32)], ins,
        prefetch=(where,), phases=phases, in_place={2: 0} if stack is not None else None,
    )


def _adam_stack(w, g, m, v, name, after=()):
    b, r, c = w.shape
    tr = _row_tile(r, c)

    def body(w_ref, g_ref, m_ref, v_ref, *rest):
        go_ref, d_ref, mo_ref, vo_ref = rest[-4:]
        gv = g_ref[...]
        d, mo, vo = _adam(w_ref[...], gv, m_ref[...], v_ref[...])
        go_ref[...] = gv
        d_ref[...] = d
        mo_ref[...] = mo
        vo_ref[...] = vo

    spec = pl.BlockSpec((1, tr, c), lambda bb, i: (bb, i, 0))
    outs, _ = _call(
        body, name, (b, r // tr), [spec] * 4 + [_ANY] * len(after), [spec] * 4, [jax.ShapeDtypeStruct(w.shape, F32)] * 4,
        [w, g, m, v, *after],
    )
    return outs


def _mod_fwd(c_all, w_mod, b_cols, phases=()):
    n_layers, d, n = w_mod.shape
    tn = _pick(n, (768, 512, 384, 256, 128))

    def body(c_ref, w_ref, b_ref, o_ref):
        cv = c_ref[...]
        ca = (cv * _sigmoid(cv)).astype(BF16)
        o_ref[0] = _dot(ca, w_ref[0].astype(BF16)) + b_ref[0]

    return _call(
        body, "mod_fwd", (n_layers, n // tn),
        [
            pl.BlockSpec((N_DEV, d), lambda l, j: (0, 0)),
            pl.BlockSpec((1, d, tn), lambda l, j: (l, 0, j)),
            pl.BlockSpec((1, 1, tn), lambda l, j: (l, 0, j)),
        ],
        [pl.BlockSpec((1, N_DEV, tn), lambda l, j: (l, 0, j))],
        [jax.ShapeDtypeStruct((n_layers, N_DEV, n), F32)], [c_all, w_mod, b_cols], phases=phases,
    )


def _mod_bwd_adam(c_all_t, dmod_cols, w, m, v, after=()):
    n_layers, d, n = w.shape
    tn = _pick(n, (384, 256, 128))

    def body(c_ref, dm_ref, w_ref, m_ref, v_ref, *rest):
        g_ref, d_ref, mo_ref, vo_ref = rest[-4:]
        cv = c_ref[...]
        ca = (cv * _sigmoid(cv)).astype(BF16)
        g = _dot(ca, dm_ref[0].astype(BF16))
        g_ref[0] = g
        dl, mo, vo = _adam(w_ref[0], g, m_ref[0], v_ref[0])
        d_ref[0] = dl
        mo_ref[0] = mo
        vo_ref[0] = vo

    wspec = pl.BlockSpec((1, d, tn), lambda l, j: (l, 0, j))
    outs, _ = _call(
        body, "mod_bwd_adam", (n_layers, n // tn),
        [pl.BlockSpec((d, N_DEV), lambda l, j: (0, 0)), pl.BlockSpec((1, N_DEV, tn), lambda l, j: (l, 0, j)), wspec, wspec, wspec]
        + [_ANY] * len(after),
        [wspec] * 4, [jax.ShapeDtypeStruct(w.shape, F32)] * 4, [c_all_t, dmod_cols, w, m, v, *after],
    )
    return outs


def _ffn_fwd(x, vec, w_in, w_out, name, phases=()):
    s, d = x.shape
    f = w_out.shape[1]
    tm = _pick(s, (1024, 512, 256, 128))
    tf = _pick(f, (256, 128))
    nf = f // tf

    def body(x_ref, vec_ref, wg_ref, wu_ref, wo_ref, xo_ref, g_ref, u_ref, y_ref, h_sc, acc_sc):
        j = pl.program_id(1)

        @pl.when(j == 0)
        def _():
            h_sc[...] = _modulate(x_ref[...], vec_ref).astype(BF16)
            acc_sc[...] = jnp.zeros_like(acc_sc)

        h = h_sc[...]
        g = _dot(h, wg_ref[0])
        u = _dot(h, wu_ref[0])
        g_ref[...] = g.astype(BF16)
        u_ref[...] = u.astype(BF16)
        a = (g * _sigmoid(g) * u).astype(BF16)
        acc_sc[...] += _dot(a, wo_ref[0])

        @pl.when(j == nf - 1)
        def _():
            yv = acc_sc[...]
            xo_ref[...] = x_ref[...] + 0.5 * vec_ref[3:4, :] * yv
            y_ref[...] = yv.astype(BF16)

    row = pl.BlockSpec((tm, d), lambda i, j: (i, 0))
    hid = pl.BlockSpec((tm, tf), lambda i, j: (i, j))
    return _call(
        body, name, (s // tm, nf),
        [
            row,
            pl.BlockSpec((8, d), lambda i, j: (0, 0)),
            pl.BlockSpec((1, d, tf), lambda i, j: (0, 0, j)),
            pl.BlockSpec((1, d, tf), lambda i, j: (0, 0, nf + j)),
            pl.BlockSpec((1, tf, d), lambda i, j: (0, j, 0)),
        ],
        [row, hid, hid, row],
        [
            jax.ShapeDtypeStruct((s, d), F32),
            jax.ShapeDtypeStruct((s, f), BF16),
            jax.ShapeDtypeStruct((s, f), BF16),
            jax.ShapeDtypeStruct((s, d), BF16),
        ],
        [x, vec, w_in, w_in, w_out],
        scratch=[pltpu.VMEM((tm, d), BF16), pltpu.VMEM((tm, d), F32)], phases=phases,
    )


def _ffn_bwd(dxo, x, vec, gg, uu, y, w_in, w_out, name, phases=()):
    s, d = x.shape
    f = w_out.shape[1]
    tm = _pick(s, (512, 256, 128))
    tf = _pick(f, (256, 128))
    nf = f // tf

    def body(dxo_ref, x_ref, vec_ref, g_ref, u_ref, y_ref, wg_ref, wu_ref, wo_ref,
             dx_ref, dg_ref, du_ref, a_ref, h_ref, dy_ref, dvec_ref, acc_sc):
        i, j = pl.program_id(0), pl.program_id(1)

        @pl.when((i == 0) & (j == 0))
        def _():
            dvec_ref[...] = jnp.zeros_like(dvec_ref)

        @pl.when(j == 0)
        def _():
            dxo_v = dxo_ref[...]
            dy_ref[...] = (0.5 * vec_ref[3:4, :] * dxo_v).astype(BF16)
            dvec_ref[3:4, :] += 0.5 * jnp.sum(dxo_v * y_ref[...].astype(F32), axis=0, keepdims=True)
            acc_sc[...] = jnp.zeros_like(acc_sc)

        da = _dot_nt(dy_ref[...], wo_ref[0])
        g = g_ref[...].astype(F32)
        u = u_ref[...].astype(F32)
        sig = _sigmoid(g)
        sl = g * sig
        a_ref[...] = (sl * u).astype(BF16)
        dg = (da * u * (sig * (1.0 + g * (1.0 - sig)))).astype(BF16)
        du = (da * sl).astype(BF16)
        dg_ref[...] = dg
        du_ref[...] = du
        acc_sc[...] += _dot_nt(dg, wg_ref[0]) + _dot_nt(du, wu_ref[0])

        @pl.when(j == nf - 1)
        def _():
            dx, h = _modulate_bwd(x_ref[...], acc_sc[...], vec_ref, dvec_ref)
            dx_ref[...] = dxo_ref[...] + dx
            h_ref[...] = h.astype(BF16)

    row = pl.BlockSpec((tm, d), lambda i, j: (i, 0))
    hid = pl.BlockSpec((tm, tf), lambda i, j: (i, j))
    vecs = pl.BlockSpec((8, d), lambda i, j: (0, 0))
    return _call(
        body, name, (s // tm, nf),
        [
            row, row, vecs, hid, hid, row,
            pl.BlockSpec((1, d, tf), lambda i, j: (0, 0, j)),
            pl.BlockSpec((1, d, tf), lambda i, j: (0, 0, nf + j)),
            pl.BlockSpec((1, tf, d), lambda i, j: (0, j, 0)),
        ],
        [row, hid, hid, hid, row, row, vecs],
        [
            jax.ShapeDtypeStruct((s, d), F32),
            jax.ShapeDtypeStruct((s, f), BF16),
            jax.ShapeDtypeStruct((s, f), BF16),
            jax.ShapeDtypeStruct((s, f), BF16),
            jax.ShapeDtypeStruct((s, d), BF16),
            jax.ShapeDtypeStruct((s, d), BF16),
            jax.ShapeDtypeStruct((8, d), F32),
        ],
        [dxo, x, vec, gg, uu, y, w_in, w_in, w_out],
        scratch=[pltpu.VMEM((tm, d), F32)], phases=phases,
    )


def _grad_half(a, bs, big, where, mine, recv, name, phases=()):
    s, k1 = a.shape
    n = bs[0].shape[1]
    groups = len(bs)
    rows_halved = big.h3 == 1
    assert rows_halved or groups == 1
    kk, nn = (k1 // 2, n) if rows_halved else (k1, n // 2)
    tk = _pick(kk, (1408, 1024, 512, 256, 128))
    tn = _pick(nn, (1408, 1024, 640, 512, 256, 128))
    nkb, nnb = kk // tk, nn // tn
    assert (recv is None) == (not mine)

    def half(pref):
        return pref[1] if mine else 1 - pref[1]

    def body(_, a_ref, *rest):
        q = pl.program_id(1)
        for p in range(groups):

            @pl.when(q == p)
            def _(p=p):
                acc = _dot_tn(a_ref[...], rest[p][...])
                if recv is not None:
                    acc = acc + rest[groups][0].astype(F32)
                rest[-1][0] = acc.astype(BF16)

    def b_block(p):
        def index(i, q, j, pref):
            jj = jnp.where(q == p, j, jnp.where(q < p, 0, nnb - 1))
            return (0, jj + (0 if rows_halved else half(pref) * nnb))

        return pl.BlockSpec((s, tn), index)

    out_spec = pl.BlockSpec((1, tk, tn), lambda i, q, j, pref: (0, i, q * nnb + j))
    in_specs = [pl.BlockSpec((s, tk), lambda i, q, j, pref: (0, i + (half(pref) * nkb if rows_halved else 0)))]
    in_specs += [b_block(p) for p in range(groups)]
    ins = [a, *bs]
    if recv is not None:
        in_specs.append(out_spec)
        ins.append(recv)
    return _call(
        body, name, (nkb, groups, nnb), in_specs, [out_spec], [jax.ShapeDtypeStruct(big.dims("half"), BF16)], ins,
        prefetch=(where,), phases=phases,
    )


def _proj_mod_fwd(x, vec, w, phases=()):
    s, d = x.shape
    n = w.shape[2]
    tm = _pick(s, (512, 256, 128))
    tn = _pick(n, (640, 512, 256, 128))

    def body(x_ref, vec_ref, w_ref, o_ref, h_sc):
        @pl.when(pl.program_id(1) == 0)
        def _():
            h_sc[...] = _modulate(x_ref[...], vec_ref).astype(BF16)

        o_ref[...] = _dot(h_sc[...], w_ref[0])

    return _call(
        body, "ab_in_fwd", (s // tm, n // tn),
        [
            pl.BlockSpec((tm, d), lambda i, j: (i, 0)),
            pl.BlockSpec((8, d), lambda i, j: (0, 0)),
            pl.BlockSpec((1, d, tn), lambda i, j: (0, 0, j)),
        ],
        [pl.BlockSpec((tm, tn), lambda i, j: (i, j))],
        [jax.ShapeDtypeStruct((s, n), F32)], [x, vec, w],
        scratch=[pltpu.VMEM((tm, d), BF16)], phases=phases,
    )


def _proj_res_fwd(a, w, x, vec, phases=()):
    s, kd = a.shape
    d = x.shape[1]
    tm = _pick(s, (512, 256, 128))

    def body(a_ref, w_ref, x_ref, vec_ref, xo_ref, y_ref):
        yv = _dot(a_ref[...], w_ref[0])
        xo_ref[...] = x_ref[...] + vec_ref[3:4, :] * yv
        y_ref[...] = yv.astype(BF16)

    row = pl.BlockSpec((tm, d), lambda i: (i, 0))
    return _call(
        body, "ab_out_fwd", (s // tm,),
        [pl.BlockSpec((tm, kd), lambda i: (i, 0)), pl.BlockSpec((1, kd, d), lambda i: (0, 0, 0)), row, pl.BlockSpec((8, d), lambda i: (0, 0))],
        [row, row],
        [jax.ShapeDtypeStruct((s, d), F32), jax.ShapeDtypeStruct((s, d), BF16)], [a, w, x, vec], phases=phases,
    )


def _proj_res_bwd(dxo, y, vec, w, phases=()):
    s, d = dxo.shape
    kd = w.shape[1]
    tm = _pick(s, (512, 256, 128))

    def body(dxo_ref, y_ref, vec_ref, w_ref, dy_ref, da_ref, dgate_ref):
        @pl.when(pl.program_id(0) == 0)
        def _():
            dgate_ref[...] = jnp.zeros_like(dgate_ref)

        dxo_v = dxo_ref[...]
        dy = (vec_ref[3:4, :] * dxo_v).astype(BF16)
        dy_ref[...] = dy
        dgate_ref[3:4, :] += jnp.sum(dxo_v * y_ref[...].astype(F32), axis=0, keepdims=True)
        da_ref[...] = _dot_nt(dy, w_ref[0]).astype(BF16)

    row = pl.BlockSpec((tm, d), lambda i: (i, 0))
    vecs = pl.BlockSpec((8, d), lambda i: (0, 0))
    return _call(
        body, "ab_out_bwd", (s // tm,),
        [row, row, vecs, pl.BlockSpec((1, kd, d), lambda i: (0, 0, 0))],
        [row, pl.BlockSpec((tm, kd), lambda i: (i, 0)), vecs],
        [jax.ShapeDtypeStruct((s, d), BF16), jax.ShapeDtypeStruct((s, kd), BF16), jax.ShapeDtypeStruct((8, d), F32)],
        [dxo, y, vec, w], phases=phases,
    )


def _proj_mod_bwd(dproj, w, x, vec, dxo, dvec_in, name, phases=()):
    parts, s, n_part = dproj.shape
    d = x.shape[1]
    tm = _pick(s, (512, 256, 128))
    tk = _pick(n_part, (1408, 1280, 1024, 512, 256, 128))
    per_part = n_part // tk
    nk = parts * per_part

    def body(dp_ref, w_ref, x_ref, vec_ref, dxo_ref, dvi_ref, dx_ref, h_ref, dvec_ref, acc_sc):
        i, k = pl.program_id(0), pl.program_id(1)

        @pl.when((i == 0) & (k == 0))
        def _():
            dvec_ref[...] = dvi_ref[...]

        @pl.when(k == 0)
        def _():
            acc_sc[...] = jnp.zeros_like(acc_sc)

        acc_sc[...] += _dot_nt(dp_ref[0], w_ref[0])

        @pl.when(k == nk - 1)
        def _():
            dx, h = _modulate_bwd(x_ref[...], acc_sc[...], vec_ref, dvec_ref)
            dx_ref[...] = dxo_ref[...] + dx
            h_ref[...] = h.astype(BF16)

    row = pl.BlockSpec((tm, d), lambda i, k: (i, 0))
    vecs = pl.BlockSpec((8, d), lambda i, k: (0, 0))
    return _call(
        body, name, (s // tm, nk),
        [
            pl.BlockSpec((1, tm, tk), lambda i, k: (k // per_part, i, k % per_part)),
            pl.BlockSpec((1, d, tk), lambda i, k: (0, 0, k)),
            row, vecs, row, vecs,
        ],
        [row, row, vecs],
        [jax.ShapeDtypeStruct((s, d), F32), jax.ShapeDtypeStruct((s, d), BF16), jax.ShapeDtypeStruct((8, d), F32)],
        [dproj, w, x, vec, dxo, dvec_in], scratch=[pltpu.VMEM((tm, d), F32)], phases=phases,
    )


def _tril(n):
    return lax.broadcasted_iota(jnp.int32, (n, n), 0) >= lax.broadcasted_iota(jnp.int32, (n, n), 1)


def _layernorm_stats(gv):
    mu = jnp.mean(gv, axis=-1, keepdims=True)
    cen = gv - mu
    rstd = lax.rsqrt(jnp.mean(cen * cen, axis=-1, keepdims=True) + EPS)
    return cen * rstd, rstd


def _shift_down(q, k, above_ref, c_cg, c_xb, first):
    width = q.shape[1]
    rows = lax.broadcasted_iota(jnp.int32, q.shape, 0)
    out = pltpu.roll(q, k, 0)
    for r in range(k):
        src = CONV_HALO - k + r
        above = above_ref[src : src + 1, c_cg : c_cg + width] * above_ref[src : src + 1, c_xb : c_xb + width]
        above = jnp.where(first, 0.0, above)
        out = jnp.where(rows == r, above, out)
    return out


def _ab_mix_fwd(proj, norm_v, w_s, b_rows, conv_w, phases=()):
    s, n = proj.shape
    heads, chunk, _ = w_s.shape
    da = norm_v.shape[1]
    hd = da // heads
    db = conv_w.shape[1]
    tm = _pick(s, (512, 256, 128))

    def body(p_ref, ph_ref, nv_ref, ws_ref, b_ref, cw_ref, o_ref):
        first = pl.program_id(0) == 0
        gu, _ = _gelu(p_ref[:, 0:da])
        gv, _ = _gelu(p_ref[:, da : 2 * da])
        xhat, _ = _layernorm_stats(gv)
        vn = (xhat * nv_ref[...]).astype(BF16)
        mask = _tril(chunk)
        for hh in range(heads):
            wm = jnp.where(mask, ws_ref[hh], 0.0).astype(BF16)
            cols = slice(hh * hd, (hh + 1) * hd)
            for nn in range(tm // chunk):
                rows = slice(nn * chunk, (nn + 1) * chunk)
                z = _dot(wm, vn[rows, cols]) + b_ref[:, cols]
                o_ref[rows, cols] = (gu[rows, cols] * z).astype(BF16)
        c_cg, c_xb = 2 * da + db, 2 * da + 2 * db
        bg = p_ref[:, 2 * da : 2 * da + db]
        q = p_ref[:, c_cg : c_cg + db] * p_ref[:, c_xb : c_xb + db]
        q1 = _shift_down(q, 1, ph_ref, c_cg, c_xb, first)
        q2 = _shift_down(q, 2, ph_ref, c_cg, c_xb, first)
        conv = cw_ref[0:1, :] * q2 + cw_ref[1:2, :] * q1 + cw_ref[2:3, :] * q
        o_ref[:, da : da + db] = (bg * conv).astype(BF16)

    nh = tm // CONV_HALO
    return _call(
        body, "ab_mix_fwd", (s // tm,),
        [
            pl.BlockSpec((tm, n), lambda i: (i, 0)),
            pl.BlockSpec((CONV_HALO, n), lambda i: (jnp.maximum(i * nh - 1, 0), 0)),
            pl.BlockSpec((1, da), lambda i: (0, 0)),
            pl.BlockSpec((heads, chunk, chunk), lambda i: (0, 0, 0)),
            pl.BlockSpec((chunk, da), lambda i: (0, 0)),
            pl.BlockSpec((3, db), lambda i: (0, 0)),
        ],
        [pl.BlockSpec((tm, da + db), lambda i: (i, 0))],
        [jax.ShapeDtypeStruct((s, da + db), BF16)], [proj, proj, norm_v, w_s, b_rows, conv_w], phases=phases,
    )


def _ab_mix_bwd(proj, dcat, norm_v, w_s, b_rows, conv_w, phases=()):
    s, n = proj.shape
    heads, chunk, _ = w_s.shape
    da = norm_v.shape[1]
    hd = da // heads
    db = conv_w.shape[1]
    tm = _pick(s, (512, 256, 128))
    nblk = s // tm
    dhalo = 2 * CONV_HALO

    def body(p_ref, pa_ref, pb_ref, dc_ref, dcb_ref, nv_ref, ws_ref, b_ref, cw_ref,
             dp_ref, dnv_ref, dws_ref, dzs_ref, dcw_ref, dvn_sc):
        i = pl.program_id(0)
        first, last = i == 0, i == nblk - 1

        @pl.when(first)
        def _():
            dnv_ref[...] = jnp.zeros_like(dnv_ref)
            dws_ref[...] = jnp.zeros_like(dws_ref)
            dzs_ref[...] = jnp.zeros_like(dzs_ref)
            dcw_ref[...] = jnp.zeros_like(dcw_ref)

        uu = p_ref[:, 0:da]
        gu, gu_grad = _gelu(uu)
        gv, gv_grad = _gelu(p_ref[:, da : 2 * da])
        xhat, rstd = _layernorm_stats(gv)
        nv = nv_ref[...]
        vn = (xhat * nv).astype(BF16)
        dya = dc_ref[:, 0:da].astype(F32)
        dz = (dya * gu).astype(BF16)
        mask = _tril(chunk)
        for hh in range(heads):
            wm = jnp.where(mask, ws_ref[hh], 0.0).astype(BF16)
            cols = slice(hh * hd, (hh + 1) * hd)
            dws = jnp.zeros((chunk, chunk), F32)
            for nn in range(tm // chunk):
                rows = slice(nn * chunk, (nn + 1) * chunk)
                z = _dot(wm, vn[rows, cols]) + b_ref[:, cols]
                dp_ref[rows, cols] = (dya[rows, cols] * z * gu_grad[rows, cols]).astype(BF16)
                dz_blk = dz[rows, cols]
                dws = dws + _dot_nt(dz_blk, vn[rows, cols])
                dzs_ref[:, cols] += dz_blk.astype(F32)
                dvn = _dot_tn(wm, dz_blk)
                dnv_ref[:, cols] += jnp.sum(dvn * xhat[rows, cols], axis=0, keepdims=True)
                dvn_sc[rows, cols] = dvn
            dws_ref[hh] += jnp.where(mask, dws, 0.0)
        dxhat = dvn_sc[...] * nv
        dgv = rstd * (dxhat - jnp.mean(dxhat, axis=-1, keepdims=True) - xhat * jnp.mean(dxhat * xhat, axis=-1, keepdims=True))
        dp_ref[:, da : 2 * da] = (dgv * gv_grad).astype(BF16)

        c_bg, c_cg, c_xb = 2 * da, 2 * da + db, 2 * da + 2 * db
        bg = p_ref[:, c_bg : c_bg + db]
        cg = p_ref[:, c_cg : c_cg + db]
        xb = p_ref[:, c_xb : c_xb + db]
        q = cg * xb
        q1 = _shift_down(q, 1, pa_ref, c_cg, c_xb, first)
        q2 = _shift_down(q, 2, pa_ref, c_cg, c_xb, first)
        dyb = dc_ref[:, da : da + db].astype(F32)
        conv = cw_ref[0:1, :] * q2 + cw_ref[1:2, :] * q1 + cw_ref[2:3, :] * q
        dp_ref[:, c_bg : c_bg + db] = (dyb * conv).astype(BF16)
        e = dyb * bg
        dcw_ref[0:1, :] += jnp.sum(e * q2, axis=0, keepdims=True)
        dcw_ref[1:2, :] += jnp.sum(e * q1, axis=0, keepdims=True)
        dcw_ref[2:3, :] += jnp.sum(e * q, axis=0, keepdims=True)
        rows = lax.broadcasted_iota(jnp.int32, e.shape, 0)
        dq = cw_ref[2:3, :] * e
        for kk in (1, 2):
            ek = pltpu.roll(e, tm - kk, 0)
            for r in range(kk):
                below = dcb_ref[r : r + 1, da : da + db].astype(F32) * pb_ref[r : r + 1, c_bg : c_bg + db]
                below = jnp.where(last, 0.0, below)
                ek = jnp.where(rows == tm - kk + r, below, ek)
            dq = dq + cw_ref[2 - kk : 3 - kk, :] * ek
        dp_ref[:, c_cg : c_cg + db] = (dq * xb).astype(BF16)
        dp_ref[:, c_xb : c_xb + db] = (dq * cg).astype(BF16)

    nh = tm // CONV_HALO
    nhb = tm // dhalo
    const2 = lambda i: (0, 0)
    return _call(
        body, "ab_mix_bwd", (nblk,),
        [
            pl.BlockSpec((tm, n), lambda i: (i, 0)),
            pl.BlockSpec((CONV_HALO, n), lambda i: (jnp.maximum(i * nh - 1, 0), 0)),
            pl.BlockSpec((CONV_HALO, n), lambda i: (jnp.minimum((i + 1) * nh, s // CONV_HALO - 1), 0)),
            pl.BlockSpec((tm, da + db), lambda i: (i, 0)),
            pl.BlockSpec((dhalo, da + db), lambda i: (jnp.minimum((i + 1) * nhb, s // dhalo - 1), 0)),
            pl.BlockSpec((1, da), const2),
            pl.BlockSpec((heads, chunk, chunk), lambda i: (0, 0, 0)),
            pl.BlockSpec((chunk, da), const2),
            pl.BlockSpec((3, db), const2),
        ],
        [
            pl.BlockSpec((tm, n), lambda i: (i, 0)),
            pl.BlockSpec((1, da), const2),
            pl.BlockSpec((heads, chunk, chunk), lambda i: (0, 0, 0)),
            pl.BlockSpec((chunk, da), const2),
            pl.BlockSpec((3, db), const2),
        ],
        [
            jax.ShapeDtypeStruct((s, n), BF16),
            jax.ShapeDtypeStruct((1, da), F32),
            jax.ShapeDtypeStruct((heads, chunk, chunk), F32),
            jax.ShapeDtypeStruct((chunk, da), F32),
            jax.ShapeDtypeStruct((3, db), F32),
        ],
        [proj, proj, proj, dcat, dcat, norm_v, w_s, b_rows, conv_w],
        scratch=[pltpu.VMEM((tm, da), F32)], phases=phases,
    )


def _pool_counts(tm, i, w):
    t = i * tm + lax.broadcasted_iota(jnp.int32, (tm, 1), 0)
    return jnp.minimum(t + 1, w).astype(F32)


def _pool_fwd(x, vec, w_grp, scale, phases=()):
    s, d = x.shape
    groups, gd, _ = w_grp.shape
    tm = _pick(s, (512, 256, 128))

    def body(x_ref, xa_ref, vec_ref, w_ref, sc_ref, xo_ref, p_ref, o_ref):
        i = pl.program_id(0)
        h = _modulate(x_ref[...], vec_ref)
        ha = jnp.where(i == 0, 0.0, _modulate(xa_ref[...], vec_ref))
        ext = jnp.concatenate([ha, h], axis=0)
        for gi, w in enumerate(POOL_WINDOWS):
            cols = slice(gi * gd, (gi + 1) * gd)
            acc = ext[:, cols]
            step = 1
            while step < w:
                acc = acc + pltpu.roll(acc, step, 0)
                step *= 2
            p = (acc[POOL_HALO:, :] / _pool_counts(tm, i, w) - h[:, cols]).astype(BF16)
            p_ref[:, cols] = p
            o_ref[:, cols] = _dot(p, w_ref[gi]).astype(BF16)
        xo_ref[...] = x_ref[...] + vec_ref[3:4, :] * (o_ref[...].astype(F32) * sc_ref[...])

    nh = tm // POOL_HALO
    row = pl.BlockSpec((tm, d), lambda i: (i, 0))
    return _call(
        body, "pool_fwd", (s // tm,),
        [
            row,
            pl.BlockSpec((POOL_HALO, d), lambda i: (jnp.maximum(i * nh - 1, 0), 0)),
            pl.BlockSpec((8, d), lambda i: (0, 0)),
            pl.BlockSpec((groups, gd, gd), lambda i: (0, 0, 0)),
            pl.BlockSpec((1, d), lambda i: (0, 0)),
        ],
        [row, row, row],
        [jax.ShapeDtypeStruct((s, d), F32), jax.ShapeDtypeStruct((s, d), BF16), jax.ShapeDtypeStruct((s, d), BF16)],
        [x, x, vec, w_grp, scale], phases=phases,
    )


def _pool_bwd(dxo, x, vec, p, o, w_grp, scale, phases=()):
    s, d = x.shape
    groups, gd, _ = w_grp.shape
    tm = _pick(s, (512, 256, 128))
    nblk = s // tm

    def body(dxo_ref, dxb_ref, x_ref, vec_ref, p_ref, o_ref, w_ref, sc_ref, dx_ref, dw_ref, dsc_ref, dvec_ref, dw_sc):
        i = pl.program_id(0)

        @pl.when(i == 0)
        def _():
            dw_sc[...] = jnp.zeros_like(dw_sc)
            dsc_ref[...] = jnp.zeros_like(dsc_ref)
            dvec_ref[...] = jnp.zeros_like(dvec_ref)

        gate, sc = vec_ref[3:4, :], sc_ref[...]
        dxo_v = dxo_ref[...]
        ov = o_ref[...].astype(F32)
        dvec_ref[3:4, :] += jnp.sum(dxo_v * (ov * sc), axis=0, keepdims=True)
        dy = gate * dxo_v
        dsc_ref[...] += jnp.sum(dy * ov, axis=0, keepdims=True)
        dout = (dy * sc).astype(BF16)
        dout_b = jnp.where(i == nblk - 1, 0.0, gate * dxb_ref[...] * sc).astype(BF16)
        for gi, w in enumerate(POOL_WINDOWS):
            cols = slice(gi * gd, (gi + 1) * gd)
            dw_sc[gi] += _dot_tn(p_ref[:, cols], dout[:, cols])
            wb = w_ref[gi]
            dp = _dot_nt(dout[:, cols], wb)
            dp_b = _dot_nt(dout_b[:, cols], wb)
            e = dp / _pool_counts(tm, i, w)
            t_below = (i + 1) * tm + lax.broadcasted_iota(jnp.int32, (POOL_HALO, 1), 0)
            e_b = dp_b / jnp.minimum(t_below + 1, w).astype(F32)
            acc = jnp.concatenate([e, e_b], axis=0)
            step = 1
            while step < w:
                acc = acc + pltpu.roll(acc, tm + POOL_HALO - step, 0)
                step *= 2
            dx_ref[:, cols] = acc[:tm, :] - dp
        dx, _ = _modulate_bwd(x_ref[...], dx_ref[...], vec_ref, dvec_ref)
        dx_ref[...] = dxo_v + dx

        @pl.when(i == nblk - 1)
        def _():
            dw_ref[...] = dw_sc[...].astype(BF16)

    nh = tm // POOL_HALO
    row = pl.BlockSpec((tm, d), lambda i: (i, 0))
    vecs = pl.BlockSpec((8, d), lambda i: (0, 0))
    wspec = pl.BlockSpec((groups, gd, gd), lambda i: (0, 0, 0))
    return _call(
        body, "pool_bwd", (nblk,),
        [
            row,
            pl.BlockSpec((POOL_HALO, d), lambda i: (jnp.minimum((i + 1) * nh, s // POOL_HALO - 1), 0)),
            row, vecs, row, row, wspec,
            pl.BlockSpec((1, d), lambda i: (0, 0)),
        ],
        [row, wspec, pl.BlockSpec((1, d), lambda i: (0, 0)), vecs],
        [
            jax.ShapeDtypeStruct((s, d), F32),
            jax.ShapeDtypeStruct((groups, gd, gd), BF16),
            jax.ShapeDtypeStruct((1, d), F32),
            jax.ShapeDtypeStruct((8, d), F32),
        ],
        [dxo, dxo, x, vec, p, o, w_grp, scale],
        scratch=[pltpu.VMEM((groups, gd, gd), F32)], phases=phases,
    )


def _loss_head(x, gain, target, phases=()):
    s, d = x.shape
    tm = _pick(s, (512, 256, 128))

    def body(x_ref, g_ref, t_ref, dx_ref, aux_ref):
        @pl.when(pl.program_id(0) == 0)
        def _():
            aux_ref[...] = jnp.zeros_like(aux_ref)

        xv = x_ref[...]
        rstd = _rstd(xv)
        r = xv * rstd
        gain_v = g_ref[...]
        err = r * gain_v - t_ref[...]
        aux_ref[1:2, :] += jnp.sum(err * err, axis=0, keepdims=True)
        dout = err * (1.0 / d)
        aux_ref[0:1, :] += jnp.sum(dout * r, axis=0, keepdims=True)
        dr = dout * gain_v
        dx_ref[...] = rstd * (dr - r * jnp.mean(dr * r, axis=-1, keepdims=True))

    row = pl.BlockSpec((tm, d), lambda i: (i, 0))
    return _call(
        body, "loss_head", (s // tm,),
        [row, pl.BlockSpec((1, d), lambda i: (0, 0)), row],
        [row, pl.BlockSpec((8, d), lambda i: (0, 0))],
        [jax.ShapeDtypeStruct((s, d), F32), jax.ShapeDtypeStruct((8, d), F32)], [x, gain, target], phases=phases,
    )


def _small_adam(gathered, gathered_ws, layout, smalls, chip):
    names = list(smalls)
    n = len(names)
    loss_row, _, _, n_feat = layout["loss"]

    def body(*refs):
        chip_ref, g_ref, gws_ref = refs[0], refs[1], refs[2]
        wmv = refs[3 : 3 + 3 * n]
        outs = refs[3 + 3 * n : 3 + 7 * n]
        total = refs[-1]
        total[...] = g_ref[0]
        for kdev in range(1, N_DEV):
            total[...] += g_ref[kdev]
        total_ws = gws_ref[0]
        for kdev in range(1, N_DEV):
            total_ws = total_ws + gws_ref[kdev]
        my_chip = chip_ref[0]
        for a, name in enumerate(names):
            w_ref, m_ref, v_ref = wmv[3 * a : 3 * a + 3]
            if name == "ab_w_s":
                g = total_ws
            else:
                row0, rows, col0, cols = layout[name]
                if col0 is None:
                    g = jnp.zeros((rows, cols), F32)
                    for j in range(N_CHIPS):
                        g = g + jnp.where(my_chip == j, total[row0 : row0 + rows, j * cols : (j + 1) * cols], 0.0)
                else:
                    g = total[row0 : row0 + rows, col0 : col0 + cols]
            dl, mo, vo = _adam(w_ref[...], g, m_ref[...], v_ref[...])
            outs[4 * a][...] = g
            outs[4 * a + 1][...] = dl
            outs[4 * a + 2][...] = mo
            outs[4 * a + 3][...] = vo
        refs[3 + 7 * n][...] = 0.5 * jnp.sum(total[loss_row : loss_row + 1, 0:n_feat], axis=1, keepdims=True) / n_feat

    ins = [gathered, gathered_ws]
    out_shapes = []
    for name in names:
        ins.extend(smalls[name])
        out_shapes.extend([jax.ShapeDtypeStruct(smalls[name][0].shape, F32)] * 4)
    out_shapes.append(jax.ShapeDtypeStruct((1, 1), F32))
    whole = lambda shape: pl.BlockSpec(shape, functools.partial(lambda nd, i, c: (0,) * nd, len(shape)))
    res = pl.pallas_call(
        body, name="small_adam",
        grid_spec=pltpu.PrefetchScalarGridSpec(
            num_scalar_prefetch=1, grid=(1,),
            in_specs=[whole(a.shape) for a in ins], out_specs=[whole(o.shape) for o in out_shapes],
            scratch_shapes=[pltpu.VMEM(gathered.shape[1:], F32)],
        ),
        out_shape=out_shapes,
        compiler_params=pltpu.CompilerParams(dimension_semantics=("arbitrary",), vmem_limit_bytes=VMEM_LIMIT_BYTES),
    )(chip.reshape(1).astype(jnp.int32), *ins)
    return {name: res[4 * a : 4 * a + 4] for a, name in enumerate(names)}, res[4 * n]


def _pad_rows(a, rows=8):
    extra = (-a.shape[0]) % rows
    return jnp.pad(a, ((0, extra), (0, 0))) if extra else a


def _pad_cols(a, cols):
    return jnp.pad(a, ((0, 0), (0, cols - a.shape[1]))) if a.shape[1] < cols else a


def _run(fn, *phases):
    outs, p_outs = fn(list(phases))
    for p, po in zip(phases, p_outs):
        p.then(po)
    return outs


def kernel(x, c, norm_g, w_mod, b_mod, w_ffn_in, w_ffn_out, ab_w_in, ab_norm_v, ab_w_s, ab_b_s, ab_conv_w, ab_w_out, pool_w_grp, pool_scale, final_g, loss_target, m_norm_g, m_w_mod, m_b_mod, m_w_ffn_in, m_w_ffn_out, m_ab_w_in, m_ab_norm_v, m_ab_w_s, m_ab_b_s, m_ab_conv_w, m_ab_w_out, m_pool_w_grp, m_pool_scale, m_final_g, v_norm_g, v_w_mod, v_b_mod, v_w_ffn_in, v_w_ffn_out, v_ab_w_in, v_ab_norm_v, v_ab_w_s, v_ab_b_s, v_ab_conv_w, v_ab_w_out, v_pool_w_grp, v_pool_scale, v_final_g):
    ix, iy, ic = _place()
    chip = 2 * ix + iy
    me = 4 * ix + 2 * iy + ic
    where = jnp.stack([chip, ic]).astype(jnp.int32)
    s, d = x.shape[1], x.shape[2]
    x0 = x.reshape(s, d)
    target = loss_target.reshape(s, d)
    n_layers = norm_g.shape[0]
    dq = d // N_CHIPS
    heads, chunk = ab_w_s.shape[1], ab_w_s.shape[2]
    da = ab_norm_v.shape[1]
    db = ab_conv_w.shape[2] * N_CHIPS
    f_hidden = w_ffn_out.shape[2] * N_CHIPS
    assert n_layers == 2 and da % heads == 0

    cw_pad = _pad_cols(ab_conv_w.reshape(3, db // N_CHIPS), dq)
    packed = jnp.concatenate(
        [_pad_rows(c.reshape(N_CHIPS, dq)), _pad_rows(norm_g.reshape(-1, dq)), _pad_rows(pool_scale.reshape(1, dq)), _pad_rows(cw_pad)],
        axis=0,
    )
    ncol = w_mod.shape[2]
    b_cols = lax.dynamic_slice(b_mod, (0, chip * ncol), (n_layers, ncol)).reshape(n_layers, 1, ncol)
    small = {}

    def small_gather(key, arrs):
        def then(outs):
            small[key] = outs

        return _phase_small_gather(arrs, then)

    stacks = {
        "w_ffn_in": tuple(a.reshape((-1,) + a.shape[2:]) for a in (w_ffn_in, m_w_ffn_in, v_w_ffn_in)),
        "w_ffn_out": tuple(a.reshape((-1,) + a.shape[2:]) for a in (w_ffn_out, m_w_ffn_out, v_w_ffn_out)),
        "ab_w_in": (ab_w_in, m_ab_w_in, v_ab_w_in),
        "ab_w_out": (ab_w_out, m_ab_w_out, v_ab_w_out),
        "pool_w_grp": (pool_w_grp[0], m_pool_w_grp[0], v_pool_w_grp[0]),
    }
    big_in = _Big((1, d, 2 * f_hidden), 2, 1)
    big_out = _Big((1, f_hidden, d), 1, 2)
    units = {}
    for l in range(n_layers):
        for k in range(2):
            units[f"in{l}{k}"] = (big_in, "w_ffn_in", 2 * l + k)
            units[f"out{l}{k}"] = (big_out, "w_ffn_out", 2 * l + k)
    units["abin"] = (_Big((1, d, ab_w_in.shape[2] * N_CHIPS), 2, 1), "ab_w_in", 0)
    units["about"] = (_Big((1, ab_w_out.shape[1] * N_CHIPS, d), 1, 2), "ab_w_out", 0)
    units["pool"] = (_Big((pool_w_grp.shape[1], pool_w_grp.shape[2] * N_CHIPS, pool_w_grp.shape[3]), 1, 0), "pool_w_grp", 0)
    big = {u: g for u, (g, _, _) in units.items()}

    weight = {}
    complete = set()

    def cast(u):
        g, st, b0 = units[u]

        def launch(phases):
            (weight[u],), p_outs = _cast_into_full(stacks[st][0], b0, g, where, "cast_" + u, phases)
            return None, p_outs

        return launch

    def gather_ici(*us):
        def then(outs):
            for u, o in zip(us, outs):
                weight[u] = o

        return _phase_gather_ici([weight[u] for u in us], [big[u] for u in us], then)

    def gather_sibling(*us):
        def then(outs):
            for u, o in zip(us, outs):
                weight[u] = o
                complete.add(u)

        return _phase_gather_sibling([weight[u] for u in us], [big[u] for u in us], then)

    def w_of(u):
        assert u in complete, u
        return weight[u]

    _run(cast("in00"), small_gather("inputs", [packed]))
    small_all = small["inputs"][0]
    by_chip = small_all[0::2]
    c_all = small_all[:, 0:N_CHIPS, :].reshape(N_DEV, d)
    norm_full = by_chip[:, 8 : 8 + 3 * n_layers, :].transpose(1, 0, 2).reshape(3 * n_layers, d)
    pool_scale_full = by_chip[:, 16:17, :].transpose(1, 0, 2).reshape(1, d)
    conv_full = by_chip[:, 24:27, : db // N_CHIPS].transpose(1, 0, 2).reshape(3, db)
    pieces = [("in00", "out00"), ("abin", "about"), ("in01", "out01"), ("in10", "out10", "pool"), ("in11", "out11")]
    in_flight = {}

    def start_gather(p):
        in_flight[p] = _split_start(gather_ici(*pieces[p]), f"gather_{p}_start")

    def started():
        return _after(*[flight.token for flight in in_flight.values()])

    def finish_gather(p, after, meanwhile=None):
        flight = in_flight.pop(p)
        _split_wait(flight, list(after) + list(started().ins), f"gather_{p}_wait")
        crossing = _split_start(gather_sibling(*pieces[p]), f"gather_{p}_forward")
        behind = [crossing.token]
        if p + 2 < len(pieces):
            for u in pieces[p + 2]:
                _run(cast(u), _after(crossing.token))
            start_gather(p + 2)
            behind = list(started().ins)
        if meanwhile is not None:
            behind = behind + meanwhile(_after(crossing.token))
        _split_wait(crossing, behind, f"gather_{p}_forwarded")

    mod_cols = _run(lambda phases: _mod_fwd(c_all, w_mod, b_cols, phases))[0]
    def mod_rows(outs):
        small["mod"] = outs

    _run(cast("out00"), _phase_small_exchange(mod_cols.transpose(1, 0, 2), mod_rows))
    start_gather(0)
    _run(cast("abin"), started())
    _run(cast("about"), started())
    start_gather(1)
    mod_mine = small["mod"][0][0::2]
    mod = mod_mine.transpose(1, 0, 2).reshape(n_layers, 3, 3, d)
    vecs = {
        (l, sub): jnp.pad(norm_full[3 * l + sub][None], ((0, 7), (0, 0))) + jnp.pad(mod[l, sub], ((1, 4), (0, 0)))
        for l in range(n_layers)
        for sub in range(3)
    }
    b_rows = jnp.broadcast_to(ab_b_s[0].T[:, :, None], (chunk, heads, da // heads)).reshape(chunk, da)

    saved = {}

    def ffn_forward(xs, l, sub, k, *phases):
        saved[l, sub, "x"] = xs
        xs, gg, uu, yb = _run(
            lambda ph: _ffn_fwd(xs, vecs[l, sub], w_of(f"in{l}{k}"), w_of(f"out{l}{k}"), f"ffn_fwd_{l}{k}", ph), *phases
        )
        saved[l, sub, "act"] = (gg, uu, yb)
        return xs

    finish_gather(0, [vecs[0, 0]])
    xs = ffn_forward(x0, 0, 0, 0, started())
    saved[0, 1, "x"] = xs
    finish_gather(1, [xs])
    (proj,) = _run(lambda ph: _proj_mod_fwd(xs, vecs[0, 1], w_of("abin"), ph), started())
    (cat,) = _run(lambda ph: _ab_mix_fwd(proj, ab_norm_v, ab_w_s[0], b_rows, conv_full, ph))
    xs, yb = _run(lambda ph: _proj_res_fwd(cat, w_of("about"), xs, vecs[0, 1], ph))
    saved[0, 1, "act"] = (proj, cat, yb)
    finish_gather(2, [xs])
    xs = ffn_forward(xs, 0, 2, 1, started())
    finish_gather(3, [xs])
    xs = ffn_forward(xs, 1, 0, 0, started())
    saved[1, 1, "x"] = xs
    pooled = []

    def pool_forward(behind):
        pooled.extend(_run(lambda ph: _pool_fwd(xs, vecs[1, 1], w_of("pool"), pool_scale_full, ph), behind))
        return [pooled[0]]

    finish_gather(4, [xs], pool_forward)
    xs, pp, oo = pooled
    saved[1, 1, "act"] = (pp, oo)
    xs = ffn_forward(xs, 1, 2, 1)
    dxs, aux = _run(lambda ph: _loss_head(xs, final_g.reshape(1, d), target, ph))

    grad = {}
    recv = {}
    csum = {}
    parts = {}
    reduced = {}
    done = set()
    dvecs, small_g = {}, {}

    def pair_exchange(*us):
        def then(outs):
            for u, o in zip(us, outs):
                recv[u] = o

        return _phase_pair_exchange([grad[u] for u in us], [big[u] for u in us], then)

    def grad_half(u, a, bs, mine, name, *phases):
        (res,) = _run(lambda ph: _grad_half(a, bs, big[u], where, mine, recv[u] if mine else None, name, ph), *phases)
        return res

    def pair_sum(u, *phases):
        def launch(ph):
            (csum[u],), p_outs = _pair_sum(grad[u], recv[u], big[u], where, "pair_sum_" + u, ph)
            return None, p_outs

        _run(launch, *phases)

    def chip_exchange(*us):
        def then(outs):
            for u, o in zip(us, outs):
                parts[u] = o

        return _phase_chip_exchange([csum[u] for u in us], [big[u] for u in us], then)

    def chip_sum(*us, carried=()):
        for n_u, u in enumerate(us):
            g, st, b0 = units[u]

            def launch(ph):
                (reduced[st],), p_outs = _chip_sum(
                    csum[u], parts[u], g, where, reduced.get(st), stacks[st][0].shape, b0, "chip_sum_" + u, ph
                )
                return None, p_outs

            _run(launch, *(carried if n_u == 0 else ()))

    def pair_broadcast(*us):
        sts = [units[u][1] for u in us]
        assert len(set(sts)) == len(sts)

        def then(outs):
            for u, st, o in zip(us, sts, outs):
                reduced[st] = o
                done.add(u)

        return _phase_pair_broadcast([reduced[st] for st in sts], [big[u] for u in us], [units[u][2] for u in us], then)

    def ffn_backward(dxs, l, sub, k, carried_bwd, carried_send, carried_mine):
        gg, uu, yb = saved[l, sub, "act"]
        w_in, w_out = w_of(f"in{l}{k}"), w_of(f"out{l}{k}")
        uo, ui, tag = f"out{l}{k}", f"in{l}{k}", f"{l}{k}"
        dxs, dg, du, a, h, dy, dvecs[l, sub] = _run(
            lambda ph: _ffn_bwd(dxs, saved[l, sub, "x"], vecs[l, sub], gg, uu, yb, w_in, w_out, "ffn_bwd_" + tag, ph), *carried_bwd()
        )
        grad[uo] = grad_half(uo, a, [dy], False, "dw_out_send_" + tag, *carried_send())
        grad[ui] = grad_half(ui, h, [dg, du], False, "dw_in_send_" + tag, pair_exchange(uo))
        csum[uo] = grad_half(uo, a, [dy], True, "dw_out_" + tag, pair_exchange(ui))
        csum[ui] = grad_half(ui, h, [dg, du], True, "dw_in_" + tag, *carried_mine())
        return dxs

    none = lambda: ()
    dxs = ffn_backward(dxs, 1, 2, 1, none, none, none)
    pp, oo = saved[1, 1, "act"]
    dxs, grad["pool"], small_g["pool_scale"], dvecs[1, 1] = _run(
        lambda ph: _pool_bwd(dxs, saved[1, 1, "x"], vecs[1, 1], pp, oo, w_of("pool"), pool_scale_full, ph)
    )

    def after_11():
        return (chip_exchange("in11", "out11"), pair_exchange("pool"))

    def bcast_11():
        chip_sum("in11", "out11")
        pair_sum("pool")
        return (pair_broadcast("in11", "out11"), chip_exchange("pool"))

    dxs = ffn_backward(dxs, 1, 0, 0, after_11, bcast_11, none)

    def after_10():
        return (chip_exchange("in10", "out10"),)

    def bcast_10():
        chip_sum("in10", "out10", "pool")
        return (pair_broadcast("in10", "out10", "pool"),)

    dxs = ffn_backward(dxs, 0, 2, 1, after_10, bcast_10, none)

    proj, cat, yb = saved[0, 1, "act"]
    out01 = _split_start(chip_exchange("out01"), "reduce_out01_start")
    dy, dcat, dgate = _run(lambda ph: _proj_res_bwd(dxs, yb, vecs[0, 1], w_of("about"), ph), _after(out01.token))
    grad["about"] = grad_half("about", cat, [dy], False, "dw_ab_out_send")
    dproj, small_g["ab_norm_v"], small_g["ab_w_s"], dzs, small_g["ab_conv_w"] = _run(
        lambda ph: _ab_mix_bwd(proj, dcat, ab_norm_v, ab_w_s[0], b_rows, conv_full, ph), pair_exchange("about")
    )
    small_g["ab_b_s"] = dzs.reshape(chunk, heads, da // heads).sum(axis=2).T
    dxs, h, dvecs[0, 1] = _run(
        lambda ph: _proj_mod_bwd(dproj[None], w_of("abin"), saved[0, 1, "x"], vecs[0, 1], dxs, dgate, "ab_in_bwd", ph)
    )
    grad["abin"] = grad_half("abin", h, [dproj], False, "dw_ab_in_send")
    (csum["out01"],) = _split_wait(out01, [grad["abin"]], "reduce_out01_wait")
    chip_sum("out01", carried=(pair_exchange("abin"),))
    csum["about"] = grad_half("about", cat, [dy], True, "dw_ab_out", pair_broadcast("out01"))
    csum["abin"] = grad_half("abin", h, [dproj], True, "dw_ab_in")

    layout = {}
    tail = {}

    def after_01():
        tail["01"] = _split_start(chip_exchange("in01", "abin", "about"), "reduce_01_start")
        return (_after(tail["01"].token),)

    def pack_small_grads():
        dvec_all = jnp.stack([dvecs[l, sub] for l in range(n_layers) for sub in range(3)])
        dgain = dvec_all[:, 0, :]
        dmod = dvec_all[:, 1:4, :].reshape(3 * 3 * n_layers, d)
        rows = {
            "norm_g": (dgain, None, dq), "final_g": (aux[0:1], 0, d), "pool_scale": (small_g["pool_scale"], None, dq),
            "b_mod": (dmod, 0, d), "ab_norm_v": (small_g["ab_norm_v"], 0, da),
            "ab_conv_w": (small_g["ab_conv_w"], None, db // N_CHIPS), "ab_b_s": (small_g["ab_b_s"], 0, chunk),
            "loss": (aux[1:2], 0, d),
        }
        row0 = 0
        for nm, (pc, col0, cols) in rows.items():
            layout[nm] = (row0, pc.shape[0], col0, cols)
            row0 += pc.shape[0]
        packed_rows = -(-row0 // 8) * 8
        return sum(
            jnp.pad(pc, ((layout[nm][0], packed_rows - layout[nm][0] - pc.shape[0]), (0, d - pc.shape[1])))
            for nm, (pc, _, _) in rows.items()
        )

    def bcast_01():
        csum["in01"], csum["abin"], csum["about"] = _split_wait(tail["01"], [dvecs[0, 0]], "reduce_01_wait")
        chip_sum("in01", "abin", "about")
        grads_small = [pack_small_grads(), small_g["ab_w_s"].reshape(heads * chunk, chunk)]
        tail["small"] = _split_start(small_gather("grads", grads_small), "gather_small_grads_start")
        return (pair_broadcast("in01", "abin", "about"), _after(tail["small"].token))

    def reduce_out00():
        tail["out00"] = _split_start(chip_exchange("out00"), "reduce_out00_start")
        return (_after(tail["out00"].token),)

    dxs = ffn_backward(dxs, 0, 0, 0, after_01, bcast_01, reduce_out00)
    grad_x = dxs.reshape(x.shape)

    last = _split_start(chip_exchange("in00"), "reduce_last_start")
    (csum["out00"],) = _split_wait(tail["out00"], [last.token], "reduce_out00_wait")
    chip_sum("out00")
    _flush("broadcast_out00", pair_broadcast("out00"))
    _split_wait(tail["small"], [reduced["w_ffn_out"]], "gather_small_grads_wait")
    g_all, gws_all = small["grads"]

    out = {}

    def adam_stack(st, after=()):
        w3, m3, v3 = stacks[st]
        assert all(u in done for u, (_, ust, _) in units.items() if ust == st), st
        shape = {"w_ffn_in": w_ffn_in.shape, "w_ffn_out": w_ffn_out.shape, "pool_w_grp": pool_w_grp.shape}.get(st, w3.shape)
        out[st] = tuple(a.reshape(shape) for a in _adam_stack(w3, reduced[st], m3, v3, "adam_" + st, after))

    for st in ("w_ffn_out", "ab_w_in", "ab_w_out", "pool_w_grp"):
        adam_stack(st, (last.token,))

    shapes2d = {
        "norm_g": (3 * n_layers, dq), "b_mod": (9 * n_layers, d), "final_g": (1, d), "ab_norm_v": (1, da),
        "pool_scale": (1, dq), "ab_conv_w": (3, db // N_CHIPS), "ab_b_s": (heads, chunk), "ab_w_s": (heads * chunk, chunk),
    }
    small_w = {"norm_g": (norm_g, m_norm_g, v_norm_g), "b_mod": (b_mod, m_b_mod, v_b_mod), "final_g": (final_g, m_final_g, v_final_g),
               "ab_norm_v": (ab_norm_v, m_ab_norm_v, v_ab_norm_v), "pool_scale": (pool_scale, m_pool_scale, v_pool_scale),
               "ab_conv_w": (ab_conv_w, m_ab_conv_w, v_ab_conv_w), "ab_b_s": (ab_b_s, m_ab_b_s, v_ab_b_s), "ab_w_s": (ab_w_s, m_ab_w_s, v_ab_w_s)}
    smalls = {nm: tuple(a.reshape(shapes2d[nm]) for a in wmv) for nm, wmv in small_w.items()}
    small_out, loss = _small_adam(g_all, gws_all, layout, smalls, chip)
    loss = loss.reshape(())
    for nm, res in small_out.items():
        out[nm] = tuple(a.reshape(small_w[nm][0].shape) for a in res)

    mod_row0 = layout["b_mod"][0]
    dmod_all = g_all[:, mod_row0 : mod_row0 + 9 * n_layers, :].reshape(N_DEV, n_layers, 9 * d)
    dmod_cols = lax.dynamic_slice(dmod_all, (0, 0, chip * ncol), (N_DEV, n_layers, ncol)).transpose(1, 0, 2)
    out["w_mod"] = tuple(_mod_bwd_adam(c_all.T, dmod_cols, w_mod, m_w_mod, v_w_mod, (last.token,)))

    (csum["in00"],) = _split_wait(
        last, [out[st][1] for st in ("w_mod", "w_ffn_out", "ab_w_in", "ab_w_out", "pool_w_grp")], "reduce_last_wait"
    )
    chip_sum("in00")
    _flush("broadcast_last", pair_broadcast("in00"))
    adam_stack("w_ffn_in")

    order = ["norm_g", "w_mod", "b_mod", "w_ffn_in", "w_ffn_out", "ab_w_in", "ab_norm_v", "ab_w_s", "ab_b_s", "ab_conv_w", "ab_w_out", "pool_w_grp", "pool_scale", "final_g"]
    return (loss, grad_x, *[out[nm][0] for nm in order], *[out[nm][1] for nm in order], *[out[nm][2] for nm in order], *[out[nm][3] for nm in order])
```

```python
import functools
import math

import jax
import jax.numpy as jnp
from jax import lax
from jax.experimental import pallas as pl
from jax.experimental.pallas import tpu as pltpu

F32 = jnp.float32
BF16 = jnp.bfloat16
MESH = pl.DeviceIdType.MESH

EPS = 1e-6
ADAM_LR = 0.001
ADAM_B1 = 0.9
ADAM_B2 = 0.999
ADAM_EPS = 1e-08
ADAM_WD = 0.01
ADAM_STEP = 10
POOL_WINDOWS = (2, 4, 8, 16)
POOL_HALO = 16
CONV_HALO = 8
N_CHIPS = 4
N_DEV = 8
VMEM_LIMIT_BYTES = 48 * 1024 * 1024
EW_BLOCK_ELEMS = 1024 * 1024
ADAM_BLOCK_ELEMS = 512 * 1024


def _pick(n, prefs):
    for p in prefs:
        if p <= n and n % p == 0:
            return p
    return n


def _row_tile(rows, cols, block_elems=EW_BLOCK_ELEMS):
    best = None
    for d in range(16, rows + 1, 16):
        if rows % d == 0 and d * cols <= block_elems:
            best = d
    return best or rows


def _dot(a, b):
    return jnp.dot(a, b, preferred_element_type=F32)


def _dot_nt(a, b):
    return lax.dot_general(a, b, (((1,), (1,)), ((), ())), preferred_element_type=F32)


def _dot_tn(a, b):
    return lax.dot_general(a, b, (((0,), (0,)), ((), ())), preferred_element_type=F32)


def _sigmoid(x):
    return 0.5 * jnp.tanh(0.5 * x) + 0.5


_GELU_C = math.sqrt(2.0 / math.pi)


def _gelu(x):
    x2 = x * x
    t = jnp.tanh(_GELU_C * (x + 0.044715 * x2 * x))
    val = 0.5 * x * (1.0 + t)
    grad = 0.5 * (1.0 + t) + 0.5 * x * (1.0 - t * t) * (_GELU_C * (1.0 + 3.0 * 0.044715 * x2))
    return val, grad


def _rstd(x):
    return lax.rsqrt(jnp.mean(x * x, axis=-1, keepdims=True) + EPS)


def _modulate(x, vec_ref):
    return (x * _rstd(x)) * vec_ref[0:1, :] * (1.0 + vec_ref[2:3, :]) + vec_ref[1:2, :]


def _modulate_bwd(x, dh, vec_ref, dvec_ref):
    gn, sh, sc = vec_ref[0:1, :], vec_ref[1:2, :], vec_ref[2:3, :]
    rstd = _rstd(x)
    r = x * rstd
    dvec_ref[0:1, :] += jnp.sum(dh * r * (1.0 + sc), axis=0, keepdims=True)
    dvec_ref[1:2, :] += jnp.sum(dh, axis=0, keepdims=True)
    dvec_ref[2:3, :] += jnp.sum(dh * r * gn, axis=0, keepdims=True)
    gm = gn * (1.0 + sc)
    dr = dh * gm
    dx = rstd * (dr - r * jnp.mean(dr * r, axis=-1, keepdims=True))
    return dx, r * gm + sh


def _adam(w, g, m, v):
    m = ADAM_B1 * m + (1.0 - ADAM_B1) * g
    v = ADAM_B2 * v + (1.0 - ADAM_B2) * (g * g)
    m_hat = m / (1.0 - ADAM_B1**ADAM_STEP)
    v_hat = v / (1.0 - ADAM_B2**ADAM_STEP)
    delta = -ADAM_LR * (m_hat / (jnp.sqrt(v_hat) + ADAM_EPS) + ADAM_WD * w)
    return delta, m, v


_ANY = pl.BlockSpec(memory_space=pl.ANY)


class _Phase:
    def __init__(self, ins, out_shapes, aliases, n_sems, start, finish, then):
        self.ins, self.out_shapes, self.aliases, self.n_sems = list(ins), list(out_shapes), dict(aliases), n_sems
        self.start, self.finish, self.then = start, finish, then


def _call(body, name, grid, in_specs, out_specs, out_shape, ins, scratch=(), prefetch=(), phases=(), in_place=None):
    n_pre, n_in, n_out, n_sc = len(prefetch), len(in_specs), len(out_specs), len(scratch)
    ph_in = [len(p.ins) for p in phases]
    ph_out = [len(p.out_shapes) for p in phases]

    def kernel_body(*refs):
        pos = [0]

        def take(k):
            pos[0] += k
            return refs[pos[0] - k : pos[0]]

        pre, ins_ = take(n_pre), take(n_in)
        p_ins = [take(k) for k in ph_in]
        outs_ = take(n_out)
        p_outs = [take(k) for k in ph_out]
        sc = take(n_sc)
        sems = [take(2) for _ in phases]
        if phases:
            ids = [pl.program_id(a) for a in range(len(grid))]
            first = functools.reduce(jnp.logical_and, [i == 0 for i in ids])
            last = functools.reduce(jnp.logical_and, [i == g - 1 for i, g in zip(ids, grid)])

            @pl.when(first)
            def _():
                for p, pi, po, (send, recv) in zip(phases, p_ins, p_outs, sems):
                    p.start(pi, po, send, recv)

        if body is not None:
            body(*pre, *ins_, *outs_, *sc)
        if phases:

            @pl.when(last)
            def _():
                for p, pi, po, (send, recv) in zip(phases, p_ins, p_outs, sems):
                    p.finish(pi, po, send, recv)

    aliases = {n_pre + i: o for i, o in (in_place or {}).items()}
    i0, o0 = n_pre + n_in, n_out
    for p in phases:
        for i, o in p.aliases.items():
            aliases[i0 + i] = o0 + o
        i0 += len(p.ins)
        o0 += len(p.out_shapes)
    all_in = list(in_specs) + [_ANY] * sum(ph_in)
    all_out = list(out_specs) + [_ANY] * sum(ph_out)
    all_scratch = list(scratch)
    for p in phases:
        all_scratch += [pltpu.SemaphoreType.DMA((p.n_sems,)), pltpu.SemaphoreType.DMA((p.n_sems,))]
    shapes = list(out_shape) + [s for p in phases for s in p.out_shapes]
    operands = list(prefetch) + list(ins) + [a for p in phases for a in p.ins]
    sem = ("arbitrary",) * len(grid)
    params = pltpu.CompilerParams(dimension_semantics=sem, vmem_limit_bytes=VMEM_LIMIT_BYTES)
    if n_pre:
        res = pl.pallas_call(
            kernel_body, name=name, out_shape=shapes, input_output_aliases=aliases, compiler_params=params,
            grid_spec=pltpu.PrefetchScalarGridSpec(
                num_scalar_prefetch=n_pre, grid=grid, in_specs=all_in, out_specs=all_out, scratch_shapes=all_scratch
            ),
        )(*operands)
    else:
        res = pl.pallas_call(
            kernel_body, name=name, grid=grid, in_specs=all_in, out_specs=all_out, out_shape=shapes,
            scratch_shapes=all_scratch, input_output_aliases=aliases, compiler_params=params,
        )(*operands)
    res = list(res)
    outs, rest = res[:n_out], res[n_out:]
    p_res = []
    for k in ph_out:
        p_res.append(rest[:k])
        rest = rest[k:]
    return outs, p_res


def _place():
    return lax.axis_index("x"), lax.axis_index("y"), lax.axis_index("c")


def _other_chips():
    x, y, _ = _place()
    return [(1 - x, y), (x, 1 - y), (1 - x, 1 - y)]


def _flip(k):
    x, y, c = _place()
    return (1 - x if k & 4 else x, 1 - y if k & 2 else y, 1 - c if k & 1 else c)


def _remote(src, dst, send, recv, k, to):
    return pltpu.make_async_remote_copy(
        src_ref=src, dst_ref=dst, send_sem=send.at[k], recv_sem=recv.at[k], device_id=to, device_id_type=MESH
    )


def _phase_small_gather(arrs, then):
    n = len(arrs)

    def copies(ins, outs, send, recv):
        x, y, c = _place()
        me = 4 * x + 2 * y + c
        local = [pltpu.make_async_copy(ins[a], outs[a].at[me], send.at[a * N_DEV]) for a in range(n)]
        remote = [_remote(ins[a], outs[a].at[me], send, recv, a * N_DEV + k, _flip(k)) for a in range(n) for k in range(1, N_DEV)]
        return local, remote

    def start(ins, outs, send, recv):
        local, remote = copies(ins, outs, send, recv)
        for cp in local + remote:
            cp.start()

    def finish(ins, outs, send, recv):
        local, remote = copies(ins, outs, send, recv)
        for cp in remote + local:
            cp.wait()

    shapes = [jax.ShapeDtypeStruct((N_DEV,) + a.shape, a.dtype) for a in arrs]
    return _Phase(arrs, shapes, {}, n * N_DEV, start, finish, then)


def _phase_small_exchange(arr, then):
    def copies(ins, outs, send, recv):
        x, y, c = _place()
        me = 4 * x + 2 * y + c
        local = pltpu.make_async_copy(ins[0].at[me], outs[0].at[me], send.at[0])
        remote = []
        for k in range(1, N_DEV):
            px, py, pc = _flip(k)
            remote.append(_remote(ins[0].at[4 * px + 2 * py + pc], outs[0].at[me], send, recv, k, (px, py, pc)))
        return [local] + remote

    def start(ins, outs, send, recv):
        for cp in copies(ins, outs, send, recv):
            cp.start()

    def finish(ins, outs, send, recv):
        for cp in copies(ins, outs, send, recv):
            cp.wait()

    return _Phase([arr], [jax.ShapeDtypeStruct(arr.shape, arr.dtype)], {}, N_DEV, start, finish, then)


def _after(*arrs):
    nothing = lambda *args: None
    return _Phase(arrs, [], {}, 1, nothing, nothing, nothing)


def _flush(name, *phases):
    _, p_outs = _call(None, name, (1,), [], [], [], [], phases=list(phases))
    for p, po in zip(phases, p_outs):
        p.then(po)


class _Big:
    KINDS = {"full": (True, True), "half": (True, False), "shard": (False, True), "block": (False, False)}

    def __init__(self, f3, s3, h3):
        assert s3 != h3
        self.f3, self.s3, self.h3 = tuple(f3), s3, h3
        self.bd = tuple(f3[a] // (N_CHIPS if a == s3 else 1) // (2 if a == h3 else 1) for a in range(3))
        self.tile = (1, _row_tile(self.bd[1], self.bd[2]), self.bd[2])
        self.grid = tuple(self.bd[a] // self.tile[a] for a in range(3))

    def dims(self, kind):
        chips, halves = self.KINDS[kind]
        return tuple(
            self.bd[a] * (N_CHIPS if chips and a == self.s3 else 1) * (2 if halves and a == self.h3 else 1) for a in range(3)
        )

    def view(self, ref, chip=None, half=None, batch0=0, both_halves=True):
        start = [batch0, 0, 0]
        size = list(ref.shape)
        size[0] = self.bd[0] * (2 if self.h3 == 0 and both_halves else 1)
        if chip is not None:
            start[self.s3] += chip * self.bd[self.s3]
            size[self.s3] = self.bd[self.s3]
        if half is not None:
            start[self.h3] += half * self.bd[self.h3]
            size[self.h3] = self.bd[self.h3]
        return ref.at[tuple(pl.ds(st, sz) for st, sz in zip(start, size))]

    def spec(self, chip_from=None, half_from=None, lead=(), batch0=0):
        extra = "grid" in (chip_from, half_from)

        def index(*args):
            pref, idx = args[-1], list(args[int(extra) : -1])
            idx[0] += batch0
            if chip_from:
                idx[self.s3] += (pref[0] if chip_from == "pref" else args[0]) * self.grid[self.s3]
            if half_from:
                idx[self.h3] += (pref[1] if half_from == "pref" else args[0]) * self.grid[self.h3]
            return (0,) * len(lead) + tuple(idx)

        return pl.BlockSpec(tuple(lead) + self.tile, index)


def _same(arrs):
    return [jax.ShapeDtypeStruct(a.shape, a.dtype) for a in arrs]


def _phase_gather_ici(arrs, bigs, then):
    n = len(arrs)

    def copies(outs, send, recv, arriving):
        x, y, c = _place()
        return [
            _remote(blk, blk, send, recv, 3 * a + j, (*chip, c))
            for j, chip in enumerate(_other_chips())
            for a in range(n)
            for blk in [bigs[a].view(outs[a], 2 * chip[0] + chip[1] if arriving else 2 * x + y, c)]
        ]

    def start(ins, outs, send, recv):
        for cp in copies(outs, send, recv, False):
            cp.start()

    def finish(ins, outs, send, recv):
        for cp in copies(outs, send, recv, True):
            cp.wait_recv()
        for cp in copies(outs, send, recv, False):
            cp.wait_send()

    return _Phase(arrs, _same(arrs), {a: a for a in range(n)}, 3 * n, start, finish, then)


def _phase_gather_sibling(arrs, bigs, then):
    n = len(arrs)

    def copies(outs, send, recv, arriving):
        x, y, c = _place()
        return [
            _remote(blk, blk, send, recv, 3 * a + j, (x, y, 1 - c))
            for j, chip in enumerate(_other_chips())
            for a in range(n)
            for blk in [bigs[a].view(outs[a], 2 * chip[0] + chip[1], 1 - c if arriving else c)]
        ]

    def start(ins, outs, send, recv):
        for cp in copies(outs, send, recv, False):
            cp.start()

    def finish(ins, outs, send, recv):
        for cp in copies(outs, send, recv, True):
            cp.wait_recv()
        for cp in copies(outs, send, recv, False):
            cp.wait_send()

    return _Phase(arrs, _same(arrs), {a: a for a in range(n)}, 3 * n, start, finish, then)


def _phase_pair_exchange(grads, bigs, then):
    n = len(grads)

    def copies(ins, outs, send, recv):
        x, y, c = _place()
        srcs = [ins[a] if ins[a].shape == outs[a].shape else bigs[a].view(ins[a], None, 1 - c) for a in range(n)]
        return [_remote(srcs[a], outs[a], send, recv, a, (x, y, 1 - c)) for a in range(n)]

    def start(ins, outs, send, recv):
        for cp in copies(ins, outs, send, recv):
            cp.start()

    def finish(ins, outs, send, recv):
        for cp in copies(ins, outs, send, recv):
            cp.wait()

    shapes = [jax.ShapeDtypeStruct(b.dims("half"), BF16) for b in bigs]
    return _Phase(grads, shapes, {}, n, start, finish, then)


def _phase_chip_exchange(sums, bigs, then):
    n = len(sums)

    def copies(ins, outs, send, recv):
        _, _, c = _place()
        return [
            _remote(bigs[a].view(ins[a], 2 * chip[0] + chip[1], both_halves=False), outs[a].at[j], send, recv, 3 * a + j, (*chip, c))
            for j, chip in enumerate(_other_chips())
            for a in range(n)
        ]

    def start(ins, outs, send, recv):
        for cp in copies(ins, outs, send, recv):
            cp.start()

    def finish(ins, outs, send, recv):
        for cp in copies(ins, outs, send, recv):
            cp.wait()

    shapes = [jax.ShapeDtypeStruct((N_CHIPS - 1,) + b.dims("block"), BF16) for b in bigs]
    return _Phase(sums, shapes, {}, 3 * n, start, finish, then)


_HBM = pl.BlockSpec(memory_space=pltpu.HBM)
_SEM = pl.BlockSpec(memory_space=pltpu.SEMAPHORE)
_DATAFLOW = pltpu.SideEffectType.DATAFLOW_SIDE_EFFECTING


class _InFlight:
    def __init__(self, phase, send, recv, arrays, token):
        self.phase, self.send, self.recv, self.arrays, self.token = phase, send, recv, arrays, token


def _phase_results(phase, refs):
    n_in = len(phase.ins)
    updated = {o: i for i, o in phase.aliases.items()}
    fresh = [o for o in range(len(phase.out_shapes)) if o not in updated]
    return [refs[updated[o]] if o in updated else refs[n_in + fresh.index(o)] for o in range(len(phase.out_shapes))]


def _split_start(phase, name):
    n_in = len(phase.ins)
    fresh = [s for o, s in enumerate(phase.out_shapes) if o not in phase.aliases.values()]
    arrays = list(phase.ins) + [lax.empty(s.shape, s.dtype) for s in fresh]
    n = len(arrays)

    def body(*refs):
        phase.start(refs[:n_in], _phase_results(phase, refs[:n]), refs[n], refs[n + 1])
        refs[-1][...] = jnp.zeros_like(refs[-1])

    operands = [pltpu.with_memory_space_constraint(a, pltpu.HBM) for a in arrays]
    res = pl.pallas_call(
        body, name=name,
        out_shape=[pltpu.SemaphoreType.DMA((phase.n_sems,)), pltpu.SemaphoreType.DMA((phase.n_sems,))]
        + [pltpu.HBM(a.shape, a.dtype) for a in arrays] + [jax.ShapeDtypeStruct((8, 128), F32)],
        in_specs=[_HBM] * n, out_specs=[_SEM, _SEM] + [_HBM] * n + [pl.BlockSpec(memory_space=pltpu.VMEM)],
        input_output_aliases={i: 2 + i for i in range(n)},
        compiler_params=pltpu.CompilerParams(has_side_effects=_DATAFLOW),
    )(*operands)
    return _InFlight(phase, res[0], res[1], list(res[2 : 2 + n]), res[-1])


def _split_wait(flight, after, name):
    phase, n = flight.phase, len(flight.arrays)
    n_in = len(phase.ins)

    def body(*refs):
        phase.finish(refs[:n_in], _phase_results(phase, refs[:n]), refs[n], refs[n + 1])

    res = pl.pallas_call(
        body, name=name, out_shape=[pltpu.HBM(a.shape, a.dtype) for a in flight.arrays],
        in_specs=[_HBM] * n + [_SEM, _SEM] + [_ANY] * len(after), out_specs=[_HBM] * n,
        input_output_aliases={i: i for i in range(n)},
        compiler_params=pltpu.CompilerParams(has_side_effects=_DATAFLOW),
    )(*flight.arrays, flight.send, flight.recv, *after)
    res = list(res)
    phase.then(_phase_results(phase, res))
    return res[:n_in]


def _phase_pair_broadcast(stacks, bigs, batch0s, then):
    n = len(stacks)

    def start(ins, outs, send, recv):
        x, y, c = _place()
        for a in range(n):
            blk = bigs[a].view(outs[a], None, c, batch0s[a])
            _remote(blk, blk, send, recv, a, (x, y, 1 - c)).start()

    def finish(ins, outs, send, recv):
        x, y, c = _place()
        for a in range(n):
            mine = bigs[a].view(outs[a], None, c, batch0s[a])
            theirs = bigs[a].view(outs[a], None, 1 - c, batch0s[a])
            _remote(mine, mine, send, recv, a, (x, y, 1 - c)).wait_send()
            _remote(theirs, theirs, send, recv, a, (x, y, 1 - c)).wait_recv()

    return _Phase(stacks, _same(stacks), {a: a for a in range(n)}, n, start, finish, then)


def _tile_call(body, name, big, where, extra, ins, in_specs, out_specs, out_shape, phases=()):
    grid = ((extra,) if extra else ()) + big.grid
    return _call(body, name, grid, in_specs, out_specs, out_shape, ins, prefetch=(where,), phases=phases)


def _cast_into_full(w_stack, batch0, big, where, name, phases=()):
    def body(_, w_ref, o_ref):
        o_ref[...] = w_ref[...].astype(BF16)

    return _tile_call(
        body, name, big, where, 2, [w_stack], [big.spec(None, "grid", batch0=batch0)], [big.spec("pref", "grid")],
        [jax.ShapeDtypeStruct(big.dims("full"), BF16)], phases,
    )


def _pair_sum(g_full, recv_half, big, where, name, phases=()):
    def body(_, g_ref, r_ref, o_ref):
        o_ref[...] = (g_ref[...].astype(F32) + r_ref[...].astype(F32)).astype(BF16)

    half = big.spec("grid", None)
    return _tile_call(
        body, name, big, where, N_CHIPS, [g_full, recv_half], [big.spec("grid", "pref"), half], [half],
        [jax.ShapeDtypeStruct(big.dims("half"), BF16)], phases,
    )


def _chip_sum(chip_sum, parts, big, where, stack, stack_shape, batch0, name, phases=()):
    def body(_, own_ref, p_ref, *rest):
        acc = own_ref[...].astype(F32)
        for k in range(N_CHIPS - 1):
            acc = acc + p_ref[k].astype(F32)
        rest[-1][...] = acc

    ins = [chip_sum, parts] + ([stack] if stack is not None else [])
    in_specs = [big.spec("pref", None), big.spec(None, None, lead=(N_CHIPS - 1,))] + ([_ANY] if stack is not None else [])
    return _call(
        body, name, big.grid, in_specs, [big.spec(None, "pref", batch0=batch0)], [jax.ShapeDtypeStruct(stack_shape, F32)], ins,
        prefetch=(where,), phases=phases, in_place={2: 0} if stack is not None else None,
    )


def _adam_stack(w, g, m, v, name, after=()):
    b, r, c = w.shape
    tr = _row_tile(r, c, ADAM_BLOCK_ELEMS)

    def body(w_ref, g_ref, m_ref, v_ref, *rest):
        go_ref, d_ref, mo_ref, vo_ref = rest[-4:]
        gv = g_ref[...]
        d, mo, vo = _adam(w_ref[...], gv, m_ref[...], v_ref[...])
        go_ref[...] = gv
        d_ref[...] = d
        mo_ref[...] = mo
        vo_ref[...] = vo

    spec = pl.BlockSpec((1, tr, c), lambda bb, i: (bb, i, 0))
    outs, _ = _call(
        body, name, (b, r // tr), [spec] * 4 + [_ANY] * len(after), [spec] * 4, [jax.ShapeDtypeStruct(w.shape, F32)] * 4,
        [w, g, m, v, *after],
    )
    return outs


def _mod_fwd(c_all, w_mod, b_cols, phases=()):
    n_layers, d, n = w_mod.shape
    tn = _pick(n, (768, 512, 384, 256, 128))

    def body(c_ref, w_ref, b_ref, o_ref):
        cv = c_ref[...]
        ca = (cv * _sigmoid(cv)).astype(BF16)
        o_ref[0] = _dot(ca, w_ref[0].astype(BF16)) + b_ref[0]

    return _call(
        body, "mod_fwd", (n_layers, n // tn),
        [
            pl.BlockSpec((N_DEV, d), lambda l, j: (0, 0)),
            pl.BlockSpec((1, d, tn), lambda l, j: (l, 0, j)),
            pl.BlockSpec((1, 1, tn), lambda l, j: (l, 0, j)),
        ],
        [pl.BlockSpec((1, N_DEV, tn), lambda l, j: (l, 0, j))],
        [jax.ShapeDtypeStruct((n_layers, N_DEV, n), F32)], [c_all, w_mod, b_cols], phases=phases,
    )


def _mod_bwd_adam(c_all_t, dmod_cols, w, m, v, after=()):
    n_layers, d, n = w.shape
    tn = _pick(n, (384, 256, 128))

    def body(c_ref, dm_ref, w_ref, m_ref, v_ref, *rest):
        g_ref, d_ref, mo_ref, vo_ref = rest[-4:]
        cv = c_ref[...]
        ca = (cv * _sigmoid(cv)).astype(BF16)
        g = _dot(ca, dm_ref[0].astype(BF16))
        g_ref[0] = g
        dl, mo, vo = _adam(w_ref[0], g, m_ref[0], v_ref[0])
        d_ref[0] = dl
        mo_ref[0] = mo
        vo_ref[0] = vo

    wspec = pl.BlockSpec((1, d, tn), lambda l, j: (l, 0, j))
    outs, _ = _call(
        body, "mod_bwd_adam", (n_layers, n // tn),
        [pl.BlockSpec((d, N_DEV), lambda l, j: (0, 0)), pl.BlockSpec((1, N_DEV, tn), lambda l, j: (l, 0, j)), wspec, wspec, wspec]
        + [_ANY] * len(after),
        [wspec] * 4, [jax.ShapeDtypeStruct(w.shape, F32)] * 4, [c_all_t, dmod_cols, w, m, v, *after],
    )
    return outs


def _ffn_fwd(x, vec, w_in, w_out, name, phases=()):
    s, d = x.shape
    f = w_out.shape[1]
    tm = _pick(s, (1024, 512, 256, 128))
    tf = _pick(f, (256, 128))
    nf = f // tf

    def body(x_ref, vec_ref, wg_ref, wu_ref, wo_ref, xo_ref, g_ref, u_ref, y_ref, h_sc, acc_sc):
        j = pl.program_id(1)

        @pl.when(j == 0)
        def _():
            h_sc[...] = _modulate(x_ref[...], vec_ref).astype(BF16)
            acc_sc[...] = jnp.zeros_like(acc_sc)

        h = h_sc[...]
        g = _dot(h, wg_ref[0])
        u = _dot(h, wu_ref[0])
        g_ref[...] = g.astype(BF16)
        u_ref[...] = u.astype(BF16)
        a = (g * _sigmoid(g) * u).astype(BF16)
        acc_sc[...] += _dot(a, wo_ref[0])

        @pl.when(j == nf - 1)
        def _():
            yv = acc_sc[...]
            xo_ref[...] = x_ref[...] + 0.5 * vec_ref[3:4, :] * yv
            y_ref[...] = yv.astype(BF16)

    row = pl.BlockSpec((tm, d), lambda i, j: (i, 0))
    hid = pl.BlockSpec((tm, tf), lambda i, j: (i, j))
    return _call(
        body, name, (s // tm, nf),
        [
            row,
            pl.BlockSpec((8, d), lambda i, j: (0, 0)),
            pl.BlockSpec((1, d, tf), lambda i, j: (0, 0, j)),
            pl.BlockSpec((1, d, tf), lambda i, j: (0, 0, nf + j)),
            pl.BlockSpec((1, tf, d), lambda i, j: (0, j, 0)),
        ],
        [row, hid, hid, row],
        [
            jax.ShapeDtypeStruct((s, d), F32),
            jax.ShapeDtypeStruct((s, f), BF16),
            jax.ShapeDtypeStruct((s, f), BF16),
            jax.ShapeDtypeStruct((s, d), BF16),
        ],
        [x, vec, w_in, w_in, w_out],
        scratch=[pltpu.VMEM((tm, d), BF16), pltpu.VMEM((tm, d), F32)], phases=phases,
    )


def _ffn_bwd(dxo, x, vec, gg, uu, y, w_in, w_out, name, phases=()):
    s, d = x.shape
    f = w_out.shape[1]
    tm = _pick(s, (512, 256, 128))
    tf = _pick(f, (256, 128))
    nf = f // tf

    def body(dxo_ref, x_ref, vec_ref, g_ref, u_ref, y_ref, wg_ref, wu_ref, wo_ref,
             dx_ref, dg_ref, du_ref, a_ref, h_ref, dy_ref, dvec_ref, acc_sc):
        i, j = pl.program_id(0), pl.program_id(1)

        @pl.when((i == 0) & (j == 0))
        def _():
            dvec_ref[...] = jnp.zeros_like(dvec_ref)

        @pl.when(j == 0)
        def _():
            dxo_v = dxo_ref[...]
            dy_ref[...] = (0.5 * vec_ref[3:4, :] * dxo_v).astype(BF16)
            dvec_ref[3:4, :] += 0.5 * jnp.sum(dxo_v * y_ref[...].astype(F32), axis=0, keepdims=True)
            acc_sc[...] = jnp.zeros_like(acc_sc)

        da = _dot_nt(dy_ref[...], wo_ref[0])
        g = g_ref[...].astype(F32)
        u = u_ref[...].astype(F32)
        sig = _sigmoid(g)
        sl = g * sig
        a_ref[...] = (sl * u).astype(BF16)
        dg = (da * u * (sig * (1.0 + g * (1.0 - sig)))).astype(BF16)
        du = (da * sl).astype(BF16)
        dg_ref[...] = dg
        du_ref[...] = du
        acc_sc[...] += _dot_nt(dg, wg_ref[0]) + _dot_nt(du, wu_ref[0])

        @pl.when(j == nf - 1)
        def _():
            dx, h = _modulate_bwd(x_ref[...], acc_sc[...], vec_ref, dvec_ref)
            dx_ref[...] = dxo_ref[...] + dx
            h_ref[...] = h.astype(BF16)

    row = pl.BlockSpec((tm, d), lambda i, j: (i, 0))
    hid = pl.BlockSpec((tm, tf), lambda i, j: (i, j))
    vecs = pl.BlockSpec((8, d), lambda i, j: (0, 0))
    return _call(
        body, name, (s // tm, nf),
        [
            row, row, vecs, hid, hid, row,
            pl.BlockSpec((1, d, tf), lambda i, j: (0, 0, j)),
            pl.BlockSpec((1, d, tf), lambda i, j: (0, 0, nf + j)),
            pl.BlockSpec((1, tf, d), lambda i, j: (0, j, 0)),
        ],
        [row, hid, hid, hid, row, row, vecs],
        [
            jax.ShapeDtypeStruct((s, d), F32),
            jax.ShapeDtypeStruct((s, f), BF16),
            jax.ShapeDtypeStruct((s, f), BF16),
            jax.ShapeDtypeStruct((s, f), BF16),
            jax.ShapeDtypeStruct((s, d), BF16),
            jax.ShapeDtypeStruct((s, d), BF16),
            jax.ShapeDtypeStruct((8, d), F32),
        ],
        [dxo, x, vec, gg, uu, y, w_in, w_in, w_out],
        scratch=[pltpu.VMEM((tm, d), F32)], phases=phases,
    )


def _grad_half(a, bs, big, where, mine, recv, name, phases=()):
    s, k1 = a.shape
    n = bs[0].shape[1]
    groups = len(bs)
    rows_halved = big.h3 == 1
    assert rows_halved or groups == 1
    kk, nn = (k1 // 2, n) if rows_halved else (k1, n // 2)
    tk = _pick(kk, (1408, 1024, 512, 256, 128))
    tn = _pick(nn, (1408, 1024, 640, 512, 256, 128))
    nkb, nnb = kk // tk, nn // tn
    assert (recv is None) == (not mine)

    def half(pref):
        return pref[1] if mine else 1 - pref[1]

    def body(_, a_ref, *rest):
        q = pl.program_id(1)
        for p in range(groups):

            @pl.when(q == p)
            def _(p=p):
                acc = _dot_tn(a_ref[...], rest[p][...])
                if recv is not None:
                    acc = acc + rest[groups][0].astype(F32)
                rest[-1][0] = acc.astype(BF16)

    def b_block(p):
        def index(i, q, j, pref):
            jj = jnp.where(q == p, j, jnp.where(q < p, 0, nnb - 1))
            return (0, jj + (0 if rows_halved else half(pref) * nnb))

        return pl.BlockSpec((s, tn), index)

    out_spec = pl.BlockSpec((1, tk, tn), lambda i, q, j, pref: (0, i, q * nnb + j))
    in_specs = [pl.BlockSpec((s, tk), lambda i, q, j, pref: (0, i + (half(pref) * nkb if rows_halved else 0)))]
    in_specs += [b_block(p) for p in range(groups)]
    ins = [a, *bs]
    if recv is not None:
        in_specs.append(out_spec)
        ins.append(recv)
    return _call(
        body, name, (nkb, groups, nnb), in_specs, [out_spec], [jax.ShapeDtypeStruct(big.dims("half"), BF16)], ins,
        prefetch=(where,), phases=phases,
    )


def _proj_mod_fwd(x, vec, w, phases=()):
    s, d = x.shape
    n = w.shape[2]
    tm = _pick(s, (512, 256, 128))
    tn = _pick(n, (640, 512, 256, 128))

    def body(x_ref, vec_ref, w_ref, o_ref, h_sc):
        @pl.when(pl.program_id(1) == 0)
        def _():
            h_sc[...] = _modulate(x_ref[...], vec_ref).astype(BF16)

        o_ref[...] = _dot(h_sc[...], w_ref[0])

    return _call(
        body, "ab_in_fwd", (s // tm, n // tn),
        [
            pl.BlockSpec((tm, d), lambda i, j: (i, 0)),
            pl.BlockSpec((8, d), lambda i, j: (0, 0)),
            pl.BlockSpec((1, d, tn), lambda i, j: (0, 0, j)),
        ],
        [pl.BlockSpec((tm, tn), lambda i, j: (i, j))],
        [jax.ShapeDtypeStruct((s, n), F32)], [x, vec, w],
        scratch=[pltpu.VMEM((tm, d), BF16)], phases=phases,
    )


def _proj_res_fwd(a, w, x, vec, phases=()):
    s, kd = a.shape
    d = x.shape[1]
    tm = _pick(s, (512, 256, 128))

    def body(a_ref, w_ref, x_ref, vec_ref, xo_ref, y_ref):
        yv = _dot(a_ref[...], w_ref[0])
        xo_ref[...] = x_ref[...] + vec_ref[3:4, :] * yv
        y_ref[...] = yv.astype(BF16)

    row = pl.BlockSpec((tm, d), lambda i: (i, 0))
    return _call(
        body, "ab_out_fwd", (s // tm,),
        [pl.BlockSpec((tm, kd), lambda i: (i, 0)), pl.BlockSpec((1, kd, d), lambda i: (0, 0, 0)), row, pl.BlockSpec((8, d), lambda i: (0, 0))],
        [row, row],
        [jax.ShapeDtypeStruct((s, d), F32), jax.ShapeDtypeStruct((s, d), BF16)], [a, w, x, vec], phases=phases,
    )


def _proj_res_bwd(dxo, y, vec, w, phases=()):
    s, d = dxo.shape
    kd = w.shape[1]
    tm = _pick(s, (512, 256, 128))

    def body(dxo_ref, y_ref, vec_ref, w_ref, dy_ref, da_ref, dgate_ref):
        @pl.when(pl.program_id(0) == 0)
        def _():
            dgate_ref[...] = jnp.zeros_like(dgate_ref)

        dxo_v = dxo_ref[...]
        dy = (vec_ref[3:4, :] * dxo_v).astype(BF16)
        dy_ref[...] = dy
        dgate_ref[3:4, :] += jnp.sum(dxo_v * y_ref[...].astype(F32), axis=0, keepdims=True)
        da_ref[...] = _dot_nt(dy, w_ref[0]).astype(BF16)

    row = pl.BlockSpec((tm, d), lambda i: (i, 0))
    vecs = pl.BlockSpec((8, d), lambda i: (0, 0))
    return _call(
        body, "ab_out_bwd", (s // tm,),
        [row, row, vecs, pl.BlockSpec((1, kd, d), lambda i: (0, 0, 0))],
        [row, pl.BlockSpec((tm, kd), lambda i: (i, 0)), vecs],
        [jax.ShapeDtypeStruct((s, d), BF16), jax.ShapeDtypeStruct((s, kd), BF16), jax.ShapeDtypeStruct((8, d), F32)],
        [dxo, y, vec, w], phases=phases,
    )


def _proj_mod_bwd(dproj, w, x, vec, dxo, dvec_in, name, phases=()):
    parts, s, n_part = dproj.shape
    d = x.shape[1]
    tm = _pick(s, (512, 256, 128))
    tk = _pick(n_part, (1408, 1280, 1024, 512, 256, 128))
    per_part = n_part // tk
    nk = parts * per_part

    def body(dp_ref, w_ref, x_ref, vec_ref, dxo_ref, dvi_ref, dx_ref, h_ref, dvec_ref, acc_sc):
        i, k = pl.program_id(0), pl.program_id(1)

        @pl.when((i == 0) & (k == 0))
        def _():
            dvec_ref[...] = dvi_ref[...]

        @pl.when(k == 0)
        def _():
            acc_sc[...] = jnp.zeros_like(acc_sc)

        acc_sc[...] += _dot_nt(dp_ref[0], w_ref[0])

        @pl.when(k == nk - 1)
        def _():
            dx, h = _modulate_bwd(x_ref[...], acc_sc[...], vec_ref, dvec_ref)
            dx_ref[...] = dxo_ref[...] + dx
            h_ref[...] = h.astype(BF16)

    row = pl.BlockSpec((tm, d), lambda i, k: (i, 0))
    vecs = pl.BlockSpec((8, d), lambda i, k: (0, 0))
    return _call(
        body, name, (s // tm, nk),
        [
            pl.BlockSpec((1, tm, tk), lambda i, k: (k // per_part, i, k % per_part)),
            pl.BlockSpec((1, d, tk), lambda i, k: (0, 0, k)),
            row, vecs, row, vecs,
        ],
        [row, row, vecs],
        [jax.ShapeDtypeStruct((s, d), F32), jax.ShapeDtypeStruct((s, d), BF16), jax.ShapeDtypeStruct((8, d), F32)],
        [dproj, w, x, vec, dxo, dvec_in], scratch=[pltpu.VMEM((tm, d), F32)], phases=phases,
    )


def _tril(n):
    return lax.broadcasted_iota(jnp.int32, (n, n), 0) >= lax.broadcasted_iota(jnp.int32, (n, n), 1)


def _layernorm_stats(gv):
    mu = jnp.mean(gv, axis=-1, keepdims=True)
    cen = gv - mu
    rstd = lax.rsqrt(jnp.mean(cen * cen, axis=-1, keepdims=True) + EPS)
    return cen * rstd, rstd


def _shift_down(q, k, above_ref, c_cg, c_xb, first):
    width = q.shape[1]
    rows = lax.broadcasted_iota(jnp.int32, q.shape, 0)
    out = pltpu.roll(q, k, 0)
    for r in range(k):
        src = CONV_HALO - k + r
        above = above_ref[src : src + 1, c_cg : c_cg + width] * above_ref[src : src + 1, c_xb : c_xb + width]
        above = jnp.where(first, 0.0, above)
        out = jnp.where(rows == r, above, out)
    return out


def _ab_mix_fwd(proj, norm_v, w_s, b_rows, conv_w, phases=()):
    s, n = proj.shape
    heads, chunk, _ = w_s.shape
    da = norm_v.shape[1]
    hd = da // heads
    db = conv_w.shape[1]
    tm = _pick(s, (512, 256, 128))

    def body(p_ref, ph_ref, nv_ref, ws_ref, b_ref, cw_ref, o_ref):
        first = pl.program_id(0) == 0
        gu, _ = _gelu(p_ref[:, 0:da])
        gv, _ = _gelu(p_ref[:, da : 2 * da])
        xhat, _ = _layernorm_stats(gv)
        vn = (xhat * nv_ref[...]).astype(BF16)
        mask = _tril(chunk)
        for hh in range(heads):
            wm = jnp.where(mask, ws_ref[hh], 0.0).astype(BF16)
            cols = slice(hh * hd, (hh + 1) * hd)
            for nn in range(tm // chunk):
                rows = slice(nn * chunk, (nn + 1) * chunk)
                z = _dot(wm, vn[rows, cols]) + b_ref[:, cols]
                o_ref[rows, cols] = (gu[rows, cols] * z).astype(BF16)
        c_cg, c_xb = 2 * da + db, 2 * da + 2 * db
        bg = p_ref[:, 2 * da : 2 * da + db]
        q = p_ref[:, c_cg : c_cg + db] * p_ref[:, c_xb : c_xb + db]
        q1 = _shift_down(q, 1, ph_ref, c_cg, c_xb, first)
        q2 = _shift_down(q, 2, ph_ref, c_cg, c_xb, first)
        conv = cw_ref[0:1, :] * q2 + cw_ref[1:2, :] * q1 + cw_ref[2:3, :] * q
        o_ref[:, da : da + db] = (bg * conv).astype(BF16)

    nh = tm // CONV_HALO
    return _call(
        body, "ab_mix_fwd", (s // tm,),
        [
            pl.BlockSpec((tm, n), lambda i: (i, 0)),
            pl.BlockSpec((CONV_HALO, n), lambda i: (jnp.maximum(i * nh - 1, 0), 0)),
            pl.BlockSpec((1, da), lambda i: (0, 0)),
            pl.BlockSpec((heads, chunk, chunk), lambda i: (0, 0, 0)),
            pl.BlockSpec((chunk, da), lambda i: (0, 0)),
            pl.BlockSpec((3, db), lambda i: (0, 0)),
        ],
        [pl.BlockSpec((tm, da + db), lambda i: (i, 0))],
        [jax.ShapeDtypeStruct((s, da + db), BF16)], [proj, proj, norm_v, w_s, b_rows, conv_w], phases=phases,
    )


def _ab_mix_bwd(proj, dcat, norm_v, w_s, b_rows, conv_w, phases=()):
    s, n = proj.shape
    heads, chunk, _ = w_s.shape
    da = norm_v.shape[1]
    hd = da // heads
    db = conv_w.shape[1]
    tm = _pick(s, (512, 256, 128))
    nblk = s // tm
    dhalo = 2 * CONV_HALO

    def body(p_ref, pa_ref, pb_ref, dc_ref, dcb_ref, nv_ref, ws_ref, b_ref, cw_ref,
             dp_ref, dnv_ref, dws_ref, dzs_ref, dcw_ref, dvn_sc):
        i = pl.program_id(0)
        first, last = i == 0, i == nblk - 1

        @pl.when(first)
        def _():
            dnv_ref[...] = jnp.zeros_like(dnv_ref)
            dws_ref[...] = jnp.zeros_like(dws_ref)
            dzs_ref[...] = jnp.zeros_like(dzs_ref)
            dcw_ref[...] = jnp.zeros_like(dcw_ref)

        uu = p_ref[:, 0:da]
        gu, gu_grad = _gelu(uu)
        gv, gv_grad = _gelu(p_ref[:, da : 2 * da])
        xhat, rstd = _layernorm_stats(gv)
        nv = nv_ref[...]
        vn = (xhat * nv).astype(BF16)
        dya = dc_ref[:, 0:da].astype(F32)
        dz = (dya * gu).astype(BF16)
        mask = _tril(chunk)
        for hh in range(heads):
            wm = jnp.where(mask, ws_ref[hh], 0.0).astype(BF16)
            cols = slice(hh * hd, (hh + 1) * hd)
            dws = jnp.zeros((chunk, chunk), F32)
            for nn in range(tm // chunk):
                rows = slice(nn * chunk, (nn + 1) * chunk)
                z = _dot(wm, vn[rows, cols]) + b_ref[:, cols]
                dp_ref[rows, cols] = (dya[rows, cols] * z * gu_grad[rows, cols]).astype(BF16)
                dz_blk = dz[rows, cols]
                dws = dws + _dot_nt(dz_blk, vn[rows, cols])
                dzs_ref[:, cols] += dz_blk.astype(F32)
                dvn = _dot_tn(wm, dz_blk)
                dnv_ref[:, cols] += jnp.sum(dvn * xhat[rows, cols], axis=0, keepdims=True)
                dvn_sc[rows, cols] = dvn
            dws_ref[hh] += jnp.where(mask, dws, 0.0)
        dxhat = dvn_sc[...] * nv
        dgv = rstd * (dxhat - jnp.mean(dxhat, axis=-1, keepdims=True) - xhat * jnp.mean(dxhat * xhat, axis=-1, keepdims=True))
        dp_ref[:, da : 2 * da] = (dgv * gv_grad).astype(BF16)

        c_bg, c_cg, c_xb = 2 * da, 2 * da + db, 2 * da + 2 * db
        bg = p_ref[:, c_bg : c_bg + db]
        cg = p_ref[:, c_cg : c_cg + db]
        xb = p_ref[:, c_xb : c_xb + db]
        q = cg * xb
        q1 = _shift_down(q, 1, pa_ref, c_cg, c_xb, first)
        q2 = _shift_down(q, 2, pa_ref, c_cg, c_xb, first)
        dyb = dc_ref[:, da : da + db].astype(F32)
        conv = cw_ref[0:1, :] * q2 + cw_ref[1:2, :] * q1 + cw_ref[2:3, :] * q
        dp_ref[:, c_bg : c_bg + db] = (dyb * conv).astype(BF16)
        e = dyb * bg
        dcw_ref[0:1, :] += jnp.sum(e * q2, axis=0, keepdims=True)
        dcw_ref[1:2, :] += jnp.sum(e * q1, axis=0, keepdims=True)
        dcw_ref[2:3, :] += jnp.sum(e * q, axis=0, keepdims=True)
        rows = lax.broadcasted_iota(jnp.int32, e.shape, 0)
        dq = cw_ref[2:3, :] * e
        for kk in (1, 2):
            ek = pltpu.roll(e, tm - kk, 0)
            for r in range(kk):
                below = dcb_ref[r : r + 1, da : da + db].astype(F32) * pb_ref[r : r + 1, c_bg : c_bg + db]
                below = jnp.where(last, 0.0, below)
                ek = jnp.where(rows == tm - kk + r, below, ek)
            dq = dq + cw_ref[2 - kk : 3 - kk, :] * ek
        dp_ref[:, c_cg : c_cg + db] = (dq * xb).astype(BF16)
        dp_ref[:, c_xb : c_xb + db] = (dq * cg).astype(BF16)

    nh = tm // CONV_HALO
    nhb = tm // dhalo
    const2 = lambda i: (0, 0)
    return _call(
        body, "ab_mix_bwd", (nblk,),
        [
            pl.BlockSpec((tm, n), lambda i: (i, 0)),
            pl.BlockSpec((CONV_HALO, n), lambda i: (jnp.maximum(i * nh - 1, 0), 0)),
            pl.BlockSpec((CONV_HALO, n), lambda i: (jnp.minimum((i + 1) * nh, s // CONV_HALO - 1), 0)),
            pl.BlockSpec((tm, da + db), lambda i: (i, 0)),
            pl.BlockSpec((dhalo, da + db), lambda i: (jnp.minimum((i + 1) * nhb, s // dhalo - 1), 0)),
            pl.BlockSpec((1, da), const2),
            pl.BlockSpec((heads, chunk, chunk), lambda i: (0, 0, 0)),
            pl.BlockSpec((chunk, da), const2),
            pl.BlockSpec((3, db), const2),
        ],
        [
            pl.BlockSpec((tm, n), lambda i: (i, 0)),
            pl.BlockSpec((1, da), const2),
            pl.BlockSpec((heads, chunk, chunk), lambda i: (0, 0, 0)),
            pl.BlockSpec((chunk, da), const2),
            pl.BlockSpec((3, db), const2),
        ],
        [
            jax.ShapeDtypeStruct((s, n), BF16),
            jax.ShapeDtypeStruct((1, da), F32),
            jax.ShapeDtypeStruct((heads, chunk, chunk), F32),
            jax.ShapeDtypeStruct((chunk, da), F32),
            jax.ShapeDtypeStruct((3, db), F32),
        ],
        [proj, proj, proj, dcat, dcat, norm_v, w_s, b_rows, conv_w],
        scratch=[pltpu.VMEM((tm, da), F32)], phases=phases,
    )


def _pool_counts(tm, i, w):
    t = i * tm + lax.broadcasted_iota(jnp.int32, (tm, 1), 0)
    return jnp.minimum(t + 1, w).astype(F32)


def _pool_fwd(x, vec, w_grp, scale, phases=()):
    s, d = x.shape
    groups, gd, _ = w_grp.shape
    tm = _pick(s, (512, 256, 128))

    def body(x_ref, xa_ref, vec_ref, w_ref, sc_ref, xo_ref, p_ref, o_ref):
        i = pl.program_id(0)
        h = _modulate(x_ref[...], vec_ref)
        ha = jnp.where(i == 0, 0.0, _modulate(xa_ref[...], vec_ref))
        ext = jnp.concatenate([ha, h], axis=0)
        for gi, w in enumerate(POOL_WINDOWS):
            cols = slice(gi * gd, (gi + 1) * gd)
            acc = ext[:, cols]
            step = 1
            while step < w:
                acc = acc + pltpu.roll(acc, step, 0)
                step *= 2
            p = (acc[POOL_HALO:, :] / _pool_counts(tm, i, w) - h[:, cols]).astype(BF16)
            p_ref[:, cols] = p
            o_ref[:, cols] = _dot(p, w_ref[gi]).astype(BF16)
        xo_ref[...] = x_ref[...] + vec_ref[3:4, :] * (o_ref[...].astype(F32) * sc_ref[...])

    nh = tm // POOL_HALO
    row = pl.BlockSpec((tm, d), lambda i: (i, 0))
    return _call(
        body, "pool_fwd", (s // tm,),
        [
            row,
            pl.BlockSpec((POOL_HALO, d), lambda i: (jnp.maximum(i * nh - 1, 0), 0)),
            pl.BlockSpec((8, d), lambda i: (0, 0)),
            pl.BlockSpec((groups, gd, gd), lambda i: (0, 0, 0)),
            pl.BlockSpec((1, d), lambda i: (0, 0)),
        ],
        [row, row, row],
        [jax.ShapeDtypeStruct((s, d), F32), jax.ShapeDtypeStruct((s, d), BF16), jax.ShapeDtypeStruct((s, d), BF16)],
        [x, x, vec, w_grp, scale], phases=phases,
    )


def _pool_bwd(dxo, x, vec, p, o, w_grp, scale, phases=()):
    s, d = x.shape
    groups, gd, _ = w_grp.shape
    tm = _pick(s, (512, 256, 128))
    nblk = s // tm

    def body(dxo_ref, dxb_ref, x_ref, vec_ref, p_ref, o_ref, w_ref, sc_ref, dx_ref, dw_ref, dsc_ref, dvec_ref, dw_sc):
        i = pl.program_id(0)

        @pl.when(i == 0)
        def _():
            dw_sc[...] = jnp.zeros_like(dw_sc)
            dsc_ref[...] = jnp.zeros_like(dsc_ref)
            dvec_ref[...] = jnp.zeros_like(dvec_ref)

        gate, sc = vec_ref[3:4, :], sc_ref[...]
        dxo_v = dxo_ref[...]
        ov = o_ref[...].astype(F32)
        dvec_ref[3:4, :] += jnp.sum(dxo_v * (ov * sc), axis=0, keepdims=True)
        dy = gate * dxo_v
        dsc_ref[...] += jnp.sum(dy * ov, axis=0, keepdims=True)
        dout = (dy * sc).astype(BF16)
        dout_b = jnp.where(i == nblk - 1, 0.0, gate * dxb_ref[...] * sc).astype(BF16)
        for gi, w in enumerate(POOL_WINDOWS):
            cols = slice(gi * gd, (gi + 1) * gd)
            dw_sc[gi] += _dot_tn(p_ref[:, cols], dout[:, cols])
            wb = w_ref[gi]
            dp = _dot_nt(dout[:, cols], wb)
            dp_b = _dot_nt(dout_b[:, cols], wb)
            e = dp / _pool_counts(tm, i, w)
            t_below = (i + 1) * tm + lax.broadcasted_iota(jnp.int32, (POOL_HALO, 1), 0)
            e_b = dp_b / jnp.minimum(t_below + 1, w).astype(F32)
            acc = jnp.concatenate([e, e_b], axis=0)
            step = 1
            while step < w:
                acc = acc + pltpu.roll(acc, tm + POOL_HALO - step, 0)
                step *= 2
            dx_ref[:, cols] = acc[:tm, :] - dp
        dx, _ = _modulate_bwd(x_ref[...], dx_ref[...], vec_ref, dvec_ref)
        dx_ref[...] = dxo_v + dx

        @pl.when(i == nblk - 1)
        def _():
            dw_ref[...] = dw_sc[...].astype(BF16)

    nh = tm // POOL_HALO
    row = pl.BlockSpec((tm, d), lambda i: (i, 0))
    vecs = pl.BlockSpec((8, d), lambda i: (0, 0))
    wspec = pl.BlockSpec((groups, gd, gd), lambda i: (0, 0, 0))
    return _call(
        body, "pool_bwd", (nblk,),
        [
            row,
            pl.BlockSpec((POOL_HALO, d), lambda i: (jnp.minimum((i + 1) * nh, s // POOL_HALO - 1), 0)),
            row, vecs, row, row, wspec,
            pl.BlockSpec((1, d), lambda i: (0, 0)),
        ],
        [row, wspec, pl.BlockSpec((1, d), lambda i: (0, 0)), vecs],
        [
            jax.ShapeDtypeStruct((s, d), F32),
            jax.ShapeDtypeStruct((groups, gd, gd), BF16),
            jax.ShapeDtypeStruct((1, d), F32),
            jax.ShapeDtypeStruct((8, d), F32),
        ],
        [dxo, dxo, x, vec, p, o, w_grp, scale],
        scratch=[pltpu.VMEM((groups, gd, gd), F32)], phases=phases,
    )


def _loss_head(x, gain, target, phases=()):
    s, d = x.shape
    tm = _pick(s, (512, 256, 128))

    def body(x_ref, g_ref, t_ref, dx_ref, aux_ref):
        @pl.when(pl.program_id(0) == 0)
        def _():
            aux_ref[...] = jnp.zeros_like(aux_ref)

        xv = x_ref[...]
        rstd = _rstd(xv)
        r = xv * rstd
        gain_v = g_ref[...]
        err = r * gain_v - t_ref[...]
        aux_ref[1:2, :] += jnp.sum(err * err, axis=0, keepdims=True)
        dout = err * (1.0 / d)
        aux_ref[0:1, :] += jnp.sum(dout * r, axis=0, keepdims=True)
        dr = dout * gain_v
        dx_ref[...] = rstd * (dr - r * jnp.mean(dr * r, axis=-1, keepdims=True))

    row = pl.BlockSpec((tm, d), lambda i: (i, 0))
    return _call(
        body, "loss_head", (s // tm,),
        [row, pl.BlockSpec((1, d), lambda i: (0, 0)), row],
        [row, pl.BlockSpec((8, d), lambda i: (0, 0))],
        [jax.ShapeDtypeStruct((s, d), F32), jax.ShapeDtypeStruct((8, d), F32)], [x, gain, target], phases=phases,
    )


def _small_adam(gathered, gathered_ws, layout, smalls, chip):
    names = list(smalls)
    n = len(names)
    loss_row, _, _, n_feat = layout["loss"]

    def body(*refs):
        chip_ref, g_ref, gws_ref = refs[0], refs[1], refs[2]
        wmv = refs[3 : 3 + 3 * n]
        outs = refs[3 + 3 * n : 3 + 7 * n]
        total = refs[-1]
        total[...] = g_ref[0]
        for kdev in range(1, N_DEV):
            total[...] += g_ref[kdev]
        total_ws = gws_ref[0]
        for kdev in range(1, N_DEV):
            total_ws = total_ws + gws_ref[kdev]
        my_chip = chip_ref[0]
        for a, name in enumerate(names):
            w_ref, m_ref, v_ref = wmv[3 * a : 3 * a + 3]
            if name == "ab_w_s":
                g = total_ws
            else:
                row0, rows, col0, cols = layout[name]
                if col0 is None:
                    g = jnp.zeros((rows, cols), F32)
                    for j in range(N_CHIPS):
                        g = g + jnp.where(my_chip == j, total[row0 : row0 + rows, j * cols : (j + 1) * cols], 0.0)
                else:
                    g = total[row0 : row0 + rows, col0 : col0 + cols]
            dl, mo, vo = _adam(w_ref[...], g, m_ref[...], v_ref[...])
            outs[4 * a][...] = g
            outs[4 * a + 1][...] = dl
            outs[4 * a + 2][...] = mo
            outs[4 * a + 3][...] = vo
        refs[3 + 7 * n][...] = 0.5 * jnp.sum(total[loss_row : loss_row + 1, 0:n_feat], axis=1, keepdims=True) / n_feat

    ins = [gathered, gathered_ws]
    out_shapes = []
    for name in names:
        ins.extend(smalls[name])
        out_shapes.extend([jax.ShapeDtypeStruct(smalls[name][0].shape, F32)] * 4)
    out_shapes.append(jax.ShapeDtypeStruct((1, 1), F32))
    whole = lambda shape: pl.BlockSpec(shape, functools.partial(lambda nd, i, c: (0,) * nd, len(shape)))
    res = pl.pallas_call(
        body, name="small_adam",
        grid_spec=pltpu.PrefetchScalarGridSpec(
            num_scalar_prefetch=1, grid=(1,),
            in_specs=[whole(a.shape) for a in ins], out_specs=[whole(o.shape) for o in out_shapes],
            scratch_shapes=[pltpu.VMEM(gathered.shape[1:], F32)],
        ),
        out_shape=out_shapes,
        compiler_params=pltpu.CompilerParams(dimension_semantics=("arbitrary",), vmem_limit_bytes=VMEM_LIMIT_BYTES),
    )(chip.reshape(1).astype(jnp.int32), *ins)
    return {name: res[4 * a : 4 * a + 4] for a, name in enumerate(names)}, res[4 * n]


def _pad_rows(a, rows=8):
    extra = (-a.shape[0]) % rows
    return jnp.pad(a, ((0, extra), (0, 0))) if extra else a


def _pad_cols(a, cols):
    return jnp.pad(a, ((0, 0), (0, cols - a.shape[1]))) if a.shape[1] < cols else a


def _run(fn, *phases):
    outs, p_outs = fn(list(phases))
    for p, po in zip(phases, p_outs):
        p.then(po)
    return outs


def kernel(x, c, norm_g, w_mod, b_mod, w_ffn_in, w_ffn_out, ab_w_in, ab_norm_v, ab_w_s, ab_b_s, ab_conv_w, ab_w_out, pool_w_grp, pool_scale, final_g, loss_target, m_norm_g, m_w_mod, m_b_mod, m_w_ffn_in, m_w_ffn_out, m_ab_w_in, m_ab_norm_v, m_ab_w_s, m_ab_b_s, m_ab_conv_w, m_ab_w_out, m_pool_w_grp, m_pool_scale, m_final_g, v_norm_g, v_w_mod, v_b_mod, v_w_ffn_in, v_w_ffn_out, v_ab_w_in, v_ab_norm_v, v_ab_w_s, v_ab_b_s, v_ab_conv_w, v_ab_w_out, v_pool_w_grp, v_pool_scale, v_final_g):
    ix, iy, ic = _place()
    chip = 2 * ix + iy
    me = 4 * ix + 2 * iy + ic
    where = jnp.stack([chip, ic]).astype(jnp.int32)
    s, d = x.shape[1], x.shape[2]
    x0 = x.reshape(s, d)
    target = loss_target.reshape(s, d)
    n_layers = norm_g.shape[0]
    dq = d // N_CHIPS
    heads, chunk = ab_w_s.shape[1], ab_w_s.shape[2]
    da = ab_norm_v.shape[1]
    db = ab_conv_w.shape[2] * N_CHIPS
    f_hidden = w_ffn_out.shape[2] * N_CHIPS
    assert n_layers == 2 and da % heads == 0

    cw_pad = _pad_cols(ab_conv_w.reshape(3, db // N_CHIPS), dq)
    packed = jnp.concatenate(
        [_pad_rows(c.reshape(N_CHIPS, dq)), _pad_rows(norm_g.reshape(-1, dq)), _pad_rows(pool_scale.reshape(1, dq)), _pad_rows(cw_pad)],
        axis=0,
    )
    ncol = w_mod.shape[2]
    b_cols = lax.dynamic_slice(b_mod, (0, chip * ncol), (n_layers, ncol)).reshape(n_layers, 1, ncol)
    small = {}

    def small_gather(key, arrs):
        def then(outs):
            small[key] = outs

        return _phase_small_gather(arrs, then)

    stacks = {
        "w_ffn_in": tuple(a.reshape((-1,) + a.shape[2:]) for a in (w_ffn_in, m_w_ffn_in, v_w_ffn_in)),
        "w_ffn_out": tuple(a.reshape((-1,) + a.shape[2:]) for a in (w_ffn_out, m_w_ffn_out, v_w_ffn_out)),
        "ab_w_in": (ab_w_in, m_ab_w_in, v_ab_w_in),
        "ab_w_out": (ab_w_out, m_ab_w_out, v_ab_w_out),
        "pool_w_grp": (pool_w_grp[0], m_pool_w_grp[0], v_pool_w_grp[0]),
    }
    big_in = _Big((1, d, 2 * f_hidden), 2, 1)
    big_out = _Big((1, f_hidden, d), 1, 2)
    units = {}
    for l in range(n_layers):
        for k in range(2):
            units[f"in{l}{k}"] = (big_in, "w_ffn_in", 2 * l + k)
            units[f"out{l}{k}"] = (big_out, "w_ffn_out", 2 * l + k)
    units["abin"] = (_Big((1, d, ab_w_in.shape[2] * N_CHIPS), 2, 1), "ab_w_in", 0)
    units["about"] = (_Big((1, ab_w_out.shape[1] * N_CHIPS, d), 1, 2), "ab_w_out", 0)
    units["pool"] = (_Big((pool_w_grp.shape[1], pool_w_grp.shape[2] * N_CHIPS, pool_w_grp.shape[3]), 1, 0), "pool_w_grp", 0)
    big = {u: g for u, (g, _, _) in units.items()}

    weight = {}
    complete = set()

    def cast(u):
        g, st, b0 = units[u]

        def launch(phases):
            (weight[u],), p_outs = _cast_into_full(stacks[st][0], b0, g, where, "cast_" + u, phases)
            return None, p_outs

        return launch

    def gather_ici(*us):
        def then(outs):
            for u, o in zip(us, outs):
                weight[u] = o

        return _phase_gather_ici([weight[u] for u in us], [big[u] for u in us], then)

    def gather_sibling(*us):
        def then(outs):
            for u, o in zip(us, outs):
                weight[u] = o
                complete.add(u)

        return _phase_gather_sibling([weight[u] for u in us], [big[u] for u in us], then)

    def w_of(u):
        assert u in complete, u
        return weight[u]

    _run(cast("in00"), small_gather("inputs", [packed]))
    small_all = small["inputs"][0]
    by_chip = small_all[0::2]
    c_all = small_all[:, 0:N_CHIPS, :].reshape(N_DEV, d)
    norm_full = by_chip[:, 8 : 8 + 3 * n_layers, :].transpose(1, 0, 2).reshape(3 * n_layers, d)
    pool_scale_full = by_chip[:, 16:17, :].transpose(1, 0, 2).reshape(1, d)
    conv_full = by_chip[:, 24:27, : db // N_CHIPS].transpose(1, 0, 2).reshape(3, db)
    pieces = [("in00", "out00"), ("abin", "about"), ("in01", "out01"), ("in10", "out10", "pool"), ("in11", "out11")]
    in_flight = {}

    def start_gather(p):
        in_flight[p] = _split_start(gather_ici(*pieces[p]), f"gather_{p}_start")

    def started():
        return _after(*[flight.token for flight in in_flight.values()])

    def finish_gather(p, after, meanwhile=None):
        flight = in_flight.pop(p)
        _split_wait(flight, list(after) + list(started().ins), f"gather_{p}_wait")
        crossing = _split_start(gather_sibling(*pieces[p]), f"gather_{p}_forward")
        behind = [crossing.token]
        if p + 2 < len(pieces):
            for u in pieces[p + 2]:
                _run(cast(u), _after(crossing.token))
            start_gather(p + 2)
            behind = list(started().ins)
        if meanwhile is not None:
            behind = behind + meanwhile(_after(crossing.token))
        _split_wait(crossing, behind, f"gather_{p}_forwarded")

    mod_cols = _run(lambda phases: _mod_fwd(c_all, w_mod, b_cols, phases))[0]
    def mod_rows(outs):
        small["mod"] = outs

    _run(cast("out00"), _phase_small_exchange(mod_cols.transpose(1, 0, 2), mod_rows))
    start_gather(0)
    _run(cast("abin"), started())
    _run(cast("about"), started())
    start_gather(1)
    mod_mine = small["mod"][0][0::2]
    mod = mod_mine.transpose(1, 0, 2).reshape(n_layers, 3, 3, d)
    vecs = {
        (l, sub): jnp.pad(norm_full[3 * l + sub][None], ((0, 7), (0, 0))) + jnp.pad(mod[l, sub], ((1, 4), (0, 0)))
        for l in range(n_layers)
        for sub in range(3)
    }
    b_rows = jnp.broadcast_to(ab_b_s[0].T[:, :, None], (chunk, heads, da // heads)).reshape(chunk, da)

    saved = {}

    def ffn_forward(xs, l, sub, k, *phases):
        saved[l, sub, "x"] = xs
        xs, gg, uu, yb = _run(
            lambda ph: _ffn_fwd(xs, vecs[l, sub], w_of(f"in{l}{k}"), w_of(f"out{l}{k}"), f"ffn_fwd_{l}{k}", ph), *phases
        )
        saved[l, sub, "act"] = (gg, uu, yb)
        return xs

    finish_gather(0, [vecs[0, 0]])
    xs = ffn_forward(x0, 0, 0, 0, started())
    saved[0, 1, "x"] = xs
    finish_gather(1, [xs])
    (proj,) = _run(lambda ph: _proj_mod_fwd(xs, vecs[0, 1], w_of("abin"), ph), started())
    (cat,) = _run(lambda ph: _ab_mix_fwd(proj, ab_norm_v, ab_w_s[0], b_rows, conv_full, ph))
    xs, yb = _run(lambda ph: _proj_res_fwd(cat, w_of("about"), xs, vecs[0, 1], ph))
    saved[0, 1, "act"] = (proj, cat, yb)
    finish_gather(2, [xs])
    xs = ffn_forward(xs, 0, 2, 1, started())
    finish_gather(3, [xs])
    xs = ffn_forward(xs, 1, 0, 0, started())
    saved[1, 1, "x"] = xs
    pooled = []

    def pool_forward(behind):
        pooled.extend(_run(lambda ph: _pool_fwd(xs, vecs[1, 1], w_of("pool"), pool_scale_full, ph), behind))
        return [pooled[0]]

    finish_gather(4, [xs], pool_forward)
    xs, pp, oo = pooled
    saved[1, 1, "act"] = (pp, oo)
    xs = ffn_forward(xs, 1, 2, 1)
    dxs, aux = _run(lambda ph: _loss_head(xs, final_g.reshape(1, d), target, ph))

    grad = {}
    recv = {}
    csum = {}
    parts = {}
    reduced = {}
    done = set()
    dvecs, small_g = {}, {}

    def pair_exchange(*us):
        def then(outs):
            for u, o in zip(us, outs):
                recv[u] = o

        return _phase_pair_exchange([grad[u] for u in us], [big[u] for u in us], then)

    def grad_half(u, a, bs, mine, name, *phases):
        (res,) = _run(lambda ph: _grad_half(a, bs, big[u], where, mine, recv[u] if mine else None, name, ph), *phases)
        return res

    def pair_sum(u, *phases):
        def launch(ph):
            (csum[u],), p_outs = _pair_sum(grad[u], recv[u], big[u], where, "pair_sum_" + u, ph)
            return None, p_outs

        _run(launch, *phases)

    def chip_exchange(*us):
        def then(outs):
            for u, o in zip(us, outs):
                parts[u] = o

        return _phase_chip_exchange([csum[u] for u in us], [big[u] for u in us], then)

    def chip_sum(*us, carried=()):
        for n_u, u in enumerate(us):
            g, st, b0 = units[u]

            def launch(ph):
                (reduced[st],), p_outs = _chip_sum(
                    csum[u], parts[u], g, where, reduced.get(st), stacks[st][0].shape, b0, "chip_sum_" + u, ph
                )
                return None, p_outs

            _run(launch, *(carried if n_u == 0 else ()))

    def pair_broadcast(*us):
        sts = [units[u][1] for u in us]
        assert len(set(sts)) == len(sts)

        def then(outs):
            for u, st, o in zip(us, sts, outs):
                reduced[st] = o
                done.add(u)

        return _phase_pair_broadcast([reduced[st] for st in sts], [big[u] for u in us], [units[u][2] for u in us], then)

    def ffn_backward(dxs, l, sub, k, carried_bwd, carried_send, carried_mine):
        gg, uu, yb = saved[l, sub, "act"]
        w_in, w_out = w_of(f"in{l}{k}"), w_of(f"out{l}{k}")
        uo, ui, tag = f"out{l}{k}", f"in{l}{k}", f"{l}{k}"
        dxs, dg, du, a, h, dy, dvecs[l, sub] = _run(
            lambda ph: _ffn_bwd(dxs, saved[l, sub, "x"], vecs[l, sub], gg, uu, yb, w_in, w_out, "ffn_bwd_" + tag, ph), *carried_bwd()
        )
        grad[uo] = grad_half(uo, a, [dy], False, "dw_out_send_" + tag, *carried_send())
        grad[ui] = grad_half(ui, h, [dg, du], False, "dw_in_send_" + tag, pair_exchange(uo))
        csum[uo] = grad_half(uo, a, [dy], True, "dw_out_" + tag, pair_exchange(ui))
        csum[ui] = grad_half(ui, h, [dg, du], True, "dw_in_" + tag, *carried_mine())
        return dxs

    none = lambda: ()
    dxs = ffn_backward(dxs, 1, 2, 1, none, none, none)
    pp, oo = saved[1, 1, "act"]
    dxs, grad["pool"], small_g["pool_scale"], dvecs[1, 1] = _run(
        lambda ph: _pool_bwd(dxs, saved[1, 1, "x"], vecs[1, 1], pp, oo, w_of("pool"), pool_scale_full, ph)
    )

    def after_11():
        return (chip_exchange("in11", "out11"), pair_exchange("pool"))

    def bcast_11():
        chip_sum("in11", "out11")
        pair_sum("pool")
        return (pair_broadcast("in11", "out11"), chip_exchange("pool"))

    dxs = ffn_backward(dxs, 1, 0, 0, after_11, bcast_11, none)

    def after_10():
        return (chip_exchange("in10", "out10"),)

    def bcast_10():
        chip_sum("in10", "out10", "pool")
        return (pair_broadcast("in10", "out10", "pool"),)

    dxs = ffn_backward(dxs, 0, 2, 1, after_10, bcast_10, none)

    proj, cat, yb = saved[0, 1, "act"]
    out01 = _split_start(chip_exchange("out01"), "reduce_out01_start")
    dy, dcat, dgate = _run(lambda ph: _proj_res_bwd(dxs, yb, vecs[0, 1], w_of("about"), ph), _after(out01.token))
    grad["about"] = grad_half("about", cat, [dy], False, "dw_ab_out_send")
    dproj, small_g["ab_norm_v"], small_g["ab_w_s"], dzs, small_g["ab_conv_w"] = _run(
        lambda ph: _ab_mix_bwd(proj, dcat, ab_norm_v, ab_w_s[0], b_rows, conv_full, ph), pair_exchange("about")
    )
    small_g["ab_b_s"] = dzs.reshape(chunk, heads, da // heads).sum(axis=2).T
    dxs, h, dvecs[0, 1] = _run(
        lambda ph: _proj_mod_bwd(dproj[None], w_of("abin"), saved[0, 1, "x"], vecs[0, 1], dxs, dgate, "ab_in_bwd", ph)
    )
    grad["abin"] = grad_half("abin", h, [dproj], False, "dw_ab_in_send")
    (csum["out01"],) = _split_wait(out01, [grad["abin"]], "reduce_out01_wait")
    chip_sum("out01", carried=(pair_exchange("abin"),))
    csum["about"] = grad_half("about", cat, [dy], True, "dw_ab_out", pair_broadcast("out01"))
    csum["abin"] = grad_half("abin", h, [dproj], True, "dw_ab_in")

    layout = {}
    tail = {}

    def after_01():
        tail["01"] = _split_start(chip_exchange("in01", "abin", "about"), "reduce_01_start")
        return (_after(tail["01"].token),)

    def pack_small_grads():
        dvec_all = jnp.stack([dvecs[l, sub] for l in range(n_layers) for sub in range(3)])
        dgain = dvec_all[:, 0, :]
        dmod = dvec_all[:, 1:4, :].reshape(3 * 3 * n_layers, d)
        rows = {
            "norm_g": (dgain, None, dq), "final_g": (aux[0:1], 0, d), "pool_scale": (small_g["pool_scale"], None, dq),
            "b_mod": (dmod, 0, d), "ab_norm_v": (small_g["ab_norm_v"], 0, da),
            "ab_conv_w": (small_g["ab_conv_w"], None, db // N_CHIPS), "ab_b_s": (small_g["ab_b_s"], 0, chunk),
            "loss": (aux[1:2], 0, d),
        }
        row0 = 0
        for nm, (pc, col0, cols) in rows.items():
            layout[nm] = (row0, pc.shape[0], col0, cols)
            row0 += pc.shape[0]
        packed_rows = -(-row0 // 8) * 8
        return sum(
            jnp.pad(pc, ((layout[nm][0], packed_rows - layout[nm][0] - pc.shape[0]), (0, d - pc.shape[1])))
            for nm, (pc, _, _) in rows.items()
        )

    def bcast_01():
        csum["in01"], csum["abin"], csum["about"] = _split_wait(tail["01"], [dvecs[0, 0]], "reduce_01_wait")
        chip_sum("in01", "abin", "about")
        grads_small = [pack_small_grads(), small_g["ab_w_s"].reshape(heads * chunk, chunk)]
        tail["small"] = _split_start(small_gather("grads", grads_small), "gather_small_grads_start")
        return (pair_broadcast("in01", "abin", "about"), _after(tail["small"].token))

    def reduce_out00():
        tail["out00"] = _split_start(chip_exchange("out00"), "reduce_out00_start")
        return (_after(tail["out00"].token),)

    dxs = ffn_backward(dxs, 0, 0, 0, after_01, bcast_01, reduce_out00)
    grad_x = dxs.reshape(x.shape)

    last = _split_start(chip_exchange("in00"), "reduce_last_start")
    (csum["out00"],) = _split_wait(tail["out00"], [last.token], "reduce_out00_wait")
    chip_sum("out00")
    _flush("broadcast_out00", pair_broadcast("out00"))
    _split_wait(tail["small"], [reduced["w_ffn_out"]], "gather_small_grads_wait")
    g_all, gws_all = small["grads"]

    out = {}

    def adam_stack(st, after=()):
        w3, m3, v3 = stacks[st]
        assert all(u in done for u, (_, ust, _) in units.items() if ust == st), st
        shape = {"w_ffn_in": w_ffn_in.shape, "w_ffn_out": w_ffn_out.shape, "pool_w_grp": pool_w_grp.shape}.get(st, w3.shape)
        out[st] = tuple(a.reshape(shape) for a in _adam_stack(w3, reduced[st], m3, v3, "adam_" + st, after))

    for st in ("w_ffn_out", "ab_w_in", "ab_w_out", "pool_w_grp"):
        adam_stack(st, (last.token,))

    shapes2d = {
        "norm_g": (3 * n_layers, dq), "b_mod": (9 * n_layers, d), "final_g": (1, d), "ab_norm_v": (1, da),
        "pool_scale": (1, dq), "ab_conv_w": (3, db // N_CHIPS), "ab_b_s": (heads, chunk), "ab_w_s": (heads * chunk, chunk),
    }
    small_w = {"norm_g": (norm_g, m_norm_g, v_norm_g), "b_mod": (b_mod, m_b_mod, v_b_mod), "final_g": (final_g, m_final_g, v_final_g),
               "ab_norm_v": (ab_norm_v, m_ab_norm_v, v_ab_norm_v), "pool_scale": (pool_scale, m_pool_scale, v_pool_scale),
               "ab_conv_w": (ab_conv_w, m_ab_conv_w, v_ab_conv_w), "ab_b_s": (ab_b_s, m_ab_b_s, v_ab_b_s), "ab_w_s": (ab_w_s, m_ab_w_s, v_ab_w_s)}
    smalls = {nm: tuple(a.reshape(shapes2d[nm]) for a in wmv) for nm, wmv in small_w.items()}
    small_out, loss = _small_adam(g_all, gws_all, layout, smalls, chip)
    loss = loss.reshape(())
    for nm, res in small_out.items():
        out[nm] = tuple(a.reshape(small_w[nm][0].shape) for a in res)

    mod_row0 = layout["b_mod"][0]
    dmod_all = g_all[:, mod_row0 : mod_row0 + 9 * n_layers, :].reshape(N_DEV, n_layers, 9 * d)
    dmod_cols = lax.dynamic_slice(dmod_all, (0, 0, chip * ncol), (N_DEV, n_layers, ncol)).transpose(1, 0, 2)
    out["w_mod"] = tuple(_mod_bwd_adam(c_all.T, dmod_cols, w_mod, m_w_mod, v_w_mod, (last.token,)))

    (csum["in00"],) = _split_wait(
        last, [out[st][1] for st in ("w_mod", "w_ffn_out", "ab_w_in", "ab_w_out", "pool_w_grp")], "reduce_last_wait"
    )
    chip_sum("in00")
    _flush("broadcast_last", pair_broadcast("in00"))
    adam_stack("w_ffn_in")

    order = ["norm_g", "w_mod", "b_mod", "w_ffn_in", "w_ffn_out", "ab_w_in", "ab_norm_v", "ab_w_s", "ab_b_s", "ab_conv_w", "ab_w_out", "pool_w_grp", "pool_scale", "final_g"]
    return (loss, grad_x, *[out[nm][0] for nm in order], *[out[nm][1] for nm in order], *[out[nm][2] for nm in order], *[out[nm][3] for nm in order])
```

```python
import functools
import math

import jax
import jax.numpy as jnp
from jax import lax
from jax.experimental import pallas as pl
from jax.experimental.pallas import tpu as pltpu

F32 = jnp.float32
BF16 = jnp.bfloat16
MESH = pl.DeviceIdType.MESH

EPS = 1e-6
ADAM_LR = 0.001
ADAM_B1 = 0.9
ADAM_B2 = 0.999
ADAM_EPS = 1e-08
ADAM_WD = 0.01
ADAM_STEP = 10
POOL_WINDOWS = (2, 4, 8, 16)
POOL_HALO = 16
CONV_HALO = 8
N_CHIPS = 4
N_DEV = 8
VMEM_LIMIT_BYTES = 48 * 1024 * 1024
EW_BLOCK_ELEMS = 1024 * 1024
ADAM_BLOCK_ELEMS = 512 * 1024


def _pick(n, prefs):
    for p in prefs:
        if p <= n and n % p == 0:
            return p
    return n


def _row_tile(rows, cols, block_elems=EW_BLOCK_ELEMS):
    best = None
    for d in range(16, rows + 1, 16):
        if rows % d == 0 and d * cols <= block_elems:
            best = d
    return best or rows


def _dot(a, b):
    return jnp.dot(a, b, preferred_element_type=F32)


def _dot_nt(a, b):
    return lax.dot_general(a, b, (((1,), (1,)), ((), ())), preferred_element_type=F32)


def _dot_tn(a, b):
    return lax.dot_general(a, b, (((0,), (0,)), ((), ())), preferred_element_type=F32)


def _sigmoid(x):
    return 0.5 * jnp.tanh(0.5 * x) + 0.5


_GELU_C = math.sqrt(2.0 / math.pi)


def _gelu(x):
    x2 = x * x
    t = jnp.tanh(_GELU_C * (x + 0.044715 * x2 * x))
    val = 0.5 * x * (1.0 + t)
    grad = 0.5 * (1.0 + t) + 0.5 * x * (1.0 - t * t) * (_GELU_C * (1.0 + 3.0 * 0.044715 * x2))
    return val, grad


def _rstd(x):
    return lax.rsqrt(jnp.mean(x * x, axis=-1, keepdims=True) + EPS)


def _modulate(x, vec_ref):
    return (x * _rstd(x)) * vec_ref[0:1, :] * (1.0 + vec_ref[2:3, :]) + vec_ref[1:2, :]


def _modulate_bwd(x, dh, vec_ref, dvec_ref):
    gn, sh, sc = vec_ref[0:1, :], vec_ref[1:2, :], vec_ref[2:3, :]
    rstd = _rstd(x)
    r = x * rstd
    dvec_ref[0:1, :] += jnp.sum(dh * r * (1.0 + sc), axis=0, keepdims=True)
    dvec_ref[1:2, :] += jnp.sum(dh, axis=0, keepdims=True)
    dvec_ref[2:3, :] += jnp.sum(dh * r * gn, axis=0, keepdims=True)
    gm = gn * (1.0 + sc)
    dr = dh * gm
    dx = rstd * (dr - r * jnp.mean(dr * r, axis=-1, keepdims=True))
    return dx, r * gm + sh


def _adam(w, g, m, v):
    m = ADAM_B1 * m + (1.0 - ADAM_B1) * g
    v = ADAM_B2 * v + (1.0 - ADAM_B2) * (g * g)
    m_hat = m / (1.0 - ADAM_B1**ADAM_STEP)
    v_hat = v / (1.0 - ADAM_B2**ADAM_STEP)
    delta = -ADAM_LR * (m_hat / (jnp.sqrt(v_hat) + ADAM_EPS) + ADAM_WD * w)
    return delta, m, v


_ANY = pl.BlockSpec(memory_space=pl.ANY)


class _Phase:
    def __init__(self, ins, out_shapes, aliases, n_sems, start, finish, then):
        self.ins, self.out_shapes, self.aliases, self.n_sems = list(ins), list(out_shapes), dict(aliases), n_sems
        self.start, self.finish, self.then = start, finish, then


def _call(body, name, grid, in_specs, out_specs, out_shape, ins, scratch=(), prefetch=(), phases=(), in_place=None):
    n_pre, n_in, n_out, n_sc = len(prefetch), len(in_specs), len(out_specs), len(scratch)
    ph_in = [len(p.ins) for p in phases]
    ph_out = [len(p.out_shapes) for p in phases]

    def kernel_body(*refs):
        pos = [0]

        def take(k):
            pos[0] += k
            return refs[pos[0] - k : pos[0]]

        pre, ins_ = take(n_pre), take(n_in)
        p_ins = [take(k) for k in ph_in]
        outs_ = take(n_out)
        p_outs = [take(k) for k in ph_out]
        sc = take(n_sc)
        sems = [take(2) for _ in phases]
        if phases:
            ids = [pl.program_id(a) for a in range(len(grid))]
            first = functools.reduce(jnp.logical_and, [i == 0 for i in ids])
            last = functools.reduce(jnp.logical_and, [i == g - 1 for i, g in zip(ids, grid)])

            @pl.when(first)
            def _():
                for p, pi, po, (send, recv) in zip(phases, p_ins, p_outs, sems):
                    p.start(pi, po, send, recv)

        if body is not None:
            body(*pre, *ins_, *outs_, *sc)
        if phases:

            @pl.when(last)
            def _():
                for p, pi, po, (send, recv) in zip(phases, p_ins, p_outs, sems):
                    p.finish(pi, po, send, recv)

    aliases = {n_pre + i: o for i, o in (in_place or {}).items()}
    i0, o0 = n_pre + n_in, n_out
    for p in phases:
        for i, o in p.aliases.items():
            aliases[i0 + i] = o0 + o
        i0 += len(p.ins)
        o0 += len(p.out_shapes)
    all_in = list(in_specs) + [_ANY] * sum(ph_in)
    all_out = list(out_specs) + [_ANY] * sum(ph_out)
    all_scratch = list(scratch)
    for p in phases:
        all_scratch += [pltpu.SemaphoreType.DMA((p.n_sems,)), pltpu.SemaphoreType.DMA((p.n_sems,))]
    shapes = list(out_shape) + [s for p in phases for s in p.out_shapes]
    operands = list(prefetch) + list(ins) + [a for p in phases for a in p.ins]
    sem = ("arbitrary",) * len(grid)
    params = pltpu.CompilerParams(dimension_semantics=sem, vmem_limit_bytes=VMEM_LIMIT_BYTES)
    if n_pre:
        res = pl.pallas_call(
            kernel_body, name=name, out_shape=shapes, input_output_aliases=aliases, compiler_params=params,
            grid_spec=pltpu.PrefetchScalarGridSpec(
                num_scalar_prefetch=n_pre, grid=grid, in_specs=all_in, out_specs=all_out, scratch_shapes=all_scratch
            ),
        )(*operands)
    else:
        res = pl.pallas_call(
            kernel_body, name=name, grid=grid, in_specs=all_in, out_specs=all_out, out_shape=shapes,
            scratch_shapes=all_scratch, input_output_aliases=aliases, compiler_params=params,
        )(*operands)
    res = list(res)
    outs, rest = res[:n_out], res[n_out:]
    p_res = []
    for k in ph_out:
        p_res.append(rest[:k])
        rest = rest[k:]
    return outs, p_res


def _place():
    return lax.axis_index("x"), lax.axis_index("y"), lax.axis_index("c")


def _other_chips():
    x, y, _ = _place()
    return [(1 - x, y), (x, 1 - y), (1 - x, 1 - y)]


def _flip(k):
    x, y, c = _place()
    return (1 - x if k & 4 else x, 1 - y if k & 2 else y, 1 - c if k & 1 else c)


def _remote(src, dst, send, recv, k, to):
    return pltpu.make_async_remote_copy(
        src_ref=src, dst_ref=dst, send_sem=send.at[k], recv_sem=recv.at[k], device_id=to, device_id_type=MESH
    )


def _phase_small_gather(arrs, then):
    n = len(arrs)

    def copies(ins, outs, send, recv):
        x, y, c = _place()
        me = 4 * x + 2 * y + c
        local = [pltpu.make_async_copy(ins[a], outs[a].at[me], send.at[a * N_DEV]) for a in range(n)]
        remote = [_remote(ins[a], outs[a].at[me], send, recv, a * N_DEV + k, _flip(k)) for a in range(n) for k in range(1, N_DEV)]
        return local, remote

    def start(ins, outs, send, recv):
        local, remote = copies(ins, outs, send, recv)
        for cp in local + remote:
            cp.start()

    def finish(ins, outs, send, recv):
        local, remote = copies(ins, outs, send, recv)
        for cp in remote + local:
            cp.wait()

    shapes = [jax.ShapeDtypeStruct((N_DEV,) + a.shape, a.dtype) for a in arrs]
    return _Phase(arrs, shapes, {}, n * N_DEV, start, finish, then)


def _phase_small_exchange(arr, then):
    def copies(ins, outs, send, recv):
        x, y, c = _place()
        me = 4 * x + 2 * y + c
        local = pltpu.make_async_copy(ins[0].at[me], outs[0].at[me], send.at[0])
        remote = []
        for k in range(1, N_DEV):
            px, py, pc = _flip(k)
            remote.append(_remote(ins[0].at[4 * px + 2 * py + pc], outs[0].at[me], send, recv, k, (px, py, pc)))
        return [local] + remote

    def start(ins, outs, send, recv):
        for cp in copies(ins, outs, send, recv):
            cp.start()

    def finish(ins, outs, send, recv):
        for cp in copies(ins, outs, send, recv):
            cp.wait()

    return _Phase([arr], [jax.ShapeDtypeStruct(arr.shape, arr.dtype)], {}, N_DEV, start, finish, then)


def _after(*arrs):
    nothing = lambda *args: None
    return _Phase(arrs, [], {}, 1, nothing, nothing, nothing)


def _flush(name, *phases):
    _, p_outs = _call(None, name, (1,), [], [], [], [], phases=list(phases))
    for p, po in zip(phases, p_outs):
        p.then(po)


class _Big:
    KINDS = {"full": (True, True), "half": (True, False), "shard": (False, True), "block": (False, False)}

    def __init__(self, f3, s3, h3):
        assert s3 != h3
        self.f3, self.s3, self.h3 = tuple(f3), s3, h3
        self.bd = tuple(f3[a] // (N_CHIPS if a == s3 else 1) // (2 if a == h3 else 1) for a in range(3))
        self.tile = (1, _row_tile(self.bd[1], self.bd[2]), self.bd[2])
        self.grid = tuple(self.bd[a] // self.tile[a] for a in range(3))

    def dims(self, kind):
        chips, halves = self.KINDS[kind]
        return tuple(
            self.bd[a] * (N_CHIPS if chips and a == self.s3 else 1) * (2 if halves and a == self.h3 else 1) for a in range(3)
        )

    def view(self, ref, chip=None, half=None, batch0=0, both_halves=True):
        start = [batch0, 0, 0]
        size = list(ref.shape)
        size[0] = self.bd[0] * (2 if self.h3 == 0 and both_halves else 1)
        if chip is not None:
            start[self.s3] += chip * self.bd[self.s3]
            size[self.s3] = self.bd[self.s3]
        if half is not None:
            start[self.h3] += half * self.bd[self.h3]
            size[self.h3] = self.bd[self.h3]
        return ref.at[tuple(pl.ds(st, sz) for st, sz in zip(start, size))]

    def spec(self, chip_from=None, half_from=None, lead=(), batch0=0):
        extra = "grid" in (chip_from, half_from)

        def index(*args):
            pref, idx = args[-1], list(args[int(extra) : -1])
            idx[0] += batch0
            if chip_from:
                idx[self.s3] += (pref[0] if chip_from == "pref" else args[0]) * self.grid[self.s3]
            if half_from:
                idx[self.h3] += (pref[1] if half_from == "pref" else args[0]) * self.grid[self.h3]
            return (0,) * len(lead) + tuple(idx)

        return pl.BlockSpec(tuple(lead) + self.tile, index)


def _same(arrs):
    return [jax.ShapeDtypeStruct(a.shape, a.dtype) for a in arrs]


def _phase_gather_ici(arrs, bigs, then):
    n = len(arrs)

    def copies(outs, send, recv, arriving):
        x, y, c = _place()
        return [
            _remote(blk, blk, send, recv, 3 * a + j, (*chip, c))
            for j, chip in enumerate(_other_chips())
            for a in range(n)
            for blk in [bigs[a].view(outs[a], 2 * chip[0] + chip[1] if arriving else 2 * x + y, c)]
        ]

    def start(ins, outs, send, recv):
        for cp in copies(outs, send, recv, False):
            cp.start()

    def finish(ins, outs, send, recv):
        for cp in copies(outs, send, recv, True):
            cp.wait_recv()
        for cp in copies(outs, send, recv, False):
            cp.wait_send()

    return _Phase(arrs, _same(arrs), {a: a for a in range(n)}, 3 * n, start, finish, then)


def _phase_gather_sibling(arrs, bigs, then):
    n = len(arrs)

    def copies(outs, send, recv, arriving):
        x, y, c = _place()
        return [
            _remote(blk, blk, send, recv, 3 * a + j, (x, y, 1 - c))
            for j, chip in enumerate(_other_chips())
            for a in range(n)
            for blk in [bigs[a].view(outs[a], 2 * chip[0] + chip[1], 1 - c if arriving else c)]
        ]

    def start(ins, outs, send, recv):
        for cp in copies(outs, send, recv, False):
            cp.start()

    def finish(ins, outs, send, recv):
        for cp in copies(outs, send, recv, True):
            cp.wait_recv()
        for cp in copies(outs, send, recv, False):
            cp.wait_send()

    return _Phase(arrs, _same(arrs), {a: a for a in range(n)}, 3 * n, start, finish, then)


def _phase_pair_exchange(grads, bigs, then):
    n = len(grads)

    def copies(ins, outs, send, recv):
        x, y, c = _place()
        srcs = [ins[a] if ins[a].shape == outs[a].shape else bigs[a].view(ins[a], None, 1 - c) for a in range(n)]
        return [_remote(srcs[a], outs[a], send, recv, a, (x, y, 1 - c)) for a in range(n)]

    def start(ins, outs, send, recv):
        for cp in copies(ins, outs, send, recv):
            cp.start()

    def finish(ins, outs, send, recv):
        for cp in copies(ins, outs, send, recv):
            cp.wait()

    shapes = [jax.ShapeDtypeStruct(b.dims("half"), BF16) for b in bigs]
    return _Phase(grads, shapes, {}, n, start, finish, then)


def _phase_chip_exchange(sums, bigs, then):
    n = len(sums)

    def copies(ins, outs, send, recv):
        _, _, c = _place()
        return [
            _remote(bigs[a].view(ins[a], 2 * chip[0] + chip[1], both_halves=False), outs[a].at[j], send, recv, 3 * a + j, (*chip, c))
            for j, chip in enumerate(_other_chips())
            for a in range(n)
        ]

    def start(ins, outs, send, recv):
        for cp in copies(ins, outs, send, recv):
            cp.start()

    def finish(ins, outs, send, recv):
        for cp in copies(ins, outs, send, recv):
            cp.wait()

    shapes = [jax.ShapeDtypeStruct((N_CHIPS - 1,) + b.dims("block"), BF16) for b in bigs]
    return _Phase(sums, shapes, {}, 3 * n, start, finish, then)


_HBM = pl.BlockSpec(memory_space=pltpu.HBM)
_SEM = pl.BlockSpec(memory_space=pltpu.SEMAPHORE)
_DATAFLOW = pltpu.SideEffectType.DATAFLOW_SIDE_EFFECTING


class _InFlight:
    def __init__(self, phase, send, recv, arrays, token):
        self.phase, self.send, self.recv, self.arrays, self.token = phase, send, recv, arrays, token


def _phase_results(phase, refs):
    n_in = len(phase.ins)
    updated = {o: i for i, o in phase.aliases.items()}
    fresh = [o for o in range(len(phase.out_shapes)) if o not in updated]
    return [refs[updated[o]] if o in updated else refs[n_in + fresh.index(o)] for o in range(len(phase.out_shapes))]


def _split_start(phase, name):
    n_in = len(phase.ins)
    fresh = [s for o, s in enumerate(phase.out_shapes) if o not in phase.aliases.values()]
    arrays = list(phase.ins) + [lax.empty(s.shape, s.dtype) for s in fresh]
    n = len(arrays)

    def body(*refs):
        phase.start(refs[:n_in], _phase_results(phase, refs[:n]), refs[n], refs[n + 1])
        refs[-1][...] = jnp.zeros_like(refs[-1])

    operands = [pltpu.with_memory_space_constraint(a, pltpu.HBM) for a in arrays]
    res = pl.pallas_call(
        body, name=name,
        out_shape=[pltpu.SemaphoreType.DMA((phase.n_sems,)), pltpu.SemaphoreType.DMA((phase.n_sems,))]
        + [pltpu.HBM(a.shape, a.dtype) for a in arrays] + [jax.ShapeDtypeStruct((8, 128), F32)],
        in_specs=[_HBM] * n, out_specs=[_SEM, _SEM] + [_HBM] * n + [pl.BlockSpec(memory_space=pltpu.VMEM)],
        input_output_aliases={i: 2 + i for i in range(n)},
        compiler_params=pltpu.CompilerParams(has_side_effects=_DATAFLOW),
    )(*operands)
    return _InFlight(phase, res[0], res[1], list(res[2 : 2 + n]), res[-1])


def _split_wait(flight, after, name):
    phase, n = flight.phase, len(flight.arrays)
    n_in = len(phase.ins)

    def body(*refs):
        phase.finish(refs[:n_in], _phase_results(phase, refs[:n]), refs[n], refs[n + 1])

    res = pl.pallas_call(
        body, name=name, out_shape=[pltpu.HBM(a.shape, a.dtype) for a in flight.arrays],
        in_specs=[_HBM] * n + [_SEM, _SEM] + [_ANY] * len(after), out_specs=[_HBM] * n,
        input_output_aliases={i: i for i in range(n)},
        compiler_params=pltpu.CompilerParams(has_side_effects=_DATAFLOW),
    )(*flight.arrays, flight.send, flight.recv, *after)
    res = list(res)
    phase.then(_phase_results(phase, res))
    return res[:n_in]


def _phase_pair_broadcast(stacks, bigs, batch0s, then):
    n = len(stacks)

    def start(ins, outs, send, recv):
        x, y, c = _place()
        for a in range(n):
            blk = bigs[a].view(outs[a], None, c, batch0s[a])
            _remote(blk, blk, send, recv, a, (x, y, 1 - c)).start()

    def finish(ins, outs, send, recv):
        x, y, c = _place()
        for a in range(n):
            mine = bigs[a].view(outs[a], None, c, batch0s[a])
            theirs = bigs[a].view(outs[a], None, 1 - c, batch0s[a])
            _remote(mine, mine, send, recv, a, (x, y, 1 - c)).wait_send()
            _remote(theirs, theirs, send, recv, a, (x, y, 1 - c)).wait_recv()

    return _Phase(stacks, _same(stacks), {a: a for a in range(n)}, n, start, finish, then)


def _tile_call(body, name, big, where, extra, ins, in_specs, out_specs, out_shape, phases=()):
    grid = ((extra,) if extra else ()) + big.grid
    return _call(body, name, grid, in_specs, out_specs, out_shape, ins, prefetch=(where,), phases=phases)


def _cast_into_full(w_stack, batch0, big, where, name, phases=()):
    def body(_, w_ref, o_ref):
        o_ref[...] = w_ref[...].astype(BF16)

    return _tile_call(
        body, name, big, where, 2, [w_stack], [big.spec(None, "grid", batch0=batch0)], [big.spec("pref", "grid")],
        [jax.ShapeDtypeStruct(big.dims("full"), BF16)], phases,
    )


def _pair_sum(g_full, recv_half, big, where, name, phases=()):
    def body(_, g_ref, r_ref, o_ref):
        o_ref[...] = (g_ref[...].astype(F32) + r_ref[...].astype(F32)).astype(BF16)

    half = big.spec("grid", None)
    return _tile_call(
        body, name, big, where, N_CHIPS, [g_full, recv_half], [big.spec("grid", "pref"), half], [half],
        [jax.ShapeDtypeStruct(big.dims("half"), BF16)], phases,
    )


def _chip_sum(chip_sum, parts, big, where, stack, stack_shape, batch0, name, phases=()):
    def body(_, own_ref, p_ref, *rest):
        acc = own_ref[...].astype(F32)
        for k in range(N_CHIPS - 1):
            acc = acc + p_ref[k].astype(F32)
        rest[-1][...] = acc

    ins = [chip_sum, parts] + ([stack] if stack is not None else [])
    in_specs = [big.spec("pref", None), big.spec(None, None, lead=(N_CHIPS - 1,))] + ([_ANY] if stack is not None else [])
    return _call(
        body, name, big.grid, in_specs, [big.spec(None, "pref", batch0=batch0)], [jax.ShapeDtypeStruct(stack_shape, F32)], ins,
        prefetch=(where,), phases=phases, in_place={2: 0} if stack is not None else None,
    )


def _adam_stack(w, g, m, v, name, after=()):
    b, r, c = w.shape
    tr = _row_tile(r, c, ADAM_BLOCK_ELEMS)

    def body(w_ref, g_ref, m_ref, v_ref, *rest):
        go_ref, d_ref, mo_ref, vo_ref = rest[-4:]
        gv = g_ref[...]
        d, mo, vo = _adam(w_ref[...], gv, m_ref[...], v_ref[...])
        go_ref[...] = gv
        d_ref[...] = d
        mo_ref[...] = mo
        vo_ref[...] = vo

    spec = pl.BlockSpec((1, tr, c), lambda bb, i: (bb, i, 0))
    outs, _ = _call(
        body, name, (b, r // tr), [spec] * 4 + [_ANY] * len(after), [spec] * 4, [jax.ShapeDtypeStruct(w.shape, F32)] * 4,
        [w, g, m, v, *after],
    )
    return outs


def _mod_fwd(c_all, w_mod, b_cols, phases=()):
    n_layers, d, n = w_mod.shape
    tn = _pick(n, (768, 512, 384, 256, 128))

    def body(c_ref, w_ref, b_ref, o_ref):
        cv = c_ref[...]
        ca = (cv * _sigmoid(cv)).astype(BF16)
        o_ref[0] = _dot(ca, w_ref[0].astype(BF16)) + b_ref[0]

    return _call(
        body, "mod_fwd", (n_layers, n // tn),
        [
            pl.BlockSpec((N_DEV, d), lambda l, j: (0, 0)),
            pl.BlockSpec((1, d, tn), lambda l, j: (l, 0, j)),
            pl.BlockSpec((1, 1, tn), lambda l, j: (l, 0, j)),
        ],
        [pl.BlockSpec((1, N_DEV, tn), lambda l, j: (l, 0, j))],
        [jax.ShapeDtypeStruct((n_layers, N_DEV, n), F32)], [c_all, w_mod, b_cols], phases=phases,
    )


def _mod_bwd_adam(c_all_t, dmod_cols, w, m, v, after=()):
    n_layers, d, n = w.shape
    tn = _pick(n, (384, 256, 128))

    def body(c_ref, dm_ref, w_ref, m_ref, v_ref, *rest):
        g_ref, d_ref, mo_ref, vo_ref = rest[-4:]
        cv = c_ref[...]
        ca = (cv * _sigmoid(cv)).astype(BF16)
        g = _dot(ca, dm_ref[0].astype(BF16))
        g_ref[0] = g
        dl, mo, vo = _adam(w_ref[0], g, m_ref[0], v_ref[0])
        d_ref[0] = dl
        mo_ref[0] = mo
        vo_ref[0] = vo

    wspec = pl.BlockSpec((1, d, tn), lambda l, j: (l, 0, j))
    outs, _ = _call(
        body, "mod_bwd_adam", (n_layers, n // tn),
        [pl.BlockSpec((d, N_DEV), lambda l, j: (0, 0)), pl.BlockSpec((1, N_DEV, tn), lambda l, j: (l, 0, j)), wspec, wspec, wspec]
        + [_ANY] * len(after),
        [wspec] * 4, [jax.ShapeDtypeStruct(w.shape, F32)] * 4, [c_all_t, dmod_cols, w, m, v, *after],
    )
    return outs


def _ffn_fwd(x, vec, w_in, w_out, name, phases=()):
    s, d = x.shape
    f = w_out.shape[1]
    tm = _pick(s, (1024, 512, 256, 128))
    tf = _pick(f, (256, 128))
    nf = f // tf

    def body(x_ref, vec_ref, wg_ref, wu_ref, wo_ref, xo_ref, g_ref, u_ref, y_ref, h_sc, acc_sc):
        j = pl.program_id(1)

        @pl.when(j == 0)
        def _():
            h_sc[...] = _modulate(x_ref[...], vec_ref).astype(BF16)
            acc_sc[...] = jnp.zeros_like(acc_sc)

        h = h_sc[...]
        g = _dot(h, wg_ref[0])
        u = _dot(h, wu_ref[0])
        g_ref[...] = g.astype(BF16)
        u_ref[...] = u.astype(BF16)
        a = (g * _sigmoid(g) * u).astype(BF16)
        acc_sc[...] += _dot(a, wo_ref[0])

        @pl.when(j == nf - 1)
        def _():
            yv = acc_sc[...]
            xo_ref[...] = x_ref[...] + 0.5 * vec_ref[3:4, :] * yv
            y_ref[...] = yv.astype(BF16)

    row = pl.BlockSpec((tm, d), lambda i, j: (i, 0))
    hid = pl.BlockSpec((tm, tf), lambda i, j: (i, j))
    return _call(
        body, name, (s // tm, nf),
        [
            row,
            pl.BlockSpec((8, d), lambda i, j: (0, 0)),
            pl.BlockSpec((1, d, tf), lambda i, j: (0, 0, j)),
            pl.BlockSpec((1, d, tf), lambda i, j: (0, 0, nf + j)),
            pl.BlockSpec((1, tf, d), lambda i, j: (0, j, 0)),
        ],
        [row, hid, hid, row],
        [
            jax.ShapeDtypeStruct((s, d), F32),
            jax.ShapeDtypeStruct((s, f), BF16),
            jax.ShapeDtypeStruct((s, f), BF16),
            jax.ShapeDtypeStruct((s, d), BF16),
        ],
        [x, vec, w_in, w_in, w_out],
        scratch=[pltpu.VMEM((tm, d), BF16), pltpu.VMEM((tm, d), F32)], phases=phases,
    )


def _ffn_bwd(dxo, x, vec, gg, uu, y, w_in, w_out, name, phases=()):
    s, d = x.shape
    f = w_out.shape[1]
    tm = _pick(s, (512, 256, 128))
    tf = _pick(f, (256, 128))
    nf = f // tf

    def body(dxo_ref, x_ref, vec_ref, g_ref, u_ref, y_ref, wg_ref, wu_ref, wo_ref,
             dx_ref, dg_ref, du_ref, a_ref, h_ref, dy_ref, dvec_ref, acc_sc):
        i, j = pl.program_id(0), pl.program_id(1)

        @pl.when((i == 0) & (j == 0))
        def _():
            dvec_ref[...] = jnp.zeros_like(dvec_ref)

        @pl.when(j == 0)
        def _():
            dxo_v = dxo_ref[...]
            dy_ref[...] = (0.5 * vec_ref[3:4, :] * dxo_v).astype(BF16)
            dvec_ref[3:4, :] += 0.5 * jnp.sum(dxo_v * y_ref[...].astype(F32), axis=0, keepdims=True)
            acc_sc[...] = jnp.zeros_like(acc_sc)

        da = _dot_nt(dy_ref[...], wo_ref[0])
        g = g_ref[...].astype(F32)
        u = u_ref[...].astype(F32)
        sig = _sigmoid(g)
        sl = g * sig
        a_ref[...] = (sl * u).astype(BF16)
        dg = (da * u * (sig * (1.0 + g * (1.0 - sig)))).astype(BF16)
        du = (da * sl).astype(BF16)
        dg_ref[...] = dg
        du_ref[...] = du
        acc_sc[...] += _dot_nt(dg, wg_ref[0]) + _dot_nt(du, wu_ref[0])

        @pl.when(j == nf - 1)
        def _():
            dx, h = _modulate_bwd(x_ref[...], acc_sc[...], vec_ref, dvec_ref)
            dx_ref[...] = dxo_ref[...] + dx
            h_ref[...] = h.astype(BF16)

    row = pl.BlockSpec((tm, d), lambda i, j: (i, 0))
    hid = pl.BlockSpec((tm, tf), lambda i, j: (i, j))
    vecs = pl.BlockSpec((8, d), lambda i, j: (0, 0))
    return _call(
        body, name, (s // tm, nf),
        [
            row, row, vecs, hid, hid, row,
            pl.BlockSpec((1, d, tf), lambda i, j: (0, 0, j)),
            pl.BlockSpec((1, d, tf), lambda i, j: (0, 0, nf + j)),
            pl.BlockSpec((1, tf, d), lambda i, j: (0, j, 0)),
        ],
        [row, hid, hid, hid, row, row, vecs],
        [
            jax.ShapeDtypeStruct((s, d), F32),
            jax.ShapeDtypeStruct((s, f), BF16),
            jax.ShapeDtypeStruct((s, f), BF16),
            jax.ShapeDtypeStruct((s, f), BF16),
            jax.ShapeDtypeStruct((s, d), BF16),
            jax.ShapeDtypeStruct((s, d), BF16),
            jax.ShapeDtypeStruct((8, d), F32),
        ],
        [dxo, x, vec, gg, uu, y, w_in, w_in, w_out],
        scratch=[pltpu.VMEM((tm, d), F32)], phases=phases,
    )


def _grad_half(a, bs, big, where, mine, recv, name, phases=()):
    s, k1 = a.shape
    n = bs[0].shape[1]
    groups = len(bs)
    rows_halved = big.h3 == 1
    assert rows_halved or groups == 1
    kk, nn = (k1 // 2, n) if rows_halved else (k1, n // 2)
    tk = _pick(kk, (1408, 1024, 512, 256, 128))
    tn = _pick(nn, (1408, 1024, 640, 512, 256, 128))
    nkb, nnb = kk // tk, nn // tn
    assert (recv is None) == (not mine)

    def half(pref):
        return pref[1] if mine else 1 - pref[1]

    def body(_, a_ref, *rest):
        q = pl.program_id(1)
        for p in range(groups):

            @pl.when(q == p)
            def _(p=p):
                acc = _dot_tn(a_ref[...], rest[p][...])
                if recv is not None:
                    acc = acc + rest[groups][0].astype(F32)
                rest[-1][0] = acc.astype(BF16)

    def b_block(p):
        def index(i, q, j, pref):
            jj = jnp.where(q == p, j, jnp.where(q < p, 0, nnb - 1))
            return (0, jj + (0 if rows_halved else half(pref) * nnb))

        return pl.BlockSpec((s, tn), index)

    out_spec = pl.BlockSpec((1, tk, tn), lambda i, q, j, pref: (0, i, q * nnb + j))
    in_specs = [pl.BlockSpec((s, tk), lambda i, q, j, pref: (0, i + (half(pref) * nkb if rows_halved else 0)))]
    in_specs += [b_block(p) for p in range(groups)]
    ins = [a, *bs]
    if recv is not None:
        in_specs.append(out_spec)
        ins.append(recv)
    return _call(
        body, name, (nkb, groups, nnb), in_specs, [out_spec], [jax.ShapeDtypeStruct(big.dims("half"), BF16)], ins,
        prefetch=(where,), phases=phases,
    )


def _proj_mod_fwd(x, vec, w, phases=()):
    s, d = x.shape
    n = w.shape[2]
    tm = _pick(s, (1024, 512, 256, 128))
    tn = _pick(n, (640, 512, 256, 128))

    def body(x_ref, vec_ref, w_ref, o_ref, h_sc):
        @pl.when(pl.program_id(1) == 0)
        def _():
            h_sc[...] = _modulate(x_ref[...], vec_ref).astype(BF16)

        o_ref[...] = _dot(h_sc[...], w_ref[0])

    return _call(
        body, "ab_in_fwd", (s // tm, n // tn),
        [
            pl.BlockSpec((tm, d), lambda i, j: (i, 0)),
            pl.BlockSpec((8, d), lambda i, j: (0, 0)),
            pl.BlockSpec((1, d, tn), lambda i, j: (0, 0, j)),
        ],
        [pl.BlockSpec((tm, tn), lambda i, j: (i, j))],
        [jax.ShapeDtypeStruct((s, n), F32)], [x, vec, w],
        scratch=[pltpu.VMEM((tm, d), BF16)], phases=phases,
    )


def _proj_res_fwd(a, w, x, vec, phases=()):
    s, kd = a.shape
    d = x.shape[1]
    tm = _pick(s, (1024, 512, 256, 128))

    def body(a_ref, w_ref, x_ref, vec_ref, xo_ref, y_ref):
        yv = _dot(a_ref[...], w_ref[0])
        xo_ref[...] = x_ref[...] + vec_ref[3:4, :] * yv
        y_ref[...] = yv.astype(BF16)

    row = pl.BlockSpec((tm, d), lambda i: (i, 0))
    return _call(
        body, "ab_out_fwd", (s // tm,),
        [pl.BlockSpec((tm, kd), lambda i: (i, 0)), pl.BlockSpec((1, kd, d), lambda i: (0, 0, 0)), row, pl.BlockSpec((8, d), lambda i: (0, 0))],
        [row, row],
        [jax.ShapeDtypeStruct((s, d), F32), jax.ShapeDtypeStruct((s, d), BF16)], [a, w, x, vec], phases=phases,
    )


def _proj_res_bwd(dxo, y, vec, w, phases=()):
    s, d = dxo.shape
    kd = w.shape[1]
    tm = _pick(s, (1024, 512, 256, 128))

    def body(dxo_ref, y_ref, vec_ref, w_ref, dy_ref, da_ref, dgate_ref):
        @pl.when(pl.program_id(0) == 0)
        def _():
            dgate_ref[...] = jnp.zeros_like(dgate_ref)

        dxo_v = dxo_ref[...]
        dy = (vec_ref[3:4, :] * dxo_v).astype(BF16)
        dy_ref[...] = dy
        dgate_ref[3:4, :] += jnp.sum(dxo_v * y_ref[...].astype(F32), axis=0, keepdims=True)
        da_ref[...] = _dot_nt(dy, w_ref[0]).astype(BF16)

    row = pl.BlockSpec((tm, d), lambda i: (i, 0))
    vecs = pl.BlockSpec((8, d), lambda i: (0, 0))
    return _call(
        body, "ab_out_bwd", (s // tm,),
        [row, row, vecs, pl.BlockSpec((1, kd, d), lambda i: (0, 0, 0))],
        [row, pl.BlockSpec((tm, kd), lambda i: (i, 0)), vecs],
        [jax.ShapeDtypeStruct((s, d), BF16), jax.ShapeDtypeStruct((s, kd), BF16), jax.ShapeDtypeStruct((8, d), F32)],
        [dxo, y, vec, w], phases=phases,
    )


def _proj_mod_bwd(dproj, w, x, vec, dxo, dvec_in, name, phases=()):
    parts, s, n_part = dproj.shape
    d = x.shape[1]
    tm = _pick(s, (512, 256, 128))
    tk = _pick(n_part, (1408, 1280, 1024, 512, 256, 128))
    per_part = n_part // tk
    nk = parts * per_part

    def body(dp_ref, w_ref, x_ref, vec_ref, dxo_ref, dvi_ref, dx_ref, h_ref, dvec_ref, acc_sc):
        i, k = pl.program_id(0), pl.program_id(1)

        @pl.when((i == 0) & (k == 0))
        def _():
            dvec_ref[...] = dvi_ref[...]

        @pl.when(k == 0)
        def _():
            acc_sc[...] = jnp.zeros_like(acc_sc)

        acc_sc[...] += _dot_nt(dp_ref[0], w_ref[0])

        @pl.when(k == nk - 1)
        def _():
            dx, h = _modulate_bwd(x_ref[...], acc_sc[...], vec_ref, dvec_ref)
            dx_ref[...] = dxo_ref[...] + dx
            h_ref[...] = h.astype(BF16)

    row = pl.BlockSpec((tm, d), lambda i, k: (i, 0))
    vecs = pl.BlockSpec((8, d), lambda i, k: (0, 0))
    return _call(
        body, name, (s // tm, nk),
        [
            pl.BlockSpec((1, tm, tk), lambda i, k: (k // per_part, i, k % per_part)),
            pl.BlockSpec((1, d, tk), lambda i, k: (0, 0, k)),
            row, vecs, row, vecs,
        ],
        [row, row, vecs],
        [jax.ShapeDtypeStruct((s, d), F32), jax.ShapeDtypeStruct((s, d), BF16), jax.ShapeDtypeStruct((8, d), F32)],
        [dproj, w, x, vec, dxo, dvec_in], scratch=[pltpu.VMEM((tm, d), F32)], phases=phases,
    )


def _tril(n):
    return lax.broadcasted_iota(jnp.int32, (n, n), 0) >= lax.broadcasted_iota(jnp.int32, (n, n), 1)


def _layernorm_stats(gv):
    mu = jnp.mean(gv, axis=-1, keepdims=True)
    cen = gv - mu
    rstd = lax.rsqrt(jnp.mean(cen * cen, axis=-1, keepdims=True) + EPS)
    return cen * rstd, rstd


def _shift_down(q, k, above_ref, c_cg, c_xb, first):
    width = q.shape[1]
    rows = lax.broadcasted_iota(jnp.int32, q.shape, 0)
    out = pltpu.roll(q, k, 0)
    for r in range(k):
        src = CONV_HALO - k + r
        above = above_ref[src : src + 1, c_cg : c_cg + width] * above_ref[src : src + 1, c_xb : c_xb + width]
        above = jnp.where(first, 0.0, above)
        out = jnp.where(rows == r, above, out)
    return out


def _ab_mix_fwd(proj, norm_v, w_s, b_rows, conv_w, phases=()):
    s, n = proj.shape
    heads, chunk, _ = w_s.shape
    da = norm_v.shape[1]
    hd = da // heads
    db = conv_w.shape[1]
    tm = _pick(s, (512, 256, 128))

    def body(p_ref, ph_ref, nv_ref, ws_ref, b_ref, cw_ref, o_ref):
        first = pl.program_id(0) == 0
        gu, _ = _gelu(p_ref[:, 0:da])
        gv, _ = _gelu(p_ref[:, da : 2 * da])
        xhat, _ = _layernorm_stats(gv)
        vn = (xhat * nv_ref[...]).astype(BF16)
        mask = _tril(chunk)
        for hh in range(heads):
            wm = jnp.where(mask, ws_ref[hh], 0.0).astype(BF16)
            cols = slice(hh * hd, (hh + 1) * hd)
            for nn in range(tm // chunk):
                rows = slice(nn * chunk, (nn + 1) * chunk)
                z = _dot(wm, vn[rows, cols]) + b_ref[:, cols]
                o_ref[rows, cols] = (gu[rows, cols] * z).astype(BF16)
        c_cg, c_xb = 2 * da + db, 2 * da + 2 * db
        bg = p_ref[:, 2 * da : 2 * da + db]
        q = p_ref[:, c_cg : c_cg + db] * p_ref[:, c_xb : c_xb + db]
        q1 = _shift_down(q, 1, ph_ref, c_cg, c_xb, first)
        q2 = _shift_down(q, 2, ph_ref, c_cg, c_xb, first)
        conv = cw_ref[0:1, :] * q2 + cw_ref[1:2, :] * q1 + cw_ref[2:3, :] * q
        o_ref[:, da : da + db] = (bg * conv).astype(BF16)

    nh = tm // CONV_HALO
    return _call(
        body, "ab_mix_fwd", (s // tm,),
        [
            pl.BlockSpec((tm, n), lambda i: (i, 0)),
            pl.BlockSpec((CONV_HALO, n), lambda i: (jnp.maximum(i * nh - 1, 0), 0)),
            pl.BlockSpec((1, da), lambda i: (0, 0)),
            pl.BlockSpec((heads, chunk, chunk), lambda i: (0, 0, 0)),
            pl.BlockSpec((chunk, da), lambda i: (0, 0)),
            pl.BlockSpec((3, db), lambda i: (0, 0)),
        ],
        [pl.BlockSpec((tm, da + db), lambda i: (i, 0))],
        [jax.ShapeDtypeStruct((s, da + db), BF16)], [proj, proj, norm_v, w_s, b_rows, conv_w], phases=phases,
    )


def _ab_mix_bwd(proj, dcat, norm_v, w_s, b_rows, conv_w, phases=()):
    s, n = proj.shape
    heads, chunk, _ = w_s.shape
    da = norm_v.shape[1]
    hd = da // heads
    db = conv_w.shape[1]
    tm = _pick(s, (512, 256, 128))
    nblk = s // tm
    dhalo = 2 * CONV_HALO

    def body(p_ref, pa_ref, pb_ref, dc_ref, dcb_ref, nv_ref, ws_ref, b_ref, cw_ref,
             dp_ref, dnv_ref, dws_ref, dzs_ref, dcw_ref, dvn_sc):
        i = pl.program_id(0)
        first, last = i == 0, i == nblk - 1

        @pl.when(first)
        def _():
            dnv_ref[...] = jnp.zeros_like(dnv_ref)
            dws_ref[...] = jnp.zeros_like(dws_ref)
            dzs_ref[...] = jnp.zeros_like(dzs_ref)
            dcw_ref[...] = jnp.zeros_like(dcw_ref)

        uu = p_ref[:, 0:da]
        gu, gu_grad = _gelu(uu)
        gv, gv_grad = _gelu(p_ref[:, da : 2 * da])
        xhat, rstd = _layernorm_stats(gv)
        nv = nv_ref[...]
        vn = (xhat * nv).astype(BF16)
        dya = dc_ref[:, 0:da].astype(F32)
        dz = (dya * gu).astype(BF16)
        mask = _tril(chunk)
        for hh in range(heads):
            wm = jnp.where(mask, ws_ref[hh], 0.0).astype(BF16)
            cols = slice(hh * hd, (hh + 1) * hd)
            dws = jnp.zeros((chunk, chunk), F32)
            for nn in range(tm // chunk):
                rows = slice(nn * chunk, (nn + 1) * chunk)
                z = _dot(wm, vn[rows, cols]) + b_ref[:, cols]
                dp_ref[rows, cols] = (dya[rows, cols] * z * gu_grad[rows, cols]).astype(BF16)
                dz_blk = dz[rows, cols]
                dws = dws + _dot_nt(dz_blk, vn[rows, cols])
                dzs_ref[:, cols] += dz_blk.astype(F32)
                dvn = _dot_tn(wm, dz_blk)
                dnv_ref[:, cols] += jnp.sum(dvn * xhat[rows, cols], axis=0, keepdims=True)
                dvn_sc[rows, cols] = dvn
            dws_ref[hh] += jnp.where(mask, dws, 0.0)
        dxhat = dvn_sc[...] * nv
        dgv = rstd * (dxhat - jnp.mean(dxhat, axis=-1, keepdims=True) - xhat * jnp.mean(dxhat * xhat, axis=-1, keepdims=True))
        dp_ref[:, da : 2 * da] = (dgv * gv_grad).astype(BF16)

        c_bg, c_cg, c_xb = 2 * da, 2 * da + db, 2 * da + 2 * db
        bg = p_ref[:, c_bg : c_bg + db]
        cg = p_ref[:, c_cg : c_cg + db]
        xb = p_ref[:, c_xb : c_xb + db]
        q = cg * xb
        q1 = _shift_down(q, 1, pa_ref, c_cg, c_xb, first)
        q2 = _shift_down(q, 2, pa_ref, c_cg, c_xb, first)
        dyb = dc_ref[:, da : da + db].astype(F32)
        conv = cw_ref[0:1, :] * q2 + cw_ref[1:2, :] * q1 + cw_ref[2:3, :] * q
        dp_ref[:, c_bg : c_bg + db] = (dyb * conv).astype(BF16)
        e = dyb * bg
        dcw_ref[0:1, :] += jnp.sum(e * q2, axis=0, keepdims=True)
        dcw_ref[1:2, :] += jnp.sum(e * q1, axis=0, keepdims=True)
        dcw_ref[2:3, :] += jnp.sum(e * q, axis=0, keepdims=True)
        rows = lax.broadcasted_iota(jnp.int32, e.shape, 0)
        dq = cw_ref[2:3, :] * e
        for kk in (1, 2):
            ek = pltpu.roll(e, tm - kk, 0)
            for r in range(kk):
                below = dcb_ref[r : r + 1, da : da + db].astype(F32) * pb_ref[r : r + 1, c_bg : c_bg + db]
                below = jnp.where(last, 0.0, below)
                ek = jnp.where(rows == tm - kk + r, below, ek)
            dq = dq + cw_ref[2 - kk : 3 - kk, :] * ek
        dp_ref[:, c_cg : c_cg + db] = (dq * xb).astype(BF16)
        dp_ref[:, c_xb : c_xb + db] = (dq * cg).astype(BF16)

    nh = tm // CONV_HALO
    nhb = tm // dhalo
    const2 = lambda i: (0, 0)
    return _call(
        body, "ab_mix_bwd", (nblk,),
        [
            pl.BlockSpec((tm, n), lambda i: (i, 0)),
            pl.BlockSpec((CONV_HALO, n), lambda i: (jnp.maximum(i * nh - 1, 0), 0)),
            pl.BlockSpec((CONV_HALO, n), lambda i: (jnp.minimum((i + 1) * nh, s // CONV_HALO - 1), 0)),
            pl.BlockSpec((tm, da + db), lambda i: (i, 0)),
            pl.BlockSpec((dhalo, da + db), lambda i: (jnp.minimum((i + 1) * nhb, s // dhalo - 1), 0)),
            pl.BlockSpec((1, da), const2),
            pl.BlockSpec((heads, chunk, chunk), lambda i: (0, 0, 0)),
            pl.BlockSpec((chunk, da), const2),
            pl.BlockSpec((3, db), const2),
        ],
        [
            pl.BlockSpec((tm, n), lambda i: (i, 0)),
            pl.BlockSpec((1, da), const2),
            pl.BlockSpec((heads, chunk, chunk), lambda i: (0, 0, 0)),
            pl.BlockSpec((chunk, da), const2),
            pl.BlockSpec((3, db), const2),
        ],
        [
            jax.ShapeDtypeStruct((s, n), BF16),
            jax.ShapeDtypeStruct((1, da), F32),
            jax.ShapeDtypeStruct((heads, chunk, chunk), F32),
            jax.ShapeDtypeStruct((chunk, da), F32),
            jax.ShapeDtypeStruct((3, db), F32),
        ],
        [proj, proj, proj, dcat, dcat, norm_v, w_s, b_rows, conv_w],
        scratch=[pltpu.VMEM((tm, da), F32)], phases=phases,
    )


def _pool_counts(tm, i, w):
    t = i * tm + lax.broadcasted_iota(jnp.int32, (tm, 1), 0)
    return jnp.minimum(t + 1, w).astype(F32)


def _pool_fwd(x, vec, w_grp, scale, phases=()):
    s, d = x.shape
    groups, gd, _ = w_grp.shape
    tm = _pick(s, (512, 256, 128))

    def body(x_ref, xa_ref, vec_ref, w_ref, sc_ref, xo_ref, p_ref, o_ref):
        i = pl.program_id(0)
        h = _modulate(x_ref[...], vec_ref)
        ha = jnp.where(i == 0, 0.0, _modulate(xa_ref[...], vec_ref))
        ext = jnp.concatenate([ha, h], axis=0)
        for gi, w in enumerate(POOL_WINDOWS):
            cols = slice(gi * gd, (gi + 1) * gd)
            acc = ext[:, cols]
            step = 1
            while step < w:
                acc = acc + pltpu.roll(acc, step, 0)
                step *= 2
            p = (acc[POOL_HALO:, :] / _pool_counts(tm, i, w) - h[:, cols]).astype(BF16)
            p_ref[:, cols] = p
            o_ref[:, cols] = _dot(p, w_ref[gi]).astype(BF16)
        xo_ref[...] = x_ref[...] + vec_ref[3:4, :] * (o_ref[...].astype(F32) * sc_ref[...])

    nh = tm // POOL_HALO
    row = pl.BlockSpec((tm, d), lambda i: (i, 0))
    return _call(
        body, "pool_fwd", (s // tm,),
        [
            row,
            pl.BlockSpec((POOL_HALO, d), lambda i: (jnp.maximum(i * nh - 1, 0), 0)),
            pl.BlockSpec((8, d), lambda i: (0, 0)),
            pl.BlockSpec((groups, gd, gd), lambda i: (0, 0, 0)),
            pl.BlockSpec((1, d), lambda i: (0, 0)),
        ],
        [row, row, row],
        [jax.ShapeDtypeStruct((s, d), F32), jax.ShapeDtypeStruct((s, d), BF16), jax.ShapeDtypeStruct((s, d), BF16)],
        [x, x, vec, w_grp, scale], phases=phases,
    )


def _pool_bwd(dxo, x, vec, p, o, w_grp, scale, phases=()):
    s, d = x.shape
    groups, gd, _ = w_grp.shape
    tm = _pick(s, (512, 256, 128))
    nblk = s // tm

    def body(dxo_ref, dxb_ref, x_ref, vec_ref, p_ref, o_ref, w_ref, sc_ref, dx_ref, dw_ref, dsc_ref, dvec_ref, dw_sc):
        i = pl.program_id(0)

        @pl.when(i == 0)
        def _():
            dw_sc[...] = jnp.zeros_like(dw_sc)
            dsc_ref[...] = jnp.zeros_like(dsc_ref)
            dvec_ref[...] = jnp.zeros_like(dvec_ref)

        gate, sc = vec_ref[3:4, :], sc_ref[...]
        dxo_v = dxo_ref[...]
        ov = o_ref[...].astype(F32)
        dvec_ref[3:4, :] += jnp.sum(dxo_v * (ov * sc), axis=0, keepdims=True)
        dy = gate * dxo_v
        dsc_ref[...] += jnp.sum(dy * ov, axis=0, keepdims=True)
        dout = (dy * sc).astype(BF16)
        dout_b = jnp.where(i == nblk - 1, 0.0, gate * dxb_ref[...] * sc).astype(BF16)
        for gi, w in enumerate(POOL_WINDOWS):
            cols = slice(gi * gd, (gi + 1) * gd)
            dw_sc[gi] += _dot_tn(p_ref[:, cols], dout[:, cols])
            wb = w_ref[gi]
            dp = _dot_nt(dout[:, cols], wb)
            dp_b = _dot_nt(dout_b[:, cols], wb)
            e = dp / _pool_counts(tm, i, w)
            t_below = (i + 1) * tm + lax.broadcasted_iota(jnp.int32, (POOL_HALO, 1), 0)
            e_b = dp_b / jnp.minimum(t_below + 1, w).astype(F32)
            acc = jnp.concatenate([e, e_b], axis=0)
            step = 1
            while step < w:
                acc = acc + pltpu.roll(acc, tm + POOL_HALO - step, 0)
                step *= 2
            dx_ref[:, cols] = acc[:tm, :] - dp
        dx, _ = _modulate_bwd(x_ref[...], dx_ref[...], vec_ref, dvec_ref)
        dx_ref[...] = dxo_v + dx

        @pl.when(i == nblk - 1)
        def _():
            dw_ref[...] = dw_sc[...].astype(BF16)

    nh = tm // POOL_HALO
    row = pl.BlockSpec((tm, d), lambda i: (i, 0))
    vecs = pl.BlockSpec((8, d), lambda i: (0, 0))
    wspec = pl.BlockSpec((groups, gd, gd), lambda i: (0, 0, 0))
    return _call(
        body, "pool_bwd", (nblk,),
        [
            row,
            pl.BlockSpec((POOL_HALO, d), lambda i: (jnp.minimum((i + 1) * nh, s // POOL_HALO - 1), 0)),
            row, vecs, row, row, wspec,
            pl.BlockSpec((1, d), lambda i: (0, 0)),
        ],
        [row, wspec, pl.BlockSpec((1, d), lambda i: (0, 0)), vecs],
        [
            jax.ShapeDtypeStruct((s, d), F32),
            jax.ShapeDtypeStruct((groups, gd, gd), BF16),
            jax.ShapeDtypeStruct((1, d), F32),
            jax.ShapeDtypeStruct((8, d), F32),
        ],
        [dxo, dxo, x, vec, p, o, w_grp, scale],
        scratch=[pltpu.VMEM((groups, gd, gd), F32)], phases=phases,
    )


def _loss_head(x, gain, target, phases=()):
    s, d = x.shape
    tm = _pick(s, (512, 256, 128))

    def body(x_ref, g_ref, t_ref, dx_ref, aux_ref):
        @pl.when(pl.program_id(0) == 0)
        def _():
            aux_ref[...] = jnp.zeros_like(aux_ref)

        xv = x_ref[...]
        rstd = _rstd(xv)
        r = xv * rstd
        gain_v = g_ref[...]
        err = r * gain_v - t_ref[...]
        aux_ref[1:2, :] += jnp.sum(err * err, axis=0, keepdims=True)
        dout = err * (1.0 / d)
        aux_ref[0:1, :] += jnp.sum(dout * r, axis=0, keepdims=True)
        dr = dout * gain_v
        dx_ref[...] = rstd * (dr - r * jnp.mean(dr * r, axis=-1, keepdims=True))

    row = pl.BlockSpec((tm, d), lambda i: (i, 0))
    return _call(
        body, "loss_head", (s // tm,),
        [row, pl.BlockSpec((1, d), lambda i: (0, 0)), row],
        [row, pl.BlockSpec((8, d), lambda i: (0, 0))],
        [jax.ShapeDtypeStruct((s, d), F32), jax.ShapeDtypeStruct((8, d), F32)], [x, gain, target], phases=phases,
    )


def _small_adam(gathered, gathered_ws, layout, smalls, chip):
    names = list(smalls)
    n = len(names)
    loss_row, _, _, n_feat = layout["loss"]

    def body(*refs):
        chip_ref, g_ref, gws_ref = refs[0], refs[1], refs[2]
        wmv = refs[3 : 3 + 3 * n]
        outs = refs[3 + 3 * n : 3 + 7 * n]
        total = refs[-1]
        total[...] = g_ref[0]
        for kdev in range(1, N_DEV):
            total[...] += g_ref[kdev]
        total_ws = gws_ref[0]
        for kdev in range(1, N_DEV):
            total_ws = total_ws + gws_ref[kdev]
        my_chip = chip_ref[0]
        for a, name in enumerate(names):
            w_ref, m_ref, v_ref = wmv[3 * a : 3 * a + 3]
            if name == "ab_w_s":
                g = total_ws
            else:
                row0, rows, col0, cols = layout[name]
                if col0 is None:
                    g = jnp.zeros((rows, cols), F32)
                    for j in range(N_CHIPS):
                        g = g + jnp.where(my_chip == j, total[row0 : row0 + rows, j * cols : (j + 1) * cols], 0.0)
                else:
                    g = total[row0 : row0 + rows, col0 : col0 + cols]
            dl, mo, vo = _adam(w_ref[...], g, m_ref[...], v_ref[...])
            outs[4 * a][...] = g
            outs[4 * a + 1][...] = dl
            outs[4 * a + 2][...] = mo
            outs[4 * a + 3][...] = vo
        refs[3 + 7 * n][...] = 0.5 * jnp.sum(total[loss_row : loss_row + 1, 0:n_feat], axis=1, keepdims=True) / n_feat

    ins = [gathered, gathered_ws]
    out_shapes = []
    for name in names:
        ins.extend(smalls[name])
        out_shapes.extend([jax.ShapeDtypeStruct(smalls[name][0].shape, F32)] * 4)
    out_shapes.append(jax.ShapeDtypeStruct((1, 1), F32))
    whole = lambda shape: pl.BlockSpec(shape, functools.partial(lambda nd, i, c: (0,) * nd, len(shape)))
    res = pl.pallas_call(
        body, name="small_adam",
        grid_spec=pltpu.PrefetchScalarGridSpec(
            num_scalar_prefetch=1, grid=(1,),
            in_specs=[whole(a.shape) for a in ins], out_specs=[whole(o.shape) for o in out_shapes],
            scratch_shapes=[pltpu.VMEM(gathered.shape[1:], F32)],
        ),
        out_shape=out_shapes,
        compiler_params=pltpu.CompilerParams(dimension_semantics=("arbitrary",), vmem_limit_bytes=VMEM_LIMIT_BYTES),
    )(chip.reshape(1).astype(jnp.int32), *ins)
    return {name: res[4 * a : 4 * a + 4] for a, name in enumerate(names)}, res[4 * n]


def _pad_rows(a, rows=8):
    extra = (-a.shape[0]) % rows
    return jnp.pad(a, ((0, extra), (0, 0))) if extra else a


def _pad_cols(a, cols):
    return jnp.pad(a, ((0, 0), (0, cols - a.shape[1]))) if a.shape[1] < cols else a


def _run(fn, *phases):
    outs, p_outs = fn(list(phases))
    for p, po in zip(phases, p_outs):
        p.then(po)
    return outs


def kernel(x, c, norm_g, w_mod, b_mod, w_ffn_in, w_ffn_out, ab_w_in, ab_norm_v, ab_w_s, ab_b_s, ab_conv_w, ab_w_out, pool_w_grp, pool_scale, final_g, loss_target, m_norm_g, m_w_mod, m_b_mod, m_w_ffn_in, m_w_ffn_out, m_ab_w_in, m_ab_norm_v, m_ab_w_s, m_ab_b_s, m_ab_conv_w, m_ab_w_out, m_pool_w_grp, m_pool_scale, m_final_g, v_norm_g, v_w_mod, v_b_mod, v_w_ffn_in, v_w_ffn_out, v_ab_w_in, v_ab_norm_v, v_ab_w_s, v_ab_b_s, v_ab_conv_w, v_ab_w_out, v_pool_w_grp, v_pool_scale, v_final_g):
    ix, iy, ic = _place()
    chip = 2 * ix + iy
    me = 4 * ix + 2 * iy + ic
    where = jnp.stack([chip, ic]).astype(jnp.int32)
    s, d = x.shape[1], x.shape[2]
    x0 = x.reshape(s, d)
    target = loss_target.reshape(s, d)
    n_layers = norm_g.shape[0]
    dq = d // N_CHIPS
    heads, chunk = ab_w_s.shape[1], ab_w_s.shape[2]
    da = ab_norm_v.shape[1]
    db = ab_conv_w.shape[2] * N_CHIPS
    f_hidden = w_ffn_out.shape[2] * N_CHIPS
    assert n_layers == 2 and da % heads == 0

    cw_pad = _pad_cols(ab_conv_w.reshape(3, db // N_CHIPS), dq)
    packed = jnp.concatenate(
        [_pad_rows(c.reshape(N_CHIPS, dq)), _pad_rows(norm_g.reshape(-1, dq)), _pad_rows(pool_scale.reshape(1, dq)), _pad_rows(cw_pad)],
        axis=0,
    )
    ncol = w_mod.shape[2]
    b_cols = lax.dynamic_slice(b_mod, (0, chip * ncol), (n_layers, ncol)).reshape(n_layers, 1, ncol)
    small = {}

    def small_gather(key, arrs):
        def then(outs):
            small[key] = outs

        return _phase_small_gather(arrs, then)

    stacks = {
        "w_ffn_in": tuple(a.reshape((-1,) + a.shape[2:]) for a in (w_ffn_in, m_w_ffn_in, v_w_ffn_in)),
        "w_ffn_out": tuple(a.reshape((-1,) + a.shape[2:]) for a in (w_ffn_out, m_w_ffn_out, v_w_ffn_out)),
        "ab_w_in": (ab_w_in, m_ab_w_in, v_ab_w_in),
        "ab_w_out": (ab_w_out, m_ab_w_out, v_ab_w_out),
        "pool_w_grp": (pool_w_grp[0], m_pool_w_grp[0], v_pool_w_grp[0]),
    }
    big_in = _Big((1, d, 2 * f_hidden), 2, 1)
    big_out = _Big((1, f_hidden, d), 1, 2)
    units = {}
    for l in range(n_layers):
        for k in range(2):
            units[f"in{l}{k}"] = (big_in, "w_ffn_in", 2 * l + k)
            units[f"out{l}{k}"] = (big_out, "w_ffn_out", 2 * l + k)
    units["abin"] = (_Big((1, d, ab_w_in.shape[2] * N_CHIPS), 2, 1), "ab_w_in", 0)
    units["about"] = (_Big((1, ab_w_out.shape[1] * N_CHIPS, d), 1, 2), "ab_w_out", 0)
    units["pool"] = (_Big((pool_w_grp.shape[1], pool_w_grp.shape[2] * N_CHIPS, pool_w_grp.shape[3]), 1, 0), "pool_w_grp", 0)
    big = {u: g for u, (g, _, _) in units.items()}

    weight = {}
    complete = set()

    def cast(u):
        g, st, b0 = units[u]

        def launch(phases):
            (weight[u],), p_outs = _cast_into_full(stacks[st][0], b0, g, where, "cast_" + u, phases)
            return None, p_outs

        return launch

    def gather_ici(*us):
        def then(outs):
            for u, o in zip(us, outs):
                weight[u] = o

        return _phase_gather_ici([weight[u] for u in us], [big[u] for u in us], then)

    def gather_sibling(*us):
        def then(outs):
            for u, o in zip(us, outs):
                weight[u] = o
                complete.add(u)

        return _phase_gather_sibling([weight[u] for u in us], [big[u] for u in us], then)

    def w_of(u):
        assert u in complete, u
        return weight[u]

    _run(cast("in00"), small_gather("inputs", [packed]))
    small_all = small["inputs"][0]
    by_chip = small_all[0::2]
    c_all = small_all[:, 0:N_CHIPS, :].reshape(N_DEV, d)
    norm_full = by_chip[:, 8 : 8 + 3 * n_layers, :].transpose(1, 0, 2).reshape(3 * n_layers, d)
    pool_scale_full = by_chip[:, 16:17, :].transpose(1, 0, 2).reshape(1, d)
    conv_full = by_chip[:, 24:27, : db // N_CHIPS].transpose(1, 0, 2).reshape(3, db)
    pieces = [("in00", "out00"), ("abin", "about"), ("in01", "out01"), ("in10", "out10", "pool"), ("in11", "out11")]
    in_flight = {}

    def start_gather(p):
        in_flight[p] = _split_start(gather_ici(*pieces[p]), f"gather_{p}_start")

    def started():
        return _after(*[flight.token for flight in in_flight.values()])

    def finish_gather(p, after, meanwhile=None):
        flight = in_flight.pop(p)
        _split_wait(flight, list(after) + list(started().ins), f"gather_{p}_wait")
        crossing = _split_start(gather_sibling(*pieces[p]), f"gather_{p}_forward")
        behind = [crossing.token]
        if p + 2 < len(pieces):
            for u in pieces[p + 2]:
                _run(cast(u), _after(crossing.token))
            start_gather(p + 2)
            behind = list(started().ins)
        if meanwhile is not None:
            behind = behind + meanwhile(_after(crossing.token))
        _split_wait(crossing, behind, f"gather_{p}_forwarded")

    mod_cols = _run(lambda phases: _mod_fwd(c_all, w_mod, b_cols, phases))[0]
    def mod_rows(outs):
        small["mod"] = outs

    _run(cast("out00"), _phase_small_exchange(mod_cols.transpose(1, 0, 2), mod_rows))
    start_gather(0)
    _run(cast("abin"), started())
    _run(cast("about"), started())
    start_gather(1)
    mod_mine = small["mod"][0][0::2]
    mod = mod_mine.transpose(1, 0, 2).reshape(n_layers, 3, 3, d)
    vecs = {
        (l, sub): jnp.pad(norm_full[3 * l + sub][None], ((0, 7), (0, 0))) + jnp.pad(mod[l, sub], ((1, 4), (0, 0)))
        for l in range(n_layers)
        for sub in range(3)
    }
    b_rows = jnp.broadcast_to(ab_b_s[0].T[:, :, None], (chunk, heads, da // heads)).reshape(chunk, da)

    saved = {}

    def ffn_forward(xs, l, sub, k, *phases):
        saved[l, sub, "x"] = xs
        xs, gg, uu, yb = _run(
            lambda ph: _ffn_fwd(xs, vecs[l, sub], w_of(f"in{l}{k}"), w_of(f"out{l}{k}"), f"ffn_fwd_{l}{k}", ph), *phases
        )
        saved[l, sub, "act"] = (gg, uu, yb)
        return xs

    finish_gather(0, [vecs[0, 0]])
    xs = ffn_forward(x0, 0, 0, 0, started())
    saved[0, 1, "x"] = xs
    finish_gather(1, [xs])
    (proj,) = _run(lambda ph: _proj_mod_fwd(xs, vecs[0, 1], w_of("abin"), ph), started())
    (cat,) = _run(lambda ph: _ab_mix_fwd(proj, ab_norm_v, ab_w_s[0], b_rows, conv_full, ph))
    xs, yb = _run(lambda ph: _proj_res_fwd(cat, w_of("about"), xs, vecs[0, 1], ph))
    saved[0, 1, "act"] = (proj, cat, yb)
    finish_gather(2, [xs])
    xs = ffn_forward(xs, 0, 2, 1, started())
    finish_gather(3, [xs])
    xs = ffn_forward(xs, 1, 0, 0, started())
    saved[1, 1, "x"] = xs
    pooled = []

    def pool_forward(behind):
        pooled.extend(_run(lambda ph: _pool_fwd(xs, vecs[1, 1], w_of("pool"), pool_scale_full, ph), behind))
        return [pooled[0]]

    finish_gather(4, [xs], pool_forward)
    xs, pp, oo = pooled
    saved[1, 1, "act"] = (pp, oo)
    xs = ffn_forward(xs, 1, 2, 1)
    dxs, aux = _run(lambda ph: _loss_head(xs, final_g.reshape(1, d), target, ph))

    grad = {}
    recv = {}
    csum = {}
    parts = {}
    reduced = {}
    done = set()
    dvecs, small_g = {}, {}

    def pair_exchange(*us):
        def then(outs):
            for u, o in zip(us, outs):
                recv[u] = o

        return _phase_pair_exchange([grad[u] for u in us], [big[u] for u in us], then)

    def grad_half(u, a, bs, mine, name, *phases):
        (res,) = _run(lambda ph: _grad_half(a, bs, big[u], where, mine, recv[u] if mine else None, name, ph), *phases)
        return res

    def pair_sum(u, *phases):
        def launch(ph):
            (csum[u],), p_outs = _pair_sum(grad[u], recv[u], big[u], where, "pair_sum_" + u, ph)
            return None, p_outs

        _run(launch, *phases)

    def chip_exchange(*us):
        def then(outs):
            for u, o in zip(us, outs):
                parts[u] = o

        return _phase_chip_exchange([csum[u] for u in us], [big[u] for u in us], then)

    def chip_sum(*us, carried=()):
        for n_u, u in enumerate(us):
            g, st, b0 = units[u]

            def launch(ph):
                (reduced[st],), p_outs = _chip_sum(
                    csum[u], parts[u], g, where, reduced.get(st), stacks[st][0].shape, b0, "chip_sum_" + u, ph
                )
                return None, p_outs

            _run(launch, *(carried if n_u == 0 else ()))

    def pair_broadcast(*us):
        sts = [units[u][1] for u in us]
        assert len(set(sts)) == len(sts)

        def then(outs):
            for u, st, o in zip(us, sts, outs):
                reduced[st] = o
                done.add(u)

        return _phase_pair_broadcast([reduced[st] for st in sts], [big[u] for u in us], [units[u][2] for u in us], then)

    def ffn_backward(dxs, l, sub, k, carried_bwd, carried_send, carried_mine):
        gg, uu, yb = saved[l, sub, "act"]
        w_in, w_out = w_of(f"in{l}{k}"), w_of(f"out{l}{k}")
        uo, ui, tag = f"out{l}{k}", f"in{l}{k}", f"{l}{k}"
        dxs, dg, du, a, h, dy, dvecs[l, sub] = _run(
            lambda ph: _ffn_bwd(dxs, saved[l, sub, "x"], vecs[l, sub], gg, uu, yb, w_in, w_out, "ffn_bwd_" + tag, ph), *carried_bwd()
        )
        grad[uo] = grad_half(uo, a, [dy], False, "dw_out_send_" + tag, *carried_send())
        grad[ui] = grad_half(ui, h, [dg, du], False, "dw_in_send_" + tag, pair_exchange(uo))
        csum[uo] = grad_half(uo, a, [dy], True, "dw_out_" + tag, pair_exchange(ui))
        csum[ui] = grad_half(ui, h, [dg, du], True, "dw_in_" + tag, *carried_mine())
        return dxs

    none = lambda: ()
    dxs = ffn_backward(dxs, 1, 2, 1, none, none, none)
    pp, oo = saved[1, 1, "act"]
    dxs, grad["pool"], small_g["pool_scale"], dvecs[1, 1] = _run(
        lambda ph: _pool_bwd(dxs, saved[1, 1, "x"], vecs[1, 1], pp, oo, w_of("pool"), pool_scale_full, ph)
    )

    def after_11():
        return (chip_exchange("in11", "out11"), pair_exchange("pool"))

    def bcast_11():
        chip_sum("in11", "out11")
        pair_sum("pool")
        return (pair_broadcast("in11", "out11"), chip_exchange("pool"))

    dxs = ffn_backward(dxs, 1, 0, 0, after_11, bcast_11, none)

    def after_10():
        return (chip_exchange("in10", "out10"),)

    def bcast_10():
        chip_sum("in10", "out10", "pool")
        return (pair_broadcast("in10", "out10", "pool"),)

    dxs = ffn_backward(dxs, 0, 2, 1, after_10, bcast_10, none)

    proj, cat, yb = saved[0, 1, "act"]
    out01 = _split_start(chip_exchange("out01"), "reduce_out01_start")
    dy, dcat, dgate = _run(lambda ph: _proj_res_bwd(dxs, yb, vecs[0, 1], w_of("about"), ph), _after(out01.token))
    grad["about"] = grad_half("about", cat, [dy], False, "dw_ab_out_send")
    dproj, small_g["ab_norm_v"], small_g["ab_w_s"], dzs, small_g["ab_conv_w"] = _run(
        lambda ph: _ab_mix_bwd(proj, dcat, ab_norm_v, ab_w_s[0], b_rows, conv_full, ph), pair_exchange("about")
    )
    small_g["ab_b_s"] = dzs.reshape(chunk, heads, da // heads).sum(axis=2).T
    dxs, h, dvecs[0, 1] = _run(
        lambda ph: _proj_mod_bwd(dproj[None], w_of("abin"), saved[0, 1, "x"], vecs[0, 1], dxs, dgate, "ab_in_bwd", ph)
    )
    grad["abin"] = grad_half("abin", h, [dproj], False, "dw_ab_in_send")
    (csum["out01"],) = _split_wait(out01, [grad["abin"]], "reduce_out01_wait")
    chip_sum("out01", carried=(pair_exchange("abin"),))
    csum["about"] = grad_half("about", cat, [dy], True, "dw_ab_out", pair_broadcast("out01"))
    csum["abin"] = grad_half("abin", h, [dproj], True, "dw_ab_in")

    layout = {}
    tail = {}

    def after_01():
        tail["01"] = _split_start(chip_exchange("in01", "abin", "about"), "reduce_01_start")
        return (_after(tail["01"].token),)

    def pack_small_grads():
        dvec_all = jnp.stack([dvecs[l, sub] for l in range(n_layers) for sub in range(3)])
        dgain = dvec_all[:, 0, :]
        dmod = dvec_all[:, 1:4, :].reshape(3 * 3 * n_layers, d)
        rows = {
            "norm_g": (dgain, None, dq), "final_g": (aux[0:1], 0, d), "pool_scale": (small_g["pool_scale"], None, dq),
            "b_mod": (dmod, 0, d), "ab_norm_v": (small_g["ab_norm_v"], 0, da),
            "ab_conv_w": (small_g["ab_conv_w"], None, db // N_CHIPS), "ab_b_s": (small_g["ab_b_s"], 0, chunk),
            "loss": (aux[1:2], 0, d),
        }
        row0 = 0
        for nm, (pc, col0, cols) in rows.items():
            layout[nm] = (row0, pc.shape[0], col0, cols)
            row0 += pc.shape[0]
        packed_rows = -(-row0 // 8) * 8
        return sum(
            jnp.pad(pc, ((layout[nm][0], packed_rows - layout[nm][0] - pc.shape[0]), (0, d - pc.shape[1])))
            for nm, (pc, _, _) in rows.items()
        )

    def bcast_01():
        csum["in01"], csum["abin"], csum["about"] = _split_wait(tail["01"], [dvecs[0, 0]], "reduce_01_wait")
        chip_sum("in01", "abin", "about")
        grads_small = [pack_small_grads(), small_g["ab_w_s"].reshape(heads * chunk, chunk)]
        tail["small"] = _split_start(small_gather("grads", grads_small), "gather_small_grads_start")
        return (pair_broadcast("in01", "abin", "about"), _after(tail["small"].token))

    def reduce_out00():
        tail["out00"] = _split_start(chip_exchange("out00"), "reduce_out00_start")
        return (_after(tail["out00"].token),)

    dxs = ffn_backward(dxs, 0, 0, 0, after_01, bcast_01, reduce_out00)
    grad_x = dxs.reshape(x.shape)

    last = _split_start(chip_exchange("in00"), "reduce_last_start")
    (csum["out00"],) = _split_wait(tail["out00"], [last.token], "reduce_out00_wait")
    chip_sum("out00")
    _flush("broadcast_out00", pair_broadcast("out00"))
    _split_wait(tail["small"], [reduced["w_ffn_out"]], "gather_small_grads_wait")
    g_all, gws_all = small["grads"]

    out = {}

    def adam_stack(st, after=()):
        w3, m3, v3 = stacks[st]
        assert all(u in done for u, (_, ust, _) in units.items() if ust == st), st
        shape = {"w_ffn_in": w_ffn_in.shape, "w_ffn_out": w_ffn_out.shape, "pool_w_grp": pool_w_grp.shape}.get(st, w3.shape)
        out[st] = tuple(a.reshape(shape) for a in _adam_stack(w3, reduced[st], m3, v3, "adam_" + st, after))

    for st in ("w_ffn_out", "ab_w_in", "ab_w_out", "pool_w_grp"):
        adam_stack(st, (last.token,))

    shapes2d = {
        "norm_g": (3 * n_layers, dq), "b_mod": (9 * n_layers, d), "final_g": (1, d), "ab_norm_v": (1, da),
        "pool_scale": (1, dq), "ab_conv_w": (3, db // N_CHIPS), "ab_b_s": (heads, chunk), "ab_w_s": (heads * chunk, chunk),
    }
    small_w = {"norm_g": (norm_g, m_norm_g, v_norm_g), "b_mod": (b_mod, m_b_mod, v_b_mod), "final_g": (final_g, m_final_g, v_final_g),
               "ab_norm_v": (ab_norm_v, m_ab_norm_v, v_ab_norm_v), "pool_scale": (pool_scale, m_pool_scale, v_pool_scale),
               "ab_conv_w": (ab_conv_w, m_ab_conv_w, v_ab_conv_w), "ab_b_s": (ab_b_s, m_ab_b_s, v_ab_b_s), "ab_w_s": (ab_w_s, m_ab_w_s, v_ab_w_s)}
    smalls = {nm: tuple(a.reshape(shapes2d[nm]) for a in wmv) for nm, wmv in small_w.items()}
    small_out, loss = _small_adam(g_all, gws_all, layout, smalls, chip)
    loss = loss.reshape(())
    for nm, res in small_out.items():
        out[nm] = tuple(a.reshape(small_w[nm][0].shape) for a in res)

    mod_row0 = layout["b_mod"][0]
    dmod_all = g_all[:, mod_row0 : mod_row0 + 9 * n_layers, :].reshape(N_DEV, n_layers, 9 * d)
    dmod_cols = lax.dynamic_slice(dmod_all, (0, 0, chip * ncol), (N_DEV, n_layers, ncol)).transpose(1, 0, 2)
    out["w_mod"] = tuple(_mod_bwd_adam(c_all.T, dmod_cols, w_mod, m_w_mod, v_w_mod, (last.token,)))

    (csum["in00"],) = _split_wait(
        last, [out[st][1] for st in ("w_mod", "w_ffn_out", "ab_w_in", "ab_w_out", "pool_w_grp")], "reduce_last_wait"
    )
    chip_sum("in00")
    _flush("broadcast_last", pair_broadcast("in00"))
    adam_stack("w_ffn_in")

    order = ["norm_g", "w_mod", "b_mod", "w_ffn_in", "w_ffn_out", "ab_w_in", "ab_norm_v", "ab_w_s", "ab_b_s", "ab_conv_w", "ab_w_out", "pool_w_grp", "pool_scale", "final_g"]
    return (loss, grad_x, *[out[nm][0] for nm in order], *[out[nm][1] for nm in order], *[out[nm][2] for nm in order], *[out[nm][3] for nm in order])
```

```python
import functools
import math

import jax
import jax.numpy as jnp
from jax import lax
from jax.experimental import pallas as pl
from jax.experimental.pallas import tpu as pltpu

F32 = jnp.float32
BF16 = jnp.bfloat16
MESH = pl.DeviceIdType.MESH

EPS = 1e-6
ADAM_LR = 0.001
ADAM_B1 = 0.9
ADAM_B2 = 0.999
ADAM_EPS = 1e-08
ADAM_WD = 0.01
ADAM_STEP = 10
POOL_WINDOWS = (2, 4, 8, 16)
POOL_HALO = 16
CONV_HALO = 8
N_CHIPS = 4
N_DEV = 8
VMEM_LIMIT_BYTES = 48 * 1024 * 1024
EW_BLOCK_ELEMS = 1024 * 1024
ADAM_BLOCK_ELEMS = 512 * 1024


def _pick(n, prefs):
    for p in prefs:
        if p <= n and n % p == 0:
            return p
    return n


def _row_tile(rows, cols, block_elems=EW_BLOCK_ELEMS):
    best = None
    for d in range(16, rows + 1, 16):
        if rows % d == 0 and d * cols <= block_elems:
            best = d
    return best or rows


def _dot(a, b):
    return jnp.dot(a, b, preferred_element_type=F32)


def _dot_nt(a, b):
    return lax.dot_general(a, b, (((1,), (1,)), ((), ())), preferred_element_type=F32)


def _dot_tn(a, b):
    return lax.dot_general(a, b, (((0,), (0,)), ((), ())), preferred_element_type=F32)


def _sigmoid(x):
    return 0.5 * jnp.tanh(0.5 * x) + 0.5


_GELU_C = math.sqrt(2.0 / math.pi)


def _gelu(x):
    x2 = x * x
    t = jnp.tanh(_GELU_C * (x + 0.044715 * x2 * x))
    val = 0.5 * x * (1.0 + t)
    grad = 0.5 * (1.0 + t) + 0.5 * x * (1.0 - t * t) * (_GELU_C * (1.0 + 3.0 * 0.044715 * x2))
    return val, grad


def _rstd(x):
    return lax.rsqrt(jnp.mean(x * x, axis=-1, keepdims=True) + EPS)


def _modulate(x, vec_ref):
    return (x * _rstd(x)) * vec_ref[0:1, :] * (1.0 + vec_ref[2:3, :]) + vec_ref[1:2, :]


def _modulate_bwd(x, dh, vec_ref, dvec_ref):
    gn, sh, sc = vec_ref[0:1, :], vec_ref[1:2, :], vec_ref[2:3, :]
    rstd = _rstd(x)
    r = x * rstd
    dvec_ref[0:1, :] += jnp.sum(dh * r * (1.0 + sc), axis=0, keepdims=True)
    dvec_ref[1:2, :] += jnp.sum(dh, axis=0, keepdims=True)
    dvec_ref[2:3, :] += jnp.sum(dh * r * gn, axis=0, keepdims=True)
    gm = gn * (1.0 + sc)
    dr = dh * gm
    dx = rstd * (dr - r * jnp.mean(dr * r, axis=-1, keepdims=True))
    return dx, r * gm + sh


def _adam(w, g, m, v):
    m = ADAM_B1 * m + (1.0 - ADAM_B1) * g
    v = ADAM_B2 * v + (1.0 - ADAM_B2) * (g * g)
    m_hat = m / (1.0 - ADAM_B1**ADAM_STEP)
    v_hat = v / (1.0 - ADAM_B2**ADAM_STEP)
    delta = -ADAM_LR * (m_hat / (jnp.sqrt(v_hat) + ADAM_EPS) + ADAM_WD * w)
    return delta, m, v


_ANY = pl.BlockSpec(memory_space=pl.ANY)


class _Phase:
    def __init__(self, ins, out_shapes, aliases, n_sems, start, finish, then):
        self.ins, self.out_shapes, self.aliases, self.n_sems = list(ins), list(out_shapes), dict(aliases), n_sems
        self.start, self.finish, self.then = start, finish, then


def _call(body, name, grid, in_specs, out_specs, out_shape, ins, scratch=(), prefetch=(), phases=(), in_place=None):
    n_pre, n_in, n_out, n_sc = len(prefetch), len(in_specs), len(out_specs), len(scratch)
    ph_in = [len(p.ins) for p in phases]
    ph_out = [len(p.out_shapes) for p in phases]

    def kernel_body(*refs):
        pos = [0]

        def take(k):
            pos[0] += k
            return refs[pos[0] - k : pos[0]]

        pre, ins_ = take(n_pre), take(n_in)
        p_ins = [take(k) for k in ph_in]
        outs_ = take(n_out)
        p_outs = [take(k) for k in ph_out]
        sc = take(n_sc)
        sems = [take(2) for _ in phases]
        if phases:
            ids = [pl.program_id(a) for a in range(len(grid))]
            first = functools.reduce(jnp.logical_and, [i == 0 for i in ids])
            last = functools.reduce(jnp.logical_and, [i == g - 1 for i, g in zip(ids, grid)])

            @pl.when(first)
            def _():
                for p, pi, po, (send, recv) in zip(phases, p_ins, p_outs, sems):
                    p.start(pi, po, send, recv)

        if body is not None:
            body(*pre, *ins_, *outs_, *sc)
        if phases:

            @pl.when(last)
            def _():
                for p, pi, po, (send, recv) in zip(phases, p_ins, p_outs, sems):
                    p.finish(pi, po, send, recv)

    aliases = {n_pre + i: o for i, o in (in_place or {}).items()}
    i0, o0 = n_pre + n_in, n_out
    for p in phases:
        for i, o in p.aliases.items():
            aliases[i0 + i] = o0 + o
        i0 += len(p.ins)
        o0 += len(p.out_shapes)
    all_in = list(in_specs) + [_ANY] * sum(ph_in)
    all_out = list(out_specs) + [_ANY] * sum(ph_out)
    all_scratch = list(scratch)
    for p in phases:
        all_scratch += [pltpu.SemaphoreType.DMA((p.n_sems,)), pltpu.SemaphoreType.DMA((p.n_sems,))]
    shapes = list(out_shape) + [s for p in phases for s in p.out_shapes]
    operands = list(prefetch) + list(ins) + [a for p in phases for a in p.ins]
    sem = ("arbitrary",) * len(grid)
    params = pltpu.CompilerParams(dimension_semantics=sem, vmem_limit_bytes=VMEM_LIMIT_BYTES)
    if n_pre:
        res = pl.pallas_call(
            kernel_body, name=name, out_shape=shapes, input_output_aliases=aliases, compiler_params=params,
            grid_spec=pltpu.PrefetchScalarGridSpec(
                num_scalar_prefetch=n_pre, grid=grid, in_specs=all_in, out_specs=all_out, scratch_shapes=all_scratch
            ),
        )(*operands)
    else:
        res = pl.pallas_call(
            kernel_body, name=name, grid=grid, in_specs=all_in, out_specs=all_out, out_shape=shapes,
            scratch_shapes=all_scratch, input_output_aliases=aliases, compiler_params=params,
        )(*operands)
    res = list(res)
    outs, rest = res[:n_out], res[n_out:]
    p_res = []
    for k in ph_out:
        p_res.append(rest[:k])
        rest = rest[k:]
    return outs, p_res


def _place():
    return lax.axis_index("x"), lax.axis_index("y"), lax.axis_index("c")


def _other_chips():
    x, y, _ = _place()
    return [(1 - x, y), (x, 1 - y), (1 - x, 1 - y)]


def _flip(k):
    x, y, c = _place()
    return (1 - x if k & 4 else x, 1 - y if k & 2 else y, 1 - c if k & 1 else c)


def _remote(src, dst, send, recv, k, to):
    return pltpu.make_async_remote_copy(
        src_ref=src, dst_ref=dst, send_sem=send.at[k], recv_sem=recv.at[k], device_id=to, device_id_type=MESH
    )


def _phase_small_gather(arrs, then):
    n = len(arrs)

    def copies(ins, outs, send, recv):
        x, y, c = _place()
        me = 4 * x + 2 * y + c
        local = [pltpu.make_async_copy(ins[a], outs[a].at[me], send.at[a * N_DEV]) for a in range(n)]
        remote = [_remote(ins[a], outs[a].at[me], send, recv, a * N_DEV + k, _flip(k)) for a in range(n) for k in range(1, N_DEV)]
        return local, remote

    def start(ins, outs, send, recv):
        local, remote = copies(ins, outs, send, recv)
        for cp in local + remote:
            cp.start()

    def finish(ins, outs, send, recv):
        local, remote = copies(ins, outs, send, recv)
        for cp in remote + local:
            cp.wait()

    shapes = [jax.ShapeDtypeStruct((N_DEV,) + a.shape, a.dtype) for a in arrs]
    return _Phase(arrs, shapes, {}, n * N_DEV, start, finish, then)


def _phase_small_exchange(arr, then):
    def copies(ins, outs, send, recv):
        x, y, c = _place()
        me = 4 * x + 2 * y + c
        local = pltpu.make_async_copy(ins[0].at[me], outs[0].at[me], send.at[0])
        remote = []
        for k in range(1, N_DEV):
            px, py, pc = _flip(k)
            remote.append(_remote(ins[0].at[4 * px + 2 * py + pc], outs[0].at[me], send, recv, k, (px, py, pc)))
        return [local] + remote

    def start(ins, outs, send, recv):
        for cp in copies(ins, outs, send, recv):
            cp.start()

    def finish(ins, outs, send, recv):
        for cp in copies(ins, outs, send, recv):
            cp.wait()

    return _Phase([arr], [jax.ShapeDtypeStruct(arr.shape, arr.dtype)], {}, N_DEV, start, finish, then)


def _after(*arrs):
    nothing = lambda *args: None
    return _Phase(arrs, [], {}, 1, nothing, nothing, nothing)


def _flush(name, *phases):
    _, p_outs = _call(None, name, (1,), [], [], [], [], phases=list(phases))
    for p, po in zip(phases, p_outs):
        p.then(po)


class _Big:
    KINDS = {"full": (True, True), "half": (True, False), "shard": (False, True), "block": (False, False)}

    def __init__(self, f3, s3, h3):
        assert s3 != h3
        self.f3, self.s3, self.h3 = tuple(f3), s3, h3
        self.bd = tuple(f3[a] // (N_CHIPS if a == s3 else 1) // (2 if a == h3 else 1) for a in range(3))
        self.tile = (1, _row_tile(self.bd[1], self.bd[2]), self.bd[2])
        self.grid = tuple(self.bd[a] // self.tile[a] for a in range(3))

    def dims(self, kind):
        chips, halves = self.KINDS[kind]
        return tuple(
            self.bd[a] * (N_CHIPS if chips and a == self.s3 else 1) * (2 if halves and a == self.h3 else 1) for a in range(3)
        )

    def view(self, ref, chip=None, half=None, batch0=0, both_halves=True):
        start = [batch0, 0, 0]
        size = list(ref.shape)
        size[0] = self.bd[0] * (2 if self.h3 == 0 and both_halves else 1)
        if chip is not None:
            start[self.s3] += chip * self.bd[self.s3]
            size[self.s3] = self.bd[self.s3]
        if half is not None:
            start[self.h3] += half * self.bd[self.h3]
            size[self.h3] = self.bd[self.h3]
        return ref.at[tuple(pl.ds(st, sz) for st, sz in zip(start, size))]

    def spec(self, chip_from=None, half_from=None, lead=(), batch0=0):
        extra = "grid" in (chip_from, half_from)

        def index(*args):
            pref, idx = args[-1], list(args[int(extra) : -1])
            idx[0] += batch0
            if chip_from:
                idx[self.s3] += (pref[0] if chip_from == "pref" else args[0]) * self.grid[self.s3]
            if half_from:
                idx[self.h3] += (pref[1] if half_from == "pref" else args[0]) * self.grid[self.h3]
            return (0,) * len(lead) + tuple(idx)

        return pl.BlockSpec(tuple(lead) + self.tile, index)


def _same(arrs):
    return [jax.ShapeDtypeStruct(a.shape, a.dtype) for a in arrs]


def _phase_gather_ici(arrs, bigs, then):
    n = len(arrs)

    def copies(outs, send, recv, arriving):
        x, y, c = _place()
        return [
            _remote(blk, blk, send, recv, 3 * a + j, (*chip, c))
            for j, chip in enumerate(_other_chips())
            for a in range(n)
            for blk in [bigs[a].view(outs[a], 2 * chip[0] + chip[1] if arriving else 2 * x + y, c)]
        ]

    def start(ins, outs, send, recv):
        for cp in copies(outs, send, recv, False):
            cp.start()

    def finish(ins, outs, send, recv):
        for cp in copies(outs, send, recv, True):
            cp.wait_recv()
        for cp in copies(outs, send, recv, False):
            cp.wait_send()

    return _Phase(arrs, _same(arrs), {a: a for a in range(n)}, 3 * n, start, finish, then)


def _phase_gather_sibling(arrs, bigs, then):
    n = len(arrs)

    def copies(outs, send, recv, arriving):
        x, y, c = _place()
        return [
            _remote(blk, blk, send, recv, 3 * a + j, (x, y, 1 - c))
            for j, chip in enumerate(_other_chips())
            for a in range(n)
            for blk in [bigs[a].view(outs[a], 2 * chip[0] + chip[1], 1 - c if arriving else c)]
        ]

    def start(ins, outs, send, recv):
        for cp in copies(outs, send, recv, False):
            cp.start()

    def finish(ins, outs, send, recv):
        for cp in copies(outs, send, recv, True):
            cp.wait_recv()
        for cp in copies(outs, send, recv, False):
            cp.wait_send()

    return _Phase(arrs, _same(arrs), {a: a for a in range(n)}, 3 * n, start, finish, then)


def _phase_pair_exchange(grads, bigs, then):
    n = len(grads)

    def copies(ins, outs, send, recv):
        x, y, c = _place()
        srcs = [ins[a] if ins[a].shape == outs[a].shape else bigs[a].view(ins[a], None, 1 - c) for a in range(n)]
        return [_remote(srcs[a], outs[a], send, recv, a, (x, y, 1 - c)) for a in range(n)]

    def start(ins, outs, send, recv):
        for cp in copies(ins, outs, send, recv):
            cp.start()

    def finish(ins, outs, send, recv):
        for cp in copies(ins, outs, send, recv):
            cp.wait()

    shapes = [jax.ShapeDtypeStruct(b.dims("half"), BF16) for b in bigs]
    return _Phase(grads, shapes, {}, n, start, finish, then)


def _phase_chip_exchange(sums, bigs, then):
    n = len(sums)

    def copies(ins, outs, send, recv):
        _, _, c = _place()
        return [
            _remote(bigs[a].view(ins[a], 2 * chip[0] + chip[1], both_halves=False), outs[a].at[j], send, recv, 3 * a + j, (*chip, c))
            for j, chip in enumerate(_other_chips())
            for a in range(n)
        ]

    def start(ins, outs, send, recv):
        for cp in copies(ins, outs, send, recv):
            cp.start()

    def finish(ins, outs, send, recv):
        for cp in copies(ins, outs, send, recv):
            cp.wait()

    shapes = [jax.ShapeDtypeStruct((N_CHIPS - 1,) + b.dims("block"), BF16) for b in bigs]
    return _Phase(sums, shapes, {}, 3 * n, start, finish, then)


_HBM = pl.BlockSpec(memory_space=pltpu.HBM)
_SEM = pl.BlockSpec(memory_space=pltpu.SEMAPHORE)
_DATAFLOW = pltpu.SideEffectType.DATAFLOW_SIDE_EFFECTING


class _InFlight:
    def __init__(self, phase, send, recv, arrays, token):
        self.phase, self.send, self.recv, self.arrays, self.token = phase, send, recv, arrays, token


def _phase_results(phase, refs):
    n_in = len(phase.ins)
    updated = {o: i for i, o in phase.aliases.items()}
    fresh = [o for o in range(len(phase.out_shapes)) if o not in updated]
    return [refs[updated[o]] if o in updated else refs[n_in + fresh.index(o)] for o in range(len(phase.out_shapes))]


def _split_start(phase, name):
    n_in = len(phase.ins)
    fresh = [s for o, s in enumerate(phase.out_shapes) if o not in phase.aliases.values()]
    arrays = list(phase.ins) + [lax.empty(s.shape, s.dtype) for s in fresh]
    n = len(arrays)

    def body(*refs):
        phase.start(refs[:n_in], _phase_results(phase, refs[:n]), refs[n], refs[n + 1])
        refs[-1][...] = jnp.zeros_like(refs[-1])

    operands = [pltpu.with_memory_space_constraint(a, pltpu.HBM) for a in arrays]
    res = pl.pallas_call(
        body, name=name,
        out_shape=[pltpu.SemaphoreType.DMA((phase.n_sems,)), pltpu.SemaphoreType.DMA((phase.n_sems,))]
        + [pltpu.HBM(a.shape, a.dtype) for a in arrays] + [jax.ShapeDtypeStruct((8, 128), F32)],
        in_specs=[_HBM] * n, out_specs=[_SEM, _SEM] + [_HBM] * n + [pl.BlockSpec(memory_space=pltpu.VMEM)],
        input_output_aliases={i: 2 + i for i in range(n)},
        compiler_params=pltpu.CompilerParams(has_side_effects=_DATAFLOW),
    )(*operands)
    return _InFlight(phase, res[0], res[1], list(res[2 : 2 + n]), res[-1])


def _split_wait(flight, after, name):
    phase, n = flight.phase, len(flight.arrays)
    n_in = len(phase.ins)

    def body(*refs):
        phase.finish(refs[:n_in], _phase_results(phase, refs[:n]), refs[n], refs[n + 1])

    res = pl.pallas_call(
        body, name=name, out_shape=[pltpu.HBM(a.shape, a.dtype) for a in flight.arrays],
        in_specs=[_HBM] * n + [_SEM, _SEM] + [_ANY] * len(after), out_specs=[_HBM] * n,
        input_output_aliases={i: i for i in range(n)},
        compiler_params=pltpu.CompilerParams(has_side_effects=_DATAFLOW),
    )(*flight.arrays, flight.send, flight.recv, *after)
    res = list(res)
    phase.then(_phase_results(phase, res))
    return res[:n_in]


def _phase_pair_broadcast(stacks, bigs, batch0s, then):
    n = len(stacks)

    def start(ins, outs, send, recv):
        x, y, c = _place()
        for a in range(n):
            blk = bigs[a].view(outs[a], None, c, batch0s[a])
            _remote(blk, blk, send, recv, a, (x, y, 1 - c)).start()

    def finish(ins, outs, send, recv):
        x, y, c = _place()
        for a in range(n):
            mine = bigs[a].view(outs[a], None, c, batch0s[a])
            theirs = bigs[a].view(outs[a], None, 1 - c, batch0s[a])
            _remote(mine, mine, send, recv, a, (x, y, 1 - c)).wait_send()
            _remote(theirs, theirs, send, recv, a, (x, y, 1 - c)).wait_recv()

    return _Phase(stacks, _same(stacks), {a: a for a in range(n)}, n, start, finish, then)


def _tile_call(body, name, big, where, extra, ins, in_specs, out_specs, out_shape, phases=()):
    grid = ((extra,) if extra else ()) + big.grid
    return _call(body, name, grid, in_specs, out_specs, out_shape, ins, prefetch=(where,), phases=phases)


def _cast_into_full(w_stack, batch0, big, where, name, phases=()):
    def body(_, w_ref, o_ref):
        o_ref[...] = w_ref[...].astype(BF16)

    return _tile_call(
        body, name, big, where, 2, [w_stack], [big.spec(None, "grid", batch0=batch0)], [big.spec("pref", "grid")],
        [jax.ShapeDtypeStruct(big.dims("full"), BF16)], phases,
    )


def _pair_sum(g_full, recv_half, big, where, name, phases=()):
    def body(_, g_ref, r_ref, o_ref):
        o_ref[...] = (g_ref[...].astype(F32) + r_ref[...].astype(F32)).astype(BF16)

    half = big.spec("grid", None)
    return _tile_call(
        body, name, big, where, N_CHIPS, [g_full, recv_half], [big.spec("grid", "pref"), half], [half],
        [jax.ShapeDtypeStruct(big.dims("half"), BF16)], phases,
    )


def _chip_sum(chip_sum, parts, big, where, stack, stack_shape, batch0, name, phases=()):
    def body(_, own_ref, p_ref, *rest):
        acc = own_ref[...].astype(F32)
        for k in range(N_CHIPS - 1):
            acc = acc + p_ref[k].astype(F32)
        rest[-1][...] = acc

    ins = [chip_sum, parts] + ([stack] if stack is not None else [])
    in_specs = [big.spec("pref", None), big.spec(None, None, lead=(N_CHIPS - 1,))] + ([_ANY] if stack is not None else [])
    return _call(
        body, name, big.grid, in_specs, [big.spec(None, "pref", batch0=batch0)], [jax.ShapeDtypeStruct(stack_shape, F32)], ins,
        prefetch=(where,), phases=phases, in_place={2: 0} if stack is not None else None,
    )


def _adam_stack(w, g, m, v, name, after=()):
    b, r, c = w.shape
    tr = _row_tile(r, c, ADAM_BLOCK_ELEMS)

    def body(w_ref, g_ref, m_ref, v_ref, *rest):
        go_ref, d_ref, mo_ref, vo_ref = rest[-4:]
        gv = g_ref[...]
        d, mo, vo = _adam(w_ref[...], gv, m_ref[...], v_ref[...])
        go_ref[...] = gv
        d_ref[...] = d
        mo_ref[...] = mo
        vo_ref[...] = vo

    spec = pl.BlockSpec((1, tr, c), lambda bb, i: (bb, i, 0))
    outs, _ = _call(
        body, name, (b, r // tr), [spec] * 4 + [_ANY] * len(after), [spec] * 4, [jax.ShapeDtypeStruct(w.shape, F32)] * 4,
        [w, g, m, v, *after],
    )
    return outs


def _mod_fwd(c_all, w_mod, b_cols, phases=()):
    n_layers, d, n = w_mod.shape
    tn = _pick(n, (768, 512, 384, 256, 128))

    def body(c_ref, w_ref, b_ref, o_ref):
        cv = c_ref[...]
        ca = (cv * _sigmoid(cv)).astype(BF16)
        o_ref[0] = _dot(ca, w_ref[0].astype(BF16)) + b_ref[0]

    return _call(
        body, "mod_fwd", (n_layers, n // tn),
        [
            pl.BlockSpec((N_DEV, d), lambda l, j: (0, 0)),
            pl.BlockSpec((1, d, tn), lambda l, j: (l, 0, j)),
            pl.BlockSpec((1, 1, tn), lambda l, j: (l, 0, j)),
        ],
        [pl.BlockSpec((1, N_DEV, tn), lambda l, j: (l, 0, j))],
        [jax.ShapeDtypeStruct((n_layers, N_DEV, n), F32)], [c_all, w_mod, b_cols], phases=phases,
    )


def _mod_bwd_adam(c_all_t, dmod_cols, w, m, v, after=()):
    n_layers, d, n = w.shape
    tn = _pick(n, (384, 256, 128))

    def body(c_ref, dm_ref, w_ref, m_ref, v_ref, *rest):
        g_ref, d_ref, mo_ref, vo_ref = rest[-4:]
        cv = c_ref[...]
        ca = (cv * _sigmoid(cv)).astype(BF16)
        g = _dot(ca, dm_ref[0].astype(BF16))
        g_ref[0] = g
        dl, mo, vo = _adam(w_ref[0], g, m_ref[0], v_ref[0])
        d_ref[0] = dl
        mo_ref[0] = mo
        vo_ref[0] = vo

    wspec = pl.BlockSpec((1, d, tn), lambda l, j: (l, 0, j))
    outs, _ = _call(
        body, "mod_bwd_adam", (n_layers, n // tn),
        [pl.BlockSpec((d, N_DEV), lambda l, j: (0, 0)), pl.BlockSpec((1, N_DEV, tn), lambda l, j: (l, 0, j)), wspec, wspec, wspec]
        + [_ANY] * len(after),
        [wspec] * 4, [jax.ShapeDtypeStruct(w.shape, F32)] * 4, [c_all_t, dmod_cols, w, m, v, *after],
    )
    return outs


def _ffn_fwd(x, vec, w_in, w_out, name, phases=()):
    s, d = x.shape
    f = w_out.shape[1]
    tm = _pick(s, (1024, 512, 256, 128))
    tf = _pick(f, (256, 128))
    nf = f // tf

    def body(x_ref, vec_ref, wg_ref, wu_ref, wo_ref, xo_ref, g_ref, u_ref, y_ref, h_sc, acc_sc):
        j = pl.program_id(1)

        @pl.when(j == 0)
        def _():
            h_sc[...] = _modulate(x_ref[...], vec_ref).astype(BF16)
            acc_sc[...] = jnp.zeros_like(acc_sc)

        h = h_sc[...]
        g = _dot(h, wg_ref[0])
        u = _dot(h, wu_ref[0])
        g_ref[...] = g.astype(BF16)
        u_ref[...] = u.astype(BF16)
        a = (g * _sigmoid(g) * u).astype(BF16)
        acc_sc[...] += _dot(a, wo_ref[0])

        @pl.when(j == nf - 1)
        def _():
            yv = acc_sc[...]
            xo_ref[...] = x_ref[...] + 0.5 * vec_ref[3:4, :] * yv
            y_ref[...] = yv.astype(BF16)

    row = pl.BlockSpec((tm, d), lambda i, j: (i, 0))
    hid = pl.BlockSpec((tm, tf), lambda i, j: (i, j))
    return _call(
        body, name, (s // tm, nf),
        [
            row,
            pl.BlockSpec((8, d), lambda i, j: (0, 0)),
            pl.BlockSpec((1, d, tf), lambda i, j: (0, 0, j)),
            pl.BlockSpec((1, d, tf), lambda i, j: (0, 0, nf + j)),
            pl.BlockSpec((1, tf, d), lambda i, j: (0, j, 0)),
        ],
        [row, hid, hid, row],
        [
            jax.ShapeDtypeStruct((s, d), F32),
            jax.ShapeDtypeStruct((s, f), BF16),
            jax.ShapeDtypeStruct((s, f), BF16),
            jax.ShapeDtypeStruct((s, d), BF16),
        ],
        [x, vec, w_in, w_in, w_out],
        scratch=[pltpu.VMEM((tm, d), BF16), pltpu.VMEM((tm, d), F32)], phases=phases,
    )


def _ffn_in_fwd(x, vec, w_in, name, phases=()):
    s, d = x.shape
    f = w_in.shape[2] // 2
    tm = _pick(s, (1024, 512, 256, 128))
    tf = _pick(f, (256, 128))
    nf = f // tf

    def body(x_ref, vec_ref, wg_ref, wu_ref, g_ref, u_ref, a_ref, h_sc):
        @pl.when(pl.program_id(1) == 0)
        def _():
            h_sc[...] = _modulate(x_ref[...], vec_ref).astype(BF16)

        h = h_sc[...]
        g = _dot(h, wg_ref[0])
        u = _dot(h, wu_ref[0])
        g_ref[...] = g.astype(BF16)
        u_ref[...] = u.astype(BF16)
        a_ref[...] = (g * _sigmoid(g) * u).astype(BF16)

    hid = pl.BlockSpec((tm, tf), lambda i, j: (i, j))
    return _call(
        body, name, (s // tm, nf),
        [
            pl.BlockSpec((tm, d), lambda i, j: (i, 0)),
            pl.BlockSpec((8, d), lambda i, j: (0, 0)),
            pl.BlockSpec((1, d, tf), lambda i, j: (0, 0, j)),
            pl.BlockSpec((1, d, tf), lambda i, j: (0, 0, nf + j)),
        ],
        [hid, hid, hid], [jax.ShapeDtypeStruct((s, f), BF16)] * 3, [x, vec, w_in, w_in],
        scratch=[pltpu.VMEM((tm, d), BF16)], phases=phases,
    )


def _ffn_bwd(dxo, x, vec, gg, uu, y, w_in, w_out, name, phases=()):
    s, d = x.shape
    f = w_out.shape[1]
    tm = _pick(s, (512, 256, 128))
    tf = _pick(f, (256, 128))
    nf = f // tf

    def body(dxo_ref, x_ref, vec_ref, g_ref, u_ref, y_ref, wg_ref, wu_ref, wo_ref,
             dx_ref, dg_ref, du_ref, a_ref, h_ref, dy_ref, dvec_ref, acc_sc):
        i, j = pl.program_id(0), pl.program_id(1)

        @pl.when((i == 0) & (j == 0))
        def _():
            dvec_ref[...] = jnp.zeros_like(dvec_ref)

        @pl.when(j == 0)
        def _():
            dxo_v = dxo_ref[...]
            dy_ref[...] = (0.5 * vec_ref[3:4, :] * dxo_v).astype(BF16)
            dvec_ref[3:4, :] += 0.5 * jnp.sum(dxo_v * y_ref[...].astype(F32), axis=0, keepdims=True)
            acc_sc[...] = jnp.zeros_like(acc_sc)

        da = _dot_nt(dy_ref[...], wo_ref[0])
        g = g_ref[...].astype(F32)
        u = u_ref[...].astype(F32)
        sig = _sigmoid(g)
        sl = g * sig
        a_ref[...] = (sl * u).astype(BF16)
        dg = (da * u * (sig * (1.0 + g * (1.0 - sig)))).astype(BF16)
        du = (da * sl).astype(BF16)
        dg_ref[...] = dg
        du_ref[...] = du
        acc_sc[...] += _dot_nt(dg, wg_ref[0]) + _dot_nt(du, wu_ref[0])

        @pl.when(j == nf - 1)
        def _():
            dx, h = _modulate_bwd(x_ref[...], acc_sc[...], vec_ref, dvec_ref)
            dx_ref[...] = dxo_ref[...] + dx
            h_ref[...] = h.astype(BF16)

    row = pl.BlockSpec((tm, d), lambda i, j: (i, 0))
    hid = pl.BlockSpec((tm, tf), lambda i, j: (i, j))
    vecs = pl.BlockSpec((8, d), lambda i, j: (0, 0))
    return _call(
        body, name, (s // tm, nf),
        [
            row, row, vecs, hid, hid, row,
            pl.BlockSpec((1, d, tf), lambda i, j: (0, 0, j)),
            pl.BlockSpec((1, d, tf), lambda i, j: (0, 0, nf + j)),
            pl.BlockSpec((1, tf, d), lambda i, j: (0, j, 0)),
        ],
        [row, hid, hid, hid, row, row, vecs],
        [
            jax.ShapeDtypeStruct((s, d), F32),
            jax.ShapeDtypeStruct((s, f), BF16),
            jax.ShapeDtypeStruct((s, f), BF16),
            jax.ShapeDtypeStruct((s, f), BF16),
            jax.ShapeDtypeStruct((s, d), BF16),
            jax.ShapeDtypeStruct((s, d), BF16),
            jax.ShapeDtypeStruct((8, d), F32),
        ],
        [dxo, x, vec, gg, uu, y, w_in, w_in, w_out],
        scratch=[pltpu.VMEM((tm, d), F32)], phases=phases,
    )


def _grad_half(a, bs, big, where, mine, recv, name, phases=()):
    s, k1 = a.shape
    n = bs[0].shape[1]
    groups = len(bs)
    rows_halved = big.h3 == 1
    assert rows_halved or groups == 1
    kk, nn = (k1 // 2, n) if rows_halved else (k1, n // 2)
    tk = _pick(kk, (1408, 1024, 512, 256, 128))
    tn = _pick(nn, (1408, 1024, 640, 512, 256, 128))
    nkb, nnb = kk // tk, nn // tn
    assert (recv is None) == (not mine)

    def half(pref):
        return pref[1] if mine else 1 - pref[1]

    def body(_, a_ref, *rest):
        q = pl.program_id(1)
        for p in range(groups):

            @pl.when(q == p)
            def _(p=p):
                acc = _dot_tn(a_ref[...], rest[p][...])
                if recv is not None:
                    acc = acc + rest[groups][0].astype(F32)
                rest[-1][0] = acc.astype(BF16)

    def b_block(p):
        def index(i, q, j, pref):
            jj = jnp.where(q == p, j, jnp.where(q < p, 0, nnb - 1))
            return (0, jj + (0 if rows_halved else half(pref) * nnb))

        return pl.BlockSpec((s, tn), index)

    out_spec = pl.BlockSpec((1, tk, tn), lambda i, q, j, pref: (0, i, q * nnb + j))
    in_specs = [pl.BlockSpec((s, tk), lambda i, q, j, pref: (0, i + (half(pref) * nkb if rows_halved else 0)))]
    in_specs += [b_block(p) for p in range(groups)]
    ins = [a, *bs]
    if recv is not None:
        in_specs.append(out_spec)
        ins.append(recv)
    return _call(
        body, name, (nkb, groups, nnb), in_specs, [out_spec], [jax.ShapeDtypeStruct(big.dims("half"), BF16)], ins,
        prefetch=(where,), phases=phases,
    )


def _proj_mod_fwd(x, vec, w, phases=()):
    s, d = x.shape
    n = w.shape[2]
    tm = _pick(s, (1024, 512, 256, 128))
    tn = _pick(n, (640, 512, 256, 128))

    def body(x_ref, vec_ref, w_ref, o_ref, h_sc):
        @pl.when(pl.program_id(1) == 0)
        def _():
            h_sc[...] = _modulate(x_ref[...], vec_ref).astype(BF16)

        o_ref[...] = _dot(h_sc[...], w_ref[0])

    return _call(
        body, "ab_in_fwd", (s // tm, n // tn),
        [
            pl.BlockSpec((tm, d), lambda i, j: (i, 0)),
            pl.BlockSpec((8, d), lambda i, j: (0, 0)),
            pl.BlockSpec((1, d, tn), lambda i, j: (0, 0, j)),
        ],
        [pl.BlockSpec((tm, tn), lambda i, j: (i, j))],
        [jax.ShapeDtypeStruct((s, n), F32)], [x, vec, w],
        scratch=[pltpu.VMEM((tm, d), BF16)], phases=phases,
    )


def _proj_res_fwd(a, w, x, vec, name, weight=1.0, phases=()):
    s, kd = a.shape
    d = x.shape[1]
    tm = _pick(s, (1024, 512, 256, 128) if kd <= 1024 else (512, 256, 128))

    def body(a_ref, w_ref, x_ref, vec_ref, xo_ref, y_ref):
        yv = _dot(a_ref[...], w_ref[0])
        gate = vec_ref[3:4, :] if weight == 1.0 else weight * vec_ref[3:4, :]
        xo_ref[...] = x_ref[...] + gate * yv
        y_ref[...] = yv.astype(BF16)

    row = pl.BlockSpec((tm, d), lambda i: (i, 0))
    return _call(
        body, name, (s // tm,),
        [pl.BlockSpec((tm, kd), lambda i: (i, 0)), pl.BlockSpec((1, kd, d), lambda i: (0, 0, 0)), row, pl.BlockSpec((8, d), lambda i: (0, 0))],
        [row, row],
        [jax.ShapeDtypeStruct((s, d), F32), jax.ShapeDtypeStruct((s, d), BF16)], [a, w, x, vec], phases=phases,
    )


def _proj_res_bwd(dxo, y, vec, w, phases=()):
    s, d = dxo.shape
    kd = w.shape[1]
    tm = _pick(s, (1024, 512, 256, 128))

    def body(dxo_ref, y_ref, vec_ref, w_ref, dy_ref, da_ref, dgate_ref):
        @pl.when(pl.program_id(0) == 0)
        def _():
            dgate_ref[...] = jnp.zeros_like(dgate_ref)

        dxo_v = dxo_ref[...]
        dy = (vec_ref[3:4, :] * dxo_v).astype(BF16)
        dy_ref[...] = dy
        dgate_ref[3:4, :] += jnp.sum(dxo_v * y_ref[...].astype(F32), axis=0, keepdims=True)
        da_ref[...] = _dot_nt(dy, w_ref[0]).astype(BF16)

    row = pl.BlockSpec((tm, d), lambda i: (i, 0))
    vecs = pl.BlockSpec((8, d), lambda i: (0, 0))
    return _call(
        body, "ab_out_bwd", (s // tm,),
        [row, row, vecs, pl.BlockSpec((1, kd, d), lambda i: (0, 0, 0))],
        [row, pl.BlockSpec((tm, kd), lambda i: (i, 0)), vecs],
        [jax.ShapeDtypeStruct((s, d), BF16), jax.ShapeDtypeStruct((s, kd), BF16), jax.ShapeDtypeStruct((8, d), F32)],
        [dxo, y, vec, w], phases=phases,
    )


def _proj_mod_bwd(dproj, w, x, vec, dxo, dvec_in, name, phases=()):
    parts, s, n_part = dproj.shape
    d = x.shape[1]
    tm = _pick(s, (512, 256, 128))
    tk = _pick(n_part, (1408, 1280, 1024, 512, 256, 128))
    per_part = n_part // tk
    nk = parts * per_part

    def body(dp_ref, w_ref, x_ref, vec_ref, dxo_ref, dvi_ref, dx_ref, h_ref, dvec_ref, acc_sc):
        i, k = pl.program_id(0), pl.program_id(1)

        @pl.when((i == 0) & (k == 0))
        def _():
            dvec_ref[...] = dvi_ref[...]

        @pl.when(k == 0)
        def _():
            acc_sc[...] = jnp.zeros_like(acc_sc)

        acc_sc[...] += _dot_nt(dp_ref[0], w_ref[0])

        @pl.when(k == nk - 1)
        def _():
            dx, h = _modulate_bwd(x_ref[...], acc_sc[...], vec_ref, dvec_ref)
            dx_ref[...] = dxo_ref[...] + dx
            h_ref[...] = h.astype(BF16)

    row = pl.BlockSpec((tm, d), lambda i, k: (i, 0))
    vecs = pl.BlockSpec((8, d), lambda i, k: (0, 0))
    return _call(
        body, name, (s // tm, nk),
        [
            pl.BlockSpec((1, tm, tk), lambda i, k: (k // per_part, i, k % per_part)),
            pl.BlockSpec((1, d, tk), lambda i, k: (0, 0, k)),
            row, vecs, row, vecs,
        ],
        [row, row, vecs],
        [jax.ShapeDtypeStruct((s, d), F32), jax.ShapeDtypeStruct((s, d), BF16), jax.ShapeDtypeStruct((8, d), F32)],
        [dproj, w, x, vec, dxo, dvec_in], scratch=[pltpu.VMEM((tm, d), F32)], phases=phases,
    )


def _tril(n):
    return lax.broadcasted_iota(jnp.int32, (n, n), 0) >= lax.broadcasted_iota(jnp.int32, (n, n), 1)


def _layernorm_stats(gv):
    mu = jnp.mean(gv, axis=-1, keepdims=True)
    cen = gv - mu
    rstd = lax.rsqrt(jnp.mean(cen * cen, axis=-1, keepdims=True) + EPS)
    return cen * rstd, rstd


def _shift_down(q, k, above_ref, c_cg, c_xb, first):
    width = q.shape[1]
    rows = lax.broadcasted_iota(jnp.int32, q.shape, 0)
    out = pltpu.roll(q, k, 0)
    for r in range(k):
        src = CONV_HALO - k + r
        above = above_ref[src : src + 1, c_cg : c_cg + width] * above_ref[src : src + 1, c_xb : c_xb + width]
        above = jnp.where(first, 0.0, above)
        out = jnp.where(rows == r, above, out)
    return out


def _ab_mix_fwd(proj, norm_v, w_s, b_rows, conv_w, phases=()):
    s, n = proj.shape
    heads, chunk, _ = w_s.shape
    da = norm_v.shape[1]
    hd = da // heads
    db = conv_w.shape[1]
    tm = _pick(s, (512, 256, 128))

    def body(p_ref, ph_ref, nv_ref, ws_ref, b_ref, cw_ref, o_ref):
        first = pl.program_id(0) == 0
        gu, _ = _gelu(p_ref[:, 0:da])
        gv, _ = _gelu(p_ref[:, da : 2 * da])
        xhat, _ = _layernorm_stats(gv)
        vn = (xhat * nv_ref[...]).astype(BF16)
        mask = _tril(chunk)
        for hh in range(heads):
            wm = jnp.where(mask, ws_ref[hh], 0.0).astype(BF16)
            cols = slice(hh * hd, (hh + 1) * hd)
            for nn in range(tm // chunk):
                rows = slice(nn * chunk, (nn + 1) * chunk)
                z = _dot(wm, vn[rows, cols]) + b_ref[:, cols]
                o_ref[rows, cols] = (gu[rows, cols] * z).astype(BF16)
        c_cg, c_xb = 2 * da + db, 2 * da + 2 * db
        bg = p_ref[:, 2 * da : 2 * da + db]
        q = p_ref[:, c_cg : c_cg + db] * p_ref[:, c_xb : c_xb + db]
        q1 = _shift_down(q, 1, ph_ref, c_cg, c_xb, first)
        q2 = _shift_down(q, 2, ph_ref, c_cg, c_xb, first)
        conv = cw_ref[0:1, :] * q2 + cw_ref[1:2, :] * q1 + cw_ref[2:3, :] * q
        o_ref[:, da : da + db] = (bg * conv).astype(BF16)

    nh = tm // CONV_HALO
    return _call(
        body, "ab_mix_fwd", (s // tm,),
        [
            pl.BlockSpec((tm, n), lambda i: (i, 0)),
            pl.BlockSpec((CONV_HALO, n), lambda i: (jnp.maximum(i * nh - 1, 0), 0)),
            pl.BlockSpec((1, da), lambda i: (0, 0)),
            pl.BlockSpec((heads, chunk, chunk), lambda i: (0, 0, 0)),
            pl.BlockSpec((chunk, da), lambda i: (0, 0)),
            pl.BlockSpec((3, db), lambda i: (0, 0)),
        ],
        [pl.BlockSpec((tm, da + db), lambda i: (i, 0))],
        [jax.ShapeDtypeStruct((s, da + db), BF16)], [proj, proj, norm_v, w_s, b_rows, conv_w], phases=phases,
    )


def _ab_mix_bwd(proj, dcat, norm_v, w_s, b_rows, conv_w, phases=()):
    s, n = proj.shape
    heads, chunk, _ = w_s.shape
    da = norm_v.shape[1]
    hd = da // heads
    db = conv_w.shape[1]
    tm = _pick(s, (512, 256, 128))
    nblk = s // tm
    dhalo = 2 * CONV_HALO

    def body(p_ref, pa_ref, pb_ref, dc_ref, dcb_ref, nv_ref, ws_ref, b_ref, cw_ref,
             dp_ref, dnv_ref, dws_ref, dzs_ref, dcw_ref, dvn_sc):
        i = pl.program_id(0)
        first, last = i == 0, i == nblk - 1

        @pl.when(first)
        def _():
            dnv_ref[...] = jnp.zeros_like(dnv_ref)
            dws_ref[...] = jnp.zeros_like(dws_ref)
            dzs_ref[...] = jnp.zeros_like(dzs_ref)
            dcw_ref[...] = jnp.zeros_like(dcw_ref)

        uu = p_ref[:, 0:da]
        gu, gu_grad = _gelu(uu)
        gv, gv_grad = _gelu(p_ref[:, da : 2 * da])
        xhat, rstd = _layernorm_stats(gv)
        nv = nv_ref[...]
        vn = (xhat * nv).astype(BF16)
        dya = dc_ref[:, 0:da].astype(F32)
        dz = (dya * gu).astype(BF16)
        mask = _tril(chunk)
        for hh in range(heads):
            wm = jnp.where(mask, ws_ref[hh], 0.0).astype(BF16)
            cols = slice(hh * hd, (hh + 1) * hd)
            dws = jnp.zeros((chunk, chunk), F32)
            for nn in range(tm // chunk):
                rows = slice(nn * chunk, (nn + 1) * chunk)
                z = _dot(wm, vn[rows, cols]) + b_ref[:, cols]
                dp_ref[rows, cols] = (dya[rows, cols] * z * gu_grad[rows, cols]).astype(BF16)
                dz_blk = dz[rows, cols]
                dws = dws + _dot_nt(dz_blk, vn[rows, cols])
                dzs_ref[:, cols] += dz_blk.astype(F32)
                dvn = _dot_tn(wm, dz_blk)
                dnv_ref[:, cols] += jnp.sum(dvn * xhat[rows, cols], axis=0, keepdims=True)
                dvn_sc[rows, cols] = dvn
            dws_ref[hh] += jnp.where(mask, dws, 0.0)
        dxhat = dvn_sc[...] * nv
        dgv = rstd * (dxhat - jnp.mean(dxhat, axis=-1, keepdims=True) - xhat * jnp.mean(dxhat * xhat, axis=-1, keepdims=True))
        dp_ref[:, da : 2 * da] = (dgv * gv_grad).astype(BF16)

        c_bg, c_cg, c_xb = 2 * da, 2 * da + db, 2 * da + 2 * db
        bg = p_ref[:, c_bg : c_bg + db]
        cg = p_ref[:, c_cg : c_cg + db]
        xb = p_ref[:, c_xb : c_xb + db]
        q = cg * xb
        q1 = _shift_down(q, 1, pa_ref, c_cg, c_xb, first)
        q2 = _shift_down(q, 2, pa_ref, c_cg, c_xb, first)
        dyb = dc_ref[:, da : da + db].astype(F32)
        conv = cw_ref[0:1, :] * q2 + cw_ref[1:2, :] * q1 + cw_ref[2:3, :] * q
        dp_ref[:, c_bg : c_bg + db] = (dyb * conv).astype(BF16)
        e = dyb * bg
        dcw_ref[0:1, :] += jnp.sum(e * q2, axis=0, keepdims=True)
        dcw_ref[1:2, :] += jnp.sum(e * q1, axis=0, keepdims=True)
        dcw_ref[2:3, :] += jnp.sum(e * q, axis=0, keepdims=True)
        rows = lax.broadcasted_iota(jnp.int32, e.shape, 0)
        dq = cw_ref[2:3, :] * e
        for kk in (1, 2):
            ek = pltpu.roll(e, tm - kk, 0)
            for r in range(kk):
                below = dcb_ref[r : r + 1, da : da + db].astype(F32) * pb_ref[r : r + 1, c_bg : c_bg + db]
                below = jnp.where(last, 0.0, below)
                ek = jnp.where(rows == tm - kk + r, below, ek)
            dq = dq + cw_ref[2 - kk : 3 - kk, :] * ek
        dp_ref[:, c_cg : c_cg + db] = (dq * xb).astype(BF16)
        dp_ref[:, c_xb : c_xb + db] = (dq * cg).astype(BF16)

    nh = tm // CONV_HALO
    nhb = tm // dhalo
    const2 = lambda i: (0, 0)
    return _call(
        body, "ab_mix_bwd", (nblk,),
        [
            pl.BlockSpec((tm, n), lambda i: (i, 0)),
            pl.BlockSpec((CONV_HALO, n), lambda i: (jnp.maximum(i * nh - 1, 0), 0)),
            pl.BlockSpec((CONV_HALO, n), lambda i: (jnp.minimum((i + 1) * nh, s // CONV_HALO - 1), 0)),
            pl.BlockSpec((tm, da + db), lambda i: (i, 0)),
            pl.BlockSpec((dhalo, da + db), lambda i: (jnp.minimum((i + 1) * nhb, s // dhalo - 1), 0)),
            pl.BlockSpec((1, da), const2),
            pl.BlockSpec((heads, chunk, chunk), lambda i: (0, 0, 0)),
            pl.BlockSpec((chunk, da), const2),
            pl.BlockSpec((3, db), const2),
        ],
        [
            pl.BlockSpec((tm, n), lambda i: (i, 0)),
            pl.BlockSpec((1, da), const2),
            pl.BlockSpec((heads, chunk, chunk), lambda i: (0, 0, 0)),
            pl.BlockSpec((chunk, da), const2),
            pl.BlockSpec((3, db), const2),
        ],
        [
            jax.ShapeDtypeStruct((s, n), BF16),
            jax.ShapeDtypeStruct((1, da), F32),
            jax.ShapeDtypeStruct((heads, chunk, chunk), F32),
            jax.ShapeDtypeStruct((chunk, da), F32),
            jax.ShapeDtypeStruct((3, db), F32),
        ],
        [proj, proj, proj, dcat, dcat, norm_v, w_s, b_rows, conv_w],
        scratch=[pltpu.VMEM((tm, da), F32)], phases=phases,
    )


def _pool_counts(tm, i, w):
    t = i * tm + lax.broadcasted_iota(jnp.int32, (tm, 1), 0)
    return jnp.minimum(t + 1, w).astype(F32)


def _pool_fwd(x, vec, w_grp, scale, phases=()):
    s, d = x.shape
    groups, gd, _ = w_grp.shape
    tm = _pick(s, (512, 256, 128))

    def body(x_ref, xa_ref, vec_ref, w_ref, sc_ref, xo_ref, p_ref, o_ref):
        i = pl.program_id(0)
        h = _modulate(x_ref[...], vec_ref)
        ha = jnp.where(i == 0, 0.0, _modulate(xa_ref[...], vec_ref))
        ext = jnp.concatenate([ha, h], axis=0)
        for gi, w in enumerate(POOL_WINDOWS):
            cols = slice(gi * gd, (gi + 1) * gd)
            acc = ext[:, cols]
            step = 1
            while step < w:
                acc = acc + pltpu.roll(acc, step, 0)
                step *= 2
            p = (acc[POOL_HALO:, :] / _pool_counts(tm, i, w) - h[:, cols]).astype(BF16)
            p_ref[:, cols] = p
            o_ref[:, cols] = _dot(p, w_ref[gi]).astype(BF16)
        xo_ref[...] = x_ref[...] + vec_ref[3:4, :] * (o_ref[...].astype(F32) * sc_ref[...])

    nh = tm // POOL_HALO
    row = pl.BlockSpec((tm, d), lambda i: (i, 0))
    return _call(
        body, "pool_fwd", (s // tm,),
        [
            row,
            pl.BlockSpec((POOL_HALO, d), lambda i: (jnp.maximum(i * nh - 1, 0), 0)),
            pl.BlockSpec((8, d), lambda i: (0, 0)),
            pl.BlockSpec((groups, gd, gd), lambda i: (0, 0, 0)),
            pl.BlockSpec((1, d), lambda i: (0, 0)),
        ],
        [row, row, row],
        [jax.ShapeDtypeStruct((s, d), F32), jax.ShapeDtypeStruct((s, d), BF16), jax.ShapeDtypeStruct((s, d), BF16)],
        [x, x, vec, w_grp, scale], phases=phases,
    )


def _pool_bwd(dxo, x, vec, p, o, w_grp, scale, phases=()):
    s, d = x.shape
    groups, gd, _ = w_grp.shape
    tm = _pick(s, (512, 256, 128))
    nblk = s // tm

    def body(dxo_ref, dxb_ref, x_ref, vec_ref, p_ref, o_ref, w_ref, sc_ref, dx_ref, dw_ref, dsc_ref, dvec_ref, dw_sc):
        i = pl.program_id(0)

        @pl.when(i == 0)
        def _():
            dw_sc[...] = jnp.zeros_like(dw_sc)
            dsc_ref[...] = jnp.zeros_like(dsc_ref)
            dvec_ref[...] = jnp.zeros_like(dvec_ref)

        gate, sc = vec_ref[3:4, :], sc_ref[...]
        dxo_v = dxo_ref[...]
        ov = o_ref[...].astype(F32)
        dvec_ref[3:4, :] += jnp.sum(dxo_v * (ov * sc), axis=0, keepdims=True)
        dy = gate * dxo_v
        dsc_ref[...] += jnp.sum(dy * ov, axis=0, keepdims=True)
        dout = (dy * sc).astype(BF16)
        dout_b = jnp.where(i == nblk - 1, 0.0, gate * dxb_ref[...] * sc).astype(BF16)
        for gi, w in enumerate(POOL_WINDOWS):
            cols = slice(gi * gd, (gi + 1) * gd)
            dw_sc[gi] += _dot_tn(p_ref[:, cols], dout[:, cols])
            wb = w_ref[gi]
            dp = _dot_nt(dout[:, cols], wb)
            dp_b = _dot_nt(dout_b[:, cols], wb)
            e = dp / _pool_counts(tm, i, w)
            t_below = (i + 1) * tm + lax.broadcasted_iota(jnp.int32, (POOL_HALO, 1), 0)
            e_b = dp_b / jnp.minimum(t_below + 1, w).astype(F32)
            acc = jnp.concatenate([e, e_b], axis=0)
            step = 1
            while step < w:
                acc = acc + pltpu.roll(acc, tm + POOL_HALO - step, 0)
                step *= 2
            dx_ref[:, cols] = acc[:tm, :] - dp
        dx, _ = _modulate_bwd(x_ref[...], dx_ref[...], vec_ref, dvec_ref)
        dx_ref[...] = dxo_v + dx

        @pl.when(i == nblk - 1)
        def _():
            dw_ref[...] = dw_sc[...].astype(BF16)

    nh = tm // POOL_HALO
    row = pl.BlockSpec((tm, d), lambda i: (i, 0))
    vecs = pl.BlockSpec((8, d), lambda i: (0, 0))
    wspec = pl.BlockSpec((groups, gd, gd), lambda i: (0, 0, 0))
    return _call(
        body, "pool_bwd", (nblk,),
        [
            row,
            pl.BlockSpec((POOL_HALO, d), lambda i: (jnp.minimum((i + 1) * nh, s // POOL_HALO - 1), 0)),
            row, vecs, row, row, wspec,
            pl.BlockSpec((1, d), lambda i: (0, 0)),
        ],
        [row, wspec, pl.BlockSpec((1, d), lambda i: (0, 0)), vecs],
        [
            jax.ShapeDtypeStruct((s, d), F32),
            jax.ShapeDtypeStruct((groups, gd, gd), BF16),
            jax.ShapeDtypeStruct((1, d), F32),
            jax.ShapeDtypeStruct((8, d), F32),
        ],
        [dxo, dxo, x, vec, p, o, w_grp, scale],
        scratch=[pltpu.VMEM((groups, gd, gd), F32)], phases=phases,
    )


def _loss_head(x, gain, target, phases=()):
    s, d = x.shape
    tm = _pick(s, (512, 256, 128))

    def body(x_ref, g_ref, t_ref, dx_ref, aux_ref):
        @pl.when(pl.program_id(0) == 0)
        def _():
            aux_ref[...] = jnp.zeros_like(aux_ref)

        xv = x_ref[...]
        rstd = _rstd(xv)
        r = xv * rstd
        gain_v = g_ref[...]
        err = r * gain_v - t_ref[...]
        aux_ref[1:2, :] += jnp.sum(err * err, axis=0, keepdims=True)
        dout = err * (1.0 / d)
        aux_ref[0:1, :] += jnp.sum(dout * r, axis=0, keepdims=True)
        dr = dout * gain_v
        dx_ref[...] = rstd * (dr - r * jnp.mean(dr * r, axis=-1, keepdims=True))

    row = pl.BlockSpec((tm, d), lambda i: (i, 0))
    return _call(
        body, "loss_head", (s // tm,),
        [row, pl.BlockSpec((1, d), lambda i: (0, 0)), row],
        [row, pl.BlockSpec((8, d), lambda i: (0, 0))],
        [jax.ShapeDtypeStruct((s, d), F32), jax.ShapeDtypeStruct((8, d), F32)], [x, gain, target], phases=phases,
    )


def _small_adam(gathered, gathered_ws, layout, smalls, chip):
    names = list(smalls)
    n = len(names)
    loss_row, _, _, n_feat = layout["loss"]

    def body(*refs):
        chip_ref, g_ref, gws_ref = refs[0], refs[1], refs[2]
        wmv = refs[3 : 3 + 3 * n]
        outs = refs[3 + 3 * n : 3 + 7 * n]
        total = refs[-1]
        total[...] = g_ref[0]
        for kdev in range(1, N_DEV):
            total[...] += g_ref[kdev]
        total_ws = gws_ref[0]
        for kdev in range(1, N_DEV):
            total_ws = total_ws + gws_ref[kdev]
        my_chip = chip_ref[0]
        for a, name in enumerate(names):
            w_ref, m_ref, v_ref = wmv[3 * a : 3 * a + 3]
            if name == "ab_w_s":
                g = total_ws
            else:
                row0, rows, col0, cols = layout[name]
                if col0 is None:
                    g = jnp.zeros((rows, cols), F32)
                    for j in range(N_CHIPS):
                        g = g + jnp.where(my_chip == j, total[row0 : row0 + rows, j * cols : (j + 1) * cols], 0.0)
                else:
                    g = total[row0 : row0 + rows, col0 : col0 + cols]
            dl, mo, vo = _adam(w_ref[...], g, m_ref[...], v_ref[...])
            outs[4 * a][...] = g
            outs[4 * a + 1][...] = dl
            outs[4 * a + 2][...] = mo
            outs[4 * a + 3][...] = vo
        refs[3 + 7 * n][...] = 0.5 * jnp.sum(total[loss_row : loss_row + 1, 0:n_feat], axis=1, keepdims=True) / n_feat

    ins = [gathered, gathered_ws]
    out_shapes = []
    for name in names:
        ins.extend(smalls[name])
        out_shapes.extend([jax.ShapeDtypeStruct(smalls[name][0].shape, F32)] * 4)
    out_shapes.append(jax.ShapeDtypeStruct((1, 1), F32))
    whole = lambda shape: pl.BlockSpec(shape, functools.partial(lambda nd, i, c: (0,) * nd, len(shape)))
    res = pl.pallas_call(
        body, name="small_adam",
        grid_spec=pltpu.PrefetchScalarGridSpec(
            num_scalar_prefetch=1, grid=(1,),
            in_specs=[whole(a.shape) for a in ins], out_specs=[whole(o.shape) for o in out_shapes],
            scratch_shapes=[pltpu.VMEM(gathered.shape[1:], F32)],
        ),
        out_shape=out_shapes,
        compiler_params=pltpu.CompilerParams(dimension_semantics=("arbitrary",), vmem_limit_bytes=VMEM_LIMIT_BYTES),
    )(chip.reshape(1).astype(jnp.int32), *ins)
    return {name: res[4 * a : 4 * a + 4] for a, name in enumerate(names)}, res[4 * n]


def _pad_rows(a, rows=8):
    extra = (-a.shape[0]) % rows
    return jnp.pad(a, ((0, extra), (0, 0))) if extra else a


def _pad_cols(a, cols):
    return jnp.pad(a, ((0, 0), (0, cols - a.shape[1]))) if a.shape[1] < cols else a


def _run(fn, *phases):
    outs, p_outs = fn(list(phases))
    for p, po in zip(phases, p_outs):
        p.then(po)
    return outs


def kernel(x, c, norm_g, w_mod, b_mod, w_ffn_in, w_ffn_out, ab_w_in, ab_norm_v, ab_w_s, ab_b_s, ab_conv_w, ab_w_out, pool_w_grp, pool_scale, final_g, loss_target, m_norm_g, m_w_mod, m_b_mod, m_w_ffn_in, m_w_ffn_out, m_ab_w_in, m_ab_norm_v, m_ab_w_s, m_ab_b_s, m_ab_conv_w, m_ab_w_out, m_pool_w_grp, m_pool_scale, m_final_g, v_norm_g, v_w_mod, v_b_mod, v_w_ffn_in, v_w_ffn_out, v_ab_w_in, v_ab_norm_v, v_ab_w_s, v_ab_b_s, v_ab_conv_w, v_ab_w_out, v_pool_w_grp, v_pool_scale, v_final_g):
    ix, iy, ic = _place()
    chip = 2 * ix + iy
    me = 4 * ix + 2 * iy + ic
    where = jnp.stack([chip, ic]).astype(jnp.int32)
    s, d = x.shape[1], x.shape[2]
    x0 = x.reshape(s, d)
    target = loss_target.reshape(s, d)
    n_layers = norm_g.shape[0]
    dq = d // N_CHIPS
    heads, chunk = ab_w_s.shape[1], ab_w_s.shape[2]
    da = ab_norm_v.shape[1]
    db = ab_conv_w.shape[2] * N_CHIPS
    f_hidden = w_ffn_out.shape[2] * N_CHIPS
    assert n_layers == 2 and da % heads == 0

    cw_pad = _pad_cols(ab_conv_w.reshape(3, db // N_CHIPS), dq)
    packed = jnp.concatenate(
        [_pad_rows(c.reshape(N_CHIPS, dq)), _pad_rows(norm_g.reshape(-1, dq)), _pad_rows(pool_scale.reshape(1, dq)), _pad_rows(cw_pad)],
        axis=0,
    )
    ncol = w_mod.shape[2]
    b_cols = lax.dynamic_slice(b_mod, (0, chip * ncol), (n_layers, ncol)).reshape(n_layers, 1, ncol)
    small = {}

    def small_gather(key, arrs):
        def then(outs):
            small[key] = outs

        return _phase_small_gather(arrs, then)

    stacks = {
        "w_ffn_in": tuple(a.reshape((-1,) + a.shape[2:]) for a in (w_ffn_in, m_w_ffn_in, v_w_ffn_in)),
        "w_ffn_out": tuple(a.reshape((-1,) + a.shape[2:]) for a in (w_ffn_out, m_w_ffn_out, v_w_ffn_out)),
        "ab_w_in": (ab_w_in, m_ab_w_in, v_ab_w_in),
        "ab_w_out": (ab_w_out, m_ab_w_out, v_ab_w_out),
        "pool_w_grp": (pool_w_grp[0], m_pool_w_grp[0], v_pool_w_grp[0]),
    }
    big_in = _Big((1, d, 2 * f_hidden), 2, 1)
    big_out = _Big((1, f_hidden, d), 1, 2)
    units = {}
    for l in range(n_layers):
        for k in range(2):
            units[f"in{l}{k}"] = (big_in, "w_ffn_in", 2 * l + k)
            units[f"out{l}{k}"] = (big_out, "w_ffn_out", 2 * l + k)
    units["abin"] = (_Big((1, d, ab_w_in.shape[2] * N_CHIPS), 2, 1), "ab_w_in", 0)
    units["about"] = (_Big((1, ab_w_out.shape[1] * N_CHIPS, d), 1, 2), "ab_w_out", 0)
    units["pool"] = (_Big((pool_w_grp.shape[1], pool_w_grp.shape[2] * N_CHIPS, pool_w_grp.shape[3]), 1, 0), "pool_w_grp", 0)
    big = {u: g for u, (g, _, _) in units.items()}

    weight = {}
    complete = set()

    def cast(u):
        g, st, b0 = units[u]

        def launch(phases):
            (weight[u],), p_outs = _cast_into_full(stacks[st][0], b0, g, where, "cast_" + u, phases)
            return None, p_outs

        return launch

    def gather_ici(*us):
        def then(outs):
            for u, o in zip(us, outs):
                weight[u] = o

        return _phase_gather_ici([weight[u] for u in us], [big[u] for u in us], then)

    def gather_sibling(*us):
        def then(outs):
            for u, o in zip(us, outs):
                weight[u] = o
                complete.add(u)

        return _phase_gather_sibling([weight[u] for u in us], [big[u] for u in us], then)

    def w_of(u):
        assert u in complete, u
        return weight[u]

    _run(cast("in00"), small_gather("inputs", [packed]))
    small_all = small["inputs"][0]
    by_chip = small_all[0::2]
    c_all = small_all[:, 0:N_CHIPS, :].reshape(N_DEV, d)
    norm_full = by_chip[:, 8 : 8 + 3 * n_layers, :].transpose(1, 0, 2).reshape(3 * n_layers, d)
    pool_scale_full = by_chip[:, 16:17, :].transpose(1, 0, 2).reshape(1, d)
    conv_full = by_chip[:, 24:27, : db // N_CHIPS].transpose(1, 0, 2).reshape(3, db)
    pieces = [("in00",), ("out00",), ("abin", "about"), ("in01", "out01"), ("in10", "out10", "pool"), ("in11", "out11")]
    in_flight = {}

    def start_gather(p):
        in_flight[p] = _split_start(gather_ici(*pieces[p]), f"gather_{p}_start")

    def started():
        return _after(*[flight.token for flight in in_flight.values()])

    def finish_gather(p, after, meanwhile=None):
        flight = in_flight.pop(p)
        _split_wait(flight, list(after) + list(started().ins), f"gather_{p}_wait")
        crossing = _split_start(gather_sibling(*pieces[p]), f"gather_{p}_forward")
        behind = [crossing.token]
        if p + 2 < len(pieces):
            for u in pieces[p + 2]:
                _run(cast(u), _after(crossing.token))
            start_gather(p + 2)
            behind = list(started().ins)
        if meanwhile is not None:
            behind = behind + meanwhile(_after(crossing.token))
        _split_wait(crossing, behind, f"gather_{p}_forwarded")

    mod_cols = _run(lambda phases: _mod_fwd(c_all, w_mod, b_cols, phases))[0]
    def mod_rows(outs):
        small["mod"] = outs

    _run(cast("out00"), _phase_small_exchange(mod_cols.transpose(1, 0, 2), mod_rows))
    start_gather(0)
    start_gather(1)
    mod_mine = small["mod"][0][0::2]
    mod = mod_mine.transpose(1, 0, 2).reshape(n_layers, 3, 3, d)
    vecs = {
        (l, sub): jnp.pad(norm_full[3 * l + sub][None], ((0, 7), (0, 0))) + jnp.pad(mod[l, sub], ((1, 4), (0, 0)))
        for l in range(n_layers)
        for sub in range(3)
    }
    b_rows = jnp.broadcast_to(ab_b_s[0].T[:, :, None], (chunk, heads, da // heads)).reshape(chunk, da)

    saved = {}

    def ffn_forward(xs, l, sub, k, *phases):
        saved[l, sub, "x"] = xs
        xs, gg, uu, yb = _run(
            lambda ph: _ffn_fwd(xs, vecs[l, sub], w_of(f"in{l}{k}"), w_of(f"out{l}{k}"), f"ffn_fwd_{l}{k}", ph), *phases
        )
        saved[l, sub, "act"] = (gg, uu, yb)
        return xs

    finish_gather(0, [vecs[0, 0]])
    saved[0, 0, "x"] = x0
    gg, uu, act = _run(lambda ph: _ffn_in_fwd(x0, vecs[0, 0], w_of("in00"), "ffn_in_fwd_00", ph), started())
    finish_gather(1, [act])
    xs, yb = _run(lambda ph: _proj_res_fwd(act, w_of("out00"), x0, vecs[0, 0], "ffn_out_fwd_00", 0.5, ph), started())
    saved[0, 0, "act"] = (gg, uu, yb)
    saved[0, 1, "x"] = xs
    finish_gather(2, [xs])
    (proj,) = _run(lambda ph: _proj_mod_fwd(xs, vecs[0, 1], w_of("abin"), ph), started())
    (cat,) = _run(lambda ph: _ab_mix_fwd(proj, ab_norm_v, ab_w_s[0], b_rows, conv_full, ph))
    xs, yb = _run(lambda ph: _proj_res_fwd(cat, w_of("about"), xs, vecs[0, 1], "ab_out_fwd", 1.0, ph))
    saved[0, 1, "act"] = (proj, cat, yb)
    finish_gather(3, [xs])
    xs = ffn_forward(xs, 0, 2, 1, started())
    finish_gather(4, [xs])
    xs = ffn_forward(xs, 1, 0, 0, started())
    saved[1, 1, "x"] = xs
    pooled = []

    def pool_forward(behind):
        pooled.extend(_run(lambda ph: _pool_fwd(xs, vecs[1, 1], w_of("pool"), pool_scale_full, ph), behind))
        return [pooled[0]]

    finish_gather(5, [xs], pool_forward)
    xs, pp, oo = pooled
    saved[1, 1, "act"] = (pp, oo)
    xs = ffn_forward(xs, 1, 2, 1)
    dxs, aux = _run(lambda ph: _loss_head(xs, final_g.reshape(1, d), target, ph))

    grad = {}
    recv = {}
    csum = {}
    parts = {}
    reduced = {}
    done = set()
    dvecs, small_g = {}, {}

    def pair_exchange(*us):
        def then(outs):
            for u, o in zip(us, outs):
                recv[u] = o

        return _phase_pair_exchange([grad[u] for u in us], [big[u] for u in us], then)

    def grad_half(u, a, bs, mine, name, *phases):
        (res,) = _run(lambda ph: _grad_half(a, bs, big[u], where, mine, recv[u] if mine else None, name, ph), *phases)
        return res

    def pair_sum(u, *phases):
        def launch(ph):
            (csum[u],), p_outs = _pair_sum(grad[u], recv[u], big[u], where, "pair_sum_" + u, ph)
            return None, p_outs

        _run(launch, *phases)

    def chip_exchange(*us):
        def then(outs):
            for u, o in zip(us, outs):
                parts[u] = o

        return _phase_chip_exchange([csum[u] for u in us], [big[u] for u in us], then)

    def chip_sum(*us, carried=()):
        for n_u, u in enumerate(us):
            g, st, b0 = units[u]

            def launch(ph):
                (reduced[st],), p_outs = _chip_sum(
                    csum[u], parts[u], g, where, reduced.get(st), stacks[st][0].shape, b0, "chip_sum_" + u, ph
                )
                return None, p_outs

            _run(launch, *(carried if n_u == 0 else ()))

    def pair_broadcast(*us):
        sts = [units[u][1] for u in us]
        assert len(set(sts)) == len(sts)

        def then(outs):
            for u, st, o in zip(us, sts, outs):
                reduced[st] = o
                done.add(u)

        return _phase_pair_broadcast([reduced[st] for st in sts], [big[u] for u in us], [units[u][2] for u in us], then)

    def ffn_backward(dxs, l, sub, k, carried_bwd, carried_send, carried_mine):
        gg, uu, yb = saved[l, sub, "act"]
        w_in, w_out = w_of(f"in{l}{k}"), w_of(f"out{l}{k}")
        uo, ui, tag = f"out{l}{k}", f"in{l}{k}", f"{l}{k}"
        dxs, dg, du, a, h, dy, dvecs[l, sub] = _run(
            lambda ph: _ffn_bwd(dxs, saved[l, sub, "x"], vecs[l, sub], gg, uu, yb, w_in, w_out, "ffn_bwd_" + tag, ph), *carried_bwd()
        )
        grad[uo] = grad_half(uo, a, [dy], False, "dw_out_send_" + tag, *carried_send())
        grad[ui] = grad_half(ui, h, [dg, du], False, "dw_in_send_" + tag, pair_exchange(uo))
        csum[uo] = grad_half(uo, a, [dy], True, "dw_out_" + tag, pair_exchange(ui))
        csum[ui] = grad_half(ui, h, [dg, du], True, "dw_in_" + tag, *carried_mine())
        return dxs

    none = lambda: ()
    dxs = ffn_backward(dxs, 1, 2, 1, none, none, none)
    pp, oo = saved[1, 1, "act"]
    dxs, grad["pool"], small_g["pool_scale"], dvecs[1, 1] = _run(
        lambda ph: _pool_bwd(dxs, saved[1, 1, "x"], vecs[1, 1], pp, oo, w_of("pool"), pool_scale_full, ph)
    )

    def after_11():
        return (chip_exchange("in11", "out11"), pair_exchange("pool"))

    def bcast_11():
        chip_sum("in11", "out11")
        pair_sum("pool")
        return (pair_broadcast("in11", "out11"), chip_exchange("pool"))

    dxs = ffn_backward(dxs, 1, 0, 0, after_11, bcast_11, none)

    def after_10():
        return (chip_exchange("in10", "out10"),)

    def bcast_10():
        chip_sum("in10", "out10", "pool")
        return (pair_broadcast("in10", "out10", "pool"),)

    dxs = ffn_backward(dxs, 0, 2, 1, after_10, bcast_10, none)

    proj, cat, yb = saved[0, 1, "act"]
    out01 = _split_start(chip_exchange("out01"), "reduce_out01_start")
    dy, dcat, dgate = _run(lambda ph: _proj_res_bwd(dxs, yb, vecs[0, 1], w_of("about"), ph), _after(out01.token))
    grad["about"] = grad_half("about", cat, [dy], False, "dw_ab_out_send")
    dproj, small_g["ab_norm_v"], small_g["ab_w_s"], dzs, small_g["ab_conv_w"] = _run(
        lambda ph: _ab_mix_bwd(proj, dcat, ab_norm_v, ab_w_s[0], b_rows, conv_full, ph), pair_exchange("about")
    )
    small_g["ab_b_s"] = dzs.reshape(chunk, heads, da // heads).sum(axis=2).T
    dxs, h, dvecs[0, 1] = _run(
        lambda ph: _proj_mod_bwd(dproj[None], w_of("abin"), saved[0, 1, "x"], vecs[0, 1], dxs, dgate, "ab_in_bwd", ph)
    )
    grad["abin"] = grad_half("abin", h, [dproj], False, "dw_ab_in_send")
    (csum["out01"],) = _split_wait(out01, [grad["abin"]], "reduce_out01_wait")
    chip_sum("out01", carried=(pair_exchange("abin"),))
    csum["about"] = grad_half("about", cat, [dy], True, "dw_ab_out", pair_broadcast("out01"))
    csum["abin"] = grad_half("abin", h, [dproj], True, "dw_ab_in")

    layout = {}
    tail = {}

    def after_01():
        tail["01"] = _split_start(chip_exchange("in01", "abin", "about"), "reduce_01_start")
        return (_after(tail["01"].token),)

    def pack_small_grads():
        dvec_all = jnp.stack([dvecs[l, sub] for l in range(n_layers) for sub in range(3)])
        dgain = dvec_all[:, 0, :]
        dmod = dvec_all[:, 1:4, :].reshape(3 * 3 * n_layers, d)
        rows = {
            "norm_g": (dgain, None, dq), "final_g": (aux[0:1], 0, d), "pool_scale": (small_g["pool_scale"], None, dq),
            "b_mod": (dmod, 0, d), "ab_norm_v": (small_g["ab_norm_v"], 0, da),
            "ab_conv_w": (small_g["ab_conv_w"], None, db // N_CHIPS), "ab_b_s": (small_g["ab_b_s"], 0, chunk),
            "loss": (aux[1:2], 0, d),
        }
        row0 = 0
        for nm, (pc, col0, cols) in rows.items():
            layout[nm] = (row0, pc.shape[0], col0, cols)
            row0 += pc.shape[0]
        packed_rows = -(-row0 // 8) * 8
        return sum(
            jnp.pad(pc, ((layout[nm][0], packed_rows - layout[nm][0] - pc.shape[0]), (0, d - pc.shape[1])))
            for nm, (pc, _, _) in rows.items()
        )

    def bcast_01():
        csum["in01"], csum["abin"], csum["about"] = _split_wait(tail["01"], [dvecs[0, 0]], "reduce_01_wait")
        chip_sum("in01", "abin", "about")
        grads_small = [pack_small_grads(), small_g["ab_w_s"].reshape(heads * chunk, chunk)]
        tail["small"] = _split_start(small_gather("grads", grads_small), "gather_small_grads_start")
        return (pair_broadcast("in01", "abin", "about"), _after(tail["small"].token))

    def reduce_out00():
        tail["out00"] = _split_start(chip_exchange("out00"), "reduce_out00_start")
        return (_after(tail["out00"].token),)

    dxs = ffn_backward(dxs, 0, 0, 0, after_01, bcast_01, reduce_out00)
    grad_x = dxs.reshape(x.shape)

    last = _split_start(chip_exchange("in00"), "reduce_last_start")
    (csum["out00"],) = _split_wait(tail["out00"], [last.token], "reduce_out00_wait")
    chip_sum("out00")
    _flush("broadcast_out00", pair_broadcast("out00"))
    _split_wait(tail["small"], [reduced["w_ffn_out"]], "gather_small_grads_wait")
    g_all, gws_all = small["grads"]

    out = {}

    def adam_stack(st, after=()):
        w3, m3, v3 = stacks[st]
        assert all(u in done for u, (_, ust, _) in units.items() if ust == st), st
        shape = {"w_ffn_in": w_ffn_in.shape, "w_ffn_out": w_ffn_out.shape, "pool_w_grp": pool_w_grp.shape}.get(st, w3.shape)
        out[st] = tuple(a.reshape(shape) for a in _adam_stack(w3, reduced[st], m3, v3, "adam_" + st, after))

    for st in ("w_ffn_out", "ab_w_in", "ab_w_out", "pool_w_grp"):
        adam_stack(st, (last.token,))

    shapes2d = {
        "norm_g": (3 * n_layers, dq), "b_mod": (9 * n_layers, d), "final_g": (1, d), "ab_norm_v": (1, da),
        "pool_scale": (1, dq), "ab_conv_w": (3, db // N_CHIPS), "ab_b_s": (heads, chunk), "ab_w_s": (heads * chunk, chunk),
    }
    small_w = {"norm_g": (norm_g, m_norm_g, v_norm_g), "b_mod": (b_mod, m_b_mod, v_b_mod), "final_g": (final_g, m_final_g, v_final_g),
               "ab_norm_v": (ab_norm_v, m_ab_norm_v, v_ab_norm_v), "pool_scale": (pool_scale, m_pool_scale, v_pool_scale),
               "ab_conv_w": (ab_conv_w, m_ab_conv_w, v_ab_conv_w), "ab_b_s": (ab_b_s, m_ab_b_s, v_ab_b_s), "ab_w_s": (ab_w_s, m_ab_w_s, v_ab_w_s)}
    smalls = {nm: tuple(a.reshape(shapes2d[nm]) for a in wmv) for nm, wmv in small_w.items()}
    small_out, loss = _small_adam(g_all, gws_all, layout, smalls, chip)
    loss = loss.reshape(())
    for nm, res in small_out.items():
        out[nm] = tuple(a.reshape(small_w[nm][0].shape) for a in res)

    mod_row0 = layout["b_mod"][0]
    dmod_all = g_all[:, mod_row0 : mod_row0 + 9 * n_layers, :].reshape(N_DEV, n_layers, 9 * d)
    dmod_cols = lax.dynamic_slice(dmod_all, (0, 0, chip * ncol), (N_DEV, n_layers, ncol)).transpose(1, 0, 2)
    out["w_mod"] = tuple(_mod_bwd_adam(c_all.T, dmod_cols, w_mod, m_w_mod, v_w_mod, (last.token,)))

    (csum["in00"],) = _split_wait(
        last, [out[st][1] for st in ("w_mod", "w_ffn_out", "ab_w_in", "ab_w_out", "pool_w_grp")], "reduce_last_wait"
    )
    chip_sum("in00")
    _flush("broadcast_last", pair_broadcast("in00"))
    adam_stack("w_ffn_in")

    order = ["norm_g", "w_mod", "b_mod", "w_ffn_in", "w_ffn_out", "ab_w_in", "ab_norm_v", "ab_w_s", "ab_b_s", "ab_conv_w", "ab_w_out", "pool_w_grp", "pool_scale", "final_g"]
    return (loss, grad_x, *[out[nm][0] for nm in order], *[out[nm][1] for nm in order], *[out[nm][2] for nm in order], *[out[nm][3] for nm in order])
```

```python
import functools
import math

import jax
import jax.numpy as jnp
from jax import lax
from jax.experimental import pallas as pl
from jax.experimental.pallas import tpu as pltpu

F32 = jnp.float32
BF16 = jnp.bfloat16
MESH = pl.DeviceIdType.MESH

EPS = 1e-6
ADAM_LR = 0.001
ADAM_B1 = 0.9
ADAM_B2 = 0.999
ADAM_EPS = 1e-08
ADAM_WD = 0.01
ADAM_STEP = 10
POOL_WINDOWS = (2, 4, 8, 16)
POOL_HALO = 16
CONV_HALO = 8
N_CHIPS = 4
N_DEV = 8
VMEM_LIMIT_BYTES = 48 * 1024 * 1024
EW_BLOCK_ELEMS = 1024 * 1024
ADAM_BLOCK_ELEMS = 512 * 1024


def _pick(n, prefs):
    for p in prefs:
        if p <= n and n % p == 0:
            return p
    return n


def _row_tile(rows, cols, block_elems=EW_BLOCK_ELEMS):
    best = None
    for d in range(16, rows + 1, 16):
        if rows % d == 0 and d * cols <= block_elems:
            best = d
    return best or rows


def _dot(a, b):
    return jnp.dot(a, b, preferred_element_type=F32)


def _dot_nt(a, b):
    return lax.dot_general(a, b, (((1,), (1,)), ((), ())), preferred_element_type=F32)


def _dot_tn(a, b):
    return lax.dot_general(a, b, (((0,), (0,)), ((), ())), preferred_element_type=F32)


def _sigmoid(x):
    return 0.5 * jnp.tanh(0.5 * x) + 0.5


_GELU_C = math.sqrt(2.0 / math.pi)


def _gelu(x):
    x2 = x * x
    t = jnp.tanh(_GELU_C * (x + 0.044715 * x2 * x))
    val = 0.5 * x * (1.0 + t)
    grad = 0.5 * (1.0 + t) + 0.5 * x * (1.0 - t * t) * (_GELU_C * (1.0 + 3.0 * 0.044715 * x2))
    return val, grad


def _rstd(x):
    return lax.rsqrt(jnp.mean(x * x, axis=-1, keepdims=True) + EPS)


def _modulate(x, vec_ref):
    return (x * _rstd(x)) * vec_ref[0:1, :] * (1.0 + vec_ref[2:3, :]) + vec_ref[1:2, :]


def _modulate_bwd(x, dh, vec_ref, dvec_ref):
    gn, sh, sc = vec_ref[0:1, :], vec_ref[1:2, :], vec_ref[2:3, :]
    rstd = _rstd(x)
    r = x * rstd
    dvec_ref[0:1, :] += jnp.sum(dh * r * (1.0 + sc), axis=0, keepdims=True)
    dvec_ref[1:2, :] += jnp.sum(dh, axis=0, keepdims=True)
    dvec_ref[2:3, :] += jnp.sum(dh * r * gn, axis=0, keepdims=True)
    gm = gn * (1.0 + sc)
    dr = dh * gm
    dx = rstd * (dr - r * jnp.mean(dr * r, axis=-1, keepdims=True))
    return dx, r * gm + sh


def _adam(w, g, m, v):
    m = ADAM_B1 * m + (1.0 - ADAM_B1) * g
    v = ADAM_B2 * v + (1.0 - ADAM_B2) * (g * g)
    m_hat = m / (1.0 - ADAM_B1**ADAM_STEP)
    v_hat = v / (1.0 - ADAM_B2**ADAM_STEP)
    delta = -ADAM_LR * (m_hat / (jnp.sqrt(v_hat) + ADAM_EPS) + ADAM_WD * w)
    return delta, m, v


_ANY = pl.BlockSpec(memory_space=pl.ANY)


class _Phase:
    def __init__(self, ins, out_shapes, aliases, n_sems, start, finish, then):
        self.ins, self.out_shapes, self.aliases, self.n_sems = list(ins), list(out_shapes), dict(aliases), n_sems
        self.start, self.finish, self.then = start, finish, then


def _call(body, name, grid, in_specs, out_specs, out_shape, ins, scratch=(), prefetch=(), phases=(), in_place=None):
    n_pre, n_in, n_out, n_sc = len(prefetch), len(in_specs), len(out_specs), len(scratch)
    ph_in = [len(p.ins) for p in phases]
    ph_out = [len(p.out_shapes) for p in phases]

    def kernel_body(*refs):
        pos = [0]

        def take(k):
            pos[0] += k
            return refs[pos[0] - k : pos[0]]

        pre, ins_ = take(n_pre), take(n_in)
        p_ins = [take(k) for k in ph_in]
        outs_ = take(n_out)
        p_outs = [take(k) for k in ph_out]
        sc = take(n_sc)
        sems = [take(2) for _ in phases]
        if phases:
            ids = [pl.program_id(a) for a in range(len(grid))]
            first = functools.reduce(jnp.logical_and, [i == 0 for i in ids])
            last = functools.reduce(jnp.logical_and, [i == g - 1 for i, g in zip(ids, grid)])

            @pl.when(first)
            def _():
                for p, pi, po, (send, recv) in zip(phases, p_ins, p_outs, sems):
                    p.start(pi, po, send, recv)

        if body is not None:
            body(*pre, *ins_, *outs_, *sc)
        if phases:

            @pl.when(last)
            def _():
                for p, pi, po, (send, recv) in zip(phases, p_ins, p_outs, sems):
                    p.finish(pi, po, send, recv)

    aliases = {n_pre + i: o for i, o in (in_place or {}).items()}
    i0, o0 = n_pre + n_in, n_out
    for p in phases:
        for i, o in p.aliases.items():
            aliases[i0 + i] = o0 + o
        i0 += len(p.ins)
        o0 += len(p.out_shapes)
    all_in = list(in_specs) + [_ANY] * sum(ph_in)
    all_out = list(out_specs) + [_ANY] * sum(ph_out)
    all_scratch = list(scratch)
    for p in phases:
        all_scratch += [pltpu.SemaphoreType.DMA((p.n_sems,)), pltpu.SemaphoreType.DMA((p.n_sems,))]
    shapes = list(out_shape) + [s for p in phases for s in p.out_shapes]
    operands = list(prefetch) + list(ins) + [a for p in phases for a in p.ins]
    sem = ("arbitrary",) * len(grid)
    params = pltpu.CompilerParams(dimension_semantics=sem, vmem_limit_bytes=VMEM_LIMIT_BYTES)
    if n_pre:
        res = pl.pallas_call(
            kernel_body, name=name, out_shape=shapes, input_output_aliases=aliases, compiler_params=params,
            grid_spec=pltpu.PrefetchScalarGridSpec(
                num_scalar_prefetch=n_pre, grid=grid, in_specs=all_in, out_specs=all_out, scratch_shapes=all_scratch
            ),
        )(*operands)
    else:
        res = pl.pallas_call(
            kernel_body, name=name, grid=grid, in_specs=all_in, out_specs=all_out, out_shape=shapes,
            scratch_shapes=all_scratch, input_output_aliases=aliases, compiler_params=params,
        )(*operands)
    res = list(res)
    outs, rest = res[:n_out], res[n_out:]
    p_res = []
    for k in ph_out:
        p_res.append(rest[:k])
        rest = rest[k:]
    return outs, p_res


def _place():
    return lax.axis_index("x"), lax.axis_index("y"), lax.axis_index("c")


def _other_chips():
    x, y, _ = _place()
    return [(1 - x, y), (x, 1 - y), (1 - x, 1 - y)]


def _flip(k):
    x, y, c = _place()
    return (1 - x if k & 4 else x, 1 - y if k & 2 else y, 1 - c if k & 1 else c)


def _remote(src, dst, send, recv, k, to):
    return pltpu.make_async_remote_copy(
        src_ref=src, dst_ref=dst, send_sem=send.at[k], recv_sem=recv.at[k], device_id=to, device_id_type=MESH
    )


def _phase_small_gather(arrs, then):
    n = len(arrs)

    def copies(ins, outs, send, recv):
        x, y, c = _place()
        me = 4 * x + 2 * y + c
        local = [pltpu.make_async_copy(ins[a], outs[a].at[me], send.at[a * N_DEV]) for a in range(n)]
        remote = [_remote(ins[a], outs[a].at[me], send, recv, a * N_DEV + k, _flip(k)) for a in range(n) for k in range(1, N_DEV)]
        return local, remote

    def start(ins, outs, send, recv):
        local, remote = copies(ins, outs, send, recv)
        for cp in local + remote:
            cp.start()

    def finish(ins, outs, send, recv):
        local, remote = copies(ins, outs, send, recv)
        for cp in remote + local:
            cp.wait()

    shapes = [jax.ShapeDtypeStruct((N_DEV,) + a.shape, a.dtype) for a in arrs]
    return _Phase(arrs, shapes, {}, n * N_DEV, start, finish, then)


def _phase_small_exchange(arr, then):
    def copies(ins, outs, send, recv):
        x, y, c = _place()
        me = 4 * x + 2 * y + c
        local = pltpu.make_async_copy(ins[0].at[me], outs[0].at[me], send.at[0])
        remote = []
        for k in range(1, N_DEV):
            px, py, pc = _flip(k)
            remote.append(_remote(ins[0].at[4 * px + 2 * py + pc], outs[0].at[me], send, recv, k, (px, py, pc)))
        return [local] + remote

    def start(ins, outs, send, recv):
        for cp in copies(ins, outs, send, recv):
            cp.start()

    def finish(ins, outs, send, recv):
        for cp in copies(ins, outs, send, recv):
            cp.wait()

    return _Phase([arr], [jax.ShapeDtypeStruct(arr.shape, arr.dtype)], {}, N_DEV, start, finish, then)


def _after(*arrs):
    nothing = lambda *args: None
    return _Phase(arrs, [], {}, 1, nothing, nothing, nothing)


def _flush(name, *phases):
    _, p_outs = _call(None, name, (1,), [], [], [], [], phases=list(phases))
    for p, po in zip(phases, p_outs):
        p.then(po)


class _Big:
    KINDS = {"full": (True, True), "half": (True, False), "shard": (False, True), "block": (False, False)}

    def __init__(self, f3, s3, h3):
        assert s3 != h3
        self.f3, self.s3, self.h3 = tuple(f3), s3, h3
        self.bd = tuple(f3[a] // (N_CHIPS if a == s3 else 1) // (2 if a == h3 else 1) for a in range(3))
        self.tile = (1, _row_tile(self.bd[1], self.bd[2]), self.bd[2])
        self.grid = tuple(self.bd[a] // self.tile[a] for a in range(3))

    def dims(self, kind):
        chips, halves = self.KINDS[kind]
        return tuple(
            self.bd[a] * (N_CHIPS if chips and a == self.s3 else 1) * (2 if halves and a == self.h3 else 1) for a in range(3)
        )

    def view(self, ref, chip=None, half=None, batch0=0, both_halves=True, part=None):
        start = [batch0, 0, 0]
        size = list(ref.shape)
        size[0] = self.bd[0] * (2 if self.h3 == 0 and both_halves else 1)
        if chip is not None:
            start[self.s3] += chip * self.bd[self.s3]
            size[self.s3] = self.bd[self.s3]
        if half is not None:
            start[self.h3] += half * self.bd[self.h3]
            size[self.h3] = self.bd[self.h3]
        if part is not None:
            size[1] //= 2
            start[1] += part * size[1]
        return ref.at[tuple(pl.ds(st, sz) for st, sz in zip(start, size))]

    def spec(self, chip_from=None, half_from=None, lead=(), batch0=0):
        extra = "grid" in (chip_from, half_from)

        def index(*args):
            pref, idx = args[-1], list(args[int(extra) : -1])
            idx[0] += batch0
            if chip_from:
                idx[self.s3] += (pref[0] if chip_from == "pref" else args[0]) * self.grid[self.s3]
            if half_from:
                idx[self.h3] += (pref[1] if half_from == "pref" else args[0]) * self.grid[self.h3]
            return (0,) * len(lead) + tuple(idx)

        return pl.BlockSpec(tuple(lead) + self.tile, index)


def _same(arrs):
    return [jax.ShapeDtypeStruct(a.shape, a.dtype) for a in arrs]


def _phase_gather_relay(arrs, bigs, second, then):
    n = len(arrs)
    per = 4 if second else 2

    def copies(outs, send, recv, arriving):
        x, y, c = _place()
        xn, yn, dg = (1 - x, y), (x, 1 - y), (1 - x, 1 - y)
        if not second:
            plan = [((xn if arriving else (x, y)), 0, xn), ((yn if arriving else (x, y)), 1, yn)]
        elif arriving:
            plan = [(yn, 0, yn), (dg, 0, yn), (xn, 1, xn), (dg, 1, xn)]
        else:
            plan = [((x, y), 0, yn), (xn, 0, yn), ((x, y), 1, xn), (yn, 1, xn)]
        res = []
        for a in range(n):
            for k, (chip, part, to) in enumerate(plan):
                blk = bigs[a].view(outs[a], 2 * chip[0] + chip[1], c, part=part)
                res.append(_remote(blk, blk, send, recv, per * a + k, (*to, c)))
        return res

    def start(ins, outs, send, recv):
        for cp in copies(outs, send, recv, False):
            cp.start()

    def finish(ins, outs, send, recv):
        for cp in copies(outs, send, recv, True):
            cp.wait_recv()
        for cp in copies(outs, send, recv, False):
            cp.wait_send()

    return _Phase(arrs, _same(arrs), {a: a for a in range(n)}, per * n, start, finish, then)


def _phase_gather_sibling(arrs, bigs, then):
    n = len(arrs)

    def copies(outs, send, recv, arriving):
        x, y, c = _place()
        return [
            _remote(blk, blk, send, recv, 3 * a + j, (x, y, 1 - c))
            for j, chip in enumerate(_other_chips())
            for a in range(n)
            for blk in [bigs[a].view(outs[a], 2 * chip[0] + chip[1], 1 - c if arriving else c)]
        ]

    def start(ins, outs, send, recv):
        for cp in copies(outs, send, recv, False):
            cp.start()

    def finish(ins, outs, send, recv):
        for cp in copies(outs, send, recv, True):
            cp.wait_recv()
        for cp in copies(outs, send, recv, False):
            cp.wait_send()

    return _Phase(arrs, _same(arrs), {a: a for a in range(n)}, 3 * n, start, finish, then)


def _phase_pair_exchange(grads, bigs, then):
    n = len(grads)

    def copies(ins, outs, send, recv):
        x, y, c = _place()
        srcs = [ins[a] if ins[a].shape == outs[a].shape else bigs[a].view(ins[a], None, 1 - c) for a in range(n)]
        return [_remote(srcs[a], outs[a], send, recv, a, (x, y, 1 - c)) for a in range(n)]

    def start(ins, outs, send, recv):
        for cp in copies(ins, outs, send, recv):
            cp.start()

    def finish(ins, outs, send, recv):
        for cp in copies(ins, outs, send, recv):
            cp.wait()

    shapes = [jax.ShapeDtypeStruct(b.dims("half"), BF16) for b in bigs]
    return _Phase(grads, shapes, {}, n, start, finish, then)


def _phase_chip_exchange(sums, bigs, then):
    n = len(sums)

    def copies(ins, outs, send, recv):
        _, _, c = _place()
        return [
            _remote(bigs[a].view(ins[a], 2 * chip[0] + chip[1], both_halves=False), outs[a].at[j], send, recv, 3 * a + j, (*chip, c))
            for j, chip in enumerate(_other_chips())
            for a in range(n)
        ]

    def start(ins, outs, send, recv):
        for cp in copies(ins, outs, send, recv):
            cp.start()

    def finish(ins, outs, send, recv):
        for cp in copies(ins, outs, send, recv):
            cp.wait()

    shapes = [jax.ShapeDtypeStruct((N_CHIPS - 1,) + b.dims("block"), BF16) for b in bigs]
    return _Phase(sums, shapes, {}, 3 * n, start, finish, then)


_HBM = pl.BlockSpec(memory_space=pltpu.HBM)
_SEM = pl.BlockSpec(memory_space=pltpu.SEMAPHORE)
_DATAFLOW = pltpu.SideEffectType.DATAFLOW_SIDE_EFFECTING


class _InFlight:
    def __init__(self, phase, send, recv, arrays, token):
        self.phase, self.send, self.recv, self.arrays, self.token = phase, send, recv, arrays, token


def _phase_results(phase, refs):
    n_in = len(phase.ins)
    updated = {o: i for i, o in phase.aliases.items()}
    fresh = [o for o in range(len(phase.out_shapes)) if o not in updated]
    return [refs[updated[o]] if o in updated else refs[n_in + fresh.index(o)] for o in range(len(phase.out_shapes))]


def _split_start(phase, name):
    n_in = len(phase.ins)
    fresh = [s for o, s in enumerate(phase.out_shapes) if o not in phase.aliases.values()]
    arrays = list(phase.ins) + [lax.empty(s.shape, s.dtype) for s in fresh]
    n = len(arrays)

    def body(*refs):
        phase.start(refs[:n_in], _phase_results(phase, refs[:n]), refs[n], refs[n + 1])
        refs[-1][...] = jnp.zeros_like(refs[-1])

    operands = [pltpu.with_memory_space_constraint(a, pltpu.HBM) for a in arrays]
    res = pl.pallas_call(
        body, name=name,
        out_shape=[pltpu.SemaphoreType.DMA((phase.n_sems,)), pltpu.SemaphoreType.DMA((phase.n_sems,))]
        + [pltpu.HBM(a.shape, a.dtype) for a in arrays] + [jax.ShapeDtypeStruct((8, 128), F32)],
        in_specs=[_HBM] * n, out_specs=[_SEM, _SEM] + [_HBM] * n + [pl.BlockSpec(memory_space=pltpu.VMEM)],
        input_output_aliases={i: 2 + i for i in range(n)},
        compiler_params=pltpu.CompilerParams(has_side_effects=_DATAFLOW),
    )(*operands)
    return _InFlight(phase, res[0], res[1], list(res[2 : 2 + n]), res[-1])


def _split_wait(flight, after, name):
    phase, n = flight.phase, len(flight.arrays)
    n_in = len(phase.ins)

    def body(*refs):
        phase.finish(refs[:n_in], _phase_results(phase, refs[:n]), refs[n], refs[n + 1])

    res = pl.pallas_call(
        body, name=name, out_shape=[pltpu.HBM(a.shape, a.dtype) for a in flight.arrays],
        in_specs=[_HBM] * n + [_SEM, _SEM] + [_ANY] * len(after), out_specs=[_HBM] * n,
        input_output_aliases={i: i for i in range(n)},
        compiler_params=pltpu.CompilerParams(has_side_effects=_DATAFLOW),
    )(*flight.arrays, flight.send, flight.recv, *after)
    res = list(res)
    phase.then(_phase_results(phase, res))
    return res[:n_in]


def _phase_pair_broadcast(stacks, bigs, batch0s, then):
    n = len(stacks)

    def start(ins, outs, send, recv):
        x, y, c = _place()
        for a in range(n):
            blk = bigs[a].view(outs[a], None, c, batch0s[a])
            _remote(blk, blk, send, recv, a, (x, y, 1 - c)).start()

    def finish(ins, outs, send, recv):
        x, y, c = _place()
        for a in range(n):
            mine = bigs[a].view(outs[a], None, c, batch0s[a])
            theirs = bigs[a].view(outs[a], None, 1 - c, batch0s[a])
            _remote(mine, mine, send, recv, a, (x, y, 1 - c)).wait_send()
            _remote(theirs, theirs, send, recv, a, (x, y, 1 - c)).wait_recv()

    return _Phase(stacks, _same(stacks), {a: a for a in range(n)}, n, start, finish, then)


def _tile_call(body, name, big, where, extra, ins, in_specs, out_specs, out_shape, phases=()):
    grid = ((extra,) if extra else ()) + big.grid
    return _call(body, name, grid, in_specs, out_specs, out_shape, ins, prefetch=(where,), phases=phases)


def _cast_into_full(w_stack, batch0, big, where, name, phases=()):
    def body(_, w_ref, o_ref):
        o_ref[...] = w_ref[...].astype(BF16)

    return _tile_call(
        body, name, big, where, 2, [w_stack], [big.spec(None, "grid", batch0=batch0)], [big.spec("pref", "grid")],
        [jax.ShapeDtypeStruct(big.dims("full"), BF16)], phases,
    )


def _pair_sum(g_full, recv_half, big, where, name, phases=()):
    def body(_, g_ref, r_ref, o_ref):
        o_ref[...] = (g_ref[...].astype(F32) + r_ref[...].astype(F32)).astype(BF16)

    half = big.spec("grid", None)
    return _tile_call(
        body, name, big, where, N_CHIPS, [g_full, recv_half], [big.spec("grid", "pref"), half], [half],
        [jax.ShapeDtypeStruct(big.dims("half"), BF16)], phases,
    )


def _chip_sum(chip_sum, parts, big, where, stack, stack_shape, batch0, name, phases=()):
    def body(_, own_ref, p_ref, *rest):
        acc = own_ref[...].astype(F32)
        for k in range(N_CHIPS - 1):
            acc = acc + p_ref[k].astype(F32)
        rest[-1][...] = acc

    ins = [chip_sum, parts] + ([stack] if stack is not None else [])
    in_specs = [big.spec("pref", None), big.spec(None, None, lead=(N_CHIPS - 1,))] + ([_ANY] if stack is not None else [])
    return _call(
        body, name, big.grid, in_specs, [big.spec(None, "pref", batch0=batch0)], [jax.ShapeDtypeStruct(stack_shape, F32)], ins,
        prefetch=(where,), phases=phases, in_place={2: 0} if stack is not None else None,
    )


def _adam_stack(w, g, m, v, name, after=()):
    b, r, c = w.shape
    tr = _row_tile(r, c, ADAM_BLOCK_ELEMS)

    def body(w_ref, g_ref, m_ref, v_ref, *rest):
        go_ref, d_ref, mo_ref, vo_ref = rest[-4:]
        gv = g_ref[...]
        d, mo, vo = _adam(w_ref[...], gv, m_ref[...], v_ref[...])
        go_ref[...] = gv
        d_ref[...] = d
        mo_ref[...] = mo
        vo_ref[...] = vo

    spec = pl.BlockSpec((1, tr, c), lambda bb, i: (bb, i, 0))
    outs, _ = _call(
        body, name, (b, r // tr), [spec] * 4 + [_ANY] * len(after), [spec] * 4, [jax.ShapeDtypeStruct(w.shape, F32)] * 4,
        [w, g, m, v, *after],
    )
    return outs


def _mod_fwd(c_all, w_mod, b_cols, phases=()):
    n_layers, d, n = w_mod.shape
    tn = _pick(n, (768, 512, 384, 256, 128))

    def body(c_ref, w_ref, b_ref, o_ref):
        cv = c_ref[...]
        ca = (cv * _sigmoid(cv)).astype(BF16)
        o_ref[0] = _dot(ca, w_ref[0].astype(BF16)) + b_ref[0]

    return _call(
        body, "mod_fwd", (n_layers, n // tn),
        [
            pl.BlockSpec((N_DEV, d), lambda l, j: (0, 0)),
            pl.BlockSpec((1, d, tn), lambda l, j: (l, 0, j)),
            pl.BlockSpec((1, 1, tn), lambda l, j: (l, 0, j)),
        ],
        [pl.BlockSpec((1, N_DEV, tn), lambda l, j: (l, 0, j))],
        [jax.ShapeDtypeStruct((n_layers, N_DEV, n), F32)], [c_all, w_mod, b_cols], phases=phases,
    )


def _mod_bwd_adam(c_all_t, dmod_cols, w, m, v, after=()):
    n_layers, d, n = w.shape
    tn = _pick(n, (384, 256, 128))

    def body(c_ref, dm_ref, w_ref, m_ref, v_ref, *rest):
        g_ref, d_ref, mo_ref, vo_ref = rest[-4:]
        cv = c_ref[...]
        ca = (cv * _sigmoid(cv)).astype(BF16)
        g = _dot(ca, dm_ref[0].astype(BF16))
        g_ref[0] = g
        dl, mo, vo = _adam(w_ref[0], g, m_ref[0], v_ref[0])
        d_ref[0] = dl
        mo_ref[0] = mo
        vo_ref[0] = vo

    wspec = pl.BlockSpec((1, d, tn), lambda l, j: (l, 0, j))
    outs, _ = _call(
        body, "mod_bwd_adam", (n_layers, n // tn),
        [pl.BlockSpec((d, N_DEV), lambda l, j: (0, 0)), pl.BlockSpec((1, N_DEV, tn), lambda l, j: (l, 0, j)), wspec, wspec, wspec]
        + [_ANY] * len(after),
        [wspec] * 4, [jax.ShapeDtypeStruct(w.shape, F32)] * 4, [c_all_t, dmod_cols, w, m, v, *after],
    )
    return outs


def _ffn_fwd(x, vec, w_in, w_out, name, phases=()):
    s, d = x.shape
    f = w_out.shape[1]
    tm = _pick(s, (1024, 512, 256, 128))
    tf = _pick(f, (256, 128))
    nf = f // tf

    def body(x_ref, vec_ref, wg_ref, wu_ref, wo_ref, xo_ref, g_ref, u_ref, y_ref, h_sc, acc_sc):
        j = pl.program_id(1)

        @pl.when(j == 0)
        def _():
            h_sc[...] = _modulate(x_ref[...], vec_ref).astype(BF16)
            acc_sc[...] = jnp.zeros_like(acc_sc)

        h = h_sc[...]
        g = _dot(h, wg_ref[0])
        u = _dot(h, wu_ref[0])
        g_ref[...] = g.astype(BF16)
        u_ref[...] = u.astype(BF16)
        a = (g * _sigmoid(g) * u).astype(BF16)
        acc_sc[...] += _dot(a, wo_ref[0])

        @pl.when(j == nf - 1)
        def _():
            yv = acc_sc[...]
            xo_ref[...] = x_ref[...] + 0.5 * vec_ref[3:4, :] * yv
            y_ref[...] = yv.astype(BF16)

    row = pl.BlockSpec((tm, d), lambda i, j: (i, 0))
    hid = pl.BlockSpec((tm, tf), lambda i, j: (i, j))
    return _call(
        body, name, (s // tm, nf),
        [
            row,
            pl.BlockSpec((8, d), lambda i, j: (0, 0)),
            pl.BlockSpec((1, d, tf), lambda i, j: (0, 0, j)),
            pl.BlockSpec((1, d, tf), lambda i, j: (0, 0, nf + j)),
            pl.BlockSpec((1, tf, d), lambda i, j: (0, j, 0)),
        ],
        [row, hid, hid, row],
        [
            jax.ShapeDtypeStruct((s, d), F32),
            jax.ShapeDtypeStruct((s, f), BF16),
            jax.ShapeDtypeStruct((s, f), BF16),
            jax.ShapeDtypeStruct((s, d), BF16),
        ],
        [x, vec, w_in, w_in, w_out],
        scratch=[pltpu.VMEM((tm, d), BF16), pltpu.VMEM((tm, d), F32)], phases=phases,
    )


def _ffn_in_fwd(x, vec, w_in, name, phases=()):
    s, d = x.shape
    f = w_in.shape[2] // 2
    tm = _pick(s, (1024, 512, 256, 128))
    tf = _pick(f, (256, 128))
    nf = f // tf

    def body(x_ref, vec_ref, wg_ref, wu_ref, g_ref, u_ref, a_ref, h_sc):
        @pl.when(pl.program_id(1) == 0)
        def _():
            h_sc[...] = _modulate(x_ref[...], vec_ref).astype(BF16)

        h = h_sc[...]
        g = _dot(h, wg_ref[0])
        u = _dot(h, wu_ref[0])
        g_ref[...] = g.astype(BF16)
        u_ref[...] = u.astype(BF16)
        a_ref[...] = (g * _sigmoid(g) * u).astype(BF16)

    hid = pl.BlockSpec((tm, tf), lambda i, j: (i, j))
    return _call(
        body, name, (s // tm, nf),
        [
            pl.BlockSpec((tm, d), lambda i, j: (i, 0)),
            pl.BlockSpec((8, d), lambda i, j: (0, 0)),
            pl.BlockSpec((1, d, tf), lambda i, j: (0, 0, j)),
            pl.BlockSpec((1, d, tf), lambda i, j: (0, 0, nf + j)),
        ],
        [hid, hid, hid], [jax.ShapeDtypeStruct((s, f), BF16)] * 3, [x, vec, w_in, w_in],
        scratch=[pltpu.VMEM((tm, d), BF16)], phases=phases,
    )


def _ffn_bwd(dxo, x, vec, gg, uu, y, w_in, w_out, name, phases=()):
    s, d = x.shape
    f = w_out.shape[1]
    tm = _pick(s, (512, 256, 128))
    tf = _pick(f, (256, 128))
    nf = f // tf

    def body(dxo_ref, x_ref, vec_ref, g_ref, u_ref, y_ref, wg_ref, wu_ref, wo_ref,
             dx_ref, dg_ref, du_ref, a_ref, h_ref, dy_ref, dvec_ref, acc_sc):
        i, j = pl.program_id(0), pl.program_id(1)

        @pl.when((i == 0) & (j == 0))
        def _():
            dvec_ref[...] = jnp.zeros_like(dvec_ref)

        @pl.when(j == 0)
        def _():
            dxo_v = dxo_ref[...]
            dy_ref[...] = (0.5 * vec_ref[3:4, :] * dxo_v).astype(BF16)
            dvec_ref[3:4, :] += 0.5 * jnp.sum(dxo_v * y_ref[...].astype(F32), axis=0, keepdims=True)
            acc_sc[...] = jnp.zeros_like(acc_sc)

        da = _dot_nt(dy_ref[...], wo_ref[0])
        g = g_ref[...].astype(F32)
        u = u_ref[...].astype(F32)
        sig = _sigmoid(g)
        sl = g * sig
        a_ref[...] = (sl * u).astype(BF16)
        dg = (da * u * (sig * (1.0 + g * (1.0 - sig)))).astype(BF16)
        du = (da * sl).astype(BF16)
        dg_ref[...] = dg
        du_ref[...] = du
        acc_sc[...] += _dot_nt(dg, wg_ref[0]) + _dot_nt(du, wu_ref[0])

        @pl.when(j == nf - 1)
        def _():
            dx, h = _modulate_bwd(x_ref[...], acc_sc[...], vec_ref, dvec_ref)
            dx_ref[...] = dxo_ref[...] + dx
            h_ref[...] = h.astype(BF16)

    row = pl.BlockSpec((tm, d), lambda i, j: (i, 0))
    hid = pl.BlockSpec((tm, tf), lambda i, j: (i, j))
    vecs = pl.BlockSpec((8, d), lambda i, j: (0, 0))
    return _call(
        body, name, (s // tm, nf),
        [
            row, row, vecs, hid, hid, row,
            pl.BlockSpec((1, d, tf), lambda i, j: (0, 0, j)),
            pl.BlockSpec((1, d, tf), lambda i, j: (0, 0, nf + j)),
            pl.BlockSpec((1, tf, d), lambda i, j: (0, j, 0)),
        ],
        [row, hid, hid, hid, row, row, vecs],
        [
            jax.ShapeDtypeStruct((s, d), F32),
            jax.ShapeDtypeStruct((s, f), BF16),
            jax.ShapeDtypeStruct((s, f), BF16),
            jax.ShapeDtypeStruct((s, f), BF16),
            jax.ShapeDtypeStruct((s, d), BF16),
            jax.ShapeDtypeStruct((s, d), BF16),
            jax.ShapeDtypeStruct((8, d), F32),
        ],
        [dxo, x, vec, gg, uu, y, w_in, w_in, w_out],
        scratch=[pltpu.VMEM((tm, d), F32)], phases=phases,
    )


def _grad_half(a, bs, big, where, mine, recv, name, phases=()):
    s, k1 = a.shape
    n = bs[0].shape[1]
    groups = len(bs)
    rows_halved = big.h3 == 1
    assert rows_halved or groups == 1
    kk, nn = (k1 // 2, n) if rows_halved else (k1, n // 2)
    tk = _pick(kk, (1408, 1024, 512, 256, 128))
    tn = _pick(nn, (1408, 1024, 640, 512, 256, 128))
    nkb, nnb = kk // tk, nn // tn
    assert (recv is None) == (not mine)

    def half(pref):
        return pref[1] if mine else 1 - pref[1]

    def body(_, a_ref, *rest):
        q = pl.program_id(1)
        for p in range(groups):

            @pl.when(q == p)
            def _(p=p):
                acc = _dot_tn(a_ref[...], rest[p][...])
                if recv is not None:
                    acc = acc + rest[groups][0].astype(F32)
                rest[-1][0] = acc.astype(BF16)

    def b_block(p):
        def index(i, q, j, pref):
            jj = jnp.where(q == p, j, jnp.where(q < p, 0, nnb - 1))
            return (0, jj + (0 if rows_halved else half(pref) * nnb))

        return pl.BlockSpec((s, tn), index)

    out_spec = pl.BlockSpec((1, tk, tn), lambda i, q, j, pref: (0, i, q * nnb + j))
    in_specs = [pl.BlockSpec((s, tk), lambda i, q, j, pref: (0, i + (half(pref) * nkb if rows_halved else 0)))]
    in_specs += [b_block(p) for p in range(groups)]
    ins = [a, *bs]
    if recv is not None:
        in_specs.append(out_spec)
        ins.append(recv)
    return _call(
        body, name, (nkb, groups, nnb), in_specs, [out_spec], [jax.ShapeDtypeStruct(big.dims("half"), BF16)], ins,
        prefetch=(where,), phases=phases,
    )


def _proj_mod_fwd(x, vec, w, phases=()):
    s, d = x.shape
    n = w.shape[2]
    tm = _pick(s, (1024, 512, 256, 128))
    tn = _pick(n, (640, 512, 256, 128))

    def body(x_ref, vec_ref, w_ref, o_ref, h_sc):
        @pl.when(pl.program_id(1) == 0)
        def _():
            h_sc[...] = _modulate(x_ref[...], vec_ref).astype(BF16)

        o_ref[...] = _dot(h_sc[...], w_ref[0])

    return _call(
        body, "ab_in_fwd", (s // tm, n // tn),
        [
            pl.BlockSpec((tm, d), lambda i, j: (i, 0)),
            pl.BlockSpec((8, d), lambda i, j: (0, 0)),
            pl.BlockSpec((1, d, tn), lambda i, j: (0, 0, j)),
        ],
        [pl.BlockSpec((tm, tn), lambda i, j: (i, j))],
        [jax.ShapeDtypeStruct((s, n), F32)], [x, vec, w],
        scratch=[pltpu.VMEM((tm, d), BF16)], phases=phases,
    )


def _proj_res_fwd(a, w, x, vec, name, weight=1.0, phases=()):
    s, kd = a.shape
    d = x.shape[1]
    tm = _pick(s, (1024, 512, 256, 128) if kd <= 1024 else (512, 256, 128))

    def body(a_ref, w_ref, x_ref, vec_ref, xo_ref, y_ref):
        yv = _dot(a_ref[...], w_ref[0])
        gate = vec_ref[3:4, :] if weight == 1.0 else weight * vec_ref[3:4, :]
        xo_ref[...] = x_ref[...] + gate * yv
        y_ref[...] = yv.astype(BF16)

    row = pl.BlockSpec((tm, d), lambda i: (i, 0))
    return _call(
        body, name, (s // tm,),
        [pl.BlockSpec((tm, kd), lambda i: (i, 0)), pl.BlockSpec((1, kd, d), lambda i: (0, 0, 0)), row, pl.BlockSpec((8, d), lambda i: (0, 0))],
        [row, row],
        [jax.ShapeDtypeStruct((s, d), F32), jax.ShapeDtypeStruct((s, d), BF16)], [a, w, x, vec], phases=phases,
    )


def _proj_res_bwd(dxo, y, vec, w, phases=()):
    s, d = dxo.shape
    kd = w.shape[1]
    tm = _pick(s, (1024, 512, 256, 128))

    def body(dxo_ref, y_ref, vec_ref, w_ref, dy_ref, da_ref, dgate_ref):
        @pl.when(pl.program_id(0) == 0)
        def _():
            dgate_ref[...] = jnp.zeros_like(dgate_ref)

        dxo_v = dxo_ref[...]
        dy = (vec_ref[3:4, :] * dxo_v).astype(BF16)
        dy_ref[...] = dy
        dgate_ref[3:4, :] += jnp.sum(dxo_v * y_ref[...].astype(F32), axis=0, keepdims=True)
        da_ref[...] = _dot_nt(dy, w_ref[0]).astype(BF16)

    row = pl.BlockSpec((tm, d), lambda i: (i, 0))
    vecs = pl.BlockSpec((8, d), lambda i: (0, 0))
    return _call(
        body, "ab_out_bwd", (s // tm,),
        [row, row, vecs, pl.BlockSpec((1, kd, d), lambda i: (0, 0, 0))],
        [row, pl.BlockSpec((tm, kd), lambda i: (i, 0)), vecs],
        [jax.ShapeDtypeStruct((s, d), BF16), jax.ShapeDtypeStruct((s, kd), BF16), jax.ShapeDtypeStruct((8, d), F32)],
        [dxo, y, vec, w], phases=phases,
    )


def _proj_mod_bwd(dproj, w, x, vec, dxo, dvec_in, name, phases=()):
    parts, s, n_part = dproj.shape
    d = x.shape[1]
    tm = _pick(s, (512, 256, 128))
    tk = _pick(n_part, (1408, 1280, 1024, 512, 256, 128))
    per_part = n_part // tk
    nk = parts * per_part

    def body(dp_ref, w_ref, x_ref, vec_ref, dxo_ref, dvi_ref, dx_ref, h_ref, dvec_ref, acc_sc):
        i, k = pl.program_id(0), pl.program_id(1)

        @pl.when((i == 0) & (k == 0))
        def _():
            dvec_ref[...] = dvi_ref[...]

        @pl.when(k == 0)
        def _():
            acc_sc[...] = jnp.zeros_like(acc_sc)

        acc_sc[...] += _dot_nt(dp_ref[0], w_ref[0])

        @pl.when(k == nk - 1)
        def _():
            dx, h = _modulate_bwd(x_ref[...], acc_sc[...], vec_ref, dvec_ref)
            dx_ref[...] = dxo_ref[...] + dx
            h_ref[...] = h.astype(BF16)

    row = pl.BlockSpec((tm, d), lambda i, k: (i, 0))
    vecs = pl.BlockSpec((8, d), lambda i, k: (0, 0))
    return _call(
        body, name, (s // tm, nk),
        [
            pl.BlockSpec((1, tm, tk), lambda i, k: (k // per_part, i, k % per_part)),
            pl.BlockSpec((1, d, tk), lambda i, k: (0, 0, k)),
            row, vecs, row, vecs,
        ],
        [row, row, vecs],
        [jax.ShapeDtypeStruct((s, d), F32), jax.ShapeDtypeStruct((s, d), BF16), jax.ShapeDtypeStruct((8, d), F32)],
        [dproj, w, x, vec, dxo, dvec_in], scratch=[pltpu.VMEM((tm, d), F32)], phases=phases,
    )


def _tril(n):
    return lax.broadcasted_iota(jnp.int32, (n, n), 0) >= lax.broadcasted_iota(jnp.int32, (n, n), 1)


def _layernorm_stats(gv):
    mu = jnp.mean(gv, axis=-1, keepdims=True)
    cen = gv - mu
    rstd = lax.rsqrt(jnp.mean(cen * cen, axis=-1, keepdims=True) + EPS)
    return cen * rstd, rstd


def _shift_down(q, k, above_ref, c_cg, c_xb, first):
    width = q.shape[1]
    rows = lax.broadcasted_iota(jnp.int32, q.shape, 0)
    out = pltpu.roll(q, k, 0)
    for r in range(k):
        src = CONV_HALO - k + r
        above = above_ref[src : src + 1, c_cg : c_cg + width] * above_ref[src : src + 1, c_xb : c_xb + width]
        above = jnp.where(first, 0.0, above)
        out = jnp.where(rows == r, above, out)
    return out


def _ab_mix_fwd(proj, norm_v, w_s, b_rows, conv_w, phases=()):
    s, n = proj.shape
    heads, chunk, _ = w_s.shape
    da = norm_v.shape[1]
    hd = da // heads
    db = conv_w.shape[1]
    tm = _pick(s, (512, 256, 128))

    def body(p_ref, ph_ref, nv_ref, ws_ref, b_ref, cw_ref, o_ref):
        first = pl.program_id(0) == 0
        gu, _ = _gelu(p_ref[:, 0:da])
        gv, _ = _gelu(p_ref[:, da : 2 * da])
        xhat, _ = _layernorm_stats(gv)
        vn = (xhat * nv_ref[...]).astype(BF16)
        mask = _tril(chunk)
        for hh in range(heads):
            wm = jnp.where(mask, ws_ref[hh], 0.0).astype(BF16)
            cols = slice(hh * hd, (hh + 1) * hd)
            for nn in range(tm // chunk):
                rows = slice(nn * chunk, (nn + 1) * chunk)
                z = _dot(wm, vn[rows, cols]) + b_ref[:, cols]
                o_ref[rows, cols] = (gu[rows, cols] * z).astype(BF16)
        c_cg, c_xb = 2 * da + db, 2 * da + 2 * db
        bg = p_ref[:, 2 * da : 2 * da + db]
        q = p_ref[:, c_cg : c_cg + db] * p_ref[:, c_xb : c_xb + db]
        q1 = _shift_down(q, 1, ph_ref, c_cg, c_xb, first)
        q2 = _shift_down(q, 2, ph_ref, c_cg, c_xb, first)
        conv = cw_ref[0:1, :] * q2 + cw_ref[1:2, :] * q1 + cw_ref[2:3, :] * q
        o_ref[:, da : da + db] = (bg * conv).astype(BF16)

    nh = tm // CONV_HALO
    return _call(
        body, "ab_mix_fwd", (s // tm,),
        [
            pl.BlockSpec((tm, n), lambda i: (i, 0)),
            pl.BlockSpec((CONV_HALO, n), lambda i: (jnp.maximum(i * nh - 1, 0), 0)),
            pl.BlockSpec((1, da), lambda i: (0, 0)),
            pl.BlockSpec((heads, chunk, chunk), lambda i: (0, 0, 0)),
            pl.BlockSpec((chunk, da), lambda i: (0, 0)),
            pl.BlockSpec((3, db), lambda i: (0, 0)),
        ],
        [pl.BlockSpec((tm, da + db), lambda i: (i, 0))],
        [jax.ShapeDtypeStruct((s, da + db), BF16)], [proj, proj, norm_v, w_s, b_rows, conv_w], phases=phases,
    )


def _ab_mix_bwd(proj, dcat, norm_v, w_s, b_rows, conv_w, phases=()):
    s, n = proj.shape
    heads, chunk, _ = w_s.shape
    da = norm_v.shape[1]
    hd = da // heads
    db = conv_w.shape[1]
    tm = _pick(s, (512, 256, 128))
    nblk = s // tm
    dhalo = 2 * CONV_HALO

    def body(p_ref, pa_ref, pb_ref, dc_ref, dcb_ref, nv_ref, ws_ref, b_ref, cw_ref,
             dp_ref, dnv_ref, dws_ref, dzs_ref, dcw_ref, dvn_sc):
        i = pl.program_id(0)
        first, last = i == 0, i == nblk - 1

        @pl.when(first)
        def _():
            dnv_ref[...] = jnp.zeros_like(dnv_ref)
            dws_ref[...] = jnp.zeros_like(dws_ref)
            dzs_ref[...] = jnp.zeros_like(dzs_ref)
            dcw_ref[...] = jnp.zeros_like(dcw_ref)

        uu = p_ref[:, 0:da]
        gu, gu_grad = _gelu(uu)
        gv, gv_grad = _gelu(p_ref[:, da : 2 * da])
        xhat, rstd = _layernorm_stats(gv)
        nv = nv_ref[...]
        vn = (xhat * nv).astype(BF16)
        dya = dc_ref[:, 0:da].astype(F32)
        dz = (dya * gu).astype(BF16)
        mask = _tril(chunk)
        for hh in range(heads):
            wm = jnp.where(mask, ws_ref[hh], 0.0).astype(BF16)
            cols = slice(hh * hd, (hh + 1) * hd)
            dws = jnp.zeros((chunk, chunk), F32)
            for nn in range(tm // chunk):
                rows = slice(nn * chunk, (nn + 1) * chunk)
                z = _dot(wm, vn[rows, cols]) + b_ref[:, cols]
                dp_ref[rows, cols] = (dya[rows, cols] * z * gu_grad[rows, cols]).astype(BF16)
                dz_blk = dz[rows, cols]
                dws = dws + _dot_nt(dz_blk, vn[rows, cols])
                dzs_ref[:, cols] += dz_blk.astype(F32)
                dvn = _dot_tn(wm, dz_blk)
                dnv_ref[:, cols] += jnp.sum(dvn * xhat[rows, cols], axis=0, keepdims=True)
                dvn_sc[rows, cols] = dvn
            dws_ref[hh] += jnp.where(mask, dws, 0.0)
        dxhat = dvn_sc[...] * nv
        dgv = rstd * (dxhat - jnp.mean(dxhat, axis=-1, keepdims=True) - xhat * jnp.mean(dxhat * xhat, axis=-1, keepdims=True))
        dp_ref[:, da : 2 * da] = (dgv * gv_grad).astype(BF16)

        c_bg, c_cg, c_xb = 2 * da, 2 * da + db, 2 * da + 2 * db
        bg = p_ref[:, c_bg : c_bg + db]
        cg = p_ref[:, c_cg : c_cg + db]
        xb = p_ref[:, c_xb : c_xb + db]
        q = cg * xb
        q1 = _shift_down(q, 1, pa_ref, c_cg, c_xb, first)
        q2 = _shift_down(q, 2, pa_ref, c_cg, c_xb, first)
        dyb = dc_ref[:, da : da + db].astype(F32)
        conv = cw_ref[0:1, :] * q2 + cw_ref[1:2, :] * q1 + cw_ref[2:3, :] * q
        dp_ref[:, c_bg : c_bg + db] = (dyb * conv).astype(BF16)
        e = dyb * bg
        dcw_ref[0:1, :] += jnp.sum(e * q2, axis=0, keepdims=True)
        dcw_ref[1:2, :] += jnp.sum(e * q1, axis=0, keepdims=True)
        dcw_ref[2:3, :] += jnp.sum(e * q, axis=0, keepdims=True)
        rows = lax.broadcasted_iota(jnp.int32, e.shape, 0)
        dq = cw_ref[2:3, :] * e
        for kk in (1, 2):
            ek = pltpu.roll(e, tm - kk, 0)
            for r in range(kk):
                below = dcb_ref[r : r + 1, da : da + db].astype(F32) * pb_ref[r : r + 1, c_bg : c_bg + db]
                below = jnp.where(last, 0.0, below)
                ek = jnp.where(rows == tm - kk + r, below, ek)
            dq = dq + cw_ref[2 - kk : 3 - kk, :] * ek
        dp_ref[:, c_cg : c_cg + db] = (dq * xb).astype(BF16)
        dp_ref[:, c_xb : c_xb + db] = (dq * cg).astype(BF16)

    nh = tm // CONV_HALO
    nhb = tm // dhalo
    const2 = lambda i: (0, 0)
    return _call(
        body, "ab_mix_bwd", (nblk,),
        [
            pl.BlockSpec((tm, n), lambda i: (i, 0)),
            pl.BlockSpec((CONV_HALO, n), lambda i: (jnp.maximum(i * nh - 1, 0), 0)),
            pl.BlockSpec((CONV_HALO, n), lambda i: (jnp.minimum((i + 1) * nh, s // CONV_HALO - 1), 0)),
            pl.BlockSpec((tm, da + db), lambda i: (i, 0)),
            pl.BlockSpec((dhalo, da + db), lambda i: (jnp.minimum((i + 1) * nhb, s // dhalo - 1), 0)),
            pl.BlockSpec((1, da), const2),
            pl.BlockSpec((heads, chunk, chunk), lambda i: (0, 0, 0)),
            pl.BlockSpec((chunk, da), const2),
            pl.BlockSpec((3, db), const2),
        ],
        [
            pl.BlockSpec((tm, n), lambda i: (i, 0)),
            pl.BlockSpec((1, da), const2),
            pl.BlockSpec((heads, chunk, chunk), lambda i: (0, 0, 0)),
            pl.BlockSpec((chunk, da), const2),
            pl.BlockSpec((3, db), const2),
        ],
        [
            jax.ShapeDtypeStruct((s, n), BF16),
            jax.ShapeDtypeStruct((1, da), F32),
            jax.ShapeDtypeStruct((heads, chunk, chunk), F32),
            jax.ShapeDtypeStruct((chunk, da), F32),
            jax.ShapeDtypeStruct((3, db), F32),
        ],
        [proj, proj, proj, dcat, dcat, norm_v, w_s, b_rows, conv_w],
        scratch=[pltpu.VMEM((tm, da), F32)], phases=phases,
    )


def _pool_counts(tm, i, w):
    t = i * tm + lax.broadcasted_iota(jnp.int32, (tm, 1), 0)
    return jnp.minimum(t + 1, w).astype(F32)


def _pool_fwd(x, vec, w_grp, scale, phases=()):
    s, d = x.shape
    groups, gd, _ = w_grp.shape
    tm = _pick(s, (512, 256, 128))

    def body(x_ref, xa_ref, vec_ref, w_ref, sc_ref, xo_ref, p_ref, o_ref):
        i = pl.program_id(0)
        h = _modulate(x_ref[...], vec_ref)
        ha = jnp.where(i == 0, 0.0, _modulate(xa_ref[...], vec_ref))
        ext = jnp.concatenate([ha, h], axis=0)
        for gi, w in enumerate(POOL_WINDOWS):
            cols = slice(gi * gd, (gi + 1) * gd)
            acc = ext[:, cols]
            step = 1
            while step < w:
                acc = acc + pltpu.roll(acc, step, 0)
                step *= 2
            p = (acc[POOL_HALO:, :] / _pool_counts(tm, i, w) - h[:, cols]).astype(BF16)
            p_ref[:, cols] = p
            o_ref[:, cols] = _dot(p, w_ref[gi]).astype(BF16)
        xo_ref[...] = x_ref[...] + vec_ref[3:4, :] * (o_ref[...].astype(F32) * sc_ref[...])

    nh = tm // POOL_HALO
    row = pl.BlockSpec((tm, d), lambda i: (i, 0))
    return _call(
        body, "pool_fwd", (s // tm,),
        [
            row,
            pl.BlockSpec((POOL_HALO, d), lambda i: (jnp.maximum(i * nh - 1, 0), 0)),
            pl.BlockSpec((8, d), lambda i: (0, 0)),
            pl.BlockSpec((groups, gd, gd), lambda i: (0, 0, 0)),
            pl.BlockSpec((1, d), lambda i: (0, 0)),
        ],
        [row, row, row],
        [jax.ShapeDtypeStruct((s, d), F32), jax.ShapeDtypeStruct((s, d), BF16), jax.ShapeDtypeStruct((s, d), BF16)],
        [x, x, vec, w_grp, scale], phases=phases,
    )


def _pool_bwd(dxo, x, vec, p, o, w_grp, scale, phases=()):
    s, d = x.shape
    groups, gd, _ = w_grp.shape
    tm = _pick(s, (512, 256, 128))
    nblk = s // tm

    def body(dxo_ref, dxb_ref, x_ref, vec_ref, p_ref, o_ref, w_ref, sc_ref, dx_ref, dw_ref, dsc_ref, dvec_ref, dw_sc):
        i = pl.program_id(0)

        @pl.when(i == 0)
        def _():
            dw_sc[...] = jnp.zeros_like(dw_sc)
            dsc_ref[...] = jnp.zeros_like(dsc_ref)
            dvec_ref[...] = jnp.zeros_like(dvec_ref)

        gate, sc = vec_ref[3:4, :], sc_ref[...]
        dxo_v = dxo_ref[...]
        ov = o_ref[...].astype(F32)
        dvec_ref[3:4, :] += jnp.sum(dxo_v * (ov * sc), axis=0, keepdims=True)
        dy = gate * dxo_v
        dsc_ref[...] += jnp.sum(dy * ov, axis=0, keepdims=True)
        dout = (dy * sc).astype(BF16)
        dout_b = jnp.where(i == nblk - 1, 0.0, gate * dxb_ref[...] * sc).astype(BF16)
        for gi, w in enumerate(POOL_WINDOWS):
            cols = slice(gi * gd, (gi + 1) * gd)
            dw_sc[gi] += _dot_tn(p_ref[:, cols], dout[:, cols])
            wb = w_ref[gi]
            dp = _dot_nt(dout[:, cols], wb)
            dp_b = _dot_nt(dout_b[:, cols], wb)
            e = dp / _pool_counts(tm, i, w)
            t_below = (i + 1) * tm + lax.broadcasted_iota(jnp.int32, (POOL_HALO, 1), 0)
            e_b = dp_b / jnp.minimum(t_below + 1, w).astype(F32)
            acc = jnp.concatenate([e, e_b], axis=0)
            step = 1
            while step < w:
                acc = acc + pltpu.roll(acc, tm + POOL_HALO - step, 0)
                step *= 2
            dx_ref[:, cols] = acc[:tm, :] - dp
        dx, _ = _modulate_bwd(x_ref[...], dx_ref[...], vec_ref, dvec_ref)
        dx_ref[...] = dxo_v + dx

        @pl.when(i == nblk - 1)
        def _():
            dw_ref[...] = dw_sc[...].astype(BF16)

    nh = tm // POOL_HALO
    row = pl.BlockSpec((tm, d), lambda i: (i, 0))
    vecs = pl.BlockSpec((8, d), lambda i: (0, 0))
    wspec = pl.BlockSpec((groups, gd, gd), lambda i: (0, 0, 0))
    return _call(
        body, "pool_bwd", (nblk,),
        [
            row,
            pl.BlockSpec((POOL_HALO, d), lambda i: (jnp.minimum((i + 1) * nh, s // POOL_HALO - 1), 0)),
            row, vecs, row, row, wspec,
            pl.BlockSpec((1, d), lambda i: (0, 0)),
        ],
        [row, wspec, pl.BlockSpec((1, d), lambda i: (0, 0)), vecs],
        [
            jax.ShapeDtypeStruct((s, d), F32),
            jax.ShapeDtypeStruct((groups, gd, gd), BF16),
            jax.ShapeDtypeStruct((1, d), F32),
            jax.ShapeDtypeStruct((8, d), F32),
        ],
        [dxo, dxo, x, vec, p, o, w_grp, scale],
        scratch=[pltpu.VMEM((groups, gd, gd), F32)], phases=phases,
    )


def _loss_head(x, gain, target, phases=()):
    s, d = x.shape
    tm = _pick(s, (512, 256, 128))

    def body(x_ref, g_ref, t_ref, dx_ref, aux_ref):
        @pl.when(pl.program_id(0) == 0)
        def _():
            aux_ref[...] = jnp.zeros_like(aux_ref)

        xv = x_ref[...]
        rstd = _rstd(xv)
        r = xv * rstd
        gain_v = g_ref[...]
        err = r * gain_v - t_ref[...]
        aux_ref[1:2, :] += jnp.sum(err * err, axis=0, keepdims=True)
        dout = err * (1.0 / d)
        aux_ref[0:1, :] += jnp.sum(dout * r, axis=0, keepdims=True)
        dr = dout * gain_v
        dx_ref[...] = rstd * (dr - r * jnp.mean(dr * r, axis=-1, keepdims=True))

    row = pl.BlockSpec((tm, d), lambda i: (i, 0))
    return _call(
        body, "loss_head", (s // tm,),
        [row, pl.BlockSpec((1, d), lambda i: (0, 0)), row],
        [row, pl.BlockSpec((8, d), lambda i: (0, 0))],
        [jax.ShapeDtypeStruct((s, d), F32), jax.ShapeDtypeStruct((8, d), F32)], [x, gain, target], phases=phases,
    )


def _small_adam(gathered, gathered_ws, layout, smalls, chip):
    names = list(smalls)
    n = len(names)
    loss_row, _, _, n_feat = layout["loss"]

    def body(*refs):
        chip_ref, g_ref, gws_ref = refs[0], refs[1], refs[2]
        wmv = refs[3 : 3 + 3 * n]
        outs = refs[3 + 3 * n : 3 + 7 * n]
        total = refs[-1]
        total[...] = g_ref[0]
        for kdev in range(1, N_DEV):
            total[...] += g_ref[kdev]
        total_ws = gws_ref[0]
        for kdev in range(1, N_DEV):
            total_ws = total_ws + gws_ref[kdev]
        my_chip = chip_ref[0]
        for a, name in enumerate(names):
            w_ref, m_ref, v_ref = wmv[3 * a : 3 * a + 3]
            if name == "ab_w_s":
                g = total_ws
            else:
                row0, rows, col0, cols = layout[name]
                if col0 is None:
                    g = jnp.zeros((rows, cols), F32)
                    for j in range(N_CHIPS):
                        g = g + jnp.where(my_chip == j, total[row0 : row0 + rows, j * cols : (j + 1) * cols], 0.0)
                else:
                    g = total[row0 : row0 + rows, col0 : col0 + cols]
            dl, mo, vo = _adam(w_ref[...], g, m_ref[...], v_ref[...])
            outs[4 * a][...] = g
            outs[4 * a + 1][...] = dl
            outs[4 * a + 2][...] = mo
            outs[4 * a + 3][...] = vo
        refs[3 + 7 * n][...] = 0.5 * jnp.sum(total[loss_row : loss_row + 1, 0:n_feat], axis=1, keepdims=True) / n_feat

    ins = [gathered, gathered_ws]
    out_shapes = []
    for name in names:
        ins.extend(smalls[name])
        out_shapes.extend([jax.ShapeDtypeStruct(smalls[name][0].shape, F32)] * 4)
    out_shapes.append(jax.ShapeDtypeStruct((1, 1), F32))
    whole = lambda shape: pl.BlockSpec(shape, functools.partial(lambda nd, i, c: (0,) * nd, len(shape)))
    res = pl.pallas_call(
        body, name="small_adam",
        grid_spec=pltpu.PrefetchScalarGridSpec(
            num_scalar_prefetch=1, grid=(1,),
            in_specs=[whole(a.shape) for a in ins], out_specs=[whole(o.shape) for o in out_shapes],
            scratch_shapes=[pltpu.VMEM(gathered.shape[1:], F32)],
        ),
        out_shape=out_shapes,
        compiler_params=pltpu.CompilerParams(dimension_semantics=("arbitrary",), vmem_limit_bytes=VMEM_LIMIT_BYTES),
    )(chip.reshape(1).astype(jnp.int32), *ins)
    return {name: res[4 * a : 4 * a + 4] for a, name in enumerate(names)}, res[4 * n]


def _pad_rows(a, rows=8):
    extra = (-a.shape[0]) % rows
    return jnp.pad(a, ((0, extra), (0, 0))) if extra else a


def _pad_cols(a, cols):
    return jnp.pad(a, ((0, 0), (0, cols - a.shape[1]))) if a.shape[1] < cols else a


def _run(fn, *phases):
    outs, p_outs = fn(list(phases))
    for p, po in zip(phases, p_outs):
        p.then(po)
    return outs


def kernel(x, c, norm_g, w_mod, b_mod, w_ffn_in, w_ffn_out, ab_w_in, ab_norm_v, ab_w_s, ab_b_s, ab_conv_w, ab_w_out, pool_w_grp, pool_scale, final_g, loss_target, m_norm_g, m_w_mod, m_b_mod, m_w_ffn_in, m_w_ffn_out, m_ab_w_in, m_ab_norm_v, m_ab_w_s, m_ab_b_s, m_ab_conv_w, m_ab_w_out, m_pool_w_grp, m_pool_scale, m_final_g, v_norm_g, v_w_mod, v_b_mod, v_w_ffn_in, v_w_ffn_out, v_ab_w_in, v_ab_norm_v, v_ab_w_s, v_ab_b_s, v_ab_conv_w, v_ab_w_out, v_pool_w_grp, v_pool_scale, v_final_g):
    ix, iy, ic = _place()
    chip = 2 * ix + iy
    me = 4 * ix + 2 * iy + ic
    where = jnp.stack([chip, ic]).astype(jnp.int32)
    s, d = x.shape[1], x.shape[2]
    x0 = x.reshape(s, d)
    target = loss_target.reshape(s, d)
    n_layers = norm_g.shape[0]
    dq = d // N_CHIPS
    heads, chunk = ab_w_s.shape[1], ab_w_s.shape[2]
    da = ab_norm_v.shape[1]
    db = ab_conv_w.shape[2] * N_CHIPS
    f_hidden = w_ffn_out.shape[2] * N_CHIPS
    assert n_layers == 2 and da % heads == 0

    cw_pad = _pad_cols(ab_conv_w.reshape(3, db // N_CHIPS), dq)
    packed = jnp.concatenate(
        [_pad_rows(c.reshape(N_CHIPS, dq)), _pad_rows(norm_g.reshape(-1, dq)), _pad_rows(pool_scale.reshape(1, dq)), _pad_rows(cw_pad)],
        axis=0,
    )
    ncol = w_mod.shape[2]
    b_cols = lax.dynamic_slice(b_mod, (0, chip * ncol), (n_layers, ncol)).reshape(n_layers, 1, ncol)
    small = {}

    def small_gather(key, arrs):
        def then(outs):
            small[key] = outs

        return _phase_small_gather(arrs, then)

    stacks = {
        "w_ffn_in": tuple(a.reshape((-1,) + a.shape[2:]) for a in (w_ffn_in, m_w_ffn_in, v_w_ffn_in)),
        "w_ffn_out": tuple(a.reshape((-1,) + a.shape[2:]) for a in (w_ffn_out, m_w_ffn_out, v_w_ffn_out)),
        "ab_w_in": (ab_w_in, m_ab_w_in, v_ab_w_in),
        "ab_w_out": (ab_w_out, m_ab_w_out, v_ab_w_out),
        "pool_w_grp": (pool_w_grp[0], m_pool_w_grp[0], v_pool_w_grp[0]),
    }
    big_in = _Big((1, d, 2 * f_hidden), 2, 1)
    big_out = _Big((1, f_hidden, d), 1, 2)
    units = {}
    for l in range(n_layers):
        for k in range(2):
            units[f"in{l}{k}"] = (big_in, "w_ffn_in", 2 * l + k)
            units[f"out{l}{k}"] = (big_out, "w_ffn_out", 2 * l + k)
    units["abin"] = (_Big((1, d, ab_w_in.shape[2] * N_CHIPS), 2, 1), "ab_w_in", 0)
    units["about"] = (_Big((1, ab_w_out.shape[1] * N_CHIPS, d), 1, 2), "ab_w_out", 0)
    units["pool"] = (_Big((pool_w_grp.shape[1], pool_w_grp.shape[2] * N_CHIPS, pool_w_grp.shape[3]), 1, 0), "pool_w_grp", 0)
    big = {u: g for u, (g, _, _) in units.items()}

    weight = {}
    complete = set()

    def cast(u):
        g, st, b0 = units[u]

        def launch(phases):
            (weight[u],), p_outs = _cast_into_full(stacks[st][0], b0, g, where, "cast_" + u, phases)
            return None, p_outs

        return launch

    def gather_relay(us, second):
        def then(outs):
            for u, o in zip(us, outs):
                weight[u] = o

        return _phase_gather_relay([weight[u] for u in us], [big[u] for u in us], second, then)

    def gather_sibling(*us):
        def then(outs):
            for u, o in zip(us, outs):
                weight[u] = o
                complete.add(u)

        return _phase_gather_sibling([weight[u] for u in us], [big[u] for u in us], then)

    def w_of(u):
        assert u in complete, u
        return weight[u]

    _run(cast("in00"), small_gather("inputs", [packed]))
    small_all = small["inputs"][0]
    by_chip = small_all[0::2]
    c_all = small_all[:, 0:N_CHIPS, :].reshape(N_DEV, d)
    norm_full = by_chip[:, 8 : 8 + 3 * n_layers, :].transpose(1, 0, 2).reshape(3 * n_layers, d)
    pool_scale_full = by_chip[:, 16:17, :].transpose(1, 0, 2).reshape(1, d)
    conv_full = by_chip[:, 24:27, : db // N_CHIPS].transpose(1, 0, 2).reshape(3, db)
    pieces = [("in00",), ("out00",), ("abin", "about"), ("in01", "out01"), ("in10", "out10", "pool"), ("in11", "out11")]
    in_flight = {}

    def start_gather(p):
        in_flight[p, 0] = _split_start(gather_relay(pieces[p], False), f"gather_{p}_start")

    def relay_gather(p):
        flight = in_flight.pop((p, 0))
        _split_wait(flight, list(started().ins), f"gather_{p}_arrived")
        in_flight[p, 1] = _split_start(gather_relay(pieces[p], True), f"gather_{p}_relay")

    def started():
        return _after(*[flight.token for flight in in_flight.values()])

    def finish_gather(p, after, meanwhile=None):
        flight = in_flight.pop((p, 1))
        _split_wait(flight, list(after) + list(started().ins), f"gather_{p}_wait")
        crossing = _split_start(gather_sibling(*pieces[p]), f"gather_{p}_forward")
        behind = [crossing.token]
        if p + 2 < len(pieces):
            for u in pieces[p + 2]:
                _run(cast(u), _after(crossing.token))
        if p + 1 < len(pieces):
            relay_gather(p + 1)
        if p + 2 < len(pieces):
            start_gather(p + 2)
        behind = behind + list(started().ins)
        if meanwhile is not None:
            behind = behind + meanwhile(_after(crossing.token))
        _split_wait(crossing, behind, f"gather_{p}_forwarded")

    mod_cols = _run(lambda phases: _mod_fwd(c_all, w_mod, b_cols, phases))[0]
    def mod_rows(outs):
        small["mod"] = outs

    _run(cast("out00"), _phase_small_exchange(mod_cols.transpose(1, 0, 2), mod_rows))
    start_gather(0)
    start_gather(1)
    relay_gather(0)
    mod_mine = small["mod"][0][0::2]
    mod = mod_mine.transpose(1, 0, 2).reshape(n_layers, 3, 3, d)
    vecs = {
        (l, sub): jnp.pad(norm_full[3 * l + sub][None], ((0, 7), (0, 0))) + jnp.pad(mod[l, sub], ((1, 4), (0, 0)))
        for l in range(n_layers)
        for sub in range(3)
    }
    b_rows = jnp.broadcast_to(ab_b_s[0].T[:, :, None], (chunk, heads, da // heads)).reshape(chunk, da)

    saved = {}

    def ffn_forward(xs, l, sub, k, *phases):
        saved[l, sub, "x"] = xs
        xs, gg, uu, yb = _run(
            lambda ph: _ffn_fwd(xs, vecs[l, sub], w_of(f"in{l}{k}"), w_of(f"out{l}{k}"), f"ffn_fwd_{l}{k}", ph), *phases
        )
        saved[l, sub, "act"] = (gg, uu, yb)
        return xs

    finish_gather(0, [vecs[0, 0]])
    saved[0, 0, "x"] = x0
    gg, uu, act = _run(lambda ph: _ffn_in_fwd(x0, vecs[0, 0], w_of("in00"), "ffn_in_fwd_00", ph), started())
    finish_gather(1, [act])
    xs, yb = _run(lambda ph: _proj_res_fwd(act, w_of("out00"), x0, vecs[0, 0], "ffn_out_fwd_00", 0.5, ph), started())
    saved[0, 0, "act"] = (gg, uu, yb)
    saved[0, 1, "x"] = xs
    finish_gather(2, [xs])
    (proj,) = _run(lambda ph: _proj_mod_fwd(xs, vecs[0, 1], w_of("abin"), ph), started())
    (cat,) = _run(lambda ph: _ab_mix_fwd(proj, ab_norm_v, ab_w_s[0], b_rows, conv_full, ph))
    xs, yb = _run(lambda ph: _proj_res_fwd(cat, w_of("about"), xs, vecs[0, 1], "ab_out_fwd", 1.0, ph))
    saved[0, 1, "act"] = (proj, cat, yb)
    finish_gather(3, [xs])
    xs = ffn_forward(xs, 0, 2, 1, started())
    finish_gather(4, [xs])
    xs = ffn_forward(xs, 1, 0, 0, started())
    saved[1, 1, "x"] = xs
    pooled = []

    def pool_forward(behind):
        pooled.extend(_run(lambda ph: _pool_fwd(xs, vecs[1, 1], w_of("pool"), pool_scale_full, ph), behind))
        return [pooled[0]]

    finish_gather(5, [xs], pool_forward)
    xs, pp, oo = pooled
    saved[1, 1, "act"] = (pp, oo)
    xs = ffn_forward(xs, 1, 2, 1)
    dxs, aux = _run(lambda ph: _loss_head(xs, final_g.reshape(1, d), target, ph))

    grad = {}
    recv = {}
    csum = {}
    parts = {}
    reduced = {}
    done = set()
    dvecs, small_g = {}, {}

    def pair_exchange(*us):
        def then(outs):
            for u, o in zip(us, outs):
                recv[u] = o

        return _phase_pair_exchange([grad[u] for u in us], [big[u] for u in us], then)

    def grad_half(u, a, bs, mine, name, *phases):
        (res,) = _run(lambda ph: _grad_half(a, bs, big[u], where, mine, recv[u] if mine else None, name, ph), *phases)
        return res

    def pair_sum(u, *phases):
        def launch(ph):
            (csum[u],), p_outs = _pair_sum(grad[u], recv[u], big[u], where, "pair_sum_" + u, ph)
            return None, p_outs

        _run(launch, *phases)

    def chip_exchange(*us):
        def then(outs):
            for u, o in zip(us, outs):
                parts[u] = o

        return _phase_chip_exchange([csum[u] for u in us], [big[u] for u in us], then)

    def chip_sum(*us, carried=()):
        for n_u, u in enumerate(us):
            g, st, b0 = units[u]

            def launch(ph):
                (reduced[st],), p_outs = _chip_sum(
                    csum[u], parts[u], g, where, reduced.get(st), stacks[st][0].shape, b0, "chip_sum_" + u, ph
                )
                return None, p_outs

            _run(launch, *(carried if n_u == 0 else ()))

    def pair_broadcast(*us):
        sts = [units[u][1] for u in us]
        assert len(set(sts)) == len(sts)

        def then(outs):
            for u, st, o in zip(us, sts, outs):
                reduced[st] = o
                done.add(u)

        return _phase_pair_broadcast([reduced[st] for st in sts], [big[u] for u in us], [units[u][2] for u in us], then)

    def ffn_backward(dxs, l, sub, k, carried_bwd, carried_send, carried_mine):
        gg, uu, yb = saved[l, sub, "act"]
        w_in, w_out = w_of(f"in{l}{k}"), w_of(f"out{l}{k}")
        uo, ui, tag = f"out{l}{k}", f"in{l}{k}", f"{l}{k}"
        dxs, dg, du, a, h, dy, dvecs[l, sub] = _run(
            lambda ph: _ffn_bwd(dxs, saved[l, sub, "x"], vecs[l, sub], gg, uu, yb, w_in, w_out, "ffn_bwd_" + tag, ph), *carried_bwd()
        )
        grad[uo] = grad_half(uo, a, [dy], False, "dw_out_send_" + tag, *carried_send())
        grad[ui] = grad_half(ui, h, [dg, du], False, "dw_in_send_" + tag, pair_exchange(uo))
        csum[uo] = grad_half(uo, a, [dy], True, "dw_out_" + tag, pair_exchange(ui))
        csum[ui] = grad_half(ui, h, [dg, du], True, "dw_in_" + tag, *carried_mine())
        return dxs

    none = lambda: ()
    dxs = ffn_backward(dxs, 1, 2, 1, none, none, none)
    pp, oo = saved[1, 1, "act"]
    dxs, grad["pool"], small_g["pool_scale"], dvecs[1, 1] = _run(
        lambda ph: _pool_bwd(dxs, saved[1, 1, "x"], vecs[1, 1], pp, oo, w_of("pool"), pool_scale_full, ph)
    )

    def after_11():
        return (chip_exchange("in11", "out11"), pair_exchange("pool"))

    def bcast_11():
        chip_sum("in11", "out11")
        pair_sum("pool")
        return (pair_broadcast("in11", "out11"), chip_exchange("pool"))

    dxs = ffn_backward(dxs, 1, 0, 0, after_11, bcast_11, none)

    def after_10():
        return (chip_exchange("in10", "out10"),)

    def bcast_10():
        chip_sum("in10", "out10", "pool")
        return (pair_broadcast("in10", "out10", "pool"),)

    dxs = ffn_backward(dxs, 0, 2, 1, after_10, bcast_10, none)

    proj, cat, yb = saved[0, 1, "act"]
    out01 = _split_start(chip_exchange("out01"), "reduce_out01_start")
    dy, dcat, dgate = _run(lambda ph: _proj_res_bwd(dxs, yb, vecs[0, 1], w_of("about"), ph), _after(out01.token))
    grad["about"] = grad_half("about", cat, [dy], False, "dw_ab_out_send")
    dproj, small_g["ab_norm_v"], small_g["ab_w_s"], dzs, small_g["ab_conv_w"] = _run(
        lambda ph: _ab_mix_bwd(proj, dcat, ab_norm_v, ab_w_s[0], b_rows, conv_full, ph), pair_exchange("about")
    )
    small_g["ab_b_s"] = dzs.reshape(chunk, heads, da // heads).sum(axis=2).T
    dxs, h, dvecs[0, 1] = _run(
        lambda ph: _proj_mod_bwd(dproj[None], w_of("abin"), saved[0, 1, "x"], vecs[0, 1], dxs, dgate, "ab_in_bwd", ph)
    )
    grad["abin"] = grad_half("abin", h, [dproj], False, "dw_ab_in_send")
    (csum["out01"],) = _split_wait(out01, [grad["abin"]], "reduce_out01_wait")
    chip_sum("out01", carried=(pair_exchange("abin"),))
    csum["about"] = grad_half("about", cat, [dy], True, "dw_ab_out", pair_broadcast("out01"))
    csum["abin"] = grad_half("abin", h, [dproj], True, "dw_ab_in")

    layout = {}
    tail = {}

    def after_01():
        tail["01"] = _split_start(chip_exchange("in01", "abin", "about"), "reduce_01_start")
        return (_after(tail["01"].token),)

    def pack_small_grads():
        dvec_all = jnp.stack([dvecs[l, sub] for l in range(n_layers) for sub in range(3)])
        dgain = dvec_all[:, 0, :]
        dmod = dvec_all[:, 1:4, :].reshape(3 * 3 * n_layers, d)
        rows = {
            "norm_g": (dgain, None, dq), "final_g": (aux[0:1], 0, d), "pool_scale": (small_g["pool_scale"], None, dq),
            "b_mod": (dmod, 0, d), "ab_norm_v": (small_g["ab_norm_v"], 0, da),
            "ab_conv_w": (small_g["ab_conv_w"], None, db // N_CHIPS), "ab_b_s": (small_g["ab_b_s"], 0, chunk),
            "loss": (aux[1:2], 0, d),
        }
        row0 = 0
        for nm, (pc, col0, cols) in rows.items():
            layout[nm] = (row0, pc.shape[0], col0, cols)
            row0 += pc.shape[0]
        packed_rows = -(-row0 // 8) * 8
        return sum(
            jnp.pad(pc, ((layout[nm][0], packed_rows - layout[nm][0] - pc.shape[0]), (0, d - pc.shape[1])))
            for nm, (pc, _, _) in rows.items()
        )

    def bcast_01():
        csum["in01"], csum["abin"], csum["about"] = _split_wait(tail["01"], [dvecs[0, 0]], "reduce_01_wait")
        chip_sum("in01", "abin", "about")
        grads_small = [pack_small_grads(), small_g["ab_w_s"].reshape(heads * chunk, chunk)]
        tail["small"] = _split_start(small_gather("grads", grads_small), "gather_small_grads_start")
        return (pair_broadcast("in01", "abin", "about"), _after(tail["small"].token))

    def reduce_out00():
        tail["out00"] = _split_start(chip_exchange("out00"), "reduce_out00_start")
        return (_after(tail["out00"].token),)

    dxs = ffn_backward(dxs, 0, 0, 0, after_01, bcast_01, reduce_out00)
    grad_x = dxs.reshape(x.shape)

    last = _split_start(chip_exchange("in00"), "reduce_last_start")
    (csum["out00"],) = _split_wait(tail["out00"], [last.token], "reduce_out00_wait")
    chip_sum("out00")
    _flush("broadcast_out00", pair_broadcast("out00"))
    _split_wait(tail["small"], [reduced["w_ffn_out"]], "gather_small_grads_wait")
    g_all, gws_all = small["grads"]

    out = {}

    def adam_stack(st, after=()):
        w3, m3, v3 = stacks[st]
        assert all(u in done for u, (_, ust, _) in units.items() if ust == st), st
        shape = {"w_ffn_in": w_ffn_in.shape, "w_ffn_out": w_ffn_out.shape, "pool_w_grp": pool_w_grp.shape}.get(st, w3.shape)
        out[st] = tuple(a.reshape(shape) for a in _adam_stack(w3, reduced[st], m3, v3, "adam_" + st, after))

    for st in ("w_ffn_out", "ab_w_in", "ab_w_out", "pool_w_grp"):
        adam_stack(st, (last.token,))

    shapes2d = {
        "norm_g": (3 * n_layers, dq), "b_mod": (9 * n_layers, d), "final_g": (1, d), "ab_norm_v": (1, da),
        "pool_scale": (1, dq), "ab_conv_w": (3, db // N_CHIPS), "ab_b_s": (heads, chunk), "ab_w_s": (heads * chunk, chunk),
    }
    small_w = {"norm_g": (norm_g, m_norm_g, v_norm_g), "b_mod": (b_mod, m_b_mod, v_b_mod), "final_g": (final_g, m_final_g, v_final_g),
               "ab_norm_v": (ab_norm_v, m_ab_norm_v, v_ab_norm_v), "pool_scale": (pool_scale, m_pool_scale, v_pool_scale),
               "ab_conv_w": (ab_conv_w, m_ab_conv_w, v_ab_conv_w), "ab_b_s": (ab_b_s, m_ab_b_s, v_ab_b_s), "ab_w_s": (ab_w_s, m_ab_w_s, v_ab_w_s)}
    smalls = {nm: tuple(a.reshape(shapes2d[nm]) for a in wmv) for nm, wmv in small_w.items()}
    small_out, loss = _small_adam(g_all, gws_all, layout, smalls, chip)
    loss = loss.reshape(())
    for nm, res in small_out.items():
        out[nm] = tuple(a.reshape(small_w[nm][0].shape) for a in res)

    mod_row0 = layout["b_mod"][0]
    dmod_all = g_all[:, mod_row0 : mod_row0 + 9 * n_layers, :].reshape(N_DEV, n_layers, 9 * d)
    dmod_cols = lax.dynamic_slice(dmod_all, (0, 0, chip * ncol), (N_DEV, n_layers, ncol)).transpose(1, 0, 2)
    out["w_mod"] = tuple(_mod_bwd_adam(c_all.T, dmod_cols, w_mod, m_w_mod, v_w_mod, (last.token,)))

    (csum["in00"],) = _split_wait(
        last, [out[st][1] for st in ("w_mod", "w_ffn_out", "ab_w_in", "ab_w_out", "pool_w_grp")], "reduce_last_wait"
    )
    chip_sum("in00")
    _flush("broadcast_last", pair_broadcast("in00"))
    adam_stack("w_ffn_in")

    order = ["norm_g", "w_mod", "b_mod", "w_ffn_in", "w_ffn_out", "ab_w_in", "ab_norm_v", "ab_w_s", "ab_b_s", "ab_conv_w", "ab_w_out", "pool_w_grp", "pool_scale", "final_g"]
    return (loss, grad_x, *[out[nm][0] for nm in order], *[out[nm][1] for nm in order], *[out[nm][2] for nm in order], *[out[nm][3] for nm in order])
```

```python
import functools
import math

import jax
import jax.numpy as jnp
from jax import lax
from jax.experimental import pallas as pl
from jax.experimental.pallas import tpu as pltpu

F32 = jnp.float32
BF16 = jnp.bfloat16
MESH = pl.DeviceIdType.MESH

EPS = 1e-6
ADAM_LR = 0.001
ADAM_B1 = 0.9
ADAM_B2 = 0.999
ADAM_EPS = 1e-08
ADAM_WD = 0.01
ADAM_STEP = 10
POOL_WINDOWS = (2, 4, 8, 16)
POOL_HALO = 16
CONV_HALO = 8
N_CHIPS = 4
N_DEV = 8
VMEM_LIMIT_BYTES = 48 * 1024 * 1024
EW_BLOCK_ELEMS = 1024 * 1024
ADAM_BLOCK_ELEMS = 512 * 1024


def _pick(n, prefs):
    for p in prefs:
        if p <= n and n % p == 0:
            return p
    return n


def _row_tile(rows, cols, block_elems=EW_BLOCK_ELEMS):
    best = None
    for d in range(16, rows + 1, 16):
        if rows % d == 0 and d * cols <= block_elems:
            best = d
    return best or rows


def _dot(a, b):
    return jnp.dot(a, b, preferred_element_type=F32)


def _dot_nt(a, b):
    return lax.dot_general(a, b, (((1,), (1,)), ((), ())), preferred_element_type=F32)


def _dot_tn(a, b):
    return lax.dot_general(a, b, (((0,), (0,)), ((), ())), preferred_element_type=F32)


def _sigmoid(x):
    return 0.5 * jnp.tanh(0.5 * x) + 0.5


_GELU_C = math.sqrt(2.0 / math.pi)


def _gelu(x):
    x2 = x * x
    t = jnp.tanh(_GELU_C * (x + 0.044715 * x2 * x))
    val = 0.5 * x * (1.0 + t)
    grad = 0.5 * (1.0 + t) + 0.5 * x * (1.0 - t * t) * (_GELU_C * (1.0 + 3.0 * 0.044715 * x2))
    return val, grad


def _rstd(x):
    return lax.rsqrt(jnp.mean(x * x, axis=-1, keepdims=True) + EPS)


def _modulate(x, vec_ref):
    return (x * _rstd(x)) * vec_ref[0:1, :] * (1.0 + vec_ref[2:3, :]) + vec_ref[1:2, :]


def _modulate_bwd(x, dh, vec_ref, dvec_ref):
    gn, sh, sc = vec_ref[0:1, :], vec_ref[1:2, :], vec_ref[2:3, :]
    rstd = _rstd(x)
    r = x * rstd
    dvec_ref[0:1, :] += jnp.sum(dh * r * (1.0 + sc), axis=0, keepdims=True)
    dvec_ref[1:2, :] += jnp.sum(dh, axis=0, keepdims=True)
    dvec_ref[2:3, :] += jnp.sum(dh * r * gn, axis=0, keepdims=True)
    gm = gn * (1.0 + sc)
    dr = dh * gm
    dx = rstd * (dr - r * jnp.mean(dr * r, axis=-1, keepdims=True))
    return dx, r * gm + sh


def _adam(w, g, m, v):
    m = ADAM_B1 * m + (1.0 - ADAM_B1) * g
    v = ADAM_B2 * v + (1.0 - ADAM_B2) * (g * g)
    m_hat = m / (1.0 - ADAM_B1**ADAM_STEP)
    v_hat = v / (1.0 - ADAM_B2**ADAM_STEP)
    delta = -ADAM_LR * (m_hat / (jnp.sqrt(v_hat) + ADAM_EPS) + ADAM_WD * w)
    return delta, m, v


_ANY = pl.BlockSpec(memory_space=pl.ANY)


class _Phase:
    def __init__(self, ins, out_shapes, aliases, n_sems, start, finish, then):
        self.ins, self.out_shapes, self.aliases, self.n_sems = list(ins), list(out_shapes), dict(aliases), n_sems
        self.start, self.finish, self.then = start, finish, then


def _call(body, name, grid, in_specs, out_specs, out_shape, ins, scratch=(), prefetch=(), phases=(), in_place=None):
    n_pre, n_in, n_out, n_sc = len(prefetch), len(in_specs), len(out_specs), len(scratch)
    ph_in = [len(p.ins) for p in phases]
    ph_out = [len(p.out_shapes) for p in phases]

    def kernel_body(*refs):
        pos = [0]

        def take(k):
            pos[0] += k
            return refs[pos[0] - k : pos[0]]

        pre, ins_ = take(n_pre), take(n_in)
        p_ins = [take(k) for k in ph_in]
        outs_ = take(n_out)
        p_outs = [take(k) for k in ph_out]
        sc = take(n_sc)
        sems = [take(2) for _ in phases]
        if phases:
            ids = [pl.program_id(a) for a in range(len(grid))]
            first = functools.reduce(jnp.logical_and, [i == 0 for i in ids])
            last = functools.reduce(jnp.logical_and, [i == g - 1 for i, g in zip(ids, grid)])

            @pl.when(first)
            def _():
                for p, pi, po, (send, recv) in zip(phases, p_ins, p_outs, sems):
                    p.start(pi, po, send, recv)

        if body is not None:
            body(*pre, *ins_, *outs_, *sc)
        if phases:

            @pl.when(last)
            def _():
                for p, pi, po, (send, recv) in zip(phases, p_ins, p_outs, sems):
                    p.finish(pi, po, send, recv)

    aliases = {n_pre + i: o for i, o in (in_place or {}).items()}
    i0, o0 = n_pre + n_in, n_out
    for p in phases:
        for i, o in p.aliases.items():
            aliases[i0 + i] = o0 + o
        i0 += len(p.ins)
        o0 += len(p.out_shapes)
    all_in = list(in_specs) + [_ANY] * sum(ph_in)
    all_out = list(out_specs) + [_ANY] * sum(ph_out)
    all_scratch = list(scratch)
    for p in phases:
        all_scratch += [pltpu.SemaphoreType.DMA((p.n_sems,)), pltpu.SemaphoreType.DMA((p.n_sems,))]
    shapes = list(out_shape) + [s for p in phases for s in p.out_shapes]
    operands = list(prefetch) + list(ins) + [a for p in phases for a in p.ins]
    sem = ("arbitrary",) * len(grid)
    params = pltpu.CompilerParams(dimension_semantics=sem, vmem_limit_bytes=VMEM_LIMIT_BYTES)
    if n_pre:
        res = pl.pallas_call(
            kernel_body, name=name, out_shape=shapes, input_output_aliases=aliases, compiler_params=params,
            grid_spec=pltpu.PrefetchScalarGridSpec(
                num_scalar_prefetch=n_pre, grid=grid, in_specs=all_in, out_specs=all_out, scratch_shapes=all_scratch
            ),
        )(*operands)
    else:
        res = pl.pallas_call(
            kernel_body, name=name, grid=grid, in_specs=all_in, out_specs=all_out, out_shape=shapes,
            scratch_shapes=all_scratch, input_output_aliases=aliases, compiler_params=params,
        )(*operands)
    res = list(res)
    outs, rest = res[:n_out], res[n_out:]
    p_res = []
    for k in ph_out:
        p_res.append(rest[:k])
        rest = rest[k:]
    return outs, p_res


def _place():
    return lax.axis_index("x"), lax.axis_index("y"), lax.axis_index("c")


def _other_chips():
    x, y, _ = _place()
    return [(1 - x, y), (x, 1 - y), (1 - x, 1 - y)]


def _flip(k):
    x, y, c = _place()
    return (1 - x if k & 4 else x, 1 - y if k & 2 else y, 1 - c if k & 1 else c)


def _remote(src, dst, send, recv, k, to):
    return pltpu.make_async_remote_copy(
        src_ref=src, dst_ref=dst, send_sem=send.at[k], recv_sem=recv.at[k], device_id=to, device_id_type=MESH
    )


def _phase_small_gather(arrs, then):
    n = len(arrs)

    def copies(ins, outs, send, recv):
        x, y, c = _place()
        me = 4 * x + 2 * y + c
        local = [pltpu.make_async_copy(ins[a], outs[a].at[me], send.at[a * N_DEV]) for a in range(n)]
        remote = [_remote(ins[a], outs[a].at[me], send, recv, a * N_DEV + k, _flip(k)) for a in range(n) for k in range(1, N_DEV)]
        return local, remote

    def start(ins, outs, send, recv):
        local, remote = copies(ins, outs, send, recv)
        for cp in local + remote:
            cp.start()

    def finish(ins, outs, send, recv):
        local, remote = copies(ins, outs, send, recv)
        for cp in remote + local:
            cp.wait()

    shapes = [jax.ShapeDtypeStruct((N_DEV,) + a.shape, a.dtype) for a in arrs]
    return _Phase(arrs, shapes, {}, n * N_DEV, start, finish, then)


def _phase_small_exchange(arr, then):
    def copies(ins, outs, send, recv):
        x, y, c = _place()
        me = 4 * x + 2 * y + c
        local = pltpu.make_async_copy(ins[0].at[me], outs[0].at[me], send.at[0])
        remote = []
        for k in range(1, N_DEV):
            px, py, pc = _flip(k)
            remote.append(_remote(ins[0].at[4 * px + 2 * py + pc], outs[0].at[me], send, recv, k, (px, py, pc)))
        return [local] + remote

    def start(ins, outs, send, recv):
        for cp in copies(ins, outs, send, recv):
            cp.start()

    def finish(ins, outs, send, recv):
        for cp in copies(ins, outs, send, recv):
            cp.wait()

    return _Phase([arr], [jax.ShapeDtypeStruct(arr.shape, arr.dtype)], {}, N_DEV, start, finish, then)


def _after(*arrs):
    nothing = lambda *args: None
    return _Phase(arrs, [], {}, 1, nothing, nothing, nothing)


def _flush(name, *phases):
    _, p_outs = _call(None, name, (1,), [], [], [], [], phases=list(phases))
    for p, po in zip(phases, p_outs):
        p.then(po)


class _Big:
    KINDS = {"full": (True, True), "half": (True, False), "shard": (False, True), "block": (False, False)}

    def __init__(self, f3, s3, h3):
        assert s3 != h3
        self.f3, self.s3, self.h3 = tuple(f3), s3, h3
        self.bd = tuple(f3[a] // (N_CHIPS if a == s3 else 1) // (2 if a == h3 else 1) for a in range(3))
        self.tile = (1, _row_tile(self.bd[1], self.bd[2]), self.bd[2])
        self.grid = tuple(self.bd[a] // self.tile[a] for a in range(3))

    def dims(self, kind):
        chips, halves = self.KINDS[kind]
        return tuple(
            self.bd[a] * (N_CHIPS if chips and a == self.s3 else 1) * (2 if halves and a == self.h3 else 1) for a in range(3)
        )

    def view(self, ref, chip=None, half=None, batch0=0, both_halves=True, part=None):
        start = [batch0, 0, 0]
        size = list(ref.shape)
        size[0] = self.bd[0] * (2 if self.h3 == 0 and both_halves else 1)
        if chip is not None:
            start[self.s3] += chip * self.bd[self.s3]
            size[self.s3] = self.bd[self.s3]
        if half is not None:
            start[self.h3] += half * self.bd[self.h3]
            size[self.h3] = self.bd[self.h3]
        if part is not None:
            size[1] //= 2
            start[1] += part * size[1]
        return ref.at[tuple(pl.ds(st, sz) for st, sz in zip(start, size))]

    def spec(self, chip_from=None, half_from=None, lead=(), batch0=0):
        extra = "grid" in (chip_from, half_from)

        def index(*args):
            pref, idx = args[-1], list(args[int(extra) : -1])
            idx[0] += batch0
            if chip_from:
                idx[self.s3] += (pref[0] if chip_from == "pref" else args[0]) * self.grid[self.s3]
            if half_from:
                idx[self.h3] += (pref[1] if half_from == "pref" else args[0]) * self.grid[self.h3]
            return (0,) * len(lead) + tuple(idx)

        return pl.BlockSpec(tuple(lead) + self.tile, index)


def _same(arrs):
    return [jax.ShapeDtypeStruct(a.shape, a.dtype) for a in arrs]


def _phase_gather_relay(arrs, bigs, second, then):
    n = len(arrs)
    per = 4 if second else 2

    def copies(outs, send, recv, arriving):
        x, y, c = _place()
        xn, yn, dg = (1 - x, y), (x, 1 - y), (1 - x, 1 - y)
        if not second:
            plan = [((xn if arriving else (x, y)), 0, xn), ((yn if arriving else (x, y)), 1, yn)]
        elif arriving:
            plan = [(yn, 0, yn), (dg, 0, yn), (xn, 1, xn), (dg, 1, xn)]
        else:
            plan = [((x, y), 0, yn), (xn, 0, yn), ((x, y), 1, xn), (yn, 1, xn)]
        res = []
        for a in range(n):
            for k, (chip, part, to) in enumerate(plan):
                blk = bigs[a].view(outs[a], 2 * chip[0] + chip[1], c, part=part)
                res.append(_remote(blk, blk, send, recv, per * a + k, (*to, c)))
        return res

    def start(ins, outs, send, recv):
        for cp in copies(outs, send, recv, False):
            cp.start()

    def finish(ins, outs, send, recv):
        for cp in copies(outs, send, recv, True):
            cp.wait_recv()
        for cp in copies(outs, send, recv, False):
            cp.wait_send()

    return _Phase(arrs, _same(arrs), {a: a for a in range(n)}, per * n, start, finish, then)


def _phase_gather_sibling(arrs, bigs, then):
    n = len(arrs)

    def copies(outs, send, recv, arriving):
        x, y, c = _place()
        return [
            _remote(blk, blk, send, recv, 3 * a + j, (x, y, 1 - c))
            for j, chip in enumerate(_other_chips())
            for a in range(n)
            for blk in [bigs[a].view(outs[a], 2 * chip[0] + chip[1], 1 - c if arriving else c)]
        ]

    def start(ins, outs, send, recv):
        for cp in copies(outs, send, recv, False):
            cp.start()

    def finish(ins, outs, send, recv):
        for cp in copies(outs, send, recv, True):
            cp.wait_recv()
        for cp in copies(outs, send, recv, False):
            cp.wait_send()

    return _Phase(arrs, _same(arrs), {a: a for a in range(n)}, 3 * n, start, finish, then)


def _phase_pair_exchange(grads, bigs, then):
    n = len(grads)

    def copies(ins, outs, send, recv):
        x, y, c = _place()
        srcs = [ins[a] if ins[a].shape == outs[a].shape else bigs[a].view(ins[a], None, 1 - c) for a in range(n)]
        return [_remote(srcs[a], outs[a], send, recv, a, (x, y, 1 - c)) for a in range(n)]

    def start(ins, outs, send, recv):
        for cp in copies(ins, outs, send, recv):
            cp.start()

    def finish(ins, outs, send, recv):
        for cp in copies(ins, outs, send, recv):
            cp.wait()

    shapes = [jax.ShapeDtypeStruct(b.dims("half"), BF16) for b in bigs]
    return _Phase(grads, shapes, {}, n, start, finish, then)


def _phase_chip_exchange(sums, bigs, then):
    n = len(sums)

    def copies(ins, outs, send, recv):
        _, _, c = _place()
        return [
            _remote(bigs[a].view(ins[a], 2 * chip[0] + chip[1], both_halves=False), outs[a].at[j], send, recv, 3 * a + j, (*chip, c))
            for j, chip in enumerate(_other_chips())
            for a in range(n)
        ]

    def start(ins, outs, send, recv):
        for cp in copies(ins, outs, send, recv):
            cp.start()

    def finish(ins, outs, send, recv):
        for cp in copies(ins, outs, send, recv):
            cp.wait()

    shapes = [jax.ShapeDtypeStruct((N_CHIPS - 1,) + b.dims("block"), BF16) for b in bigs]
    return _Phase(sums, shapes, {}, 3 * n, start, finish, then)


_HBM = pl.BlockSpec(memory_space=pltpu.HBM)
_SEM = pl.BlockSpec(memory_space=pltpu.SEMAPHORE)
_DATAFLOW = pltpu.SideEffectType.DATAFLOW_SIDE_EFFECTING


class _InFlight:
    def __init__(self, phase, send, recv, arrays, token):
        self.phase, self.send, self.recv, self.arrays, self.token = phase, send, recv, arrays, token


def _phase_results(phase, refs):
    n_in = len(phase.ins)
    updated = {o: i for i, o in phase.aliases.items()}
    fresh = [o for o in range(len(phase.out_shapes)) if o not in updated]
    return [refs[updated[o]] if o in updated else refs[n_in + fresh.index(o)] for o in range(len(phase.out_shapes))]


def _split_start(phase, name):
    n_in = len(phase.ins)
    fresh = [s for o, s in enumerate(phase.out_shapes) if o not in phase.aliases.values()]
    arrays = list(phase.ins) + [lax.empty(s.shape, s.dtype) for s in fresh]
    n = len(arrays)

    def body(*refs):
        phase.start(refs[:n_in], _phase_results(phase, refs[:n]), refs[n], refs[n + 1])
        refs[-1][...] = jnp.zeros_like(refs[-1])

    operands = [pltpu.with_memory_space_constraint(a, pltpu.HBM) for a in arrays]
    res = pl.pallas_call(
        body, name=name,
        out_shape=[pltpu.SemaphoreType.DMA((phase.n_sems,)), pltpu.SemaphoreType.DMA((phase.n_sems,))]
        + [pltpu.HBM(a.shape, a.dtype) for a in arrays] + [jax.ShapeDtypeStruct((8, 128), F32)],
        in_specs=[_HBM] * n, out_specs=[_SEM, _SEM] + [_HBM] * n + [pl.BlockSpec(memory_space=pltpu.VMEM)],
        input_output_aliases={i: 2 + i for i in range(n)},
        compiler_params=pltpu.CompilerParams(has_side_effects=_DATAFLOW),
    )(*operands)
    return _InFlight(phase, res[0], res[1], list(res[2 : 2 + n]), res[-1])


def _split_wait(flight, after, name):
    phase, n = flight.phase, len(flight.arrays)
    n_in = len(phase.ins)

    def body(*refs):
        phase.finish(refs[:n_in], _phase_results(phase, refs[:n]), refs[n], refs[n + 1])

    res = pl.pallas_call(
        body, name=name, out_shape=[pltpu.HBM(a.shape, a.dtype) for a in flight.arrays],
        in_specs=[_HBM] * n + [_SEM, _SEM] + [_ANY] * len(after), out_specs=[_HBM] * n,
        input_output_aliases={i: i for i in range(n)},
        compiler_params=pltpu.CompilerParams(has_side_effects=_DATAFLOW),
    )(*flight.arrays, flight.send, flight.recv, *after)
    res = list(res)
    phase.then(_phase_results(phase, res))
    return res[:n_in]


def _phase_pair_broadcast(stacks, bigs, batch0s, then):
    n = len(stacks)

    def start(ins, outs, send, recv):
        x, y, c = _place()
        for a in range(n):
            blk = bigs[a].view(outs[a], None, c, batch0s[a])
            _remote(blk, blk, send, recv, a, (x, y, 1 - c)).start()

    def finish(ins, outs, send, recv):
        x, y, c = _place()
        for a in range(n):
            mine = bigs[a].view(outs[a], None, c, batch0s[a])
            theirs = bigs[a].view(outs[a], None, 1 - c, batch0s[a])
            _remote(mine, mine, send, recv, a, (x, y, 1 - c)).wait_send()
            _remote(theirs, theirs, send, recv, a, (x, y, 1 - c)).wait_recv()

    return _Phase(stacks, _same(stacks), {a: a for a in range(n)}, n, start, finish, then)


def _tile_call(body, name, big, where, extra, ins, in_specs, out_specs, out_shape, phases=()):
    grid = ((extra,) if extra else ()) + big.grid
    return _call(body, name, grid, in_specs, out_specs, out_shape, ins, prefetch=(where,), phases=phases)


def _cast_into_full(w_stack, batch0, big, where, name, phases=()):
    def body(_, w_ref, o_ref):
        o_ref[...] = w_ref[...].astype(BF16)

    return _tile_call(
        body, name, big, where, 2, [w_stack], [big.spec(None, "grid", batch0=batch0)], [big.spec("pref", "grid")],
        [jax.ShapeDtypeStruct(big.dims("full"), BF16)], phases,
    )


def _pair_sum(g_full, recv_half, big, where, name, phases=()):
    def body(_, g_ref, r_ref, o_ref):
        o_ref[...] = (g_ref[...].astype(F32) + r_ref[...].astype(F32)).astype(BF16)

    half = big.spec("grid", None)
    return _tile_call(
        body, name, big, where, N_CHIPS, [g_full, recv_half], [big.spec("grid", "pref"), half], [half],
        [jax.ShapeDtypeStruct(big.dims("half"), BF16)], phases,
    )


def _chip_sum(chip_sum, parts, big, where, stack, stack_shape, batch0, name, phases=()):
    def body(_, own_ref, p_ref, *rest):
        acc = own_ref[...].astype(F32)
        for k in range(N_CHIPS - 1):
            acc = acc + p_ref[k].astype(F32)
        rest[-1][...] = acc

    ins = [chip_sum, parts] + ([stack] if stack is not None else [])
    in_specs = [big.spec("pref", None), big.spec(None, None, lead=(N_CHIPS - 1,))] + ([_ANY] if stack is not None else [])
    return _call(
        body, name, big.grid, in_specs, [big.spec(None, "pref", batch0=batch0)], [jax.ShapeDtypeStruct(stack_shape, F32)], ins,
        prefetch=(where,), phases=phases, in_place={2: 0} if stack is not None else None,
    )


def _adam_stack(w, g, m, v, name, after=()):
    b, r, c = w.shape
    tr = _row_tile(r, c, ADAM_BLOCK_ELEMS)

    def body(w_ref, g_ref, m_ref, v_ref, *rest):
        go_ref, d_ref, mo_ref, vo_ref = rest[-4:]
        gv = g_ref[...]
        d, mo, vo = _adam(w_ref[...], gv, m_ref[...], v_ref[...])
        go_ref[...] = gv
        d_ref[...] = d
        mo_ref[...] = mo
        vo_ref[...] = vo

    spec = pl.BlockSpec((1, tr, c), lambda bb, i: (bb, i, 0))
    outs, _ = _call(
        body, name, (b, r // tr), [spec] * 4 + [_ANY] * len(after), [spec] * 4, [jax.ShapeDtypeStruct(w.shape, F32)] * 4,
        [w, g, m, v, *after],
    )
    return outs


def _mod_fwd(c_all, w_mod, b_cols, phases=()):
    n_layers, d, n = w_mod.shape
    tn = _pick(n, (768, 512, 384, 256, 128))

    def body(c_ref, w_ref, b_ref, o_ref):
        cv = c_ref[...]
        ca = (cv * _sigmoid(cv)).astype(BF16)
        o_ref[0] = _dot(ca, w_ref[0].astype(BF16)) + b_ref[0]

    return _call(
        body, "mod_fwd", (n_layers, n // tn),
        [
            pl.BlockSpec((N_DEV, d), lambda l, j: (0, 0)),
            pl.BlockSpec((1, d, tn), lambda l, j: (l, 0, j)),
            pl.BlockSpec((1, 1, tn), lambda l, j: (l, 0, j)),
        ],
        [pl.BlockSpec((1, N_DEV, tn), lambda l, j: (l, 0, j))],
        [jax.ShapeDtypeStruct((n_layers, N_DEV, n), F32)], [c_all, w_mod, b_cols], phases=phases,
    )


def _mod_bwd_adam(c_all_t, dmod_cols, w, m, v, after=()):
    n_layers, d, n = w.shape
    tn = _pick(n, (384, 256, 128))

    def body(c_ref, dm_ref, w_ref, m_ref, v_ref, *rest):
        g_ref, d_ref, mo_ref, vo_ref = rest[-4:]
        cv = c_ref[...]
        ca = (cv * _sigmoid(cv)).astype(BF16)
        g = _dot(ca, dm_ref[0].astype(BF16))
        g_ref[0] = g
        dl, mo, vo = _adam(w_ref[0], g, m_ref[0], v_ref[0])
        d_ref[0] = dl
        mo_ref[0] = mo
        vo_ref[0] = vo

    wspec = pl.BlockSpec((1, d, tn), lambda l, j: (l, 0, j))
    outs, _ = _call(
        body, "mod_bwd_adam", (n_layers, n // tn),
        [pl.BlockSpec((d, N_DEV), lambda l, j: (0, 0)), pl.BlockSpec((1, N_DEV, tn), lambda l, j: (l, 0, j)), wspec, wspec, wspec]
        + [_ANY] * len(after),
        [wspec] * 4, [jax.ShapeDtypeStruct(w.shape, F32)] * 4, [c_all_t, dmod_cols, w, m, v, *after],
    )
    return outs


def _ffn_fwd(x, vec, w_in, w_out, name, phases=()):
    s, d = x.shape
    f = w_out.shape[1]
    tm = _pick(s, (1024, 512, 256, 128))
    tf = _pick(f, (256, 128))
    nf = f // tf

    def body(x_ref, vec_ref, wg_ref, wu_ref, wo_ref, xo_ref, g_ref, u_ref, y_ref, h_sc, acc_sc):
        j = pl.program_id(1)

        @pl.when(j == 0)
        def _():
            h_sc[...] = _modulate(x_ref[...], vec_ref).astype(BF16)
            acc_sc[...] = jnp.zeros_like(acc_sc)

        h = h_sc[...]
        g = _dot(h, wg_ref[0])
        u = _dot(h, wu_ref[0])
        g_ref[...] = g.astype(BF16)
        u_ref[...] = u.astype(BF16)
        a = (g * _sigmoid(g) * u).astype(BF16)
        acc_sc[...] += _dot(a, wo_ref[0])

        @pl.when(j == nf - 1)
        def _():
            yv = acc_sc[...]
            xo_ref[...] = x_ref[...] + 0.5 * vec_ref[3:4, :] * yv
            y_ref[...] = yv.astype(BF16)

    row = pl.BlockSpec((tm, d), lambda i, j: (i, 0))
    hid = pl.BlockSpec((tm, tf), lambda i, j: (i, j))
    return _call(
        body, name, (s // tm, nf),
        [
            row,
            pl.BlockSpec((8, d), lambda i, j: (0, 0)),
            pl.BlockSpec((1, d, tf), lambda i, j: (0, 0, j)),
            pl.BlockSpec((1, d, tf), lambda i, j: (0, 0, nf + j)),
            pl.BlockSpec((1, tf, d), lambda i, j: (0, j, 0)),
        ],
        [row, hid, hid, row],
        [
            jax.ShapeDtypeStruct((s, d), F32),
            jax.ShapeDtypeStruct((s, f), BF16),
            jax.ShapeDtypeStruct((s, f), BF16),
            jax.ShapeDtypeStruct((s, d), BF16),
        ],
        [x, vec, w_in, w_in, w_out],
        scratch=[pltpu.VMEM((tm, d), BF16), pltpu.VMEM((tm, d), F32)], phases=phases,
    )


def _ffn_bwd(dxo, x, vec, gg, uu, y, w_in, w_out, name, phases=()):
    s, d = x.shape
    f = w_out.shape[1]
    tm = _pick(s, (512, 256, 128))
    tf = _pick(f, (256, 128))
    nf = f // tf

    def body(dxo_ref, x_ref, vec_ref, g_ref, u_ref, y_ref, wg_ref, wu_ref, wo_ref,
             dx_ref, dg_ref, du_ref, a_ref, h_ref, dy_ref, dvec_ref, acc_sc):
        i, j = pl.program_id(0), pl.program_id(1)

        @pl.when((i == 0) & (j == 0))
        def _():
            dvec_ref[...] = jnp.zeros_like(dvec_ref)

        @pl.when(j == 0)
        def _():
            dxo_v = dxo_ref[...]
            dy_ref[...] = (0.5 * vec_ref[3:4, :] * dxo_v).astype(BF16)
            dvec_ref[3:4, :] += 0.5 * jnp.sum(dxo_v * y_ref[...].astype(F32), axis=0, keepdims=True)
            acc_sc[...] = jnp.zeros_like(acc_sc)

        da = _dot_nt(dy_ref[...], wo_ref[0])
        g = g_ref[...].astype(F32)
        u = u_ref[...].astype(F32)
        sig = _sigmoid(g)
        sl = g * sig
        a_ref[...] = (sl * u).astype(BF16)
        dg = (da * u * (sig * (1.0 + g * (1.0 - sig)))).astype(BF16)
        du = (da * sl).astype(BF16)
        dg_ref[...] = dg
        du_ref[...] = du
        acc_sc[...] += _dot_nt(dg, wg_ref[0]) + _dot_nt(du, wu_ref[0])

        @pl.when(j == nf - 1)
        def _():
            dx, h = _modulate_bwd(x_ref[...], acc_sc[...], vec_ref, dvec_ref)
            dx_ref[...] = dxo_ref[...] + dx
            h_ref[...] = h.astype(BF16)

    row = pl.BlockSpec((tm, d), lambda i, j: (i, 0))
    hid = pl.BlockSpec((tm, tf), lambda i, j: (i, j))
    vecs = pl.BlockSpec((8, d), lambda i, j: (0, 0))
    return _call(
        body, name, (s // tm, nf),
        [
            row, row, vecs, hid, hid, row,
            pl.BlockSpec((1, d, tf), lambda i, j: (0, 0, j)),
            pl.BlockSpec((1, d, tf), lambda i, j: (0, 0, nf + j)),
            pl.BlockSpec((1, tf, d), lambda i, j: (0, j, 0)),
        ],
        [row, hid, hid, hid, row, row, vecs],
        [
            jax.ShapeDtypeStruct((s, d), F32),
            jax.ShapeDtypeStruct((s, f), BF16),
            jax.ShapeDtypeStruct((s, f), BF16),
            jax.ShapeDtypeStruct((s, f), BF16),
            jax.ShapeDtypeStruct((s, d), BF16),
            jax.ShapeDtypeStruct((s, d), BF16),
            jax.ShapeDtypeStruct((8, d), F32),
        ],
        [dxo, x, vec, gg, uu, y, w_in, w_in, w_out],
        scratch=[pltpu.VMEM((tm, d), F32)], phases=phases,
    )


def _grad_half(a, bs, big, where, mine, recv, name, phases=()):
    s, k1 = a.shape
    n = bs[0].shape[1]
    groups = len(bs)
    rows_halved = big.h3 == 1
    assert rows_halved or groups == 1
    kk, nn = (k1 // 2, n) if rows_halved else (k1, n // 2)
    tk = _pick(kk, (1408, 1024, 512, 256, 128))
    tn = _pick(nn, (1408, 1024, 640, 512, 256, 128))
    nkb, nnb = kk // tk, nn // tn
    assert (recv is None) == (not mine)

    def half(pref):
        return pref[1] if mine else 1 - pref[1]

    def body(_, a_ref, *rest):
        q = pl.program_id(1)
        for p in range(groups):

            @pl.when(q == p)
            def _(p=p):
                acc = _dot_tn(a_ref[...], rest[p][...])
                if recv is not None:
                    acc = acc + rest[groups][0].astype(F32)
                rest[-1][0] = acc.astype(BF16)

    def b_block(p):
        def index(i, q, j, pref):
            jj = jnp.where(q == p, j, jnp.where(q < p, 0, nnb - 1))
            return (0, jj + (0 if rows_halved else half(pref) * nnb))

        return pl.BlockSpec((s, tn), index)

    out_spec = pl.BlockSpec((1, tk, tn), lambda i, q, j, pref: (0, i, q * nnb + j))
    in_specs = [pl.BlockSpec((s, tk), lambda i, q, j, pref: (0, i + (half(pref) * nkb if rows_halved else 0)))]
    in_specs += [b_block(p) for p in range(groups)]
    ins = [a, *bs]
    if recv is not None:
        in_specs.append(out_spec)
        ins.append(recv)
    return _call(
        body, name, (nkb, groups, nnb), in_specs, [out_spec], [jax.ShapeDtypeStruct(big.dims("half"), BF16)], ins,
        prefetch=(where,), phases=phases,
    )


def _proj_mod_fwd(x, vec, w, phases=()):
    s, d = x.shape
    n = w.shape[2]
    tm = _pick(s, (1024, 512, 256, 128))
    tn = _pick(n, (640, 512, 256, 128))

    def body(x_ref, vec_ref, w_ref, o_ref, h_sc):
        @pl.when(pl.program_id(1) == 0)
        def _():
            h_sc[...] = _modulate(x_ref[...], vec_ref).astype(BF16)

        o_ref[...] = _dot(h_sc[...], w_ref[0])

    return _call(
        body, "ab_in_fwd", (s // tm, n // tn),
        [
            pl.BlockSpec((tm, d), lambda i, j: (i, 0)),
            pl.BlockSpec((8, d), lambda i, j: (0, 0)),
            pl.BlockSpec((1, d, tn), lambda i, j: (0, 0, j)),
        ],
        [pl.BlockSpec((tm, tn), lambda i, j: (i, j))],
        [jax.ShapeDtypeStruct((s, n), F32)], [x, vec, w],
        scratch=[pltpu.VMEM((tm, d), BF16)], phases=phases,
    )


def _proj_res_fwd(a, w, x, vec, phases=()):
    s, kd = a.shape
    d = x.shape[1]
    tm = _pick(s, (1024, 512, 256, 128))

    def body(a_ref, w_ref, x_ref, vec_ref, xo_ref, y_ref):
        yv = _dot(a_ref[...], w_ref[0])
        xo_ref[...] = x_ref[...] + vec_ref[3:4, :] * yv
        y_ref[...] = yv.astype(BF16)

    row = pl.BlockSpec((tm, d), lambda i: (i, 0))
    return _call(
        body, "ab_out_fwd", (s // tm,),
        [pl.BlockSpec((tm, kd), lambda i: (i, 0)), pl.BlockSpec((1, kd, d), lambda i: (0, 0, 0)), row, pl.BlockSpec((8, d), lambda i: (0, 0))],
        [row, row],
        [jax.ShapeDtypeStruct((s, d), F32), jax.ShapeDtypeStruct((s, d), BF16)], [a, w, x, vec], phases=phases,
    )


def _proj_res_bwd(dxo, y, vec, w, phases=()):
    s, d = dxo.shape
    kd = w.shape[1]
    tm = _pick(s, (1024, 512, 256, 128))

    def body(dxo_ref, y_ref, vec_ref, w_ref, dy_ref, da_ref, dgate_ref):
        @pl.when(pl.program_id(0) == 0)
        def _():
            dgate_ref[...] = jnp.zeros_like(dgate_ref)

        dxo_v = dxo_ref[...]
        dy = (vec_ref[3:4, :] * dxo_v).astype(BF16)
        dy_ref[...] = dy
        dgate_ref[3:4, :] += jnp.sum(dxo_v * y_ref[...].astype(F32), axis=0, keepdims=True)
        da_ref[...] = _dot_nt(dy, w_ref[0]).astype(BF16)

    row = pl.BlockSpec((tm, d), lambda i: (i, 0))
    vecs = pl.BlockSpec((8, d), lambda i: (0, 0))
    return _call(
        body, "ab_out_bwd", (s // tm,),
        [row, row, vecs, pl.BlockSpec((1, kd, d), lambda i: (0, 0, 0))],
        [row, pl.BlockSpec((tm, kd), lambda i: (i, 0)), vecs],
        [jax.ShapeDtypeStruct((s, d), BF16), jax.ShapeDtypeStruct((s, kd), BF16), jax.ShapeDtypeStruct((8, d), F32)],
        [dxo, y, vec, w], phases=phases,
    )


def _proj_mod_bwd(dproj, w, x, vec, dxo, dvec_in, name, phases=()):
    parts, s, n_part = dproj.shape
    d = x.shape[1]
    tm = _pick(s, (512, 256, 128))
    tk = _pick(n_part, (1408, 1280, 1024, 512, 256, 128))
    per_part = n_part // tk
    nk = parts * per_part

    def body(dp_ref, w_ref, x_ref, vec_ref, dxo_ref, dvi_ref, dx_ref, h_ref, dvec_ref, acc_sc):
        i, k = pl.program_id(0), pl.program_id(1)

        @pl.when((i == 0) & (k == 0))
        def _():
            dvec_ref[...] = dvi_ref[...]

        @pl.when(k == 0)
        def _():
            acc_sc[...] = jnp.zeros_like(acc_sc)

        acc_sc[...] += _dot_nt(dp_ref[0], w_ref[0])

        @pl.when(k == nk - 1)
        def _():
            dx, h = _modulate_bwd(x_ref[...], acc_sc[...], vec_ref, dvec_ref)
            dx_ref[...] = dxo_ref[...] + dx
            h_ref[...] = h.astype(BF16)

    row = pl.BlockSpec((tm, d), lambda i, k: (i, 0))
    vecs = pl.BlockSpec((8, d), lambda i, k: (0, 0))
    return _call(
        body, name, (s // tm, nk),
        [
            pl.BlockSpec((1, tm, tk), lambda i, k: (k // per_part, i, k % per_part)),
            pl.BlockSpec((1, d, tk), lambda i, k: (0, 0, k)),
            row, vecs, row, vecs,
        ],
        [row, row, vecs],
        [jax.ShapeDtypeStruct((s, d), F32), jax.ShapeDtypeStruct((s, d), BF16), jax.ShapeDtypeStruct((8, d), F32)],
        [dproj, w, x, vec, dxo, dvec_in], scratch=[pltpu.VMEM((tm, d), F32)], phases=phases,
    )


def _tril(n):
    return lax.broadcasted_iota(jnp.int32, (n, n), 0) >= lax.broadcasted_iota(jnp.int32, (n, n), 1)


def _layernorm_stats(gv):
    mu = jnp.mean(gv, axis=-1, keepdims=True)
    cen = gv - mu
    rstd = lax.rsqrt(jnp.mean(cen * cen, axis=-1, keepdims=True) + EPS)
    return cen * rstd, rstd


def _shift_down(q, k, above_ref, c_cg, c_xb, first):
    width = q.shape[1]
    rows = lax.broadcasted_iota(jnp.int32, q.shape, 0)
    out = pltpu.roll(q, k, 0)
    for r in range(k):
        src = CONV_HALO - k + r
        above = above_ref[src : src + 1, c_cg : c_cg + width] * above_ref[src : src + 1, c_xb : c_xb + width]
        above = jnp.where(first, 0.0, above)
        out = jnp.where(rows == r, above, out)
    return out


def _ab_mix_fwd(proj, norm_v, w_s, b_rows, conv_w, phases=()):
    s, n = proj.shape
    heads, chunk, _ = w_s.shape
    da = norm_v.shape[1]
    hd = da // heads
    db = conv_w.shape[1]
    tm = _pick(s, (512, 256, 128))

    def body(p_ref, ph_ref, nv_ref, ws_ref, b_ref, cw_ref, o_ref):
        first = pl.program_id(0) == 0
        gu, _ = _gelu(p_ref[:, 0:da])
        gv, _ = _gelu(p_ref[:, da : 2 * da])
        xhat, _ = _layernorm_stats(gv)
        vn = (xhat * nv_ref[...]).astype(BF16)
        mask = _tril(chunk)
        for hh in range(heads):
            wm = jnp.where(mask, ws_ref[hh], 0.0).astype(BF16)
            cols = slice(hh * hd, (hh + 1) * hd)
            for nn in range(tm // chunk):
                rows = slice(nn * chunk, (nn + 1) * chunk)
                z = _dot(wm, vn[rows, cols]) + b_ref[:, cols]
                o_ref[rows, cols] = (gu[rows, cols] * z).astype(BF16)
        c_cg, c_xb = 2 * da + db, 2 * da + 2 * db
        bg = p_ref[:, 2 * da : 2 * da + db]
        q = p_ref[:, c_cg : c_cg + db] * p_ref[:, c_xb : c_xb + db]
        q1 = _shift_down(q, 1, ph_ref, c_cg, c_xb, first)
        q2 = _shift_down(q, 2, ph_ref, c_cg, c_xb, first)
        conv = cw_ref[0:1, :] * q2 + cw_ref[1:2, :] * q1 + cw_ref[2:3, :] * q
        o_ref[:, da : da + db] = (bg * conv).astype(BF16)

    nh = tm // CONV_HALO
    return _call(
        body, "ab_mix_fwd", (s // tm,),
        [
            pl.BlockSpec((tm, n), lambda i: (i, 0)),
            pl.BlockSpec((CONV_HALO, n), lambda i: (jnp.maximum(i * nh - 1, 0), 0)),
            pl.BlockSpec((1, da), lambda i: (0, 0)),
            pl.BlockSpec((heads, chunk, chunk), lambda i: (0, 0, 0)),
            pl.BlockSpec((chunk, da), lambda i: (0, 0)),
            pl.BlockSpec((3, db), lambda i: (0, 0)),
        ],
        [pl.BlockSpec((tm, da + db), lambda i: (i, 0))],
        [jax.ShapeDtypeStruct((s, da + db), BF16)], [proj, proj, norm_v, w_s, b_rows, conv_w], phases=phases,
    )


def _ab_mix_bwd(proj, dcat, norm_v, w_s, b_rows, conv_w, phases=()):
    s, n = proj.shape
    heads, chunk, _ = w_s.shape
    da = norm_v.shape[1]
    hd = da // heads
    db = conv_w.shape[1]
    tm = _pick(s, (512, 256, 128))
    nblk = s // tm
    dhalo = 2 * CONV_HALO

    def body(p_ref, pa_ref, pb_ref, dc_ref, dcb_ref, nv_ref, ws_ref, b_ref, cw_ref,
             dp_ref, dnv_ref, dws_ref, dzs_ref, dcw_ref, dvn_sc):
        i = pl.program_id(0)
        first, last = i == 0, i == nblk - 1

        @pl.when(first)
        def _():
            dnv_ref[...] = jnp.zeros_like(dnv_ref)
            dws_ref[...] = jnp.zeros_like(dws_ref)
            dzs_ref[...] = jnp.zeros_like(dzs_ref)
            dcw_ref[...] = jnp.zeros_like(dcw_ref)

        uu = p_ref[:, 0:da]
        gu, gu_grad = _gelu(uu)
        gv, gv_grad = _gelu(p_ref[:, da : 2 * da])
        xhat, rstd = _layernorm_stats(gv)
        nv = nv_ref[...]
        vn = (xhat * nv).astype(BF16)
        dya = dc_ref[:, 0:da].astype(F32)
        dz = (dya * gu).astype(BF16)
        mask = _tril(chunk)
        for hh in range(heads):
            wm = jnp.where(mask, ws_ref[hh], 0.0).astype(BF16)
            cols = slice(hh * hd, (hh + 1) * hd)
            dws = jnp.zeros((chunk, chunk), F32)
            for nn in range(tm // chunk):
                rows = slice(nn * chunk, (nn + 1) * chunk)
                z = _dot(wm, vn[rows, cols]) + b_ref[:, cols]
                dp_ref[rows, cols] = (dya[rows, cols] * z * gu_grad[rows, cols]).astype(BF16)
                dz_blk = dz[rows, cols]
                dws = dws + _dot_nt(dz_blk, vn[rows, cols])
                dzs_ref[:, cols] += dz_blk.astype(F32)
                dvn = _dot_tn(wm, dz_blk)
                dnv_ref[:, cols] += jnp.sum(dvn * xhat[rows, cols], axis=0, keepdims=True)
                dvn_sc[rows, cols] = dvn
            dws_ref[hh] += jnp.where(mask, dws, 0.0)
        dxhat = dvn_sc[...] * nv
        dgv = rstd * (dxhat - jnp.mean(dxhat, axis=-1, keepdims=True) - xhat * jnp.mean(dxhat * xhat, axis=-1, keepdims=True))
        dp_ref[:, da : 2 * da] = (dgv * gv_grad).astype(BF16)

        c_bg, c_cg, c_xb = 2 * da, 2 * da + db, 2 * da + 2 * db
        bg = p_ref[:, c_bg : c_bg + db]
        cg = p_ref[:, c_cg : c_cg + db]
        xb = p_ref[:, c_xb : c_xb + db]
        q = cg * xb
        q1 = _shift_down(q, 1, pa_ref, c_cg, c_xb, first)
        q2 = _shift_down(q, 2, pa_ref, c_cg, c_xb, first)
        dyb = dc_ref[:, da : da + db].astype(F32)
        conv = cw_ref[0:1, :] * q2 + cw_ref[1:2, :] * q1 + cw_ref[2:3, :] * q
        dp_ref[:, c_bg : c_bg + db] = (dyb * conv).astype(BF16)
        e = dyb * bg
        dcw_ref[0:1, :] += jnp.sum(e * q2, axis=0, keepdims=True)
        dcw_ref[1:2, :] += jnp.sum(e * q1, axis=0, keepdims=True)
        dcw_ref[2:3, :] += jnp.sum(e * q, axis=0, keepdims=True)
        rows = lax.broadcasted_iota(jnp.int32, e.shape, 0)
        dq = cw_ref[2:3, :] * e
        for kk in (1, 2):
            ek = pltpu.roll(e, tm - kk, 0)
            for r in range(kk):
                below = dcb_ref[r : r + 1, da : da + db].astype(F32) * pb_ref[r : r + 1, c_bg : c_bg + db]
                below = jnp.where(last, 0.0, below)
                ek = jnp.where(rows == tm - kk + r, below, ek)
            dq = dq + cw_ref[2 - kk : 3 - kk, :] * ek
        dp_ref[:, c_cg : c_cg + db] = (dq * xb).astype(BF16)
        dp_ref[:, c_xb : c_xb + db] = (dq * cg).astype(BF16)

    nh = tm // CONV_HALO
    nhb = tm // dhalo
    const2 = lambda i: (0, 0)
    return _call(
        body, "ab_mix_bwd", (nblk,),
        [
            pl.BlockSpec((tm, n), lambda i: (i, 0)),
            pl.BlockSpec((CONV_HALO, n), lambda i: (jnp.maximum(i * nh - 1, 0), 0)),
            pl.BlockSpec((CONV_HALO, n), lambda i: (jnp.minimum((i + 1) * nh, s // CONV_HALO - 1), 0)),
            pl.BlockSpec((tm, da + db), lambda i: (i, 0)),
            pl.BlockSpec((dhalo, da + db), lambda i: (jnp.minimum((i + 1) * nhb, s // dhalo - 1), 0)),
            pl.BlockSpec((1, da), const2),
            pl.BlockSpec((heads, chunk, chunk), lambda i: (0, 0, 0)),
            pl.BlockSpec((chunk, da), const2),
            pl.BlockSpec((3, db), const2),
        ],
        [
            pl.BlockSpec((tm, n), lambda i: (i, 0)),
            pl.BlockSpec((1, da), const2),
            pl.BlockSpec((heads, chunk, chunk), lambda i: (0, 0, 0)),
            pl.BlockSpec((chunk, da), const2),
            pl.BlockSpec((3, db), const2),
        ],
        [
            jax.ShapeDtypeStruct((s, n), BF16),
            jax.ShapeDtypeStruct((1, da), F32),
            jax.ShapeDtypeStruct((heads, chunk, chunk), F32),
            jax.ShapeDtypeStruct((chunk, da), F32),
            jax.ShapeDtypeStruct((3, db), F32),
        ],
        [proj, proj, proj, dcat, dcat, norm_v, w_s, b_rows, conv_w],
        scratch=[pltpu.VMEM((tm, da), F32)], phases=phases,
    )


def _pool_counts(tm, i, w):
    t = i * tm + lax.broadcasted_iota(jnp.int32, (tm, 1), 0)
    return jnp.minimum(t + 1, w).astype(F32)


def _pool_fwd(x, vec, w_grp, scale, phases=()):
    s, d = x.shape
    groups, gd, _ = w_grp.shape
    tm = _pick(s, (512, 256, 128))

    def body(x_ref, xa_ref, vec_ref, w_ref, sc_ref, xo_ref, p_ref, o_ref):
        i = pl.program_id(0)
        h = _modulate(x_ref[...], vec_ref)
        ha = jnp.where(i == 0, 0.0, _modulate(xa_ref[...], vec_ref))
        ext = jnp.concatenate([ha, h], axis=0)
        for gi, w in enumerate(POOL_WINDOWS):
            cols = slice(gi * gd, (gi + 1) * gd)
            acc = ext[:, cols]
            step = 1
            while step < w:
                acc = acc + pltpu.roll(acc, step, 0)
                step *= 2
            p = (acc[POOL_HALO:, :] / _pool_counts(tm, i, w) - h[:, cols]).astype(BF16)
            p_ref[:, cols] = p
            o_ref[:, cols] = _dot(p, w_ref[gi]).astype(BF16)
        xo_ref[...] = x_ref[...] + vec_ref[3:4, :] * (o_ref[...].astype(F32) * sc_ref[...])

    nh = tm // POOL_HALO
    row = pl.BlockSpec((tm, d), lambda i: (i, 0))
    return _call(
        body, "pool_fwd", (s // tm,),
        [
            row,
            pl.BlockSpec((POOL_HALO, d), lambda i: (jnp.maximum(i * nh - 1, 0), 0)),
            pl.BlockSpec((8, d), lambda i: (0, 0)),
            pl.BlockSpec((groups, gd, gd), lambda i: (0, 0, 0)),
            pl.BlockSpec((1, d), lambda i: (0, 0)),
        ],
        [row, row, row],
        [jax.ShapeDtypeStruct((s, d), F32), jax.ShapeDtypeStruct((s, d), BF16), jax.ShapeDtypeStruct((s, d), BF16)],
        [x, x, vec, w_grp, scale], phases=phases,
    )


def _pool_bwd(dxo, x, vec, p, o, w_grp, scale, phases=()):
    s, d = x.shape
    groups, gd, _ = w_grp.shape
    tm = _pick(s, (512, 256, 128))
    nblk = s // tm

    def body(dxo_ref, dxb_ref, x_ref, vec_ref, p_ref, o_ref, w_ref, sc_ref, dx_ref, dw_ref, dsc_ref, dvec_ref, dw_sc):
        i = pl.program_id(0)

        @pl.when(i == 0)
        def _():
            dw_sc[...] = jnp.zeros_like(dw_sc)
            dsc_ref[...] = jnp.zeros_like(dsc_ref)
            dvec_ref[...] = jnp.zeros_like(dvec_ref)

        gate, sc = vec_ref[3:4, :], sc_ref[...]
        dxo_v = dxo_ref[...]
        ov = o_ref[...].astype(F32)
        dvec_ref[3:4, :] += jnp.sum(dxo_v * (ov * sc), axis=0, keepdims=True)
        dy = gate * dxo_v
        dsc_ref[...] += jnp.sum(dy * ov, axis=0, keepdims=True)
        dout = (dy * sc).astype(BF16)
        dout_b = jnp.where(i == nblk - 1, 0.0, gate * dxb_ref[...] * sc).astype(BF16)
        for gi, w in enumerate(POOL_WINDOWS):
            cols = slice(gi * gd, (gi + 1) * gd)
            dw_sc[gi] += _dot_tn(p_ref[:, cols], dout[:, cols])
            wb = w_ref[gi]
            dp = _dot_nt(dout[:, cols], wb)
            dp_b = _dot_nt(dout_b[:, cols], wb)
            e = dp / _pool_counts(tm, i, w)
            t_below = (i + 1) * tm + lax.broadcasted_iota(jnp.int32, (POOL_HALO, 1), 0)
            e_b = dp_b / jnp.minimum(t_below + 1, w).astype(F32)
            acc = jnp.concatenate([e, e_b], axis=0)
            step = 1
            while step < w:
                acc = acc + pltpu.roll(acc, tm + POOL_HALO - step, 0)
                step *= 2
            dx_ref[:, cols] = acc[:tm, :] - dp
        dx, _ = _modulate_bwd(x_ref[...], dx_ref[...], vec_ref, dvec_ref)
        dx_ref[...] = dxo_v + dx

        @pl.when(i == nblk - 1)
        def _():
            dw_ref[...] = dw_sc[...].astype(BF16)

    nh = tm // POOL_HALO
    row = pl.BlockSpec((tm, d), lambda i: (i, 0))
    vecs = pl.BlockSpec((8, d), lambda i: (0, 0))
    wspec = pl.BlockSpec((groups, gd, gd), lambda i: (0, 0, 0))
    return _call(
        body, "pool_bwd", (nblk,),
        [
            row,
            pl.BlockSpec((POOL_HALO, d), lambda i: (jnp.minimum((i + 1) * nh, s // POOL_HALO - 1), 0)),
            row, vecs, row, row, wspec,
            pl.BlockSpec((1, d), lambda i: (0, 0)),
        ],
        [row, wspec, pl.BlockSpec((1, d), lambda i: (0, 0)), vecs],
        [
            jax.ShapeDtypeStruct((s, d), F32),
            jax.ShapeDtypeStruct((groups, gd, gd), BF16),
            jax.ShapeDtypeStruct((1, d), F32),
            jax.ShapeDtypeStruct((8, d), F32),
        ],
        [dxo, dxo, x, vec, p, o, w_grp, scale],
        scratch=[pltpu.VMEM((groups, gd, gd), F32)], phases=phases,
    )


def _loss_head(x, gain, target, phases=()):
    s, d = x.shape
    tm = _pick(s, (512, 256, 128))

    def body(x_ref, g_ref, t_ref, dx_ref, aux_ref):
        @pl.when(pl.program_id(0) == 0)
        def _():
            aux_ref[...] = jnp.zeros_like(aux_ref)

        xv = x_ref[...]
        rstd = _rstd(xv)
        r = xv * rstd
        gain_v = g_ref[...]
        err = r * gain_v - t_ref[...]
        aux_ref[1:2, :] += jnp.sum(err * err, axis=0, keepdims=True)
        dout = err * (1.0 / d)
        aux_ref[0:1, :] += jnp.sum(dout * r, axis=0, keepdims=True)
        dr = dout * gain_v
        dx_ref[...] = rstd * (dr - r * jnp.mean(dr * r, axis=-1, keepdims=True))

    row = pl.BlockSpec((tm, d), lambda i: (i, 0))
    return _call(
        body, "loss_head", (s // tm,),
        [row, pl.BlockSpec((1, d), lambda i: (0, 0)), row],
        [row, pl.BlockSpec((8, d), lambda i: (0, 0))],
        [jax.ShapeDtypeStruct((s, d), F32), jax.ShapeDtypeStruct((8, d), F32)], [x, gain, target], phases=phases,
    )


def _small_adam(gathered, gathered_ws, layout, smalls, chip):
    names = list(smalls)
    n = len(names)
    loss_row, _, _, n_feat = layout["loss"]

    def body(*refs):
        chip_ref, g_ref, gws_ref = refs[0], refs[1], refs[2]
        wmv = refs[3 : 3 + 3 * n]
        outs = refs[3 + 3 * n : 3 + 7 * n]
        total = refs[-1]
        total[...] = g_ref[0]
        for kdev in range(1, N_DEV):
            total[...] += g_ref[kdev]
        total_ws = gws_ref[0]
        for kdev in range(1, N_DEV):
            total_ws = total_ws + gws_ref[kdev]
        my_chip = chip_ref[0]
        for a, name in enumerate(names):
            w_ref, m_ref, v_ref = wmv[3 * a : 3 * a + 3]
            if name == "ab_w_s":
                g = total_ws
            else:
                row0, rows, col0, cols = layout[name]
                if col0 is None:
                    g = jnp.zeros((rows, cols), F32)
                    for j in range(N_CHIPS):
                        g = g + jnp.where(my_chip == j, total[row0 : row0 + rows, j * cols : (j + 1) * cols], 0.0)
                else:
                    g = total[row0 : row0 + rows, col0 : col0 + cols]
            dl, mo, vo = _adam(w_ref[...], g, m_ref[...], v_ref[...])
            outs[4 * a][...] = g
            outs[4 * a + 1][...] = dl
            outs[4 * a + 2][...] = mo
            outs[4 * a + 3][...] = vo
        refs[3 + 7 * n][...] = 0.5 * jnp.sum(total[loss_row : loss_row + 1, 0:n_feat], axis=1, keepdims=True) / n_feat

    ins = [gathered, gathered_ws]
    out_shapes = []
    for name in names:
        ins.extend(smalls[name])
        out_shapes.extend([jax.ShapeDtypeStruct(smalls[name][0].shape, F32)] * 4)
    out_shapes.append(jax.ShapeDtypeStruct((1, 1), F32))
    whole = lambda shape: pl.BlockSpec(shape, functools.partial(lambda nd, i, c: (0,) * nd, len(shape)))
    res = pl.pallas_call(
        body, name="small_adam",
        grid_spec=pltpu.PrefetchScalarGridSpec(
            num_scalar_prefetch=1, grid=(1,),
            in_specs=[whole(a.shape) for a in ins], out_specs=[whole(o.shape) for o in out_shapes],
            scratch_shapes=[pltpu.VMEM(gathered.shape[1:], F32)],
        ),
        out_shape=out_shapes,
        compiler_params=pltpu.CompilerParams(dimension_semantics=("arbitrary",), vmem_limit_bytes=VMEM_LIMIT_BYTES),
    )(chip.reshape(1).astype(jnp.int32), *ins)
    return {name: res[4 * a : 4 * a + 4] for a, name in enumerate(names)}, res[4 * n]


def _pad_rows(a, rows=8):
    extra = (-a.shape[0]) % rows
    return jnp.pad(a, ((0, extra), (0, 0))) if extra else a


def _pad_cols(a, cols):
    return jnp.pad(a, ((0, 0), (0, cols - a.shape[1]))) if a.shape[1] < cols else a


def _run(fn, *phases):
    outs, p_outs = fn(list(phases))
    for p, po in zip(phases, p_outs):
        p.then(po)
    return outs


def kernel(x, c, norm_g, w_mod, b_mod, w_ffn_in, w_ffn_out, ab_w_in, ab_norm_v, ab_w_s, ab_b_s, ab_conv_w, ab_w_out, pool_w_grp, pool_scale, final_g, loss_target, m_norm_g, m_w_mod, m_b_mod, m_w_ffn_in, m_w_ffn_out, m_ab_w_in, m_ab_norm_v, m_ab_w_s, m_ab_b_s, m_ab_conv_w, m_ab_w_out, m_pool_w_grp, m_pool_scale, m_final_g, v_norm_g, v_w_mod, v_b_mod, v_w_ffn_in, v_w_ffn_out, v_ab_w_in, v_ab_norm_v, v_ab_w_s, v_ab_b_s, v_ab_conv_w, v_ab_w_out, v_pool_w_grp, v_pool_scale, v_final_g):
    ix, iy, ic = _place()
    chip = 2 * ix + iy
    me = 4 * ix + 2 * iy + ic
    where = jnp.stack([chip, ic]).astype(jnp.int32)
    s, d = x.shape[1], x.shape[2]
    x0 = x.reshape(s, d)
    target = loss_target.reshape(s, d)
    n_layers = norm_g.shape[0]
    dq = d // N_CHIPS
    heads, chunk = ab_w_s.shape[1], ab_w_s.shape[2]
    da = ab_norm_v.shape[1]
    db = ab_conv_w.shape[2] * N_CHIPS
    f_hidden = w_ffn_out.shape[2] * N_CHIPS
    assert n_layers == 2 and da % heads == 0

    cw_pad = _pad_cols(ab_conv_w.reshape(3, db // N_CHIPS), dq)
    packed = jnp.concatenate(
        [_pad_rows(c.reshape(N_CHIPS, dq)), _pad_rows(norm_g.reshape(-1, dq)), _pad_rows(pool_scale.reshape(1, dq)), _pad_rows(cw_pad)],
        axis=0,
    )
    ncol = w_mod.shape[2]
    b_cols = lax.dynamic_slice(b_mod, (0, chip * ncol), (n_layers, ncol)).reshape(n_layers, 1, ncol)
    small = {}

    def small_gather(key, arrs):
        def then(outs):
            small[key] = outs

        return _phase_small_gather(arrs, then)

    stacks = {
        "w_ffn_in": tuple(a.reshape((-1,) + a.shape[2:]) for a in (w_ffn_in, m_w_ffn_in, v_w_ffn_in)),
        "w_ffn_out": tuple(a.reshape((-1,) + a.shape[2:]) for a in (w_ffn_out, m_w_ffn_out, v_w_ffn_out)),
        "ab_w_in": (ab_w_in, m_ab_w_in, v_ab_w_in),
        "ab_w_out": (ab_w_out, m_ab_w_out, v_ab_w_out),
        "pool_w_grp": (pool_w_grp[0], m_pool_w_grp[0], v_pool_w_grp[0]),
    }
    big_in = _Big((1, d, 2 * f_hidden), 2, 1)
    big_out = _Big((1, f_hidden, d), 1, 2)
    units = {}
    for l in range(n_layers):
        for k in range(2):
            units[f"in{l}{k}"] = (big_in, "w_ffn_in", 2 * l + k)
            units[f"out{l}{k}"] = (big_out, "w_ffn_out", 2 * l + k)
    units["abin"] = (_Big((1, d, ab_w_in.shape[2] * N_CHIPS), 2, 1), "ab_w_in", 0)
    units["about"] = (_Big((1, ab_w_out.shape[1] * N_CHIPS, d), 1, 2), "ab_w_out", 0)
    units["pool"] = (_Big((pool_w_grp.shape[1], pool_w_grp.shape[2] * N_CHIPS, pool_w_grp.shape[3]), 1, 0), "pool_w_grp", 0)
    big = {u: g for u, (g, _, _) in units.items()}

    weight = {}
    complete = set()

    def cast(u):
        g, st, b0 = units[u]

        def launch(phases):
            (weight[u],), p_outs = _cast_into_full(stacks[st][0], b0, g, where, "cast_" + u, phases)
            return None, p_outs

        return launch

    def gather_relay(us, second):
        def then(outs):
            for u, o in zip(us, outs):
                weight[u] = o

        return _phase_gather_relay([weight[u] for u in us], [big[u] for u in us], second, then)

    def gather_sibling(*us):
        def then(outs):
            for u, o in zip(us, outs):
                weight[u] = o
                complete.add(u)

        return _phase_gather_sibling([weight[u] for u in us], [big[u] for u in us], then)

    def w_of(u):
        assert u in complete, u
        return weight[u]

    _run(cast("in00"), small_gather("inputs", [packed]))
    small_all = small["inputs"][0]
    by_chip = small_all[0::2]
    c_all = small_all[:, 0:N_CHIPS, :].reshape(N_DEV, d)
    norm_full = by_chip[:, 8 : 8 + 3 * n_layers, :].transpose(1, 0, 2).reshape(3 * n_layers, d)
    pool_scale_full = by_chip[:, 16:17, :].transpose(1, 0, 2).reshape(1, d)
    conv_full = by_chip[:, 24:27, : db // N_CHIPS].transpose(1, 0, 2).reshape(3, db)
    pieces = [("in00", "out00"), ("abin", "about"), ("in01", "out01"), ("in10", "out10", "pool"), ("in11", "out11")]
    in_flight = {}

    def start_gather(p):
        in_flight[p, 0] = _split_start(gather_relay(pieces[p], False), f"gather_{p}_start")

    def relay_gather(p):
        flight = in_flight.pop((p, 0))
        _split_wait(flight, list(started().ins), f"gather_{p}_arrived")
        in_flight[p, 1] = _split_start(gather_relay(pieces[p], True), f"gather_{p}_relay")

    def started():
        return _after(*[flight.token for flight in in_flight.values()])

    def finish_gather(p, after, meanwhile=None):
        flight = in_flight.pop((p, 1))
        _split_wait(flight, list(after) + list(started().ins), f"gather_{p}_wait")
        crossing = _split_start(gather_sibling(*pieces[p]), f"gather_{p}_forward")
        behind = [crossing.token]
        if p + 2 < len(pieces):
            for u in pieces[p + 2]:
                _run(cast(u), _after(crossing.token))
        if p + 1 < len(pieces):
            relay_gather(p + 1)
        if p + 2 < len(pieces):
            start_gather(p + 2)
        behind = behind + list(started().ins)
        if meanwhile is not None:
            behind = behind + meanwhile(_after(crossing.token))
        _split_wait(crossing, behind, f"gather_{p}_forwarded")

    mod_cols = _run(lambda phases: _mod_fwd(c_all, w_mod, b_cols, phases))[0]
    def mod_rows(outs):
        small["mod"] = outs

    _run(cast("out00"), _phase_small_exchange(mod_cols.transpose(1, 0, 2), mod_rows))
    start_gather(0)
    _run(cast("abin"), started())
    _run(cast("about"), started())
    start_gather(1)
    relay_gather(0)
    mod_mine = small["mod"][0][0::2]
    mod = mod_mine.transpose(1, 0, 2).reshape(n_layers, 3, 3, d)
    vecs = {
        (l, sub): jnp.pad(norm_full[3 * l + sub][None], ((0, 7), (0, 0))) + jnp.pad(mod[l, sub], ((1, 4), (0, 0)))
        for l in range(n_layers)
        for sub in range(3)
    }
    b_rows = jnp.broadcast_to(ab_b_s[0].T[:, :, None], (chunk, heads, da // heads)).reshape(chunk, da)

    saved = {}

    def ffn_forward(xs, l, sub, k, *phases):
        saved[l, sub, "x"] = xs
        xs, gg, uu, yb = _run(
            lambda ph: _ffn_fwd(xs, vecs[l, sub], w_of(f"in{l}{k}"), w_of(f"out{l}{k}"), f"ffn_fwd_{l}{k}", ph), *phases
        )
        saved[l, sub, "act"] = (gg, uu, yb)
        return xs

    finish_gather(0, [vecs[0, 0]])
    xs = ffn_forward(x0, 0, 0, 0, started())
    saved[0, 1, "x"] = xs
    finish_gather(1, [xs])
    (proj,) = _run(lambda ph: _proj_mod_fwd(xs, vecs[0, 1], w_of("abin"), ph), started())
    (cat,) = _run(lambda ph: _ab_mix_fwd(proj, ab_norm_v, ab_w_s[0], b_rows, conv_full, ph))
    xs, yb = _run(lambda ph: _proj_res_fwd(cat, w_of("about"), xs, vecs[0, 1], ph))
    saved[0, 1, "act"] = (proj, cat, yb)
    finish_gather(2, [xs])
    xs = ffn_forward(xs, 0, 2, 1, started())
    finish_gather(3, [xs])
    xs = ffn_forward(xs, 1, 0, 0, started())
    saved[1, 1, "x"] = xs
    pooled = []

    def pool_forward(behind):
        pooled.extend(_run(lambda ph: _pool_fwd(xs, vecs[1, 1], w_of("pool"), pool_scale_full, ph), behind))
        return [pooled[0]]

    finish_gather(4, [xs], pool_forward)
    xs, pp, oo = pooled
    saved[1, 1, "act"] = (pp, oo)
    xs = ffn_forward(xs, 1, 2, 1)
    dxs, aux = _run(lambda ph: _loss_head(xs, final_g.reshape(1, d), target, ph))

    grad = {}
    recv = {}
    csum = {}
    parts = {}
    reduced = {}
    done = set()
    dvecs, small_g = {}, {}

    def pair_exchange(*us):
        def then(outs):
            for u, o in zip(us, outs):
                recv[u] = o

        return _phase_pair_exchange([grad[u] for u in us], [big[u] for u in us], then)

    def grad_half(u, a, bs, mine, name, *phases):
        (res,) = _run(lambda ph: _grad_half(a, bs, big[u], where, mine, recv[u] if mine else None, name, ph), *phases)
        return res

    def pair_sum(u, *phases):
        def launch(ph):
            (csum[u],), p_outs = _pair_sum(grad[u], recv[u], big[u], where, "pair_sum_" + u, ph)
            return None, p_outs

        _run(launch, *phases)

    def chip_exchange(*us):
        def then(outs):
            for u, o in zip(us, outs):
                parts[u] = o

        return _phase_chip_exchange([csum[u] for u in us], [big[u] for u in us], then)

    def chip_sum(*us, carried=()):
        for n_u, u in enumerate(us):
            g, st, b0 = units[u]

            def launch(ph):
                (reduced[st],), p_outs = _chip_sum(
                    csum[u], parts[u], g, where, reduced.get(st), stacks[st][0].shape, b0, "chip_sum_" + u, ph
                )
                return None, p_outs

            _run(launch, *(carried if n_u == 0 else ()))

    def pair_broadcast(*us):
        sts = [units[u][1] for u in us]
        assert len(set(sts)) == len(sts)

        def then(outs):
            for u, st, o in zip(us, sts, outs):
                reduced[st] = o
                done.add(u)

        return _phase_pair_broadcast([reduced[st] for st in sts], [big[u] for u in us], [units[u][2] for u in us], then)

    def ffn_backward(dxs, l, sub, k, carried_bwd, carried_send, carried_mine):
        gg, uu, yb = saved[l, sub, "act"]
        w_in, w_out = w_of(f"in{l}{k}"), w_of(f"out{l}{k}")
        uo, ui, tag = f"out{l}{k}", f"in{l}{k}", f"{l}{k}"
        dxs, dg, du, a, h, dy, dvecs[l, sub] = _run(
            lambda ph: _ffn_bwd(dxs, saved[l, sub, "x"], vecs[l, sub], gg, uu, yb, w_in, w_out, "ffn_bwd_" + tag, ph), *carried_bwd()
        )
        grad[uo] = grad_half(uo, a, [dy], False, "dw_out_send_" + tag, *carried_send())
        grad[ui] = grad_half(ui, h, [dg, du], False, "dw_in_send_" + tag, pair_exchange(uo))
        csum[uo] = grad_half(uo, a, [dy], True, "dw_out_" + tag, pair_exchange(ui))
        csum[ui] = grad_half(ui, h, [dg, du], True, "dw_in_" + tag, *carried_mine())
        return dxs

    none = lambda: ()
    dxs = ffn_backward(dxs, 1, 2, 1, none, none, none)
    pp, oo = saved[1, 1, "act"]
    dxs, grad["pool"], small_g["pool_scale"], dvecs[1, 1] = _run(
        lambda ph: _pool_bwd(dxs, saved[1, 1, "x"], vecs[1, 1], pp, oo, w_of("pool"), pool_scale_full, ph)
    )

    def after_11():
        return (chip_exchange("in11", "out11"), pair_exchange("pool"))

    def bcast_11():
        chip_sum("in11", "out11")
        pair_sum("pool")
        return (pair_broadcast("in11", "out11"), chip_exchange("pool"))

    dxs = ffn_backward(dxs, 1, 0, 0, after_11, bcast_11, none)

    def after_10():
        return (chip_exchange("in10", "out10"),)

    def bcast_10():
        chip_sum("in10", "out10", "pool")
        return (pair_broadcast("in10", "out10", "pool"),)

    dxs = ffn_backward(dxs, 0, 2, 1, after_10, bcast_10, none)

    proj, cat, yb = saved[0, 1, "act"]
    out01 = _split_start(chip_exchange("out01"), "reduce_out01_start")
    dy, dcat, dgate = _run(lambda ph: _proj_res_bwd(dxs, yb, vecs[0, 1], w_of("about"), ph), _after(out01.token))
    grad["about"] = grad_half("about", cat, [dy], False, "dw_ab_out_send")
    dproj, small_g["ab_norm_v"], small_g["ab_w_s"], dzs, small_g["ab_conv_w"] = _run(
        lambda ph: _ab_mix_bwd(proj, dcat, ab_norm_v, ab_w_s[0], b_rows, conv_full, ph), pair_exchange("about")
    )
    small_g["ab_b_s"] = dzs.reshape(chunk, heads, da // heads).sum(axis=2).T
    dxs, h, dvecs[0, 1] = _run(
        lambda ph: _proj_mod_bwd(dproj[None], w_of("abin"), saved[0, 1, "x"], vecs[0, 1], dxs, dgate, "ab_in_bwd", ph)
    )
    grad["abin"] = grad_half("abin", h, [dproj], False, "dw_ab_in_send")
    (csum["out01"],) = _split_wait(out01, [grad["abin"]], "reduce_out01_wait")
    chip_sum("out01", carried=(pair_exchange("abin"),))
    csum["about"] = grad_half("about", cat, [dy], True, "dw_ab_out", pair_broadcast("out01"))
    csum["abin"] = grad_half("abin", h, [dproj], True, "dw_ab_in")

    layout = {}
    tail = {}

    def after_01():
        tail["01"] = _split_start(chip_exchange("in01", "abin", "about"), "reduce_01_start")
        return (_after(tail["01"].token),)

    def pack_small_grads():
        dvec_all = jnp.stack([dvecs[l, sub] for l in range(n_layers) for sub in range(3)])
        dgain = dvec_all[:, 0, :]
        dmod = dvec_all[:, 1:4, :].reshape(3 * 3 * n_layers, d)
        rows = {
            "norm_g": (dgain, None, dq), "final_g": (aux[0:1], 0, d), "pool_scale": (small_g["pool_scale"], None, dq),
            "b_mod": (dmod, 0, d), "ab_norm_v": (small_g["ab_norm_v"], 0, da),
            "ab_conv_w": (small_g["ab_conv_w"], None, db // N_CHIPS), "ab_b_s": (small_g["ab_b_s"], 0, chunk),
            "loss": (aux[1:2], 0, d),
        }
        row0 = 0
        for nm, (pc, col0, cols) in rows.items():
            layout[nm] = (row0, pc.shape[0], col0, cols)
            row0 += pc.shape[0]
        packed_rows = -(-row0 // 8) * 8
        return sum(
            jnp.pad(pc, ((layout[nm][0], packed_rows - layout[nm][0] - pc.shape[0]), (0, d - pc.shape[1])))
            for nm, (pc, _, _) in rows.items()
        )

    def bcast_01():
        csum["in01"], csum["abin"], csum["about"] = _split_wait(tail["01"], [dvecs[0, 0]], "reduce_01_wait")
        chip_sum("in01", "abin", "about")
        grads_small = [pack_small_grads(), small_g["ab_w_s"].reshape(heads * chunk, chunk)]
        tail["small"] = _split_start(small_gather("grads", grads_small), "gather_small_grads_start")
        return (pair_broadcast("in01", "abin", "about"), _after(tail["small"].token))

    def reduce_out00():
        tail["out00"] = _split_start(chip_exchange("out00"), "reduce_out00_start")
        return (_after(tail["out00"].token),)

    dxs = ffn_backward(dxs, 0, 0, 0, after_01, bcast_01, reduce_out00)
    grad_x = dxs.reshape(x.shape)

    last = _split_start(chip_exchange("in00"), "reduce_last_start")
    (csum["out00"],) = _split_wait(tail["out00"], [last.token], "reduce_out00_wait")
    chip_sum("out00")
    _flush("broadcast_out00", pair_broadcast("out00"))
    _split_wait(tail["small"], [reduced["w_ffn_out"]], "gather_small_grads_wait")
    g_all, gws_all = small["grads"]

    out = {}

    def adam_stack(st, after=()):
        w3, m3, v3 = stacks[st]
        assert all(u in done for u, (_, ust, _) in units.items() if ust == st), st
        shape = {"w_ffn_in": w_ffn_in.shape, "w_ffn_out": w_ffn_out.shape, "pool_w_grp": pool_w_grp.shape}.get(st, w3.shape)
        out[st] = tuple(a.reshape(shape) for a in _adam_stack(w3, reduced[st], m3, v3, "adam_" + st, after))

    for st in ("w_ffn_out", "ab_w_in", "ab_w_out", "pool_w_grp"):
        adam_stack(st, (last.token,))

    shapes2d = {
        "norm_g": (3 * n_layers, dq), "b_mod": (9 * n_layers, d), "final_g": (1, d), "ab_norm_v": (1, da),
        "pool_scale": (1, dq), "ab_conv_w": (3, db // N_CHIPS), "ab_b_s": (heads, chunk), "ab_w_s": (heads * chunk, chunk),
    }
    small_w = {"norm_g": (norm_g, m_norm_g, v_norm_g), "b_mod": (b_mod, m_b_mod, v_b_mod), "final_g": (final_g, m_final_g, v_final_g),
               "ab_norm_v": (ab_norm_v, m_ab_norm_v, v_ab_norm_v), "pool_scale": (pool_scale, m_pool_scale, v_pool_scale),
               "ab_conv_w": (ab_conv_w, m_ab_conv_w, v_ab_conv_w), "ab_b_s": (ab_b_s, m_ab_b_s, v_ab_b_s), "ab_w_s": (ab_w_s, m_ab_w_s, v_ab_w_s)}
    smalls = {nm: tuple(a.reshape(shapes2d[nm]) for a in wmv) for nm, wmv in small_w.items()}
    small_out, loss = _small_adam(g_all, gws_all, layout, smalls, chip)
    loss = loss.reshape(())
    for nm, res in small_out.items():
        out[nm] = tuple(a.reshape(small_w[nm][0].shape) for a in res)

    mod_row0 = layout["b_mod"][0]
    dmod_all = g_all[:, mod_row0 : mod_row0 + 9 * n_layers, :].reshape(N_DEV, n_layers, 9 * d)
    dmod_cols = lax.dynamic_slice(dmod_all, (0, 0, chip * ncol), (N_DEV, n_layers, ncol)).transpose(1, 0, 2)
    out["w_mod"] = tuple(_mod_bwd_adam(c_all.T, dmod_cols, w_mod, m_w_mod, v_w_mod, (last.token,)))

    (csum["in00"],) = _split_wait(
        last, [out[st][1] for st in ("w_mod", "w_ffn_out", "ab_w_in", "ab_w_out", "pool_w_grp")], "reduce_last_wait"
    )
    chip_sum("in00")
    _flush("broadcast_last", pair_broadcast("in00"))
    adam_stack("w_ffn_in")

    order = ["norm_g", "w_mod", "b_mod", "w_ffn_in", "w_ffn_out", "ab_w_in", "ab_norm_v", "ab_w_s", "ab_b_s", "ab_conv_w", "ab_w_out", "pool_w_grp", "pool_scale", "final_g"]
    return (loss, grad_x, *[out[nm][0] for nm in order], *[out[nm][1] for nm in order], *[out[nm][2] for nm in order], *[out[nm][3] for nm in order])
```

```python
import functools
import math

import jax
import jax.numpy as jnp
from jax import lax
from jax.experimental import pallas as pl
from jax.experimental.pallas import tpu as pltpu

F32 = jnp.float32
BF16 = jnp.bfloat16
MESH = pl.DeviceIdType.MESH

EPS = 1e-6
ADAM_LR = 0.001
ADAM_B1 = 0.9
ADAM_B2 = 0.999
ADAM_EPS = 1e-08
ADAM_WD = 0.01
ADAM_STEP = 10
POOL_WINDOWS = (2, 4, 8, 16)
POOL_HALO = 16
CONV_HALO = 8
N_CHIPS = 4
N_DEV = 8
VMEM_LIMIT_BYTES = 48 * 1024 * 1024
EW_BLOCK_ELEMS = 1024 * 1024
ADAM_BLOCK_ELEMS = 512 * 1024


def _pick(n, prefs):
    for p in prefs:
        if p <= n and n % p == 0:
            return p
    return n


def _row_tile(rows, cols, block_elems=EW_BLOCK_ELEMS):
    best = None
    for d in range(16, rows + 1, 16):
        if rows % d == 0 and d * cols <= block_elems:
            best = d
    return best or rows


def _dot(a, b):
    return jnp.dot(a, b, preferred_element_type=F32)


def _dot_nt(a, b):
    return lax.dot_general(a, b, (((1,), (1,)), ((), ())), preferred_element_type=F32)


def _dot_tn(a, b):
    return lax.dot_general(a, b, (((0,), (0,)), ((), ())), preferred_element_type=F32)


def _sigmoid(x):
    return 0.5 * jnp.tanh(0.5 * x) + 0.5


_GELU_C = math.sqrt(2.0 / math.pi)


def _gelu(x):
    x2 = x * x
    t = jnp.tanh(_GELU_C * (x + 0.044715 * x2 * x))
    val = 0.5 * x * (1.0 + t)
    grad = 0.5 * (1.0 + t) + 0.5 * x * (1.0 - t * t) * (_GELU_C * (1.0 + 3.0 * 0.044715 * x2))
    return val, grad


def _rstd(x):
    return lax.rsqrt(jnp.mean(x * x, axis=-1, keepdims=True) + EPS)


def _modulate(x, vec_ref):
    return (x * _rstd(x)) * vec_ref[0:1, :] * (1.0 + vec_ref[2:3, :]) + vec_ref[1:2, :]


def _modulate_bwd(x, dh, vec_ref, dvec_ref):
    gn, sh, sc = vec_ref[0:1, :], vec_ref[1:2, :], vec_ref[2:3, :]
    rstd = _rstd(x)
    r = x * rstd
    dvec_ref[0:1, :] += jnp.sum(dh * r * (1.0 + sc), axis=0, keepdims=True)
    dvec_ref[1:2, :] += jnp.sum(dh, axis=0, keepdims=True)
    dvec_ref[2:3, :] += jnp.sum(dh * r * gn, axis=0, keepdims=True)
    gm = gn * (1.0 + sc)
    dr = dh * gm
    dx = rstd * (dr - r * jnp.mean(dr * r, axis=-1, keepdims=True))
    return dx, r * gm + sh


def _adam(w, g, m, v):
    m = ADAM_B1 * m + (1.0 - ADAM_B1) * g
    v = ADAM_B2 * v + (1.0 - ADAM_B2) * (g * g)
    m_hat = m / (1.0 - ADAM_B1**ADAM_STEP)
    v_hat = v / (1.0 - ADAM_B2**ADAM_STEP)
    delta = -ADAM_LR * (m_hat / (jnp.sqrt(v_hat) + ADAM_EPS) + ADAM_WD * w)
    return delta, m, v


_ANY = pl.BlockSpec(memory_space=pl.ANY)


class _Phase:
    def __init__(self, ins, out_shapes, aliases, n_sems, start, finish, then):
        self.ins, self.out_shapes, self.aliases, self.n_sems = list(ins), list(out_shapes), dict(aliases), n_sems
        self.start, self.finish, self.then = start, finish, then


def _call(body, name, grid, in_specs, out_specs, out_shape, ins, scratch=(), prefetch=(), phases=(), in_place=None):
    n_pre, n_in, n_out, n_sc = len(prefetch), len(in_specs), len(out_specs), len(scratch)
    ph_in = [len(p.ins) for p in phases]
    ph_out = [len(p.out_shapes) for p in phases]

    def kernel_body(*refs):
        pos = [0]

        def take(k):
            pos[0] += k
            return refs[pos[0] - k : pos[0]]

        pre, ins_ = take(n_pre), take(n_in)
        p_ins = [take(k) for k in ph_in]
        outs_ = take(n_out)
        p_outs = [take(k) for k in ph_out]
        sc = take(n_sc)
        sems = [take(2) for _ in phases]
        if phases:
            ids = [pl.program_id(a) for a in range(len(grid))]
            first = functools.reduce(jnp.logical_and, [i == 0 for i in ids])
            last = functools.reduce(jnp.logical_and, [i == g - 1 for i, g in zip(ids, grid)])

            @pl.when(first)
            def _():
                for p, pi, po, (send, recv) in zip(phases, p_ins, p_outs, sems):
                    p.start(pi, po, send, recv)

        if body is not None:
            body(*pre, *ins_, *outs_, *sc)
        if phases:

            @pl.when(last)
            def _():
                for p, pi, po, (send, recv) in zip(phases, p_ins, p_outs, sems):
                    p.finish(pi, po, send, recv)

    aliases = {n_pre + i: o for i, o in (in_place or {}).items()}
    i0, o0 = n_pre + n_in, n_out
    for p in phases:
        for i, o in p.aliases.items():
            aliases[i0 + i] = o0 + o
        i0 += len(p.ins)
        o0 += len(p.out_shapes)
    all_in = list(in_specs) + [_ANY] * sum(ph_in)
    all_out = list(out_specs) + [_ANY] * sum(ph_out)
    all_scratch = list(scratch)
    for p in phases:
        all_scratch += [pltpu.SemaphoreType.DMA((p.n_sems,)), pltpu.SemaphoreType.DMA((p.n_sems,))]
    shapes = list(out_shape) + [s for p in phases for s in p.out_shapes]
    operands = list(prefetch) + list(ins) + [a for p in phases for a in p.ins]
    sem = ("arbitrary",) * len(grid)
    params = pltpu.CompilerParams(dimension_semantics=sem, vmem_limit_bytes=VMEM_LIMIT_BYTES)
    if n_pre:
        res = pl.pallas_call(
            kernel_body, name=name, out_shape=shapes, input_output_aliases=aliases, compiler_params=params,
            grid_spec=pltpu.PrefetchScalarGridSpec(
                num_scalar_prefetch=n_pre, grid=grid, in_specs=all_in, out_specs=all_out, scratch_shapes=all_scratch
            ),
        )(*operands)
    else:
        res = pl.pallas_call(
            kernel_body, name=name, grid=grid, in_specs=all_in, out_specs=all_out, out_shape=shapes,
            scratch_shapes=all_scratch, input_output_aliases=aliases, compiler_params=params,
        )(*operands)
    res = list(res)
    outs, rest = res[:n_out], res[n_out:]
    p_res = []
    for k in ph_out:
        p_res.append(rest[:k])
        rest = rest[k:]
    return outs, p_res


def _place():
    return lax.axis_index("x"), lax.axis_index("y"), lax.axis_index("c")


def _other_chips():
    x, y, _ = _place()
    return [(1 - x, y), (x, 1 - y), (1 - x, 1 - y)]


def _flip(k):
    x, y, c = _place()
    return (1 - x if k & 4 else x, 1 - y if k & 2 else y, 1 - c if k & 1 else c)


def _remote(src, dst, send, recv, k, to):
    return pltpu.make_async_remote_copy(
        src_ref=src, dst_ref=dst, send_sem=send.at[k], recv_sem=recv.at[k], device_id=to, device_id_type=MESH
    )


def _phase_small_gather(arrs, then):
    n = len(arrs)

    def copies(ins, outs, send, recv):
        x, y, c = _place()
        me = 4 * x + 2 * y + c
        local = [pltpu.make_async_copy(ins[a], outs[a].at[me], send.at[a * N_DEV]) for a in range(n)]
        remote = [_remote(ins[a], outs[a].at[me], send, recv, a * N_DEV + k, _flip(k)) for a in range(n) for k in range(1, N_DEV)]
        return local, remote

    def start(ins, outs, send, recv):
        local, remote = copies(ins, outs, send, recv)
        for cp in local + remote:
            cp.start()

    def finish(ins, outs, send, recv):
        local, remote = copies(ins, outs, send, recv)
        for cp in remote + local:
            cp.wait()

    shapes = [jax.ShapeDtypeStruct((N_DEV,) + a.shape, a.dtype) for a in arrs]
    return _Phase(arrs, shapes, {}, n * N_DEV, start, finish, then)


def _phase_small_exchange(arr, then):
    def copies(ins, outs, send, recv):
        x, y, c = _place()
        me = 4 * x + 2 * y + c
        local = pltpu.make_async_copy(ins[0].at[me], outs[0].at[me], send.at[0])
        remote = []
        for k in range(1, N_DEV):
            px, py, pc = _flip(k)
            remote.append(_remote(ins[0].at[4 * px + 2 * py + pc], outs[0].at[me], send, recv, k, (px, py, pc)))
        return [local] + remote

    def start(ins, outs, send, recv):
        for cp in copies(ins, outs, send, recv):
            cp.start()

    def finish(ins, outs, send, recv):
        for cp in copies(ins, outs, send, recv):
            cp.wait()

    return _Phase([arr], [jax.ShapeDtypeStruct(arr.shape, arr.dtype)], {}, N_DEV, start, finish, then)


def _after(*arrs):
    nothing = lambda *args: None
    return _Phase(arrs, [], {}, 1, nothing, nothing, nothing)


def _flush(name, *phases):
    _, p_outs = _call(None, name, (1,), [], [], [], [], phases=list(phases))
    for p, po in zip(phases, p_outs):
        p.then(po)


class _Big:
    KINDS = {"full": (True, True), "half": (True, False), "shard": (False, True), "block": (False, False)}

    def __init__(self, f3, s3, h3):
        assert s3 != h3
        self.f3, self.s3, self.h3 = tuple(f3), s3, h3
        self.bd = tuple(f3[a] // (N_CHIPS if a == s3 else 1) // (2 if a == h3 else 1) for a in range(3))
        self.tile = (1, _row_tile(self.bd[1], self.bd[2]), self.bd[2])
        self.grid = tuple(self.bd[a] // self.tile[a] for a in range(3))

    def dims(self, kind):
        chips, halves = self.KINDS[kind]
        return tuple(
            self.bd[a] * (N_CHIPS if chips and a == self.s3 else 1) * (2 if halves and a == self.h3 else 1) for a in range(3)
        )

    def view(self, ref, chip=None, half=None, batch0=0, both_halves=True, part=None):
        start = [batch0, 0, 0]
        size = list(ref.shape)
        size[0] = self.bd[0] * (2 if self.h3 == 0 and both_halves else 1)
        if chip is not None:
            start[self.s3] += chip * self.bd[self.s3]
            size[self.s3] = self.bd[self.s3]
        if half is not None:
            start[self.h3] += half * self.bd[self.h3]
            size[self.h3] = self.bd[self.h3]
        if part is not None:
            size[1] //= 2
            start[1] += part * size[1]
        return ref.at[tuple(pl.ds(st, sz) for st, sz in zip(start, size))]

    def spec(self, chip_from=None, half_from=None, lead=(), batch0=0):
        extra = "grid" in (chip_from, half_from)

        def index(*args):
            pref, idx = args[-1], list(args[int(extra) : -1])
            idx[0] += batch0
            if chip_from:
                idx[self.s3] += (pref[0] if chip_from == "pref" else args[0]) * self.grid[self.s3]
            if half_from:
                idx[self.h3] += (pref[1] if half_from == "pref" else args[0]) * self.grid[self.h3]
            return (0,) * len(lead) + tuple(idx)

        return pl.BlockSpec(tuple(lead) + self.tile, index)


def _same(arrs):
    return [jax.ShapeDtypeStruct(a.shape, a.dtype) for a in arrs]


def _phase_gather_relay(arrs, bigs, second, then):
    n = len(arrs)
    per = 4 if second else 2

    def copies(outs, send, recv, arriving):
        x, y, c = _place()
        xn, yn, dg = (1 - x, y), (x, 1 - y), (1 - x, 1 - y)
        if not second:
            plan = [((xn if arriving else (x, y)), 0, xn), ((yn if arriving else (x, y)), 1, yn)]
        elif arriving:
            plan = [(yn, 0, yn), (dg, 0, yn), (xn, 1, xn), (dg, 1, xn)]
        else:
            plan = [((x, y), 0, yn), (xn, 0, yn), ((x, y), 1, xn), (yn, 1, xn)]
        res = []
        for a in range(n):
            for k, (chip, part, to) in enumerate(plan):
                blk = bigs[a].view(outs[a], 2 * chip[0] + chip[1], c, part=part)
                res.append(_remote(blk, blk, send, recv, per * a + k, (*to, c)))
        return res

    def start(ins, outs, send, recv):
        for cp in copies(outs, send, recv, False):
            cp.start()

    def finish(ins, outs, send, recv):
        for cp in copies(outs, send, recv, True):
            cp.wait_recv()
        for cp in copies(outs, send, recv, False):
            cp.wait_send()

    return _Phase(arrs, _same(arrs), {a: a for a in range(n)}, per * n, start, finish, then)


def _phase_gather_sibling(arrs, bigs, then):
    n = len(arrs)

    def copies(outs, send, recv, arriving):
        x, y, c = _place()
        return [
            _remote(blk, blk, send, recv, 3 * a + j, (x, y, 1 - c))
            for j, chip in enumerate(_other_chips())
            for a in range(n)
            for blk in [bigs[a].view(outs[a], 2 * chip[0] + chip[1], 1 - c if arriving else c)]
        ]

    def start(ins, outs, send, recv):
        for cp in copies(outs, send, recv, False):
            cp.start()

    def finish(ins, outs, send, recv):
        for cp in copies(outs, send, recv, True):
            cp.wait_recv()
        for cp in copies(outs, send, recv, False):
            cp.wait_send()

    return _Phase(arrs, _same(arrs), {a: a for a in range(n)}, 3 * n, start, finish, then)


def _phase_pair_exchange(grads, bigs, then):
    n = len(grads)

    def copies(ins, outs, send, recv):
        x, y, c = _place()
        srcs = [ins[a] if ins[a].shape == outs[a].shape else bigs[a].view(ins[a], None, 1 - c) for a in range(n)]
        return [_remote(srcs[a], outs[a], send, recv, a, (x, y, 1 - c)) for a in range(n)]

    def start(ins, outs, send, recv):
        for cp in copies(ins, outs, send, recv):
            cp.start()

    def finish(ins, outs, send, recv):
        for cp in copies(ins, outs, send, recv):
            cp.wait()

    shapes = [jax.ShapeDtypeStruct(b.dims("half"), BF16) for b in bigs]
    return _Phase(grads, shapes, {}, n, start, finish, then)


def _phase_chip_exchange(sums, bigs, then):
    n = len(sums)

    def copies(ins, outs, send, recv):
        _, _, c = _place()
        return [
            _remote(bigs[a].view(ins[a], 2 * chip[0] + chip[1], both_halves=False), outs[a].at[j], send, recv, 3 * a + j, (*chip, c))
            for j, chip in enumerate(_other_chips())
            for a in range(n)
        ]

    def start(ins, outs, send, recv):
        for cp in copies(ins, outs, send, recv):
            cp.start()

    def finish(ins, outs, send, recv):
        for cp in copies(ins, outs, send, recv):
            cp.wait()

    shapes = [jax.ShapeDtypeStruct((N_CHIPS - 1,) + b.dims("block"), BF16) for b in bigs]
    return _Phase(sums, shapes, {}, 3 * n, start, finish, then)


_HBM = pl.BlockSpec(memory_space=pltpu.HBM)
_SEM = pl.BlockSpec(memory_space=pltpu.SEMAPHORE)
_DATAFLOW = pltpu.SideEffectType.DATAFLOW_SIDE_EFFECTING


class _InFlight:
    def __init__(self, phase, send, recv, arrays, token):
        self.phase, self.send, self.recv, self.arrays, self.token = phase, send, recv, arrays, token


def _phase_results(phase, refs):
    n_in = len(phase.ins)
    updated = {o: i for i, o in phase.aliases.items()}
    fresh = [o for o in range(len(phase.out_shapes)) if o not in updated]
    return [refs[updated[o]] if o in updated else refs[n_in + fresh.index(o)] for o in range(len(phase.out_shapes))]


def _split_start(phase, name):
    n_in = len(phase.ins)
    fresh = [s for o, s in enumerate(phase.out_shapes) if o not in phase.aliases.values()]
    arrays = list(phase.ins) + [lax.empty(s.shape, s.dtype) for s in fresh]
    n = len(arrays)

    def body(*refs):
        phase.start(refs[:n_in], _phase_results(phase, refs[:n]), refs[n], refs[n + 1])
        refs[-1][...] = jnp.zeros_like(refs[-1])

    operands = [pltpu.with_memory_space_constraint(a, pltpu.HBM) for a in arrays]
    res = pl.pallas_call(
        body, name=name,
        out_shape=[pltpu.SemaphoreType.DMA((phase.n_sems,)), pltpu.SemaphoreType.DMA((phase.n_sems,))]
        + [pltpu.HBM(a.shape, a.dtype) for a in arrays] + [jax.ShapeDtypeStruct((8, 128), F32)],
        in_specs=[_HBM] * n, out_specs=[_SEM, _SEM] + [_HBM] * n + [pl.BlockSpec(memory_space=pltpu.VMEM)],
        input_output_aliases={i: 2 + i for i in range(n)},
        compiler_params=pltpu.CompilerParams(has_side_effects=_DATAFLOW),
    )(*operands)
    return _InFlight(phase, res[0], res[1], list(res[2 : 2 + n]), res[-1])


def _split_wait(flight, after, name):
    phase, n = flight.phase, len(flight.arrays)
    n_in = len(phase.ins)

    def body(*refs):
        phase.finish(refs[:n_in], _phase_results(phase, refs[:n]), refs[n], refs[n + 1])

    res = pl.pallas_call(
        body, name=name, out_shape=[pltpu.HBM(a.shape, a.dtype) for a in flight.arrays],
        in_specs=[_HBM] * n + [_SEM, _SEM] + [_ANY] * len(after), out_specs=[_HBM] * n,
        input_output_aliases={i: i for i in range(n)},
        compiler_params=pltpu.CompilerParams(has_side_effects=_DATAFLOW),
    )(*flight.arrays, flight.send, flight.recv, *after)
    res = list(res)
    phase.then(_phase_results(phase, res))
    return res[:n_in]


def _phase_pair_broadcast(stacks, bigs, batch0s, then):
    n = len(stacks)

    def start(ins, outs, send, recv):
        x, y, c = _place()
        for a in range(n):
            blk = bigs[a].view(outs[a], None, c, batch0s[a])
            _remote(blk, blk, send, recv, a, (x, y, 1 - c)).start()

    def finish(ins, outs, send, recv):
        x, y, c = _place()
        for a in range(n):
            mine = bigs[a].view(outs[a], None, c, batch0s[a])
            theirs = bigs[a].view(outs[a], None, 1 - c, batch0s[a])
            _remote(mine, mine, send, recv, a, (x, y, 1 - c)).wait_send()
            _remote(theirs, theirs, send, recv, a, (x, y, 1 - c)).wait_recv()

    return _Phase(stacks, _same(stacks), {a: a for a in range(n)}, n, start, finish, then)


def _tile_call(body, name, big, where, extra, ins, in_specs, out_specs, out_shape, phases=()):
    grid = ((extra,) if extra else ()) + big.grid
    return _call(body, name, grid, in_specs, out_specs, out_shape, ins, prefetch=(where,), phases=phases)


def _cast_into_full(w_stack, batch0, big, where, name, phases=()):
    def body(_, w_ref, o_ref):
        o_ref[...] = w_ref[...].astype(BF16)

    return _tile_call(
        body, name, big, where, 2, [w_stack], [big.spec(None, "grid", batch0=batch0)], [big.spec("pref", "grid")],
        [jax.ShapeDtypeStruct(big.dims("full"), BF16)], phases,
    )


def _pair_sum(g_full, recv_half, big, where, name, phases=()):
    def body(_, g_ref, r_ref, o_ref):
        o_ref[...] = (g_ref[...].astype(F32) + r_ref[...].astype(F32)).astype(BF16)

    half = big.spec("grid", None)
    return _tile_call(
        body, name, big, where, N_CHIPS, [g_full, recv_half], [big.spec("grid", "pref"), half], [half],
        [jax.ShapeDtypeStruct(big.dims("half"), BF16)], phases,
    )


def _chip_sum(chip_sum, parts, big, where, stack, stack_shape, batch0, name, phases=()):
    def body(_, own_ref, p_ref, *rest):
        acc = own_ref[...].astype(F32)
        for k in range(N_CHIPS - 1):
            acc = acc + p_ref[k].astype(F32)
        rest[-1][...] = acc

    ins = [chip_sum, parts] + ([stack] if stack is not None else [])
    in_specs = [big.spec("pref", None), big.spec(None, None, lead=(N_CHIPS - 1,))] + ([_ANY] if stack is not None else [])
    return _call(
        body, name, big.grid, in_specs, [big.spec(None, "pref", batch0=batch0)], [jax.ShapeDtypeStruct(stack_shape, F32)], ins,
        prefetch=(where,), phases=phases, in_place={2: 0} if stack is not None else None,
    )


def _adam_stack(w, g, m, v, name, after=()):
    b, r, c = w.shape
    tr = _row_tile(r, c, ADAM_BLOCK_ELEMS)

    def body(w_ref, g_ref, m_ref, v_ref, *rest):
        go_ref, d_ref, mo_ref, vo_ref = rest[-4:]
        gv = g_ref[...]
        d, mo, vo = _adam(w_ref[...], gv, m_ref[...], v_ref[...])
        go_ref[...] = gv
        d_ref[...] = d
        mo_ref[...] = mo
        vo_ref[...] = vo

    spec = pl.BlockSpec((1, tr, c), lambda bb, i: (bb, i, 0))
    outs, _ = _call(
        body, name, (b, r // tr), [spec] * 4 + [_ANY] * len(after), [spec] * 4, [jax.ShapeDtypeStruct(w.shape, F32)] * 4,
        [w, g, m, v, *after],
    )
    return outs


def _mod_fwd(c_all, w_mod, b_cols, phases=()):
    n_layers, d, n = w_mod.shape
    tn = _pick(n, (768, 512, 384, 256, 128))

    def body(c_ref, w_ref, b_ref, o_ref):
        cv = c_ref[...]
        ca = (cv * _sigmoid(cv)).astype(BF16)
        o_ref[0] = _dot(ca, w_ref[0].astype(BF16)) + b_ref[0]

    return _call(
        body, "mod_fwd", (n_layers, n // tn),
        [
            pl.BlockSpec((N_DEV, d), lambda l, j: (0, 0)),
            pl.BlockSpec((1, d, tn), lambda l, j: (l, 0, j)),
            pl.BlockSpec((1, 1, tn), lambda l, j: (l, 0, j)),
        ],
        [pl.BlockSpec((1, N_DEV, tn), lambda l, j: (l, 0, j))],
        [jax.ShapeDtypeStruct((n_layers, N_DEV, n), F32)], [c_all, w_mod, b_cols], phases=phases,
    )


def _mod_bwd_adam(c_all_t, dmod_cols, w, m, v, after=()):
    n_layers, d, n = w.shape
    tn = _pick(n, (384, 256, 128))

    def body(c_ref, dm_ref, w_ref, m_ref, v_ref, *rest):
        g_ref, d_ref, mo_ref, vo_ref = rest[-4:]
        cv = c_ref[...]
        ca = (cv * _sigmoid(cv)).astype(BF16)
        g = _dot(ca, dm_ref[0].astype(BF16))
        g_ref[0] = g
        dl, mo, vo = _adam(w_ref[0], g, m_ref[0], v_ref[0])
        d_ref[0] = dl
        mo_ref[0] = mo
        vo_ref[0] = vo

    wspec = pl.BlockSpec((1, d, tn), lambda l, j: (l, 0, j))
    outs, _ = _call(
        body, "mod_bwd_adam", (n_layers, n // tn),
        [pl.BlockSpec((d, N_DEV), lambda l, j: (0, 0)), pl.BlockSpec((1, N_DEV, tn), lambda l, j: (l, 0, j)), wspec, wspec, wspec]
        + [_ANY] * len(after),
        [wspec] * 4, [jax.ShapeDtypeStruct(w.shape, F32)] * 4, [c_all_t, dmod_cols, w, m, v, *after],
    )
    return outs


def _ffn_fwd(x, vec, w_in, w_out, name, phases=()):
    s, d = x.shape
    f = w_out.shape[1]
    tm = _pick(s, (1024, 512, 256, 128))
    tf = _pick(f, (256, 128))
    nf = f // tf

    def body(x_ref, vec_ref, wg_ref, wu_ref, wo_ref, xo_ref, g_ref, u_ref, y_ref, h_sc, acc_sc):
        j = pl.program_id(1)

        @pl.when(j == 0)
        def _():
            h_sc[...] = _modulate(x_ref[...], vec_ref).astype(BF16)
            acc_sc[...] = jnp.zeros_like(acc_sc)

        h = h_sc[...]
        g = _dot(h, wg_ref[0])
        u = _dot(h, wu_ref[0])
        g_ref[...] = g.astype(BF16)
        u_ref[...] = u.astype(BF16)
        a = (g * _sigmoid(g) * u).astype(BF16)
        acc_sc[...] += _dot(a, wo_ref[0])

        @pl.when(j == nf - 1)
        def _():
            yv = acc_sc[...]
            xo_ref[...] = x_ref[...] + 0.5 * vec_ref[3:4, :] * yv
            y_ref[...] = yv.astype(BF16)

    row = pl.BlockSpec((tm, d), lambda i, j: (i, 0))
    hid = pl.BlockSpec((tm, tf), lambda i, j: (i, j))
    return _call(
        body, name, (s // tm, nf),
        [
            row,
            pl.BlockSpec((8, d), lambda i, j: (0, 0)),
            pl.BlockSpec((1, d, tf), lambda i, j: (0, 0, j)),
            pl.BlockSpec((1, d, tf), lambda i, j: (0, 0, nf + j)),
            pl.BlockSpec((1, tf, d), lambda i, j: (0, j, 0)),
        ],
        [row, hid, hid, row],
        [
            jax.ShapeDtypeStruct((s, d), F32),
            jax.ShapeDtypeStruct((s, f), BF16),
            jax.ShapeDtypeStruct((s, f), BF16),
            jax.ShapeDtypeStruct((s, d), BF16),
        ],
        [x, vec, w_in, w_in, w_out],
        scratch=[pltpu.VMEM((tm, d), BF16), pltpu.VMEM((tm, d), F32)], phases=phases,
    )


def _ffn_bwd(dxo, x, vec, gg, uu, y, w_in, w_out, name, phases=()):
    s, d = x.shape
    f = w_out.shape[1]
    tm = _pick(s, (512, 256, 128))
    tf = _pick(f, (256, 128))
    nf = f // tf

    def body(dxo_ref, x_ref, vec_ref, g_ref, u_ref, y_ref, wg_ref, wu_ref, wo_ref,
             dx_ref, dg_ref, du_ref, a_ref, h_ref, dy_ref, dvec_ref, acc_sc):
        i, j = pl.program_id(0), pl.program_id(1)

        @pl.when((i == 0) & (j == 0))
        def _():
            dvec_ref[...] = jnp.zeros_like(dvec_ref)

        @pl.when(j == 0)
        def _():
            dxo_v = dxo_ref[...]
            dy_ref[...] = (0.5 * vec_ref[3:4, :] * dxo_v).astype(BF16)
            dvec_ref[3:4, :] += 0.5 * jnp.sum(dxo_v * y_ref[...].astype(F32), axis=0, keepdims=True)
            acc_sc[...] = jnp.zeros_like(acc_sc)

        da = _dot_nt(dy_ref[...], wo_ref[0])
        g = g_ref[...].astype(F32)
        u = u_ref[...].astype(F32)
        sig = _sigmoid(g)
        sl = g * sig
        a_ref[...] = (sl * u).astype(BF16)
        dg = (da * u * (sig * (1.0 + g * (1.0 - sig)))).astype(BF16)
        du = (da * sl).astype(BF16)
        dg_ref[...] = dg
        du_ref[...] = du
        acc_sc[...] += _dot_nt(dg, wg_ref[0]) + _dot_nt(du, wu_ref[0])

        @pl.when(j == nf - 1)
        def _():
            dx, h = _modulate_bwd(x_ref[...], acc_sc[...], vec_ref, dvec_ref)
            dx_ref[...] = dxo_ref[...] + dx
            h_ref[...] = h.astype(BF16)

    row = pl.BlockSpec((tm, d), lambda i, j: (i, 0))
    hid = pl.BlockSpec((tm, tf), lambda i, j: (i, j))
    vecs = pl.BlockSpec((8, d), lambda i, j: (0, 0))
    return _call(
        body, name, (s // tm, nf),
        [
            row, row, vecs, hid, hid, row,
            pl.BlockSpec((1, d, tf), lambda i, j: (0, 0, j)),
            pl.BlockSpec((1, d, tf), lambda i, j: (0, 0, nf + j)),
            pl.BlockSpec((1, tf, d), lambda i, j: (0, j, 0)),
        ],
        [row, hid, hid, hid, row, row, vecs],
        [
            jax.ShapeDtypeStruct((s, d), F32),
            jax.ShapeDtypeStruct((s, f), BF16),
            jax.ShapeDtypeStruct((s, f), BF16),
            jax.ShapeDtypeStruct((s, f), BF16),
            jax.ShapeDtypeStruct((s, d), BF16),
            jax.ShapeDtypeStruct((s, d), BF16),
            jax.ShapeDtypeStruct((8, d), F32),
        ],
        [dxo, x, vec, gg, uu, y, w_in, w_in, w_out],
        scratch=[pltpu.VMEM((tm, d), F32)], phases=phases,
    )


def _grad_half(a, bs, big, where, mine, recv, name, phases=()):
    s, k1 = a.shape
    n = bs[0].shape[1]
    groups = len(bs)
    rows_halved = big.h3 == 1
    assert rows_halved or groups == 1
    kk, nn = (k1 // 2, n) if rows_halved else (k1, n // 2)
    tk = _pick(kk, (1408, 1024, 512, 256, 128))
    tn = _pick(nn, (1408, 1024, 640, 512, 256, 128))
    nkb, nnb = kk // tk, nn // tn
    assert (recv is None) == (not mine)

    def half(pref):
        return pref[1] if mine else 1 - pref[1]

    def body(_, a_ref, *rest):
        q = pl.program_id(1)
        for p in range(groups):

            @pl.when(q == p)
            def _(p=p):
                acc = _dot_tn(a_ref[...], rest[p][...])
                if recv is not None:
                    acc = acc + rest[groups][0].astype(F32)
                rest[-1][0] = acc.astype(BF16)

    def b_block(p):
        def index(i, q, j, pref):
            jj = jnp.where(q == p, j, jnp.where(q < p, 0, nnb - 1))
            return (0, jj + (0 if rows_halved else half(pref) * nnb))

        return pl.BlockSpec((s, tn), index)

    out_spec = pl.BlockSpec((1, tk, tn), lambda i, q, j, pref: (0, i, q * nnb + j))
    in_specs = [pl.BlockSpec((s, tk), lambda i, q, j, pref: (0, i + (half(pref) * nkb if rows_halved else 0)))]
    in_specs += [b_block(p) for p in range(groups)]
    ins = [a, *bs]
    if recv is not None:
        in_specs.append(out_spec)
        ins.append(recv)
    return _call(
        body, name, (nkb, groups, nnb), in_specs, [out_spec], [jax.ShapeDtypeStruct(big.dims("half"), BF16)], ins,
        prefetch=(where,), phases=phases,
    )


def _proj_mod_fwd(x, vec, w, phases=()):
    s, d = x.shape
    n = w.shape[2]
    tm = _pick(s, (1024, 512, 256, 128))
    tn = _pick(n, (640, 512, 256, 128))

    def body(x_ref, vec_ref, w_ref, o_ref, h_sc):
        @pl.when(pl.program_id(1) == 0)
        def _():
            h_sc[...] = _modulate(x_ref[...], vec_ref).astype(BF16)

        o_ref[...] = _dot(h_sc[...], w_ref[0])

    return _call(
        body, "ab_in_fwd", (s // tm, n // tn),
        [
            pl.BlockSpec((tm, d), lambda i, j: (i, 0)),
            pl.BlockSpec((8, d), lambda i, j: (0, 0)),
            pl.BlockSpec((1, d, tn), lambda i, j: (0, 0, j)),
        ],
        [pl.BlockSpec((tm, tn), lambda i, j: (i, j))],
        [jax.ShapeDtypeStruct((s, n), F32)], [x, vec, w],
        scratch=[pltpu.VMEM((tm, d), BF16)], phases=phases,
    )


def _proj_res_fwd(a, w, x, vec, phases=()):
    s, kd = a.shape
    d = x.shape[1]
    tm = _pick(s, (1024, 512, 256, 128))

    def body(a_ref, w_ref, x_ref, vec_ref, xo_ref, y_ref):
        yv = _dot(a_ref[...], w_ref[0])
        xo_ref[...] = x_ref[...] + vec_ref[3:4, :] * yv
        y_ref[...] = yv.astype(BF16)

    row = pl.BlockSpec((tm, d), lambda i: (i, 0))
    return _call(
        body, "ab_out_fwd", (s // tm,),
        [pl.BlockSpec((tm, kd), lambda i: (i, 0)), pl.BlockSpec((1, kd, d), lambda i: (0, 0, 0)), row, pl.BlockSpec((8, d), lambda i: (0, 0))],
        [row, row],
        [jax.ShapeDtypeStruct((s, d), F32), jax.ShapeDtypeStruct((s, d), BF16)], [a, w, x, vec], phases=phases,
    )


def _proj_res_bwd(dxo, y, vec, w, phases=()):
    s, d = dxo.shape
    kd = w.shape[1]
    tm = _pick(s, (1024, 512, 256, 128))

    def body(dxo_ref, y_ref, vec_ref, w_ref, dy_ref, da_ref, dgate_ref):
        @pl.when(pl.program_id(0) == 0)
        def _():
            dgate_ref[...] = jnp.zeros_like(dgate_ref)

        dxo_v = dxo_ref[...]
        dy = (vec_ref[3:4, :] * dxo_v).astype(BF16)
        dy_ref[...] = dy
        dgate_ref[3:4, :] += jnp.sum(dxo_v * y_ref[...].astype(F32), axis=0, keepdims=True)
        da_ref[...] = _dot_nt(dy, w_ref[0]).astype(BF16)

    row = pl.BlockSpec((tm, d), lambda i: (i, 0))
    vecs = pl.BlockSpec((8, d), lambda i: (0, 0))
    return _call(
        body, "ab_out_bwd", (s // tm,),
        [row, row, vecs, pl.BlockSpec((1, kd, d), lambda i: (0, 0, 0))],
        [row, pl.BlockSpec((tm, kd), lambda i: (i, 0)), vecs],
        [jax.ShapeDtypeStruct((s, d), BF16), jax.ShapeDtypeStruct((s, kd), BF16), jax.ShapeDtypeStruct((8, d), F32)],
        [dxo, y, vec, w], phases=phases,
    )


def _proj_mod_bwd(dproj, w, x, vec, dxo, dvec_in, name, phases=()):
    parts, s, n_part = dproj.shape
    d = x.shape[1]
    tm = _pick(s, (512, 256, 128))
    tk = _pick(n_part, (1408, 1280, 1024, 512, 256, 128))
    per_part = n_part // tk
    nk = parts * per_part

    def body(dp_ref, w_ref, x_ref, vec_ref, dxo_ref, dvi_ref, dx_ref, h_ref, dvec_ref, acc_sc):
        i, k = pl.program_id(0), pl.program_id(1)

        @pl.when((i == 0) & (k == 0))
        def _():
            dvec_ref[...] = dvi_ref[...]

        @pl.when(k == 0)
        def _():
            acc_sc[...] = jnp.zeros_like(acc_sc)

        acc_sc[...] += _dot_nt(dp_ref[0], w_ref[0])

        @pl.when(k == nk - 1)
        def _():
            dx, h = _modulate_bwd(x_ref[...], acc_sc[...], vec_ref, dvec_ref)
            dx_ref[...] = dxo_ref[...] + dx
            h_ref[...] = h.astype(BF16)

    row = pl.BlockSpec((tm, d), lambda i, k: (i, 0))
    vecs = pl.BlockSpec((8, d), lambda i, k: (0, 0))
    return _call(
        body, name, (s // tm, nk),
        [
            pl.BlockSpec((1, tm, tk), lambda i, k: (k // per_part, i, k % per_part)),
            pl.BlockSpec((1, d, tk), lambda i, k: (0, 0, k)),
            row, vecs, row, vecs,
        ],
        [row, row, vecs],
        [jax.ShapeDtypeStruct((s, d), F32), jax.ShapeDtypeStruct((s, d), BF16), jax.ShapeDtypeStruct((8, d), F32)],
        [dproj, w, x, vec, dxo, dvec_in], scratch=[pltpu.VMEM((tm, d), F32)], phases=phases,
    )


def _tril(n):
    return lax.broadcasted_iota(jnp.int32, (n, n), 0) >= lax.broadcasted_iota(jnp.int32, (n, n), 1)


def _layernorm_stats(gv):
    mu = jnp.mean(gv, axis=-1, keepdims=True)
    cen = gv - mu
    rstd = lax.rsqrt(jnp.mean(cen * cen, axis=-1, keepdims=True) + EPS)
    return cen * rstd, rstd


def _shift_down(q, k, above_ref, c_cg, c_xb, first):
    width = q.shape[1]
    rows = lax.broadcasted_iota(jnp.int32, q.shape, 0)
    out = pltpu.roll(q, k, 0)
    for r in range(k):
        src = CONV_HALO - k + r
        above = above_ref[src : src + 1, c_cg : c_cg + width] * above_ref[src : src + 1, c_xb : c_xb + width]
        above = jnp.where(first, 0.0, above)
        out = jnp.where(rows == r, above, out)
    return out


def _ab_mix_fwd(proj, norm_v, w_s, b_rows, conv_w, phases=()):
    s, n = proj.shape
    heads, chunk, _ = w_s.shape
    da = norm_v.shape[1]
    hd = da // heads
    db = conv_w.shape[1]
    tm = _pick(s, (512, 256, 128))

    def body(p_ref, ph_ref, nv_ref, ws_ref, b_ref, cw_ref, o_ref):
        first = pl.program_id(0) == 0
        gu, _ = _gelu(p_ref[:, 0:da])
        gv, _ = _gelu(p_ref[:, da : 2 * da])
        xhat, _ = _layernorm_stats(gv)
        vn = (xhat * nv_ref[...]).astype(BF16)
        mask = _tril(chunk)
        for hh in range(heads):
            wm = jnp.where(mask, ws_ref[hh], 0.0).astype(BF16)
            cols = slice(hh * hd, (hh + 1) * hd)
            for nn in range(tm // chunk):
                rows = slice(nn * chunk, (nn + 1) * chunk)
                z = _dot(wm, vn[rows, cols]) + b_ref[:, cols]
                o_ref[rows, cols] = (gu[rows, cols] * z).astype(BF16)
        c_cg, c_xb = 2 * da + db, 2 * da + 2 * db
        bg = p_ref[:, 2 * da : 2 * da + db]
        q = p_ref[:, c_cg : c_cg + db] * p_ref[:, c_xb : c_xb + db]
        q1 = _shift_down(q, 1, ph_ref, c_cg, c_xb, first)
        q2 = _shift_down(q, 2, ph_ref, c_cg, c_xb, first)
        conv = cw_ref[0:1, :] * q2 + cw_ref[1:2, :] * q1 + cw_ref[2:3, :] * q
        o_ref[:, da : da + db] = (bg * conv).astype(BF16)

    nh = tm // CONV_HALO
    return _call(
        body, "ab_mix_fwd", (s // tm,),
        [
            pl.BlockSpec((tm, n), lambda i: (i, 0)),
            pl.BlockSpec((CONV_HALO, n), lambda i: (jnp.maximum(i * nh - 1, 0), 0)),
            pl.BlockSpec((1, da), lambda i: (0, 0)),
            pl.BlockSpec((heads, chunk, chunk), lambda i: (0, 0, 0)),
            pl.BlockSpec((chunk, da), lambda i: (0, 0)),
            pl.BlockSpec((3, db), lambda i: (0, 0)),
        ],
        [pl.BlockSpec((tm, da + db), lambda i: (i, 0))],
        [jax.ShapeDtypeStruct((s, da + db), BF16)], [proj, proj, norm_v, w_s, b_rows, conv_w], phases=phases,
    )


def _ab_mix_bwd(proj, dcat, norm_v, w_s, b_rows, conv_w, phases=()):
    s, n = proj.shape
    heads, chunk, _ = w_s.shape
    da = norm_v.shape[1]
    hd = da // heads
    db = conv_w.shape[1]
    tm = _pick(s, (512, 256, 128))
    nblk = s // tm
    dhalo = 2 * CONV_HALO

    def body(p_ref, pa_ref, pb_ref, dc_ref, dcb_ref, nv_ref, ws_ref, b_ref, cw_ref,
             dp_ref, dnv_ref, dws_ref, dzs_ref, dcw_ref, dvn_sc):
        i = pl.program_id(0)
        first, last = i == 0, i == nblk - 1

        @pl.when(first)
        def _():
            dnv_ref[...] = jnp.zeros_like(dnv_ref)
            dws_ref[...] = jnp.zeros_like(dws_ref)
            dzs_ref[...] = jnp.zeros_like(dzs_ref)
            dcw_ref[...] = jnp.zeros_like(dcw_ref)

        uu = p_ref[:, 0:da]
        gu, gu_grad = _gelu(uu)
        gv, gv_grad = _gelu(p_ref[:, da : 2 * da])
        xhat, rstd = _layernorm_stats(gv)
        nv = nv_ref[...]
        vn = (xhat * nv).astype(BF16)
        dya = dc_ref[:, 0:da].astype(F32)
        dz = (dya * gu).astype(BF16)
        mask = _tril(chunk)
        for hh in range(heads):
            wm = jnp.where(mask, ws_ref[hh], 0.0).astype(BF16)
            cols = slice(hh * hd, (hh + 1) * hd)
            dws = jnp.zeros((chunk, chunk), F32)
            for nn in range(tm // chunk):
                rows = slice(nn * chunk, (nn + 1) * chunk)
                z = _dot(wm, vn[rows, cols]) + b_ref[:, cols]
                dp_ref[rows, cols] = (dya[rows, cols] * z * gu_grad[rows, cols]).astype(BF16)
                dz_blk = dz[rows, cols]
                dws = dws + _dot_nt(dz_blk, vn[rows, cols])
                dzs_ref[:, cols] += dz_blk.astype(F32)
                dvn = _dot_tn(wm, dz_blk)
                dnv_ref[:, cols] += jnp.sum(dvn * xhat[rows, cols], axis=0, keepdims=True)
                dvn_sc[rows, cols] = dvn
            dws_ref[hh] += jnp.where(mask, dws, 0.0)
        dxhat = dvn_sc[...] * nv
        dgv = rstd * (dxhat - jnp.mean(dxhat, axis=-1, keepdims=True) - xhat * jnp.mean(dxhat * xhat, axis=-1, keepdims=True))
        dp_ref[:, da : 2 * da] = (dgv * gv_grad).astype(BF16)

        c_bg, c_cg, c_xb = 2 * da, 2 * da + db, 2 * da + 2 * db
        bg = p_ref[:, c_bg : c_bg + db]
        cg = p_ref[:, c_cg : c_cg + db]
        xb = p_ref[:, c_xb : c_xb + db]
        q = cg * xb
        q1 = _shift_down(q, 1, pa_ref, c_cg, c_xb, first)
        q2 = _shift_down(q, 2, pa_ref, c_cg, c_xb, first)
        dyb = dc_ref[:, da : da + db].astype(F32)
        conv = cw_ref[0:1, :] * q2 + cw_ref[1:2, :] * q1 + cw_ref[2:3, :] * q
        dp_ref[:, c_bg : c_bg + db] = (dyb * conv).astype(BF16)
        e = dyb * bg
        dcw_ref[0:1, :] += jnp.sum(e * q2, axis=0, keepdims=True)
        dcw_ref[1:2, :] += jnp.sum(e * q1, axis=0, keepdims=True)
        dcw_ref[2:3, :] += jnp.sum(e * q, axis=0, keepdims=True)
        rows = lax.broadcasted_iota(jnp.int32, e.shape, 0)
        dq = cw_ref[2:3, :] * e
        for kk in (1, 2):
            ek = pltpu.roll(e, tm - kk, 0)
            for r in range(kk):
                below = dcb_ref[r : r + 1, da : da + db].astype(F32) * pb_ref[r : r + 1, c_bg : c_bg + db]
                below = jnp.where(last, 0.0, below)
                ek = jnp.where(rows == tm - kk + r, below, ek)
            dq = dq + cw_ref[2 - kk : 3 - kk, :] * ek
        dp_ref[:, c_cg : c_cg + db] = (dq * xb).astype(BF16)
        dp_ref[:, c_xb : c_xb + db] = (dq * cg).astype(BF16)

    nh = tm // CONV_HALO
    nhb = tm // dhalo
    const2 = lambda i: (0, 0)
    return _call(
        body, "ab_mix_bwd", (nblk,),
        [
            pl.BlockSpec((tm, n), lambda i: (i, 0)),
            pl.BlockSpec((CONV_HALO, n), lambda i: (jnp.maximum(i * nh - 1, 0), 0)),
            pl.BlockSpec((CONV_HALO, n), lambda i: (jnp.minimum((i + 1) * nh, s // CONV_HALO - 1), 0)),
            pl.BlockSpec((tm, da + db), lambda i: (i, 0)),
            pl.BlockSpec((dhalo, da + db), lambda i: (jnp.minimum((i + 1) * nhb, s // dhalo - 1), 0)),
            pl.BlockSpec((1, da), const2),
            pl.BlockSpec((heads, chunk, chunk), lambda i: (0, 0, 0)),
            pl.BlockSpec((chunk, da), const2),
            pl.BlockSpec((3, db), const2),
        ],
        [
            pl.BlockSpec((tm, n), lambda i: (i, 0)),
            pl.BlockSpec((1, da), const2),
            pl.BlockSpec((heads, chunk, chunk), lambda i: (0, 0, 0)),
            pl.BlockSpec((chunk, da), const2),
            pl.BlockSpec((3, db), const2),
        ],
        [
            jax.ShapeDtypeStruct((s, n), BF16),
            jax.ShapeDtypeStruct((1, da), F32),
            jax.ShapeDtypeStruct((heads, chunk, chunk), F32),
            jax.ShapeDtypeStruct((chunk, da), F32),
            jax.ShapeDtypeStruct((3, db), F32),
        ],
        [proj, proj, proj, dcat, dcat, norm_v, w_s, b_rows, conv_w],
        scratch=[pltpu.VMEM((tm, da), F32)], phases=phases,
    )


def _pool_counts(tm, i, w):
    t = i * tm + lax.broadcasted_iota(jnp.int32, (tm, 1), 0)
    return jnp.minimum(t + 1, w).astype(F32)


def _pool_fwd(x, vec, w_grp, scale, phases=()):
    s, d = x.shape
    groups, gd, _ = w_grp.shape
    tm = _pick(s, (512, 256, 128))

    def body(x_ref, xa_ref, vec_ref, w_ref, sc_ref, xo_ref, p_ref, o_ref):
        i = pl.program_id(0)
        h = _modulate(x_ref[...], vec_ref)
        ha = jnp.where(i == 0, 0.0, _modulate(xa_ref[...], vec_ref))
        ext = jnp.concatenate([ha, h], axis=0)
        for gi, w in enumerate(POOL_WINDOWS):
            cols = slice(gi * gd, (gi + 1) * gd)
            acc = ext[:, cols]
            step = 1
            while step < w:
                acc = acc + pltpu.roll(acc, step, 0)
                step *= 2
            p = (acc[POOL_HALO:, :] / _pool_counts(tm, i, w) - h[:, cols]).astype(BF16)
            p_ref[:, cols] = p
            o_ref[:, cols] = _dot(p, w_ref[gi]).astype(BF16)
        xo_ref[...] = x_ref[...] + vec_ref[3:4, :] * (o_ref[...].astype(F32) * sc_ref[...])

    nh = tm // POOL_HALO
    row = pl.BlockSpec((tm, d), lambda i: (i, 0))
    return _call(
        body, "pool_fwd", (s // tm,),
        [
            row,
            pl.BlockSpec((POOL_HALO, d), lambda i: (jnp.maximum(i * nh - 1, 0), 0)),
            pl.BlockSpec((8, d), lambda i: (0, 0)),
            pl.BlockSpec((groups, gd, gd), lambda i: (0, 0, 0)),
            pl.BlockSpec((1, d), lambda i: (0, 0)),
        ],
        [row, row, row],
        [jax.ShapeDtypeStruct((s, d), F32), jax.ShapeDtypeStruct((s, d), BF16), jax.ShapeDtypeStruct((s, d), BF16)],
        [x, x, vec, w_grp, scale], phases=phases,
    )


def _pool_bwd(dxo, x, vec, p, o, w_grp, scale, phases=()):
    s, d = x.shape
    groups, gd, _ = w_grp.shape
    tm = _pick(s, (512, 256, 128))
    nblk = s // tm

    def body(dxo_ref, dxb_ref, x_ref, vec_ref, p_ref, o_ref, w_ref, sc_ref, dx_ref, dw_ref, dsc_ref, dvec_ref, dw_sc):
        i = pl.program_id(0)

        @pl.when(i == 0)
        def _():
            dw_sc[...] = jnp.zeros_like(dw_sc)
            dsc_ref[...] = jnp.zeros_like(dsc_ref)
            dvec_ref[...] = jnp.zeros_like(dvec_ref)

        gate, sc = vec_ref[3:4, :], sc_ref[...]
        dxo_v = dxo_ref[...]
        ov = o_ref[...].astype(F32)
        dvec_ref[3:4, :] += jnp.sum(dxo_v * (ov * sc), axis=0, keepdims=True)
        dy = gate * dxo_v
        dsc_ref[...] += jnp.sum(dy * ov, axis=0, keepdims=True)
        dout = (dy * sc).astype(BF16)
        dout_b = jnp.where(i == nblk - 1, 0.0, gate * dxb_ref[...] * sc).astype(BF16)
        for gi, w in enumerate(POOL_WINDOWS):
            cols = slice(gi * gd, (gi + 1) * gd)
            dw_sc[gi] += _dot_tn(p_ref[:, cols], dout[:, cols])
            wb = w_ref[gi]
            dp = _dot_nt(dout[:, cols], wb)
            dp_b = _dot_nt(dout_b[:, cols], wb)
            e = dp / _pool_counts(tm, i, w)
            t_below = (i + 1) * tm + lax.broadcasted_iota(jnp.int32, (POOL_HALO, 1), 0)
            e_b = dp_b / jnp.minimum(t_below + 1, w).astype(F32)
            acc = jnp.concatenate([e, e_b], axis=0)
            step = 1
            while step < w:
                acc = acc + pltpu.roll(acc, tm + POOL_HALO - step, 0)
                step *= 2
            dx_ref[:, cols] = acc[:tm, :] - dp
        dx, _ = _modulate_bwd(x_ref[...], dx_ref[...], vec_ref, dvec_ref)
        dx_ref[...] = dxo_v + dx

        @pl.when(i == nblk - 1)
        def _():
            dw_ref[...] = dw_sc[...].astype(BF16)

    nh = tm // POOL_HALO
    row = pl.BlockSpec((tm, d), lambda i: (i, 0))
    vecs = pl.BlockSpec((8, d), lambda i: (0, 0))
    wspec = pl.BlockSpec((groups, gd, gd), lambda i: (0, 0, 0))
    return _call(
        body, "pool_bwd", (nblk,),
        [
            row,
            pl.BlockSpec((POOL_HALO, d), lambda i: (jnp.minimum((i + 1) * nh, s // POOL_HALO - 1), 0)),
            row, vecs, row, row, wspec,
            pl.BlockSpec((1, d), lambda i: (0, 0)),
        ],
        [row, wspec, pl.BlockSpec((1, d), lambda i: (0, 0)), vecs],
        [
            jax.ShapeDtypeStruct((s, d), F32),
            jax.ShapeDtypeStruct((groups, gd, gd), BF16),
            jax.ShapeDtypeStruct((1, d), F32),
            jax.ShapeDtypeStruct((8, d), F32),
        ],
        [dxo, dxo, x, vec, p, o, w_grp, scale],
        scratch=[pltpu.VMEM((groups, gd, gd), F32)], phases=phases,
    )


def _loss_head(x, gain, target, phases=()):
    s, d = x.shape
    tm = _pick(s, (512, 256, 128))

    def body(x_ref, g_ref, t_ref, dx_ref, aux_ref):
        @pl.when(pl.program_id(0) == 0)
        def _():
            aux_ref[...] = jnp.zeros_like(aux_ref)

        xv = x_ref[...]
        rstd = _rstd(xv)
        r = xv * rstd
        gain_v = g_ref[...]
        err = r * gain_v - t_ref[...]
        aux_ref[1:2, :] += jnp.sum(err * err, axis=0, keepdims=True)
        dout = err * (1.0 / d)
        aux_ref[0:1, :] += jnp.sum(dout * r, axis=0, keepdims=True)
        dr = dout * gain_v
        dx_ref[...] = rstd * (dr - r * jnp.mean(dr * r, axis=-1, keepdims=True))

    row = pl.BlockSpec((tm, d), lambda i: (i, 0))
    return _call(
        body, "loss_head", (s // tm,),
        [row, pl.BlockSpec((1, d), lambda i: (0, 0)), row],
        [row, pl.BlockSpec((8, d), lambda i: (0, 0))],
        [jax.ShapeDtypeStruct((s, d), F32), jax.ShapeDtypeStruct((8, d), F32)], [x, gain, target], phases=phases,
    )


def _small_adam(gathered, gathered_ws, layout, smalls, chip):
    names = list(smalls)
    n = len(names)
    loss_row, _, _, n_feat = layout["loss"]

    def body(*refs):
        chip_ref, g_ref, gws_ref = refs[0], refs[1], refs[2]
        wmv = refs[3 : 3 + 3 * n]
        outs = refs[3 + 3 * n : 3 + 7 * n]
        total = refs[-1]
        total[...] = g_ref[0]
        for kdev in range(1, N_DEV):
            total[...] += g_ref[kdev]
        total_ws = gws_ref[0]
        for kdev in range(1, N_DEV):
            total_ws = total_ws + gws_ref[kdev]
        my_chip = chip_ref[0]
        for a, name in enumerate(names):
            w_ref, m_ref, v_ref = wmv[3 * a : 3 * a + 3]
            if name == "ab_w_s":
                g = total_ws
            else:
                row0, rows, col0, cols = layout[name]
                if col0 is None:
                    g = jnp.zeros((rows, cols), F32)
                    for j in range(N_CHIPS):
                        g = g + jnp.where(my_chip == j, total[row0 : row0 + rows, j * cols : (j + 1) * cols], 0.0)
                else:
                    g = total[row0 : row0 + rows, col0 : col0 + cols]
            dl, mo, vo = _adam(w_ref[...], g, m_ref[...], v_ref[...])
            outs[4 * a][...] = g
            outs[4 * a + 1][...] = dl
            outs[4 * a + 2][...] = mo
            outs[4 * a + 3][...] = vo
        refs[3 + 7 * n][...] = 0.5 * jnp.sum(total[loss_row : loss_row + 1, 0:n_feat], axis=1, keepdims=True) / n_feat

    ins = [gathered, gathered_ws]
    out_shapes = []
    for name in names:
        ins.extend(smalls[name])
        out_shapes.extend([jax.ShapeDtypeStruct(smalls[name][0].shape, F32)] * 4)
    out_shapes.append(jax.ShapeDtypeStruct((1, 1), F32))
    whole = lambda shape: pl.BlockSpec(shape, functools.partial(lambda nd, i, c: (0,) * nd, len(shape)))
    res = pl.pallas_call(
        body, name="small_adam",
        grid_spec=pltpu.PrefetchScalarGridSpec(
            num_scalar_prefetch=1, grid=(1,),
            in_specs=[whole(a.shape) for a in ins], out_specs=[whole(o.shape) for o in out_shapes],
            scratch_shapes=[pltpu.VMEM(gathered.shape[1:], F32)],
        ),
        out_shape=out_shapes,
        compiler_params=pltpu.CompilerParams(dimension_semantics=("arbitrary",), vmem_limit_bytes=VMEM_LIMIT_BYTES),
    )(chip.reshape(1).astype(jnp.int32), *ins)
    return {name: res[4 * a : 4 * a + 4] for a, name in enumerate(names)}, res[4 * n]


def _pad_rows(a, rows=8):
    extra = (-a.shape[0]) % rows
    return jnp.pad(a, ((0, extra), (0, 0))) if extra else a


def _pad_cols(a, cols):
    return jnp.pad(a, ((0, 0), (0, cols - a.shape[1]))) if a.shape[1] < cols else a


def _run(fn, *phases):
    outs, p_outs = fn(list(phases))
    for p, po in zip(phases, p_outs):
        p.then(po)
    return outs


def kernel(x, c, norm_g, w_mod, b_mod, w_ffn_in, w_ffn_out, ab_w_in, ab_norm_v, ab_w_s, ab_b_s, ab_conv_w, ab_w_out, pool_w_grp, pool_scale, final_g, loss_target, m_norm_g, m_w_mod, m_b_mod, m_w_ffn_in, m_w_ffn_out, m_ab_w_in, m_ab_norm_v, m_ab_w_s, m_ab_b_s, m_ab_conv_w, m_ab_w_out, m_pool_w_grp, m_pool_scale, m_final_g, v_norm_g, v_w_mod, v_b_mod, v_w_ffn_in, v_w_ffn_out, v_ab_w_in, v_ab_norm_v, v_ab_w_s, v_ab_b_s, v_ab_conv_w, v_ab_w_out, v_pool_w_grp, v_pool_scale, v_final_g):
    ix, iy, ic = _place()
    chip = 2 * ix + iy
    me = 4 * ix + 2 * iy + ic
    where = jnp.stack([chip, ic]).astype(jnp.int32)
    s, d = x.shape[1], x.shape[2]
    x0 = x.reshape(s, d)
    target = loss_target.reshape(s, d)
    n_layers = norm_g.shape[0]
    dq = d // N_CHIPS
    heads, chunk = ab_w_s.shape[1], ab_w_s.shape[2]
    da = ab_norm_v.shape[1]
    db = ab_conv_w.shape[2] * N_CHIPS
    f_hidden = w_ffn_out.shape[2] * N_CHIPS
    assert n_layers == 2 and da % heads == 0

    cw_pad = _pad_cols(ab_conv_w.reshape(3, db // N_CHIPS), dq)
    packed = jnp.concatenate(
        [_pad_rows(c.reshape(N_CHIPS, dq)), _pad_rows(norm_g.reshape(-1, dq)), _pad_rows(pool_scale.reshape(1, dq)), _pad_rows(cw_pad)],
        axis=0,
    )
    ncol = w_mod.shape[2]
    b_cols = lax.dynamic_slice(b_mod, (0, chip * ncol), (n_layers, ncol)).reshape(n_layers, 1, ncol)
    small = {}

    def small_gather(key, arrs):
        def then(outs):
            small[key] = outs

        return _phase_small_gather(arrs, then)

    stacks = {
        "w_ffn_in": tuple(a.reshape((-1,) + a.shape[2:]) for a in (w_ffn_in, m_w_ffn_in, v_w_ffn_in)),
        "w_ffn_out": tuple(a.reshape((-1,) + a.shape[2:]) for a in (w_ffn_out, m_w_ffn_out, v_w_ffn_out)),
        "ab_w_in": (ab_w_in, m_ab_w_in, v_ab_w_in),
        "ab_w_out": (ab_w_out, m_ab_w_out, v_ab_w_out),
        "pool_w_grp": (pool_w_grp[0], m_pool_w_grp[0], v_pool_w_grp[0]),
    }
    big_in = _Big((1, d, 2 * f_hidden), 2, 1)
    big_out = _Big((1, f_hidden, d), 1, 2)
    units = {}
    for l in range(n_layers):
        for k in range(2):
            units[f"in{l}{k}"] = (big_in, "w_ffn_in", 2 * l + k)
            units[f"out{l}{k}"] = (big_out, "w_ffn_out", 2 * l + k)
    units["abin"] = (_Big((1, d, ab_w_in.shape[2] * N_CHIPS), 2, 1), "ab_w_in", 0)
    units["about"] = (_Big((1, ab_w_out.shape[1] * N_CHIPS, d), 1, 2), "ab_w_out", 0)
    units["pool"] = (_Big((pool_w_grp.shape[1], pool_w_grp.shape[2] * N_CHIPS, pool_w_grp.shape[3]), 1, 0), "pool_w_grp", 0)
    big = {u: g for u, (g, _, _) in units.items()}

    weight = {}
    complete = set()

    def cast(u):
        g, st, b0 = units[u]

        def launch(phases):
            (weight[u],), p_outs = _cast_into_full(stacks[st][0], b0, g, where, "cast_" + u, phases)
            return None, p_outs

        return launch

    def gather_relay(us, second):
        def then(outs):
            for u, o in zip(us, outs):
                weight[u] = o

        return _phase_gather_relay([weight[u] for u in us], [big[u] for u in us], second, then)

    def gather_sibling(*us):
        def then(outs):
            for u, o in zip(us, outs):
                weight[u] = o
                complete.add(u)

        return _phase_gather_sibling([weight[u] for u in us], [big[u] for u in us], then)

    def w_of(u):
        assert u in complete, u
        return weight[u]

    _run(cast("in00"), small_gather("inputs", [packed]))
    small_all = small["inputs"][0]
    by_chip = small_all[0::2]
    c_all = small_all[:, 0:N_CHIPS, :].reshape(N_DEV, d)
    norm_full = by_chip[:, 8 : 8 + 3 * n_layers, :].transpose(1, 0, 2).reshape(3 * n_layers, d)
    pool_scale_full = by_chip[:, 16:17, :].transpose(1, 0, 2).reshape(1, d)
    conv_full = by_chip[:, 24:27, : db // N_CHIPS].transpose(1, 0, 2).reshape(3, db)
    pieces = [("in00", "out00"), ("abin", "about"), ("in01", "out01"), ("in10", "out10", "pool"), ("in11", "out11")]
    in_flight = {}

    def start_gather(p):
        in_flight[p, 0] = _split_start(gather_relay(pieces[p], False), f"gather_{p}_start")

    def relay_gather(p):
        flight = in_flight.pop((p, 0))
        _split_wait(flight, list(started().ins), f"gather_{p}_arrived")
        in_flight[p, 1] = _split_start(gather_relay(pieces[p], True), f"gather_{p}_relay")

    def started():
        return _after(*[flight.token for flight in in_flight.values()])

    def finish_gather(p, after, meanwhile=None):
        flight = in_flight.pop((p, 1))
        _split_wait(flight, list(after) + list(started().ins), f"gather_{p}_wait")
        crossing = _split_start(gather_sibling(*pieces[p]), f"gather_{p}_forward")
        behind = [crossing.token]
        if p + 2 < len(pieces):
            for u in pieces[p + 2]:
                _run(cast(u), _after(crossing.token))
        if p + 1 < len(pieces):
            relay_gather(p + 1)
        if p + 2 < len(pieces):
            start_gather(p + 2)
        behind = behind + list(started().ins)
        if meanwhile is not None:
            behind = behind + meanwhile(_after(crossing.token))
        _split_wait(crossing, behind, f"gather_{p}_forwarded")

    _run(cast("out00"))
    start_gather(0)
    mod_cols = _run(lambda phases: _mod_fwd(c_all, w_mod, b_cols, phases), started())[0]

    def mod_rows(outs):
        small["mod"] = outs

    _run(cast("abin"), _phase_small_exchange(mod_cols.transpose(1, 0, 2), mod_rows), started())
    _run(cast("about"), started())
    start_gather(1)
    relay_gather(0)
    mod_mine = small["mod"][0][0::2]
    mod = mod_mine.transpose(1, 0, 2).reshape(n_layers, 3, 3, d)
    vecs = {
        (l, sub): jnp.pad(norm_full[3 * l + sub][None], ((0, 7), (0, 0))) + jnp.pad(mod[l, sub], ((1, 4), (0, 0)))
        for l in range(n_layers)
        for sub in range(3)
    }
    b_rows = jnp.broadcast_to(ab_b_s[0].T[:, :, None], (chunk, heads, da // heads)).reshape(chunk, da)

    saved = {}

    def ffn_forward(xs, l, sub, k, *phases):
        saved[l, sub, "x"] = xs
        xs, gg, uu, yb = _run(
            lambda ph: _ffn_fwd(xs, vecs[l, sub], w_of(f"in{l}{k}"), w_of(f"out{l}{k}"), f"ffn_fwd_{l}{k}", ph), *phases
        )
        saved[l, sub, "act"] = (gg, uu, yb)
        return xs

    finish_gather(0, [vecs[0, 0]])
    xs = ffn_forward(x0, 0, 0, 0, started())
    saved[0, 1, "x"] = xs
    finish_gather(1, [xs])
    (proj,) = _run(lambda ph: _proj_mod_fwd(xs, vecs[0, 1], w_of("abin"), ph), started())
    (cat,) = _run(lambda ph: _ab_mix_fwd(proj, ab_norm_v, ab_w_s[0], b_rows, conv_full, ph))
    xs, yb = _run(lambda ph: _proj_res_fwd(cat, w_of("about"), xs, vecs[0, 1], ph))
    saved[0, 1, "act"] = (proj, cat, yb)
    finish_gather(2, [xs])
    xs = ffn_forward(xs, 0, 2, 1, started())
    finish_gather(3, [xs])
    xs = ffn_forward(xs, 1, 0, 0, started())
    saved[1, 1, "x"] = xs
    pooled = []

    def pool_forward(behind):
        pooled.extend(_run(lambda ph: _pool_fwd(xs, vecs[1, 1], w_of("pool"), pool_scale_full, ph), behind))
        return [pooled[0]]

    finish_gather(4, [xs], pool_forward)
    xs, pp, oo = pooled
    saved[1, 1, "act"] = (pp, oo)
    xs = ffn_forward(xs, 1, 2, 1)
    dxs, aux = _run(lambda ph: _loss_head(xs, final_g.reshape(1, d), target, ph))

    grad = {}
    recv = {}
    csum = {}
    parts = {}
    reduced = {}
    done = set()
    dvecs, small_g = {}, {}

    def pair_exchange(*us):
        def then(outs):
            for u, o in zip(us, outs):
                recv[u] = o

        return _phase_pair_exchange([grad[u] for u in us], [big[u] for u in us], then)

    def grad_half(u, a, bs, mine, name, *phases):
        (res,) = _run(lambda ph: _grad_half(a, bs, big[u], where, mine, recv[u] if mine else None, name, ph), *phases)
        return res

    def pair_sum(u, *phases):
        def launch(ph):
            (csum[u],), p_outs = _pair_sum(grad[u], recv[u], big[u], where, "pair_sum_" + u, ph)
            return None, p_outs

        _run(launch, *phases)

    def chip_exchange(*us):
        def then(outs):
            for u, o in zip(us, outs):
                parts[u] = o

        return _phase_chip_exchange([csum[u] for u in us], [big[u] for u in us], then)

    def chip_sum(*us, carried=()):
        for n_u, u in enumerate(us):
            g, st, b0 = units[u]

            def launch(ph):
                (reduced[st],), p_outs = _chip_sum(
                    csum[u], parts[u], g, where, reduced.get(st), stacks[st][0].shape, b0, "chip_sum_" + u, ph
                )
                return None, p_outs

            _run(launch, *(carried if n_u == 0 else ()))

    def pair_broadcast(*us):
        sts = [units[u][1] for u in us]
        assert len(set(sts)) == len(sts)

        def then(outs):
            for u, st, o in zip(us, sts, outs):
                reduced[st] = o
                done.add(u)

        return _phase_pair_broadcast([reduced[st] for st in sts], [big[u] for u in us], [units[u][2] for u in us], then)

    def ffn_backward(dxs, l, sub, k, carried_bwd, carried_send, carried_mine):
        gg, uu, yb = saved[l, sub, "act"]
        w_in, w_out = w_of(f"in{l}{k}"), w_of(f"out{l}{k}")
        uo, ui, tag = f"out{l}{k}", f"in{l}{k}", f"{l}{k}"
        dxs, dg, du, a, h, dy, dvecs[l, sub] = _run(
            lambda ph: _ffn_bwd(dxs, saved[l, sub, "x"], vecs[l, sub], gg, uu, yb, w_in, w_out, "ffn_bwd_" + tag, ph), *carried_bwd()
        )
        grad[uo] = grad_half(uo, a, [dy], False, "dw_out_send_" + tag, *carried_send())
        grad[ui] = grad_half(ui, h, [dg, du], False, "dw_in_send_" + tag, pair_exchange(uo))
        csum[uo] = grad_half(uo, a, [dy], True, "dw_out_" + tag, pair_exchange(ui))
        csum[ui] = grad_half(ui, h, [dg, du], True, "dw_in_" + tag, *carried_mine())
        return dxs

    none = lambda: ()
    dxs = ffn_backward(dxs, 1, 2, 1, none, none, none)
    pp, oo = saved[1, 1, "act"]
    dxs, grad["pool"], small_g["pool_scale"], dvecs[1, 1] = _run(
        lambda ph: _pool_bwd(dxs, saved[1, 1, "x"], vecs[1, 1], pp, oo, w_of("pool"), pool_scale_full, ph)
    )

    def after_11():
        return (chip_exchange("in11", "out11"), pair_exchange("pool"))

    def bcast_11():
        chip_sum("in11", "out11")
        pair_sum("pool")
        return (pair_broadcast("in11", "out11"), chip_exchange("pool"))

    dxs = ffn_backward(dxs, 1, 0, 0, after_11, bcast_11, none)

    def after_10():
        return (chip_exchange("in10", "out10"),)

    def bcast_10():
        chip_sum("in10", "out10", "pool")
        return (pair_broadcast("in10", "out10", "pool"),)

    dxs = ffn_backward(dxs, 0, 2, 1, after_10, bcast_10, none)

    proj, cat, yb = saved[0, 1, "act"]
    out01 = _split_start(chip_exchange("out01"), "reduce_out01_start")
    dy, dcat, dgate = _run(lambda ph: _proj_res_bwd(dxs, yb, vecs[0, 1], w_of("about"), ph), _after(out01.token))
    grad["about"] = grad_half("about", cat, [dy], False, "dw_ab_out_send")
    dproj, small_g["ab_norm_v"], small_g["ab_w_s"], dzs, small_g["ab_conv_w"] = _run(
        lambda ph: _ab_mix_bwd(proj, dcat, ab_norm_v, ab_w_s[0], b_rows, conv_full, ph), pair_exchange("about")
    )
    small_g["ab_b_s"] = dzs.reshape(chunk, heads, da // heads).sum(axis=2).T
    dxs, h, dvecs[0, 1] = _run(
        lambda ph: _proj_mod_bwd(dproj[None], w_of("abin"), saved[0, 1, "x"], vecs[0, 1], dxs, dgate, "ab_in_bwd", ph)
    )
    grad["abin"] = grad_half("abin", h, [dproj], False, "dw_ab_in_send")
    (csum["out01"],) = _split_wait(out01, [grad["abin"]], "reduce_out01_wait")
    chip_sum("out01", carried=(pair_exchange("abin"),))
    csum["about"] = grad_half("about", cat, [dy], True, "dw_ab_out", pair_broadcast("out01"))
    csum["abin"] = grad_half("abin", h, [dproj], True, "dw_ab_in")

    layout = {}
    tail = {}

    def after_01():
        tail["01"] = _split_start(chip_exchange("in01", "abin", "about"), "reduce_01_start")
        return (_after(tail["01"].token),)

    def pack_small_grads():
        dvec_all = jnp.stack([dvecs[l, sub] for l in range(n_layers) for sub in range(3)])
        dgain = dvec_all[:, 0, :]
        dmod = dvec_all[:, 1:4, :].reshape(3 * 3 * n_layers, d)
        rows = {
            "norm_g": (dgain, None, dq), "final_g": (aux[0:1], 0, d), "pool_scale": (small_g["pool_scale"], None, dq),
            "b_mod": (dmod, 0, d), "ab_norm_v": (small_g["ab_norm_v"], 0, da),
            "ab_conv_w": (small_g["ab_conv_w"], None, db // N_CHIPS), "ab_b_s": (small_g["ab_b_s"], 0, chunk),
            "loss": (aux[1:2], 0, d),
        }
        row0 = 0
        for nm, (pc, col0, cols) in rows.items():
            layout[nm] = (row0, pc.shape[0], col0, cols)
            row0 += pc.shape[0]
        packed_rows = -(-row0 // 8) * 8
        return sum(
            jnp.pad(pc, ((layout[nm][0], packed_rows - layout[nm][0] - pc.shape[0]), (0, d - pc.shape[1])))
            for nm, (pc, _, _) in rows.items()
        )

    def bcast_01():
        csum["in01"], csum["abin"], csum["about"] = _split_wait(tail["01"], [dvecs[0, 0]], "reduce_01_wait")
        chip_sum("in01", "abin", "about")
        grads_small = [pack_small_grads(), small_g["ab_w_s"].reshape(heads * chunk, chunk)]
        tail["small"] = _split_start(small_gather("grads", grads_small), "gather_small_grads_start")
        return (pair_broadcast("in01", "abin", "about"), _after(tail["small"].token))

    def reduce_out00():
        tail["out00"] = _split_start(chip_exchange("out00"), "reduce_out00_start")
        return (_after(tail["out00"].token),)

    dxs = ffn_backward(dxs, 0, 0, 0, after_01, bcast_01, reduce_out00)
    grad_x = dxs.reshape(x.shape)

    last = _split_start(chip_exchange("in00"), "reduce_last_start")
    (csum["out00"],) = _split_wait(tail["out00"], [last.token], "reduce_out00_wait")
    chip_sum("out00")
    _flush("broadcast_out00", pair_broadcast("out00"))
    _split_wait(tail["small"], [reduced["w_ffn_out"]], "gather_small_grads_wait")
    g_all, gws_all = small["grads"]

    out = {}

    def adam_stack(st, after=()):
        w3, m3, v3 = stacks[st]
        assert all(u in done for u, (_, ust, _) in units.items() if ust == st), st
        shape = {"w_ffn_in": w_ffn_in.shape, "w_ffn_out": w_ffn_out.shape, "pool_w_grp": pool_w_grp.shape}.get(st, w3.shape)
        out[st] = tuple(a.reshape(shape) for a in _adam_stack(w3, reduced[st], m3, v3, "adam_" + st, after))

    for st in ("w_ffn_out", "ab_w_in", "ab_w_out", "pool_w_grp"):
        adam_stack(st, (last.token,))

    shapes2d = {
        "norm_g": (3 * n_layers, dq), "b_mod": (9 * n_layers, d), "final_g": (1, d), "ab_norm_v": (1, da),
        "pool_scale": (1, dq), "ab_conv_w": (3, db // N_CHIPS), "ab_b_s": (heads, chunk), "ab_w_s": (heads * chunk, chunk),
    }
    small_w = {"norm_g": (norm_g, m_norm_g, v_norm_g), "b_mod": (b_mod, m_b_mod, v_b_mod), "final_g": (final_g, m_final_g, v_final_g),
               "ab_norm_v": (ab_norm_v, m_ab_norm_v, v_ab_norm_v), "pool_scale": (pool_scale, m_pool_scale, v_pool_scale),
               "ab_conv_w": (ab_conv_w, m_ab_conv_w, v_ab_conv_w), "ab_b_s": (ab_b_s, m_ab_b_s, v_ab_b_s), "ab_w_s": (ab_w_s, m_ab_w_s, v_ab_w_s)}
    smalls = {nm: tuple(a.reshape(shapes2d[nm]) for a in wmv) for nm, wmv in small_w.items()}
    small_out, loss = _small_adam(g_all, gws_all, layout, smalls, chip)
    loss = loss.reshape(())
    for nm, res in small_out.items():
        out[nm] = tuple(a.reshape(small_w[nm][0].shape) for a in res)

    mod_row0 = layout["b_mod"][0]
    dmod_all = g_all[:, mod_row0 : mod_row0 + 9 * n_layers, :].reshape(N_DEV, n_layers, 9 * d)
    dmod_cols = lax.dynamic_slice(dmod_all, (0, 0, chip * ncol), (N_DEV, n_layers, ncol)).transpose(1, 0, 2)
    out["w_mod"] = tuple(_mod_bwd_adam(c_all.T, dmod_cols, w_mod, m_w_mod, v_w_mod, (last.token,)))

    (csum["in00"],) = _split_wait(
        last, [out[st][1] for st in ("w_mod", "w_ffn_out", "ab_w_in", "ab_w_out", "pool_w_grp")], "reduce_last_wait"
    )
    chip_sum("in00")
    _flush("broadcast_last", pair_broadcast("in00"))
    adam_stack("w_ffn_in")

    order = ["norm_g", "w_mod", "b_mod", "w_ffn_in", "w_ffn_out", "ab_w_in", "ab_norm_v", "ab_w_s", "ab_b_s", "ab_conv_w", "ab_w_out", "pool_w_grp", "pool_scale", "final_g"]
    return (loss, grad_x, *[out[nm][0] for nm in order], *[out[nm][1] for nm in order], *[out[nm][2] for nm in order], *[out[nm][3] for nm in order])
```

```python
import functools
import math

import jax
import jax.numpy as jnp
from jax import lax
from jax.experimental import pallas as pl
from jax.experimental.pallas import tpu as pltpu

F32 = jnp.float32
BF16 = jnp.bfloat16
MESH = pl.DeviceIdType.MESH

EPS = 1e-6
ADAM_LR = 0.001
ADAM_B1 = 0.9
ADAM_B2 = 0.999
ADAM_EPS = 1e-08
ADAM_WD = 0.01
ADAM_STEP = 10
POOL_WINDOWS = (2, 4, 8, 16)
POOL_HALO = 16
CONV_HALO = 8
N_CHIPS = 4
N_DEV = 8
VMEM_LIMIT_BYTES = 48 * 1024 * 1024
EW_BLOCK_ELEMS = 1024 * 1024
ADAM_BLOCK_ELEMS = 512 * 1024


def _pick(n, prefs):
    for p in prefs:
        if p <= n and n % p == 0:
            return p
    return n


def _row_tile(rows, cols, block_elems=EW_BLOCK_ELEMS):
    best = None
    for d in range(16, rows + 1, 16):
        if rows % d == 0 and d * cols <= block_elems:
            best = d
    return best or rows


def _dot(a, b):
    return jnp.dot(a, b, preferred_element_type=F32)


def _dot_nt(a, b):
    return lax.dot_general(a, b, (((1,), (1,)), ((), ())), preferred_element_type=F32)


def _dot_tn(a, b):
    return lax.dot_general(a, b, (((0,), (0,)), ((), ())), preferred_element_type=F32)


def _sigmoid(x):
    return 0.5 * jnp.tanh(0.5 * x) + 0.5


_GELU_C = math.sqrt(2.0 / math.pi)


def _gelu(x):
    x2 = x * x
    t = jnp.tanh(_GELU_C * (x + 0.044715 * x2 * x))
    val = 0.5 * x * (1.0 + t)
    grad = 0.5 * (1.0 + t) + 0.5 * x * (1.0 - t * t) * (_GELU_C * (1.0 + 3.0 * 0.044715 * x2))
    return val, grad


def _rstd(x):
    return lax.rsqrt(jnp.mean(x * x, axis=-1, keepdims=True) + EPS)


def _modulate(x, vec_ref):
    return (x * _rstd(x)) * vec_ref[0:1, :] * (1.0 + vec_ref[2:3, :]) + vec_ref[1:2, :]


def _modulate_bwd(x, dh, vec_ref, dvec_ref):
    gn, sh, sc = vec_ref[0:1, :], vec_ref[1:2, :], vec_ref[2:3, :]
    rstd = _rstd(x)
    r = x * rstd
    dvec_ref[0:1, :] += jnp.sum(dh * r * (1.0 + sc), axis=0, keepdims=True)
    dvec_ref[1:2, :] += jnp.sum(dh, axis=0, keepdims=True)
    dvec_ref[2:3, :] += jnp.sum(dh * r * gn, axis=0, keepdims=True)
    gm = gn * (1.0 + sc)
    dr = dh * gm
    dx = rstd * (dr - r * jnp.mean(dr * r, axis=-1, keepdims=True))
    return dx, r * gm + sh


def _adam(w, g, m, v):
    m = ADAM_B1 * m + (1.0 - ADAM_B1) * g
    v = ADAM_B2 * v + (1.0 - ADAM_B2) * (g * g)
    m_hat = m / (1.0 - ADAM_B1**ADAM_STEP)
    v_hat = v / (1.0 - ADAM_B2**ADAM_STEP)
    delta = -ADAM_LR * (m_hat / (jnp.sqrt(v_hat) + ADAM_EPS) + ADAM_WD * w)
    return delta, m, v


_ANY = pl.BlockSpec(memory_space=pl.ANY)


class _Phase:
    def __init__(self, ins, out_shapes, aliases, n_sems, start, finish, then):
        self.ins, self.out_shapes, self.aliases, self.n_sems = list(ins), list(out_shapes), dict(aliases), n_sems
        self.start, self.finish, self.then = start, finish, then


def _call(body, name, grid, in_specs, out_specs, out_shape, ins, scratch=(), prefetch=(), phases=(), in_place=None):
    n_pre, n_in, n_out, n_sc = len(prefetch), len(in_specs), len(out_specs), len(scratch)
    ph_in = [len(p.ins) for p in phases]
    ph_out = [len(p.out_shapes) for p in phases]

    def kernel_body(*refs):
        pos = [0]

        def take(k):
            pos[0] += k
            return refs[pos[0] - k : pos[0]]

        pre, ins_ = take(n_pre), take(n_in)
        p_ins = [take(k) for k in ph_in]
        outs_ = take(n_out)
        p_outs = [take(k) for k in ph_out]
        sc = take(n_sc)
        sems = [take(2) for _ in phases]
        if phases:
            ids = [pl.program_id(a) for a in range(len(grid))]
            first = functools.reduce(jnp.logical_and, [i == 0 for i in ids])
            last = functools.reduce(jnp.logical_and, [i == g - 1 for i, g in zip(ids, grid)])

            @pl.when(first)
            def _():
                for p, pi, po, (send, recv) in zip(phases, p_ins, p_outs, sems):
                    p.start(pi, po, send, recv)

        if body is not None:
            body(*pre, *ins_, *outs_, *sc)
        if phases:

            @pl.when(last)
            def _():
                for p, pi, po, (send, recv) in zip(phases, p_ins, p_outs, sems):
                    p.finish(pi, po, send, recv)

    aliases = {n_pre + i: o for i, o in (in_place or {}).items()}
    i0, o0 = n_pre + n_in, n_out
    for p in phases:
        for i, o in p.aliases.items():
            aliases[i0 + i] = o0 + o
        i0 += len(p.ins)
        o0 += len(p.out_shapes)
    all_in = list(in_specs) + [_ANY] * sum(ph_in)
    all_out = list(out_specs) + [_ANY] * sum(ph_out)
    all_scratch = list(scratch)
    for p in phases:
        all_scratch += [pltpu.SemaphoreType.DMA((p.n_sems,)), pltpu.SemaphoreType.DMA((p.n_sems,))]
    shapes = list(out_shape) + [s for p in phases for s in p.out_shapes]
    operands = list(prefetch) + list(ins) + [a for p in phases for a in p.ins]
    sem = ("arbitrary",) * len(grid)
    params = pltpu.CompilerParams(dimension_semantics=sem, vmem_limit_bytes=VMEM_LIMIT_BYTES)
    if n_pre:
        res = pl.pallas_call(
            kernel_body, name=name, out_shape=shapes, input_output_aliases=aliases, compiler_params=params,
            grid_spec=pltpu.PrefetchScalarGridSpec(
                num_scalar_prefetch=n_pre, grid=grid, in_specs=all_in, out_specs=all_out, scratch_shapes=all_scratch
            ),
        )(*operands)
    else:
        res = pl.pallas_call(
            kernel_body, name=name, grid=grid, in_specs=all_in, out_specs=all_out, out_shape=shapes,
            scratch_shapes=all_scratch, input_output_aliases=aliases, compiler_params=params,
        )(*operands)
    res = list(res)
    outs, rest = res[:n_out], res[n_out:]
    p_res = []
    for k in ph_out:
        p_res.append(rest[:k])
        rest = rest[k:]
    return outs, p_res


def _place():
    return lax.axis_index("x"), lax.axis_index("y"), lax.axis_index("c")


def _other_chips():
    x, y, _ = _place()
    return [(1 - x, y), (x, 1 - y), (1 - x, 1 - y)]


def _flip(k):
    x, y, c = _place()
    return (1 - x if k & 4 else x, 1 - y if k & 2 else y, 1 - c if k & 1 else c)


def _remote(src, dst, send, recv, k, to):
    return pltpu.make_async_remote_copy(
        src_ref=src, dst_ref=dst, send_sem=send.at[k], recv_sem=recv.at[k], device_id=to, device_id_type=MESH
    )


def _phase_small_gather(arrs, then):
    n = len(arrs)

    def copies(ins, outs, send, recv):
        x, y, c = _place()
        me = 4 * x + 2 * y + c
        local = [pltpu.make_async_copy(ins[a], outs[a].at[me], send.at[a * N_DEV]) for a in range(n)]
        remote = [_remote(ins[a], outs[a].at[me], send, recv, a * N_DEV + k, _flip(k)) for a in range(n) for k in range(1, N_DEV)]
        return local, remote

    def start(ins, outs, send, recv):
        local, remote = copies(ins, outs, send, recv)
        for cp in local + remote:
            cp.start()

    def finish(ins, outs, send, recv):
        local, remote = copies(ins, outs, send, recv)
        for cp in remote + local:
            cp.wait()

    shapes = [jax.ShapeDtypeStruct((N_DEV,) + a.shape, a.dtype) for a in arrs]
    return _Phase(arrs, shapes, {}, n * N_DEV, start, finish, then)


def _phase_small_exchange(arr, then):
    def copies(ins, outs, send, recv):
        x, y, c = _place()
        me = 4 * x + 2 * y + c
        local = pltpu.make_async_copy(ins[0].at[me], outs[0].at[me], send.at[0])
        remote = []
        for k in range(1, N_DEV):
            px, py, pc = _flip(k)
            remote.append(_remote(ins[0].at[4 * px + 2 * py + pc], outs[0].at[me], send, recv, k, (px, py, pc)))
        return [local] + remote

    def start(ins, outs, send, recv):
        for cp in copies(ins, outs, send, recv):
            cp.start()

    def finish(ins, outs, send, recv):
        for cp in copies(ins, outs, send, recv):
            cp.wait()

    return _Phase([arr], [jax.ShapeDtypeStruct(arr.shape, arr.dtype)], {}, N_DEV, start, finish, then)


def _after(*arrs):
    nothing = lambda *args: None
    return _Phase(arrs, [], {}, 1, nothing, nothing, nothing)


def _flush(name, *phases):
    _, p_outs = _call(None, name, (1,), [], [], [], [], phases=list(phases))
    for p, po in zip(phases, p_outs):
        p.then(po)


class _Big:
    KINDS = {"full": (True, True), "half": (True, False), "shard": (False, True), "block": (False, False)}

    def __init__(self, f3, s3, h3):
        assert s3 != h3
        self.f3, self.s3, self.h3 = tuple(f3), s3, h3
        self.bd = tuple(f3[a] // (N_CHIPS if a == s3 else 1) // (2 if a == h3 else 1) for a in range(3))
        self.tile = (1, _row_tile(self.bd[1], self.bd[2]), self.bd[2])
        self.grid = tuple(self.bd[a] // self.tile[a] for a in range(3))

    def dims(self, kind):
        chips, halves = self.KINDS[kind]
        return tuple(
            self.bd[a] * (N_CHIPS if chips and a == self.s3 else 1) * (2 if halves and a == self.h3 else 1) for a in range(3)
        )

    def view(self, ref, chip=None, half=None, batch0=0, both_halves=True, part=None):
        start = [batch0, 0, 0]
        size = list(ref.shape)
        size[0] = self.bd[0] * (2 if self.h3 == 0 and both_halves else 1)
        if chip is not None:
            start[self.s3] += chip * self.bd[self.s3]
            size[self.s3] = self.bd[self.s3]
        if half is not None:
            start[self.h3] += half * self.bd[self.h3]
            size[self.h3] = self.bd[self.h3]
        if part is not None:
            size[1] //= 2
            start[1] += part * size[1]
        return ref.at[tuple(pl.ds(st, sz) for st, sz in zip(start, size))]

    def spec(self, chip_from=None, half_from=None, lead=(), batch0=0):
        extra = "grid" in (chip_from, half_from)

        def index(*args):
            pref, idx = args[-1], list(args[int(extra) : -1])
            idx[0] += batch0
            if chip_from:
                idx[self.s3] += (pref[0] if chip_from == "pref" else args[0]) * self.grid[self.s3]
            if half_from:
                idx[self.h3] += (pref[1] if half_from == "pref" else args[0]) * self.grid[self.h3]
            return (0,) * len(lead) + tuple(idx)

        return pl.BlockSpec(tuple(lead) + self.tile, index)


def _same(arrs):
    return [jax.ShapeDtypeStruct(a.shape, a.dtype) for a in arrs]


def _phase_gather_relay(arrs, bigs, second, then):
    n = len(arrs)
    per = 2

    def copies(outs, send, recv, arriving):
        x, y, c = _place()
        xn, yn, dg = (1 - x, y), (x, 1 - y), (1 - x, 1 - y)
        if not second:
            plan = [((xn if arriving else (x, y)), None, xn), ((yn if arriving else (x, y)), None, yn)]
        elif arriving:
            plan = [(dg, 0, yn), (dg, 1, xn)]
        else:
            plan = [(xn, 0, yn), (yn, 1, xn)]
        res = []
        for a in range(n):
            for k, (chip, part, to) in enumerate(plan):
                blk = bigs[a].view(outs[a], 2 * chip[0] + chip[1], c, part=part)
                res.append(_remote(blk, blk, send, recv, per * a + k, (*to, c)))
        return res

    def start(ins, outs, send, recv):
        for cp in copies(outs, send, recv, False):
            cp.start()

    def finish(ins, outs, send, recv):
        for cp in copies(outs, send, recv, True):
            cp.wait_recv()
        for cp in copies(outs, send, recv, False):
            cp.wait_send()

    return _Phase(arrs, _same(arrs), {a: a for a in range(n)}, per * n, start, finish, then)


def _phase_gather_sibling(arrs, bigs, then):
    n = len(arrs)

    def copies(outs, send, recv, arriving):
        x, y, c = _place()
        return [
            _remote(blk, blk, send, recv, 3 * a + j, (x, y, 1 - c))
            for j, chip in enumerate(_other_chips())
            for a in range(n)
            for blk in [bigs[a].view(outs[a], 2 * chip[0] + chip[1], 1 - c if arriving else c)]
        ]

    def start(ins, outs, send, recv):
        for cp in copies(outs, send, recv, False):
            cp.start()

    def finish(ins, outs, send, recv):
        for cp in copies(outs, send, recv, True):
            cp.wait_recv()
        for cp in copies(outs, send, recv, False):
            cp.wait_send()

    return _Phase(arrs, _same(arrs), {a: a for a in range(n)}, 3 * n, start, finish, then)


def _phase_pair_exchange(grads, bigs, then):
    n = len(grads)

    def copies(ins, outs, send, recv):
        x, y, c = _place()
        srcs = [ins[a] if ins[a].shape == outs[a].shape else bigs[a].view(ins[a], None, 1 - c) for a in range(n)]
        return [_remote(srcs[a], outs[a], send, recv, a, (x, y, 1 - c)) for a in range(n)]

    def start(ins, outs, send, recv):
        for cp in copies(ins, outs, send, recv):
            cp.start()

    def finish(ins, outs, send, recv):
        for cp in copies(ins, outs, send, recv):
            cp.wait()

    shapes = [jax.ShapeDtypeStruct(b.dims("half"), BF16) for b in bigs]
    return _Phase(grads, shapes, {}, n, start, finish, then)


def _phase_chip_exchange(sums, bigs, then):
    n = len(sums)

    def copies(ins, outs, send, recv):
        _, _, c = _place()
        return [
            _remote(bigs[a].view(ins[a], 2 * chip[0] + chip[1], both_halves=False), outs[a].at[j], send, recv, 3 * a + j, (*chip, c))
            for j, chip in enumerate(_other_chips())
            for a in range(n)
        ]

    def start(ins, outs, send, recv):
        for cp in copies(ins, outs, send, recv):
            cp.start()

    def finish(ins, outs, send, recv):
        for cp in copies(ins, outs, send, recv):
            cp.wait()

    shapes = [jax.ShapeDtypeStruct((N_CHIPS - 1,) + b.dims("block"), BF16) for b in bigs]
    return _Phase(sums, shapes, {}, 3 * n, start, finish, then)


_HBM = pl.BlockSpec(memory_space=pltpu.HBM)
_SEM = pl.BlockSpec(memory_space=pltpu.SEMAPHORE)
_DATAFLOW = pltpu.SideEffectType.DATAFLOW_SIDE_EFFECTING


class _InFlight:
    def __init__(self, phase, send, recv, arrays, token):
        self.phase, self.send, self.recv, self.arrays, self.token = phase, send, recv, arrays, token


def _phase_results(phase, refs):
    n_in = len(phase.ins)
    updated = {o: i for i, o in phase.aliases.items()}
    fresh = [o for o in range(len(phase.out_shapes)) if o not in updated]
    return [refs[updated[o]] if o in updated else refs[n_in + fresh.index(o)] for o in range(len(phase.out_shapes))]


def _split_start(phase, name):
    n_in = len(phase.ins)
    fresh = [s for o, s in enumerate(phase.out_shapes) if o not in phase.aliases.values()]
    arrays = list(phase.ins) + [lax.empty(s.shape, s.dtype) for s in fresh]
    n = len(arrays)

    def body(*refs):
        phase.start(refs[:n_in], _phase_results(phase, refs[:n]), refs[n], refs[n + 1])
        refs[-1][...] = jnp.zeros_like(refs[-1])

    operands = [pltpu.with_memory_space_constraint(a, pltpu.HBM) for a in arrays]
    res = pl.pallas_call(
        body, name=name,
        out_shape=[pltpu.SemaphoreType.DMA((phase.n_sems,)), pltpu.SemaphoreType.DMA((phase.n_sems,))]
        + [pltpu.HBM(a.shape, a.dtype) for a in arrays] + [jax.ShapeDtypeStruct((8, 128), F32)],
        in_specs=[_HBM] * n, out_specs=[_SEM, _SEM] + [_HBM] * n + [pl.BlockSpec(memory_space=pltpu.VMEM)],
        input_output_aliases={i: 2 + i for i in range(n)},
        compiler_params=pltpu.CompilerParams(has_side_effects=_DATAFLOW),
    )(*operands)
    return _InFlight(phase, res[0], res[1], list(res[2 : 2 + n]), res[-1])


def _split_wait(flight, after, name):
    phase, n = flight.phase, len(flight.arrays)
    n_in = len(phase.ins)

    def body(*refs):
        phase.finish(refs[:n_in], _phase_results(phase, refs[:n]), refs[n], refs[n + 1])

    res = pl.pallas_call(
        body, name=name, out_shape=[pltpu.HBM(a.shape, a.dtype) for a in flight.arrays],
        in_specs=[_HBM] * n + [_SEM, _SEM] + [_ANY] * len(after), out_specs=[_HBM] * n,
        input_output_aliases={i: i for i in range(n)},
        compiler_params=pltpu.CompilerParams(has_side_effects=_DATAFLOW),
    )(*flight.arrays, flight.send, flight.recv, *after)
    res = list(res)
    phase.then(_phase_results(phase, res))
    return res[:n_in]


def _phase_pair_broadcast(stacks, bigs, batch0s, then):
    n = len(stacks)

    def start(ins, outs, send, recv):
        x, y, c = _place()
        for a in range(n):
            blk = bigs[a].view(outs[a], None, c, batch0s[a])
            _remote(blk, blk, send, recv, a, (x, y, 1 - c)).start()

    def finish(ins, outs, send, recv):
        x, y, c = _place()
        for a in range(n):
            mine = bigs[a].view(outs[a], None, c, batch0s[a])
            theirs = bigs[a].view(outs[a], None, 1 - c, batch0s[a])
            _remote(mine, mine, send, recv, a, (x, y, 1 - c)).wait_send()
            _remote(theirs, theirs, send, recv, a, (x, y, 1 - c)).wait_recv()

    return _Phase(stacks, _same(stacks), {a: a for a in range(n)}, n, start, finish, then)


def _tile_call(body, name, big, where, extra, ins, in_specs, out_specs, out_shape, phases=()):
    grid = ((extra,) if extra else ()) + big.grid
    return _call(body, name, grid, in_specs, out_specs, out_shape, ins, prefetch=(where,), phases=phases)


def _cast_into_full(w_stack, batch0, big, where, name, phases=()):
    def body(_, w_ref, o_ref):
        o_ref[...] = w_ref[...].astype(BF16)

    return _tile_call(
        body, name, big, where, 2, [w_stack], [big.spec(None, "grid", batch0=batch0)], [big.spec("pref", "grid")],
        [jax.ShapeDtypeStruct(big.dims("full"), BF16)], phases,
    )


def _pair_sum(g_full, recv_half, big, where, name, phases=()):
    def body(_, g_ref, r_ref, o_ref):
        o_ref[...] = (g_ref[...].astype(F32) + r_ref[...].astype(F32)).astype(BF16)

    half = big.spec("grid", None)
    return _tile_call(
        body, name, big, where, N_CHIPS, [g_full, recv_half], [big.spec("grid", "pref"), half], [half],
        [jax.ShapeDtypeStruct(big.dims("half"), BF16)], phases,
    )


def _chip_sum(chip_sum, parts, big, where, stack, stack_shape, batch0, name, phases=()):
    def body(_, own_ref, p_ref, *rest):
        acc = own_ref[...].astype(F32)
        for k in range(N_CHIPS - 1):
            acc = acc + p_ref[k].astype(F32)
        rest[-1][...] = acc

    ins = [chip_sum, parts] + ([stack] if stack is not None else [])
    in_specs = [big.spec("pref", None), big.spec(None, None, lead=(N_CHIPS - 1,))] + ([_ANY] if stack is not None else [])
    return _call(
        body, name, big.grid, in_specs, [big.spec(None, "pref", batch0=batch0)], [jax.ShapeDtypeStruct(stack_shape, F32)], ins,
        prefetch=(where,), phases=phases, in_place={2: 0} if stack is not None else None,
    )


def _adam_stack(w, g, m, v, name, after=()):
    b, r, c = w.shape
    tr = _row_tile(r, c, ADAM_BLOCK_ELEMS)

    def body(w_ref, g_ref, m_ref, v_ref, *rest):
        go_ref, d_ref, mo_ref, vo_ref = rest[-4:]
        gv = g_ref[...]
        d, mo, vo = _adam(w_ref[...], gv, m_ref[...], v_ref[...])
        go_ref[...] = gv
        d_ref[...] = d
        mo_ref[...] = mo
        vo_ref[...] = vo

    spec = pl.BlockSpec((1, tr, c), lambda bb, i: (bb, i, 0))
    outs, _ = _call(
        body, name, (b, r // tr), [spec] * 4 + [_ANY] * len(after), [spec] * 4, [jax.ShapeDtypeStruct(w.shape, F32)] * 4,
        [w, g, m, v, *after],
    )
    return outs


def _mod_fwd(c_all, w_mod, b_cols, phases=()):
    n_layers, d, n = w_mod.shape
    tn = _pick(n, (768, 512, 384, 256, 128))

    def body(c_ref, w_ref, b_ref, o_ref):
        cv = c_ref[...]
        ca = (cv * _sigmoid(cv)).astype(BF16)
        o_ref[0] = _dot(ca, w_ref[0].astype(BF16)) + b_ref[0]

    return _call(
        body, "mod_fwd", (n_layers, n // tn),
        [
            pl.BlockSpec((N_DEV, d), lambda l, j: (0, 0)),
            pl.BlockSpec((1, d, tn), lambda l, j: (l, 0, j)),
            pl.BlockSpec((1, 1, tn), lambda l, j: (l, 0, j)),
        ],
        [pl.BlockSpec((1, N_DEV, tn), lambda l, j: (l, 0, j))],
        [jax.ShapeDtypeStruct((n_layers, N_DEV, n), F32)], [c_all, w_mod, b_cols], phases=phases,
    )


def _mod_bwd_adam(c_all_t, dmod_cols, w, m, v, after=()):
    n_layers, d, n = w.shape
    tn = _pick(n, (384, 256, 128))

    def body(c_ref, dm_ref, w_ref, m_ref, v_ref, *rest):
        g_ref, d_ref, mo_ref, vo_ref = rest[-4:]
        cv = c_ref[...]
        ca = (cv * _sigmoid(cv)).astype(BF16)
        g = _dot(ca, dm_ref[0].astype(BF16))
        g_ref[0] = g
        dl, mo, vo = _adam(w_ref[0], g, m_ref[0], v_ref[0])
        d_ref[0] = dl
        mo_ref[0] = mo
        vo_ref[0] = vo

    wspec = pl.BlockSpec((1, d, tn), lambda l, j: (l, 0, j))
    outs, _ = _call(
        body, "mod_bwd_adam", (n_layers, n // tn),
        [pl.BlockSpec((d, N_DEV), lambda l, j: (0, 0)), pl.BlockSpec((1, N_DEV, tn), lambda l, j: (l, 0, j)), wspec, wspec, wspec]
        + [_ANY] * len(after),
        [wspec] * 4, [jax.ShapeDtypeStruct(w.shape, F32)] * 4, [c_all_t, dmod_cols, w, m, v, *after],
    )
    return outs


def _ffn_fwd(x, vec, w_in, w_out, name, phases=()):
    s, d = x.shape
    f = w_out.shape[1]
    tm = _pick(s, (1024, 512, 256, 128))
    tf = _pick(f, (256, 128))
    nf = f // tf

    def body(x_ref, vec_ref, wg_ref, wu_ref, wo_ref, xo_ref, g_ref, u_ref, y_ref, h_sc, acc_sc):
        j = pl.program_id(1)

        @pl.when(j == 0)
        def _():
            h_sc[...] = _modulate(x_ref[...], vec_ref).astype(BF16)
            acc_sc[...] = jnp.zeros_like(acc_sc)

        h = h_sc[...]
        g = _dot(h, wg_ref[0])
        u = _dot(h, wu_ref[0])
        g_ref[...] = g.astype(BF16)
        u_ref[...] = u.astype(BF16)
        a = (g * _sigmoid(g) * u).astype(BF16)
        acc_sc[...] += _dot(a, wo_ref[0])

        @pl.when(j == nf - 1)
        def _():
            yv = acc_sc[...]
            xo_ref[...] = x_ref[...] + 0.5 * vec_ref[3:4, :] * yv
            y_ref[...] = yv.astype(BF16)

    row = pl.BlockSpec((tm, d), lambda i, j: (i, 0))
    hid = pl.BlockSpec((tm, tf), lambda i, j: (i, j))
    return _call(
        body, name, (s // tm, nf),
        [
            row,
            pl.BlockSpec((8, d), lambda i, j: (0, 0)),
            pl.BlockSpec((1, d, tf), lambda i, j: (0, 0, j)),
            pl.BlockSpec((1, d, tf), lambda i, j: (0, 0, nf + j)),
            pl.BlockSpec((1, tf, d), lambda i, j: (0, j, 0)),
        ],
        [row, hid, hid, row],
        [
            jax.ShapeDtypeStruct((s, d), F32),
            jax.ShapeDtypeStruct((s, f), BF16),
            jax.ShapeDtypeStruct((s, f), BF16),
            jax.ShapeDtypeStruct((s, d), BF16),
        ],
        [x, vec, w_in, w_in, w_out],
        scratch=[pltpu.VMEM((tm, d), BF16), pltpu.VMEM((tm, d), F32)], phases=phases,
    )


def _ffn_bwd(dxo, x, vec, gg, uu, y, w_in, w_out, name, phases=()):
    s, d = x.shape
    f = w_out.shape[1]
    tm = _pick(s, (512, 256, 128))
    tf = _pick(f, (256, 128))
    nf = f // tf

    def body(dxo_ref, x_ref, vec_ref, g_ref, u_ref, y_ref, wg_ref, wu_ref, wo_ref,
             dx_ref, dg_ref, du_ref, a_ref, h_ref, dy_ref, dvec_ref, acc_sc):
        i, j = pl.program_id(0), pl.program_id(1)

        @pl.when((i == 0) & (j == 0))
        def _():
            dvec_ref[...] = jnp.zeros_like(dvec_ref)

        @pl.when(j == 0)
        def _():
            dxo_v = dxo_ref[...]
            dy_ref[...] = (0.5 * vec_ref[3:4, :] * dxo_v).astype(BF16)
            dvec_ref[3:4, :] += 0.5 * jnp.sum(dxo_v * y_ref[...].astype(F32), axis=0, keepdims=True)
            acc_sc[...] = jnp.zeros_like(acc_sc)

        da = _dot_nt(dy_ref[...], wo_ref[0])
        g = g_ref[...].astype(F32)
        u = u_ref[...].astype(F32)
        sig = _sigmoid(g)
        sl = g * sig
        a_ref[...] = (sl * u).astype(BF16)
        dg = (da * u * (sig * (1.0 + g * (1.0 - sig)))).astype(BF16)
        du = (da * sl).astype(BF16)
        dg_ref[...] = dg
        du_ref[...] = du
        acc_sc[...] += _dot_nt(dg, wg_ref[0]) + _dot_nt(du, wu_ref[0])

        @pl.when(j == nf - 1)
        def _():
            dx, h = _modulate_bwd(x_ref[...], acc_sc[...], vec_ref, dvec_ref)
            dx_ref[...] = dxo_ref[...] + dx
            h_ref[...] = h.astype(BF16)

    row = pl.BlockSpec((tm, d), lambda i, j: (i, 0))
    hid = pl.BlockSpec((tm, tf), lambda i, j: (i, j))
    vecs = pl.BlockSpec((8, d), lambda i, j: (0, 0))
    return _call(
        body, name, (s // tm, nf),
        [
            row, row, vecs, hid, hid, row,
            pl.BlockSpec((1, d, tf), lambda i, j: (0, 0, j)),
            pl.BlockSpec((1, d, tf), lambda i, j: (0, 0, nf + j)),
            pl.BlockSpec((1, tf, d), lambda i, j: (0, j, 0)),
        ],
        [row, hid, hid, hid, row, row, vecs],
        [
            jax.ShapeDtypeStruct((s, d), F32),
            jax.ShapeDtypeStruct((s, f), BF16),
            jax.ShapeDtypeStruct((s, f), BF16),
            jax.ShapeDtypeStruct((s, f), BF16),
            jax.ShapeDtypeStruct((s, d), BF16),
            jax.ShapeDtypeStruct((s, d), BF16),
            jax.ShapeDtypeStruct((8, d), F32),
        ],
        [dxo, x, vec, gg, uu, y, w_in, w_in, w_out],
        scratch=[pltpu.VMEM((tm, d), F32)], phases=phases,
    )


def _grad_half(a, bs, big, where, mine, recv, name, phases=()):
    s, k1 = a.shape
    n = bs[0].shape[1]
    groups = len(bs)
    rows_halved = big.h3 == 1
    assert rows_halved or groups == 1
    kk, nn = (k1 // 2, n) if rows_halved else (k1, n // 2)
    tk = _pick(kk, (1408, 1024, 512, 256, 128))
    tn = _pick(nn, (1408, 1024, 640, 512, 256, 128))
    nkb, nnb = kk // tk, nn // tn
    assert (recv is None) == (not mine)

    def half(pref):
        return pref[1] if mine else 1 - pref[1]

    def body(_, a_ref, *rest):
        q = pl.program_id(1)
        for p in range(groups):

            @pl.when(q == p)
            def _(p=p):
                acc = _dot_tn(a_ref[...], rest[p][...])
                if recv is not None:
                    acc = acc + rest[groups][0].astype(F32)
                rest[-1][0] = acc.astype(BF16)

    def b_block(p):
        def index(i, q, j, pref):
            jj = jnp.where(q == p, j, jnp.where(q < p, 0, nnb - 1))
            return (0, jj + (0 if rows_halved else half(pref) * nnb))

        return pl.BlockSpec((s, tn), index)

    out_spec = pl.BlockSpec((1, tk, tn), lambda i, q, j, pref: (0, i, q * nnb + j))
    in_specs = [pl.BlockSpec((s, tk), lambda i, q, j, pref: (0, i + (half(pref) * nkb if rows_halved else 0)))]
    in_specs += [b_block(p) for p in range(groups)]
    ins = [a, *bs]
    if recv is not None:
        in_specs.append(out_spec)
        ins.append(recv)
    return _call(
        body, name, (nkb, groups, nnb), in_specs, [out_spec], [jax.ShapeDtypeStruct(big.dims("half"), BF16)], ins,
        prefetch=(where,), phases=phases,
    )


def _proj_mod_fwd(x, vec, w, phases=()):
    s, d = x.shape
    n = w.shape[2]
    tm = _pick(s, (1024, 512, 256, 128))
    tn = _pick(n, (640, 512, 256, 128))

    def body(x_ref, vec_ref, w_ref, o_ref, h_sc):
        @pl.when(pl.program_id(1) == 0)
        def _():
            h_sc[...] = _modulate(x_ref[...], vec_ref).astype(BF16)

        o_ref[...] = _dot(h_sc[...], w_ref[0])

    return _call(
        body, "ab_in_fwd", (s // tm, n // tn),
        [
            pl.BlockSpec((tm, d), lambda i, j: (i, 0)),
            pl.BlockSpec((8, d), lambda i, j: (0, 0)),
            pl.BlockSpec((1, d, tn), lambda i, j: (0, 0, j)),
        ],
        [pl.BlockSpec((tm, tn), lambda i, j: (i, j))],
        [jax.ShapeDtypeStruct((s, n), F32)], [x, vec, w],
        scratch=[pltpu.VMEM((tm, d), BF16)], phases=phases,
    )


def _proj_res_fwd(a, w, x, vec, phases=()):
    s, kd = a.shape
    d = x.shape[1]
    tm = _pick(s, (1024, 512, 256, 128))

    def body(a_ref, w_ref, x_ref, vec_ref, xo_ref, y_ref):
        yv = _dot(a_ref[...], w_ref[0])
        xo_ref[...] = x_ref[...] + vec_ref[3:4, :] * yv
        y_ref[...] = yv.astype(BF16)

    row = pl.BlockSpec((tm, d), lambda i: (i, 0))
    return _call(
        body, "ab_out_fwd", (s // tm,),
        [pl.BlockSpec((tm, kd), lambda i: (i, 0)), pl.BlockSpec((1, kd, d), lambda i: (0, 0, 0)), row, pl.BlockSpec((8, d), lambda i: (0, 0))],
        [row, row],
        [jax.ShapeDtypeStruct((s, d), F32), jax.ShapeDtypeStruct((s, d), BF16)], [a, w, x, vec], phases=phases,
    )


def _proj_res_bwd(dxo, y, vec, w, phases=()):
    s, d = dxo.shape
    kd = w.shape[1]
    tm = _pick(s, (1024, 512, 256, 128))

    def body(dxo_ref, y_ref, vec_ref, w_ref, dy_ref, da_ref, dgate_ref):
        @pl.when(pl.program_id(0) == 0)
        def _():
            dgate_ref[...] = jnp.zeros_like(dgate_ref)

        dxo_v = dxo_ref[...]
        dy = (vec_ref[3:4, :] * dxo_v).astype(BF16)
        dy_ref[...] = dy
        dgate_ref[3:4, :] += jnp.sum(dxo_v * y_ref[...].astype(F32), axis=0, keepdims=True)
        da_ref[...] = _dot_nt(dy, w_ref[0]).astype(BF16)

    row = pl.BlockSpec((tm, d), lambda i: (i, 0))
    vecs = pl.BlockSpec((8, d), lambda i: (0, 0))
    return _call(
        body, "ab_out_bwd", (s // tm,),
        [row, row, vecs, pl.BlockSpec((1, kd, d), lambda i: (0, 0, 0))],
        [row, pl.BlockSpec((tm, kd), lambda i: (i, 0)), vecs],
        [jax.ShapeDtypeStruct((s, d), BF16), jax.ShapeDtypeStruct((s, kd), BF16), jax.ShapeDtypeStruct((8, d), F32)],
        [dxo, y, vec, w], phases=phases,
    )


def _proj_mod_bwd(dproj, w, x, vec, dxo, dvec_in, name, phases=()):
    parts, s, n_part = dproj.shape
    d = x.shape[1]
    tm = _pick(s, (512, 256, 128))
    tk = _pick(n_part, (1408, 1280, 1024, 512, 256, 128))
    per_part = n_part // tk
    nk = parts * per_part

    def body(dp_ref, w_ref, x_ref, vec_ref, dxo_ref, dvi_ref, dx_ref, h_ref, dvec_ref, acc_sc):
        i, k = pl.program_id(0), pl.program_id(1)

        @pl.when((i == 0) & (k == 0))
        def _():
            dvec_ref[...] = dvi_ref[...]

        @pl.when(k == 0)
        def _():
            acc_sc[...] = jnp.zeros_like(acc_sc)

        acc_sc[...] += _dot_nt(dp_ref[0], w_ref[0])

        @pl.when(k == nk - 1)
        def _():
            dx, h = _modulate_bwd(x_ref[...], acc_sc[...], vec_ref, dvec_ref)
            dx_ref[...] = dxo_ref[...] + dx
            h_ref[...] = h.astype(BF16)

    row = pl.BlockSpec((tm, d), lambda i, k: (i, 0))
    vecs = pl.BlockSpec((8, d), lambda i, k: (0, 0))
    return _call(
        body, name, (s // tm, nk),
        [
            pl.BlockSpec((1, tm, tk), lambda i, k: (k // per_part, i, k % per_part)),
            pl.BlockSpec((1, d, tk), lambda i, k: (0, 0, k)),
            row, vecs, row, vecs,
        ],
        [row, row, vecs],
        [jax.ShapeDtypeStruct((s, d), F32), jax.ShapeDtypeStruct((s, d), BF16), jax.ShapeDtypeStruct((8, d), F32)],
        [dproj, w, x, vec, dxo, dvec_in], scratch=[pltpu.VMEM((tm, d), F32)], phases=phases,
    )


def _tril(n):
    return lax.broadcasted_iota(jnp.int32, (n, n), 0) >= lax.broadcasted_iota(jnp.int32, (n, n), 1)


def _layernorm_stats(gv):
    mu = jnp.mean(gv, axis=-1, keepdims=True)
    cen = gv - mu
    rstd = lax.rsqrt(jnp.mean(cen * cen, axis=-1, keepdims=True) + EPS)
    return cen * rstd, rstd


def _shift_down(q, k, above_ref, c_cg, c_xb, first):
    width = q.shape[1]
    rows = lax.broadcasted_iota(jnp.int32, q.shape, 0)
    out = pltpu.roll(q, k, 0)
    for r in range(k):
        src = CONV_HALO - k + r
        above = above_ref[src : src + 1, c_cg : c_cg + width] * above_ref[src : src + 1, c_xb : c_xb + width]
        above = jnp.where(first, 0.0, above)
        out = jnp.where(rows == r, above, out)
    return out


def _ab_mix_fwd(proj, norm_v, w_s, b_rows, conv_w, phases=()):
    s, n = proj.shape
    heads, chunk, _ = w_s.shape
    da = norm_v.shape[1]
    hd = da // heads
    db = conv_w.shape[1]
    tm = _pick(s, (512, 256, 128))

    def body(p_ref, ph_ref, nv_ref, ws_ref, b_ref, cw_ref, o_ref):
        first = pl.program_id(0) == 0
        gu, _ = _gelu(p_ref[:, 0:da])
        gv, _ = _gelu(p_ref[:, da : 2 * da])
        xhat, _ = _layernorm_stats(gv)
        vn = (xhat * nv_ref[...]).astype(BF16)
        mask = _tril(chunk)
        for hh in range(heads):
            wm = jnp.where(mask, ws_ref[hh], 0.0).astype(BF16)
            cols = slice(hh * hd, (hh + 1) * hd)
            for nn in range(tm // chunk):
                rows = slice(nn * chunk, (nn + 1) * chunk)
                z = _dot(wm, vn[rows, cols]) + b_ref[:, cols]
                o_ref[rows, cols] = (gu[rows, cols] * z).astype(BF16)
        c_cg, c_xb = 2 * da + db, 2 * da + 2 * db
        bg = p_ref[:, 2 * da : 2 * da + db]
        q = p_ref[:, c_cg : c_cg + db] * p_ref[:, c_xb : c_xb + db]
        q1 = _shift_down(q, 1, ph_ref, c_cg, c_xb, first)
        q2 = _shift_down(q, 2, ph_ref, c_cg, c_xb, first)
        conv = cw_ref[0:1, :] * q2 + cw_ref[1:2, :] * q1 + cw_ref[2:3, :] * q
        o_ref[:, da : da + db] = (bg * conv).astype(BF16)

    nh = tm // CONV_HALO
    return _call(
        body, "ab_mix_fwd", (s // tm,),
        [
            pl.BlockSpec((tm, n), lambda i: (i, 0)),
            pl.BlockSpec((CONV_HALO, n), lambda i: (jnp.maximum(i * nh - 1, 0), 0)),
            pl.BlockSpec((1, da), lambda i: (0, 0)),
            pl.BlockSpec((heads, chunk, chunk), lambda i: (0, 0, 0)),
            pl.BlockSpec((chunk, da), lambda i: (0, 0)),
            pl.BlockSpec((3, db), lambda i: (0, 0)),
        ],
        [pl.BlockSpec((tm, da + db), lambda i: (i, 0))],
        [jax.ShapeDtypeStruct((s, da + db), BF16)], [proj, proj, norm_v, w_s, b_rows, conv_w], phases=phases,
    )


def _ab_mix_bwd(proj, dcat, norm_v, w_s, b_rows, conv_w, phases=()):
    s, n = proj.shape
    heads, chunk, _ = w_s.shape
    da = norm_v.shape[1]
    hd = da // heads
    db = conv_w.shape[1]
    tm = _pick(s, (512, 256, 128))
    nblk = s // tm
    dhalo = 2 * CONV_HALO

    def body(p_ref, pa_ref, pb_ref, dc_ref, dcb_ref, nv_ref, ws_ref, b_ref, cw_ref,
             dp_ref, dnv_ref, dws_ref, dzs_ref, dcw_ref, dvn_sc):
        i = pl.program_id(0)
        first, last = i == 0, i == nblk - 1

        @pl.when(first)
        def _():
            dnv_ref[...] = jnp.zeros_like(dnv_ref)
            dws_ref[...] = jnp.zeros_like(dws_ref)
            dzs_ref[...] = jnp.zeros_like(dzs_ref)
            dcw_ref[...] = jnp.zeros_like(dcw_ref)

        uu = p_ref[:, 0:da]
        gu, gu_grad = _gelu(uu)
        gv, gv_grad = _gelu(p_ref[:, da : 2 * da])
        xhat, rstd = _layernorm_stats(gv)
        nv = nv_ref[...]
        vn = (xhat * nv).astype(BF16)
        dya = dc_ref[:, 0:da].astype(F32)
        dz = (dya * gu).astype(BF16)
        mask = _tril(chunk)
        for hh in range(heads):
            wm = jnp.where(mask, ws_ref[hh], 0.0).astype(BF16)
            cols = slice(hh * hd, (hh + 1) * hd)
            dws = jnp.zeros((chunk, chunk), F32)
            for nn in range(tm // chunk):
                rows = slice(nn * chunk, (nn + 1) * chunk)
                z = _dot(wm, vn[rows, cols]) + b_ref[:, cols]
                dp_ref[rows, cols] = (dya[rows, cols] * z * gu_grad[rows, cols]).astype(BF16)
                dz_blk = dz[rows, cols]
                dws = dws + _dot_nt(dz_blk, vn[rows, cols])
                dzs_ref[:, cols] += dz_blk.astype(F32)
                dvn = _dot_tn(wm, dz_blk)
                dnv_ref[:, cols] += jnp.sum(dvn * xhat[rows, cols], axis=0, keepdims=True)
                dvn_sc[rows, cols] = dvn
            dws_ref[hh] += jnp.where(mask, dws, 0.0)
        dxhat = dvn_sc[...] * nv
        dgv = rstd * (dxhat - jnp.mean(dxhat, axis=-1, keepdims=True) - xhat * jnp.mean(dxhat * xhat, axis=-1, keepdims=True))
        dp_ref[:, da : 2 * da] = (dgv * gv_grad).astype(BF16)

        c_bg, c_cg, c_xb = 2 * da, 2 * da + db, 2 * da + 2 * db
        bg = p_ref[:, c_bg : c_bg + db]
        cg = p_ref[:, c_cg : c_cg + db]
        xb = p_ref[:, c_xb : c_xb + db]
        q = cg * xb
        q1 = _shift_down(q, 1, pa_ref, c_cg, c_xb, first)
        q2 = _shift_down(q, 2, pa_ref, c_cg, c_xb, first)
        dyb = dc_ref[:, da : da + db].astype(F32)
        conv = cw_ref[0:1, :] * q2 + cw_ref[1:2, :] * q1 + cw_ref[2:3, :] * q
        dp_ref[:, c_bg : c_bg + db] = (dyb * conv).astype(BF16)
        e = dyb * bg
        dcw_ref[0:1, :] += jnp.sum(e * q2, axis=0, keepdims=True)
        dcw_ref[1:2, :] += jnp.sum(e * q1, axis=0, keepdims=True)
        dcw_ref[2:3, :] += jnp.sum(e * q, axis=0, keepdims=True)
        rows = lax.broadcasted_iota(jnp.int32, e.shape, 0)
        dq = cw_ref[2:3, :] * e
        for kk in (1, 2):
            ek = pltpu.roll(e, tm - kk, 0)
            for r in range(kk):
                below = dcb_ref[r : r + 1, da : da + db].astype(F32) * pb_ref[r : r + 1, c_bg : c_bg + db]
                below = jnp.where(last, 0.0, below)
                ek = jnp.where(rows == tm - kk + r, below, ek)
            dq = dq + cw_ref[2 - kk : 3 - kk, :] * ek
        dp_ref[:, c_cg : c_cg + db] = (dq * xb).astype(BF16)
        dp_ref[:, c_xb : c_xb + db] = (dq * cg).astype(BF16)

    nh = tm // CONV_HALO
    nhb = tm // dhalo
    const2 = lambda i: (0, 0)
    return _call(
        body, "ab_mix_bwd", (nblk,),
        [
            pl.BlockSpec((tm, n), lambda i: (i, 0)),
            pl.BlockSpec((CONV_HALO, n), lambda i: (jnp.maximum(i * nh - 1, 0), 0)),
            pl.BlockSpec((CONV_HALO, n), lambda i: (jnp.minimum((i + 1) * nh, s // CONV_HALO - 1), 0)),
            pl.BlockSpec((tm, da + db), lambda i: (i, 0)),
            pl.BlockSpec((dhalo, da + db), lambda i: (jnp.minimum((i + 1) * nhb, s // dhalo - 1), 0)),
            pl.BlockSpec((1, da), const2),
            pl.BlockSpec((heads, chunk, chunk), lambda i: (0, 0, 0)),
            pl.BlockSpec((chunk, da), const2),
            pl.BlockSpec((3, db), const2),
        ],
        [
            pl.BlockSpec((tm, n), lambda i: (i, 0)),
            pl.BlockSpec((1, da), const2),
            pl.BlockSpec((heads, chunk, chunk), lambda i: (0, 0, 0)),
            pl.BlockSpec((chunk, da), const2),
            pl.BlockSpec((3, db), const2),
        ],
        [
            jax.ShapeDtypeStruct((s, n), BF16),
            jax.ShapeDtypeStruct((1, da), F32),
            jax.ShapeDtypeStruct((heads, chunk, chunk), F32),
            jax.ShapeDtypeStruct((chunk, da), F32),
            jax.ShapeDtypeStruct((3, db), F32),
        ],
        [proj, proj, proj, dcat, dcat, norm_v, w_s, b_rows, conv_w],
        scratch=[pltpu.VMEM((tm, da), F32)], phases=phases,
    )


def _pool_counts(tm, i, w):
    t = i * tm + lax.broadcasted_iota(jnp.int32, (tm, 1), 0)
    return jnp.minimum(t + 1, w).astype(F32)


def _pool_fwd(x, vec, w_grp, scale, phases=()):
    s, d = x.shape
    groups, gd, _ = w_grp.shape
    tm = _pick(s, (512, 256, 128))

    def body(x_ref, xa_ref, vec_ref, w_ref, sc_ref, xo_ref, p_ref, o_ref):
        i = pl.program_id(0)
        h = _modulate(x_ref[...], vec_ref)
        ha = jnp.where(i == 0, 0.0, _modulate(xa_ref[...], vec_ref))
        ext = jnp.concatenate([ha, h], axis=0)
        for gi, w in enumerate(POOL_WINDOWS):
            cols = slice(gi * gd, (gi + 1) * gd)
            acc = ext[:, cols]
            step = 1
            while step < w:
                acc = acc + pltpu.roll(acc, step, 0)
                step *= 2
            p = (acc[POOL_HALO:, :] / _pool_counts(tm, i, w) - h[:, cols]).astype(BF16)
            p_ref[:, cols] = p
            o_ref[:, cols] = _dot(p, w_ref[gi]).astype(BF16)
        xo_ref[...] = x_ref[...] + vec_ref[3:4, :] * (o_ref[...].astype(F32) * sc_ref[...])

    nh = tm // POOL_HALO
    row = pl.BlockSpec((tm, d), lambda i: (i, 0))
    return _call(
        body, "pool_fwd", (s // tm,),
        [
            row,
            pl.BlockSpec((POOL_HALO, d), lambda i: (jnp.maximum(i * nh - 1, 0), 0)),
            pl.BlockSpec((8, d), lambda i: (0, 0)),
            pl.BlockSpec((groups, gd, gd), lambda i: (0, 0, 0)),
            pl.BlockSpec((1, d), lambda i: (0, 0)),
        ],
        [row, row, row],
        [jax.ShapeDtypeStruct((s, d), F32), jax.ShapeDtypeStruct((s, d), BF16), jax.ShapeDtypeStruct((s, d), BF16)],
        [x, x, vec, w_grp, scale], phases=phases,
    )


def _pool_bwd(dxo, x, vec, p, o, w_grp, scale, phases=()):
    s, d = x.shape
    groups, gd, _ = w_grp.shape
    tm = _pick(s, (512, 256, 128))
    nblk = s // tm

    def body(dxo_ref, dxb_ref, x_ref, vec_ref, p_ref, o_ref, w_ref, sc_ref, dx_ref, dw_ref, dsc_ref, dvec_ref, dw_sc):
        i = pl.program_id(0)

        @pl.when(i == 0)
        def _():
            dw_sc[...] = jnp.zeros_like(dw_sc)
            dsc_ref[...] = jnp.zeros_like(dsc_ref)
            dvec_ref[...] = jnp.zeros_like(dvec_ref)

        gate, sc = vec_ref[3:4, :], sc_ref[...]
        dxo_v = dxo_ref[...]
        ov = o_ref[...].astype(F32)
        dvec_ref[3:4, :] += jnp.sum(dxo_v * (ov * sc), axis=0, keepdims=True)
        dy = gate * dxo_v
        dsc_ref[...] += jnp.sum(dy * ov, axis=0, keepdims=True)
        dout = (dy * sc).astype(BF16)
        dout_b = jnp.where(i == nblk - 1, 0.0, gate * dxb_ref[...] * sc).astype(BF16)
        for gi, w in enumerate(POOL_WINDOWS):
            cols = slice(gi * gd, (gi + 1) * gd)
            dw_sc[gi] += _dot_tn(p_ref[:, cols], dout[:, cols])
            wb = w_ref[gi]
            dp = _dot_nt(dout[:, cols], wb)
            dp_b = _dot_nt(dout_b[:, cols], wb)
            e = dp / _pool_counts(tm, i, w)
            t_below = (i + 1) * tm + lax.broadcasted_iota(jnp.int32, (POOL_HALO, 1), 0)
            e_b = dp_b / jnp.minimum(t_below + 1, w).astype(F32)
            acc = jnp.concatenate([e, e_b], axis=0)
            step = 1
            while step < w:
                acc = acc + pltpu.roll(acc, tm + POOL_HALO - step, 0)
                step *= 2
            dx_ref[:, cols] = acc[:tm, :] - dp
        dx, _ = _modulate_bwd(x_ref[...], dx_ref[...], vec_ref, dvec_ref)
        dx_ref[...] = dxo_v + dx

        @pl.when(i == nblk - 1)
        def _():
            dw_ref[...] = dw_sc[...].astype(BF16)

    nh = tm // POOL_HALO
    row = pl.BlockSpec((tm, d), lambda i: (i, 0))
    vecs = pl.BlockSpec((8, d), lambda i: (0, 0))
    wspec = pl.BlockSpec((groups, gd, gd), lambda i: (0, 0, 0))
    return _call(
        body, "pool_bwd", (nblk,),
        [
            row,
            pl.BlockSpec((POOL_HALO, d), lambda i: (jnp.minimum((i + 1) * nh, s // POOL_HALO - 1), 0)),
            row, vecs, row, row, wspec,
            pl.BlockSpec((1, d), lambda i: (0, 0)),
        ],
        [row, wspec, pl.BlockSpec((1, d), lambda i: (0, 0)), vecs],
        [
            jax.ShapeDtypeStruct((s, d), F32),
            jax.ShapeDtypeStruct((groups, gd, gd), BF16),
            jax.ShapeDtypeStruct((1, d), F32),
            jax.ShapeDtypeStruct((8, d), F32),
        ],
        [dxo, dxo, x, vec, p, o, w_grp, scale],
        scratch=[pltpu.VMEM((groups, gd, gd), F32)], phases=phases,
    )


def _loss_head(x, gain, target, phases=()):
    s, d = x.shape
    tm = _pick(s, (512, 256, 128))

    def body(x_ref, g_ref, t_ref, dx_ref, aux_ref):
        @pl.when(pl.program_id(0) == 0)
        def _():
            aux_ref[...] = jnp.zeros_like(aux_ref)

        xv = x_ref[...]
        rstd = _rstd(xv)
        r = xv * rstd
        gain_v = g_ref[...]
        err = r * gain_v - t_ref[...]
        aux_ref[1:2, :] += jnp.sum(err * err, axis=0, keepdims=True)
        dout = err * (1.0 / d)
        aux_ref[0:1, :] += jnp.sum(dout * r, axis=0, keepdims=True)
        dr = dout * gain_v
        dx_ref[...] = rstd * (dr - r * jnp.mean(dr * r, axis=-1, keepdims=True))

    row = pl.BlockSpec((tm, d), lambda i: (i, 0))
    return _call(
        body, "loss_head", (s // tm,),
        [row, pl.BlockSpec((1, d), lambda i: (0, 0)), row],
        [row, pl.BlockSpec((8, d), lambda i: (0, 0))],
        [jax.ShapeDtypeStruct((s, d), F32), jax.ShapeDtypeStruct((8, d), F32)], [x, gain, target], phases=phases,
    )


def _small_adam(gathered, gathered_ws, layout, smalls, chip):
    names = list(smalls)
    n = len(names)
    loss_row, _, _, n_feat = layout["loss"]

    def body(*refs):
        chip_ref, g_ref, gws_ref = refs[0], refs[1], refs[2]
        wmv = refs[3 : 3 + 3 * n]
        outs = refs[3 + 3 * n : 3 + 7 * n]
        total = refs[-1]
        total[...] = g_ref[0]
        for kdev in range(1, N_DEV):
            total[...] += g_ref[kdev]
        total_ws = gws_ref[0]
        for kdev in range(1, N_DEV):
            total_ws = total_ws + gws_ref[kdev]
        my_chip = chip_ref[0]
        for a, name in enumerate(names):
            w_ref, m_ref, v_ref = wmv[3 * a : 3 * a + 3]
            if name == "ab_w_s":
                g = total_ws
            else:
                row0, rows, col0, cols = layout[name]
                if col0 is None:
                    g = jnp.zeros((rows, cols), F32)
                    for j in range(N_CHIPS):
                        g = g + jnp.where(my_chip == j, total[row0 : row0 + rows, j * cols : (j + 1) * cols], 0.0)
                else:
                    g = total[row0 : row0 + rows, col0 : col0 + cols]
            dl, mo, vo = _adam(w_ref[...], g, m_ref[...], v_ref[...])
            outs[4 * a][...] = g
            outs[4 * a + 1][...] = dl
            outs[4 * a + 2][...] = mo
            outs[4 * a + 3][...] = vo
        refs[3 + 7 * n][...] = 0.5 * jnp.sum(total[loss_row : loss_row + 1, 0:n_feat], axis=1, keepdims=True) / n_feat

    ins = [gathered, gathered_ws]
    out_shapes = []
    for name in names:
        ins.extend(smalls[name])
        out_shapes.extend([jax.ShapeDtypeStruct(smalls[name][0].shape, F32)] * 4)
    out_shapes.append(jax.ShapeDtypeStruct((1, 1), F32))
    whole = lambda shape: pl.BlockSpec(shape, functools.partial(lambda nd, i, c: (0,) * nd, len(shape)))
    res = pl.pallas_call(
        body, name="small_adam",
        grid_spec=pltpu.PrefetchScalarGridSpec(
            num_scalar_prefetch=1, grid=(1,),
            in_specs=[whole(a.shape) for a in ins], out_specs=[whole(o.shape) for o in out_shapes],
            scratch_shapes=[pltpu.VMEM(gathered.shape[1:], F32)],
        ),
        out_shape=out_shapes,
        compiler_params=pltpu.CompilerParams(dimension_semantics=("arbitrary",), vmem_limit_bytes=VMEM_LIMIT_BYTES),
    )(chip.reshape(1).astype(jnp.int32), *ins)
    return {name: res[4 * a : 4 * a + 4] for a, name in enumerate(names)}, res[4 * n]


def _pad_rows(a, rows=8):
    extra = (-a.shape[0]) % rows
    return jnp.pad(a, ((0, extra), (0, 0))) if extra else a


def _pad_cols(a, cols):
    return jnp.pad(a, ((0, 0), (0, cols - a.shape[1]))) if a.shape[1] < cols else a


def _run(fn, *phases):
    outs, p_outs = fn(list(phases))
    for p, po in zip(phases, p_outs):
        p.then(po)
    return outs


def kernel(x, c, norm_g, w_mod, b_mod, w_ffn_in, w_ffn_out, ab_w_in, ab_norm_v, ab_w_s, ab_b_s, ab_conv_w, ab_w_out, pool_w_grp, pool_scale, final_g, loss_target, m_norm_g, m_w_mod, m_b_mod, m_w_ffn_in, m_w_ffn_out, m_ab_w_in, m_ab_norm_v, m_ab_w_s, m_ab_b_s, m_ab_conv_w, m_ab_w_out, m_pool_w_grp, m_pool_scale, m_final_g, v_norm_g, v_w_mod, v_b_mod, v_w_ffn_in, v_w_ffn_out, v_ab_w_in, v_ab_norm_v, v_ab_w_s, v_ab_b_s, v_ab_conv_w, v_ab_w_out, v_pool_w_grp, v_pool_scale, v_final_g):
    ix, iy, ic = _place()
    chip = 2 * ix + iy
    me = 4 * ix + 2 * iy + ic
    where = jnp.stack([chip, ic]).astype(jnp.int32)
    s, d = x.shape[1], x.shape[2]
    x0 = x.reshape(s, d)
    target = loss_target.reshape(s, d)
    n_layers = norm_g.shape[0]
    dq = d // N_CHIPS
    heads, chunk = ab_w_s.shape[1], ab_w_s.shape[2]
    da = ab_norm_v.shape[1]
    db = ab_conv_w.shape[2] * N_CHIPS
    f_hidden = w_ffn_out.shape[2] * N_CHIPS
    assert n_layers == 2 and da % heads == 0

    cw_pad = _pad_cols(ab_conv_w.reshape(3, db // N_CHIPS), dq)
    packed = jnp.concatenate(
        [_pad_rows(c.reshape(N_CHIPS, dq)), _pad_rows(norm_g.reshape(-1, dq)), _pad_rows(pool_scale.reshape(1, dq)), _pad_rows(cw_pad)],
        axis=0,
    )
    ncol = w_mod.shape[2]
    b_cols = lax.dynamic_slice(b_mod, (0, chip * ncol), (n_layers, ncol)).reshape(n_layers, 1, ncol)
    small = {}

    def small_gather(key, arrs):
        def then(outs):
            small[key] = outs

        return _phase_small_gather(arrs, then)

    stacks = {
        "w_ffn_in": tuple(a.reshape((-1,) + a.shape[2:]) for a in (w_ffn_in, m_w_ffn_in, v_w_ffn_in)),
        "w_ffn_out": tuple(a.reshape((-1,) + a.shape[2:]) for a in (w_ffn_out, m_w_ffn_out, v_w_ffn_out)),
        "ab_w_in": (ab_w_in, m_ab_w_in, v_ab_w_in),
        "ab_w_out": (ab_w_out, m_ab_w_out, v_ab_w_out),
        "pool_w_grp": (pool_w_grp[0], m_pool_w_grp[0], v_pool_w_grp[0]),
    }
    big_in = _Big((1, d, 2 * f_hidden), 2, 1)
    big_out = _Big((1, f_hidden, d), 1, 2)
    units = {}
    for l in range(n_layers):
        for k in range(2):
            units[f"in{l}{k}"] = (big_in, "w_ffn_in", 2 * l + k)
            units[f"out{l}{k}"] = (big_out, "w_ffn_out", 2 * l + k)
    units["abin"] = (_Big((1, d, ab_w_in.shape[2] * N_CHIPS), 2, 1), "ab_w_in", 0)
    units["about"] = (_Big((1, ab_w_out.shape[1] * N_CHIPS, d), 1, 2), "ab_w_out", 0)
    units["pool"] = (_Big((pool_w_grp.shape[1], pool_w_grp.shape[2] * N_CHIPS, pool_w_grp.shape[3]), 1, 0), "pool_w_grp", 0)
    big = {u: g for u, (g, _, _) in units.items()}

    weight = {}
    complete = set()

    def cast(u):
        g, st, b0 = units[u]

        def launch(phases):
            (weight[u],), p_outs = _cast_into_full(stacks[st][0], b0, g, where, "cast_" + u, phases)
            return None, p_outs

        return launch

    def gather_relay(us, second):
        def then(outs):
            for u, o in zip(us, outs):
                weight[u] = o

        return _phase_gather_relay([weight[u] for u in us], [big[u] for u in us], second, then)

    def gather_sibling(*us):
        def then(outs):
            for u, o in zip(us, outs):
                weight[u] = o
                complete.add(u)

        return _phase_gather_sibling([weight[u] for u in us], [big[u] for u in us], then)

    def w_of(u):
        assert u in complete, u
        return weight[u]

    _run(cast("in00"), small_gather("inputs", [packed]))
    small_all = small["inputs"][0]
    by_chip = small_all[0::2]
    c_all = small_all[:, 0:N_CHIPS, :].reshape(N_DEV, d)
    norm_full = by_chip[:, 8 : 8 + 3 * n_layers, :].transpose(1, 0, 2).reshape(3 * n_layers, d)
    pool_scale_full = by_chip[:, 16:17, :].transpose(1, 0, 2).reshape(1, d)
    conv_full = by_chip[:, 24:27, : db // N_CHIPS].transpose(1, 0, 2).reshape(3, db)
    pieces = [("in00", "out00"), ("abin", "about"), ("in01", "out01"), ("in10", "out10", "pool"), ("in11", "out11")]
    in_flight = {}

    def start_gather(p):
        in_flight[p, 0] = _split_start(gather_relay(pieces[p], False), f"gather_{p}_start")

    def relay_gather(p):
        flight = in_flight.pop((p, 0))
        _split_wait(flight, list(started().ins), f"gather_{p}_arrived")
        in_flight[p, 1] = _split_start(gather_relay(pieces[p], True), f"gather_{p}_relay")

    def started():
        return _after(*[flight.token for flight in in_flight.values()])

    def finish_gather(p, after, meanwhile=None):
        flight = in_flight.pop((p, 1))
        _split_wait(flight, list(after) + list(started().ins), f"gather_{p}_wait")
        crossing = _split_start(gather_sibling(*pieces[p]), f"gather_{p}_forward")
        behind = [crossing.token]
        if p + 2 < len(pieces):
            for u in pieces[p + 2]:
                _run(cast(u), _after(crossing.token))
        if p + 1 < len(pieces):
            relay_gather(p + 1)
        if p + 2 < len(pieces):
            start_gather(p + 2)
        behind = behind + list(started().ins)
        if meanwhile is not None:
            behind = behind + meanwhile(_after(crossing.token))
        _split_wait(crossing, behind, f"gather_{p}_forwarded")

    _run(cast("out00"))
    start_gather(0)
    mod_cols = _run(lambda phases: _mod_fwd(c_all, w_mod, b_cols, phases), started())[0]

    def mod_rows(outs):
        small["mod"] = outs

    _run(cast("about"), started())
    _run(cast("abin"), _phase_small_exchange(mod_cols.transpose(1, 0, 2), mod_rows), started())
    relay_gather(0)
    start_gather(1)
    mod_mine = small["mod"][0][0::2]
    mod = mod_mine.transpose(1, 0, 2).reshape(n_layers, 3, 3, d)
    vecs = {
        (l, sub): jnp.pad(norm_full[3 * l + sub][None], ((0, 7), (0, 0))) + jnp.pad(mod[l, sub], ((1, 4), (0, 0)))
        for l in range(n_layers)
        for sub in range(3)
    }
    b_rows = jnp.broadcast_to(ab_b_s[0].T[:, :, None], (chunk, heads, da // heads)).reshape(chunk, da)

    saved = {}

    def ffn_forward(xs, l, sub, k, *phases):
        saved[l, sub, "x"] = xs
        xs, gg, uu, yb = _run(
            lambda ph: _ffn_fwd(xs, vecs[l, sub], w_of(f"in{l}{k}"), w_of(f"out{l}{k}"), f"ffn_fwd_{l}{k}", ph), *phases
        )
        saved[l, sub, "act"] = (gg, uu, yb)
        return xs

    finish_gather(0, [vecs[0, 0]])
    xs = ffn_forward(x0, 0, 0, 0, started())
    saved[0, 1, "x"] = xs
    finish_gather(1, [xs])
    (proj,) = _run(lambda ph: _proj_mod_fwd(xs, vecs[0, 1], w_of("abin"), ph), started())
    (cat,) = _run(lambda ph: _ab_mix_fwd(proj, ab_norm_v, ab_w_s[0], b_rows, conv_full, ph))
    xs, yb = _run(lambda ph: _proj_res_fwd(cat, w_of("about"), xs, vecs[0, 1], ph))
    saved[0, 1, "act"] = (proj, cat, yb)
    finish_gather(2, [xs])
    xs = ffn_forward(xs, 0, 2, 1, started())
    finish_gather(3, [xs])
    xs = ffn_forward(xs, 1, 0, 0, started())
    saved[1, 1, "x"] = xs
    pooled = []

    def pool_forward(behind):
        pooled.extend(_run(lambda ph: _pool_fwd(xs, vecs[1, 1], w_of("pool"), pool_scale_full, ph), behind))
        return [pooled[0]]

    finish_gather(4, [xs], pool_forward)
    xs, pp, oo = pooled
    saved[1, 1, "act"] = (pp, oo)
    xs = ffn_forward(xs, 1, 2, 1)
    dxs, aux = _run(lambda ph: _loss_head(xs, final_g.reshape(1, d), target, ph))

    grad = {}
    recv = {}
    csum = {}
    parts = {}
    reduced = {}
    done = set()
    dvecs, small_g = {}, {}

    def pair_exchange(*us):
        def then(outs):
            for u, o in zip(us, outs):
                recv[u] = o

        return _phase_pair_exchange([grad[u] for u in us], [big[u] for u in us], then)

    def grad_half(u, a, bs, mine, name, *phases):
        (res,) = _run(lambda ph: _grad_half(a, bs, big[u], where, mine, recv[u] if mine else None, name, ph), *phases)
        return res

    def pair_sum(u, *phases):
        def launch(ph):
            (csum[u],), p_outs = _pair_sum(grad[u], recv[u], big[u], where, "pair_sum_" + u, ph)
            return None, p_outs

        _run(launch, *phases)

    def chip_exchange(*us):
        def then(outs):
            for u, o in zip(us, outs):
                parts[u] = o

        return _phase_chip_exchange([csum[u] for u in us], [big[u] for u in us], then)

    def chip_sum(*us, carried=()):
        for n_u, u in enumerate(us):
            g, st, b0 = units[u]

            def launch(ph):
                (reduced[st],), p_outs = _chip_sum(
                    csum[u], parts[u], g, where, reduced.get(st), stacks[st][0].shape, b0, "chip_sum_" + u, ph
                )
                return None, p_outs

            _run(launch, *(carried if n_u == 0 else ()))

    def pair_broadcast(*us):
        sts = [units[u][1] for u in us]
        assert len(set(sts)) == len(sts)

        def then(outs):
            for u, st, o in zip(us, sts, outs):
                reduced[st] = o
                done.add(u)

        return _phase_pair_broadcast([reduced[st] for st in sts], [big[u] for u in us], [units[u][2] for u in us], then)

    def ffn_backward(dxs, l, sub, k, carried_bwd, carried_send, carried_mine):
        gg, uu, yb = saved[l, sub, "act"]
        w_in, w_out = w_of(f"in{l}{k}"), w_of(f"out{l}{k}")
        uo, ui, tag = f"out{l}{k}", f"in{l}{k}", f"{l}{k}"
        dxs, dg, du, a, h, dy, dvecs[l, sub] = _run(
            lambda ph: _ffn_bwd(dxs, saved[l, sub, "x"], vecs[l, sub], gg, uu, yb, w_in, w_out, "ffn_bwd_" + tag, ph), *carried_bwd()
        )
        grad[uo] = grad_half(uo, a, [dy], False, "dw_out_send_" + tag, *carried_send())
        grad[ui] = grad_half(ui, h, [dg, du], False, "dw_in_send_" + tag, pair_exchange(uo))
        csum[uo] = grad_half(uo, a, [dy], True, "dw_out_" + tag, pair_exchange(ui))
        csum[ui] = grad_half(ui, h, [dg, du], True, "dw_in_" + tag, *carried_mine())
        return dxs

    none = lambda: ()
    dxs = ffn_backward(dxs, 1, 2, 1, none, none, none)
    pp, oo = saved[1, 1, "act"]
    dxs, grad["pool"], small_g["pool_scale"], dvecs[1, 1] = _run(
        lambda ph: _pool_bwd(dxs, saved[1, 1, "x"], vecs[1, 1], pp, oo, w_of("pool"), pool_scale_full, ph)
    )

    def after_11():
        return (chip_exchange("in11", "out11"), pair_exchange("pool"))

    def bcast_11():
        chip_sum("in11", "out11")
        pair_sum("pool")
        return (pair_broadcast("in11", "out11"), chip_exchange("pool"))

    dxs = ffn_backward(dxs, 1, 0, 0, after_11, bcast_11, none)

    def after_10():
        return (chip_exchange("in10", "out10"),)

    def bcast_10():
        chip_sum("in10", "out10", "pool")
        return (pair_broadcast("in10", "out10", "pool"),)

    dxs = ffn_backward(dxs, 0, 2, 1, after_10, bcast_10, none)

    proj, cat, yb = saved[0, 1, "act"]
    out01 = _split_start(chip_exchange("out01"), "reduce_out01_start")
    dy, dcat, dgate = _run(lambda ph: _proj_res_bwd(dxs, yb, vecs[0, 1], w_of("about"), ph), _after(out01.token))
    grad["about"] = grad_half("about", cat, [dy], False, "dw_ab_out_send")
    dproj, small_g["ab_norm_v"], small_g["ab_w_s"], dzs, small_g["ab_conv_w"] = _run(
        lambda ph: _ab_mix_bwd(proj, dcat, ab_norm_v, ab_w_s[0], b_rows, conv_full, ph), pair_exchange("about")
    )
    small_g["ab_b_s"] = dzs.reshape(chunk, heads, da // heads).sum(axis=2).T
    dxs, h, dvecs[0, 1] = _run(
        lambda ph: _proj_mod_bwd(dproj[None], w_of("abin"), saved[0, 1, "x"], vecs[0, 1], dxs, dgate, "ab_in_bwd", ph)
    )
    grad["abin"] = grad_half("abin", h, [dproj], False, "dw_ab_in_send")
    (csum["out01"],) = _split_wait(out01, [grad["abin"]], "reduce_out01_wait")
    chip_sum("out01", carried=(pair_exchange("abin"),))
    csum["about"] = grad_half("about", cat, [dy], True, "dw_ab_out", pair_broadcast("out01"))
    csum["abin"] = grad_half("abin", h, [dproj], True, "dw_ab_in")

    layout = {}
    tail = {}

    def after_01():
        tail["01"] = _split_start(chip_exchange("in01", "abin", "about"), "reduce_01_start")
        return (_after(tail["01"].token),)

    def pack_small_grads():
        dvec_all = jnp.stack([dvecs[l, sub] for l in range(n_layers) for sub in range(3)])
        dgain = dvec_all[:, 0, :]
        dmod = dvec_all[:, 1:4, :].reshape(3 * 3 * n_layers, d)
        rows = {
            "norm_g": (dgain, None, dq), "final_g": (aux[0:1], 0, d), "pool_scale": (small_g["pool_scale"], None, dq),
            "b_mod": (dmod, 0, d), "ab_norm_v": (small_g["ab_norm_v"], 0, da),
            "ab_conv_w": (small_g["ab_conv_w"], None, db // N_CHIPS), "ab_b_s": (small_g["ab_b_s"], 0, chunk),
            "loss": (aux[1:2], 0, d),
        }
        row0 = 0
        for nm, (pc, col0, cols) in rows.items():
            layout[nm] = (row0, pc.shape[0], col0, cols)
            row0 += pc.shape[0]
        packed_rows = -(-row0 // 8) * 8
        return sum(
            jnp.pad(pc, ((layout[nm][0], packed_rows - layout[nm][0] - pc.shape[0]), (0, d - pc.shape[1])))
            for nm, (pc, _, _) in rows.items()
        )

    def bcast_01():
        csum["in01"], csum["abin"], csum["about"] = _split_wait(tail["01"], [dvecs[0, 0]], "reduce_01_wait")
        chip_sum("in01", "abin", "about")
        grads_small = [pack_small_grads(), small_g["ab_w_s"].reshape(heads * chunk, chunk)]
        tail["small"] = _split_start(small_gather("grads", grads_small), "gather_small_grads_start")
        return (pair_broadcast("in01", "abin", "about"), _after(tail["small"].token))

    def reduce_out00():
        tail["out00"] = _split_start(chip_exchange("out00"), "reduce_out00_start")
        return (_after(tail["out00"].token),)

    dxs = ffn_backward(dxs, 0, 0, 0, after_01, bcast_01, reduce_out00)
    grad_x = dxs.reshape(x.shape)

    last = _split_start(chip_exchange("in00"), "reduce_last_start")
    (csum["out00"],) = _split_wait(tail["out00"], [last.token], "reduce_out00_wait")
    chip_sum("out00")
    _flush("broadcast_out00", pair_broadcast("out00"))
    _split_wait(tail["small"], [reduced["w_ffn_out"]], "gather_small_grads_wait")
    g_all, gws_all = small["grads"]

    out = {}

    def adam_stack(st, after=()):
        w3, m3, v3 = stacks[st]
        assert all(u in done for u, (_, ust, _) in units.items() if ust == st), st
        shape = {"w_ffn_in": w_ffn_in.shape, "w_ffn_out": w_ffn_out.shape, "pool_w_grp": pool_w_grp.shape}.get(st, w3.shape)
        out[st] = tuple(a.reshape(shape) for a in _adam_stack(w3, reduced[st], m3, v3, "adam_" + st, after))

    for st in ("w_ffn_out", "ab_w_in", "ab_w_out", "pool_w_grp"):
        adam_stack(st, (last.token,))

    shapes2d = {
        "norm_g": (3 * n_layers, dq), "b_mod": (9 * n_layers, d), "final_g": (1, d), "ab_norm_v": (1, da),
        "pool_scale": (1, dq), "ab_conv_w": (3, db // N_CHIPS), "ab_b_s": (heads, chunk), "ab_w_s": (heads * chunk, chunk),
    }
    small_w = {"norm_g": (norm_g, m_norm_g, v_norm_g), "b_mod": (b_mod, m_b_mod, v_b_mod), "final_g": (final_g, m_final_g, v_final_g),
               "ab_norm_v": (ab_norm_v, m_ab_norm_v, v_ab_norm_v), "pool_scale": (pool_scale, m_pool_scale, v_pool_scale),
               "ab_conv_w": (ab_conv_w, m_ab_conv_w, v_ab_conv_w), "ab_b_s": (ab_b_s, m_ab_b_s, v_ab_b_s), "ab_w_s": (ab_w_s, m_ab_w_s, v_ab_w_s)}
    smalls = {nm: tuple(a.reshape(shapes2d[nm]) for a in wmv) for nm, wmv in small_w.items()}
    small_out, loss = _small_adam(g_all, gws_all, layout, smalls, chip)
    loss = loss.reshape(())
    for nm, res in small_out.items():
        out[nm] = tuple(a.reshape(small_w[nm][0].shape) for a in res)

    mod_row0 = layout["b_mod"][0]
    dmod_all = g_all[:, mod_row0 : mod_row0 + 9 * n_layers, :].reshape(N_DEV, n_layers, 9 * d)
    dmod_cols = lax.dynamic_slice(dmod_all, (0, 0, chip * ncol), (N_DEV, n_layers, ncol)).transpose(1, 0, 2)
    out["w_mod"] = tuple(_mod_bwd_adam(c_all.T, dmod_cols, w_mod, m_w_mod, v_w_mod, (last.token,)))

    (csum["in00"],) = _split_wait(
        last, [out[st][1] for st in ("w_mod", "w_ffn_out", "ab_w_in", "ab_w_out", "pool_w_grp")], "reduce_last_wait"
    )
    chip_sum("in00")
    _flush("broadcast_last", pair_broadcast("in00"))
    adam_stack("w_ffn_in")

    order = ["norm_g", "w_mod", "b_mod", "w_ffn_in", "w_ffn_out", "ab_w_in", "ab_norm_v", "ab_w_s", "ab_b_s", "ab_conv_w", "ab_w_out", "pool_w_grp", "pool_scale", "final_g"]
    return (loss, grad_x, *[out[nm][0] for nm in order], *[out[nm][1] for nm in order], *[out[nm][2] for nm in order], *[out[nm][3] for nm in order])
```

```python
import functools
import math

import jax
import jax.numpy as jnp
from jax import lax
from jax.experimental import pallas as pl
from jax.experimental.pallas import tpu as pltpu

F32 = jnp.float32
BF16 = jnp.bfloat16
MESH = pl.DeviceIdType.MESH

EPS = 1e-6
ADAM_LR = 0.001
ADAM_B1 = 0.9
ADAM_B2 = 0.999
ADAM_EPS = 1e-08
ADAM_WD = 0.01
ADAM_STEP = 10
POOL_WINDOWS = (2, 4, 8, 16)
POOL_HALO = 16
CONV_HALO = 8
N_CHIPS = 4
N_DEV = 8
VMEM_LIMIT_BYTES = 48 * 1024 * 1024
EW_BLOCK_ELEMS = 1024 * 1024
ADAM_BLOCK_ELEMS = 512 * 1024


def _pick(n, prefs):
    for p in prefs:
        if p <= n and n % p == 0:
            return p
    return n


def _row_tile(rows, cols, block_elems=EW_BLOCK_ELEMS):
    best = None
    for d in range(16, rows + 1, 16):
        if rows % d == 0 and d * cols <= block_elems:
            best = d
    return best or rows


def _dot(a, b):
    return jnp.dot(a, b, preferred_element_type=F32)


def _dot_nt(a, b):
    return lax.dot_general(a, b, (((1,), (1,)), ((), ())), preferred_element_type=F32)


def _dot_tn(a, b):
    return lax.dot_general(a, b, (((0,), (0,)), ((), ())), preferred_element_type=F32)


def _sigmoid(x):
    return 0.5 * jnp.tanh(0.5 * x) + 0.5


_GELU_C = math.sqrt(2.0 / math.pi)


def _gelu(x):
    x2 = x * x
    t = jnp.tanh(_GELU_C * (x + 0.044715 * x2 * x))
    val = 0.5 * x * (1.0 + t)
    grad = 0.5 * (1.0 + t) + 0.5 * x * (1.0 - t * t) * (_GELU_C * (1.0 + 3.0 * 0.044715 * x2))
    return val, grad


def _rstd(x):
    return lax.rsqrt(jnp.mean(x * x, axis=-1, keepdims=True) + EPS)


def _modulate(x, vec_ref):
    return (x * _rstd(x)) * vec_ref[0:1, :] * (1.0 + vec_ref[2:3, :]) + vec_ref[1:2, :]


def _modulate_bwd(x, dh, vec_ref, dvec_ref):
    gn, sh, sc = vec_ref[0:1, :], vec_ref[1:2, :], vec_ref[2:3, :]
    rstd = _rstd(x)
    r = x * rstd
    dvec_ref[0:1, :] += jnp.sum(dh * r * (1.0 + sc), axis=0, keepdims=True)
    dvec_ref[1:2, :] += jnp.sum(dh, axis=0, keepdims=True)
    dvec_ref[2:3, :] += jnp.sum(dh * r * gn, axis=0, keepdims=True)
    gm = gn * (1.0 + sc)
    dr = dh * gm
    dx = rstd * (dr - r * jnp.mean(dr * r, axis=-1, keepdims=True))
    return dx, r * gm + sh


def _adam(w, g, m, v):
    m = ADAM_B1 * m + (1.0 - ADAM_B1) * g
    v = ADAM_B2 * v + (1.0 - ADAM_B2) * (g * g)
    m_hat = m / (1.0 - ADAM_B1**ADAM_STEP)
    v_hat = v / (1.0 - ADAM_B2**ADAM_STEP)
    delta = -ADAM_LR * (m_hat / (jnp.sqrt(v_hat) + ADAM_EPS) + ADAM_WD * w)
    return delta, m, v


_ANY = pl.BlockSpec(memory_space=pl.ANY)


class _Phase:
    def __init__(self, ins, out_shapes, aliases, n_sems, start, finish, then):
        self.ins, self.out_shapes, self.aliases, self.n_sems = list(ins), list(out_shapes), dict(aliases), n_sems
        self.start, self.finish, self.then = start, finish, then


def _call(body, name, grid, in_specs, out_specs, out_shape, ins, scratch=(), prefetch=(), phases=(), in_place=None):
    n_pre, n_in, n_out, n_sc = len(prefetch), len(in_specs), len(out_specs), len(scratch)
    ph_in = [len(p.ins) for p in phases]
    ph_out = [len(p.out_shapes) for p in phases]

    def kernel_body(*refs):
        pos = [0]

        def take(k):
            pos[0] += k
            return refs[pos[0] - k : pos[0]]

        pre, ins_ = take(n_pre), take(n_in)
        p_ins = [take(k) for k in ph_in]
        outs_ = take(n_out)
        p_outs = [take(k) for k in ph_out]
        sc = take(n_sc)
        sems = [take(2) for _ in phases]
        if phases:
            ids = [pl.program_id(a) for a in range(len(grid))]
            first = functools.reduce(jnp.logical_and, [i == 0 for i in ids])
            last = functools.reduce(jnp.logical_and, [i == g - 1 for i, g in zip(ids, grid)])

            @pl.when(first)
            def _():
                for p, pi, po, (send, recv) in zip(phases, p_ins, p_outs, sems):
                    p.start(pi, po, send, recv)

        if body is not None:
            body(*pre, *ins_, *outs_, *sc)
        if phases:

            @pl.when(last)
            def _():
                for p, pi, po, (send, recv) in zip(phases, p_ins, p_outs, sems):
                    p.finish(pi, po, send, recv)

    aliases = {n_pre + i: o for i, o in (in_place or {}).items()}
    i0, o0 = n_pre + n_in, n_out
    for p in phases:
        for i, o in p.aliases.items():
            aliases[i0 + i] = o0 + o
        i0 += len(p.ins)
        o0 += len(p.out_shapes)
    all_in = list(in_specs) + [_ANY] * sum(ph_in)
    all_out = list(out_specs) + [_ANY] * sum(ph_out)
    all_scratch = list(scratch)
    for p in phases:
        all_scratch += [pltpu.SemaphoreType.DMA((p.n_sems,)), pltpu.SemaphoreType.DMA((p.n_sems,))]
    shapes = list(out_shape) + [s for p in phases for s in p.out_shapes]
    operands = list(prefetch) + list(ins) + [a for p in phases for a in p.ins]
    sem = ("arbitrary",) * len(grid)
    params = pltpu.CompilerParams(dimension_semantics=sem, vmem_limit_bytes=VMEM_LIMIT_BYTES)
    if n_pre:
        res = pl.pallas_call(
            kernel_body, name=name, out_shape=shapes, input_output_aliases=aliases, compiler_params=params,
            grid_spec=pltpu.PrefetchScalarGridSpec(
                num_scalar_prefetch=n_pre, grid=grid, in_specs=all_in, out_specs=all_out, scratch_shapes=all_scratch
            ),
        )(*operands)
    else:
        res = pl.pallas_call(
            kernel_body, name=name, grid=grid, in_specs=all_in, out_specs=all_out, out_shape=shapes,
            scratch_shapes=all_scratch, input_output_aliases=aliases, compiler_params=params,
        )(*operands)
    res = list(res)
    outs, rest = res[:n_out], res[n_out:]
    p_res = []
    for k in ph_out:
        p_res.append(rest[:k])
        rest = rest[k:]
    return outs, p_res


def _place():
    return lax.axis_index("x"), lax.axis_index("y"), lax.axis_index("c")


def _other_chips():
    x, y, _ = _place()
    return [(1 - x, y), (x, 1 - y), (1 - x, 1 - y)]


def _flip(k):
    x, y, c = _place()
    return (1 - x if k & 4 else x, 1 - y if k & 2 else y, 1 - c if k & 1 else c)


def _remote(src, dst, send, recv, k, to):
    return pltpu.make_async_remote_copy(
        src_ref=src, dst_ref=dst, send_sem=send.at[k], recv_sem=recv.at[k], device_id=to, device_id_type=MESH
    )


def _phase_small_gather(arrs, then):
    n = len(arrs)

    def copies(ins, outs, send, recv):
        x, y, c = _place()
        me = 4 * x + 2 * y + c
        local = [pltpu.make_async_copy(ins[a], outs[a].at[me], send.at[a * N_DEV]) for a in range(n)]
        remote = [_remote(ins[a], outs[a].at[me], send, recv, a * N_DEV + k, _flip(k)) for a in range(n) for k in range(1, N_DEV)]
        return local, remote

    def start(ins, outs, send, recv):
        local, remote = copies(ins, outs, send, recv)
        for cp in local + remote:
            cp.start()

    def finish(ins, outs, send, recv):
        local, remote = copies(ins, outs, send, recv)
        for cp in remote + local:
            cp.wait()

    shapes = [jax.ShapeDtypeStruct((N_DEV,) + a.shape, a.dtype) for a in arrs]
    return _Phase(arrs, shapes, {}, n * N_DEV, start, finish, then)


def _phase_small_exchange(arr, then):
    def copies(ins, outs, send, recv):
        x, y, c = _place()
        me = 4 * x + 2 * y + c
        local = pltpu.make_async_copy(ins[0].at[me], outs[0].at[me], send.at[0])
        remote = []
        for k in range(1, N_DEV):
            px, py, pc = _flip(k)
            remote.append(_remote(ins[0].at[4 * px + 2 * py + pc], outs[0].at[me], send, recv, k, (px, py, pc)))
        return [local] + remote

    def start(ins, outs, send, recv):
        for cp in copies(ins, outs, send, recv):
            cp.start()

    def finish(ins, outs, send, recv):
        for cp in copies(ins, outs, send, recv):
            cp.wait()

    return _Phase([arr], [jax.ShapeDtypeStruct(arr.shape, arr.dtype)], {}, N_DEV, start, finish, then)


def _after(*arrs):
    nothing = lambda *args: None
    return _Phase(arrs, [], {}, 1, nothing, nothing, nothing)


def _flush(name, *phases):
    _, p_outs = _call(None, name, (1,), [], [], [], [], phases=list(phases))
    for p, po in zip(phases, p_outs):
        p.then(po)


class _Big:
    KINDS = {"full": (True, True), "half": (True, False), "shard": (False, True), "block": (False, False)}

    def __init__(self, f3, s3, h3):
        assert s3 != h3
        self.f3, self.s3, self.h3 = tuple(f3), s3, h3
        self.bd = tuple(f3[a] // (N_CHIPS if a == s3 else 1) // (2 if a == h3 else 1) for a in range(3))
        self.tile = (1, _row_tile(self.bd[1], self.bd[2]), self.bd[2])
        self.grid = tuple(self.bd[a] // self.tile[a] for a in range(3))

    def dims(self, kind):
        chips, halves = self.KINDS[kind]
        return tuple(
            self.bd[a] * (N_CHIPS if chips and a == self.s3 else 1) * (2 if halves and a == self.h3 else 1) for a in range(3)
        )

    def view(self, ref, chip=None, half=None, batch0=0, both_halves=True, part=None):
        start = [batch0, 0, 0]
        size = list(ref.shape)
        size[0] = self.bd[0] * (2 if self.h3 == 0 and both_halves else 1)
        if chip is not None:
            start[self.s3] += chip * self.bd[self.s3]
            size[self.s3] = self.bd[self.s3]
        if half is not None:
            start[self.h3] += half * self.bd[self.h3]
            size[self.h3] = self.bd[self.h3]
        if part is not None:
            size[1] //= 2
            start[1] += part * size[1]
        return ref.at[tuple(pl.ds(st, sz) for st, sz in zip(start, size))]

    def spec(self, chip_from=None, half_from=None, lead=(), batch0=0):
        extra = "grid" in (chip_from, half_from)

        def index(*args):
            pref, idx = args[-1], list(args[int(extra) : -1])
            idx[0] += batch0
            if chip_from:
                idx[self.s3] += (pref[0] if chip_from == "pref" else args[0]) * self.grid[self.s3]
            if half_from:
                idx[self.h3] += (pref[1] if half_from == "pref" else args[0]) * self.grid[self.h3]
            return (0,) * len(lead) + tuple(idx)

        return pl.BlockSpec(tuple(lead) + self.tile, index)


def _same(arrs):
    return [jax.ShapeDtypeStruct(a.shape, a.dtype) for a in arrs]


def _phase_gather_relay(arrs, bigs, second, whole_first, then):
    n = len(arrs)
    per = 4 if second and not whole_first else 2

    def copies(outs, send, recv, arriving):
        x, y, c = _place()
        me, xn, yn, dg = (x, y), (1 - x, y), (x, 1 - y), (1 - x, 1 - y)
        if not second:
            part = (None, None) if whole_first else (0, 1)
            plan = [((xn if arriving else me), part[0], xn), ((yn if arriving else me), part[1], yn)]
        elif whole_first:
            plan = [(dg, 0, yn), (dg, 1, xn)] if arriving else [(xn, 0, yn), (yn, 1, xn)]
        elif arriving:
            plan = [(yn, 0, yn), (dg, 0, yn), (xn, 1, xn), (dg, 1, xn)]
        else:
            plan = [(me, 0, yn), (xn, 0, yn), (me, 1, xn), (yn, 1, xn)]
        res = []
        for a in range(n):
            for k, (chip, part, to) in enumerate(plan):
                blk = bigs[a].view(outs[a], 2 * chip[0] + chip[1], c, part=part)
                res.append(_remote(blk, blk, send, recv, per * a + k, (*to, c)))
        return res

    def start(ins, outs, send, recv):
        for cp in copies(outs, send, recv, False):
            cp.start()

    def finish(ins, outs, send, recv):
        for cp in copies(outs, send, recv, True):
            cp.wait_recv()
        for cp in copies(outs, send, recv, False):
            cp.wait_send()

    return _Phase(arrs, _same(arrs), {a: a for a in range(n)}, per * n, start, finish, then)


def _phase_gather_sibling(arrs, bigs, then):
    n = len(arrs)

    def copies(outs, send, recv, arriving):
        x, y, c = _place()
        return [
            _remote(blk, blk, send, recv, 3 * a + j, (x, y, 1 - c))
            for j, chip in enumerate(_other_chips())
            for a in range(n)
            for blk in [bigs[a].view(outs[a], 2 * chip[0] + chip[1], 1 - c if arriving else c)]
        ]

    def start(ins, outs, send, recv):
        for cp in copies(outs, send, recv, False):
            cp.start()

    def finish(ins, outs, send, recv):
        for cp in copies(outs, send, recv, True):
            cp.wait_recv()
        for cp in copies(outs, send, recv, False):
            cp.wait_send()

    return _Phase(arrs, _same(arrs), {a: a for a in range(n)}, 3 * n, start, finish, then)


def _phase_pair_exchange(grads, bigs, then):
    n = len(grads)

    def copies(ins, outs, send, recv):
        x, y, c = _place()
        srcs = [ins[a] if ins[a].shape == outs[a].shape else bigs[a].view(ins[a], None, 1 - c) for a in range(n)]
        return [_remote(srcs[a], outs[a], send, recv, a, (x, y, 1 - c)) for a in range(n)]

    def start(ins, outs, send, recv):
        for cp in copies(ins, outs, send, recv):
            cp.start()

    def finish(ins, outs, send, recv):
        for cp in copies(ins, outs, send, recv):
            cp.wait()

    shapes = [jax.ShapeDtypeStruct(b.dims("half"), BF16) for b in bigs]
    return _Phase(grads, shapes, {}, n, start, finish, then)


def _phase_chip_exchange(sums, bigs, then):
    n = len(sums)

    def copies(ins, outs, send, recv):
        _, _, c = _place()
        return [
            _remote(bigs[a].view(ins[a], 2 * chip[0] + chip[1], both_halves=False), outs[a].at[j], send, recv, 3 * a + j, (*chip, c))
            for j, chip in enumerate(_other_chips())
            for a in range(n)
        ]

    def start(ins, outs, send, recv):
        for cp in copies(ins, outs, send, recv):
            cp.start()

    def finish(ins, outs, send, recv):
        for cp in copies(ins, outs, send, recv):
            cp.wait()

    shapes = [jax.ShapeDtypeStruct((N_CHIPS - 1,) + b.dims("block"), BF16) for b in bigs]
    return _Phase(sums, shapes, {}, 3 * n, start, finish, then)


_HBM = pl.BlockSpec(memory_space=pltpu.HBM)
_SEM = pl.BlockSpec(memory_space=pltpu.SEMAPHORE)
_DATAFLOW = pltpu.SideEffectType.DATAFLOW_SIDE_EFFECTING


class _InFlight:
    def __init__(self, phase, send, recv, arrays, token):
        self.phase, self.send, self.recv, self.arrays, self.token = phase, send, recv, arrays, token


def _phase_results(phase, refs):
    n_in = len(phase.ins)
    updated = {o: i for i, o in phase.aliases.items()}
    fresh = [o for o in range(len(phase.out_shapes)) if o not in updated]
    return [refs[updated[o]] if o in updated else refs[n_in + fresh.index(o)] for o in range(len(phase.out_shapes))]


def _split_start(phase, name):
    n_in = len(phase.ins)
    fresh = [s for o, s in enumerate(phase.out_shapes) if o not in phase.aliases.values()]
    arrays = list(phase.ins) + [lax.empty(s.shape, s.dtype) for s in fresh]
    n = len(arrays)

    def body(*refs):
        phase.start(refs[:n_in], _phase_results(phase, refs[:n]), refs[n], refs[n + 1])
        refs[-1][...] = jnp.zeros_like(refs[-1])

    operands = [pltpu.with_memory_space_constraint(a, pltpu.HBM) for a in arrays]
    res = pl.pallas_call(
        body, name=name,
        out_shape=[pltpu.SemaphoreType.DMA((phase.n_sems,)), pltpu.SemaphoreType.DMA((phase.n_sems,))]
        + [pltpu.HBM(a.shape, a.dtype) for a in arrays] + [jax.ShapeDtypeStruct((8, 128), F32)],
        in_specs=[_HBM] * n, out_specs=[_SEM, _SEM] + [_HBM] * n + [pl.BlockSpec(memory_space=pltpu.VMEM)],
        input_output_aliases={i: 2 + i for i in range(n)},
        compiler_params=pltpu.CompilerParams(has_side_effects=_DATAFLOW),
    )(*operands)
    return _InFlight(phase, res[0], res[1], list(res[2 : 2 + n]), res[-1])


def _split_wait(flight, after, name):
    phase, n = flight.phase, len(flight.arrays)
    n_in = len(phase.ins)

    def body(*refs):
        phase.finish(refs[:n_in], _phase_results(phase, refs[:n]), refs[n], refs[n + 1])

    res = pl.pallas_call(
        body, name=name, out_shape=[pltpu.HBM(a.shape, a.dtype) for a in flight.arrays],
        in_specs=[_HBM] * n + [_SEM, _SEM] + [_ANY] * len(after), out_specs=[_HBM] * n,
        input_output_aliases={i: i for i in range(n)},
        compiler_params=pltpu.CompilerParams(has_side_effects=_DATAFLOW),
    )(*flight.arrays, flight.send, flight.recv, *after)
    res = list(res)
    phase.then(_phase_results(phase, res))
    return res[:n_in]


def _phase_pair_broadcast(stacks, bigs, batch0s, then):
    n = len(stacks)

    def start(ins, outs, send, recv):
        x, y, c = _place()
        for a in range(n):
            blk = bigs[a].view(outs[a], None, c, batch0s[a])
            _remote(blk, blk, send, recv, a, (x, y, 1 - c)).start()

    def finish(ins, outs, send, recv):
        x, y, c = _place()
        for a in range(n):
            mine = bigs[a].view(outs[a], None, c, batch0s[a])
            theirs = bigs[a].view(outs[a], None, 1 - c, batch0s[a])
            _remote(mine, mine, send, recv, a, (x, y, 1 - c)).wait_send()
            _remote(theirs, theirs, send, recv, a, (x, y, 1 - c)).wait_recv()

    return _Phase(stacks, _same(stacks), {a: a for a in range(n)}, n, start, finish, then)


def _tile_call(body, name, big, where, extra, ins, in_specs, out_specs, out_shape, phases=()):
    grid = ((extra,) if extra else ()) + big.grid
    return _call(body, name, grid, in_specs, out_specs, out_shape, ins, prefetch=(where,), phases=phases)


def _cast_into_full(w_stack, batch0, big, where, name, phases=()):
    def body(_, w_ref, o_ref):
        o_ref[...] = w_ref[...].astype(BF16)

    return _tile_call(
        body, name, big, where, 2, [w_stack], [big.spec(None, "grid", batch0=batch0)], [big.spec("pref", "grid")],
        [jax.ShapeDtypeStruct(big.dims("full"), BF16)], phases,
    )


def _pair_sum(g_full, recv_half, big, where, name, phases=()):
    def body(_, g_ref, r_ref, o_ref):
        o_ref[...] = (g_ref[...].astype(F32) + r_ref[...].astype(F32)).astype(BF16)

    half = big.spec("grid", None)
    return _tile_call(
        body, name, big, where, N_CHIPS, [g_full, recv_half], [big.spec("grid", "pref"), half], [half],
        [jax.ShapeDtypeStruct(big.dims("half"), BF16)], phases,
    )


def _chip_sum(chip_sum, parts, big, where, stack, stack_shape, batch0, name, phases=()):
    def body(_, own_ref, p_ref, *rest):
        acc = own_ref[...].astype(F32)
        for k in range(N_CHIPS - 1):
            acc = acc + p_ref[k].astype(F32)
        rest[-1][...] = acc

    ins = [chip_sum, parts] + ([stack] if stack is not None else [])
    in_specs = [big.spec("pref", None), big.spec(None, None, lead=(N_CHIPS - 1,))] + ([_ANY] if stack is not None else [])
    return _call(
        body, name, big.grid, in_specs, [big.spec(None, "pref", batch0=batch0)], [jax.ShapeDtypeStruct(stack_shape, F32)], ins,
        prefetch=(where,), phases=phases, in_place={2: 0} if stack is not None else None,
    )


def _adam_stack(w, g, m, v, name, after=()):
    b, r, c = w.shape
    tr = _row_tile(r, c, ADAM_BLOCK_ELEMS)

    def body(w_ref, g_ref, m_ref, v_ref, *rest):
        go_ref, d_ref, mo_ref, vo_ref = rest[-4:]
        gv = g_ref[...]
        d, mo, vo = _adam(w_ref[...], gv, m_ref[...], v_ref[...])
        go_ref[...] = gv
        d_ref[...] = d
        mo_ref[...] = mo
        vo_ref[...] = vo

    spec = pl.BlockSpec((1, tr, c), lambda bb, i: (bb, i, 0))
    outs, _ = _call(
        body, name, (b, r // tr), [spec] * 4 + [_ANY] * len(after), [spec] * 4, [jax.ShapeDtypeStruct(w.shape, F32)] * 4,
        [w, g, m, v, *after],
    )
    return outs


def _mod_fwd(c_all, w_mod, b_cols, phases=()):
    n_layers, d, n = w_mod.shape
    tn = _pick(n, (768, 512, 384, 256, 128))

    def body(c_ref, w_ref, b_ref, o_ref):
        cv = c_ref[...]
        ca = (cv * _sigmoid(cv)).astype(BF16)
        o_ref[0] = _dot(ca, w_ref[0].astype(BF16)) + b_ref[0]

    return _call(
        body, "mod_fwd", (n_layers, n // tn),
        [
            pl.BlockSpec((N_DEV, d), lambda l, j: (0, 0)),
            pl.BlockSpec((1, d, tn), lambda l, j: (l, 0, j)),
            pl.BlockSpec((1, 1, tn), lambda l, j: (l, 0, j)),
        ],
        [pl.BlockSpec((1, N_DEV, tn), lambda l, j: (l, 0, j))],
        [jax.ShapeDtypeStruct((n_layers, N_DEV, n), F32)], [c_all, w_mod, b_cols], phases=phases,
    )


def _mod_bwd_adam(c_all_t, dmod_cols, w, m, v, after=()):
    n_layers, d, n = w.shape
    tn = _pick(n, (384, 256, 128))

    def body(c_ref, dm_ref, w_ref, m_ref, v_ref, *rest):
        g_ref, d_ref, mo_ref, vo_ref = rest[-4:]
        cv = c_ref[...]
        ca = (cv * _sigmoid(cv)).astype(BF16)
        g = _dot(ca, dm_ref[0].astype(BF16))
        g_ref[0] = g
        dl, mo, vo = _adam(w_ref[0], g, m_ref[0], v_ref[0])
        d_ref[0] = dl
        mo_ref[0] = mo
        vo_ref[0] = vo

    wspec = pl.BlockSpec((1, d, tn), lambda l, j: (l, 0, j))
    outs, _ = _call(
        body, "mod_bwd_adam", (n_layers, n // tn),
        [pl.BlockSpec((d, N_DEV), lambda l, j: (0, 0)), pl.BlockSpec((1, N_DEV, tn), lambda l, j: (l, 0, j)), wspec, wspec, wspec]
        + [_ANY] * len(after),
        [wspec] * 4, [jax.ShapeDtypeStruct(w.shape, F32)] * 4, [c_all_t, dmod_cols, w, m, v, *after],
    )
    return outs


def _ffn_fwd(x, vec, w_in, w_out, name, phases=()):
    s, d = x.shape
    f = w_out.shape[1]
    tm = _pick(s, (1024, 512, 256, 128))
    tf = _pick(f, (256, 128))
    nf = f // tf

    def body(x_ref, vec_ref, wg_ref, wu_ref, wo_ref, xo_ref, g_ref, u_ref, y_ref, h_sc, acc_sc):
        j = pl.program_id(1)

        @pl.when(j == 0)
        def _():
            h_sc[...] = _modulate(x_ref[...], vec_ref).astype(BF16)
            acc_sc[...] = jnp.zeros_like(acc_sc)

        h = h_sc[...]
        g = _dot(h, wg_ref[0])
        u = _dot(h, wu_ref[0])
        g_ref[...] = g.astype(BF16)
        u_ref[...] = u.astype(BF16)
        a = (g * _sigmoid(g) * u).astype(BF16)
        acc_sc[...] += _dot(a, wo_ref[0])

        @pl.when(j == nf - 1)
        def _():
            yv = acc_sc[...]
            xo_ref[...] = x_ref[...] + 0.5 * vec_ref[3:4, :] * yv
            y_ref[...] = yv.astype(BF16)

    row = pl.BlockSpec((tm, d), lambda i, j: (i, 0))
    hid = pl.BlockSpec((tm, tf), lambda i, j: (i, j))
    return _call(
        body, name, (s // tm, nf),
        [
            row,
            pl.BlockSpec((8, d), lambda i, j: (0, 0)),
            pl.BlockSpec((1, d, tf), lambda i, j: (0, 0, j)),
            pl.BlockSpec((1, d, tf), lambda i, j: (0, 0, nf + j)),
            pl.BlockSpec((1, tf, d), lambda i, j: (0, j, 0)),
        ],
        [row, hid, hid, row],
        [
            jax.ShapeDtypeStruct((s, d), F32),
            jax.ShapeDtypeStruct((s, f), BF16),
            jax.ShapeDtypeStruct((s, f), BF16),
            jax.ShapeDtypeStruct((s, d), BF16),
        ],
        [x, vec, w_in, w_in, w_out],
        scratch=[pltpu.VMEM((tm, d), BF16), pltpu.VMEM((tm, d), F32)], phases=phases,
    )


def _ffn_bwd(dxo, x, vec, gg, uu, y, w_in, w_out, name, phases=()):
    s, d = x.shape
    f = w_out.shape[1]
    tm = _pick(s, (512, 256, 128))
    tf = _pick(f, (256, 128))
    nf = f // tf

    def body(dxo_ref, x_ref, vec_ref, g_ref, u_ref, y_ref, wg_ref, wu_ref, wo_ref,
             dx_ref, dg_ref, du_ref, a_ref, h_ref, dy_ref, dvec_ref, acc_sc):
        i, j = pl.program_id(0), pl.program_id(1)

        @pl.when((i == 0) & (j == 0))
        def _():
            dvec_ref[...] = jnp.zeros_like(dvec_ref)

        @pl.when(j == 0)
        def _():
            dxo_v = dxo_ref[...]
            dy_ref[...] = (0.5 * vec_ref[3:4, :] * dxo_v).astype(BF16)
            dvec_ref[3:4, :] += 0.5 * jnp.sum(dxo_v * y_ref[...].astype(F32), axis=0, keepdims=True)
            acc_sc[...] = jnp.zeros_like(acc_sc)

        da = _dot_nt(dy_ref[...], wo_ref[0])
        g = g_ref[...].astype(F32)
        u = u_ref[...].astype(F32)
        sig = _sigmoid(g)
        sl = g * sig
        a_ref[...] = (sl * u).astype(BF16)
        dg = (da * u * (sig * (1.0 + g * (1.0 - sig)))).astype(BF16)
        du = (da * sl).astype(BF16)
        dg_ref[...] = dg
        du_ref[...] = du
        acc_sc[...] += _dot_nt(dg, wg_ref[0]) + _dot_nt(du, wu_ref[0])

        @pl.when(j == nf - 1)
        def _():
            dx, h = _modulate_bwd(x_ref[...], acc_sc[...], vec_ref, dvec_ref)
            dx_ref[...] = dxo_ref[...] + dx
            h_ref[...] = h.astype(BF16)

    row = pl.BlockSpec((tm, d), lambda i, j: (i, 0))
    hid = pl.BlockSpec((tm, tf), lambda i, j: (i, j))
    vecs = pl.BlockSpec((8, d), lambda i, j: (0, 0))
    return _call(
        body, name, (s // tm, nf),
        [
            row, row, vecs, hid, hid, row,
            pl.BlockSpec((1, d, tf), lambda i, j: (0, 0, j)),
            pl.BlockSpec((1, d, tf), lambda i, j: (0, 0, nf + j)),
            pl.BlockSpec((1, tf, d), lambda i, j: (0, j, 0)),
        ],
        [row, hid, hid, hid, row, row, vecs],
        [
            jax.ShapeDtypeStruct((s, d), F32),
            jax.ShapeDtypeStruct((s, f), BF16),
            jax.ShapeDtypeStruct((s, f), BF16),
            jax.ShapeDtypeStruct((s, f), BF16),
            jax.ShapeDtypeStruct((s, d), BF16),
            jax.ShapeDtypeStruct((s, d), BF16),
            jax.ShapeDtypeStruct((8, d), F32),
        ],
        [dxo, x, vec, gg, uu, y, w_in, w_in, w_out],
        scratch=[pltpu.VMEM((tm, d), F32)], phases=phases,
    )


def _grad_half(a, bs, big, where, mine, recv, name, phases=()):
    s, k1 = a.shape
    n = bs[0].shape[1]
    groups = len(bs)
    rows_halved = big.h3 == 1
    assert rows_halved or groups == 1
    kk, nn = (k1 // 2, n) if rows_halved else (k1, n // 2)
    tk = _pick(kk, (1408, 1024, 512, 256, 128))
    tn = _pick(nn, (1408, 1024, 640, 512, 256, 128))
    nkb, nnb = kk // tk, nn // tn
    assert (recv is None) == (not mine)

    def half(pref):
        return pref[1] if mine else 1 - pref[1]

    def body(_, a_ref, *rest):
        q = pl.program_id(1)
        for p in range(groups):

            @pl.when(q == p)
            def _(p=p):
                acc = _dot_tn(a_ref[...], rest[p][...])
                if recv is not None:
                    acc = acc + rest[groups][0].astype(F32)
                rest[-1][0] = acc.astype(BF16)

    def b_block(p):
        def index(i, q, j, pref):
            jj = jnp.where(q == p, j, jnp.where(q < p, 0, nnb - 1))
            return (0, jj + (0 if rows_halved else half(pref) * nnb))

        return pl.BlockSpec((s, tn), index)

    out_spec = pl.BlockSpec((1, tk, tn), lambda i, q, j, pref: (0, i, q * nnb + j))
    in_specs = [pl.BlockSpec((s, tk), lambda i, q, j, pref: (0, i + (half(pref) * nkb if rows_halved else 0)))]
    in_specs += [b_block(p) for p in range(groups)]
    ins = [a, *bs]
    if recv is not None:
        in_specs.append(out_spec)
        ins.append(recv)
    return _call(
        body, name, (nkb, groups, nnb), in_specs, [out_spec], [jax.ShapeDtypeStruct(big.dims("half"), BF16)], ins,
        prefetch=(where,), phases=phases,
    )


def _proj_mod_fwd(x, vec, w, phases=()):
    s, d = x.shape
    n = w.shape[2]
    tm = _pick(s, (1024, 512, 256, 128))
    tn = _pick(n, (640, 512, 256, 128))

    def body(x_ref, vec_ref, w_ref, o_ref, h_sc):
        @pl.when(pl.program_id(1) == 0)
        def _():
            h_sc[...] = _modulate(x_ref[...], vec_ref).astype(BF16)

        o_ref[...] = _dot(h_sc[...], w_ref[0])

    return _call(
        body, "ab_in_fwd", (s // tm, n // tn),
        [
            pl.BlockSpec((tm, d), lambda i, j: (i, 0)),
            pl.BlockSpec((8, d), lambda i, j: (0, 0)),
            pl.BlockSpec((1, d, tn), lambda i, j: (0, 0, j)),
        ],
        [pl.BlockSpec((tm, tn), lambda i, j: (i, j))],
        [jax.ShapeDtypeStruct((s, n), F32)], [x, vec, w],
        scratch=[pltpu.VMEM((tm, d), BF16)], phases=phases,
    )


def _proj_res_fwd(a, w, x, vec, phases=()):
    s, kd = a.shape
    d = x.shape[1]
    tm = _pick(s, (1024, 512, 256, 128))

    def body(a_ref, w_ref, x_ref, vec_ref, xo_ref, y_ref):
        yv = _dot(a_ref[...], w_ref[0])
        xo_ref[...] = x_ref[...] + vec_ref[3:4, :] * yv
        y_ref[...] = yv.astype(BF16)

    row = pl.BlockSpec((tm, d), lambda i: (i, 0))
    return _call(
        body, "ab_out_fwd", (s // tm,),
        [pl.BlockSpec((tm, kd), lambda i: (i, 0)), pl.BlockSpec((1, kd, d), lambda i: (0, 0, 0)), row, pl.BlockSpec((8, d), lambda i: (0, 0))],
        [row, row],
        [jax.ShapeDtypeStruct((s, d), F32), jax.ShapeDtypeStruct((s, d), BF16)], [a, w, x, vec], phases=phases,
    )


def _proj_res_bwd(dxo, y, vec, w, phases=()):
    s, d = dxo.shape
    kd = w.shape[1]
    tm = _pick(s, (1024, 512, 256, 128))

    def body(dxo_ref, y_ref, vec_ref, w_ref, dy_ref, da_ref, dgate_ref):
        @pl.when(pl.program_id(0) == 0)
        def _():
            dgate_ref[...] = jnp.zeros_like(dgate_ref)

        dxo_v = dxo_ref[...]
        dy = (vec_ref[3:4, :] * dxo_v).astype(BF16)
        dy_ref[...] = dy
        dgate_ref[3:4, :] += jnp.sum(dxo_v * y_ref[...].astype(F32), axis=0, keepdims=True)
        da_ref[...] = _dot_nt(dy, w_ref[0]).astype(BF16)

    row = pl.BlockSpec((tm, d), lambda i: (i, 0))
    vecs = pl.BlockSpec((8, d), lambda i: (0, 0))
    return _call(
        body, "ab_out_bwd", (s // tm,),
        [row, row, vecs, pl.BlockSpec((1, kd, d), lambda i: (0, 0, 0))],
        [row, pl.BlockSpec((tm, kd), lambda i: (i, 0)), vecs],
        [jax.ShapeDtypeStruct((s, d), BF16), jax.ShapeDtypeStruct((s, kd), BF16), jax.ShapeDtypeStruct((8, d), F32)],
        [dxo, y, vec, w], phases=phases,
    )


def _proj_mod_bwd(dproj, w, x, vec, dxo, dvec_in, name, phases=()):
    parts, s, n_part = dproj.shape
    d = x.shape[1]
    tm = _pick(s, (512, 256, 128))
    tk = _pick(n_part, (1408, 1280, 1024, 512, 256, 128))
    per_part = n_part // tk
    nk = parts * per_part

    def body(dp_ref, w_ref, x_ref, vec_ref, dxo_ref, dvi_ref, dx_ref, h_ref, dvec_ref, acc_sc):
        i, k = pl.program_id(0), pl.program_id(1)

        @pl.when((i == 0) & (k == 0))
        def _():
            dvec_ref[...] = dvi_ref[...]

        @pl.when(k == 0)
        def _():
            acc_sc[...] = jnp.zeros_like(acc_sc)

        acc_sc[...] += _dot_nt(dp_ref[0], w_ref[0])

        @pl.when(k == nk - 1)
        def _():
            dx, h = _modulate_bwd(x_ref[...], acc_sc[...], vec_ref, dvec_ref)
            dx_ref[...] = dxo_ref[...] + dx
            h_ref[...] = h.astype(BF16)

    row = pl.BlockSpec((tm, d), lambda i, k: (i, 0))
    vecs = pl.BlockSpec((8, d), lambda i, k: (0, 0))
    return _call(
        body, name, (s // tm, nk),
        [
            pl.BlockSpec((1, tm, tk), lambda i, k: (k // per_part, i, k % per_part)),
            pl.BlockSpec((1, d, tk), lambda i, k: (0, 0, k)),
            row, vecs, row, vecs,
        ],
        [row, row, vecs],
        [jax.ShapeDtypeStruct((s, d), F32), jax.ShapeDtypeStruct((s, d), BF16), jax.ShapeDtypeStruct((8, d), F32)],
        [dproj, w, x, vec, dxo, dvec_in], scratch=[pltpu.VMEM((tm, d), F32)], phases=phases,
    )


def _tril(n):
    return lax.broadcasted_iota(jnp.int32, (n, n), 0) >= lax.broadcasted_iota(jnp.int32, (n, n), 1)


def _layernorm_stats(gv):
    mu = jnp.mean(gv, axis=-1, keepdims=True)
    cen = gv - mu
    rstd = lax.rsqrt(jnp.mean(cen * cen, axis=-1, keepdims=True) + EPS)
    return cen * rstd, rstd


def _shift_down(q, k, above_ref, c_cg, c_xb, first):
    width = q.shape[1]
    rows = lax.broadcasted_iota(jnp.int32, q.shape, 0)
    out = pltpu.roll(q, k, 0)
    for r in range(k):
        src = CONV_HALO - k + r
        above = above_ref[src : src + 1, c_cg : c_cg + width] * above_ref[src : src + 1, c_xb : c_xb + width]
        above = jnp.where(first, 0.0, above)
        out = jnp.where(rows == r, above, out)
    return out


def _ab_mix_fwd(proj, norm_v, w_s, b_rows, conv_w, phases=()):
    s, n = proj.shape
    heads, chunk, _ = w_s.shape
    da = norm_v.shape[1]
    hd = da // heads
    db = conv_w.shape[1]
    tm = _pick(s, (512, 256, 128))

    def body(p_ref, ph_ref, nv_ref, ws_ref, b_ref, cw_ref, o_ref):
        first = pl.program_id(0) == 0
        gu, _ = _gelu(p_ref[:, 0:da])
        gv, _ = _gelu(p_ref[:, da : 2 * da])
        xhat, _ = _layernorm_stats(gv)
        vn = (xhat * nv_ref[...]).astype(BF16)
        mask = _tril(chunk)
        for hh in range(heads):
            wm = jnp.where(mask, ws_ref[hh], 0.0).astype(BF16)
            cols = slice(hh * hd, (hh + 1) * hd)
            for nn in range(tm // chunk):
                rows = slice(nn * chunk, (nn + 1) * chunk)
                z = _dot(wm, vn[rows, cols]) + b_ref[:, cols]
                o_ref[rows, cols] = (gu[rows, cols] * z).astype(BF16)
        c_cg, c_xb = 2 * da + db, 2 * da + 2 * db
        bg = p_ref[:, 2 * da : 2 * da + db]
        q = p_ref[:, c_cg : c_cg + db] * p_ref[:, c_xb : c_xb + db]
        q1 = _shift_down(q, 1, ph_ref, c_cg, c_xb, first)
        q2 = _shift_down(q, 2, ph_ref, c_cg, c_xb, first)
        conv = cw_ref[0:1, :] * q2 + cw_ref[1:2, :] * q1 + cw_ref[2:3, :] * q
        o_ref[:, da : da + db] = (bg * conv).astype(BF16)

    nh = tm // CONV_HALO
    return _call(
        body, "ab_mix_fwd", (s // tm,),
        [
            pl.BlockSpec((tm, n), lambda i: (i, 0)),
            pl.BlockSpec((CONV_HALO, n), lambda i: (jnp.maximum(i * nh - 1, 0), 0)),
            pl.BlockSpec((1, da), lambda i: (0, 0)),
            pl.BlockSpec((heads, chunk, chunk), lambda i: (0, 0, 0)),
            pl.BlockSpec((chunk, da), lambda i: (0, 0)),
            pl.BlockSpec((3, db), lambda i: (0, 0)),
        ],
        [pl.BlockSpec((tm, da + db), lambda i: (i, 0))],
        [jax.ShapeDtypeStruct((s, da + db), BF16)], [proj, proj, norm_v, w_s, b_rows, conv_w], phases=phases,
    )


def _ab_mix_bwd(proj, dcat, norm_v, w_s, b_rows, conv_w, phases=()):
    s, n = proj.shape
    heads, chunk, _ = w_s.shape
    da = norm_v.shape[1]
    hd = da // heads
    db = conv_w.shape[1]
    tm = _pick(s, (512, 256, 128))
    nblk = s // tm
    dhalo = 2 * CONV_HALO

    def body(p_ref, pa_ref, pb_ref, dc_ref, dcb_ref, nv_ref, ws_ref, b_ref, cw_ref,
             dp_ref, dnv_ref, dws_ref, dzs_ref, dcw_ref, dvn_sc):
        i = pl.program_id(0)
        first, last = i == 0, i == nblk - 1

        @pl.when(first)
        def _():
            dnv_ref[...] = jnp.zeros_like(dnv_ref)
            dws_ref[...] = jnp.zeros_like(dws_ref)
            dzs_ref[...] = jnp.zeros_like(dzs_ref)
            dcw_ref[...] = jnp.zeros_like(dcw_ref)

        uu = p_ref[:, 0:da]
        gu, gu_grad = _gelu(uu)
        gv, gv_grad = _gelu(p_ref[:, da : 2 * da])
        xhat, rstd = _layernorm_stats(gv)
        nv = nv_ref[...]
        vn = (xhat * nv).astype(BF16)
        dya = dc_ref[:, 0:da].astype(F32)
        dz = (dya * gu).astype(BF16)
        mask = _tril(chunk)
        for hh in range(heads):
            wm = jnp.where(mask, ws_ref[hh], 0.0).astype(BF16)
            cols = slice(hh * hd, (hh + 1) * hd)
            dws = jnp.zeros((chunk, chunk), F32)
            for nn in range(tm // chunk):
                rows = slice(nn * chunk, (nn + 1) * chunk)
                z = _dot(wm, vn[rows, cols]) + b_ref[:, cols]
                dp_ref[rows, cols] = (dya[rows, cols] * z * gu_grad[rows, cols]).astype(BF16)
                dz_blk = dz[rows, cols]
                dws = dws + _dot_nt(dz_blk, vn[rows, cols])
                dzs_ref[:, cols] += dz_blk.astype(F32)
                dvn = _dot_tn(wm, dz_blk)
                dnv_ref[:, cols] += jnp.sum(dvn * xhat[rows, cols], axis=0, keepdims=True)
                dvn_sc[rows, cols] = dvn
            dws_ref[hh] += jnp.where(mask, dws, 0.0)
        dxhat = dvn_sc[...] * nv
        dgv = rstd * (dxhat - jnp.mean(dxhat, axis=-1, keepdims=True) - xhat * jnp.mean(dxhat * xhat, axis=-1, keepdims=True))
        dp_ref[:, da : 2 * da] = (dgv * gv_grad).astype(BF16)

        c_bg, c_cg, c_xb = 2 * da, 2 * da + db, 2 * da + 2 * db
        bg = p_ref[:, c_bg : c_bg + db]
        cg = p_ref[:, c_cg : c_cg + db]
        xb = p_ref[:, c_xb : c_xb + db]
        q = cg * xb
        q1 = _shift_down(q, 1, pa_ref, c_cg, c_xb, first)
        q2 = _shift_down(q, 2, pa_ref, c_cg, c_xb, first)
        dyb = dc_ref[:, da : da + db].astype(F32)
        conv = cw_ref[0:1, :] * q2 + cw_ref[1:2, :] * q1 + cw_ref[2:3, :] * q
        dp_ref[:, c_bg : c_bg + db] = (dyb * conv).astype(BF16)
        e = dyb * bg
        dcw_ref[0:1, :] += jnp.sum(e * q2, axis=0, keepdims=True)
        dcw_ref[1:2, :] += jnp.sum(e * q1, axis=0, keepdims=True)
        dcw_ref[2:3, :] += jnp.sum(e * q, axis=0, keepdims=True)
        rows = lax.broadcasted_iota(jnp.int32, e.shape, 0)
        dq = cw_ref[2:3, :] * e
        for kk in (1, 2):
            ek = pltpu.roll(e, tm - kk, 0)
            for r in range(kk):
                below = dcb_ref[r : r + 1, da : da + db].astype(F32) * pb_ref[r : r + 1, c_bg : c_bg + db]
                below = jnp.where(last, 0.0, below)
                ek = jnp.where(rows == tm - kk + r, below, ek)
            dq = dq + cw_ref[2 - kk : 3 - kk, :] * ek
        dp_ref[:, c_cg : c_cg + db] = (dq * xb).astype(BF16)
        dp_ref[:, c_xb : c_xb + db] = (dq * cg).astype(BF16)

    nh = tm // CONV_HALO
    nhb = tm // dhalo
    const2 = lambda i: (0, 0)
    return _call(
        body, "ab_mix_bwd", (nblk,),
        [
            pl.BlockSpec((tm, n), lambda i: (i, 0)),
            pl.BlockSpec((CONV_HALO, n), lambda i: (jnp.maximum(i * nh - 1, 0), 0)),
            pl.BlockSpec((CONV_HALO, n), lambda i: (jnp.minimum((i + 1) * nh, s // CONV_HALO - 1), 0)),
            pl.BlockSpec((tm, da + db), lambda i: (i, 0)),
            pl.BlockSpec((dhalo, da + db), lambda i: (jnp.minimum((i + 1) * nhb, s // dhalo - 1), 0)),
            pl.BlockSpec((1, da), const2),
            pl.BlockSpec((heads, chunk, chunk), lambda i: (0, 0, 0)),
            pl.BlockSpec((chunk, da), const2),
            pl.BlockSpec((3, db), const2),
        ],
        [
            pl.BlockSpec((tm, n), lambda i: (i, 0)),
            pl.BlockSpec((1, da), const2),
            pl.BlockSpec((heads, chunk, chunk), lambda i: (0, 0, 0)),
            pl.BlockSpec((chunk, da), const2),
            pl.BlockSpec((3, db), const2),
        ],
        [
            jax.ShapeDtypeStruct((s, n), BF16),
            jax.ShapeDtypeStruct((1, da), F32),
            jax.ShapeDtypeStruct((heads, chunk, chunk), F32),
            jax.ShapeDtypeStruct((chunk, da), F32),
            jax.ShapeDtypeStruct((3, db), F32),
        ],
        [proj, proj, proj, dcat, dcat, norm_v, w_s, b_rows, conv_w],
        scratch=[pltpu.VMEM((tm, da), F32)], phases=phases,
    )


def _pool_counts(tm, i, w):
    t = i * tm + lax.broadcasted_iota(jnp.int32, (tm, 1), 0)
    return jnp.minimum(t + 1, w).astype(F32)


def _pool_fwd(x, vec, w_grp, scale, phases=()):
    s, d = x.shape
    groups, gd, _ = w_grp.shape
    tm = _pick(s, (512, 256, 128))

    def body(x_ref, xa_ref, vec_ref, w_ref, sc_ref, xo_ref, p_ref, o_ref):
        i = pl.program_id(0)
        h = _modulate(x_ref[...], vec_ref)
        ha = jnp.where(i == 0, 0.0, _modulate(xa_ref[...], vec_ref))
        ext = jnp.concatenate([ha, h], axis=0)
        for gi, w in enumerate(POOL_WINDOWS):
            cols = slice(gi * gd, (gi + 1) * gd)
            acc = ext[:, cols]
            step = 1
            while step < w:
                acc = acc + pltpu.roll(acc, step, 0)
                step *= 2
            p = (acc[POOL_HALO:, :] / _pool_counts(tm, i, w) - h[:, cols]).astype(BF16)
            p_ref[:, cols] = p
            o_ref[:, cols] = _dot(p, w_ref[gi]).astype(BF16)
        xo_ref[...] = x_ref[...] + vec_ref[3:4, :] * (o_ref[...].astype(F32) * sc_ref[...])

    nh = tm // POOL_HALO
    row = pl.BlockSpec((tm, d), lambda i: (i, 0))
    return _call(
        body, "pool_fwd", (s // tm,),
        [
            row,
            pl.BlockSpec((POOL_HALO, d), lambda i: (jnp.maximum(i * nh - 1, 0), 0)),
            pl.BlockSpec((8, d), lambda i: (0, 0)),
            pl.BlockSpec((groups, gd, gd), lambda i: (0, 0, 0)),
            pl.BlockSpec((1, d), lambda i: (0, 0)),
        ],
        [row, row, row],
        [jax.ShapeDtypeStruct((s, d), F32), jax.ShapeDtypeStruct((s, d), BF16), jax.ShapeDtypeStruct((s, d), BF16)],
        [x, x, vec, w_grp, scale], phases=phases,
    )


def _pool_bwd(dxo, x, vec, p, o, w_grp, scale, phases=()):
    s, d = x.shape
    groups, gd, _ = w_grp.shape
    tm = _pick(s, (512, 256, 128))
    nblk = s // tm

    def body(dxo_ref, dxb_ref, x_ref, vec_ref, p_ref, o_ref, w_ref, sc_ref, dx_ref, dw_ref, dsc_ref, dvec_ref, dw_sc):
        i = pl.program_id(0)

        @pl.when(i == 0)
        def _():
            dw_sc[...] = jnp.zeros_like(dw_sc)
            dsc_ref[...] = jnp.zeros_like(dsc_ref)
            dvec_ref[...] = jnp.zeros_like(dvec_ref)

        gate, sc = vec_ref[3:4, :], sc_ref[...]
        dxo_v = dxo_ref[...]
        ov = o_ref[...].astype(F32)
        dvec_ref[3:4, :] += jnp.sum(dxo_v * (ov * sc), axis=0, keepdims=True)
        dy = gate * dxo_v
        dsc_ref[...] += jnp.sum(dy * ov, axis=0, keepdims=True)
        dout = (dy * sc).astype(BF16)
        dout_b = jnp.where(i == nblk - 1, 0.0, gate * dxb_ref[...] * sc).astype(BF16)
        for gi, w in enumerate(POOL_WINDOWS):
            cols = slice(gi * gd, (gi + 1) * gd)
            dw_sc[gi] += _dot_tn(p_ref[:, cols], dout[:, cols])
            wb = w_ref[gi]
            dp = _dot_nt(dout[:, cols], wb)
            dp_b = _dot_nt(dout_b[:, cols], wb)
            e = dp / _pool_counts(tm, i, w)
            t_below = (i + 1) * tm + lax.broadcasted_iota(jnp.int32, (POOL_HALO, 1), 0)
            e_b = dp_b / jnp.minimum(t_below + 1, w).astype(F32)
            acc = jnp.concatenate([e, e_b], axis=0)
            step = 1
            while step < w:
                acc = acc + pltpu.roll(acc, tm + POOL_HALO - step, 0)
                step *= 2
            dx_ref[:, cols] = acc[:tm, :] - dp
        dx, _ = _modulate_bwd(x_ref[...], dx_ref[...], vec_ref, dvec_ref)
        dx_ref[...] = dxo_v + dx

        @pl.when(i == nblk - 1)
        def _():
            dw_ref[...] = dw_sc[...].astype(BF16)

    nh = tm // POOL_HALO
    row = pl.BlockSpec((tm, d), lambda i: (i, 0))
    vecs = pl.BlockSpec((8, d), lambda i: (0, 0))
    wspec = pl.BlockSpec((groups, gd, gd), lambda i: (0, 0, 0))
    return _call(
        body, "pool_bwd", (nblk,),
        [
            row,
            pl.BlockSpec((POOL_HALO, d), lambda i: (jnp.minimum((i + 1) * nh, s // POOL_HALO - 1), 0)),
            row, vecs, row, row, wspec,
            pl.BlockSpec((1, d), lambda i: (0, 0)),
        ],
        [row, wspec, pl.BlockSpec((1, d), lambda i: (0, 0)), vecs],
        [
            jax.ShapeDtypeStruct((s, d), F32),
            jax.ShapeDtypeStruct((groups, gd, gd), BF16),
            jax.ShapeDtypeStruct((1, d), F32),
            jax.ShapeDtypeStruct((8, d), F32),
        ],
        [dxo, dxo, x, vec, p, o, w_grp, scale],
        scratch=[pltpu.VMEM((groups, gd, gd), F32)], phases=phases,
    )


def _loss_head(x, gain, target, phases=()):
    s, d = x.shape
    tm = _pick(s, (512, 256, 128))

    def body(x_ref, g_ref, t_ref, dx_ref, aux_ref):
        @pl.when(pl.program_id(0) == 0)
        def _():
            aux_ref[...] = jnp.zeros_like(aux_ref)

        xv = x_ref[...]
        rstd = _rstd(xv)
        r = xv * rstd
        gain_v = g_ref[...]
        err = r * gain_v - t_ref[...]
        aux_ref[1:2, :] += jnp.sum(err * err, axis=0, keepdims=True)
        dout = err * (1.0 / d)
        aux_ref[0:1, :] += jnp.sum(dout * r, axis=0, keepdims=True)
        dr = dout * gain_v
        dx_ref[...] = rstd * (dr - r * jnp.mean(dr * r, axis=-1, keepdims=True))

    row = pl.BlockSpec((tm, d), lambda i: (i, 0))
    return _call(
        body, "loss_head", (s // tm,),
        [row, pl.BlockSpec((1, d), lambda i: (0, 0)), row],
        [row, pl.BlockSpec((8, d), lambda i: (0, 0))],
        [jax.ShapeDtypeStruct((s, d), F32), jax.ShapeDtypeStruct((8, d), F32)], [x, gain, target], phases=phases,
    )


def _small_adam(gathered, gathered_ws, layout, smalls, chip):
    names = list(smalls)
    n = len(names)
    loss_row, _, _, n_feat = layout["loss"]

    def body(*refs):
        chip_ref, g_ref, gws_ref = refs[0], refs[1], refs[2]
        wmv = refs[3 : 3 + 3 * n]
        outs = refs[3 + 3 * n : 3 + 7 * n]
        total = refs[-1]
        total[...] = g_ref[0]
        for kdev in range(1, N_DEV):
            total[...] += g_ref[kdev]
        total_ws = gws_ref[0]
        for kdev in range(1, N_DEV):
            total_ws = total_ws + gws_ref[kdev]
        my_chip = chip_ref[0]
        for a, name in enumerate(names):
            w_ref, m_ref, v_ref = wmv[3 * a : 3 * a + 3]
            if name == "ab_w_s":
                g = total_ws
            else:
                row0, rows, col0, cols = layout[name]
                if col0 is None:
                    g = jnp.zeros((rows, cols), F32)
                    for j in range(N_CHIPS):
                        g = g + jnp.where(my_chip == j, total[row0 : row0 + rows, j * cols : (j + 1) * cols], 0.0)
                else:
                    g = total[row0 : row0 + rows, col0 : col0 + cols]
            dl, mo, vo = _adam(w_ref[...], g, m_ref[...], v_ref[...])
            outs[4 * a][...] = g
            outs[4 * a + 1][...] = dl
            outs[4 * a + 2][...] = mo
            outs[4 * a + 3][...] = vo
        refs[3 + 7 * n][...] = 0.5 * jnp.sum(total[loss_row : loss_row + 1, 0:n_feat], axis=1, keepdims=True) / n_feat

    ins = [gathered, gathered_ws]
    out_shapes = []
    for name in names:
        ins.extend(smalls[name])
        out_shapes.extend([jax.ShapeDtypeStruct(smalls[name][0].shape, F32)] * 4)
    out_shapes.append(jax.ShapeDtypeStruct((1, 1), F32))
    whole = lambda shape: pl.BlockSpec(shape, functools.partial(lambda nd, i, c: (0,) * nd, len(shape)))
    res = pl.pallas_call(
        body, name="small_adam",
        grid_spec=pltpu.PrefetchScalarGridSpec(
            num_scalar_prefetch=1, grid=(1,),
            in_specs=[whole(a.shape) for a in ins], out_specs=[whole(o.shape) for o in out_shapes],
            scratch_shapes=[pltpu.VMEM(gathered.shape[1:], F32)],
        ),
        out_shape=out_shapes,
        compiler_params=pltpu.CompilerParams(dimension_semantics=("arbitrary",), vmem_limit_bytes=VMEM_LIMIT_BYTES),
    )(chip.reshape(1).astype(jnp.int32), *ins)
    return {name: res[4 * a : 4 * a + 4] for a, name in enumerate(names)}, res[4 * n]


def _pad_rows(a, rows=8):
    extra = (-a.shape[0]) % rows
    return jnp.pad(a, ((0, extra), (0, 0))) if extra else a


def _pad_cols(a, cols):
    return jnp.pad(a, ((0, 0), (0, cols - a.shape[1]))) if a.shape[1] < cols else a


def _run(fn, *phases):
    outs, p_outs = fn(list(phases))
    for p, po in zip(phases, p_outs):
        p.then(po)
    return outs


def kernel(x, c, norm_g, w_mod, b_mod, w_ffn_in, w_ffn_out, ab_w_in, ab_norm_v, ab_w_s, ab_b_s, ab_conv_w, ab_w_out, pool_w_grp, pool_scale, final_g, loss_target, m_norm_g, m_w_mod, m_b_mod, m_w_ffn_in, m_w_ffn_out, m_ab_w_in, m_ab_norm_v, m_ab_w_s, m_ab_b_s, m_ab_conv_w, m_ab_w_out, m_pool_w_grp, m_pool_scale, m_final_g, v_norm_g, v_w_mod, v_b_mod, v_w_ffn_in, v_w_ffn_out, v_ab_w_in, v_ab_norm_v, v_ab_w_s, v_ab_b_s, v_ab_conv_w, v_ab_w_out, v_pool_w_grp, v_pool_scale, v_final_g):
    ix, iy, ic = _place()
    chip = 2 * ix + iy
    me = 4 * ix + 2 * iy + ic
    where = jnp.stack([chip, ic]).astype(jnp.int32)
    s, d = x.shape[1], x.shape[2]
    x0 = x.reshape(s, d)
    target = loss_target.reshape(s, d)
    n_layers = norm_g.shape[0]
    dq = d // N_CHIPS
    heads, chunk = ab_w_s.shape[1], ab_w_s.shape[2]
    da = ab_norm_v.shape[1]
    db = ab_conv_w.shape[2] * N_CHIPS
    f_hidden = w_ffn_out.shape[2] * N_CHIPS
    assert n_layers == 2 and da % heads == 0

    cw_pad = _pad_cols(ab_conv_w.reshape(3, db // N_CHIPS), dq)
    packed = jnp.concatenate(
        [_pad_rows(c.reshape(N_CHIPS, dq)), _pad_rows(norm_g.reshape(-1, dq)), _pad_rows(pool_scale.reshape(1, dq)), _pad_rows(cw_pad)],
        axis=0,
    )
    ncol = w_mod.shape[2]
    b_cols = lax.dynamic_slice(b_mod, (0, chip * ncol), (n_layers, ncol)).reshape(n_layers, 1, ncol)
    small = {}

    def small_gather(key, arrs):
        def then(outs):
            small[key] = outs

        return _phase_small_gather(arrs, then)

    stacks = {
        "w_ffn_in": tuple(a.reshape((-1,) + a.shape[2:]) for a in (w_ffn_in, m_w_ffn_in, v_w_ffn_in)),
        "w_ffn_out": tuple(a.reshape((-1,) + a.shape[2:]) for a in (w_ffn_out, m_w_ffn_out, v_w_ffn_out)),
        "ab_w_in": (ab_w_in, m_ab_w_in, v_ab_w_in),
        "ab_w_out": (ab_w_out, m_ab_w_out, v_ab_w_out),
        "pool_w_grp": (pool_w_grp[0], m_pool_w_grp[0], v_pool_w_grp[0]),
    }
    big_in = _Big((1, d, 2 * f_hidden), 2, 1)
    big_out = _Big((1, f_hidden, d), 1, 2)
    units = {}
    for l in range(n_layers):
        for k in range(2):
            units[f"in{l}{k}"] = (big_in, "w_ffn_in", 2 * l + k)
            units[f"out{l}{k}"] = (big_out, "w_ffn_out", 2 * l + k)
    units["abin"] = (_Big((1, d, ab_w_in.shape[2] * N_CHIPS), 2, 1), "ab_w_in", 0)
    units["about"] = (_Big((1, ab_w_out.shape[1] * N_CHIPS, d), 1, 2), "ab_w_out", 0)
    units["pool"] = (_Big((pool_w_grp.shape[1], pool_w_grp.shape[2] * N_CHIPS, pool_w_grp.shape[3]), 1, 0), "pool_w_grp", 0)
    big = {u: g for u, (g, _, _) in units.items()}

    weight = {}
    complete = set()

    def cast(u):
        g, st, b0 = units[u]

        def launch(phases):
            (weight[u],), p_outs = _cast_into_full(stacks[st][0], b0, g, where, "cast_" + u, phases)
            return None, p_outs

        return launch

    def gather_relay(us, second, whole_first):
        def then(outs):
            for u, o in zip(us, outs):
                weight[u] = o

        return _phase_gather_relay([weight[u] for u in us], [big[u] for u in us], second, whole_first, then)

    def gather_sibling(*us):
        def then(outs):
            for u, o in zip(us, outs):
                weight[u] = o
                complete.add(u)

        return _phase_gather_sibling([weight[u] for u in us], [big[u] for u in us], then)

    def w_of(u):
        assert u in complete, u
        return weight[u]

    _run(cast("in00"), small_gather("inputs", [packed]))
    small_all = small["inputs"][0]
    by_chip = small_all[0::2]
    c_all = small_all[:, 0:N_CHIPS, :].reshape(N_DEV, d)
    norm_full = by_chip[:, 8 : 8 + 3 * n_layers, :].transpose(1, 0, 2).reshape(3 * n_layers, d)
    pool_scale_full = by_chip[:, 16:17, :].transpose(1, 0, 2).reshape(1, d)
    conv_full = by_chip[:, 24:27, : db // N_CHIPS].transpose(1, 0, 2).reshape(3, db)
    pieces = [("in00", "out00"), ("abin", "about"), ("in01", "out01"), ("in10", "out10", "pool"), ("in11", "out11")]
    in_flight = {}

    def start_gather(p):
        in_flight[p, 0] = _split_start(gather_relay(pieces[p], False, p == 0), f"gather_{p}_start")

    def relay_gather(p):
        flight = in_flight.pop((p, 0))
        _split_wait(flight, list(started().ins), f"gather_{p}_arrived")
        in_flight[p, 1] = _split_start(gather_relay(pieces[p], True, p == 0), f"gather_{p}_relay")

    def started():
        return _after(*[flight.token for flight in in_flight.values()])

    def finish_gather(p, after, meanwhile=None):
        flight = in_flight.pop((p, 1))
        _split_wait(flight, list(after) + list(started().ins), f"gather_{p}_wait")
        crossing = _split_start(gather_sibling(*pieces[p]), f"gather_{p}_forward")
        behind = [crossing.token]
        if p + 1 < len(pieces):
            relay_gather(p + 1)
        if p + 2 < len(pieces):
            start_gather(p + 2)
        behind = behind + list(started().ins)
        if meanwhile is not None:
            behind = behind + meanwhile(_after(crossing.token))
        _split_wait(crossing, behind, f"gather_{p}_forwarded")

    _run(cast("out00"))
    start_gather(0)
    mod_cols = _run(lambda phases: _mod_fwd(c_all, w_mod, b_cols, phases), started())[0]

    def mod_rows(outs):
        small["mod"] = outs

    for piece in pieces[2:]:
        for u in piece:
            _run(cast(u), started())
    _run(cast("about"), started())
    _run(cast("abin"), _phase_small_exchange(mod_cols.transpose(1, 0, 2), mod_rows), started())
    relay_gather(0)
    start_gather(1)
    mod_mine = small["mod"][0][0::2]
    mod = mod_mine.transpose(1, 0, 2).reshape(n_layers, 3, 3, d)
    vecs = {
        (l, sub): jnp.pad(norm_full[3 * l + sub][None], ((0, 7), (0, 0))) + jnp.pad(mod[l, sub], ((1, 4), (0, 0)))
        for l in range(n_layers)
        for sub in range(3)
    }
    b_rows = jnp.broadcast_to(ab_b_s[0].T[:, :, None], (chunk, heads, da // heads)).reshape(chunk, da)

    saved = {}

    def ffn_forward(xs, l, sub, k, *phases):
        saved[l, sub, "x"] = xs
        xs, gg, uu, yb = _run(
            lambda ph: _ffn_fwd(xs, vecs[l, sub], w_of(f"in{l}{k}"), w_of(f"out{l}{k}"), f"ffn_fwd_{l}{k}", ph), *phases
        )
        saved[l, sub, "act"] = (gg, uu, yb)
        return xs

    finish_gather(0, [vecs[0, 0]])
    xs = ffn_forward(x0, 0, 0, 0, started())
    saved[0, 1, "x"] = xs
    finish_gather(1, [xs])
    (proj,) = _run(lambda ph: _proj_mod_fwd(xs, vecs[0, 1], w_of("abin"), ph), started())
    (cat,) = _run(lambda ph: _ab_mix_fwd(proj, ab_norm_v, ab_w_s[0], b_rows, conv_full, ph))
    xs, yb = _run(lambda ph: _proj_res_fwd(cat, w_of("about"), xs, vecs[0, 1], ph))
    saved[0, 1, "act"] = (proj, cat, yb)
    finish_gather(2, [xs])
    xs = ffn_forward(xs, 0, 2, 1, started())
    finish_gather(3, [xs])
    xs = ffn_forward(xs, 1, 0, 0, started())
    saved[1, 1, "x"] = xs
    pooled = []

    def pool_forward(behind):
        pooled.extend(_run(lambda ph: _pool_fwd(xs, vecs[1, 1], w_of("pool"), pool_scale_full, ph), behind))
        return [pooled[0]]

    finish_gather(4, [xs], pool_forward)
    xs, pp, oo = pooled
    saved[1, 1, "act"] = (pp, oo)
    xs = ffn_forward(xs, 1, 2, 1)
    dxs, aux = _run(lambda ph: _loss_head(xs, final_g.reshape(1, d), target, ph))

    grad = {}
    recv = {}
    csum = {}
    parts = {}
    reduced = {}
    done = set()
    dvecs, small_g = {}, {}

    def pair_exchange(*us):
        def then(outs):
            for u, o in zip(us, outs):
                recv[u] = o

        return _phase_pair_exchange([grad[u] for u in us], [big[u] for u in us], then)

    def grad_half(u, a, bs, mine, name, *phases):
        (res,) = _run(lambda ph: _grad_half(a, bs, big[u], where, mine, recv[u] if mine else None, name, ph), *phases)
        return res

    def pair_sum(u, *phases):
        def launch(ph):
            (csum[u],), p_outs = _pair_sum(grad[u], recv[u], big[u], where, "pair_sum_" + u, ph)
            return None, p_outs

        _run(launch, *phases)

    def chip_exchange(*us):
        def then(outs):
            for u, o in zip(us, outs):
                parts[u] = o

        return _phase_chip_exchange([csum[u] for u in us], [big[u] for u in us], then)

    def chip_sum(*us, carried=()):
        for n_u, u in enumerate(us):
            g, st, b0 = units[u]

            def launch(ph):
                (reduced[st],), p_outs = _chip_sum(
                    csum[u], parts[u], g, where, reduced.get(st), stacks[st][0].shape, b0, "chip_sum_" + u, ph
                )
                return None, p_outs

            _run(launch, *(carried if n_u == 0 else ()))

    def pair_broadcast(*us):
        sts = [units[u][1] for u in us]
        assert len(set(sts)) == len(sts)

        def then(outs):
            for u, st, o in zip(us, sts, outs):
                reduced[st] = o
                done.add(u)

        return _phase_pair_broadcast([reduced[st] for st in sts], [big[u] for u in us], [units[u][2] for u in us], then)

    def ffn_backward(dxs, l, sub, k, carried_bwd, carried_send, carried_mine):
        gg, uu, yb = saved[l, sub, "act"]
        w_in, w_out = w_of(f"in{l}{k}"), w_of(f"out{l}{k}")
        uo, ui, tag = f"out{l}{k}", f"in{l}{k}", f"{l}{k}"
        dxs, dg, du, a, h, dy, dvecs[l, sub] = _run(
            lambda ph: _ffn_bwd(dxs, saved[l, sub, "x"], vecs[l, sub], gg, uu, yb, w_in, w_out, "ffn_bwd_" + tag, ph), *carried_bwd()
        )
        grad[uo] = grad_half(uo, a, [dy], False, "dw_out_send_" + tag, *carried_send())
        grad[ui] = grad_half(ui, h, [dg, du], False, "dw_in_send_" + tag, pair_exchange(uo))
        csum[uo] = grad_half(uo, a, [dy], True, "dw_out_" + tag, pair_exchange(ui))
        csum[ui] = grad_half(ui, h, [dg, du], True, "dw_in_" + tag, *carried_mine())
        return dxs

    none = lambda: ()
    dxs = ffn_backward(dxs, 1, 2, 1, none, none, none)
    pp, oo = saved[1, 1, "act"]
    dxs, grad["pool"], small_g["pool_scale"], dvecs[1, 1] = _run(
        lambda ph: _pool_bwd(dxs, saved[1, 1, "x"], vecs[1, 1], pp, oo, w_of("pool"), pool_scale_full, ph)
    )

    def after_11():
        return (chip_exchange("in11", "out11"), pair_exchange("pool"))

    def bcast_11():
        chip_sum("in11", "out11")
        pair_sum("pool")
        return (pair_broadcast("in11", "out11"), chip_exchange("pool"))

    dxs = ffn_backward(dxs, 1, 0, 0, after_11, bcast_11, none)

    def after_10():
        return (chip_exchange("in10", "out10"),)

    def bcast_10():
        chip_sum("in10", "out10", "pool")
        return (pair_broadcast("in10", "out10", "pool"),)

    dxs = ffn_backward(dxs, 0, 2, 1, after_10, bcast_10, none)

    proj, cat, yb = saved[0, 1, "act"]
    out01 = _split_start(chip_exchange("out01"), "reduce_out01_start")
    dy, dcat, dgate = _run(lambda ph: _proj_res_bwd(dxs, yb, vecs[0, 1], w_of("about"), ph), _after(out01.token))
    grad["about"] = grad_half("about", cat, [dy], False, "dw_ab_out_send")
    dproj, small_g["ab_norm_v"], small_g["ab_w_s"], dzs, small_g["ab_conv_w"] = _run(
        lambda ph: _ab_mix_bwd(proj, dcat, ab_norm_v, ab_w_s[0], b_rows, conv_full, ph), pair_exchange("about")
    )
    small_g["ab_b_s"] = dzs.reshape(chunk, heads, da // heads).sum(axis=2).T
    dxs, h, dvecs[0, 1] = _run(
        lambda ph: _proj_mod_bwd(dproj[None], w_of("abin"), saved[0, 1, "x"], vecs[0, 1], dxs, dgate, "ab_in_bwd", ph)
    )
    grad["abin"] = grad_half("abin", h, [dproj], False, "dw_ab_in_send")
    (csum["out01"],) = _split_wait(out01, [grad["abin"]], "reduce_out01_wait")
    chip_sum("out01", carried=(pair_exchange("abin"),))
    csum["about"] = grad_half("about", cat, [dy], True, "dw_ab_out", pair_broadcast("out01"))
    csum["abin"] = grad_half("abin", h, [dproj], True, "dw_ab_in")

    layout = {}
    tail = {}

    def after_01():
        tail["01"] = _split_start(chip_exchange("in01", "abin", "about"), "reduce_01_start")
        return (_after(tail["01"].token),)

    def pack_small_grads():
        dvec_all = jnp.stack([dvecs[l, sub] for l in range(n_layers) for sub in range(3)])
        dgain = dvec_all[:, 0, :]
        dmod = dvec_all[:, 1:4, :].reshape(3 * 3 * n_layers, d)
        rows = {
            "norm_g": (dgain, None, dq), "final_g": (aux[0:1], 0, d), "pool_scale": (small_g["pool_scale"], None, dq),
            "b_mod": (dmod, 0, d), "ab_norm_v": (small_g["ab_norm_v"], 0, da),
            "ab_conv_w": (small_g["ab_conv_w"], None, db // N_CHIPS), "ab_b_s": (small_g["ab_b_s"], 0, chunk),
            "loss": (aux[1:2], 0, d),
        }
        row0 = 0
        for nm, (pc, col0, cols) in rows.items():
            layout[nm] = (row0, pc.shape[0], col0, cols)
            row0 += pc.shape[0]
        packed_rows = -(-row0 // 8) * 8
        return sum(
            jnp.pad(pc, ((layout[nm][0], packed_rows - layout[nm][0] - pc.shape[0]), (0, d - pc.shape[1])))
            for nm, (pc, _, _) in rows.items()
        )

    def bcast_01():
        csum["in01"], csum["abin"], csum["about"] = _split_wait(tail["01"], [dvecs[0, 0]], "reduce_01_wait")
        chip_sum("in01", "abin", "about")
        grads_small = [pack_small_grads(), small_g["ab_w_s"].reshape(heads * chunk, chunk)]
        tail["small"] = _split_start(small_gather("grads", grads_small), "gather_small_grads_start")
        return (pair_broadcast("in01", "abin", "about"), _after(tail["small"].token))

    def reduce_out00():
        tail["out00"] = _split_start(chip_exchange("out00"), "reduce_out00_start")
        return (_after(tail["out00"].token),)

    dxs = ffn_backward(dxs, 0, 0, 0, after_01, bcast_01, reduce_out00)
    grad_x = dxs.reshape(x.shape)

    last = _split_start(chip_exchange("in00"), "reduce_last_start")
    (csum["out00"],) = _split_wait(tail["out00"], [last.token], "reduce_out00_wait")
    chip_sum("out00")
    _flush("broadcast_out00", pair_broadcast("out00"))
    _split_wait(tail["small"], [reduced["w_ffn_out"]], "gather_small_grads_wait")
    g_all, gws_all = small["grads"]

    out = {}

    def adam_stack(st, after=()):
        w3, m3, v3 = stacks[st]
        assert all(u in done for u, (_, ust, _) in units.items() if ust == st), st
        shape = {"w_ffn_in": w_ffn_in.shape, "w_ffn_out": w_ffn_out.shape, "pool_w_grp": pool_w_grp.shape}.get(st, w3.shape)
        out[st] = tuple(a.reshape(shape) for a in _adam_stack(w3, reduced[st], m3, v3, "adam_" + st, after))

    for st in ("w_ffn_out", "ab_w_in", "ab_w_out", "pool_w_grp"):
        adam_stack(st, (last.token,))

    shapes2d = {
        "norm_g": (3 * n_layers, dq), "b_mod": (9 * n_layers, d), "final_g": (1, d), "ab_norm_v": (1, da),
        "pool_scale": (1, dq), "ab_conv_w": (3, db // N_CHIPS), "ab_b_s": (heads, chunk), "ab_w_s": (heads * chunk, chunk),
    }
    small_w = {"norm_g": (norm_g, m_norm_g, v_norm_g), "b_mod": (b_mod, m_b_mod, v_b_mod), "final_g": (final_g, m_final_g, v_final_g),
               "ab_norm_v": (ab_norm_v, m_ab_norm_v, v_ab_norm_v), "pool_scale": (pool_scale, m_pool_scale, v_pool_scale),
               "ab_conv_w": (ab_conv_w, m_ab_conv_w, v_ab_conv_w), "ab_b_s": (ab_b_s, m_ab_b_s, v_ab_b_s), "ab_w_s": (ab_w_s, m_ab_w_s, v_ab_w_s)}
    smalls = {nm: tuple(a.reshape(shapes2d[nm]) for a in wmv) for nm, wmv in small_w.items()}
    small_out, loss = _small_adam(g_all, gws_all, layout, smalls, chip)
    loss = loss.reshape(())
    for nm, res in small_out.items():
        out[nm] = tuple(a.reshape(small_w[nm][0].shape) for a in res)

    mod_row0 = layout["b_mod"][0]
    dmod_all = g_all[:, mod_row0 : mod_row0 + 9 * n_layers, :].reshape(N_DEV, n_layers, 9 * d)
    dmod_cols = lax.dynamic_slice(dmod_all, (0, 0, chip * ncol), (N_DEV, n_layers, ncol)).transpose(1, 0, 2)
    out["w_mod"] = tuple(_mod_bwd_adam(c_all.T, dmod_cols, w_mod, m_w_mod, v_w_mod, (last.token,)))

    (csum["in00"],) = _split_wait(
        last, [out[st][1] for st in ("w_mod", "w_ffn_out", "ab_w_in", "ab_w_out", "pool_w_grp")], "reduce_last_wait"
    )
    chip_sum("in00")
    _flush("broadcast_last", pair_broadcast("in00"))
    adam_stack("w_ffn_in")

    order = ["norm_g", "w_mod", "b_mod", "w_ffn_in", "w_ffn_out", "ab_w_in", "ab_norm_v", "ab_w_s", "ab_b_s", "ab_conv_w", "ab_w_out", "pool_w_grp", "pool_scale", "final_g"]
    return (loss, grad_x, *[out[nm][0] for nm in order], *[out[nm][1] for nm in order], *[out[nm][2] for nm in order], *[out[nm][3] for nm in order])
```

```python
import functools
import math

import jax
import jax.numpy as jnp
from jax import lax
from jax.experimental import pallas as pl
from jax.experimental.pallas import tpu as pltpu

F32 = jnp.float32
BF16 = jnp.bfloat16
MESH = pl.DeviceIdType.MESH

EPS = 1e-6
ADAM_LR = 0.001
ADAM_B1 = 0.9
ADAM_B2 = 0.999
ADAM_EPS = 1e-08
ADAM_WD = 0.01
ADAM_STEP = 10
POOL_WINDOWS = (2, 4, 8, 16)
POOL_HALO = 16
CONV_HALO = 8
N_CHIPS = 4
N_DEV = 8
VMEM_LIMIT_BYTES = 48 * 1024 * 1024
EW_BLOCK_ELEMS = 1024 * 1024
ADAM_BLOCK_ELEMS = 512 * 1024


def _pick(n, prefs):
    for p in prefs:
        if p <= n and n % p == 0:
            return p
    return n


def _row_tile(rows, cols, block_elems=EW_BLOCK_ELEMS):
    best = None
    for d in range(16, rows + 1, 16):
        if rows % d == 0 and d * cols <= block_elems:
            best = d
    return best or rows


def _dot(a, b):
    return jnp.dot(a, b, preferred_element_type=F32)


def _dot_nt(a, b):
    return lax.dot_general(a, b, (((1,), (1,)), ((), ())), preferred_element_type=F32)


def _dot_tn(a, b):
    return lax.dot_general(a, b, (((0,), (0,)), ((), ())), preferred_element_type=F32)


def _sigmoid(x):
    return 0.5 * jnp.tanh(0.5 * x) + 0.5


_GELU_C = math.sqrt(2.0 / math.pi)


def _gelu(x):
    x2 = x * x
    t = jnp.tanh(_GELU_C * (x + 0.044715 * x2 * x))
    val = 0.5 * x * (1.0 + t)
    grad = 0.5 * (1.0 + t) + 0.5 * x * (1.0 - t * t) * (_GELU_C * (1.0 + 3.0 * 0.044715 * x2))
    return val, grad


def _rstd(x):
    return lax.rsqrt(jnp.mean(x * x, axis=-1, keepdims=True) + EPS)


def _modulate(x, vec_ref):
    return (x * _rstd(x)) * vec_ref[0:1, :] * (1.0 + vec_ref[2:3, :]) + vec_ref[1:2, :]


def _modulate_bwd(x, dh, vec_ref, dvec_ref):
    gn, sh, sc = vec_ref[0:1, :], vec_ref[1:2, :], vec_ref[2:3, :]
    rstd = _rstd(x)
    r = x * rstd
    dvec_ref[0:1, :] += jnp.sum(dh * r * (1.0 + sc), axis=0, keepdims=True)
    dvec_ref[1:2, :] += jnp.sum(dh, axis=0, keepdims=True)
    dvec_ref[2:3, :] += jnp.sum(dh * r * gn, axis=0, keepdims=True)
    gm = gn * (1.0 + sc)
    dr = dh * gm
    dx = rstd * (dr - r * jnp.mean(dr * r, axis=-1, keepdims=True))
    return dx, r * gm + sh


def _adam(w, g, m, v):
    m = ADAM_B1 * m + (1.0 - ADAM_B1) * g
    v = ADAM_B2 * v + (1.0 - ADAM_B2) * (g * g)
    m_hat = m / (1.0 - ADAM_B1**ADAM_STEP)
    v_hat = v / (1.0 - ADAM_B2**ADAM_STEP)
    delta = -ADAM_LR * (m_hat / (jnp.sqrt(v_hat) + ADAM_EPS) + ADAM_WD * w)
    return delta, m, v


_ANY = pl.BlockSpec(memory_space=pl.ANY)


class _Phase:
    def __init__(self, ins, out_shapes, aliases, n_sems, start, finish, then):
        self.ins, self.out_shapes, self.aliases, self.n_sems = list(ins), list(out_shapes), dict(aliases), n_sems
        self.start, self.finish, self.then = start, finish, then


def _call(body, name, grid, in_specs, out_specs, out_shape, ins, scratch=(), prefetch=(), phases=(), in_place=None):
    n_pre, n_in, n_out, n_sc = len(prefetch), len(in_specs), len(out_specs), len(scratch)
    ph_in = [len(p.ins) for p in phases]
    ph_out = [len(p.out_shapes) for p in phases]

    def kernel_body(*refs):
        pos = [0]

        def take(k):
            pos[0] += k
            return refs[pos[0] - k : pos[0]]

        pre, ins_ = take(n_pre), take(n_in)
        p_ins = [take(k) for k in ph_in]
        outs_ = take(n_out)
        p_outs = [take(k) for k in ph_out]
        sc = take(n_sc)
        sems = [take(2) for _ in phases]
        if phases:
            ids = [pl.program_id(a) for a in range(len(grid))]
            first = functools.reduce(jnp.logical_and, [i == 0 for i in ids])
            last = functools.reduce(jnp.logical_and, [i == g - 1 for i, g in zip(ids, grid)])

            @pl.when(first)
            def _():
                for p, pi, po, (send, recv) in zip(phases, p_ins, p_outs, sems):
                    p.start(pi, po, send, recv)

        if body is not None:
            body(*pre, *ins_, *outs_, *sc)
        if phases:

            @pl.when(last)
            def _():
                for p, pi, po, (send, recv) in zip(phases, p_ins, p_outs, sems):
                    p.finish(pi, po, send, recv)

    aliases = {n_pre + i: o for i, o in (in_place or {}).items()}
    i0, o0 = n_pre + n_in, n_out
    for p in phases:
        for i, o in p.aliases.items():
            aliases[i0 + i] = o0 + o
        i0 += len(p.ins)
        o0 += len(p.out_shapes)
    all_in = list(in_specs) + [_ANY] * sum(ph_in)
    all_out = list(out_specs) + [_ANY] * sum(ph_out)
    all_scratch = list(scratch)
    for p in phases:
        all_scratch += [pltpu.SemaphoreType.DMA((p.n_sems,)), pltpu.SemaphoreType.DMA((p.n_sems,))]
    shapes = list(out_shape) + [s for p in phases for s in p.out_shapes]
    operands = list(prefetch) + list(ins) + [a for p in phases for a in p.ins]
    sem = ("arbitrary",) * len(grid)
    params = pltpu.CompilerParams(dimension_semantics=sem, vmem_limit_bytes=VMEM_LIMIT_BYTES)
    if n_pre:
        res = pl.pallas_call(
            kernel_body, name=name, out_shape=shapes, input_output_aliases=aliases, compiler_params=params,
            grid_spec=pltpu.PrefetchScalarGridSpec(
                num_scalar_prefetch=n_pre, grid=grid, in_specs=all_in, out_specs=all_out, scratch_shapes=all_scratch
            ),
        )(*operands)
    else:
        res = pl.pallas_call(
            kernel_body, name=name, grid=grid, in_specs=all_in, out_specs=all_out, out_shape=shapes,
            scratch_shapes=all_scratch, input_output_aliases=aliases, compiler_params=params,
        )(*operands)
    res = list(res)
    outs, rest = res[:n_out], res[n_out:]
    p_res = []
    for k in ph_out:
        p_res.append(rest[:k])
        rest = rest[k:]
    return outs, p_res


def _place():
    return lax.axis_index("x"), lax.axis_index("y"), lax.axis_index("c")


def _other_chips():
    x, y, _ = _place()
    return [(1 - x, y), (x, 1 - y), (1 - x, 1 - y)]


def _flip(k):
    x, y, c = _place()
    return (1 - x if k & 4 else x, 1 - y if k & 2 else y, 1 - c if k & 1 else c)


def _remote(src, dst, send, recv, k, to):
    return pltpu.make_async_remote_copy(
        src_ref=src, dst_ref=dst, send_sem=send.at[k], recv_sem=recv.at[k], device_id=to, device_id_type=MESH
    )


def _phase_small_gather(arrs, then):
    n = len(arrs)

    def copies(ins, outs, send, recv):
        x, y, c = _place()
        me = 4 * x + 2 * y + c
        local = [pltpu.make_async_copy(ins[a], outs[a].at[me], send.at[a * N_DEV]) for a in range(n)]
        remote = [_remote(ins[a], outs[a].at[me], send, recv, a * N_DEV + k, _flip(k)) for a in range(n) for k in range(1, N_DEV)]
        return local, remote

    def start(ins, outs, send, recv):
        local, remote = copies(ins, outs, send, recv)
        for cp in local + remote:
            cp.start()

    def finish(ins, outs, send, recv):
        local, remote = copies(ins, outs, send, recv)
        for cp in remote + local:
            cp.wait()

    shapes = [jax.ShapeDtypeStruct((N_DEV,) + a.shape, a.dtype) for a in arrs]
    return _Phase(arrs, shapes, {}, n * N_DEV, start, finish, then)


def _phase_small_exchange(arr, then):
    def copies(ins, outs, send, recv):
        x, y, c = _place()
        me = 4 * x + 2 * y + c
        local = pltpu.make_async_copy(ins[0].at[me], outs[0].at[me], send.at[0])
        remote = []
        for k in range(1, N_DEV):
            px, py, pc = _flip(k)
            remote.append(_remote(ins[0].at[4 * px + 2 * py + pc], outs[0].at[me], send, recv, k, (px, py, pc)))
        return [local] + remote

    def start(ins, outs, send, recv):
        for cp in copies(ins, outs, send, recv):
            cp.start()

    def finish(ins, outs, send, recv):
        for cp in copies(ins, outs, send, recv):
            cp.wait()

    return _Phase([arr], [jax.ShapeDtypeStruct(arr.shape, arr.dtype)], {}, N_DEV, start, finish, then)


def _after(*arrs):
    nothing = lambda *args: None
    return _Phase(arrs, [], {}, 1, nothing, nothing, nothing)


def _flush(name, *phases):
    _, p_outs = _call(None, name, (1,), [], [], [], [], phases=list(phases))
    for p, po in zip(phases, p_outs):
        p.then(po)


class _Big:
    KINDS = {"full": (True, True), "half": (True, False), "shard": (False, True), "block": (False, False)}

    def __init__(self, f3, s3, h3):
        assert s3 != h3
        self.f3, self.s3, self.h3 = tuple(f3), s3, h3
        self.bd = tuple(f3[a] // (N_CHIPS if a == s3 else 1) // (2 if a == h3 else 1) for a in range(3))
        self.tile = (1, _row_tile(self.bd[1], self.bd[2]), self.bd[2])
        self.grid = tuple(self.bd[a] // self.tile[a] for a in range(3))

    def dims(self, kind):
        chips, halves = self.KINDS[kind]
        return tuple(
            self.bd[a] * (N_CHIPS if chips and a == self.s3 else 1) * (2 if halves and a == self.h3 else 1) for a in range(3)
        )

    def view(self, ref, chip=None, half=None, batch0=0, both_halves=True, part=None):
        start = [batch0, 0, 0]
        size = list(ref.shape)
        size[0] = self.bd[0] * (2 if self.h3 == 0 and both_halves else 1)
        if chip is not None:
            start[self.s3] += chip * self.bd[self.s3]
            size[self.s3] = self.bd[self.s3]
        if half is not None:
            start[self.h3] += half * self.bd[self.h3]
            size[self.h3] = self.bd[self.h3]
        if part is not None:
            size[1] //= 2
            start[1] += part * size[1]
        return ref.at[tuple(pl.ds(st, sz) for st, sz in zip(start, size))]

    def spec(self, chip_from=None, half_from=None, lead=(), batch0=0):
        extra = "grid" in (chip_from, half_from)

        def index(*args):
            pref, idx = args[-1], list(args[int(extra) : -1])
            idx[0] += batch0
            if chip_from:
                idx[self.s3] += (pref[0] if chip_from == "pref" else args[0]) * self.grid[self.s3]
            if half_from:
                idx[self.h3] += (pref[1] if half_from == "pref" else args[0]) * self.grid[self.h3]
            return (0,) * len(lead) + tuple(idx)

        return pl.BlockSpec(tuple(lead) + self.tile, index)


def _same(arrs):
    return [jax.ShapeDtypeStruct(a.shape, a.dtype) for a in arrs]


def _phase_gather_relay(arrs, bigs, second, whole_first, then):
    n = len(arrs)
    per = 4 if second and not whole_first else 2

    def copies(outs, send, recv, arriving):
        x, y, c = _place()
        me, xn, yn, dg = (x, y), (1 - x, y), (x, 1 - y), (1 - x, 1 - y)
        if not second:
            part = (None, None) if whole_first else (0, 1)
            plan = [((xn if arriving else me), part[0], xn), ((yn if arriving else me), part[1], yn)]
        elif whole_first:
            plan = [(dg, 0, yn), (dg, 1, xn)] if arriving else [(xn, 0, yn), (yn, 1, xn)]
        elif arriving:
            plan = [(yn, 0, yn), (dg, 0, yn), (xn, 1, xn), (dg, 1, xn)]
        else:
            plan = [(me, 0, yn), (xn, 0, yn), (me, 1, xn), (yn, 1, xn)]
        res = []
        for a in range(n):
            for k, (chip, part, to) in enumerate(plan):
                blk = bigs[a].view(outs[a], 2 * chip[0] + chip[1], c, part=part)
                res.append(_remote(blk, blk, send, recv, per * a + k, (*to, c)))
        return res

    def start(ins, outs, send, recv):
        for cp in copies(outs, send, recv, False):
            cp.start()

    def finish(ins, outs, send, recv):
        for cp in copies(outs, send, recv, True):
            cp.wait_recv()
        for cp in copies(outs, send, recv, False):
            cp.wait_send()

    return _Phase(arrs, _same(arrs), {a: a for a in range(n)}, per * n, start, finish, then)


def _phase_gather_sibling(arrs, bigs, then):
    n = len(arrs)

    def copies(outs, send, recv, arriving):
        x, y, c = _place()
        return [
            _remote(blk, blk, send, recv, 3 * a + j, (x, y, 1 - c))
            for j, chip in enumerate(_other_chips())
            for a in range(n)
            for blk in [bigs[a].view(outs[a], 2 * chip[0] + chip[1], 1 - c if arriving else c)]
        ]

    def start(ins, outs, send, recv):
        for cp in copies(outs, send, recv, False):
            cp.start()

    def finish(ins, outs, send, recv):
        for cp in copies(outs, send, recv, True):
            cp.wait_recv()
        for cp in copies(outs, send, recv, False):
            cp.wait_send()

    return _Phase(arrs, _same(arrs), {a: a for a in range(n)}, 3 * n, start, finish, then)


def _phase_pair_exchange(grads, bigs, then):
    n = len(grads)

    def copies(ins, outs, send, recv):
        x, y, c = _place()
        srcs = [ins[a] if ins[a].shape == outs[a].shape else bigs[a].view(ins[a], None, 1 - c) for a in range(n)]
        return [_remote(srcs[a], outs[a], send, recv, a, (x, y, 1 - c)) for a in range(n)]

    def start(ins, outs, send, recv):
        for cp in copies(ins, outs, send, recv):
            cp.start()

    def finish(ins, outs, send, recv):
        for cp in copies(ins, outs, send, recv):
            cp.wait()

    shapes = [jax.ShapeDtypeStruct(b.dims("half"), BF16) for b in bigs]
    return _Phase(grads, shapes, {}, n, start, finish, then)


def _phase_chip_exchange(sums, bigs, then):
    n = len(sums)

    def copies(ins, outs, send, recv):
        _, _, c = _place()
        return [
            _remote(bigs[a].view(ins[a], 2 * chip[0] + chip[1], both_halves=False), outs[a].at[j], send, recv, 3 * a + j, (*chip, c))
            for j, chip in enumerate(_other_chips())
            for a in range(n)
        ]

    def start(ins, outs, send, recv):
        for cp in copies(ins, outs, send, recv):
            cp.start()

    def finish(ins, outs, send, recv):
        for cp in copies(ins, outs, send, recv):
            cp.wait()

    shapes = [jax.ShapeDtypeStruct((N_CHIPS - 1,) + b.dims("block"), BF16) for b in bigs]
    return _Phase(sums, shapes, {}, 3 * n, start, finish, then)


_HBM = pl.BlockSpec(memory_space=pltpu.HBM)
_SEM = pl.BlockSpec(memory_space=pltpu.SEMAPHORE)
_DATAFLOW = pltpu.SideEffectType.DATAFLOW_SIDE_EFFECTING


class _InFlight:
    def __init__(self, phase, send, recv, arrays, token):
        self.phase, self.send, self.recv, self.arrays, self.token = phase, send, recv, arrays, token


def _phase_results(phase, refs):
    n_in = len(phase.ins)
    updated = {o: i for i, o in phase.aliases.items()}
    fresh = [o for o in range(len(phase.out_shapes)) if o not in updated]
    return [refs[updated[o]] if o in updated else refs[n_in + fresh.index(o)] for o in range(len(phase.out_shapes))]


def _split_start(phase, name):
    n_in = len(phase.ins)
    fresh = [s for o, s in enumerate(phase.out_shapes) if o not in phase.aliases.values()]
    arrays = list(phase.ins) + [lax.empty(s.shape, s.dtype) for s in fresh]
    n = len(arrays)

    def body(*refs):
        phase.start(refs[:n_in], _phase_results(phase, refs[:n]), refs[n], refs[n + 1])
        refs[-1][...] = jnp.zeros_like(refs[-1])

    operands = [pltpu.with_memory_space_constraint(a, pltpu.HBM) for a in arrays]
    res = pl.pallas_call(
        body, name=name,
        out_shape=[pltpu.SemaphoreType.DMA((phase.n_sems,)), pltpu.SemaphoreType.DMA((phase.n_sems,))]
        + [pltpu.HBM(a.shape, a.dtype) for a in arrays] + [jax.ShapeDtypeStruct((8, 128), F32)],
        in_specs=[_HBM] * n, out_specs=[_SEM, _SEM] + [_HBM] * n + [pl.BlockSpec(memory_space=pltpu.VMEM)],
        input_output_aliases={i: 2 + i for i in range(n)},
        compiler_params=pltpu.CompilerParams(has_side_effects=_DATAFLOW),
    )(*operands)
    return _InFlight(phase, res[0], res[1], list(res[2 : 2 + n]), res[-1])


def _split_wait(flight, after, name):
    phase, n = flight.phase, len(flight.arrays)
    n_in = len(phase.ins)

    def body(*refs):
        phase.finish(refs[:n_in], _phase_results(phase, refs[:n]), refs[n], refs[n + 1])

    res = pl.pallas_call(
        body, name=name, out_shape=[pltpu.HBM(a.shape, a.dtype) for a in flight.arrays],
        in_specs=[_HBM] * n + [_SEM, _SEM] + [_ANY] * len(after), out_specs=[_HBM] * n,
        input_output_aliases={i: i for i in range(n)},
        compiler_params=pltpu.CompilerParams(has_side_effects=_DATAFLOW),
    )(*flight.arrays, flight.send, flight.recv, *after)
    res = list(res)
    phase.then(_phase_results(phase, res))
    return res[:n_in]


def _phase_pair_broadcast(stacks, bigs, batch0s, then):
    n = len(stacks)

    def start(ins, outs, send, recv):
        x, y, c = _place()
        for a in range(n):
            blk = bigs[a].view(outs[a], None, c, batch0s[a])
            _remote(blk, blk, send, recv, a, (x, y, 1 - c)).start()

    def finish(ins, outs, send, recv):
        x, y, c = _place()
        for a in range(n):
            mine = bigs[a].view(outs[a], None, c, batch0s[a])
            theirs = bigs[a].view(outs[a], None, 1 - c, batch0s[a])
            _remote(mine, mine, send, recv, a, (x, y, 1 - c)).wait_send()
            _remote(theirs, theirs, send, recv, a, (x, y, 1 - c)).wait_recv()

    return _Phase(stacks, _same(stacks), {a: a for a in range(n)}, n, start, finish, then)


def _tile_call(body, name, big, where, extra, ins, in_specs, out_specs, out_shape, phases=()):
    grid = ((extra,) if extra else ()) + big.grid
    return _call(body, name, grid, in_specs, out_specs, out_shape, ins, prefetch=(where,), phases=phases)


def _cast_into_full(w_stack, batch0, big, where, name, phases=()):
    def body(_, w_ref, o_ref):
        o_ref[...] = w_ref[...].astype(BF16)

    return _tile_call(
        body, name, big, where, 2, [w_stack], [big.spec(None, "grid", batch0=batch0)], [big.spec("pref", "grid")],
        [jax.ShapeDtypeStruct(big.dims("full"), BF16)], phases,
    )


def _pair_sum(g_full, recv_half, big, where, name, phases=()):
    def body(_, g_ref, r_ref, o_ref):
        o_ref[...] = (g_ref[...].astype(F32) + r_ref[...].astype(F32)).astype(BF16)

    half = big.spec("grid", None)
    return _tile_call(
        body, name, big, where, N_CHIPS, [g_full, recv_half], [big.spec("grid", "pref"), half], [half],
        [jax.ShapeDtypeStruct(big.dims("half"), BF16)], phases,
    )


def _chip_sum(chip_sum, parts, big, where, stack, stack_shape, batch0, name, phases=()):
    def body(_, own_ref, p_ref, *rest):
        acc = own_ref[...].astype(F32)
        for k in range(N_CHIPS - 1):
            acc = acc + p_ref[k].astype(F32)
        rest[-1][...] = acc

    ins = [chip_sum, parts] + ([stack] if stack is not None else [])
    in_specs = [big.spec("pref", None), big.spec(None, None, lead=(N_CHIPS - 1,))] + ([_ANY] if stack is not None else [])
    return _call(
        body, name, big.grid, in_specs, [big.spec(None, "pref", batch0=batch0)], [jax.ShapeDtypeStruct(stack_shape, F32)], ins,
        prefetch=(where,), phases=phases, in_place={2: 0} if stack is not None else None,
    )


def _adam_stack(w, g, m, v, name, after=()):
    b, r, c = w.shape
    tr = _row_tile(r, c, ADAM_BLOCK_ELEMS)

    def body(w_ref, g_ref, m_ref, v_ref, *rest):
        go_ref, d_ref, mo_ref, vo_ref = rest[-4:]
        gv = g_ref[...]
        d, mo, vo = _adam(w_ref[...], gv, m_ref[...], v_ref[...])
        go_ref[...] = gv
        d_ref[...] = d
        mo_ref[...] = mo
        vo_ref[...] = vo

    spec = pl.BlockSpec((1, tr, c), lambda bb, i: (bb, i, 0))
    outs, _ = _call(
        body, name, (b, r // tr), [spec] * 4 + [_ANY] * len(after), [spec] * 4, [jax.ShapeDtypeStruct(w.shape, F32)] * 4,
        [w, g, m, v, *after],
    )
    return outs


def _mod_fwd(c_all, w_mod, b_cols, phases=()):
    n_layers, d, n = w_mod.shape
    tn = _pick(n, (768, 512, 384, 256, 128))

    def body(c_ref, w_ref, b_ref, o_ref):
        cv = c_ref[...]
        ca = (cv * _sigmoid(cv)).astype(BF16)
        o_ref[0] = _dot(ca, w_ref[0].astype(BF16)) + b_ref[0]

    return _call(
        body, "mod_fwd", (n_layers, n // tn),
        [
            pl.BlockSpec((N_DEV, d), lambda l, j: (0, 0)),
            pl.BlockSpec((1, d, tn), lambda l, j: (l, 0, j)),
            pl.BlockSpec((1, 1, tn), lambda l, j: (l, 0, j)),
        ],
        [pl.BlockSpec((1, N_DEV, tn), lambda l, j: (l, 0, j))],
        [jax.ShapeDtypeStruct((n_layers, N_DEV, n), F32)], [c_all, w_mod, b_cols], phases=phases,
    )


def _mod_bwd_adam(c_all_t, dmod_cols, w, m, v, after=()):
    n_layers, d, n = w.shape
    tn = _pick(n, (384, 256, 128))

    def body(c_ref, dm_ref, w_ref, m_ref, v_ref, *rest):
        g_ref, d_ref, mo_ref, vo_ref = rest[-4:]
        cv = c_ref[...]
        ca = (cv * _sigmoid(cv)).astype(BF16)
        g = _dot(ca, dm_ref[0].astype(BF16))
        g_ref[0] = g
        dl, mo, vo = _adam(w_ref[0], g, m_ref[0], v_ref[0])
        d_ref[0] = dl
        mo_ref[0] = mo
        vo_ref[0] = vo

    wspec = pl.BlockSpec((1, d, tn), lambda l, j: (l, 0, j))
    outs, _ = _call(
        body, "mod_bwd_adam", (n_layers, n // tn),
        [pl.BlockSpec((d, N_DEV), lambda l, j: (0, 0)), pl.BlockSpec((1, N_DEV, tn), lambda l, j: (l, 0, j)), wspec, wspec, wspec]
        + [_ANY] * len(after),
        [wspec] * 4, [jax.ShapeDtypeStruct(w.shape, F32)] * 4, [c_all_t, dmod_cols, w, m, v, *after],
    )
    return outs


def _ffn_fwd(x, vec, w_in, w_out, name, phases=()):
    s, d = x.shape
    f = w_out.shape[1]
    tm = _pick(s, (1024, 512, 256, 128))
    tf = _pick(f, (256, 128))
    nf = f // tf

    def body(x_ref, vec_ref, wg_ref, wu_ref, wo_ref, xo_ref, g_ref, u_ref, y_ref, h_sc, acc_sc):
        j = pl.program_id(1)

        @pl.when(j == 0)
        def _():
            h_sc[...] = _modulate(x_ref[...], vec_ref).astype(BF16)
            acc_sc[...] = jnp.zeros_like(acc_sc)

        h = h_sc[...]
        g = _dot(h, wg_ref[0])
        u = _dot(h, wu_ref[0])
        g_ref[...] = g.astype(BF16)
        u_ref[...] = u.astype(BF16)
        a = (g * _sigmoid(g) * u).astype(BF16)
        acc_sc[...] += _dot(a, wo_ref[0])

        @pl.when(j == nf - 1)
        def _():
            yv = acc_sc[...]
            xo_ref[...] = x_ref[...] + 0.5 * vec_ref[3:4, :] * yv
            y_ref[...] = yv.astype(BF16)

    row = pl.BlockSpec((tm, d), lambda i, j: (i, 0))
    hid = pl.BlockSpec((tm, tf), lambda i, j: (i, j))
    return _call(
        body, name, (s // tm, nf),
        [
            row,
            pl.BlockSpec((8, d), lambda i, j: (0, 0)),
            pl.BlockSpec((1, d, tf), lambda i, j: (0, 0, j)),
            pl.BlockSpec((1, d, tf), lambda i, j: (0, 0, nf + j)),
            pl.BlockSpec((1, tf, d), lambda i, j: (0, j, 0)),
        ],
        [row, hid, hid, row],
        [
            jax.ShapeDtypeStruct((s, d), F32),
            jax.ShapeDtypeStruct((s, f), BF16),
            jax.ShapeDtypeStruct((s, f), BF16),
            jax.ShapeDtypeStruct((s, d), BF16),
        ],
        [x, vec, w_in, w_in, w_out],
        scratch=[pltpu.VMEM((tm, d), BF16), pltpu.VMEM((tm, d), F32)], phases=phases,
    )


def _ffn_bwd(dxo, x, vec, gg, uu, y, w_in, w_out, name, phases=()):
    s, d = x.shape
    f = w_out.shape[1]
    tm = _pick(s, (512, 256, 128))
    tf = _pick(f, (256, 128))
    nf = f // tf

    def body(dxo_ref, x_ref, vec_ref, g_ref, u_ref, y_ref, wg_ref, wu_ref, wo_ref,
             dx_ref, dg_ref, du_ref, a_ref, h_ref, dy_ref, dvec_ref, acc_sc):
        i, j = pl.program_id(0), pl.program_id(1)

        @pl.when((i == 0) & (j == 0))
        def _():
            dvec_ref[...] = jnp.zeros_like(dvec_ref)

        @pl.when(j == 0)
        def _():
            dxo_v = dxo_ref[...]
            dy_ref[...] = (0.5 * vec_ref[3:4, :] * dxo_v).astype(BF16)
            dvec_ref[3:4, :] += 0.5 * jnp.sum(dxo_v * y_ref[...].astype(F32), axis=0, keepdims=True)
            acc_sc[...] = jnp.zeros_like(acc_sc)

        da = _dot_nt(dy_ref[...], wo_ref[0])
        g = g_ref[...].astype(F32)
        u = u_ref[...].astype(F32)
        sig = _sigmoid(g)
        sl = g * sig
        a_ref[...] = (sl * u).astype(BF16)
        dg = (da * u * (sig * (1.0 + g * (1.0 - sig)))).astype(BF16)
        du = (da * sl).astype(BF16)
        dg_ref[...] = dg
        du_ref[...] = du
        acc_sc[...] += _dot_nt(dg, wg_ref[0]) + _dot_nt(du, wu_ref[0])

        @pl.when(j == nf - 1)
        def _():
            dx, h = _modulate_bwd(x_ref[...], acc_sc[...], vec_ref, dvec_ref)
            dx_ref[...] = dxo_ref[...] + dx
            h_ref[...] = h.astype(BF16)

    row = pl.BlockSpec((tm, d), lambda i, j: (i, 0))
    hid = pl.BlockSpec((tm, tf), lambda i, j: (i, j))
    vecs = pl.BlockSpec((8, d), lambda i, j: (0, 0))
    return _call(
        body, name, (s // tm, nf),
        [
            row, row, vecs, hid, hid, row,
            pl.BlockSpec((1, d, tf), lambda i, j: (0, 0, j)),
            pl.BlockSpec((1, d, tf), lambda i, j: (0, 0, nf + j)),
            pl.BlockSpec((1, tf, d), lambda i, j: (0, j, 0)),
        ],
        [row, hid, hid, hid, row, row, vecs],
        [
            jax.ShapeDtypeStruct((s, d), F32),
            jax.ShapeDtypeStruct((s, f), BF16),
            jax.ShapeDtypeStruct((s, f), BF16),
            jax.ShapeDtypeStruct((s, f), BF16),
            jax.ShapeDtypeStruct((s, d), BF16),
            jax.ShapeDtypeStruct((s, d), BF16),
            jax.ShapeDtypeStruct((8, d), F32),
        ],
        [dxo, x, vec, gg, uu, y, w_in, w_in, w_out],
        scratch=[pltpu.VMEM((tm, d), F32)], phases=phases,
    )


def _grad_half(a, bs, big, where, mine, recv, name, phases=()):
    s, k1 = a.shape
    n = bs[0].shape[1]
    groups = len(bs)
    rows_halved = big.h3 == 1
    assert rows_halved or groups == 1
    kk, nn = (k1 // 2, n) if rows_halved else (k1, n // 2)
    tk = _pick(kk, (1408, 1024, 512, 256, 128))
    tn = _pick(nn, (1408, 1024, 640, 512, 256, 128))
    nkb, nnb = kk // tk, nn // tn
    assert (recv is None) == (not mine)

    def half(pref):
        return pref[1] if mine else 1 - pref[1]

    def body(_, a_ref, *rest):
        q = pl.program_id(1)
        for p in range(groups):

            @pl.when(q == p)
            def _(p=p):
                acc = _dot_tn(a_ref[...], rest[p][...])
                if recv is not None:
                    acc = acc + rest[groups][0].astype(F32)
                rest[-1][0] = acc.astype(BF16)

    def b_block(p):
        def index(i, q, j, pref):
            jj = jnp.where(q == p, j, jnp.where(q < p, 0, nnb - 1))
            return (0, jj + (0 if rows_halved else half(pref) * nnb))

        return pl.BlockSpec((s, tn), index)

    out_spec = pl.BlockSpec((1, tk, tn), lambda i, q, j, pref: (0, i, q * nnb + j))
    in_specs = [pl.BlockSpec((s, tk), lambda i, q, j, pref: (0, i + (half(pref) * nkb if rows_halved else 0)))]
    in_specs += [b_block(p) for p in range(groups)]
    ins = [a, *bs]
    if recv is not None:
        in_specs.append(out_spec)
        ins.append(recv)
    return _call(
        body, name, (nkb, groups, nnb), in_specs, [out_spec], [jax.ShapeDtypeStruct(big.dims("half"), BF16)], ins,
        prefetch=(where,), phases=phases,
    )


def _proj_mod_fwd(x, vec, w, phases=()):
    s, d = x.shape
    n = w.shape[2]
    tm = _pick(s, (1024, 512, 256, 128))
    tn = _pick(n, (640, 512, 256, 128))

    def body(x_ref, vec_ref, w_ref, o_ref, h_sc):
        @pl.when(pl.program_id(1) == 0)
        def _():
            h_sc[...] = _modulate(x_ref[...], vec_ref).astype(BF16)

        o_ref[...] = _dot(h_sc[...], w_ref[0])

    return _call(
        body, "ab_in_fwd", (s // tm, n // tn),
        [
            pl.BlockSpec((tm, d), lambda i, j: (i, 0)),
            pl.BlockSpec((8, d), lambda i, j: (0, 0)),
            pl.BlockSpec((1, d, tn), lambda i, j: (0, 0, j)),
        ],
        [pl.BlockSpec((tm, tn), lambda i, j: (i, j))],
        [jax.ShapeDtypeStruct((s, n), F32)], [x, vec, w],
        scratch=[pltpu.VMEM((tm, d), BF16)], phases=phases,
    )


def _proj_res_fwd(a, w, x, vec, phases=()):
    s, kd = a.shape
    d = x.shape[1]
    tm = _pick(s, (1024, 512, 256, 128))

    def body(a_ref, w_ref, x_ref, vec_ref, xo_ref, y_ref):
        yv = _dot(a_ref[...], w_ref[0])
        xo_ref[...] = x_ref[...] + vec_ref[3:4, :] * yv
        y_ref[...] = yv.astype(BF16)

    row = pl.BlockSpec((tm, d), lambda i: (i, 0))
    return _call(
        body, "ab_out_fwd", (s // tm,),
        [pl.BlockSpec((tm, kd), lambda i: (i, 0)), pl.BlockSpec((1, kd, d), lambda i: (0, 0, 0)), row, pl.BlockSpec((8, d), lambda i: (0, 0))],
        [row, row],
        [jax.ShapeDtypeStruct((s, d), F32), jax.ShapeDtypeStruct((s, d), BF16)], [a, w, x, vec], phases=phases,
    )


def _proj_res_bwd(dxo, y, vec, w, phases=()):
    s, d = dxo.shape
    kd = w.shape[1]
    tm = _pick(s, (1024, 512, 256, 128))

    def body(dxo_ref, y_ref, vec_ref, w_ref, dy_ref, da_ref, dgate_ref):
        @pl.when(pl.program_id(0) == 0)
        def _():
            dgate_ref[...] = jnp.zeros_like(dgate_ref)

        dxo_v = dxo_ref[...]
        dy = (vec_ref[3:4, :] * dxo_v).astype(BF16)
        dy_ref[...] = dy
        dgate_ref[3:4, :] += jnp.sum(dxo_v * y_ref[...].astype(F32), axis=0, keepdims=True)
        da_ref[...] = _dot_nt(dy, w_ref[0]).astype(BF16)

    row = pl.BlockSpec((tm, d), lambda i: (i, 0))
    vecs = pl.BlockSpec((8, d), lambda i: (0, 0))
    return _call(
        body, "ab_out_bwd", (s // tm,),
        [row, row, vecs, pl.BlockSpec((1, kd, d), lambda i: (0, 0, 0))],
        [row, pl.BlockSpec((tm, kd), lambda i: (i, 0)), vecs],
        [jax.ShapeDtypeStruct((s, d), BF16), jax.ShapeDtypeStruct((s, kd), BF16), jax.ShapeDtypeStruct((8, d), F32)],
        [dxo, y, vec, w], phases=phases,
    )


def _proj_mod_bwd(dproj, w, x, vec, dxo, dvec_in, name, phases=()):
    parts, s, n_part = dproj.shape
    d = x.shape[1]
    tm = _pick(s, (512, 256, 128))
    tk = _pick(n_part, (1408, 1280, 1024, 512, 256, 128))
    per_part = n_part // tk
    nk = parts * per_part

    def body(dp_ref, w_ref, x_ref, vec_ref, dxo_ref, dvi_ref, dx_ref, h_ref, dvec_ref, acc_sc):
        i, k = pl.program_id(0), pl.program_id(1)

        @pl.when((i == 0) & (k == 0))
        def _():
            dvec_ref[...] = dvi_ref[...]

        @pl.when(k == 0)
        def _():
            acc_sc[...] = jnp.zeros_like(acc_sc)

        acc_sc[...] += _dot_nt(dp_ref[0], w_ref[0])

        @pl.when(k == nk - 1)
        def _():
            dx, h = _modulate_bwd(x_ref[...], acc_sc[...], vec_ref, dvec_ref)
            dx_ref[...] = dxo_ref[...] + dx
            h_ref[...] = h.astype(BF16)

    row = pl.BlockSpec((tm, d), lambda i, k: (i, 0))
    vecs = pl.BlockSpec((8, d), lambda i, k: (0, 0))
    return _call(
        body, name, (s // tm, nk),
        [
            pl.BlockSpec((1, tm, tk), lambda i, k: (k // per_part, i, k % per_part)),
            pl.BlockSpec((1, d, tk), lambda i, k: (0, 0, k)),
            row, vecs, row, vecs,
        ],
        [row, row, vecs],
        [jax.ShapeDtypeStruct((s, d), F32), jax.ShapeDtypeStruct((s, d), BF16), jax.ShapeDtypeStruct((8, d), F32)],
        [dproj, w, x, vec, dxo, dvec_in], scratch=[pltpu.VMEM((tm, d), F32)], phases=phases,
    )


def _tril(n):
    return lax.broadcasted_iota(jnp.int32, (n, n), 0) >= lax.broadcasted_iota(jnp.int32, (n, n), 1)


def _layernorm_stats(gv):
    mu = jnp.mean(gv, axis=-1, keepdims=True)
    cen = gv - mu
    rstd = lax.rsqrt(jnp.mean(cen * cen, axis=-1, keepdims=True) + EPS)
    return cen * rstd, rstd


def _shift_down(q, k, above_ref, c_cg, c_xb, first):
    width = q.shape[1]
    rows = lax.broadcasted_iota(jnp.int32, q.shape, 0)
    out = pltpu.roll(q, k, 0)
    for r in range(k):
        src = CONV_HALO - k + r
        above = above_ref[src : src + 1, c_cg : c_cg + width] * above_ref[src : src + 1, c_xb : c_xb + width]
        above = jnp.where(first, 0.0, above)
        out = jnp.where(rows == r, above, out)
    return out


def _ab_mix_fwd(proj, norm_v, w_s, b_rows, conv_w, phases=()):
    s, n = proj.shape
    heads, chunk, _ = w_s.shape
    da = norm_v.shape[1]
    hd = da // heads
    db = conv_w.shape[1]
    tm = _pick(s, (512, 256, 128))

    def body(p_ref, ph_ref, nv_ref, ws_ref, b_ref, cw_ref, o_ref):
        first = pl.program_id(0) == 0
        gu, _ = _gelu(p_ref[:, 0:da])
        gv, _ = _gelu(p_ref[:, da : 2 * da])
        xhat, _ = _layernorm_stats(gv)
        vn = (xhat * nv_ref[...]).astype(BF16)
        mask = _tril(chunk)
        for hh in range(heads):
            wm = jnp.where(mask, ws_ref[hh], 0.0).astype(BF16)
            cols = slice(hh * hd, (hh + 1) * hd)
            for nn in range(tm // chunk):
                rows = slice(nn * chunk, (nn + 1) * chunk)
                z = _dot(wm, vn[rows, cols]) + b_ref[:, cols]
                o_ref[rows, cols] = (gu[rows, cols] * z).astype(BF16)
        c_cg, c_xb = 2 * da + db, 2 * da + 2 * db
        bg = p_ref[:, 2 * da : 2 * da + db]
        q = p_ref[:, c_cg : c_cg + db] * p_ref[:, c_xb : c_xb + db]
        q1 = _shift_down(q, 1, ph_ref, c_cg, c_xb, first)
        q2 = _shift_down(q, 2, ph_ref, c_cg, c_xb, first)
        conv = cw_ref[0:1, :] * q2 + cw_ref[1:2, :] * q1 + cw_ref[2:3, :] * q
        o_ref[:, da : da + db] = (bg * conv).astype(BF16)

    nh = tm // CONV_HALO
    return _call(
        body, "ab_mix_fwd", (s // tm,),
        [
            pl.BlockSpec((tm, n), lambda i: (i, 0)),
            pl.BlockSpec((CONV_HALO, n), lambda i: (jnp.maximum(i * nh - 1, 0), 0)),
            pl.BlockSpec((1, da), lambda i: (0, 0)),
            pl.BlockSpec((heads, chunk, chunk), lambda i: (0, 0, 0)),
            pl.BlockSpec((chunk, da), lambda i: (0, 0)),
            pl.BlockSpec((3, db), lambda i: (0, 0)),
        ],
        [pl.BlockSpec((tm, da + db), lambda i: (i, 0))],
        [jax.ShapeDtypeStruct((s, da + db), BF16)], [proj, proj, norm_v, w_s, b_rows, conv_w], phases=phases,
    )


def _ab_mix_bwd(proj, dcat, norm_v, w_s, b_rows, conv_w, phases=()):
    s, n = proj.shape
    heads, chunk, _ = w_s.shape
    da = norm_v.shape[1]
    hd = da // heads
    db = conv_w.shape[1]
    tm = _pick(s, (512, 256, 128))
    nblk = s // tm
    dhalo = 2 * CONV_HALO

    def body(p_ref, pa_ref, pb_ref, dc_ref, dcb_ref, nv_ref, ws_ref, b_ref, cw_ref,
             dp_ref, dnv_ref, dws_ref, dzs_ref, dcw_ref, dvn_sc):
        i = pl.program_id(0)
        first, last = i == 0, i == nblk - 1

        @pl.when(first)
        def _():
            dnv_ref[...] = jnp.zeros_like(dnv_ref)
            dws_ref[...] = jnp.zeros_like(dws_ref)
            dzs_ref[...] = jnp.zeros_like(dzs_ref)
            dcw_ref[...] = jnp.zeros_like(dcw_ref)

        uu = p_ref[:, 0:da]
        gu, gu_grad = _gelu(uu)
        gv, gv_grad = _gelu(p_ref[:, da : 2 * da])
        xhat, rstd = _layernorm_stats(gv)
        nv = nv_ref[...]
        vn = (xhat * nv).astype(BF16)
        dya = dc_ref[:, 0:da].astype(F32)
        dz = (dya * gu).astype(BF16)
        mask = _tril(chunk)
        for hh in range(heads):
            wm = jnp.where(mask, ws_ref[hh], 0.0).astype(BF16)
            cols = slice(hh * hd, (hh + 1) * hd)
            dws = jnp.zeros((chunk, chunk), F32)
            for nn in range(tm // chunk):
                rows = slice(nn * chunk, (nn + 1) * chunk)
                z = _dot(wm, vn[rows, cols]) + b_ref[:, cols]
                dp_ref[rows, cols] = (dya[rows, cols] * z * gu_grad[rows, cols]).astype(BF16)
                dz_blk = dz[rows, cols]
                dws = dws + _dot_nt(dz_blk, vn[rows, cols])
                dzs_ref[:, cols] += dz_blk.astype(F32)
                dvn = _dot_tn(wm, dz_blk)
                dnv_ref[:, cols] += jnp.sum(dvn * xhat[rows, cols], axis=0, keepdims=True)
                dvn_sc[rows, cols] = dvn
            dws_ref[hh] += jnp.where(mask, dws, 0.0)
        dxhat = dvn_sc[...] * nv
        dgv = rstd * (dxhat - jnp.mean(dxhat, axis=-1, keepdims=True) - xhat * jnp.mean(dxhat * xhat, axis=-1, keepdims=True))
        dp_ref[:, da : 2 * da] = (dgv * gv_grad).astype(BF16)

        c_bg, c_cg, c_xb = 2 * da, 2 * da + db, 2 * da + 2 * db
        bg = p_ref[:, c_bg : c_bg + db]
        cg = p_ref[:, c_cg : c_cg + db]
        xb = p_ref[:, c_xb : c_xb + db]
        q = cg * xb
        q1 = _shift_down(q, 1, pa_ref, c_cg, c_xb, first)
        q2 = _shift_down(q, 2, pa_ref, c_cg, c_xb, first)
        dyb = dc_ref[:, da : da + db].astype(F32)
        conv = cw_ref[0:1, :] * q2 + cw_ref[1:2, :] * q1 + cw_ref[2:3, :] * q
        dp_ref[:, c_bg : c_bg + db] = (dyb * conv).astype(BF16)
        e = dyb * bg
        dcw_ref[0:1, :] += jnp.sum(e * q2, axis=0, keepdims=True)
        dcw_ref[1:2, :] += jnp.sum(e * q1, axis=0, keepdims=True)
        dcw_ref[2:3, :] += jnp.sum(e * q, axis=0, keepdims=True)
        rows = lax.broadcasted_iota(jnp.int32, e.shape, 0)
        dq = cw_ref[2:3, :] * e
        for kk in (1, 2):
            ek = pltpu.roll(e, tm - kk, 0)
            for r in range(kk):
                below = dcb_ref[r : r + 1, da : da + db].astype(F32) * pb_ref[r : r + 1, c_bg : c_bg + db]
                below = jnp.where(last, 0.0, below)
                ek = jnp.where(rows == tm - kk + r, below, ek)
            dq = dq + cw_ref[2 - kk : 3 - kk, :] * ek
        dp_ref[:, c_cg : c_cg + db] = (dq * xb).astype(BF16)
        dp_ref[:, c_xb : c_xb + db] = (dq * cg).astype(BF16)

    nh = tm // CONV_HALO
    nhb = tm // dhalo
    const2 = lambda i: (0, 0)
    return _call(
        body, "ab_mix_bwd", (nblk,),
        [
            pl.BlockSpec((tm, n), lambda i: (i, 0)),
            pl.BlockSpec((CONV_HALO, n), lambda i: (jnp.maximum(i * nh - 1, 0), 0)),
            pl.BlockSpec((CONV_HALO, n), lambda i: (jnp.minimum((i + 1) * nh, s // CONV_HALO - 1), 0)),
            pl.BlockSpec((tm, da + db), lambda i: (i, 0)),
            pl.BlockSpec((dhalo, da + db), lambda i: (jnp.minimum((i + 1) * nhb, s // dhalo - 1), 0)),
            pl.BlockSpec((1, da), const2),
            pl.BlockSpec((heads, chunk, chunk), lambda i: (0, 0, 0)),
            pl.BlockSpec((chunk, da), const2),
            pl.BlockSpec((3, db), const2),
        ],
        [
            pl.BlockSpec((tm, n), lambda i: (i, 0)),
            pl.BlockSpec((1, da), const2),
            pl.BlockSpec((heads, chunk, chunk), lambda i: (0, 0, 0)),
            pl.BlockSpec((chunk, da), const2),
            pl.BlockSpec((3, db), const2),
        ],
        [
            jax.ShapeDtypeStruct((s, n), BF16),
            jax.ShapeDtypeStruct((1, da), F32),
            jax.ShapeDtypeStruct((heads, chunk, chunk), F32),
            jax.ShapeDtypeStruct((chunk, da), F32),
            jax.ShapeDtypeStruct((3, db), F32),
        ],
        [proj, proj, proj, dcat, dcat, norm_v, w_s, b_rows, conv_w],
        scratch=[pltpu.VMEM((tm, da), F32)], phases=phases,
    )


def _pool_counts(tm, i, w):
    t = i * tm + lax.broadcasted_iota(jnp.int32, (tm, 1), 0)
    return jnp.minimum(t + 1, w).astype(F32)


def _pool_fwd(x, vec, w_grp, scale, phases=()):
    s, d = x.shape
    groups, gd, _ = w_grp.shape
    tm = _pick(s, (512, 256, 128))

    def body(x_ref, xa_ref, vec_ref, w_ref, sc_ref, xo_ref, p_ref, o_ref):
        i = pl.program_id(0)
        h = _modulate(x_ref[...], vec_ref)
        ha = jnp.where(i == 0, 0.0, _modulate(xa_ref[...], vec_ref))
        ext = jnp.concatenate([ha, h], axis=0)
        for gi, w in enumerate(POOL_WINDOWS):
            cols = slice(gi * gd, (gi + 1) * gd)
            acc = ext[:, cols]
            step = 1
            while step < w:
                acc = acc + pltpu.roll(acc, step, 0)
                step *= 2
            p = (acc[POOL_HALO:, :] / _pool_counts(tm, i, w) - h[:, cols]).astype(BF16)
            p_ref[:, cols] = p
            o_ref[:, cols] = _dot(p, w_ref[gi]).astype(BF16)
        xo_ref[...] = x_ref[...] + vec_ref[3:4, :] * (o_ref[...].astype(F32) * sc_ref[...])

    nh = tm // POOL_HALO
    row = pl.BlockSpec((tm, d), lambda i: (i, 0))
    return _call(
        body, "pool_fwd", (s // tm,),
        [
            row,
            pl.BlockSpec((POOL_HALO, d), lambda i: (jnp.maximum(i * nh - 1, 0), 0)),
            pl.BlockSpec((8, d), lambda i: (0, 0)),
            pl.BlockSpec((groups, gd, gd), lambda i: (0, 0, 0)),
            pl.BlockSpec((1, d), lambda i: (0, 0)),
        ],
        [row, row, row],
        [jax.ShapeDtypeStruct((s, d), F32), jax.ShapeDtypeStruct((s, d), BF16), jax.ShapeDtypeStruct((s, d), BF16)],
        [x, x, vec, w_grp, scale], phases=phases,
    )


def _pool_bwd(dxo, x, vec, p, o, w_grp, scale, phases=()):
    s, d = x.shape
    groups, gd, _ = w_grp.shape
    tm = _pick(s, (512, 256, 128))
    nblk = s // tm

    def body(dxo_ref, dxb_ref, x_ref, vec_ref, p_ref, o_ref, w_ref, sc_ref, dx_ref, dw_ref, dsc_ref, dvec_ref, dw_sc):
        i = pl.program_id(0)

        @pl.when(i == 0)
        def _():
            dw_sc[...] = jnp.zeros_like(dw_sc)
            dsc_ref[...] = jnp.zeros_like(dsc_ref)
            dvec_ref[...] = jnp.zeros_like(dvec_ref)

        gate, sc = vec_ref[3:4, :], sc_ref[...]
        dxo_v = dxo_ref[...]
        ov = o_ref[...].astype(F32)
        dvec_ref[3:4, :] += jnp.sum(dxo_v * (ov * sc), axis=0, keepdims=True)
        dy = gate * dxo_v
        dsc_ref[...] += jnp.sum(dy * ov, axis=0, keepdims=True)
        dout = (dy * sc).astype(BF16)
        dout_b = jnp.where(i == nblk - 1, 0.0, gate * dxb_ref[...] * sc).astype(BF16)
        for gi, w in enumerate(POOL_WINDOWS):
            cols = slice(gi * gd, (gi + 1) * gd)
            dw_sc[gi] += _dot_tn(p_ref[:, cols], dout[:, cols])
            wb = w_ref[gi]
            dp = _dot_nt(dout[:, cols], wb)
            dp_b = _dot_nt(dout_b[:, cols], wb)
            e = dp / _pool_counts(tm, i, w)
            t_below = (i + 1) * tm + lax.broadcasted_iota(jnp.int32, (POOL_HALO, 1), 0)
            e_b = dp_b / jnp.minimum(t_below + 1, w).astype(F32)
            acc = jnp.concatenate([e, e_b], axis=0)
            step = 1
            while step < w:
                acc = acc + pltpu.roll(acc, tm + POOL_HALO - step, 0)
                step *= 2
            dx_ref[:, cols] = acc[:tm, :] - dp
        dx, _ = _modulate_bwd(x_ref[...], dx_ref[...], vec_ref, dvec_ref)
        dx_ref[...] = dxo_v + dx

        @pl.when(i == nblk - 1)
        def _():
            dw_ref[...] = dw_sc[...].astype(BF16)

    nh = tm // POOL_HALO
    row = pl.BlockSpec((tm, d), lambda i: (i, 0))
    vecs = pl.BlockSpec((8, d), lambda i: (0, 0))
    wspec = pl.BlockSpec((groups, gd, gd), lambda i: (0, 0, 0))
    return _call(
        body, "pool_bwd", (nblk,),
        [
            row,
            pl.BlockSpec((POOL_HALO, d), lambda i: (jnp.minimum((i + 1) * nh, s // POOL_HALO - 1), 0)),
            row, vecs, row, row, wspec,
            pl.BlockSpec((1, d), lambda i: (0, 0)),
        ],
        [row, wspec, pl.BlockSpec((1, d), lambda i: (0, 0)), vecs],
        [
            jax.ShapeDtypeStruct((s, d), F32),
            jax.ShapeDtypeStruct((groups, gd, gd), BF16),
            jax.ShapeDtypeStruct((1, d), F32),
            jax.ShapeDtypeStruct((8, d), F32),
        ],
        [dxo, dxo, x, vec, p, o, w_grp, scale],
        scratch=[pltpu.VMEM((groups, gd, gd), F32)], phases=phases,
    )


def _loss_head(x, gain, target, phases=()):
    s, d = x.shape
    tm = _pick(s, (512, 256, 128))

    def body(x_ref, g_ref, t_ref, dx_ref, aux_ref):
        @pl.when(pl.program_id(0) == 0)
        def _():
            aux_ref[...] = jnp.zeros_like(aux_ref)

        xv = x_ref[...]
        rstd = _rstd(xv)
        r = xv * rstd
        gain_v = g_ref[...]
        err = r * gain_v - t_ref[...]
        aux_ref[1:2, :] += jnp.sum(err * err, axis=0, keepdims=True)
        dout = err * (1.0 / d)
        aux_ref[0:1, :] += jnp.sum(dout * r, axis=0, keepdims=True)
        dr = dout * gain_v
        dx_ref[...] = rstd * (dr - r * jnp.mean(dr * r, axis=-1, keepdims=True))

    row = pl.BlockSpec((tm, d), lambda i: (i, 0))
    return _call(
        body, "loss_head", (s // tm,),
        [row, pl.BlockSpec((1, d), lambda i: (0, 0)), row],
        [row, pl.BlockSpec((8, d), lambda i: (0, 0))],
        [jax.ShapeDtypeStruct((s, d), F32), jax.ShapeDtypeStruct((8, d), F32)], [x, gain, target], phases=phases,
    )


def _small_adam(gathered, gathered_ws, layout, smalls, chip):
    names = list(smalls)
    n = len(names)
    loss_row, _, _, n_feat = layout["loss"]

    def body(*refs):
        chip_ref, g_ref, gws_ref = refs[0], refs[1], refs[2]
        wmv = refs[3 : 3 + 3 * n]
        outs = refs[3 + 3 * n : 3 + 7 * n]
        total = refs[-1]
        total[...] = g_ref[0]
        for kdev in range(1, N_DEV):
            total[...] += g_ref[kdev]
        total_ws = gws_ref[0]
        for kdev in range(1, N_DEV):
            total_ws = total_ws + gws_ref[kdev]
        my_chip = chip_ref[0]
        for a, name in enumerate(names):
            w_ref, m_ref, v_ref = wmv[3 * a : 3 * a + 3]
            if name == "ab_w_s":
                g = total_ws
            else:
                row0, rows, col0, cols = layout[name]
                if col0 is None:
                    g = jnp.zeros((rows, cols), F32)
                    for j in range(N_CHIPS):
                        g = g + jnp.where(my_chip == j, total[row0 : row0 + rows, j * cols : (j + 1) * cols], 0.0)
                else:
                    g = total[row0 : row0 + rows, col0 : col0 + cols]
            dl, mo, vo = _adam(w_ref[...], g, m_ref[...], v_ref[...])
            outs[4 * a][...] = g
            outs[4 * a + 1][...] = dl
            outs[4 * a + 2][...] = mo
            outs[4 * a + 3][...] = vo
        refs[3 + 7 * n][...] = 0.5 * jnp.sum(total[loss_row : loss_row + 1, 0:n_feat], axis=1, keepdims=True) / n_feat

    ins = [gathered, gathered_ws]
    out_shapes = []
    for name in names:
        ins.extend(smalls[name])
        out_shapes.extend([jax.ShapeDtypeStruct(smalls[name][0].shape, F32)] * 4)
    out_shapes.append(jax.ShapeDtypeStruct((1, 1), F32))
    whole = lambda shape: pl.BlockSpec(shape, functools.partial(lambda nd, i, c: (0,) * nd, len(shape)))
    res = pl.pallas_call(
        body, name="small_adam",
        grid_spec=pltpu.PrefetchScalarGridSpec(
            num_scalar_prefetch=1, grid=(1,),
            in_specs=[whole(a.shape) for a in ins], out_specs=[whole(o.shape) for o in out_shapes],
            scratch_shapes=[pltpu.VMEM(gathered.shape[1:], F32)],
        ),
        out_shape=out_shapes,
        compiler_params=pltpu.CompilerParams(dimension_semantics=("arbitrary",), vmem_limit_bytes=VMEM_LIMIT_BYTES),
    )(chip.reshape(1).astype(jnp.int32), *ins)
    return {name: res[4 * a : 4 * a + 4] for a, name in enumerate(names)}, res[4 * n]


def _pad_rows(a, rows=8):
    extra = (-a.shape[0]) % rows
    return jnp.pad(a, ((0, extra), (0, 0))) if extra else a


def _pad_cols(a, cols):
    return jnp.pad(a, ((0, 0), (0, cols - a.shape[1]))) if a.shape[1] < cols else a


def _run(fn, *phases):
    outs, p_outs = fn(list(phases))
    for p, po in zip(phases, p_outs):
        p.then(po)
    return outs


def kernel(x, c, norm_g, w_mod, b_mod, w_ffn_in, w_ffn_out, ab_w_in, ab_norm_v, ab_w_s, ab_b_s, ab_conv_w, ab_w_out, pool_w_grp, pool_scale, final_g, loss_target, m_norm_g, m_w_mod, m_b_mod, m_w_ffn_in, m_w_ffn_out, m_ab_w_in, m_ab_norm_v, m_ab_w_s, m_ab_b_s, m_ab_conv_w, m_ab_w_out, m_pool_w_grp, m_pool_scale, m_final_g, v_norm_g, v_w_mod, v_b_mod, v_w_ffn_in, v_w_ffn_out, v_ab_w_in, v_ab_norm_v, v_ab_w_s, v_ab_b_s, v_ab_conv_w, v_ab_w_out, v_pool_w_grp, v_pool_scale, v_final_g):
    ix, iy, ic = _place()
    chip = 2 * ix + iy
    me = 4 * ix + 2 * iy + ic
    where = jnp.stack([chip, ic]).astype(jnp.int32)
    s, d = x.shape[1], x.shape[2]
    x0 = x.reshape(s, d)
    target = loss_target.reshape(s, d)
    n_layers = norm_g.shape[0]
    dq = d // N_CHIPS
    heads, chunk = ab_w_s.shape[1], ab_w_s.shape[2]
    da = ab_norm_v.shape[1]
    db = ab_conv_w.shape[2] * N_CHIPS
    f_hidden = w_ffn_out.shape[2] * N_CHIPS
    assert n_layers == 2 and da % heads == 0

    cw_pad = _pad_cols(ab_conv_w.reshape(3, db // N_CHIPS), dq)
    packed = jnp.concatenate(
        [_pad_rows(c.reshape(N_CHIPS, dq)), _pad_rows(norm_g.reshape(-1, dq)), _pad_rows(pool_scale.reshape(1, dq)), _pad_rows(cw_pad)],
        axis=0,
    )
    ncol = w_mod.shape[2]
    b_cols = lax.dynamic_slice(b_mod, (0, chip * ncol), (n_layers, ncol)).reshape(n_layers, 1, ncol)
    small = {}

    def small_gather(key, arrs):
        def then(outs):
            small[key] = outs

        return _phase_small_gather(arrs, then)

    stacks = {
        "w_ffn_in": tuple(a.reshape((-1,) + a.shape[2:]) for a in (w_ffn_in, m_w_ffn_in, v_w_ffn_in)),
        "w_ffn_out": tuple(a.reshape((-1,) + a.shape[2:]) for a in (w_ffn_out, m_w_ffn_out, v_w_ffn_out)),
        "ab_w_in": (ab_w_in, m_ab_w_in, v_ab_w_in),
        "ab_w_out": (ab_w_out, m_ab_w_out, v_ab_w_out),
        "pool_w_grp": (pool_w_grp[0], m_pool_w_grp[0], v_pool_w_grp[0]),
    }
    big_in = _Big((1, d, 2 * f_hidden), 2, 1)
    big_out = _Big((1, f_hidden, d), 1, 2)
    units = {}
    for l in range(n_layers):
        for k in range(2):
            units[f"in{l}{k}"] = (big_in, "w_ffn_in", 2 * l + k)
            units[f"out{l}{k}"] = (big_out, "w_ffn_out", 2 * l + k)
    units["abin"] = (_Big((1, d, ab_w_in.shape[2] * N_CHIPS), 2, 1), "ab_w_in", 0)
    units["about"] = (_Big((1, ab_w_out.shape[1] * N_CHIPS, d), 1, 2), "ab_w_out", 0)
    units["pool"] = (_Big((pool_w_grp.shape[1], pool_w_grp.shape[2] * N_CHIPS, pool_w_grp.shape[3]), 1, 0), "pool_w_grp", 0)
    big = {u: g for u, (g, _, _) in units.items()}

    weight = {}
    complete = set()

    def cast(u):
        g, st, b0 = units[u]

        def launch(phases):
            (weight[u],), p_outs = _cast_into_full(stacks[st][0], b0, g, where, "cast_" + u, phases)
            return None, p_outs

        return launch

    def gather_relay(us, second, whole_first):
        def then(outs):
            for u, o in zip(us, outs):
                weight[u] = o

        return _phase_gather_relay([weight[u] for u in us], [big[u] for u in us], second, whole_first, then)

    def gather_sibling(*us):
        def then(outs):
            for u, o in zip(us, outs):
                weight[u] = o
                complete.add(u)

        return _phase_gather_sibling([weight[u] for u in us], [big[u] for u in us], then)

    def w_of(u):
        assert u in complete, u
        return weight[u]

    _run(cast("in00"), small_gather("inputs", [packed]))
    small_all = small["inputs"][0]
    by_chip = small_all[0::2]
    c_all = small_all[:, 0:N_CHIPS, :].reshape(N_DEV, d)
    norm_full = by_chip[:, 8 : 8 + 3 * n_layers, :].transpose(1, 0, 2).reshape(3 * n_layers, d)
    pool_scale_full = by_chip[:, 16:17, :].transpose(1, 0, 2).reshape(1, d)
    conv_full = by_chip[:, 24:27, : db // N_CHIPS].transpose(1, 0, 2).reshape(3, db)
    pieces = [("in00", "out00"), ("abin", "about"), ("in01", "out01"), ("in10", "out10", "pool"), ("in11", "out11")]
    in_flight = {}

    def start_gather(p):
        in_flight[p, 0] = _split_start(gather_relay(pieces[p], False, p == 0), f"gather_{p}_start")

    def relay_gather(p, after=()):
        flight = in_flight.pop((p, 0))
        _split_wait(flight, list(after) + list(started().ins), f"gather_{p}_arrived")
        in_flight[p, 1] = _split_start(gather_relay(pieces[p], True, p == 0), f"gather_{p}_relay")

    def started():
        return _after(*[flight.token for flight in in_flight.values()])

    def finish_gather(p, after, meanwhile=None):
        flight = in_flight.pop((p, 1))
        _split_wait(flight, list(after) + list(started().ins), f"gather_{p}_wait")
        crossing = _split_start(gather_sibling(*pieces[p]), f"gather_{p}_forward")
        behind = [crossing.token]
        if p + 1 < len(pieces):
            relay_gather(p + 1)
        if p + 2 < len(pieces):
            start_gather(p + 2)
        behind = behind + list(started().ins)
        if meanwhile is not None:
            behind = behind + meanwhile(_after(crossing.token))
        _split_wait(crossing, behind, f"gather_{p}_forwarded")

    _run(cast("out00"))
    start_gather(0)
    mod_cols = _run(lambda phases: _mod_fwd(c_all, w_mod, b_cols, phases), started())[0]

    def mod_rows(outs):
        small["mod"] = outs

    for piece in pieces[2:]:
        for u in piece:
            _run(cast(u), started())
    _run(cast("about"), started())
    _run(cast("abin"), _phase_small_exchange(mod_cols.transpose(1, 0, 2), mod_rows), started())
    relay_gather(0, [weight[u] for piece in pieces[2:] for u in piece])
    start_gather(1)
    mod_mine = small["mod"][0][0::2]
    mod = mod_mine.transpose(1, 0, 2).reshape(n_layers, 3, 3, d)
    vecs = {
        (l, sub): jnp.pad(norm_full[3 * l + sub][None], ((0, 7), (0, 0))) + jnp.pad(mod[l, sub], ((1, 4), (0, 0)))
        for l in range(n_layers)
        for sub in range(3)
    }
    b_rows = jnp.broadcast_to(ab_b_s[0].T[:, :, None], (chunk, heads, da // heads)).reshape(chunk, da)

    saved = {}

    def ffn_forward(xs, l, sub, k, *phases):
        saved[l, sub, "x"] = xs
        xs, gg, uu, yb = _run(
            lambda ph: _ffn_fwd(xs, vecs[l, sub], w_of(f"in{l}{k}"), w_of(f"out{l}{k}"), f"ffn_fwd_{l}{k}", ph), *phases
        )
        saved[l, sub, "act"] = (gg, uu, yb)
        return xs

    finish_gather(0, [vecs[0, 0]])
    xs = ffn_forward(x0, 0, 0, 0, started())
    saved[0, 1, "x"] = xs
    finish_gather(1, [xs])
    (proj,) = _run(lambda ph: _proj_mod_fwd(xs, vecs[0, 1], w_of("abin"), ph), started())
    (cat,) = _run(lambda ph: _ab_mix_fwd(proj, ab_norm_v, ab_w_s[0], b_rows, conv_full, ph))
    xs, yb = _run(lambda ph: _proj_res_fwd(cat, w_of("about"), xs, vecs[0, 1], ph))
    saved[0, 1, "act"] = (proj, cat, yb)
    finish_gather(2, [xs])
    xs = ffn_forward(xs, 0, 2, 1, started())
    finish_gather(3, [xs])
    xs = ffn_forward(xs, 1, 0, 0, started())
    saved[1, 1, "x"] = xs
    pooled = []

    def pool_forward(behind):
        pooled.extend(_run(lambda ph: _pool_fwd(xs, vecs[1, 1], w_of("pool"), pool_scale_full, ph), behind))
        return [pooled[0]]

    finish_gather(4, [xs], pool_forward)
    xs, pp, oo = pooled
    saved[1, 1, "act"] = (pp, oo)
    xs = ffn_forward(xs, 1, 2, 1)
    dxs, aux = _run(lambda ph: _loss_head(xs, final_g.reshape(1, d), target, ph))

    grad = {}
    recv = {}
    csum = {}
    parts = {}
    reduced = {}
    done = set()
    dvecs, small_g = {}, {}

    def pair_exchange(*us):
        def then(outs):
            for u, o in zip(us, outs):
                recv[u] = o

        return _phase_pair_exchange([grad[u] for u in us], [big[u] for u in us], then)

    def grad_half(u, a, bs, mine, name, *phases):
        (res,) = _run(lambda ph: _grad_half(a, bs, big[u], where, mine, recv[u] if mine else None, name, ph), *phases)
        return res

    def pair_sum(u, *phases):
        def launch(ph):
            (csum[u],), p_outs = _pair_sum(grad[u], recv[u], big[u], where, "pair_sum_" + u, ph)
            return None, p_outs

        _run(launch, *phases)

    def chip_exchange(*us):
        def then(outs):
            for u, o in zip(us, outs):
                parts[u] = o

        return _phase_chip_exchange([csum[u] for u in us], [big[u] for u in us], then)

    def chip_sum(*us, carried=()):
        for n_u, u in enumerate(us):
            g, st, b0 = units[u]

            def launch(ph):
                (reduced[st],), p_outs = _chip_sum(
                    csum[u], parts[u], g, where, reduced.get(st), stacks[st][0].shape, b0, "chip_sum_" + u, ph
                )
                return None, p_outs

            _run(launch, *(carried if n_u == 0 else ()))

    def pair_broadcast(*us):
        sts = [units[u][1] for u in us]
        assert len(set(sts)) == len(sts)

        def then(outs):
            for u, st, o in zip(us, sts, outs):
                reduced[st] = o
                done.add(u)

        return _phase_pair_broadcast([reduced[st] for st in sts], [big[u] for u in us], [units[u][2] for u in us], then)

    def ffn_backward(dxs, l, sub, k, carried_bwd, carried_send, carried_mine):
        gg, uu, yb = saved[l, sub, "act"]
        w_in, w_out = w_of(f"in{l}{k}"), w_of(f"out{l}{k}")
        uo, ui, tag = f"out{l}{k}", f"in{l}{k}", f"{l}{k}"
        dxs, dg, du, a, h, dy, dvecs[l, sub] = _run(
            lambda ph: _ffn_bwd(dxs, saved[l, sub, "x"], vecs[l, sub], gg, uu, yb, w_in, w_out, "ffn_bwd_" + tag, ph), *carried_bwd()
        )
        grad[uo] = grad_half(uo, a, [dy], False, "dw_out_send_" + tag, *carried_send())
        grad[ui] = grad_half(ui, h, [dg, du], False, "dw_in_send_" + tag, pair_exchange(uo))
        csum[uo] = grad_half(uo, a, [dy], True, "dw_out_" + tag, pair_exchange(ui))
        csum[ui] = grad_half(ui, h, [dg, du], True, "dw_in_" + tag, *carried_mine())
        return dxs

    none = lambda: ()
    dxs = ffn_backward(dxs, 1, 2, 1, none, none, none)
    pp, oo = saved[1, 1, "act"]
    dxs, grad["pool"], small_g["pool_scale"], dvecs[1, 1] = _run(
        lambda ph: _pool_bwd(dxs, saved[1, 1, "x"], vecs[1, 1], pp, oo, w_of("pool"), pool_scale_full, ph)
    )

    def after_11():
        return (chip_exchange("in11", "out11"), pair_exchange("pool"))

    def bcast_11():
        chip_sum("in11", "out11")
        pair_sum("pool")
        return (pair_broadcast("in11", "out11"), chip_exchange("pool"))

    dxs = ffn_backward(dxs, 1, 0, 0, after_11, bcast_11, none)

    def after_10():
        return (chip_exchange("in10", "out10"),)

    def bcast_10():
        chip_sum("in10", "out10", "pool")
        return (pair_broadcast("in10", "out10", "pool"),)

    dxs = ffn_backward(dxs, 0, 2, 1, after_10, bcast_10, none)

    proj, cat, yb = saved[0, 1, "act"]
    out01 = _split_start(chip_exchange("out01"), "reduce_out01_start")
    dy, dcat, dgate = _run(lambda ph: _proj_res_bwd(dxs, yb, vecs[0, 1], w_of("about"), ph), _after(out01.token))
    grad["about"] = grad_half("about", cat, [dy], False, "dw_ab_out_send")
    dproj, small_g["ab_norm_v"], small_g["ab_w_s"], dzs, small_g["ab_conv_w"] = _run(
        lambda ph: _ab_mix_bwd(proj, dcat, ab_norm_v, ab_w_s[0], b_rows, conv_full, ph), pair_exchange("about")
    )
    small_g["ab_b_s"] = dzs.reshape(chunk, heads, da // heads).sum(axis=2).T
    dxs, h, dvecs[0, 1] = _run(
        lambda ph: _proj_mod_bwd(dproj[None], w_of("abin"), saved[0, 1, "x"], vecs[0, 1], dxs, dgate, "ab_in_bwd", ph)
    )
    grad["abin"] = grad_half("abin", h, [dproj], False, "dw_ab_in_send")
    (csum["out01"],) = _split_wait(out01, [grad["abin"]], "reduce_out01_wait")
    chip_sum("out01", carried=(pair_exchange("abin"),))
    csum["about"] = grad_half("about", cat, [dy], True, "dw_ab_out", pair_broadcast("out01"))
    csum["abin"] = grad_half("abin", h, [dproj], True, "dw_ab_in")

    layout = {}
    tail = {}

    def after_01():
        tail["01"] = _split_start(chip_exchange("in01", "abin", "about"), "reduce_01_start")
        return (_after(tail["01"].token),)

    def pack_small_grads():
        dvec_all = jnp.stack([dvecs[l, sub] for l in range(n_layers) for sub in range(3)])
        dgain = dvec_all[:, 0, :]
        dmod = dvec_all[:, 1:4, :].reshape(3 * 3 * n_layers, d)
        rows = {
            "norm_g": (dgain, None, dq), "final_g": (aux[0:1], 0, d), "pool_scale": (small_g["pool_scale"], None, dq),
            "b_mod": (dmod, 0, d), "ab_norm_v": (small_g["ab_norm_v"], 0, da),
            "ab_conv_w": (small_g["ab_conv_w"], None, db // N_CHIPS), "ab_b_s": (small_g["ab_b_s"], 0, chunk),
            "loss": (aux[1:2], 0, d),
        }
        row0 = 0
        for nm, (pc, col0, cols) in rows.items():
            layout[nm] = (row0, pc.shape[0], col0, cols)
            row0 += pc.shape[0]
        packed_rows = -(-row0 // 8) * 8
        return sum(
            jnp.pad(pc, ((layout[nm][0], packed_rows - layout[nm][0] - pc.shape[0]), (0, d - pc.shape[1])))
            for nm, (pc, _, _) in rows.items()
        )

    def bcast_01():
        csum["in01"], csum["abin"], csum["about"] = _split_wait(tail["01"], [dvecs[0, 0]], "reduce_01_wait")
        chip_sum("in01", "abin", "about")
        grads_small = [pack_small_grads(), small_g["ab_w_s"].reshape(heads * chunk, chunk)]
        tail["small"] = _split_start(small_gather("grads", grads_small), "gather_small_grads_start")
        return (pair_broadcast("in01", "abin", "about"), _after(tail["small"].token))

    def reduce_out00():
        tail["out00"] = _split_start(chip_exchange("out00"), "reduce_out00_start")
        return (_after(tail["out00"].token),)

    dxs = ffn_backward(dxs, 0, 0, 0, after_01, bcast_01, reduce_out00)
    grad_x = dxs.reshape(x.shape)

    last = _split_start(chip_exchange("in00"), "reduce_last_start")
    (csum["out00"],) = _split_wait(tail["out00"], [last.token], "reduce_out00_wait")
    chip_sum("out00")
    _flush("broadcast_out00", pair_broadcast("out00"))
    _split_wait(tail["small"], [reduced["w_ffn_out"]], "gather_small_grads_wait")
    g_all, gws_all = small["grads"]

    out = {}

    def adam_stack(st, after=()):
        w3, m3, v3 = stacks[st]
        assert all(u in done for u, (_, ust, _) in units.items() if ust == st), st
        shape = {"w_ffn_in": w_ffn_in.shape, "w_ffn_out": w_ffn_out.shape, "pool_w_grp": pool_w_grp.shape}.get(st, w3.shape)
        out[st] = tuple(a.reshape(shape) for a in _adam_stack(w3, reduced[st], m3, v3, "adam_" + st, after))

    for st in ("w_ffn_out", "ab_w_in", "ab_w_out", "pool_w_grp"):
        adam_stack(st, (last.token,))

    shapes2d = {
        "norm_g": (3 * n_layers, dq), "b_mod": (9 * n_layers, d), "final_g": (1, d), "ab_norm_v": (1, da),
        "pool_scale": (1, dq), "ab_conv_w": (3, db // N_CHIPS), "ab_b_s": (heads, chunk), "ab_w_s": (heads * chunk, chunk),
    }
    small_w = {"norm_g": (norm_g, m_norm_g, v_norm_g), "b_mod": (b_mod, m_b_mod, v_b_mod), "final_g": (final_g, m_final_g, v_final_g),
               "ab_norm_v": (ab_norm_v, m_ab_norm_v, v_ab_norm_v), "pool_scale": (pool_scale, m_pool_scale, v_pool_scale),
               "ab_conv_w": (ab_conv_w, m_ab_conv_w, v_ab_conv_w), "ab_b_s": (ab_b_s, m_ab_b_s, v_ab_b_s), "ab_w_s": (ab_w_s, m_ab_w_s, v_ab_w_s)}
    smalls = {nm: tuple(a.reshape(shapes2d[nm]) for a in wmv) for nm, wmv in small_w.items()}
    small_out, loss = _small_adam(g_all, gws_all, layout, smalls, chip)
    loss = loss.reshape(())
    for nm, res in small_out.items():
        out[nm] = tuple(a.reshape(small_w[nm][0].shape) for a in res)

    mod_row0 = layout["b_mod"][0]
    dmod_all = g_all[:, mod_row0 : mod_row0 + 9 * n_layers, :].reshape(N_DEV, n_layers, 9 * d)
    dmod_cols = lax.dynamic_slice(dmod_all, (0, 0, chip * ncol), (N_DEV, n_layers, ncol)).transpose(1, 0, 2)
    out["w_mod"] = tuple(_mod_bwd_adam(c_all.T, dmod_cols, w_mod, m_w_mod, v_w_mod, (last.token,)))

    (csum["in00"],) = _split_wait(
        last, [out[st][1] for st in ("w_mod", "w_ffn_out", "ab_w_in", "ab_w_out", "pool_w_grp")], "reduce_last_wait"
    )
    chip_sum("in00")
    _flush("broadcast_last", pair_broadcast("in00"))
    adam_stack("w_ffn_in")

    order = ["norm_g", "w_mod", "b_mod", "w_ffn_in", "w_ffn_out", "ab_w_in", "ab_norm_v", "ab_w_s", "ab_b_s", "ab_conv_w", "ab_w_out", "pool_w_grp", "pool_scale", "final_g"]
    return (loss, grad_x, *[out[nm][0] for nm in order], *[out[nm][1] for nm in order], *[out[nm][2] for nm in order], *[out[nm][3] for nm in order])
```

```python
import functools
import math

import jax
import jax.numpy as jnp
from jax import lax
from jax.experimental import pallas as pl
from jax.experimental.pallas import tpu as pltpu

F32 = jnp.float32
BF16 = jnp.bfloat16
MESH = pl.DeviceIdType.MESH

EPS = 1e-6
ADAM_LR = 0.001
ADAM_B1 = 0.9
ADAM_B2 = 0.999
ADAM_EPS = 1e-08
ADAM_WD = 0.01
ADAM_STEP = 10
POOL_WINDOWS = (2, 4, 8, 16)
POOL_HALO = 16
CONV_HALO = 8
N_CHIPS = 4
N_DEV = 8
VMEM_LIMIT_BYTES = 48 * 1024 * 1024
EW_BLOCK_ELEMS = 1024 * 1024
ADAM_BLOCK_ELEMS = 512 * 1024


def _pick(n, prefs):
    for p in prefs:
        if p <= n and n % p == 0:
            return p
    return n


def _row_tile(rows, cols, block_elems=EW_BLOCK_ELEMS):
    best = None
    for d in range(16, rows + 1, 16):
        if rows % d == 0 and d * cols <= block_elems:
            best = d
    return best or rows


def _dot(a, b):
    return jnp.dot(a, b, preferred_element_type=F32)


def _dot_nt(a, b):
    return lax.dot_general(a, b, (((1,), (1,)), ((), ())), preferred_element_type=F32)


def _dot_tn(a, b):
    return lax.dot_general(a, b, (((0,), (0,)), ((), ())), preferred_element_type=F32)


def _sigmoid(x):
    return 0.5 * jnp.tanh(0.5 * x) + 0.5


_GELU_C = math.sqrt(2.0 / math.pi)


def _gelu(x):
    x2 = x * x
    t = jnp.tanh(_GELU_C * (x + 0.044715 * x2 * x))
    val = 0.5 * x * (1.0 + t)
    grad = 0.5 * (1.0 + t) + 0.5 * x * (1.0 - t * t) * (_GELU_C * (1.0 + 3.0 * 0.044715 * x2))
    return val, grad


def _rstd(x):
    return lax.rsqrt(jnp.mean(x * x, axis=-1, keepdims=True) + EPS)


def _modulate(x, vec_ref):
    return (x * _rstd(x)) * vec_ref[0:1, :] * (1.0 + vec_ref[2:3, :]) + vec_ref[1:2, :]


def _modulate_bwd(x, dh, vec_ref, dvec_ref):
    gn, sh, sc = vec_ref[0:1, :], vec_ref[1:2, :], vec_ref[2:3, :]
    rstd = _rstd(x)
    r = x * rstd
    dvec_ref[0:1, :] += jnp.sum(dh * r * (1.0 + sc), axis=0, keepdims=True)
    dvec_ref[1:2, :] += jnp.sum(dh, axis=0, keepdims=True)
    dvec_ref[2:3, :] += jnp.sum(dh * r * gn, axis=0, keepdims=True)
    gm = gn * (1.0 + sc)
    dr = dh * gm
    dx = rstd * (dr - r * jnp.mean(dr * r, axis=-1, keepdims=True))
    return dx, r * gm + sh


def _adam(w, g, m, v):
    m = ADAM_B1 * m + (1.0 - ADAM_B1) * g
    v = ADAM_B2 * v + (1.0 - ADAM_B2) * (g * g)
    m_hat = m / (1.0 - ADAM_B1**ADAM_STEP)
    v_hat = v / (1.0 - ADAM_B2**ADAM_STEP)
    delta = -ADAM_LR * (m_hat / (jnp.sqrt(v_hat) + ADAM_EPS) + ADAM_WD * w)
    return delta, m, v


_ANY = pl.BlockSpec(memory_space=pl.ANY)


class _Phase:
    def __init__(self, ins, out_shapes, aliases, n_sems, start, finish, then):
        self.ins, self.out_shapes, self.aliases, self.n_sems = list(ins), list(out_shapes), dict(aliases), n_sems
        self.start, self.finish, self.then = start, finish, then


def _call(body, name, grid, in_specs, out_specs, out_shape, ins, scratch=(), prefetch=(), phases=(), in_place=None):
    n_pre, n_in, n_out, n_sc = len(prefetch), len(in_specs), len(out_specs), len(scratch)
    ph_in = [len(p.ins) for p in phases]
    ph_out = [len(p.out_shapes) for p in phases]

    def kernel_body(*refs):
        pos = [0]

        def take(k):
            pos[0] += k
            return refs[pos[0] - k : pos[0]]

        pre, ins_ = take(n_pre), take(n_in)
        p_ins = [take(k) for k in ph_in]
        outs_ = take(n_out)
        p_outs = [take(k) for k in ph_out]
        sc = take(n_sc)
        sems = [take(2) for _ in phases]
        if phases:
            ids = [pl.program_id(a) for a in range(len(grid))]
            first = functools.reduce(jnp.logical_and, [i == 0 for i in ids])
            last = functools.reduce(jnp.logical_and, [i == g - 1 for i, g in zip(ids, grid)])

            @pl.when(first)
            def _():
                for p, pi, po, (send, recv) in zip(phases, p_ins, p_outs, sems):
                    p.start(pi, po, send, recv)

        if body is not None:
            body(*pre, *ins_, *outs_, *sc)
        if phases:

            @pl.when(last)
            def _():
                for p, pi, po, (send, recv) in zip(phases, p_ins, p_outs, sems):
                    p.finish(pi, po, send, recv)

    aliases = {n_pre + i: o for i, o in (in_place or {}).items()}
    i0, o0 = n_pre + n_in, n_out
    for p in phases:
        for i, o in p.aliases.items():
            aliases[i0 + i] = o0 + o
        i0 += len(p.ins)
        o0 += len(p.out_shapes)
    all_in = list(in_specs) + [_ANY] * sum(ph_in)
    all_out = list(out_specs) + [_ANY] * sum(ph_out)
    all_scratch = list(scratch)
    for p in phases:
        all_scratch += [pltpu.SemaphoreType.DMA((p.n_sems,)), pltpu.SemaphoreType.DMA((p.n_sems,))]
    shapes = list(out_shape) + [s for p in phases for s in p.out_shapes]
    operands = list(prefetch) + list(ins) + [a for p in phases for a in p.ins]
    sem = ("arbitrary",) * len(grid)
    params = pltpu.CompilerParams(dimension_semantics=sem, vmem_limit_bytes=VMEM_LIMIT_BYTES)
    if n_pre:
        res = pl.pallas_call(
            kernel_body, name=name, out_shape=shapes, input_output_aliases=aliases, compiler_params=params,
            grid_spec=pltpu.PrefetchScalarGridSpec(
                num_scalar_prefetch=n_pre, grid=grid, in_specs=all_in, out_specs=all_out, scratch_shapes=all_scratch
            ),
        )(*operands)
    else:
        res = pl.pallas_call(
            kernel_body, name=name, grid=grid, in_specs=all_in, out_specs=all_out, out_shape=shapes,
            scratch_shapes=all_scratch, input_output_aliases=aliases, compiler_params=params,
        )(*operands)
    res = list(res)
    outs, rest = res[:n_out], res[n_out:]
    p_res = []
    for k in ph_out:
        p_res.append(rest[:k])
        rest = rest[k:]
    return outs, p_res


def _place():
    return lax.axis_index("x"), lax.axis_index("y"), lax.axis_index("c")


def _other_chips():
    x, y, _ = _place()
    return [(1 - x, y), (x, 1 - y), (1 - x, 1 - y)]


def _flip(k):
    x, y, c = _place()
    return (1 - x if k & 4 else x, 1 - y if k & 2 else y, 1 - c if k & 1 else c)


def _remote(src, dst, send, recv, k, to):
    return pltpu.make_async_remote_copy(
        src_ref=src, dst_ref=dst, send_sem=send.at[k], recv_sem=recv.at[k], device_id=to, device_id_type=MESH
    )


def _phase_small_gather(arrs, then):
    n = len(arrs)

    def copies(ins, outs, send, recv):
        x, y, c = _place()
        me = 4 * x + 2 * y + c
        local = [pltpu.make_async_copy(ins[a], outs[a].at[me], send.at[a * N_DEV]) for a in range(n)]
        remote = [_remote(ins[a], outs[a].at[me], send, recv, a * N_DEV + k, _flip(k)) for a in range(n) for k in range(1, N_DEV)]
        return local, remote

    def start(ins, outs, send, recv):
        local, remote = copies(ins, outs, send, recv)
        for cp in local + remote:
            cp.start()

    def finish(ins, outs, send, recv):
        local, remote = copies(ins, outs, send, recv)
        for cp in remote + local:
            cp.wait()

    shapes = [jax.ShapeDtypeStruct((N_DEV,) + a.shape, a.dtype) for a in arrs]
    return _Phase(arrs, shapes, {}, n * N_DEV, start, finish, then)


def _phase_small_exchange(arr, then):
    def copies(ins, outs, send, recv):
        x, y, c = _place()
        me = 4 * x + 2 * y + c
        local = pltpu.make_async_copy(ins[0].at[me], outs[0].at[me], send.at[0])
        remote = []
        for k in range(1, N_DEV):
            px, py, pc = _flip(k)
            remote.append(_remote(ins[0].at[4 * px + 2 * py + pc], outs[0].at[me], send, recv, k, (px, py, pc)))
        return [local] + remote

    def start(ins, outs, send, recv):
        for cp in copies(ins, outs, send, recv):
            cp.start()

    def finish(ins, outs, send, recv):
        for cp in copies(ins, outs, send, recv):
            cp.wait()

    return _Phase([arr], [jax.ShapeDtypeStruct(arr.shape, arr.dtype)], {}, N_DEV, start, finish, then)


def _after(*arrs):
    nothing = lambda *args: None
    return _Phase(arrs, [], {}, 1, nothing, nothing, nothing)


def _flush(name, *phases):
    _, p_outs = _call(None, name, (1,), [], [], [], [], phases=list(phases))
    for p, po in zip(phases, p_outs):
        p.then(po)


class _Big:
    KINDS = {"full": (True, True), "half": (True, False), "shard": (False, True), "block": (False, False)}

    def __init__(self, f3, s3, h3):
        assert s3 != h3
        self.f3, self.s3, self.h3 = tuple(f3), s3, h3
        self.bd = tuple(f3[a] // (N_CHIPS if a == s3 else 1) // (2 if a == h3 else 1) for a in range(3))
        self.tile = (1, _row_tile(self.bd[1], self.bd[2]), self.bd[2])
        self.grid = tuple(self.bd[a] // self.tile[a] for a in range(3))

    def dims(self, kind):
        chips, halves = self.KINDS[kind]
        return tuple(
            self.bd[a] * (N_CHIPS if chips and a == self.s3 else 1) * (2 if halves and a == self.h3 else 1) for a in range(3)
        )

    def view(self, ref, chip=None, half=None, batch0=0, both_halves=True, part=None):
        start = [batch0, 0, 0]
        size = list(ref.shape)
        size[0] = self.bd[0] * (2 if self.h3 == 0 and both_halves else 1)
        if chip is not None:
            start[self.s3] += chip * self.bd[self.s3]
            size[self.s3] = self.bd[self.s3]
        if half is not None:
            start[self.h3] += half * self.bd[self.h3]
            size[self.h3] = self.bd[self.h3]
        if part is not None:
            size[1] //= 2
            start[1] += part * size[1]
        return ref.at[tuple(pl.ds(st, sz) for st, sz in zip(start, size))]

    def spec(self, chip_from=None, half_from=None, lead=(), batch0=0):
        extra = "grid" in (chip_from, half_from)

        def index(*args):
            pref, idx = args[-1], list(args[int(extra) : -1])
            idx[0] += batch0
            if chip_from:
                idx[self.s3] += (pref[0] if chip_from == "pref" else args[0]) * self.grid[self.s3]
            if half_from:
                idx[self.h3] += (pref[1] if half_from == "pref" else args[0]) * self.grid[self.h3]
            return (0,) * len(lead) + tuple(idx)

        return pl.BlockSpec(tuple(lead) + self.tile, index)


def _same(arrs):
    return [jax.ShapeDtypeStruct(a.shape, a.dtype) for a in arrs]


def _phase_gather_relay(arrs, bigs, second, whole_first, then):
    n = len(arrs)
    per = 4 if second and not whole_first else 2

    def copies(outs, send, recv, arriving):
        x, y, c = _place()
        me, xn, yn, dg = (x, y), (1 - x, y), (x, 1 - y), (1 - x, 1 - y)
        if not second:
            part = (None, None) if whole_first else (0, 1)
            plan = [((xn if arriving else me), part[0], xn), ((yn if arriving else me), part[1], yn)]
        elif whole_first:
            plan = [(dg, 0, yn), (dg, 1, xn)] if arriving else [(xn, 0, yn), (yn, 1, xn)]
        elif arriving:
            plan = [(yn, 0, yn), (dg, 0, yn), (xn, 1, xn), (dg, 1, xn)]
        else:
            plan = [(me, 0, yn), (xn, 0, yn), (me, 1, xn), (yn, 1, xn)]
        res = []
        for a in range(n):
            for k, (chip, part, to) in enumerate(plan):
                blk = bigs[a].view(outs[a], 2 * chip[0] + chip[1], c, part=part)
                res.append(_remote(blk, blk, send, recv, per * a + k, (*to, c)))
        return res

    def start(ins, outs, send, recv):
        for cp in copies(outs, send, recv, False):
            cp.start()

    def finish(ins, outs, send, recv):
        for cp in copies(outs, send, recv, True):
            cp.wait_recv()
        for cp in copies(outs, send, recv, False):
            cp.wait_send()

    return _Phase(arrs, _same(arrs), {a: a for a in range(n)}, per * n, start, finish, then)


def _phase_gather_sibling(arrs, bigs, then):
    n = len(arrs)

    def copies(outs, send, recv, arriving):
        x, y, c = _place()
        return [
            _remote(blk, blk, send, recv, 3 * a + j, (x, y, 1 - c))
            for j, chip in enumerate(_other_chips())
            for a in range(n)
            for blk in [bigs[a].view(outs[a], 2 * chip[0] + chip[1], 1 - c if arriving else c)]
        ]

    def start(ins, outs, send, recv):
        for cp in copies(outs, send, recv, False):
            cp.start()

    def finish(ins, outs, send, recv):
        for cp in copies(outs, send, recv, True):
            cp.wait_recv()
        for cp in copies(outs, send, recv, False):
            cp.wait_send()

    return _Phase(arrs, _same(arrs), {a: a for a in range(n)}, 3 * n, start, finish, then)


def _phase_pair_exchange(grads, bigs, then):
    n = len(grads)

    def copies(ins, outs, send, recv):
        x, y, c = _place()
        srcs = [ins[a] if ins[a].shape == outs[a].shape else bigs[a].view(ins[a], None, 1 - c) for a in range(n)]
        return [_remote(srcs[a], outs[a], send, recv, a, (x, y, 1 - c)) for a in range(n)]

    def start(ins, outs, send, recv):
        for cp in copies(ins, outs, send, recv):
            cp.start()

    def finish(ins, outs, send, recv):
        for cp in copies(ins, outs, send, recv):
            cp.wait()

    shapes = [jax.ShapeDtypeStruct(b.dims("half"), BF16) for b in bigs]
    return _Phase(grads, shapes, {}, n, start, finish, then)


def _phase_chip_exchange(sums, bigs, then):
    n = len(sums)

    def copies(ins, outs, send, recv):
        _, _, c = _place()
        return [
            _remote(bigs[a].view(ins[a], 2 * chip[0] + chip[1], both_halves=False), outs[a].at[j], send, recv, 3 * a + j, (*chip, c))
            for j, chip in enumerate(_other_chips())
            for a in range(n)
        ]

    def start(ins, outs, send, recv):
        for cp in copies(ins, outs, send, recv):
            cp.start()

    def finish(ins, outs, send, recv):
        for cp in copies(ins, outs, send, recv):
            cp.wait()

    shapes = [jax.ShapeDtypeStruct((N_CHIPS - 1,) + b.dims("block"), BF16) for b in bigs]
    return _Phase(sums, shapes, {}, 3 * n, start, finish, then)


_HBM = pl.BlockSpec(memory_space=pltpu.HBM)
_SEM = pl.BlockSpec(memory_space=pltpu.SEMAPHORE)
_DATAFLOW = pltpu.SideEffectType.DATAFLOW_SIDE_EFFECTING


class _InFlight:
    def __init__(self, phase, send, recv, arrays, token):
        self.phase, self.send, self.recv, self.arrays, self.token = phase, send, recv, arrays, token


def _phase_results(phase, refs):
    n_in = len(phase.ins)
    updated = {o: i for i, o in phase.aliases.items()}
    fresh = [o for o in range(len(phase.out_shapes)) if o not in updated]
    return [refs[updated[o]] if o in updated else refs[n_in + fresh.index(o)] for o in range(len(phase.out_shapes))]


def _split_start(phase, name):
    n_in = len(phase.ins)
    fresh = [s for o, s in enumerate(phase.out_shapes) if o not in phase.aliases.values()]
    arrays = list(phase.ins) + [lax.empty(s.shape, s.dtype) for s in fresh]
    n = len(arrays)

    def body(*refs):
        phase.start(refs[:n_in], _phase_results(phase, refs[:n]), refs[n], refs[n + 1])
        refs[-1][...] = jnp.zeros_like(refs[-1])

    operands = [pltpu.with_memory_space_constraint(a, pltpu.HBM) for a in arrays]
    res = pl.pallas_call(
        body, name=name,
        out_shape=[pltpu.SemaphoreType.DMA((phase.n_sems,)), pltpu.SemaphoreType.DMA((phase.n_sems,))]
        + [pltpu.HBM(a.shape, a.dtype) for a in arrays] + [jax.ShapeDtypeStruct((8, 128), F32)],
        in_specs=[_HBM] * n, out_specs=[_SEM, _SEM] + [_HBM] * n + [pl.BlockSpec(memory_space=pltpu.VMEM)],
        input_output_aliases={i: 2 + i for i in range(n)},
        compiler_params=pltpu.CompilerParams(has_side_effects=_DATAFLOW),
    )(*operands)
    return _InFlight(phase, res[0], res[1], list(res[2 : 2 + n]), res[-1])


def _split_wait(flight, after, name):
    phase, n = flight.phase, len(flight.arrays)
    n_in = len(phase.ins)

    def body(*refs):
        phase.finish(refs[:n_in], _phase_results(phase, refs[:n]), refs[n], refs[n + 1])

    res = pl.pallas_call(
        body, name=name, out_shape=[pltpu.HBM(a.shape, a.dtype) for a in flight.arrays],
        in_specs=[_HBM] * n + [_SEM, _SEM] + [_ANY] * len(after), out_specs=[_HBM] * n,
        input_output_aliases={i: i for i in range(n)},
        compiler_params=pltpu.CompilerParams(has_side_effects=_DATAFLOW),
    )(*flight.arrays, flight.send, flight.recv, *after)
    res = list(res)
    phase.then(_phase_results(phase, res))
    return res[:n_in]


def _phase_pair_broadcast(stacks, bigs, batch0s, then):
    n = len(stacks)

    def start(ins, outs, send, recv):
        x, y, c = _place()
        for a in range(n):
            blk = bigs[a].view(outs[a], None, c, batch0s[a])
            _remote(blk, blk, send, recv, a, (x, y, 1 - c)).start()

    def finish(ins, outs, send, recv):
        x, y, c = _place()
        for a in range(n):
            mine = bigs[a].view(outs[a], None, c, batch0s[a])
            theirs = bigs[a].view(outs[a], None, 1 - c, batch0s[a])
            _remote(mine, mine, send, recv, a, (x, y, 1 - c)).wait_send()
            _remote(theirs, theirs, send, recv, a, (x, y, 1 - c)).wait_recv()

    return _Phase(stacks, _same(stacks), {a: a for a in range(n)}, n, start, finish, then)


def _tile_call(body, name, big, where, extra, ins, in_specs, out_specs, out_shape, phases=()):
    grid = ((extra,) if extra else ()) + big.grid
    return _call(body, name, grid, in_specs, out_specs, out_shape, ins, prefetch=(where,), phases=phases)


def _cast_into_full(w_stack, batch0, big, where, name, phases=()):
    def body(_, w_ref, o_ref):
        o_ref[...] = w_ref[...].astype(BF16)

    return _tile_call(
        body, name, big, where, 2, [w_stack], [big.spec(None, "grid", batch0=batch0)], [big.spec("pref", "grid")],
        [jax.ShapeDtypeStruct(big.dims("full"), BF16)], phases,
    )


def _pair_sum(g_full, recv_half, big, where, name, phases=()):
    def body(_, g_ref, r_ref, o_ref):
        o_ref[...] = (g_ref[...].astype(F32) + r_ref[...].astype(F32)).astype(BF16)

    half = big.spec("grid", None)
    return _tile_call(
        body, name, big, where, N_CHIPS, [g_full, recv_half], [big.spec("grid", "pref"), half], [half],
        [jax.ShapeDtypeStruct(big.dims("half"), BF16)], phases,
    )


def _chip_sum(chip_sum, parts, big, where, stack, stack_shape, batch0, name, phases=()):
    def body(_, own_ref, p_ref, *rest):
        acc = own_ref[...].astype(F32)
        for k in range(N_CHIPS - 1):
            acc = acc + p_ref[k].astype(F32)
        rest[-1][...] = acc

    ins = [chip_sum, parts] + ([stack] if stack is not None else [])
    in_specs = [big.spec("pref", None), big.spec(None, None, lead=(N_CHIPS - 1,))] + ([_ANY] if stack is not None else [])
    return _call(
        body, name, big.grid, in_specs, [big.spec(None, "pref", batch0=batch0)], [jax.ShapeDtypeStruct(stack_shape, F32)], ins,
        prefetch=(where,), phases=phases, in_place={2: 0} if stack is not None else None,
    )


def _adam_stack(w, g, m, v, name, after=()):
    b, r, c = w.shape
    tr = _row_tile(r, c, ADAM_BLOCK_ELEMS)

    def body(w_ref, g_ref, m_ref, v_ref, *rest):
        go_ref, d_ref, mo_ref, vo_ref = rest[-4:]
        gv = g_ref[...]
        d, mo, vo = _adam(w_ref[...], gv, m_ref[...], v_ref[...])
        go_ref[...] = gv
        d_ref[...] = d
        mo_ref[...] = mo
        vo_ref[...] = vo

    spec = pl.BlockSpec((1, tr, c), lambda bb, i: (bb, i, 0))
    outs, _ = _call(
        body, name, (b, r // tr), [spec] * 4 + [_ANY] * len(after), [spec] * 4, [jax.ShapeDtypeStruct(w.shape, F32)] * 4,
        [w, g, m, v, *after],
    )
    return outs


def _mod_fwd(c_all, w_mod, b_cols, phases=()):
    n_layers, d, n = w_mod.shape
    tn = _pick(n, (768, 512, 384, 256, 128))

    def body(c_ref, w_ref, b_ref, o_ref):
        cv = c_ref[...]
        ca = (cv * _sigmoid(cv)).astype(BF16)
        o_ref[0] = _dot(ca, w_ref[0].astype(BF16)) + b_ref[0]

    return _call(
        body, "mod_fwd", (n_layers, n // tn),
        [
            pl.BlockSpec((N_DEV, d), lambda l, j: (0, 0)),
            pl.BlockSpec((1, d, tn), lambda l, j: (l, 0, j)),
            pl.BlockSpec((1, 1, tn), lambda l, j: (l, 0, j)),
        ],
        [pl.BlockSpec((1, N_DEV, tn), lambda l, j: (l, 0, j))],
        [jax.ShapeDtypeStruct((n_layers, N_DEV, n), F32)], [c_all, w_mod, b_cols], phases=phases,
    )


def _mod_bwd_adam(c_all_t, dmod_cols, w, m, v, after=()):
    n_layers, d, n = w.shape
    tn = _pick(n, (384, 256, 128))

    def body(c_ref, dm_ref, w_ref, m_ref, v_ref, *rest):
        g_ref, d_ref, mo_ref, vo_ref = rest[-4:]
        cv = c_ref[...]
        ca = (cv * _sigmoid(cv)).astype(BF16)
        g = _dot(ca, dm_ref[0].astype(BF16))
        g_ref[0] = g
        dl, mo, vo = _adam(w_ref[0], g, m_ref[0], v_ref[0])
        d_ref[0] = dl
        mo_ref[0] = mo
        vo_ref[0] = vo

    wspec = pl.BlockSpec((1, d, tn), lambda l, j: (l, 0, j))
    outs, _ = _call(
        body, "mod_bwd_adam", (n_layers, n // tn),
        [pl.BlockSpec((d, N_DEV), lambda l, j: (0, 0)), pl.BlockSpec((1, N_DEV, tn), lambda l, j: (l, 0, j)), wspec, wspec, wspec]
        + [_ANY] * len(after),
        [wspec] * 4, [jax.ShapeDtypeStruct(w.shape, F32)] * 4, [c_all_t, dmod_cols, w, m, v, *after],
    )
    return outs


def _ffn_fwd(x, vec, w_in, w_out, name, phases=()):
    s, d = x.shape
    f = w_out.shape[1]
    tm = _pick(s, (1024, 512, 256, 128))
    tf = _pick(f, (256, 128))
    nf = f // tf

    def body(x_ref, vec_ref, wg_ref, wu_ref, wo_ref, xo_ref, g_ref, u_ref, y_ref, h_sc, acc_sc):
        j = pl.program_id(1)

        @pl.when(j == 0)
        def _():
            h_sc[...] = _modulate(x_ref[...], vec_ref).astype(BF16)
            acc_sc[...] = jnp.zeros_like(acc_sc)

        h = h_sc[...]
        g = _dot(h, wg_ref[0])
        u = _dot(h, wu_ref[0])
        g_ref[...] = g.astype(BF16)
        u_ref[...] = u.astype(BF16)
        a = (g * _sigmoid(g) * u).astype(BF16)
        acc_sc[...] += _dot(a, wo_ref[0])

        @pl.when(j == nf - 1)
        def _():
            yv = acc_sc[...]
            xo_ref[...] = x_ref[...] + 0.5 * vec_ref[3:4, :] * yv
            y_ref[...] = yv.astype(BF16)

    row = pl.BlockSpec((tm, d), lambda i, j: (i, 0))
    hid = pl.BlockSpec((tm, tf), lambda i, j: (i, j))
    return _call(
        body, name, (s // tm, nf),
        [
            row,
            pl.BlockSpec((8, d), lambda i, j: (0, 0)),
            pl.BlockSpec((1, d, tf), lambda i, j: (0, 0, j)),
            pl.BlockSpec((1, d, tf), lambda i, j: (0, 0, nf + j)),
            pl.BlockSpec((1, tf, d), lambda i, j: (0, j, 0)),
        ],
        [row, hid, hid, row],
        [
            jax.ShapeDtypeStruct((s, d), F32),
            jax.ShapeDtypeStruct((s, f), BF16),
            jax.ShapeDtypeStruct((s, f), BF16),
            jax.ShapeDtypeStruct((s, d), BF16),
        ],
        [x, vec, w_in, w_in, w_out],
        scratch=[pltpu.VMEM((tm, d), BF16), pltpu.VMEM((tm, d), F32)], phases=phases,
    )


def _ffn_bwd(dxo, x, vec, gg, uu, y, w_in, w_out, name, phases=()):
    s, d = x.shape
    f = w_out.shape[1]
    tm = _pick(s, (512, 256, 128))
    tf = _pick(f, (256, 128))
    nf = f // tf

    def body(dxo_ref, x_ref, vec_ref, g_ref, u_ref, y_ref, wg_ref, wu_ref, wo_ref,
             dx_ref, dg_ref, du_ref, a_ref, h_ref, dy_ref, dvec_ref, acc_sc):
        i, j = pl.program_id(0), pl.program_id(1)

        @pl.when((i == 0) & (j == 0))
        def _():
            dvec_ref[...] = jnp.zeros_like(dvec_ref)

        @pl.when(j == 0)
        def _():
            dxo_v = dxo_ref[...]
            dy_ref[...] = (0.5 * vec_ref[3:4, :] * dxo_v).astype(BF16)
            dvec_ref[3:4, :] += 0.5 * jnp.sum(dxo_v * y_ref[...].astype(F32), axis=0, keepdims=True)
            acc_sc[...] = jnp.zeros_like(acc_sc)

        da = _dot_nt(dy_ref[...], wo_ref[0])
        g = g_ref[...].astype(F32)
        u = u_ref[...].astype(F32)
        sig = _sigmoid(g)
        sl = g * sig
        a_ref[...] = (sl * u).astype(BF16)
        dg = (da * u * (sig * (1.0 + g * (1.0 - sig)))).astype(BF16)
        du = (da * sl).astype(BF16)
        dg_ref[...] = dg
        du_ref[...] = du
        acc_sc[...] += _dot_nt(dg, wg_ref[0]) + _dot_nt(du, wu_ref[0])

        @pl.when(j == nf - 1)
        def _():
            dx, h = _modulate_bwd(x_ref[...], acc_sc[...], vec_ref, dvec_ref)
            dx_ref[...] = dxo_ref[...] + dx
            h_ref[...] = h.astype(BF16)

    row = pl.BlockSpec((tm, d), lambda i, j: (i, 0))
    hid = pl.BlockSpec((tm, tf), lambda i, j: (i, j))
    vecs = pl.BlockSpec((8, d), lambda i, j: (0, 0))
    return _call(
        body, name, (s // tm, nf),
        [
            row, row, vecs, hid, hid, row,
            pl.BlockSpec((1, d, tf), lambda i, j: (0, 0, j)),
            pl.BlockSpec((1, d, tf), lambda i, j: (0, 0, nf + j)),
            pl.BlockSpec((1, tf, d), lambda i, j: (0, j, 0)),
        ],
        [row, hid, hid, hid, row, row, vecs],
        [
            jax.ShapeDtypeStruct((s, d), F32),
            jax.ShapeDtypeStruct((s, f), BF16),
            jax.ShapeDtypeStruct((s, f), BF16),
            jax.ShapeDtypeStruct((s, f), BF16),
            jax.ShapeDtypeStruct((s, d), BF16),
            jax.ShapeDtypeStruct((s, d), BF16),
            jax.ShapeDtypeStruct((8, d), F32),
        ],
        [dxo, x, vec, gg, uu, y, w_in, w_in, w_out],
        scratch=[pltpu.VMEM((tm, d), F32)], phases=phases,
    )


def _grad_half(a, bs, big, where, mine, recv, name, phases=()):
    s, k1 = a.shape
    n = bs[0].shape[1]
    groups = len(bs)
    rows_halved = big.h3 == 1
    assert rows_halved or groups == 1
    kk, nn = (k1 // 2, n) if rows_halved else (k1, n // 2)
    tk = _pick(kk, (1408, 1024, 512, 256, 128))
    tn = _pick(nn, (1408, 1024, 640, 512, 256, 128))
    nkb, nnb = kk // tk, nn // tn
    assert (recv is None) == (not mine)

    def half(pref):
        return pref[1] if mine else 1 - pref[1]

    def body(_, a_ref, *rest):
        q = pl.program_id(1)
        for p in range(groups):

            @pl.when(q == p)
            def _(p=p):
                acc = _dot_tn(a_ref[...], rest[p][...])
                if recv is not None:
                    acc = acc + rest[groups][0].astype(F32)
                rest[-1][0] = acc.astype(BF16)

    def b_block(p):
        def index(i, q, j, pref):
            jj = jnp.where(q == p, j, jnp.where(q < p, 0, nnb - 1))
            return (0, jj + (0 if rows_halved else half(pref) * nnb))

        return pl.BlockSpec((s, tn), index)

    out_spec = pl.BlockSpec((1, tk, tn), lambda i, q, j, pref: (0, i, q * nnb + j))
    in_specs = [pl.BlockSpec((s, tk), lambda i, q, j, pref: (0, i + (half(pref) * nkb if rows_halved else 0)))]
    in_specs += [b_block(p) for p in range(groups)]
    ins = [a, *bs]
    if recv is not None:
        in_specs.append(out_spec)
        ins.append(recv)
    return _call(
        body, name, (nkb, groups, nnb), in_specs, [out_spec], [jax.ShapeDtypeStruct(big.dims("half"), BF16)], ins,
        prefetch=(where,), phases=phases,
    )


def _proj_mod_fwd(x, vec, w, phases=()):
    s, d = x.shape
    n = w.shape[2]
    tm = _pick(s, (1024, 512, 256, 128))
    tn = _pick(n, (640, 512, 256, 128))

    def body(x_ref, vec_ref, w_ref, o_ref, h_sc):
        @pl.when(pl.program_id(1) == 0)
        def _():
            h_sc[...] = _modulate(x_ref[...], vec_ref).astype(BF16)

        o_ref[...] = _dot(h_sc[...], w_ref[0])

    return _call(
        body, "ab_in_fwd", (s // tm, n // tn),
        [
            pl.BlockSpec((tm, d), lambda i, j: (i, 0)),
            pl.BlockSpec((8, d), lambda i, j: (0, 0)),
            pl.BlockSpec((1, d, tn), lambda i, j: (0, 0, j)),
        ],
        [pl.BlockSpec((tm, tn), lambda i, j: (i, j))],
        [jax.ShapeDtypeStruct((s, n), F32)], [x, vec, w],
        scratch=[pltpu.VMEM((tm, d), BF16)], phases=phases,
    )


def _proj_res_fwd(a, w, x, vec, phases=()):
    s, kd = a.shape
    d = x.shape[1]
    tm = _pick(s, (1024, 512, 256, 128))

    def body(a_ref, w_ref, x_ref, vec_ref, xo_ref, y_ref):
        yv = _dot(a_ref[...], w_ref[0])
        xo_ref[...] = x_ref[...] + vec_ref[3:4, :] * yv
        y_ref[...] = yv.astype(BF16)

    row = pl.BlockSpec((tm, d), lambda i: (i, 0))
    return _call(
        body, "ab_out_fwd", (s // tm,),
        [pl.BlockSpec((tm, kd), lambda i: (i, 0)), pl.BlockSpec((1, kd, d), lambda i: (0, 0, 0)), row, pl.BlockSpec((8, d), lambda i: (0, 0))],
        [row, row],
        [jax.ShapeDtypeStruct((s, d), F32), jax.ShapeDtypeStruct((s, d), BF16)], [a, w, x, vec], phases=phases,
    )


def _proj_res_bwd(dxo, y, vec, w, phases=()):
    s, d = dxo.shape
    kd = w.shape[1]
    tm = _pick(s, (1024, 512, 256, 128))

    def body(dxo_ref, y_ref, vec_ref, w_ref, dy_ref, da_ref, dgate_ref):
        @pl.when(pl.program_id(0) == 0)
        def _():
            dgate_ref[...] = jnp.zeros_like(dgate_ref)

        dxo_v = dxo_ref[...]
        dy = (vec_ref[3:4, :] * dxo_v).astype(BF16)
        dy_ref[...] = dy
        dgate_ref[3:4, :] += jnp.sum(dxo_v * y_ref[...].astype(F32), axis=0, keepdims=True)
        da_ref[...] = _dot_nt(dy, w_ref[0]).astype(BF16)

    row = pl.BlockSpec((tm, d), lambda i: (i, 0))
    vecs = pl.BlockSpec((8, d), lambda i: (0, 0))
    return _call(
        body, "ab_out_bwd", (s // tm,),
        [row, row, vecs, pl.BlockSpec((1, kd, d), lambda i: (0, 0, 0))],
        [row, pl.BlockSpec((tm, kd), lambda i: (i, 0)), vecs],
        [jax.ShapeDtypeStruct((s, d), BF16), jax.ShapeDtypeStruct((s, kd), BF16), jax.ShapeDtypeStruct((8, d), F32)],
        [dxo, y, vec, w], phases=phases,
    )


def _proj_mod_bwd(dproj, w, x, vec, dxo, dvec_in, name, phases=()):
    parts, s, n_part = dproj.shape
    d = x.shape[1]
    tm = _pick(s, (512, 256, 128))
    tk = _pick(n_part, (1408, 1280, 1024, 512, 256, 128))
    per_part = n_part // tk
    nk = parts * per_part

    def body(dp_ref, w_ref, x_ref, vec_ref, dxo_ref, dvi_ref, dx_ref, h_ref, dvec_ref, acc_sc):
        i, k = pl.program_id(0), pl.program_id(1)

        @pl.when((i == 0) & (k == 0))
        def _():
            dvec_ref[...] = dvi_ref[...]

        @pl.when(k == 0)
        def _():
            acc_sc[...] = jnp.zeros_like(acc_sc)

        acc_sc[...] += _dot_nt(dp_ref[0], w_ref[0])

        @pl.when(k == nk - 1)
        def _():
            dx, h = _modulate_bwd(x_ref[...], acc_sc[...], vec_ref, dvec_ref)
            dx_ref[...] = dxo_ref[...] + dx
            h_ref[...] = h.astype(BF16)

    row = pl.BlockSpec((tm, d), lambda i, k: (i, 0))
    vecs = pl.BlockSpec((8, d), lambda i, k: (0, 0))
    return _call(
        body, name, (s // tm, nk),
        [
            pl.BlockSpec((1, tm, tk), lambda i, k: (k // per_part, i, k % per_part)),
            pl.BlockSpec((1, d, tk), lambda i, k: (0, 0, k)),
            row, vecs, row, vecs,
        ],
        [row, row, vecs],
        [jax.ShapeDtypeStruct((s, d), F32), jax.ShapeDtypeStruct((s, d), BF16), jax.ShapeDtypeStruct((8, d), F32)],
        [dproj, w, x, vec, dxo, dvec_in], scratch=[pltpu.VMEM((tm, d), F32)], phases=phases,
    )


def _tril(n):
    return lax.broadcasted_iota(jnp.int32, (n, n), 0) >= lax.broadcasted_iota(jnp.int32, (n, n), 1)


def _layernorm_stats(gv):
    mu = jnp.mean(gv, axis=-1, keepdims=True)
    cen = gv - mu
    rstd = lax.rsqrt(jnp.mean(cen * cen, axis=-1, keepdims=True) + EPS)
    return cen * rstd, rstd


def _shift_down(q, k, above_ref, c_cg, c_xb, first):
    width = q.shape[1]
    rows = lax.broadcasted_iota(jnp.int32, q.shape, 0)
    out = pltpu.roll(q, k, 0)
    for r in range(k):
        src = CONV_HALO - k + r
        above = above_ref[src : src + 1, c_cg : c_cg + width] * above_ref[src : src + 1, c_xb : c_xb + width]
        above = jnp.where(first, 0.0, above)
        out = jnp.where(rows == r, above, out)
    return out


def _ab_mix_fwd(proj, norm_v, w_s, b_rows, conv_w, phases=()):
    s, n = proj.shape
    heads, chunk, _ = w_s.shape
    da = norm_v.shape[1]
    hd = da // heads
    db = conv_w.shape[1]
    tm = _pick(s, (512, 256, 128))

    def body(p_ref, ph_ref, nv_ref, ws_ref, b_ref, cw_ref, o_ref):
        first = pl.program_id(0) == 0
        gu, _ = _gelu(p_ref[:, 0:da])
        gv, _ = _gelu(p_ref[:, da : 2 * da])
        xhat, _ = _layernorm_stats(gv)
        vn = (xhat * nv_ref[...]).astype(BF16)
        mask = _tril(chunk)
        for hh in range(heads):
            wm = jnp.where(mask, ws_ref[hh], 0.0).astype(BF16)
            cols = slice(hh * hd, (hh + 1) * hd)
            for nn in range(tm // chunk):
                rows = slice(nn * chunk, (nn + 1) * chunk)
                z = _dot(wm, vn[rows, cols]) + b_ref[:, cols]
                o_ref[rows, cols] = (gu[rows, cols] * z).astype(BF16)
        c_cg, c_xb = 2 * da + db, 2 * da + 2 * db
        bg = p_ref[:, 2 * da : 2 * da + db]
        q = p_ref[:, c_cg : c_cg + db] * p_ref[:, c_xb : c_xb + db]
        q1 = _shift_down(q, 1, ph_ref, c_cg, c_xb, first)
        q2 = _shift_down(q, 2, ph_ref, c_cg, c_xb, first)
        conv = cw_ref[0:1, :] * q2 + cw_ref[1:2, :] * q1 + cw_ref[2:3, :] * q
        o_ref[:, da : da + db] = (bg * conv).astype(BF16)

    nh = tm // CONV_HALO
    return _call(
        body, "ab_mix_fwd", (s // tm,),
        [
            pl.BlockSpec((tm, n), lambda i: (i, 0)),
            pl.BlockSpec((CONV_HALO, n), lambda i: (jnp.maximum(i * nh - 1, 0), 0)),
            pl.BlockSpec((1, da), lambda i: (0, 0)),
            pl.BlockSpec((heads, chunk, chunk), lambda i: (0, 0, 0)),
            pl.BlockSpec((chunk, da), lambda i: (0, 0)),
            pl.BlockSpec((3, db), lambda i: (0, 0)),
        ],
        [pl.BlockSpec((tm, da + db), lambda i: (i, 0))],
        [jax.ShapeDtypeStruct((s, da + db), BF16)], [proj, proj, norm_v, w_s, b_rows, conv_w], phases=phases,
    )


def _ab_mix_bwd(proj, dcat, norm_v, w_s, b_rows, conv_w, phases=()):
    s, n = proj.shape
    heads, chunk, _ = w_s.shape
    da = norm_v.shape[1]
    hd = da // heads
    db = conv_w.shape[1]
    tm = _pick(s, (512, 256, 128))
    nblk = s // tm
    dhalo = 2 * CONV_HALO

    def body(p_ref, pa_ref, pb_ref, dc_ref, dcb_ref, nv_ref, ws_ref, b_ref, cw_ref,
             dp_ref, dnv_ref, dws_ref, dzs_ref, dcw_ref, dvn_sc):
        i = pl.program_id(0)
        first, last = i == 0, i == nblk - 1

        @pl.when(first)
        def _():
            dnv_ref[...] = jnp.zeros_like(dnv_ref)
            dws_ref[...] = jnp.zeros_like(dws_ref)
            dzs_ref[...] = jnp.zeros_like(dzs_ref)
            dcw_ref[...] = jnp.zeros_like(dcw_ref)

        uu = p_ref[:, 0:da]
        gu, gu_grad = _gelu(uu)
        gv, gv_grad = _gelu(p_ref[:, da : 2 * da])
        xhat, rstd = _layernorm_stats(gv)
        nv = nv_ref[...]
        vn = (xhat * nv).astype(BF16)
        dya = dc_ref[:, 0:da].astype(F32)
        dz = (dya * gu).astype(BF16)
        mask = _tril(chunk)
        for hh in range(heads):
            wm = jnp.where(mask, ws_ref[hh], 0.0).astype(BF16)
            cols = slice(hh * hd, (hh + 1) * hd)
            dws = jnp.zeros((chunk, chunk), F32)
            for nn in range(tm // chunk):
                rows = slice(nn * chunk, (nn + 1) * chunk)
                z = _dot(wm, vn[rows, cols]) + b_ref[:, cols]
                dp_ref[rows, cols] = (dya[rows, cols] * z * gu_grad[rows, cols]).astype(BF16)
                dz_blk = dz[rows, cols]
                dws = dws + _dot_nt(dz_blk, vn[rows, cols])
                dzs_ref[:, cols] += dz_blk.astype(F32)
                dvn = _dot_tn(wm, dz_blk)
                dnv_ref[:, cols] += jnp.sum(dvn * xhat[rows, cols], axis=0, keepdims=True)
                dvn_sc[rows, cols] = dvn
            dws_ref[hh] += jnp.where(mask, dws, 0.0)
        dxhat = dvn_sc[...] * nv
        dgv = rstd * (dxhat - jnp.mean(dxhat, axis=-1, keepdims=True) - xhat * jnp.mean(dxhat * xhat, axis=-1, keepdims=True))
        dp_ref[:, da : 2 * da] = (dgv * gv_grad).astype(BF16)

        c_bg, c_cg, c_xb = 2 * da, 2 * da + db, 2 * da + 2 * db
        bg = p_ref[:, c_bg : c_bg + db]
        cg = p_ref[:, c_cg : c_cg + db]
        xb = p_ref[:, c_xb : c_xb + db]
        q = cg * xb
        q1 = _shift_down(q, 1, pa_ref, c_cg, c_xb, first)
        q2 = _shift_down(q, 2, pa_ref, c_cg, c_xb, first)
        dyb = dc_ref[:, da : da + db].astype(F32)
        conv = cw_ref[0:1, :] * q2 + cw_ref[1:2, :] * q1 + cw_ref[2:3, :] * q
        dp_ref[:, c_bg : c_bg + db] = (dyb * conv).astype(BF16)
        e = dyb * bg
        dcw_ref[0:1, :] += jnp.sum(e * q2, axis=0, keepdims=True)
        dcw_ref[1:2, :] += jnp.sum(e * q1, axis=0, keepdims=True)
        dcw_ref[2:3, :] += jnp.sum(e * q, axis=0, keepdims=True)
        rows = lax.broadcasted_iota(jnp.int32, e.shape, 0)
        dq = cw_ref[2:3, :] * e
        for kk in (1, 2):
            ek = pltpu.roll(e, tm - kk, 0)
            for r in range(kk):
                below = dcb_ref[r : r + 1, da : da + db].astype(F32) * pb_ref[r : r + 1, c_bg : c_bg + db]
                below = jnp.where(last, 0.0, below)
                ek = jnp.where(rows == tm - kk + r, below, ek)
            dq = dq + cw_ref[2 - kk : 3 - kk, :] * ek
        dp_ref[:, c_cg : c_cg + db] = (dq * xb).astype(BF16)
        dp_ref[:, c_xb : c_xb + db] = (dq * cg).astype(BF16)

    nh = tm // CONV_HALO
    nhb = tm // dhalo
    const2 = lambda i: (0, 0)
    return _call(
        body, "ab_mix_bwd", (nblk,),
        [
            pl.BlockSpec((tm, n), lambda i: (i, 0)),
            pl.BlockSpec((CONV_HALO, n), lambda i: (jnp.maximum(i * nh - 1, 0), 0)),
            pl.BlockSpec((CONV_HALO, n), lambda i: (jnp.minimum((i + 1) * nh, s // CONV_HALO - 1), 0)),
            pl.BlockSpec((tm, da + db), lambda i: (i, 0)),
            pl.BlockSpec((dhalo, da + db), lambda i: (jnp.minimum((i + 1) * nhb, s // dhalo - 1), 0)),
            pl.BlockSpec((1, da), const2),
            pl.BlockSpec((heads, chunk, chunk), lambda i: (0, 0, 0)),
            pl.BlockSpec((chunk, da), const2),
            pl.BlockSpec((3, db), const2),
        ],
        [
            pl.BlockSpec((tm, n), lambda i: (i, 0)),
            pl.BlockSpec((1, da), const2),
            pl.BlockSpec((heads, chunk, chunk), lambda i: (0, 0, 0)),
            pl.BlockSpec((chunk, da), const2),
            pl.BlockSpec((3, db), const2),
        ],
        [
            jax.ShapeDtypeStruct((s, n), BF16),
            jax.ShapeDtypeStruct((1, da), F32),
            jax.ShapeDtypeStruct((heads, chunk, chunk), F32),
            jax.ShapeDtypeStruct((chunk, da), F32),
            jax.ShapeDtypeStruct((3, db), F32),
        ],
        [proj, proj, proj, dcat, dcat, norm_v, w_s, b_rows, conv_w],
        scratch=[pltpu.VMEM((tm, da), F32)], phases=phases,
    )


def _pool_counts(tm, i, w):
    t = i * tm + lax.broadcasted_iota(jnp.int32, (tm, 1), 0)
    return jnp.minimum(t + 1, w).astype(F32)


def _pool_fwd(x, vec, w_grp, scale, phases=()):
    s, d = x.shape
    groups, gd, _ = w_grp.shape
    tm = _pick(s, (512, 256, 128))

    def body(x_ref, xa_ref, vec_ref, w_ref, sc_ref, xo_ref, p_ref, o_ref):
        i = pl.program_id(0)
        h = _modulate(x_ref[...], vec_ref)
        ha = jnp.where(i == 0, 0.0, _modulate(xa_ref[...], vec_ref))
        ext = jnp.concatenate([ha, h], axis=0)
        for gi, w in enumerate(POOL_WINDOWS):
            cols = slice(gi * gd, (gi + 1) * gd)
            acc = ext[:, cols]
            step = 1
            while step < w:
                acc = acc + pltpu.roll(acc, step, 0)
                step *= 2
            p = (acc[POOL_HALO:, :] / _pool_counts(tm, i, w) - h[:, cols]).astype(BF16)
            p_ref[:, cols] = p
            o_ref[:, cols] = _dot(p, w_ref[gi]).astype(BF16)
        xo_ref[...] = x_ref[...] + vec_ref[3:4, :] * (o_ref[...].astype(F32) * sc_ref[...])

    nh = tm // POOL_HALO
    row = pl.BlockSpec((tm, d), lambda i: (i, 0))
    return _call(
        body, "pool_fwd", (s // tm,),
        [
            row,
            pl.BlockSpec((POOL_HALO, d), lambda i: (jnp.maximum(i * nh - 1, 0), 0)),
            pl.BlockSpec((8, d), lambda i: (0, 0)),
            pl.BlockSpec((groups, gd, gd), lambda i: (0, 0, 0)),
            pl.BlockSpec((1, d), lambda i: (0, 0)),
        ],
        [row, row, row],
        [jax.ShapeDtypeStruct((s, d), F32), jax.ShapeDtypeStruct((s, d), BF16), jax.ShapeDtypeStruct((s, d), BF16)],
        [x, x, vec, w_grp, scale], phases=phases,
    )


def _pool_bwd(dxo, x, vec, p, o, w_grp, scale, phases=()):
    s, d = x.shape
    groups, gd, _ = w_grp.shape
    tm = _pick(s, (512, 256, 128))
    nblk = s // tm

    def body(dxo_ref, dxb_ref, x_ref, vec_ref, p_ref, o_ref, w_ref, sc_ref, dx_ref, dw_ref, dsc_ref, dvec_ref, dw_sc):
        i = pl.program_id(0)

        @pl.when(i == 0)
        def _():
            dw_sc[...] = jnp.zeros_like(dw_sc)
            dsc_ref[...] = jnp.zeros_like(dsc_ref)
            dvec_ref[...] = jnp.zeros_like(dvec_ref)

        gate, sc = vec_ref[3:4, :], sc_ref[...]
        dxo_v = dxo_ref[...]
        ov = o_ref[...].astype(F32)
        dvec_ref[3:4, :] += jnp.sum(dxo_v * (ov * sc), axis=0, keepdims=True)
        dy = gate * dxo_v
        dsc_ref[...] += jnp.sum(dy * ov, axis=0, keepdims=True)
        dout = (dy * sc).astype(BF16)
        dout_b = jnp.where(i == nblk - 1, 0.0, gate * dxb_ref[...] * sc).astype(BF16)
        for gi, w in enumerate(POOL_WINDOWS):
            cols = slice(gi * gd, (gi + 1) * gd)
            dw_sc[gi] += _dot_tn(p_ref[:, cols], dout[:, cols])
            wb = w_ref[gi]
            dp = _dot_nt(dout[:, cols], wb)
            dp_b = _dot_nt(dout_b[:, cols], wb)
            e = dp / _pool_counts(tm, i, w)
            t_below = (i + 1) * tm + lax.broadcasted_iota(jnp.int32, (POOL_HALO, 1), 0)
            e_b = dp_b / jnp.minimum(t_below + 1, w).astype(F32)
            acc = jnp.concatenate([e, e_b], axis=0)
            step = 1
            while step < w:
                acc = acc + pltpu.roll(acc, tm + POOL_HALO - step, 0)
                step *= 2
            dx_ref[:, cols] = acc[:tm, :] - dp
        dx, _ = _modulate_bwd(x_ref[...], dx_ref[...], vec_ref, dvec_ref)
        dx_ref[...] = dxo_v + dx

        @pl.when(i == nblk - 1)
        def _():
            dw_ref[...] = dw_sc[...].astype(BF16)

    nh = tm // POOL_HALO
    row = pl.BlockSpec((tm, d), lambda i: (i, 0))
    vecs = pl.BlockSpec((8, d), lambda i: (0, 0))
    wspec = pl.BlockSpec((groups, gd, gd), lambda i: (0, 0, 0))
    return _call(
        body, "pool_bwd", (nblk,),
        [
            row,
            pl.BlockSpec((POOL_HALO, d), lambda i: (jnp.minimum((i + 1) * nh, s // POOL_HALO - 1), 0)),
            row, vecs, row, row, wspec,
            pl.BlockSpec((1, d), lambda i: (0, 0)),
        ],
        [row, wspec, pl.BlockSpec((1, d), lambda i: (0, 0)), vecs],
        [
            jax.ShapeDtypeStruct((s, d), F32),
            jax.ShapeDtypeStruct((groups, gd, gd), BF16),
            jax.ShapeDtypeStruct((1, d), F32),
            jax.ShapeDtypeStruct((8, d), F32),
        ],
        [dxo, dxo, x, vec, p, o, w_grp, scale],
        scratch=[pltpu.VMEM((groups, gd, gd), F32)], phases=phases,
    )


def _loss_head(x, gain, target, phases=()):
    s, d = x.shape
    tm = _pick(s, (512, 256, 128))

    def body(x_ref, g_ref, t_ref, dx_ref, aux_ref):
        @pl.when(pl.program_id(0) == 0)
        def _():
            aux_ref[...] = jnp.zeros_like(aux_ref)

        xv = x_ref[...]
        rstd = _rstd(xv)
        r = xv * rstd
        gain_v = g_ref[...]
        err = r * gain_v - t_ref[...]
        aux_ref[1:2, :] += jnp.sum(err * err, axis=0, keepdims=True)
        dout = err * (1.0 / d)
        aux_ref[0:1, :] += jnp.sum(dout * r, axis=0, keepdims=True)
        dr = dout * gain_v
        dx_ref[...] = rstd * (dr - r * jnp.mean(dr * r, axis=-1, keepdims=True))

    row = pl.BlockSpec((tm, d), lambda i: (i, 0))
    return _call(
        body, "loss_head", (s // tm,),
        [row, pl.BlockSpec((1, d), lambda i: (0, 0)), row],
        [row, pl.BlockSpec((8, d), lambda i: (0, 0))],
        [jax.ShapeDtypeStruct((s, d), F32), jax.ShapeDtypeStruct((8, d), F32)], [x, gain, target], phases=phases,
    )


def _small_adam(gathered, gathered_ws, layout, smalls, chip):
    names = list(smalls)
    n = len(names)
    loss_row, _, _, n_feat = layout["loss"]

    def body(*refs):
        chip_ref, g_ref, gws_ref = refs[0], refs[1], refs[2]
        wmv = refs[3 : 3 + 3 * n]
        outs = refs[3 + 3 * n : 3 + 7 * n]
        total = refs[-1]
        total[...] = g_ref[0]
        for kdev in range(1, N_DEV):
            total[...] += g_ref[kdev]
        total_ws = gws_ref[0]
        for kdev in range(1, N_DEV):
            total_ws = total_ws + gws_ref[kdev]
        my_chip = chip_ref[0]
        for a, name in enumerate(names):
            w_ref, m_ref, v_ref = wmv[3 * a : 3 * a + 3]
            if name == "ab_w_s":
                g = total_ws
            else:
                row0, rows, col0, cols = layout[name]
                if col0 is None:
                    g = jnp.zeros((rows, cols), F32)
                    for j in range(N_CHIPS):
                        g = g + jnp.where(my_chip == j, total[row0 : row0 + rows, j * cols : (j + 1) * cols], 0.0)
                else:
                    g = total[row0 : row0 + rows, col0 : col0 + cols]
            dl, mo, vo = _adam(w_ref[...], g, m_ref[...], v_ref[...])
            outs[4 * a][...] = g
            outs[4 * a + 1][...] = dl
            outs[4 * a + 2][...] = mo
            outs[4 * a + 3][...] = vo
        refs[3 + 7 * n][...] = 0.5 * jnp.sum(total[loss_row : loss_row + 1, 0:n_feat], axis=1, keepdims=True) / n_feat

    ins = [gathered, gathered_ws]
    out_shapes = []
    for name in names:
        ins.extend(smalls[name])
        out_shapes.extend([jax.ShapeDtypeStruct(smalls[name][0].shape, F32)] * 4)
    out_shapes.append(jax.ShapeDtypeStruct((1, 1), F32))
    whole = lambda shape: pl.BlockSpec(shape, functools.partial(lambda nd, i, c: (0,) * nd, len(shape)))
    res = pl.pallas_call(
        body, name="small_adam",
        grid_spec=pltpu.PrefetchScalarGridSpec(
            num_scalar_prefetch=1, grid=(1,),
            in_specs=[whole(a.shape) for a in ins], out_specs=[whole(o.shape) for o in out_shapes],
            scratch_shapes=[pltpu.VMEM(gathered.shape[1:], F32)],
        ),
        out_shape=out_shapes,
        compiler_params=pltpu.CompilerParams(dimension_semantics=("arbitrary",), vmem_limit_bytes=VMEM_LIMIT_BYTES),
    )(chip.reshape(1).astype(jnp.int32), *ins)
    return {name: res[4 * a : 4 * a + 4] for a, name in enumerate(names)}, res[4 * n]


def _pad_rows(a, rows=8):
    extra = (-a.shape[0]) % rows
    return jnp.pad(a, ((0, extra), (0, 0))) if extra else a


def _pad_cols(a, cols):
    return jnp.pad(a, ((0, 0), (0, cols - a.shape[1]))) if a.shape[1] < cols else a


def _run(fn, *phases):
    outs, p_outs = fn(list(phases))
    for p, po in zip(phases, p_outs):
        p.then(po)
    return outs


def kernel(x, c, norm_g, w_mod, b_mod, w_ffn_in, w_ffn_out, ab_w_in, ab_norm_v, ab_w_s, ab_b_s, ab_conv_w, ab_w_out, pool_w_grp, pool_scale, final_g, loss_target, m_norm_g, m_w_mod, m_b_mod, m_w_ffn_in, m_w_ffn_out, m_ab_w_in, m_ab_norm_v, m_ab_w_s, m_ab_b_s, m_ab_conv_w, m_ab_w_out, m_pool_w_grp, m_pool_scale, m_final_g, v_norm_g, v_w_mod, v_b_mod, v_w_ffn_in, v_w_ffn_out, v_ab_w_in, v_ab_norm_v, v_ab_w_s, v_ab_b_s, v_ab_conv_w, v_ab_w_out, v_pool_w_grp, v_pool_scale, v_final_g):
    ix, iy, ic = _place()
    chip = 2 * ix + iy
    me = 4 * ix + 2 * iy + ic
    where = jnp.stack([chip, ic]).astype(jnp.int32)
    s, d = x.shape[1], x.shape[2]
    x0 = x.reshape(s, d)
    target = loss_target.reshape(s, d)
    n_layers = norm_g.shape[0]
    dq = d // N_CHIPS
    heads, chunk = ab_w_s.shape[1], ab_w_s.shape[2]
    da = ab_norm_v.shape[1]
    db = ab_conv_w.shape[2] * N_CHIPS
    f_hidden = w_ffn_out.shape[2] * N_CHIPS
    assert n_layers == 2 and da % heads == 0

    cw_pad = _pad_cols(ab_conv_w.reshape(3, db // N_CHIPS), dq)
    packed = jnp.concatenate(
        [_pad_rows(c.reshape(N_CHIPS, dq)), _pad_rows(norm_g.reshape(-1, dq)), _pad_rows(pool_scale.reshape(1, dq)), _pad_rows(cw_pad)],
        axis=0,
    )
    ncol = w_mod.shape[2]
    b_cols = lax.dynamic_slice(b_mod, (0, chip * ncol), (n_layers, ncol)).reshape(n_layers, 1, ncol)
    small = {}

    def small_gather(key, arrs):
        def then(outs):
            small[key] = outs

        return _phase_small_gather(arrs, then)

    stacks = {
        "w_ffn_in": tuple(a.reshape((-1,) + a.shape[2:]) for a in (w_ffn_in, m_w_ffn_in, v_w_ffn_in)),
        "w_ffn_out": tuple(a.reshape((-1,) + a.shape[2:]) for a in (w_ffn_out, m_w_ffn_out, v_w_ffn_out)),
        "ab_w_in": (ab_w_in, m_ab_w_in, v_ab_w_in),
        "ab_w_out": (ab_w_out, m_ab_w_out, v_ab_w_out),
        "pool_w_grp": (pool_w_grp[0], m_pool_w_grp[0], v_pool_w_grp[0]),
    }
    big_in = _Big((1, d, 2 * f_hidden), 2, 1)
    big_out = _Big((1, f_hidden, d), 1, 2)
    units = {}
    for l in range(n_layers):
        for k in range(2):
            units[f"in{l}{k}"] = (big_in, "w_ffn_in", 2 * l + k)
            units[f"out{l}{k}"] = (big_out, "w_ffn_out", 2 * l + k)
    units["abin"] = (_Big((1, d, ab_w_in.shape[2] * N_CHIPS), 2, 1), "ab_w_in", 0)
    units["about"] = (_Big((1, ab_w_out.shape[1] * N_CHIPS, d), 1, 2), "ab_w_out", 0)
    units["pool"] = (_Big((pool_w_grp.shape[1], pool_w_grp.shape[2] * N_CHIPS, pool_w_grp.shape[3]), 1, 0), "pool_w_grp", 0)
    big = {u: g for u, (g, _, _) in units.items()}

    weight = {}
    complete = set()

    def cast(u):
        g, st, b0 = units[u]

        def launch(phases):
            (weight[u],), p_outs = _cast_into_full(stacks[st][0], b0, g, where, "cast_" + u, phases)
            return None, p_outs

        return launch

    def gather_relay(us, second, whole_first):
        def then(outs):
            for u, o in zip(us, outs):
                weight[u] = o

        return _phase_gather_relay([weight[u] for u in us], [big[u] for u in us], second, whole_first, then)

    def gather_sibling(*us):
        def then(outs):
            for u, o in zip(us, outs):
                weight[u] = o
                complete.add(u)

        return _phase_gather_sibling([weight[u] for u in us], [big[u] for u in us], then)

    def w_of(u):
        assert u in complete, u
        return weight[u]

    _run(cast("in00"), small_gather("inputs", [packed]))
    small_all = small["inputs"][0]
    by_chip = small_all[0::2]
    c_all = small_all[:, 0:N_CHIPS, :].reshape(N_DEV, d)
    norm_full = by_chip[:, 8 : 8 + 3 * n_layers, :].transpose(1, 0, 2).reshape(3 * n_layers, d)
    pool_scale_full = by_chip[:, 16:17, :].transpose(1, 0, 2).reshape(1, d)
    conv_full = by_chip[:, 24:27, : db // N_CHIPS].transpose(1, 0, 2).reshape(3, db)
    pieces = [("in00", "out00"), ("abin", "about"), ("in01", "out01"), ("in10", "out10", "pool"), ("in11", "out11")]
    in_flight = {}

    def start_gather(p):
        in_flight[p, 0] = _split_start(gather_relay(pieces[p], False, p == 0), f"gather_{p}_start")

    def relay_gather(p, after=()):
        flight = in_flight.pop((p, 0))
        _split_wait(flight, list(after) + list(started().ins), f"gather_{p}_arrived")
        in_flight[p, 1] = _split_start(gather_relay(pieces[p], True, p == 0), f"gather_{p}_relay")

    def started():
        return _after(*[flight.token for flight in in_flight.values()])

    def finish_gather(p, after, meanwhile=None):
        flight = in_flight.pop((p, 1))
        _split_wait(flight, list(after) + list(started().ins), f"gather_{p}_wait")
        crossing = _split_start(gather_sibling(*pieces[p]), f"gather_{p}_forward")
        behind = [crossing.token]
        if p + 1 < len(pieces):
            relay_gather(p + 1)
        if p + 3 < len(pieces):
            start_gather(p + 3)
        behind = behind + list(started().ins)
        if meanwhile is not None:
            behind = behind + meanwhile(_after(crossing.token))
        _split_wait(crossing, behind, f"gather_{p}_forwarded")

    _run(cast("out00"))
    start_gather(0)
    mod_cols = _run(lambda phases: _mod_fwd(c_all, w_mod, b_cols, phases), started())[0]

    def mod_rows(outs):
        small["mod"] = outs

    _run(cast("about"), started())
    early = [u for piece in pieces[2:4] for u in piece]
    for u in early:
        _run(cast(u), started())
    _run(
        cast("abin"), _phase_small_exchange(mod_cols.transpose(1, 0, 2), mod_rows),
        started(), _after(*[weight[u] for u in early]),
    )
    relay_gather(0)
    start_gather(1)
    start_gather(2)
    for u in pieces[4]:
        _run(cast(u), started())
    mod_mine = small["mod"][0][0::2]
    mod = mod_mine.transpose(1, 0, 2).reshape(n_layers, 3, 3, d)
    vecs = {
        (l, sub): jnp.pad(norm_full[3 * l + sub][None], ((0, 7), (0, 0))) + jnp.pad(mod[l, sub], ((1, 4), (0, 0)))
        for l in range(n_layers)
        for sub in range(3)
    }
    b_rows = jnp.broadcast_to(ab_b_s[0].T[:, :, None], (chunk, heads, da // heads)).reshape(chunk, da)

    saved = {}

    def ffn_forward(xs, l, sub, k, *phases):
        saved[l, sub, "x"] = xs
        xs, gg, uu, yb = _run(
            lambda ph: _ffn_fwd(xs, vecs[l, sub], w_of(f"in{l}{k}"), w_of(f"out{l}{k}"), f"ffn_fwd_{l}{k}", ph), *phases
        )
        saved[l, sub, "act"] = (gg, uu, yb)
        return xs

    finish_gather(0, [vecs[0, 0]] + [weight[u] for u in pieces[4]])
    xs = ffn_forward(x0, 0, 0, 0, started())
    saved[0, 1, "x"] = xs
    finish_gather(1, [xs])
    (proj,) = _run(lambda ph: _proj_mod_fwd(xs, vecs[0, 1], w_of("abin"), ph), started())
    (cat,) = _run(lambda ph: _ab_mix_fwd(proj, ab_norm_v, ab_w_s[0], b_rows, conv_full, ph))
    xs, yb = _run(lambda ph: _proj_res_fwd(cat, w_of("about"), xs, vecs[0, 1], ph))
    saved[0, 1, "act"] = (proj, cat, yb)
    finish_gather(2, [xs])
    xs = ffn_forward(xs, 0, 2, 1, started())
    finish_gather(3, [xs])
    xs = ffn_forward(xs, 1, 0, 0, started())
    saved[1, 1, "x"] = xs
    pooled = []

    def pool_forward(behind):
        pooled.extend(_run(lambda ph: _pool_fwd(xs, vecs[1, 1], w_of("pool"), pool_scale_full, ph), behind))
        return [pooled[0]]

    finish_gather(4, [xs], pool_forward)
    xs, pp, oo = pooled
    saved[1, 1, "act"] = (pp, oo)
    xs = ffn_forward(xs, 1, 2, 1)
    dxs, aux = _run(lambda ph: _loss_head(xs, final_g.reshape(1, d), target, ph))

    grad = {}
    recv = {}
    csum = {}
    parts = {}
    reduced = {}
    done = set()
    dvecs, small_g = {}, {}

    def pair_exchange(*us):
        def then(outs):
            for u, o in zip(us, outs):
                recv[u] = o

        return _phase_pair_exchange([grad[u] for u in us], [big[u] for u in us], then)

    def grad_half(u, a, bs, mine, name, *phases):
        (res,) = _run(lambda ph: _grad_half(a, bs, big[u], where, mine, recv[u] if mine else None, name, ph), *phases)
        return res

    def pair_sum(u, *phases):
        def launch(ph):
            (csum[u],), p_outs = _pair_sum(grad[u], recv[u], big[u], where, "pair_sum_" + u, ph)
            return None, p_outs

        _run(launch, *phases)

    def chip_exchange(*us):
        def then(outs):
            for u, o in zip(us, outs):
                parts[u] = o

        return _phase_chip_exchange([csum[u] for u in us], [big[u] for u in us], then)

    def chip_sum(*us, carried=()):
        for n_u, u in enumerate(us):
            g, st, b0 = units[u]

            def launch(ph):
                (reduced[st],), p_outs = _chip_sum(
                    csum[u], parts[u], g, where, reduced.get(st), stacks[st][0].shape, b0, "chip_sum_" + u, ph
                )
                return None, p_outs

            _run(launch, *(carried if n_u == 0 else ()))

    def pair_broadcast(*us):
        sts = [units[u][1] for u in us]
        assert len(set(sts)) == len(sts)

        def then(outs):
            for u, st, o in zip(us, sts, outs):
                reduced[st] = o
                done.add(u)

        return _phase_pair_broadcast([reduced[st] for st in sts], [big[u] for u in us], [units[u][2] for u in us], then)

    def ffn_backward(dxs, l, sub, k, carried_bwd, carried_send, carried_mine):
        gg, uu, yb = saved[l, sub, "act"]
        w_in, w_out = w_of(f"in{l}{k}"), w_of(f"out{l}{k}")
        uo, ui, tag = f"out{l}{k}", f"in{l}{k}", f"{l}{k}"
        dxs, dg, du, a, h, dy, dvecs[l, sub] = _run(
            lambda ph: _ffn_bwd(dxs, saved[l, sub, "x"], vecs[l, sub], gg, uu, yb, w_in, w_out, "ffn_bwd_" + tag, ph), *carried_bwd()
        )
        grad[uo] = grad_half(uo, a, [dy], False, "dw_out_send_" + tag, *carried_send())
        grad[ui] = grad_half(ui, h, [dg, du], False, "dw_in_send_" + tag, pair_exchange(uo))
        csum[uo] = grad_half(uo, a, [dy], True, "dw_out_" + tag, pair_exchange(ui))
        csum[ui] = grad_half(ui, h, [dg, du], True, "dw_in_" + tag, *carried_mine())
        return dxs

    none = lambda: ()
    dxs = ffn_backward(dxs, 1, 2, 1, none, none, none)
    pp, oo = saved[1, 1, "act"]
    dxs, grad["pool"], small_g["pool_scale"], dvecs[1, 1] = _run(
        lambda ph: _pool_bwd(dxs, saved[1, 1, "x"], vecs[1, 1], pp, oo, w_of("pool"), pool_scale_full, ph)
    )

    def after_11():
        return (chip_exchange("in11", "out11"), pair_exchange("pool"))

    def bcast_11():
        chip_sum("in11", "out11")
        pair_sum("pool")
        return (pair_broadcast("in11", "out11"), chip_exchange("pool"))

    dxs = ffn_backward(dxs, 1, 0, 0, after_11, bcast_11, none)

    def after_10():
        return (chip_exchange("in10", "out10"),)

    def bcast_10():
        chip_sum("in10", "out10", "pool")
        return (pair_broadcast("in10", "out10", "pool"),)

    dxs = ffn_backward(dxs, 0, 2, 1, after_10, bcast_10, none)

    proj, cat, yb = saved[0, 1, "act"]
    out01 = _split_start(chip_exchange("out01"), "reduce_out01_start")
    dy, dcat, dgate = _run(lambda ph: _proj_res_bwd(dxs, yb, vecs[0, 1], w_of("about"), ph), _after(out01.token))
    grad["about"] = grad_half("about", cat, [dy], False, "dw_ab_out_send")
    dproj, small_g["ab_norm_v"], small_g["ab_w_s"], dzs, small_g["ab_conv_w"] = _run(
        lambda ph: _ab_mix_bwd(proj, dcat, ab_norm_v, ab_w_s[0], b_rows, conv_full, ph), pair_exchange("about")
    )
    small_g["ab_b_s"] = dzs.reshape(chunk, heads, da // heads).sum(axis=2).T
    dxs, h, dvecs[0, 1] = _run(
        lambda ph: _proj_mod_bwd(dproj[None], w_of("abin"), saved[0, 1, "x"], vecs[0, 1], dxs, dgate, "ab_in_bwd", ph)
    )
    grad["abin"] = grad_half("abin", h, [dproj], False, "dw_ab_in_send")
    (csum["out01"],) = _split_wait(out01, [grad["abin"]], "reduce_out01_wait")
    chip_sum("out01", carried=(pair_exchange("abin"),))
    csum["about"] = grad_half("about", cat, [dy], True, "dw_ab_out", pair_broadcast("out01"))
    csum["abin"] = grad_half("abin", h, [dproj], True, "dw_ab_in")

    layout = {}
    tail = {}

    def after_01():
        tail["01"] = _split_start(chip_exchange("in01", "abin", "about"), "reduce_01_start")
        return (_after(tail["01"].token),)

    def pack_small_grads():
        dvec_all = jnp.stack([dvecs[l, sub] for l in range(n_layers) for sub in range(3)])
        dgain = dvec_all[:, 0, :]
        dmod = dvec_all[:, 1:4, :].reshape(3 * 3 * n_layers, d)
        rows = {
            "norm_g": (dgain, None, dq), "final_g": (aux[0:1], 0, d), "pool_scale": (small_g["pool_scale"], None, dq),
            "b_mod": (dmod, 0, d), "ab_norm_v": (small_g["ab_norm_v"], 0, da),
            "ab_conv_w": (small_g["ab_conv_w"], None, db // N_CHIPS), "ab_b_s": (small_g["ab_b_s"], 0, chunk),
            "loss": (aux[1:2], 0, d),
        }
        row0 = 0
        for nm, (pc, col0, cols) in rows.items():
            layout[nm] = (row0, pc.shape[0], col0, cols)
            row0 += pc.shape[0]
        packed_rows = -(-row0 // 8) * 8
        return sum(
            jnp.pad(pc, ((layout[nm][0], packed_rows - layout[nm][0] - pc.shape[0]), (0, d - pc.shape[1])))
            for nm, (pc, _, _) in rows.items()
        )

    def bcast_01():
        csum["in01"], csum["abin"], csum["about"] = _split_wait(tail["01"], [dvecs[0, 0]], "reduce_01_wait")
        chip_sum("in01", "abin", "about")
        grads_small = [pack_small_grads(), small_g["ab_w_s"].reshape(heads * chunk, chunk)]
        tail["small"] = _split_start(small_gather("grads", grads_small), "gather_small_grads_start")
        return (pair_broadcast("in01", "abin", "about"), _after(tail["small"].token))

    def reduce_out00():
        tail["out00"] = _split_start(chip_exchange("out00"), "reduce_out00_start")
        return (_after(tail["out00"].token),)

    dxs = ffn_backward(dxs, 0, 0, 0, after_01, bcast_01, reduce_out00)
    grad_x = dxs.reshape(x.shape)

    last = _split_start(chip_exchange("in00"), "reduce_last_start")
    (csum["out00"],) = _split_wait(tail["out00"], [last.token], "reduce_out00_wait")
    chip_sum("out00")
    _flush("broadcast_out00", pair_broadcast("out00"))
    _split_wait(tail["small"], [reduced["w_ffn_out"]], "gather_small_grads_wait")
    g_all, gws_all = small["grads"]

    out = {}

    def adam_stack(st, after=()):
        w3, m3, v3 = stacks[st]
        assert all(u in done for u, (_, ust, _) in units.items() if ust == st), st
        shape = {"w_ffn_in": w_ffn_in.shape, "w_ffn_out": w_ffn_out.shape, "pool_w_grp": pool_w_grp.shape}.get(st, w3.shape)
        out[st] = tuple(a.reshape(shape) for a in _adam_stack(w3, reduced[st], m3, v3, "adam_" + st, after))

    for st in ("w_ffn_out", "ab_w_in", "ab_w_out", "pool_w_grp"):
        adam_stack(st, (last.token,))

    shapes2d = {
        "norm_g": (3 * n_layers, dq), "b_mod": (9 * n_layers, d), "final_g": (1, d), "ab_norm_v": (1, da),
        "pool_scale": (1, dq), "ab_conv_w": (3, db // N_CHIPS), "ab_b_s": (heads, chunk), "ab_w_s": (heads * chunk, chunk),
    }
    small_w = {"norm_g": (norm_g, m_norm_g, v_norm_g), "b_mod": (b_mod, m_b_mod, v_b_mod), "final_g": (final_g, m_final_g, v_final_g),
               "ab_norm_v": (ab_norm_v, m_ab_norm_v, v_ab_norm_v), "pool_scale": (pool_scale, m_pool_scale, v_pool_scale),
               "ab_conv_w": (ab_conv_w, m_ab_conv_w, v_ab_conv_w), "ab_b_s": (ab_b_s, m_ab_b_s, v_ab_b_s), "ab_w_s": (ab_w_s, m_ab_w_s, v_ab_w_s)}
    smalls = {nm: tuple(a.reshape(shapes2d[nm]) for a in wmv) for nm, wmv in small_w.items()}
    small_out, loss = _small_adam(g_all, gws_all, layout, smalls, chip)
    loss = loss.reshape(())
    for nm, res in small_out.items():
        out[nm] = tuple(a.reshape(small_w[nm][0].shape) for a in res)

    mod_row0 = layout["b_mod"][0]
    dmod_all = g_all[:, mod_row0 : mod_row0 + 9 * n_layers, :].reshape(N_DEV, n_layers, 9 * d)
    dmod_cols = lax.dynamic_slice(dmod_all, (0, 0, chip * ncol), (N_DEV, n_layers, ncol)).transpose(1, 0, 2)
    out["w_mod"] = tuple(_mod_bwd_adam(c_all.T, dmod_cols, w_mod, m_w_mod, v_w_mod, (last.token,)))

    (csum["in00"],) = _split_wait(
        last, [out[st][1] for st in ("w_mod", "w_ffn_out", "ab_w_in", "ab_w_out", "pool_w_grp")], "reduce_last_wait"
    )
    chip_sum("in00")
    _flush("broadcast_last", pair_broadcast("in00"))
    adam_stack("w_ffn_in")

    order = ["norm_g", "w_mod", "b_mod", "w_ffn_in", "w_ffn_out", "ab_w_in", "ab_norm_v", "ab_w_s", "ab_b_s", "ab_conv_w", "ab_w_out", "pool_w_grp", "pool_scale", "final_g"]
    return (loss, grad_x, *[out[nm][0] for nm in order], *[out[nm][1] for nm in order], *[out[nm][2] for nm in order], *[out[nm][3] for nm in order])
```

```python
import functools
import math

import jax
import jax.numpy as jnp
from jax import lax
from jax.experimental import pallas as pl
from jax.experimental.pallas import tpu as pltpu

F32 = jnp.float32
BF16 = jnp.bfloat16
MESH = pl.DeviceIdType.MESH

EPS = 1e-6
ADAM_LR = 0.001
ADAM_B1 = 0.9
ADAM_B2 = 0.999
ADAM_EPS = 1e-08
ADAM_WD = 0.01
ADAM_STEP = 10
POOL_WINDOWS = (2, 4, 8, 16)
POOL_HALO = 16
CONV_HALO = 8
N_CHIPS = 4
N_DEV = 8
VMEM_LIMIT_BYTES = 48 * 1024 * 1024
EW_BLOCK_ELEMS = 1024 * 1024
ADAM_BLOCK_ELEMS = 512 * 1024


def _pick(n, prefs):
    for p in prefs:
        if p <= n and n % p == 0:
            return p
    return n


def _row_tile(rows, cols, block_elems=EW_BLOCK_ELEMS):
    best = None
    for d in range(16, rows + 1, 16):
        if rows % d == 0 and d * cols <= block_elems:
            best = d
    return best or rows


def _dot(a, b):
    return jnp.dot(a, b, preferred_element_type=F32)


def _dot_nt(a, b):
    return lax.dot_general(a, b, (((1,), (1,)), ((), ())), preferred_element_type=F32)


def _dot_tn(a, b):
    return lax.dot_general(a, b, (((0,), (0,)), ((), ())), preferred_element_type=F32)


def _sigmoid(x):
    return 0.5 * jnp.tanh(0.5 * x) + 0.5


_GELU_C = math.sqrt(2.0 / math.pi)


def _gelu(x):
    x2 = x * x
    t = jnp.tanh(_GELU_C * (x + 0.044715 * x2 * x))
    val = 0.5 * x * (1.0 + t)
    grad = 0.5 * (1.0 + t) + 0.5 * x * (1.0 - t * t) * (_GELU_C * (1.0 + 3.0 * 0.044715 * x2))
    return val, grad


def _rstd(x):
    return lax.rsqrt(jnp.mean(x * x, axis=-1, keepdims=True) + EPS)


def _modulate(x, vec_ref):
    return (x * _rstd(x)) * vec_ref[0:1, :] * (1.0 + vec_ref[2:3, :]) + vec_ref[1:2, :]


def _modulate_bwd(x, dh, vec_ref, dvec_ref):
    gn, sh, sc = vec_ref[0:1, :], vec_ref[1:2, :], vec_ref[2:3, :]
    rstd = _rstd(x)
    r = x * rstd
    dvec_ref[0:1, :] += jnp.sum(dh * r * (1.0 + sc), axis=0, keepdims=True)
    dvec_ref[1:2, :] += jnp.sum(dh, axis=0, keepdims=True)
    dvec_ref[2:3, :] += jnp.sum(dh * r * gn, axis=0, keepdims=True)
    gm = gn * (1.0 + sc)
    dr = dh * gm
    dx = rstd * (dr - r * jnp.mean(dr * r, axis=-1, keepdims=True))
    return dx, r * gm + sh


def _adam(w, g, m, v):
    m = ADAM_B1 * m + (1.0 - ADAM_B1) * g
    v = ADAM_B2 * v + (1.0 - ADAM_B2) * (g * g)
    m_hat = m / (1.0 - ADAM_B1**ADAM_STEP)
    v_hat = v / (1.0 - ADAM_B2**ADAM_STEP)
    delta = -ADAM_LR * (m_hat / (jnp.sqrt(v_hat) + ADAM_EPS) + ADAM_WD * w)
    return delta, m, v


_ANY = pl.BlockSpec(memory_space=pl.ANY)


class _Phase:
    def __init__(self, ins, out_shapes, aliases, n_sems, start, finish, then):
        self.ins, self.out_shapes, self.aliases, self.n_sems = list(ins), list(out_shapes), dict(aliases), n_sems
        self.start, self.finish, self.then = start, finish, then


def _call(body, name, grid, in_specs, out_specs, out_shape, ins, scratch=(), prefetch=(), phases=(), in_place=None):
    n_pre, n_in, n_out, n_sc = len(prefetch), len(in_specs), len(out_specs), len(scratch)
    ph_in = [len(p.ins) for p in phases]
    ph_out = [len(p.out_shapes) for p in phases]

    def kernel_body(*refs):
        pos = [0]

        def take(k):
            pos[0] += k
            return refs[pos[0] - k : pos[0]]

        pre, ins_ = take(n_pre), take(n_in)
        p_ins = [take(k) for k in ph_in]
        outs_ = take(n_out)
        p_outs = [take(k) for k in ph_out]
        sc = take(n_sc)
        sems = [take(2) for _ in phases]
        if phases:
            ids = [pl.program_id(a) for a in range(len(grid))]
            first = functools.reduce(jnp.logical_and, [i == 0 for i in ids])
            last = functools.reduce(jnp.logical_and, [i == g - 1 for i, g in zip(ids, grid)])

            @pl.when(first)
            def _():
                for p, pi, po, (send, recv) in zip(phases, p_ins, p_outs, sems):
                    p.start(pi, po, send, recv)

        if body is not None:
            body(*pre, *ins_, *outs_, *sc)
        if phases:

            @pl.when(last)
            def _():
                for p, pi, po, (send, recv) in zip(phases, p_ins, p_outs, sems):
                    p.finish(pi, po, send, recv)

    aliases = {n_pre + i: o for i, o in (in_place or {}).items()}
    i0, o0 = n_pre + n_in, n_out
    for p in phases:
        for i, o in p.aliases.items():
            aliases[i0 + i] = o0 + o
        i0 += len(p.ins)
        o0 += len(p.out_shapes)
    all_in = list(in_specs) + [_ANY] * sum(ph_in)
    all_out = list(out_specs) + [_ANY] * sum(ph_out)
    all_scratch = list(scratch)
    for p in phases:
        all_scratch += [pltpu.SemaphoreType.DMA((p.n_sems,)), pltpu.SemaphoreType.DMA((p.n_sems,))]
    shapes = list(out_shape) + [s for p in phases for s in p.out_shapes]
    operands = list(prefetch) + list(ins) + [a for p in phases for a in p.ins]
    sem = ("arbitrary",) * len(grid)
    params = pltpu.CompilerParams(dimension_semantics=sem, vmem_limit_bytes=VMEM_LIMIT_BYTES)
    if n_pre:
        res = pl.pallas_call(
            kernel_body, name=name, out_shape=shapes, input_output_aliases=aliases, compiler_params=params,
            grid_spec=pltpu.PrefetchScalarGridSpec(
                num_scalar_prefetch=n_pre, grid=grid, in_specs=all_in, out_specs=all_out, scratch_shapes=all_scratch
            ),
        )(*operands)
    else:
        res = pl.pallas_call(
            kernel_body, name=name, grid=grid, in_specs=all_in, out_specs=all_out, out_shape=shapes,
            scratch_shapes=all_scratch, input_output_aliases=aliases, compiler_params=params,
        )(*operands)
    res = list(res)
    outs, rest = res[:n_out], res[n_out:]
    p_res = []
    for k in ph_out:
        p_res.append(rest[:k])
        rest = rest[k:]
    return outs, p_res


def _place():
    return lax.axis_index("x"), lax.axis_index("y"), lax.axis_index("c")


def _other_chips():
    x, y, _ = _place()
    return [(1 - x, y), (x, 1 - y), (1 - x, 1 - y)]


def _flip(k):
    x, y, c = _place()
    return (1 - x if k & 4 else x, 1 - y if k & 2 else y, 1 - c if k & 1 else c)


def _remote(src, dst, send, recv, k, to):
    return pltpu.make_async_remote_copy(
        src_ref=src, dst_ref=dst, send_sem=send.at[k], recv_sem=recv.at[k], device_id=to, device_id_type=MESH
    )


def _phase_small_gather(arrs, then):
    n = len(arrs)

    def copies(ins, outs, send, recv):
        x, y, c = _place()
        me = 4 * x + 2 * y + c
        local = [pltpu.make_async_copy(ins[a], outs[a].at[me], send.at[a * N_DEV]) for a in range(n)]
        remote = [_remote(ins[a], outs[a].at[me], send, recv, a * N_DEV + k, _flip(k)) for a in range(n) for k in range(1, N_DEV)]
        return local, remote

    def start(ins, outs, send, recv):
        local, remote = copies(ins, outs, send, recv)
        for cp in local + remote:
            cp.start()

    def finish(ins, outs, send, recv):
        local, remote = copies(ins, outs, send, recv)
        for cp in remote + local:
            cp.wait()

    shapes = [jax.ShapeDtypeStruct((N_DEV,) + a.shape, a.dtype) for a in arrs]
    return _Phase(arrs, shapes, {}, n * N_DEV, start, finish, then)


def _phase_small_exchange(arr, then):
    def copies(ins, outs, send, recv):
        x, y, c = _place()
        me = 4 * x + 2 * y + c
        local = pltpu.make_async_copy(ins[0].at[me], outs[0].at[me], send.at[0])
        remote = []
        for k in range(1, N_DEV):
            px, py, pc = _flip(k)
            remote.append(_remote(ins[0].at[4 * px + 2 * py + pc], outs[0].at[me], send, recv, k, (px, py, pc)))
        return [local] + remote

    def start(ins, outs, send, recv):
        for cp in copies(ins, outs, send, recv):
            cp.start()

    def finish(ins, outs, send, recv):
        for cp in copies(ins, outs, send, recv):
            cp.wait()

    return _Phase([arr], [jax.ShapeDtypeStruct(arr.shape, arr.dtype)], {}, N_DEV, start, finish, then)


def _after(*arrs):
    nothing = lambda *args: None
    return _Phase(arrs, [], {}, 1, nothing, nothing, nothing)


def _flush(name, *phases):
    _, p_outs = _call(None, name, (1,), [], [], [], [], phases=list(phases))
    for p, po in zip(phases, p_outs):
        p.then(po)


class _Big:
    KINDS = {"full": (True, True), "half": (True, False), "shard": (False, True), "block": (False, False)}

    def __init__(self, f3, s3, h3):
        assert s3 != h3
        self.f3, self.s3, self.h3 = tuple(f3), s3, h3
        self.bd = tuple(f3[a] // (N_CHIPS if a == s3 else 1) // (2 if a == h3 else 1) for a in range(3))
        self.tile = (1, _row_tile(self.bd[1], self.bd[2]), self.bd[2])
        self.grid = tuple(self.bd[a] // self.tile[a] for a in range(3))

    def dims(self, kind):
        chips, halves = self.KINDS[kind]
        return tuple(
            self.bd[a] * (N_CHIPS if chips and a == self.s3 else 1) * (2 if halves and a == self.h3 else 1) for a in range(3)
        )

    def view(self, ref, chip=None, half=None, batch0=0, both_halves=True, part=None):
        start = [batch0, 0, 0]
        size = list(ref.shape)
        size[0] = self.bd[0] * (2 if self.h3 == 0 and both_halves else 1)
        if chip is not None:
            start[self.s3] += chip * self.bd[self.s3]
            size[self.s3] = self.bd[self.s3]
        if half is not None:
            start[self.h3] += half * self.bd[self.h3]
            size[self.h3] = self.bd[self.h3]
        if part is not None:
            size[1] //= 2
            start[1] += part * size[1]
        return ref.at[tuple(pl.ds(st, sz) for st, sz in zip(start, size))]

    def spec(self, chip_from=None, half_from=None, lead=(), batch0=0):
        extra = "grid" in (chip_from, half_from)

        def index(*args):
            pref, idx = args[-1], list(args[int(extra) : -1])
            idx[0] += batch0
            if chip_from:
                idx[self.s3] += (pref[0] if chip_from == "pref" else args[0]) * self.grid[self.s3]
            if half_from:
                idx[self.h3] += (pref[1] if half_from == "pref" else args[0]) * self.grid[self.h3]
            return (0,) * len(lead) + tuple(idx)

        return pl.BlockSpec(tuple(lead) + self.tile, index)


def _same(arrs):
    return [jax.ShapeDtypeStruct(a.shape, a.dtype) for a in arrs]


def _phase_gather_relay(arrs, bigs, second, whole_first, then):
    n = len(arrs)
    per = 4 if second and not whole_first else 2

    def copies(outs, send, recv, arriving):
        x, y, c = _place()
        me, xn, yn, dg = (x, y), (1 - x, y), (x, 1 - y), (1 - x, 1 - y)
        if not second:
            part = (None, None) if whole_first else (0, 1)
            plan = [((xn if arriving else me), part[0], xn), ((yn if arriving else me), part[1], yn)]
        elif whole_first:
            plan = [(dg, 0, yn), (dg, 1, xn)] if arriving else [(xn, 0, yn), (yn, 1, xn)]
        elif arriving:
            plan = [(yn, 0, yn), (dg, 0, yn), (xn, 1, xn), (dg, 1, xn)]
        else:
            plan = [(me, 0, yn), (xn, 0, yn), (me, 1, xn), (yn, 1, xn)]
        res = []
        for a in range(n):
            for k, (chip, part, to) in enumerate(plan):
                blk = bigs[a].view(outs[a], 2 * chip[0] + chip[1], c, part=part)
                res.append(_remote(blk, blk, send, recv, per * a + k, (*to, c)))
        return res

    def start(ins, outs, send, recv):
        for cp in copies(outs, send, recv, False):
            cp.start()

    def finish(ins, outs, send, recv):
        for cp in copies(outs, send, recv, True):
            cp.wait_recv()
        for cp in copies(outs, send, recv, False):
            cp.wait_send()

    return _Phase(arrs, _same(arrs), {a: a for a in range(n)}, per * n, start, finish, then)


def _phase_gather_sibling(arrs, bigs, then):
    n = len(arrs)

    def copies(outs, send, recv, arriving):
        x, y, c = _place()
        return [
            _remote(blk, blk, send, recv, 3 * a + j, (x, y, 1 - c))
            for j, chip in enumerate(_other_chips())
            for a in range(n)
            for blk in [bigs[a].view(outs[a], 2 * chip[0] + chip[1], 1 - c if arriving else c)]
        ]

    def start(ins, outs, send, recv):
        for cp in copies(outs, send, recv, False):
            cp.start()

    def finish(ins, outs, send, recv):
        for cp in copies(outs, send, recv, True):
            cp.wait_recv()
        for cp in copies(outs, send, recv, False):
            cp.wait_send()

    return _Phase(arrs, _same(arrs), {a: a for a in range(n)}, 3 * n, start, finish, then)


def _phase_pair_exchange(grads, bigs, then):
    n = len(grads)

    def copies(ins, outs, send, recv):
        x, y, c = _place()
        srcs = [ins[a] if ins[a].shape == outs[a].shape else bigs[a].view(ins[a], None, 1 - c) for a in range(n)]
        return [_remote(srcs[a], outs[a], send, recv, a, (x, y, 1 - c)) for a in range(n)]

    def start(ins, outs, send, recv):
        for cp in copies(ins, outs, send, recv):
            cp.start()

    def finish(ins, outs, send, recv):
        for cp in copies(ins, outs, send, recv):
            cp.wait()

    shapes = [jax.ShapeDtypeStruct(b.dims("half"), BF16) for b in bigs]
    return _Phase(grads, shapes, {}, n, start, finish, then)


def _phase_chip_exchange(sums, bigs, then):
    n = len(sums)

    def copies(ins, outs, send, recv):
        _, _, c = _place()
        return [
            _remote(bigs[a].view(ins[a], 2 * chip[0] + chip[1], both_halves=False), outs[a].at[j], send, recv, 3 * a + j, (*chip, c))
            for j, chip in enumerate(_other_chips())
            for a in range(n)
        ]

    def start(ins, outs, send, recv):
        for cp in copies(ins, outs, send, recv):
            cp.start()

    def finish(ins, outs, send, recv):
        for cp in copies(ins, outs, send, recv):
            cp.wait()

    shapes = [jax.ShapeDtypeStruct((N_CHIPS - 1,) + b.dims("block"), BF16) for b in bigs]
    return _Phase(sums, shapes, {}, 3 * n, start, finish, then)


_HBM = pl.BlockSpec(memory_space=pltpu.HBM)
_SEM = pl.BlockSpec(memory_space=pltpu.SEMAPHORE)
_DATAFLOW = pltpu.SideEffectType.DATAFLOW_SIDE_EFFECTING


class _InFlight:
    def __init__(self, phase, send, recv, arrays, token):
        self.phase, self.send, self.recv, self.arrays, self.token = phase, send, recv, arrays, token


def _phase_results(phase, refs):
    n_in = len(phase.ins)
    updated = {o: i for i, o in phase.aliases.items()}
    fresh = [o for o in range(len(phase.out_shapes)) if o not in updated]
    return [refs[updated[o]] if o in updated else refs[n_in + fresh.index(o)] for o in range(len(phase.out_shapes))]


def _split_starts(phases, name):
    arrays, spans = [], []
    for phase in phases:
        fresh = [s for o, s in enumerate(phase.out_shapes) if o not in phase.aliases.values()]
        spans.append((len(arrays), len(phase.ins) + len(fresh)))
        arrays += list(phase.ins) + [lax.empty(s.shape, s.dtype) for s in fresh]
    n, k = len(arrays), len(phases)

    def body(*refs):
        for i, (phase, (first, count)) in enumerate(zip(phases, spans)):
            mine = refs[first : first + count]
            phase.start(mine[: len(phase.ins)], _phase_results(phase, mine), refs[n + 2 * i], refs[n + 2 * i + 1])
        refs[-1][...] = jnp.zeros_like(refs[-1])

    operands = [pltpu.with_memory_space_constraint(a, pltpu.HBM) for a in arrays]
    res = pl.pallas_call(
        body, name=name,
        out_shape=[pltpu.SemaphoreType.DMA((phase.n_sems,)) for phase in phases for _ in range(2)]
        + [pltpu.HBM(a.shape, a.dtype) for a in arrays] + [jax.ShapeDtypeStruct((8, 128), F32)],
        in_specs=[_HBM] * n, out_specs=[_SEM] * (2 * k) + [_HBM] * n + [pl.BlockSpec(memory_space=pltpu.VMEM)],
        input_output_aliases={i: 2 * k + i for i in range(n)},
        compiler_params=pltpu.CompilerParams(has_side_effects=_DATAFLOW),
    )(*operands)
    return [
        _InFlight(phase, res[2 * i], res[2 * i + 1], list(res[2 * k + first : 2 * k + first + count]), res[-1])
        for i, (phase, (first, count)) in enumerate(zip(phases, spans))
    ]


def _split_start(phase, name):
    return _split_starts([phase], name)[0]


def _split_waits(flights, after, name):
    arrays, spans = [], []
    for flight in flights:
        spans.append((len(arrays), len(flight.arrays)))
        arrays += flight.arrays
    n, k = len(arrays), len(flights)

    def body(*refs):
        for i, (flight, (first, count)) in enumerate(zip(flights, spans)):
            phase, mine = flight.phase, refs[first : first + count]
            phase.finish(mine[: len(phase.ins)], _phase_results(phase, mine), refs[n + 2 * i], refs[n + 2 * i + 1])

    res = pl.pallas_call(
        body, name=name, out_shape=[pltpu.HBM(a.shape, a.dtype) for a in arrays],
        in_specs=[_HBM] * n + [_SEM] * (2 * k) + [_ANY] * len(after), out_specs=[_HBM] * n,
        input_output_aliases={i: i for i in range(n)},
        compiler_params=pltpu.CompilerParams(has_side_effects=_DATAFLOW),
    )(*arrays, *[sem for flight in flights for sem in (flight.send, flight.recv)], *after)
    operands = []
    for flight, (first, count) in zip(flights, spans):
        mine = list(res[first : first + count])
        flight.phase.then(_phase_results(flight.phase, mine))
        operands.append(mine[: len(flight.phase.ins)])
    return operands


def _split_wait(flight, after, name):
    return _split_waits([flight], after, name)[0]


def _phase_pair_broadcast(stacks, bigs, batch0s, then):
    n = len(stacks)

    def start(ins, outs, send, recv):
        x, y, c = _place()
        for a in range(n):
            blk = bigs[a].view(outs[a], None, c, batch0s[a])
            _remote(blk, blk, send, recv, a, (x, y, 1 - c)).start()

    def finish(ins, outs, send, recv):
        x, y, c = _place()
        for a in range(n):
            mine = bigs[a].view(outs[a], None, c, batch0s[a])
            theirs = bigs[a].view(outs[a], None, 1 - c, batch0s[a])
            _remote(mine, mine, send, recv, a, (x, y, 1 - c)).wait_send()
            _remote(theirs, theirs, send, recv, a, (x, y, 1 - c)).wait_recv()

    return _Phase(stacks, _same(stacks), {a: a for a in range(n)}, n, start, finish, then)


def _tile_call(body, name, big, where, extra, ins, in_specs, out_specs, out_shape, phases=()):
    grid = ((extra,) if extra else ()) + big.grid
    return _call(body, name, grid, in_specs, out_specs, out_shape, ins, prefetch=(where,), phases=phases)


def _cast_into_full(w_stack, batch0, big, where, name, phases=()):
    def body(_, w_ref, o_ref):
        o_ref[...] = w_ref[...].astype(BF16)

    return _tile_call(
        body, name, big, where, 2, [w_stack], [big.spec(None, "grid", batch0=batch0)], [big.spec("pref", "grid")],
        [jax.ShapeDtypeStruct(big.dims("full"), BF16)], phases,
    )


def _pair_sum(g_full, recv_half, big, where, name, phases=()):
    def body(_, g_ref, r_ref, o_ref):
        o_ref[...] = (g_ref[...].astype(F32) + r_ref[...].astype(F32)).astype(BF16)

    half = big.spec("grid", None)
    return _tile_call(
        body, name, big, where, N_CHIPS, [g_full, recv_half], [big.spec("grid", "pref"), half], [half],
        [jax.ShapeDtypeStruct(big.dims("half"), BF16)], phases,
    )


def _chip_sum(chip_sum, parts, big, where, stack, stack_shape, batch0, name, phases=()):
    def body(_, own_ref, p_ref, *rest):
        acc = own_ref[...].astype(F32)
        for k in range(N_CHIPS - 1):
            acc = acc + p_ref[k].astype(F32)
        rest[-1][...] = acc

    ins = [chip_sum, parts] + ([stack] if stack is not None else [])
    in_specs = [big.spec("pref", None), big.spec(None, None, lead=(N_CHIPS - 1,))] + ([_ANY] if stack is not None else [])
    return _call(
        body, name, big.grid, in_specs, [big.spec(None, "pref", batch0=batch0)], [jax.ShapeDtypeStruct(stack_shape, F32)], ins,
        prefetch=(where,), phases=phases, in_place={2: 0} if stack is not None else None,
    )


def _adam_stack(w, g, m, v, name, after=()):
    b, r, c = w.shape
    tr = _row_tile(r, c, ADAM_BLOCK_ELEMS)

    def body(w_ref, g_ref, m_ref, v_ref, *rest):
        go_ref, d_ref, mo_ref, vo_ref = rest[-4:]
        gv = g_ref[...]
        d, mo, vo = _adam(w_ref[...], gv, m_ref[...], v_ref[...])
        go_ref[...] = gv
        d_ref[...] = d
        mo_ref[...] = mo
        vo_ref[...] = vo

    spec = pl.BlockSpec((1, tr, c), lambda bb, i: (bb, i, 0))
    outs, _ = _call(
        body, name, (b, r // tr), [spec] * 4 + [_ANY] * len(after), [spec] * 4, [jax.ShapeDtypeStruct(w.shape, F32)] * 4,
        [w, g, m, v, *after],
    )
    return outs


def _mod_fwd(c_all, w_mod, b_cols, phases=()):
    n_layers, d, n = w_mod.shape
    tn = _pick(n, (768, 512, 384, 256, 128))

    def body(c_ref, w_ref, b_ref, o_ref):
        cv = c_ref[...]
        ca = (cv * _sigmoid(cv)).astype(BF16)
        o_ref[0] = _dot(ca, w_ref[0].astype(BF16)) + b_ref[0]

    return _call(
        body, "mod_fwd", (n_layers, n // tn),
        [
            pl.BlockSpec((N_DEV, d), lambda l, j: (0, 0)),
            pl.BlockSpec((1, d, tn), lambda l, j: (l, 0, j)),
            pl.BlockSpec((1, 1, tn), lambda l, j: (l, 0, j)),
        ],
        [pl.BlockSpec((1, N_DEV, tn), lambda l, j: (l, 0, j))],
        [jax.ShapeDtypeStruct((n_layers, N_DEV, n), F32)], [c_all, w_mod, b_cols], phases=phases,
    )


def _mod_bwd_adam(c_all_t, dmod_cols, w, m, v, after=()):
    n_layers, d, n = w.shape
    tn = _pick(n, (384, 256, 128))

    def body(c_ref, dm_ref, w_ref, m_ref, v_ref, *rest):
        g_ref, d_ref, mo_ref, vo_ref = rest[-4:]
        cv = c_ref[...]
        ca = (cv * _sigmoid(cv)).astype(BF16)
        g = _dot(ca, dm_ref[0].astype(BF16))
        g_ref[0] = g
        dl, mo, vo = _adam(w_ref[0], g, m_ref[0], v_ref[0])
        d_ref[0] = dl
        mo_ref[0] = mo
        vo_ref[0] = vo

    wspec = pl.BlockSpec((1, d, tn), lambda l, j: (l, 0, j))
    outs, _ = _call(
        body, "mod_bwd_adam", (n_layers, n // tn),
        [pl.BlockSpec((d, N_DEV), lambda l, j: (0, 0)), pl.BlockSpec((1, N_DEV, tn), lambda l, j: (l, 0, j)), wspec, wspec, wspec]
        + [_ANY] * len(after),
        [wspec] * 4, [jax.ShapeDtypeStruct(w.shape, F32)] * 4, [c_all_t, dmod_cols, w, m, v, *after],
    )
    return outs


def _ffn_fwd(x, vec, w_in, w_out, name, phases=()):
    s, d = x.shape
    f = w_out.shape[1]
    tm = _pick(s, (1024, 512, 256, 128))
    tf = _pick(f, (256, 128))
    nf = f // tf

    def body(x_ref, vec_ref, wg_ref, wu_ref, wo_ref, xo_ref, g_ref, u_ref, y_ref, h_sc, acc_sc):
        j = pl.program_id(1)

        @pl.when(j == 0)
        def _():
            h_sc[...] = _modulate(x_ref[...], vec_ref).astype(BF16)
            acc_sc[...] = jnp.zeros_like(acc_sc)

        h = h_sc[...]
        g = _dot(h, wg_ref[0])
        u = _dot(h, wu_ref[0])
        g_ref[...] = g.astype(BF16)
        u_ref[...] = u.astype(BF16)
        a = (g * _sigmoid(g) * u).astype(BF16)
        acc_sc[...] += _dot(a, wo_ref[0])

        @pl.when(j == nf - 1)
        def _():
            yv = acc_sc[...]
            xo_ref[...] = x_ref[...] + 0.5 * vec_ref[3:4, :] * yv
            y_ref[...] = yv.astype(BF16)

    row = pl.BlockSpec((tm, d), lambda i, j: (i, 0))
    hid = pl.BlockSpec((tm, tf), lambda i, j: (i, j))
    return _call(
        body, name, (s // tm, nf),
        [
            row,
            pl.BlockSpec((8, d), lambda i, j: (0, 0)),
            pl.BlockSpec((1, d, tf), lambda i, j: (0, 0, j)),
            pl.BlockSpec((1, d, tf), lambda i, j: (0, 0, nf + j)),
            pl.BlockSpec((1, tf, d), lambda i, j: (0, j, 0)),
        ],
        [row, hid, hid, row],
        [
            jax.ShapeDtypeStruct((s, d), F32),
            jax.ShapeDtypeStruct((s, f), BF16),
            jax.ShapeDtypeStruct((s, f), BF16),
            jax.ShapeDtypeStruct((s, d), BF16),
        ],
        [x, vec, w_in, w_in, w_out],
        scratch=[pltpu.VMEM((tm, d), BF16), pltpu.VMEM((tm, d), F32)], phases=phases,
    )


def _ffn_bwd(dxo, x, vec, gg, uu, y, w_in, w_out, name, phases=()):
    s, d = x.shape
    f = w_out.shape[1]
    tm = _pick(s, (512, 256, 128))
    tf = _pick(f, (256, 128))
    nf = f // tf

    def body(dxo_ref, x_ref, vec_ref, g_ref, u_ref, y_ref, wg_ref, wu_ref, wo_ref,
             dx_ref, dg_ref, du_ref, a_ref, h_ref, dy_ref, dvec_ref, acc_sc):
        i, j = pl.program_id(0), pl.program_id(1)

        @pl.when((i == 0) & (j == 0))
        def _():
            dvec_ref[...] = jnp.zeros_like(dvec_ref)

        @pl.when(j == 0)
        def _():
            dxo_v = dxo_ref[...]
            dy_ref[...] = (0.5 * vec_ref[3:4, :] * dxo_v).astype(BF16)
            dvec_ref[3:4, :] += 0.5 * jnp.sum(dxo_v * y_ref[...].astype(F32), axis=0, keepdims=True)
            acc_sc[...] = jnp.zeros_like(acc_sc)

        da = _dot_nt(dy_ref[...], wo_ref[0])
        g = g_ref[...].astype(F32)
        u = u_ref[...].astype(F32)
        sig = _sigmoid(g)
        sl = g * sig
        a_ref[...] = (sl * u).astype(BF16)
        dg = (da * u * (sig * (1.0 + g * (1.0 - sig)))).astype(BF16)
        du = (da * sl).astype(BF16)
        dg_ref[...] = dg
        du_ref[...] = du
        acc_sc[...] += _dot_nt(dg, wg_ref[0]) + _dot_nt(du, wu_ref[0])

        @pl.when(j == nf - 1)
        def _():
            dx, h = _modulate_bwd(x_ref[...], acc_sc[...], vec_ref, dvec_ref)
            dx_ref[...] = dxo_ref[...] + dx
            h_ref[...] = h.astype(BF16)

    row = pl.BlockSpec((tm, d), lambda i, j: (i, 0))
    hid = pl.BlockSpec((tm, tf), lambda i, j: (i, j))
    vecs = pl.BlockSpec((8, d), lambda i, j: (0, 0))
    return _call(
        body, name, (s // tm, nf),
        [
            row, row, vecs, hid, hid, row,
            pl.BlockSpec((1, d, tf), lambda i, j: (0, 0, j)),
            pl.BlockSpec((1, d, tf), lambda i, j: (0, 0, nf + j)),
            pl.BlockSpec((1, tf, d), lambda i, j: (0, j, 0)),
        ],
        [row, hid, hid, hid, row, row, vecs],
        [
            jax.ShapeDtypeStruct((s, d), F32),
            jax.ShapeDtypeStruct((s, f), BF16),
            jax.ShapeDtypeStruct((s, f), BF16),
            jax.ShapeDtypeStruct((s, f), BF16),
            jax.ShapeDtypeStruct((s, d), BF16),
            jax.ShapeDtypeStruct((s, d), BF16),
            jax.ShapeDtypeStruct((8, d), F32),
        ],
        [dxo, x, vec, gg, uu, y, w_in, w_in, w_out],
        scratch=[pltpu.VMEM((tm, d), F32)], phases=phases,
    )


def _grad_half(a, bs, big, where, mine, recv, name, phases=()):
    s, k1 = a.shape
    n = bs[0].shape[1]
    groups = len(bs)
    rows_halved = big.h3 == 1
    assert rows_halved or groups == 1
    kk, nn = (k1 // 2, n) if rows_halved else (k1, n // 2)
    tk = _pick(kk, (1408, 1024, 512, 256, 128))
    tn = _pick(nn, (1408, 1024, 640, 512, 256, 128))
    nkb, nnb = kk // tk, nn // tn
    assert (recv is None) == (not mine)

    def half(pref):
        return pref[1] if mine else 1 - pref[1]

    def body(_, a_ref, *rest):
        q = pl.program_id(1)
        for p in range(groups):

            @pl.when(q == p)
            def _(p=p):
                acc = _dot_tn(a_ref[...], rest[p][...])
                if recv is not None:
                    acc = acc + rest[groups][0].astype(F32)
                rest[-1][0] = acc.astype(BF16)

    def b_block(p):
        def index(i, q, j, pref):
            jj = jnp.where(q == p, j, jnp.where(q < p, 0, nnb - 1))
            return (0, jj + (0 if rows_halved else half(pref) * nnb))

        return pl.BlockSpec((s, tn), index)

    out_spec = pl.BlockSpec((1, tk, tn), lambda i, q, j, pref: (0, i, q * nnb + j))
    in_specs = [pl.BlockSpec((s, tk), lambda i, q, j, pref: (0, i + (half(pref) * nkb if rows_halved else 0)))]
    in_specs += [b_block(p) for p in range(groups)]
    ins = [a, *bs]
    if recv is not None:
        in_specs.append(out_spec)
        ins.append(recv)
    return _call(
        body, name, (nkb, groups, nnb), in_specs, [out_spec], [jax.ShapeDtypeStruct(big.dims("half"), BF16)], ins,
        prefetch=(where,), phases=phases,
    )


def _proj_mod_fwd(x, vec, w, phases=()):
    s, d = x.shape
    n = w.shape[2]
    tm = _pick(s, (1024, 512, 256, 128))
    tn = _pick(n, (640, 512, 256, 128))

    def body(x_ref, vec_ref, w_ref, o_ref, h_sc):
        @pl.when(pl.program_id(1) == 0)
        def _():
            h_sc[...] = _modulate(x_ref[...], vec_ref).astype(BF16)

        o_ref[...] = _dot(h_sc[...], w_ref[0])

    return _call(
        body, "ab_in_fwd", (s // tm, n // tn),
        [
            pl.BlockSpec((tm, d), lambda i, j: (i, 0)),
            pl.BlockSpec((8, d), lambda i, j: (0, 0)),
            pl.BlockSpec((1, d, tn), lambda i, j: (0, 0, j)),
        ],
        [pl.BlockSpec((tm, tn), lambda i, j: (i, j))],
        [jax.ShapeDtypeStruct((s, n), F32)], [x, vec, w],
        scratch=[pltpu.VMEM((tm, d), BF16)], phases=phases,
    )


def _proj_res_fwd(a, w, x, vec, phases=()):
    s, kd = a.shape
    d = x.shape[1]
    tm = _pick(s, (1024, 512, 256, 128))

    def body(a_ref, w_ref, x_ref, vec_ref, xo_ref, y_ref):
        yv = _dot(a_ref[...], w_ref[0])
        xo_ref[...] = x_ref[...] + vec_ref[3:4, :] * yv
        y_ref[...] = yv.astype(BF16)

    row = pl.BlockSpec((tm, d), lambda i: (i, 0))
    return _call(
        body, "ab_out_fwd", (s // tm,),
        [pl.BlockSpec((tm, kd), lambda i: (i, 0)), pl.BlockSpec((1, kd, d), lambda i: (0, 0, 0)), row, pl.BlockSpec((8, d), lambda i: (0, 0))],
        [row, row],
        [jax.ShapeDtypeStruct((s, d), F32), jax.ShapeDtypeStruct((s, d), BF16)], [a, w, x, vec], phases=phases,
    )


def _proj_res_bwd(dxo, y, vec, w, phases=()):
    s, d = dxo.shape
    kd = w.shape[1]
    tm = _pick(s, (1024, 512, 256, 128))

    def body(dxo_ref, y_ref, vec_ref, w_ref, dy_ref, da_ref, dgate_ref):
        @pl.when(pl.program_id(0) == 0)
        def _():
            dgate_ref[...] = jnp.zeros_like(dgate_ref)

        dxo_v = dxo_ref[...]
        dy = (vec_ref[3:4, :] * dxo_v).astype(BF16)
        dy_ref[...] = dy
        dgate_ref[3:4, :] += jnp.sum(dxo_v * y_ref[...].astype(F32), axis=0, keepdims=True)
        da_ref[...] = _dot_nt(dy, w_ref[0]).astype(BF16)

    row = pl.BlockSpec((tm, d), lambda i: (i, 0))
    vecs = pl.BlockSpec((8, d), lambda i: (0, 0))
    return _call(
        body, "ab_out_bwd", (s // tm,),
        [row, row, vecs, pl.BlockSpec((1, kd, d), lambda i: (0, 0, 0))],
        [row, pl.BlockSpec((tm, kd), lambda i: (i, 0)), vecs],
        [jax.ShapeDtypeStruct((s, d), BF16), jax.ShapeDtypeStruct((s, kd), BF16), jax.ShapeDtypeStruct((8, d), F32)],
        [dxo, y, vec, w], phases=phases,
    )


def _proj_mod_bwd(dproj, w, x, vec, dxo, dvec_in, name, phases=()):
    parts, s, n_part = dproj.shape
    d = x.shape[1]
    tm = _pick(s, (512, 256, 128))
    tk = _pick(n_part, (1408, 1280, 1024, 512, 256, 128))
    per_part = n_part // tk
    nk = parts * per_part

    def body(dp_ref, w_ref, x_ref, vec_ref, dxo_ref, dvi_ref, dx_ref, h_ref, dvec_ref, acc_sc):
        i, k = pl.program_id(0), pl.program_id(1)

        @pl.when((i == 0) & (k == 0))
        def _():
            dvec_ref[...] = dvi_ref[...]

        @pl.when(k == 0)
        def _():
            acc_sc[...] = jnp.zeros_like(acc_sc)

        acc_sc[...] += _dot_nt(dp_ref[0], w_ref[0])

        @pl.when(k == nk - 1)
        def _():
            dx, h = _modulate_bwd(x_ref[...], acc_sc[...], vec_ref, dvec_ref)
            dx_ref[...] = dxo_ref[...] + dx
            h_ref[...] = h.astype(BF16)

    row = pl.BlockSpec((tm, d), lambda i, k: (i, 0))
    vecs = pl.BlockSpec((8, d), lambda i, k: (0, 0))
    return _call(
        body, name, (s // tm, nk),
        [
            pl.BlockSpec((1, tm, tk), lambda i, k: (k // per_part, i, k % per_part)),
            pl.BlockSpec((1, d, tk), lambda i, k: (0, 0, k)),
            row, vecs, row, vecs,
        ],
        [row, row, vecs],
        [jax.ShapeDtypeStruct((s, d), F32), jax.ShapeDtypeStruct((s, d), BF16), jax.ShapeDtypeStruct((8, d), F32)],
        [dproj, w, x, vec, dxo, dvec_in], scratch=[pltpu.VMEM((tm, d), F32)], phases=phases,
    )


def _tril(n):
    return lax.broadcasted_iota(jnp.int32, (n, n), 0) >= lax.broadcasted_iota(jnp.int32, (n, n), 1)


def _layernorm_stats(gv):
    mu = jnp.mean(gv, axis=-1, keepdims=True)
    cen = gv - mu
    rstd = lax.rsqrt(jnp.mean(cen * cen, axis=-1, keepdims=True) + EPS)
    return cen * rstd, rstd


def _shift_down(q, k, above_ref, c_cg, c_xb, first):
    width = q.shape[1]
    rows = lax.broadcasted_iota(jnp.int32, q.shape, 0)
    out = pltpu.roll(q, k, 0)
    for r in range(k):
        src = CONV_HALO - k + r
        above = above_ref[src : src + 1, c_cg : c_cg + width] * above_ref[src : src + 1, c_xb : c_xb + width]
        above = jnp.where(first, 0.0, above)
        out = jnp.where(rows == r, above, out)
    return out


def _ab_mix_fwd(proj, norm_v, w_s, b_rows, conv_w, phases=()):
    s, n = proj.shape
    heads, chunk, _ = w_s.shape
    da = norm_v.shape[1]
    hd = da // heads
    db = conv_w.shape[1]
    tm = _pick(s, (512, 256, 128))

    def body(p_ref, ph_ref, nv_ref, ws_ref, b_ref, cw_ref, o_ref):
        first = pl.program_id(0) == 0
        gu, _ = _gelu(p_ref[:, 0:da])
        gv, _ = _gelu(p_ref[:, da : 2 * da])
        xhat, _ = _layernorm_stats(gv)
        vn = (xhat * nv_ref[...]).astype(BF16)
        mask = _tril(chunk)
        for hh in range(heads):
            wm = jnp.where(mask, ws_ref[hh], 0.0).astype(BF16)
            cols = slice(hh * hd, (hh + 1) * hd)
            for nn in range(tm // chunk):
                rows = slice(nn * chunk, (nn + 1) * chunk)
                z = _dot(wm, vn[rows, cols]) + b_ref[:, cols]
                o_ref[rows, cols] = (gu[rows, cols] * z).astype(BF16)
        c_cg, c_xb = 2 * da + db, 2 * da + 2 * db
        bg = p_ref[:, 2 * da : 2 * da + db]
        q = p_ref[:, c_cg : c_cg + db] * p_ref[:, c_xb : c_xb + db]
        q1 = _shift_down(q, 1, ph_ref, c_cg, c_xb, first)
        q2 = _shift_down(q, 2, ph_ref, c_cg, c_xb, first)
        conv = cw_ref[0:1, :] * q2 + cw_ref[1:2, :] * q1 + cw_ref[2:3, :] * q
        o_ref[:, da : da + db] = (bg * conv).astype(BF16)

    nh = tm // CONV_HALO
    return _call(
        body, "ab_mix_fwd", (s // tm,),
        [
            pl.BlockSpec((tm, n), lambda i: (i, 0)),
            pl.BlockSpec((CONV_HALO, n), lambda i: (jnp.maximum(i * nh - 1, 0), 0)),
            pl.BlockSpec((1, da), lambda i: (0, 0)),
            pl.BlockSpec((heads, chunk, chunk), lambda i: (0, 0, 0)),
            pl.BlockSpec((chunk, da), lambda i: (0, 0)),
            pl.BlockSpec((3, db), lambda i: (0, 0)),
        ],
        [pl.BlockSpec((tm, da + db), lambda i: (i, 0))],
        [jax.ShapeDtypeStruct((s, da + db), BF16)], [proj, proj, norm_v, w_s, b_rows, conv_w], phases=phases,
    )


def _ab_mix_bwd(proj, dcat, norm_v, w_s, b_rows, conv_w, phases=()):
    s, n = proj.shape
    heads, chunk, _ = w_s.shape
    da = norm_v.shape[1]
    hd = da // heads
    db = conv_w.shape[1]
    tm = _pick(s, (512, 256, 128))
    nblk = s // tm
    dhalo = 2 * CONV_HALO

    def body(p_ref, pa_ref, pb_ref, dc_ref, dcb_ref, nv_ref, ws_ref, b_ref, cw_ref,
             dp_ref, dnv_ref, dws_ref, dzs_ref, dcw_ref, dvn_sc):
        i = pl.program_id(0)
        first, last = i == 0, i == nblk - 1

        @pl.when(first)
        def _():
            dnv_ref[...] = jnp.zeros_like(dnv_ref)
            dws_ref[...] = jnp.zeros_like(dws_ref)
            dzs_ref[...] = jnp.zeros_like(dzs_ref)
            dcw_ref[...] = jnp.zeros_like(dcw_ref)

        uu = p_ref[:, 0:da]
        gu, gu_grad = _gelu(uu)
        gv, gv_grad = _gelu(p_ref[:, da : 2 * da])
        xhat, rstd = _layernorm_stats(gv)
        nv = nv_ref[...]
        vn = (xhat * nv).astype(BF16)
        dya = dc_ref[:, 0:da].astype(F32)
        dz = (dya * gu).astype(BF16)
        mask = _tril(chunk)
        for hh in range(heads):
            wm = jnp.where(mask, ws_ref[hh], 0.0).astype(BF16)
            cols = slice(hh * hd, (hh + 1) * hd)
            dws = jnp.zeros((chunk, chunk), F32)
            for nn in range(tm // chunk):
                rows = slice(nn * chunk, (nn + 1) * chunk)
                z = _dot(wm, vn[rows, cols]) + b_ref[:, cols]
                dp_ref[rows, cols] = (dya[rows, cols] * z * gu_grad[rows, cols]).astype(BF16)
                dz_blk = dz[rows, cols]
                dws = dws + _dot_nt(dz_blk, vn[rows, cols])
                dzs_ref[:, cols] += dz_blk.astype(F32)
                dvn = _dot_tn(wm, dz_blk)
                dnv_ref[:, cols] += jnp.sum(dvn * xhat[rows, cols], axis=0, keepdims=True)
                dvn_sc[rows, cols] = dvn
            dws_ref[hh] += jnp.where(mask, dws, 0.0)
        dxhat = dvn_sc[...] * nv
        dgv = rstd * (dxhat - jnp.mean(dxhat, axis=-1, keepdims=True) - xhat * jnp.mean(dxhat * xhat, axis=-1, keepdims=True))
        dp_ref[:, da : 2 * da] = (dgv * gv_grad).astype(BF16)

        c_bg, c_cg, c_xb = 2 * da, 2 * da + db, 2 * da + 2 * db
        bg = p_ref[:, c_bg : c_bg + db]
        cg = p_ref[:, c_cg : c_cg + db]
        xb = p_ref[:, c_xb : c_xb + db]
        q = cg * xb
        q1 = _shift_down(q, 1, pa_ref, c_cg, c_xb, first)
        q2 = _shift_down(q, 2, pa_ref, c_cg, c_xb, first)
        dyb = dc_ref[:, da : da + db].astype(F32)
        conv = cw_ref[0:1, :] * q2 + cw_ref[1:2, :] * q1 + cw_ref[2:3, :] * q
        dp_ref[:, c_bg : c_bg + db] = (dyb * conv).astype(BF16)
        e = dyb * bg
        dcw_ref[0:1, :] += jnp.sum(e * q2, axis=0, keepdims=True)
        dcw_ref[1:2, :] += jnp.sum(e * q1, axis=0, keepdims=True)
        dcw_ref[2:3, :] += jnp.sum(e * q, axis=0, keepdims=True)
        rows = lax.broadcasted_iota(jnp.int32, e.shape, 0)
        dq = cw_ref[2:3, :] * e
        for kk in (1, 2):
            ek = pltpu.roll(e, tm - kk, 0)
            for r in range(kk):
                below = dcb_ref[r : r + 1, da : da + db].astype(F32) * pb_ref[r : r + 1, c_bg : c_bg + db]
                below = jnp.where(last, 0.0, below)
                ek = jnp.where(rows == tm - kk + r, below, ek)
            dq = dq + cw_ref[2 - kk : 3 - kk, :] * ek
        dp_ref[:, c_cg : c_cg + db] = (dq * xb).astype(BF16)
        dp_ref[:, c_xb : c_xb + db] = (dq * cg).astype(BF16)

    nh = tm // CONV_HALO
    nhb = tm // dhalo
    const2 = lambda i: (0, 0)
    return _call(
        body, "ab_mix_bwd", (nblk,),
        [
            pl.BlockSpec((tm, n), lambda i: (i, 0)),
            pl.BlockSpec((CONV_HALO, n), lambda i: (jnp.maximum(i * nh - 1, 0), 0)),
            pl.BlockSpec((CONV_HALO, n), lambda i: (jnp.minimum((i + 1) * nh, s // CONV_HALO - 1), 0)),
            pl.BlockSpec((tm, da + db), lambda i: (i, 0)),
            pl.BlockSpec((dhalo, da + db), lambda i: (jnp.minimum((i + 1) * nhb, s // dhalo - 1), 0)),
            pl.BlockSpec((1, da), const2),
            pl.BlockSpec((heads, chunk, chunk), lambda i: (0, 0, 0)),
            pl.BlockSpec((chunk, da), const2),
            pl.BlockSpec((3, db), const2),
        ],
        [
            pl.BlockSpec((tm, n), lambda i: (i, 0)),
            pl.BlockSpec((1, da), const2),
            pl.BlockSpec((heads, chunk, chunk), lambda i: (0, 0, 0)),
            pl.BlockSpec((chunk, da), const2),
            pl.BlockSpec((3, db), const2),
        ],
        [
            jax.ShapeDtypeStruct((s, n), BF16),
            jax.ShapeDtypeStruct((1, da), F32),
            jax.ShapeDtypeStruct((heads, chunk, chunk), F32),
            jax.ShapeDtypeStruct((chunk, da), F32),
            jax.ShapeDtypeStruct((3, db), F32),
        ],
        [proj, proj, proj, dcat, dcat, norm_v, w_s, b_rows, conv_w],
        scratch=[pltpu.VMEM((tm, da), F32)], phases=phases,
    )


def _pool_counts(tm, i, w):
    t = i * tm + lax.broadcasted_iota(jnp.int32, (tm, 1), 0)
    return jnp.minimum(t + 1, w).astype(F32)


def _pool_fwd(x, vec, w_grp, scale, phases=()):
    s, d = x.shape
    groups, gd, _ = w_grp.shape
    tm = _pick(s, (512, 256, 128))

    def body(x_ref, xa_ref, vec_ref, w_ref, sc_ref, xo_ref, p_ref, o_ref):
        i = pl.program_id(0)
        h = _modulate(x_ref[...], vec_ref)
        ha = jnp.where(i == 0, 0.0, _modulate(xa_ref[...], vec_ref))
        ext = jnp.concatenate([ha, h], axis=0)
        for gi, w in enumerate(POOL_WINDOWS):
            cols = slice(gi * gd, (gi + 1) * gd)
            acc = ext[:, cols]
            step = 1
            while step < w:
                acc = acc + pltpu.roll(acc, step, 0)
                step *= 2
            p = (acc[POOL_HALO:, :] / _pool_counts(tm, i, w) - h[:, cols]).astype(BF16)
            p_ref[:, cols] = p
            o_ref[:, cols] = _dot(p, w_ref[gi]).astype(BF16)
        xo_ref[...] = x_ref[...] + vec_ref[3:4, :] * (o_ref[...].astype(F32) * sc_ref[...])

    nh = tm // POOL_HALO
    row = pl.BlockSpec((tm, d), lambda i: (i, 0))
    return _call(
        body, "pool_fwd", (s // tm,),
        [
            row,
            pl.BlockSpec((POOL_HALO, d), lambda i: (jnp.maximum(i * nh - 1, 0), 0)),
            pl.BlockSpec((8, d), lambda i: (0, 0)),
            pl.BlockSpec((groups, gd, gd), lambda i: (0, 0, 0)),
            pl.BlockSpec((1, d), lambda i: (0, 0)),
        ],
        [row, row, row],
        [jax.ShapeDtypeStruct((s, d), F32), jax.ShapeDtypeStruct((s, d), BF16), jax.ShapeDtypeStruct((s, d), BF16)],
        [x, x, vec, w_grp, scale], phases=phases,
    )


def _pool_bwd(dxo, x, vec, p, o, w_grp, scale, phases=()):
    s, d = x.shape
    groups, gd, _ = w_grp.shape
    tm = _pick(s, (512, 256, 128))
    nblk = s // tm

    def body(dxo_ref, dxb_ref, x_ref, vec_ref, p_ref, o_ref, w_ref, sc_ref, dx_ref, dw_ref, dsc_ref, dvec_ref, dw_sc):
        i = pl.program_id(0)

        @pl.when(i == 0)
        def _():
            dw_sc[...] = jnp.zeros_like(dw_sc)
            dsc_ref[...] = jnp.zeros_like(dsc_ref)
            dvec_ref[...] = jnp.zeros_like(dvec_ref)

        gate, sc = vec_ref[3:4, :], sc_ref[...]
        dxo_v = dxo_ref[...]
        ov = o_ref[...].astype(F32)
        dvec_ref[3:4, :] += jnp.sum(dxo_v * (ov * sc), axis=0, keepdims=True)
        dy = gate * dxo_v
        dsc_ref[...] += jnp.sum(dy * ov, axis=0, keepdims=True)
        dout = (dy * sc).astype(BF16)
        dout_b = jnp.where(i == nblk - 1, 0.0, gate * dxb_ref[...] * sc).astype(BF16)
        for gi, w in enumerate(POOL_WINDOWS):
            cols = slice(gi * gd, (gi + 1) * gd)
            dw_sc[gi] += _dot_tn(p_ref[:, cols], dout[:, cols])
            wb = w_ref[gi]
            dp = _dot_nt(dout[:, cols], wb)
            dp_b = _dot_nt(dout_b[:, cols], wb)
            e = dp / _pool_counts(tm, i, w)
            t_below = (i + 1) * tm + lax.broadcasted_iota(jnp.int32, (POOL_HALO, 1), 0)
            e_b = dp_b / jnp.minimum(t_below + 1, w).astype(F32)
            acc = jnp.concatenate([e, e_b], axis=0)
            step = 1
            while step < w:
                acc = acc + pltpu.roll(acc, tm + POOL_HALO - step, 0)
                step *= 2
            dx_ref[:, cols] = acc[:tm, :] - dp
        dx, _ = _modulate_bwd(x_ref[...], dx_ref[...], vec_ref, dvec_ref)
        dx_ref[...] = dxo_v + dx

        @pl.when(i == nblk - 1)
        def _():
            dw_ref[...] = dw_sc[...].astype(BF16)

    nh = tm // POOL_HALO
    row = pl.BlockSpec((tm, d), lambda i: (i, 0))
    vecs = pl.BlockSpec((8, d), lambda i: (0, 0))
    wspec = pl.BlockSpec((groups, gd, gd), lambda i: (0, 0, 0))
    return _call(
        body, "pool_bwd", (nblk,),
        [
            row,
            pl.BlockSpec((POOL_HALO, d), lambda i: (jnp.minimum((i + 1) * nh, s // POOL_HALO - 1), 0)),
            row, vecs, row, row, wspec,
            pl.BlockSpec((1, d), lambda i: (0, 0)),
        ],
        [row, wspec, pl.BlockSpec((1, d), lambda i: (0, 0)), vecs],
        [
            jax.ShapeDtypeStruct((s, d), F32),
            jax.ShapeDtypeStruct((groups, gd, gd), BF16),
            jax.ShapeDtypeStruct((1, d), F32),
            jax.ShapeDtypeStruct((8, d), F32),
        ],
        [dxo, dxo, x, vec, p, o, w_grp, scale],
        scratch=[pltpu.VMEM((groups, gd, gd), F32)], phases=phases,
    )


def _loss_head(x, gain, target, phases=()):
    s, d = x.shape
    tm = _pick(s, (512, 256, 128))

    def body(x_ref, g_ref, t_ref, dx_ref, aux_ref):
        @pl.when(pl.program_id(0) == 0)
        def _():
            aux_ref[...] = jnp.zeros_like(aux_ref)

        xv = x_ref[...]
        rstd = _rstd(xv)
        r = xv * rstd
        gain_v = g_ref[...]
        err = r * gain_v - t_ref[...]
        aux_ref[1:2, :] += jnp.sum(err * err, axis=0, keepdims=True)
        dout = err * (1.0 / d)
        aux_ref[0:1, :] += jnp.sum(dout * r, axis=0, keepdims=True)
        dr = dout * gain_v
        dx_ref[...] = rstd * (dr - r * jnp.mean(dr * r, axis=-1, keepdims=True))

    row = pl.BlockSpec((tm, d), lambda i: (i, 0))
    return _call(
        body, "loss_head", (s // tm,),
        [row, pl.BlockSpec((1, d), lambda i: (0, 0)), row],
        [row, pl.BlockSpec((8, d), lambda i: (0, 0))],
        [jax.ShapeDtypeStruct((s, d), F32), jax.ShapeDtypeStruct((8, d), F32)], [x, gain, target], phases=phases,
    )


def _small_adam(gathered, gathered_ws, layout, smalls, chip):
    names = list(smalls)
    n = len(names)
    loss_row, _, _, n_feat = layout["loss"]

    def body(*refs):
        chip_ref, g_ref, gws_ref = refs[0], refs[1], refs[2]
        wmv = refs[3 : 3 + 3 * n]
        outs = refs[3 + 3 * n : 3 + 7 * n]
        total = refs[-1]
        total[...] = g_ref[0]
        for kdev in range(1, N_DEV):
            total[...] += g_ref[kdev]
        total_ws = gws_ref[0]
        for kdev in range(1, N_DEV):
            total_ws = total_ws + gws_ref[kdev]
        my_chip = chip_ref[0]
        for a, name in enumerate(names):
            w_ref, m_ref, v_ref = wmv[3 * a : 3 * a + 3]
            if name == "ab_w_s":
                g = total_ws
            else:
                row0, rows, col0, cols = layout[name]
                if col0 is None:
                    g = jnp.zeros((rows, cols), F32)
                    for j in range(N_CHIPS):
                        g = g + jnp.where(my_chip == j, total[row0 : row0 + rows, j * cols : (j + 1) * cols], 0.0)
                else:
                    g = total[row0 : row0 + rows, col0 : col0 + cols]
            dl, mo, vo = _adam(w_ref[...], g, m_ref[...], v_ref[...])
            outs[4 * a][...] = g
            outs[4 * a + 1][...] = dl
            outs[4 * a + 2][...] = mo
            outs[4 * a + 3][...] = vo
        refs[3 + 7 * n][...] = 0.5 * jnp.sum(total[loss_row : loss_row + 1, 0:n_feat], axis=1, keepdims=True) / n_feat

    ins = [gathered, gathered_ws]
    out_shapes = []
    for name in names:
        ins.extend(smalls[name])
        out_shapes.extend([jax.ShapeDtypeStruct(smalls[name][0].shape, F32)] * 4)
    out_shapes.append(jax.ShapeDtypeStruct((1, 1), F32))
    whole = lambda shape: pl.BlockSpec(shape, functools.partial(lambda nd, i, c: (0,) * nd, len(shape)))
    res = pl.pallas_call(
        body, name="small_adam",
        grid_spec=pltpu.PrefetchScalarGridSpec(
            num_scalar_prefetch=1, grid=(1,),
            in_specs=[whole(a.shape) for a in ins], out_specs=[whole(o.shape) for o in out_shapes],
            scratch_shapes=[pltpu.VMEM(gathered.shape[1:], F32)],
        ),
        out_shape=out_shapes,
        compiler_params=pltpu.CompilerParams(dimension_semantics=("arbitrary",), vmem_limit_bytes=VMEM_LIMIT_BYTES),
    )(chip.reshape(1).astype(jnp.int32), *ins)
    return {name: res[4 * a : 4 * a + 4] for a, name in enumerate(names)}, res[4 * n]


def _pad_rows(a, rows=8):
    extra = (-a.shape[0]) % rows
    return jnp.pad(a, ((0, extra), (0, 0))) if extra else a


def _pad_cols(a, cols):
    return jnp.pad(a, ((0, 0), (0, cols - a.shape[1]))) if a.shape[1] < cols else a


def _run(fn, *phases):
    outs, p_outs = fn(list(phases))
    for p, po in zip(phases, p_outs):
        p.then(po)
    return outs


def kernel(x, c, norm_g, w_mod, b_mod, w_ffn_in, w_ffn_out, ab_w_in, ab_norm_v, ab_w_s, ab_b_s, ab_conv_w, ab_w_out, pool_w_grp, pool_scale, final_g, loss_target, m_norm_g, m_w_mod, m_b_mod, m_w_ffn_in, m_w_ffn_out, m_ab_w_in, m_ab_norm_v, m_ab_w_s, m_ab_b_s, m_ab_conv_w, m_ab_w_out, m_pool_w_grp, m_pool_scale, m_final_g, v_norm_g, v_w_mod, v_b_mod, v_w_ffn_in, v_w_ffn_out, v_ab_w_in, v_ab_norm_v, v_ab_w_s, v_ab_b_s, v_ab_conv_w, v_ab_w_out, v_pool_w_grp, v_pool_scale, v_final_g):
    ix, iy, ic = _place()
    chip = 2 * ix + iy
    me = 4 * ix + 2 * iy + ic
    where = jnp.stack([chip, ic]).astype(jnp.int32)
    s, d = x.shape[1], x.shape[2]
    x0 = x.reshape(s, d)
    target = loss_target.reshape(s, d)
    n_layers = norm_g.shape[0]
    dq = d // N_CHIPS
    heads, chunk = ab_w_s.shape[1], ab_w_s.shape[2]
    da = ab_norm_v.shape[1]
    db = ab_conv_w.shape[2] * N_CHIPS
    f_hidden = w_ffn_out.shape[2] * N_CHIPS
    assert n_layers == 2 and da % heads == 0

    cw_pad = _pad_cols(ab_conv_w.reshape(3, db // N_CHIPS), dq)
    packed = jnp.concatenate(
        [_pad_rows(c.reshape(N_CHIPS, dq)), _pad_rows(norm_g.reshape(-1, dq)), _pad_rows(pool_scale.reshape(1, dq)), _pad_rows(cw_pad)],
        axis=0,
    )
    ncol = w_mod.shape[2]
    b_cols = lax.dynamic_slice(b_mod, (0, chip * ncol), (n_layers, ncol)).reshape(n_layers, 1, ncol)
    small = {}

    def small_gather(key, arrs):
        def then(outs):
            small[key] = outs

        return _phase_small_gather(arrs, then)

    stacks = {
        "w_ffn_in": tuple(a.reshape((-1,) + a.shape[2:]) for a in (w_ffn_in, m_w_ffn_in, v_w_ffn_in)),
        "w_ffn_out": tuple(a.reshape((-1,) + a.shape[2:]) for a in (w_ffn_out, m_w_ffn_out, v_w_ffn_out)),
        "ab_w_in": (ab_w_in, m_ab_w_in, v_ab_w_in),
        "ab_w_out": (ab_w_out, m_ab_w_out, v_ab_w_out),
        "pool_w_grp": (pool_w_grp[0], m_pool_w_grp[0], v_pool_w_grp[0]),
    }
    big_in = _Big((1, d, 2 * f_hidden), 2, 1)
    big_out = _Big((1, f_hidden, d), 1, 2)
    units = {}
    for l in range(n_layers):
        for k in range(2):
            units[f"in{l}{k}"] = (big_in, "w_ffn_in", 2 * l + k)
            units[f"out{l}{k}"] = (big_out, "w_ffn_out", 2 * l + k)
    units["abin"] = (_Big((1, d, ab_w_in.shape[2] * N_CHIPS), 2, 1), "ab_w_in", 0)
    units["about"] = (_Big((1, ab_w_out.shape[1] * N_CHIPS, d), 1, 2), "ab_w_out", 0)
    units["pool"] = (_Big((pool_w_grp.shape[1], pool_w_grp.shape[2] * N_CHIPS, pool_w_grp.shape[3]), 1, 0), "pool_w_grp", 0)
    big = {u: g for u, (g, _, _) in units.items()}

    weight = {}
    complete = set()

    def cast(u):
        g, st, b0 = units[u]

        def launch(phases):
            (weight[u],), p_outs = _cast_into_full(stacks[st][0], b0, g, where, "cast_" + u, phases)
            return None, p_outs

        return launch

    def gather_relay(us, second, whole_first):
        def then(outs):
            for u, o in zip(us, outs):
                weight[u] = o

        return _phase_gather_relay([weight[u] for u in us], [big[u] for u in us], second, whole_first, then)

    def gather_sibling(*us):
        def then(outs):
            for u, o in zip(us, outs):
                weight[u] = o
                complete.add(u)

        return _phase_gather_sibling([weight[u] for u in us], [big[u] for u in us], then)

    def w_of(u):
        assert u in complete, u
        return weight[u]

    _run(cast("in00"), small_gather("inputs", [packed]))
    small_all = small["inputs"][0]
    by_chip = small_all[0::2]
    c_all = small_all[:, 0:N_CHIPS, :].reshape(N_DEV, d)
    norm_full = by_chip[:, 8 : 8 + 3 * n_layers, :].transpose(1, 0, 2).reshape(3 * n_layers, d)
    pool_scale_full = by_chip[:, 16:17, :].transpose(1, 0, 2).reshape(1, d)
    conv_full = by_chip[:, 24:27, : db // N_CHIPS].transpose(1, 0, 2).reshape(3, db)
    pieces = [("in00", "out00"), ("abin", "about"), ("in01", "out01"), ("in10", "out10", "pool"), ("in11", "out11")]
    in_flight = {}

    def start_gather(p):
        in_flight[p, 0] = _split_start(gather_relay(pieces[p], False, p == 0), f"gather_{p}_start")

    def relay_gather(p, after=()):
        flight = in_flight.pop((p, 0))
        _split_wait(flight, list(after) + list(started().ins), f"gather_{p}_arrived")
        in_flight[p, 1] = _split_start(gather_relay(pieces[p], True, p == 0), f"gather_{p}_relay")

    def started():
        return _after(*[flight.token for flight in in_flight.values()])

    def finish_gather(p, after, meanwhile=None):
        arrived = [(p, 1)] + [(p + 1, 0)] * (p + 1 < len(pieces))
        tokens = list(started().ins)
        _split_waits([in_flight.pop(key) for key in arrived], list(after) + tokens, f"gather_{p}_wait")
        leaving = [(p + 1, 1)] * (p + 1 < len(pieces)) + [(p + 3, 0)] * (p + 3 < len(pieces))
        crossing, *flights = _split_starts(
            [gather_sibling(*pieces[p])] + [gather_relay(pieces[q], step == 1, q == 0) for q, step in leaving],
            f"gather_{p}_forward",
        )
        in_flight.update(zip(leaving, flights))
        behind = [crossing.token] + list(started().ins)
        if meanwhile is not None:
            behind = behind + meanwhile(_after(crossing.token))
        _split_wait(crossing, behind, f"gather_{p}_forwarded")

    _run(cast("out00"))
    start_gather(0)
    mod_cols = _run(lambda phases: _mod_fwd(c_all, w_mod, b_cols, phases), started())[0]

    def mod_rows(outs):
        small["mod"] = outs

    _run(cast("about"), started())
    early = [u for piece in pieces[2:4] for u in piece]
    for u in early:
        _run(cast(u), started())
    _run(
        cast("abin"), _phase_small_exchange(mod_cols.transpose(1, 0, 2), mod_rows),
        started(), _after(*[weight[u] for u in early]),
    )
    relay_gather(0)
    start_gather(1)
    start_gather(2)
    for u in pieces[4]:
        _run(cast(u), started())
    mod_mine = small["mod"][0][0::2]
    mod = mod_mine.transpose(1, 0, 2).reshape(n_layers, 3, 3, d)
    vecs = {
        (l, sub): jnp.pad(norm_full[3 * l + sub][None], ((0, 7), (0, 0))) + jnp.pad(mod[l, sub], ((1, 4), (0, 0)))
        for l in range(n_layers)
        for sub in range(3)
    }
    b_rows = jnp.broadcast_to(ab_b_s[0].T[:, :, None], (chunk, heads, da // heads)).reshape(chunk, da)

    saved = {}

    def ffn_forward(xs, l, sub, k, *phases):
        saved[l, sub, "x"] = xs
        xs, gg, uu, yb = _run(
            lambda ph: _ffn_fwd(xs, vecs[l, sub], w_of(f"in{l}{k}"), w_of(f"out{l}{k}"), f"ffn_fwd_{l}{k}", ph), *phases
        )
        saved[l, sub, "act"] = (gg, uu, yb)
        return xs

    finish_gather(0, [vecs[0, 0]] + [weight[u] for u in pieces[4]])
    xs = ffn_forward(x0, 0, 0, 0, started())
    saved[0, 1, "x"] = xs
    finish_gather(1, [xs])
    (proj,) = _run(lambda ph: _proj_mod_fwd(xs, vecs[0, 1], w_of("abin"), ph), started())
    (cat,) = _run(lambda ph: _ab_mix_fwd(proj, ab_norm_v, ab_w_s[0], b_rows, conv_full, ph))
    xs, yb = _run(lambda ph: _proj_res_fwd(cat, w_of("about"), xs, vecs[0, 1], ph))
    saved[0, 1, "act"] = (proj, cat, yb)
    finish_gather(2, [xs])
    xs = ffn_forward(xs, 0, 2, 1, started())
    finish_gather(3, [xs])
    xs = ffn_forward(xs, 1, 0, 0, started())
    saved[1, 1, "x"] = xs
    pooled = []

    def pool_forward(behind):
        pooled.extend(_run(lambda ph: _pool_fwd(xs, vecs[1, 1], w_of("pool"), pool_scale_full, ph), behind))
        return [pooled[0]]

    finish_gather(4, [xs], pool_forward)
    xs, pp, oo = pooled
    saved[1, 1, "act"] = (pp, oo)
    xs = ffn_forward(xs, 1, 2, 1)
    dxs, aux = _run(lambda ph: _loss_head(xs, final_g.reshape(1, d), target, ph))

    grad = {}
    recv = {}
    csum = {}
    parts = {}
    reduced = {}
    done = set()
    dvecs, small_g = {}, {}

    def pair_exchange(*us):
        def then(outs):
            for u, o in zip(us, outs):
                recv[u] = o

        return _phase_pair_exchange([grad[u] for u in us], [big[u] for u in us], then)

    def grad_half(u, a, bs, mine, name, *phases):
        (res,) = _run(lambda ph: _grad_half(a, bs, big[u], where, mine, recv[u] if mine else None, name, ph), *phases)
        return res

    def pair_sum(u, *phases):
        def launch(ph):
            (csum[u],), p_outs = _pair_sum(grad[u], recv[u], big[u], where, "pair_sum_" + u, ph)
            return None, p_outs

        _run(launch, *phases)

    def chip_exchange(*us):
        def then(outs):
            for u, o in zip(us, outs):
                parts[u] = o

        return _phase_chip_exchange([csum[u] for u in us], [big[u] for u in us], then)

    def chip_sum(*us, carried=()):
        for n_u, u in enumerate(us):
            g, st, b0 = units[u]

            def launch(ph):
                (reduced[st],), p_outs = _chip_sum(
                    csum[u], parts[u], g, where, reduced.get(st), stacks[st][0].shape, b0, "chip_sum_" + u, ph
                )
                return None, p_outs

            _run(launch, *(carried if n_u == 0 else ()))

    def pair_broadcast(*us):
        sts = [units[u][1] for u in us]
        assert len(set(sts)) == len(sts)

        def then(outs):
            for u, st, o in zip(us, sts, outs):
                reduced[st] = o
                done.add(u)

        return _phase_pair_broadcast([reduced[st] for st in sts], [big[u] for u in us], [units[u][2] for u in us], then)

    def ffn_backward(dxs, l, sub, k, carried_bwd, carried_send, carried_mine):
        gg, uu, yb = saved[l, sub, "act"]
        w_in, w_out = w_of(f"in{l}{k}"), w_of(f"out{l}{k}")
        uo, ui, tag = f"out{l}{k}", f"in{l}{k}", f"{l}{k}"
        dxs, dg, du, a, h, dy, dvecs[l, sub] = _run(
            lambda ph: _ffn_bwd(dxs, saved[l, sub, "x"], vecs[l, sub], gg, uu, yb, w_in, w_out, "ffn_bwd_" + tag, ph), *carried_bwd()
        )
        grad[uo] = grad_half(uo, a, [dy], False, "dw_out_send_" + tag, *carried_send())
        grad[ui] = grad_half(ui, h, [dg, du], False, "dw_in_send_" + tag, pair_exchange(uo))
        csum[uo] = grad_half(uo, a, [dy], True, "dw_out_" + tag, pair_exchange(ui))
        csum[ui] = grad_half(ui, h, [dg, du], True, "dw_in_" + tag, *carried_mine())
        return dxs

    none = lambda: ()
    dxs = ffn_backward(dxs, 1, 2, 1, none, none, none)
    pp, oo = saved[1, 1, "act"]
    dxs, grad["pool"], small_g["pool_scale"], dvecs[1, 1] = _run(
        lambda ph: _pool_bwd(dxs, saved[1, 1, "x"], vecs[1, 1], pp, oo, w_of("pool"), pool_scale_full, ph)
    )

    def after_11():
        return (chip_exchange("in11", "out11"), pair_exchange("pool"))

    def bcast_11():
        chip_sum("in11", "out11")
        pair_sum("pool")
        return (pair_broadcast("in11", "out11"), chip_exchange("pool"))

    dxs = ffn_backward(dxs, 1, 0, 0, after_11, bcast_11, none)

    def after_10():
        return (chip_exchange("in10", "out10"),)

    def bcast_10():
        chip_sum("in10", "out10", "pool")
        return (pair_broadcast("in10", "out10", "pool"),)

    dxs = ffn_backward(dxs, 0, 2, 1, after_10, bcast_10, none)

    proj, cat, yb = saved[0, 1, "act"]
    out01 = _split_start(chip_exchange("out01"), "reduce_out01_start")
    dy, dcat, dgate = _run(lambda ph: _proj_res_bwd(dxs, yb, vecs[0, 1], w_of("about"), ph), _after(out01.token))
    grad["about"] = grad_half("about", cat, [dy], False, "dw_ab_out_send")
    dproj, small_g["ab_norm_v"], small_g["ab_w_s"], dzs, small_g["ab_conv_w"] = _run(
        lambda ph: _ab_mix_bwd(proj, dcat, ab_norm_v, ab_w_s[0], b_rows, conv_full, ph), pair_exchange("about")
    )
    small_g["ab_b_s"] = dzs.reshape(chunk, heads, da // heads).sum(axis=2).T
    dxs, h, dvecs[0, 1] = _run(
        lambda ph: _proj_mod_bwd(dproj[None], w_of("abin"), saved[0, 1, "x"], vecs[0, 1], dxs, dgate, "ab_in_bwd", ph)
    )
    grad["abin"] = grad_half("abin", h, [dproj], False, "dw_ab_in_send")
    (csum["out01"],) = _split_wait(out01, [grad["abin"]], "reduce_out01_wait")
    chip_sum("out01", carried=(pair_exchange("abin"),))
    csum["about"] = grad_half("about", cat, [dy], True, "dw_ab_out", pair_broadcast("out01"))
    csum["abin"] = grad_half("abin", h, [dproj], True, "dw_ab_in")

    layout = {}
    tail = {}

    def after_01():
        tail["01"] = _split_start(chip_exchange("in01", "abin", "about"), "reduce_01_start")
        return (_after(tail["01"].token),)

    def pack_small_grads():
        dvec_all = jnp.stack([dvecs[l, sub] for l in range(n_layers) for sub in range(3)])
        dgain = dvec_all[:, 0, :]
        dmod = dvec_all[:, 1:4, :].reshape(3 * 3 * n_layers, d)
        rows = {
            "norm_g": (dgain, None, dq), "final_g": (aux[0:1], 0, d), "pool_scale": (small_g["pool_scale"], None, dq),
            "b_mod": (dmod, 0, d), "ab_norm_v": (small_g["ab_norm_v"], 0, da),
            "ab_conv_w": (small_g["ab_conv_w"], None, db // N_CHIPS), "ab_b_s": (small_g["ab_b_s"], 0, chunk),
            "loss": (aux[1:2], 0, d),
        }
        row0 = 0
        for nm, (pc, col0, cols) in rows.items():
            layout[nm] = (row0, pc.shape[0], col0, cols)
            row0 += pc.shape[0]
        packed_rows = -(-row0 // 8) * 8
        return sum(
            jnp.pad(pc, ((layout[nm][0], packed_rows - layout[nm][0] - pc.shape[0]), (0, d - pc.shape[1])))
            for nm, (pc, _, _) in rows.items()
        )

    def bcast_01():
        csum["in01"], csum["abin"], csum["about"] = _split_wait(tail["01"], [dvecs[0, 0]], "reduce_01_wait")
        chip_sum("in01", "abin", "about")
        grads_small = [pack_small_grads(), small_g["ab_w_s"].reshape(heads * chunk, chunk)]
        tail["small"] = _split_start(small_gather("grads", grads_small), "gather_small_grads_start")
        return (pair_broadcast("in01", "abin", "about"), _after(tail["small"].token))

    def reduce_out00():
        tail["out00"] = _split_start(chip_exchange("out00"), "reduce_out00_start")
        return (_after(tail["out00"].token),)

    dxs = ffn_backward(dxs, 0, 0, 0, after_01, bcast_01, reduce_out00)
    grad_x = dxs.reshape(x.shape)

    last = _split_start(chip_exchange("in00"), "reduce_last_start")
    (csum["out00"],) = _split_wait(tail["out00"], [last.token], "reduce_out00_wait")
    chip_sum("out00")
    _flush("broadcast_out00", pair_broadcast("out00"))
    _split_wait(tail["small"], [reduced["w_ffn_out"]], "gather_small_grads_wait")
    g_all, gws_all = small["grads"]

    out = {}

    def adam_stack(st, after=()):
        w3, m3, v3 = stacks[st]
        assert all(u in done for u, (_, ust, _) in units.items() if ust == st), st
        shape = {"w_ffn_in": w_ffn_in.shape, "w_ffn_out": w_ffn_out.shape, "pool_w_grp": pool_w_grp.shape}.get(st, w3.shape)
        out[st] = tuple(a.reshape(shape) for a in _adam_stack(w3, reduced[st], m3, v3, "adam_" + st, after))

    for st in ("w_ffn_out", "ab_w_in", "ab_w_out", "pool_w_grp"):
        adam_stack(st, (last.token,))

    shapes2d = {
        "norm_g": (3 * n_layers, dq), "b_mod": (9 * n_layers, d), "final_g": (1, d), "ab_norm_v": (1, da),
        "pool_scale": (1, dq), "ab_conv_w": (3, db // N_CHIPS), "ab_b_s": (heads, chunk), "ab_w_s": (heads * chunk, chunk),
    }
    small_w = {"norm_g": (norm_g, m_norm_g, v_norm_g), "b_mod": (b_mod, m_b_mod, v_b_mod), "final_g": (final_g, m_final_g, v_final_g),
               "ab_norm_v": (ab_norm_v, m_ab_norm_v, v_ab_norm_v), "pool_scale": (pool_scale, m_pool_scale, v_pool_scale),
               "ab_conv_w": (ab_conv_w, m_ab_conv_w, v_ab_conv_w), "ab_b_s": (ab_b_s, m_ab_b_s, v_ab_b_s), "ab_w_s": (ab_w_s, m_ab_w_s, v_ab_w_s)}
    smalls = {nm: tuple(a.reshape(shapes2d[nm]) for a in wmv) for nm, wmv in small_w.items()}
    small_out, loss = _small_adam(g_all, gws_all, layout, smalls, chip)
    loss = loss.reshape(())
    for nm, res in small_out.items():
        out[nm] = tuple(a.reshape(small_w[nm][0].shape) for a in res)

    mod_row0 = layout["b_mod"][0]
    dmod_all = g_all[:, mod_row0 : mod_row0 + 9 * n_layers, :].reshape(N_DEV, n_layers, 9 * d)
    dmod_cols = lax.dynamic_slice(dmod_all, (0, 0, chip * ncol), (N_DEV, n_layers, ncol)).transpose(1, 0, 2)
    out["w_mod"] = tuple(_mod_bwd_adam(c_all.T, dmod_cols, w_mod, m_w_mod, v_w_mod, (last.token,)))

    (csum["in00"],) = _split_wait(
        last, [out[st][1] for st in ("w_mod", "w_ffn_out", "ab_w_in", "ab_w_out", "pool_w_grp")], "reduce_last_wait"
    )
    chip_sum("in00")
    _flush("broadcast_last", pair_broadcast("in00"))
    adam_stack("w_ffn_in")

    order = ["norm_g", "w_mod", "b_mod", "w_ffn_in", "w_ffn_out", "ab_w_in", "ab_norm_v", "ab_w_s", "ab_b_s", "ab_conv_w", "ab_w_out", "pool_w_grp", "pool_scale", "final_g"]
    return (loss, grad_x, *[out[nm][0] for nm in order], *[out[nm][1] for nm in order], *[out[nm][2] for nm in order], *[out[nm][3] for nm in order])
```

```python
import functools
import math

import jax
import jax.numpy as jnp
from jax import lax
from jax.experimental import pallas as pl
from jax.experimental.pallas import tpu as pltpu

F32 = jnp.float32
BF16 = jnp.bfloat16
MESH = pl.DeviceIdType.MESH

EPS = 1e-6
ADAM_LR = 0.001
ADAM_B1 = 0.9
ADAM_B2 = 0.999
ADAM_EPS = 1e-08
ADAM_WD = 0.01
ADAM_STEP = 10
POOL_WINDOWS = (2, 4, 8, 16)
POOL_HALO = 16
CONV_HALO = 8
N_CHIPS = 4
N_DEV = 8
VMEM_LIMIT_BYTES = 48 * 1024 * 1024
EW_BLOCK_ELEMS = 1024 * 1024
ADAM_BLOCK_ELEMS = 512 * 1024


def _pick(n, prefs):
    for p in prefs:
        if p <= n and n % p == 0:
            return p
    return n


def _row_tile(rows, cols, block_elems=EW_BLOCK_ELEMS):
    best = None
    for d in range(16, rows + 1, 16):
        if rows % d == 0 and d * cols <= block_elems:
            best = d
    return best or rows


def _dot(a, b):
    return jnp.dot(a, b, preferred_element_type=F32)


def _dot_nt(a, b):
    return lax.dot_general(a, b, (((1,), (1,)), ((), ())), preferred_element_type=F32)


def _dot_tn(a, b):
    return lax.dot_general(a, b, (((0,), (0,)), ((), ())), preferred_element_type=F32)


def _sigmoid(x):
    return 0.5 * jnp.tanh(0.5 * x) + 0.5


_GELU_C = math.sqrt(2.0 / math.pi)


def _gelu(x):
    x2 = x * x
    t = jnp.tanh(_GELU_C * (x + 0.044715 * x2 * x))
    val = 0.5 * x * (1.0 + t)
    grad = 0.5 * (1.0 + t) + 0.5 * x * (1.0 - t * t) * (_GELU_C * (1.0 + 3.0 * 0.044715 * x2))
    return val, grad


def _rstd(x):
    return lax.rsqrt(jnp.mean(x * x, axis=-1, keepdims=True) + EPS)


def _modulate(x, vec_ref):
    return (x * _rstd(x)) * vec_ref[0:1, :] * (1.0 + vec_ref[2:3, :]) + vec_ref[1:2, :]


def _modulate_bwd(x, dh, vec_ref, dvec_ref):
    gn, sh, sc = vec_ref[0:1, :], vec_ref[1:2, :], vec_ref[2:3, :]
    rstd = _rstd(x)
    r = x * rstd
    dvec_ref[0:1, :] += jnp.sum(dh * r * (1.0 + sc), axis=0, keepdims=True)
    dvec_ref[1:2, :] += jnp.sum(dh, axis=0, keepdims=True)
    dvec_ref[2:3, :] += jnp.sum(dh * r * gn, axis=0, keepdims=True)
    gm = gn * (1.0 + sc)
    dr = dh * gm
    dx = rstd * (dr - r * jnp.mean(dr * r, axis=-1, keepdims=True))
    return dx, r * gm + sh


def _adam(w, g, m, v):
    m = ADAM_B1 * m + (1.0 - ADAM_B1) * g
    v = ADAM_B2 * v + (1.0 - ADAM_B2) * (g * g)
    m_hat = m / (1.0 - ADAM_B1**ADAM_STEP)
    v_hat = v / (1.0 - ADAM_B2**ADAM_STEP)
    delta = -ADAM_LR * (m_hat / (jnp.sqrt(v_hat) + ADAM_EPS) + ADAM_WD * w)
    return delta, m, v


_ANY = pl.BlockSpec(memory_space=pl.ANY)


class _Phase:
    def __init__(self, ins, out_shapes, aliases, n_sems, start, finish, then):
        self.ins, self.out_shapes, self.aliases, self.n_sems = list(ins), list(out_shapes), dict(aliases), n_sems
        self.start, self.finish, self.then = start, finish, then


def _call(body, name, grid, in_specs, out_specs, out_shape, ins, scratch=(), prefetch=(), phases=(), in_place=None):
    n_pre, n_in, n_out, n_sc = len(prefetch), len(in_specs), len(out_specs), len(scratch)
    ph_in = [len(p.ins) for p in phases]
    ph_out = [len(p.out_shapes) for p in phases]

    def kernel_body(*refs):
        pos = [0]

        def take(k):
            pos[0] += k
            return refs[pos[0] - k : pos[0]]

        pre, ins_ = take(n_pre), take(n_in)
        p_ins = [take(k) for k in ph_in]
        outs_ = take(n_out)
        p_outs = [take(k) for k in ph_out]
        sc = take(n_sc)
        sems = [take(2) for _ in phases]
        if phases:
            ids = [pl.program_id(a) for a in range(len(grid))]
            first = functools.reduce(jnp.logical_and, [i == 0 for i in ids])
            last = functools.reduce(jnp.logical_and, [i == g - 1 for i, g in zip(ids, grid)])

            @pl.when(first)
            def _():
                for p, pi, po, (send, recv) in zip(phases, p_ins, p_outs, sems):
                    p.start(pi, po, send, recv)

        if body is not None:
            body(*pre, *ins_, *outs_, *sc)
        if phases:

            @pl.when(last)
            def _():
                for p, pi, po, (send, recv) in zip(phases, p_ins, p_outs, sems):
                    p.finish(pi, po, send, recv)

    aliases = {n_pre + i: o for i, o in (in_place or {}).items()}
    i0, o0 = n_pre + n_in, n_out
    for p in phases:
        for i, o in p.aliases.items():
            aliases[i0 + i] = o0 + o
        i0 += len(p.ins)
        o0 += len(p.out_shapes)
    all_in = list(in_specs) + [_ANY] * sum(ph_in)
    all_out = list(out_specs) + [_ANY] * sum(ph_out)
    all_scratch = list(scratch)
    for p in phases:
        all_scratch += [pltpu.SemaphoreType.DMA((p.n_sems,)), pltpu.SemaphoreType.DMA((p.n_sems,))]
    shapes = [pltpu.HBM(s.shape, s.dtype) for s in out_shape] + [s for p in phases for s in p.out_shapes]
    operands = list(prefetch) + list(ins) + [a for p in phases for a in p.ins]
    sem = ("arbitrary",) * len(grid)
    params = pltpu.CompilerParams(dimension_semantics=sem, vmem_limit_bytes=VMEM_LIMIT_BYTES)
    if n_pre:
        res = pl.pallas_call(
            kernel_body, name=name, out_shape=shapes, input_output_aliases=aliases, compiler_params=params,
            grid_spec=pltpu.PrefetchScalarGridSpec(
                num_scalar_prefetch=n_pre, grid=grid, in_specs=all_in, out_specs=all_out, scratch_shapes=all_scratch
            ),
        )(*operands)
    else:
        res = pl.pallas_call(
            kernel_body, name=name, grid=grid, in_specs=all_in, out_specs=all_out, out_shape=shapes,
            scratch_shapes=all_scratch, input_output_aliases=aliases, compiler_params=params,
        )(*operands)
    res = list(res)
    outs, rest = res[:n_out], res[n_out:]
    p_res = []
    for k in ph_out:
        p_res.append(rest[:k])
        rest = rest[k:]
    return outs, p_res


def _place():
    return lax.axis_index("x"), lax.axis_index("y"), lax.axis_index("c")


def _other_chips():
    x, y, _ = _place()
    return [(1 - x, y), (x, 1 - y), (1 - x, 1 - y)]


def _flip(k):
    x, y, c = _place()
    return (1 - x if k & 4 else x, 1 - y if k & 2 else y, 1 - c if k & 1 else c)


def _remote(src, dst, send, recv, k, to):
    return pltpu.make_async_remote_copy(
        src_ref=src, dst_ref=dst, send_sem=send.at[k], recv_sem=recv.at[k], device_id=to, device_id_type=MESH
    )


def _phase_small_gather(arrs, then):
    n = len(arrs)

    def copies(ins, outs, send, recv):
        x, y, c = _place()
        me = 4 * x + 2 * y + c
        local = [pltpu.make_async_copy(ins[a], outs[a].at[me], send.at[a * N_DEV]) for a in range(n)]
        remote = [_remote(ins[a], outs[a].at[me], send, recv, a * N_DEV + k, _flip(k)) for a in range(n) for k in range(1, N_DEV)]
        return local, remote

    def start(ins, outs, send, recv):
        local, remote = copies(ins, outs, send, recv)
        for cp in local + remote:
            cp.start()

    def finish(ins, outs, send, recv):
        local, remote = copies(ins, outs, send, recv)
        for cp in remote + local:
            cp.wait()

    shapes = [jax.ShapeDtypeStruct((N_DEV,) + a.shape, a.dtype) for a in arrs]
    return _Phase(arrs, shapes, {}, n * N_DEV, start, finish, then)


def _phase_small_exchange(arr, then):
    def copies(ins, outs, send, recv):
        x, y, c = _place()
        me = 4 * x + 2 * y + c
        local = pltpu.make_async_copy(ins[0].at[me], outs[0].at[me], send.at[0])
        remote = []
        for k in range(1, N_DEV):
            px, py, pc = _flip(k)
            remote.append(_remote(ins[0].at[4 * px + 2 * py + pc], outs[0].at[me], send, recv, k, (px, py, pc)))
        return [local] + remote

    def start(ins, outs, send, recv):
        for cp in copies(ins, outs, send, recv):
            cp.start()

    def finish(ins, outs, send, recv):
        for cp in copies(ins, outs, send, recv):
            cp.wait()

    return _Phase([arr], [jax.ShapeDtypeStruct(arr.shape, arr.dtype)], {}, N_DEV, start, finish, then)


def _after(*arrs):
    nothing = lambda *args: None
    return _Phase(arrs, [], {}, 1, nothing, nothing, nothing)


def _flush(name, *phases):
    _, p_outs = _call(None, name, (1,), [], [], [], [], phases=list(phases))
    for p, po in zip(phases, p_outs):
        p.then(po)


class _Big:
    KINDS = {"full": (True, True), "half": (True, False), "shard": (False, True), "block": (False, False)}

    def __init__(self, f3, s3, h3):
        assert s3 != h3
        self.f3, self.s3, self.h3 = tuple(f3), s3, h3
        self.bd = tuple(f3[a] // (N_CHIPS if a == s3 else 1) // (2 if a == h3 else 1) for a in range(3))
        self.tile = (1, _row_tile(self.bd[1], self.bd[2]), self.bd[2])
        self.grid = tuple(self.bd[a] // self.tile[a] for a in range(3))

    def dims(self, kind):
        chips, halves = self.KINDS[kind]
        return tuple(
            self.bd[a] * (N_CHIPS if chips and a == self.s3 else 1) * (2 if halves and a == self.h3 else 1) for a in range(3)
        )

    def view(self, ref, chip=None, half=None, batch0=0, both_halves=True, part=None):
        start = [batch0, 0, 0]
        size = list(ref.shape)
        size[0] = self.bd[0] * (2 if self.h3 == 0 and both_halves else 1)
        if chip is not None:
            start[self.s3] += chip * self.bd[self.s3]
            size[self.s3] = self.bd[self.s3]
        if half is not None:
            start[self.h3] += half * self.bd[self.h3]
            size[self.h3] = self.bd[self.h3]
        if part is not None:
            size[1] //= 2
            start[1] += part * size[1]
        return ref.at[tuple(pl.ds(st, sz) for st, sz in zip(start, size))]

    def spec(self, chip_from=None, half_from=None, lead=(), batch0=0):
        extra = "grid" in (chip_from, half_from)

        def index(*args):
            pref, idx = args[-1], list(args[int(extra) : -1])
            idx[0] += batch0
            if chip_from:
                idx[self.s3] += (pref[0] if chip_from == "pref" else args[0]) * self.grid[self.s3]
            if half_from:
                idx[self.h3] += (pref[1] if half_from == "pref" else args[0]) * self.grid[self.h3]
            return (0,) * len(lead) + tuple(idx)

        return pl.BlockSpec(tuple(lead) + self.tile, index)


def _same(arrs):
    return [jax.ShapeDtypeStruct(a.shape, a.dtype) for a in arrs]


def _phase_gather_relay(arrs, bigs, second, whole_first, then):
    n = len(arrs)
    per = 4 if second and not whole_first else 2

    def copies(outs, send, recv, arriving):
        x, y, c = _place()
        me, xn, yn, dg = (x, y), (1 - x, y), (x, 1 - y), (1 - x, 1 - y)
        if not second:
            part = (None, None) if whole_first else (0, 1)
            plan = [((xn if arriving else me), part[0], xn), ((yn if arriving else me), part[1], yn)]
        elif whole_first:
            plan = [(dg, 0, yn), (dg, 1, xn)] if arriving else [(xn, 0, yn), (yn, 1, xn)]
        elif arriving:
            plan = [(yn, 0, yn), (dg, 0, yn), (xn, 1, xn), (dg, 1, xn)]
        else:
            plan = [(me, 0, yn), (xn, 0, yn), (me, 1, xn), (yn, 1, xn)]
        res = []
        for a in range(n):
            for k, (chip, part, to) in enumerate(plan):
                blk = bigs[a].view(outs[a], 2 * chip[0] + chip[1], c, part=part)
                res.append(_remote(blk, blk, send, recv, per * a + k, (*to, c)))
        return res

    def start(ins, outs, send, recv):
        for cp in copies(outs, send, recv, False):
            cp.start()

    def finish(ins, outs, send, recv):
        for cp in copies(outs, send, recv, True):
            cp.wait_recv()
        for cp in copies(outs, send, recv, False):
            cp.wait_send()

    return _Phase(arrs, _same(arrs), {a: a for a in range(n)}, per * n, start, finish, then)


def _phase_gather_sibling(arrs, bigs, then):
    n = len(arrs)

    def copies(outs, send, recv, arriving):
        x, y, c = _place()
        return [
            _remote(blk, blk, send, recv, 3 * a + j, (x, y, 1 - c))
            for j, chip in enumerate(_other_chips())
            for a in range(n)
            for blk in [bigs[a].view(outs[a], 2 * chip[0] + chip[1], 1 - c if arriving else c)]
        ]

    def start(ins, outs, send, recv):
        for cp in copies(outs, send, recv, False):
            cp.start()

    def finish(ins, outs, send, recv):
        for cp in copies(outs, send, recv, True):
            cp.wait_recv()
        for cp in copies(outs, send, recv, False):
            cp.wait_send()

    return _Phase(arrs, _same(arrs), {a: a for a in range(n)}, 3 * n, start, finish, then)


def _phase_pair_exchange(grads, bigs, then):
    n = len(grads)

    def copies(ins, outs, send, recv):
        x, y, c = _place()
        srcs = [ins[a] if ins[a].shape == outs[a].shape else bigs[a].view(ins[a], None, 1 - c) for a in range(n)]
        return [_remote(srcs[a], outs[a], send, recv, a, (x, y, 1 - c)) for a in range(n)]

    def start(ins, outs, send, recv):
        for cp in copies(ins, outs, send, recv):
            cp.start()

    def finish(ins, outs, send, recv):
        for cp in copies(ins, outs, send, recv):
            cp.wait()

    shapes = [jax.ShapeDtypeStruct(b.dims("half"), BF16) for b in bigs]
    return _Phase(grads, shapes, {}, n, start, finish, then)


def _phase_chip_exchange(sums, bigs, then):
    n = len(sums)

    def copies(ins, outs, send, recv):
        _, _, c = _place()
        return [
            _remote(bigs[a].view(ins[a], 2 * chip[0] + chip[1], both_halves=False), outs[a].at[j], send, recv, 3 * a + j, (*chip, c))
            for j, chip in enumerate(_other_chips())
            for a in range(n)
        ]

    def start(ins, outs, send, recv):
        for cp in copies(ins, outs, send, recv):
            cp.start()

    def finish(ins, outs, send, recv):
        for cp in copies(ins, outs, send, recv):
            cp.wait()

    shapes = [jax.ShapeDtypeStruct((N_CHIPS - 1,) + b.dims("block"), BF16) for b in bigs]
    return _Phase(sums, shapes, {}, 3 * n, start, finish, then)


_HBM = pl.BlockSpec(memory_space=pltpu.HBM)
_SEM = pl.BlockSpec(memory_space=pltpu.SEMAPHORE)
_DATAFLOW = pltpu.SideEffectType.DATAFLOW_SIDE_EFFECTING


class _InFlight:
    def __init__(self, phase, send, recv, arrays, token):
        self.phase, self.send, self.recv, self.arrays, self.token = phase, send, recv, arrays, token


def _phase_results(phase, refs):
    n_in = len(phase.ins)
    updated = {o: i for i, o in phase.aliases.items()}
    fresh = [o for o in range(len(phase.out_shapes)) if o not in updated]
    return [refs[updated[o]] if o in updated else refs[n_in + fresh.index(o)] for o in range(len(phase.out_shapes))]


def _split_start(phase, name):
    n_in = len(phase.ins)
    fresh = [s for o, s in enumerate(phase.out_shapes) if o not in phase.aliases.values()]
    arrays = list(phase.ins) + [lax.empty(s.shape, s.dtype) for s in fresh]
    n = len(arrays)

    def body(*refs):
        phase.start(refs[:n_in], _phase_results(phase, refs[:n]), refs[n], refs[n + 1])
        refs[-1][...] = jnp.zeros_like(refs[-1])

    operands = [pltpu.with_memory_space_constraint(a, pltpu.HBM) for a in arrays]
    res = pl.pallas_call(
        body, name=name,
        out_shape=[pltpu.SemaphoreType.DMA((phase.n_sems,)), pltpu.SemaphoreType.DMA((phase.n_sems,))]
        + [pltpu.HBM(a.shape, a.dtype) for a in arrays] + [jax.ShapeDtypeStruct((8, 128), F32)],
        in_specs=[_HBM] * n, out_specs=[_SEM, _SEM] + [_HBM] * n + [pl.BlockSpec(memory_space=pltpu.VMEM)],
        input_output_aliases={i: 2 + i for i in range(n)},
        compiler_params=pltpu.CompilerParams(has_side_effects=_DATAFLOW),
    )(*operands)
    return _InFlight(phase, res[0], res[1], list(res[2 : 2 + n]), res[-1])


def _split_wait(flight, after, name):
    phase, n = flight.phase, len(flight.arrays)
    n_in = len(phase.ins)

    def body(*refs):
        phase.finish(refs[:n_in], _phase_results(phase, refs[:n]), refs[n], refs[n + 1])

    res = pl.pallas_call(
        body, name=name, out_shape=[pltpu.HBM(a.shape, a.dtype) for a in flight.arrays],
        in_specs=[_HBM] * n + [_SEM, _SEM] + [_ANY] * len(after), out_specs=[_HBM] * n,
        input_output_aliases={i: i for i in range(n)},
        compiler_params=pltpu.CompilerParams(has_side_effects=_DATAFLOW),
    )(*flight.arrays, flight.send, flight.recv, *after)
    res = list(res)
    phase.then(_phase_results(phase, res))
    return res[:n_in]


def _phase_pair_broadcast(stacks, bigs, batch0s, then):
    n = len(stacks)

    def start(ins, outs, send, recv):
        x, y, c = _place()
        for a in range(n):
            blk = bigs[a].view(outs[a], None, c, batch0s[a])
            _remote(blk, blk, send, recv, a, (x, y, 1 - c)).start()

    def finish(ins, outs, send, recv):
        x, y, c = _place()
        for a in range(n):
            mine = bigs[a].view(outs[a], None, c, batch0s[a])
            theirs = bigs[a].view(outs[a], None, 1 - c, batch0s[a])
            _remote(mine, mine, send, recv, a, (x, y, 1 - c)).wait_send()
            _remote(theirs, theirs, send, recv, a, (x, y, 1 - c)).wait_recv()

    return _Phase(stacks, _same(stacks), {a: a for a in range(n)}, n, start, finish, then)


def _tile_call(body, name, big, where, extra, ins, in_specs, out_specs, out_shape, phases=()):
    grid = ((extra,) if extra else ()) + big.grid
    return _call(body, name, grid, in_specs, out_specs, out_shape, ins, prefetch=(where,), phases=phases)


def _cast_into_full(w_stack, batch0, big, where, name, phases=()):
    def body(_, w_ref, o_ref):
        o_ref[...] = w_ref[...].astype(BF16)

    return _tile_call(
        body, name, big, where, 2, [w_stack], [big.spec(None, "grid", batch0=batch0)], [big.spec("pref", "grid")],
        [jax.ShapeDtypeStruct(big.dims("full"), BF16)], phases,
    )


def _pair_sum(g_full, recv_half, big, where, name, phases=()):
    def body(_, g_ref, r_ref, o_ref):
        o_ref[...] = (g_ref[...].astype(F32) + r_ref[...].astype(F32)).astype(BF16)

    half = big.spec("grid", None)
    return _tile_call(
        body, name, big, where, N_CHIPS, [g_full, recv_half], [big.spec("grid", "pref"), half], [half],
        [jax.ShapeDtypeStruct(big.dims("half"), BF16)], phases,
    )


def _chip_sum(chip_sum, parts, big, where, stack, stack_shape, batch0, name, phases=()):
    def body(_, own_ref, p_ref, *rest):
        acc = own_ref[...].astype(F32)
        for k in range(N_CHIPS - 1):
            acc = acc + p_ref[k].astype(F32)
        rest[-1][...] = acc

    ins = [chip_sum, parts] + ([stack] if stack is not None else [])
    in_specs = [big.spec("pref", None), big.spec(None, None, lead=(N_CHIPS - 1,))] + ([_ANY] if stack is not None else [])
    return _call(
        body, name, big.grid, in_specs, [big.spec(None, "pref", batch0=batch0)], [jax.ShapeDtypeStruct(stack_shape, F32)], ins,
        prefetch=(where,), phases=phases, in_place={2: 0} if stack is not None else None,
    )


def _adam_stack(w, g, m, v, name, after=()):
    b, r, c = w.shape
    tr = _row_tile(r, c, ADAM_BLOCK_ELEMS)

    def body(w_ref, g_ref, m_ref, v_ref, *rest):
        go_ref, d_ref, mo_ref, vo_ref = rest[-4:]
        gv = g_ref[...]
        d, mo, vo = _adam(w_ref[...], gv, m_ref[...], v_ref[...])
        go_ref[...] = gv
        d_ref[...] = d
        mo_ref[...] = mo
        vo_ref[...] = vo

    spec = pl.BlockSpec((1, tr, c), lambda bb, i: (bb, i, 0))
    outs, _ = _call(
        body, name, (b, r // tr), [spec] * 4 + [_ANY] * len(after), [spec] * 4, [jax.ShapeDtypeStruct(w.shape, F32)] * 4,
        [w, g, m, v, *after],
    )
    return outs


def _mod_fwd(c_all, w_mod, b_cols, phases=()):
    n_layers, d, n = w_mod.shape
    tn = _pick(n, (768, 512, 384, 256, 128))

    def body(c_ref, w_ref, b_ref, o_ref):
        cv = c_ref[...]
        ca = (cv * _sigmoid(cv)).astype(BF16)
        o_ref[0] = _dot(ca, w_ref[0].astype(BF16)) + b_ref[0]

    return _call(
        body, "mod_fwd", (n_layers, n // tn),
        [
            pl.BlockSpec((N_DEV, d), lambda l, j: (0, 0)),
            pl.BlockSpec((1, d, tn), lambda l, j: (l, 0, j)),
            pl.BlockSpec((1, 1, tn), lambda l, j: (l, 0, j)),
        ],
        [pl.BlockSpec((1, N_DEV, tn), lambda l, j: (l, 0, j))],
        [jax.ShapeDtypeStruct((n_layers, N_DEV, n), F32)], [c_all, w_mod, b_cols], phases=phases,
    )


def _mod_bwd_adam(c_all_t, dmod_cols, w, m, v, after=()):
    n_layers, d, n = w.shape
    tn = _pick(n, (384, 256, 128))

    def body(c_ref, dm_ref, w_ref, m_ref, v_ref, *rest):
        g_ref, d_ref, mo_ref, vo_ref = rest[-4:]
        cv = c_ref[...]
        ca = (cv * _sigmoid(cv)).astype(BF16)
        g = _dot(ca, dm_ref[0].astype(BF16))
        g_ref[0] = g
        dl, mo, vo = _adam(w_ref[0], g, m_ref[0], v_ref[0])
        d_ref[0] = dl
        mo_ref[0] = mo
        vo_ref[0] = vo

    wspec = pl.BlockSpec((1, d, tn), lambda l, j: (l, 0, j))
    outs, _ = _call(
        body, "mod_bwd_adam", (n_layers, n // tn),
        [pl.BlockSpec((d, N_DEV), lambda l, j: (0, 0)), pl.BlockSpec((1, N_DEV, tn), lambda l, j: (l, 0, j)), wspec, wspec, wspec]
        + [_ANY] * len(after),
        [wspec] * 4, [jax.ShapeDtypeStruct(w.shape, F32)] * 4, [c_all_t, dmod_cols, w, m, v, *after],
    )
    return outs


def _ffn_fwd(x, vec, w_in, w_out, name, phases=()):
    s, d = x.shape
    f = w_out.shape[1]
    tm = _pick(s, (1024, 512, 256, 128))
    tf = _pick(f, (256, 128))
    nf = f // tf

    def body(x_ref, vec_ref, wg_ref, wu_ref, wo_ref, xo_ref, g_ref, u_ref, y_ref, h_sc, acc_sc):
        j = pl.program_id(1)

        @pl.when(j == 0)
        def _():
            h_sc[...] = _modulate(x_ref[...], vec_ref).astype(BF16)
            acc_sc[...] = jnp.zeros_like(acc_sc)

        h = h_sc[...]
        g = _dot(h, wg_ref[0])
        u = _dot(h, wu_ref[0])
        g_ref[...] = g.astype(BF16)
        u_ref[...] = u.astype(BF16)
        a = (g * _sigmoid(g) * u).astype(BF16)
        acc_sc[...] += _dot(a, wo_ref[0])

        @pl.when(j == nf - 1)
        def _():
            yv = acc_sc[...]
            xo_ref[...] = x_ref[...] + 0.5 * vec_ref[3:4, :] * yv
            y_ref[...] = yv.astype(BF16)

    row = pl.BlockSpec((tm, d), lambda i, j: (i, 0))
    hid = pl.BlockSpec((tm, tf), lambda i, j: (i, j))
    return _call(
        body, name, (s // tm, nf),
        [
            row,
            pl.BlockSpec((8, d), lambda i, j: (0, 0)),
            pl.BlockSpec((1, d, tf), lambda i, j: (0, 0, j)),
            pl.BlockSpec((1, d, tf), lambda i, j: (0, 0, nf + j)),
            pl.BlockSpec((1, tf, d), lambda i, j: (0, j, 0)),
        ],
        [row, hid, hid, row],
        [
            jax.ShapeDtypeStruct((s, d), F32),
            jax.ShapeDtypeStruct((s, f), BF16),
            jax.ShapeDtypeStruct((s, f), BF16),
            jax.ShapeDtypeStruct((s, d), BF16),
        ],
        [x, vec, w_in, w_in, w_out],
        scratch=[pltpu.VMEM((tm, d), BF16), pltpu.VMEM((tm, d), F32)], phases=phases,
    )


def _ffn_bwd(dxo, x, vec, gg, uu, y, w_in, w_out, name, phases=()):
    s, d = x.shape
    f = w_out.shape[1]
    tm = _pick(s, (512, 256, 128))
    tf = _pick(f, (256, 128))
    nf = f // tf

    def body(dxo_ref, x_ref, vec_ref, g_ref, u_ref, y_ref, wg_ref, wu_ref, wo_ref,
             dx_ref, dg_ref, du_ref, a_ref, h_ref, dy_ref, dvec_ref, acc_sc):
        i, j = pl.program_id(0), pl.program_id(1)

        @pl.when((i == 0) & (j == 0))
        def _():
            dvec_ref[...] = jnp.zeros_like(dvec_ref)

        @pl.when(j == 0)
        def _():
            dxo_v = dxo_ref[...]
            dy_ref[...] = (0.5 * vec_ref[3:4, :] * dxo_v).astype(BF16)
            dvec_ref[3:4, :] += 0.5 * jnp.sum(dxo_v * y_ref[...].astype(F32), axis=0, keepdims=True)
            acc_sc[...] = jnp.zeros_like(acc_sc)

        da = _dot_nt(dy_ref[...], wo_ref[0])
        g = g_ref[...].astype(F32)
        u = u_ref[...].astype(F32)
        sig = _sigmoid(g)
        sl = g * sig
        a_ref[...] = (sl * u).astype(BF16)
        dg = (da * u * (sig * (1.0 + g * (1.0 - sig)))).astype(BF16)
        du = (da * sl).astype(BF16)
        dg_ref[...] = dg
        du_ref[...] = du
        acc_sc[...] += _dot_nt(dg, wg_ref[0]) + _dot_nt(du, wu_ref[0])

        @pl.when(j == nf - 1)
        def _():
            dx, h = _modulate_bwd(x_ref[...], acc_sc[...], vec_ref, dvec_ref)
            dx_ref[...] = dxo_ref[...] + dx
            h_ref[...] = h.astype(BF16)

    row = pl.BlockSpec((tm, d), lambda i, j: (i, 0))
    hid = pl.BlockSpec((tm, tf), lambda i, j: (i, j))
    vecs = pl.BlockSpec((8, d), lambda i, j: (0, 0))
    return _call(
        body, name, (s // tm, nf),
        [
            row, row, vecs, hid, hid, row,
            pl.BlockSpec((1, d, tf), lambda i, j: (0, 0, j)),
            pl.BlockSpec((1, d, tf), lambda i, j: (0, 0, nf + j)),
            pl.BlockSpec((1, tf, d), lambda i, j: (0, j, 0)),
        ],
        [row, hid, hid, hid, row, row, vecs],
        [
            jax.ShapeDtypeStruct((s, d), F32),
            jax.ShapeDtypeStruct((s, f), BF16),
            jax.ShapeDtypeStruct((s, f), BF16),
            jax.ShapeDtypeStruct((s, f), BF16),
            jax.ShapeDtypeStruct((s, d), BF16),
            jax.ShapeDtypeStruct((s, d), BF16),
            jax.ShapeDtypeStruct((8, d), F32),
        ],
        [dxo, x, vec, gg, uu, y, w_in, w_in, w_out],
        scratch=[pltpu.VMEM((tm, d), F32)], phases=phases,
    )


def _grad_half(a, bs, big, where, mine, recv, name, phases=()):
    s, k1 = a.shape
    n = bs[0].shape[1]
    groups = len(bs)
    rows_halved = big.h3 == 1
    assert rows_halved or groups == 1
    kk, nn = (k1 // 2, n) if rows_halved else (k1, n // 2)
    tk = _pick(kk, (1408, 1024, 512, 256, 128))
    tn = _pick(nn, (1408, 1024, 640, 512, 256, 128))
    nkb, nnb = kk // tk, nn // tn
    assert (recv is None) == (not mine)

    def half(pref):
        return pref[1] if mine else 1 - pref[1]

    def body(_, a_ref, *rest):
        q = pl.program_id(1)
        for p in range(groups):

            @pl.when(q == p)
            def _(p=p):
                acc = _dot_tn(a_ref[...], rest[p][...])
                if recv is not None:
                    acc = acc + rest[groups][0].astype(F32)
                rest[-1][0] = acc.astype(BF16)

    def b_block(p):
        def index(i, q, j, pref):
            jj = jnp.where(q == p, j, jnp.where(q < p, 0, nnb - 1))
            return (0, jj + (0 if rows_halved else half(pref) * nnb))

        return pl.BlockSpec((s, tn), index)

    out_spec = pl.BlockSpec((1, tk, tn), lambda i, q, j, pref: (0, i, q * nnb + j))
    in_specs = [pl.BlockSpec((s, tk), lambda i, q, j, pref: (0, i + (half(pref) * nkb if rows_halved else 0)))]
    in_specs += [b_block(p) for p in range(groups)]
    ins = [a, *bs]
    if recv is not None:
        in_specs.append(out_spec)
        ins.append(recv)
    return _call(
        body, name, (nkb, groups, nnb), in_specs, [out_spec], [jax.ShapeDtypeStruct(big.dims("half"), BF16)], ins,
        prefetch=(where,), phases=phases,
    )


def _proj_mod_fwd(x, vec, w, phases=()):
    s, d = x.shape
    n = w.shape[2]
    tm = _pick(s, (1024, 512, 256, 128))
    tn = _pick(n, (640, 512, 256, 128))

    def body(x_ref, vec_ref, w_ref, o_ref, h_sc):
        @pl.when(pl.program_id(1) == 0)
        def _():
            h_sc[...] = _modulate(x_ref[...], vec_ref).astype(BF16)

        o_ref[...] = _dot(h_sc[...], w_ref[0])

    return _call(
        body, "ab_in_fwd", (s // tm, n // tn),
        [
            pl.BlockSpec((tm, d), lambda i, j: (i, 0)),
            pl.BlockSpec((8, d), lambda i, j: (0, 0)),
            pl.BlockSpec((1, d, tn), lambda i, j: (0, 0, j)),
        ],
        [pl.BlockSpec((tm, tn), lambda i, j: (i, j))],
        [jax.ShapeDtypeStruct((s, n), F32)], [x, vec, w],
        scratch=[pltpu.VMEM((tm, d), BF16)], phases=phases,
    )


def _proj_res_fwd(a, w, x, vec, phases=()):
    s, kd = a.shape
    d = x.shape[1]
    tm = _pick(s, (1024, 512, 256, 128))

    def body(a_ref, w_ref, x_ref, vec_ref, xo_ref, y_ref):
        yv = _dot(a_ref[...], w_ref[0])
        xo_ref[...] = x_ref[...] + vec_ref[3:4, :] * yv
        y_ref[...] = yv.astype(BF16)

    row = pl.BlockSpec((tm, d), lambda i: (i, 0))
    return _call(
        body, "ab_out_fwd", (s // tm,),
        [pl.BlockSpec((tm, kd), lambda i: (i, 0)), pl.BlockSpec((1, kd, d), lambda i: (0, 0, 0)), row, pl.BlockSpec((8, d), lambda i: (0, 0))],
        [row, row],
        [jax.ShapeDtypeStruct((s, d), F32), jax.ShapeDtypeStruct((s, d), BF16)], [a, w, x, vec], phases=phases,
    )


def _proj_res_bwd(dxo, y, vec, w, phases=()):
    s, d = dxo.shape
    kd = w.shape[1]
    tm = _pick(s, (1024, 512, 256, 128))

    def body(dxo_ref, y_ref, vec_ref, w_ref, dy_ref, da_ref, dgate_ref):
        @pl.when(pl.program_id(0) == 0)
        def _():
            dgate_ref[...] = jnp.zeros_like(dgate_ref)

        dxo_v = dxo_ref[...]
        dy = (vec_ref[3:4, :] * dxo_v).astype(BF16)
        dy_ref[...] = dy
        dgate_ref[3:4, :] += jnp.sum(dxo_v * y_ref[...].astype(F32), axis=0, keepdims=True)
        da_ref[...] = _dot_nt(dy, w_ref[0]).astype(BF16)

    row = pl.BlockSpec((tm, d), lambda i: (i, 0))
    vecs = pl.BlockSpec((8, d), lambda i: (0, 0))
    return _call(
        body, "ab_out_bwd", (s // tm,),
        [row, row, vecs, pl.BlockSpec((1, kd, d), lambda i: (0, 0, 0))],
        [row, pl.BlockSpec((tm, kd), lambda i: (i, 0)), vecs],
        [jax.ShapeDtypeStruct((s, d), BF16), jax.ShapeDtypeStruct((s, kd), BF16), jax.ShapeDtypeStruct((8, d), F32)],
        [dxo, y, vec, w], phases=phases,
    )


def _proj_mod_bwd(dproj, w, x, vec, dxo, dvec_in, name, phases=()):
    parts, s, n_part = dproj.shape
    d = x.shape[1]
    tm = _pick(s, (512, 256, 128))
    tk = _pick(n_part, (1408, 1280, 1024, 512, 256, 128))
    per_part = n_part // tk
    nk = parts * per_part

    def body(dp_ref, w_ref, x_ref, vec_ref, dxo_ref, dvi_ref, dx_ref, h_ref, dvec_ref, acc_sc):
        i, k = pl.program_id(0), pl.program_id(1)

        @pl.when((i == 0) & (k == 0))
        def _():
            dvec_ref[...] = dvi_ref[...]

        @pl.when(k == 0)
        def _():
            acc_sc[...] = jnp.zeros_like(acc_sc)

        acc_sc[...] += _dot_nt(dp_ref[0], w_ref[0])

        @pl.when(k == nk - 1)
        def _():
            dx, h = _modulate_bwd(x_ref[...], acc_sc[...], vec_ref, dvec_ref)
            dx_ref[...] = dxo_ref[...] + dx
            h_ref[...] = h.astype(BF16)

    row = pl.BlockSpec((tm, d), lambda i, k: (i, 0))
    vecs = pl.BlockSpec((8, d), lambda i, k: (0, 0))
    return _call(
        body, name, (s // tm, nk),
        [
            pl.BlockSpec((1, tm, tk), lambda i, k: (k // per_part, i, k % per_part)),
            pl.BlockSpec((1, d, tk), lambda i, k: (0, 0, k)),
            row, vecs, row, vecs,
        ],
        [row, row, vecs],
        [jax.ShapeDtypeStruct((s, d), F32), jax.ShapeDtypeStruct((s, d), BF16), jax.ShapeDtypeStruct((8, d), F32)],
        [dproj, w, x, vec, dxo, dvec_in], scratch=[pltpu.VMEM((tm, d), F32)], phases=phases,
    )


def _tril(n):
    return lax.broadcasted_iota(jnp.int32, (n, n), 0) >= lax.broadcasted_iota(jnp.int32, (n, n), 1)


def _layernorm_stats(gv):
    mu = jnp.mean(gv, axis=-1, keepdims=True)
    cen = gv - mu
    rstd = lax.rsqrt(jnp.mean(cen * cen, axis=-1, keepdims=True) + EPS)
    return cen * rstd, rstd


def _shift_down(q, k, above_ref, c_cg, c_xb, first):
    width = q.shape[1]
    rows = lax.broadcasted_iota(jnp.int32, q.shape, 0)
    out = pltpu.roll(q, k, 0)
    for r in range(k):
        src = CONV_HALO - k + r
        above = above_ref[src : src + 1, c_cg : c_cg + width] * above_ref[src : src + 1, c_xb : c_xb + width]
        above = jnp.where(first, 0.0, above)
        out = jnp.where(rows == r, above, out)
    return out


def _ab_mix_fwd(proj, norm_v, w_s, b_rows, conv_w, phases=()):
    s, n = proj.shape
    heads, chunk, _ = w_s.shape
    da = norm_v.shape[1]
    hd = da // heads
    db = conv_w.shape[1]
    tm = _pick(s, (512, 256, 128))

    def body(p_ref, ph_ref, nv_ref, ws_ref, b_ref, cw_ref, o_ref):
        first = pl.program_id(0) == 0
        gu, _ = _gelu(p_ref[:, 0:da])
        gv, _ = _gelu(p_ref[:, da : 2 * da])
        xhat, _ = _layernorm_stats(gv)
        vn = (xhat * nv_ref[...]).astype(BF16)
        mask = _tril(chunk)
        for hh in range(heads):
            wm = jnp.where(mask, ws_ref[hh], 0.0).astype(BF16)
            cols = slice(hh * hd, (hh + 1) * hd)
            for nn in range(tm // chunk):
                rows = slice(nn * chunk, (nn + 1) * chunk)
                z = _dot(wm, vn[rows, cols]) + b_ref[:, cols]
                o_ref[rows, cols] = (gu[rows, cols] * z).astype(BF16)
        c_cg, c_xb = 2 * da + db, 2 * da + 2 * db
        bg = p_ref[:, 2 * da : 2 * da + db]
        q = p_ref[:, c_cg : c_cg + db] * p_ref[:, c_xb : c_xb + db]
        q1 = _shift_down(q, 1, ph_ref, c_cg, c_xb, first)
        q2 = _shift_down(q, 2, ph_ref, c_cg, c_xb, first)
        conv = cw_ref[0:1, :] * q2 + cw_ref[1:2, :] * q1 + cw_ref[2:3, :] * q
        o_ref[:, da : da + db] = (bg * conv).astype(BF16)

    nh = tm // CONV_HALO
    return _call(
        body, "ab_mix_fwd", (s // tm,),
        [
            pl.BlockSpec((tm, n), lambda i: (i, 0)),
            pl.BlockSpec((CONV_HALO, n), lambda i: (jnp.maximum(i * nh - 1, 0), 0)),
            pl.BlockSpec((1, da), lambda i: (0, 0)),
            pl.BlockSpec((heads, chunk, chunk), lambda i: (0, 0, 0)),
            pl.BlockSpec((chunk, da), lambda i: (0, 0)),
            pl.BlockSpec((3, db), lambda i: (0, 0)),
        ],
        [pl.BlockSpec((tm, da + db), lambda i: (i, 0))],
        [jax.ShapeDtypeStruct((s, da + db), BF16)], [proj, proj, norm_v, w_s, b_rows, conv_w], phases=phases,
    )


def _ab_mix_bwd(proj, dcat, norm_v, w_s, b_rows, conv_w, phases=()):
    s, n = proj.shape
    heads, chunk, _ = w_s.shape
    da = norm_v.shape[1]
    hd = da // heads
    db = conv_w.shape[1]
    tm = _pick(s, (512, 256, 128))
    nblk = s // tm
    dhalo = 2 * CONV_HALO

    def body(p_ref, pa_ref, pb_ref, dc_ref, dcb_ref, nv_ref, ws_ref, b_ref, cw_ref,
             dp_ref, dnv_ref, dws_ref, dzs_ref, dcw_ref, dvn_sc):
        i = pl.program_id(0)
        first, last = i == 0, i == nblk - 1

        @pl.when(first)
        def _():
            dnv_ref[...] = jnp.zeros_like(dnv_ref)
            dws_ref[...] = jnp.zeros_like(dws_ref)
            dzs_ref[...] = jnp.zeros_like(dzs_ref)
            dcw_ref[...] = jnp.zeros_like(dcw_ref)

        uu = p_ref[:, 0:da]
        gu, gu_grad = _gelu(uu)
        gv, gv_grad = _gelu(p_ref[:, da : 2 * da])
        xhat, rstd = _layernorm_stats(gv)
        nv = nv_ref[...]
        vn = (xhat * nv).astype(BF16)
        dya = dc_ref[:, 0:da].astype(F32)
        dz = (dya * gu).astype(BF16)
        mask = _tril(chunk)
        for hh in range(heads):
            wm = jnp.where(mask, ws_ref[hh], 0.0).astype(BF16)
            cols = slice(hh * hd, (hh + 1) * hd)
            dws = jnp.zeros((chunk, chunk), F32)
            for nn in range(tm // chunk):
                rows = slice(nn * chunk, (nn + 1) * chunk)
                z = _dot(wm, vn[rows, cols]) + b_ref[:, cols]
                dp_ref[rows, cols] = (dya[rows, cols] * z * gu_grad[rows, cols]).astype(BF16)
                dz_blk = dz[rows, cols]
                dws = dws + _dot_nt(dz_blk, vn[rows, cols])
                dzs_ref[:, cols] += dz_blk.astype(F32)
                dvn = _dot_tn(wm, dz_blk)
                dnv_ref[:, cols] += jnp.sum(dvn * xhat[rows, cols], axis=0, keepdims=True)
                dvn_sc[rows, cols] = dvn
            dws_ref[hh] += jnp.where(mask, dws, 0.0)
        dxhat = dvn_sc[...] * nv
        dgv = rstd * (dxhat - jnp.mean(dxhat, axis=-1, keepdims=True) - xhat * jnp.mean(dxhat * xhat, axis=-1, keepdims=True))
        dp_ref[:, da : 2 * da] = (dgv * gv_grad).astype(BF16)

        c_bg, c_cg, c_xb = 2 * da, 2 * da + db, 2 * da + 2 * db
        bg = p_ref[:, c_bg : c_bg + db]
        cg = p_ref[:, c_cg : c_cg + db]
        xb = p_ref[:, c_xb : c_xb + db]
        q = cg * xb
        q1 = _shift_down(q, 1, pa_ref, c_cg, c_xb, first)
        q2 = _shift_down(q, 2, pa_ref, c_cg, c_xb, first)
        dyb = dc_ref[:, da : da + db].astype(F32)
        conv = cw_ref[0:1, :] * q2 + cw_ref[1:2, :] * q1 + cw_ref[2:3, :] * q
        dp_ref[:, c_bg : c_bg + db] = (dyb * conv).astype(BF16)
        e = dyb * bg
        dcw_ref[0:1, :] += jnp.sum(e * q2, axis=0, keepdims=True)
        dcw_ref[1:2, :] += jnp.sum(e * q1, axis=0, keepdims=True)
        dcw_ref[2:3, :] += jnp.sum(e * q, axis=0, keepdims=True)
        rows = lax.broadcasted_iota(jnp.int32, e.shape, 0)
        dq = cw_ref[2:3, :] * e
        for kk in (1, 2):
            ek = pltpu.roll(e, tm - kk, 0)
            for r in range(kk):
                below = dcb_ref[r : r + 1, da : da + db].astype(F32) * pb_ref[r : r + 1, c_bg : c_bg + db]
                below = jnp.where(last, 0.0, below)
                ek = jnp.where(rows == tm - kk + r, below, ek)
            dq = dq + cw_ref[2 - kk : 3 - kk, :] * ek
        dp_ref[:, c_cg : c_cg + db] = (dq * xb).astype(BF16)
        dp_ref[:, c_xb : c_xb + db] = (dq * cg).astype(BF16)

    nh = tm // CONV_HALO
    nhb = tm // dhalo
    const2 = lambda i: (0, 0)
    return _call(
        body, "ab_mix_bwd", (nblk,),
        [
            pl.BlockSpec((tm, n), lambda i: (i, 0)),
            pl.BlockSpec((CONV_HALO, n), lambda i: (jnp.maximum(i * nh - 1, 0), 0)),
            pl.BlockSpec((CONV_HALO, n), lambda i: (jnp.minimum((i + 1) * nh, s // CONV_HALO - 1), 0)),
            pl.BlockSpec((tm, da + db), lambda i: (i, 0)),
            pl.BlockSpec((dhalo, da + db), lambda i: (jnp.minimum((i + 1) * nhb, s // dhalo - 1), 0)),
            pl.BlockSpec((1, da), const2),
            pl.BlockSpec((heads, chunk, chunk), lambda i: (0, 0, 0)),
            pl.BlockSpec((chunk, da), const2),
            pl.BlockSpec((3, db), const2),
        ],
        [
            pl.BlockSpec((tm, n), lambda i: (i, 0)),
            pl.BlockSpec((1, da), const2),
            pl.BlockSpec((heads, chunk, chunk), lambda i: (0, 0, 0)),
            pl.BlockSpec((chunk, da), const2),
            pl.BlockSpec((3, db), const2),
        ],
        [
            jax.ShapeDtypeStruct((s, n), BF16),
            jax.ShapeDtypeStruct((1, da), F32),
            jax.ShapeDtypeStruct((heads, chunk, chunk), F32),
            jax.ShapeDtypeStruct((chunk, da), F32),
            jax.ShapeDtypeStruct((3, db), F32),
        ],
        [proj, proj, proj, dcat, dcat, norm_v, w_s, b_rows, conv_w],
        scratch=[pltpu.VMEM((tm, da), F32)], phases=phases,
    )


def _pool_counts(tm, i, w):
    t = i * tm + lax.broadcasted_iota(jnp.int32, (tm, 1), 0)
    return jnp.minimum(t + 1, w).astype(F32)


def _pool_fwd(x, vec, w_grp, scale, phases=()):
    s, d = x.shape
    groups, gd, _ = w_grp.shape
    tm = _pick(s, (512, 256, 128))

    def body(x_ref, xa_ref, vec_ref, w_ref, sc_ref, xo_ref, p_ref, o_ref):
        i = pl.program_id(0)
        h = _modulate(x_ref[...], vec_ref)
        ha = jnp.where(i == 0, 0.0, _modulate(xa_ref[...], vec_ref))
        ext = jnp.concatenate([ha, h], axis=0)
        for gi, w in enumerate(POOL_WINDOWS):
            cols = slice(gi * gd, (gi + 1) * gd)
            acc = ext[:, cols]
            step = 1
            while step < w:
                acc = acc + pltpu.roll(acc, step, 0)
                step *= 2
            p = (acc[POOL_HALO:, :] / _pool_counts(tm, i, w) - h[:, cols]).astype(BF16)
            p_ref[:, cols] = p
            o_ref[:, cols] = _dot(p, w_ref[gi]).astype(BF16)
        xo_ref[...] = x_ref[...] + vec_ref[3:4, :] * (o_ref[...].astype(F32) * sc_ref[...])

    nh = tm // POOL_HALO
    row = pl.BlockSpec((tm, d), lambda i: (i, 0))
    return _call(
        body, "pool_fwd", (s // tm,),
        [
            row,
            pl.BlockSpec((POOL_HALO, d), lambda i: (jnp.maximum(i * nh - 1, 0), 0)),
            pl.BlockSpec((8, d), lambda i: (0, 0)),
            pl.BlockSpec((groups, gd, gd), lambda i: (0, 0, 0)),
            pl.BlockSpec((1, d), lambda i: (0, 0)),
        ],
        [row, row, row],
        [jax.ShapeDtypeStruct((s, d), F32), jax.ShapeDtypeStruct((s, d), BF16), jax.ShapeDtypeStruct((s, d), BF16)],
        [x, x, vec, w_grp, scale], phases=phases,
    )


def _pool_bwd(dxo, x, vec, p, o, w_grp, scale, phases=()):
    s, d = x.shape
    groups, gd, _ = w_grp.shape
    tm = _pick(s, (512, 256, 128))
    nblk = s // tm

    def body(dxo_ref, dxb_ref, x_ref, vec_ref, p_ref, o_ref, w_ref, sc_ref, dx_ref, dw_ref, dsc_ref, dvec_ref, dw_sc):
        i = pl.program_id(0)

        @pl.when(i == 0)
        def _():
            dw_sc[...] = jnp.zeros_like(dw_sc)
            dsc_ref[...] = jnp.zeros_like(dsc_ref)
            dvec_ref[...] = jnp.zeros_like(dvec_ref)

        gate, sc = vec_ref[3:4, :], sc_ref[...]
        dxo_v = dxo_ref[...]
        ov = o_ref[...].astype(F32)
        dvec_ref[3:4, :] += jnp.sum(dxo_v * (ov * sc), axis=0, keepdims=True)
        dy = gate * dxo_v
        dsc_ref[...] += jnp.sum(dy * ov, axis=0, keepdims=True)
        dout = (dy * sc).astype(BF16)
        dout_b = jnp.where(i == nblk - 1, 0.0, gate * dxb_ref[...] * sc).astype(BF16)
        for gi, w in enumerate(POOL_WINDOWS):
            cols = slice(gi * gd, (gi + 1) * gd)
            dw_sc[gi] += _dot_tn(p_ref[:, cols], dout[:, cols])
            wb = w_ref[gi]
            dp = _dot_nt(dout[:, cols], wb)
            dp_b = _dot_nt(dout_b[:, cols], wb)
            e = dp / _pool_counts(tm, i, w)
            t_below = (i + 1) * tm + lax.broadcasted_iota(jnp.int32, (POOL_HALO, 1), 0)
            e_b = dp_b / jnp.minimum(t_below + 1, w).astype(F32)
            acc = jnp.concatenate([e, e_b], axis=0)
            step = 1
            while step < w:
                acc = acc + pltpu.roll(acc, tm + POOL_HALO - step, 0)
                step *= 2
            dx_ref[:, cols] = acc[:tm, :] - dp
        dx, _ = _modulate_bwd(x_ref[...], dx_ref[...], vec_ref, dvec_ref)
        dx_ref[...] = dxo_v + dx

        @pl.when(i == nblk - 1)
        def _():
            dw_ref[...] = dw_sc[...].astype(BF16)

    nh = tm // POOL_HALO
    row = pl.BlockSpec((tm, d), lambda i: (i, 0))
    vecs = pl.BlockSpec((8, d), lambda i: (0, 0))
    wspec = pl.BlockSpec((groups, gd, gd), lambda i: (0, 0, 0))
    return _call(
        body, "pool_bwd", (nblk,),
        [
            row,
            pl.BlockSpec((POOL_HALO, d), lambda i: (jnp.minimum((i + 1) * nh, s // POOL_HALO - 1), 0)),
            row, vecs, row, row, wspec,
            pl.BlockSpec((1, d), lambda i: (0, 0)),
        ],
        [row, wspec, pl.BlockSpec((1, d), lambda i: (0, 0)), vecs],
        [
            jax.ShapeDtypeStruct((s, d), F32),
            jax.ShapeDtypeStruct((groups, gd, gd), BF16),
            jax.ShapeDtypeStruct((1, d), F32),
            jax.ShapeDtypeStruct((8, d), F32),
        ],
        [dxo, dxo, x, vec, p, o, w_grp, scale],
        scratch=[pltpu.VMEM((groups, gd, gd), F32)], phases=phases,
    )


def _loss_head(x, gain, target, phases=()):
    s, d = x.shape
    tm = _pick(s, (512, 256, 128))

    def body(x_ref, g_ref, t_ref, dx_ref, aux_ref):
        @pl.when(pl.program_id(0) == 0)
        def _():
            aux_ref[...] = jnp.zeros_like(aux_ref)

        xv = x_ref[...]
        rstd = _rstd(xv)
        r = xv * rstd
        gain_v = g_ref[...]
        err = r * gain_v - t_ref[...]
        aux_ref[1:2, :] += jnp.sum(err * err, axis=0, keepdims=True)
        dout = err * (1.0 / d)
        aux_ref[0:1, :] += jnp.sum(dout * r, axis=0, keepdims=True)
        dr = dout * gain_v
        dx_ref[...] = rstd * (dr - r * jnp.mean(dr * r, axis=-1, keepdims=True))

    row = pl.BlockSpec((tm, d), lambda i: (i, 0))
    return _call(
        body, "loss_head", (s // tm,),
        [row, pl.BlockSpec((1, d), lambda i: (0, 0)), row],
        [row, pl.BlockSpec((8, d), lambda i: (0, 0))],
        [jax.ShapeDtypeStruct((s, d), F32), jax.ShapeDtypeStruct((8, d), F32)], [x, gain, target], phases=phases,
    )


def _small_adam(gathered, gathered_ws, layout, smalls, chip):
    names = list(smalls)
    n = len(names)
    loss_row, _, _, n_feat = layout["loss"]

    def body(*refs):
        chip_ref, g_ref, gws_ref = refs[0], refs[1], refs[2]
        wmv = refs[3 : 3 + 3 * n]
        outs = refs[3 + 3 * n : 3 + 7 * n]
        total = refs[-1]
        total[...] = g_ref[0]
        for kdev in range(1, N_DEV):
            total[...] += g_ref[kdev]
        total_ws = gws_ref[0]
        for kdev in range(1, N_DEV):
            total_ws = total_ws + gws_ref[kdev]
        my_chip = chip_ref[0]
        for a, name in enumerate(names):
            w_ref, m_ref, v_ref = wmv[3 * a : 3 * a + 3]
            if name == "ab_w_s":
                g = total_ws
            else:
                row0, rows, col0, cols = layout[name]
                if col0 is None:
                    g = jnp.zeros((rows, cols), F32)
                    for j in range(N_CHIPS):
                        g = g + jnp.where(my_chip == j, total[row0 : row0 + rows, j * cols : (j + 1) * cols], 0.0)
                else:
                    g = total[row0 : row0 + rows, col0 : col0 + cols]
            dl, mo, vo = _adam(w_ref[...], g, m_ref[...], v_ref[...])
            outs[4 * a][...] = g
            outs[4 * a + 1][...] = dl
            outs[4 * a + 2][...] = mo
            outs[4 * a + 3][...] = vo
        refs[3 + 7 * n][...] = 0.5 * jnp.sum(total[loss_row : loss_row + 1, 0:n_feat], axis=1, keepdims=True) / n_feat

    ins = [gathered, gathered_ws]
    out_shapes = []
    for name in names:
        ins.extend(smalls[name])
        out_shapes.extend([jax.ShapeDtypeStruct(smalls[name][0].shape, F32)] * 4)
    out_shapes.append(jax.ShapeDtypeStruct((1, 1), F32))
    whole = lambda shape: pl.BlockSpec(shape, functools.partial(lambda nd, i, c: (0,) * nd, len(shape)))
    res = pl.pallas_call(
        body, name="small_adam",
        grid_spec=pltpu.PrefetchScalarGridSpec(
            num_scalar_prefetch=1, grid=(1,),
            in_specs=[whole(a.shape) for a in ins], out_specs=[whole(o.shape) for o in out_shapes],
            scratch_shapes=[pltpu.VMEM(gathered.shape[1:], F32)],
        ),
        out_shape=out_shapes,
        compiler_params=pltpu.CompilerParams(dimension_semantics=("arbitrary",), vmem_limit_bytes=VMEM_LIMIT_BYTES),
    )(chip.reshape(1).astype(jnp.int32), *ins)
    return {name: res[4 * a : 4 * a + 4] for a, name in enumerate(names)}, res[4 * n]


def _pad_rows(a, rows=8):
    extra = (-a.shape[0]) % rows
    return jnp.pad(a, ((0, extra), (0, 0))) if extra else a


def _pad_cols(a, cols):
    return jnp.pad(a, ((0, 0), (0, cols - a.shape[1]))) if a.shape[1] < cols else a


def _run(fn, *phases):
    outs, p_outs = fn(list(phases))
    for p, po in zip(phases, p_outs):
        p.then(po)
    return outs


def kernel(x, c, norm_g, w_mod, b_mod, w_ffn_in, w_ffn_out, ab_w_in, ab_norm_v, ab_w_s, ab_b_s, ab_conv_w, ab_w_out, pool_w_grp, pool_scale, final_g, loss_target, m_norm_g, m_w_mod, m_b_mod, m_w_ffn_in, m_w_ffn_out, m_ab_w_in, m_ab_norm_v, m_ab_w_s, m_ab_b_s, m_ab_conv_w, m_ab_w_out, m_pool_w_grp, m_pool_scale, m_final_g, v_norm_g, v_w_mod, v_b_mod, v_w_ffn_in, v_w_ffn_out, v_ab_w_in, v_ab_norm_v, v_ab_w_s, v_ab_b_s, v_ab_conv_w, v_ab_w_out, v_pool_w_grp, v_pool_scale, v_final_g):
    ix, iy, ic = _place()
    chip = 2 * ix + iy
    me = 4 * ix + 2 * iy + ic
    where = jnp.stack([chip, ic]).astype(jnp.int32)
    s, d = x.shape[1], x.shape[2]
    x0 = x.reshape(s, d)
    target = loss_target.reshape(s, d)
    n_layers = norm_g.shape[0]
    dq = d // N_CHIPS
    heads, chunk = ab_w_s.shape[1], ab_w_s.shape[2]
    da = ab_norm_v.shape[1]
    db = ab_conv_w.shape[2] * N_CHIPS
    f_hidden = w_ffn_out.shape[2] * N_CHIPS
    assert n_layers == 2 and da % heads == 0

    cw_pad = _pad_cols(ab_conv_w.reshape(3, db // N_CHIPS), dq)
    packed = jnp.concatenate(
        [_pad_rows(c.reshape(N_CHIPS, dq)), _pad_rows(norm_g.reshape(-1, dq)), _pad_rows(pool_scale.reshape(1, dq)), _pad_rows(cw_pad)],
        axis=0,
    )
    ncol = w_mod.shape[2]
    b_cols = lax.dynamic_slice(b_mod, (0, chip * ncol), (n_layers, ncol)).reshape(n_layers, 1, ncol)
    small = {}

    def small_gather(key, arrs):
        def then(outs):
            small[key] = outs

        return _phase_small_gather(arrs, then)

    stacks = {
        "w_ffn_in": tuple(a.reshape((-1,) + a.shape[2:]) for a in (w_ffn_in, m_w_ffn_in, v_w_ffn_in)),
        "w_ffn_out": tuple(a.reshape((-1,) + a.shape[2:]) for a in (w_ffn_out, m_w_ffn_out, v_w_ffn_out)),
        "ab_w_in": (ab_w_in, m_ab_w_in, v_ab_w_in),
        "ab_w_out": (ab_w_out, m_ab_w_out, v_ab_w_out),
        "pool_w_grp": (pool_w_grp[0], m_pool_w_grp[0], v_pool_w_grp[0]),
    }
    big_in = _Big((1, d, 2 * f_hidden), 2, 1)
    big_out = _Big((1, f_hidden, d), 1, 2)
    units = {}
    for l in range(n_layers):
        for k in range(2):
            units[f"in{l}{k}"] = (big_in, "w_ffn_in", 2 * l + k)
            units[f"out{l}{k}"] = (big_out, "w_ffn_out", 2 * l + k)
    units["abin"] = (_Big((1, d, ab_w_in.shape[2] * N_CHIPS), 2, 1), "ab_w_in", 0)
    units["about"] = (_Big((1, ab_w_out.shape[1] * N_CHIPS, d), 1, 2), "ab_w_out", 0)
    units["pool"] = (_Big((pool_w_grp.shape[1], pool_w_grp.shape[2] * N_CHIPS, pool_w_grp.shape[3]), 1, 0), "pool_w_grp", 0)
    big = {u: g for u, (g, _, _) in units.items()}

    weight = {}
    complete = set()

    def cast(u):
        g, st, b0 = units[u]

        def launch(phases):
            (weight[u],), p_outs = _cast_into_full(stacks[st][0], b0, g, where, "cast_" + u, phases)
            return None, p_outs

        return launch

    def gather_relay(us, second, whole_first):
        def then(outs):
            for u, o in zip(us, outs):
                weight[u] = o

        return _phase_gather_relay([weight[u] for u in us], [big[u] for u in us], second, whole_first, then)

    def gather_sibling(*us):
        def then(outs):
            for u, o in zip(us, outs):
                weight[u] = o
                complete.add(u)

        return _phase_gather_sibling([weight[u] for u in us], [big[u] for u in us], then)

    def w_of(u):
        assert u in complete, u
        return weight[u]

    _run(cast("in00"), small_gather("inputs", [packed]))
    small_all = small["inputs"][0]
    by_chip = small_all[0::2]
    c_all = small_all[:, 0:N_CHIPS, :].reshape(N_DEV, d)
    norm_full = by_chip[:, 8 : 8 + 3 * n_layers, :].transpose(1, 0, 2).reshape(3 * n_layers, d)
    pool_scale_full = by_chip[:, 16:17, :].transpose(1, 0, 2).reshape(1, d)
    conv_full = by_chip[:, 24:27, : db // N_CHIPS].transpose(1, 0, 2).reshape(3, db)
    pieces = [("in00", "out00"), ("abin", "about"), ("in01", "out01"), ("in10", "out10", "pool"), ("in11", "out11")]
    in_flight = {}

    def start_gather(p):
        in_flight[p, 0] = _split_start(gather_relay(pieces[p], False, p == 0), f"gather_{p}_start")

    def relay_gather(p, after=()):
        flight = in_flight.pop((p, 0))
        _split_wait(flight, list(after) + list(started().ins), f"gather_{p}_arrived")
        in_flight[p, 1] = _split_start(gather_relay(pieces[p], True, p == 0), f"gather_{p}_relay")

    def started():
        return _after(*[flight.token for flight in in_flight.values()])

    def finish_gather(p, after, meanwhile=None):
        flight = in_flight.pop((p, 1))
        _split_wait(flight, list(after) + list(started().ins), f"gather_{p}_wait")
        crossing = _split_start(gather_sibling(*pieces[p]), f"gather_{p}_forward")
        behind = [crossing.token]
        if p + 1 < len(pieces):
            relay_gather(p + 1)
        if p + 3 < len(pieces):
            start_gather(p + 3)
        behind = behind + list(started().ins)
        if meanwhile is not None:
            behind = behind + meanwhile(_after(crossing.token))
        _split_wait(crossing, behind, f"gather_{p}_forwarded")

    _run(cast("out00"))
    start_gather(0)
    mod_cols = _run(lambda phases: _mod_fwd(c_all, w_mod, b_cols, phases), started())[0]

    def mod_rows(outs):
        small["mod"] = outs

    _run(cast("about"), started())
    early = [u for piece in pieces[2:4] for u in piece]
    for u in early:
        _run(cast(u), started())
    _run(
        cast("abin"), _phase_small_exchange(mod_cols.transpose(1, 0, 2), mod_rows),
        started(), _after(*[weight[u] for u in early]),
    )
    relay_gather(0)
    start_gather(1)
    start_gather(2)
    for u in pieces[4]:
        _run(cast(u), started())
    mod_mine = small["mod"][0][0::2]
    mod = mod_mine.transpose(1, 0, 2).reshape(n_layers, 3, 3, d)
    vecs = {
        (l, sub): jnp.pad(norm_full[3 * l + sub][None], ((0, 7), (0, 0))) + jnp.pad(mod[l, sub], ((1, 4), (0, 0)))
        for l in range(n_layers)
        for sub in range(3)
    }
    b_rows = jnp.broadcast_to(ab_b_s[0].T[:, :, None], (chunk, heads, da // heads)).reshape(chunk, da)

    saved = {}

    def ffn_forward(xs, l, sub, k, *phases):
        saved[l, sub, "x"] = xs
        xs, gg, uu, yb = _run(
            lambda ph: _ffn_fwd(xs, vecs[l, sub], w_of(f"in{l}{k}"), w_of(f"out{l}{k}"), f"ffn_fwd_{l}{k}", ph), *phases
        )
        saved[l, sub, "act"] = (gg, uu, yb)
        return xs

    finish_gather(0, [vecs[0, 0]] + [weight[u] for u in pieces[4]])
    xs = ffn_forward(x0, 0, 0, 0, started())
    saved[0, 1, "x"] = xs
    finish_gather(1, [xs])
    (proj,) = _run(lambda ph: _proj_mod_fwd(xs, vecs[0, 1], w_of("abin"), ph), started())
    (cat,) = _run(lambda ph: _ab_mix_fwd(proj, ab_norm_v, ab_w_s[0], b_rows, conv_full, ph))
    xs, yb = _run(lambda ph: _proj_res_fwd(cat, w_of("about"), xs, vecs[0, 1], ph))
    saved[0, 1, "act"] = (proj, cat, yb)
    finish_gather(2, [xs])
    xs = ffn_forward(xs, 0, 2, 1, started())
    finish_gather(3, [xs])
    xs = ffn_forward(xs, 1, 0, 0, started())
    saved[1, 1, "x"] = xs
    pooled = []

    def pool_forward(behind):
        pooled.extend(_run(lambda ph: _pool_fwd(xs, vecs[1, 1], w_of("pool"), pool_scale_full, ph), behind))
        return [pooled[0]]

    finish_gather(4, [xs], pool_forward)
    xs, pp, oo = pooled
    saved[1, 1, "act"] = (pp, oo)
    xs = ffn_forward(xs, 1, 2, 1)
    dxs, aux = _run(lambda ph: _loss_head(xs, final_g.reshape(1, d), target, ph))

    grad = {}
    recv = {}
    csum = {}
    parts = {}
    reduced = {}
    done = set()
    dvecs, small_g = {}, {}

    def pair_exchange(*us):
        def then(outs):
            for u, o in zip(us, outs):
                recv[u] = o

        return _phase_pair_exchange([grad[u] for u in us], [big[u] for u in us], then)

    def grad_half(u, a, bs, mine, name, *phases):
        (res,) = _run(lambda ph: _grad_half(a, bs, big[u], where, mine, recv[u] if mine else None, name, ph), *phases)
        return res

    def pair_sum(u, *phases):
        def launch(ph):
            (csum[u],), p_outs = _pair_sum(grad[u], recv[u], big[u], where, "pair_sum_" + u, ph)
            return None, p_outs

        _run(launch, *phases)

    def chip_exchange(*us):
        def then(outs):
            for u, o in zip(us, outs):
                parts[u] = o

        return _phase_chip_exchange([csum[u] for u in us], [big[u] for u in us], then)

    def chip_sum(*us, carried=()):
        for n_u, u in enumerate(us):
            g, st, b0 = units[u]

            def launch(ph):
                (reduced[st],), p_outs = _chip_sum(
                    csum[u], parts[u], g, where, reduced.get(st), stacks[st][0].shape, b0, "chip_sum_" + u, ph
                )
                return None, p_outs

            _run(launch, *(carried if n_u == 0 else ()))

    def pair_broadcast(*us):
        sts = [units[u][1] for u in us]
        assert len(set(sts)) == len(sts)

        def then(outs):
            for u, st, o in zip(us, sts, outs):
                reduced[st] = o
                done.add(u)

        return _phase_pair_broadcast([reduced[st] for st in sts], [big[u] for u in us], [units[u][2] for u in us], then)

    def ffn_backward(dxs, l, sub, k, carried_bwd, carried_send, carried_mine):
        gg, uu, yb = saved[l, sub, "act"]
        w_in, w_out = w_of(f"in{l}{k}"), w_of(f"out{l}{k}")
        uo, ui, tag = f"out{l}{k}", f"in{l}{k}", f"{l}{k}"
        dxs, dg, du, a, h, dy, dvecs[l, sub] = _run(
            lambda ph: _ffn_bwd(dxs, saved[l, sub, "x"], vecs[l, sub], gg, uu, yb, w_in, w_out, "ffn_bwd_" + tag, ph), *carried_bwd()
        )
        grad[uo] = grad_half(uo, a, [dy], False, "dw_out_send_" + tag, *carried_send())
        grad[ui] = grad_half(ui, h, [dg, du], False, "dw_in_send_" + tag, pair_exchange(uo))
        csum[uo] = grad_half(uo, a, [dy], True, "dw_out_" + tag, pair_exchange(ui))
        csum[ui] = grad_half(ui, h, [dg, du], True, "dw_in_" + tag, *carried_mine())
        return dxs

    none = lambda: ()
    dxs = ffn_backward(dxs, 1, 2, 1, none, none, none)
    pp, oo = saved[1, 1, "act"]
    dxs, grad["pool"], small_g["pool_scale"], dvecs[1, 1] = _run(
        lambda ph: _pool_bwd(dxs, saved[1, 1, "x"], vecs[1, 1], pp, oo, w_of("pool"), pool_scale_full, ph)
    )

    def after_11():
        return (chip_exchange("in11", "out11"), pair_exchange("pool"))

    def bcast_11():
        chip_sum("in11", "out11")
        pair_sum("pool")
        return (pair_broadcast("in11", "out11"), chip_exchange("pool"))

    dxs = ffn_backward(dxs, 1, 0, 0, after_11, bcast_11, none)

    def after_10():
        return (chip_exchange("in10", "out10"),)

    def bcast_10():
        chip_sum("in10", "out10", "pool")
        return (pair_broadcast("in10", "out10", "pool"),)

    dxs = ffn_backward(dxs, 0, 2, 1, after_10, bcast_10, none)

    proj, cat, yb = saved[0, 1, "act"]
    out01 = _split_start(chip_exchange("out01"), "reduce_out01_start")
    dy, dcat, dgate = _run(lambda ph: _proj_res_bwd(dxs, yb, vecs[0, 1], w_of("about"), ph), _after(out01.token))
    grad["about"] = grad_half("about", cat, [dy], False, "dw_ab_out_send")
    dproj, small_g["ab_norm_v"], small_g["ab_w_s"], dzs, small_g["ab_conv_w"] = _run(
        lambda ph: _ab_mix_bwd(proj, dcat, ab_norm_v, ab_w_s[0], b_rows, conv_full, ph), pair_exchange("about")
    )
    small_g["ab_b_s"] = dzs.reshape(chunk, heads, da // heads).sum(axis=2).T
    dxs, h, dvecs[0, 1] = _run(
        lambda ph: _proj_mod_bwd(dproj[None], w_of("abin"), saved[0, 1, "x"], vecs[0, 1], dxs, dgate, "ab_in_bwd", ph)
    )
    grad["abin"] = grad_half("abin", h, [dproj], False, "dw_ab_in_send")
    (csum["out01"],) = _split_wait(out01, [grad["abin"]], "reduce_out01_wait")
    chip_sum("out01", carried=(pair_exchange("abin"),))
    csum["about"] = grad_half("about", cat, [dy], True, "dw_ab_out", pair_broadcast("out01"))
    csum["abin"] = grad_half("abin", h, [dproj], True, "dw_ab_in")

    layout = {}
    tail = {}

    def after_01():
        tail["01"] = _split_start(chip_exchange("in01", "abin", "about"), "reduce_01_start")
        return (_after(tail["01"].token),)

    def pack_small_grads():
        dvec_all = jnp.stack([dvecs[l, sub] for l in range(n_layers) for sub in range(3)])
        dgain = dvec_all[:, 0, :]
        dmod = dvec_all[:, 1:4, :].reshape(3 * 3 * n_layers, d)
        rows = {
            "norm_g": (dgain, None, dq), "final_g": (aux[0:1], 0, d), "pool_scale": (small_g["pool_scale"], None, dq),
            "b_mod": (dmod, 0, d), "ab_norm_v": (small_g["ab_norm_v"], 0, da),
            "ab_conv_w": (small_g["ab_conv_w"], None, db // N_CHIPS), "ab_b_s": (small_g["ab_b_s"], 0, chunk),
            "loss": (aux[1:2], 0, d),
        }
        row0 = 0
        for nm, (pc, col0, cols) in rows.items():
            layout[nm] = (row0, pc.shape[0], col0, cols)
            row0 += pc.shape[0]
        packed_rows = -(-row0 // 8) * 8
        return sum(
            jnp.pad(pc, ((layout[nm][0], packed_rows - layout[nm][0] - pc.shape[0]), (0, d - pc.shape[1])))
            for nm, (pc, _, _) in rows.items()
        )

    def bcast_01():
        csum["in01"], csum["abin"], csum["about"] = _split_wait(tail["01"], [dvecs[0, 0]], "reduce_01_wait")
        chip_sum("in01", "abin", "about")
        grads_small = [pack_small_grads(), small_g["ab_w_s"].reshape(heads * chunk, chunk)]
        tail["small"] = _split_start(small_gather("grads", grads_small), "gather_small_grads_start")
        return (pair_broadcast("in01", "abin", "about"), _after(tail["small"].token))

    def reduce_out00():
        tail["out00"] = _split_start(chip_exchange("out00"), "reduce_out00_start")
        return (_after(tail["out00"].token),)

    dxs = ffn_backward(dxs, 0, 0, 0, after_01, bcast_01, reduce_out00)
    grad_x = dxs.reshape(x.shape)

    last = _split_start(chip_exchange("in00"), "reduce_last_start")
    (csum["out00"],) = _split_wait(tail["out00"], [last.token], "reduce_out00_wait")
    chip_sum("out00")
    _flush("broadcast_out00", pair_broadcast("out00"))
    _split_wait(tail["small"], [reduced["w_ffn_out"]], "gather_small_grads_wait")
    g_all, gws_all = small["grads"]

    out = {}

    def adam_stack(st, after=()):
        w3, m3, v3 = stacks[st]
        assert all(u in done for u, (_, ust, _) in units.items() if ust == st), st
        shape = {"w_ffn_in": w_ffn_in.shape, "w_ffn_out": w_ffn_out.shape, "pool_w_grp": pool_w_grp.shape}.get(st, w3.shape)
        out[st] = tuple(a.reshape(shape) for a in _adam_stack(w3, reduced[st], m3, v3, "adam_" + st, after))

    for st in ("w_ffn_out", "ab_w_in", "ab_w_out", "pool_w_grp"):
        adam_stack(st, (last.token,))

    shapes2d = {
        "norm_g": (3 * n_layers, dq), "b_mod": (9 * n_layers, d), "final_g": (1, d), "ab_norm_v": (1, da),
        "pool_scale": (1, dq), "ab_conv_w": (3, db // N_CHIPS), "ab_b_s": (heads, chunk), "ab_w_s": (heads * chunk, chunk),
    }
    small_w = {"norm_g": (norm_g, m_norm_g, v_norm_g), "b_mod": (b_mod, m_b_mod, v_b_mod), "final_g": (final_g, m_final_g, v_final_g),
               "ab_norm_v": (ab_norm_v, m_ab_norm_v, v_ab_norm_v), "pool_scale": (pool_scale, m_pool_scale, v_pool_scale),
               "ab_conv_w": (ab_conv_w, m_ab_conv_w, v_ab_conv_w), "ab_b_s": (ab_b_s, m_ab_b_s, v_ab_b_s), "ab_w_s": (ab_w_s, m_ab_w_s, v_ab_w_s)}
    smalls = {nm: tuple(a.reshape(shapes2d[nm]) for a in wmv) for nm, wmv in small_w.items()}
    small_out, loss = _small_adam(g_all, gws_all, layout, smalls, chip)
    loss = loss.reshape(())
    for nm, res in small_out.items():
        out[nm] = tuple(a.reshape(small_w[nm][0].shape) for a in res)

    mod_row0 = layout["b_mod"][0]
    dmod_all = g_all[:, mod_row0 : mod_row0 + 9 * n_layers, :].reshape(N_DEV, n_layers, 9 * d)
    dmod_cols = lax.dynamic_slice(dmod_all, (0, 0, chip * ncol), (N_DEV, n_layers, ncol)).transpose(1, 0, 2)
    out["w_mod"] = tuple(_mod_bwd_adam(c_all.T, dmod_cols, w_mod, m_w_mod, v_w_mod, (last.token,)))

    (csum["in00"],) = _split_wait(
        last, [out[st][1] for st in ("w_mod", "w_ffn_out", "ab_w_in", "ab_w_out", "pool_w_grp")], "reduce_last_wait"
    )
    chip_sum("in00")
    _flush("broadcast_last", pair_broadcast("in00"))
    adam_stack("w_ffn_in")

    order = ["norm_g", "w_mod", "b_mod", "w_ffn_in", "w_ffn_out", "ab_w_in", "ab_norm_v", "ab_w_s", "ab_b_s", "ab_conv_w", "ab_w_out", "pool_w_grp", "pool_scale", "final_g"]
    return (loss, grad_x, *[out[nm][0] for nm in order], *[out[nm][1] for nm in order], *[out[nm][2] for nm in order], *[out[nm][3] for nm in order])
```

```python
import functools
import math

import jax
import jax.numpy as jnp
from jax import lax
from jax.experimental import pallas as pl
from jax.experimental.pallas import tpu as pltpu

F32 = jnp.float32
BF16 = jnp.bfloat16
MESH = pl.DeviceIdType.MESH

EPS = 1e-6
ADAM_LR = 0.001
ADAM_B1 = 0.9
ADAM_B2 = 0.999
ADAM_EPS = 1e-08
ADAM_WD = 0.01
ADAM_STEP = 10
POOL_WINDOWS = (2, 4, 8, 16)
POOL_HALO = 16
CONV_HALO = 8
N_CHIPS = 4
N_DEV = 8
VMEM_LIMIT_BYTES = 48 * 1024 * 1024
EW_BLOCK_ELEMS = 1024 * 1024
ADAM_BLOCK_ELEMS = 512 * 1024


def _pick(n, prefs):
    for p in prefs:
        if p <= n and n % p == 0:
            return p
    return n


def _row_tile(rows, cols, block_elems=EW_BLOCK_ELEMS):
    best = None
    for d in range(16, rows + 1, 16):
        if rows % d == 0 and d * cols <= block_elems:
            best = d
    return best or rows


def _dot(a, b):
    return jnp.dot(a, b, preferred_element_type=F32)


def _dot_nt(a, b):
    return lax.dot_general(a, b, (((1,), (1,)), ((), ())), preferred_element_type=F32)


def _dot_tn(a, b):
    return lax.dot_general(a, b, (((0,), (0,)), ((), ())), preferred_element_type=F32)


def _sigmoid(x):
    return 0.5 * jnp.tanh(0.5 * x) + 0.5


_GELU_C = math.sqrt(2.0 / math.pi)


def _gelu(x):
    x2 = x * x
    t = jnp.tanh(_GELU_C * (x + 0.044715 * x2 * x))
    val = 0.5 * x * (1.0 + t)
    grad = 0.5 * (1.0 + t) + 0.5 * x * (1.0 - t * t) * (_GELU_C * (1.0 + 3.0 * 0.044715 * x2))
    return val, grad


def _rstd(x):
    return lax.rsqrt(jnp.mean(x * x, axis=-1, keepdims=True) + EPS)


def _modulate(x, vec_ref):
    return (x * _rstd(x)) * vec_ref[0:1, :] * (1.0 + vec_ref[2:3, :]) + vec_ref[1:2, :]


def _modulate_bwd(x, dh, vec_ref, dvec_ref):
    gn, sh, sc = vec_ref[0:1, :], vec_ref[1:2, :], vec_ref[2:3, :]
    rstd = _rstd(x)
    r = x * rstd
    dvec_ref[0:1, :] += jnp.sum(dh * r * (1.0 + sc), axis=0, keepdims=True)
    dvec_ref[1:2, :] += jnp.sum(dh, axis=0, keepdims=True)
    dvec_ref[2:3, :] += jnp.sum(dh * r * gn, axis=0, keepdims=True)
    gm = gn * (1.0 + sc)
    dr = dh * gm
    dx = rstd * (dr - r * jnp.mean(dr * r, axis=-1, keepdims=True))
    return dx, r * gm + sh


def _adam(w, g, m, v):
    m = ADAM_B1 * m + (1.0 - ADAM_B1) * g
    v = ADAM_B2 * v + (1.0 - ADAM_B2) * (g * g)
    m_hat = m / (1.0 - ADAM_B1**ADAM_STEP)
    v_hat = v / (1.0 - ADAM_B2**ADAM_STEP)
    delta = -ADAM_LR * (m_hat / (jnp.sqrt(v_hat) + ADAM_EPS) + ADAM_WD * w)
    return delta, m, v


_ANY = pl.BlockSpec(memory_space=pl.ANY)


class _Phase:
    def __init__(self, ins, out_shapes, aliases, n_sems, start, finish, then):
        self.ins, self.out_shapes, self.aliases, self.n_sems = list(ins), list(out_shapes), dict(aliases), n_sems
        self.start, self.finish, self.then = start, finish, then


def _call(body, name, grid, in_specs, out_specs, out_shape, ins, scratch=(), prefetch=(), phases=(), in_place=None):
    n_pre, n_in, n_out, n_sc = len(prefetch), len(in_specs), len(out_specs), len(scratch)
    ph_in = [len(p.ins) for p in phases]
    ph_out = [len(p.out_shapes) for p in phases]

    def kernel_body(*refs):
        pos = [0]

        def take(k):
            pos[0] += k
            return refs[pos[0] - k : pos[0]]

        pre, ins_ = take(n_pre), take(n_in)
        p_ins = [take(k) for k in ph_in]
        outs_ = take(n_out)
        p_outs = [take(k) for k in ph_out]
        sc = take(n_sc)
        sems = [take(2) for _ in phases]
        if phases:
            ids = [pl.program_id(a) for a in range(len(grid))]
            first = functools.reduce(jnp.logical_and, [i == 0 for i in ids])
            last = functools.reduce(jnp.logical_and, [i == g - 1 for i, g in zip(ids, grid)])

            @pl.when(first)
            def _():
                for p, pi, po, (send, recv) in zip(phases, p_ins, p_outs, sems):
                    p.start(pi, po, send, recv)

        if body is not None:
            body(*pre, *ins_, *outs_, *sc)
        if phases:

            @pl.when(last)
            def _():
                for p, pi, po, (send, recv) in zip(phases, p_ins, p_outs, sems):
                    p.finish(pi, po, send, recv)

    aliases = {n_pre + i: o for i, o in (in_place or {}).items()}
    i0, o0 = n_pre + n_in, n_out
    for p in phases:
        for i, o in p.aliases.items():
            aliases[i0 + i] = o0 + o
        i0 += len(p.ins)
        o0 += len(p.out_shapes)
    all_in = list(in_specs) + [_ANY] * sum(ph_in)
    all_out = list(out_specs) + [_ANY] * sum(ph_out)
    all_scratch = list(scratch)
    for p in phases:
        all_scratch += [pltpu.SemaphoreType.DMA((p.n_sems,)), pltpu.SemaphoreType.DMA((p.n_sems,))]
    shapes = list(out_shape) + [s for p in phases for s in p.out_shapes]
    operands = list(prefetch) + list(ins) + [a for p in phases for a in p.ins]
    sem = ("arbitrary",) * len(grid)
    params = pltpu.CompilerParams(dimension_semantics=sem, vmem_limit_bytes=VMEM_LIMIT_BYTES)
    if n_pre:
        res = pl.pallas_call(
            kernel_body, name=name, out_shape=shapes, input_output_aliases=aliases, compiler_params=params,
            grid_spec=pltpu.PrefetchScalarGridSpec(
                num_scalar_prefetch=n_pre, grid=grid, in_specs=all_in, out_specs=all_out, scratch_shapes=all_scratch
            ),
        )(*operands)
    else:
        res = pl.pallas_call(
            kernel_body, name=name, grid=grid, in_specs=all_in, out_specs=all_out, out_shape=shapes,
            scratch_shapes=all_scratch, input_output_aliases=aliases, compiler_params=params,
        )(*operands)
    res = list(res)
    outs, rest = res[:n_out], res[n_out:]
    p_res = []
    for k in ph_out:
        p_res.append(rest[:k])
        rest = rest[k:]
    return outs, p_res


def _place():
    return lax.axis_index("x"), lax.axis_index("y"), lax.axis_index("c")


def _other_chips():
    x, y, _ = _place()
    return [(1 - x, y), (x, 1 - y), (1 - x, 1 - y)]


def _flip(k):
    x, y, c = _place()
    return (1 - x if k & 4 else x, 1 - y if k & 2 else y, 1 - c if k & 1 else c)


def _remote(src, dst, send, recv, k, to):
    return pltpu.make_async_remote_copy(
        src_ref=src, dst_ref=dst, send_sem=send.at[k], recv_sem=recv.at[k], device_id=to, device_id_type=MESH
    )


def _phase_small_gather(arrs, then):
    n = len(arrs)

    def copies(ins, outs, send, recv):
        x, y, c = _place()
        me = 4 * x + 2 * y + c
        local = [pltpu.make_async_copy(ins[a], outs[a].at[me], send.at[a * N_DEV]) for a in range(n)]
        remote = [_remote(ins[a], outs[a].at[me], send, recv, a * N_DEV + k, _flip(k)) for a in range(n) for k in range(1, N_DEV)]
        return local, remote

    def start(ins, outs, send, recv):
        local, remote = copies(ins, outs, send, recv)
        for cp in local + remote:
            cp.start()

    def finish(ins, outs, send, recv):
        local, remote = copies(ins, outs, send, recv)
        for cp in remote + local:
            cp.wait()

    shapes = [jax.ShapeDtypeStruct((N_DEV,) + a.shape, a.dtype) for a in arrs]
    return _Phase(arrs, shapes, {}, n * N_DEV, start, finish, then)


def _phase_small_exchange(arr, then):
    def copies(ins, outs, send, recv):
        x, y, c = _place()
        me = 4 * x + 2 * y + c
        local = pltpu.make_async_copy(ins[0].at[me], outs[0].at[me], send.at[0])
        remote = []
        for k in range(1, N_DEV):
            px, py, pc = _flip(k)
            remote.append(_remote(ins[0].at[4 * px + 2 * py + pc], outs[0].at[me], send, recv, k, (px, py, pc)))
        return [local] + remote

    def start(ins, outs, send, recv):
        for cp in copies(ins, outs, send, recv):
            cp.start()

    def finish(ins, outs, send, recv):
        for cp in copies(ins, outs, send, recv):
            cp.wait()

    return _Phase([arr], [jax.ShapeDtypeStruct(arr.shape, arr.dtype)], {}, N_DEV, start, finish, then)


def _after(*arrs):
    nothing = lambda *args: None
    return _Phase(arrs, [], {}, 1, nothing, nothing, nothing)


def _flush(name, *phases):
    _, p_outs = _call(None, name, (1,), [], [], [], [], phases=list(phases))
    for p, po in zip(phases, p_outs):
        p.then(po)


class _Big:
    KINDS = {"full": (True, True), "half": (True, False), "shard": (False, True), "block": (False, False)}

    def __init__(self, f3, s3, h3):
        assert s3 != h3
        self.f3, self.s3, self.h3 = tuple(f3), s3, h3
        self.bd = tuple(f3[a] // (N_CHIPS if a == s3 else 1) // (2 if a == h3 else 1) for a in range(3))
        self.tile = (1, _row_tile(self.bd[1], self.bd[2]), self.bd[2])
        self.grid = tuple(self.bd[a] // self.tile[a] for a in range(3))

    def dims(self, kind):
        chips, halves = self.KINDS[kind]
        return tuple(
            self.bd[a] * (N_CHIPS if chips and a == self.s3 else 1) * (2 if halves and a == self.h3 else 1) for a in range(3)
        )

    def view(self, ref, chip=None, half=None, batch0=0, both_halves=True, part=None):
        start = [batch0, 0, 0]
        size = list(ref.shape)
        size[0] = self.bd[0] * (2 if self.h3 == 0 and both_halves else 1)
        if chip is not None:
            start[self.s3] += chip * self.bd[self.s3]
            size[self.s3] = self.bd[self.s3]
        if half is not None:
            start[self.h3] += half * self.bd[self.h3]
            size[self.h3] = self.bd[self.h3]
        if part is not None:
            size[1] //= 2
            start[1] += part * size[1]
        return ref.at[tuple(pl.ds(st, sz) for st, sz in zip(start, size))]

    def spec(self, chip_from=None, half_from=None, lead=(), batch0=0):
        extra = "grid" in (chip_from, half_from)

        def index(*args):
            pref, idx = args[-1], list(args[int(extra) : -1])
            idx[0] += batch0
            if chip_from:
                idx[self.s3] += (pref[0] if chip_from == "pref" else args[0]) * self.grid[self.s3]
            if half_from:
                idx[self.h3] += (pref[1] if half_from == "pref" else args[0]) * self.grid[self.h3]
            return (0,) * len(lead) + tuple(idx)

        return pl.BlockSpec(tuple(lead) + self.tile, index)


def _same(arrs):
    return [jax.ShapeDtypeStruct(a.shape, a.dtype) for a in arrs]


def _phase_gather_relay(arrs, bigs, second, whole_first, then):
    n = len(arrs)
    per = 4 if second and not whole_first else 2

    def copies(outs, send, recv, arriving):
        x, y, c = _place()
        me, xn, yn, dg = (x, y), (1 - x, y), (x, 1 - y), (1 - x, 1 - y)
        if not second:
            part = (None, None) if whole_first else (0, 1)
            plan = [((xn if arriving else me), part[0], xn), ((yn if arriving else me), part[1], yn)]
        elif whole_first:
            plan = [(dg, 0, yn), (dg, 1, xn)] if arriving else [(xn, 0, yn), (yn, 1, xn)]
        elif arriving:
            plan = [(yn, 0, yn), (dg, 0, yn), (xn, 1, xn), (dg, 1, xn)]
        else:
            plan = [(me, 0, yn), (xn, 0, yn), (me, 1, xn), (yn, 1, xn)]
        res = []
        for a in range(n):
            for k, (chip, part, to) in enumerate(plan):
                blk = bigs[a].view(outs[a], 2 * chip[0] + chip[1], c, part=part)
                res.append(_remote(blk, blk, send, recv, per * a + k, (*to, c)))
        return res

    def start(ins, outs, send, recv):
        for cp in copies(outs, send, recv, False):
            cp.start()

    def finish(ins, outs, send, recv):
        for cp in copies(outs, send, recv, True):
            cp.wait_recv()
        for cp in copies(outs, send, recv, False):
            cp.wait_send()

    return _Phase(arrs, _same(arrs), {a: a for a in range(n)}, per * n, start, finish, then)


def _phase_gather_sibling(arrs, bigs, then):
    n = len(arrs)

    def copies(outs, send, recv, arriving):
        x, y, c = _place()
        return [
            _remote(blk, blk, send, recv, 3 * a + j, (x, y, 1 - c))
            for j, chip in enumerate(_other_chips())
            for a in range(n)
            for blk in [bigs[a].view(outs[a], 2 * chip[0] + chip[1], 1 - c if arriving else c)]
        ]

    def start(ins, outs, send, recv):
        for cp in copies(outs, send, recv, False):
            cp.start()

    def finish(ins, outs, send, recv):
        for cp in copies(outs, send, recv, True):
            cp.wait_recv()
        for cp in copies(outs, send, recv, False):
            cp.wait_send()

    return _Phase(arrs, _same(arrs), {a: a for a in range(n)}, 3 * n, start, finish, then)


def _phase_pair_exchange(grads, bigs, then):
    n = len(grads)

    def copies(ins, outs, send, recv):
        x, y, c = _place()
        srcs = [ins[a] if ins[a].shape == outs[a].shape else bigs[a].view(ins[a], None, 1 - c) for a in range(n)]
        return [_remote(srcs[a], outs[a], send, recv, a, (x, y, 1 - c)) for a in range(n)]

    def start(ins, outs, send, recv):
        for cp in copies(ins, outs, send, recv):
            cp.start()

    def finish(ins, outs, send, recv):
        for cp in copies(ins, outs, send, recv):
            cp.wait()

    shapes = [jax.ShapeDtypeStruct(b.dims("half"), BF16) for b in bigs]
    return _Phase(grads, shapes, {}, n, start, finish, then)


def _phase_chip_exchange(sums, bigs, then):
    n = len(sums)

    def copies(ins, outs, send, recv):
        _, _, c = _place()
        return [
            _remote(bigs[a].view(ins[a], 2 * chip[0] + chip[1], both_halves=False), outs[a].at[j], send, recv, 3 * a + j, (*chip, c))
            for j, chip in enumerate(_other_chips())
            for a in range(n)
        ]

    def start(ins, outs, send, recv):
        for cp in copies(ins, outs, send, recv):
            cp.start()

    def finish(ins, outs, send, recv):
        for cp in copies(ins, outs, send, recv):
            cp.wait()

    shapes = [jax.ShapeDtypeStruct((N_CHIPS - 1,) + b.dims("block"), BF16) for b in bigs]
    return _Phase(sums, shapes, {}, 3 * n, start, finish, then)


_HBM = pl.BlockSpec(memory_space=pltpu.HBM)
_SEM = pl.BlockSpec(memory_space=pltpu.SEMAPHORE)
_DATAFLOW = pltpu.SideEffectType.DATAFLOW_SIDE_EFFECTING


class _InFlight:
    def __init__(self, phase, send, recv, arrays, token):
        self.phase, self.send, self.recv, self.arrays, self.token = phase, send, recv, arrays, token


def _phase_results(phase, refs):
    n_in = len(phase.ins)
    updated = {o: i for i, o in phase.aliases.items()}
    fresh = [o for o in range(len(phase.out_shapes)) if o not in updated]
    return [refs[updated[o]] if o in updated else refs[n_in + fresh.index(o)] for o in range(len(phase.out_shapes))]


def _split_start(phase, name):
    n_in = len(phase.ins)
    fresh = [s for o, s in enumerate(phase.out_shapes) if o not in phase.aliases.values()]
    arrays = list(phase.ins) + [lax.empty(s.shape, s.dtype) for s in fresh]
    n = len(arrays)

    def body(*refs):
        phase.start(refs[:n_in], _phase_results(phase, refs[:n]), refs[n], refs[n + 1])
        refs[-1][...] = jnp.zeros_like(refs[-1])

    operands = [pltpu.with_memory_space_constraint(a, pltpu.HBM) for a in arrays]
    res = pl.pallas_call(
        body, name=name,
        out_shape=[pltpu.SemaphoreType.DMA((phase.n_sems,)), pltpu.SemaphoreType.DMA((phase.n_sems,))]
        + [pltpu.HBM(a.shape, a.dtype) for a in arrays] + [jax.ShapeDtypeStruct((8, 128), F32)],
        in_specs=[_HBM] * n, out_specs=[_SEM, _SEM] + [_HBM] * n + [pl.BlockSpec(memory_space=pltpu.VMEM)],
        input_output_aliases={i: 2 + i for i in range(n)},
        compiler_params=pltpu.CompilerParams(has_side_effects=_DATAFLOW),
    )(*operands)
    return _InFlight(phase, res[0], res[1], list(res[2 : 2 + n]), res[-1])


def _split_wait(flight, after, name):
    phase, n = flight.phase, len(flight.arrays)
    n_in = len(phase.ins)

    def body(*refs):
        phase.finish(refs[:n_in], _phase_results(phase, refs[:n]), refs[n], refs[n + 1])

    res = pl.pallas_call(
        body, name=name, out_shape=[pltpu.HBM(a.shape, a.dtype) for a in flight.arrays],
        in_specs=[_HBM] * n + [_SEM, _SEM] + [_ANY] * len(after), out_specs=[_HBM] * n,
        input_output_aliases={i: i for i in range(n)},
        compiler_params=pltpu.CompilerParams(has_side_effects=_DATAFLOW),
    )(*flight.arrays, flight.send, flight.recv, *after)
    res = list(res)
    phase.then(_phase_results(phase, res))
    return res[:n_in]


def _phase_pair_broadcast(stacks, bigs, batch0s, then):
    n = len(stacks)

    def start(ins, outs, send, recv):
        x, y, c = _place()
        for a in range(n):
            blk = bigs[a].view(outs[a], None, c, batch0s[a])
            _remote(blk, blk, send, recv, a, (x, y, 1 - c)).start()

    def finish(ins, outs, send, recv):
        x, y, c = _place()
        for a in range(n):
            mine = bigs[a].view(outs[a], None, c, batch0s[a])
            theirs = bigs[a].view(outs[a], None, 1 - c, batch0s[a])
            _remote(mine, mine, send, recv, a, (x, y, 1 - c)).wait_send()
            _remote(theirs, theirs, send, recv, a, (x, y, 1 - c)).wait_recv()

    return _Phase(stacks, _same(stacks), {a: a for a in range(n)}, n, start, finish, then)


def _tile_call(body, name, big, where, extra, ins, in_specs, out_specs, out_shape, phases=()):
    grid = ((extra,) if extra else ()) + big.grid
    return _call(body, name, grid, in_specs, out_specs, out_shape, ins, prefetch=(where,), phases=phases)


def _cast_into_full(w_stack, batch0, big, where, name, phases=()):
    def body(_, w_ref, o_ref):
        o_ref[...] = w_ref[...].astype(BF16)

    return _tile_call(
        body, name, big, where, 2, [w_stack], [big.spec(None, "grid", batch0=batch0)], [big.spec("pref", "grid")],
        [jax.ShapeDtypeStruct(big.dims("full"), BF16)], phases,
    )


def _pair_sum(g_full, recv_half, big, where, name, phases=()):
    def body(_, g_ref, r_ref, o_ref):
        o_ref[...] = (g_ref[...].astype(F32) + r_ref[...].astype(F32)).astype(BF16)

    half = big.spec("grid", None)
    return _tile_call(
        body, name, big, where, N_CHIPS, [g_full, recv_half], [big.spec("grid", "pref"), half], [half],
        [jax.ShapeDtypeStruct(big.dims("half"), BF16)], phases,
    )


def _chip_sum(chip_sum, parts, big, where, stack, stack_shape, batch0, name, phases=()):
    def body(_, own_ref, p_ref, *rest):
        acc = own_ref[...].astype(F32)
        for k in range(N_CHIPS - 1):
            acc = acc + p_ref[k].astype(F32)
        rest[-1][...] = acc

    ins = [chip_sum, parts] + ([stack] if stack is not None else [])
    in_specs = [big.spec("pref", None), big.spec(None, None, lead=(N_CHIPS - 1,))] + ([_ANY] if stack is not None else [])
    return _call(
        body, name, big.grid, in_specs, [big.spec(None, "pref", batch0=batch0)], [jax.ShapeDtypeStruct(stack_shape, F32)], ins,
        prefetch=(where,), phases=phases, in_place={2: 0} if stack is not None else None,
    )


def _adam_stack(w, g, m, v, name, after=()):
    b, r, c = w.shape
    tr = _row_tile(r, c, ADAM_BLOCK_ELEMS)

    def body(w_ref, g_ref, m_ref, v_ref, *rest):
        go_ref, d_ref, mo_ref, vo_ref = rest[-4:]
        gv = g_ref[...]
        d, mo, vo = _adam(w_ref[...], gv, m_ref[...], v_ref[...])
        go_ref[...] = gv
        d_ref[...] = d
        mo_ref[...] = mo
        vo_ref[...] = vo

    spec = pl.BlockSpec((1, tr, c), lambda bb, i: (bb, i, 0))
    outs, _ = _call(
        body, name, (b, r // tr), [spec] * 4 + [_ANY] * len(after), [spec] * 4, [jax.ShapeDtypeStruct(w.shape, F32)] * 4,
        [w, g, m, v, *after],
    )
    return outs


def _mod_fwd(c_all, w_mod, b_cols, phases=()):
    n_layers, d, n = w_mod.shape
    tn = _pick(n, (768, 512, 384, 256, 128))

    def body(c_ref, w_ref, b_ref, o_ref):
        cv = c_ref[...]
        ca = (cv * _sigmoid(cv)).astype(BF16)
        o_ref[0] = _dot(ca, w_ref[0].astype(BF16)) + b_ref[0]

    return _call(
        body, "mod_fwd", (n_layers, n // tn),
        [
            pl.BlockSpec((N_DEV, d), lambda l, j: (0, 0)),
            pl.BlockSpec((1, d, tn), lambda l, j: (l, 0, j)),
            pl.BlockSpec((1, 1, tn), lambda l, j: (l, 0, j)),
        ],
        [pl.BlockSpec((1, N_DEV, tn), lambda l, j: (l, 0, j))],
        [jax.ShapeDtypeStruct((n_layers, N_DEV, n), F32)], [c_all, w_mod, b_cols], phases=phases,
    )


def _mod_bwd_adam(c_all_t, dmod_cols, w, m, v, after=()):
    n_layers, d, n = w.shape
    tn = _pick(n, (384, 256, 128))

    def body(c_ref, dm_ref, w_ref, m_ref, v_ref, *rest):
        g_ref, d_ref, mo_ref, vo_ref = rest[-4:]
        cv = c_ref[...]
        ca = (cv * _sigmoid(cv)).astype(BF16)
        g = _dot(ca, dm_ref[0].astype(BF16))
        g_ref[0] = g
        dl, mo, vo = _adam(w_ref[0], g, m_ref[0], v_ref[0])
        d_ref[0] = dl
        mo_ref[0] = mo
        vo_ref[0] = vo

    wspec = pl.BlockSpec((1, d, tn), lambda l, j: (l, 0, j))
    outs, _ = _call(
        body, "mod_bwd_adam", (n_layers, n // tn),
        [pl.BlockSpec((d, N_DEV), lambda l, j: (0, 0)), pl.BlockSpec((1, N_DEV, tn), lambda l, j: (l, 0, j)), wspec, wspec, wspec]
        + [_ANY] * len(after),
        [wspec] * 4, [jax.ShapeDtypeStruct(w.shape, F32)] * 4, [c_all_t, dmod_cols, w, m, v, *after],
    )
    return outs


def _ffn_fwd(x, vec, w_in, w_out, name, phases=()):
    s, d = x.shape
    f = w_out.shape[1]
    tm = _pick(s, (1024, 512, 256, 128))
    tf = _pick(f, (256, 128))
    nf = f // tf

    def body(x_ref, vec_ref, wg_ref, wu_ref, wo_ref, xo_ref, g_ref, u_ref, y_ref, h_sc, acc_sc):
        j = pl.program_id(1)

        @pl.when(j == 0)
        def _():
            h_sc[...] = _modulate(x_ref[...], vec_ref).astype(BF16)
            acc_sc[...] = jnp.zeros_like(acc_sc)

        h = h_sc[...]
        g = _dot(h, wg_ref[0])
        u = _dot(h, wu_ref[0])
        g_ref[...] = g.astype(BF16)
        u_ref[...] = u.astype(BF16)
        a = (g * _sigmoid(g) * u).astype(BF16)
        acc_sc[...] += _dot(a, wo_ref[0])

        @pl.when(j == nf - 1)
        def _():
            yv = acc_sc[...]
            xo_ref[...] = x_ref[...] + 0.5 * vec_ref[3:4, :] * yv
            y_ref[...] = yv.astype(BF16)

    row = pl.BlockSpec((tm, d), lambda i, j: (i, 0))
    hid = pl.BlockSpec((tm, tf), lambda i, j: (i, j))
    return _call(
        body, name, (s // tm, nf),
        [
            row,
            pl.BlockSpec((8, d), lambda i, j: (0, 0)),
            pl.BlockSpec((1, d, tf), lambda i, j: (0, 0, j)),
            pl.BlockSpec((1, d, tf), lambda i, j: (0, 0, nf + j)),
            pl.BlockSpec((1, tf, d), lambda i, j: (0, j, 0)),
        ],
        [row, hid, hid, row],
        [
            jax.ShapeDtypeStruct((s, d), F32),
            jax.ShapeDtypeStruct((s, f), BF16),
            jax.ShapeDtypeStruct((s, f), BF16),
            jax.ShapeDtypeStruct((s, d), BF16),
        ],
        [x, vec, w_in, w_in, w_out],
        scratch=[pltpu.VMEM((tm, d), BF16), pltpu.VMEM((tm, d), F32)], phases=phases,
    )


def _ffn_bwd(dxo, x, vec, gg, uu, y, w_in, w_out, name, phases=()):
    s, d = x.shape
    f = w_out.shape[1]
    tm = _pick(s, (512, 256, 128))
    tf = _pick(f, (256, 128))
    nf = f // tf

    def body(dxo_ref, x_ref, vec_ref, g_ref, u_ref, y_ref, wg_ref, wu_ref, wo_ref,
             dx_ref, dg_ref, du_ref, a_ref, h_ref, dy_ref, dvec_ref, acc_sc):
        i, j = pl.program_id(0), pl.program_id(1)

        @pl.when((i == 0) & (j == 0))
        def _():
            dvec_ref[...] = jnp.zeros_like(dvec_ref)

        @pl.when(j == 0)
        def _():
            dxo_v = dxo_ref[...]
            dy_ref[...] = (0.5 * vec_ref[3:4, :] * dxo_v).astype(BF16)
            dvec_ref[3:4, :] += 0.5 * jnp.sum(dxo_v * y_ref[...].astype(F32), axis=0, keepdims=True)
            acc_sc[...] = jnp.zeros_like(acc_sc)

        da = _dot_nt(dy_ref[...], wo_ref[0])
        g = g_ref[...].astype(F32)
        u = u_ref[...].astype(F32)
        sig = _sigmoid(g)
        sl = g * sig
        a_ref[...] = (sl * u).astype(BF16)
        dg = (da * u * (sig * (1.0 + g * (1.0 - sig)))).astype(BF16)
        du = (da * sl).astype(BF16)
        dg_ref[...] = dg
        du_ref[...] = du
        acc_sc[...] += _dot_nt(dg, wg_ref[0]) + _dot_nt(du, wu_ref[0])

        @pl.when(j == nf - 1)
        def _():
            dx, h = _modulate_bwd(x_ref[...], acc_sc[...], vec_ref, dvec_ref)
            dx_ref[...] = dxo_ref[...] + dx
            h_ref[...] = h.astype(BF16)

    row = pl.BlockSpec((tm, d), lambda i, j: (i, 0))
    hid = pl.BlockSpec((tm, tf), lambda i, j: (i, j))
    vecs = pl.BlockSpec((8, d), lambda i, j: (0, 0))
    return _call(
        body, name, (s // tm, nf),
        [
            row, row, vecs, hid, hid, row,
            pl.BlockSpec((1, d, tf), lambda i, j: (0, 0, j)),
            pl.BlockSpec((1, d, tf), lambda i, j: (0, 0, nf + j)),
            pl.BlockSpec((1, tf, d), lambda i, j: (0, j, 0)),
        ],
        [row, hid, hid, hid, row, row, vecs],
        [
            jax.ShapeDtypeStruct((s, d), F32),
            jax.ShapeDtypeStruct((s, f), BF16),
            jax.ShapeDtypeStruct((s, f), BF16),
            jax.ShapeDtypeStruct((s, f), BF16),
            jax.ShapeDtypeStruct((s, d), BF16),
            jax.ShapeDtypeStruct((s, d), BF16),
            jax.ShapeDtypeStruct((8, d), F32),
        ],
        [dxo, x, vec, gg, uu, y, w_in, w_in, w_out],
        scratch=[pltpu.VMEM((tm, d), F32)], phases=phases,
    )


def _grad_half(a, bs, big, where, mine, recv, name, phases=()):
    s, k1 = a.shape
    n = bs[0].shape[1]
    groups = len(bs)
    rows_halved = big.h3 == 1
    assert rows_halved or groups == 1
    kk, nn = (k1 // 2, n) if rows_halved else (k1, n // 2)
    tk = _pick(kk, (1408, 1024, 512, 256, 128))
    tn = _pick(nn, (1408, 1024, 640, 512, 256, 128))
    nkb, nnb = kk // tk, nn // tn
    assert (recv is None) == (not mine)

    def half(pref):
        return pref[1] if mine else 1 - pref[1]

    def body(_, a_ref, *rest):
        q = pl.program_id(1)
        for p in range(groups):

            @pl.when(q == p)
            def _(p=p):
                acc = _dot_tn(a_ref[...], rest[p][...])
                if recv is not None:
                    acc = acc + rest[groups][0].astype(F32)
                rest[-1][0] = acc.astype(BF16)

    def b_block(p):
        def index(i, q, j, pref):
            jj = jnp.where(q == p, j, jnp.where(q < p, 0, nnb - 1))
            return (0, jj + (0 if rows_halved else half(pref) * nnb))

        return pl.BlockSpec((s, tn), index)

    out_spec = pl.BlockSpec((1, tk, tn), lambda i, q, j, pref: (0, i, q * nnb + j))
    in_specs = [pl.BlockSpec((s, tk), lambda i, q, j, pref: (0, i + (half(pref) * nkb if rows_halved else 0)))]
    in_specs += [b_block(p) for p in range(groups)]
    ins = [a, *bs]
    if recv is not None:
        in_specs.append(out_spec)
        ins.append(recv)
    return _call(
        body, name, (nkb, groups, nnb), in_specs, [out_spec], [jax.ShapeDtypeStruct(big.dims("half"), BF16)], ins,
        prefetch=(where,), phases=phases,
    )


def _proj_mod_fwd(x, vec, w, phases=()):
    s, d = x.shape
    n = w.shape[2]
    tm = _pick(s, (1024, 512, 256, 128))
    tn = _pick(n, (640, 512, 256, 128))

    def body(x_ref, vec_ref, w_ref, o_ref, h_sc):
        @pl.when(pl.program_id(1) == 0)
        def _():
            h_sc[...] = _modulate(x_ref[...], vec_ref).astype(BF16)

        o_ref[...] = _dot(h_sc[...], w_ref[0])

    return _call(
        body, "ab_in_fwd", (s // tm, n // tn),
        [
            pl.BlockSpec((tm, d), lambda i, j: (i, 0)),
            pl.BlockSpec((8, d), lambda i, j: (0, 0)),
            pl.BlockSpec((1, d, tn), lambda i, j: (0, 0, j)),
        ],
        [pl.BlockSpec((tm, tn), lambda i, j: (i, j))],
        [jax.ShapeDtypeStruct((s, n), F32)], [x, vec, w],
        scratch=[pltpu.VMEM((tm, d), BF16)], phases=phases,
    )


def _proj_res_fwd(a, w, x, vec, phases=()):
    s, kd = a.shape
    d = x.shape[1]
    tm = _pick(s, (1024, 512, 256, 128))

    def body(a_ref, w_ref, x_ref, vec_ref, xo_ref, y_ref):
        yv = _dot(a_ref[...], w_ref[0])
        xo_ref[...] = x_ref[...] + vec_ref[3:4, :] * yv
        y_ref[...] = yv.astype(BF16)

    row = pl.BlockSpec((tm, d), lambda i: (i, 0))
    return _call(
        body, "ab_out_fwd", (s // tm,),
        [pl.BlockSpec((tm, kd), lambda i: (i, 0)), pl.BlockSpec((1, kd, d), lambda i: (0, 0, 0)), row, pl.BlockSpec((8, d), lambda i: (0, 0))],
        [row, row],
        [jax.ShapeDtypeStruct((s, d), F32), jax.ShapeDtypeStruct((s, d), BF16)], [a, w, x, vec], phases=phases,
    )


def _proj_res_bwd(dxo, y, vec, w, phases=()):
    s, d = dxo.shape
    kd = w.shape[1]
    tm = _pick(s, (1024, 512, 256, 128))

    def body(dxo_ref, y_ref, vec_ref, w_ref, dy_ref, da_ref, dgate_ref):
        @pl.when(pl.program_id(0) == 0)
        def _():
            dgate_ref[...] = jnp.zeros_like(dgate_ref)

        dxo_v = dxo_ref[...]
        dy = (vec_ref[3:4, :] * dxo_v).astype(BF16)
        dy_ref[...] = dy
        dgate_ref[3:4, :] += jnp.sum(dxo_v * y_ref[...].astype(F32), axis=0, keepdims=True)
        da_ref[...] = _dot_nt(dy, w_ref[0]).astype(BF16)

    row = pl.BlockSpec((tm, d), lambda i: (i, 0))
    vecs = pl.BlockSpec((8, d), lambda i: (0, 0))
    return _call(
        body, "ab_out_bwd", (s // tm,),
        [row, row, vecs, pl.BlockSpec((1, kd, d), lambda i: (0, 0, 0))],
        [row, pl.BlockSpec((tm, kd), lambda i: (i, 0)), vecs],
        [jax.ShapeDtypeStruct((s, d), BF16), jax.ShapeDtypeStruct((s, kd), BF16), jax.ShapeDtypeStruct((8, d), F32)],
        [dxo, y, vec, w], phases=phases,
    )


def _proj_mod_bwd(dproj, w, x, vec, dxo, dvec_in, name, phases=()):
    parts, s, n_part = dproj.shape
    d = x.shape[1]
    tm = _pick(s, (512, 256, 128))
    tk = _pick(n_part, (1408, 1280, 1024, 512, 256, 128))
    per_part = n_part // tk
    nk = parts * per_part

    def body(dp_ref, w_ref, x_ref, vec_ref, dxo_ref, dvi_ref, dx_ref, h_ref, dvec_ref, acc_sc):
        i, k = pl.program_id(0), pl.program_id(1)

        @pl.when((i == 0) & (k == 0))
        def _():
            dvec_ref[...] = dvi_ref[...]

        @pl.when(k == 0)
        def _():
            acc_sc[...] = jnp.zeros_like(acc_sc)

        acc_sc[...] += _dot_nt(dp_ref[0], w_ref[0])

        @pl.when(k == nk - 1)
        def _():
            dx, h = _modulate_bwd(x_ref[...], acc_sc[...], vec_ref, dvec_ref)
            dx_ref[...] = dxo_ref[...] + dx
            h_ref[...] = h.astype(BF16)

    row = pl.BlockSpec((tm, d), lambda i, k: (i, 0))
    vecs = pl.BlockSpec((8, d), lambda i, k: (0, 0))
    return _call(
        body, name, (s // tm, nk),
        [
            pl.BlockSpec((1, tm, tk), lambda i, k: (k // per_part, i, k % per_part)),
            pl.BlockSpec((1, d, tk), lambda i, k: (0, 0, k)),
            row, vecs, row, vecs,
        ],
        [row, row, vecs],
        [jax.ShapeDtypeStruct((s, d), F32), jax.ShapeDtypeStruct((s, d), BF16), jax.ShapeDtypeStruct((8, d), F32)],
        [dproj, w, x, vec, dxo, dvec_in], scratch=[pltpu.VMEM((tm, d), F32)], phases=phases,
    )


def _tril(n):
    return lax.broadcasted_iota(jnp.int32, (n, n), 0) >= lax.broadcasted_iota(jnp.int32, (n, n), 1)


def _layernorm_stats(gv):
    mu = jnp.mean(gv, axis=-1, keepdims=True)
    cen = gv - mu
    rstd = lax.rsqrt(jnp.mean(cen * cen, axis=-1, keepdims=True) + EPS)
    return cen * rstd, rstd


def _shift_down(q, k, above_ref, c_cg, c_xb, first):
    width = q.shape[1]
    rows = lax.broadcasted_iota(jnp.int32, q.shape, 0)
    out = pltpu.roll(q, k, 0)
    for r in range(k):
        src = CONV_HALO - k + r
        above = above_ref[src : src + 1, c_cg : c_cg + width] * above_ref[src : src + 1, c_xb : c_xb + width]
        above = jnp.where(first, 0.0, above)
        out = jnp.where(rows == r, above, out)
    return out


def _ab_mix_fwd(proj, norm_v, w_s, b_rows, conv_w, phases=()):
    s, n = proj.shape
    heads, chunk, _ = w_s.shape
    da = norm_v.shape[1]
    hd = da // heads
    db = conv_w.shape[1]
    tm = _pick(s, (512, 256, 128))

    def body(p_ref, ph_ref, nv_ref, ws_ref, b_ref, cw_ref, o_ref):
        first = pl.program_id(0) == 0
        gu, _ = _gelu(p_ref[:, 0:da])
        gv, _ = _gelu(p_ref[:, da : 2 * da])
        xhat, _ = _layernorm_stats(gv)
        vn = (xhat * nv_ref[...]).astype(BF16)
        mask = _tril(chunk)
        for hh in range(heads):
            wm = jnp.where(mask, ws_ref[hh], 0.0).astype(BF16)
            cols = slice(hh * hd, (hh + 1) * hd)
            for nn in range(tm // chunk):
                rows = slice(nn * chunk, (nn + 1) * chunk)
                z = _dot(wm, vn[rows, cols]) + b_ref[:, cols]
                o_ref[rows, cols] = (gu[rows, cols] * z).astype(BF16)
        c_cg, c_xb = 2 * da + db, 2 * da + 2 * db
        bg = p_ref[:, 2 * da : 2 * da + db]
        q = p_ref[:, c_cg : c_cg + db] * p_ref[:, c_xb : c_xb + db]
        q1 = _shift_down(q, 1, ph_ref, c_cg, c_xb, first)
        q2 = _shift_down(q, 2, ph_ref, c_cg, c_xb, first)
        conv = cw_ref[0:1, :] * q2 + cw_ref[1:2, :] * q1 + cw_ref[2:3, :] * q
        o_ref[:, da : da + db] = (bg * conv).astype(BF16)

    nh = tm // CONV_HALO
    return _call(
        body, "ab_mix_fwd", (s // tm,),
        [
            pl.BlockSpec((tm, n), lambda i: (i, 0)),
            pl.BlockSpec((CONV_HALO, n), lambda i: (jnp.maximum(i * nh - 1, 0), 0)),
            pl.BlockSpec((1, da), lambda i: (0, 0)),
            pl.BlockSpec((heads, chunk, chunk), lambda i: (0, 0, 0)),
            pl.BlockSpec((chunk, da), lambda i: (0, 0)),
            pl.BlockSpec((3, db), lambda i: (0, 0)),
        ],
        [pl.BlockSpec((tm, da + db), lambda i: (i, 0))],
        [jax.ShapeDtypeStruct((s, da + db), BF16)], [proj, proj, norm_v, w_s, b_rows, conv_w], phases=phases,
    )


def _ab_mix_bwd(proj, dcat, norm_v, w_s, b_rows, conv_w, phases=()):
    s, n = proj.shape
    heads, chunk, _ = w_s.shape
    da = norm_v.shape[1]
    hd = da // heads
    db = conv_w.shape[1]
    tm = _pick(s, (512, 256, 128))
    nblk = s // tm
    dhalo = 2 * CONV_HALO

    def body(p_ref, pa_ref, pb_ref, dc_ref, dcb_ref, nv_ref, ws_ref, b_ref, cw_ref,
             dp_ref, dnv_ref, dws_ref, dzs_ref, dcw_ref, dvn_sc):
        i = pl.program_id(0)
        first, last = i == 0, i == nblk - 1

        @pl.when(first)
        def _():
            dnv_ref[...] = jnp.zeros_like(dnv_ref)
            dws_ref[...] = jnp.zeros_like(dws_ref)
            dzs_ref[...] = jnp.zeros_like(dzs_ref)
            dcw_ref[...] = jnp.zeros_like(dcw_ref)

        uu = p_ref[:, 0:da]
        gu, gu_grad = _gelu(uu)
        gv, gv_grad = _gelu(p_ref[:, da : 2 * da])
        xhat, rstd = _layernorm_stats(gv)
        nv = nv_ref[...]
        vn = (xhat * nv).astype(BF16)
        dya = dc_ref[:, 0:da].astype(F32)
        dz = (dya * gu).astype(BF16)
        mask = _tril(chunk)
        for hh in range(heads):
            wm = jnp.where(mask, ws_ref[hh], 0.0).astype(BF16)
            cols = slice(hh * hd, (hh + 1) * hd)
            dws = jnp.zeros((chunk, chunk), F32)
            for nn in range(tm // chunk):
                rows = slice(nn * chunk, (nn + 1) * chunk)
                z = _dot(wm, vn[rows, cols]) + b_ref[:, cols]
                dp_ref[rows, cols] = (dya[rows, cols] * z * gu_grad[rows, cols]).astype(BF16)
                dz_blk = dz[rows, cols]
                dws = dws + _dot_nt(dz_blk, vn[rows, cols])
                dzs_ref[:, cols] += dz_blk.astype(F32)
                dvn = _dot_tn(wm, dz_blk)
                dnv_ref[:, cols] += jnp.sum(dvn * xhat[rows, cols], axis=0, keepdims=True)
                dvn_sc[rows, cols] = dvn
            dws_ref[hh] += jnp.where(mask, dws, 0.0)
        dxhat = dvn_sc[...] * nv
        dgv = rstd * (dxhat - jnp.mean(dxhat, axis=-1, keepdims=True) - xhat * jnp.mean(dxhat * xhat, axis=-1, keepdims=True))
        dp_ref[:, da : 2 * da] = (dgv * gv_grad).astype(BF16)

        c_bg, c_cg, c_xb = 2 * da, 2 * da + db, 2 * da + 2 * db
        bg = p_ref[:, c_bg : c_bg + db]
        cg = p_ref[:, c_cg : c_cg + db]
        xb = p_ref[:, c_xb : c_xb + db]
        q = cg * xb
        q1 = _shift_down(q, 1, pa_ref, c_cg, c_xb, first)
        q2 = _shift_down(q, 2, pa_ref, c_cg, c_xb, first)
        dyb = dc_ref[:, da : da + db].astype(F32)
        conv = cw_ref[0:1, :] * q2 + cw_ref[1:2, :] * q1 + cw_ref[2:3, :] * q
        dp_ref[:, c_bg : c_bg + db] = (dyb * conv).astype(BF16)
        e = dyb * bg
        dcw_ref[0:1, :] += jnp.sum(e * q2, axis=0, keepdims=True)
        dcw_ref[1:2, :] += jnp.sum(e * q1, axis=0, keepdims=True)
        dcw_ref[2:3, :] += jnp.sum(e * q, axis=0, keepdims=True)
        rows = lax.broadcasted_iota(jnp.int32, e.shape, 0)
        dq = cw_ref[2:3, :] * e
        for kk in (1, 2):
            ek = pltpu.roll(e, tm - kk, 0)
            for r in range(kk):
                below = dcb_ref[r : r + 1, da : da + db].astype(F32) * pb_ref[r : r + 1, c_bg : c_bg + db]
                below = jnp.where(last, 0.0, below)
                ek = jnp.where(rows == tm - kk + r, below, ek)
            dq = dq + cw_ref[2 - kk : 3 - kk, :] * ek
        dp_ref[:, c_cg : c_cg + db] = (dq * xb).astype(BF16)
        dp_ref[:, c_xb : c_xb + db] = (dq * cg).astype(BF16)

    nh = tm // CONV_HALO
    nhb = tm // dhalo
    const2 = lambda i: (0, 0)
    return _call(
        body, "ab_mix_bwd", (nblk,),
        [
            pl.BlockSpec((tm, n), lambda i: (i, 0)),
            pl.BlockSpec((CONV_HALO, n), lambda i: (jnp.maximum(i * nh - 1, 0), 0)),
            pl.BlockSpec((CONV_HALO, n), lambda i: (jnp.minimum((i + 1) * nh, s // CONV_HALO - 1), 0)),
            pl.BlockSpec((tm, da + db), lambda i: (i, 0)),
            pl.BlockSpec((dhalo, da + db), lambda i: (jnp.minimum((i + 1) * nhb, s // dhalo - 1), 0)),
            pl.BlockSpec((1, da), const2),
            pl.BlockSpec((heads, chunk, chunk), lambda i: (0, 0, 0)),
            pl.BlockSpec((chunk, da), const2),
            pl.BlockSpec((3, db), const2),
        ],
        [
            pl.BlockSpec((tm, n), lambda i: (i, 0)),
            pl.BlockSpec((1, da), const2),
            pl.BlockSpec((heads, chunk, chunk), lambda i: (0, 0, 0)),
            pl.BlockSpec((chunk, da), const2),
            pl.BlockSpec((3, db), const2),
        ],
        [
            jax.ShapeDtypeStruct((s, n), BF16),
            jax.ShapeDtypeStruct((1, da), F32),
            jax.ShapeDtypeStruct((heads, chunk, chunk), F32),
            jax.ShapeDtypeStruct((chunk, da), F32),
            jax.ShapeDtypeStruct((3, db), F32),
        ],
        [proj, proj, proj, dcat, dcat, norm_v, w_s, b_rows, conv_w],
        scratch=[pltpu.VMEM((tm, da), F32)], phases=phases,
    )


def _pool_counts(tm, i, w):
    t = i * tm + lax.broadcasted_iota(jnp.int32, (tm, 1), 0)
    return jnp.minimum(t + 1, w).astype(F32)


def _pool_fwd(x, vec, w_grp, scale, phases=()):
    s, d = x.shape
    groups, gd, _ = w_grp.shape
    tm = _pick(s, (512, 256, 128))

    def body(x_ref, xa_ref, vec_ref, w_ref, sc_ref, xo_ref, p_ref, o_ref):
        i = pl.program_id(0)
        h = _modulate(x_ref[...], vec_ref)
        ha = jnp.where(i == 0, 0.0, _modulate(xa_ref[...], vec_ref))
        ext = jnp.concatenate([ha, h], axis=0)
        for gi, w in enumerate(POOL_WINDOWS):
            cols = slice(gi * gd, (gi + 1) * gd)
            acc = ext[:, cols]
            step = 1
            while step < w:
                acc = acc + pltpu.roll(acc, step, 0)
                step *= 2
            p = (acc[POOL_HALO:, :] / _pool_counts(tm, i, w) - h[:, cols]).astype(BF16)
            p_ref[:, cols] = p
            o_ref[:, cols] = _dot(p, w_ref[gi]).astype(BF16)
        xo_ref[...] = x_ref[...] + vec_ref[3:4, :] * (o_ref[...].astype(F32) * sc_ref[...])

    nh = tm // POOL_HALO
    row = pl.BlockSpec((tm, d), lambda i: (i, 0))
    return _call(
        body, "pool_fwd", (s // tm,),
        [
            row,
            pl.BlockSpec((POOL_HALO, d), lambda i: (jnp.maximum(i * nh - 1, 0), 0)),
            pl.BlockSpec((8, d), lambda i: (0, 0)),
            pl.BlockSpec((groups, gd, gd), lambda i: (0, 0, 0)),
            pl.BlockSpec((1, d), lambda i: (0, 0)),
        ],
        [row, row, row],
        [jax.ShapeDtypeStruct((s, d), F32), jax.ShapeDtypeStruct((s, d), BF16), jax.ShapeDtypeStruct((s, d), BF16)],
        [x, x, vec, w_grp, scale], phases=phases,
    )


def _pool_bwd(dxo, x, vec, p, o, w_grp, scale, phases=()):
    s, d = x.shape
    groups, gd, _ = w_grp.shape
    tm = _pick(s, (512, 256, 128))
    nblk = s // tm

    def body(dxo_ref, dxb_ref, x_ref, vec_ref, p_ref, o_ref, w_ref, sc_ref, dx_ref, dw_ref, dsc_ref, dvec_ref, dw_sc):
        i = pl.program_id(0)

        @pl.when(i == 0)
        def _():
            dw_sc[...] = jnp.zeros_like(dw_sc)
            dsc_ref[...] = jnp.zeros_like(dsc_ref)
            dvec_ref[...] = jnp.zeros_like(dvec_ref)

        gate, sc = vec_ref[3:4, :], sc_ref[...]
        dxo_v = dxo_ref[...]
        ov = o_ref[...].astype(F32)
        dvec_ref[3:4, :] += jnp.sum(dxo_v * (ov * sc), axis=0, keepdims=True)
        dy = gate * dxo_v
        dsc_ref[...] += jnp.sum(dy * ov, axis=0, keepdims=True)
        dout = (dy * sc).astype(BF16)
        dout_b = jnp.where(i == nblk - 1, 0.0, gate * dxb_ref[...] * sc).astype(BF16)
        for gi, w in enumerate(POOL_WINDOWS):
            cols = slice(gi * gd, (gi + 1) * gd)
            dw_sc[gi] += _dot_tn(p_ref[:, cols], dout[:, cols])
            wb = w_ref[gi]
            dp = _dot_nt(dout[:, cols], wb)
            dp_b = _dot_nt(dout_b[:, cols], wb)
            e = dp / _pool_counts(tm, i, w)
            t_below = (i + 1) * tm + lax.broadcasted_iota(jnp.int32, (POOL_HALO, 1), 0)
            e_b = dp_b / jnp.minimum(t_below + 1, w).astype(F32)
            acc = jnp.concatenate([e, e_b], axis=0)
            step = 1
            while step < w:
                acc = acc + pltpu.roll(acc, tm + POOL_HALO - step, 0)
                step *= 2
            dx_ref[:, cols] = acc[:tm, :] - dp
        dx, _ = _modulate_bwd(x_ref[...], dx_ref[...], vec_ref, dvec_ref)
        dx_ref[...] = dxo_v + dx

        @pl.when(i == nblk - 1)
        def _():
            dw_ref[...] = dw_sc[...].astype(BF16)

    nh = tm // POOL_HALO
    row = pl.BlockSpec((tm, d), lambda i: (i, 0))
    vecs = pl.BlockSpec((8, d), lambda i: (0, 0))
    wspec = pl.BlockSpec((groups, gd, gd), lambda i: (0, 0, 0))
    return _call(
        body, "pool_bwd", (nblk,),
        [
            row,
            pl.BlockSpec((POOL_HALO, d), lambda i: (jnp.minimum((i + 1) * nh, s // POOL_HALO - 1), 0)),
            row, vecs, row, row, wspec,
            pl.BlockSpec((1, d), lambda i: (0, 0)),
        ],
        [row, wspec, pl.BlockSpec((1, d), lambda i: (0, 0)), vecs],
        [
            jax.ShapeDtypeStruct((s, d), F32),
            jax.ShapeDtypeStruct((groups, gd, gd), BF16),
            jax.ShapeDtypeStruct((1, d), F32),
            jax.ShapeDtypeStruct((8, d), F32),
        ],
        [dxo, dxo, x, vec, p, o, w_grp, scale],
        scratch=[pltpu.VMEM((groups, gd, gd), F32)], phases=phases,
    )


def _loss_head(x, gain, target, phases=()):
    s, d = x.shape
    tm = _pick(s, (512, 256, 128))

    def body(x_ref, g_ref, t_ref, dx_ref, aux_ref):
        @pl.when(pl.program_id(0) == 0)
        def _():
            aux_ref[...] = jnp.zeros_like(aux_ref)

        xv = x_ref[...]
        rstd = _rstd(xv)
        r = xv * rstd
        gain_v = g_ref[...]
        err = r * gain_v - t_ref[...]
        aux_ref[1:2, :] += jnp.sum(err * err, axis=0, keepdims=True)
        dout = err * (1.0 / d)
        aux_ref[0:1, :] += jnp.sum(dout * r, axis=0, keepdims=True)
        dr = dout * gain_v
        dx_ref[...] = rstd * (dr - r * jnp.mean(dr * r, axis=-1, keepdims=True))

    row = pl.BlockSpec((tm, d), lambda i: (i, 0))
    return _call(
        body, "loss_head", (s // tm,),
        [row, pl.BlockSpec((1, d), lambda i: (0, 0)), row],
        [row, pl.BlockSpec((8, d), lambda i: (0, 0))],
        [jax.ShapeDtypeStruct((s, d), F32), jax.ShapeDtypeStruct((8, d), F32)], [x, gain, target], phases=phases,
    )


def _small_adam(gathered, gathered_ws, layout, smalls, chip):
    names = list(smalls)
    n = len(names)
    loss_row, _, _, n_feat = layout["loss"]

    def body(*refs):
        chip_ref, g_ref, gws_ref = refs[0], refs[1], refs[2]
        wmv = refs[3 : 3 + 3 * n]
        outs = refs[3 + 3 * n : 3 + 7 * n]
        total = refs[-1]
        total[...] = g_ref[0]
        for kdev in range(1, N_DEV):
            total[...] += g_ref[kdev]
        total_ws = gws_ref[0]
        for kdev in range(1, N_DEV):
            total_ws = total_ws + gws_ref[kdev]
        my_chip = chip_ref[0]
        for a, name in enumerate(names):
            w_ref, m_ref, v_ref = wmv[3 * a : 3 * a + 3]
            if name == "ab_w_s":
                g = total_ws
            else:
                row0, rows, col0, cols = layout[name]
                if col0 is None:
                    g = jnp.zeros((rows, cols), F32)
                    for j in range(N_CHIPS):
                        g = g + jnp.where(my_chip == j, total[row0 : row0 + rows, j * cols : (j + 1) * cols], 0.0)
                else:
                    g = total[row0 : row0 + rows, col0 : col0 + cols]
            dl, mo, vo = _adam(w_ref[...], g, m_ref[...], v_ref[...])
            outs[4 * a][...] = g
            outs[4 * a + 1][...] = dl
            outs[4 * a + 2][...] = mo
            outs[4 * a + 3][...] = vo
        refs[3 + 7 * n][...] = 0.5 * jnp.sum(total[loss_row : loss_row + 1, 0:n_feat], axis=1, keepdims=True) / n_feat

    ins = [gathered, gathered_ws]
    out_shapes = []
    for name in names:
        ins.extend(smalls[name])
        out_shapes.extend([jax.ShapeDtypeStruct(smalls[name][0].shape, F32)] * 4)
    out_shapes.append(jax.ShapeDtypeStruct((1, 1), F32))
    whole = lambda shape: pl.BlockSpec(shape, functools.partial(lambda nd, i, c: (0,) * nd, len(shape)))
    res = pl.pallas_call(
        body, name="small_adam",
        grid_spec=pltpu.PrefetchScalarGridSpec(
            num_scalar_prefetch=1, grid=(1,),
            in_specs=[whole(a.shape) for a in ins], out_specs=[whole(o.shape) for o in out_shapes],
            scratch_shapes=[pltpu.VMEM(gathered.shape[1:], F32)],
        ),
        out_shape=out_shapes,
        compiler_params=pltpu.CompilerParams(dimension_semantics=("arbitrary",), vmem_limit_bytes=VMEM_LIMIT_BYTES),
    )(chip.reshape(1).astype(jnp.int32), *ins)
    return {name: res[4 * a : 4 * a + 4] for a, name in enumerate(names)}, res[4 * n]


def _pad_rows(a, rows=8):
    extra = (-a.shape[0]) % rows
    return jnp.pad(a, ((0, extra), (0, 0))) if extra else a


def _pad_cols(a, cols):
    return jnp.pad(a, ((0, 0), (0, cols - a.shape[1]))) if a.shape[1] < cols else a


def _run(fn, *phases):
    outs, p_outs = fn(list(phases))
    for p, po in zip(phases, p_outs):
        p.then(po)
    return outs


def kernel(x, c, norm_g, w_mod, b_mod, w_ffn_in, w_ffn_out, ab_w_in, ab_norm_v, ab_w_s, ab_b_s, ab_conv_w, ab_w_out, pool_w_grp, pool_scale, final_g, loss_target, m_norm_g, m_w_mod, m_b_mod, m_w_ffn_in, m_w_ffn_out, m_ab_w_in, m_ab_norm_v, m_ab_w_s, m_ab_b_s, m_ab_conv_w, m_ab_w_out, m_pool_w_grp, m_pool_scale, m_final_g, v_norm_g, v_w_mod, v_b_mod, v_w_ffn_in, v_w_ffn_out, v_ab_w_in, v_ab_norm_v, v_ab_w_s, v_ab_b_s, v_ab_conv_w, v_ab_w_out, v_pool_w_grp, v_pool_scale, v_final_g):
    ix, iy, ic = _place()
    chip = 2 * ix + iy
    me = 4 * ix + 2 * iy + ic
    where = jnp.stack([chip, ic]).astype(jnp.int32)
    s, d = x.shape[1], x.shape[2]
    x0 = x.reshape(s, d)
    target = loss_target.reshape(s, d)
    n_layers = norm_g.shape[0]
    dq = d // N_CHIPS
    heads, chunk = ab_w_s.shape[1], ab_w_s.shape[2]
    da = ab_norm_v.shape[1]
    db = ab_conv_w.shape[2] * N_CHIPS
    f_hidden = w_ffn_out.shape[2] * N_CHIPS
    assert n_layers == 2 and da % heads == 0

    cw_pad = _pad_cols(ab_conv_w.reshape(3, db // N_CHIPS), dq)
    packed = jnp.concatenate(
        [_pad_rows(c.reshape(N_CHIPS, dq)), _pad_rows(norm_g.reshape(-1, dq)), _pad_rows(pool_scale.reshape(1, dq)), _pad_rows(cw_pad)],
        axis=0,
    )
    ncol = w_mod.shape[2]
    b_cols = lax.dynamic_slice(b_mod, (0, chip * ncol), (n_layers, ncol)).reshape(n_layers, 1, ncol)
    small = {}

    def small_gather(key, arrs):
        def then(outs):
            small[key] = outs

        return _phase_small_gather(arrs, then)

    stacks = {
        "w_ffn_in": tuple(a.reshape((-1,) + a.shape[2:]) for a in (w_ffn_in, m_w_ffn_in, v_w_ffn_in)),
        "w_ffn_out": tuple(a.reshape((-1,) + a.shape[2:]) for a in (w_ffn_out, m_w_ffn_out, v_w_ffn_out)),
        "ab_w_in": (ab_w_in, m_ab_w_in, v_ab_w_in),
        "ab_w_out": (ab_w_out, m_ab_w_out, v_ab_w_out),
        "pool_w_grp": (pool_w_grp[0], m_pool_w_grp[0], v_pool_w_grp[0]),
    }
    big_in = _Big((1, d, 2 * f_hidden), 2, 1)
    big_out = _Big((1, f_hidden, d), 1, 2)
    units = {}
    for l in range(n_layers):
        for k in range(2):
            units[f"in{l}{k}"] = (big_in, "w_ffn_in", 2 * l + k)
            units[f"out{l}{k}"] = (big_out, "w_ffn_out", 2 * l + k)
    units["abin"] = (_Big((1, d, ab_w_in.shape[2] * N_CHIPS), 2, 1), "ab_w_in", 0)
    units["about"] = (_Big((1, ab_w_out.shape[1] * N_CHIPS, d), 1, 2), "ab_w_out", 0)
    units["pool"] = (_Big((pool_w_grp.shape[1], pool_w_grp.shape[2] * N_CHIPS, pool_w_grp.shape[3]), 1, 0), "pool_w_grp", 0)
    big = {u: g for u, (g, _, _) in units.items()}

    weight = {}
    complete = set()

    def cast(u):
        g, st, b0 = units[u]

        def launch(phases):
            (weight[u],), p_outs = _cast_into_full(stacks[st][0], b0, g, where, "cast_" + u, phases)
            return None, p_outs

        return launch

    def gather_relay(us, second, whole_first):
        def then(outs):
            for u, o in zip(us, outs):
                weight[u] = o

        return _phase_gather_relay([weight[u] for u in us], [big[u] for u in us], second, whole_first, then)

    def gather_sibling(*us):
        def then(outs):
            for u, o in zip(us, outs):
                weight[u] = o
                complete.add(u)

        return _phase_gather_sibling([weight[u] for u in us], [big[u] for u in us], then)

    def w_of(u):
        assert u in complete, u
        return weight[u]

    _run(cast("in00"), small_gather("inputs", [packed]))
    small_all = small["inputs"][0]
    by_chip = small_all[0::2]
    c_all = small_all[:, 0:N_CHIPS, :].reshape(N_DEV, d)
    norm_full = by_chip[:, 8 : 8 + 3 * n_layers, :].transpose(1, 0, 2).reshape(3 * n_layers, d)
    pool_scale_full = by_chip[:, 16:17, :].transpose(1, 0, 2).reshape(1, d)
    conv_full = by_chip[:, 24:27, : db // N_CHIPS].transpose(1, 0, 2).reshape(3, db)
    pieces = [("in00", "out00"), ("abin", "about"), ("in01", "out01"), ("in10", "out10", "pool"), ("in11", "out11")]
    in_flight = {}

    def start_gather(p):
        in_flight[p, 0] = _split_start(gather_relay(pieces[p], False, p == 0), f"gather_{p}_start")

    def relay_gather(p, after=()):
        flight = in_flight.pop((p, 0))
        _split_wait(flight, list(after) + list(started().ins), f"gather_{p}_arrived")
        in_flight[p, 1] = _split_start(gather_relay(pieces[p], True, p == 0), f"gather_{p}_relay")

    def started():
        return _after(*[flight.token for flight in in_flight.values()])

    def finish_gather(p, after, meanwhile=None):
        flight = in_flight.pop((p, 1))
        _split_wait(flight, list(after) + list(started().ins), f"gather_{p}_wait")
        crossing = _split_start(gather_sibling(*pieces[p]), f"gather_{p}_forward")
        behind = [crossing.token]
        if p + 1 < len(pieces):
            relay_gather(p + 1)
        if p + 3 < len(pieces):
            start_gather(p + 3)
        behind = behind + list(started().ins)
        if meanwhile is not None:
            behind = behind + meanwhile(_after(crossing.token))
        _split_wait(crossing, behind, f"gather_{p}_forwarded")

    _run(cast("out00"))
    start_gather(0)
    mod_cols = _run(lambda phases: _mod_fwd(c_all, w_mod, b_cols, phases), started())[0]

    def mod_rows(outs):
        small["mod"] = outs

    _run(cast("about"), started())
    early = [u for piece in pieces[2:4] for u in piece]
    for u in early:
        _run(cast(u), started())
    _run(
        cast("abin"), _phase_small_exchange(mod_cols.transpose(1, 0, 2), mod_rows),
        started(), _after(*[weight[u] for u in early]),
    )
    relay_gather(0)
    start_gather(1)
    start_gather(2)
    for u in pieces[4]:
        _run(cast(u), started())
    mod_mine = small["mod"][0][0::2]
    mod = mod_mine.transpose(1, 0, 2).reshape(n_layers, 3, 3, d)
    vecs = {
        (l, sub): jnp.pad(norm_full[3 * l + sub][None], ((0, 7), (0, 0))) + jnp.pad(mod[l, sub], ((1, 4), (0, 0)))
        for l in range(n_layers)
        for sub in range(3)
    }
    b_rows = jnp.broadcast_to(ab_b_s[0].T[:, :, None], (chunk, heads, da // heads)).reshape(chunk, da)

    saved = {}

    def ffn_forward(xs, l, sub, k, *phases):
        saved[l, sub, "x"] = xs
        xs, gg, uu, yb = _run(
            lambda ph: _ffn_fwd(xs, vecs[l, sub], w_of(f"in{l}{k}"), w_of(f"out{l}{k}"), f"ffn_fwd_{l}{k}", ph), *phases
        )
        saved[l, sub, "act"] = (gg, uu, yb)
        return xs

    finish_gather(0, [vecs[0, 0]] + [weight[u] for u in pieces[4]])
    xs = ffn_forward(x0, 0, 0, 0, started())
    saved[0, 1, "x"] = xs
    finish_gather(1, [xs])
    (proj,) = _run(lambda ph: _proj_mod_fwd(xs, vecs[0, 1], w_of("abin"), ph), started())
    (cat,) = _run(lambda ph: _ab_mix_fwd(proj, ab_norm_v, ab_w_s[0], b_rows, conv_full, ph))
    xs, yb = _run(lambda ph: _proj_res_fwd(cat, w_of("about"), xs, vecs[0, 1], ph))
    saved[0, 1, "act"] = (proj, cat, yb)
    finish_gather(2, [xs])
    xs = ffn_forward(xs, 0, 2, 1, started())
    finish_gather(3, [xs])
    xs = ffn_forward(xs, 1, 0, 0, started())
    saved[1, 1, "x"] = xs
    pooled = []

    def pool_forward(behind):
        pooled.extend(_run(lambda ph: _pool_fwd(xs, vecs[1, 1], w_of("pool"), pool_scale_full, ph), behind))
        return [pooled[0]]

    finish_gather(4, [xs], pool_forward)
    xs, pp, oo = pooled
    saved[1, 1, "act"] = (pp, oo)
    xs = ffn_forward(xs, 1, 2, 1)
    dxs, aux = _run(lambda ph: _loss_head(xs, final_g.reshape(1, d), target, ph))

    grad = {}
    recv = {}
    csum = {}
    parts = {}
    reduced = {}
    done = set()
    dvecs, small_g = {}, {}

    def pair_exchange(*us):
        def then(outs):
            for u, o in zip(us, outs):
                recv[u] = o

        return _phase_pair_exchange([grad[u] for u in us], [big[u] for u in us], then)

    def grad_half(u, a, bs, mine, name, *phases):
        (res,) = _run(lambda ph: _grad_half(a, bs, big[u], where, mine, recv[u] if mine else None, name, ph), *phases)
        return res

    def pair_sum(u, *phases):
        def launch(ph):
            (csum[u],), p_outs = _pair_sum(grad[u], recv[u], big[u], where, "pair_sum_" + u, ph)
            return None, p_outs

        _run(launch, *phases)

    def chip_exchange(*us):
        def then(outs):
            for u, o in zip(us, outs):
                parts[u] = o

        return _phase_chip_exchange([csum[u] for u in us], [big[u] for u in us], then)

    def chip_sum(*us, carried=()):
        for n_u, u in enumerate(us):
            g, st, b0 = units[u]

            def launch(ph):
                (reduced[st],), p_outs = _chip_sum(
                    csum[u], parts[u], g, where, reduced.get(st), stacks[st][0].shape, b0, "chip_sum_" + u, ph
                )
                return None, p_outs

            _run(launch, *(carried if n_u == 0 else ()))

    def pair_broadcast(*us):
        sts = [units[u][1] for u in us]
        assert len(set(sts)) == len(sts)

        def then(outs):
            for u, st, o in zip(us, sts, outs):
                reduced[st] = o
                done.add(u)

        return _phase_pair_broadcast([reduced[st] for st in sts], [big[u] for u in us], [units[u][2] for u in us], then)

    def ffn_backward(dxs, l, sub, k, carried_bwd, carried_send, carried_mine):
        gg, uu, yb = saved[l, sub, "act"]
        w_in, w_out = w_of(f"in{l}{k}"), w_of(f"out{l}{k}")
        uo, ui, tag = f"out{l}{k}", f"in{l}{k}", f"{l}{k}"
        dxs, dg, du, a, h, dy, dvecs[l, sub] = _run(
            lambda ph: _ffn_bwd(dxs, saved[l, sub, "x"], vecs[l, sub], gg, uu, yb, w_in, w_out, "ffn_bwd_" + tag, ph), *carried_bwd()
        )
        grad[uo] = grad_half(uo, a, [dy], False, "dw_out_send_" + tag, *carried_send())
        grad[ui] = grad_half(ui, h, [dg, du], False, "dw_in_send_" + tag, pair_exchange(uo))
        csum[uo] = grad_half(uo, a, [dy], True, "dw_out_" + tag, pair_exchange(ui))
        csum[ui] = grad_half(ui, h, [dg, du], True, "dw_in_" + tag, *carried_mine())
        return dxs

    none = lambda: ()
    dxs = ffn_backward(dxs, 1, 2, 1, none, none, none)
    pp, oo = saved[1, 1, "act"]
    dxs, grad["pool"], small_g["pool_scale"], dvecs[1, 1] = _run(
        lambda ph: _pool_bwd(dxs, saved[1, 1, "x"], vecs[1, 1], pp, oo, w_of("pool"), pool_scale_full, ph)
    )

    def after_11():
        return (chip_exchange("in11", "out11"), pair_exchange("pool"))

    def bcast_11():
        chip_sum("in11", "out11")
        pair_sum("pool")
        return (pair_broadcast("in11", "out11"), chip_exchange("pool"))

    dxs = ffn_backward(dxs, 1, 0, 0, after_11, bcast_11, none)

    def after_10():
        return (chip_exchange("in10", "out10"),)

    def bcast_10():
        chip_sum("in10", "out10", "pool")
        return (pair_broadcast("in10", "out10", "pool"),)

    dxs = ffn_backward(dxs, 0, 2, 1, after_10, bcast_10, none)

    proj, cat, yb = saved[0, 1, "act"]
    out01 = _split_start(chip_exchange("out01"), "reduce_out01_start")
    dy, dcat, dgate = _run(lambda ph: _proj_res_bwd(dxs, yb, vecs[0, 1], w_of("about"), ph), _after(out01.token))
    grad["about"] = grad_half("about", cat, [dy], False, "dw_ab_out_send")
    dproj, small_g["ab_norm_v"], small_g["ab_w_s"], dzs, small_g["ab_conv_w"] = _run(
        lambda ph: _ab_mix_bwd(proj, dcat, ab_norm_v, ab_w_s[0], b_rows, conv_full, ph), pair_exchange("about")
    )
    small_g["ab_b_s"] = dzs.reshape(chunk, heads, da // heads).sum(axis=2).T
    dxs, h, dvecs[0, 1] = _run(
        lambda ph: _proj_mod_bwd(dproj[None], w_of("abin"), saved[0, 1, "x"], vecs[0, 1], dxs, dgate, "ab_in_bwd", ph)
    )
    grad["abin"] = grad_half("abin", h, [dproj], False, "dw_ab_in_send")
    (csum["out01"],) = _split_wait(out01, [grad["abin"]], "reduce_out01_wait")
    chip_sum("out01", carried=(pair_exchange("abin"),))
    csum["about"] = grad_half("about", cat, [dy], True, "dw_ab_out", pair_broadcast("out01"))
    csum["abin"] = grad_half("abin", h, [dproj], True, "dw_ab_in")

    layout = {}
    tail = {}

    def after_01():
        tail["01"] = _split_start(chip_exchange("in01", "abin", "about"), "reduce_01_start")
        return (_after(tail["01"].token),)

    def pack_small_grads():
        dvec_all = jnp.stack([dvecs[l, sub] for l in range(n_layers) for sub in range(3)])
        dgain = dvec_all[:, 0, :]
        dmod = dvec_all[:, 1:4, :].reshape(3 * 3 * n_layers, d)
        rows = {
            "norm_g": (dgain, None, dq), "final_g": (aux[0:1], 0, d), "pool_scale": (small_g["pool_scale"], None, dq),
            "b_mod": (dmod, 0, d), "ab_norm_v": (small_g["ab_norm_v"], 0, da),
            "ab_conv_w": (small_g["ab_conv_w"], None, db // N_CHIPS), "ab_b_s": (small_g["ab_b_s"], 0, chunk),
            "loss": (aux[1:2], 0, d),
        }
        row0 = 0
        for nm, (pc, col0, cols) in rows.items():
            layout[nm] = (row0, pc.shape[0], col0, cols)
            row0 += pc.shape[0]
        packed_rows = -(-row0 // 8) * 8
        return sum(
            jnp.pad(pc, ((layout[nm][0], packed_rows - layout[nm][0] - pc.shape[0]), (0, d - pc.shape[1])))
            for nm, (pc, _, _) in rows.items()
        )

    def bcast_01():
        csum["in01"], csum["abin"], csum["about"] = _split_wait(tail["01"], [dvecs[0, 0]], "reduce_01_wait")
        chip_sum("in01", "abin", "about")
        grads_small = [pack_small_grads(), small_g["ab_w_s"].reshape(heads * chunk, chunk)]
        tail["small"] = _split_start(small_gather("grads", grads_small), "gather_small_grads_start")
        return (pair_broadcast("in01", "abin", "about"), _after(tail["small"].token))

    def reduce_out00():
        tail["out00"] = _split_start(chip_exchange("out00"), "reduce_out00_start")
        return (_after(tail["out00"].token),)

    dxs = ffn_backward(dxs, 0, 0, 0, after_01, bcast_01, reduce_out00)
    grad_x = dxs.reshape(x.shape)

    last = _split_start(chip_exchange("in00"), "reduce_last_start")
    _split_wait(tail["small"], [last.token], "gather_small_grads_wait")
    g_all, gws_all = small["grads"]

    out = {}

    def adam_stack(st, after=()):
        w3, m3, v3 = stacks[st]
        assert all(u in done for u, (_, ust, _) in units.items() if ust == st), st
        shape = {"w_ffn_in": w_ffn_in.shape, "w_ffn_out": w_ffn_out.shape, "pool_w_grp": pool_w_grp.shape}.get(st, w3.shape)
        out[st] = tuple(a.reshape(shape) for a in _adam_stack(w3, reduced[st], m3, v3, "adam_" + st, after))

    shapes2d = {
        "norm_g": (3 * n_layers, dq), "b_mod": (9 * n_layers, d), "final_g": (1, d), "ab_norm_v": (1, da),
        "pool_scale": (1, dq), "ab_conv_w": (3, db // N_CHIPS), "ab_b_s": (heads, chunk), "ab_w_s": (heads * chunk, chunk),
    }
    small_w = {"norm_g": (norm_g, m_norm_g, v_norm_g), "b_mod": (b_mod, m_b_mod, v_b_mod), "final_g": (final_g, m_final_g, v_final_g),
               "ab_norm_v": (ab_norm_v, m_ab_norm_v, v_ab_norm_v), "pool_scale": (pool_scale, m_pool_scale, v_pool_scale),
               "ab_conv_w": (ab_conv_w, m_ab_conv_w, v_ab_conv_w), "ab_b_s": (ab_b_s, m_ab_b_s, v_ab_b_s), "ab_w_s": (ab_w_s, m_ab_w_s, v_ab_w_s)}
    smalls = {nm: tuple(a.reshape(shapes2d[nm]) for a in wmv) for nm, wmv in small_w.items()}
    small_out, loss = _small_adam(g_all, gws_all, layout, smalls, chip)
    loss = loss.reshape(())
    for nm, res in small_out.items():
        out[nm] = tuple(a.reshape(small_w[nm][0].shape) for a in res)

    mod_row0 = layout["b_mod"][0]
    dmod_all = g_all[:, mod_row0 : mod_row0 + 9 * n_layers, :].reshape(N_DEV, n_layers, 9 * d)
    dmod_cols = lax.dynamic_slice(dmod_all, (0, 0, chip * ncol), (N_DEV, n_layers, ncol)).transpose(1, 0, 2)
    out["w_mod"] = tuple(_mod_bwd_adam(c_all.T, dmod_cols, w_mod, m_w_mod, v_w_mod, (last.token,)))

    (csum["out00"],) = _split_wait(tail["out00"], [out["w_mod"][1]], "reduce_out00_wait")
    chip_sum("out00")
    crossing = _split_start(pair_broadcast("out00"), "broadcast_out00_start")
    for st in ("ab_w_in", "ab_w_out", "pool_w_grp"):
        adam_stack(st, (crossing.token,))
    _split_wait(crossing, [out[st][1] for st in ("ab_w_in", "ab_w_out", "pool_w_grp")], "broadcast_out00_wait")
    (csum["in00"],) = _split_wait(last, [reduced["w_ffn_out"]], "reduce_last_wait")
    chip_sum("in00")
    crossing = _split_start(pair_broadcast("in00"), "broadcast_last_start")
    adam_stack("w_ffn_out", (crossing.token,))
    _split_wait(crossing, [out["w_ffn_out"][1]], "broadcast_last_wait")
    adam_stack("w_ffn_in")

    order = ["norm_g", "w_mod", "b_mod", "w_ffn_in", "w_ffn_out", "ab_w_in", "ab_norm_v", "ab_w_s", "ab_b_s", "ab_conv_w", "ab_w_out", "pool_w_grp", "pool_scale", "final_g"]
    return (loss, grad_x, *[out[nm][0] for nm in order], *[out[nm][1] for nm in order], *[out[nm][2] for nm in order], *[out[nm][3] for nm in order])
```

```python
import functools
import math

import jax
import jax.numpy as jnp
from jax import lax
from jax.experimental import pallas as pl
from jax.experimental.pallas import tpu as pltpu

F32 = jnp.float32
BF16 = jnp.bfloat16
MESH = pl.DeviceIdType.MESH

EPS = 1e-6
ADAM_LR = 0.001
ADAM_B1 = 0.9
ADAM_B2 = 0.999
ADAM_EPS = 1e-08
ADAM_WD = 0.01
ADAM_STEP = 10
POOL_WINDOWS = (2, 4, 8, 16)
POOL_HALO = 16
CONV_HALO = 8
N_CHIPS = 4
N_DEV = 8
VMEM_LIMIT_BYTES = 48 * 1024 * 1024
EW_BLOCK_ELEMS = 1024 * 1024
ADAM_BLOCK_ELEMS = 512 * 1024


def _pick(n, prefs):
    for p in prefs:
        if p <= n and n % p == 0:
            return p
    return n


def _row_tile(rows, cols, block_elems=EW_BLOCK_ELEMS):
    best = None
    for d in range(16, rows + 1, 16):
        if rows % d == 0 and d * cols <= block_elems:
            best = d
    return best or rows


def _dot(a, b):
    return jnp.dot(a, b, preferred_element_type=F32)


def _dot_nt(a, b):
    return lax.dot_general(a, b, (((1,), (1,)), ((), ())), preferred_element_type=F32)


def _dot_tn(a, b):
    return lax.dot_general(a, b, (((0,), (0,)), ((), ())), preferred_element_type=F32)


def _sigmoid(x):
    return 0.5 * jnp.tanh(0.5 * x) + 0.5


_GELU_C = math.sqrt(2.0 / math.pi)


def _gelu(x):
    x2 = x * x
    t = jnp.tanh(_GELU_C * (x + 0.044715 * x2 * x))
    val = 0.5 * x * (1.0 + t)
    grad = 0.5 * (1.0 + t) + 0.5 * x * (1.0 - t * t) * (_GELU_C * (1.0 + 3.0 * 0.044715 * x2))
    return val, grad


def _rstd(x):
    return lax.rsqrt(jnp.mean(x * x, axis=-1, keepdims=True) + EPS)


def _modulate(x, vec_ref):
    return (x * _rstd(x)) * vec_ref[0:1, :] * (1.0 + vec_ref[2:3, :]) + vec_ref[1:2, :]


def _modulate_bwd(x, dh, vec_ref, dvec_ref):
    gn, sh, sc = vec_ref[0:1, :], vec_ref[1:2, :], vec_ref[2:3, :]
    rstd = _rstd(x)
    r = x * rstd
    dvec_ref[0:1, :] += jnp.sum(dh * r * (1.0 + sc), axis=0, keepdims=True)
    dvec_ref[1:2, :] += jnp.sum(dh, axis=0, keepdims=True)
    dvec_ref[2:3, :] += jnp.sum(dh * r * gn, axis=0, keepdims=True)
    gm = gn * (1.0 + sc)
    dr = dh * gm
    dx = rstd * (dr - r * jnp.mean(dr * r, axis=-1, keepdims=True))
    return dx, r * gm + sh


def _adam(w, g, m, v):
    m = ADAM_B1 * m + (1.0 - ADAM_B1) * g
    v = ADAM_B2 * v + (1.0 - ADAM_B2) * (g * g)
    m_hat = m / (1.0 - ADAM_B1**ADAM_STEP)
    v_hat = v / (1.0 - ADAM_B2**ADAM_STEP)
    delta = -ADAM_LR * (m_hat / (jnp.sqrt(v_hat) + ADAM_EPS) + ADAM_WD * w)
    return delta, m, v


_ANY = pl.BlockSpec(memory_space=pl.ANY)


class _Phase:
    def __init__(self, ins, out_shapes, aliases, n_sems, start, finish, then):
        self.ins, self.out_shapes, self.aliases, self.n_sems = list(ins), list(out_shapes), dict(aliases), n_sems
        self.start, self.finish, self.then = start, finish, then


def _call(body, name, grid, in_specs, out_specs, out_shape, ins, scratch=(), prefetch=(), phases=(), in_place=None):
    n_pre, n_in, n_out, n_sc = len(prefetch), len(in_specs), len(out_specs), len(scratch)
    ph_in = [len(p.ins) for p in phases]
    ph_out = [len(p.out_shapes) for p in phases]

    def kernel_body(*refs):
        pos = [0]

        def take(k):
            pos[0] += k
            return refs[pos[0] - k : pos[0]]

        pre, ins_ = take(n_pre), take(n_in)
        p_ins = [take(k) for k in ph_in]
        outs_ = take(n_out)
        p_outs = [take(k) for k in ph_out]
        sc = take(n_sc)
        sems = [take(2) for _ in phases]
        if phases:
            ids = [pl.program_id(a) for a in range(len(grid))]
            first = functools.reduce(jnp.logical_and, [i == 0 for i in ids])
            last = functools.reduce(jnp.logical_and, [i == g - 1 for i, g in zip(ids, grid)])

            @pl.when(first)
            def _():
                for p, pi, po, (send, recv) in zip(phases, p_ins, p_outs, sems):
                    p.start(pi, po, send, recv)

        if body is not None:
            body(*pre, *ins_, *outs_, *sc)
        if phases:

            @pl.when(last)
            def _():
                for p, pi, po, (send, recv) in zip(phases, p_ins, p_outs, sems):
                    p.finish(pi, po, send, recv)

    aliases = {n_pre + i: o for i, o in (in_place or {}).items()}
    i0, o0 = n_pre + n_in, n_out
    for p in phases:
        for i, o in p.aliases.items():
            aliases[i0 + i] = o0 + o
        i0 += len(p.ins)
        o0 += len(p.out_shapes)
    all_in = list(in_specs) + [_ANY] * sum(ph_in)
    all_out = list(out_specs) + [_ANY] * sum(ph_out)
    all_scratch = list(scratch)
    for p in phases:
        all_scratch += [pltpu.SemaphoreType.DMA((p.n_sems,)), pltpu.SemaphoreType.DMA((p.n_sems,))]
    shapes = list(out_shape) + [s for p in phases for s in p.out_shapes]
    operands = list(prefetch) + list(ins) + [a for p in phases for a in p.ins]
    sem = ("arbitrary",) * len(grid)
    params = pltpu.CompilerParams(dimension_semantics=sem, vmem_limit_bytes=VMEM_LIMIT_BYTES)
    if n_pre:
        res = pl.pallas_call(
            kernel_body, name=name, out_shape=shapes, input_output_aliases=aliases, compiler_params=params,
            grid_spec=pltpu.PrefetchScalarGridSpec(
                num_scalar_prefetch=n_pre, grid=grid, in_specs=all_in, out_specs=all_out, scratch_shapes=all_scratch
            ),
        )(*operands)
    else:
        res = pl.pallas_call(
            kernel_body, name=name, grid=grid, in_specs=all_in, out_specs=all_out, out_shape=shapes,
            scratch_shapes=all_scratch, input_output_aliases=aliases, compiler_params=params,
        )(*operands)
    res = list(res)
    outs, rest = res[:n_out], res[n_out:]
    p_res = []
    for k in ph_out:
        p_res.append(rest[:k])
        rest = rest[k:]
    return outs, p_res


def _place():
    return lax.axis_index("x"), lax.axis_index("y"), lax.axis_index("c")


def _other_chips():
    x, y, _ = _place()
    return [(1 - x, y), (x, 1 - y), (1 - x, 1 - y)]


def _flip(k):
    x, y, c = _place()
    return (1 - x if k & 4 else x, 1 - y if k & 2 else y, 1 - c if k & 1 else c)


def _remote(src, dst, send, recv, k, to):
    return pltpu.make_async_remote_copy(
        src_ref=src, dst_ref=dst, send_sem=send.at[k], recv_sem=recv.at[k], device_id=to, device_id_type=MESH
    )


def _phase_small_gather(arrs, then):
    n = len(arrs)

    def copies(ins, outs, send, recv):
        x, y, c = _place()
        me = 4 * x + 2 * y + c
        local = [pltpu.make_async_copy(ins[a], outs[a].at[me], send.at[a * N_DEV]) for a in range(n)]
        remote = [_remote(ins[a], outs[a].at[me], send, recv, a * N_DEV + k, _flip(k)) for a in range(n) for k in range(1, N_DEV)]
        return local, remote

    def start(ins, outs, send, recv):
        local, remote = copies(ins, outs, send, recv)
        for cp in local + remote:
            cp.start()

    def finish(ins, outs, send, recv):
        local, remote = copies(ins, outs, send, recv)
        for cp in remote + local:
            cp.wait()

    shapes = [jax.ShapeDtypeStruct((N_DEV,) + a.shape, a.dtype) for a in arrs]
    return _Phase(arrs, shapes, {}, n * N_DEV, start, finish, then)


def _phase_small_exchange(arr, then):
    def copies(ins, outs, send, recv):
        x, y, c = _place()
        me = 4 * x + 2 * y + c
        local = pltpu.make_async_copy(ins[0].at[me], outs[0].at[me], send.at[0])
        remote = []
        for k in range(1, N_DEV):
            px, py, pc = _flip(k)
            remote.append(_remote(ins[0].at[4 * px + 2 * py + pc], outs[0].at[me], send, recv, k, (px, py, pc)))
        return [local] + remote

    def start(ins, outs, send, recv):
        for cp in copies(ins, outs, send, recv):
            cp.start()

    def finish(ins, outs, send, recv):
        for cp in copies(ins, outs, send, recv):
            cp.wait()

    return _Phase([arr], [jax.ShapeDtypeStruct(arr.shape, arr.dtype)], {}, N_DEV, start, finish, then)


def _after(*arrs):
    nothing = lambda *args: None
    return _Phase(arrs, [], {}, 1, nothing, nothing, nothing)


def _flush(name, *phases):
    _, p_outs = _call(None, name, (1,), [], [], [], [], phases=list(phases))
    for p, po in zip(phases, p_outs):
        p.then(po)


class _Big:
    KINDS = {"full": (True, True), "half": (True, False), "shard": (False, True), "block": (False, False)}

    def __init__(self, f3, s3, h3):
        assert s3 != h3
        self.f3, self.s3, self.h3 = tuple(f3), s3, h3
        self.bd = tuple(f3[a] // (N_CHIPS if a == s3 else 1) // (2 if a == h3 else 1) for a in range(3))
        self.tile = (1, _row_tile(self.bd[1], self.bd[2]), self.bd[2])
        self.grid = tuple(self.bd[a] // self.tile[a] for a in range(3))

    def dims(self, kind):
        chips, halves = self.KINDS[kind]
        return tuple(
            self.bd[a] * (N_CHIPS if chips and a == self.s3 else 1) * (2 if halves and a == self.h3 else 1) for a in range(3)
        )

    def view(self, ref, chip=None, half=None, batch0=0, both_halves=True, part=None):
        start = [batch0, 0, 0]
        size = list(ref.shape)
        size[0] = self.bd[0] * (2 if self.h3 == 0 and both_halves else 1)
        if chip is not None:
            start[self.s3] += chip * self.bd[self.s3]
            size[self.s3] = self.bd[self.s3]
        if half is not None:
            start[self.h3] += half * self.bd[self.h3]
            size[self.h3] = self.bd[self.h3]
        if part is not None:
            size[1] //= 2
            start[1] += part * size[1]
        return ref.at[tuple(pl.ds(st, sz) for st, sz in zip(start, size))]

    def spec(self, chip_from=None, half_from=None, lead=(), batch0=0):
        extra = "grid" in (chip_from, half_from)

        def index(*args):
            pref, idx = args[-1], list(args[int(extra) : -1])
            idx[0] += batch0
            if chip_from:
                idx[self.s3] += (pref[0] if chip_from == "pref" else args[0]) * self.grid[self.s3]
            if half_from:
                idx[self.h3] += (pref[1] if half_from == "pref" else args[0]) * self.grid[self.h3]
            return (0,) * len(lead) + tuple(idx)

        return pl.BlockSpec(tuple(lead) + self.tile, index)


def _same(arrs):
    return [jax.ShapeDtypeStruct(a.shape, a.dtype) for a in arrs]


def _phase_gather_relay(arrs, bigs, second, whole_first, then):
    n = len(arrs)
    per = 4 if second and not whole_first else 2

    def copies(outs, send, recv, arriving):
        x, y, c = _place()
        me, xn, yn, dg = (x, y), (1 - x, y), (x, 1 - y), (1 - x, 1 - y)
        if not second:
            part = (None, None) if whole_first else (0, 1)
            plan = [((xn if arriving else me), part[0], xn), ((yn if arriving else me), part[1], yn)]
        elif whole_first:
            plan = [(dg, 0, yn), (dg, 1, xn)] if arriving else [(xn, 0, yn), (yn, 1, xn)]
        elif arriving:
            plan = [(yn, 0, yn), (dg, 0, yn), (xn, 1, xn), (dg, 1, xn)]
        else:
            plan = [(me, 0, yn), (xn, 0, yn), (me, 1, xn), (yn, 1, xn)]
        res = []
        for a in range(n):
            for k, (chip, part, to) in enumerate(plan):
                blk = bigs[a].view(outs[a], 2 * chip[0] + chip[1], c, part=part)
                res.append(_remote(blk, blk, send, recv, per * a + k, (*to, c)))
        return res

    def start(ins, outs, send, recv):
        for cp in copies(outs, send, recv, False):
            cp.start()

    def finish(ins, outs, send, recv):
        for cp in copies(outs, send, recv, True):
            cp.wait_recv()
        for cp in copies(outs, send, recv, False):
            cp.wait_send()

    return _Phase(arrs, _same(arrs), {a: a for a in range(n)}, per * n, start, finish, then)


def _phase_gather_sibling(arrs, bigs, then):
    n = len(arrs)

    def copies(outs, send, recv, arriving):
        x, y, c = _place()
        return [
            _remote(blk, blk, send, recv, 3 * a + j, (x, y, 1 - c))
            for j, chip in enumerate(_other_chips())
            for a in range(n)
            for blk in [bigs[a].view(outs[a], 2 * chip[0] + chip[1], 1 - c if arriving else c)]
        ]

    def start(ins, outs, send, recv):
        for cp in copies(outs, send, recv, False):
            cp.start()

    def finish(ins, outs, send, recv):
        for cp in copies(outs, send, recv, True):
            cp.wait_recv()
        for cp in copies(outs, send, recv, False):
            cp.wait_send()

    return _Phase(arrs, _same(arrs), {a: a for a in range(n)}, 3 * n, start, finish, then)


def _phase_pair_exchange(grads, bigs, then):
    n = len(grads)

    def copies(ins, outs, send, recv):
        x, y, c = _place()
        srcs = [ins[a] if ins[a].shape == outs[a].shape else bigs[a].view(ins[a], None, 1 - c) for a in range(n)]
        return [_remote(srcs[a], outs[a], send, recv, a, (x, y, 1 - c)) for a in range(n)]

    def start(ins, outs, send, recv):
        for cp in copies(ins, outs, send, recv):
            cp.start()

    def finish(ins, outs, send, recv):
        for cp in copies(ins, outs, send, recv):
            cp.wait()

    shapes = [jax.ShapeDtypeStruct(b.dims("half"), BF16) for b in bigs]
    return _Phase(grads, shapes, {}, n, start, finish, then)


def _phase_chip_exchange(sums, bigs, then):
    n = len(sums)

    def copies(ins, outs, send, recv):
        _, _, c = _place()
        return [
            _remote(bigs[a].view(ins[a], 2 * chip[0] + chip[1], both_halves=False), outs[a].at[j], send, recv, 3 * a + j, (*chip, c))
            for j, chip in enumerate(_other_chips())
            for a in range(n)
        ]

    def start(ins, outs, send, recv):
        for cp in copies(ins, outs, send, recv):
            cp.start()

    def finish(ins, outs, send, recv):
        for cp in copies(ins, outs, send, recv):
            cp.wait()

    shapes = [jax.ShapeDtypeStruct((N_CHIPS - 1,) + b.dims("block"), BF16) for b in bigs]
    return _Phase(sums, shapes, {}, 3 * n, start, finish, then)


_HBM = pl.BlockSpec(memory_space=pltpu.HBM)
_SEM = pl.BlockSpec(memory_space=pltpu.SEMAPHORE)
_DATAFLOW = pltpu.SideEffectType.DATAFLOW_SIDE_EFFECTING


class _InFlight:
    def __init__(self, phase, send, recv, arrays, token):
        self.phase, self.send, self.recv, self.arrays, self.token = phase, send, recv, arrays, token


def _phase_results(phase, refs):
    n_in = len(phase.ins)
    updated = {o: i for i, o in phase.aliases.items()}
    fresh = [o for o in range(len(phase.out_shapes)) if o not in updated]
    return [refs[updated[o]] if o in updated else refs[n_in + fresh.index(o)] for o in range(len(phase.out_shapes))]


def _split_start(phase, name):
    n_in = len(phase.ins)
    fresh = [s for o, s in enumerate(phase.out_shapes) if o not in phase.aliases.values()]
    arrays = list(phase.ins) + [lax.empty(s.shape, s.dtype) for s in fresh]
    n = len(arrays)

    def body(*refs):
        phase.start(refs[:n_in], _phase_results(phase, refs[:n]), refs[n], refs[n + 1])
        refs[-1][...] = jnp.zeros_like(refs[-1])

    operands = [pltpu.with_memory_space_constraint(a, pltpu.HBM) for a in arrays]
    res = pl.pallas_call(
        body, name=name,
        out_shape=[pltpu.SemaphoreType.DMA((phase.n_sems,)), pltpu.SemaphoreType.DMA((phase.n_sems,))]
        + [pltpu.HBM(a.shape, a.dtype) for a in arrays] + [jax.ShapeDtypeStruct((8, 128), F32)],
        in_specs=[_HBM] * n, out_specs=[_SEM, _SEM] + [_HBM] * n + [pl.BlockSpec(memory_space=pltpu.VMEM)],
        input_output_aliases={i: 2 + i for i in range(n)},
        compiler_params=pltpu.CompilerParams(has_side_effects=_DATAFLOW),
    )(*operands)
    return _InFlight(phase, res[0], res[1], list(res[2 : 2 + n]), res[-1])


def _split_wait(flight, after, name):
    phase, n = flight.phase, len(flight.arrays)
    n_in = len(phase.ins)

    def body(*refs):
        phase.finish(refs[:n_in], _phase_results(phase, refs[:n]), refs[n], refs[n + 1])

    res = pl.pallas_call(
        body, name=name, out_shape=[pltpu.HBM(a.shape, a.dtype) for a in flight.arrays],
        in_specs=[_HBM] * n + [_SEM, _SEM] + [_ANY] * len(after), out_specs=[_HBM] * n,
        input_output_aliases={i: i for i in range(n)},
        compiler_params=pltpu.CompilerParams(has_side_effects=_DATAFLOW),
    )(*flight.arrays, flight.send, flight.recv, *after)
    res = list(res)
    phase.then(_phase_results(phase, res))
    return res[:n_in]


def _phase_pair_broadcast(stacks, bigs, batch0s, then):
    n = len(stacks)

    def start(ins, outs, send, recv):
        x, y, c = _place()
        for a in range(n):
            blk = bigs[a].view(outs[a], None, c, batch0s[a])
            _remote(blk, blk, send, recv, a, (x, y, 1 - c)).start()

    def finish(ins, outs, send, recv):
        x, y, c = _place()
        for a in range(n):
            mine = bigs[a].view(outs[a], None, c, batch0s[a])
            theirs = bigs[a].view(outs[a], None, 1 - c, batch0s[a])
            _remote(mine, mine, send, recv, a, (x, y, 1 - c)).wait_send()
            _remote(theirs, theirs, send, recv, a, (x, y, 1 - c)).wait_recv()

    return _Phase(stacks, _same(stacks), {a: a for a in range(n)}, n, start, finish, then)


def _tile_call(body, name, big, where, extra, ins, in_specs, out_specs, out_shape, phases=()):
    grid = ((extra,) if extra else ()) + big.grid
    return _call(body, name, grid, in_specs, out_specs, out_shape, ins, prefetch=(where,), phases=phases)


def _cast_into_full(w_stack, batch0, big, where, name, phases=()):
    def body(_, w_ref, o_ref):
        o_ref[...] = w_ref[...].astype(BF16)

    return _tile_call(
        body, name, big, where, 2, [w_stack], [big.spec(None, "grid", batch0=batch0)], [big.spec("pref", "grid")],
        [jax.ShapeDtypeStruct(big.dims("full"), BF16)], phases,
    )


def _pair_sum(g_full, recv_half, big, where, name, phases=()):
    def body(_, g_ref, r_ref, o_ref):
        o_ref[...] = (g_ref[...].astype(F32) + r_ref[...].astype(F32)).astype(BF16)

    half = big.spec("grid", None)
    return _tile_call(
        body, name, big, where, N_CHIPS, [g_full, recv_half], [big.spec("grid", "pref"), half], [half],
        [jax.ShapeDtypeStruct(big.dims("half"), BF16)], phases,
    )


def _chip_sum(chip_sum, parts, big, where, stack, stack_shape, batch0, name, phases=()):
    def body(_, own_ref, p_ref, *rest):
        acc = own_ref[...].astype(F32)
        for k in range(N_CHIPS - 1):
            acc = acc + p_ref[k].astype(F32)
        rest[-1][...] = acc

    ins = [chip_sum, parts] + ([stack] if stack is not None else [])
    in_specs = [big.spec("pref", None), big.spec(None, None, lead=(N_CHIPS - 1,))] + ([_ANY] if stack is not None else [])
    return _call(
        body, name, big.grid, in_specs, [big.spec(None, "pref", batch0=batch0)], [jax.ShapeDtypeStruct(stack_shape, F32)], ins,
        prefetch=(where,), phases=phases, in_place={2: 0} if stack is not None else None,
    )


def _adam_stack(w, g, m, v, name, after=()):
    b, r, c = w.shape
    tr = _row_tile(r, c, ADAM_BLOCK_ELEMS)

    def body(w_ref, g_ref, m_ref, v_ref, *rest):
        go_ref, d_ref, mo_ref, vo_ref = rest[-4:]
        gv = g_ref[...]
        d, mo, vo = _adam(w_ref[...], gv, m_ref[...], v_ref[...])
        go_ref[...] = gv
        d_ref[...] = d
        mo_ref[...] = mo
        vo_ref[...] = vo

    spec = pl.BlockSpec((1, tr, c), lambda bb, i: (bb, i, 0))
    outs, _ = _call(
        body, name, (b, r // tr), [spec] * 4 + [_ANY] * len(after), [spec] * 4, [jax.ShapeDtypeStruct(w.shape, F32)] * 4,
        [w, g, m, v, *after],
    )
    return outs


def _mod_fwd(c_all, w_mod, b_cols, phases=()):
    n_layers, d, n = w_mod.shape
    tn = _pick(n, (768, 512, 384, 256, 128))

    def body(c_ref, w_ref, b_ref, o_ref):
        cv = c_ref[...]
        ca = (cv * _sigmoid(cv)).astype(BF16)
        o_ref[0] = _dot(ca, w_ref[0].astype(BF16)) + b_ref[0]

    return _call(
        body, "mod_fwd", (n_layers, n // tn),
        [
            pl.BlockSpec((N_DEV, d), lambda l, j: (0, 0)),
            pl.BlockSpec((1, d, tn), lambda l, j: (l, 0, j)),
            pl.BlockSpec((1, 1, tn), lambda l, j: (l, 0, j)),
        ],
        [pl.BlockSpec((1, N_DEV, tn), lambda l, j: (l, 0, j))],
        [jax.ShapeDtypeStruct((n_layers, N_DEV, n), F32)], [c_all, w_mod, b_cols], phases=phases,
    )


def _mod_bwd_adam(c_all_t, dmod_cols, w, m, v, after=()):
    n_layers, d, n = w.shape
    tn = _pick(n, (384, 256, 128))

    def body(c_ref, dm_ref, w_ref, m_ref, v_ref, *rest):
        g_ref, d_ref, mo_ref, vo_ref = rest[-4:]
        cv = c_ref[...]
        ca = (cv * _sigmoid(cv)).astype(BF16)
        g = _dot(ca, dm_ref[0].astype(BF16))
        g_ref[0] = g
        dl, mo, vo = _adam(w_ref[0], g, m_ref[0], v_ref[0])
        d_ref[0] = dl
        mo_ref[0] = mo
        vo_ref[0] = vo

    wspec = pl.BlockSpec((1, d, tn), lambda l, j: (l, 0, j))
    outs, _ = _call(
        body, "mod_bwd_adam", (n_layers, n // tn),
        [pl.BlockSpec((d, N_DEV), lambda l, j: (0, 0)), pl.BlockSpec((1, N_DEV, tn), lambda l, j: (l, 0, j)), wspec, wspec, wspec]
        + [_ANY] * len(after),
        [wspec] * 4, [jax.ShapeDtypeStruct(w.shape, F32)] * 4, [c_all_t, dmod_cols, w, m, v, *after],
    )
    return outs


def _ffn_fwd(x, vec, w_in, w_out, name, phases=()):
    s, d = x.shape
    f = w_out.shape[1]
    tm = _pick(s, (1024, 512, 256, 128))
    tf = _pick(f, (256, 128))
    nf = f // tf

    def body(x_ref, vec_ref, wg_ref, wu_ref, wo_ref, xo_ref, g_ref, u_ref, y_ref, h_sc, acc_sc):
        j = pl.program_id(1)

        @pl.when(j == 0)
        def _():
            h_sc[...] = _modulate(x_ref[...], vec_ref).astype(BF16)
            acc_sc[...] = jnp.zeros_like(acc_sc)

        h = h_sc[...]
        g = _dot(h, wg_ref[0])
        u = _dot(h, wu_ref[0])
        g_ref[...] = g.astype(BF16)
        u_ref[...] = u.astype(BF16)
        a = (g * _sigmoid(g) * u).astype(BF16)
        acc_sc[...] += _dot(a, wo_ref[0])

        @pl.when(j == nf - 1)
        def _():
            yv = acc_sc[...]
            xo_ref[...] = x_ref[...] + 0.5 * vec_ref[3:4, :] * yv
            y_ref[...] = yv.astype(BF16)

    row = pl.BlockSpec((tm, d), lambda i, j: (i, 0))
    hid = pl.BlockSpec((tm, tf), lambda i, j: (i, j))
    return _call(
        body, name, (s // tm, nf),
        [
            row,
            pl.BlockSpec((8, d), lambda i, j: (0, 0)),
            pl.BlockSpec((1, d, tf), lambda i, j: (0, 0, j)),
            pl.BlockSpec((1, d, tf), lambda i, j: (0, 0, nf + j)),
            pl.BlockSpec((1, tf, d), lambda i, j: (0, j, 0)),
        ],
        [row, hid, hid, row],
        [
            jax.ShapeDtypeStruct((s, d), F32),
            jax.ShapeDtypeStruct((s, f), BF16),
            jax.ShapeDtypeStruct((s, f), BF16),
            jax.ShapeDtypeStruct((s, d), BF16),
        ],
        [x, vec, w_in, w_in, w_out],
        scratch=[pltpu.VMEM((tm, d), BF16), pltpu.VMEM((tm, d), F32)], phases=phases,
    )


def _ffn_bwd(dxo, x, vec, gg, uu, y, w_in, w_out, name, phases=()):
    s, d = x.shape
    f = w_out.shape[1]
    tm = _pick(s, (512, 256, 128))
    tf = _pick(f, (256, 128))
    nf = f // tf

    def body(dxo_ref, x_ref, vec_ref, g_ref, u_ref, y_ref, wg_ref, wu_ref, wo_ref,
             dx_ref, dg_ref, du_ref, a_ref, h_ref, dy_ref, dvec_ref, acc_sc):
        i, j = pl.program_id(0), pl.program_id(1)

        @pl.when((i == 0) & (j == 0))
        def _():
            dvec_ref[...] = jnp.zeros_like(dvec_ref)

        @pl.when(j == 0)
        def _():
            dxo_v = dxo_ref[...]
            dy_ref[...] = (0.5 * vec_ref[3:4, :] * dxo_v).astype(BF16)
            dvec_ref[3:4, :] += 0.5 * jnp.sum(dxo_v * y_ref[...].astype(F32), axis=0, keepdims=True)
            acc_sc[...] = jnp.zeros_like(acc_sc)

        da = _dot_nt(dy_ref[...], wo_ref[0])
        g = g_ref[...].astype(F32)
        u = u_ref[...].astype(F32)
        sig = _sigmoid(g)
        sl = g * sig
        a_ref[...] = (sl * u).astype(BF16)
        dg = (da * u * (sig * (1.0 + g * (1.0 - sig)))).astype(BF16)
        du = (da * sl).astype(BF16)
        dg_ref[...] = dg
        du_ref[...] = du
        acc_sc[...] += _dot_nt(dg, wg_ref[0]) + _dot_nt(du, wu_ref[0])

        @pl.when(j == nf - 1)
        def _():
            dx, h = _modulate_bwd(x_ref[...], acc_sc[...], vec_ref, dvec_ref)
            dx_ref[...] = dxo_ref[...] + dx
            h_ref[...] = h.astype(BF16)

    row = pl.BlockSpec((tm, d), lambda i, j: (i, 0))
    hid = pl.BlockSpec((tm, tf), lambda i, j: (i, j))
    vecs = pl.BlockSpec((8, d), lambda i, j: (0, 0))
    return _call(
        body, name, (s // tm, nf),
        [
            row, row, vecs, hid, hid, row,
            pl.BlockSpec((1, d, tf), lambda i, j: (0, 0, j)),
            pl.BlockSpec((1, d, tf), lambda i, j: (0, 0, nf + j)),
            pl.BlockSpec((1, tf, d), lambda i, j: (0, j, 0)),
        ],
        [row, hid, hid, hid, row, row, vecs],
        [
            jax.ShapeDtypeStruct((s, d), F32),
            jax.ShapeDtypeStruct((s, f), BF16),
            jax.ShapeDtypeStruct((s, f), BF16),
            jax.ShapeDtypeStruct((s, f), BF16),
            jax.ShapeDtypeStruct((s, d), BF16),
            jax.ShapeDtypeStruct((s, d), BF16),
            jax.ShapeDtypeStruct((8, d), F32),
        ],
        [dxo, x, vec, gg, uu, y, w_in, w_in, w_out],
        scratch=[pltpu.VMEM((tm, d), F32)], phases=phases,
    )


def _grad_half(a, bs, big, where, mine, recv, name, phases=()):
    s, k1 = a.shape
    n = bs[0].shape[1]
    groups = len(bs)
    rows_halved = big.h3 == 1
    assert rows_halved or groups == 1
    kk, nn = (k1 // 2, n) if rows_halved else (k1, n // 2)
    tk = _pick(kk, (1408, 1024, 512, 256, 128))
    tn = _pick(nn, (1408, 1024, 640, 512, 256, 128))
    nkb, nnb = kk // tk, nn // tn
    assert (recv is None) == (not mine)

    def half(pref):
        return pref[1] if mine else 1 - pref[1]

    def body(_, a_ref, *rest):
        q = pl.program_id(1)
        for p in range(groups):

            @pl.when(q == p)
            def _(p=p):
                acc = _dot_tn(a_ref[...], rest[p][...])
                if recv is not None:
                    acc = acc + rest[groups][0].astype(F32)
                rest[-1][0] = acc.astype(BF16)

    def b_block(p):
        def index(i, q, j, pref):
            jj = jnp.where(q == p, j, jnp.where(q < p, 0, nnb - 1))
            return (0, jj + (0 if rows_halved else half(pref) * nnb))

        return pl.BlockSpec((s, tn), index)

    out_spec = pl.BlockSpec((1, tk, tn), lambda i, q, j, pref: (0, i, q * nnb + j))
    in_specs = [pl.BlockSpec((s, tk), lambda i, q, j, pref: (0, i + (half(pref) * nkb if rows_halved else 0)))]
    in_specs += [b_block(p) for p in range(groups)]
    ins = [a, *bs]
    if recv is not None:
        in_specs.append(out_spec)
        ins.append(recv)
    return _call(
        body, name, (nkb, groups, nnb), in_specs, [out_spec], [jax.ShapeDtypeStruct(big.dims("half"), BF16)], ins,
        prefetch=(where,), phases=phases,
    )


def _proj_mod_fwd(x, vec, w, phases=()):
    s, d = x.shape
    n = w.shape[2]
    tm = _pick(s, (1024, 512, 256, 128))
    tn = _pick(n, (640, 512, 256, 128))

    def body(x_ref, vec_ref, w_ref, o_ref, h_sc):
        @pl.when(pl.program_id(1) == 0)
        def _():
            h_sc[...] = _modulate(x_ref[...], vec_ref).astype(BF16)

        o_ref[...] = _dot(h_sc[...], w_ref[0])

    return _call(
        body, "ab_in_fwd", (s // tm, n // tn),
        [
            pl.BlockSpec((tm, d), lambda i, j: (i, 0)),
            pl.BlockSpec((8, d), lambda i, j: (0, 0)),
            pl.BlockSpec((1, d, tn), lambda i, j: (0, 0, j)),
        ],
        [pl.BlockSpec((tm, tn), lambda i, j: (i, j))],
        [jax.ShapeDtypeStruct((s, n), F32)], [x, vec, w],
        scratch=[pltpu.VMEM((tm, d), BF16)], phases=phases,
    )


def _proj_res_fwd(a, w, x, vec, phases=()):
    s, kd = a.shape
    d = x.shape[1]
    tm = _pick(s, (1024, 512, 256, 128))

    def body(a_ref, w_ref, x_ref, vec_ref, xo_ref, y_ref):
        yv = _dot(a_ref[...], w_ref[0])
        xo_ref[...] = x_ref[...] + vec_ref[3:4, :] * yv
        y_ref[...] = yv.astype(BF16)

    row = pl.BlockSpec((tm, d), lambda i: (i, 0))
    return _call(
        body, "ab_out_fwd", (s // tm,),
        [pl.BlockSpec((tm, kd), lambda i: (i, 0)), pl.BlockSpec((1, kd, d), lambda i: (0, 0, 0)), row, pl.BlockSpec((8, d), lambda i: (0, 0))],
        [row, row],
        [jax.ShapeDtypeStruct((s, d), F32), jax.ShapeDtypeStruct((s, d), BF16)], [a, w, x, vec], phases=phases,
    )


def _proj_res_bwd(dxo, y, vec, w, phases=()):
    s, d = dxo.shape
    kd = w.shape[1]
    tm = _pick(s, (1024, 512, 256, 128))

    def body(dxo_ref, y_ref, vec_ref, w_ref, dy_ref, da_ref, dgate_ref):
        @pl.when(pl.program_id(0) == 0)
        def _():
            dgate_ref[...] = jnp.zeros_like(dgate_ref)

        dxo_v = dxo_ref[...]
        dy = (vec_ref[3:4, :] * dxo_v).astype(BF16)
        dy_ref[...] = dy
        dgate_ref[3:4, :] += jnp.sum(dxo_v * y_ref[...].astype(F32), axis=0, keepdims=True)
        da_ref[...] = _dot_nt(dy, w_ref[0]).astype(BF16)

    row = pl.BlockSpec((tm, d), lambda i: (i, 0))
    vecs = pl.BlockSpec((8, d), lambda i: (0, 0))
    return _call(
        body, "ab_out_bwd", (s // tm,),
        [row, row, vecs, pl.BlockSpec((1, kd, d), lambda i: (0, 0, 0))],
        [row, pl.BlockSpec((tm, kd), lambda i: (i, 0)), vecs],
        [jax.ShapeDtypeStruct((s, d), BF16), jax.ShapeDtypeStruct((s, kd), BF16), jax.ShapeDtypeStruct((8, d), F32)],
        [dxo, y, vec, w], phases=phases,
    )


def _proj_mod_bwd(dproj, w, x, vec, dxo, dvec_in, name, phases=()):
    parts, s, n_part = dproj.shape
    d = x.shape[1]
    tm = _pick(s, (512, 256, 128))
    tk = _pick(n_part, (1408, 1280, 1024, 512, 256, 128))
    per_part = n_part // tk
    nk = parts * per_part

    def body(dp_ref, w_ref, x_ref, vec_ref, dxo_ref, dvi_ref, dx_ref, h_ref, dvec_ref, acc_sc):
        i, k = pl.program_id(0), pl.program_id(1)

        @pl.when((i == 0) & (k == 0))
        def _():
            dvec_ref[...] = dvi_ref[...]

        @pl.when(k == 0)
        def _():
            acc_sc[...] = jnp.zeros_like(acc_sc)

        acc_sc[...] += _dot_nt(dp_ref[0], w_ref[0])

        @pl.when(k == nk - 1)
        def _():
            dx, h = _modulate_bwd(x_ref[...], acc_sc[...], vec_ref, dvec_ref)
            dx_ref[...] = dxo_ref[...] + dx
            h_ref[...] = h.astype(BF16)

    row = pl.BlockSpec((tm, d), lambda i, k: (i, 0))
    vecs = pl.BlockSpec((8, d), lambda i, k: (0, 0))
    return _call(
        body, name, (s // tm, nk),
        [
            pl.BlockSpec((1, tm, tk), lambda i, k: (k // per_part, i, k % per_part)),
            pl.BlockSpec((1, d, tk), lambda i, k: (0, 0, k)),
            row, vecs, row, vecs,
        ],
        [row, row, vecs],
        [jax.ShapeDtypeStruct((s, d), F32), jax.ShapeDtypeStruct((s, d), BF16), jax.ShapeDtypeStruct((8, d), F32)],
        [dproj, w, x, vec, dxo, dvec_in], scratch=[pltpu.VMEM((tm, d), F32)], phases=phases,
    )


def _tril(n):
    return lax.broadcasted_iota(jnp.int32, (n, n), 0) >= lax.broadcasted_iota(jnp.int32, (n, n), 1)


def _layernorm_stats(gv):
    mu = jnp.mean(gv, axis=-1, keepdims=True)
    cen = gv - mu
    rstd = lax.rsqrt(jnp.mean(cen * cen, axis=-1, keepdims=True) + EPS)
    return cen * rstd, rstd


def _shift_down(q, k, above_ref, c_cg, c_xb, first):
    width = q.shape[1]
    rows = lax.broadcasted_iota(jnp.int32, q.shape, 0)
    out = pltpu.roll(q, k, 0)
    for r in range(k):
        src = CONV_HALO - k + r
        above = above_ref[src : src + 1, c_cg : c_cg + width] * above_ref[src : src + 1, c_xb : c_xb + width]
        above = jnp.where(first, 0.0, above)
        out = jnp.where(rows == r, above, out)
    return out


def _ab_mix_fwd(proj, norm_v, w_s, b_rows, conv_w, phases=()):
    s, n = proj.shape
    heads, chunk, _ = w_s.shape
    da = norm_v.shape[1]
    hd = da // heads
    db = conv_w.shape[1]
    tm = _pick(s, (512, 256, 128))

    def body(p_ref, ph_ref, nv_ref, ws_ref, b_ref, cw_ref, o_ref):
        first = pl.program_id(0) == 0
        gu, _ = _gelu(p_ref[:, 0:da])
        gv, _ = _gelu(p_ref[:, da : 2 * da])
        xhat, _ = _layernorm_stats(gv)
        vn = (xhat * nv_ref[...]).astype(BF16)
        mask = _tril(chunk)
        for hh in range(heads):
            wm = jnp.where(mask, ws_ref[hh], 0.0).astype(BF16)
            cols = slice(hh * hd, (hh + 1) * hd)
            for nn in range(tm // chunk):
                rows = slice(nn * chunk, (nn + 1) * chunk)
                z = _dot(wm, vn[rows, cols]) + b_ref[:, cols]
                o_ref[rows, cols] = (gu[rows, cols] * z).astype(BF16)
        c_cg, c_xb = 2 * da + db, 2 * da + 2 * db
        bg = p_ref[:, 2 * da : 2 * da + db]
        q = p_ref[:, c_cg : c_cg + db] * p_ref[:, c_xb : c_xb + db]
        q1 = _shift_down(q, 1, ph_ref, c_cg, c_xb, first)
        q2 = _shift_down(q, 2, ph_ref, c_cg, c_xb, first)
        conv = cw_ref[0:1, :] * q2 + cw_ref[1:2, :] * q1 + cw_ref[2:3, :] * q
        o_ref[:, da : da + db] = (bg * conv).astype(BF16)

    nh = tm // CONV_HALO
    return _call(
        body, "ab_mix_fwd", (s // tm,),
        [
            pl.BlockSpec((tm, n), lambda i: (i, 0)),
            pl.BlockSpec((CONV_HALO, n), lambda i: (jnp.maximum(i * nh - 1, 0), 0)),
            pl.BlockSpec((1, da), lambda i: (0, 0)),
            pl.BlockSpec((heads, chunk, chunk), lambda i: (0, 0, 0)),
            pl.BlockSpec((chunk, da), lambda i: (0, 0)),
            pl.BlockSpec((3, db), lambda i: (0, 0)),
        ],
        [pl.BlockSpec((tm, da + db), lambda i: (i, 0))],
        [jax.ShapeDtypeStruct((s, da + db), BF16)], [proj, proj, norm_v, w_s, b_rows, conv_w], phases=phases,
    )


def _ab_mix_bwd(proj, dcat, norm_v, w_s, b_rows, conv_w, phases=()):
    s, n = proj.shape
    heads, chunk, _ = w_s.shape
    da = norm_v.shape[1]
    hd = da // heads
    db = conv_w.shape[1]
    tm = _pick(s, (512, 256, 128))
    nblk = s // tm
    dhalo = 2 * CONV_HALO

    def body(p_ref, pa_ref, pb_ref, dc_ref, dcb_ref, nv_ref, ws_ref, b_ref, cw_ref,
             dp_ref, dnv_ref, dws_ref, dzs_ref, dcw_ref, dvn_sc):
        i = pl.program_id(0)
        first, last = i == 0, i == nblk - 1

        @pl.when(first)
        def _():
            dnv_ref[...] = jnp.zeros_like(dnv_ref)
            dws_ref[...] = jnp.zeros_like(dws_ref)
            dzs_ref[...] = jnp.zeros_like(dzs_ref)
            dcw_ref[...] = jnp.zeros_like(dcw_ref)

        uu = p_ref[:, 0:da]
        gu, gu_grad = _gelu(uu)
        gv, gv_grad = _gelu(p_ref[:, da : 2 * da])
        xhat, rstd = _layernorm_stats(gv)
        nv = nv_ref[...]
        vn = (xhat * nv).astype(BF16)
        dya = dc_ref[:, 0:da].astype(F32)
        dz = (dya * gu).astype(BF16)
        mask = _tril(chunk)
        for hh in range(heads):
            wm = jnp.where(mask, ws_ref[hh], 0.0).astype(BF16)
            cols = slice(hh * hd, (hh + 1) * hd)
            dws = jnp.zeros((chunk, chunk), F32)
            for nn in range(tm // chunk):
                rows = slice(nn * chunk, (nn + 1) * chunk)
                z = _dot(wm, vn[rows, cols]) + b_ref[:, cols]
                dp_ref[rows, cols] = (dya[rows, cols] * z * gu_grad[rows, cols]).astype(BF16)
                dz_blk = dz[rows, cols]
                dws = dws + _dot_nt(dz_blk, vn[rows, cols])
                dzs_ref[:, cols] += dz_blk.astype(F32)
                dvn = _dot_tn(wm, dz_blk)
                dnv_ref[:, cols] += jnp.sum(dvn * xhat[rows, cols], axis=0, keepdims=True)
                dvn_sc[rows, cols] = dvn
            dws_ref[hh] += jnp.where(mask, dws, 0.0)
        dxhat = dvn_sc[...] * nv
        dgv = rstd * (dxhat - jnp.mean(dxhat, axis=-1, keepdims=True) - xhat * jnp.mean(dxhat * xhat, axis=-1, keepdims=True))
        dp_ref[:, da : 2 * da] = (dgv * gv_grad).astype(BF16)

        c_bg, c_cg, c_xb = 2 * da, 2 * da + db, 2 * da + 2 * db
        bg = p_ref[:, c_bg : c_bg + db]
        cg = p_ref[:, c_cg : c_cg + db]
        xb = p_ref[:, c_xb : c_xb + db]
        q = cg * xb
        q1 = _shift_down(q, 1, pa_ref, c_cg, c_xb, first)
        q2 = _shift_down(q, 2, pa_ref, c_cg, c_xb, first)
        dyb = dc_ref[:, da : da + db].astype(F32)
        conv = cw_ref[0:1, :] * q2 + cw_ref[1:2, :] * q1 + cw_ref[2:3, :] * q
        dp_ref[:, c_bg : c_bg + db] = (dyb * conv).astype(BF16)
        e = dyb * bg
        dcw_ref[0:1, :] += jnp.sum(e * q2, axis=0, keepdims=True)
        dcw_ref[1:2, :] += jnp.sum(e * q1, axis=0, keepdims=True)
        dcw_ref[2:3, :] += jnp.sum(e * q, axis=0, keepdims=True)
        rows = lax.broadcasted_iota(jnp.int32, e.shape, 0)
        dq = cw_ref[2:3, :] * e
        for kk in (1, 2):
            ek = pltpu.roll(e, tm - kk, 0)
            for r in range(kk):
                below = dcb_ref[r : r + 1, da : da + db].astype(F32) * pb_ref[r : r + 1, c_bg : c_bg + db]
                below = jnp.where(last, 0.0, below)
                ek = jnp.where(rows == tm - kk + r, below, ek)
            dq = dq + cw_ref[2 - kk : 3 - kk, :] * ek
        dp_ref[:, c_cg : c_cg + db] = (dq * xb).astype(BF16)
        dp_ref[:, c_xb : c_xb + db] = (dq * cg).astype(BF16)

    nh = tm // CONV_HALO
    nhb = tm // dhalo
    const2 = lambda i: (0, 0)
    return _call(
        body, "ab_mix_bwd", (nblk,),
        [
            pl.BlockSpec((tm, n), lambda i: (i, 0)),
            pl.BlockSpec((CONV_HALO, n), lambda i: (jnp.maximum(i * nh - 1, 0), 0)),
            pl.BlockSpec((CONV_HALO, n), lambda i: (jnp.minimum((i + 1) * nh, s // CONV_HALO - 1), 0)),
            pl.BlockSpec((tm, da + db), lambda i: (i, 0)),
            pl.BlockSpec((dhalo, da + db), lambda i: (jnp.minimum((i + 1) * nhb, s // dhalo - 1), 0)),
            pl.BlockSpec((1, da), const2),
            pl.BlockSpec((heads, chunk, chunk), lambda i: (0, 0, 0)),
            pl.BlockSpec((chunk, da), const2),
            pl.BlockSpec((3, db), const2),
        ],
        [
            pl.BlockSpec((tm, n), lambda i: (i, 0)),
            pl.BlockSpec((1, da), const2),
            pl.BlockSpec((heads, chunk, chunk), lambda i: (0, 0, 0)),
            pl.BlockSpec((chunk, da), const2),
            pl.BlockSpec((3, db), const2),
        ],
        [
            jax.ShapeDtypeStruct((s, n), BF16),
            jax.ShapeDtypeStruct((1, da), F32),
            jax.ShapeDtypeStruct((heads, chunk, chunk), F32),
            jax.ShapeDtypeStruct((chunk, da), F32),
            jax.ShapeDtypeStruct((3, db), F32),
        ],
        [proj, proj, proj, dcat, dcat, norm_v, w_s, b_rows, conv_w],
        scratch=[pltpu.VMEM((tm, da), F32)], phases=phases,
    )


def _pool_counts(tm, i, w):
    t = i * tm + lax.broadcasted_iota(jnp.int32, (tm, 1), 0)
    return jnp.minimum(t + 1, w).astype(F32)


def _pool_fwd(x, vec, w_grp, scale, phases=()):
    s, d = x.shape
    groups, gd, _ = w_grp.shape
    tm = _pick(s, (512, 256, 128))

    def body(x_ref, xa_ref, vec_ref, w_ref, sc_ref, xo_ref, p_ref, o_ref):
        i = pl.program_id(0)
        h = _modulate(x_ref[...], vec_ref)
        ha = jnp.where(i == 0, 0.0, _modulate(xa_ref[...], vec_ref))
        ext = jnp.concatenate([ha, h], axis=0)
        for gi, w in enumerate(POOL_WINDOWS):
            cols = slice(gi * gd, (gi + 1) * gd)
            acc = ext[:, cols]
            step = 1
            while step < w:
                acc = acc + pltpu.roll(acc, step, 0)
                step *= 2
            p = (acc[POOL_HALO:, :] / _pool_counts(tm, i, w) - h[:, cols]).astype(BF16)
            p_ref[:, cols] = p
            o_ref[:, cols] = _dot(p, w_ref[gi]).astype(BF16)
        xo_ref[...] = x_ref[...] + vec_ref[3:4, :] * (o_ref[...].astype(F32) * sc_ref[...])

    nh = tm // POOL_HALO
    row = pl.BlockSpec((tm, d), lambda i: (i, 0))
    return _call(
        body, "pool_fwd", (s // tm,),
        [
            row,
            pl.BlockSpec((POOL_HALO, d), lambda i: (jnp.maximum(i * nh - 1, 0), 0)),
            pl.BlockSpec((8, d), lambda i: (0, 0)),
            pl.BlockSpec((groups, gd, gd), lambda i: (0, 0, 0)),
            pl.BlockSpec((1, d), lambda i: (0, 0)),
        ],
        [row, row, row],
        [jax.ShapeDtypeStruct((s, d), F32), jax.ShapeDtypeStruct((s, d), BF16), jax.ShapeDtypeStruct((s, d), BF16)],
        [x, x, vec, w_grp, scale], phases=phases,
    )


def _pool_bwd(dxo, x, vec, p, o, w_grp, scale, phases=()):
    s, d = x.shape
    groups, gd, _ = w_grp.shape
    tm = _pick(s, (512, 256, 128))
    nblk = s // tm

    def body(dxo_ref, dxb_ref, x_ref, vec_ref, p_ref, o_ref, w_ref, sc_ref, dx_ref, dw_ref, dsc_ref, dvec_ref, dw_sc):
        i = pl.program_id(0)

        @pl.when(i == 0)
        def _():
            dw_sc[...] = jnp.zeros_like(dw_sc)
            dsc_ref[...] = jnp.zeros_like(dsc_ref)
            dvec_ref[...] = jnp.zeros_like(dvec_ref)

        gate, sc = vec_ref[3:4, :], sc_ref[...]
        dxo_v = dxo_ref[...]
        ov = o_ref[...].astype(F32)
        dvec_ref[3:4, :] += jnp.sum(dxo_v * (ov * sc), axis=0, keepdims=True)
        dy = gate * dxo_v
        dsc_ref[...] += jnp.sum(dy * ov, axis=0, keepdims=True)
        dout = (dy * sc).astype(BF16)
        dout_b = jnp.where(i == nblk - 1, 0.0, gate * dxb_ref[...] * sc).astype(BF16)
        for gi, w in enumerate(POOL_WINDOWS):
            cols = slice(gi * gd, (gi + 1) * gd)
            dw_sc[gi] += _dot_tn(p_ref[:, cols], dout[:, cols])
            wb = w_ref[gi]
            dp = _dot_nt(dout[:, cols], wb)
            dp_b = _dot_nt(dout_b[:, cols], wb)
            e = dp / _pool_counts(tm, i, w)
            t_below = (i + 1) * tm + lax.broadcasted_iota(jnp.int32, (POOL_HALO, 1), 0)
            e_b = dp_b / jnp.minimum(t_below + 1, w).astype(F32)
            acc = jnp.concatenate([e, e_b], axis=0)
            step = 1
            while step < w:
                acc = acc + pltpu.roll(acc, tm + POOL_HALO - step, 0)
                step *= 2
            dx_ref[:, cols] = acc[:tm, :] - dp
        dx, _ = _modulate_bwd(x_ref[...], dx_ref[...], vec_ref, dvec_ref)
        dx_ref[...] = dxo_v + dx

        @pl.when(i == nblk - 1)
        def _():
            dw_ref[...] = dw_sc[...].astype(BF16)

    nh = tm // POOL_HALO
    row = pl.BlockSpec((tm, d), lambda i: (i, 0))
    vecs = pl.BlockSpec((8, d), lambda i: (0, 0))
    wspec = pl.BlockSpec((groups, gd, gd), lambda i: (0, 0, 0))
    return _call(
        body, "pool_bwd", (nblk,),
        [
            row,
            pl.BlockSpec((POOL_HALO, d), lambda i: (jnp.minimum((i + 1) * nh, s // POOL_HALO - 1), 0)),
            row, vecs, row, row, wspec,
            pl.BlockSpec((1, d), lambda i: (0, 0)),
        ],
        [row, wspec, pl.BlockSpec((1, d), lambda i: (0, 0)), vecs],
        [
            jax.ShapeDtypeStruct((s, d), F32),
            jax.ShapeDtypeStruct((groups, gd, gd), BF16),
            jax.ShapeDtypeStruct((1, d), F32),
            jax.ShapeDtypeStruct((8, d), F32),
        ],
        [dxo, dxo, x, vec, p, o, w_grp, scale],
        scratch=[pltpu.VMEM((groups, gd, gd), F32)], phases=phases,
    )


def _loss_head(x, gain, target, phases=()):
    s, d = x.shape
    tm = _pick(s, (512, 256, 128))

    def body(x_ref, g_ref, t_ref, dx_ref, aux_ref):
        @pl.when(pl.program_id(0) == 0)
        def _():
            aux_ref[...] = jnp.zeros_like(aux_ref)

        xv = x_ref[...]
        rstd = _rstd(xv)
        r = xv * rstd
        gain_v = g_ref[...]
        err = r * gain_v - t_ref[...]
        aux_ref[1:2, :] += jnp.sum(err * err, axis=0, keepdims=True)
        dout = err * (1.0 / d)
        aux_ref[0:1, :] += jnp.sum(dout * r, axis=0, keepdims=True)
        dr = dout * gain_v
        dx_ref[...] = rstd * (dr - r * jnp.mean(dr * r, axis=-1, keepdims=True))

    row = pl.BlockSpec((tm, d), lambda i: (i, 0))
    return _call(
        body, "loss_head", (s // tm,),
        [row, pl.BlockSpec((1, d), lambda i: (0, 0)), row],
        [row, pl.BlockSpec((8, d), lambda i: (0, 0))],
        [jax.ShapeDtypeStruct((s, d), F32), jax.ShapeDtypeStruct((8, d), F32)], [x, gain, target], phases=phases,
    )


def _small_adam(gathered, gathered_ws, layout, smalls, chip):
    names = list(smalls)
    n = len(names)
    loss_row, _, _, n_feat = layout["loss"]

    def body(*refs):
        chip_ref, g_ref, gws_ref = refs[0], refs[1], refs[2]
        wmv = refs[3 : 3 + 3 * n]
        outs = refs[3 + 3 * n : 3 + 7 * n]
        total = refs[-1]
        total[...] = g_ref[0]
        for kdev in range(1, N_DEV):
            total[...] += g_ref[kdev]
        total_ws = gws_ref[0]
        for kdev in range(1, N_DEV):
            total_ws = total_ws + gws_ref[kdev]
        my_chip = chip_ref[0]
        for a, name in enumerate(names):
            w_ref, m_ref, v_ref = wmv[3 * a : 3 * a + 3]
            if name == "ab_w_s":
                g = total_ws
            else:
                row0, rows, col0, cols = layout[name]
                if col0 is None:
                    g = jnp.zeros((rows, cols), F32)
                    for j in range(N_CHIPS):
                        g = g + jnp.where(my_chip == j, total[row0 : row0 + rows, j * cols : (j + 1) * cols], 0.0)
                else:
                    g = total[row0 : row0 + rows, col0 : col0 + cols]
            dl, mo, vo = _adam(w_ref[...], g, m_ref[...], v_ref[...])
            outs[4 * a][...] = g
            outs[4 * a + 1][...] = dl
            outs[4 * a + 2][...] = mo
            outs[4 * a + 3][...] = vo
        refs[3 + 7 * n][...] = 0.5 * jnp.sum(total[loss_row : loss_row + 1, 0:n_feat], axis=1, keepdims=True) / n_feat

    ins = [gathered, gathered_ws]
    out_shapes = []
    for name in names:
        ins.extend(smalls[name])
        out_shapes.extend([jax.ShapeDtypeStruct(smalls[name][0].shape, F32)] * 4)
    out_shapes.append(jax.ShapeDtypeStruct((1, 1), F32))
    whole = lambda shape: pl.BlockSpec(shape, functools.partial(lambda nd, i, c: (0,) * nd, len(shape)))
    res = pl.pallas_call(
        body, name="small_adam",
        grid_spec=pltpu.PrefetchScalarGridSpec(
            num_scalar_prefetch=1, grid=(1,),
            in_specs=[whole(a.shape) for a in ins], out_specs=[whole(o.shape) for o in out_shapes],
            scratch_shapes=[pltpu.VMEM(gathered.shape[1:], F32)],
        ),
        out_shape=out_shapes,
        compiler_params=pltpu.CompilerParams(dimension_semantics=("arbitrary",), vmem_limit_bytes=VMEM_LIMIT_BYTES),
    )(chip.reshape(1).astype(jnp.int32), *ins)
    return {name: res[4 * a : 4 * a + 4] for a, name in enumerate(names)}, res[4 * n]


def _pad_rows(a, rows=8):
    extra = (-a.shape[0]) % rows
    return jnp.pad(a, ((0, extra), (0, 0))) if extra else a


def _pad_cols(a, cols):
    return jnp.pad(a, ((0, 0), (0, cols - a.shape[1]))) if a.shape[1] < cols else a


def _run(fn, *phases):
    outs, p_outs = fn(list(phases))
    for p, po in zip(phases, p_outs):
        p.then(po)
    return outs


def kernel(x, c, norm_g, w_mod, b_mod, w_ffn_in, w_ffn_out, ab_w_in, ab_norm_v, ab_w_s, ab_b_s, ab_conv_w, ab_w_out, pool_w_grp, pool_scale, final_g, loss_target, m_norm_g, m_w_mod, m_b_mod, m_w_ffn_in, m_w_ffn_out, m_ab_w_in, m_ab_norm_v, m_ab_w_s, m_ab_b_s, m_ab_conv_w, m_ab_w_out, m_pool_w_grp, m_pool_scale, m_final_g, v_norm_g, v_w_mod, v_b_mod, v_w_ffn_in, v_w_ffn_out, v_ab_w_in, v_ab_norm_v, v_ab_w_s, v_ab_b_s, v_ab_conv_w, v_ab_w_out, v_pool_w_grp, v_pool_scale, v_final_g):
    ix, iy, ic = _place()
    chip = 2 * ix + iy
    me = 4 * ix + 2 * iy + ic
    where = jnp.stack([chip, ic]).astype(jnp.int32)
    s, d = x.shape[1], x.shape[2]
    x0 = x.reshape(s, d)
    target = loss_target.reshape(s, d)
    n_layers = norm_g.shape[0]
    dq = d // N_CHIPS
    heads, chunk = ab_w_s.shape[1], ab_w_s.shape[2]
    da = ab_norm_v.shape[1]
    db = ab_conv_w.shape[2] * N_CHIPS
    f_hidden = w_ffn_out.shape[2] * N_CHIPS
    assert n_layers == 2 and da % heads == 0

    cw_pad = _pad_cols(ab_conv_w.reshape(3, db // N_CHIPS), dq)
    packed = jnp.concatenate(
        [_pad_rows(c.reshape(N_CHIPS, dq)), _pad_rows(norm_g.reshape(-1, dq)), _pad_rows(pool_scale.reshape(1, dq)), _pad_rows(cw_pad)],
        axis=0,
    )
    ncol = w_mod.shape[2]
    b_cols = lax.dynamic_slice(b_mod, (0, chip * ncol), (n_layers, ncol)).reshape(n_layers, 1, ncol)
    small = {}

    def small_gather(key, arrs):
        def then(outs):
            small[key] = outs

        return _phase_small_gather(arrs, then)

    stacks = {
        "w_ffn_in": tuple(a.reshape((-1,) + a.shape[2:]) for a in (w_ffn_in, m_w_ffn_in, v_w_ffn_in)),
        "w_ffn_out": tuple(a.reshape((-1,) + a.shape[2:]) for a in (w_ffn_out, m_w_ffn_out, v_w_ffn_out)),
        "ab_w_in": (ab_w_in, m_ab_w_in, v_ab_w_in),
        "ab_w_out": (ab_w_out, m_ab_w_out, v_ab_w_out),
        "pool_w_grp": (pool_w_grp[0], m_pool_w_grp[0], v_pool_w_grp[0]),
    }
    big_in = _Big((1, d, 2 * f_hidden), 2, 1)
    big_out = _Big((1, f_hidden, d), 1, 2)
    units = {}
    for l in range(n_layers):
        for k in range(2):
            units[f"in{l}{k}"] = (big_in, "w_ffn_in", 2 * l + k)
            units[f"out{l}{k}"] = (big_out, "w_ffn_out", 2 * l + k)
    units["abin"] = (_Big((1, d, ab_w_in.shape[2] * N_CHIPS), 2, 1), "ab_w_in", 0)
    units["about"] = (_Big((1, ab_w_out.shape[1] * N_CHIPS, d), 1, 2), "ab_w_out", 0)
    units["pool"] = (_Big((pool_w_grp.shape[1], pool_w_grp.shape[2] * N_CHIPS, pool_w_grp.shape[3]), 1, 0), "pool_w_grp", 0)
    big = {u: g for u, (g, _, _) in units.items()}

    weight = {}
    complete = set()

    def cast(u):
        g, st, b0 = units[u]

        def launch(phases):
            (weight[u],), p_outs = _cast_into_full(stacks[st][0], b0, g, where, "cast_" + u, phases)
            return None, p_outs

        return launch

    def gather_relay(us, second, whole_first):
        def then(outs):
            for u, o in zip(us, outs):
                weight[u] = o

        return _phase_gather_relay([weight[u] for u in us], [big[u] for u in us], second, whole_first, then)

    def gather_sibling(*us):
        def then(outs):
            for u, o in zip(us, outs):
                weight[u] = o
                complete.add(u)

        return _phase_gather_sibling([weight[u] for u in us], [big[u] for u in us], then)

    def w_of(u):
        assert u in complete, u
        return weight[u]

    _run(cast("in00"), small_gather("inputs", [packed]))
    small_all = small["inputs"][0]
    by_chip = small_all[0::2]
    c_all = small_all[:, 0:N_CHIPS, :].reshape(N_DEV, d)
    norm_full = by_chip[:, 8 : 8 + 3 * n_layers, :].transpose(1, 0, 2).reshape(3 * n_layers, d)
    pool_scale_full = by_chip[:, 16:17, :].transpose(1, 0, 2).reshape(1, d)
    conv_full = by_chip[:, 24:27, : db // N_CHIPS].transpose(1, 0, 2).reshape(3, db)
    pieces = [("in00", "out00"), ("abin", "about"), ("in01", "out01"), ("in10", "out10", "pool"), ("in11", "out11")]
    in_flight = {}

    def start_gather(p):
        in_flight[p, 0] = _split_start(gather_relay(pieces[p], False, p == 0), f"gather_{p}_start")

    def relay_gather(p, after=()):
        flight = in_flight.pop((p, 0))
        _split_wait(flight, list(after) + list(started().ins), f"gather_{p}_arrived")
        in_flight[p, 1] = _split_start(gather_relay(pieces[p], True, p == 0), f"gather_{p}_relay")

    def started():
        return _after(*[flight.token for flight in in_flight.values()])

    def finish_gather(p, after, meanwhile=None):
        flight = in_flight.pop((p, 1))
        _split_wait(flight, list(after) + list(started().ins), f"gather_{p}_wait")
        crossing = _split_start(gather_sibling(*pieces[p]), f"gather_{p}_forward")
        behind = [crossing.token]
        if p + 1 < len(pieces):
            relay_gather(p + 1)
        if p + 3 < len(pieces):
            start_gather(p + 3)
        behind = behind + list(started().ins)
        if meanwhile is not None:
            behind = behind + meanwhile(_after(crossing.token))
        _split_wait(crossing, behind, f"gather_{p}_forwarded")

    _run(cast("out00"))
    start_gather(0)
    mod_cols = _run(lambda phases: _mod_fwd(c_all, w_mod, b_cols, phases), started())[0]

    def mod_rows(outs):
        small["mod"] = outs

    _run(cast("about"), started())
    early = [u for piece in pieces[2:4] for u in piece]
    for u in early:
        _run(cast(u), started())
    _run(
        cast("abin"), _phase_small_exchange(mod_cols.transpose(1, 0, 2), mod_rows),
        started(), _after(*[weight[u] for u in early]),
    )
    relay_gather(0)
    start_gather(1)
    start_gather(2)
    for u in pieces[4]:
        _run(cast(u), started())
    mod_mine = small["mod"][0][0::2]
    mod = mod_mine.transpose(1, 0, 2).reshape(n_layers, 3, 3, d)
    vecs = {
        (l, sub): jnp.pad(norm_full[3 * l + sub][None], ((0, 7), (0, 0))) + jnp.pad(mod[l, sub], ((1, 4), (0, 0)))
        for l in range(n_layers)
        for sub in range(3)
    }
    b_rows = jnp.broadcast_to(ab_b_s[0].T[:, :, None], (chunk, heads, da // heads)).reshape(chunk, da)

    saved = {}

    def ffn_forward(xs, l, sub, k, *phases):
        saved[l, sub, "x"] = xs
        xs, gg, uu, yb = _run(
            lambda ph: _ffn_fwd(xs, vecs[l, sub], w_of(f"in{l}{k}"), w_of(f"out{l}{k}"), f"ffn_fwd_{l}{k}", ph), *phases
        )
        saved[l, sub, "act"] = (gg, uu, yb)
        return xs

    finish_gather(0, [vecs[0, 0]] + [weight[u] for u in pieces[4]])
    xs = ffn_forward(x0, 0, 0, 0, started())
    saved[0, 1, "x"] = xs
    finish_gather(1, [xs])
    (proj,) = _run(lambda ph: _proj_mod_fwd(xs, vecs[0, 1], w_of("abin"), ph), started())
    (cat,) = _run(lambda ph: _ab_mix_fwd(proj, ab_norm_v, ab_w_s[0], b_rows, conv_full, ph))
    xs, yb = _run(lambda ph: _proj_res_fwd(cat, w_of("about"), xs, vecs[0, 1], ph))
    saved[0, 1, "act"] = (proj, cat, yb)
    finish_gather(2, [xs])
    xs = ffn_forward(xs, 0, 2, 1, started())
    finish_gather(3, [xs])
    xs = ffn_forward(xs, 1, 0, 0, started())
    saved[1, 1, "x"] = xs
    pooled = []

    def pool_forward(behind):
        pooled.extend(_run(lambda ph: _pool_fwd(xs, vecs[1, 1], w_of("pool"), pool_scale_full, ph), behind))
        return [pooled[0]]

    finish_gather(4, [xs], pool_forward)
    xs, pp, oo = pooled
    saved[1, 1, "act"] = (pp, oo)
    xs = ffn_forward(xs, 1, 2, 1)
    dxs, aux = _run(lambda ph: _loss_head(xs, final_g.reshape(1, d), target, ph))

    grad = {}
    recv = {}
    csum = {}
    parts = {}
    reduced = {}
    done = set()
    dvecs, small_g = {}, {}

    def pair_exchange(*us):
        def then(outs):
            for u, o in zip(us, outs):
                recv[u] = o

        return _phase_pair_exchange([grad[u] for u in us], [big[u] for u in us], then)

    def grad_half(u, a, bs, mine, name, *phases):
        (res,) = _run(lambda ph: _grad_half(a, bs, big[u], where, mine, recv[u] if mine else None, name, ph), *phases)
        return res

    def pair_sum(u, *phases):
        def launch(ph):
            (csum[u],), p_outs = _pair_sum(grad[u], recv[u], big[u], where, "pair_sum_" + u, ph)
            return None, p_outs

        _run(launch, *phases)

    def chip_exchange(*us):
        def then(outs):
            for u, o in zip(us, outs):
                parts[u] = o

        return _phase_chip_exchange([csum[u] for u in us], [big[u] for u in us], then)

    def chip_sum(*us, carried=()):
        for n_u, u in enumerate(us):
            g, st, b0 = units[u]

            def launch(ph):
                (reduced[st],), p_outs = _chip_sum(
                    csum[u], parts[u], g, where, reduced.get(st), stacks[st][0].shape, b0, "chip_sum_" + u, ph
                )
                return None, p_outs

            _run(launch, *(carried if n_u == 0 else ()))

    def pair_broadcast(*us):
        sts = [units[u][1] for u in us]
        assert len(set(sts)) == len(sts)

        def then(outs):
            for u, st, o in zip(us, sts, outs):
                reduced[st] = o
                done.add(u)

        return _phase_pair_broadcast([reduced[st] for st in sts], [big[u] for u in us], [units[u][2] for u in us], then)

    def ffn_backward(dxs, l, sub, k, carried_bwd, carried_send, carried_mine):
        gg, uu, yb = saved[l, sub, "act"]
        w_in, w_out = w_of(f"in{l}{k}"), w_of(f"out{l}{k}")
        uo, ui, tag = f"out{l}{k}", f"in{l}{k}", f"{l}{k}"
        dxs, dg, du, a, h, dy, dvecs[l, sub] = _run(
            lambda ph: _ffn_bwd(dxs, saved[l, sub, "x"], vecs[l, sub], gg, uu, yb, w_in, w_out, "ffn_bwd_" + tag, ph), *carried_bwd()
        )
        grad[uo] = grad_half(uo, a, [dy], False, "dw_out_send_" + tag, *carried_send())
        grad[ui] = grad_half(ui, h, [dg, du], False, "dw_in_send_" + tag, pair_exchange(uo))
        csum[uo] = grad_half(uo, a, [dy], True, "dw_out_" + tag, pair_exchange(ui))
        csum[ui] = grad_half(ui, h, [dg, du], True, "dw_in_" + tag, *carried_mine())
        return dxs

    none = lambda: ()
    dxs = ffn_backward(dxs, 1, 2, 1, none, none, none)
    pp, oo = saved[1, 1, "act"]
    dxs, grad["pool"], small_g["pool_scale"], dvecs[1, 1] = _run(
        lambda ph: _pool_bwd(dxs, saved[1, 1, "x"], vecs[1, 1], pp, oo, w_of("pool"), pool_scale_full, ph)
    )

    def after_11():
        return (chip_exchange("in11", "out11"), pair_exchange("pool"))

    def bcast_11():
        chip_sum("in11", "out11")
        pair_sum("pool")
        return (pair_broadcast("in11", "out11"), chip_exchange("pool"))

    dxs = ffn_backward(dxs, 1, 0, 0, after_11, bcast_11, none)

    def after_10():
        return (chip_exchange("in10", "out10"),)

    def bcast_10():
        chip_sum("in10", "out10", "pool")
        return (pair_broadcast("in10", "out10", "pool"),)

    dxs = ffn_backward(dxs, 0, 2, 1, after_10, bcast_10, none)

    proj, cat, yb = saved[0, 1, "act"]
    out01 = _split_start(chip_exchange("out01"), "reduce_out01_start")
    dy, dcat, dgate = _run(lambda ph: _proj_res_bwd(dxs, yb, vecs[0, 1], w_of("about"), ph), _after(out01.token))
    grad["about"] = grad_half("about", cat, [dy], False, "dw_ab_out_send")
    dproj, small_g["ab_norm_v"], small_g["ab_w_s"], dzs, small_g["ab_conv_w"] = _run(
        lambda ph: _ab_mix_bwd(proj, dcat, ab_norm_v, ab_w_s[0], b_rows, conv_full, ph), pair_exchange("about")
    )
    small_g["ab_b_s"] = dzs.reshape(chunk, heads, da // heads).sum(axis=2).T
    dxs, h, dvecs[0, 1] = _run(
        lambda ph: _proj_mod_bwd(dproj[None], w_of("abin"), saved[0, 1, "x"], vecs[0, 1], dxs, dgate, "ab_in_bwd", ph)
    )
    grad["abin"] = grad_half("abin", h, [dproj], False, "dw_ab_in_send")
    (csum["out01"],) = _split_wait(out01, [grad["abin"]], "reduce_out01_wait")
    chip_sum("out01")
    csum["about"] = grad_half("about", cat, [dy], True, "dw_ab_out", pair_broadcast("out01"), pair_exchange("abin"))
    csum["abin"] = grad_half("abin", h, [dproj], True, "dw_ab_in")

    layout = {}
    tail = {}

    def after_01():
        tail["01"] = _split_start(chip_exchange("in01", "abin", "about"), "reduce_01_start")
        return (_after(tail["01"].token),)

    def pack_small_grads():
        dvec_all = jnp.stack([dvecs[l, sub] for l in range(n_layers) for sub in range(3)])
        dgain = dvec_all[:, 0, :]
        dmod = dvec_all[:, 1:4, :].reshape(3 * 3 * n_layers, d)
        rows = {
            "norm_g": (dgain, None, dq), "final_g": (aux[0:1], 0, d), "pool_scale": (small_g["pool_scale"], None, dq),
            "b_mod": (dmod, 0, d), "ab_norm_v": (small_g["ab_norm_v"], 0, da),
            "ab_conv_w": (small_g["ab_conv_w"], None, db // N_CHIPS), "ab_b_s": (small_g["ab_b_s"], 0, chunk),
            "loss": (aux[1:2], 0, d),
        }
        row0 = 0
        for nm, (pc, col0, cols) in rows.items():
            layout[nm] = (row0, pc.shape[0], col0, cols)
            row0 += pc.shape[0]
        packed_rows = -(-row0 // 8) * 8
        return sum(
            jnp.pad(pc, ((layout[nm][0], packed_rows - layout[nm][0] - pc.shape[0]), (0, d - pc.shape[1])))
            for nm, (pc, _, _) in rows.items()
        )

    def bcast_01():
        csum["in01"], csum["abin"], csum["about"] = _split_wait(tail["01"], [dvecs[0, 0]], "reduce_01_wait")
        chip_sum("in01", "abin", "about")
        grads_small = [pack_small_grads(), small_g["ab_w_s"].reshape(heads * chunk, chunk)]
        tail["small"] = _split_start(small_gather("grads", grads_small), "gather_small_grads_start")
        return (pair_broadcast("in01", "abin", "about"), _after(tail["small"].token))

    def reduce_out00():
        tail["out00"] = _split_start(chip_exchange("out00"), "reduce_out00_start")
        return (_after(tail["out00"].token),)

    dxs = ffn_backward(dxs, 0, 0, 0, after_01, bcast_01, reduce_out00)
    grad_x = dxs.reshape(x.shape)

    last = _split_start(chip_exchange("in00"), "reduce_last_start")
    _split_wait(tail["small"], [last.token], "gather_small_grads_wait")
    g_all, gws_all = small["grads"]

    out = {}

    def adam_stack(st, after=()):
        w3, m3, v3 = stacks[st]
        assert all(u in done for u, (_, ust, _) in units.items() if ust == st), st
        shape = {"w_ffn_in": w_ffn_in.shape, "w_ffn_out": w_ffn_out.shape, "pool_w_grp": pool_w_grp.shape}.get(st, w3.shape)
        out[st] = tuple(a.reshape(shape) for a in _adam_stack(w3, reduced[st], m3, v3, "adam_" + st, after))

    shapes2d = {
        "norm_g": (3 * n_layers, dq), "b_mod": (9 * n_layers, d), "final_g": (1, d), "ab_norm_v": (1, da),
        "pool_scale": (1, dq), "ab_conv_w": (3, db // N_CHIPS), "ab_b_s": (heads, chunk), "ab_w_s": (heads * chunk, chunk),
    }
    small_w = {"norm_g": (norm_g, m_norm_g, v_norm_g), "b_mod": (b_mod, m_b_mod, v_b_mod), "final_g": (final_g, m_final_g, v_final_g),
               "ab_norm_v": (ab_norm_v, m_ab_norm_v, v_ab_norm_v), "pool_scale": (pool_scale, m_pool_scale, v_pool_scale),
               "ab_conv_w": (ab_conv_w, m_ab_conv_w, v_ab_conv_w), "ab_b_s": (ab_b_s, m_ab_b_s, v_ab_b_s), "ab_w_s": (ab_w_s, m_ab_w_s, v_ab_w_s)}
    smalls = {nm: tuple(a.reshape(shapes2d[nm]) for a in wmv) for nm, wmv in small_w.items()}
    small_out, loss = _small_adam(g_all, gws_all, layout, smalls, chip)
    loss = loss.reshape(())
    for nm, res in small_out.items():
        out[nm] = tuple(a.reshape(small_w[nm][0].shape) for a in res)

    mod_row0 = layout["b_mod"][0]
    dmod_all = g_all[:, mod_row0 : mod_row0 + 9 * n_layers, :].reshape(N_DEV, n_layers, 9 * d)
    dmod_cols = lax.dynamic_slice(dmod_all, (0, 0, chip * ncol), (N_DEV, n_layers, ncol)).transpose(1, 0, 2)
    out["w_mod"] = tuple(_mod_bwd_adam(c_all.T, dmod_cols, w_mod, m_w_mod, v_w_mod, (last.token,)))

    (csum["out00"],) = _split_wait(tail["out00"], [out["w_mod"][1]], "reduce_out00_wait")
    chip_sum("out00")
    crossing = _split_start(pair_broadcast("out00"), "broadcast_out00_start")
    for st in ("ab_w_in", "ab_w_out", "pool_w_grp"):
        adam_stack(st, (crossing.token,))
    _split_wait(crossing, [out[st][1] for st in ("ab_w_in", "ab_w_out", "pool_w_grp")], "broadcast_out00_wait")
    (csum["in00"],) = _split_wait(last, [reduced["w_ffn_out"]], "reduce_last_wait")
    chip_sum("in00")
    crossing = _split_start(pair_broadcast("in00"), "broadcast_last_start")
    adam_stack("w_ffn_out", (crossing.token,))
    _split_wait(crossing, [out["w_ffn_out"][1]], "broadcast_last_wait")
    adam_stack("w_ffn_in")

    order = ["norm_g", "w_mod", "b_mod", "w_ffn_in", "w_ffn_out", "ab_w_in", "ab_norm_v", "ab_w_s", "ab_b_s", "ab_conv_w", "ab_w_out", "pool_w_grp", "pool_scale", "final_g"]
    return (loss, grad_x, *[out[nm][0] for nm in order], *[out[nm][1] for nm in order], *[out[nm][2] for nm in order], *[out[nm][3] for nm in order])
```

```python
import functools
import math

import jax
import jax.numpy as jnp
from jax import lax
from jax.experimental import pallas as pl
from jax.experimental.pallas import tpu as pltpu

F32 = jnp.float32
BF16 = jnp.bfloat16
MESH = pl.DeviceIdType.MESH

EPS = 1e-6
ADAM_LR = 0.001
ADAM_B1 = 0.9
ADAM_B2 = 0.999
ADAM_EPS = 1e-08
ADAM_WD = 0.01
ADAM_STEP = 10
POOL_WINDOWS = (2, 4, 8, 16)
POOL_HALO = 16
CONV_HALO = 8
N_CHIPS = 4
N_DEV = 8
VMEM_LIMIT_BYTES = 48 * 1024 * 1024
EW_BLOCK_ELEMS = 1024 * 1024
ADAM_BLOCK_ELEMS = 512 * 1024


def _pick(n, prefs):
    for p in prefs:
        if p <= n and n % p == 0:
            return p
    return n


def _row_tile(rows, cols, block_elems=EW_BLOCK_ELEMS):
    best = None
    for d in range(16, rows + 1, 16):
        if rows % d == 0 and d * cols <= block_elems:
            best = d
    return best or rows


def _dot(a, b):
    return jnp.dot(a, b, preferred_element_type=F32)


def _dot_nt(a, b):
    return lax.dot_general(a, b, (((1,), (1,)), ((), ())), preferred_element_type=F32)


def _dot_tn(a, b):
    return lax.dot_general(a, b, (((0,), (0,)), ((), ())), preferred_element_type=F32)


def _sigmoid(x):
    return 0.5 * jnp.tanh(0.5 * x) + 0.5


_GELU_C = math.sqrt(2.0 / math.pi)


def _gelu(x):
    x2 = x * x
    t = jnp.tanh(_GELU_C * (x + 0.044715 * x2 * x))
    val = 0.5 * x * (1.0 + t)
    grad = 0.5 * (1.0 + t) + 0.5 * x * (1.0 - t * t) * (_GELU_C * (1.0 + 3.0 * 0.044715 * x2))
    return val, grad


def _rstd(x):
    return lax.rsqrt(jnp.mean(x * x, axis=-1, keepdims=True) + EPS)


def _modulate(x, vec_ref):
    return (x * _rstd(x)) * vec_ref[0:1, :] * (1.0 + vec_ref[2:3, :]) + vec_ref[1:2, :]


def _modulate_bwd(x, dh, vec_ref, dvec_ref):
    gn, sh, sc = vec_ref[0:1, :], vec_ref[1:2, :], vec_ref[2:3, :]
    rstd = _rstd(x)
    r = x * rstd
    dvec_ref[0:1, :] += jnp.sum(dh * r * (1.0 + sc), axis=0, keepdims=True)
    dvec_ref[1:2, :] += jnp.sum(dh, axis=0, keepdims=True)
    dvec_ref[2:3, :] += jnp.sum(dh * r * gn, axis=0, keepdims=True)
    gm = gn * (1.0 + sc)
    dr = dh * gm
    dx = rstd * (dr - r * jnp.mean(dr * r, axis=-1, keepdims=True))
    return dx, r * gm + sh


def _adam(w, g, m, v):
    m = ADAM_B1 * m + (1.0 - ADAM_B1) * g
    v = ADAM_B2 * v + (1.0 - ADAM_B2) * (g * g)
    m_hat = m / (1.0 - ADAM_B1**ADAM_STEP)
    v_hat = v / (1.0 - ADAM_B2**ADAM_STEP)
    delta = -ADAM_LR * (m_hat / (jnp.sqrt(v_hat) + ADAM_EPS) + ADAM_WD * w)
    return delta, m, v


_ANY = pl.BlockSpec(memory_space=pl.ANY)


class _Phase:
    def __init__(self, ins, out_shapes, aliases, n_sems, start, finish, then):
        self.ins, self.out_shapes, self.aliases, self.n_sems = list(ins), list(out_shapes), dict(aliases), n_sems
        self.start, self.finish, self.then = start, finish, then


def _call(body, name, grid, in_specs, out_specs, out_shape, ins, scratch=(), prefetch=(), phases=(), in_place=None):
    n_pre, n_in, n_out, n_sc = len(prefetch), len(in_specs), len(out_specs), len(scratch)
    ph_in = [len(p.ins) for p in phases]
    ph_out = [len(p.out_shapes) for p in phases]

    def kernel_body(*refs):
        pos = [0]

        def take(k):
            pos[0] += k
            return refs[pos[0] - k : pos[0]]

        pre, ins_ = take(n_pre), take(n_in)
        p_ins = [take(k) for k in ph_in]
        outs_ = take(n_out)
        p_outs = [take(k) for k in ph_out]
        sc = take(n_sc)
        sems = [take(2) for _ in phases]
        if phases:
            ids = [pl.program_id(a) for a in range(len(grid))]
            first = functools.reduce(jnp.logical_and, [i == 0 for i in ids])
            last = functools.reduce(jnp.logical_and, [i == g - 1 for i, g in zip(ids, grid)])

            @pl.when(first)
            def _():
                for p, pi, po, (send, recv) in zip(phases, p_ins, p_outs, sems):
                    p.start(pi, po, send, recv)

        if body is not None:
            body(*pre, *ins_, *outs_, *sc)
        if phases:

            @pl.when(last)
            def _():
                for p, pi, po, (send, recv) in zip(phases, p_ins, p_outs, sems):
                    p.finish(pi, po, send, recv)

    aliases = {n_pre + i: o for i, o in (in_place or {}).items()}
    i0, o0 = n_pre + n_in, n_out
    for p in phases:
        for i, o in p.aliases.items():
            aliases[i0 + i] = o0 + o
        i0 += len(p.ins)
        o0 += len(p.out_shapes)
    all_in = list(in_specs) + [_ANY] * sum(ph_in)
    all_out = list(out_specs) + [_ANY] * sum(ph_out)
    all_scratch = list(scratch)
    for p in phases:
        all_scratch += [pltpu.SemaphoreType.DMA((p.n_sems,)), pltpu.SemaphoreType.DMA((p.n_sems,))]
    shapes = list(out_shape) + [s for p in phases for s in p.out_shapes]
    operands = list(prefetch) + list(ins) + [a for p in phases for a in p.ins]
    sem = ("arbitrary",) * len(grid)
    params = pltpu.CompilerParams(dimension_semantics=sem, vmem_limit_bytes=VMEM_LIMIT_BYTES)
    if n_pre:
        res = pl.pallas_call(
            kernel_body, name=name, out_shape=shapes, input_output_aliases=aliases, compiler_params=params,
            grid_spec=pltpu.PrefetchScalarGridSpec(
                num_scalar_prefetch=n_pre, grid=grid, in_specs=all_in, out_specs=all_out, scratch_shapes=all_scratch
            ),
        )(*operands)
    else:
        res = pl.pallas_call(
            kernel_body, name=name, grid=grid, in_specs=all_in, out_specs=all_out, out_shape=shapes,
            scratch_shapes=all_scratch, input_output_aliases=aliases, compiler_params=params,
        )(*operands)
    res = list(res)
    outs, rest = res[:n_out], res[n_out:]
    p_res = []
    for k in ph_out:
        p_res.append(rest[:k])
        rest = rest[k:]
    return outs, p_res


def _place():
    return lax.axis_index("x"), lax.axis_index("y"), lax.axis_index("c")


def _other_chips():
    x, y, _ = _place()
    return [(1 - x, y), (x, 1 - y), (1 - x, 1 - y)]


def _flip(k):
    x, y, c = _place()
    return (1 - x if k & 4 else x, 1 - y if k & 2 else y, 1 - c if k & 1 else c)


def _remote(src, dst, send, recv, k, to):
    return pltpu.make_async_remote_copy(
        src_ref=src, dst_ref=dst, send_sem=send.at[k], recv_sem=recv.at[k], device_id=to, device_id_type=MESH
    )


def _phase_small_gather(arrs, then):
    n = len(arrs)

    def copies(ins, outs, send, recv):
        x, y, c = _place()
        me = 4 * x + 2 * y + c
        local = [pltpu.make_async_copy(ins[a], outs[a].at[me], send.at[a * N_DEV]) for a in range(n)]
        remote = [_remote(ins[a], outs[a].at[me], send, recv, a * N_DEV + k, _flip(k)) for a in range(n) for k in range(1, N_DEV)]
        return local, remote

    def start(ins, outs, send, recv):
        local, remote = copies(ins, outs, send, recv)
        for cp in local + remote:
            cp.start()

    def finish(ins, outs, send, recv):
        local, remote = copies(ins, outs, send, recv)
        for cp in remote + local:
            cp.wait()

    shapes = [jax.ShapeDtypeStruct((N_DEV,) + a.shape, a.dtype) for a in arrs]
    return _Phase(arrs, shapes, {}, n * N_DEV, start, finish, then)


def _phase_small_exchange(arr, then):
    def copies(ins, outs, send, recv):
        x, y, c = _place()
        me = 4 * x + 2 * y + c
        local = pltpu.make_async_copy(ins[0].at[me], outs[0].at[me], send.at[0])
        remote = []
        for k in range(1, N_DEV):
            px, py, pc = _flip(k)
            remote.append(_remote(ins[0].at[4 * px + 2 * py + pc], outs[0].at[me], send, recv, k, (px, py, pc)))
        return [local] + remote

    def start(ins, outs, send, recv):
        for cp in copies(ins, outs, send, recv):
            cp.start()

    def finish(ins, outs, send, recv):
        for cp in copies(ins, outs, send, recv):
            cp.wait()

    return _Phase([arr], [jax.ShapeDtypeStruct(arr.shape, arr.dtype)], {}, N_DEV, start, finish, then)


def _after(*arrs):
    nothing = lambda *args: None
    return _Phase(arrs, [], {}, 1, nothing, nothing, nothing)


def _flush(name, *phases):
    _, p_outs = _call(None, name, (1,), [], [], [], [], phases=list(phases))
    for p, po in zip(phases, p_outs):
        p.then(po)


class _Big:
    KINDS = {"full": (True, True), "half": (True, False), "shard": (False, True), "block": (False, False)}

    def __init__(self, f3, s3, h3):
        assert s3 != h3
        self.f3, self.s3, self.h3 = tuple(f3), s3, h3
        self.bd = tuple(f3[a] // (N_CHIPS if a == s3 else 1) // (2 if a == h3 else 1) for a in range(3))
        self.tile = (1, _row_tile(self.bd[1], self.bd[2]), self.bd[2])
        self.grid = tuple(self.bd[a] // self.tile[a] for a in range(3))

    def dims(self, kind):
        chips, halves = self.KINDS[kind]
        return tuple(
            self.bd[a] * (N_CHIPS if chips and a == self.s3 else 1) * (2 if halves and a == self.h3 else 1) for a in range(3)
        )

    def view(self, ref, chip=None, half=None, batch0=0, both_halves=True, part=None):
        start = [batch0, 0, 0]
        size = list(ref.shape)
        size[0] = self.bd[0] * (2 if self.h3 == 0 and both_halves else 1)
        if chip is not None:
            start[self.s3] += chip * self.bd[self.s3]
            size[self.s3] = self.bd[self.s3]
        if half is not None:
            start[self.h3] += half * self.bd[self.h3]
            size[self.h3] = self.bd[self.h3]
        if part is not None:
            size[1] //= 2
            start[1] += part * size[1]
        return ref.at[tuple(pl.ds(st, sz) for st, sz in zip(start, size))]

    def spec(self, chip_from=None, half_from=None, lead=(), batch0=0):
        extra = "grid" in (chip_from, half_from)

        def index(*args):
            pref, idx = args[-1], list(args[int(extra) : -1])
            idx[0] += batch0
            if chip_from:
                idx[self.s3] += (pref[0] if chip_from == "pref" else args[0]) * self.grid[self.s3]
            if half_from:
                idx[self.h3] += (pref[1] if half_from == "pref" else args[0]) * self.grid[self.h3]
            return (0,) * len(lead) + tuple(idx)

        return pl.BlockSpec(tuple(lead) + self.tile, index)


def _same(arrs):
    return [jax.ShapeDtypeStruct(a.shape, a.dtype) for a in arrs]


def _phase_gather_relay(arrs, bigs, second, whole_first, then):
    n = len(arrs)
    per = 4 if second and not whole_first else 2

    def copies(outs, send, recv, arriving):
        x, y, c = _place()
        me, xn, yn, dg = (x, y), (1 - x, y), (x, 1 - y), (1 - x, 1 - y)
        if not second:
            part = (None, None) if whole_first else (0, 1)
            plan = [((xn if arriving else me), part[0], xn), ((yn if arriving else me), part[1], yn)]
        elif whole_first:
            plan = [(dg, 0, yn), (dg, 1, xn)] if arriving else [(xn, 0, yn), (yn, 1, xn)]
        elif arriving:
            plan = [(yn, 0, yn), (dg, 0, yn), (xn, 1, xn), (dg, 1, xn)]
        else:
            plan = [(me, 0, yn), (xn, 0, yn), (me, 1, xn), (yn, 1, xn)]
        res = []
        for a in range(n):
            for k, (chip, part, to) in enumerate(plan):
                blk = bigs[a].view(outs[a], 2 * chip[0] + chip[1], c, part=part)
                res.append(_remote(blk, blk, send, recv, per * a + k, (*to, c)))
        return res

    def start(ins, outs, send, recv):
        for cp in copies(outs, send, recv, False):
            cp.start()

    def finish(ins, outs, send, recv):
        for cp in copies(outs, send, recv, True):
            cp.wait_recv()
        for cp in copies(outs, send, recv, False):
            cp.wait_send()

    return _Phase(arrs, _same(arrs), {a: a for a in range(n)}, per * n, start, finish, then)


def _phase_gather_sibling(arrs, bigs, then):
    n = len(arrs)

    def copies(outs, send, recv, arriving):
        x, y, c = _place()
        return [
            _remote(blk, blk, send, recv, 3 * a + j, (x, y, 1 - c))
            for j, chip in enumerate(_other_chips())
            for a in range(n)
            for blk in [bigs[a].view(outs[a], 2 * chip[0] + chip[1], 1 - c if arriving else c)]
        ]

    def start(ins, outs, send, recv):
        for cp in copies(outs, send, recv, False):
            cp.start()

    def finish(ins, outs, send, recv):
        for cp in copies(outs, send, recv, True):
            cp.wait_recv()
        for cp in copies(outs, send, recv, False):
            cp.wait_send()

    return _Phase(arrs, _same(arrs), {a: a for a in range(n)}, 3 * n, start, finish, then)


def _phase_pair_exchange(grads, bigs, then):
    n = len(grads)

    def copies(ins, outs, send, recv):
        x, y, c = _place()
        srcs = [ins[a] if ins[a].shape == outs[a].shape else bigs[a].view(ins[a], None, 1 - c) for a in range(n)]
        return [_remote(srcs[a], outs[a], send, recv, a, (x, y, 1 - c)) for a in range(n)]

    def start(ins, outs, send, recv):
        for cp in copies(ins, outs, send, recv):
            cp.start()

    def finish(ins, outs, send, recv):
        for cp in copies(ins, outs, send, recv):
            cp.wait()

    shapes = [jax.ShapeDtypeStruct(b.dims("half"), BF16) for b in bigs]
    return _Phase(grads, shapes, {}, n, start, finish, then)


def _phase_chip_exchange(sums, bigs, then):
    n = len(sums)

    def copies(ins, outs, send, recv):
        _, _, c = _place()
        return [
            _remote(bigs[a].view(ins[a], 2 * chip[0] + chip[1], both_halves=False), outs[a].at[j], send, recv, 3 * a + j, (*chip, c))
            for j, chip in enumerate(_other_chips())
            for a in range(n)
        ]

    def start(ins, outs, send, recv):
        for cp in copies(ins, outs, send, recv):
            cp.start()

    def finish(ins, outs, send, recv):
        for cp in copies(ins, outs, send, recv):
            cp.wait()

    shapes = [jax.ShapeDtypeStruct((N_CHIPS - 1,) + b.dims("block"), BF16) for b in bigs]
    return _Phase(sums, shapes, {}, 3 * n, start, finish, then)


_HBM = pl.BlockSpec(memory_space=pltpu.HBM)
_SEM = pl.BlockSpec(memory_space=pltpu.SEMAPHORE)
_DATAFLOW = pltpu.SideEffectType.DATAFLOW_SIDE_EFFECTING


class _InFlight:
    def __init__(self, phase, send, recv, arrays, token):
        self.phase, self.send, self.recv, self.arrays, self.token = phase, send, recv, arrays, token


def _phase_results(phase, refs):
    n_in = len(phase.ins)
    updated = {o: i for i, o in phase.aliases.items()}
    fresh = [o for o in range(len(phase.out_shapes)) if o not in updated]
    return [refs[updated[o]] if o in updated else refs[n_in + fresh.index(o)] for o in range(len(phase.out_shapes))]


def _split_start(phase, name):
    n_in = len(phase.ins)
    fresh = [s for o, s in enumerate(phase.out_shapes) if o not in phase.aliases.values()]
    arrays = list(phase.ins) + [lax.empty(s.shape, s.dtype) for s in fresh]
    n = len(arrays)

    def body(*refs):
        phase.start(refs[:n_in], _phase_results(phase, refs[:n]), refs[n], refs[n + 1])
        refs[-1][...] = jnp.zeros_like(refs[-1])

    operands = [pltpu.with_memory_space_constraint(a, pltpu.HBM) for a in arrays]
    res = pl.pallas_call(
        body, name=name,
        out_shape=[pltpu.SemaphoreType.DMA((phase.n_sems,)), pltpu.SemaphoreType.DMA((phase.n_sems,))]
        + [pltpu.HBM(a.shape, a.dtype) for a in arrays] + [jax.ShapeDtypeStruct((8, 128), F32)],
        in_specs=[_HBM] * n, out_specs=[_SEM, _SEM] + [_HBM] * n + [pl.BlockSpec(memory_space=pltpu.VMEM)],
        input_output_aliases={i: 2 + i for i in range(n)},
        compiler_params=pltpu.CompilerParams(has_side_effects=_DATAFLOW),
    )(*operands)
    return _InFlight(phase, res[0], res[1], list(res[2 : 2 + n]), res[-1])


def _split_wait(flight, after, name):
    phase, n = flight.phase, len(flight.arrays)
    n_in = len(phase.ins)

    def body(*refs):
        phase.finish(refs[:n_in], _phase_results(phase, refs[:n]), refs[n], refs[n + 1])

    res = pl.pallas_call(
        body, name=name, out_shape=[pltpu.HBM(a.shape, a.dtype) for a in flight.arrays],
        in_specs=[_HBM] * n + [_SEM, _SEM] + [_ANY] * len(after), out_specs=[_HBM] * n,
        input_output_aliases={i: i for i in range(n)},
        compiler_params=pltpu.CompilerParams(has_side_effects=_DATAFLOW),
    )(*flight.arrays, flight.send, flight.recv, *after)
    res = list(res)
    phase.then(_phase_results(phase, res))
    return res[:n_in]


def _phase_pair_broadcast(stacks, bigs, batch0s, then):
    n = len(stacks)

    def start(ins, outs, send, recv):
        x, y, c = _place()
        for a in range(n):
            blk = bigs[a].view(outs[a], None, c, batch0s[a])
            _remote(blk, blk, send, recv, a, (x, y, 1 - c)).start()

    def finish(ins, outs, send, recv):
        x, y, c = _place()
        for a in range(n):
            mine = bigs[a].view(outs[a], None, c, batch0s[a])
            theirs = bigs[a].view(outs[a], None, 1 - c, batch0s[a])
            _remote(mine, mine, send, recv, a, (x, y, 1 - c)).wait_send()
            _remote(theirs, theirs, send, recv, a, (x, y, 1 - c)).wait_recv()

    return _Phase(stacks, _same(stacks), {a: a for a in range(n)}, n, start, finish, then)


def _tile_call(body, name, big, where, extra, ins, in_specs, out_specs, out_shape, phases=()):
    grid = ((extra,) if extra else ()) + big.grid
    return _call(body, name, grid, in_specs, out_specs, out_shape, ins, prefetch=(where,), phases=phases)


def _cast_into_full(w_stack, batch0, big, where, name, phases=()):
    def body(_, w_ref, o_ref):
        o_ref[...] = w_ref[...].astype(BF16)

    return _tile_call(
        body, name, big, where, 2, [w_stack], [big.spec(None, "grid", batch0=batch0)], [big.spec("pref", "grid")],
        [jax.ShapeDtypeStruct(big.dims("full"), BF16)], phases,
    )


def _pair_sum(g_full, recv_half, big, where, name, phases=()):
    def body(_, g_ref, r_ref, o_ref):
        o_ref[...] = (g_ref[...].astype(F32) + r_ref[...].astype(F32)).astype(BF16)

    half = big.spec("grid", None)
    return _tile_call(
        body, name, big, where, N_CHIPS, [g_full, recv_half], [big.spec("grid", "pref"), half], [half],
        [jax.ShapeDtypeStruct(big.dims("half"), BF16)], phases,
    )


def _chip_sum(chip_sum, parts, big, where, stack, stack_shape, batch0, name, phases=()):
    def body(_, own_ref, p_ref, *rest):
        acc = own_ref[...].astype(F32)
        for k in range(N_CHIPS - 1):
            acc = acc + p_ref[k].astype(F32)
        rest[-1][...] = acc

    ins = [chip_sum, parts] + ([stack] if stack is not None else [])
    in_specs = [big.spec("pref", None), big.spec(None, None, lead=(N_CHIPS - 1,))] + ([_ANY] if stack is not None else [])
    return _call(
        body, name, big.grid, in_specs, [big.spec(None, "pref", batch0=batch0)], [jax.ShapeDtypeStruct(stack_shape, F32)], ins,
        prefetch=(where,), phases=phases, in_place={2: 0} if stack is not None else None,
    )


def _adam_stack(w, g, m, v, name, after=()):
    b, r, c = w.shape
    tr = _row_tile(r, c, ADAM_BLOCK_ELEMS)

    def body(w_ref, g_ref, m_ref, v_ref, *rest):
        go_ref, d_ref, mo_ref, vo_ref = rest[-4:]
        gv = g_ref[...]
        d, mo, vo = _adam(w_ref[...], gv, m_ref[...], v_ref[...])
        go_ref[...] = gv
        d_ref[...] = d
        mo_ref[...] = mo
        vo_ref[...] = vo

    spec = pl.BlockSpec((1, tr, c), lambda bb, i: (bb, i, 0))
    outs, _ = _call(
        body, name, (b, r // tr), [spec] * 4 + [_ANY] * len(after), [spec] * 4, [jax.ShapeDtypeStruct(w.shape, F32)] * 4,
        [w, g, m, v, *after],
    )
    return outs


def _mod_fwd(c_all, w_mod, b_cols, phases=()):
    n_layers, d, n = w_mod.shape
    tn = _pick(n, (768, 512, 384, 256, 128))

    def body(c_ref, w_ref, b_ref, o_ref):
        cv = c_ref[...]
        ca = (cv * _sigmoid(cv)).astype(BF16)
        o_ref[0] = _dot(ca, w_ref[0].astype(BF16)) + b_ref[0]

    return _call(
        body, "mod_fwd", (n_layers, n // tn),
        [
            pl.BlockSpec((N_DEV, d), lambda l, j: (0, 0)),
            pl.BlockSpec((1, d, tn), lambda l, j: (l, 0, j)),
            pl.BlockSpec((1, 1, tn), lambda l, j: (l, 0, j)),
        ],
        [pl.BlockSpec((1, N_DEV, tn), lambda l, j: (l, 0, j))],
        [jax.ShapeDtypeStruct((n_layers, N_DEV, n), F32)], [c_all, w_mod, b_cols], phases=phases,
    )


def _mod_bwd_adam(c_all_t, dmod_cols, w, m, v, after=()):
    n_layers, d, n = w.shape
    tn = _pick(n, (384, 256, 128))

    def body(c_ref, dm_ref, w_ref, m_ref, v_ref, *rest):
        g_ref, d_ref, mo_ref, vo_ref = rest[-4:]
        cv = c_ref[...]
        ca = (cv * _sigmoid(cv)).astype(BF16)
        g = _dot(ca, dm_ref[0].astype(BF16))
        g_ref[0] = g
        dl, mo, vo = _adam(w_ref[0], g, m_ref[0], v_ref[0])
        d_ref[0] = dl
        mo_ref[0] = mo
        vo_ref[0] = vo

    wspec = pl.BlockSpec((1, d, tn), lambda l, j: (l, 0, j))
    outs, _ = _call(
        body, "mod_bwd_adam", (n_layers, n // tn),
        [pl.BlockSpec((d, N_DEV), lambda l, j: (0, 0)), pl.BlockSpec((1, N_DEV, tn), lambda l, j: (l, 0, j)), wspec, wspec, wspec]
        + [_ANY] * len(after),
        [wspec] * 4, [jax.ShapeDtypeStruct(w.shape, F32)] * 4, [c_all_t, dmod_cols, w, m, v, *after],
    )
    return outs


def _ffn_fwd(x, vec, w_in, w_out, name, phases=()):
    s, d = x.shape
    f = w_out.shape[1]
    tm = _pick(s, (1024, 512, 256, 128))
    tf = _pick(f, (256, 128))
    nf = f // tf

    def body(x_ref, vec_ref, wg_ref, wu_ref, wo_ref, xo_ref, g_ref, u_ref, y_ref, h_sc, acc_sc):
        j = pl.program_id(1)

        @pl.when(j == 0)
        def _():
            h_sc[...] = _modulate(x_ref[...], vec_ref).astype(BF16)
            acc_sc[...] = jnp.zeros_like(acc_sc)

        h = h_sc[...]
        g = _dot(h, wg_ref[0])
        u = _dot(h, wu_ref[0])
        g_ref[...] = g.astype(BF16)
        u_ref[...] = u.astype(BF16)
        a = (g * _sigmoid(g) * u).astype(BF16)
        acc_sc[...] += _dot(a, wo_ref[0])

        @pl.when(j == nf - 1)
        def _():
            yv = acc_sc[...]
            xo_ref[...] = x_ref[...] + 0.5 * vec_ref[3:4, :] * yv
            y_ref[...] = yv.astype(BF16)

    row = pl.BlockSpec((tm, d), lambda i, j: (i, 0))
    hid = pl.BlockSpec((tm, tf), lambda i, j: (i, j))
    return _call(
        body, name, (s // tm, nf),
        [
            row,
            pl.BlockSpec((8, d), lambda i, j: (0, 0)),
            pl.BlockSpec((1, d, tf), lambda i, j: (0, 0, j)),
            pl.BlockSpec((1, d, tf), lambda i, j: (0, 0, nf + j)),
            pl.BlockSpec((1, tf, d), lambda i, j: (0, j, 0)),
        ],
        [row, hid, hid, row],
        [
            jax.ShapeDtypeStruct((s, d), F32),
            jax.ShapeDtypeStruct((s, f), BF16),
            jax.ShapeDtypeStruct((s, f), BF16),
            jax.ShapeDtypeStruct((s, d), BF16),
        ],
        [x, vec, w_in, w_in, w_out],
        scratch=[pltpu.VMEM((tm, d), BF16), pltpu.VMEM((tm, d), F32)], phases=phases,
    )


def _ffn_bwd(dxo, x, vec, gg, uu, y, w_in, w_out, name, phases=()):
    s, d = x.shape
    f = w_out.shape[1]
    tm = _pick(s, (512, 256, 128))
    tf = _pick(f, (256, 128))
    nf = f // tf

    def body(dxo_ref, x_ref, vec_ref, g_ref, u_ref, y_ref, wg_ref, wu_ref, wo_ref,
             dx_ref, dg_ref, du_ref, a_ref, h_ref, dy_ref, dvec_ref, acc_sc):
        i, j = pl.program_id(0), pl.program_id(1)

        @pl.when((i == 0) & (j == 0))
        def _():
            dvec_ref[...] = jnp.zeros_like(dvec_ref)

        @pl.when(j == 0)
        def _():
            dxo_v = dxo_ref[...]
            dy_ref[...] = (0.5 * vec_ref[3:4, :] * dxo_v).astype(BF16)
            dvec_ref[3:4, :] += 0.5 * jnp.sum(dxo_v * y_ref[...].astype(F32), axis=0, keepdims=True)
            acc_sc[...] = jnp.zeros_like(acc_sc)

        da = _dot_nt(dy_ref[...], wo_ref[0])
        g = g_ref[...].astype(F32)
        u = u_ref[...].astype(F32)
        sig = _sigmoid(g)
        sl = g * sig
        a_ref[...] = (sl * u).astype(BF16)
        dg = (da * u * (sig * (1.0 + g * (1.0 - sig)))).astype(BF16)
        du = (da * sl).astype(BF16)
        dg_ref[...] = dg
        du_ref[...] = du
        acc_sc[...] += _dot_nt(dg, wg_ref[0]) + _dot_nt(du, wu_ref[0])

        @pl.when(j == nf - 1)
        def _():
            dx, h = _modulate_bwd(x_ref[...], acc_sc[...], vec_ref, dvec_ref)
            dx_ref[...] = dxo_ref[...] + dx
            h_ref[...] = h.astype(BF16)

    row = pl.BlockSpec((tm, d), lambda i, j: (i, 0))
    hid = pl.BlockSpec((tm, tf), lambda i, j: (i, j))
    vecs = pl.BlockSpec((8, d), lambda i, j: (0, 0))
    return _call(
        body, name, (s // tm, nf),
        [
            row, row, vecs, hid, hid, row,
            pl.BlockSpec((1, d, tf), lambda i, j: (0, 0, j)),
            pl.BlockSpec((1, d, tf), lambda i, j: (0, 0, nf + j)),
            pl.BlockSpec((1, tf, d), lambda i, j: (0, j, 0)),
        ],
        [row, hid, hid, hid, row, row, vecs],
        [
            jax.ShapeDtypeStruct((s, d), F32),
            jax.ShapeDtypeStruct((s, f), BF16),
            jax.ShapeDtypeStruct((s, f), BF16),
            jax.ShapeDtypeStruct((s, f), BF16),
            jax.ShapeDtypeStruct((s, d), BF16),
            jax.ShapeDtypeStruct((s, d), BF16),
            jax.ShapeDtypeStruct((8, d), F32),
        ],
        [dxo, x, vec, gg, uu, y, w_in, w_in, w_out],
        scratch=[pltpu.VMEM((tm, d), F32)], phases=phases,
    )


def _grad_half(a, bs, big, where, mine, recv, name, phases=()):
    s, k1 = a.shape
    n = bs[0].shape[1]
    groups = len(bs)
    rows_halved = big.h3 == 1
    assert rows_halved or groups == 1
    kk, nn = (k1 // 2, n) if rows_halved else (k1, n // 2)
    tk = _pick(kk, (1408, 1024, 512, 256, 128))
    tn = _pick(nn, (1408, 1024, 640, 512, 256, 128))
    nkb, nnb = kk // tk, nn // tn
    assert (recv is None) == (not mine)

    def half(pref):
        return pref[1] if mine else 1 - pref[1]

    def body(_, a_ref, *rest):
        q = pl.program_id(1)
        for p in range(groups):

            @pl.when(q == p)
            def _(p=p):
                acc = _dot_tn(a_ref[...], rest[p][...])
                if recv is not None:
                    acc = acc + rest[groups][0].astype(F32)
                rest[-1][0] = acc.astype(BF16)

    def b_block(p):
        def index(i, q, j, pref):
            jj = jnp.where(q == p, j, jnp.where(q < p, 0, nnb - 1))
            return (0, jj + (0 if rows_halved else half(pref) * nnb))

        return pl.BlockSpec((s, tn), index)

    out_spec = pl.BlockSpec((1, tk, tn), lambda i, q, j, pref: (0, i, q * nnb + j))
    in_specs = [pl.BlockSpec((s, tk), lambda i, q, j, pref: (0, i + (half(pref) * nkb if rows_halved else 0)))]
    in_specs += [b_block(p) for p in range(groups)]
    ins = [a, *bs]
    if recv is not None:
        in_specs.append(out_spec)
        ins.append(recv)
    return _call(
        body, name, (nkb, groups, nnb), in_specs, [out_spec], [jax.ShapeDtypeStruct(big.dims("half"), BF16)], ins,
        prefetch=(where,), phases=phases,
    )


def _proj_mod_fwd(x, vec, w, phases=()):
    s, d = x.shape
    n = w.shape[2]
    tm = _pick(s, (1024, 512, 256, 128))
    tn = _pick(n, (640, 512, 256, 128))

    def body(x_ref, vec_ref, w_ref, o_ref, h_sc):
        @pl.when(pl.program_id(1) == 0)
        def _():
            h_sc[...] = _modulate(x_ref[...], vec_ref).astype(BF16)

        o_ref[...] = _dot(h_sc[...], w_ref[0])

    return _call(
        body, "ab_in_fwd", (s // tm, n // tn),
        [
            pl.BlockSpec((tm, d), lambda i, j: (i, 0)),
            pl.BlockSpec((8, d), lambda i, j: (0, 0)),
            pl.BlockSpec((1, d, tn), lambda i, j: (0, 0, j)),
        ],
        [pl.BlockSpec((tm, tn), lambda i, j: (i, j))],
        [jax.ShapeDtypeStruct((s, n), F32)], [x, vec, w],
        scratch=[pltpu.VMEM((tm, d), BF16)], phases=phases,
    )


def _proj_res_fwd(a, w, x, vec, phases=()):
    s, kd = a.shape
    d = x.shape[1]
    tm = _pick(s, (1024, 512, 256, 128))

    def body(a_ref, w_ref, x_ref, vec_ref, xo_ref, y_ref):
        yv = _dot(a_ref[...], w_ref[0])
        xo_ref[...] = x_ref[...] + vec_ref[3:4, :] * yv
        y_ref[...] = yv.astype(BF16)

    row = pl.BlockSpec((tm, d), lambda i: (i, 0))
    return _call(
        body, "ab_out_fwd", (s // tm,),
        [pl.BlockSpec((tm, kd), lambda i: (i, 0)), pl.BlockSpec((1, kd, d), lambda i: (0, 0, 0)), row, pl.BlockSpec((8, d), lambda i: (0, 0))],
        [row, row],
        [jax.ShapeDtypeStruct((s, d), F32), jax.ShapeDtypeStruct((s, d), BF16)], [a, w, x, vec], phases=phases,
    )


def _proj_res_bwd(dxo, y, vec, w, phases=()):
    s, d = dxo.shape
    kd = w.shape[1]
    tm = _pick(s, (1024, 512, 256, 128))

    def body(dxo_ref, y_ref, vec_ref, w_ref, dy_ref, da_ref, dgate_ref):
        @pl.when(pl.program_id(0) == 0)
        def _():
            dgate_ref[...] = jnp.zeros_like(dgate_ref)

        dxo_v = dxo_ref[...]
        dy = (vec_ref[3:4, :] * dxo_v).astype(BF16)
        dy_ref[...] = dy
        dgate_ref[3:4, :] += jnp.sum(dxo_v * y_ref[...].astype(F32), axis=0, keepdims=True)
        da_ref[...] = _dot_nt(dy, w_ref[0]).astype(BF16)

    row = pl.BlockSpec((tm, d), lambda i: (i, 0))
    vecs = pl.BlockSpec((8, d), lambda i: (0, 0))
    return _call(
        body, "ab_out_bwd", (s // tm,),
        [row, row, vecs, pl.BlockSpec((1, kd, d), lambda i: (0, 0, 0))],
        [row, pl.BlockSpec((tm, kd), lambda i: (i, 0)), vecs],
        [jax.ShapeDtypeStruct((s, d), BF16), jax.ShapeDtypeStruct((s, kd), BF16), jax.ShapeDtypeStruct((8, d), F32)],
        [dxo, y, vec, w], phases=phases,
    )


def _proj_mod_bwd(dproj, w, x, vec, dxo, dvec_in, name, phases=()):
    parts, s, n_part = dproj.shape
    d = x.shape[1]
    tm = _pick(s, (512, 256, 128))
    tk = _pick(n_part, (1408, 1280, 1024, 512, 256, 128))
    per_part = n_part // tk
    nk = parts * per_part

    def body(dp_ref, w_ref, x_ref, vec_ref, dxo_ref, dvi_ref, dx_ref, h_ref, dvec_ref, acc_sc):
        i, k = pl.program_id(0), pl.program_id(1)

        @pl.when((i == 0) & (k == 0))
        def _():
            dvec_ref[...] = dvi_ref[...]

        @pl.when(k == 0)
        def _():
            acc_sc[...] = jnp.zeros_like(acc_sc)

        acc_sc[...] += _dot_nt(dp_ref[0], w_ref[0])

        @pl.when(k == nk - 1)
        def _():
            dx, h = _modulate_bwd(x_ref[...], acc_sc[...], vec_ref, dvec_ref)
            dx_ref[...] = dxo_ref[...] + dx
            h_ref[...] = h.astype(BF16)

    row = pl.BlockSpec((tm, d), lambda i, k: (i, 0))
    vecs = pl.BlockSpec((8, d), lambda i, k: (0, 0))
    return _call(
        body, name, (s // tm, nk),
        [
            pl.BlockSpec((1, tm, tk), lambda i, k: (k // per_part, i, k % per_part)),
            pl.BlockSpec((1, d, tk), lambda i, k: (0, 0, k)),
            row, vecs, row, vecs,
        ],
        [row, row, vecs],
        [jax.ShapeDtypeStruct((s, d), F32), jax.ShapeDtypeStruct((s, d), BF16), jax.ShapeDtypeStruct((8, d), F32)],
        [dproj, w, x, vec, dxo, dvec_in], scratch=[pltpu.VMEM((tm, d), F32)], phases=phases,
    )


def _tril(n):
    return lax.broadcasted_iota(jnp.int32, (n, n), 0) >= lax.broadcasted_iota(jnp.int32, (n, n), 1)


def _layernorm_stats(gv):
    mu = jnp.mean(gv, axis=-1, keepdims=True)
    cen = gv - mu
    rstd = lax.rsqrt(jnp.mean(cen * cen, axis=-1, keepdims=True) + EPS)
    return cen * rstd, rstd


def _shift_down(q, k, above_ref, c_cg, c_xb, first):
    width = q.shape[1]
    rows = lax.broadcasted_iota(jnp.int32, q.shape, 0)
    out = pltpu.roll(q, k, 0)
    for r in range(k):
        src = CONV_HALO - k + r
        above = above_ref[src : src + 1, c_cg : c_cg + width] * above_ref[src : src + 1, c_xb : c_xb + width]
        above = jnp.where(first, 0.0, above)
        out = jnp.where(rows == r, above, out)
    return out


def _ab_mix_fwd(proj, norm_v, w_s, b_rows, conv_w, phases=()):
    s, n = proj.shape
    heads, chunk, _ = w_s.shape
    da = norm_v.shape[1]
    hd = da // heads
    db = conv_w.shape[1]
    tm = _pick(s, (512, 256, 128))

    def body(p_ref, ph_ref, nv_ref, ws_ref, b_ref, cw_ref, o_ref):
        first = pl.program_id(0) == 0
        gu, _ = _gelu(p_ref[:, 0:da])
        gv, _ = _gelu(p_ref[:, da : 2 * da])
        xhat, _ = _layernorm_stats(gv)
        vn = (xhat * nv_ref[...]).astype(BF16)
        mask = _tril(chunk)
        for hh in range(heads):
            wm = jnp.where(mask, ws_ref[hh], 0.0).astype(BF16)
            cols = slice(hh * hd, (hh + 1) * hd)
            for nn in range(tm // chunk):
                rows = slice(nn * chunk, (nn + 1) * chunk)
                z = _dot(wm, vn[rows, cols]) + b_ref[:, cols]
                o_ref[rows, cols] = (gu[rows, cols] * z).astype(BF16)
        c_cg, c_xb = 2 * da + db, 2 * da + 2 * db
        bg = p_ref[:, 2 * da : 2 * da + db]
        q = p_ref[:, c_cg : c_cg + db] * p_ref[:, c_xb : c_xb + db]
        q1 = _shift_down(q, 1, ph_ref, c_cg, c_xb, first)
        q2 = _shift_down(q, 2, ph_ref, c_cg, c_xb, first)
        conv = cw_ref[0:1, :] * q2 + cw_ref[1:2, :] * q1 + cw_ref[2:3, :] * q
        o_ref[:, da : da + db] = (bg * conv).astype(BF16)

    nh = tm // CONV_HALO
    return _call(
        body, "ab_mix_fwd", (s // tm,),
        [
            pl.BlockSpec((tm, n), lambda i: (i, 0)),
            pl.BlockSpec((CONV_HALO, n), lambda i: (jnp.maximum(i * nh - 1, 0), 0)),
            pl.BlockSpec((1, da), lambda i: (0, 0)),
            pl.BlockSpec((heads, chunk, chunk), lambda i: (0, 0, 0)),
            pl.BlockSpec((chunk, da), lambda i: (0, 0)),
            pl.BlockSpec((3, db), lambda i: (0, 0)),
        ],
        [pl.BlockSpec((tm, da + db), lambda i: (i, 0))],
        [jax.ShapeDtypeStruct((s, da + db), BF16)], [proj, proj, norm_v, w_s, b_rows, conv_w], phases=phases,
    )


def _ab_mix_bwd(proj, dcat, norm_v, w_s, b_rows, conv_w, phases=()):
    s, n = proj.shape
    heads, chunk, _ = w_s.shape
    da = norm_v.shape[1]
    hd = da // heads
    db = conv_w.shape[1]
    tm = _pick(s, (512, 256, 128))
    nblk = s // tm
    dhalo = 2 * CONV_HALO

    def body(p_ref, pa_ref, pb_ref, dc_ref, dcb_ref, nv_ref, ws_ref, b_ref, cw_ref,
             dp_ref, dnv_ref, dws_ref, dzs_ref, dcw_ref, dvn_sc):
        i = pl.program_id(0)
        first, last = i == 0, i == nblk - 1

        @pl.when(first)
        def _():
            dnv_ref[...] = jnp.zeros_like(dnv_ref)
            dws_ref[...] = jnp.zeros_like(dws_ref)
            dzs_ref[...] = jnp.zeros_like(dzs_ref)
            dcw_ref[...] = jnp.zeros_like(dcw_ref)

        uu = p_ref[:, 0:da]
        gu, gu_grad = _gelu(uu)
        gv, gv_grad = _gelu(p_ref[:, da : 2 * da])
        xhat, rstd = _layernorm_stats(gv)
        nv = nv_ref[...]
        vn = (xhat * nv).astype(BF16)
        dya = dc_ref[:, 0:da].astype(F32)
        dz = (dya * gu).astype(BF16)
        mask = _tril(chunk)
        for hh in range(heads):
            wm = jnp.where(mask, ws_ref[hh], 0.0).astype(BF16)
            cols = slice(hh * hd, (hh + 1) * hd)
            dws = jnp.zeros((chunk, chunk), F32)
            for nn in range(tm // chunk):
                rows = slice(nn * chunk, (nn + 1) * chunk)
                z = _dot(wm, vn[rows, cols]) + b_ref[:, cols]
                dp_ref[rows, cols] = (dya[rows, cols] * z * gu_grad[rows, cols]).astype(BF16)
                dz_blk = dz[rows, cols]
                dws = dws + _dot_nt(dz_blk, vn[rows, cols])
                dzs_ref[:, cols] += dz_blk.astype(F32)
                dvn = _dot_tn(wm, dz_blk)
                dnv_ref[:, cols] += jnp.sum(dvn * xhat[rows, cols], axis=0, keepdims=True)
                dvn_sc[rows, cols] = dvn
            dws_ref[hh] += jnp.where(mask, dws, 0.0)
        dxhat = dvn_sc[...] * nv
        dgv = rstd * (dxhat - jnp.mean(dxhat, axis=-1, keepdims=True) - xhat * jnp.mean(dxhat * xhat, axis=-1, keepdims=True))
        dp_ref[:, da : 2 * da] = (dgv * gv_grad).astype(BF16)

        c_bg, c_cg, c_xb = 2 * da, 2 * da + db, 2 * da + 2 * db
        bg = p_ref[:, c_bg : c_bg + db]
        cg = p_ref[:, c_cg : c_cg + db]
        xb = p_ref[:, c_xb : c_xb + db]
        q = cg * xb
        q1 = _shift_down(q, 1, pa_ref, c_cg, c_xb, first)
        q2 = _shift_down(q, 2, pa_ref, c_cg, c_xb, first)
        dyb = dc_ref[:, da : da + db].astype(F32)
        conv = cw_ref[0:1, :] * q2 + cw_ref[1:2, :] * q1 + cw_ref[2:3, :] * q
        dp_ref[:, c_bg : c_bg + db] = (dyb * conv).astype(BF16)
        e = dyb * bg
        dcw_ref[0:1, :] += jnp.sum(e * q2, axis=0, keepdims=True)
        dcw_ref[1:2, :] += jnp.sum(e * q1, axis=0, keepdims=True)
        dcw_ref[2:3, :] += jnp.sum(e * q, axis=0, keepdims=True)
        rows = lax.broadcasted_iota(jnp.int32, e.shape, 0)
        dq = cw_ref[2:3, :] * e
        for kk in (1, 2):
            ek = pltpu.roll(e, tm - kk, 0)
            for r in range(kk):
                below = dcb_ref[r : r + 1, da : da + db].astype(F32) * pb_ref[r : r + 1, c_bg : c_bg + db]
                below = jnp.where(last, 0.0, below)
                ek = jnp.where(rows == tm - kk + r, below, ek)
            dq = dq + cw_ref[2 - kk : 3 - kk, :] * ek
        dp_ref[:, c_cg : c_cg + db] = (dq * xb).astype(BF16)
        dp_ref[:, c_xb : c_xb + db] = (dq * cg).astype(BF16)

    nh = tm // CONV_HALO
    nhb = tm // dhalo
    const2 = lambda i: (0, 0)
    return _call(
        body, "ab_mix_bwd", (nblk,),
        [
            pl.BlockSpec((tm, n), lambda i: (i, 0)),
            pl.BlockSpec((CONV_HALO, n), lambda i: (jnp.maximum(i * nh - 1, 0), 0)),
            pl.BlockSpec((CONV_HALO, n), lambda i: (jnp.minimum((i + 1) * nh, s // CONV_HALO - 1), 0)),
            pl.BlockSpec((tm, da + db), lambda i: (i, 0)),
            pl.BlockSpec((dhalo, da + db), lambda i: (jnp.minimum((i + 1) * nhb, s // dhalo - 1), 0)),
            pl.BlockSpec((1, da), const2),
            pl.BlockSpec((heads, chunk, chunk), lambda i: (0, 0, 0)),
            pl.BlockSpec((chunk, da), const2),
            pl.BlockSpec((3, db), const2),
        ],
        [
            pl.BlockSpec((tm, n), lambda i: (i, 0)),
            pl.BlockSpec((1, da), const2),
            pl.BlockSpec((heads, chunk, chunk), lambda i: (0, 0, 0)),
            pl.BlockSpec((chunk, da), const2),
            pl.BlockSpec((3, db), const2),
        ],
        [
            jax.ShapeDtypeStruct((s, n), BF16),
            jax.ShapeDtypeStruct((1, da), F32),
            jax.ShapeDtypeStruct((heads, chunk, chunk), F32),
            jax.ShapeDtypeStruct((chunk, da), F32),
            jax.ShapeDtypeStruct((3, db), F32),
        ],
        [proj, proj, proj, dcat, dcat, norm_v, w_s, b_rows, conv_w],
        scratch=[pltpu.VMEM((tm, da), F32)], phases=phases,
    )


def _pool_counts(tm, i, w):
    t = i * tm + lax.broadcasted_iota(jnp.int32, (tm, 1), 0)
    return jnp.minimum(t + 1, w).astype(F32)


def _pool_fwd(x, vec, w_grp, scale, phases=()):
    s, d = x.shape
    groups, gd, _ = w_grp.shape
    tm = _pick(s, (512, 256, 128))

    def body(x_ref, xa_ref, vec_ref, w_ref, sc_ref, xo_ref, p_ref, o_ref):
        i = pl.program_id(0)
        h = _modulate(x_ref[...], vec_ref)
        ha = jnp.where(i == 0, 0.0, _modulate(xa_ref[...], vec_ref))
        ext = jnp.concatenate([ha, h], axis=0)
        for gi, w in enumerate(POOL_WINDOWS):
            cols = slice(gi * gd, (gi + 1) * gd)
            acc = ext[:, cols]
            step = 1
            while step < w:
                acc = acc + pltpu.roll(acc, step, 0)
                step *= 2
            p = (acc[POOL_HALO:, :] / _pool_counts(tm, i, w) - h[:, cols]).astype(BF16)
            p_ref[:, cols] = p
            o_ref[:, cols] = _dot(p, w_ref[gi]).astype(BF16)
        xo_ref[...] = x_ref[...] + vec_ref[3:4, :] * (o_ref[...].astype(F32) * sc_ref[...])

    nh = tm // POOL_HALO
    row = pl.BlockSpec((tm, d), lambda i: (i, 0))
    return _call(
        body, "pool_fwd", (s // tm,),
        [
            row,
            pl.BlockSpec((POOL_HALO, d), lambda i: (jnp.maximum(i * nh - 1, 0), 0)),
            pl.BlockSpec((8, d), lambda i: (0, 0)),
            pl.BlockSpec((groups, gd, gd), lambda i: (0, 0, 0)),
            pl.BlockSpec((1, d), lambda i: (0, 0)),
        ],
        [row, row, row],
        [jax.ShapeDtypeStruct((s, d), F32), jax.ShapeDtypeStruct((s, d), BF16), jax.ShapeDtypeStruct((s, d), BF16)],
        [x, x, vec, w_grp, scale], phases=phases,
    )


def _pool_bwd(dxo, x, vec, p, o, w_grp, scale, phases=()):
    s, d = x.shape
    groups, gd, _ = w_grp.shape
    tm = _pick(s, (512, 256, 128))
    nblk = s // tm

    def body(dxo_ref, dxb_ref, x_ref, vec_ref, p_ref, o_ref, w_ref, sc_ref, dx_ref, dw_ref, dsc_ref, dvec_ref, dw_sc):
        i = pl.program_id(0)

        @pl.when(i == 0)
        def _():
            dw_sc[...] = jnp.zeros_like(dw_sc)
            dsc_ref[...] = jnp.zeros_like(dsc_ref)
            dvec_ref[...] = jnp.zeros_like(dvec_ref)

        gate, sc = vec_ref[3:4, :], sc_ref[...]
        dxo_v = dxo_ref[...]
        ov = o_ref[...].astype(F32)
        dvec_ref[3:4, :] += jnp.sum(dxo_v * (ov * sc), axis=0, keepdims=True)
        dy = gate * dxo_v
        dsc_ref[...] += jnp.sum(dy * ov, axis=0, keepdims=True)
        dout = (dy * sc).astype(BF16)
        dout_b = jnp.where(i == nblk - 1, 0.0, gate * dxb_ref[...] * sc).astype(BF16)
        for gi, w in enumerate(POOL_WINDOWS):
            cols = slice(gi * gd, (gi + 1) * gd)
            dw_sc[gi] += _dot_tn(p_ref[:, cols], dout[:, cols])
            wb = w_ref[gi]
            dp = _dot_nt(dout[:, cols], wb)
            dp_b = _dot_nt(dout_b[:, cols], wb)
            e = dp / _pool_counts(tm, i, w)
            t_below = (i + 1) * tm + lax.broadcasted_iota(jnp.int32, (POOL_HALO, 1), 0)
            e_b = dp_b / jnp.minimum(t_below + 1, w).astype(F32)
            acc = jnp.concatenate([e, e_b], axis=0)
            step = 1
            while step < w:
                acc = acc + pltpu.roll(acc, tm + POOL_HALO - step, 0)
                step *= 2
            dx_ref[:, cols] = acc[:tm, :] - dp
        dx, _ = _modulate_bwd(x_ref[...], dx_ref[...], vec_ref, dvec_ref)
        dx_ref[...] = dxo_v + dx

        @pl.when(i == nblk - 1)
        def _():
            dw_ref[...] = dw_sc[...].astype(BF16)

    nh = tm // POOL_HALO
    row = pl.BlockSpec((tm, d), lambda i: (i, 0))
    vecs = pl.BlockSpec((8, d), lambda i: (0, 0))
    wspec = pl.BlockSpec((groups, gd, gd), lambda i: (0, 0, 0))
    return _call(
        body, "pool_bwd", (nblk,),
        [
            row,
            pl.BlockSpec((POOL_HALO, d), lambda i: (jnp.minimum((i + 1) * nh, s // POOL_HALO - 1), 0)),
            row, vecs, row, row, wspec,
            pl.BlockSpec((1, d), lambda i: (0, 0)),
        ],
        [row, wspec, pl.BlockSpec((1, d), lambda i: (0, 0)), vecs],
        [
            jax.ShapeDtypeStruct((s, d), F32),
            jax.ShapeDtypeStruct((groups, gd, gd), BF16),
            jax.ShapeDtypeStruct((1, d), F32),
            jax.ShapeDtypeStruct((8, d), F32),
        ],
        [dxo, dxo, x, vec, p, o, w_grp, scale],
        scratch=[pltpu.VMEM((groups, gd, gd), F32)], phases=phases,
    )


def _loss_head(x, gain, target, phases=()):
    s, d = x.shape
    tm = _pick(s, (512, 256, 128))

    def body(x_ref, g_ref, t_ref, dx_ref, aux_ref):
        @pl.when(pl.program_id(0) == 0)
        def _():
            aux_ref[...] = jnp.zeros_like(aux_ref)

        xv = x_ref[...]
        rstd = _rstd(xv)
        r = xv * rstd
        gain_v = g_ref[...]
        err = r * gain_v - t_ref[...]
        aux_ref[1:2, :] += jnp.sum(err * err, axis=0, keepdims=True)
        dout = err * (1.0 / d)
        aux_ref[0:1, :] += jnp.sum(dout * r, axis=0, keepdims=True)
        dr = dout * gain_v
        dx_ref[...] = rstd * (dr - r * jnp.mean(dr * r, axis=-1, keepdims=True))

    row = pl.BlockSpec((tm, d), lambda i: (i, 0))
    return _call(
        body, "loss_head", (s // tm,),
        [row, pl.BlockSpec((1, d), lambda i: (0, 0)), row],
        [row, pl.BlockSpec((8, d), lambda i: (0, 0))],
        [jax.ShapeDtypeStruct((s, d), F32), jax.ShapeDtypeStruct((8, d), F32)], [x, gain, target], phases=phases,
    )


def _small_adam(gathered, gathered_ws, layout, smalls, chip):
    names = list(smalls)
    n = len(names)
    loss_row, _, _, n_feat = layout["loss"]

    def body(*refs):
        chip_ref, g_ref, gws_ref = refs[0], refs[1], refs[2]
        wmv = refs[3 : 3 + 3 * n]
        outs = refs[3 + 3 * n : 3 + 7 * n]
        total = refs[-1]
        total[...] = g_ref[0]
        for kdev in range(1, N_DEV):
            total[...] += g_ref[kdev]
        total_ws = gws_ref[0]
        for kdev in range(1, N_DEV):
            total_ws = total_ws + gws_ref[kdev]
        my_chip = chip_ref[0]
        for a, name in enumerate(names):
            w_ref, m_ref, v_ref = wmv[3 * a : 3 * a + 3]
            if name == "ab_w_s":
                g = total_ws
            else:
                row0, rows, col0, cols = layout[name]
                if col0 is None:
                    g = jnp.zeros((rows, cols), F32)
                    for j in range(N_CHIPS):
                        g = g + jnp.where(my_chip == j, total[row0 : row0 + rows, j * cols : (j + 1) * cols], 0.0)
                else:
                    g = total[row0 : row0 + rows, col0 : col0 + cols]
            dl, mo, vo = _adam(w_ref[...], g, m_ref[...], v_ref[...])
            outs[4 * a][...] = g
            outs[4 * a + 1][...] = dl
            outs[4 * a + 2][...] = mo
            outs[4 * a + 3][...] = vo
        refs[3 + 7 * n][...] = 0.5 * jnp.sum(total[loss_row : loss_row + 1, 0:n_feat], axis=1, keepdims=True) / n_feat

    ins = [gathered, gathered_ws]
    out_shapes = []
    for name in names:
        ins.extend(smalls[name])
        out_shapes.extend([jax.ShapeDtypeStruct(smalls[name][0].shape, F32)] * 4)
    out_shapes.append(jax.ShapeDtypeStruct((1, 1), F32))
    whole = lambda shape: pl.BlockSpec(shape, functools.partial(lambda nd, i, c: (0,) * nd, len(shape)))
    res = pl.pallas_call(
        body, name="small_adam",
        grid_spec=pltpu.PrefetchScalarGridSpec(
            num_scalar_prefetch=1, grid=(1,),
            in_specs=[whole(a.shape) for a in ins], out_specs=[whole(o.shape) for o in out_shapes],
            scratch_shapes=[pltpu.VMEM(gathered.shape[1:], F32)],
        ),
        out_shape=out_shapes,
        compiler_params=pltpu.CompilerParams(dimension_semantics=("arbitrary",), vmem_limit_bytes=VMEM_LIMIT_BYTES),
    )(chip.reshape(1).astype(jnp.int32), *ins)
    return {name: res[4 * a : 4 * a + 4] for a, name in enumerate(names)}, res[4 * n]


def _pad_rows(a, rows=8):
    extra = (-a.shape[0]) % rows
    return jnp.pad(a, ((0, extra), (0, 0))) if extra else a


def _pad_cols(a, cols):
    return jnp.pad(a, ((0, 0), (0, cols - a.shape[1]))) if a.shape[1] < cols else a


def _run(fn, *phases):
    outs, p_outs = fn(list(phases))
    for p, po in zip(phases, p_outs):
        p.then(po)
    return outs


def kernel(x, c, norm_g, w_mod, b_mod, w_ffn_in, w_ffn_out, ab_w_in, ab_norm_v, ab_w_s, ab_b_s, ab_conv_w, ab_w_out, pool_w_grp, pool_scale, final_g, loss_target, m_norm_g, m_w_mod, m_b_mod, m_w_ffn_in, m_w_ffn_out, m_ab_w_in, m_ab_norm_v, m_ab_w_s, m_ab_b_s, m_ab_conv_w, m_ab_w_out, m_pool_w_grp, m_pool_scale, m_final_g, v_norm_g, v_w_mod, v_b_mod, v_w_ffn_in, v_w_ffn_out, v_ab_w_in, v_ab_norm_v, v_ab_w_s, v_ab_b_s, v_ab_conv_w, v_ab_w_out, v_pool_w_grp, v_pool_scale, v_final_g):
    ix, iy, ic = _place()
    chip = 2 * ix + iy
    me = 4 * ix + 2 * iy + ic
    where = jnp.stack([chip, ic]).astype(jnp.int32)
    s, d = x.shape[1], x.shape[2]
    x0 = x.reshape(s, d)
    target = loss_target.reshape(s, d)
    n_layers = norm_g.shape[0]
    dq = d // N_CHIPS
    heads, chunk = ab_w_s.shape[1], ab_w_s.shape[2]
    da = ab_norm_v.shape[1]
    db = ab_conv_w.shape[2] * N_CHIPS
    f_hidden = w_ffn_out.shape[2] * N_CHIPS
    assert n_layers == 2 and da % heads == 0

    cw_pad = _pad_cols(ab_conv_w.reshape(3, db // N_CHIPS), dq)
    packed = jnp.concatenate(
        [_pad_rows(c.reshape(N_CHIPS, dq)), _pad_rows(norm_g.reshape(-1, dq)), _pad_rows(pool_scale.reshape(1, dq)), _pad_rows(cw_pad)],
        axis=0,
    )
    ncol = w_mod.shape[2]
    b_cols = lax.dynamic_slice(b_mod, (0, chip * ncol), (n_layers, ncol)).reshape(n_layers, 1, ncol)
    small = {}

    def small_gather(key, arrs):
        def then(outs):
            small[key] = outs

        return _phase_small_gather(arrs, then)

    stacks = {
        "w_ffn_in": tuple(a.reshape((-1,) + a.shape[2:]) for a in (w_ffn_in, m_w_ffn_in, v_w_ffn_in)),
        "w_ffn_out": tuple(a.reshape((-1,) + a.shape[2:]) for a in (w_ffn_out, m_w_ffn_out, v_w_ffn_out)),
        "ab_w_in": (ab_w_in, m_ab_w_in, v_ab_w_in),
        "ab_w_out": (ab_w_out, m_ab_w_out, v_ab_w_out),
        "pool_w_grp": (pool_w_grp[0], m_pool_w_grp[0], v_pool_w_grp[0]),
    }
    big_in = _Big((1, d, 2 * f_hidden), 2, 1)
    big_out = _Big((1, f_hidden, d), 1, 2)
    units = {}
    for l in range(n_layers):
        for k in range(2):
            units[f"in{l}{k}"] = (big_in, "w_ffn_in", 2 * l + k)
            units[f"out{l}{k}"] = (big_out, "w_ffn_out", 2 * l + k)
    units["abin"] = (_Big((1, d, ab_w_in.shape[2] * N_CHIPS), 2, 1), "ab_w_in", 0)
    units["about"] = (_Big((1, ab_w_out.shape[1] * N_CHIPS, d), 1, 2), "ab_w_out", 0)
    units["pool"] = (_Big((pool_w_grp.shape[1], pool_w_grp.shape[2] * N_CHIPS, pool_w_grp.shape[3]), 1, 0), "pool_w_grp", 0)
    big = {u: g for u, (g, _, _) in units.items()}

    weight = {}
    complete = set()

    def cast(u):
        g, st, b0 = units[u]

        def launch(phases):
            (weight[u],), p_outs = _cast_into_full(stacks[st][0], b0, g, where, "cast_" + u, phases)
            return None, p_outs

        return launch

    def gather_relay(us, second, whole_first):
        def then(outs):
            for u, o in zip(us, outs):
                weight[u] = o

        return _phase_gather_relay([weight[u] for u in us], [big[u] for u in us], second, whole_first, then)

    def gather_sibling(*us):
        def then(outs):
            for u, o in zip(us, outs):
                weight[u] = o
                complete.add(u)

        return _phase_gather_sibling([weight[u] for u in us], [big[u] for u in us], then)

    def w_of(u):
        assert u in complete, u
        return weight[u]

    _run(cast("in00"), small_gather("inputs", [packed]))
    small_all = small["inputs"][0]
    by_chip = small_all[0::2]
    c_all = small_all[:, 0:N_CHIPS, :].reshape(N_DEV, d)
    norm_full = by_chip[:, 8 : 8 + 3 * n_layers, :].transpose(1, 0, 2).reshape(3 * n_layers, d)
    pool_scale_full = by_chip[:, 16:17, :].transpose(1, 0, 2).reshape(1, d)
    conv_full = by_chip[:, 24:27, : db // N_CHIPS].transpose(1, 0, 2).reshape(3, db)
    pieces = [("in00", "out00"), ("abin", "about"), ("in01", "out01"), ("in10", "out10", "pool"), ("in11", "out11")]
    in_flight = {}

    def start_gather(p):
        in_flight[p, 0] = _split_start(gather_relay(pieces[p], False, p == 0), f"gather_{p}_start")

    def relay_gather(p, after=()):
        flight = in_flight.pop((p, 0))
        _split_wait(flight, list(after) + list(started().ins), f"gather_{p}_arrived")
        in_flight[p, 1] = _split_start(gather_relay(pieces[p], True, p == 0), f"gather_{p}_relay")

    def started():
        return _after(*[flight.token for flight in in_flight.values()])

    def finish_gather(p, after, meanwhile=None):
        flight = in_flight.pop((p, 1))
        _split_wait(flight, list(after) + list(started().ins), f"gather_{p}_wait")
        crossing = _split_start(gather_sibling(*pieces[p]), f"gather_{p}_forward")
        behind = [crossing.token]
        if p + 1 < len(pieces):
            relay_gather(p + 1)
        if p + 3 < len(pieces):
            start_gather(p + 3)
        behind = behind + list(started().ins)
        if meanwhile is not None:
            behind = behind + meanwhile(_after(crossing.token))
        _split_wait(crossing, behind, f"gather_{p}_forwarded")

    _run(cast("out00"))
    start_gather(0)
    mod_cols = _run(lambda phases: _mod_fwd(c_all, w_mod, b_cols, phases), started())[0]

    def mod_rows(outs):
        small["mod"] = outs

    _run(cast("about"), started())
    early = [u for piece in pieces[2:4] for u in piece]
    for u in early:
        _run(cast(u), started())
    _run(
        cast("abin"), _phase_small_exchange(mod_cols.transpose(1, 0, 2), mod_rows),
        started(), _after(*[weight[u] for u in early]),
    )
    relay_gather(0)
    start_gather(1)
    start_gather(2)
    for u in pieces[4]:
        _run(cast(u), started())
    mod_mine = small["mod"][0][0::2]
    mod = mod_mine.transpose(1, 0, 2).reshape(n_layers, 3, 3, d)
    vecs = {
        (l, sub): jnp.pad(norm_full[3 * l + sub][None], ((0, 7), (0, 0))) + jnp.pad(mod[l, sub], ((1, 4), (0, 0)))
        for l in range(n_layers)
        for sub in range(3)
    }
    b_rows = jnp.broadcast_to(ab_b_s[0].T[:, :, None], (chunk, heads, da // heads)).reshape(chunk, da)

    saved = {}

    def ffn_forward(xs, l, sub, k, *phases):
        saved[l, sub, "x"] = xs
        xs, gg, uu, yb = _run(
            lambda ph: _ffn_fwd(xs, vecs[l, sub], w_of(f"in{l}{k}"), w_of(f"out{l}{k}"), f"ffn_fwd_{l}{k}", ph), *phases
        )
        saved[l, sub, "act"] = (gg, uu, yb)
        return xs

    finish_gather(0, [vecs[0, 0]] + [weight[u] for u in pieces[4]])
    xs = ffn_forward(x0, 0, 0, 0, started())
    saved[0, 1, "x"] = xs
    finish_gather(1, [xs])
    (proj,) = _run(lambda ph: _proj_mod_fwd(xs, vecs[0, 1], w_of("abin"), ph), started())
    (cat,) = _run(lambda ph: _ab_mix_fwd(proj, ab_norm_v, ab_w_s[0], b_rows, conv_full, ph))
    xs, yb = _run(lambda ph: _proj_res_fwd(cat, w_of("about"), xs, vecs[0, 1], ph))
    saved[0, 1, "act"] = (proj, cat, yb)
    finish_gather(2, [xs])
    xs = ffn_forward(xs, 0, 2, 1, started())
    finish_gather(3, [xs])
    xs = ffn_forward(xs, 1, 0, 0, started())
    saved[1, 1, "x"] = xs
    pooled = []

    def pool_forward(behind):
        pooled.extend(_run(lambda ph: _pool_fwd(xs, vecs[1, 1], w_of("pool"), pool_scale_full, ph), behind))
        return [pooled[0]]

    finish_gather(4, [xs], pool_forward)
    xs, pp, oo = pooled
    saved[1, 1, "act"] = (pp, oo)
    xs = ffn_forward(xs, 1, 2, 1)
    dxs, aux = _run(lambda ph: _loss_head(xs, final_g.reshape(1, d), target, ph))

    grad = {}
    recv = {}
    csum = {}
    parts = {}
    reduced = {}
    done = set()
    dvecs, small_g = {}, {}

    def pair_exchange(*us):
        def then(outs):
            for u, o in zip(us, outs):
                recv[u] = o

        return _phase_pair_exchange([grad[u] for u in us], [big[u] for u in us], then)

    def grad_half(u, a, bs, mine, name, *phases):
        (res,) = _run(lambda ph: _grad_half(a, bs, big[u], where, mine, recv[u] if mine else None, name, ph), *phases)
        return res

    def pair_sum(u, *phases):
        def launch(ph):
            (csum[u],), p_outs = _pair_sum(grad[u], recv[u], big[u], where, "pair_sum_" + u, ph)
            return None, p_outs

        _run(launch, *phases)

    def chip_exchange(*us):
        def then(outs):
            for u, o in zip(us, outs):
                parts[u] = o

        return _phase_chip_exchange([csum[u] for u in us], [big[u] for u in us], then)

    def chip_sum(*us, carried=()):
        for n_u, u in enumerate(us):
            g, st, b0 = units[u]

            def launch(ph):
                (reduced[st],), p_outs = _chip_sum(
                    csum[u], parts[u], g, where, reduced.get(st), stacks[st][0].shape, b0, "chip_sum_" + u, ph
                )
                return None, p_outs

            _run(launch, *(carried if n_u == 0 else ()))

    def pair_broadcast(*us):
        sts = [units[u][1] for u in us]
        assert len(set(sts)) == len(sts)

        def then(outs):
            for u, st, o in zip(us, sts, outs):
                reduced[st] = o
                done.add(u)

        return _phase_pair_broadcast([reduced[st] for st in sts], [big[u] for u in us], [units[u][2] for u in us], then)

    def ffn_backward(dxs, l, sub, k, carried_bwd, carried_send, carried_mine):
        gg, uu, yb = saved[l, sub, "act"]
        w_in, w_out = w_of(f"in{l}{k}"), w_of(f"out{l}{k}")
        uo, ui, tag = f"out{l}{k}", f"in{l}{k}", f"{l}{k}"
        dxs, dg, du, a, h, dy, dvecs[l, sub] = _run(
            lambda ph: _ffn_bwd(dxs, saved[l, sub, "x"], vecs[l, sub], gg, uu, yb, w_in, w_out, "ffn_bwd_" + tag, ph), *carried_bwd()
        )
        grad[uo] = grad_half(uo, a, [dy], False, "dw_out_send_" + tag)
        grad[ui] = grad_half(ui, h, [dg, du], False, "dw_in_send_" + tag, pair_exchange(uo), *carried_send())
        csum[uo] = grad_half(uo, a, [dy], True, "dw_out_" + tag, pair_exchange(ui))
        csum[ui] = grad_half(ui, h, [dg, du], True, "dw_in_" + tag, *carried_mine())
        return dxs

    none = lambda: ()
    dxs = ffn_backward(dxs, 1, 2, 1, none, none, none)
    pp, oo = saved[1, 1, "act"]
    dxs, grad["pool"], small_g["pool_scale"], dvecs[1, 1] = _run(
        lambda ph: _pool_bwd(dxs, saved[1, 1, "x"], vecs[1, 1], pp, oo, w_of("pool"), pool_scale_full, ph)
    )

    def after_11():
        return (chip_exchange("in11", "out11"), pair_exchange("pool"))

    def bcast_11():
        chip_sum("in11", "out11")
        pair_sum("pool")
        return (pair_broadcast("in11", "out11"), chip_exchange("pool"))

    dxs = ffn_backward(dxs, 1, 0, 0, after_11, bcast_11, none)

    def after_10():
        return (chip_exchange("in10", "out10"),)

    def bcast_10():
        chip_sum("in10", "out10", "pool")
        return (pair_broadcast("in10", "out10", "pool"),)

    dxs = ffn_backward(dxs, 0, 2, 1, after_10, bcast_10, none)

    proj, cat, yb = saved[0, 1, "act"]
    out01 = _split_start(chip_exchange("out01"), "reduce_out01_start")
    dy, dcat, dgate = _run(lambda ph: _proj_res_bwd(dxs, yb, vecs[0, 1], w_of("about"), ph), _after(out01.token))
    grad["about"] = grad_half("about", cat, [dy], False, "dw_ab_out_send")
    dproj, small_g["ab_norm_v"], small_g["ab_w_s"], dzs, small_g["ab_conv_w"] = _run(
        lambda ph: _ab_mix_bwd(proj, dcat, ab_norm_v, ab_w_s[0], b_rows, conv_full, ph), pair_exchange("about")
    )
    small_g["ab_b_s"] = dzs.reshape(chunk, heads, da // heads).sum(axis=2).T
    dxs, h, dvecs[0, 1] = _run(
        lambda ph: _proj_mod_bwd(dproj[None], w_of("abin"), saved[0, 1, "x"], vecs[0, 1], dxs, dgate, "ab_in_bwd", ph)
    )
    grad["abin"] = grad_half("abin", h, [dproj], False, "dw_ab_in_send")
    (csum["out01"],) = _split_wait(out01, [grad["abin"]], "reduce_out01_wait")
    chip_sum("out01")
    csum["about"] = grad_half("about", cat, [dy], True, "dw_ab_out", pair_broadcast("out01"), pair_exchange("abin"))
    csum["abin"] = grad_half("abin", h, [dproj], True, "dw_ab_in")

    layout = {}
    tail = {}

    def after_01():
        tail["01"] = _split_start(chip_exchange("in01", "abin", "about"), "reduce_01_start")
        return (_after(tail["01"].token),)

    def pack_small_grads():
        dvec_all = jnp.stack([dvecs[l, sub] for l in range(n_layers) for sub in range(3)])
        dgain = dvec_all[:, 0, :]
        dmod = dvec_all[:, 1:4, :].reshape(3 * 3 * n_layers, d)
        rows = {
            "norm_g": (dgain, None, dq), "final_g": (aux[0:1], 0, d), "pool_scale": (small_g["pool_scale"], None, dq),
            "b_mod": (dmod, 0, d), "ab_norm_v": (small_g["ab_norm_v"], 0, da),
            "ab_conv_w": (small_g["ab_conv_w"], None, db // N_CHIPS), "ab_b_s": (small_g["ab_b_s"], 0, chunk),
            "loss": (aux[1:2], 0, d),
        }
        row0 = 0
        for nm, (pc, col0, cols) in rows.items():
            layout[nm] = (row0, pc.shape[0], col0, cols)
            row0 += pc.shape[0]
        packed_rows = -(-row0 // 8) * 8
        return sum(
            jnp.pad(pc, ((layout[nm][0], packed_rows - layout[nm][0] - pc.shape[0]), (0, d - pc.shape[1])))
            for nm, (pc, _, _) in rows.items()
        )

    def bcast_01():
        csum["in01"], csum["abin"], csum["about"] = _split_wait(tail["01"], [dvecs[0, 0]], "reduce_01_wait")
        chip_sum("in01", "abin", "about")
        grads_small = [pack_small_grads(), small_g["ab_w_s"].reshape(heads * chunk, chunk)]
        tail["small"] = _split_start(small_gather("grads", grads_small), "gather_small_grads_start")
        return (pair_broadcast("in01", "abin", "about"), _after(tail["small"].token))

    def reduce_out00():
        tail["out00"] = _split_start(chip_exchange("out00"), "reduce_out00_start")
        return (_after(tail["out00"].token),)

    dxs = ffn_backward(dxs, 0, 0, 0, after_01, bcast_01, reduce_out00)
    grad_x = dxs.reshape(x.shape)

    last = _split_start(chip_exchange("in00"), "reduce_last_start")
    _split_wait(tail["small"], [last.token], "gather_small_grads_wait")
    g_all, gws_all = small["grads"]

    out = {}

    def adam_stack(st, after=()):
        w3, m3, v3 = stacks[st]
        assert all(u in done for u, (_, ust, _) in units.items() if ust == st), st
        shape = {"w_ffn_in": w_ffn_in.shape, "w_ffn_out": w_ffn_out.shape, "pool_w_grp": pool_w_grp.shape}.get(st, w3.shape)
        out[st] = tuple(a.reshape(shape) for a in _adam_stack(w3, reduced[st], m3, v3, "adam_" + st, after))

    shapes2d = {
        "norm_g": (3 * n_layers, dq), "b_mod": (9 * n_layers, d), "final_g": (1, d), "ab_norm_v": (1, da),
        "pool_scale": (1, dq), "ab_conv_w": (3, db // N_CHIPS), "ab_b_s": (heads, chunk), "ab_w_s": (heads * chunk, chunk),
    }
    small_w = {"norm_g": (norm_g, m_norm_g, v_norm_g), "b_mod": (b_mod, m_b_mod, v_b_mod), "final_g": (final_g, m_final_g, v_final_g),
               "ab_norm_v": (ab_norm_v, m_ab_norm_v, v_ab_norm_v), "pool_scale": (pool_scale, m_pool_scale, v_pool_scale),
               "ab_conv_w": (ab_conv_w, m_ab_conv_w, v_ab_conv_w), "ab_b_s": (ab_b_s, m_ab_b_s, v_ab_b_s), "ab_w_s": (ab_w_s, m_ab_w_s, v_ab_w_s)}
    smalls = {nm: tuple(a.reshape(shapes2d[nm]) for a in wmv) for nm, wmv in small_w.items()}
    small_out, loss = _small_adam(g_all, gws_all, layout, smalls, chip)
    loss = loss.reshape(())
    for nm, res in small_out.items():
        out[nm] = tuple(a.reshape(small_w[nm][0].shape) for a in res)

    mod_row0 = layout["b_mod"][0]
    dmod_all = g_all[:, mod_row0 : mod_row0 + 9 * n_layers, :].reshape(N_DEV, n_layers, 9 * d)
    dmod_cols = lax.dynamic_slice(dmod_all, (0, 0, chip * ncol), (N_DEV, n_layers, ncol)).transpose(1, 0, 2)
    out["w_mod"] = tuple(_mod_bwd_adam(c_all.T, dmod_cols, w_mod, m_w_mod, v_w_mod, (last.token,)))

    (csum["out00"],) = _split_wait(tail["out00"], [out["w_mod"][1]], "reduce_out00_wait")
    chip_sum("out00")
    crossing = _split_start(pair_broadcast("out00"), "broadcast_out00_start")
    for st in ("ab_w_in", "ab_w_out", "pool_w_grp"):
        adam_stack(st, (crossing.token,))
    _split_wait(crossing, [out[st][1] for st in ("ab_w_in", "ab_w_out", "pool_w_grp")], "broadcast_out00_wait")
    (csum["in00"],) = _split_wait(last, [reduced["w_ffn_out"]], "reduce_last_wait")
    chip_sum("in00")
    crossing = _split_start(pair_broadcast("in00"), "broadcast_last_start")
    adam_stack("w_ffn_out", (crossing.token,))
    _split_wait(crossing, [out["w_ffn_out"][1]], "broadcast_last_wait")
    adam_stack("w_ffn_in")

    order = ["norm_g", "w_mod", "b_mod", "w_ffn_in", "w_ffn_out", "ab_w_in", "ab_norm_v", "ab_w_s", "ab_b_s", "ab_conv_w", "ab_w_out", "pool_w_grp", "pool_scale", "final_g"]
    return (loss, grad_x, *[out[nm][0] for nm in order], *[out[nm][1] for nm in order], *[out[nm][2] for nm in order], *[out[nm][3] for nm in order])
```

```python
import functools
import math

import jax
import jax.numpy as jnp
from jax import lax
from jax.experimental import pallas as pl
from jax.experimental.pallas import tpu as pltpu

F32 = jnp.float32
BF16 = jnp.bfloat16
MESH = pl.DeviceIdType.MESH

EPS = 1e-6
ADAM_LR = 0.001
ADAM_B1 = 0.9
ADAM_B2 = 0.999
ADAM_EPS = 1e-08
ADAM_WD = 0.01
ADAM_STEP = 10
POOL_WINDOWS = (2, 4, 8, 16)
POOL_HALO = 16
CONV_HALO = 8
N_CHIPS = 4
N_DEV = 8
VMEM_LIMIT_BYTES = 48 * 1024 * 1024
EW_BLOCK_ELEMS = 1024 * 1024
ADAM_BLOCK_ELEMS = 512 * 1024


def _pick(n, prefs):
    for p in prefs:
        if p <= n and n % p == 0:
            return p
    return n


def _row_tile(rows, cols, block_elems=EW_BLOCK_ELEMS):
    best = None
    for d in range(16, rows + 1, 16):
        if rows % d == 0 and d * cols <= block_elems:
            best = d
    return best or rows


def _dot(a, b):
    return jnp.dot(a, b, preferred_element_type=F32)


def _dot_nt(a, b):
    return lax.dot_general(a, b, (((1,), (1,)), ((), ())), preferred_element_type=F32)


def _dot_tn(a, b):
    return lax.dot_general(a, b, (((0,), (0,)), ((), ())), preferred_element_type=F32)


def _sigmoid(x):
    return 0.5 * jnp.tanh(0.5 * x) + 0.5


_GELU_C = math.sqrt(2.0 / math.pi)


def _gelu(x):
    x2 = x * x
    t = jnp.tanh(_GELU_C * (x + 0.044715 * x2 * x))
    val = 0.5 * x * (1.0 + t)
    grad = 0.5 * (1.0 + t) + 0.5 * x * (1.0 - t * t) * (_GELU_C * (1.0 + 3.0 * 0.044715 * x2))
    return val, grad


def _rstd(x):
    return lax.rsqrt(jnp.mean(x * x, axis=-1, keepdims=True) + EPS)


def _modulate(x, vec_ref):
    return (x * _rstd(x)) * vec_ref[0:1, :] * (1.0 + vec_ref[2:3, :]) + vec_ref[1:2, :]


def _modulate_bwd(x, dh, vec_ref, dvec_ref):
    gn, sh, sc = vec_ref[0:1, :], vec_ref[1:2, :], vec_ref[2:3, :]
    rstd = _rstd(x)
    r = x * rstd
    dvec_ref[0:1, :] += jnp.sum(dh * r * (1.0 + sc), axis=0, keepdims=True)
    dvec_ref[1:2, :] += jnp.sum(dh, axis=0, keepdims=True)
    dvec_ref[2:3, :] += jnp.sum(dh * r * gn, axis=0, keepdims=True)
    gm = gn * (1.0 + sc)
    dr = dh * gm
    dx = rstd * (dr - r * jnp.mean(dr * r, axis=-1, keepdims=True))
    return dx, r * gm + sh


def _adam(w, g, m, v):
    m = ADAM_B1 * m + (1.0 - ADAM_B1) * g
    v = ADAM_B2 * v + (1.0 - ADAM_B2) * (g * g)
    m_hat = m / (1.0 - ADAM_B1**ADAM_STEP)
    v_hat = v / (1.0 - ADAM_B2**ADAM_STEP)
    delta = -ADAM_LR * (m_hat / (jnp.sqrt(v_hat) + ADAM_EPS) + ADAM_WD * w)
    return delta, m, v


_ANY = pl.BlockSpec(memory_space=pl.ANY)


class _Phase:
    def __init__(self, ins, out_shapes, aliases, n_sems, start, finish, then):
        self.ins, self.out_shapes, self.aliases, self.n_sems = list(ins), list(out_shapes), dict(aliases), n_sems
        self.start, self.finish, self.then = start, finish, then


def _call(body, name, grid, in_specs, out_specs, out_shape, ins, scratch=(), prefetch=(), phases=(), in_place=None):
    n_pre, n_in, n_out, n_sc = len(prefetch), len(in_specs), len(out_specs), len(scratch)
    ph_in = [len(p.ins) for p in phases]
    ph_out = [len(p.out_shapes) for p in phases]

    def kernel_body(*refs):
        pos = [0]

        def take(k):
            pos[0] += k
            return refs[pos[0] - k : pos[0]]

        pre, ins_ = take(n_pre), take(n_in)
        p_ins = [take(k) for k in ph_in]
        outs_ = take(n_out)
        p_outs = [take(k) for k in ph_out]
        sc = take(n_sc)
        sems = [take(2) for _ in phases]
        if phases:
            ids = [pl.program_id(a) for a in range(len(grid))]
            first = functools.reduce(jnp.logical_and, [i == 0 for i in ids])
            last = functools.reduce(jnp.logical_and, [i == g - 1 for i, g in zip(ids, grid)])

            @pl.when(first)
            def _():
                for p, pi, po, (send, recv) in zip(phases, p_ins, p_outs, sems):
                    p.start(pi, po, send, recv)

        if body is not None:
            body(*pre, *ins_, *outs_, *sc)
        if phases:

            @pl.when(last)
            def _():
                for p, pi, po, (send, recv) in zip(phases, p_ins, p_outs, sems):
                    p.finish(pi, po, send, recv)

    aliases = {n_pre + i: o for i, o in (in_place or {}).items()}
    i0, o0 = n_pre + n_in, n_out
    for p in phases:
        for i, o in p.aliases.items():
            aliases[i0 + i] = o0 + o
        i0 += len(p.ins)
        o0 += len(p.out_shapes)
    all_in = list(in_specs) + [_ANY] * sum(ph_in)
    all_out = list(out_specs) + [_ANY] * sum(ph_out)
    all_scratch = list(scratch)
    for p in phases:
        all_scratch += [pltpu.SemaphoreType.DMA((p.n_sems,)), pltpu.SemaphoreType.DMA((p.n_sems,))]
    shapes = list(out_shape) + [s for p in phases for s in p.out_shapes]
    operands = list(prefetch) + list(ins) + [a for p in phases for a in p.ins]
    sem = ("arbitrary",) * len(grid)
    params = pltpu.CompilerParams(dimension_semantics=sem, vmem_limit_bytes=VMEM_LIMIT_BYTES)
    if n_pre:
        res = pl.pallas_call(
            kernel_body, name=name, out_shape=shapes, input_output_aliases=aliases, compiler_params=params,
            grid_spec=pltpu.PrefetchScalarGridSpec(
                num_scalar_prefetch=n_pre, grid=grid, in_specs=all_in, out_specs=all_out, scratch_shapes=all_scratch
            ),
        )(*operands)
    else:
        res = pl.pallas_call(
            kernel_body, name=name, grid=grid, in_specs=all_in, out_specs=all_out, out_shape=shapes,
            scratch_shapes=all_scratch, input_output_aliases=aliases, compiler_params=params,
        )(*operands)
    res = list(res)
    outs, rest = res[:n_out], res[n_out:]
    p_res = []
    for k in ph_out:
        p_res.append(rest[:k])
        rest = rest[k:]
    return outs, p_res


def _place():
    return lax.axis_index("x"), lax.axis_index("y"), lax.axis_index("c")


def _other_chips():
    x, y, _ = _place()
    return [(1 - x, y), (x, 1 - y), (1 - x, 1 - y)]


def _flip(k):
    x, y, c = _place()
    return (1 - x if k & 4 else x, 1 - y if k & 2 else y, 1 - c if k & 1 else c)


def _remote(src, dst, send, recv, k, to):
    return pltpu.make_async_remote_copy(
        src_ref=src, dst_ref=dst, send_sem=send.at[k], recv_sem=recv.at[k], device_id=to, device_id_type=MESH
    )


def _phase_small_gather(arrs, then):
    n = len(arrs)

    def copies(ins, outs, send, recv):
        x, y, c = _place()
        me = 4 * x + 2 * y + c
        local = [pltpu.make_async_copy(ins[a], outs[a].at[me], send.at[a * N_DEV]) for a in range(n)]
        remote = [_remote(ins[a], outs[a].at[me], send, recv, a * N_DEV + k, _flip(k)) for a in range(n) for k in range(1, N_DEV)]
        return local, remote

    def start(ins, outs, send, recv):
        local, remote = copies(ins, outs, send, recv)
        for cp in local + remote:
            cp.start()

    def finish(ins, outs, send, recv):
        local, remote = copies(ins, outs, send, recv)
        for cp in remote + local:
            cp.wait()

    shapes = [jax.ShapeDtypeStruct((N_DEV,) + a.shape, a.dtype) for a in arrs]
    return _Phase(arrs, shapes, {}, n * N_DEV, start, finish, then)


def _phase_small_exchange(arr, then):
    def copies(ins, outs, send, recv):
        x, y, c = _place()
        me = 4 * x + 2 * y + c
        local = pltpu.make_async_copy(ins[0].at[me], outs[0].at[me], send.at[0])
        remote = []
        for k in range(1, N_DEV):
            px, py, pc = _flip(k)
            remote.append(_remote(ins[0].at[4 * px + 2 * py + pc], outs[0].at[me], send, recv, k, (px, py, pc)))
        return [local] + remote

    def start(ins, outs, send, recv):
        for cp in copies(ins, outs, send, recv):
            cp.start()

    def finish(ins, outs, send, recv):
        for cp in copies(ins, outs, send, recv):
            cp.wait()

    return _Phase([arr], [jax.ShapeDtypeStruct(arr.shape, arr.dtype)], {}, N_DEV, start, finish, then)


def _after(*arrs):
    nothing = lambda *args: None
    return _Phase(arrs, [], {}, 1, nothing, nothing, nothing)


def _flush(name, *phases):
    _, p_outs = _call(None, name, (1,), [], [], [], [], phases=list(phases))
    for p, po in zip(phases, p_outs):
        p.then(po)


class _Big:
    KINDS = {"full": (True, True), "half": (True, False), "shard": (False, True), "block": (False, False)}

    def __init__(self, f3, s3, h3):
        assert s3 != h3
        self.f3, self.s3, self.h3 = tuple(f3), s3, h3
        self.bd = tuple(f3[a] // (N_CHIPS if a == s3 else 1) // (2 if a == h3 else 1) for a in range(3))
        self.tile = (1, _row_tile(self.bd[1], self.bd[2]), self.bd[2])
        self.grid = tuple(self.bd[a] // self.tile[a] for a in range(3))

    def dims(self, kind):
        chips, halves = self.KINDS[kind]
        return tuple(
            self.bd[a] * (N_CHIPS if chips and a == self.s3 else 1) * (2 if halves and a == self.h3 else 1) for a in range(3)
        )

    def view(self, ref, chip=None, half=None, batch0=0, both_halves=True, part=None):
        start = [batch0, 0, 0]
        size = list(ref.shape)
        size[0] = self.bd[0] * (2 if self.h3 == 0 and both_halves else 1)
        if chip is not None:
            start[self.s3] += chip * self.bd[self.s3]
            size[self.s3] = self.bd[self.s3]
        if half is not None:
            start[self.h3] += half * self.bd[self.h3]
            size[self.h3] = self.bd[self.h3]
        if part is not None:
            size[1] //= 2
            start[1] += part * size[1]
        return ref.at[tuple(pl.ds(st, sz) for st, sz in zip(start, size))]

    def spec(self, chip_from=None, half_from=None, lead=(), batch0=0):
        extra = "grid" in (chip_from, half_from)

        def index(*args):
            pref, idx = args[-1], list(args[int(extra) : -1])
            idx[0] += batch0
            if chip_from:
                idx[self.s3] += (pref[0] if chip_from == "pref" else args[0]) * self.grid[self.s3]
            if half_from:
                idx[self.h3] += (pref[1] if half_from == "pref" else args[0]) * self.grid[self.h3]
            return (0,) * len(lead) + tuple(idx)

        return pl.BlockSpec(tuple(lead) + self.tile, index)


def _same(arrs):
    return [jax.ShapeDtypeStruct(a.shape, a.dtype) for a in arrs]


def _phase_gather_relay(arrs, bigs, second, whole_first, then):
    n = len(arrs)
    per = 4 if second and not whole_first else 2

    def copies(outs, send, recv, arriving):
        x, y, c = _place()
        me, xn, yn, dg = (x, y), (1 - x, y), (x, 1 - y), (1 - x, 1 - y)
        if not second:
            part = (None, None) if whole_first else (0, 1)
            plan = [((xn if arriving else me), part[0], xn), ((yn if arriving else me), part[1], yn)]
        elif whole_first:
            plan = [(dg, 0, yn), (dg, 1, xn)] if arriving else [(xn, 0, yn), (yn, 1, xn)]
        elif arriving:
            plan = [(yn, 0, yn), (dg, 0, yn), (xn, 1, xn), (dg, 1, xn)]
        else:
            plan = [(me, 0, yn), (xn, 0, yn), (me, 1, xn), (yn, 1, xn)]
        res = []
        for a in range(n):
            for k, (chip, part, to) in enumerate(plan):
                blk = bigs[a].view(outs[a], 2 * chip[0] + chip[1], c, part=part)
                res.append(_remote(blk, blk, send, recv, per * a + k, (*to, c)))
        return res

    def start(ins, outs, send, recv):
        for cp in copies(outs, send, recv, False):
            cp.start()

    def finish(ins, outs, send, recv):
        for cp in copies(outs, send, recv, True):
            cp.wait_recv()
        for cp in copies(outs, send, recv, False):
            cp.wait_send()

    return _Phase(arrs, _same(arrs), {a: a for a in range(n)}, per * n, start, finish, then)


def _phase_gather_sibling(arrs, bigs, then):
    n = len(arrs)

    def copies(outs, send, recv, arriving):
        x, y, c = _place()
        return [
            _remote(blk, blk, send, recv, 6 * a + 2 * j + part, (x, y, 1 - c))
            for j, chip in enumerate(_other_chips())
            for a in range(n)
            for part in range(2)
            for blk in [bigs[a].view(outs[a], 2 * chip[0] + chip[1], 1 - c if arriving else c, part=part)]
        ]

    def start(ins, outs, send, recv):
        for cp in copies(outs, send, recv, False):
            cp.start()

    def finish(ins, outs, send, recv):
        for cp in copies(outs, send, recv, True):
            cp.wait_recv()
        for cp in copies(outs, send, recv, False):
            cp.wait_send()

    return _Phase(arrs, _same(arrs), {a: a for a in range(n)}, 6 * n, start, finish, then)


def _phase_pair_exchange(grads, bigs, then):
    n = len(grads)

    def copies(ins, outs, send, recv):
        x, y, c = _place()
        srcs = [ins[a] if ins[a].shape == outs[a].shape else bigs[a].view(ins[a], None, 1 - c) for a in range(n)]
        return [_remote(srcs[a], outs[a], send, recv, a, (x, y, 1 - c)) for a in range(n)]

    def start(ins, outs, send, recv):
        for cp in copies(ins, outs, send, recv):
            cp.start()

    def finish(ins, outs, send, recv):
        for cp in copies(ins, outs, send, recv):
            cp.wait()

    shapes = [jax.ShapeDtypeStruct(b.dims("half"), BF16) for b in bigs]
    return _Phase(grads, shapes, {}, n, start, finish, then)


def _phase_chip_exchange(sums, bigs, then):
    n = len(sums)

    def copies(ins, outs, send, recv):
        _, _, c = _place()
        return [
            _remote(bigs[a].view(ins[a], 2 * chip[0] + chip[1], both_halves=False), outs[a].at[j], send, recv, 3 * a + j, (*chip, c))
            for j, chip in enumerate(_other_chips())
            for a in range(n)
        ]

    def start(ins, outs, send, recv):
        for cp in copies(ins, outs, send, recv):
            cp.start()

    def finish(ins, outs, send, recv):
        for cp in copies(ins, outs, send, recv):
            cp.wait()

    shapes = [jax.ShapeDtypeStruct((N_CHIPS - 1,) + b.dims("block"), BF16) for b in bigs]
    return _Phase(sums, shapes, {}, 3 * n, start, finish, then)


_HBM = pl.BlockSpec(memory_space=pltpu.HBM)
_SEM = pl.BlockSpec(memory_space=pltpu.SEMAPHORE)
_DATAFLOW = pltpu.SideEffectType.DATAFLOW_SIDE_EFFECTING


class _InFlight:
    def __init__(self, phase, send, recv, arrays, token):
        self.phase, self.send, self.recv, self.arrays, self.token = phase, send, recv, arrays, token


def _phase_results(phase, refs):
    n_in = len(phase.ins)
    updated = {o: i for i, o in phase.aliases.items()}
    fresh = [o for o in range(len(phase.out_shapes)) if o not in updated]
    return [refs[updated[o]] if o in updated else refs[n_in + fresh.index(o)] for o in range(len(phase.out_shapes))]


def _split_start(phase, name):
    n_in = len(phase.ins)
    fresh = [s for o, s in enumerate(phase.out_shapes) if o not in phase.aliases.values()]
    arrays = list(phase.ins) + [lax.empty(s.shape, s.dtype) for s in fresh]
    n = len(arrays)

    def body(*refs):
        phase.start(refs[:n_in], _phase_results(phase, refs[:n]), refs[n], refs[n + 1])
        refs[-1][...] = jnp.zeros_like(refs[-1])

    operands = [pltpu.with_memory_space_constraint(a, pltpu.HBM) for a in arrays]
    res = pl.pallas_call(
        body, name=name,
        out_shape=[pltpu.SemaphoreType.DMA((phase.n_sems,)), pltpu.SemaphoreType.DMA((phase.n_sems,))]
        + [pltpu.HBM(a.shape, a.dtype) for a in arrays] + [jax.ShapeDtypeStruct((8, 128), F32)],
        in_specs=[_HBM] * n, out_specs=[_SEM, _SEM] + [_HBM] * n + [pl.BlockSpec(memory_space=pltpu.VMEM)],
        input_output_aliases={i: 2 + i for i in range(n)},
        compiler_params=pltpu.CompilerParams(has_side_effects=_DATAFLOW),
    )(*operands)
    return _InFlight(phase, res[0], res[1], list(res[2 : 2 + n]), res[-1])


def _split_wait(flight, after, name):
    phase, n = flight.phase, len(flight.arrays)
    n_in = len(phase.ins)

    def body(*refs):
        phase.finish(refs[:n_in], _phase_results(phase, refs[:n]), refs[n], refs[n + 1])

    res = pl.pallas_call(
        body, name=name, out_shape=[pltpu.HBM(a.shape, a.dtype) for a in flight.arrays],
        in_specs=[_HBM] * n + [_SEM, _SEM] + [_ANY] * len(after), out_specs=[_HBM] * n,
        input_output_aliases={i: i for i in range(n)},
        compiler_params=pltpu.CompilerParams(has_side_effects=_DATAFLOW),
    )(*flight.arrays, flight.send, flight.recv, *after)
    res = list(res)
    phase.then(_phase_results(phase, res))
    return res[:n_in]


def _phase_pair_broadcast(stacks, bigs, batch0s, then):
    n = len(stacks)

    def start(ins, outs, send, recv):
        x, y, c = _place()
        for a in range(n):
            blk = bigs[a].view(outs[a], None, c, batch0s[a])
            _remote(blk, blk, send, recv, a, (x, y, 1 - c)).start()

    def finish(ins, outs, send, recv):
        x, y, c = _place()
        for a in range(n):
            mine = bigs[a].view(outs[a], None, c, batch0s[a])
            theirs = bigs[a].view(outs[a], None, 1 - c, batch0s[a])
            _remote(mine, mine, send, recv, a, (x, y, 1 - c)).wait_send()
            _remote(theirs, theirs, send, recv, a, (x, y, 1 - c)).wait_recv()

    return _Phase(stacks, _same(stacks), {a: a for a in range(n)}, n, start, finish, then)


def _tile_call(body, name, big, where, extra, ins, in_specs, out_specs, out_shape, phases=()):
    grid = ((extra,) if extra else ()) + big.grid
    return _call(body, name, grid, in_specs, out_specs, out_shape, ins, prefetch=(where,), phases=phases)


def _cast_into_full(w_stack, batch0, big, where, name, phases=()):
    def body(_, w_ref, o_ref):
        o_ref[...] = w_ref[...].astype(BF16)

    return _tile_call(
        body, name, big, where, 2, [w_stack], [big.spec(None, "grid", batch0=batch0)], [big.spec("pref", "grid")],
        [jax.ShapeDtypeStruct(big.dims("full"), BF16)], phases,
    )


def _pair_sum(g_full, recv_half, big, where, name, phases=()):
    def body(_, g_ref, r_ref, o_ref):
        o_ref[...] = (g_ref[...].astype(F32) + r_ref[...].astype(F32)).astype(BF16)

    half = big.spec("grid", None)
    return _tile_call(
        body, name, big, where, N_CHIPS, [g_full, recv_half], [big.spec("grid", "pref"), half], [half],
        [jax.ShapeDtypeStruct(big.dims("half"), BF16)], phases,
    )


def _chip_sum(chip_sum, parts, big, where, stack, stack_shape, batch0, name, phases=()):
    def body(_, own_ref, p_ref, *rest):
        acc = own_ref[...].astype(F32)
        for k in range(N_CHIPS - 1):
            acc = acc + p_ref[k].astype(F32)
        rest[-1][...] = acc

    ins = [chip_sum, parts] + ([stack] if stack is not None else [])
    in_specs = [big.spec("pref", None), big.spec(None, None, lead=(N_CHIPS - 1,))] + ([_ANY] if stack is not None else [])
    return _call(
        body, name, big.grid, in_specs, [big.spec(None, "pref", batch0=batch0)], [jax.ShapeDtypeStruct(stack_shape, F32)], ins,
        prefetch=(where,), phases=phases, in_place={2: 0} if stack is not None else None,
    )


def _adam_stack(w, g, m, v, name, after=()):
    b, r, c = w.shape
    tr = _row_tile(r, c, ADAM_BLOCK_ELEMS)

    def body(w_ref, g_ref, m_ref, v_ref, *rest):
        go_ref, d_ref, mo_ref, vo_ref = rest[-4:]
        gv = g_ref[...]
        d, mo, vo = _adam(w_ref[...], gv, m_ref[...], v_ref[...])
        go_ref[...] = gv
        d_ref[...] = d
        mo_ref[...] = mo
        vo_ref[...] = vo

    spec = pl.BlockSpec((1, tr, c), lambda bb, i: (bb, i, 0))
    outs, _ = _call(
        body, name, (b, r // tr), [spec] * 4 + [_ANY] * len(after), [spec] * 4, [jax.ShapeDtypeStruct(w.shape, F32)] * 4,
        [w, g, m, v, *after],
    )
    return outs


def _mod_fwd(c_all, w_mod, b_cols, phases=()):
    n_layers, d, n = w_mod.shape
    tn = _pick(n, (768, 512, 384, 256, 128))

    def body(c_ref, w_ref, b_ref, o_ref):
        cv = c_ref[...]
        ca = (cv * _sigmoid(cv)).astype(BF16)
        o_ref[0] = _dot(ca, w_ref[0].astype(BF16)) + b_ref[0]

    return _call(
        body, "mod_fwd", (n_layers, n // tn),
        [
            pl.BlockSpec((N_DEV, d), lambda l, j: (0, 0)),
            pl.BlockSpec((1, d, tn), lambda l, j: (l, 0, j)),
            pl.BlockSpec((1, 1, tn), lambda l, j: (l, 0, j)),
        ],
        [pl.BlockSpec((1, N_DEV, tn), lambda l, j: (l, 0, j))],
        [jax.ShapeDtypeStruct((n_layers, N_DEV, n), F32)], [c_all, w_mod, b_cols], phases=phases,
    )


def _mod_bwd_adam(c_all_t, dmod_cols, w, m, v, after=()):
    n_layers, d, n = w.shape
    tn = _pick(n, (384, 256, 128))

    def body(c_ref, dm_ref, w_ref, m_ref, v_ref, *rest):
        g_ref, d_ref, mo_ref, vo_ref = rest[-4:]
        cv = c_ref[...]
        ca = (cv * _sigmoid(cv)).astype(BF16)
        g = _dot(ca, dm_ref[0].astype(BF16))
        g_ref[0] = g
        dl, mo, vo = _adam(w_ref[0], g, m_ref[0], v_ref[0])
        d_ref[0] = dl
        mo_ref[0] = mo
        vo_ref[0] = vo

    wspec = pl.BlockSpec((1, d, tn), lambda l, j: (l, 0, j))
    outs, _ = _call(
        body, "mod_bwd_adam", (n_layers, n // tn),
        [pl.BlockSpec((d, N_DEV), lambda l, j: (0, 0)), pl.BlockSpec((1, N_DEV, tn), lambda l, j: (l, 0, j)), wspec, wspec, wspec]
        + [_ANY] * len(after),
        [wspec] * 4, [jax.ShapeDtypeStruct(w.shape, F32)] * 4, [c_all_t, dmod_cols, w, m, v, *after],
    )
    return outs


def _ffn_fwd(x, vec, w_in, w_out, name, phases=()):
    s, d = x.shape
    f = w_out.shape[1]
    tm = _pick(s, (1024, 512, 256, 128))
    tf = _pick(f, (256, 128))
    nf = f // tf

    def body(x_ref, vec_ref, wg_ref, wu_ref, wo_ref, xo_ref, g_ref, u_ref, y_ref, h_sc, acc_sc):
        j = pl.program_id(1)

        @pl.when(j == 0)
        def _():
            h_sc[...] = _modulate(x_ref[...], vec_ref).astype(BF16)
            acc_sc[...] = jnp.zeros_like(acc_sc)

        h = h_sc[...]
        g = _dot(h, wg_ref[0])
        u = _dot(h, wu_ref[0])
        g_ref[...] = g.astype(BF16)
        u_ref[...] = u.astype(BF16)
        a = (g * _sigmoid(g) * u).astype(BF16)
        acc_sc[...] += _dot(a, wo_ref[0])

        @pl.when(j == nf - 1)
        def _():
            yv = acc_sc[...]
            xo_ref[...] = x_ref[...] + 0.5 * vec_ref[3:4, :] * yv
            y_ref[...] = yv.astype(BF16)

    row = pl.BlockSpec((tm, d), lambda i, j: (i, 0))
    hid = pl.BlockSpec((tm, tf), lambda i, j: (i, j))
    return _call(
        body, name, (s // tm, nf),
        [
            row,
            pl.BlockSpec((8, d), lambda i, j: (0, 0)),
            pl.BlockSpec((1, d, tf), lambda i, j: (0, 0, j)),
            pl.BlockSpec((1, d, tf), lambda i, j: (0, 0, nf + j)),
            pl.BlockSpec((1, tf, d), lambda i, j: (0, j, 0)),
        ],
        [row, hid, hid, row],
        [
            jax.ShapeDtypeStruct((s, d), F32),
            jax.ShapeDtypeStruct((s, f), BF16),
            jax.ShapeDtypeStruct((s, f), BF16),
            jax.ShapeDtypeStruct((s, d), BF16),
        ],
        [x, vec, w_in, w_in, w_out],
        scratch=[pltpu.VMEM((tm, d), BF16), pltpu.VMEM((tm, d), F32)], phases=phases,
    )


def _ffn_bwd(dxo, x, vec, gg, uu, y, w_in, w_out, name, phases=()):
    s, d = x.shape
    f = w_out.shape[1]
    tm = _pick(s, (512, 256, 128))
    tf = _pick(f, (256, 128))
    nf = f // tf

    def body(dxo_ref, x_ref, vec_ref, g_ref, u_ref, y_ref, wg_ref, wu_ref, wo_ref,
             dx_ref, dg_ref, du_ref, a_ref, h_ref, dy_ref, dvec_ref, acc_sc):
        i, j = pl.program_id(0), pl.program_id(1)

        @pl.when((i == 0) & (j == 0))
        def _():
            dvec_ref[...] = jnp.zeros_like(dvec_ref)

        @pl.when(j == 0)
        def _():
            dxo_v = dxo_ref[...]
            dy_ref[...] = (0.5 * vec_ref[3:4, :] * dxo_v).astype(BF16)
            dvec_ref[3:4, :] += 0.5 * jnp.sum(dxo_v * y_ref[...].astype(F32), axis=0, keepdims=True)
            acc_sc[...] = jnp.zeros_like(acc_sc)

        da = _dot_nt(dy_ref[...], wo_ref[0])
        g = g_ref[...].astype(F32)
        u = u_ref[...].astype(F32)
        sig = _sigmoid(g)
        sl = g * sig
        a_ref[...] = (sl * u).astype(BF16)
        dg = (da * u * (sig * (1.0 + g * (1.0 - sig)))).astype(BF16)
        du = (da * sl).astype(BF16)
        dg_ref[...] = dg
        du_ref[...] = du
        acc_sc[...] += _dot_nt(dg, wg_ref[0]) + _dot_nt(du, wu_ref[0])

        @pl.when(j == nf - 1)
        def _():
            dx, h = _modulate_bwd(x_ref[...], acc_sc[...], vec_ref, dvec_ref)
            dx_ref[...] = dxo_ref[...] + dx
            h_ref[...] = h.astype(BF16)

    row = pl.BlockSpec((tm, d), lambda i, j: (i, 0))
    hid = pl.BlockSpec((tm, tf), lambda i, j: (i, j))
    vecs = pl.BlockSpec((8, d), lambda i, j: (0, 0))
    return _call(
        body, name, (s // tm, nf),
        [
            row, row, vecs, hid, hid, row,
            pl.BlockSpec((1, d, tf), lambda i, j: (0, 0, j)),
            pl.BlockSpec((1, d, tf), lambda i, j: (0, 0, nf + j)),
            pl.BlockSpec((1, tf, d), lambda i, j: (0, j, 0)),
        ],
        [row, hid, hid, hid, row, row, vecs],
        [
            jax.ShapeDtypeStruct((s, d), F32),
            jax.ShapeDtypeStruct((s, f), BF16),
            jax.ShapeDtypeStruct((s, f), BF16),
            jax.ShapeDtypeStruct((s, f), BF16),
            jax.ShapeDtypeStruct((s, d), BF16),
            jax.ShapeDtypeStruct((s, d), BF16),
            jax.ShapeDtypeStruct((8, d), F32),
        ],
        [dxo, x, vec, gg, uu, y, w_in, w_in, w_out],
        scratch=[pltpu.VMEM((tm, d), F32)], phases=phases,
    )


def _grad_half(a, bs, big, where, mine, recv, name, phases=()):
    s, k1 = a.shape
    n = bs[0].shape[1]
    groups = len(bs)
    rows_halved = big.h3 == 1
    assert rows_halved or groups == 1
    kk, nn = (k1 // 2, n) if rows_halved else (k1, n // 2)
    tk = _pick(kk, (1408, 1024, 512, 256, 128))
    tn = _pick(nn, (1408, 1024, 640, 512, 256, 128))
    nkb, nnb = kk // tk, nn // tn
    assert (recv is None) == (not mine)

    def half(pref):
        return pref[1] if mine else 1 - pref[1]

    def body(_, a_ref, *rest):
        q = pl.program_id(1)
        for p in range(groups):

            @pl.when(q == p)
            def _(p=p):
                acc = _dot_tn(a_ref[...], rest[p][...])
                if recv is not None:
                    acc = acc + rest[groups][0].astype(F32)
                rest[-1][0] = acc.astype(BF16)

    def b_block(p):
        def index(i, q, j, pref):
            jj = jnp.where(q == p, j, jnp.where(q < p, 0, nnb - 1))
            return (0, jj + (0 if rows_halved else half(pref) * nnb))

        return pl.BlockSpec((s, tn), index)

    out_spec = pl.BlockSpec((1, tk, tn), lambda i, q, j, pref: (0, i, q * nnb + j))
    in_specs = [pl.BlockSpec((s, tk), lambda i, q, j, pref: (0, i + (half(pref) * nkb if rows_halved else 0)))]
    in_specs += [b_block(p) for p in range(groups)]
    ins = [a, *bs]
    if recv is not None:
        in_specs.append(out_spec)
        ins.append(recv)
    return _call(
        body, name, (nkb, groups, nnb), in_specs, [out_spec], [jax.ShapeDtypeStruct(big.dims("half"), BF16)], ins,
        prefetch=(where,), phases=phases,
    )


def _proj_mod_fwd(x, vec, w, phases=()):
    s, d = x.shape
    n = w.shape[2]
    tm = _pick(s, (1024, 512, 256, 128))
    tn = _pick(n, (640, 512, 256, 128))

    def body(x_ref, vec_ref, w_ref, o_ref, h_sc):
        @pl.when(pl.program_id(1) == 0)
        def _():
            h_sc[...] = _modulate(x_ref[...], vec_ref).astype(BF16)

        o_ref[...] = _dot(h_sc[...], w_ref[0])

    return _call(
        body, "ab_in_fwd", (s // tm, n // tn),
        [
            pl.BlockSpec((tm, d), lambda i, j: (i, 0)),
            pl.BlockSpec((8, d), lambda i, j: (0, 0)),
            pl.BlockSpec((1, d, tn), lambda i, j: (0, 0, j)),
        ],
        [pl.BlockSpec((tm, tn), lambda i, j: (i, j))],
        [jax.ShapeDtypeStruct((s, n), F32)], [x, vec, w],
        scratch=[pltpu.VMEM((tm, d), BF16)], phases=phases,
    )


def _proj_res_fwd(a, w, x, vec, phases=()):
    s, kd = a.shape
    d = x.shape[1]
    tm = _pick(s, (1024, 512, 256, 128))

    def body(a_ref, w_ref, x_ref, vec_ref, xo_ref, y_ref):
        yv = _dot(a_ref[...], w_ref[0])
        xo_ref[...] = x_ref[...] + vec_ref[3:4, :] * yv
        y_ref[...] = yv.astype(BF16)

    row = pl.BlockSpec((tm, d), lambda i: (i, 0))
    return _call(
        body, "ab_out_fwd", (s // tm,),
        [pl.BlockSpec((tm, kd), lambda i: (i, 0)), pl.BlockSpec((1, kd, d), lambda i: (0, 0, 0)), row, pl.BlockSpec((8, d), lambda i: (0, 0))],
        [row, row],
        [jax.ShapeDtypeStruct((s, d), F32), jax.ShapeDtypeStruct((s, d), BF16)], [a, w, x, vec], phases=phases,
    )


def _proj_res_bwd(dxo, y, vec, w, phases=()):
    s, d = dxo.shape
    kd = w.shape[1]
    tm = _pick(s, (1024, 512, 256, 128))

    def body(dxo_ref, y_ref, vec_ref, w_ref, dy_ref, da_ref, dgate_ref):
        @pl.when(pl.program_id(0) == 0)
        def _():
            dgate_ref[...] = jnp.zeros_like(dgate_ref)

        dxo_v = dxo_ref[...]
        dy = (vec_ref[3:4, :] * dxo_v).astype(BF16)
        dy_ref[...] = dy
        dgate_ref[3:4, :] += jnp.sum(dxo_v * y_ref[...].astype(F32), axis=0, keepdims=True)
        da_ref[...] = _dot_nt(dy, w_ref[0]).astype(BF16)

    row = pl.BlockSpec((tm, d), lambda i: (i, 0))
    vecs = pl.BlockSpec((8, d), lambda i: (0, 0))
    return _call(
        body, "ab_out_bwd", (s // tm,),
        [row, row, vecs, pl.BlockSpec((1, kd, d), lambda i: (0, 0, 0))],
        [row, pl.BlockSpec((tm, kd), lambda i: (i, 0)), vecs],
        [jax.ShapeDtypeStruct((s, d), BF16), jax.ShapeDtypeStruct((s, kd), BF16), jax.ShapeDtypeStruct((8, d), F32)],
        [dxo, y, vec, w], phases=phases,
    )


def _proj_mod_bwd(dproj, w, x, vec, dxo, dvec_in, name, phases=()):
    parts, s, n_part = dproj.shape
    d = x.shape[1]
    tm = _pick(s, (512, 256, 128))
    tk = _pick(n_part, (1408, 1280, 1024, 512, 256, 128))
    per_part = n_part // tk
    nk = parts * per_part

    def body(dp_ref, w_ref, x_ref, vec_ref, dxo_ref, dvi_ref, dx_ref, h_ref, dvec_ref, acc_sc):
        i, k = pl.program_id(0), pl.program_id(1)

        @pl.when((i == 0) & (k == 0))
        def _():
            dvec_ref[...] = dvi_ref[...]

        @pl.when(k == 0)
        def _():
            acc_sc[...] = jnp.zeros_like(acc_sc)

        acc_sc[...] += _dot_nt(dp_ref[0], w_ref[0])

        @pl.when(k == nk - 1)
        def _():
            dx, h = _modulate_bwd(x_ref[...], acc_sc[...], vec_ref, dvec_ref)
            dx_ref[...] = dxo_ref[...] + dx
            h_ref[...] = h.astype(BF16)

    row = pl.BlockSpec((tm, d), lambda i, k: (i, 0))
    vecs = pl.BlockSpec((8, d), lambda i, k: (0, 0))
    return _call(
        body, name, (s // tm, nk),
        [
            pl.BlockSpec((1, tm, tk), lambda i, k: (k // per_part, i, k % per_part)),
            pl.BlockSpec((1, d, tk), lambda i, k: (0, 0, k)),
            row, vecs, row, vecs,
        ],
        [row, row, vecs],
        [jax.ShapeDtypeStruct((s, d), F32), jax.ShapeDtypeStruct((s, d), BF16), jax.ShapeDtypeStruct((8, d), F32)],
        [dproj, w, x, vec, dxo, dvec_in], scratch=[pltpu.VMEM((tm, d), F32)], phases=phases,
    )


def _tril(n):
    return lax.broadcasted_iota(jnp.int32, (n, n), 0) >= lax.broadcasted_iota(jnp.int32, (n, n), 1)


def _layernorm_stats(gv):
    mu = jnp.mean(gv, axis=-1, keepdims=True)
    cen = gv - mu
    rstd = lax.rsqrt(jnp.mean(cen * cen, axis=-1, keepdims=True) + EPS)
    return cen * rstd, rstd


def _shift_down(q, k, above_ref, c_cg, c_xb, first):
    width = q.shape[1]
    rows = lax.broadcasted_iota(jnp.int32, q.shape, 0)
    out = pltpu.roll(q, k, 0)
    for r in range(k):
        src = CONV_HALO - k + r
        above = above_ref[src : src + 1, c_cg : c_cg + width] * above_ref[src : src + 1, c_xb : c_xb + width]
        above = jnp.where(first, 0.0, above)
        out = jnp.where(rows == r, above, out)
    return out


def _ab_mix_fwd(proj, norm_v, w_s, b_rows, conv_w, phases=()):
    s, n = proj.shape
    heads, chunk, _ = w_s.shape
    da = norm_v.shape[1]
    hd = da // heads
    db = conv_w.shape[1]
    tm = _pick(s, (512, 256, 128))

    def body(p_ref, ph_ref, nv_ref, ws_ref, b_ref, cw_ref, o_ref):
        first = pl.program_id(0) == 0
        gu, _ = _gelu(p_ref[:, 0:da])
        gv, _ = _gelu(p_ref[:, da : 2 * da])
        xhat, _ = _layernorm_stats(gv)
        vn = (xhat * nv_ref[...]).astype(BF16)
        mask = _tril(chunk)
        for hh in range(heads):
            wm = jnp.where(mask, ws_ref[hh], 0.0).astype(BF16)
            cols = slice(hh * hd, (hh + 1) * hd)
            for nn in range(tm // chunk):
                rows = slice(nn * chunk, (nn + 1) * chunk)
                z = _dot(wm, vn[rows, cols]) + b_ref[:, cols]
                o_ref[rows, cols] = (gu[rows, cols] * z).astype(BF16)
        c_cg, c_xb = 2 * da + db, 2 * da + 2 * db
        bg = p_ref[:, 2 * da : 2 * da + db]
        q = p_ref[:, c_cg : c_cg + db] * p_ref[:, c_xb : c_xb + db]
        q1 = _shift_down(q, 1, ph_ref, c_cg, c_xb, first)
        q2 = _shift_down(q, 2, ph_ref, c_cg, c_xb, first)
        conv = cw_ref[0:1, :] * q2 + cw_ref[1:2, :] * q1 + cw_ref[2:3, :] * q
        o_ref[:, da : da + db] = (bg * conv).astype(BF16)

    nh = tm // CONV_HALO
    return _call(
        body, "ab_mix_fwd", (s // tm,),
        [
            pl.BlockSpec((tm, n), lambda i: (i, 0)),
            pl.BlockSpec((CONV_HALO, n), lambda i: (jnp.maximum(i * nh - 1, 0), 0)),
            pl.BlockSpec((1, da), lambda i: (0, 0)),
            pl.BlockSpec((heads, chunk, chunk), lambda i: (0, 0, 0)),
            pl.BlockSpec((chunk, da), lambda i: (0, 0)),
            pl.BlockSpec((3, db), lambda i: (0, 0)),
        ],
        [pl.BlockSpec((tm, da + db), lambda i: (i, 0))],
        [jax.ShapeDtypeStruct((s, da + db), BF16)], [proj, proj, norm_v, w_s, b_rows, conv_w], phases=phases,
    )


def _ab_mix_bwd(proj, dcat, norm_v, w_s, b_rows, conv_w, phases=()):
    s, n = proj.shape
    heads, chunk, _ = w_s.shape
    da = norm_v.shape[1]
    hd = da // heads
    db = conv_w.shape[1]
    tm = _pick(s, (512, 256, 128))
    nblk = s // tm
    dhalo = 2 * CONV_HALO

    def body(p_ref, pa_ref, pb_ref, dc_ref, dcb_ref, nv_ref, ws_ref, b_ref, cw_ref,
             dp_ref, dnv_ref, dws_ref, dzs_ref, dcw_ref, dvn_sc):
        i = pl.program_id(0)
        first, last = i == 0, i == nblk - 1

        @pl.when(first)
        def _():
            dnv_ref[...] = jnp.zeros_like(dnv_ref)
            dws_ref[...] = jnp.zeros_like(dws_ref)
            dzs_ref[...] = jnp.zeros_like(dzs_ref)
            dcw_ref[...] = jnp.zeros_like(dcw_ref)

        uu = p_ref[:, 0:da]
        gu, gu_grad = _gelu(uu)
        gv, gv_grad = _gelu(p_ref[:, da : 2 * da])
        xhat, rstd = _layernorm_stats(gv)
        nv = nv_ref[...]
        vn = (xhat * nv).astype(BF16)
        dya = dc_ref[:, 0:da].astype(F32)
        dz = (dya * gu).astype(BF16)
        mask = _tril(chunk)
        for hh in range(heads):
            wm = jnp.where(mask, ws_ref[hh], 0.0).astype(BF16)
            cols = slice(hh * hd, (hh + 1) * hd)
            dws = jnp.zeros((chunk, chunk), F32)
            for nn in range(tm // chunk):
                rows = slice(nn * chunk, (nn + 1) * chunk)
                z = _dot(wm, vn[rows, cols]) + b_ref[:, cols]
                dp_ref[rows, cols] = (dya[rows, cols] * z * gu_grad[rows, cols]).astype(BF16)
                dz_blk = dz[rows, cols]
                dws = dws + _dot_nt(dz_blk, vn[rows, cols])
                dzs_ref[:, cols] += dz_blk.astype(F32)
                dvn = _dot_tn(wm, dz_blk)
                dnv_ref[:, cols] += jnp.sum(dvn * xhat[rows, cols], axis=0, keepdims=True)
                dvn_sc[rows, cols] = dvn
            dws_ref[hh] += jnp.where(mask, dws, 0.0)
        dxhat = dvn_sc[...] * nv
        dgv = rstd * (dxhat - jnp.mean(dxhat, axis=-1, keepdims=True) - xhat * jnp.mean(dxhat * xhat, axis=-1, keepdims=True))
        dp_ref[:, da : 2 * da] = (dgv * gv_grad).astype(BF16)

        c_bg, c_cg, c_xb = 2 * da, 2 * da + db, 2 * da + 2 * db
        bg = p_ref[:, c_bg : c_bg + db]
        cg = p_ref[:, c_cg : c_cg + db]
        xb = p_ref[:, c_xb : c_xb + db]
        q = cg * xb
        q1 = _shift_down(q, 1, pa_ref, c_cg, c_xb, first)
        q2 = _shift_down(q, 2, pa_ref, c_cg, c_xb, first)
        dyb = dc_ref[:, da : da + db].astype(F32)
        conv = cw_ref[0:1, :] * q2 + cw_ref[1:2, :] * q1 + cw_ref[2:3, :] * q
        dp_ref[:, c_bg : c_bg + db] = (dyb * conv).astype(BF16)
        e = dyb * bg
        dcw_ref[0:1, :] += jnp.sum(e * q2, axis=0, keepdims=True)
        dcw_ref[1:2, :] += jnp.sum(e * q1, axis=0, keepdims=True)
        dcw_ref[2:3, :] += jnp.sum(e * q, axis=0, keepdims=True)
        rows = lax.broadcasted_iota(jnp.int32, e.shape, 0)
        dq = cw_ref[2:3, :] * e
        for kk in (1, 2):
            ek = pltpu.roll(e, tm - kk, 0)
            for r in range(kk):
                below = dcb_ref[r : r + 1, da : da + db].astype(F32) * pb_ref[r : r + 1, c_bg : c_bg + db]
                below = jnp.where(last, 0.0, below)
                ek = jnp.where(rows == tm - kk + r, below, ek)
            dq = dq + cw_ref[2 - kk : 3 - kk, :] * ek
        dp_ref[:, c_cg : c_cg + db] = (dq * xb).astype(BF16)
        dp_ref[:, c_xb : c_xb + db] = (dq * cg).astype(BF16)

    nh = tm // CONV_HALO
    nhb = tm // dhalo
    const2 = lambda i: (0, 0)
    return _call(
        body, "ab_mix_bwd", (nblk,),
        [
            pl.BlockSpec((tm, n), lambda i: (i, 0)),
            pl.BlockSpec((CONV_HALO, n), lambda i: (jnp.maximum(i * nh - 1, 0), 0)),
            pl.BlockSpec((CONV_HALO, n), lambda i: (jnp.minimum((i + 1) * nh, s // CONV_HALO - 1), 0)),
            pl.BlockSpec((tm, da + db), lambda i: (i, 0)),
            pl.BlockSpec((dhalo, da + db), lambda i: (jnp.minimum((i + 1) * nhb, s // dhalo - 1), 0)),
            pl.BlockSpec((1, da), const2),
            pl.BlockSpec((heads, chunk, chunk), lambda i: (0, 0, 0)),
            pl.BlockSpec((chunk, da), const2),
            pl.BlockSpec((3, db), const2),
        ],
        [
            pl.BlockSpec((tm, n), lambda i: (i, 0)),
            pl.BlockSpec((1, da), const2),
            pl.BlockSpec((heads, chunk, chunk), lambda i: (0, 0, 0)),
            pl.BlockSpec((chunk, da), const2),
            pl.BlockSpec((3, db), const2),
        ],
        [
            jax.ShapeDtypeStruct((s, n), BF16),
            jax.ShapeDtypeStruct((1, da), F32),
            jax.ShapeDtypeStruct((heads, chunk, chunk), F32),
            jax.ShapeDtypeStruct((chunk, da), F32),
            jax.ShapeDtypeStruct((3, db), F32),
        ],
        [proj, proj, proj, dcat, dcat, norm_v, w_s, b_rows, conv_w],
        scratch=[pltpu.VMEM((tm, da), F32)], phases=phases,
    )


def _pool_counts(tm, i, w):
    t = i * tm + lax.broadcasted_iota(jnp.int32, (tm, 1), 0)
    return jnp.minimum(t + 1, w).astype(F32)


def _pool_fwd(x, vec, w_grp, scale, phases=()):
    s, d = x.shape
    groups, gd, _ = w_grp.shape
    tm = _pick(s, (512, 256, 128))

    def body(x_ref, xa_ref, vec_ref, w_ref, sc_ref, xo_ref, p_ref, o_ref):
        i = pl.program_id(0)
        h = _modulate(x_ref[...], vec_ref)
        ha = jnp.where(i == 0, 0.0, _modulate(xa_ref[...], vec_ref))
        ext = jnp.concatenate([ha, h], axis=0)
        for gi, w in enumerate(POOL_WINDOWS):
            cols = slice(gi * gd, (gi + 1) * gd)
            acc = ext[:, cols]
            step = 1
            while step < w:
                acc = acc + pltpu.roll(acc, step, 0)
                step *= 2
            p = (acc[POOL_HALO:, :] / _pool_counts(tm, i, w) - h[:, cols]).astype(BF16)
            p_ref[:, cols] = p
            o_ref[:, cols] = _dot(p, w_ref[gi]).astype(BF16)
        xo_ref[...] = x_ref[...] + vec_ref[3:4, :] * (o_ref[...].astype(F32) * sc_ref[...])

    nh = tm // POOL_HALO
    row = pl.BlockSpec((tm, d), lambda i: (i, 0))
    return _call(
        body, "pool_fwd", (s // tm,),
        [
            row,
            pl.BlockSpec((POOL_HALO, d), lambda i: (jnp.maximum(i * nh - 1, 0), 0)),
            pl.BlockSpec((8, d), lambda i: (0, 0)),
            pl.BlockSpec((groups, gd, gd), lambda i: (0, 0, 0)),
            pl.BlockSpec((1, d), lambda i: (0, 0)),
        ],
        [row, row, row],
        [jax.ShapeDtypeStruct((s, d), F32), jax.ShapeDtypeStruct((s, d), BF16), jax.ShapeDtypeStruct((s, d), BF16)],
        [x, x, vec, w_grp, scale], phases=phases,
    )


def _pool_bwd(dxo, x, vec, p, o, w_grp, scale, phases=()):
    s, d = x.shape
    groups, gd, _ = w_grp.shape
    tm = _pick(s, (512, 256, 128))
    nblk = s // tm

    def body(dxo_ref, dxb_ref, x_ref, vec_ref, p_ref, o_ref, w_ref, sc_ref, dx_ref, dw_ref, dsc_ref, dvec_ref, dw_sc):
        i = pl.program_id(0)

        @pl.when(i == 0)
        def _():
            dw_sc[...] = jnp.zeros_like(dw_sc)
            dsc_ref[...] = jnp.zeros_like(dsc_ref)
            dvec_ref[...] = jnp.zeros_like(dvec_ref)

        gate, sc = vec_ref[3:4, :], sc_ref[...]
        dxo_v = dxo_ref[...]
        ov = o_ref[...].astype(F32)
        dvec_ref[3:4, :] += jnp.sum(dxo_v * (ov * sc), axis=0, keepdims=True)
        dy = gate * dxo_v
        dsc_ref[...] += jnp.sum(dy * ov, axis=0, keepdims=True)
        dout = (dy * sc).astype(BF16)
        dout_b = jnp.where(i == nblk - 1, 0.0, gate * dxb_ref[...] * sc).astype(BF16)
        for gi, w in enumerate(POOL_WINDOWS):
            cols = slice(gi * gd, (gi + 1) * gd)
            dw_sc[gi] += _dot_tn(p_ref[:, cols], dout[:, cols])
            wb = w_ref[gi]
            dp = _dot_nt(dout[:, cols], wb)
            dp_b = _dot_nt(dout_b[:, cols], wb)
            e = dp / _pool_counts(tm, i, w)
            t_below = (i + 1) * tm + lax.broadcasted_iota(jnp.int32, (POOL_HALO, 1), 0)
            e_b = dp_b / jnp.minimum(t_below + 1, w).astype(F32)
            acc = jnp.concatenate([e, e_b], axis=0)
            step = 1
            while step < w:
                acc = acc + pltpu.roll(acc, tm + POOL_HALO - step, 0)
                step *= 2
            dx_ref[:, cols] = acc[:tm, :] - dp
        dx, _ = _modulate_bwd(x_ref[...], dx_ref[...], vec_ref, dvec_ref)
        dx_ref[...] = dxo_v + dx

        @pl.when(i == nblk - 1)
        def _():
            dw_ref[...] = dw_sc[...].astype(BF16)

    nh = tm // POOL_HALO
    row = pl.BlockSpec((tm, d), lambda i: (i, 0))
    vecs = pl.BlockSpec((8, d), lambda i: (0, 0))
    wspec = pl.BlockSpec((groups, gd, gd), lambda i: (0, 0, 0))
    return _call(
        body, "pool_bwd", (nblk,),
        [
            row,
            pl.BlockSpec((POOL_HALO, d), lambda i: (jnp.minimum((i + 1) * nh, s // POOL_HALO - 1), 0)),
            row, vecs, row, row, wspec,
            pl.BlockSpec((1, d), lambda i: (0, 0)),
        ],
        [row, wspec, pl.BlockSpec((1, d), lambda i: (0, 0)), vecs],
        [
            jax.ShapeDtypeStruct((s, d), F32),
            jax.ShapeDtypeStruct((groups, gd, gd), BF16),
            jax.ShapeDtypeStruct((1, d), F32),
            jax.ShapeDtypeStruct((8, d), F32),
        ],
        [dxo, dxo, x, vec, p, o, w_grp, scale],
        scratch=[pltpu.VMEM((groups, gd, gd), F32)], phases=phases,
    )


def _loss_head(x, gain, target, phases=()):
    s, d = x.shape
    tm = _pick(s, (512, 256, 128))

    def body(x_ref, g_ref, t_ref, dx_ref, aux_ref):
        @pl.when(pl.program_id(0) == 0)
        def _():
            aux_ref[...] = jnp.zeros_like(aux_ref)

        xv = x_ref[...]
        rstd = _rstd(xv)
        r = xv * rstd
        gain_v = g_ref[...]
        err = r * gain_v - t_ref[...]
        aux_ref[1:2, :] += jnp.sum(err * err, axis=0, keepdims=True)
        dout = err * (1.0 / d)
        aux_ref[0:1, :] += jnp.sum(dout * r, axis=0, keepdims=True)
        dr = dout * gain_v
        dx_ref[...] = rstd * (dr - r * jnp.mean(dr * r, axis=-1, keepdims=True))

    row = pl.BlockSpec((tm, d), lambda i: (i, 0))
    return _call(
        body, "loss_head", (s // tm,),
        [row, pl.BlockSpec((1, d), lambda i: (0, 0)), row],
        [row, pl.BlockSpec((8, d), lambda i: (0, 0))],
        [jax.ShapeDtypeStruct((s, d), F32), jax.ShapeDtypeStruct((8, d), F32)], [x, gain, target], phases=phases,
    )


def _small_adam(gathered, gathered_ws, layout, smalls, chip):
    names = list(smalls)
    n = len(names)
    loss_row, _, _, n_feat = layout["loss"]

    def body(*refs):
        chip_ref, g_ref, gws_ref = refs[0], refs[1], refs[2]
        wmv = refs[3 : 3 + 3 * n]
        outs = refs[3 + 3 * n : 3 + 7 * n]
        total = refs[-1]
        total[...] = g_ref[0]
        for kdev in range(1, N_DEV):
            total[...] += g_ref[kdev]
        total_ws = gws_ref[0]
        for kdev in range(1, N_DEV):
            total_ws = total_ws + gws_ref[kdev]
        my_chip = chip_ref[0]
        for a, name in enumerate(names):
            w_ref, m_ref, v_ref = wmv[3 * a : 3 * a + 3]
            if name == "ab_w_s":
                g = total_ws
            else:
                row0, rows, col0, cols = layout[name]
                if col0 is None:
                    g = jnp.zeros((rows, cols), F32)
                    for j in range(N_CHIPS):
                        g = g + jnp.where(my_chip == j, total[row0 : row0 + rows, j * cols : (j + 1) * cols], 0.0)
                else:
                    g = total[row0 : row0 + rows, col0 : col0 + cols]
            dl, mo, vo = _adam(w_ref[...], g, m_ref[...], v_ref[...])
            outs[4 * a][...] = g
            outs[4 * a + 1][...] = dl
            outs[4 * a + 2][...] = mo
            outs[4 * a + 3][...] = vo
        refs[3 + 7 * n][...] = 0.5 * jnp.sum(total[loss_row : loss_row + 1, 0:n_feat], axis=1, keepdims=True) / n_feat

    ins = [gathered, gathered_ws]
    out_shapes = []
    for name in names:
        ins.extend(smalls[name])
        out_shapes.extend([jax.ShapeDtypeStruct(smalls[name][0].shape, F32)] * 4)
    out_shapes.append(jax.ShapeDtypeStruct((1, 1), F32))
    whole = lambda shape: pl.BlockSpec(shape, functools.partial(lambda nd, i, c: (0,) * nd, len(shape)))
    res = pl.pallas_call(
        body, name="small_adam",
        grid_spec=pltpu.PrefetchScalarGridSpec(
            num_scalar_prefetch=1, grid=(1,),
            in_specs=[whole(a.shape) for a in ins], out_specs=[whole(o.shape) for o in out_shapes],
            scratch_shapes=[pltpu.VMEM(gathered.shape[1:], F32)],
        ),
        out_shape=out_shapes,
        compiler_params=pltpu.CompilerParams(dimension_semantics=("arbitrary",), vmem_limit_bytes=VMEM_LIMIT_BYTES),
    )(chip.reshape(1).astype(jnp.int32), *ins)
    return {name: res[4 * a : 4 * a + 4] for a, name in enumerate(names)}, res[4 * n]


def _pad_rows(a, rows=8):
    extra = (-a.shape[0]) % rows
    return jnp.pad(a, ((0, extra), (0, 0))) if extra else a


def _pad_cols(a, cols):
    return jnp.pad(a, ((0, 0), (0, cols - a.shape[1]))) if a.shape[1] < cols else a


def _run(fn, *phases):
    outs, p_outs = fn(list(phases))
    for p, po in zip(phases, p_outs):
        p.then(po)
    return outs


def kernel(x, c, norm_g, w_mod, b_mod, w_ffn_in, w_ffn_out, ab_w_in, ab_norm_v, ab_w_s, ab_b_s, ab_conv_w, ab_w_out, pool_w_grp, pool_scale, final_g, loss_target, m_norm_g, m_w_mod, m_b_mod, m_w_ffn_in, m_w_ffn_out, m_ab_w_in, m_ab_norm_v, m_ab_w_s, m_ab_b_s, m_ab_conv_w, m_ab_w_out, m_pool_w_grp, m_pool_scale, m_final_g, v_norm_g, v_w_mod, v_b_mod, v_w_ffn_in, v_w_ffn_out, v_ab_w_in, v_ab_norm_v, v_ab_w_s, v_ab_b_s, v_ab_conv_w, v_ab_w_out, v_pool_w_grp, v_pool_scale, v_final_g):
    ix, iy, ic = _place()
    chip = 2 * ix + iy
    me = 4 * ix + 2 * iy + ic
    where = jnp.stack([chip, ic]).astype(jnp.int32)
    s, d = x.shape[1], x.shape[2]
    x0 = x.reshape(s, d)
    target = loss_target.reshape(s, d)
    n_layers = norm_g.shape[0]
    dq = d // N_CHIPS
    heads, chunk = ab_w_s.shape[1], ab_w_s.shape[2]
    da = ab_norm_v.shape[1]
    db = ab_conv_w.shape[2] * N_CHIPS
    f_hidden = w_ffn_out.shape[2] * N_CHIPS
    assert n_layers == 2 and da % heads == 0

    cw_pad = _pad_cols(ab_conv_w.reshape(3, db // N_CHIPS), dq)
    packed = jnp.concatenate(
        [_pad_rows(c.reshape(N_CHIPS, dq)), _pad_rows(norm_g.reshape(-1, dq)), _pad_rows(pool_scale.reshape(1, dq)), _pad_rows(cw_pad)],
        axis=0,
    )
    ncol = w_mod.shape[2]
    b_cols = lax.dynamic_slice(b_mod, (0, chip * ncol), (n_layers, ncol)).reshape(n_layers, 1, ncol)
    small = {}

    def small_gather(key, arrs):
        def then(outs):
            small[key] = outs

        return _phase_small_gather(arrs, then)

    stacks = {
        "w_ffn_in": tuple(a.reshape((-1,) + a.shape[2:]) for a in (w_ffn_in, m_w_ffn_in, v_w_ffn_in)),
        "w_ffn_out": tuple(a.reshape((-1,) + a.shape[2:]) for a in (w_ffn_out, m_w_ffn_out, v_w_ffn_out)),
        "ab_w_in": (ab_w_in, m_ab_w_in, v_ab_w_in),
        "ab_w_out": (ab_w_out, m_ab_w_out, v_ab_w_out),
        "pool_w_grp": (pool_w_grp[0], m_pool_w_grp[0], v_pool_w_grp[0]),
    }
    big_in = _Big((1, d, 2 * f_hidden), 2, 1)
    big_out = _Big((1, f_hidden, d), 1, 2)
    units = {}
    for l in range(n_layers):
        for k in range(2):
            units[f"in{l}{k}"] = (big_in, "w_ffn_in", 2 * l + k)
            units[f"out{l}{k}"] = (big_out, "w_ffn_out", 2 * l + k)
    units["abin"] = (_Big((1, d, ab_w_in.shape[2] * N_CHIPS), 2, 1), "ab_w_in", 0)
    units["about"] = (_Big((1, ab_w_out.shape[1] * N_CHIPS, d), 1, 2), "ab_w_out", 0)
    units["pool"] = (_Big((pool_w_grp.shape[1], pool_w_grp.shape[2] * N_CHIPS, pool_w_grp.shape[3]), 1, 0), "pool_w_grp", 0)
    big = {u: g for u, (g, _, _) in units.items()}

    weight = {}
    complete = set()

    def cast(u):
        g, st, b0 = units[u]

        def launch(phases):
            (weight[u],), p_outs = _cast_into_full(stacks[st][0], b0, g, where, "cast_" + u, phases)
            return None, p_outs

        return launch

    def gather_relay(us, second, whole_first):
        def then(outs):
            for u, o in zip(us, outs):
                weight[u] = o

        return _phase_gather_relay([weight[u] for u in us], [big[u] for u in us], second, whole_first, then)

    def gather_sibling(*us):
        def then(outs):
            for u, o in zip(us, outs):
                weight[u] = o
                complete.add(u)

        return _phase_gather_sibling([weight[u] for u in us], [big[u] for u in us], then)

    def w_of(u):
        assert u in complete, u
        return weight[u]

    _run(cast("in00"), small_gather("inputs", [packed]))
    small_all = small["inputs"][0]
    by_chip = small_all[0::2]
    c_all = small_all[:, 0:N_CHIPS, :].reshape(N_DEV, d)
    norm_full = by_chip[:, 8 : 8 + 3 * n_layers, :].transpose(1, 0, 2).reshape(3 * n_layers, d)
    pool_scale_full = by_chip[:, 16:17, :].transpose(1, 0, 2).reshape(1, d)
    conv_full = by_chip[:, 24:27, : db // N_CHIPS].transpose(1, 0, 2).reshape(3, db)
    pieces = [("in00", "out00"), ("abin", "about"), ("in01", "out01"), ("in10", "out10", "pool"), ("in11", "out11")]
    in_flight = {}

    def start_gather(p):
        in_flight[p, 0] = _split_start(gather_relay(pieces[p], False, p == 0), f"gather_{p}_start")

    def relay_gather(p, after=()):
        flight = in_flight.pop((p, 0))
        _split_wait(flight, list(after) + list(started().ins), f"gather_{p}_arrived")
        in_flight[p, 1] = _split_start(gather_relay(pieces[p], True, p == 0), f"gather_{p}_relay")

    def started():
        return _after(*[flight.token for flight in in_flight.values()])

    def finish_gather(p, after, meanwhile=None):
        flight = in_flight.pop((p, 1))
        _split_wait(flight, list(after) + list(started().ins), f"gather_{p}_wait")
        crossing = _split_start(gather_sibling(*pieces[p]), f"gather_{p}_forward")
        behind = [crossing.token]
        if p + 1 < len(pieces):
            relay_gather(p + 1)
        if p + 3 < len(pieces):
            start_gather(p + 3)
        behind = behind + list(started().ins)
        if meanwhile is not None:
            behind = behind + meanwhile(_after(crossing.token))
        _split_wait(crossing, behind, f"gather_{p}_forwarded")

    _run(cast("out00"))
    start_gather(0)
    mod_cols = _run(lambda phases: _mod_fwd(c_all, w_mod, b_cols, phases), started())[0]

    def mod_rows(outs):
        small["mod"] = outs

    _run(cast("about"), started())
    early = [u for piece in pieces[2:4] for u in piece]
    for u in early:
        _run(cast(u), started())
    _run(
        cast("abin"), _phase_small_exchange(mod_cols.transpose(1, 0, 2), mod_rows),
        started(), _after(*[weight[u] for u in early]),
    )
    relay_gather(0)
    start_gather(1)
    start_gather(2)
    for u in pieces[4]:
        _run(cast(u), started())
    mod_mine = small["mod"][0][0::2]
    mod = mod_mine.transpose(1, 0, 2).reshape(n_layers, 3, 3, d)
    vecs = {
        (l, sub): jnp.pad(norm_full[3 * l + sub][None], ((0, 7), (0, 0))) + jnp.pad(mod[l, sub], ((1, 4), (0, 0)))
        for l in range(n_layers)
        for sub in range(3)
    }
    b_rows = jnp.broadcast_to(ab_b_s[0].T[:, :, None], (chunk, heads, da // heads)).reshape(chunk, da)

    saved = {}

    def ffn_forward(xs, l, sub, k, *phases):
        saved[l, sub, "x"] = xs
        xs, gg, uu, yb = _run(
            lambda ph: _ffn_fwd(xs, vecs[l, sub], w_of(f"in{l}{k}"), w_of(f"out{l}{k}"), f"ffn_fwd_{l}{k}", ph), *phases
        )
        saved[l, sub, "act"] = (gg, uu, yb)
        return xs

    finish_gather(0, [vecs[0, 0]] + [weight[u] for u in pieces[4]])
    xs = ffn_forward(x0, 0, 0, 0, started())
    saved[0, 1, "x"] = xs
    finish_gather(1, [xs])
    (proj,) = _run(lambda ph: _proj_mod_fwd(xs, vecs[0, 1], w_of("abin"), ph), started())
    (cat,) = _run(lambda ph: _ab_mix_fwd(proj, ab_norm_v, ab_w_s[0], b_rows, conv_full, ph))
    xs, yb = _run(lambda ph: _proj_res_fwd(cat, w_of("about"), xs, vecs[0, 1], ph))
    saved[0, 1, "act"] = (proj, cat, yb)
    finish_gather(2, [xs])
    xs = ffn_forward(xs, 0, 2, 1, started())
    finish_gather(3, [xs])
    xs = ffn_forward(xs, 1, 0, 0, started())
    saved[1, 1, "x"] = xs
    pooled = []

    def pool_forward(behind):
        pooled.extend(_run(lambda ph: _pool_fwd(xs, vecs[1, 1], w_of("pool"), pool_scale_full, ph), behind))
        return [pooled[0]]

    finish_gather(4, [xs], pool_forward)
    xs, pp, oo = pooled
    saved[1, 1, "act"] = (pp, oo)
    xs = ffn_forward(xs, 1, 2, 1)
    dxs, aux = _run(lambda ph: _loss_head(xs, final_g.reshape(1, d), target, ph))

    grad = {}
    recv = {}
    csum = {}
    parts = {}
    reduced = {}
    done = set()
    dvecs, small_g = {}, {}

    def pair_exchange(*us):
        def then(outs):
            for u, o in zip(us, outs):
                recv[u] = o

        return _phase_pair_exchange([grad[u] for u in us], [big[u] for u in us], then)

    def grad_half(u, a, bs, mine, name, *phases):
        (res,) = _run(lambda ph: _grad_half(a, bs, big[u], where, mine, recv[u] if mine else None, name, ph), *phases)
        return res

    def pair_sum(u, *phases):
        def launch(ph):
            (csum[u],), p_outs = _pair_sum(grad[u], recv[u], big[u], where, "pair_sum_" + u, ph)
            return None, p_outs

        _run(launch, *phases)

    def chip_exchange(*us):
        def then(outs):
            for u, o in zip(us, outs):
                parts[u] = o

        return _phase_chip_exchange([csum[u] for u in us], [big[u] for u in us], then)

    def chip_sum(*us, carried=()):
        for n_u, u in enumerate(us):
            g, st, b0 = units[u]

            def launch(ph):
                (reduced[st],), p_outs = _chip_sum(
                    csum[u], parts[u], g, where, reduced.get(st), stacks[st][0].shape, b0, "chip_sum_" + u, ph
                )
                return None, p_outs

            _run(launch, *(carried if n_u == 0 else ()))

    def pair_broadcast(*us):
        sts = [units[u][1] for u in us]
        assert len(set(sts)) == len(sts)

        def then(outs):
            for u, st, o in zip(us, sts, outs):
                reduced[st] = o
                done.add(u)

        return _phase_pair_broadcast([reduced[st] for st in sts], [big[u] for u in us], [units[u][2] for u in us], then)

    def ffn_backward(dxs, l, sub, k, carried_bwd, carried_send, carried_mine):
        gg, uu, yb = saved[l, sub, "act"]
        w_in, w_out = w_of(f"in{l}{k}"), w_of(f"out{l}{k}")
        uo, ui, tag = f"out{l}{k}", f"in{l}{k}", f"{l}{k}"
        dxs, dg, du, a, h, dy, dvecs[l, sub] = _run(
            lambda ph: _ffn_bwd(dxs, saved[l, sub, "x"], vecs[l, sub], gg, uu, yb, w_in, w_out, "ffn_bwd_" + tag, ph), *carried_bwd()
        )
        grad[uo] = grad_half(uo, a, [dy], False, "dw_out_send_" + tag)
        grad[ui] = grad_half(ui, h, [dg, du], False, "dw_in_send_" + tag, pair_exchange(uo), *carried_send())
        csum[uo] = grad_half(uo, a, [dy], True, "dw_out_" + tag, pair_exchange(ui))
        csum[ui] = grad_half(ui, h, [dg, du], True, "dw_in_" + tag, *carried_mine())
        return dxs

    none = lambda: ()
    dxs = ffn_backward(dxs, 1, 2, 1, none, none, none)
    pp, oo = saved[1, 1, "act"]
    dxs, grad["pool"], small_g["pool_scale"], dvecs[1, 1] = _run(
        lambda ph: _pool_bwd(dxs, saved[1, 1, "x"], vecs[1, 1], pp, oo, w_of("pool"), pool_scale_full, ph)
    )

    def after_11():
        return (chip_exchange("in11", "out11"), pair_exchange("pool"))

    def bcast_11():
        chip_sum("in11", "out11")
        pair_sum("pool")
        return (pair_broadcast("in11", "out11"), chip_exchange("pool"))

    dxs = ffn_backward(dxs, 1, 0, 0, after_11, bcast_11, none)

    def after_10():
        return (chip_exchange("in10", "out10"),)

    def bcast_10():
        chip_sum("in10", "out10", "pool")
        return (pair_broadcast("in10", "out10", "pool"),)

    dxs = ffn_backward(dxs, 0, 2, 1, after_10, bcast_10, none)

    proj, cat, yb = saved[0, 1, "act"]
    out01 = _split_start(chip_exchange("out01"), "reduce_out01_start")
    dy, dcat, dgate = _run(lambda ph: _proj_res_bwd(dxs, yb, vecs[0, 1], w_of("about"), ph), _after(out01.token))
    grad["about"] = grad_half("about", cat, [dy], False, "dw_ab_out_send")
    dproj, small_g["ab_norm_v"], small_g["ab_w_s"], dzs, small_g["ab_conv_w"] = _run(
        lambda ph: _ab_mix_bwd(proj, dcat, ab_norm_v, ab_w_s[0], b_rows, conv_full, ph), pair_exchange("about")
    )
    small_g["ab_b_s"] = dzs.reshape(chunk, heads, da // heads).sum(axis=2).T
    dxs, h, dvecs[0, 1] = _run(
        lambda ph: _proj_mod_bwd(dproj[None], w_of("abin"), saved[0, 1, "x"], vecs[0, 1], dxs, dgate, "ab_in_bwd", ph)
    )
    grad["abin"] = grad_half("abin", h, [dproj], False, "dw_ab_in_send")
    (csum["out01"],) = _split_wait(out01, [grad["abin"]], "reduce_out01_wait")
    chip_sum("out01")
    csum["about"] = grad_half("about", cat, [dy], True, "dw_ab_out", pair_broadcast("out01"), pair_exchange("abin"))
    csum["abin"] = grad_half("abin", h, [dproj], True, "dw_ab_in")

    layout = {}
    tail = {}

    def after_01():
        tail["01"] = _split_start(chip_exchange("in01", "abin", "about"), "reduce_01_start")
        return (_after(tail["01"].token),)

    def pack_small_grads():
        dvec_all = jnp.stack([dvecs[l, sub] for l in range(n_layers) for sub in range(3)])
        dgain = dvec_all[:, 0, :]
        dmod = dvec_all[:, 1:4, :].reshape(3 * 3 * n_layers, d)
        rows = {
            "norm_g": (dgain, None, dq), "final_g": (aux[0:1], 0, d), "pool_scale": (small_g["pool_scale"], None, dq),
            "b_mod": (dmod, 0, d), "ab_norm_v": (small_g["ab_norm_v"], 0, da),
            "ab_conv_w": (small_g["ab_conv_w"], None, db // N_CHIPS), "ab_b_s": (small_g["ab_b_s"], 0, chunk),
            "loss": (aux[1:2], 0, d),
        }
        row0 = 0
        for nm, (pc, col0, cols) in rows.items():
            layout[nm] = (row0, pc.shape[0], col0, cols)
            row0 += pc.shape[0]
        packed_rows = -(-row0 // 8) * 8
        return sum(
            jnp.pad(pc, ((layout[nm][0], packed_rows - layout[nm][0] - pc.shape[0]), (0, d - pc.shape[1])))
            for nm, (pc, _, _) in rows.items()
        )

    def bcast_01():
        csum["in01"], csum["abin"], csum["about"] = _split_wait(tail["01"], [dvecs[0, 0]], "reduce_01_wait")
        chip_sum("in01", "abin", "about")
        grads_small = [pack_small_grads(), small_g["ab_w_s"].reshape(heads * chunk, chunk)]
        tail["small"] = _split_start(small_gather("grads", grads_small), "gather_small_grads_start")
        return (pair_broadcast("in01", "abin", "about"), _after(tail["small"].token))

    def reduce_out00():
        tail["out00"] = _split_start(chip_exchange("out00"), "reduce_out00_start")
        return (_after(tail["out00"].token),)

    dxs = ffn_backward(dxs, 0, 0, 0, after_01, bcast_01, reduce_out00)
    grad_x = dxs.reshape(x.shape)

    last = _split_start(chip_exchange("in00"), "reduce_last_start")
    _split_wait(tail["small"], [last.token], "gather_small_grads_wait")
    g_all, gws_all = small["grads"]

    out = {}

    def adam_stack(st, after=()):
        w3, m3, v3 = stacks[st]
        assert all(u in done for u, (_, ust, _) in units.items() if ust == st), st
        shape = {"w_ffn_in": w_ffn_in.shape, "w_ffn_out": w_ffn_out.shape, "pool_w_grp": pool_w_grp.shape}.get(st, w3.shape)
        out[st] = tuple(a.reshape(shape) for a in _adam_stack(w3, reduced[st], m3, v3, "adam_" + st, after))

    shapes2d = {
        "norm_g": (3 * n_layers, dq), "b_mod": (9 * n_layers, d), "final_g": (1, d), "ab_norm_v": (1, da),
        "pool_scale": (1, dq), "ab_conv_w": (3, db // N_CHIPS), "ab_b_s": (heads, chunk), "ab_w_s": (heads * chunk, chunk),
    }
    small_w = {"norm_g": (norm_g, m_norm_g, v_norm_g), "b_mod": (b_mod, m_b_mod, v_b_mod), "final_g": (final_g, m_final_g, v_final_g),
               "ab_norm_v": (ab_norm_v, m_ab_norm_v, v_ab_norm_v), "pool_scale": (pool_scale, m_pool_scale, v_pool_scale),
               "ab_conv_w": (ab_conv_w, m_ab_conv_w, v_ab_conv_w), "ab_b_s": (ab_b_s, m_ab_b_s, v_ab_b_s), "ab_w_s": (ab_w_s, m_ab_w_s, v_ab_w_s)}
    smalls = {nm: tuple(a.reshape(shapes2d[nm]) for a in wmv) for nm, wmv in small_w.items()}
    small_out, loss = _small_adam(g_all, gws_all, layout, smalls, chip)
    loss = loss.reshape(())
    for nm, res in small_out.items():
        out[nm] = tuple(a.reshape(small_w[nm][0].shape) for a in res)

    mod_row0 = layout["b_mod"][0]
    dmod_all = g_all[:, mod_row0 : mod_row0 + 9 * n_layers, :].reshape(N_DEV, n_layers, 9 * d)
    dmod_cols = lax.dynamic_slice(dmod_all, (0, 0, chip * ncol), (N_DEV, n_layers, ncol)).transpose(1, 0, 2)
    out["w_mod"] = tuple(_mod_bwd_adam(c_all.T, dmod_cols, w_mod, m_w_mod, v_w_mod, (last.token,)))

    (csum["out00"],) = _split_wait(tail["out00"], [out["w_mod"][1]], "reduce_out00_wait")
    chip_sum("out00")
    crossing = _split_start(pair_broadcast("out00"), "broadcast_out00_start")
    for st in ("ab_w_in", "ab_w_out", "pool_w_grp"):
        adam_stack(st, (crossing.token,))
    _split_wait(crossing, [out[st][1] for st in ("ab_w_in", "ab_w_out", "pool_w_grp")], "broadcast_out00_wait")
    (csum["in00"],) = _split_wait(last, [reduced["w_ffn_out"]], "reduce_last_wait")
    chip_sum("in00")
    crossing = _split_start(pair_broadcast("in00"), "broadcast_last_start")
    adam_stack("w_ffn_out", (crossing.token,))
    _split_wait(crossing, [out["w_ffn_out"][1]], "broadcast_last_wait")
    adam_stack("w_ffn_in")

    order = ["norm_g", "w_mod", "b_mod", "w_ffn_in", "w_ffn_out", "ab_w_in", "ab_norm_v", "ab_w_s", "ab_b_s", "ab_conv_w", "ab_w_out", "pool_w_grp", "pool_scale", "final_g"]
    return (loss, grad_x, *[out[nm][0] for nm in order], *[out[nm][1] for nm in order], *[out[nm][2] for nm in order], *[out[nm][3] for nm in order])
```

```python
import functools
import math

import jax
import jax.numpy as jnp
from jax import lax
from jax.experimental import pallas as pl
from jax.experimental.pallas import tpu as pltpu

F32 = jnp.float32
BF16 = jnp.bfloat16
MESH = pl.DeviceIdType.MESH

EPS = 1e-6
ADAM_LR = 0.001
ADAM_B1 = 0.9
ADAM_B2 = 0.999
ADAM_EPS = 1e-08
ADAM_WD = 0.01
ADAM_STEP = 10
POOL_WINDOWS = (2, 4, 8, 16)
POOL_HALO = 16
CONV_HALO = 8
N_CHIPS = 4
N_DEV = 8
VMEM_LIMIT_BYTES = 48 * 1024 * 1024
EW_BLOCK_ELEMS = 1024 * 1024
ADAM_BLOCK_ELEMS = 512 * 1024
BF16_TILE_ROWS = 16
PAIR_EXCHANGE_CHUNKS = 4


def _pick(n, prefs):
    for p in prefs:
        if p <= n and n % p == 0:
            return p
    return n


def _row_tile(rows, cols, block_elems=EW_BLOCK_ELEMS):
    best = None
    for d in range(16, rows + 1, 16):
        if rows % d == 0 and d * cols <= block_elems:
            best = d
    return best or rows


def _dot(a, b):
    return jnp.dot(a, b, preferred_element_type=F32)


def _dot_nt(a, b):
    return lax.dot_general(a, b, (((1,), (1,)), ((), ())), preferred_element_type=F32)


def _dot_tn(a, b):
    return lax.dot_general(a, b, (((0,), (0,)), ((), ())), preferred_element_type=F32)


def _sigmoid(x):
    return 0.5 * jnp.tanh(0.5 * x) + 0.5


_GELU_C = math.sqrt(2.0 / math.pi)


def _gelu(x):
    x2 = x * x
    t = jnp.tanh(_GELU_C * (x + 0.044715 * x2 * x))
    val = 0.5 * x * (1.0 + t)
    grad = 0.5 * (1.0 + t) + 0.5 * x * (1.0 - t * t) * (_GELU_C * (1.0 + 3.0 * 0.044715 * x2))
    return val, grad


def _rstd(x):
    return lax.rsqrt(jnp.mean(x * x, axis=-1, keepdims=True) + EPS)


def _modulate(x, vec_ref):
    return (x * _rstd(x)) * vec_ref[0:1, :] * (1.0 + vec_ref[2:3, :]) + vec_ref[1:2, :]


def _modulate_bwd(x, dh, vec_ref, dvec_ref):
    gn, sh, sc = vec_ref[0:1, :], vec_ref[1:2, :], vec_ref[2:3, :]
    rstd = _rstd(x)
    r = x * rstd
    dvec_ref[0:1, :] += jnp.sum(dh * r * (1.0 + sc), axis=0, keepdims=True)
    dvec_ref[1:2, :] += jnp.sum(dh, axis=0, keepdims=True)
    dvec_ref[2:3, :] += jnp.sum(dh * r * gn, axis=0, keepdims=True)
    gm = gn * (1.0 + sc)
    dr = dh * gm
    dx = rstd * (dr - r * jnp.mean(dr * r, axis=-1, keepdims=True))
    return dx, r * gm + sh


def _adam(w, g, m, v):
    m = ADAM_B1 * m + (1.0 - ADAM_B1) * g
    v = ADAM_B2 * v + (1.0 - ADAM_B2) * (g * g)
    m_hat = m / (1.0 - ADAM_B1**ADAM_STEP)
    v_hat = v / (1.0 - ADAM_B2**ADAM_STEP)
    delta = -ADAM_LR * (m_hat / (jnp.sqrt(v_hat) + ADAM_EPS) + ADAM_WD * w)
    return delta, m, v


_ANY = pl.BlockSpec(memory_space=pl.ANY)


class _Phase:
    def __init__(self, ins, out_shapes, aliases, n_sems, start, finish, then):
        self.ins, self.out_shapes, self.aliases, self.n_sems = list(ins), list(out_shapes), dict(aliases), n_sems
        self.start, self.finish, self.then = start, finish, then


def _call(body, name, grid, in_specs, out_specs, out_shape, ins, scratch=(), prefetch=(), phases=(), in_place=None):
    n_pre, n_in, n_out, n_sc = len(prefetch), len(in_specs), len(out_specs), len(scratch)
    ph_in = [len(p.ins) for p in phases]
    ph_out = [len(p.out_shapes) for p in phases]

    def kernel_body(*refs):
        pos = [0]

        def take(k):
            pos[0] += k
            return refs[pos[0] - k : pos[0]]

        pre, ins_ = take(n_pre), take(n_in)
        p_ins = [take(k) for k in ph_in]
        outs_ = take(n_out)
        p_outs = [take(k) for k in ph_out]
        sc = take(n_sc)
        sems = [take(2) for _ in phases]
        if phases:
            ids = [pl.program_id(a) for a in range(len(grid))]
            first = functools.reduce(jnp.logical_and, [i == 0 for i in ids])
            last = functools.reduce(jnp.logical_and, [i == g - 1 for i, g in zip(ids, grid)])

            @pl.when(first)
            def _():
                for p, pi, po, (send, recv) in zip(phases, p_ins, p_outs, sems):
                    p.start(pi, po, send, recv)

        if body is not None:
            body(*pre, *ins_, *outs_, *sc)
        if phases:

            @pl.when(last)
            def _():
                for p, pi, po, (send, recv) in zip(phases, p_ins, p_outs, sems):
                    p.finish(pi, po, send, recv)

    aliases = {n_pre + i: o for i, o in (in_place or {}).items()}
    i0, o0 = n_pre + n_in, n_out
    for p in phases:
        for i, o in p.aliases.items():
            aliases[i0 + i] = o0 + o
        i0 += len(p.ins)
        o0 += len(p.out_shapes)
    all_in = list(in_specs) + [_ANY] * sum(ph_in)
    all_out = list(out_specs) + [_ANY] * sum(ph_out)
    all_scratch = list(scratch)
    for p in phases:
        all_scratch += [pltpu.SemaphoreType.DMA((p.n_sems,)), pltpu.SemaphoreType.DMA((p.n_sems,))]
    shapes = list(out_shape) + [s for p in phases for s in p.out_shapes]
    operands = list(prefetch) + list(ins) + [a for p in phases for a in p.ins]
    sem = ("arbitrary",) * len(grid)
    params = pltpu.CompilerParams(dimension_semantics=sem, vmem_limit_bytes=VMEM_LIMIT_BYTES)
    if n_pre:
        res = pl.pallas_call(
            kernel_body, name=name, out_shape=shapes, input_output_aliases=aliases, compiler_params=params,
            grid_spec=pltpu.PrefetchScalarGridSpec(
                num_scalar_prefetch=n_pre, grid=grid, in_specs=all_in, out_specs=all_out, scratch_shapes=all_scratch
            ),
        )(*operands)
    else:
        res = pl.pallas_call(
            kernel_body, name=name, grid=grid, in_specs=all_in, out_specs=all_out, out_shape=shapes,
            scratch_shapes=all_scratch, input_output_aliases=aliases, compiler_params=params,
        )(*operands)
    res = list(res)
    outs, rest = res[:n_out], res[n_out:]
    p_res = []
    for k in ph_out:
        p_res.append(rest[:k])
        rest = rest[k:]
    return outs, p_res


def _place():
    return lax.axis_index("x"), lax.axis_index("y"), lax.axis_index("c")


def _other_chips():
    x, y, _ = _place()
    return [(1 - x, y), (x, 1 - y), (1 - x, 1 - y)]


def _flip(k):
    x, y, c = _place()
    return (1 - x if k & 4 else x, 1 - y if k & 2 else y, 1 - c if k & 1 else c)


def _remote(src, dst, send, recv, k, to):
    return pltpu.make_async_remote_copy(
        src_ref=src, dst_ref=dst, send_sem=send.at[k], recv_sem=recv.at[k], device_id=to, device_id_type=MESH
    )


def _phase_small_gather(arrs, then):
    n = len(arrs)

    def copies(ins, outs, send, recv):
        x, y, c = _place()
        me = 4 * x + 2 * y + c
        local = [pltpu.make_async_copy(ins[a], outs[a].at[me], send.at[a * N_DEV]) for a in range(n)]
        remote = [_remote(ins[a], outs[a].at[me], send, recv, a * N_DEV + k, _flip(k)) for a in range(n) for k in range(1, N_DEV)]
        return local, remote

    def start(ins, outs, send, recv):
        local, remote = copies(ins, outs, send, recv)
        for cp in local + remote:
            cp.start()

    def finish(ins, outs, send, recv):
        local, remote = copies(ins, outs, send, recv)
        for cp in remote + local:
            cp.wait()

    shapes = [jax.ShapeDtypeStruct((N_DEV,) + a.shape, a.dtype) for a in arrs]
    return _Phase(arrs, shapes, {}, n * N_DEV, start, finish, then)


def _phase_small_exchange(arr, then):
    def copies(ins, outs, send, recv):
        x, y, c = _place()
        me = 4 * x + 2 * y + c
        local = pltpu.make_async_copy(ins[0].at[me], outs[0].at[me], send.at[0])
        remote = []
        for k in range(1, N_DEV):
            px, py, pc = _flip(k)
            remote.append(_remote(ins[0].at[4 * px + 2 * py + pc], outs[0].at[me], send, recv, k, (px, py, pc)))
        return [local] + remote

    def start(ins, outs, send, recv):
        for cp in copies(ins, outs, send, recv):
            cp.start()

    def finish(ins, outs, send, recv):
        for cp in copies(ins, outs, send, recv):
            cp.wait()

    return _Phase([arr], [jax.ShapeDtypeStruct(arr.shape, arr.dtype)], {}, N_DEV, start, finish, then)


def _after(*arrs):
    nothing = lambda *args: None
    return _Phase(arrs, [], {}, 1, nothing, nothing, nothing)


def _flush(name, *phases):
    _, p_outs = _call(None, name, (1,), [], [], [], [], phases=list(phases))
    for p, po in zip(phases, p_outs):
        p.then(po)


class _Big:
    KINDS = {"full": (True, True), "half": (True, False), "shard": (False, True), "block": (False, False)}

    def __init__(self, f3, s3, h3):
        assert s3 != h3
        self.f3, self.s3, self.h3 = tuple(f3), s3, h3
        self.bd = tuple(f3[a] // (N_CHIPS if a == s3 else 1) // (2 if a == h3 else 1) for a in range(3))
        self.tile = (1, _row_tile(self.bd[1], self.bd[2]), self.bd[2])
        self.grid = tuple(self.bd[a] // self.tile[a] for a in range(3))

    def dims(self, kind):
        chips, halves = self.KINDS[kind]
        return tuple(
            self.bd[a] * (N_CHIPS if chips and a == self.s3 else 1) * (2 if halves and a == self.h3 else 1) for a in range(3)
        )

    def view(self, ref, chip=None, half=None, batch0=0, both_halves=True, part=None):
        start = [batch0, 0, 0]
        size = list(ref.shape)
        size[0] = self.bd[0] * (2 if self.h3 == 0 and both_halves else 1)
        if chip is not None:
            start[self.s3] += chip * self.bd[self.s3]
            size[self.s3] = self.bd[self.s3]
        if half is not None:
            start[self.h3] += half * self.bd[self.h3]
            size[self.h3] = self.bd[self.h3]
        if part is not None:
            size[1] //= 2
            start[1] += part * size[1]
        return ref.at[tuple(pl.ds(st, sz) for st, sz in zip(start, size))]

    def spec(self, chip_from=None, half_from=None, lead=(), batch0=0):
        extra = "grid" in (chip_from, half_from)

        def index(*args):
            pref, idx = args[-1], list(args[int(extra) : -1])
            idx[0] += batch0
            if chip_from:
                idx[self.s3] += (pref[0] if chip_from == "pref" else args[0]) * self.grid[self.s3]
            if half_from:
                idx[self.h3] += (pref[1] if half_from == "pref" else args[0]) * self.grid[self.h3]
            return (0,) * len(lead) + tuple(idx)

        return pl.BlockSpec(tuple(lead) + self.tile, index)


def _same(arrs):
    return [jax.ShapeDtypeStruct(a.shape, a.dtype) for a in arrs]


def _phase_gather_relay(arrs, bigs, second, whole_first, then):
    n = len(arrs)
    per = 4 if second and not whole_first else 2

    def copies(outs, send, recv, arriving):
        x, y, c = _place()
        me, xn, yn, dg = (x, y), (1 - x, y), (x, 1 - y), (1 - x, 1 - y)
        if not second:
            part = (None, None) if whole_first else (0, 1)
            plan = [((xn if arriving else me), part[0], xn), ((yn if arriving else me), part[1], yn)]
        elif whole_first:
            plan = [(dg, 0, yn), (dg, 1, xn)] if arriving else [(xn, 0, yn), (yn, 1, xn)]
        elif arriving:
            plan = [(yn, 0, yn), (dg, 0, yn), (xn, 1, xn), (dg, 1, xn)]
        else:
            plan = [(me, 0, yn), (xn, 0, yn), (me, 1, xn), (yn, 1, xn)]
        res = []
        for a in range(n):
            for k, (chip, part, to) in enumerate(plan):
                blk = bigs[a].view(outs[a], 2 * chip[0] + chip[1], c, part=part)
                res.append(_remote(blk, blk, send, recv, per * a + k, (*to, c)))
        return res

    def start(ins, outs, send, recv):
        for cp in copies(outs, send, recv, False):
            cp.start()

    def finish(ins, outs, send, recv):
        for cp in copies(outs, send, recv, True):
            cp.wait_recv()
        for cp in copies(outs, send, recv, False):
            cp.wait_send()

    return _Phase(arrs, _same(arrs), {a: a for a in range(n)}, per * n, start, finish, then)


def _phase_gather_sibling(arrs, bigs, then):
    n = len(arrs)

    def copies(outs, send, recv, arriving):
        x, y, c = _place()
        return [
            _remote(blk, blk, send, recv, 6 * a + 2 * j + part, (x, y, 1 - c))
            for j, chip in enumerate(_other_chips())
            for a in range(n)
            for part in range(2)
            for blk in [bigs[a].view(outs[a], 2 * chip[0] + chip[1], 1 - c if arriving else c, part=part)]
        ]

    def start(ins, outs, send, recv):
        for cp in copies(outs, send, recv, False):
            cp.start()

    def finish(ins, outs, send, recv):
        for cp in copies(outs, send, recv, True):
            cp.wait_recv()
        for cp in copies(outs, send, recv, False):
            cp.wait_send()

    return _Phase(arrs, _same(arrs), {a: a for a in range(n)}, 6 * n, start, finish, then)


def _phase_pair_exchange(grads, bigs, then):
    n = len(grads)
    shapes = [jax.ShapeDtypeStruct(b.dims("half"), BF16) for b in bigs]
    chunks = [PAIR_EXCHANGE_CHUNKS if s.shape[1] % (PAIR_EXCHANGE_CHUNKS * BF16_TILE_ROWS) == 0 else 1 for s in shapes]

    def copies(ins, outs, send, recv):
        x, y, c = _place()
        srcs = [ins[a] if ins[a].shape == outs[a].shape else bigs[a].view(ins[a], None, 1 - c) for a in range(n)]
        res = []
        for a in range(n):
            rows = shapes[a].shape[1] // chunks[a]
            for k in range(chunks[a]):
                cut = (slice(None), pl.ds(k * rows, rows), slice(None))
                res.append(_remote(srcs[a].at[cut], outs[a].at[cut], send, recv, PAIR_EXCHANGE_CHUNKS * a + k, (x, y, 1 - c)))
        return res

    def start(ins, outs, send, recv):
        for cp in copies(ins, outs, send, recv):
            cp.start()

    def finish(ins, outs, send, recv):
        for cp in copies(ins, outs, send, recv):
            cp.wait()

    return _Phase(grads, shapes, {}, PAIR_EXCHANGE_CHUNKS * n, start, finish, then)


def _phase_chip_exchange(sums, bigs, then):
    n = len(sums)

    def copies(ins, outs, send, recv):
        _, _, c = _place()
        return [
            _remote(bigs[a].view(ins[a], 2 * chip[0] + chip[1], both_halves=False), outs[a].at[j], send, recv, 3 * a + j, (*chip, c))
            for j, chip in enumerate(_other_chips())
            for a in range(n)
        ]

    def start(ins, outs, send, recv):
        for cp in copies(ins, outs, send, recv):
            cp.start()

    def finish(ins, outs, send, recv):
        for cp in copies(ins, outs, send, recv):
            cp.wait()

    shapes = [jax.ShapeDtypeStruct((N_CHIPS - 1,) + b.dims("block"), BF16) for b in bigs]
    return _Phase(sums, shapes, {}, 3 * n, start, finish, then)


_HBM = pl.BlockSpec(memory_space=pltpu.HBM)
_SEM = pl.BlockSpec(memory_space=pltpu.SEMAPHORE)
_DATAFLOW = pltpu.SideEffectType.DATAFLOW_SIDE_EFFECTING


class _InFlight:
    def __init__(self, phase, send, recv, arrays, token):
        self.phase, self.send, self.recv, self.arrays, self.token = phase, send, recv, arrays, token


def _phase_results(phase, refs):
    n_in = len(phase.ins)
    updated = {o: i for i, o in phase.aliases.items()}
    fresh = [o for o in range(len(phase.out_shapes)) if o not in updated]
    return [refs[updated[o]] if o in updated else refs[n_in + fresh.index(o)] for o in range(len(phase.out_shapes))]


def _split_start(phase, name):
    n_in = len(phase.ins)
    fresh = [s for o, s in enumerate(phase.out_shapes) if o not in phase.aliases.values()]
    arrays = list(phase.ins) + [lax.empty(s.shape, s.dtype) for s in fresh]
    n = len(arrays)

    def body(*refs):
        phase.start(refs[:n_in], _phase_results(phase, refs[:n]), refs[n], refs[n + 1])
        refs[-1][...] = jnp.zeros_like(refs[-1])

    operands = [pltpu.with_memory_space_constraint(a, pltpu.HBM) for a in arrays]
    res = pl.pallas_call(
        body, name=name,
        out_shape=[pltpu.SemaphoreType.DMA((phase.n_sems,)), pltpu.SemaphoreType.DMA((phase.n_sems,))]
        + [pltpu.HBM(a.shape, a.dtype) for a in arrays] + [jax.ShapeDtypeStruct((8, 128), F32)],
        in_specs=[_HBM] * n, out_specs=[_SEM, _SEM] + [_HBM] * n + [pl.BlockSpec(memory_space=pltpu.VMEM)],
        input_output_aliases={i: 2 + i for i in range(n)},
        compiler_params=pltpu.CompilerParams(has_side_effects=_DATAFLOW),
    )(*operands)
    return _InFlight(phase, res[0], res[1], list(res[2 : 2 + n]), res[-1])


def _split_wait(flight, after, name):
    phase, n = flight.phase, len(flight.arrays)
    n_in = len(phase.ins)

    def body(*refs):
        phase.finish(refs[:n_in], _phase_results(phase, refs[:n]), refs[n], refs[n + 1])

    res = pl.pallas_call(
        body, name=name, out_shape=[pltpu.HBM(a.shape, a.dtype) for a in flight.arrays],
        in_specs=[_HBM] * n + [_SEM, _SEM] + [_ANY] * len(after), out_specs=[_HBM] * n,
        input_output_aliases={i: i for i in range(n)},
        compiler_params=pltpu.CompilerParams(has_side_effects=_DATAFLOW),
    )(*flight.arrays, flight.send, flight.recv, *after)
    res = list(res)
    phase.then(_phase_results(phase, res))
    return res[:n_in]


def _phase_pair_broadcast(stacks, bigs, batch0s, then):
    n = len(stacks)

    def start(ins, outs, send, recv):
        x, y, c = _place()
        for a in range(n):
            blk = bigs[a].view(outs[a], None, c, batch0s[a])
            _remote(blk, blk, send, recv, a, (x, y, 1 - c)).start()

    def finish(ins, outs, send, recv):
        x, y, c = _place()
        for a in range(n):
            mine = bigs[a].view(outs[a], None, c, batch0s[a])
            theirs = bigs[a].view(outs[a], None, 1 - c, batch0s[a])
            _remote(mine, mine, send, recv, a, (x, y, 1 - c)).wait_send()
            _remote(theirs, theirs, send, recv, a, (x, y, 1 - c)).wait_recv()

    return _Phase(stacks, _same(stacks), {a: a for a in range(n)}, n, start, finish, then)


def _tile_call(body, name, big, where, extra, ins, in_specs, out_specs, out_shape, phases=()):
    grid = ((extra,) if extra else ()) + big.grid
    return _call(body, name, grid, in_specs, out_specs, out_shape, ins, prefetch=(where,), phases=phases)


def _cast_into_full(w_stack, batch0, big, where, name, phases=()):
    def body(_, w_ref, o_ref):
        o_ref[...] = w_ref[...].astype(BF16)

    return _tile_call(
        body, name, big, where, 2, [w_stack], [big.spec(None, "grid", batch0=batch0)], [big.spec("pref", "grid")],
        [jax.ShapeDtypeStruct(big.dims("full"), BF16)], phases,
    )


def _pair_sum(g_full, recv_half, big, where, name, phases=()):
    def body(_, g_ref, r_ref, o_ref):
        o_ref[...] = (g_ref[...].astype(F32) + r_ref[...].astype(F32)).astype(BF16)

    half = big.spec("grid", None)
    return _tile_call(
        body, name, big, where, N_CHIPS, [g_full, recv_half], [big.spec("grid", "pref"), half], [half],
        [jax.ShapeDtypeStruct(big.dims("half"), BF16)], phases,
    )


def _chip_sum(chip_sum, parts, big, where, stack, stack_shape, batch0, name, phases=()):
    def body(_, own_ref, p_ref, *rest):
        acc = own_ref[...].astype(F32)
        for k in range(N_CHIPS - 1):
            acc = acc + p_ref[k].astype(F32)
        rest[-1][...] = acc

    ins = [chip_sum, parts] + ([stack] if stack is not None else [])
    in_specs = [big.spec("pref", None), big.spec(None, None, lead=(N_CHIPS - 1,))] + ([_ANY] if stack is not None else [])
    return _call(
        body, name, big.grid, in_specs, [big.spec(None, "pref", batch0=batch0)], [jax.ShapeDtypeStruct(stack_shape, F32)], ins,
        prefetch=(where,), phases=phases, in_place={2: 0} if stack is not None else None,
    )


def _adam_stack(w, g, m, v, name, after=()):
    b, r, c = w.shape
    tr = _row_tile(r, c, ADAM_BLOCK_ELEMS)

    def body(w_ref, g_ref, m_ref, v_ref, *rest):
        go_ref, d_ref, mo_ref, vo_ref = rest[-4:]
        gv = g_ref[...]
        d, mo, vo = _adam(w_ref[...], gv, m_ref[...], v_ref[...])
        go_ref[...] = gv
        d_ref[...] = d
        mo_ref[...] = mo
        vo_ref[...] = vo

    spec = pl.BlockSpec((1, tr, c), lambda bb, i: (bb, i, 0))
    outs, _ = _call(
        body, name, (b, r // tr), [spec] * 4 + [_ANY] * len(after), [spec] * 4, [jax.ShapeDtypeStruct(w.shape, F32)] * 4,
        [w, g, m, v, *after],
    )
    return outs


def _mod_fwd(c_all, w_mod, b_cols, phases=()):
    n_layers, d, n = w_mod.shape
    tn = _pick(n, (768, 512, 384, 256, 128))

    def body(c_ref, w_ref, b_ref, o_ref):
        cv = c_ref[...]
        ca = (cv * _sigmoid(cv)).astype(BF16)
        o_ref[0] = _dot(ca, w_ref[0].astype(BF16)) + b_ref[0]

    return _call(
        body, "mod_fwd", (n_layers, n // tn),
        [
            pl.BlockSpec((N_DEV, d), lambda l, j: (0, 0)),
            pl.BlockSpec((1, d, tn), lambda l, j: (l, 0, j)),
            pl.BlockSpec((1, 1, tn), lambda l, j: (l, 0, j)),
        ],
        [pl.BlockSpec((1, N_DEV, tn), lambda l, j: (l, 0, j))],
        [jax.ShapeDtypeStruct((n_layers, N_DEV, n), F32)], [c_all, w_mod, b_cols], phases=phases,
    )


def _mod_bwd_adam(c_all_t, dmod_cols, w, m, v, after=()):
    n_layers, d, n = w.shape
    tn = _pick(n, (384, 256, 128))

    def body(c_ref, dm_ref, w_ref, m_ref, v_ref, *rest):
        g_ref, d_ref, mo_ref, vo_ref = rest[-4:]
        cv = c_ref[...]
        ca = (cv * _sigmoid(cv)).astype(BF16)
        g = _dot(ca, dm_ref[0].astype(BF16))
        g_ref[0] = g
        dl, mo, vo = _adam(w_ref[0], g, m_ref[0], v_ref[0])
        d_ref[0] = dl
        mo_ref[0] = mo
        vo_ref[0] = vo

    wspec = pl.BlockSpec((1, d, tn), lambda l, j: (l, 0, j))
    outs, _ = _call(
        body, "mod_bwd_adam", (n_layers, n // tn),
        [pl.BlockSpec((d, N_DEV), lambda l, j: (0, 0)), pl.BlockSpec((1, N_DEV, tn), lambda l, j: (l, 0, j)), wspec, wspec, wspec]
        + [_ANY] * len(after),
        [wspec] * 4, [jax.ShapeDtypeStruct(w.shape, F32)] * 4, [c_all_t, dmod_cols, w, m, v, *after],
    )
    return outs


def _ffn_fwd(x, vec, w_in, w_out, name, phases=()):
    s, d = x.shape
    f = w_out.shape[1]
    tm = _pick(s, (1024, 512, 256, 128))
    tf = _pick(f, (256, 128))
    nf = f // tf

    def body(x_ref, vec_ref, wg_ref, wu_ref, wo_ref, xo_ref, g_ref, u_ref, y_ref, h_sc, acc_sc):
        j = pl.program_id(1)

        @pl.when(j == 0)
        def _():
            h_sc[...] = _modulate(x_ref[...], vec_ref).astype(BF16)
            acc_sc[...] = jnp.zeros_like(acc_sc)

        h = h_sc[...]
        g = _dot(h, wg_ref[0])
        u = _dot(h, wu_ref[0])
        g_ref[...] = g.astype(BF16)
        u_ref[...] = u.astype(BF16)
        a = (g * _sigmoid(g) * u).astype(BF16)
        acc_sc[...] += _dot(a, wo_ref[0])

        @pl.when(j == nf - 1)
        def _():
            yv = acc_sc[...]
            xo_ref[...] = x_ref[...] + 0.5 * vec_ref[3:4, :] * yv
            y_ref[...] = yv.astype(BF16)

    row = pl.BlockSpec((tm, d), lambda i, j: (i, 0))
    hid = pl.BlockSpec((tm, tf), lambda i, j: (i, j))
    return _call(
        body, name, (s // tm, nf),
        [
            row,
            pl.BlockSpec((8, d), lambda i, j: (0, 0)),
            pl.BlockSpec((1, d, tf), lambda i, j: (0, 0, j)),
            pl.BlockSpec((1, d, tf), lambda i, j: (0, 0, nf + j)),
            pl.BlockSpec((1, tf, d), lambda i, j: (0, j, 0)),
        ],
        [row, hid, hid, row],
        [
            jax.ShapeDtypeStruct((s, d), F32),
            jax.ShapeDtypeStruct((s, f), BF16),
            jax.ShapeDtypeStruct((s, f), BF16),
            jax.ShapeDtypeStruct((s, d), BF16),
        ],
        [x, vec, w_in, w_in, w_out],
        scratch=[pltpu.VMEM((tm, d), BF16), pltpu.VMEM((tm, d), F32)], phases=phases,
    )


def _ffn_bwd(dxo, x, vec, gg, uu, y, w_in, w_out, name, phases=()):
    s, d = x.shape
    f = w_out.shape[1]
    tm = _pick(s, (512, 256, 128))
    tf = _pick(f, (256, 128))
    nf = f // tf

    def body(dxo_ref, x_ref, vec_ref, g_ref, u_ref, y_ref, wg_ref, wu_ref, wo_ref,
             dx_ref, dg_ref, du_ref, a_ref, h_ref, dy_ref, dvec_ref, acc_sc):
        i, j = pl.program_id(0), pl.program_id(1)

        @pl.when((i == 0) & (j == 0))
        def _():
            dvec_ref[...] = jnp.zeros_like(dvec_ref)

        @pl.when(j == 0)
        def _():
            dxo_v = dxo_ref[...]
            dy_ref[...] = (0.5 * vec_ref[3:4, :] * dxo_v).astype(BF16)
            dvec_ref[3:4, :] += 0.5 * jnp.sum(dxo_v * y_ref[...].astype(F32), axis=0, keepdims=True)
            acc_sc[...] = jnp.zeros_like(acc_sc)

        da = _dot_nt(dy_ref[...], wo_ref[0])
        g = g_ref[...].astype(F32)
        u = u_ref[...].astype(F32)
        sig = _sigmoid(g)
        sl = g * sig
        a_ref[...] = (sl * u).astype(BF16)
        dg = (da * u * (sig * (1.0 + g * (1.0 - sig)))).astype(BF16)
        du = (da * sl).astype(BF16)
        dg_ref[...] = dg
        du_ref[...] = du
        acc_sc[...] += _dot_nt(dg, wg_ref[0]) + _dot_nt(du, wu_ref[0])

        @pl.when(j == nf - 1)
        def _():
            dx, h = _modulate_bwd(x_ref[...], acc_sc[...], vec_ref, dvec_ref)
            dx_ref[...] = dxo_ref[...] + dx
            h_ref[...] = h.astype(BF16)

    row = pl.BlockSpec((tm, d), lambda i, j: (i, 0))
    hid = pl.BlockSpec((tm, tf), lambda i, j: (i, j))
    vecs = pl.BlockSpec((8, d), lambda i, j: (0, 0))
    return _call(
        body, name, (s // tm, nf),
        [
            row, row, vecs, hid, hid, row,
            pl.BlockSpec((1, d, tf), lambda i, j: (0, 0, j)),
            pl.BlockSpec((1, d, tf), lambda i, j: (0, 0, nf + j)),
            pl.BlockSpec((1, tf, d), lambda i, j: (0, j, 0)),
        ],
        [row, hid, hid, hid, row, row, vecs],
        [
            jax.ShapeDtypeStruct((s, d), F32),
            jax.ShapeDtypeStruct((s, f), BF16),
            jax.ShapeDtypeStruct((s, f), BF16),
            jax.ShapeDtypeStruct((s, f), BF16),
            jax.ShapeDtypeStruct((s, d), BF16),
            jax.ShapeDtypeStruct((s, d), BF16),
            jax.ShapeDtypeStruct((8, d), F32),
        ],
        [dxo, x, vec, gg, uu, y, w_in, w_in, w_out],
        scratch=[pltpu.VMEM((tm, d), F32)], phases=phases,
    )


def _grad_half(a, bs, big, where, mine, recv, name, phases=()):
    s, k1 = a.shape
    n = bs[0].shape[1]
    groups = len(bs)
    rows_halved = big.h3 == 1
    assert rows_halved or groups == 1
    kk, nn = (k1 // 2, n) if rows_halved else (k1, n // 2)
    tk = _pick(kk, (1408, 1024, 512, 256, 128))
    tn = _pick(nn, (1408, 1024, 640, 512, 256, 128))
    nkb, nnb = kk // tk, nn // tn
    assert (recv is None) == (not mine)

    def half(pref):
        return pref[1] if mine else 1 - pref[1]

    def body(_, a_ref, *rest):
        q = pl.program_id(1)
        for p in range(groups):

            @pl.when(q == p)
            def _(p=p):
                acc = _dot_tn(a_ref[...], rest[p][...])
                if recv is not None:
                    acc = acc + rest[groups][0].astype(F32)
                rest[-1][0] = acc.astype(BF16)

    def b_block(p):
        def index(i, q, j, pref):
            jj = jnp.where(q == p, j, jnp.where(q < p, 0, nnb - 1))
            return (0, jj + (0 if rows_halved else half(pref) * nnb))

        return pl.BlockSpec((s, tn), index)

    out_spec = pl.BlockSpec((1, tk, tn), lambda i, q, j, pref: (0, i, q * nnb + j))
    in_specs = [pl.BlockSpec((s, tk), lambda i, q, j, pref: (0, i + (half(pref) * nkb if rows_halved else 0)))]
    in_specs += [b_block(p) for p in range(groups)]
    ins = [a, *bs]
    if recv is not None:
        in_specs.append(out_spec)
        ins.append(recv)
    return _call(
        body, name, (nkb, groups, nnb), in_specs, [out_spec], [jax.ShapeDtypeStruct(big.dims("half"), BF16)], ins,
        prefetch=(where,), phases=phases,
    )


def _proj_mod_fwd(x, vec, w, phases=()):
    s, d = x.shape
    n = w.shape[2]
    tm = _pick(s, (1024, 512, 256, 128))
    tn = _pick(n, (640, 512, 256, 128))

    def body(x_ref, vec_ref, w_ref, o_ref, h_sc):
        @pl.when(pl.program_id(1) == 0)
        def _():
            h_sc[...] = _modulate(x_ref[...], vec_ref).astype(BF16)

        o_ref[...] = _dot(h_sc[...], w_ref[0])

    return _call(
        body, "ab_in_fwd", (s // tm, n // tn),
        [
            pl.BlockSpec((tm, d), lambda i, j: (i, 0)),
            pl.BlockSpec((8, d), lambda i, j: (0, 0)),
            pl.BlockSpec((1, d, tn), lambda i, j: (0, 0, j)),
        ],
        [pl.BlockSpec((tm, tn), lambda i, j: (i, j))],
        [jax.ShapeDtypeStruct((s, n), F32)], [x, vec, w],
        scratch=[pltpu.VMEM((tm, d), BF16)], phases=phases,
    )


def _proj_res_fwd(a, w, x, vec, phases=()):
    s, kd = a.shape
    d = x.shape[1]
    tm = _pick(s, (1024, 512, 256, 128))

    def body(a_ref, w_ref, x_ref, vec_ref, xo_ref, y_ref):
        yv = _dot(a_ref[...], w_ref[0])
        xo_ref[...] = x_ref[...] + vec_ref[3:4, :] * yv
        y_ref[...] = yv.astype(BF16)

    row = pl.BlockSpec((tm, d), lambda i: (i, 0))
    return _call(
        body, "ab_out_fwd", (s // tm,),
        [pl.BlockSpec((tm, kd), lambda i: (i, 0)), pl.BlockSpec((1, kd, d), lambda i: (0, 0, 0)), row, pl.BlockSpec((8, d), lambda i: (0, 0))],
        [row, row],
        [jax.ShapeDtypeStruct((s, d), F32), jax.ShapeDtypeStruct((s, d), BF16)], [a, w, x, vec], phases=phases,
    )


def _proj_res_bwd(dxo, y, vec, w, phases=()):
    s, d = dxo.shape
    kd = w.shape[1]
    tm = _pick(s, (1024, 512, 256, 128))

    def body(dxo_ref, y_ref, vec_ref, w_ref, dy_ref, da_ref, dgate_ref):
        @pl.when(pl.program_id(0) == 0)
        def _():
            dgate_ref[...] = jnp.zeros_like(dgate_ref)

        dxo_v = dxo_ref[...]
        dy = (vec_ref[3:4, :] * dxo_v).astype(BF16)
        dy_ref[...] = dy
        dgate_ref[3:4, :] += jnp.sum(dxo_v * y_ref[...].astype(F32), axis=0, keepdims=True)
        da_ref[...] = _dot_nt(dy, w_ref[0]).astype(BF16)

    row = pl.BlockSpec((tm, d), lambda i: (i, 0))
    vecs = pl.BlockSpec((8, d), lambda i: (0, 0))
    return _call(
        body, "ab_out_bwd", (s // tm,),
        [row, row, vecs, pl.BlockSpec((1, kd, d), lambda i: (0, 0, 0))],
        [row, pl.BlockSpec((tm, kd), lambda i: (i, 0)), vecs],
        [jax.ShapeDtypeStruct((s, d), BF16), jax.ShapeDtypeStruct((s, kd), BF16), jax.ShapeDtypeStruct((8, d), F32)],
        [dxo, y, vec, w], phases=phases,
    )


def _proj_mod_bwd(dproj, w, x, vec, dxo, dvec_in, name, phases=()):
    parts, s, n_part = dproj.shape
    d = x.shape[1]
    tm = _pick(s, (512, 256, 128))
    tk = _pick(n_part, (1408, 1280, 1024, 512, 256, 128))
    per_part = n_part // tk
    nk = parts * per_part

    def body(dp_ref, w_ref, x_ref, vec_ref, dxo_ref, dvi_ref, dx_ref, h_ref, dvec_ref, acc_sc):
        i, k = pl.program_id(0), pl.program_id(1)

        @pl.when((i == 0) & (k == 0))
        def _():
            dvec_ref[...] = dvi_ref[...]

        @pl.when(k == 0)
        def _():
            acc_sc[...] = jnp.zeros_like(acc_sc)

        acc_sc[...] += _dot_nt(dp_ref[0], w_ref[0])

        @pl.when(k == nk - 1)
        def _():
            dx, h = _modulate_bwd(x_ref[...], acc_sc[...], vec_ref, dvec_ref)
            dx_ref[...] = dxo_ref[...] + dx
            h_ref[...] = h.astype(BF16)

    row = pl.BlockSpec((tm, d), lambda i, k: (i, 0))
    vecs = pl.BlockSpec((8, d), lambda i, k: (0, 0))
    return _call(
        body, name, (s // tm, nk),
        [
            pl.BlockSpec((1, tm, tk), lambda i, k: (k // per_part, i, k % per_part)),
            pl.BlockSpec((1, d, tk), lambda i, k: (0, 0, k)),
            row, vecs, row, vecs,
        ],
        [row, row, vecs],
        [jax.ShapeDtypeStruct((s, d), F32), jax.ShapeDtypeStruct((s, d), BF16), jax.ShapeDtypeStruct((8, d), F32)],
        [dproj, w, x, vec, dxo, dvec_in], scratch=[pltpu.VMEM((tm, d), F32)], phases=phases,
    )


def _tril(n):
    return lax.broadcasted_iota(jnp.int32, (n, n), 0) >= lax.broadcasted_iota(jnp.int32, (n, n), 1)


def _layernorm_stats(gv):
    mu = jnp.mean(gv, axis=-1, keepdims=True)
    cen = gv - mu
    rstd = lax.rsqrt(jnp.mean(cen * cen, axis=-1, keepdims=True) + EPS)
    return cen * rstd, rstd


def _shift_down(q, k, above_ref, c_cg, c_xb, first):
    width = q.shape[1]
    rows = lax.broadcasted_iota(jnp.int32, q.shape, 0)
    out = pltpu.roll(q, k, 0)
    for r in range(k):
        src = CONV_HALO - k + r
        above = above_ref[src : src + 1, c_cg : c_cg + width] * above_ref[src : src + 1, c_xb : c_xb + width]
        above = jnp.where(first, 0.0, above)
        out = jnp.where(rows == r, above, out)
    return out


def _ab_mix_fwd(proj, norm_v, w_s, b_rows, conv_w, phases=()):
    s, n = proj.shape
    heads, chunk, _ = w_s.shape
    da = norm_v.shape[1]
    hd = da // heads
    db = conv_w.shape[1]
    tm = _pick(s, (512, 256, 128))

    def body(p_ref, ph_ref, nv_ref, ws_ref, b_ref, cw_ref, o_ref):
        first = pl.program_id(0) == 0
        gu, _ = _gelu(p_ref[:, 0:da])
        gv, _ = _gelu(p_ref[:, da : 2 * da])
        xhat, _ = _layernorm_stats(gv)
        vn = (xhat * nv_ref[...]).astype(BF16)
        mask = _tril(chunk)
        for hh in range(heads):
            wm = jnp.where(mask, ws_ref[hh], 0.0).astype(BF16)
            cols = slice(hh * hd, (hh + 1) * hd)
            for nn in range(tm // chunk):
                rows = slice(nn * chunk, (nn + 1) * chunk)
                z = _dot(wm, vn[rows, cols]) + b_ref[:, cols]
                o_ref[rows, cols] = (gu[rows, cols] * z).astype(BF16)
        c_cg, c_xb = 2 * da + db, 2 * da + 2 * db
        bg = p_ref[:, 2 * da : 2 * da + db]
        q = p_ref[:, c_cg : c_cg + db] * p_ref[:, c_xb : c_xb + db]
        q1 = _shift_down(q, 1, ph_ref, c_cg, c_xb, first)
        q2 = _shift_down(q, 2, ph_ref, c_cg, c_xb, first)
        conv = cw_ref[0:1, :] * q2 + cw_ref[1:2, :] * q1 + cw_ref[2:3, :] * q
        o_ref[:, da : da + db] = (bg * conv).astype(BF16)

    nh = tm // CONV_HALO
    return _call(
        body, "ab_mix_fwd", (s // tm,),
        [
            pl.BlockSpec((tm, n), lambda i: (i, 0)),
            pl.BlockSpec((CONV_HALO, n), lambda i: (jnp.maximum(i * nh - 1, 0), 0)),
            pl.BlockSpec((1, da), lambda i: (0, 0)),
            pl.BlockSpec((heads, chunk, chunk), lambda i: (0, 0, 0)),
            pl.BlockSpec((chunk, da), lambda i: (0, 0)),
            pl.BlockSpec((3, db), lambda i: (0, 0)),
        ],
        [pl.BlockSpec((tm, da + db), lambda i: (i, 0))],
        [jax.ShapeDtypeStruct((s, da + db), BF16)], [proj, proj, norm_v, w_s, b_rows, conv_w], phases=phases,
    )


def _ab_mix_bwd(proj, dcat, norm_v, w_s, b_rows, conv_w, phases=()):
    s, n = proj.shape
    heads, chunk, _ = w_s.shape
    da = norm_v.shape[1]
    hd = da // heads
    db = conv_w.shape[1]
    tm = _pick(s, (512, 256, 128))
    nblk = s // tm
    dhalo = 2 * CONV_HALO

    def body(p_ref, pa_ref, pb_ref, dc_ref, dcb_ref, nv_ref, ws_ref, b_ref, cw_ref,
             dp_ref, dnv_ref, dws_ref, dzs_ref, dcw_ref, dvn_sc):
        i = pl.program_id(0)
        first, last = i == 0, i == nblk - 1

        @pl.when(first)
        def _():
            dnv_ref[...] = jnp.zeros_like(dnv_ref)
            dws_ref[...] = jnp.zeros_like(dws_ref)
            dzs_ref[...] = jnp.zeros_like(dzs_ref)
            dcw_ref[...] = jnp.zeros_like(dcw_ref)

        uu = p_ref[:, 0:da]
        gu, gu_grad = _gelu(uu)
        gv, gv_grad = _gelu(p_ref[:, da : 2 * da])
        xhat, rstd = _layernorm_stats(gv)
        nv = nv_ref[...]
        vn = (xhat * nv).astype(BF16)
        dya = dc_ref[:, 0:da].astype(F32)
        dz = (dya * gu).astype(BF16)
        mask = _tril(chunk)
        for hh in range(heads):
            wm = jnp.where(mask, ws_ref[hh], 0.0).astype(BF16)
            cols = slice(hh * hd, (hh + 1) * hd)
            dws = jnp.zeros((chunk, chunk), F32)
            for nn in range(tm // chunk):
                rows = slice(nn * chunk, (nn + 1) * chunk)
                z = _dot(wm, vn[rows, cols]) + b_ref[:, cols]
                dp_ref[rows, cols] = (dya[rows, cols] * z * gu_grad[rows, cols]).astype(BF16)
                dz_blk = dz[rows, cols]
                dws = dws + _dot_nt(dz_blk, vn[rows, cols])
                dzs_ref[:, cols] += dz_blk.astype(F32)
                dvn = _dot_tn(wm, dz_blk)
                dnv_ref[:, cols] += jnp.sum(dvn * xhat[rows, cols], axis=0, keepdims=True)
                dvn_sc[rows, cols] = dvn
            dws_ref[hh] += jnp.where(mask, dws, 0.0)
        dxhat = dvn_sc[...] * nv
        dgv = rstd * (dxhat - jnp.mean(dxhat, axis=-1, keepdims=True) - xhat * jnp.mean(dxhat * xhat, axis=-1, keepdims=True))
        dp_ref[:, da : 2 * da] = (dgv * gv_grad).astype(BF16)

        c_bg, c_cg, c_xb = 2 * da, 2 * da + db, 2 * da + 2 * db
        bg = p_ref[:, c_bg : c_bg + db]
        cg = p_ref[:, c_cg : c_cg + db]
        xb = p_ref[:, c_xb : c_xb + db]
        q = cg * xb
        q1 = _shift_down(q, 1, pa_ref, c_cg, c_xb, first)
        q2 = _shift_down(q, 2, pa_ref, c_cg, c_xb, first)
        dyb = dc_ref[:, da : da + db].astype(F32)
        conv = cw_ref[0:1, :] * q2 + cw_ref[1:2, :] * q1 + cw_ref[2:3, :] * q
        dp_ref[:, c_bg : c_bg + db] = (dyb * conv).astype(BF16)
        e = dyb * bg
        dcw_ref[0:1, :] += jnp.sum(e * q2, axis=0, keepdims=True)
        dcw_ref[1:2, :] += jnp.sum(e * q1, axis=0, keepdims=True)
        dcw_ref[2:3, :] += jnp.sum(e * q, axis=0, keepdims=True)
        rows = lax.broadcasted_iota(jnp.int32, e.shape, 0)
        dq = cw_ref[2:3, :] * e
        for kk in (1, 2):
            ek = pltpu.roll(e, tm - kk, 0)
            for r in range(kk):
                below = dcb_ref[r : r + 1, da : da + db].astype(F32) * pb_ref[r : r + 1, c_bg : c_bg + db]
                below = jnp.where(last, 0.0, below)
                ek = jnp.where(rows == tm - kk + r, below, ek)
            dq = dq + cw_ref[2 - kk : 3 - kk, :] * ek
        dp_ref[:, c_cg : c_cg + db] = (dq * xb).astype(BF16)
        dp_ref[:, c_xb : c_xb + db] = (dq * cg).astype(BF16)

    nh = tm // CONV_HALO
    nhb = tm // dhalo
    const2 = lambda i: (0, 0)
    return _call(
        body, "ab_mix_bwd", (nblk,),
        [
            pl.BlockSpec((tm, n), lambda i: (i, 0)),
            pl.BlockSpec((CONV_HALO, n), lambda i: (jnp.maximum(i * nh - 1, 0), 0)),
            pl.BlockSpec((CONV_HALO, n), lambda i: (jnp.minimum((i + 1) * nh, s // CONV_HALO - 1), 0)),
            pl.BlockSpec((tm, da + db), lambda i: (i, 0)),
            pl.BlockSpec((dhalo, da + db), lambda i: (jnp.minimum((i + 1) * nhb, s // dhalo - 1), 0)),
            pl.BlockSpec((1, da), const2),
            pl.BlockSpec((heads, chunk, chunk), lambda i: (0, 0, 0)),
            pl.BlockSpec((chunk, da), const2),
            pl.BlockSpec((3, db), const2),
        ],
        [
            pl.BlockSpec((tm, n), lambda i: (i, 0)),
            pl.BlockSpec((1, da), const2),
            pl.BlockSpec((heads, chunk, chunk), lambda i: (0, 0, 0)),
            pl.BlockSpec((chunk, da), const2),
            pl.BlockSpec((3, db), const2),
        ],
        [
            jax.ShapeDtypeStruct((s, n), BF16),
            jax.ShapeDtypeStruct((1, da), F32),
            jax.ShapeDtypeStruct((heads, chunk, chunk), F32),
            jax.ShapeDtypeStruct((chunk, da), F32),
            jax.ShapeDtypeStruct((3, db), F32),
        ],
        [proj, proj, proj, dcat, dcat, norm_v, w_s, b_rows, conv_w],
        scratch=[pltpu.VMEM((tm, da), F32)], phases=phases,
    )


def _pool_counts(tm, i, w):
    t = i * tm + lax.broadcasted_iota(jnp.int32, (tm, 1), 0)
    return jnp.minimum(t + 1, w).astype(F32)


def _pool_fwd(x, vec, w_grp, scale, phases=()):
    s, d = x.shape
    groups, gd, _ = w_grp.shape
    tm = _pick(s, (512, 256, 128))

    def body(x_ref, xa_ref, vec_ref, w_ref, sc_ref, xo_ref, p_ref, o_ref):
        i = pl.program_id(0)
        h = _modulate(x_ref[...], vec_ref)
        ha = jnp.where(i == 0, 0.0, _modulate(xa_ref[...], vec_ref))
        ext = jnp.concatenate([ha, h], axis=0)
        for gi, w in enumerate(POOL_WINDOWS):
            cols = slice(gi * gd, (gi + 1) * gd)
            acc = ext[:, cols]
            step = 1
            while step < w:
                acc = acc + pltpu.roll(acc, step, 0)
                step *= 2
            p = (acc[POOL_HALO:, :] / _pool_counts(tm, i, w) - h[:, cols]).astype(BF16)
            p_ref[:, cols] = p
            o_ref[:, cols] = _dot(p, w_ref[gi]).astype(BF16)
        xo_ref[...] = x_ref[...] + vec_ref[3:4, :] * (o_ref[...].astype(F32) * sc_ref[...])

    nh = tm // POOL_HALO
    row = pl.BlockSpec((tm, d), lambda i: (i, 0))
    return _call(
        body, "pool_fwd", (s // tm,),
        [
            row,
            pl.BlockSpec((POOL_HALO, d), lambda i: (jnp.maximum(i * nh - 1, 0), 0)),
            pl.BlockSpec((8, d), lambda i: (0, 0)),
            pl.BlockSpec((groups, gd, gd), lambda i: (0, 0, 0)),
            pl.BlockSpec((1, d), lambda i: (0, 0)),
        ],
        [row, row, row],
        [jax.ShapeDtypeStruct((s, d), F32), jax.ShapeDtypeStruct((s, d), BF16), jax.ShapeDtypeStruct((s, d), BF16)],
        [x, x, vec, w_grp, scale], phases=phases,
    )


def _pool_bwd(dxo, x, vec, p, o, w_grp, scale, phases=()):
    s, d = x.shape
    groups, gd, _ = w_grp.shape
    tm = _pick(s, (512, 256, 128))
    nblk = s // tm

    def body(dxo_ref, dxb_ref, x_ref, vec_ref, p_ref, o_ref, w_ref, sc_ref, dx_ref, dw_ref, dsc_ref, dvec_ref, dw_sc):
        i = pl.program_id(0)

        @pl.when(i == 0)
        def _():
            dw_sc[...] = jnp.zeros_like(dw_sc)
            dsc_ref[...] = jnp.zeros_like(dsc_ref)
            dvec_ref[...] = jnp.zeros_like(dvec_ref)

        gate, sc = vec_ref[3:4, :], sc_ref[...]
        dxo_v = dxo_ref[...]
        ov = o_ref[...].astype(F32)
        dvec_ref[3:4, :] += jnp.sum(dxo_v * (ov * sc), axis=0, keepdims=True)
        dy = gate * dxo_v
        dsc_ref[...] += jnp.sum(dy * ov, axis=0, keepdims=True)
        dout = (dy * sc).astype(BF16)
        dout_b = jnp.where(i == nblk - 1, 0.0, gate * dxb_ref[...] * sc).astype(BF16)
        for gi, w in enumerate(POOL_WINDOWS):
            cols = slice(gi * gd, (gi + 1) * gd)
            dw_sc[gi] += _dot_tn(p_ref[:, cols], dout[:, cols])
            wb = w_ref[gi]
            dp = _dot_nt(dout[:, cols], wb)
            dp_b = _dot_nt(dout_b[:, cols], wb)
            e = dp / _pool_counts(tm, i, w)
            t_below = (i + 1) * tm + lax.broadcasted_iota(jnp.int32, (POOL_HALO, 1), 0)
            e_b = dp_b / jnp.minimum(t_below + 1, w).astype(F32)
            acc = jnp.concatenate([e, e_b], axis=0)
            step = 1
            while step < w:
                acc = acc + pltpu.roll(acc, tm + POOL_HALO - step, 0)
                step *= 2
            dx_ref[:, cols] = acc[:tm, :] - dp
        dx, _ = _modulate_bwd(x_ref[...], dx_ref[...], vec_ref, dvec_ref)
        dx_ref[...] = dxo_v + dx

        @pl.when(i == nblk - 1)
        def _():
            dw_ref[...] = dw_sc[...].astype(BF16)

    nh = tm // POOL_HALO
    row = pl.BlockSpec((tm, d), lambda i: (i, 0))
    vecs = pl.BlockSpec((8, d), lambda i: (0, 0))
    wspec = pl.BlockSpec((groups, gd, gd), lambda i: (0, 0, 0))
    return _call(
        body, "pool_bwd", (nblk,),
        [
            row,
            pl.BlockSpec((POOL_HALO, d), lambda i: (jnp.minimum((i + 1) * nh, s // POOL_HALO - 1), 0)),
            row, vecs, row, row, wspec,
            pl.BlockSpec((1, d), lambda i: (0, 0)),
        ],
        [row, wspec, pl.BlockSpec((1, d), lambda i: (0, 0)), vecs],
        [
            jax.ShapeDtypeStruct((s, d), F32),
            jax.ShapeDtypeStruct((groups, gd, gd), BF16),
            jax.ShapeDtypeStruct((1, d), F32),
            jax.ShapeDtypeStruct((8, d), F32),
        ],
        [dxo, dxo, x, vec, p, o, w_grp, scale],
        scratch=[pltpu.VMEM((groups, gd, gd), F32)], phases=phases,
    )


def _loss_head(x, gain, target, phases=()):
    s, d = x.shape
    tm = _pick(s, (512, 256, 128))

    def body(x_ref, g_ref, t_ref, dx_ref, aux_ref):
        @pl.when(pl.program_id(0) == 0)
        def _():
            aux_ref[...] = jnp.zeros_like(aux_ref)

        xv = x_ref[...]
        rstd = _rstd(xv)
        r = xv * rstd
        gain_v = g_ref[...]
        err = r * gain_v - t_ref[...]
        aux_ref[1:2, :] += jnp.sum(err * err, axis=0, keepdims=True)
        dout = err * (1.0 / d)
        aux_ref[0:1, :] += jnp.sum(dout * r, axis=0, keepdims=True)
        dr = dout * gain_v
        dx_ref[...] = rstd * (dr - r * jnp.mean(dr * r, axis=-1, keepdims=True))

    row = pl.BlockSpec((tm, d), lambda i: (i, 0))
    return _call(
        body, "loss_head", (s // tm,),
        [row, pl.BlockSpec((1, d), lambda i: (0, 0)), row],
        [row, pl.BlockSpec((8, d), lambda i: (0, 0))],
        [jax.ShapeDtypeStruct((s, d), F32), jax.ShapeDtypeStruct((8, d), F32)], [x, gain, target], phases=phases,
    )


def _small_adam(gathered, gathered_ws, layout, smalls, chip):
    names = list(smalls)
    n = len(names)
    loss_row, _, _, n_feat = layout["loss"]

    def body(*refs):
        chip_ref, g_ref, gws_ref = refs[0], refs[1], refs[2]
        wmv = refs[3 : 3 + 3 * n]
        outs = refs[3 + 3 * n : 3 + 7 * n]
        total = refs[-1]
        total[...] = g_ref[0]
        for kdev in range(1, N_DEV):
            total[...] += g_ref[kdev]
        total_ws = gws_ref[0]
        for kdev in range(1, N_DEV):
            total_ws = total_ws + gws_ref[kdev]
        my_chip = chip_ref[0]
        for a, name in enumerate(names):
            w_ref, m_ref, v_ref = wmv[3 * a : 3 * a + 3]
            if name == "ab_w_s":
                g = total_ws
            else:
                row0, rows, col0, cols = layout[name]
                if col0 is None:
                    g = jnp.zeros((rows, cols), F32)
                    for j in range(N_CHIPS):
                        g = g + jnp.where(my_chip == j, total[row0 : row0 + rows, j * cols : (j + 1) * cols], 0.0)
                else:
                    g = total[row0 : row0 + rows, col0 : col0 + cols]
            dl, mo, vo = _adam(w_ref[...], g, m_ref[...], v_ref[...])
            outs[4 * a][...] = g
            outs[4 * a + 1][...] = dl
            outs[4 * a + 2][...] = mo
            outs[4 * a + 3][...] = vo
        refs[3 + 7 * n][...] = 0.5 * jnp.sum(total[loss_row : loss_row + 1, 0:n_feat], axis=1, keepdims=True) / n_feat

    ins = [gathered, gathered_ws]
    out_shapes = []
    for name in names:
        ins.extend(smalls[name])
        out_shapes.extend([jax.ShapeDtypeStruct(smalls[name][0].shape, F32)] * 4)
    out_shapes.append(jax.ShapeDtypeStruct((1, 1), F32))
    whole = lambda shape: pl.BlockSpec(shape, functools.partial(lambda nd, i, c: (0,) * nd, len(shape)))
    res = pl.pallas_call(
        body, name="small_adam",
        grid_spec=pltpu.PrefetchScalarGridSpec(
            num_scalar_prefetch=1, grid=(1,),
            in_specs=[whole(a.shape) for a in ins], out_specs=[whole(o.shape) for o in out_shapes],
            scratch_shapes=[pltpu.VMEM(gathered.shape[1:], F32)],
        ),
        out_shape=out_shapes,
        compiler_params=pltpu.CompilerParams(dimension_semantics=("arbitrary",), vmem_limit_bytes=VMEM_LIMIT_BYTES),
    )(chip.reshape(1).astype(jnp.int32), *ins)
    return {name: res[4 * a : 4 * a + 4] for a, name in enumerate(names)}, res[4 * n]


def _pad_rows(a, rows=8):
    extra = (-a.shape[0]) % rows
    return jnp.pad(a, ((0, extra), (0, 0))) if extra else a


def _pad_cols(a, cols):
    return jnp.pad(a, ((0, 0), (0, cols - a.shape[1]))) if a.shape[1] < cols else a


def _run(fn, *phases):
    outs, p_outs = fn(list(phases))
    for p, po in zip(phases, p_outs):
        p.then(po)
    return outs


def kernel(x, c, norm_g, w_mod, b_mod, w_ffn_in, w_ffn_out, ab_w_in, ab_norm_v, ab_w_s, ab_b_s, ab_conv_w, ab_w_out, pool_w_grp, pool_scale, final_g, loss_target, m_norm_g, m_w_mod, m_b_mod, m_w_ffn_in, m_w_ffn_out, m_ab_w_in, m_ab_norm_v, m_ab_w_s, m_ab_b_s, m_ab_conv_w, m_ab_w_out, m_pool_w_grp, m_pool_scale, m_final_g, v_norm_g, v_w_mod, v_b_mod, v_w_ffn_in, v_w_ffn_out, v_ab_w_in, v_ab_norm_v, v_ab_w_s, v_ab_b_s, v_ab_conv_w, v_ab_w_out, v_pool_w_grp, v_pool_scale, v_final_g):
    ix, iy, ic = _place()
    chip = 2 * ix + iy
    me = 4 * ix + 2 * iy + ic
    where = jnp.stack([chip, ic]).astype(jnp.int32)
    s, d = x.shape[1], x.shape[2]
    x0 = x.reshape(s, d)
    target = loss_target.reshape(s, d)
    n_layers = norm_g.shape[0]
    dq = d // N_CHIPS
    heads, chunk = ab_w_s.shape[1], ab_w_s.shape[2]
    da = ab_norm_v.shape[1]
    db = ab_conv_w.shape[2] * N_CHIPS
    f_hidden = w_ffn_out.shape[2] * N_CHIPS
    assert n_layers == 2 and da % heads == 0

    cw_pad = _pad_cols(ab_conv_w.reshape(3, db // N_CHIPS), dq)
    packed = jnp.concatenate(
        [_pad_rows(c.reshape(N_CHIPS, dq)), _pad_rows(norm_g.reshape(-1, dq)), _pad_rows(pool_scale.reshape(1, dq)), _pad_rows(cw_pad)],
        axis=0,
    )
    ncol = w_mod.shape[2]
    b_cols = lax.dynamic_slice(b_mod, (0, chip * ncol), (n_layers, ncol)).reshape(n_layers, 1, ncol)
    small = {}

    def small_gather(key, arrs):
        def then(outs):
            small[key] = outs

        return _phase_small_gather(arrs, then)

    stacks = {
        "w_ffn_in": tuple(a.reshape((-1,) + a.shape[2:]) for a in (w_ffn_in, m_w_ffn_in, v_w_ffn_in)),
        "w_ffn_out": tuple(a.reshape((-1,) + a.shape[2:]) for a in (w_ffn_out, m_w_ffn_out, v_w_ffn_out)),
        "ab_w_in": (ab_w_in, m_ab_w_in, v_ab_w_in),
        "ab_w_out": (ab_w_out, m_ab_w_out, v_ab_w_out),
        "pool_w_grp": (pool_w_grp[0], m_pool_w_grp[0], v_pool_w_grp[0]),
    }
    big_in = _Big((1, d, 2 * f_hidden), 2, 1)
    big_out = _Big((1, f_hidden, d), 1, 2)
    units = {}
    for l in range(n_layers):
        for k in range(2):
            units[f"in{l}{k}"] = (big_in, "w_ffn_in", 2 * l + k)
            units[f"out{l}{k}"] = (big_out, "w_ffn_out", 2 * l + k)
    units["abin"] = (_Big((1, d, ab_w_in.shape[2] * N_CHIPS), 2, 1), "ab_w_in", 0)
    units["about"] = (_Big((1, ab_w_out.shape[1] * N_CHIPS, d), 1, 2), "ab_w_out", 0)
    units["pool"] = (_Big((pool_w_grp.shape[1], pool_w_grp.shape[2] * N_CHIPS, pool_w_grp.shape[3]), 1, 0), "pool_w_grp", 0)
    big = {u: g for u, (g, _, _) in units.items()}

    weight = {}
    complete = set()

    def cast(u):
        g, st, b0 = units[u]

        def launch(phases):
            (weight[u],), p_outs = _cast_into_full(stacks[st][0], b0, g, where, "cast_" + u, phases)
            return None, p_outs

        return launch

    def gather_relay(us, second, whole_first):
        def then(outs):
            for u, o in zip(us, outs):
                weight[u] = o

        return _phase_gather_relay([weight[u] for u in us], [big[u] for u in us], second, whole_first, then)

    def gather_sibling(*us):
        def then(outs):
            for u, o in zip(us, outs):
                weight[u] = o
                complete.add(u)

        return _phase_gather_sibling([weight[u] for u in us], [big[u] for u in us], then)

    def w_of(u):
        assert u in complete, u
        return weight[u]

    _run(cast("in00"), small_gather("inputs", [packed]))
    small_all = small["inputs"][0]
    by_chip = small_all[0::2]
    c_all = small_all[:, 0:N_CHIPS, :].reshape(N_DEV, d)
    norm_full = by_chip[:, 8 : 8 + 3 * n_layers, :].transpose(1, 0, 2).reshape(3 * n_layers, d)
    pool_scale_full = by_chip[:, 16:17, :].transpose(1, 0, 2).reshape(1, d)
    conv_full = by_chip[:, 24:27, : db // N_CHIPS].transpose(1, 0, 2).reshape(3, db)
    pieces = [("in00", "out00"), ("abin", "about"), ("in01", "out01"), ("in10", "out10", "pool"), ("in11", "out11")]
    in_flight = {}

    def start_gather(p):
        in_flight[p, 0] = _split_start(gather_relay(pieces[p], False, p == 0), f"gather_{p}_start")

    def relay_gather(p, after=()):
        flight = in_flight.pop((p, 0))
        _split_wait(flight, list(after) + list(started().ins), f"gather_{p}_arrived")
        in_flight[p, 1] = _split_start(gather_relay(pieces[p], True, p == 0), f"gather_{p}_relay")

    def started():
        return _after(*[flight.token for flight in in_flight.values()])

    def finish_gather(p, after, meanwhile=None):
        flight = in_flight.pop((p, 1))
        _split_wait(flight, list(after) + list(started().ins), f"gather_{p}_wait")
        crossing = _split_start(gather_sibling(*pieces[p]), f"gather_{p}_forward")
        behind = [crossing.token]
        if p + 1 < len(pieces):
            relay_gather(p + 1)
        if p + 3 < len(pieces):
            start_gather(p + 3)
        behind = behind + list(started().ins)
        if meanwhile is not None:
            behind = behind + meanwhile(_after(crossing.token))
        _split_wait(crossing, behind, f"gather_{p}_forwarded")

    _run(cast("out00"))
    start_gather(0)
    mod_cols = _run(lambda phases: _mod_fwd(c_all, w_mod, b_cols, phases), started())[0]

    def mod_rows(outs):
        small["mod"] = outs

    _run(cast("about"), started())
    early = [u for piece in pieces[2:4] for u in piece]
    for u in early:
        _run(cast(u), started())
    _run(
        cast("abin"), _phase_small_exchange(mod_cols.transpose(1, 0, 2), mod_rows),
        started(), _after(*[weight[u] for u in early]),
    )
    relay_gather(0)
    start_gather(1)
    start_gather(2)
    for u in pieces[4]:
        _run(cast(u), started())
    mod_mine = small["mod"][0][0::2]
    mod = mod_mine.transpose(1, 0, 2).reshape(n_layers, 3, 3, d)
    vecs = {
        (l, sub): jnp.pad(norm_full[3 * l + sub][None], ((0, 7), (0, 0))) + jnp.pad(mod[l, sub], ((1, 4), (0, 0)))
        for l in range(n_layers)
        for sub in range(3)
    }
    b_rows = jnp.broadcast_to(ab_b_s[0].T[:, :, None], (chunk, heads, da // heads)).reshape(chunk, da)

    saved = {}

    def ffn_forward(xs, l, sub, k, *phases):
        saved[l, sub, "x"] = xs
        xs, gg, uu, yb = _run(
            lambda ph: _ffn_fwd(xs, vecs[l, sub], w_of(f"in{l}{k}"), w_of(f"out{l}{k}"), f"ffn_fwd_{l}{k}", ph), *phases
        )
        saved[l, sub, "act"] = (gg, uu, yb)
        return xs

    finish_gather(0, [vecs[0, 0]] + [weight[u] for u in pieces[4]])
    xs = ffn_forward(x0, 0, 0, 0, started())
    saved[0, 1, "x"] = xs
    finish_gather(1, [xs])
    (proj,) = _run(lambda ph: _proj_mod_fwd(xs, vecs[0, 1], w_of("abin"), ph), started())
    (cat,) = _run(lambda ph: _ab_mix_fwd(proj, ab_norm_v, ab_w_s[0], b_rows, conv_full, ph))
    xs, yb = _run(lambda ph: _proj_res_fwd(cat, w_of("about"), xs, vecs[0, 1], ph))
    saved[0, 1, "act"] = (proj, cat, yb)
    finish_gather(2, [xs])
    xs = ffn_forward(xs, 0, 2, 1, started())
    finish_gather(3, [xs])
    xs = ffn_forward(xs, 1, 0, 0, started())
    saved[1, 1, "x"] = xs
    pooled = []

    def pool_forward(behind):
        pooled.extend(_run(lambda ph: _pool_fwd(xs, vecs[1, 1], w_of("pool"), pool_scale_full, ph), behind))
        return [pooled[0]]

    finish_gather(4, [xs], pool_forward)
    xs, pp, oo = pooled
    saved[1, 1, "act"] = (pp, oo)
    xs = ffn_forward(xs, 1, 2, 1)
    dxs, aux = _run(lambda ph: _loss_head(xs, final_g.reshape(1, d), target, ph))

    grad = {}
    recv = {}
    csum = {}
    parts = {}
    reduced = {}
    done = set()
    dvecs, small_g = {}, {}

    def pair_exchange(*us):
        def then(outs):
            for u, o in zip(us, outs):
                recv[u] = o

        return _phase_pair_exchange([grad[u] for u in us], [big[u] for u in us], then)

    def grad_half(u, a, bs, mine, name, *phases):
        (res,) = _run(lambda ph: _grad_half(a, bs, big[u], where, mine, recv[u] if mine else None, name, ph), *phases)
        return res

    def pair_sum(u, *phases):
        def launch(ph):
            (csum[u],), p_outs = _pair_sum(grad[u], recv[u], big[u], where, "pair_sum_" + u, ph)
            return None, p_outs

        _run(launch, *phases)

    def chip_exchange(*us):
        def then(outs):
            for u, o in zip(us, outs):
                parts[u] = o

        return _phase_chip_exchange([csum[u] for u in us], [big[u] for u in us], then)

    def chip_sum(*us, carried=()):
        for n_u, u in enumerate(us):
            g, st, b0 = units[u]

            def launch(ph):
                (reduced[st],), p_outs = _chip_sum(
                    csum[u], parts[u], g, where, reduced.get(st), stacks[st][0].shape, b0, "chip_sum_" + u, ph
                )
                return None, p_outs

            _run(launch, *(carried if n_u == 0 else ()))

    def pair_broadcast(*us):
        sts = [units[u][1] for u in us]
        assert len(set(sts)) == len(sts)

        def then(outs):
            for u, st, o in zip(us, sts, outs):
                reduced[st] = o
                done.add(u)

        return _phase_pair_broadcast([reduced[st] for st in sts], [big[u] for u in us], [units[u][2] for u in us], then)

    def ffn_backward(dxs, l, sub, k, carried_bwd, carried_send, carried_mine):
        gg, uu, yb = saved[l, sub, "act"]
        w_in, w_out = w_of(f"in{l}{k}"), w_of(f"out{l}{k}")
        uo, ui, tag = f"out{l}{k}", f"in{l}{k}", f"{l}{k}"
        dxs, dg, du, a, h, dy, dvecs[l, sub] = _run(
            lambda ph: _ffn_bwd(dxs, saved[l, sub, "x"], vecs[l, sub], gg, uu, yb, w_in, w_out, "ffn_bwd_" + tag, ph), *carried_bwd()
        )
        grad[uo] = grad_half(uo, a, [dy], False, "dw_out_send_" + tag)
        grad[ui] = grad_half(ui, h, [dg, du], False, "dw_in_send_" + tag, pair_exchange(uo), *carried_send())
        csum[uo] = grad_half(uo, a, [dy], True, "dw_out_" + tag, pair_exchange(ui))
        csum[ui] = grad_half(ui, h, [dg, du], True, "dw_in_" + tag, *carried_mine())
        return dxs

    none = lambda: ()
    dxs = ffn_backward(dxs, 1, 2, 1, none, none, none)
    pp, oo = saved[1, 1, "act"]
    dxs, grad["pool"], small_g["pool_scale"], dvecs[1, 1] = _run(
        lambda ph: _pool_bwd(dxs, saved[1, 1, "x"], vecs[1, 1], pp, oo, w_of("pool"), pool_scale_full, ph)
    )

    def after_11():
        return (chip_exchange("in11", "out11"), pair_exchange("pool"))

    def bcast_11():
        chip_sum("in11", "out11")
        pair_sum("pool")
        return (pair_broadcast("in11", "out11"), chip_exchange("pool"))

    dxs = ffn_backward(dxs, 1, 0, 0, after_11, bcast_11, none)

    def after_10():
        return (chip_exchange("in10", "out10"),)

    def bcast_10():
        chip_sum("in10", "out10", "pool")
        return (pair_broadcast("in10", "out10", "pool"),)

    dxs = ffn_backward(dxs, 0, 2, 1, after_10, bcast_10, none)

    proj, cat, yb = saved[0, 1, "act"]
    out01 = _split_start(chip_exchange("out01"), "reduce_out01_start")
    dy, dcat, dgate = _run(lambda ph: _proj_res_bwd(dxs, yb, vecs[0, 1], w_of("about"), ph), _after(out01.token))
    grad["about"] = grad_half("about", cat, [dy], False, "dw_ab_out_send")
    dproj, small_g["ab_norm_v"], small_g["ab_w_s"], dzs, small_g["ab_conv_w"] = _run(
        lambda ph: _ab_mix_bwd(proj, dcat, ab_norm_v, ab_w_s[0], b_rows, conv_full, ph), pair_exchange("about")
    )
    small_g["ab_b_s"] = dzs.reshape(chunk, heads, da // heads).sum(axis=2).T
    dxs, h, dvecs[0, 1] = _run(
        lambda ph: _proj_mod_bwd(dproj[None], w_of("abin"), saved[0, 1, "x"], vecs[0, 1], dxs, dgate, "ab_in_bwd", ph)
    )
    grad["abin"] = grad_half("abin", h, [dproj], False, "dw_ab_in_send")
    (csum["out01"],) = _split_wait(out01, [grad["abin"]], "reduce_out01_wait")
    chip_sum("out01")
    csum["about"] = grad_half("about", cat, [dy], True, "dw_ab_out", pair_broadcast("out01"), pair_exchange("abin"))
    csum["abin"] = grad_half("abin", h, [dproj], True, "dw_ab_in")

    layout = {}
    tail = {}

    def after_01():
        tail["01"] = _split_start(chip_exchange("in01", "abin", "about"), "reduce_01_start")
        return (_after(tail["01"].token),)

    def pack_small_grads():
        dvec_all = jnp.stack([dvecs[l, sub] for l in range(n_layers) for sub in range(3)])
        dgain = dvec_all[:, 0, :]
        dmod = dvec_all[:, 1:4, :].reshape(3 * 3 * n_layers, d)
        rows = {
            "norm_g": (dgain, None, dq), "final_g": (aux[0:1], 0, d), "pool_scale": (small_g["pool_scale"], None, dq),
            "b_mod": (dmod, 0, d), "ab_norm_v": (small_g["ab_norm_v"], 0, da),
            "ab_conv_w": (small_g["ab_conv_w"], None, db // N_CHIPS), "ab_b_s": (small_g["ab_b_s"], 0, chunk),
            "loss": (aux[1:2], 0, d),
        }
        row0 = 0
        for nm, (pc, col0, cols) in rows.items():
            layout[nm] = (row0, pc.shape[0], col0, cols)
            row0 += pc.shape[0]
        packed_rows = -(-row0 // 8) * 8
        return sum(
            jnp.pad(pc, ((layout[nm][0], packed_rows - layout[nm][0] - pc.shape[0]), (0, d - pc.shape[1])))
            for nm, (pc, _, _) in rows.items()
        )

    def bcast_01():
        csum["in01"], csum["abin"], csum["about"] = _split_wait(tail["01"], [dvecs[0, 0]], "reduce_01_wait")
        chip_sum("in01", "abin", "about")
        grads_small = [pack_small_grads(), small_g["ab_w_s"].reshape(heads * chunk, chunk)]
        tail["small"] = _split_start(small_gather("grads", grads_small), "gather_small_grads_start")
        return (pair_broadcast("in01", "abin", "about"), _after(tail["small"].token))

    def reduce_out00():
        tail["out00"] = _split_start(chip_exchange("out00"), "reduce_out00_start")
        return (_after(tail["out00"].token),)

    dxs = ffn_backward(dxs, 0, 0, 0, after_01, bcast_01, reduce_out00)
    grad_x = dxs.reshape(x.shape)

    last = _split_start(chip_exchange("in00"), "reduce_last_start")
    _split_wait(tail["small"], [last.token], "gather_small_grads_wait")
    g_all, gws_all = small["grads"]

    out = {}

    def adam_stack(st, after=()):
        w3, m3, v3 = stacks[st]
        assert all(u in done for u, (_, ust, _) in units.items() if ust == st), st
        shape = {"w_ffn_in": w_ffn_in.shape, "w_ffn_out": w_ffn_out.shape, "pool_w_grp": pool_w_grp.shape}.get(st, w3.shape)
        out[st] = tuple(a.reshape(shape) for a in _adam_stack(w3, reduced[st], m3, v3, "adam_" + st, after))

    shapes2d = {
        "norm_g": (3 * n_layers, dq), "b_mod": (9 * n_layers, d), "final_g": (1, d), "ab_norm_v": (1, da),
        "pool_scale": (1, dq), "ab_conv_w": (3, db // N_CHIPS), "ab_b_s": (heads, chunk), "ab_w_s": (heads * chunk, chunk),
    }
    small_w = {"norm_g": (norm_g, m_norm_g, v_norm_g), "b_mod": (b_mod, m_b_mod, v_b_mod), "final_g": (final_g, m_final_g, v_final_g),
               "ab_norm_v": (ab_norm_v, m_ab_norm_v, v_ab_norm_v), "pool_scale": (pool_scale, m_pool_scale, v_pool_scale),
               "ab_conv_w": (ab_conv_w, m_ab_conv_w, v_ab_conv_w), "ab_b_s": (ab_b_s, m_ab_b_s, v_ab_b_s), "ab_w_s": (ab_w_s, m_ab_w_s, v_ab_w_s)}
    smalls = {nm: tuple(a.reshape(shapes2d[nm]) for a in wmv) for nm, wmv in small_w.items()}
    small_out, loss = _small_adam(g_all, gws_all, layout, smalls, chip)
    loss = loss.reshape(())
    for nm, res in small_out.items():
        out[nm] = tuple(a.reshape(small_w[nm][0].shape) for a in res)

    mod_row0 = layout["b_mod"][0]
    dmod_all = g_all[:, mod_row0 : mod_row0 + 9 * n_layers, :].reshape(N_DEV, n_layers, 9 * d)
    dmod_cols = lax.dynamic_slice(dmod_all, (0, 0, chip * ncol), (N_DEV, n_layers, ncol)).transpose(1, 0, 2)
    out["w_mod"] = tuple(_mod_bwd_adam(c_all.T, dmod_cols, w_mod, m_w_mod, v_w_mod, (last.token,)))

    (csum["out00"],) = _split_wait(tail["out00"], [out["w_mod"][1]], "reduce_out00_wait")
    chip_sum("out00")
    crossing = _split_start(pair_broadcast("out00"), "broadcast_out00_start")
    for st in ("ab_w_in", "ab_w_out", "pool_w_grp"):
        adam_stack(st, (crossing.token,))
    _split_wait(crossing, [out[st][1] for st in ("ab_w_in", "ab_w_out", "pool_w_grp")], "broadcast_out00_wait")
    (csum["in00"],) = _split_wait(last, [reduced["w_ffn_out"]], "reduce_last_wait")
    chip_sum("in00")
    crossing = _split_start(pair_broadcast("in00"), "broadcast_last_start")
    adam_stack("w_ffn_out", (crossing.token,))
    _split_wait(crossing, [out["w_ffn_out"][1]], "broadcast_last_wait")
    adam_stack("w_ffn_in")

    order = ["norm_g", "w_mod", "b_mod", "w_ffn_in", "w_ffn_out", "ab_w_in", "ab_norm_v", "ab_w_s", "ab_b_s", "ab_conv_w", "ab_w_out", "pool_w_grp", "pool_scale", "final_g"]
    return (loss, grad_x, *[out[nm][0] for nm in order], *[out[nm][1] for nm in order], *[out[nm][2] for nm in order], *[out[nm][3] for nm in order])
```
